```python
import jax, jax.numpy as jnp
from jax import lax
import numpy as np

D_MODEL = 1024
BATCH = 8
SEQ = 4096
DEPTH = 2

CHUNK = 64
QBLOCK = 128
MEM_LEN = 256
EPS = 1e-6
FOX_HEADS = 4
FOX_HD = 64
GLA_HEADS = 4
GLA_DK = 64
GLA_DV = 128
GLA_GATE_RANK = 16
GLA_TAU = 16.0
MLA_HEADS = 4
MLA_Q_RANK = 256
MLA_KV_RANK = 128
MLA_NOPE = 64
MLA_ROPE = 32
MLA_VD = 64
ROPE_BASE = 10000.0
XA_HEADS = 4
XA_HD = 128
D_FF = 4 * D_MODEL
N_BRANCH = 3

FOX_W = FOX_HEADS * FOX_HD
GLA_K_W = GLA_HEADS * GLA_DK
GLA_V_W = GLA_HEADS * GLA_DV
MLA_QK_HD = MLA_NOPE + MLA_ROPE
MLA_W = MLA_HEADS * MLA_VD
XA_W = XA_HEADS * XA_HD

IN_SIZES = (FOX_W, FOX_W, FOX_W, FOX_HEADS,
            GLA_K_W, GLA_K_W, GLA_V_W, GLA_GATE_RANK, GLA_V_W,
            MLA_Q_RANK, MLA_KV_RANK, MLA_ROPE,
            N_BRANCH * D_MODEL)
N_IN = sum(IN_SIZES)

kernel_name = 'hybrid_fox_gla_mla_gated_encoder'

F32 = jnp.float32


def rms_norm(x, g):
    xf = x.astype(F32)
    y = xf * lax.rsqrt(jnp.mean(xf * xf, axis=-1, keepdims=True) + EPS)
    return (y * g.astype(F32)).astype(x.dtype)


def split_heads(z, n):
    b, s, w = z.shape
    return z.reshape(b, s, n, w // n).transpose(0, 2, 1, 3)


def merge_heads(z):
    b, h, s, d = z.shape
    return z.transpose(0, 2, 1, 3).reshape(b, s, h * d)


def split_cols(z, sizes):
    out, start = [], 0
    for n in sizes:
        out.append(z[..., start:start + n])
        start += n
    return out


def rope(x, pos):
    half = x.shape[-1] // 2
    inv = ROPE_BASE ** (-jnp.arange(half, dtype=F32) / half)
    ang = pos.astype(F32)[:, None] * inv[None, :]
    cos, sin = jnp.cos(ang), jnp.sin(ang)
    xf = x.astype(F32)
    x1, x2 = xf[..., :half], xf[..., half:]
    return jnp.concatenate([x1 * cos - x2 * sin, x2 * cos + x1 * sin], axis=-1).astype(x.dtype)


def block_sweep_attention(q, k, v, scale, chunk_causal, log_decay=None):
    b, h, s, dk = q.shape
    nb = s // QBLOCK
    qb = q.reshape(b, h, nb, QBLOCK, dk).transpose(2, 0, 1, 3, 4)
    kpos = jnp.arange(s)
    idx = jnp.arange(nb)

    def one_block(args):
        i, qi = args[0], args[1]
        qpos = i * QBLOCK + jnp.arange(QBLOCK)
        logits = jnp.einsum('bhqd,bhkd->bhqk', qi, k, preferred_element_type=F32) * scale
        if log_decay is not None:
            logits = logits + args[2][..., :, None] - log_decay[:, :, None, :]
        limit = ((qpos // CHUNK) + 1) * CHUNK if chunk_causal else qpos + 1
        mask = kpos[None, :] < limit[:, None]
        p = jax.nn.softmax(jnp.where(mask, logits, -jnp.inf), axis=-1)
        return jnp.einsum('bhqk,bhkd->bhqd', p.astype(v.dtype), v)

    if log_decay is None:
        xs = (idx, qb)
    else:
        db = log_decay.reshape(b, h, nb, QBLOCK).transpose(2, 0, 1, 3)
        xs = (idx, qb, db)
    out = lax.map(one_block, xs)
    return out.transpose(1, 2, 0, 3, 4).reshape(b, h, s, v.shape[-1])


def fox_branch(q, k, v, f_logit, b_f):
    log_f = jax.nn.log_sigmoid(f_logit.astype(F32) + b_f.astype(F32))
    cum = jnp.cumsum(log_f, axis=1).transpose(0, 2, 1)
    o = block_sweep_attention(split_heads(q, FOX_HEADS), split_heads(k, FOX_HEADS),
                              split_heads(v, FOX_HEADS), FOX_HD ** -0.5,
                              chunk_causal=False, log_decay=cum)
    return merge_heads(o)


def gla_branch(q, k, v, g_low, r, w_gate, b_gate, g_out):
    b, s, _ = q.shape
    nc = s // CHUNK
    dt = q.dtype
    log_a = jax.nn.log_sigmoid((g_low @ w_gate + b_gate).astype(F32)) / GLA_TAU

    def chunks(z, d):
        return split_heads(z, GLA_HEADS).reshape(b, GLA_HEADS, nc, CHUNK, d)

    qc = chunks(q, GLA_DK).astype(F32) * (GLA_DK ** -0.5)
    kc = chunks(k, GLA_DK).astype(F32)
    vc = chunks(v, GLA_DV).astype(F32)
    cum = jnp.cumsum(chunks(log_a, GLA_DK), axis=3)
    end = cum[:, :, :, -1:, :]
    k_dec = kc * jnp.exp(end - cum)
    u = jnp.einsum('bhcld,bhcle->cbhde', k_dec, vc)
    a = jnp.exp(end[:, :, :, 0, :]).transpose(2, 0, 1, 3)

    def step(state, inp):
        u_c, a_c = inp
        state = a_c[..., None] * state + u_c
        return state, state

    _, states = lax.scan(step, jnp.zeros((b, GLA_HEADS, GLA_DK, GLA_DV), F32), (u, a))
    o = jnp.einsum('bhcld,cbhde->bhcle', qc, states).reshape(b, GLA_HEADS, s, GLA_DV)
    o = merge_heads(rms_norm(o, g_out)).astype(dt)
    return o * jax.nn.silu(r)


def mla_branch(c_q, c_kv, k_rope_in, g_q, w_uq, g_kv, w_ukv, pos):
    q = split_heads(rms_norm(c_q, g_q) @ w_uq, MLA_HEADS)
    kv = split_heads(rms_norm(c_kv, g_kv) @ w_ukv, MLA_HEADS)
    q_nope, q_rope = q[..., :MLA_NOPE], q[..., MLA_NOPE:]
    k_nope, v = kv[..., :MLA_NOPE], kv[..., MLA_NOPE:]
    k_rope = rope(k_rope_in, pos)[:, None]
    qh = jnp.concatenate([q_nope, rope(q_rope, pos)], axis=-1)
    kh = jnp.concatenate([k_nope, jnp.broadcast_to(k_rope, k_nope.shape[:-1] + (MLA_ROPE,))], axis=-1)
    o = block_sweep_attention(qh, kh, v, MLA_QK_HD ** -0.5, chunk_causal=True)
    return merge_heads(o)


def memory_cross_attention(h, m, w_xq, w_xkv, w_xo):
    q = split_heads(h @ w_xq, XA_HEADS)
    k, v = jnp.split(m @ w_xkv, 2, axis=-1)
    k, v = split_heads(k, XA_HEADS), split_heads(v, XA_HEADS)
    logits = jnp.einsum('bhqd,bhkd->bhqk', q, k, preferred_element_type=F32) * (XA_HD ** -0.5)
    p = jax.nn.softmax(logits, axis=-1)
    o = jnp.einsum('bhqk,bhkd->bhqd', p.astype(v.dtype), v)
    return merge_heads(o) @ w_xo


def _fwd_setup_inputs(seed: int = 0) -> dict:
    key = jax.random.key(seed)
    ks = jax.random.split(key, 32)

    def dense(k, shape, fan_in):
        return jax.random.normal(k, shape, F32) * (fan_in ** -0.5)

    def gain(k, shape):
        return 1.0 + 0.02 * jax.random.normal(k, shape, F32)

    def bias(k, shape, scale):
        return scale * jax.random.normal(k, shape, F32)

    L, D = DEPTH, D_MODEL
    return {
        'x': jax.random.normal(ks[0], (BATCH, SEQ, D), F32),
        'mem': jax.random.normal(ks[1], (BATCH, MEM_LEN, D), F32),
        'g_mix': gain(ks[2], (L, D)),
        'w_in': dense(ks[3], (L, D, N_IN), D),
        'b_fox_forget': bias(ks[4], (L, FOX_HEADS), 0.1),
        'w_gla_gate': dense(ks[5], (L, GLA_GATE_RANK, GLA_K_W), GLA_GATE_RANK),
        'b_gla_gate': bias(ks[6], (L, GLA_K_W), 0.1),
        'g_gla_out': gain(ks[7], (L, GLA_DV)),
        'g_mla_q': gain(ks[8], (L, MLA_Q_RANK)),
        'w_mla_uq': dense(ks[9], (L, MLA_Q_RANK, MLA_HEADS * MLA_QK_HD), MLA_Q_RANK),
        'g_mla_kv': gain(ks[10], (L, MLA_KV_RANK)),
        'w_mla_ukv': dense(ks[11], (L, MLA_KV_RANK, MLA_HEADS * (MLA_NOPE + MLA_VD)), MLA_KV_RANK),
        'b_branch_gate': bias(ks[12], (L, N_BRANCH * D), 0.1),
        'w_up_fox': dense(ks[13], (L, FOX_W, D), FOX_W),
        'w_up_gla': dense(ks[14], (L, GLA_V_W, D), GLA_V_W),
        'w_up_mla': dense(ks[15], (L, MLA_W, D), MLA_W),
        'w_out': dense(ks[16], (L, D, D), D),
        'g_xa': gain(ks[17], (L, D)),
        'g_mem': gain(ks[18], (L, D)),
        'w_xq': dense(ks[19], (L, D, XA_W), D),
        'w_xkv': dense(ks[20], (L, D, 2 * XA_W), D),
        'w_xo': dense(ks[21], (L, XA_W, D), XA_W),
        'g_mlp': gain(ks[22], (L, D)),
        'w_mlp1': dense(ks[23], (L, D, D_FF), D),
        'w_mlp2': dense(ks[24], (L, D_FF, D), D_FF),
        'g_final': gain(ks[25], (D,)),
    }


def _fwd_reference(x, mem, g_mix, w_in, b_fox_forget, w_gla_gate, b_gla_gate, g_gla_out,
              g_mla_q, w_mla_uq, g_mla_kv, w_mla_ukv, b_branch_gate,
              w_up_fox, w_up_gla, w_up_mla, w_out, g_xa, g_mem, w_xq, w_xkv, w_xo,
              g_mlp, w_mlp1, w_mlp2, g_final):
    b, s, d = x.shape
    pos = jnp.arange(s)
    for l in range(DEPTH):
        h = rms_norm(x, g_mix[l])
        z = h @ w_in[l]
        (fq, fk, fv, ff, gq, gk, gv, glow, gr, mq, mkv, mkr, zg) = split_cols(z, IN_SIZES)
        o_fox = fox_branch(fq, fk, fv, ff, b_fox_forget[l])
        o_gla = gla_branch(gq, gk, gv, glow, gr, w_gla_gate[l], b_gla_gate[l], g_gla_out[l])
        o_mla = mla_branch(mq, mkv, mkr, g_mla_q[l], w_mla_uq[l], g_mla_kv[l], w_mla_ukv[l], pos)
        gates = jax.nn.sigmoid((zg + b_branch_gate[l]).astype(F32)).astype(x.dtype)
        gates = gates.reshape(b, s, N_BRANCH, d)
        y = (gates[:, :, 0] * (o_fox @ w_up_fox[l])
             + gates[:, :, 1] * (o_gla @ w_up_gla[l])
             + gates[:, :, 2] * (o_mla @ w_up_mla[l]))
        x = x + y @ w_out[l]
        x = x + memory_cross_attention(rms_norm(x, g_xa[l]), rms_norm(mem, g_mem[l]),
                                       w_xq[l], w_xkv[l], w_xo[l])
        hm = rms_norm(x, g_mlp[l])
        x = x + jnp.square(jax.nn.relu(hm @ w_mlp1[l])) @ w_mlp2[l]
    return rms_norm(x, g_final)


import jax as _jax
import jax.numpy as _jnp

TWIN_FORMAT = 'train_step'
FWD_PARAMS = ['x', 'mem', 'g_mix', 'w_in', 'b_fox_forget', 'w_gla_gate', 'b_gla_gate', 'g_gla_out', 'g_mla_q', 'w_mla_uq', 'g_mla_kv', 'w_mla_ukv', 'b_branch_gate', 'w_up_fox', 'w_up_gla', 'w_up_mla', 'w_out', 'g_xa', 'g_mem', 'w_xq', 'w_xkv', 'w_xo', 'g_mlp', 'w_mlp1', 'w_mlp2', 'g_final']
TWIN_WEIGHTS = ['g_mix', 'w_in', 'b_fox_forget', 'w_gla_gate', 'b_gla_gate', 'g_gla_out', 'g_mla_q', 'w_mla_uq', 'g_mla_kv', 'w_mla_ukv', 'b_branch_gate', 'w_up_fox', 'w_up_gla', 'w_up_mla', 'w_out', 'g_xa', 'g_mem', 'w_xq', 'w_xkv', 'w_xo', 'g_mlp', 'w_mlp1', 'w_mlp2', 'g_final']
TWIN_DIFF_INPUT = 'x'
TWIN_INPUTS = ['x', 'mem', 'g_mix', 'w_in', 'b_fox_forget', 'w_gla_gate', 'b_gla_gate', 'g_gla_out', 'g_mla_q', 'w_mla_uq', 'g_mla_kv', 'w_mla_ukv', 'b_branch_gate', 'w_up_fox', 'w_up_gla', 'w_up_mla', 'w_out', 'g_xa', 'g_mem', 'w_xq', 'w_xkv', 'w_xo', 'g_mlp', 'w_mlp1', 'w_mlp2', 'g_final', 'loss_target', 'm_g_mix', 'm_w_in', 'm_b_fox_forget', 'm_w_gla_gate', 'm_b_gla_gate', 'm_g_gla_out', 'm_g_mla_q', 'm_w_mla_uq', 'm_g_mla_kv', 'm_w_mla_ukv', 'm_b_branch_gate', 'm_w_up_fox', 'm_w_up_gla', 'm_w_up_mla', 'm_w_out', 'm_g_xa', 'm_g_mem', 'm_w_xq', 'm_w_xkv', 'm_w_xo', 'm_g_mlp', 'm_w_mlp1', 'm_w_mlp2', 'm_g_final', 'v_g_mix', 'v_w_in', 'v_b_fox_forget', 'v_w_gla_gate', 'v_b_gla_gate', 'v_g_gla_out', 'v_g_mla_q', 'v_w_mla_uq', 'v_g_mla_kv', 'v_w_mla_ukv', 'v_b_branch_gate', 'v_w_up_fox', 'v_w_up_gla', 'v_w_up_mla', 'v_w_out', 'v_g_xa', 'v_g_mem', 'v_w_xq', 'v_w_xkv', 'v_w_xo', 'v_g_mlp', 'v_w_mlp1', 'v_w_mlp2', 'v_g_final']
TWIN_OUTPUTS = ['loss', 'grad_x', 'grad_g_mix', 'grad_w_in', 'grad_b_fox_forget', 'grad_w_gla_gate', 'grad_b_gla_gate', 'grad_g_gla_out', 'grad_g_mla_q', 'grad_w_mla_uq', 'grad_g_mla_kv', 'grad_w_mla_ukv', 'grad_b_branch_gate', 'grad_w_up_fox', 'grad_w_up_gla', 'grad_w_up_mla', 'grad_w_out', 'grad_g_xa', 'grad_g_mem', 'grad_w_xq', 'grad_w_xkv', 'grad_w_xo', 'grad_g_mlp', 'grad_w_mlp1', 'grad_w_mlp2', 'grad_g_final', 'delta_g_mix', 'delta_w_in', 'delta_b_fox_forget', 'delta_w_gla_gate', 'delta_b_gla_gate', 'delta_g_gla_out', 'delta_g_mla_q', 'delta_w_mla_uq', 'delta_g_mla_kv', 'delta_w_mla_ukv', 'delta_b_branch_gate', 'delta_w_up_fox', 'delta_w_up_gla', 'delta_w_up_mla', 'delta_w_out', 'delta_g_xa', 'delta_g_mem', 'delta_w_xq', 'delta_w_xkv', 'delta_w_xo', 'delta_g_mlp', 'delta_w_mlp1', 'delta_w_mlp2', 'delta_g_final', 'new_m_g_mix', 'new_m_w_in', 'new_m_b_fox_forget', 'new_m_w_gla_gate', 'new_m_b_gla_gate', 'new_m_g_gla_out', 'new_m_g_mla_q', 'new_m_w_mla_uq', 'new_m_g_mla_kv', 'new_m_w_mla_ukv', 'new_m_b_branch_gate', 'new_m_w_up_fox', 'new_m_w_up_gla', 'new_m_w_up_mla', 'new_m_w_out', 'new_m_g_xa', 'new_m_g_mem', 'new_m_w_xq', 'new_m_w_xkv', 'new_m_w_xo', 'new_m_g_mlp', 'new_m_w_mlp1', 'new_m_w_mlp2', 'new_m_g_final', 'new_v_g_mix', 'new_v_w_in', 'new_v_b_fox_forget', 'new_v_w_gla_gate', 'new_v_b_gla_gate', 'new_v_g_gla_out', 'new_v_g_mla_q', 'new_v_w_mla_uq', 'new_v_g_mla_kv', 'new_v_w_mla_ukv', 'new_v_b_branch_gate', 'new_v_w_up_fox', 'new_v_w_up_gla', 'new_v_w_up_mla', 'new_v_w_out', 'new_v_g_xa', 'new_v_g_mem', 'new_v_w_xq', 'new_v_w_xkv', 'new_v_w_xo', 'new_v_g_mlp', 'new_v_w_mlp1', 'new_v_w_mlp2', 'new_v_g_final']
TWIN_LEAF_KINDS = {'loss': 'loss', 'grad_x': 'grad_x', 'grad_g_mix': 'grad_w', 'grad_w_in': 'grad_w', 'grad_b_fox_forget': 'grad_w', 'grad_w_gla_gate': 'grad_w', 'grad_b_gla_gate': 'grad_w', 'grad_g_gla_out': 'grad_w', 'grad_g_mla_q': 'grad_w', 'grad_w_mla_uq': 'grad_w', 'grad_g_mla_kv': 'grad_w', 'grad_w_mla_ukv': 'grad_w', 'grad_b_branch_gate': 'grad_w', 'grad_w_up_fox': 'grad_w', 'grad_w_up_gla': 'grad_w', 'grad_w_up_mla': 'grad_w', 'grad_w_out': 'grad_w', 'grad_g_xa': 'grad_w', 'grad_g_mem': 'grad_w', 'grad_w_xq': 'grad_w', 'grad_w_xkv': 'grad_w', 'grad_w_xo': 'grad_w', 'grad_g_mlp': 'grad_w', 'grad_w_mlp1': 'grad_w', 'grad_w_mlp2': 'grad_w', 'grad_g_final': 'grad_w', 'delta_g_mix': 'delta_w', 'delta_w_in': 'delta_w', 'delta_b_fox_forget': 'delta_w', 'delta_w_gla_gate': 'delta_w', 'delta_b_gla_gate': 'delta_w', 'delta_g_gla_out': 'delta_w', 'delta_g_mla_q': 'delta_w', 'delta_w_mla_uq': 'delta_w', 'delta_g_mla_kv': 'delta_w', 'delta_w_mla_ukv': 'delta_w', 'delta_b_branch_gate': 'delta_w', 'delta_w_up_fox': 'delta_w', 'delta_w_up_gla': 'delta_w', 'delta_w_up_mla': 'delta_w', 'delta_w_out': 'delta_w', 'delta_g_xa': 'delta_w', 'delta_g_mem': 'delta_w', 'delta_w_xq': 'delta_w', 'delta_w_xkv': 'delta_w', 'delta_w_xo': 'delta_w', 'delta_g_mlp': 'delta_w', 'delta_w_mlp1': 'delta_w', 'delta_w_mlp2': 'delta_w', 'delta_g_final': 'delta_w', 'new_m_g_mix': 'new_m', 'new_m_w_in': 'new_m', 'new_m_b_fox_forget': 'new_m', 'new_m_w_gla_gate': 'new_m', 'new_m_b_gla_gate': 'new_m', 'new_m_g_gla_out': 'new_m', 'new_m_g_mla_q': 'new_m', 'new_m_w_mla_uq': 'new_m', 'new_m_g_mla_kv': 'new_m', 'new_m_w_mla_ukv': 'new_m', 'new_m_b_branch_gate': 'new_m', 'new_m_w_up_fox': 'new_m', 'new_m_w_up_gla': 'new_m', 'new_m_w_up_mla': 'new_m', 'new_m_w_out': 'new_m', 'new_m_g_xa': 'new_m', 'new_m_g_mem': 'new_m', 'new_m_w_xq': 'new_m', 'new_m_w_xkv': 'new_m', 'new_m_w_xo': 'new_m', 'new_m_g_mlp': 'new_m', 'new_m_w_mlp1': 'new_m', 'new_m_w_mlp2': 'new_m', 'new_m_g_final': 'new_m', 'new_v_g_mix': 'new_v', 'new_v_w_in': 'new_v', 'new_v_b_fox_forget': 'new_v', 'new_v_w_gla_gate': 'new_v', 'new_v_b_gla_gate': 'new_v', 'new_v_g_gla_out': 'new_v', 'new_v_g_mla_q': 'new_v', 'new_v_w_mla_uq': 'new_v', 'new_v_g_mla_kv': 'new_v', 'new_v_w_mla_ukv': 'new_v', 'new_v_b_branch_gate': 'new_v', 'new_v_w_up_fox': 'new_v', 'new_v_w_up_gla': 'new_v', 'new_v_w_up_mla': 'new_v', 'new_v_w_out': 'new_v', 'new_v_g_xa': 'new_v', 'new_v_g_mem': 'new_v', 'new_v_w_xq': 'new_v', 'new_v_w_xkv': 'new_v', 'new_v_w_xo': 'new_v', 'new_v_g_mlp': 'new_v', 'new_v_w_mlp1': 'new_v', 'new_v_w_mlp2': 'new_v', 'new_v_g_final': 'new_v'}


def _forward(args):
    return _fwd_reference(*[args[k] for k in FWD_PARAMS])


def _output_shape():
    out = _jax.eval_shape(lambda: _forward(_fwd_setup_inputs(0)))
    return out.shape, out.dtype

N_MICROBATCH = 1
ADAM_LR = 0.001
ADAM_B1 = 0.9
ADAM_B2 = 0.999
ADAM_EPS = 1e-08
ADAM_WD = 0.01
ADAM_STEP = 10
PER_EXAMPLE_BATCH_AXIS = {'x': 0, 'mem': 0, 'loss_target': 0}
SHARED_INPUTS = []
_WEIGHT_DTYPES = {'g_mix': _jnp.float32, 'w_in': _jnp.float32, 'b_fox_forget': _jnp.float32, 'w_gla_gate': _jnp.float32, 'b_gla_gate': _jnp.float32, 'g_gla_out': _jnp.float32, 'g_mla_q': _jnp.float32, 'w_mla_uq': _jnp.float32, 'g_mla_kv': _jnp.float32, 'w_mla_ukv': _jnp.float32, 'b_branch_gate': _jnp.float32, 'w_up_fox': _jnp.float32, 'w_up_gla': _jnp.float32, 'w_up_mla': _jnp.float32, 'w_out': _jnp.float32, 'g_xa': _jnp.float32, 'g_mem': _jnp.float32, 'w_xq': _jnp.float32, 'w_xkv': _jnp.float32, 'w_xo': _jnp.float32, 'g_mlp': _jnp.float32, 'w_mlp1': _jnp.float32, 'w_mlp2': _jnp.float32, 'g_final': _jnp.float32}
MOMENT_SCALE = {'g_mix': 1.355814e-01, 'w_in': 5.630708e-02, 'b_fox_forget': 3.259728e-01, 'w_gla_gate': 1.443271e-02, 'b_gla_gate': 4.899403e-02, 'g_gla_out': 1.457157e-01, 'g_mla_q': 2.487591e-02, 'w_mla_uq': 2.077453e-02, 'g_mla_kv': 6.785739e-02, 'w_mla_ukv': 3.153607e-02, 'b_branch_gate': 1.812974e-02, 'w_up_fox': 5.930535e-02, 'w_up_gla': 5.063858e-02, 'w_up_mla': 1.916117e-02, 'w_out': 8.031230e-02, 'g_xa': 1.680049e-02, 'g_mem': 2.498533e-02, 'w_xq': 2.325692e-02, 'w_xkv': 2.421781e-02, 'w_xo': 1.747615e-02, 'g_mlp': 1.457938e-01, 'w_mlp1': 7.375307e-02, 'w_mlp2': 1.448031e-01, 'g_final': 3.260499e+01}


def _to_microbatches(a, axis):
    t = _jnp.moveaxis(a, axis, 0)
    t = t.reshape((N_MICROBATCH, t.shape[0] // N_MICROBATCH) + t.shape[1:])
    return _jnp.moveaxis(t, 1, axis + 1)


def setup_inputs(seed: int = 0) -> dict:
    inp = _fwd_setup_inputs(seed)
    key = _jax.random.fold_in(_jax.random.key(seed), 7919)
    shape, _ = _output_shape()
    out = dict(inp)
    out["loss_target"] = _jax.random.normal(_jax.random.fold_in(key, 0), shape, _jnp.float32)
    for i, name in enumerate(TWIN_WEIGHTS):
        w = inp[name].astype(_jnp.float32)
        if MOMENT_SCALE is None:
            s = _jnp.sqrt(_jnp.mean(_jnp.square(w)) + 1e-30)
        else:
            s = MOMENT_SCALE[name]
        km, kv = _jax.random.split(_jax.random.fold_in(key, i + 1))
        out[name] = w
        out["m_" + name] = s * _jax.random.normal(km, w.shape, _jnp.float32)
        out["v_" + name] = (s * s) * _jax.random.uniform(kv, w.shape, _jnp.float32, 0.5, 1.5)
    if N_MICROBATCH > 1:
        for name, axis in PER_EXAMPLE_BATCH_AXIS.items():
            out[name] = _to_microbatches(out[name], axis)
    return {'x': out['x'], 'mem': out['mem'], 'g_mix': out['g_mix'], 'w_in': out['w_in'], 'b_fox_forget': out['b_fox_forget'], 'w_gla_gate': out['w_gla_gate'], 'b_gla_gate': out['b_gla_gate'], 'g_gla_out': out['g_gla_out'], 'g_mla_q': out['g_mla_q'], 'w_mla_uq': out['w_mla_uq'], 'g_mla_kv': out['g_mla_kv'], 'w_mla_ukv': out['w_mla_ukv'], 'b_branch_gate': out['b_branch_gate'], 'w_up_fox': out['w_up_fox'], 'w_up_gla': out['w_up_gla'], 'w_up_mla': out['w_up_mla'], 'w_out': out['w_out'], 'g_xa': out['g_xa'], 'g_mem': out['g_mem'], 'w_xq': out['w_xq'], 'w_xkv': out['w_xkv'], 'w_xo': out['w_xo'], 'g_mlp': out['g_mlp'], 'w_mlp1': out['w_mlp1'], 'w_mlp2': out['w_mlp2'], 'g_final': out['g_final'], 'loss_target': out['loss_target'], 'm_g_mix': out['m_g_mix'], 'm_w_in': out['m_w_in'], 'm_b_fox_forget': out['m_b_fox_forget'], 'm_w_gla_gate': out['m_w_gla_gate'], 'm_b_gla_gate': out['m_b_gla_gate'], 'm_g_gla_out': out['m_g_gla_out'], 'm_g_mla_q': out['m_g_mla_q'], 'm_w_mla_uq': out['m_w_mla_uq'], 'm_g_mla_kv': out['m_g_mla_kv'], 'm_w_mla_ukv': out['m_w_mla_ukv'], 'm_b_branch_gate': out['m_b_branch_gate'], 'm_w_up_fox': out['m_w_up_fox'], 'm_w_up_gla': out['m_w_up_gla'], 'm_w_up_mla': out['m_w_up_mla'], 'm_w_out': out['m_w_out'], 'm_g_xa': out['m_g_xa'], 'm_g_mem': out['m_g_mem'], 'm_w_xq': out['m_w_xq'], 'm_w_xkv': out['m_w_xkv'], 'm_w_xo': out['m_w_xo'], 'm_g_mlp': out['m_g_mlp'], 'm_w_mlp1': out['m_w_mlp1'], 'm_w_mlp2': out['m_w_mlp2'], 'm_g_final': out['m_g_final'], 'v_g_mix': out['v_g_mix'], 'v_w_in': out['v_w_in'], 'v_b_fox_forget': out['v_b_fox_forget'], 'v_w_gla_gate': out['v_w_gla_gate'], 'v_b_gla_gate': out['v_b_gla_gate'], 'v_g_gla_out': out['v_g_gla_out'], 'v_g_mla_q': out['v_g_mla_q'], 'v_w_mla_uq': out['v_w_mla_uq'], 'v_g_mla_kv': out['v_g_mla_kv'], 'v_w_mla_ukv': out['v_w_mla_ukv'], 'v_b_branch_gate': out['v_b_branch_gate'], 'v_w_up_fox': out['v_w_up_fox'], 'v_w_up_gla': out['v_w_up_gla'], 'v_w_up_mla': out['v_w_up_mla'], 'v_w_out': out['v_w_out'], 'v_g_xa': out['v_g_xa'], 'v_g_mem': out['v_g_mem'], 'v_w_xq': out['v_w_xq'], 'v_w_xkv': out['v_w_xkv'], 'v_w_xo': out['v_w_xo'], 'v_g_mlp': out['v_g_mlp'], 'v_w_mlp1': out['v_w_mlp1'], 'v_w_mlp2': out['v_w_mlp2'], 'v_g_final': out['v_g_final']}


def _loss(weights, diff, rest, loss_target):
    with _jax.named_scope("forward"):
        args = {**rest, TWIN_DIFF_INPUT: diff, **{k: w.astype(_WEIGHT_DTYPES[k]) for k, w in weights.items()}}
        y = _forward(args)
    with _jax.named_scope("loss_head"):
        err = _jnp.square(y.astype(_jnp.float32) - loss_target)
        return 0.5 * _jnp.sum(_jnp.mean(err, axis=-1)) if err.ndim else 0.5 * err


def _adamw(w, g, m, v):
    m = ADAM_B1 * m + (1.0 - ADAM_B1) * g
    v = ADAM_B2 * v + (1.0 - ADAM_B2) * _jnp.square(g)
    m_hat = m / (1.0 - ADAM_B1 ** ADAM_STEP)
    v_hat = v / (1.0 - ADAM_B2 ** ADAM_STEP)
    delta = -ADAM_LR * (m_hat / (_jnp.sqrt(v_hat) + ADAM_EPS) + ADAM_WD * w)
    return delta, m, v


def reference(x, mem, g_mix, w_in, b_fox_forget, w_gla_gate, b_gla_gate, g_gla_out, g_mla_q, w_mla_uq, g_mla_kv, w_mla_ukv, b_branch_gate, w_up_fox, w_up_gla, w_up_mla, w_out, g_xa, g_mem, w_xq, w_xkv, w_xo, g_mlp, w_mlp1, w_mlp2, g_final, loss_target, m_g_mix, m_w_in, m_b_fox_forget, m_w_gla_gate, m_b_gla_gate, m_g_gla_out, m_g_mla_q, m_w_mla_uq, m_g_mla_kv, m_w_mla_ukv, m_b_branch_gate, m_w_up_fox, m_w_up_gla, m_w_up_mla, m_w_out, m_g_xa, m_g_mem, m_w_xq, m_w_xkv, m_w_xo, m_g_mlp, m_w_mlp1, m_w_mlp2, m_g_final, v_g_mix, v_w_in, v_b_fox_forget, v_w_gla_gate, v_b_gla_gate, v_g_gla_out, v_g_mla_q, v_w_mla_uq, v_g_mla_kv, v_w_mla_ukv, v_b_branch_gate, v_w_up_fox, v_w_up_gla, v_w_up_mla, v_w_out, v_g_xa, v_g_mem, v_w_xq, v_w_xkv, v_w_xo, v_g_mlp, v_w_mlp1, v_w_mlp2, v_g_final):
    given = dict(x=x, mem=mem, g_mix=g_mix, w_in=w_in, b_fox_forget=b_fox_forget, w_gla_gate=w_gla_gate, b_gla_gate=b_gla_gate, g_gla_out=g_gla_out, g_mla_q=g_mla_q, w_mla_uq=w_mla_uq, g_mla_kv=g_mla_kv, w_mla_ukv=w_mla_ukv, b_branch_gate=b_branch_gate, w_up_fox=w_up_fox, w_up_gla=w_up_gla, w_up_mla=w_up_mla, w_out=w_out, g_xa=g_xa, g_mem=g_mem, w_xq=w_xq, w_xkv=w_xkv, w_xo=w_xo, g_mlp=g_mlp, w_mlp1=w_mlp1, w_mlp2=w_mlp2, g_final=g_final, loss_target=loss_target, m_g_mix=m_g_mix, m_w_in=m_w_in, m_b_fox_forget=m_b_fox_forget, m_w_gla_gate=m_w_gla_gate, m_b_gla_gate=m_b_gla_gate, m_g_gla_out=m_g_gla_out, m_g_mla_q=m_g_mla_q, m_w_mla_uq=m_w_mla_uq, m_g_mla_kv=m_g_mla_kv, m_w_mla_ukv=m_w_mla_ukv, m_b_branch_gate=m_b_branch_gate, m_w_up_fox=m_w_up_fox, m_w_up_gla=m_w_up_gla, m_w_up_mla=m_w_up_mla, m_w_out=m_w_out, m_g_xa=m_g_xa, m_g_mem=m_g_mem, m_w_xq=m_w_xq, m_w_xkv=m_w_xkv, m_w_xo=m_w_xo, m_g_mlp=m_g_mlp, m_w_mlp1=m_w_mlp1, m_w_mlp2=m_w_mlp2, m_g_final=m_g_final, v_g_mix=v_g_mix, v_w_in=v_w_in, v_b_fox_forget=v_b_fox_forget, v_w_gla_gate=v_w_gla_gate, v_b_gla_gate=v_b_gla_gate, v_g_gla_out=v_g_gla_out, v_g_mla_q=v_g_mla_q, v_w_mla_uq=v_w_mla_uq, v_g_mla_kv=v_g_mla_kv, v_w_mla_ukv=v_w_mla_ukv, v_b_branch_gate=v_b_branch_gate, v_w_up_fox=v_w_up_fox, v_w_up_gla=v_w_up_gla, v_w_up_mla=v_w_up_mla, v_w_out=v_w_out, v_g_xa=v_g_xa, v_g_mem=v_g_mem, v_w_xq=v_w_xq, v_w_xkv=v_w_xkv, v_w_xo=v_w_xo, v_g_mlp=v_g_mlp, v_w_mlp1=v_w_mlp1, v_w_mlp2=v_w_mlp2, v_g_final=v_g_final)
    weights = {n: given[n] for n in TWIN_WEIGHTS}
    shared = {n: given[n] for n in SHARED_INPUTS}
    per_example = {n: given[n] for n in ['x', 'mem']}
    grad_fn = _jax.value_and_grad(_loss, argnums=(0, 1))

    def one_microbatch(ex, loss_target):
        ex = dict(ex)
        diff = ex.pop(TWIN_DIFF_INPUT)
        return grad_fn(weights, diff, {**shared, **ex}, loss_target)

    if N_MICROBATCH == 1:
        loss, (grad_w, grad_x) = one_microbatch(per_example, given["loss_target"])
    else:
        def body(carry, xs):
            loss_sum, grad_sum = carry
            l_k, (gw_k, gx_k) = one_microbatch(xs[0], xs[1])
            with _jax.named_scope("update"):
                return (loss_sum + l_k, _jax.tree.map(_jnp.add, grad_sum, gw_k)), gx_k

        init = (_jnp.zeros((), _jnp.float32), _jax.tree.map(_jnp.zeros_like, weights))
        (loss, grad_w), grad_x = _jax.lax.scan(body, init, (per_example, given["loss_target"]))
    with _jax.named_scope("update"):
        delta_w, new_m, new_v = {}, {}, {}
        for n in TWIN_WEIGHTS:
            delta_w[n], new_m[n], new_v[n] = _adamw(weights[n], grad_w[n], given["m_" + n], given["v_" + n])
    return (loss, grad_x, *[grad_w[n] for n in TWIN_WEIGHTS], *[delta_w[n] for n in TWIN_WEIGHTS],
            *[new_m[n] for n in TWIN_WEIGHTS], *[new_v[n] for n in TWIN_WEIGHTS])
```

```python
import jax
import jax.numpy as jnp
import numpy as np
from jax import lax
from jax.experimental import pallas as pl
from jax.experimental.pallas import tpu as pltpu

F32 = jnp.float32
BF16 = jnp.bfloat16

EPS = 1e-6
CHUNK = 64
FOX_HEADS, FOX_HD = 4, 64
GLA_HEADS, GLA_DK, GLA_DV, GLA_RANK, GLA_TAU = 4, 64, 128, 16, 16.0
MLA_HEADS, MLA_Q_RANK, MLA_KV_RANK, MLA_NOPE, MLA_ROPE, MLA_VD = 4, 256, 128, 64, 32, 64
ROPE_BASE = 10000.0
XA_HEADS, XA_HD = 4, 128
ADAM_LR, ADAM_B1, ADAM_B2, ADAM_EPS, ADAM_WD, ADAM_STEP = 0.001, 0.9, 0.999, 1e-08, 0.01, 10

N_DEV = 8
V7X_VMEM_LIMIT = 56 * 1024 * 1024
NEG = -1e30

NN = ((1,), (0,))
NT = ((1,), (1,))
TN = ((0,), (0,))


def _dot(a, b, dims):
    return lax.dot_general(a.astype(BF16), b.astype(BF16), (dims, ((), ())), preferred_element_type=F32)


@jax.custom_vjp
def bdot(a, b):
    return _dot(a, b, NN)


bdot.defvjp(lambda a, b: (_dot(a, b, NN), (a, b)),
            lambda res, g: (_dot(g, res[1], NT), _dot(res[0], g, TN)))


@jax.custom_vjp
def bdot_nt(a, b):
    return _dot(a, b, NT)


bdot_nt.defvjp(lambda a, b: (_dot(a, b, NT), (a, b)),
               lambda res, g: (_dot(g, res[1], NN), _dot(g, res[0], TN)))


@jax.custom_vjp
def bdot_tn(a, b):
    return _dot(a, b, TN)


bdot_tn.defvjp(lambda a, b: (_dot(a, b, TN), (a, b)),
               lambda res, g: (_dot(res[1], g, NT), _dot(res[0], g, NN)))


def _split2(x):
    hi = x.astype(BF16)
    lo = (x - hi.astype(F32)).astype(BF16)
    return hi, lo


def _tri(n, lower):
    r = lax.broadcasted_iota(jnp.int32, (n, n), 0)
    c = lax.broadcasted_iota(jnp.int32, (n, n), 1)
    return jnp.where((r >= c) if lower else (r <= c), 1.0, 0.0).astype(BF16)


def _tri_dot2(x, lower):
    hi, lo = _split2(x)
    t = _tri(x.shape[0], lower)
    return _dot(t, hi, NN) + _dot(t, lo, NN)


@jax.custom_vjp
def chunk_cumsum(x):
    return _tri_dot2(x, True)


chunk_cumsum.defvjp(lambda x: (_tri_dot2(x, True), None), lambda _, g: (_tri_dot2(g, False),))


def _log_sigmoid(x):
    return jnp.minimum(x, 0.0) - jnp.log(1.0 + jnp.exp(-jnp.abs(x)))


def _sigmoid(x):
    return 1.0 / (1.0 + jnp.exp(-x))


def _rms(x, g):
    return x * lax.rsqrt(jnp.mean(x * x, axis=-1, keepdims=True) + EPS) * g


def _pick(dim, prefs):
    for p in prefs:
        if dim % p == 0:
            return p
    return dim


def _params(sem):
    return pltpu.CompilerParams(dimension_semantics=sem, vmem_limit_bytes=V7X_VMEM_LIMIT)


def _mm(a, b, *, mode, out_dtype, name, act=None, residual=None, drelu_of=None, tm=None, tn=None, tk=None):
    if mode == 'nn':
        (M, K), N = a.shape, b.shape[1]
    elif mode == 'nt':
        (M, K), N = a.shape, b.shape[0]
    else:
        (K, M), N = a.shape, b.shape[1]
    tm = tm or _pick(M, (1024, 512, 256, 128))
    tn = tn or _pick(N, (1024, 768, 640, 512, 384, 256, 128))
    tk = tk or _pick(K, (1024, 512, 256, 128))
    nk = K // tk
    dims = {'nn': NN, 'nt': NT, 'tn': TN}[mode]
    a_spec = pl.BlockSpec((tk, tm), lambda i, j, k: (k, i)) if mode == 'tn' else pl.BlockSpec((tm, tk), lambda i, j, k: (i, k))
    b_spec = pl.BlockSpec((tn, tk), lambda i, j, k: (j, k)) if mode == 'nt' else pl.BlockSpec((tk, tn), lambda i, j, k: (k, j))
    o_spec = pl.BlockSpec((tm, tn), lambda i, j, k: (i, j))
    extra = [e for e in (residual, drelu_of) if e is not None]

    def body(a_ref, b_ref, *rest):
        o_ref = rest[len(extra)]
        at = a_ref[...]
        if act == 'relu2':
            at = jnp.square(jnp.maximum(at.astype(F32), 0.0))
        part = _dot(at, b_ref[...], dims)

        def finish(acc):
            idx = 0
            if residual is not None:
                acc = acc + rest[idx][...]
                idx += 1
            if drelu_of is not None:
                acc = acc * (2.0 * jnp.maximum(rest[idx][...].astype(F32), 0.0))
            o_ref[...] = acc.astype(out_dtype)

        if nk == 1:
            finish(part)
        else:
            acc_ref = rest[len(extra) + 1]
            k = pl.program_id(2)

            @pl.when(k == 0)
            def _():
                acc_ref[...] = part

            @pl.when(k > 0)
            def _():
                acc_ref[...] += part

            @pl.when(k == nk - 1)
            def _():
                finish(acc_ref[...])

    return pl.pallas_call(
        body, name=name,
        out_shape=jax.ShapeDtypeStruct((M, N), out_dtype),
        grid=(M // tm, N // tn, nk),
        in_specs=[a_spec, b_spec] + [o_spec] * len(extra),
        out_specs=o_spec,
        scratch_shapes=[] if nk == 1 else [pltpu.VMEM((tm, tn), F32)],
        compiler_params=_params(("parallel", "parallel", "arbitrary")),
    )(a, b, *extra)


def _rms_fwd(x, g, *, name, out_dtype=BF16):
    S, D = x.shape
    tr = _pick(S, (512, 256, 128))

    def body(x_ref, g_ref, o_ref):
        o_ref[...] = _rms(x_ref[...], g_ref[...]).astype(out_dtype)

    return pl.pallas_call(
        body, name=name, out_shape=jax.ShapeDtypeStruct((S, D), out_dtype), grid=(S // tr,),
        in_specs=[pl.BlockSpec((tr, D), lambda i: (i, 0)), pl.BlockSpec((1, D), lambda i: (0, 0))],
        out_specs=pl.BlockSpec((tr, D), lambda i: (i, 0)),
        compiler_params=_params(("parallel",)),
    )(x, g.reshape(1, D))


def _rms_bwd(x, g, dy, dres, *, name):
    S, D = x.shape
    tr = _pick(S, (512, 256, 128))

    def body(x_ref, g_ref, dy_ref, *rest):
        dx_ref, dg_ref = rest[-2], rest[-1]
        x_ = x_ref[...]
        rstd = lax.rsqrt(jnp.mean(x_ * x_, axis=-1, keepdims=True) + EPS)
        xh = x_ * rstd
        dy_ = dy_ref[...].astype(F32)
        gdy = dy_ * g_ref[...]
        dx = (gdy - xh * jnp.mean(gdy * xh, axis=-1, keepdims=True)) * rstd
        if dres is not None:
            dx = dx + rest[0][...]
        dx_ref[...] = dx
        part = jnp.sum(dy_ * xh, axis=0, keepdims=True)

        @pl.when(pl.program_id(0) == 0)
        def _():
            dg_ref[...] = part

        @pl.when(pl.program_id(0) > 0)
        def _():
            dg_ref[...] += part

    row = pl.BlockSpec((tr, D), lambda i: (i, 0))
    vec = pl.BlockSpec((1, D), lambda i: (0, 0))
    ins = [x, g.reshape(1, D), dy] + ([dres] if dres is not None else [])
    return pl.pallas_call(
        body, name=name,
        out_shape=(jax.ShapeDtypeStruct((S, D), F32), jax.ShapeDtypeStruct((1, D), F32)),
        grid=(S // tr,),
        in_specs=[row, vec, row] + ([row] if dres is not None else []),
        out_specs=(row, vec),
        compiler_params=_params(("arbitrary",)),
    )(*ins)


def _loss_head(x, g, target, *, name):
    S, D = x.shape
    tr = _pick(S, (512, 256, 128))

    def body(x_ref, g_ref, t_ref, l_ref, dx_ref, dg_ref):
        x_ = x_ref[...]
        g_ = g_ref[...]
        rstd = lax.rsqrt(jnp.mean(x_ * x_, axis=-1, keepdims=True) + EPS)
        xh = x_ * rstd
        err = xh * g_ - t_ref[...]
        lpart = (0.5 / D) * jnp.sum(jnp.sum(err * err, axis=-1, keepdims=True), axis=0, keepdims=True)
        dy = err * (1.0 / D)
        gdy = dy * g_
        dx_ref[...] = (gdy - xh * jnp.mean(gdy * xh, axis=-1, keepdims=True)) * rstd
        gpart = jnp.sum(dy * xh, axis=0, keepdims=True)

        @pl.when(pl.program_id(0) == 0)
        def _():
            dg_ref[...] = gpart
            l_ref[...] = lpart

        @pl.when(pl.program_id(0) > 0)
        def _():
            dg_ref[...] += gpart
            l_ref[...] += lpart

    row = pl.BlockSpec((tr, D), lambda i: (i, 0))
    vec = pl.BlockSpec((1, D), lambda i: (0, 0))
    return pl.pallas_call(
        body, name=name,
        out_shape=(jax.ShapeDtypeStruct((1, 1), F32), jax.ShapeDtypeStruct((S, D), F32), jax.ShapeDtypeStruct((1, D), F32)),
        grid=(S // tr,),
        in_specs=[row, vec, row],
        out_specs=(pl.BlockSpec((1, 1), lambda i: (0, 0)), row, vec),
        compiler_params=_params(("arbitrary",)),
    )(x, g.reshape(1, D), target)


def _mask_of(mask, tq, tk):
    qpos = lax.broadcasted_iota(jnp.int32, (tq, tk), 0)
    kpos = lax.broadcasted_iota(jnp.int32, (tq, tk), 1)
    if mask == 'causal':
        return kpos <= qpos
    return kpos <= (qpos | (CHUNK - 1))


def _flash_fwd(q, k, v, cq, ck, *, scale, mask, name):
    H, Sq, dk = q.shape
    Sk, dv = k.shape[1], v.shape[2]
    tq = _pick(Sq, (512, 256, 128))
    tk = tq if mask else _pick(Sk, (512, 256, 128))
    nq, nk = Sq // tq, Sk // tk
    bias = cq is not None

    def body(*refs):
        q_ref, k_ref, v_ref = refs[:3]
        cq_ref, ck_ref = (refs[3], refs[4]) if bias else (None, None)
        o_ref, lse_ref, m_s, l_s, acc_s = refs[-5:]
        qi, ki = pl.program_id(0), pl.program_id(1)

        @pl.when(ki == 0)
        def _():
            m_s[...] = jnp.full(m_s.shape, NEG, F32)
            l_s[...] = jnp.zeros(l_s.shape, F32)
            acc_s[...] = jnp.zeros(acc_s.shape, F32)

        def compute(masked):
            keep = _mask_of(mask, tq, tk) if masked else None
            for h in range(H):
                s = _dot(q_ref[h], k_ref[h], NT) * scale
                if bias:
                    s = s + (cq_ref[:, h:h + 1] - ck_ref[h:h + 1, :])
                if masked:
                    s = jnp.where(keep, s, NEG)
                m_prev = m_s[h]
                m_new = jnp.maximum(m_prev, jnp.max(s, axis=1, keepdims=True))
                alpha = jnp.exp(m_prev - m_new)
                p = jnp.exp(s - m_new)
                l_s[h] = alpha * l_s[h] + jnp.sum(p, axis=1, keepdims=True)
                acc_s[h] = alpha * acc_s[h] + _dot(p, v_ref[h], NN)
                m_s[h] = m_new

        if mask is None:
            compute(False)
        else:
            pl.when(ki < qi)(lambda: compute(False))
            pl.when(ki == qi)(lambda: compute(True))

        @pl.when(ki == ((nk - 1) if mask is None else qi))
        def _():
            lse_ref[...] = jnp.zeros(lse_ref.shape, F32)
            for h in range(H):
                o_ref[h] = (acc_s[h] / l_s[h]).astype(BF16)
                lse_ref[:, h:h + 1] = m_s[h] + jnp.log(l_s[h])

    kv_idx = (lambda i, j: (0, jnp.minimum(i, j), 0)) if mask else (lambda i, j: (0, j, 0))
    ck_idx = (lambda i, j: (0, jnp.minimum(i, j))) if mask else (lambda i, j: (0, j))
    in_specs = [pl.BlockSpec((H, tq, dk), lambda i, j: (0, i, 0)),
                pl.BlockSpec((H, tk, dk), kv_idx), pl.BlockSpec((H, tk, dv), kv_idx)]
    ins = [q, k, v]
    if bias:
        in_specs += [pl.BlockSpec((tq, 8), lambda i, j: (i, 0)), pl.BlockSpec((8, tk), ck_idx)]
        ins += [cq, ck]
    return pl.pallas_call(
        body, name=name,
        out_shape=(jax.ShapeDtypeStruct((H, Sq, dv), BF16), jax.ShapeDtypeStruct((Sq, 8), F32)),
        grid=(nq, nk), in_specs=in_specs,
        out_specs=(pl.BlockSpec((H, tq, dv), lambda i, j: (0, i, 0)), pl.BlockSpec((tq, 8), lambda i, j: (i, 0))),
        scratch_shapes=[pltpu.VMEM((H, tq, 1), F32), pltpu.VMEM((H, tq, 1), F32), pltpu.VMEM((H, tq, dv), F32)],
        compiler_params=_params(("parallel", "arbitrary")),
    )(*ins)


def _flash_bwd(q, k, v, o, do, lse, cq, ck, *, scale, mask, name):
    H, Sq, dk = q.shape
    Sk, dv = k.shape[1], v.shape[2]
    tq = _pick(Sq, (512, 256, 128))
    tk = tq if mask else _pick(Sk, (512, 256, 128))
    nq, nk = Sq // tq, Sk // tk
    bias = cq is not None

    def body(*refs):
        q_ref, k_ref, v_ref, o_ref, do_ref, lse_ref = refs[:6]
        n_in = 8 if bias else 6
        cq_ref, ck_ref = (refs[6], refs[7]) if bias else (None, None)
        outs = refs[n_in:]
        dq_ref, dk_ref, dv_ref = outs[:3]
        dck_ref, dcq_ref = (outs[3], outs[4]) if bias else (None, None)
        dk_s, dv_s = refs[-2], refs[-1]
        ki, qi = pl.program_id(0), pl.program_id(1)
        first_q = ki if mask else 0

        @pl.when((ki == 0) & (qi == 0))
        def _():
            dq_ref[...] = jnp.zeros(dq_ref.shape, F32)
            if bias:
                dcq_ref[...] = jnp.zeros(dcq_ref.shape, F32)

        @pl.when(qi == first_q)
        def _():
            dk_s[...] = jnp.zeros(dk_s.shape, F32)
            dv_s[...] = jnp.zeros(dv_s.shape, F32)
            if bias:
                dck_ref[...] = jnp.zeros(dck_ref.shape, F32)

        def compute(masked):
            keep = _mask_of(mask, tq, tk) if masked else None
            rows = pl.ds(pl.multiple_of(qi * tq, tq), tq)
            for h in range(H):
                qh, kh, vh, doh = q_ref[h], k_ref[h], v_ref[h], do_ref[h]
                s = _dot(qh, kh, NT) * scale
                if bias:
                    s = s + (cq_ref[:, h:h + 1] - ck_ref[h:h + 1, :])
                if masked:
                    s = jnp.where(keep, s, NEG)
                p = jnp.exp(s - lse_ref[:, h:h + 1])
                dp = _dot(doh, vh, NT)
                delta = jnp.sum(doh.astype(F32) * o_ref[h].astype(F32), axis=1, keepdims=True)
                ds = p * (dp - delta)
                dv_s[h] += _dot(p, doh, TN)
                dk_s[h] += _dot(ds, qh, TN)
                dq_ref[h, rows, :] += _dot(ds, kh, NN) * scale
                if bias:
                    dck_ref[h:h + 1, :] -= jnp.sum(ds, axis=0, keepdims=True)
                    dcq_ref[rows, h:h + 1] += jnp.sum(ds, axis=1, keepdims=True)

        if mask is None:
            compute(False)
        else:
            pl.when(qi > ki)(lambda: compute(False))
            pl.when(qi == ki)(lambda: compute(True))

        @pl.when(qi == nq - 1)
        def _():
            dk_ref[...] = dk_s[...] * scale
            dv_ref[...] = dv_s[...]

    q_idx = (lambda j, i: (0, jnp.maximum(i, j), 0)) if mask else (lambda j, i: (0, i, 0))
    c_idx = (lambda j, i: (jnp.maximum(i, j), 0)) if mask else (lambda j, i: (i, 0))
    kv_idx = lambda j, i: (0, j, 0)
    in_specs = [pl.BlockSpec((H, tq, dk), q_idx), pl.BlockSpec((H, tk, dk), kv_idx), pl.BlockSpec((H, tk, dv), kv_idx),
                pl.BlockSpec((H, tq, dv), q_idx), pl.BlockSpec((H, tq, dv), q_idx), pl.BlockSpec((tq, 8), c_idx)]
    ins = [q, k, v, o, do, lse]
    out_shape = [jax.ShapeDtypeStruct((H, Sq, dk), F32), jax.ShapeDtypeStruct((H, Sk, dk), F32),
                 jax.ShapeDtypeStruct((H, Sk, dv), F32)]
    out_specs = [pl.BlockSpec((H, Sq, dk), lambda j, i: (0, 0, 0)), pl.BlockSpec((H, tk, dk), kv_idx),
                 pl.BlockSpec((H, tk, dv), kv_idx)]
    if bias:
        in_specs += [pl.BlockSpec((tq, 8), c_idx), pl.BlockSpec((8, tk), lambda j, i: (0, j))]
        ins += [cq, ck]
        out_shape += [jax.ShapeDtypeStruct((8, Sk), F32), jax.ShapeDtypeStruct((Sq, 8), F32)]
        out_specs += [pl.BlockSpec((8, tk), lambda j, i: (0, j)), pl.BlockSpec((Sq, 8), lambda j, i: (0, 0))]
    return pl.pallas_call(
        body, name=name, out_shape=tuple(out_shape), grid=(nk, nq), in_specs=in_specs, out_specs=tuple(out_specs),
        scratch_shapes=[pltpu.VMEM((H, tk, dk), F32), pltpu.VMEM((H, tk, dv), F32)],
        compiler_params=_params(("arbitrary", "arbitrary")),
    )(*ins)


def _split3_dot(x, t):
    hi = x.astype(BF16)
    r1 = x - hi.astype(F32)
    mid = r1.astype(BF16)
    lo = (r1 - mid.astype(F32)).astype(BF16)
    return _dot(hi, t, NN) + _dot(mid, t, NN) + _dot(lo, t, NN)


def _fox_cum_fwd(ff_t, b, *, name):
    _, S = ff_t.shape
    tb = _pick(S, (512, 256, 128))

    def body(f_ref, b_ref, o_ref, carry):
        @pl.when(pl.program_id(0) == 0)
        def _():
            carry[...] = jnp.zeros(carry.shape, F32)

        lf = _log_sigmoid(f_ref[...] + b_ref[...])
        o_ref[...] = _split3_dot(lf, _tri(tb, False)) + carry[...]
        carry[...] += jnp.sum(lf, axis=1, keepdims=True)

    return pl.pallas_call(
        body, name=name, out_shape=jax.ShapeDtypeStruct((8, S), F32), grid=(S // tb,),
        in_specs=[pl.BlockSpec((8, tb), lambda i: (0, i)), pl.BlockSpec((8, 1), lambda i: (0, 0))],
        out_specs=pl.BlockSpec((8, tb), lambda i: (0, i)),
        scratch_shapes=[pltpu.VMEM((8, 1), F32)],
        compiler_params=_params(("arbitrary",)),
    )(ff_t, b)


def _fox_cum_bwd(ff_t, b, dcum_t, *, name):
    _, S = ff_t.shape
    tb = _pick(S, (512, 256, 128))
    nb = S // tb

    def body(f_ref, b_ref, dc_ref, df_ref, db_ref, carry):
        @pl.when(pl.program_id(0) == 0)
        def _():
            carry[...] = jnp.zeros(carry.shape, F32)
            db_ref[...] = jnp.zeros(db_ref.shape, F32)

        dc = dc_ref[...]
        dlf = _split3_dot(dc, _tri(tb, True)) + carry[...]
        carry[...] += jnp.sum(dc, axis=1, keepdims=True)
        df = dlf * _sigmoid(-(f_ref[...] + b_ref[...]))
        df_ref[...] = df
        db_ref[...] += jnp.sum(df, axis=1, keepdims=True)

    rev = lambda i: (0, nb - 1 - i)
    return pl.pallas_call(
        body, name=name,
        out_shape=(jax.ShapeDtypeStruct((8, S), F32), jax.ShapeDtypeStruct((8, 1), F32)), grid=(nb,),
        in_specs=[pl.BlockSpec((8, tb), rev), pl.BlockSpec((8, 1), lambda i: (0, 0)), pl.BlockSpec((8, tb), rev)],
        out_specs=(pl.BlockSpec((8, tb), rev), pl.BlockSpec((8, 1), lambda i: (0, 0))),
        scratch_shapes=[pltpu.VMEM((8, 1), F32)],
        compiler_params=_params(("arbitrary",)),
    )(ff_t, b, dcum_t)


GLA_W = GLA_HEADS * GLA_DK
GLA_BLOCK_CHUNKS = 4


def _gla_chunk(q, k, zsm, wg, bg, go, vs, rs, states):
    la = _log_sigmoid(bdot(zsm, wg) + bg) * (1.0 / GLA_TAU)
    cum = chunk_cumsum(la)
    end = jnp.sum(la, axis=0, keepdims=True)
    kd = k * jnp.exp(end - cum)
    a = jnp.exp(end)
    qs = q * (GLA_DK ** -0.5)
    lane = lax.broadcasted_iota(jnp.int32, (1, GLA_W), 1)
    outs, new_states = [], []
    for h in range(GLA_HEADS):
        head = jnp.where((lane >= h * GLA_DK) & (lane < (h + 1) * GLA_DK), 1.0, 0.0)
        st = states[h] * a + bdot_tn(vs[h], kd * head)
        o = bdot_nt(qs, st)
        o = _rms(o, go)
        outs.append(o * (rs[h] * _sigmoid(rs[h])))
        new_states.append(st)
    return outs, new_states


def _gla_fwd(z, zsm, wg, bg, go, cols, *, name):
    S = z.shape[0]
    rb = GLA_BLOCK_CHUNKS * CHUNK
    nb = S // rb
    cq, ckk, cv, cr = cols
    H = GLA_HEADS

    def body(q_ref, k_ref, zsm_ref, wg_ref, bg_ref, go_ref, *rest):
        v_refs, r_refs = rest[:H], rest[H:2 * H]
        o_ref, st_ref, state = rest[2 * H], rest[2 * H + 1], rest[2 * H + 2]

        @pl.when(pl.program_id(0) == 0)
        def _():
            state[...] = jnp.zeros(state.shape, F32)

        wg_, bg_, go_ = wg_ref[...], bg_ref[...], go_ref[...]
        for c in range(GLA_BLOCK_CHUNKS):
            rows = pl.ds(c * CHUNK, CHUNK)
            states = [state[h] for h in range(H)]
            for h in range(H):
                st_ref[c, h] = states[h]
            outs, new_states = _gla_chunk(
                q_ref[rows, :].astype(F32), k_ref[rows, :].astype(F32), zsm_ref[rows, :], wg_, bg_, go_,
                [v_refs[h][rows, :].astype(F32) for h in range(H)], [r_refs[h][rows, :].astype(F32) for h in range(H)], states)
            for h in range(H):
                o_ref[rows, h * GLA_DV:(h + 1) * GLA_DV] = outs[h].astype(BF16)
                state[h] = new_states[h]

    def col(width, off):
        return pl.BlockSpec((rb, width), lambda i, o=off // width: (i, o))

    full = lambda shp: pl.BlockSpec(shp, lambda i: (0,) * len(shp))
    in_specs = [col(GLA_W, cq), col(GLA_W, ckk), pl.BlockSpec((rb, 128), lambda i: (i, 0)),
                full((128, GLA_W)), full((1, GLA_W)), full((1, GLA_DV))]
    in_specs += [col(GLA_DV, cv + h * GLA_DV) for h in range(H)] + [col(GLA_DV, cr + h * GLA_DV) for h in range(H)]
    return pl.pallas_call(
        body, name=name,
        out_shape=(jax.ShapeDtypeStruct((S, H * GLA_DV), BF16), jax.ShapeDtypeStruct((S // CHUNK, H, GLA_DV, GLA_W), F32)),
        grid=(nb,), in_specs=in_specs,
        out_specs=(pl.BlockSpec((rb, H * GLA_DV), lambda i: (i, 0)),
                   pl.BlockSpec((GLA_BLOCK_CHUNKS, H, GLA_DV, GLA_W), lambda i: (i, 0, 0, 0))),
        scratch_shapes=[pltpu.VMEM((H, GLA_DV, GLA_W), F32)],
        compiler_params=_params(("arbitrary",)),
    )(z, z, zsm, wg, bg, go, *([z] * (2 * H)))


def _gla_bwd(z, zsm, wg, bg, go, states, do, cols, *, name):
    S = z.shape[0]
    rb = GLA_BLOCK_CHUNKS * CHUNK
    nb = S // rb
    cq, ckk, cv, cr = cols
    H = GLA_HEADS

    def body(q_ref, k_ref, zsm_ref, wg_ref, bg_ref, go_ref, st_ref, do_ref, *rest):
        v_refs, r_refs = rest[:H], rest[H:2 * H]
        dq_ref, dk_ref, dv_ref, dr_ref, dzsm_ref, dwg_ref, dbg_ref, dgo_ref, dstate = rest[2 * H:]

        @pl.when(pl.program_id(0) == 0)
        def _():
            dstate[...] = jnp.zeros(dstate.shape, F32)
            dwg_ref[...] = jnp.zeros(dwg_ref.shape, F32)
            dbg_ref[...] = jnp.zeros(dbg_ref.shape, F32)
            dgo_ref[...] = jnp.zeros(dgo_ref.shape, F32)

        wg_, bg_, go_ = wg_ref[...], bg_ref[...], go_ref[...]
        for c in reversed(range(GLA_BLOCK_CHUNKS)):
            rows = pl.ds(c * CHUNK, CHUNK)
            prim = (q_ref[rows, :].astype(F32), k_ref[rows, :].astype(F32), zsm_ref[rows, :], wg_, bg_, go_,
                    [v_refs[h][rows, :].astype(F32) for h in range(H)], [r_refs[h][rows, :].astype(F32) for h in range(H)],
                    [st_ref[c, h] for h in range(H)])
            _, vjp = jax.vjp(_gla_chunk, *prim)
            douts = [do_ref[rows, h * GLA_DV:(h + 1) * GLA_DV].astype(F32) for h in range(H)]
            dq, dk, dzs, dwg, dbg, dgo, dvs, drs, dsts = vjp((douts, [dstate[h] for h in range(H)]))
            dq_ref[rows, :] = dq.astype(BF16)
            dk_ref[rows, :] = dk.astype(BF16)
            dzsm_ref[rows, :] = dzs
            dwg_ref[...] += dwg
            dbg_ref[...] += dbg
            dgo_ref[...] += dgo
            for h in range(H):
                dv_ref[rows, h * GLA_DV:(h + 1) * GLA_DV] = dvs[h].astype(BF16)
                dr_ref[rows, h * GLA_DV:(h + 1) * GLA_DV] = drs[h].astype(BF16)
                dstate[h] = dsts[h]

    rev = lambda i: nb - 1 - i

    def col(width, off):
        return pl.BlockSpec((rb, width), lambda i, o=off // width: (rev(i), o))

    full = lambda shp: pl.BlockSpec(shp, lambda i: (0,) * len(shp))
    rowb = lambda w: pl.BlockSpec((rb, w), lambda i: (rev(i), 0))
    in_specs = [col(GLA_W, cq), col(GLA_W, ckk), rowb(128), full((128, GLA_W)), full((1, GLA_W)), full((1, GLA_DV)),
                pl.BlockSpec((GLA_BLOCK_CHUNKS, H, GLA_DV, GLA_W), lambda i: (rev(i), 0, 0, 0)), rowb(H * GLA_DV)]
    in_specs += [col(GLA_DV, cv + h * GLA_DV) for h in range(H)] + [col(GLA_DV, cr + h * GLA_DV) for h in range(H)]
    return pl.pallas_call(
        body, name=name,
        out_shape=(jax.ShapeDtypeStruct((S, GLA_W), BF16), jax.ShapeDtypeStruct((S, GLA_W), BF16),
                   jax.ShapeDtypeStruct((S, H * GLA_DV), BF16), jax.ShapeDtypeStruct((S, H * GLA_DV), BF16),
                   jax.ShapeDtypeStruct((S, 128), F32), jax.ShapeDtypeStruct((128, GLA_W), F32),
                   jax.ShapeDtypeStruct((1, GLA_W), F32), jax.ShapeDtypeStruct((1, GLA_DV), F32)),
        grid=(nb,), in_specs=in_specs,
        out_specs=(rowb(GLA_W), rowb(GLA_W), rowb(H * GLA_DV), rowb(H * GLA_DV), rowb(128),
                   full((128, GLA_W)), full((1, GLA_W)), full((1, GLA_DV))),
        scratch_shapes=[pltpu.VMEM((H, GLA_DV, GLA_W), F32)],
        compiler_params=_params(("arbitrary",)),
    )(z, z, zsm, wg, bg, go, states, do, *([z] * (2 * H)))


def _row_spec(entry, tr):
    if isinstance(entry, tuple):
        arr, width, off = entry
        return arr, pl.BlockSpec((tr, width), lambda i, o=off // width: (i, o))
    return entry, pl.BlockSpec((tr, entry.shape[1]), lambda i: (i, 0))


def _stage_fwd(fn, rows, consts, outs, *, name, tr=None):
    first = rows[0][0] if isinstance(rows[0], tuple) else rows[0]
    S = first.shape[0]
    tr = tr or _pick(S, (512, 256, 128))
    arrs, specs = zip(*[_row_spec(e, tr) for e in rows])
    nr, nc = len(rows), len(consts)

    def body(*refs):
        vals = [r[...].astype(F32) for r in refs[:nr + nc]]
        res = fn(*vals)
        for o_ref, val in zip(refs[nr + nc:], res):
            o_ref[...] = val.astype(o_ref.dtype)

    cspecs = [pl.BlockSpec(c.shape, lambda i, n=c.ndim: (0,) * n) for c in consts]
    return pl.pallas_call(
        body, name=name,
        out_shape=tuple(jax.ShapeDtypeStruct((S, w), dt) for w, dt in outs), grid=(S // tr,),
        in_specs=list(specs) + cspecs,
        out_specs=tuple(pl.BlockSpec((tr, w), lambda i: (i, 0)) for w, _ in outs),
        compiler_params=_params(("parallel",)),
    )(*arrs, *consts)


def _stage_bwd(fn, rows, consts, cts, n_diff, drow_dtypes, *, name, tr=None):
    first = rows[0][0] if isinstance(rows[0], tuple) else rows[0]
    S = first.shape[0]
    tr = tr or _pick(S, (512, 256, 128))
    arrs, specs = zip(*[_row_spec(e, tr) for e in rows])
    widths = [e[1] if isinstance(e, tuple) else e.shape[1] for e in rows]
    nr, nc, nt = len(rows), len(consts), len(cts)

    def body(*refs):
        vals = [r[...].astype(F32) for r in refs[:nr + nc]]
        ct = [r[...].astype(F32) for r in refs[nr + nc:nr + nc + nt]]
        drow_refs = refs[nr + nc + nt:nr + nc + nt + n_diff]
        dconst_refs = refs[nr + nc + nt + n_diff:]
        rest_rows = vals[n_diff:nr]

        def f(diff_rows, cs):
            return tuple(fn(*diff_rows, *rest_rows, *cs))

        _, vjp = jax.vjp(f, vals[:n_diff], vals[nr:])
        drows, dcs = vjp(tuple(ct))
        for r, val in zip(drow_refs, drows):
            r[...] = val.astype(r.dtype)
        first_step = pl.program_id(0) == 0
        for r, val in zip(dconst_refs, dcs):
            @pl.when(first_step)
            def _(r=r, val=val):
                r[...] = val

            @pl.when(jnp.logical_not(first_step))
            def _(r=r, val=val):
                r[...] += val

    cspecs = [pl.BlockSpec(c.shape, lambda i, n=c.ndim: (0,) * n) for c in consts]
    ctspecs = [pl.BlockSpec((tr, c.shape[1]), lambda i: (i, 0)) for c in cts]
    out_shape = [jax.ShapeDtypeStruct((S, widths[j]), drow_dtypes[j]) for j in range(n_diff)]
    out_shape += [jax.ShapeDtypeStruct(c.shape, F32) for c in consts]
    out_specs = [pl.BlockSpec((tr, widths[j]), lambda i: (i, 0)) for j in range(n_diff)] + cspecs
    res = pl.pallas_call(
        body, name=name, out_shape=tuple(out_shape), grid=(S // tr,),
        in_specs=list(specs) + cspecs + ctspecs, out_specs=tuple(out_specs),
        compiler_params=_params(("arbitrary",)),
    )(*arrs, *consts, *cts)
    return list(res[:n_diff]), list(res[n_diff:])


def _mla_prep_fn(cq, ckv, kr1, kr2, cos16, sin16, cos64, sin64, gq, gkv, wq_n, wq_1, wq_2, wk, wv):
    hq = _rms(cq, gq)
    hkv = _rms(ckv, gkv)
    q1, q2 = bdot(hq, wq_1), bdot(hq, wq_2)
    return (bdot(hq, wq_n), q1 * cos64 - q2 * sin64, q2 * cos64 + q1 * sin64,
            bdot(hkv, wk), bdot(hkv, wv), kr1 * cos16 - kr2 * sin16, kr2 * cos16 + kr1 * sin16)


def _merge_fn(g0, g1, g2, of, og, om, b0, b1, b2, wf, wg, wm):
    return (_sigmoid(g0 + b0) * bdot(of, wf) + _sigmoid(g1 + b1) * bdot(og, wg) + _sigmoid(g2 + b2) * bdot(om, wm),)


_IN_SIZES = (256, 256, 256, 4, 256, 256, 512, 16, 512, 256, 128, 32, 3072)
_IN_OFF = np.concatenate([[0], np.cumsum(_IN_SIZES)])
(_O_FQ, _O_FK, _O_FV, _O_FF, _O_GQ, _O_GK, _O_GV, _O_GLOW, _O_GR, _O_MQ, _O_MKV, _O_MKR, _O_ZG) = [int(o) for o in _IN_OFF[:-1]]
N_IN = int(_IN_OFF[-1])
_BIG_GROUPS = ((_O_ZG, 3072), (_O_GV, 512), (_O_GR, 512), (_O_FQ, 256), (_O_FK, 256), (_O_FV, 256),
               (_O_GQ, 256), (_O_GK, 256), (_O_MQ, 256), (_O_MKV, 128))
Z_GATE, Z_GV, Z_GR, Z_FQ, Z_FK, Z_FV, Z_GQ, Z_GK, Z_MQ, Z_MKV = [int(o) for o in
                                                                    np.concatenate([[0], np.cumsum([w for _, w in _BIG_GROUPS])])[:-1]]
N_BIG = sum(w for _, w in _BIG_GROUPS)
SM_FF, SM_GLOW, SM_KR, N_SM = 0, 8, 32, 128
N_PAD = N_BIG + N_SM


def _in_perm():
    idx = np.concatenate([np.arange(o, o + w) for o, w in _BIG_GROUPS] + [np.zeros(N_SM, np.int64)])
    valid = np.concatenate([np.ones(N_BIG, bool), np.zeros(N_SM, bool)])
    for src, dst, w in ((_O_FF, SM_FF, 4), (_O_GLOW, SM_GLOW, 16), (_O_MKR, SM_KR, 32)):
        idx[N_BIG + dst:N_BIG + dst + w] = np.arange(src, src + w)
        valid[N_BIG + dst:N_BIG + dst + w] = True
    inv = np.zeros(N_IN, np.int64)
    inv[idx[valid]] = np.nonzero(valid)[0]
    return idx, valid, inv


_IN_IDX, _IN_VALID, _IN_INV = _in_perm()

_HALF = MLA_ROPE // 2
_QK_HD = MLA_NOPE + MLA_ROPE
_UQ_PERM = np.concatenate(
    [np.concatenate([np.arange(h * _QK_HD, h * _QK_HD + MLA_NOPE) for h in range(MLA_HEADS)]),
     np.concatenate([np.arange(h * _QK_HD + MLA_NOPE, h * _QK_HD + MLA_NOPE + _HALF) for h in range(MLA_HEADS)]),
     np.concatenate([np.arange(h * _QK_HD + MLA_NOPE + _HALF, (h + 1) * _QK_HD) for h in range(MLA_HEADS)])])
_UKV_PERM = np.concatenate(
    [np.concatenate([np.arange(h * 128, h * 128 + MLA_NOPE) for h in range(MLA_HEADS)]),
     np.concatenate([np.arange(h * 128 + MLA_NOPE, (h + 1) * 128) for h in range(MLA_HEADS)])])
_UQ_INV = np.argsort(_UQ_PERM)
_UKV_INV = np.argsort(_UKV_PERM)


def _heads(a, n):
    s, w = a.shape
    return a.reshape(s, n, w // n).transpose(1, 0, 2)


def _unheads(a):
    n, s, d = a.shape
    return a.transpose(1, 0, 2).reshape(s, n * d)


def _rope_tables(S):
    inv = ROPE_BASE ** (-jnp.arange(_HALF, dtype=F32) / _HALF)
    ang = jnp.arange(S, dtype=F32)[:, None] * inv[None, :]
    cos, sin = jnp.cos(ang), jnp.sin(ang)
    return cos, sin, jnp.tile(cos, (1, MLA_HEADS)), jnp.tile(sin, (1, MLA_HEADS))


def _prep_layer(w, l):
    p = {}
    p['w_big'] = w['w_in'][l][:, :N_BIG]
    p['w_sm'] = w['w_in'][l][:, N_BIG:]
    p['wg'] = jnp.zeros((N_SM, GLA_W), BF16).at[SM_GLOW:SM_GLOW + GLA_RANK].set(w['w_gla_gate'][l])
    uq = w['w_mla_uq'][l][:, _UQ_PERM]
    p['wq_n'], p['wq_1'], p['wq_2'] = uq[:, :256], uq[:, 256:320], uq[:, 320:]
    ukv = w['w_mla_ukv'][l][:, _UKV_PERM]
    p['wk'], p['wv'] = ukv[:, :256], ukv[:, 256:]
    for n in ('w_up_fox', 'w_up_gla', 'w_up_mla', 'w_out', 'w_xq', 'w_xkv', 'w_xo', 'w_mlp1', 'w_mlp2',
              'g_mix', 'g_xa', 'g_mem', 'g_mlp'):
        p[n] = w[n][l]
    p['b_f'] = jnp.zeros((8, 1), F32).at[:FOX_HEADS, 0].set(w['b_fox_forget'][l])
    p['bg'] = w['b_gla_gate'][l].reshape(1, GLA_W)
    p['go'] = w['g_gla_out'][l].reshape(1, GLA_DV)
    p['gq'] = w['g_mla_q'][l].reshape(1, MLA_Q_RANK)
    p['gkv'] = w['g_mla_kv'][l].reshape(1, MLA_KV_RANK)
    p['b_gate'] = [w['b_branch_gate'][l][i * 1024:(i + 1) * 1024].reshape(1, 1024) for i in range(3)]
    return p


_GLA_COLS = (Z_GQ, Z_GK, Z_GV, Z_GR)
_MLA_OUTS = [(256, BF16), (64, BF16), (64, BF16), (256, BF16), (256, BF16), (_HALF, BF16), (_HALF, BF16)]


def _mla_rows(z, zsm, rope):
    kr = zsm[:, SM_KR:SM_KR + MLA_ROPE]
    return [(z, 256, Z_MQ), (z, 128, Z_MKV), kr[:, :_HALF], kr[:, _HALF:], rope[0], rope[1], rope[2], rope[3]]


def _mla_consts(p):
    return [p['gq'], p['gkv'], p['wq_n'], p['wq_1'], p['wq_2'], p['wk'], p['wv']]


def _merge_rows(z, o_fox, o_gla, o_mla):
    return [(z, 1024, Z_GATE), (z, 1024, Z_GATE + 1024), (z, 1024, Z_GATE + 2048), o_fox, o_gla, o_mla]


def _merge_consts(p):
    return p['b_gate'] + [p['w_up_fox'], p['w_up_gla'], p['w_up_mla']]


def _layer_fwd(x0, mem, p, rope, l):
    S = x0.shape[0]
    sv = {'x0': x0}
    h1 = _rms_fwd(x0, p['g_mix'], name=f"rms_mix_{l}")
    z = _mm(h1, p['w_big'], mode='nn', out_dtype=BF16, name=f"in_big_{l}")
    zsm = _mm(h1, p['w_sm'], mode='nn', out_dtype=F32, name=f"in_small_{l}")
    sv.update(h1=h1, z=z, zsm=zsm)
    ff_t = jnp.zeros((8, S), F32).at[:FOX_HEADS].set(zsm[:, SM_FF:SM_FF + FOX_HEADS].T)
    cum_t = _fox_cum_fwd(ff_t, p['b_f'], name=f"fox_cum_{l}")
    cum = cum_t.T
    fq, fk, fv = (_heads(z[:, o:o + 256], FOX_HEADS) for o in (Z_FQ, Z_FK, Z_FV))
    o_fox_h, lse_f = _flash_fwd(fq, fk, fv, cum, cum_t, scale=FOX_HD ** -0.5, mask='causal', name=f"fox_fwd_{l}")
    o_fox = _unheads(o_fox_h)
    sv.update(ff_t=ff_t, cum=cum, cum_t=cum_t, fq=fq, fk=fk, fv=fv, o_fox_h=o_fox_h, lse_f=lse_f, o_fox=o_fox)
    o_gla, states = _gla_fwd(z, zsm, p['wg'], p['bg'], p['go'], _GLA_COLS, name=f"gla_fwd_{l}")
    sv.update(o_gla=o_gla, states=states)
    qn, q1, q2, kn, vv, k1, k2 = _stage_fwd(_mla_prep_fn, _mla_rows(z, zsm, rope), _mla_consts(p), _MLA_OUTS,
                                            name=f"mla_prep_{l}")
    mq = jnp.concatenate([qn.reshape(S, MLA_HEADS, MLA_NOPE), q1.reshape(S, MLA_HEADS, _HALF),
                          q2.reshape(S, MLA_HEADS, _HALF)], axis=-1).transpose(1, 0, 2)
    mk = jnp.concatenate([kn.reshape(S, MLA_HEADS, MLA_NOPE),
                          jnp.broadcast_to(k1[:, None, :], (S, MLA_HEADS, _HALF)),
                          jnp.broadcast_to(k2[:, None, :], (S, MLA_HEADS, _HALF))], axis=-1).transpose(1, 0, 2)
    mv = _heads(vv, MLA_HEADS)
    o_mla_h, lse_m = _flash_fwd(mq, mk, mv, None, None, scale=_QK_HD ** -0.5, mask='chunk', name=f"mla_fwd_{l}")
    o_mla = _unheads(o_mla_h)
    sv.update(mq=mq, mk=mk, mv=mv, o_mla_h=o_mla_h, lse_m=lse_m, o_mla=o_mla)
    (y,) = _stage_fwd(_merge_fn, _merge_rows(z, o_fox, o_gla, o_mla), _merge_consts(p), [(1024, BF16)], name=f"merge_{l}")
    x1 = _mm(y, p['w_out'], mode='nn', out_dtype=F32, residual=x0, name=f"out_proj_{l}")
    sv.update(y=y, x1=x1)
    h2 = _rms_fwd(x1, p['g_xa'], name=f"rms_xa_{l}")
    hm = _rms_fwd(mem, p['g_mem'], name=f"rms_mem_{l}")
    qx = _heads(_mm(h2, p['w_xq'], mode='nn', out_dtype=BF16, name=f"xq_{l}"), XA_HEADS)
    kvx = _mm(hm, p['w_xkv'], mode='nn', out_dtype=BF16, name=f"xkv_{l}")
    kx, vx = _heads(kvx[:, :512], XA_HEADS), _heads(kvx[:, 512:], XA_HEADS)
    ox_h, lse_x = _flash_fwd(qx, kx, vx, None, None, scale=XA_HD ** -0.5, mask=None, name=f"xa_fwd_{l}")
    ox = _unheads(ox_h)
    x2 = _mm(ox, p['w_xo'], mode='nn', out_dtype=F32, residual=x1, name=f"xo_{l}")
    sv.update(h2=h2, hm=hm, qx=qx, kx=kx, vx=vx, ox_h=ox_h, lse_x=lse_x, ox=ox, x2=x2)
    h3 = _rms_fwd(x2, p['g_mlp'], name=f"rms_mlp_{l}")
    a = _mm(h3, p['w_mlp1'], mode='nn', out_dtype=BF16, name=f"mlp1_{l}")
    x3 = _mm(a, p['w_mlp2'], mode='nn', out_dtype=F32, act='relu2', residual=x2, name=f"mlp2_{l}")
    sv.update(h3=h3, a=a)
    return x3, sv


def _layer_bwd(dx3, mem, p, rope, sv, l):
    S = dx3.shape[0]
    g = {}
    dx3b = dx3.astype(BF16)
    da = _mm(dx3b, p['w_mlp2'], mode='nt', out_dtype=BF16, drelu_of=sv['a'], name=f"d_mlp2_in_{l}")
    g['w_mlp2'] = _mm(sv['a'], dx3b, mode='tn', out_dtype=F32, act='relu2', name=f"d_w_mlp2_{l}")
    dh3 = _mm(da, p['w_mlp1'], mode='nt', out_dtype=F32, name=f"d_mlp1_in_{l}")
    g['w_mlp1'] = _mm(sv['h3'], da, mode='tn', out_dtype=F32, name=f"d_w_mlp1_{l}")
    dx2, g['g_mlp'] = _rms_bwd(sv['x2'], p['g_mlp'], dh3, dx3, name=f"d_rms_mlp_{l}")
    dx2b = dx2.astype(BF16)
    dox = _mm(dx2b, p['w_xo'], mode='nt', out_dtype=BF16, name=f"d_xo_in_{l}")
    g['w_xo'] = _mm(sv['ox'], dx2b, mode='tn', out_dtype=F32, name=f"d_w_xo_{l}")
    dqx, dkx, dvx = _flash_bwd(sv['qx'], sv['kx'], sv['vx'], sv['ox_h'], _heads(dox, XA_HEADS), sv['lse_x'], None, None,
                               scale=XA_HD ** -0.5, mask=None, name=f"xa_bwd_{l}")
    dqx = _unheads(dqx).astype(BF16)
    dkvx = jnp.concatenate([_unheads(dkx), _unheads(dvx)], axis=1).astype(BF16)
    dh2 = _mm(dqx, p['w_xq'], mode='nt', out_dtype=F32, name=f"d_xq_in_{l}")
    g['w_xq'] = _mm(sv['h2'], dqx, mode='tn', out_dtype=F32, name=f"d_w_xq_{l}")
    dhm = _mm(dkvx, p['w_xkv'], mode='nt', out_dtype=F32, name=f"d_xkv_in_{l}")
    g['w_xkv'] = _mm(sv['hm'], dkvx, mode='tn', out_dtype=F32, name=f"d_w_xkv_{l}")
    _, g['g_mem'] = _rms_bwd(mem, p['g_mem'], dhm, None, name=f"d_rms_mem_{l}")
    dx1, g['g_xa'] = _rms_bwd(sv['x1'], p['g_xa'], dh2, dx2, name=f"d_rms_xa_{l}")
    dx1b = dx1.astype(BF16)
    dy = _mm(dx1b, p['w_out'], mode='nt', out_dtype=F32, name=f"d_out_in_{l}")
    g['w_out'] = _mm(sv['y'], dx1b, mode='tn', out_dtype=F32, name=f"d_w_out_{l}")
    z, zsm = sv['z'], sv['zsm']
    (dg0, dg1, dg2, do_fox, do_gla, do_mla), (db0, db1, db2, g['w_up_fox'], g['w_up_gla'], g['w_up_mla']) = _stage_bwd(
        _merge_fn, _merge_rows(z, sv['o_fox'], sv['o_gla'], sv['o_mla']), _merge_consts(p), [dy], 6, [BF16] * 6,
        name=f"merge_bwd_{l}")
    g['b_branch_gate'] = jnp.concatenate([db0, db1, db2], axis=1).reshape(-1)
    dfq, dfk, dfv, dck, dcq = _flash_bwd(sv['fq'], sv['fk'], sv['fv'], sv['o_fox_h'], _heads(do_fox, FOX_HEADS), sv['lse_f'],
                                         sv['cum'], sv['cum_t'], scale=FOX_HD ** -0.5, mask='causal', name=f"fox_bwd_{l}")
    dff_t, db_f = _fox_cum_bwd(sv['ff_t'], p['b_f'], dck + dcq.T, name=f"fox_cum_bwd_{l}")
    g['b_fox_forget'] = db_f[:FOX_HEADS, 0]
    dgq, dgk, dgv, dgr, dzsm, dwg, dbg, dgo = _gla_bwd(z, zsm, p['wg'], p['bg'], p['go'], sv['states'], do_gla, _GLA_COLS,
                                                       name=f"gla_bwd_{l}")
    g['w_gla_gate'] = dwg[SM_GLOW:SM_GLOW + GLA_RANK]
    g['b_gla_gate'] = dbg.reshape(-1)
    g['g_gla_out'] = dgo.reshape(-1)
    dmq, dmk, dmv = _flash_bwd(sv['mq'], sv['mk'], sv['mv'], sv['o_mla_h'], _heads(do_mla, MLA_HEADS), sv['lse_m'], None, None,
                               scale=_QK_HD ** -0.5, mask='chunk', name=f"mla_bwd_{l}")
    dmq_r = dmq.transpose(1, 0, 2)
    cts = [dmq_r[:, :, :MLA_NOPE].reshape(S, 256), dmq_r[:, :, MLA_NOPE:MLA_NOPE + _HALF].reshape(S, 64),
           dmq_r[:, :, MLA_NOPE + _HALF:].reshape(S, 64), _unheads(dmk[:, :, :MLA_NOPE]), _unheads(dmv),
           jnp.sum(dmk[:, :, MLA_NOPE:MLA_NOPE + _HALF], axis=0), jnp.sum(dmk[:, :, MLA_NOPE + _HALF:], axis=0)]
    (dcq, dckv, dkr1, dkr2), (dgq_n, dgkv_n, dwq_n, dwq_1, dwq_2, dwk, dwv) = _stage_bwd(
        _mla_prep_fn, _mla_rows(z, zsm, rope), _mla_consts(p), cts, 4, [BF16, BF16, F32, F32], name=f"mla_prep_bwd_{l}")
    g['g_mla_q'] = dgq_n.reshape(-1)
    g['g_mla_kv'] = dgkv_n.reshape(-1)
    g['w_mla_uq'] = jnp.concatenate([dwq_n, dwq_1, dwq_2], axis=1)[:, _UQ_INV]
    g['w_mla_ukv'] = jnp.concatenate([dwk, dwv], axis=1)[:, _UKV_INV]
    dz = jnp.concatenate([dg0, dg1, dg2, dgv, dgr, _unheads(dfq).astype(BF16), _unheads(dfk).astype(BF16),
                          _unheads(dfv).astype(BF16), dgq, dgk, dcq, dckv], axis=1)
    dzsm = dzsm + jnp.concatenate([dff_t[:FOX_HEADS].T, jnp.zeros((S, SM_KR - FOX_HEADS), F32), dkr1, dkr2,
                                   jnp.zeros((S, N_SM - SM_KR - MLA_ROPE), F32)], axis=1)
    dzsm = dzsm.astype(BF16)
    dh1 = _mm(dz, p['w_big'], mode='nt', out_dtype=F32, name=f"d_in_big_{l}")
    dh1 = _mm(dzsm, p['w_sm'], mode='nt', out_dtype=F32, residual=dh1, name=f"d_in_small_{l}")
    g['w_in'] = jnp.concatenate([_mm(sv['h1'], dz, mode='tn', out_dtype=F32, name=f"d_w_big_{l}"),
                                 _mm(sv['h1'], dzsm, mode='tn', out_dtype=F32, name=f"d_w_small_{l}")], axis=1)
    dx0, g['g_mix'] = _rms_bwd(sv['x0'], p['g_mix'], dh1, dx1, name=f"d_rms_mix_{l}")
    for n in ('g_mlp', 'g_mem', 'g_xa', 'g_mix'):
        g[n] = g[n].reshape(-1)
    return dx0, g


def _local_step(x, mem, target, w):
    S = x.shape[0]
    depth = w['g_mix'].shape[0]
    rope = _rope_tables(S)
    ps = [_prep_layer(w, l) for l in range(depth)]
    saved = []
    for l in range(depth):
        x, sv = _layer_fwd(x, mem, ps[l], rope, l)
        saved.append(sv)
    loss, dx, dgf = _loss_head(x, w['g_final'], target, name="loss_head")
    grads = [None] * depth
    for l in reversed(range(depth)):
        dx, grads[l] = _layer_bwd(dx, mem, ps[l], rope, saved[l], l)
    return loss, dx, grads, dgf.reshape(-1)


_MESH_AXES = ("x", "y", "c")
_HBM = pl.BlockSpec(memory_space=pl.ANY)


def _me_and_peers():
    x, y, c = (lax.axis_index(n) for n in _MESH_AXES)
    peers = []
    for d in range(1, N_DEV):
        px = 1 - x if d & 4 else x
        py = 1 - y if d & 2 else y
        pc = 1 - c if d & 1 else c
        peers.append(((px, py, pc), 4 * px + 2 * py + pc))
    return 4 * x + 2 * y + c, peers


def _exchange(src_of, dst_of, keep, send_sems, recv_sems, local_sem):
    me, peers = _me_and_peers()
    local = pltpu.make_async_copy(src_of(me), keep(me), local_sem)
    local.start()
    sends = []
    for k, (coords, lin) in enumerate(peers):
        cp = pltpu.make_async_remote_copy(src_ref=src_of(lin), dst_ref=dst_of(me), send_sem=send_sems.at[k],
                                          recv_sem=recv_sems.at[k], device_id=coords, device_id_type=pl.DeviceIdType.MESH)
        cp.start()
        sends.append(cp)
    for k, (coords, lin) in enumerate(peers):
        sends[k].wait_send()
        pltpu.make_async_remote_copy(src_ref=src_of(lin), dst_ref=dst_of(lin), send_sem=send_sems.at[k],
                                     recv_sem=recv_sems.at[k], device_id=coords,
                                     device_id_type=pl.DeviceIdType.MESH).wait_recv()
    local.wait()


_COMM_SCRATCH = [pltpu.SemaphoreType.DMA((N_DEV - 1,)), pltpu.SemaphoreType.DMA((N_DEV - 1,)), pltpu.SemaphoreType.DMA]


def _all_gather(x, *, name):
    def body(x_ref, o_ref, send_sems, recv_sems, local_sem):
        _exchange(lambda j: x_ref, lambda j: o_ref.at[j], lambda j: o_ref.at[j], send_sems, recv_sems, local_sem)

    return pl.pallas_call(
        body, name=name, out_shape=jax.ShapeDtypeStruct((N_DEV,) + x.shape, x.dtype),
        in_specs=[_HBM], out_specs=_HBM, scratch_shapes=_COMM_SCRATCH,
        compiler_params=pltpu.CompilerParams(has_side_effects=True),
    )(x)


def _all_to_all(x, *, name):
    def body(x_ref, o_ref, send_sems, recv_sems, local_sem):
        _exchange(lambda j: x_ref.at[j], lambda j: o_ref.at[j], lambda j: o_ref.at[j], send_sems, recv_sems, local_sem)

    return pl.pallas_call(
        body, name=name, out_shape=jax.ShapeDtypeStruct(x.shape, x.dtype),
        in_specs=[_HBM], out_specs=_HBM, scratch_shapes=_COMM_SCRATCH,
        compiler_params=pltpu.CompilerParams(has_side_effects=True),
    )(x)


def _sum_slots(x, *, name):
    _, R, _ = x.shape
    tr = _pick(R, (1024, 512, 256, 128, 64, 32, 16, 8))

    def body(x_ref, o_ref):
        acc = x_ref[0].astype(F32)
        for j in range(1, N_DEV):
            acc = acc + x_ref[j].astype(F32)
        o_ref[...] = acc

    return pl.pallas_call(
        body, name=name, out_shape=jax.ShapeDtypeStruct((R, 128), F32), grid=(R // tr,),
        in_specs=[pl.BlockSpec((N_DEV, tr, 128), lambda i: (0, i, 0))], out_specs=pl.BlockSpec((tr, 128), lambda i: (i, 0)),
        compiler_params=_params(("parallel",)),
    )(x)


def _adamw(w, g, m, v, *, name):
    shape = w.shape
    cols = shape[-1]
    rows = int(np.prod(shape[:-1]))
    tr = next((t for t in (1024, 512, 256, 128, 64, 32, 16, 8) if rows % t == 0 and t * cols * 4 <= (1 << 20)), rows)

    def body(w_ref, g_ref, m_ref, v_ref, d_ref, mo_ref, vo_ref):
        g_ = g_ref[...]
        m_ = ADAM_B1 * m_ref[...] + (1.0 - ADAM_B1) * g_
        v_ = ADAM_B2 * v_ref[...] + (1.0 - ADAM_B2) * jnp.square(g_)
        m_hat = m_ / (1.0 - ADAM_B1 ** ADAM_STEP)
        v_hat = v_ / (1.0 - ADAM_B2 ** ADAM_STEP)
        d_ref[...] = -ADAM_LR * (m_hat / (jnp.sqrt(v_hat) + ADAM_EPS) + ADAM_WD * w_ref[...])
        mo_ref[...] = m_
        vo_ref[...] = v_

    blk = pl.BlockSpec((tr, cols), lambda i: (i, 0))
    outs = pl.pallas_call(
        body, name=name, out_shape=tuple(jax.ShapeDtypeStruct((rows, cols), F32) for _ in range(3)), grid=(rows // tr,),
        in_specs=[blk] * 4, out_specs=(blk,) * 3, compiler_params=_params(("parallel",)),
    )(*(a.reshape(rows, cols) for a in (w, g, m, v)))
    return tuple(o.reshape(shape) for o in outs)


_WEIGHTS = ('g_mix', 'w_in', 'b_fox_forget', 'w_gla_gate', 'b_gla_gate', 'g_gla_out', 'g_mla_q', 'w_mla_uq', 'g_mla_kv',
            'w_mla_ukv', 'b_branch_gate', 'w_up_fox', 'w_up_gla', 'w_up_mla', 'w_out', 'g_xa', 'g_mem', 'w_xq', 'w_xkv',
            'w_xo', 'g_mlp', 'w_mlp1', 'w_mlp2', 'g_final')
_SHARDED = (('w_in', 1), ('w_gla_gate', 2), ('w_mla_uq', 2), ('w_mla_ukv', 2), ('w_up_fox', 2), ('w_up_gla', 2),
            ('w_up_mla', 2), ('w_out', 1), ('w_xq', 1), ('w_xkv', 1), ('w_xo', 2), ('w_mlp1', 2), ('w_mlp2', 1))
_REPLICATED = tuple(n for n in _WEIGHTS if n not in dict(_SHARDED))
_ROW_PAD = 1024
_SMALL_ROW_PAD = 8


def _pack(flats, lead, row_pad=_ROW_PAD):
    cat = jnp.concatenate([a.reshape(a.shape[:lead] + (-1,)) for a in flats], axis=-1)
    n = cat.shape[-1]
    total = -(-n // (128 * row_pad)) * (128 * row_pad)
    cat = jnp.pad(cat, [(0, 0)] * lead + [(0, total - n)])
    return cat.reshape(cat.shape[:lead] + (total // 128, 128))


def _unpack(buf, shapes, lead):
    flat = buf.reshape(buf.shape[:lead] + (-1,))
    out, off = [], 0
    for shp in shapes:
        n = int(np.prod(shp))
        out.append(flat[..., off:off + n].reshape(buf.shape[:lead] + tuple(shp)))
        off += n
    return out


def _to_whole(g, axis):
    if axis == 1:
        return g.transpose(1, 0, 2, 3).reshape(g.shape[1], N_DEV * g.shape[2], g.shape[3])
    return g.transpose(1, 2, 0, 3).reshape(g.shape[1], g.shape[2], N_DEV * g.shape[3])


def _to_shards(w, axis):
    L, R, C = w.shape
    if axis == 1:
        return w.reshape(L, N_DEV, R // N_DEV, C).transpose(1, 0, 2, 3)
    return w.reshape(L, R, N_DEV, C // N_DEV).transpose(2, 0, 1, 3)


def kernel(x, mem, g_mix, w_in, b_fox_forget, w_gla_gate, b_gla_gate, g_gla_out, g_mla_q, w_mla_uq, g_mla_kv, w_mla_ukv, b_branch_gate, w_up_fox, w_up_gla, w_up_mla, w_out, g_xa, g_mem, w_xq, w_xkv, w_xo, g_mlp, w_mlp1, w_mlp2, g_final, loss_target, m_g_mix, m_w_in, m_b_fox_forget, m_w_gla_gate, m_b_gla_gate, m_g_gla_out, m_g_mla_q, m_w_mla_uq, m_g_mla_kv, m_w_mla_ukv, m_b_branch_gate, m_w_up_fox, m_w_up_gla, m_w_up_mla, m_w_out, m_g_xa, m_g_mem, m_w_xq, m_w_xkv, m_w_xo, m_g_mlp, m_w_mlp1, m_w_mlp2, m_g_final, v_g_mix, v_w_in, v_b_fox_forget, v_w_gla_gate, v_b_gla_gate, v_g_gla_out, v_g_mla_q, v_w_mla_uq, v_g_mla_kv, v_w_mla_ukv, v_b_branch_gate, v_w_up_fox, v_w_up_gla, v_w_up_mla, v_w_out, v_g_xa, v_g_mem, v_w_xq, v_w_xkv, v_w_xo, v_g_mlp, v_w_mlp1, v_w_mlp2, v_g_final):
    wts = dict(zip(_WEIGHTS, (g_mix, w_in, b_fox_forget, w_gla_gate, b_gla_gate, g_gla_out, g_mla_q, w_mla_uq, g_mla_kv,
                              w_mla_ukv, b_branch_gate, w_up_fox, w_up_gla, w_up_mla, w_out, g_xa, g_mem, w_xq, w_xkv, w_xo,
                              g_mlp, w_mlp1, w_mlp2, g_final)))
    mom1 = dict(zip(_WEIGHTS, (m_g_mix, m_w_in, m_b_fox_forget, m_w_gla_gate, m_b_gla_gate, m_g_gla_out, m_g_mla_q,
                               m_w_mla_uq, m_g_mla_kv, m_w_mla_ukv, m_b_branch_gate, m_w_up_fox, m_w_up_gla, m_w_up_mla,
                               m_w_out, m_g_xa, m_g_mem, m_w_xq, m_w_xkv, m_w_xo, m_g_mlp, m_w_mlp1, m_w_mlp2, m_g_final)))
    mom2 = dict(zip(_WEIGHTS, (v_g_mix, v_w_in, v_b_fox_forget, v_w_gla_gate, v_b_gla_gate, v_g_gla_out, v_g_mla_q,
                               v_w_mla_uq, v_g_mla_kv, v_w_mla_ukv, v_b_branch_gate, v_w_up_fox, v_w_up_gla, v_w_up_mla,
                               v_w_out, v_g_xa, v_g_mem, v_w_xq, v_w_xkv, v_w_xo, v_g_mlp, v_w_mlp1, v_w_mlp2, v_g_final)))
    depth = g_mix.shape[0]

    shard = {n: wts[n] for n, _ in _SHARDED}
    shard['w_in'] = jnp.where(_IN_VALID[None, None, :], w_in[:, :, _IN_IDX], 0.0)
    shard_shapes = [shard[n].shape for n, _ in _SHARDED]
    gathered = _all_gather(_pack([shard[n].astype(BF16) for n, _ in _SHARDED], 0), name="gather_weights")
    whole = {n: _to_whole(g, ax) for (n, ax), g in zip(_SHARDED, _unpack(gathered, shard_shapes, 1))}
    whole.update({n: wts[n] for n in _REPLICATED})

    loss, dx, grads, dg_final = _local_step(x[0], mem[0], loss_target[0], whole)
    loss = lax.psum(loss[0, 0], _MESH_AXES)

    slots = _pack([_to_shards(jnp.stack([grads[l][n] for l in range(depth)]), ax).astype(BF16) for n, ax in _SHARDED], 1)
    summed = _sum_slots(_all_to_all(slots, name="scatter_grads"), name="sum_grads")
    grad = dict(zip([n for n, _ in _SHARDED], _unpack(summed, shard_shapes, 0)))
    grad['w_in'] = grad['w_in'][:, :, _IN_INV]
    small = [dg_final if n == 'g_final' else jnp.stack([grads[l][n] for l in range(depth)]) for n in _REPLICATED]
    small_shapes = [wts[n].shape for n in _REPLICATED]
    small_sum = _sum_slots(_all_gather(_pack(small, 0, _SMALL_ROW_PAD), name="gather_small_grads"), name="sum_small_grads")
    grad.update(dict(zip(_REPLICATED, _unpack(small_sum, small_shapes, 0))))

    delta, new_m, new_v = {}, {}, {}
    for n, _ in _SHARDED:
        delta[n], new_m[n], new_v[n] = _adamw(wts[n], grad[n], mom1[n], mom2[n], name=f"adamw_{n}")
    packed = [_pack([d[n] for n in _REPLICATED], 0, _SMALL_ROW_PAD) for d in (wts, mom1, mom2)]
    outs = _adamw(packed[0], small_sum, packed[1], packed[2], name="adamw_small")
    for d, o in zip((delta, new_m, new_v), outs):
        d.update(dict(zip(_REPLICATED, _unpack(o, small_shapes, 0))))

    return (loss, dx[None], *[grad[n] for n in _WEIGHTS], *[delta[n] for n in _WEIGHTS],
            *[new_m[n] for n in _WEIGHTS], *[new_v[n] for n in _WEIGHTS])
```

```python
import jax
import jax.numpy as jnp
import numpy as np
from jax import lax
from jax.experimental import pallas as pl
from jax.experimental.pallas import tpu as pltpu

F32 = jnp.float32
BF16 = jnp.bfloat16

EPS = 1e-6
CHUNK = 64
FOX_HEADS, FOX_HD = 4, 64
GLA_HEADS, GLA_DK, GLA_DV, GLA_RANK, GLA_TAU = 4, 64, 128, 16, 16.0
MLA_HEADS, MLA_Q_RANK, MLA_KV_RANK, MLA_NOPE, MLA_ROPE, MLA_VD = 4, 256, 128, 64, 32, 64
ROPE_BASE = 10000.0
XA_HEADS, XA_HD = 4, 128
ADAM_LR, ADAM_B1, ADAM_B2, ADAM_EPS, ADAM_WD, ADAM_STEP = 0.001, 0.9, 0.999, 1e-08, 0.01, 10

N_DEV = 8
V7X_VMEM_LIMIT = 56 * 1024 * 1024
NEG = -1e30

NN = ((1,), (0,))
NT = ((1,), (1,))
TN = ((0,), (0,))


def _dot(a, b, dims):
    return lax.dot_general(a.astype(BF16), b.astype(BF16), (dims, ((), ())), preferred_element_type=F32)


@jax.custom_vjp
def bdot(a, b):
    return _dot(a, b, NN)


bdot.defvjp(lambda a, b: (_dot(a, b, NN), (a, b)),
            lambda res, g: (_dot(g, res[1], NT), _dot(res[0], g, TN)))


@jax.custom_vjp
def bdot_nt(a, b):
    return _dot(a, b, NT)


bdot_nt.defvjp(lambda a, b: (_dot(a, b, NT), (a, b)),
               lambda res, g: (_dot(g, res[1], NN), _dot(g, res[0], TN)))


@jax.custom_vjp
def bdot_tn(a, b):
    return _dot(a, b, TN)


bdot_tn.defvjp(lambda a, b: (_dot(a, b, TN), (a, b)),
               lambda res, g: (_dot(res[1], g, NT), _dot(res[0], g, NN)))


def _split2(x):
    hi = x.astype(BF16)
    lo = (x - hi.astype(F32)).astype(BF16)
    return hi, lo


def _tri(n, lower):
    r = lax.broadcasted_iota(jnp.int32, (n, n), 0)
    c = lax.broadcasted_iota(jnp.int32, (n, n), 1)
    return jnp.where((r >= c) if lower else (r <= c), 1.0, 0.0).astype(BF16)


def _tri_dot2(x, lower):
    hi, lo = _split2(x)
    t = _tri(x.shape[0], lower)
    return _dot(t, hi, NN) + _dot(t, lo, NN)


@jax.custom_vjp
def chunk_cumsum(x):
    return _tri_dot2(x, True)


chunk_cumsum.defvjp(lambda x: (_tri_dot2(x, True), None), lambda _, g: (_tri_dot2(g, False),))


def _log_sigmoid(x):
    return jnp.minimum(x, 0.0) - jnp.log(1.0 + jnp.exp(-jnp.abs(x)))


def _sigmoid(x):
    return 1.0 / (1.0 + jnp.exp(-x))


def _rms(x, g):
    return x * lax.rsqrt(jnp.mean(x * x, axis=-1, keepdims=True) + EPS) * g


def _pick(dim, prefs):
    for p in prefs:
        if dim % p == 0:
            return p
    return dim


def _params(sem):
    return pltpu.CompilerParams(dimension_semantics=sem, vmem_limit_bytes=V7X_VMEM_LIMIT)


def _mm(a, b, *, mode, out_dtype, name, act=None, residual=None, drelu_of=None, tm=None, tn=None, tk=None):
    if mode == 'nn':
        (M, K), N = a.shape, b.shape[1]
    elif mode == 'nt':
        (M, K), N = a.shape, b.shape[0]
    else:
        (K, M), N = a.shape, b.shape[1]
    tm = tm or _pick(M, (1024, 512, 256, 128))
    tn = tn or _pick(N, (1024, 1920, 1152, 768, 640, 512, 384, 256, 128))
    tk = tk or _pick(K, (1024, 1920, 1152, 640, 512, 256, 128))
    nk = K // tk
    dims = {'nn': NN, 'nt': NT, 'tn': TN}[mode]
    a_spec = pl.BlockSpec((tk, tm), lambda i, j, k: (k, i)) if mode == 'tn' else pl.BlockSpec((tm, tk), lambda i, j, k: (i, k))
    b_spec = pl.BlockSpec((tn, tk), lambda i, j, k: (j, k)) if mode == 'nt' else pl.BlockSpec((tk, tn), lambda i, j, k: (k, j))
    o_spec = pl.BlockSpec((tm, tn), lambda i, j, k: (i, j))
    extra = [e for e in (residual, drelu_of) if e is not None]

    def body(a_ref, b_ref, *rest):
        o_ref = rest[len(extra)]
        at = a_ref[...]
        if act == 'relu2':
            at = jnp.square(jnp.maximum(at.astype(F32), 0.0))
        part = _dot(at, b_ref[...], dims)

        def finish(acc):
            idx = 0
            if residual is not None:
                acc = acc + rest[idx][...]
                idx += 1
            if drelu_of is not None:
                acc = acc * (2.0 * jnp.maximum(rest[idx][...].astype(F32), 0.0))
            o_ref[...] = acc.astype(out_dtype)

        if nk == 1:
            finish(part)
        else:
            acc_ref = rest[len(extra) + 1]
            k = pl.program_id(2)

            @pl.when(k == 0)
            def _():
                acc_ref[...] = part

            @pl.when(k > 0)
            def _():
                acc_ref[...] += part

            @pl.when(k == nk - 1)
            def _():
                finish(acc_ref[...])

    return pl.pallas_call(
        body, name=name,
        out_shape=jax.ShapeDtypeStruct((M, N), out_dtype),
        grid=(M // tm, N // tn, nk),
        in_specs=[a_spec, b_spec] + [o_spec] * len(extra),
        out_specs=o_spec,
        scratch_shapes=[] if nk == 1 else [pltpu.VMEM((tm, tn), F32)],
        compiler_params=_params(("parallel", "parallel", "arbitrary")),
    )(a, b, *extra)


def _rms_fwd(x, g, *, name, out_dtype=BF16):
    S, D = x.shape
    tr = _pick(S, (512, 256, 128))

    def body(x_ref, g_ref, o_ref):
        o_ref[...] = _rms(x_ref[...], g_ref[...]).astype(out_dtype)

    return pl.pallas_call(
        body, name=name, out_shape=jax.ShapeDtypeStruct((S, D), out_dtype), grid=(S // tr,),
        in_specs=[pl.BlockSpec((tr, D), lambda i: (i, 0)), pl.BlockSpec((1, D), lambda i: (0, 0))],
        out_specs=pl.BlockSpec((tr, D), lambda i: (i, 0)),
        compiler_params=_params(("parallel",)),
    )(x, g.reshape(1, D))


def _rms_bwd(x, g, dy, dres, *, name):
    S, D = x.shape
    tr = _pick(S, (512, 256, 128))

    def body(x_ref, g_ref, dy_ref, *rest):
        dx_ref, dg_ref = rest[-2], rest[-1]
        x_ = x_ref[...]
        rstd = lax.rsqrt(jnp.mean(x_ * x_, axis=-1, keepdims=True) + EPS)
        xh = x_ * rstd
        dy_ = dy_ref[...].astype(F32)
        gdy = dy_ * g_ref[...]
        dx = (gdy - xh * jnp.mean(gdy * xh, axis=-1, keepdims=True)) * rstd
        if dres is not None:
            dx = dx + rest[0][...]
        dx_ref[...] = dx
        part = jnp.sum(dy_ * xh, axis=0, keepdims=True)

        @pl.when(pl.program_id(0) == 0)
        def _():
            dg_ref[...] = part

        @pl.when(pl.program_id(0) > 0)
        def _():
            dg_ref[...] += part

    row = pl.BlockSpec((tr, D), lambda i: (i, 0))
    vec = pl.BlockSpec((1, D), lambda i: (0, 0))
    ins = [x, g.reshape(1, D), dy] + ([dres] if dres is not None else [])
    return pl.pallas_call(
        body, name=name,
        out_shape=(jax.ShapeDtypeStruct((S, D), F32), jax.ShapeDtypeStruct((1, D), F32)),
        grid=(S // tr,),
        in_specs=[row, vec, row] + ([row] if dres is not None else []),
        out_specs=(row, vec),
        compiler_params=_params(("arbitrary",)),
    )(*ins)


def _loss_head(x, g, target, *, name):
    S, D = x.shape
    tr = _pick(S, (512, 256, 128))

    def body(x_ref, g_ref, t_ref, l_ref, dx_ref, dg_ref):
        x_ = x_ref[...]
        g_ = g_ref[...]
        rstd = lax.rsqrt(jnp.mean(x_ * x_, axis=-1, keepdims=True) + EPS)
        xh = x_ * rstd
        err = xh * g_ - t_ref[...]
        lpart = (0.5 / D) * jnp.sum(jnp.sum(err * err, axis=-1, keepdims=True), axis=0, keepdims=True)
        dy = err * (1.0 / D)
        gdy = dy * g_
        dx_ref[...] = (gdy - xh * jnp.mean(gdy * xh, axis=-1, keepdims=True)) * rstd
        gpart = jnp.sum(dy * xh, axis=0, keepdims=True)

        @pl.when(pl.program_id(0) == 0)
        def _():
            dg_ref[...] = gpart
            l_ref[...] = lpart

        @pl.when(pl.program_id(0) > 0)
        def _():
            dg_ref[...] += gpart
            l_ref[...] += lpart

    row = pl.BlockSpec((tr, D), lambda i: (i, 0))
    vec = pl.BlockSpec((1, D), lambda i: (0, 0))
    return pl.pallas_call(
        body, name=name,
        out_shape=(jax.ShapeDtypeStruct((1, 1), F32), jax.ShapeDtypeStruct((S, D), F32), jax.ShapeDtypeStruct((1, D), F32)),
        grid=(S // tr,),
        in_specs=[row, vec, row],
        out_specs=(pl.BlockSpec((1, 1), lambda i: (0, 0)), row, vec),
        compiler_params=_params(("arbitrary",)),
    )(x, g.reshape(1, D), target)


def _mask_of(mask, tq, tk):
    qpos = lax.broadcasted_iota(jnp.int32, (tq, tk), 0)
    kpos = lax.broadcasted_iota(jnp.int32, (tq, tk), 1)
    if mask == 'causal':
        return kpos <= qpos
    return kpos <= (qpos | (CHUNK - 1))


def _flash_fwd(q, k, v, cq, ck, *, scale, mask, name):
    H, Sq, dk = q.shape
    Sk, dv = k.shape[1], v.shape[2]
    tq = _pick(Sq, (512, 256, 128))
    tk = tq if mask else _pick(Sk, (512, 256, 128))
    nq, nk = Sq // tq, Sk // tk
    bias = cq is not None

    def body(*refs):
        q_ref, k_ref, v_ref = refs[:3]
        cq_ref, ck_ref = (refs[3], refs[4]) if bias else (None, None)
        o_ref, lse_ref, m_s, l_s, acc_s = refs[-5:]
        qi, ki = pl.program_id(0), pl.program_id(1)

        @pl.when(ki == 0)
        def _():
            m_s[...] = jnp.full(m_s.shape, NEG, F32)
            l_s[...] = jnp.zeros(l_s.shape, F32)
            acc_s[...] = jnp.zeros(acc_s.shape, F32)

        def compute(masked):
            keep = _mask_of(mask, tq, tk) if masked else None
            for h in range(H):
                s = _dot(q_ref[h], k_ref[h], NT) * scale
                if bias:
                    s = s + (cq_ref[:, h:h + 1] - ck_ref[h:h + 1, :])
                if masked:
                    s = jnp.where(keep, s, NEG)
                m_prev = m_s[h]
                m_new = jnp.maximum(m_prev, jnp.max(s, axis=1, keepdims=True))
                alpha = jnp.exp(m_prev - m_new)
                p = jnp.exp(s - m_new)
                l_s[h] = alpha * l_s[h] + jnp.sum(p, axis=1, keepdims=True)
                acc_s[h] = alpha * acc_s[h] + _dot(p, v_ref[h], NN)
                m_s[h] = m_new

        if mask is None:
            compute(False)
        else:
            pl.when(ki < qi)(lambda: compute(False))
            pl.when(ki == qi)(lambda: compute(True))

        @pl.when(ki == ((nk - 1) if mask is None else qi))
        def _():
            lse_ref[...] = jnp.zeros(lse_ref.shape, F32)
            for h in range(H):
                o_ref[h] = (acc_s[h] / l_s[h]).astype(BF16)
                lse_ref[:, h:h + 1] = m_s[h] + jnp.log(l_s[h])

    kv_idx = (lambda i, j: (0, jnp.minimum(i, j), 0)) if mask else (lambda i, j: (0, j, 0))
    ck_idx = (lambda i, j: (0, jnp.minimum(i, j))) if mask else (lambda i, j: (0, j))
    in_specs = [pl.BlockSpec((H, tq, dk), lambda i, j: (0, i, 0)),
                pl.BlockSpec((H, tk, dk), kv_idx), pl.BlockSpec((H, tk, dv), kv_idx)]
    ins = [q, k, v]
    if bias:
        in_specs += [pl.BlockSpec((tq, 8), lambda i, j: (i, 0)), pl.BlockSpec((8, tk), ck_idx)]
        ins += [cq, ck]
    return pl.pallas_call(
        body, name=name,
        out_shape=(jax.ShapeDtypeStruct((H, Sq, dv), BF16), jax.ShapeDtypeStruct((Sq, 8), F32)),
        grid=(nq, nk), in_specs=in_specs,
        out_specs=(pl.BlockSpec((H, tq, dv), lambda i, j: (0, i, 0)), pl.BlockSpec((tq, 8), lambda i, j: (i, 0))),
        scratch_shapes=[pltpu.VMEM((H, tq, 1), F32), pltpu.VMEM((H, tq, 1), F32), pltpu.VMEM((H, tq, dv), F32)],
        compiler_params=_params(("parallel", "arbitrary")),
    )(*ins)


def _flash_bwd(q, k, v, o, do, lse, cq, ck, *, scale, mask, name):
    H, Sq, dk = q.shape
    Sk, dv = k.shape[1], v.shape[2]
    tq = _pick(Sq, (512, 256, 128))
    tk = tq if mask else _pick(Sk, (512, 256, 128))
    nq, nk = Sq // tq, Sk // tk
    bias = cq is not None

    def body(*refs):
        q_ref, k_ref, v_ref, o_ref, do_ref, lse_ref = refs[:6]
        n_in = 8 if bias else 6
        cq_ref, ck_ref = (refs[6], refs[7]) if bias else (None, None)
        outs = refs[n_in:]
        dq_ref, dk_ref, dv_ref = outs[:3]
        dck_ref, dcq_ref = (outs[3], outs[4]) if bias else (None, None)
        dk_s, dv_s = refs[-2], refs[-1]
        ki, qi = pl.program_id(0), pl.program_id(1)
        first_q = ki if mask else 0

        @pl.when((ki == 0) & (qi == 0))
        def _():
            dq_ref[...] = jnp.zeros(dq_ref.shape, F32)
            if bias:
                dcq_ref[...] = jnp.zeros(dcq_ref.shape, F32)

        @pl.when(qi == first_q)
        def _():
            dk_s[...] = jnp.zeros(dk_s.shape, F32)
            dv_s[...] = jnp.zeros(dv_s.shape, F32)
            if bias:
                dck_ref[...] = jnp.zeros(dck_ref.shape, F32)

        def compute(masked):
            keep = _mask_of(mask, tq, tk) if masked else None
            rows = pl.ds(pl.multiple_of(qi * tq, tq), tq)
            for h in range(H):
                qh, kh, vh, doh = q_ref[h], k_ref[h], v_ref[h], do_ref[h]
                s = _dot(qh, kh, NT) * scale
                if bias:
                    s = s + (cq_ref[:, h:h + 1] - ck_ref[h:h + 1, :])
                if masked:
                    s = jnp.where(keep, s, NEG)
                p = jnp.exp(s - lse_ref[:, h:h + 1])
                dp = _dot(doh, vh, NT)
                delta = jnp.sum(doh.astype(F32) * o_ref[h].astype(F32), axis=1, keepdims=True)
                ds = p * (dp - delta)
                dv_s[h] += _dot(p, doh, TN)
                dk_s[h] += _dot(ds, qh, TN)
                dq_ref[h, rows, :] += _dot(ds, kh, NN) * scale
                if bias:
                    dck_ref[h:h + 1, :] -= jnp.sum(ds, axis=0, keepdims=True)
                    dcq_ref[rows, h:h + 1] += jnp.sum(ds, axis=1, keepdims=True)

        if mask is None:
            compute(False)
        else:
            pl.when(qi > ki)(lambda: compute(False))
            pl.when(qi == ki)(lambda: compute(True))

        @pl.when(qi == nq - 1)
        def _():
            dk_ref[...] = dk_s[...] * scale
            dv_ref[...] = dv_s[...]

    q_idx = (lambda j, i: (0, jnp.maximum(i, j), 0)) if mask else (lambda j, i: (0, i, 0))
    c_idx = (lambda j, i: (jnp.maximum(i, j), 0)) if mask else (lambda j, i: (i, 0))
    kv_idx = lambda j, i: (0, j, 0)
    in_specs = [pl.BlockSpec((H, tq, dk), q_idx), pl.BlockSpec((H, tk, dk), kv_idx), pl.BlockSpec((H, tk, dv), kv_idx),
                pl.BlockSpec((H, tq, dv), q_idx), pl.BlockSpec((H, tq, dv), q_idx), pl.BlockSpec((tq, 8), c_idx)]
    ins = [q, k, v, o, do, lse]
    out_shape = [jax.ShapeDtypeStruct((H, Sq, dk), F32), jax.ShapeDtypeStruct((H, Sk, dk), F32),
                 jax.ShapeDtypeStruct((H, Sk, dv), F32)]
    out_specs = [pl.BlockSpec((H, Sq, dk), lambda j, i: (0, 0, 0)), pl.BlockSpec((H, tk, dk), kv_idx),
                 pl.BlockSpec((H, tk, dv), kv_idx)]
    if bias:
        in_specs += [pl.BlockSpec((tq, 8), c_idx), pl.BlockSpec((8, tk), lambda j, i: (0, j))]
        ins += [cq, ck]
        out_shape += [jax.ShapeDtypeStruct((8, Sk), F32), jax.ShapeDtypeStruct((Sq, 8), F32)]
        out_specs += [pl.BlockSpec((8, tk), lambda j, i: (0, j)), pl.BlockSpec((Sq, 8), lambda j, i: (0, 0))]
    return pl.pallas_call(
        body, name=name, out_shape=tuple(out_shape), grid=(nk, nq), in_specs=in_specs, out_specs=tuple(out_specs),
        scratch_shapes=[pltpu.VMEM((H, tk, dk), F32), pltpu.VMEM((H, tk, dv), F32)],
        compiler_params=_params(("arbitrary", "arbitrary")),
    )(*ins)


def _split3_dot(x, t):
    hi = x.astype(BF16)
    r1 = x - hi.astype(F32)
    mid = r1.astype(BF16)
    lo = (r1 - mid.astype(F32)).astype(BF16)
    return _dot(hi, t, NN) + _dot(mid, t, NN) + _dot(lo, t, NN)


def _fox_cum_fwd(ff_t, b, *, name):
    _, S = ff_t.shape
    tb = _pick(S, (512, 256, 128))

    def body(f_ref, b_ref, o_ref, carry):
        @pl.when(pl.program_id(0) == 0)
        def _():
            carry[...] = jnp.zeros(carry.shape, F32)

        lf = _log_sigmoid(f_ref[...] + b_ref[...])
        o_ref[...] = _split3_dot(lf, _tri(tb, False)) + carry[...]
        carry[...] += jnp.sum(lf, axis=1, keepdims=True)

    return pl.pallas_call(
        body, name=name, out_shape=jax.ShapeDtypeStruct((8, S), F32), grid=(S // tb,),
        in_specs=[pl.BlockSpec((8, tb), lambda i: (0, i)), pl.BlockSpec((8, 1), lambda i: (0, 0))],
        out_specs=pl.BlockSpec((8, tb), lambda i: (0, i)),
        scratch_shapes=[pltpu.VMEM((8, 1), F32)],
        compiler_params=_params(("arbitrary",)),
    )(ff_t, b)


def _fox_cum_bwd(ff_t, b, dcum_t, *, name):
    _, S = ff_t.shape
    tb = _pick(S, (512, 256, 128))
    nb = S // tb

    def body(f_ref, b_ref, dc_ref, df_ref, db_ref, carry):
        @pl.when(pl.program_id(0) == 0)
        def _():
            carry[...] = jnp.zeros(carry.shape, F32)
            db_ref[...] = jnp.zeros(db_ref.shape, F32)

        dc = dc_ref[...]
        dlf = _split3_dot(dc, _tri(tb, True)) + carry[...]
        carry[...] += jnp.sum(dc, axis=1, keepdims=True)
        df = dlf * _sigmoid(-(f_ref[...] + b_ref[...]))
        df_ref[...] = df
        db_ref[...] += jnp.sum(df, axis=1, keepdims=True)

    rev = lambda i: (0, nb - 1 - i)
    return pl.pallas_call(
        body, name=name,
        out_shape=(jax.ShapeDtypeStruct((8, S), F32), jax.ShapeDtypeStruct((8, 1), F32)), grid=(nb,),
        in_specs=[pl.BlockSpec((8, tb), rev), pl.BlockSpec((8, 1), lambda i: (0, 0)), pl.BlockSpec((8, tb), rev)],
        out_specs=(pl.BlockSpec((8, tb), rev), pl.BlockSpec((8, 1), lambda i: (0, 0))),
        scratch_shapes=[pltpu.VMEM((8, 1), F32)],
        compiler_params=_params(("arbitrary",)),
    )(ff_t, b, dcum_t)


GLA_W = GLA_HEADS * GLA_DK
GLA_BLOCK_CHUNKS = 4


def _gla_chunk(q, k, zsm, wg, bg, go, vs, rs, states):
    la = _log_sigmoid(bdot(zsm, wg) + bg) * (1.0 / GLA_TAU)
    cum = chunk_cumsum(la)
    end = jnp.sum(la, axis=0, keepdims=True)
    kd = k * jnp.exp(end - cum)
    a = jnp.exp(end)
    qs = q * (GLA_DK ** -0.5)
    lane = lax.broadcasted_iota(jnp.int32, (1, GLA_W), 1)
    outs, new_states = [], []
    for h in range(GLA_HEADS):
        head = jnp.where((lane >= h * GLA_DK) & (lane < (h + 1) * GLA_DK), 1.0, 0.0)
        st = states[h] * a + bdot_tn(vs[h], kd * head)
        o = bdot_nt(qs, st)
        o = _rms(o, go)
        outs.append(o * (rs[h] * _sigmoid(rs[h])))
        new_states.append(st)
    return outs, new_states


def _gla_fwd(z, zsm, wg, bg, go, cols, *, name):
    S = z.shape[0]
    rb = GLA_BLOCK_CHUNKS * CHUNK
    nb = S // rb
    cq, ckk, cv, cr = cols
    H = GLA_HEADS

    def body(q_ref, k_ref, zsm_ref, wg_ref, bg_ref, go_ref, *rest):
        v_refs, r_refs = rest[:H], rest[H:2 * H]
        o_ref, st_ref, state = rest[2 * H], rest[2 * H + 1], rest[2 * H + 2]

        @pl.when(pl.program_id(0) == 0)
        def _():
            state[...] = jnp.zeros(state.shape, F32)

        wg_, bg_, go_ = wg_ref[...], bg_ref[...], go_ref[...]
        for c in range(GLA_BLOCK_CHUNKS):
            rows = pl.ds(c * CHUNK, CHUNK)
            states = [state[h] for h in range(H)]
            for h in range(H):
                st_ref[c, h] = states[h]
            outs, new_states = _gla_chunk(
                q_ref[rows, :].astype(F32), k_ref[rows, :].astype(F32), zsm_ref[rows, :], wg_, bg_, go_,
                [v_refs[h][rows, :].astype(F32) for h in range(H)], [r_refs[h][rows, :].astype(F32) for h in range(H)], states)
            for h in range(H):
                o_ref[rows, h * GLA_DV:(h + 1) * GLA_DV] = outs[h].astype(BF16)
                state[h] = new_states[h]

    def col(width, off):
        return pl.BlockSpec((rb, width), lambda i, o=off // width: (i, o))

    full = lambda shp: pl.BlockSpec(shp, lambda i: (0,) * len(shp))
    in_specs = [col(GLA_W, cq), col(GLA_W, ckk), pl.BlockSpec((rb, 128), lambda i: (i, 0)),
                full((128, GLA_W)), full((1, GLA_W)), full((1, GLA_DV))]
    in_specs += [col(GLA_DV, cv + h * GLA_DV) for h in range(H)] + [col(GLA_DV, cr + h * GLA_DV) for h in range(H)]
    return pl.pallas_call(
        body, name=name,
        out_shape=(jax.ShapeDtypeStruct((S, H * GLA_DV), BF16), jax.ShapeDtypeStruct((S // CHUNK, H, GLA_DV, GLA_W), F32)),
        grid=(nb,), in_specs=in_specs,
        out_specs=(pl.BlockSpec((rb, H * GLA_DV), lambda i: (i, 0)),
                   pl.BlockSpec((GLA_BLOCK_CHUNKS, H, GLA_DV, GLA_W), lambda i: (i, 0, 0, 0))),
        scratch_shapes=[pltpu.VMEM((H, GLA_DV, GLA_W), F32)],
        compiler_params=_params(("arbitrary",)),
    )(z, z, zsm, wg, bg, go, *([z] * (2 * H)))


def _gla_bwd(z, zsm, wg, bg, go, states, do, cols, *, name):
    S = z.shape[0]
    rb = GLA_BLOCK_CHUNKS * CHUNK
    nb = S // rb
    cq, ckk, cv, cr = cols
    H = GLA_HEADS

    def body(q_ref, k_ref, zsm_ref, wg_ref, bg_ref, go_ref, st_ref, do_ref, *rest):
        v_refs, r_refs = rest[:H], rest[H:2 * H]
        dq_ref, dk_ref, dv_ref, dr_ref, dzsm_ref, dwg_ref, dbg_ref, dgo_ref, dstate = rest[2 * H:]

        @pl.when(pl.program_id(0) == 0)
        def _():
            dstate[...] = jnp.zeros(dstate.shape, F32)
            dwg_ref[...] = jnp.zeros(dwg_ref.shape, F32)
            dbg_ref[...] = jnp.zeros(dbg_ref.shape, F32)
            dgo_ref[...] = jnp.zeros(dgo_ref.shape, F32)

        wg_, bg_, go_ = wg_ref[...], bg_ref[...], go_ref[...]
        for c in reversed(range(GLA_BLOCK_CHUNKS)):
            rows = pl.ds(c * CHUNK, CHUNK)
            prim = (q_ref[rows, :].astype(F32), k_ref[rows, :].astype(F32), zsm_ref[rows, :], wg_, bg_, go_,
                    [v_refs[h][rows, :].astype(F32) for h in range(H)], [r_refs[h][rows, :].astype(F32) for h in range(H)],
                    [st_ref[c, h] for h in range(H)])
            _, vjp = jax.vjp(_gla_chunk, *prim)
            douts = [do_ref[rows, h * GLA_DV:(h + 1) * GLA_DV].astype(F32) for h in range(H)]
            dq, dk, dzs, dwg, dbg, dgo, dvs, drs, dsts = vjp((douts, [dstate[h] for h in range(H)]))
            dq_ref[rows, :] = dq.astype(BF16)
            dk_ref[rows, :] = dk.astype(BF16)
            dzsm_ref[rows, :] = dzs
            dwg_ref[...] += dwg
            dbg_ref[...] += dbg
            dgo_ref[...] += dgo
            for h in range(H):
                dv_ref[rows, h * GLA_DV:(h + 1) * GLA_DV] = dvs[h].astype(BF16)
                dr_ref[rows, h * GLA_DV:(h + 1) * GLA_DV] = drs[h].astype(BF16)
                dstate[h] = dsts[h]

    rev = lambda i: nb - 1 - i

    def col(width, off):
        return pl.BlockSpec((rb, width), lambda i, o=off // width: (rev(i), o))

    full = lambda shp: pl.BlockSpec(shp, lambda i: (0,) * len(shp))
    rowb = lambda w: pl.BlockSpec((rb, w), lambda i: (rev(i), 0))
    in_specs = [col(GLA_W, cq), col(GLA_W, ckk), rowb(128), full((128, GLA_W)), full((1, GLA_W)), full((1, GLA_DV)),
                pl.BlockSpec((GLA_BLOCK_CHUNKS, H, GLA_DV, GLA_W), lambda i: (rev(i), 0, 0, 0)), rowb(H * GLA_DV)]
    in_specs += [col(GLA_DV, cv + h * GLA_DV) for h in range(H)] + [col(GLA_DV, cr + h * GLA_DV) for h in range(H)]
    return pl.pallas_call(
        body, name=name,
        out_shape=(jax.ShapeDtypeStruct((S, GLA_W), BF16), jax.ShapeDtypeStruct((S, GLA_W), BF16),
                   jax.ShapeDtypeStruct((S, H * GLA_DV), BF16), jax.ShapeDtypeStruct((S, H * GLA_DV), BF16),
                   jax.ShapeDtypeStruct((S, 128), F32), jax.ShapeDtypeStruct((128, GLA_W), F32),
                   jax.ShapeDtypeStruct((1, GLA_W), F32), jax.ShapeDtypeStruct((1, GLA_DV), F32)),
        grid=(nb,), in_specs=in_specs,
        out_specs=(rowb(GLA_W), rowb(GLA_W), rowb(H * GLA_DV), rowb(H * GLA_DV), rowb(128),
                   full((128, GLA_W)), full((1, GLA_W)), full((1, GLA_DV))),
        scratch_shapes=[pltpu.VMEM((H, GLA_DV, GLA_W), F32)],
        compiler_params=_params(("arbitrary",)),
    )(z, z, zsm, wg, bg, go, states, do, *([z] * (2 * H)))


def _row_spec(entry, tr):
    if isinstance(entry, tuple):
        arr, width, off = entry
        return arr, pl.BlockSpec((tr, width), lambda i, o=off // width: (i, o))
    return entry, pl.BlockSpec((tr, entry.shape[1]), lambda i: (i, 0))


def _stage_fwd(fn, rows, consts, outs, *, name, tr=None):
    first = rows[0][0] if isinstance(rows[0], tuple) else rows[0]
    S = first.shape[0]
    tr = tr or _pick(S, (512, 256, 128))
    arrs, specs = zip(*[_row_spec(e, tr) for e in rows])
    nr, nc = len(rows), len(consts)

    def body(*refs):
        vals = [r[...].astype(F32) for r in refs[:nr + nc]]
        res = fn(*vals)
        for o_ref, val in zip(refs[nr + nc:], res):
            o_ref[...] = val.astype(o_ref.dtype)

    cspecs = [pl.BlockSpec(c.shape, lambda i, n=c.ndim: (0,) * n) for c in consts]
    return pl.pallas_call(
        body, name=name,
        out_shape=tuple(jax.ShapeDtypeStruct((S, w), dt) for w, dt in outs), grid=(S // tr,),
        in_specs=list(specs) + cspecs,
        out_specs=tuple(pl.BlockSpec((tr, w), lambda i: (i, 0)) for w, _ in outs),
        compiler_params=_params(("parallel",)),
    )(*arrs, *consts)


def _stage_bwd(fn, rows, consts, cts, n_diff, drow_dtypes, *, name, tr=None):
    first = rows[0][0] if isinstance(rows[0], tuple) else rows[0]
    S = first.shape[0]
    tr = tr or _pick(S, (512, 256, 128))
    arrs, specs = zip(*[_row_spec(e, tr) for e in rows])
    widths = [e[1] if isinstance(e, tuple) else e.shape[1] for e in rows]
    nr, nc, nt = len(rows), len(consts), len(cts)

    def body(*refs):
        vals = [r[...].astype(F32) for r in refs[:nr + nc]]
        ct = [r[...].astype(F32) for r in refs[nr + nc:nr + nc + nt]]
        drow_refs = refs[nr + nc + nt:nr + nc + nt + n_diff]
        dconst_refs = refs[nr + nc + nt + n_diff:]
        rest_rows = vals[n_diff:nr]

        def f(diff_rows, cs):
            return tuple(fn(*diff_rows, *rest_rows, *cs))

        _, vjp = jax.vjp(f, vals[:n_diff], vals[nr:])
        drows, dcs = vjp(tuple(ct))
        for r, val in zip(drow_refs, drows):
            r[...] = val.astype(r.dtype)
        first_step = pl.program_id(0) == 0
        for r, val in zip(dconst_refs, dcs):
            @pl.when(first_step)
            def _(r=r, val=val):
                r[...] = val

            @pl.when(jnp.logical_not(first_step))
            def _(r=r, val=val):
                r[...] += val

    cspecs = [pl.BlockSpec(c.shape, lambda i, n=c.ndim: (0,) * n) for c in consts]
    ctspecs = [pl.BlockSpec((tr, c.shape[1]), lambda i: (i, 0)) for c in cts]
    out_shape = [jax.ShapeDtypeStruct((S, widths[j]), drow_dtypes[j]) for j in range(n_diff)]
    out_shape += [jax.ShapeDtypeStruct(c.shape, F32) for c in consts]
    out_specs = [pl.BlockSpec((tr, widths[j]), lambda i: (i, 0)) for j in range(n_diff)] + cspecs
    res = pl.pallas_call(
        body, name=name, out_shape=tuple(out_shape), grid=(S // tr,),
        in_specs=list(specs) + cspecs + ctspecs, out_specs=tuple(out_specs),
        compiler_params=_params(("arbitrary",)),
    )(*arrs, *consts, *cts)
    return list(res[:n_diff]), list(res[n_diff:])


def _mla_prep_fn(cq, ckv, kr1, kr2, cos16, sin16, cos64, sin64, gq, gkv, wq_n, wq_1, wq_2, wk, wv):
    hq = _rms(cq, gq)
    hkv = _rms(ckv, gkv)
    q1, q2 = bdot(hq, wq_1), bdot(hq, wq_2)
    return (bdot(hq, wq_n), q1 * cos64 - q2 * sin64, q2 * cos64 + q1 * sin64,
            bdot(hkv, wk), bdot(hkv, wv), kr1 * cos16 - kr2 * sin16, kr2 * cos16 + kr1 * sin16)


def _merge_fn(g0, g1, g2, of, og, om, b0, b1, b2, wf, wg, wm):
    return (_sigmoid(g0 + b0) * bdot(of, wf) + _sigmoid(g1 + b1) * bdot(og, wg) + _sigmoid(g2 + b2) * bdot(om, wm),)


_IN_SIZES = (256, 256, 256, 4, 256, 256, 512, 16, 512, 256, 128, 32, 3072)
_IN_OFF = np.concatenate([[0], np.cumsum(_IN_SIZES)])
(_O_FQ, _O_FK, _O_FV, _O_FF, _O_GQ, _O_GK, _O_GV, _O_GLOW, _O_GR, _O_MQ, _O_MKV, _O_MKR, _O_ZG) = [int(o) for o in _IN_OFF[:-1]]
N_IN = int(_IN_OFF[-1])
_BIG_GROUPS = ((_O_ZG, 3072), (_O_GV, 512), (_O_GR, 512), (_O_FQ, 256), (_O_FK, 256), (_O_FV, 256),
               (_O_GQ, 256), (_O_GK, 256), (_O_MQ, 256), (_O_MKV, 128))
Z_GATE, Z_GV, Z_GR, Z_FQ, Z_FK, Z_FV, Z_GQ, Z_GK, Z_MQ, Z_MKV = [int(o) for o in
                                                                    np.concatenate([[0], np.cumsum([w for _, w in _BIG_GROUPS])])[:-1]]
N_BIG = sum(w for _, w in _BIG_GROUPS)
SM_FF, SM_GLOW, SM_KR, N_SM = 0, 8, 32, 128
N_PAD = N_BIG + N_SM


def _in_perm():
    idx = np.concatenate([np.arange(o, o + w) for o, w in _BIG_GROUPS] + [np.zeros(N_SM, np.int64)])
    valid = np.concatenate([np.ones(N_BIG, bool), np.zeros(N_SM, bool)])
    for src, dst, w in ((_O_FF, SM_FF, 4), (_O_GLOW, SM_GLOW, 16), (_O_MKR, SM_KR, 32)):
        idx[N_BIG + dst:N_BIG + dst + w] = np.arange(src, src + w)
        valid[N_BIG + dst:N_BIG + dst + w] = True
    inv = np.zeros(N_IN, np.int64)
    inv[idx[valid]] = np.nonzero(valid)[0]
    return idx, valid, inv


_IN_IDX, _IN_VALID, _IN_INV = _in_perm()

_HALF = MLA_ROPE // 2
_QK_HD = MLA_NOPE + MLA_ROPE
_UQ_PERM = np.concatenate(
    [np.concatenate([np.arange(h * _QK_HD, h * _QK_HD + MLA_NOPE) for h in range(MLA_HEADS)]),
     np.concatenate([np.arange(h * _QK_HD + MLA_NOPE, h * _QK_HD + MLA_NOPE + _HALF) for h in range(MLA_HEADS)]),
     np.concatenate([np.arange(h * _QK_HD + MLA_NOPE + _HALF, (h + 1) * _QK_HD) for h in range(MLA_HEADS)])])
_UKV_PERM = np.concatenate(
    [np.concatenate([np.arange(h * 128, h * 128 + MLA_NOPE) for h in range(MLA_HEADS)]),
     np.concatenate([np.arange(h * 128 + MLA_NOPE, (h + 1) * 128) for h in range(MLA_HEADS)])])
_UQ_INV = np.argsort(_UQ_PERM)
_UKV_INV = np.argsort(_UKV_PERM)


def _heads(a, n):
    s, w = a.shape
    return a.reshape(s, n, w // n).transpose(1, 0, 2)


def _unheads(a):
    n, s, d = a.shape
    return a.transpose(1, 0, 2).reshape(s, n * d)


def _rope_tables(S):
    inv = ROPE_BASE ** (-jnp.arange(_HALF, dtype=F32) / _HALF)
    ang = jnp.arange(S, dtype=F32)[:, None] * inv[None, :]
    cos, sin = jnp.cos(ang), jnp.sin(ang)
    return cos, sin, jnp.tile(cos, (1, MLA_HEADS)), jnp.tile(sin, (1, MLA_HEADS))


def _prep_layer(w, l):
    p = {}
    p['w_big'] = w['w_in'][l][:, :N_BIG]
    p['w_sm'] = w['w_in'][l][:, N_BIG:]
    p['wg'] = jnp.zeros((N_SM, GLA_W), BF16).at[SM_GLOW:SM_GLOW + GLA_RANK].set(w['w_gla_gate'][l])
    uq = w['w_mla_uq'][l][:, _UQ_PERM]
    p['wq_n'], p['wq_1'], p['wq_2'] = uq[:, :256], uq[:, 256:320], uq[:, 320:]
    ukv = w['w_mla_ukv'][l][:, _UKV_PERM]
    p['wk'], p['wv'] = ukv[:, :256], ukv[:, 256:]
    for n in ('w_up_fox', 'w_up_gla', 'w_up_mla', 'w_out', 'w_xq', 'w_xkv', 'w_xo', 'w_mlp1', 'w_mlp2',
              'g_mix', 'g_xa', 'g_mem', 'g_mlp'):
        p[n] = w[n][l]
    p['b_f'] = jnp.zeros((8, 1), F32).at[:FOX_HEADS, 0].set(w['b_fox_forget'][l])
    p['bg'] = w['b_gla_gate'][l].reshape(1, GLA_W)
    p['go'] = w['g_gla_out'][l].reshape(1, GLA_DV)
    p['gq'] = w['g_mla_q'][l].reshape(1, MLA_Q_RANK)
    p['gkv'] = w['g_mla_kv'][l].reshape(1, MLA_KV_RANK)
    p['b_gate'] = [w['b_branch_gate'][l][i * 1024:(i + 1) * 1024].reshape(1, 1024) for i in range(3)]
    return p


_GLA_COLS = (Z_GQ, Z_GK, Z_GV, Z_GR)
_MLA_OUTS = [(256, BF16), (64, BF16), (64, BF16), (256, BF16), (256, BF16), (_HALF, BF16), (_HALF, BF16)]


def _mla_rows(z, zsm, rope):
    kr = zsm[:, SM_KR:SM_KR + MLA_ROPE]
    return [(z, 256, Z_MQ), (z, 128, Z_MKV), kr[:, :_HALF], kr[:, _HALF:], rope[0], rope[1], rope[2], rope[3]]


def _mla_consts(p):
    return [p['gq'], p['gkv'], p['wq_n'], p['wq_1'], p['wq_2'], p['wk'], p['wv']]


def _merge_rows(z, o_fox, o_gla, o_mla):
    return [(z, 1024, Z_GATE), (z, 1024, Z_GATE + 1024), (z, 1024, Z_GATE + 2048), o_fox, o_gla, o_mla]


def _merge_consts(p):
    return p['b_gate'] + [p['w_up_fox'], p['w_up_gla'], p['w_up_mla']]


def _layer_fwd(x0, mem, p, rope, l):
    S = x0.shape[0]
    sv = {'x0': x0}
    h1 = _rms_fwd(x0, p['g_mix'], name=f"rms_mix_{l}")
    z = _mm(h1, p['w_big'], mode='nn', out_dtype=BF16, name=f"in_big_{l}")
    zsm = _mm(h1, p['w_sm'], mode='nn', out_dtype=F32, name=f"in_small_{l}")
    sv.update(h1=h1, z=z, zsm=zsm)
    ff_t = jnp.zeros((8, S), F32).at[:FOX_HEADS].set(zsm[:, SM_FF:SM_FF + FOX_HEADS].T)
    cum_t = _fox_cum_fwd(ff_t, p['b_f'], name=f"fox_cum_{l}")
    cum = cum_t.T
    fq, fk, fv = (_heads(z[:, o:o + 256], FOX_HEADS) for o in (Z_FQ, Z_FK, Z_FV))
    o_fox_h, lse_f = _flash_fwd(fq, fk, fv, cum, cum_t, scale=FOX_HD ** -0.5, mask='causal', name=f"fox_fwd_{l}")
    o_fox = _unheads(o_fox_h)
    sv.update(ff_t=ff_t, cum=cum, cum_t=cum_t, fq=fq, fk=fk, fv=fv, o_fox_h=o_fox_h, lse_f=lse_f, o_fox=o_fox)
    o_gla, states = _gla_fwd(z, zsm, p['wg'], p['bg'], p['go'], _GLA_COLS, name=f"gla_fwd_{l}")
    sv.update(o_gla=o_gla, states=states)
    qn, q1, q2, kn, vv, k1, k2 = _stage_fwd(_mla_prep_fn, _mla_rows(z, zsm, rope), _mla_consts(p), _MLA_OUTS,
                                            name=f"mla_prep_{l}")
    mq = jnp.concatenate([qn.reshape(S, MLA_HEADS, MLA_NOPE), q1.reshape(S, MLA_HEADS, _HALF),
                          q2.reshape(S, MLA_HEADS, _HALF)], axis=-1).transpose(1, 0, 2)
    mk = jnp.concatenate([kn.reshape(S, MLA_HEADS, MLA_NOPE),
                          jnp.broadcast_to(k1[:, None, :], (S, MLA_HEADS, _HALF)),
                          jnp.broadcast_to(k2[:, None, :], (S, MLA_HEADS, _HALF))], axis=-1).transpose(1, 0, 2)
    mv = _heads(vv, MLA_HEADS)
    o_mla_h, lse_m = _flash_fwd(mq, mk, mv, None, None, scale=_QK_HD ** -0.5, mask='chunk', name=f"mla_fwd_{l}")
    o_mla = _unheads(o_mla_h)
    sv.update(mq=mq, mk=mk, mv=mv, o_mla_h=o_mla_h, lse_m=lse_m, o_mla=o_mla)
    (y,) = _stage_fwd(_merge_fn, _merge_rows(z, o_fox, o_gla, o_mla), _merge_consts(p), [(1024, BF16)], name=f"merge_{l}")
    x1 = _mm(y, p['w_out'], mode='nn', out_dtype=F32, residual=x0, name=f"out_proj_{l}")
    sv.update(y=y, x1=x1)
    h2 = _rms_fwd(x1, p['g_xa'], name=f"rms_xa_{l}")
    hm = _rms_fwd(mem, p['g_mem'], name=f"rms_mem_{l}")
    qx = _heads(_mm(h2, p['w_xq'], mode='nn', out_dtype=BF16, name=f"xq_{l}"), XA_HEADS)
    kvx = _mm(hm, p['w_xkv'], mode='nn', out_dtype=BF16, name=f"xkv_{l}")
    kx, vx = _heads(kvx[:, :512], XA_HEADS), _heads(kvx[:, 512:], XA_HEADS)
    ox_h, lse_x = _flash_fwd(qx, kx, vx, None, None, scale=XA_HD ** -0.5, mask=None, name=f"xa_fwd_{l}")
    ox = _unheads(ox_h)
    x2 = _mm(ox, p['w_xo'], mode='nn', out_dtype=F32, residual=x1, name=f"xo_{l}")
    sv.update(h2=h2, hm=hm, qx=qx, kx=kx, vx=vx, ox_h=ox_h, lse_x=lse_x, ox=ox, x2=x2)
    h3 = _rms_fwd(x2, p['g_mlp'], name=f"rms_mlp_{l}")
    a = _mm(h3, p['w_mlp1'], mode='nn', out_dtype=BF16, name=f"mlp1_{l}")
    x3 = _mm(a, p['w_mlp2'], mode='nn', out_dtype=F32, act='relu2', residual=x2, name=f"mlp2_{l}")
    sv.update(h3=h3, a=a)
    return x3, sv


def _layer_bwd(dx3, mem, p, rope, sv, l):
    S = dx3.shape[0]
    g = {}
    dx3b = dx3.astype(BF16)
    da = _mm(dx3b, p['w_mlp2'], mode='nt', out_dtype=BF16, drelu_of=sv['a'], name=f"d_mlp2_in_{l}")
    g['w_mlp2'] = _mm(sv['a'], dx3b, mode='tn', out_dtype=F32, act='relu2', name=f"d_w_mlp2_{l}")
    dh3 = _mm(da, p['w_mlp1'], mode='nt', out_dtype=F32, name=f"d_mlp1_in_{l}")
    g['w_mlp1'] = _mm(sv['h3'], da, mode='tn', out_dtype=F32, name=f"d_w_mlp1_{l}")
    dx2, g['g_mlp'] = _rms_bwd(sv['x2'], p['g_mlp'], dh3, dx3, name=f"d_rms_mlp_{l}")
    dx2b = dx2.astype(BF16)
    dox = _mm(dx2b, p['w_xo'], mode='nt', out_dtype=BF16, name=f"d_xo_in_{l}")
    g['w_xo'] = _mm(sv['ox'], dx2b, mode='tn', out_dtype=F32, name=f"d_w_xo_{l}")
    dqx, dkx, dvx = _flash_bwd(sv['qx'], sv['kx'], sv['vx'], sv['ox_h'], _heads(dox, XA_HEADS), sv['lse_x'], None, None,
                               scale=XA_HD ** -0.5, mask=None, name=f"xa_bwd_{l}")
    dqx = _unheads(dqx).astype(BF16)
    dkvx = jnp.concatenate([_unheads(dkx), _unheads(dvx)], axis=1).astype(BF16)
    dh2 = _mm(dqx, p['w_xq'], mode='nt', out_dtype=F32, name=f"d_xq_in_{l}")
    g['w_xq'] = _mm(sv['h2'], dqx, mode='tn', out_dtype=F32, name=f"d_w_xq_{l}")
    dhm = _mm(dkvx, p['w_xkv'], mode='nt', out_dtype=F32, name=f"d_xkv_in_{l}")
    g['w_xkv'] = _mm(sv['hm'], dkvx, mode='tn', out_dtype=F32, name=f"d_w_xkv_{l}")
    _, g['g_mem'] = _rms_bwd(mem, p['g_mem'], dhm, None, name=f"d_rms_mem_{l}")
    dx1, g['g_xa'] = _rms_bwd(sv['x1'], p['g_xa'], dh2, dx2, name=f"d_rms_xa_{l}")
    dx1b = dx1.astype(BF16)
    dy = _mm(dx1b, p['w_out'], mode='nt', out_dtype=F32, name=f"d_out_in_{l}")
    g['w_out'] = _mm(sv['y'], dx1b, mode='tn', out_dtype=F32, name=f"d_w_out_{l}")
    z, zsm = sv['z'], sv['zsm']
    (dg0, dg1, dg2, do_fox, do_gla, do_mla), (db0, db1, db2, g['w_up_fox'], g['w_up_gla'], g['w_up_mla']) = _stage_bwd(
        _merge_fn, _merge_rows(z, sv['o_fox'], sv['o_gla'], sv['o_mla']), _merge_consts(p), [dy], 6, [BF16] * 6,
        name=f"merge_bwd_{l}")
    g['b_branch_gate'] = jnp.concatenate([db0, db1, db2], axis=1).reshape(-1)
    dfq, dfk, dfv, dck, dcq = _flash_bwd(sv['fq'], sv['fk'], sv['fv'], sv['o_fox_h'], _heads(do_fox, FOX_HEADS), sv['lse_f'],
                                         sv['cum'], sv['cum_t'], scale=FOX_HD ** -0.5, mask='causal', name=f"fox_bwd_{l}")
    dff_t, db_f = _fox_cum_bwd(sv['ff_t'], p['b_f'], dck + dcq.T, name=f"fox_cum_bwd_{l}")
    g['b_fox_forget'] = db_f[:FOX_HEADS, 0]
    dgq, dgk, dgv, dgr, dzsm, dwg, dbg, dgo = _gla_bwd(z, zsm, p['wg'], p['bg'], p['go'], sv['states'], do_gla, _GLA_COLS,
                                                       name=f"gla_bwd_{l}")
    g['w_gla_gate'] = dwg[SM_GLOW:SM_GLOW + GLA_RANK]
    g['b_gla_gate'] = dbg.reshape(-1)
    g['g_gla_out'] = dgo.reshape(-1)
    dmq, dmk, dmv = _flash_bwd(sv['mq'], sv['mk'], sv['mv'], sv['o_mla_h'], _heads(do_mla, MLA_HEADS), sv['lse_m'], None, None,
                               scale=_QK_HD ** -0.5, mask='chunk', name=f"mla_bwd_{l}")
    dmq_r = dmq.transpose(1, 0, 2)
    cts = [dmq_r[:, :, :MLA_NOPE].reshape(S, 256), dmq_r[:, :, MLA_NOPE:MLA_NOPE + _HALF].reshape(S, 64),
           dmq_r[:, :, MLA_NOPE + _HALF:].reshape(S, 64), _unheads(dmk[:, :, :MLA_NOPE]), _unheads(dmv),
           jnp.sum(dmk[:, :, MLA_NOPE:MLA_NOPE + _HALF], axis=0), jnp.sum(dmk[:, :, MLA_NOPE + _HALF:], axis=0)]
    (dcq, dckv, dkr1, dkr2), (dgq_n, dgkv_n, dwq_n, dwq_1, dwq_2, dwk, dwv) = _stage_bwd(
        _mla_prep_fn, _mla_rows(z, zsm, rope), _mla_consts(p), cts, 4, [BF16, BF16, F32, F32], name=f"mla_prep_bwd_{l}")
    g['g_mla_q'] = dgq_n.reshape(-1)
    g['g_mla_kv'] = dgkv_n.reshape(-1)
    g['w_mla_uq'] = jnp.concatenate([dwq_n, dwq_1, dwq_2], axis=1)[:, _UQ_INV]
    g['w_mla_ukv'] = jnp.concatenate([dwk, dwv], axis=1)[:, _UKV_INV]
    dz = jnp.concatenate([dg0, dg1, dg2, dgv, dgr, _unheads(dfq).astype(BF16), _unheads(dfk).astype(BF16),
                          _unheads(dfv).astype(BF16), dgq, dgk, dcq, dckv], axis=1)
    dzsm = dzsm + jnp.concatenate([dff_t[:FOX_HEADS].T, jnp.zeros((S, SM_KR - FOX_HEADS), F32), dkr1, dkr2,
                                   jnp.zeros((S, N_SM - SM_KR - MLA_ROPE), F32)], axis=1)
    dzsm = dzsm.astype(BF16)
    dh1 = _mm(dz, p['w_big'], mode='nt', out_dtype=F32, name=f"d_in_big_{l}")
    dh1 = _mm(dzsm, p['w_sm'], mode='nt', out_dtype=F32, residual=dh1, name=f"d_in_small_{l}")
    g['w_in'] = jnp.concatenate([_mm(sv['h1'], dz, mode='tn', out_dtype=F32, name=f"d_w_big_{l}"),
                                 _mm(sv['h1'], dzsm, mode='tn', out_dtype=F32, name=f"d_w_small_{l}")], axis=1)
    dx0, g['g_mix'] = _rms_bwd(sv['x0'], p['g_mix'], dh1, dx1, name=f"d_rms_mix_{l}")
    for n in ('g_mlp', 'g_mem', 'g_xa', 'g_mix'):
        g[n] = g[n].reshape(-1)
    return dx0, g


def _local_step(x, mem, target, w):
    S = x.shape[0]
    depth = w['g_mix'].shape[0]
    rope = _rope_tables(S)
    ps = [_prep_layer(w, l) for l in range(depth)]
    saved = []
    for l in range(depth):
        x, sv = _layer_fwd(x, mem, ps[l], rope, l)
        saved.append(sv)
    loss, dx, dgf = _loss_head(x, w['g_final'], target, name="loss_head")
    grads = [None] * depth
    for l in reversed(range(depth)):
        dx, grads[l] = _layer_bwd(dx, mem, ps[l], rope, saved[l], l)
    return loss, dx, grads, dgf.reshape(-1)


_MESH_AXES = ("x", "y", "c")
_HBM = pl.BlockSpec(memory_space=pl.ANY)


N_CHIP = 4


def _place():
    x, y, c = (lax.axis_index(n) for n in _MESH_AXES)
    return (x, y, c), (x, y, 1 - c), [(1 - x, y), (x, 1 - y), (1 - x, 1 - y)]


def _remote(src, dst, sems, k, to):
    return pltpu.make_async_remote_copy(src_ref=src, dst_ref=dst, send_sem=sems[0].at[k], recv_sem=sems[1].at[k],
                                        device_id=to, device_id_type=pl.DeviceIdType.MESH)


def _all_gather(x, *, name):
    def body(x_ref, o_ref, send_sems, recv_sems, local_sem):
        me, sib, chips = _place()
        c = me[2]
        sems = (send_sems, recv_sems)
        slot = lambda px, py, pc: o_ref.at[4 * px + 2 * py + pc]
        mine = pltpu.make_async_copy(x_ref, slot(*me), local_sem)
        mine.start()
        first = [_remote(x_ref, slot(*me), sems, 0, sib)]
        first += [_remote(x_ref, slot(*me), sems, 1 + j, (*chip, c)) for j, chip in enumerate(chips)]
        for cp in first:
            cp.start()
        passed = [_remote(slot(*chip, c), slot(*chip, c), sems, 4 + j, sib) for j, chip in enumerate(chips)]
        for j, chip in enumerate(chips):
            _remote(x_ref, slot(*chip, c), sems, 1 + j, me).wait_recv()
            passed[j].start()
        _remote(x_ref, slot(*sib), sems, 0, me).wait_recv()
        for j, chip in enumerate(chips):
            _remote(x_ref, slot(*chip, 1 - c), sems, 4 + j, me).wait_recv()
        for cp in first + passed:
            cp.wait_send()
        mine.wait()

    return pl.pallas_call(
        body, name=name, out_shape=jax.ShapeDtypeStruct((N_DEV,) + x.shape, x.dtype),
        in_specs=[_HBM], out_specs=_HBM,
        scratch_shapes=[pltpu.SemaphoreType.DMA((N_DEV - 1,)), pltpu.SemaphoreType.DMA((N_DEV - 1,)), pltpu.SemaphoreType.DMA],
        compiler_params=pltpu.CompilerParams(has_side_effects=True),
    )(x)


def _sibling_swap(x, *, name):
    def body(x_ref, o_ref, send_sems, recv_sems):
        me, sib, _ = _place()
        c = me[2]
        sems = (send_sems, recv_sems)
        sends = [_remote(x_ref.at[j, 1 - c], o_ref.at[j], sems, j, sib) for j in range(N_CHIP)]
        for cp in sends:
            cp.start()
        for cp in sends:
            cp.wait_send()
            cp.wait_recv()

    return pl.pallas_call(
        body, name=name, out_shape=jax.ShapeDtypeStruct((N_CHIP,) + x.shape[2:], x.dtype),
        in_specs=[_HBM], out_specs=_HBM,
        scratch_shapes=[pltpu.SemaphoreType.DMA((N_CHIP,)), pltpu.SemaphoreType.DMA((N_CHIP,))],
        compiler_params=pltpu.CompilerParams(has_side_effects=True),
    )(x)


def _pair_sum(x, got, c, *, name):
    _, _, R, _ = x.shape
    tr = _pick(R, (1024, 512, 256, 128, 64, 32, 16, 8))

    def body(c_ref, x_ref, g_ref, o_ref):
        o_ref[...] = (x_ref[...].astype(F32) + g_ref[...].astype(F32)).astype(o_ref.dtype)

    return pl.pallas_call(
        body, name=name, out_shape=jax.ShapeDtypeStruct((N_CHIP, R, 128), x.dtype),
        grid_spec=pltpu.PrefetchScalarGridSpec(
            num_scalar_prefetch=1, grid=(N_CHIP, R // tr),
            in_specs=[pl.BlockSpec((None, None, tr, 128), lambda j, i, c_ref: (j, c_ref[0], i, 0)),
                      pl.BlockSpec((None, tr, 128), lambda j, i, c_ref: (j, i, 0))],
            out_specs=pl.BlockSpec((None, tr, 128), lambda j, i, c_ref: (j, i, 0))),
        compiler_params=_params(("parallel", "parallel")),
    )(c, x, got)


def _chip_all_to_all(x, *, name):
    def body(x_ref, o_ref, send_sems, recv_sems, local_sem):
        me, _, chips = _place()
        c = me[2]
        sems = (send_sems, recv_sems)
        mine = 2 * me[0] + me[1]
        local = pltpu.make_async_copy(x_ref.at[mine], o_ref.at[mine], local_sem)
        local.start()
        sends = [_remote(x_ref.at[2 * px + py], o_ref.at[mine], sems, j, (px, py, c)) for j, (px, py) in enumerate(chips)]
        for cp in sends:
            cp.start()
        for j, (px, py) in enumerate(chips):
            sends[j].wait_send()
            _remote(x_ref.at[mine], o_ref.at[2 * px + py], sems, j, me).wait_recv()
        local.wait()

    return pl.pallas_call(
        body, name=name, out_shape=jax.ShapeDtypeStruct(x.shape, x.dtype),
        in_specs=[_HBM], out_specs=_HBM,
        scratch_shapes=[pltpu.SemaphoreType.DMA((N_CHIP - 1,)), pltpu.SemaphoreType.DMA((N_CHIP - 1,)), pltpu.SemaphoreType.DMA],
        compiler_params=pltpu.CompilerParams(has_side_effects=True),
    )(x)


def _sum_slots(x, *, name):
    n, R, _ = x.shape
    tr = _pick(R, (1024, 512, 256, 128, 64, 32, 16, 8))

    def body(x_ref, o_ref):
        acc = x_ref[0].astype(F32)
        for j in range(1, n):
            acc = acc + x_ref[j].astype(F32)
        o_ref[...] = acc

    return pl.pallas_call(
        body, name=name, out_shape=jax.ShapeDtypeStruct((R, 128), F32), grid=(R // tr,),
        in_specs=[pl.BlockSpec((n, tr, 128), lambda i: (0, i, 0))], out_specs=pl.BlockSpec((tr, 128), lambda i: (i, 0)),
        compiler_params=_params(("parallel",)),
    )(x)


def _adamw(w, g, m, v, *, name):
    shape = w.shape
    cols = shape[-1]
    rows = int(np.prod(shape[:-1]))
    tr = next((t for t in (1024, 512, 256, 128, 64, 32, 16, 8) if rows % t == 0 and t * cols * 4 <= (1 << 20)), rows)

    def body(w_ref, g_ref, m_ref, v_ref, d_ref, mo_ref, vo_ref):
        g_ = g_ref[...]
        m_ = ADAM_B1 * m_ref[...] + (1.0 - ADAM_B1) * g_
        v_ = ADAM_B2 * v_ref[...] + (1.0 - ADAM_B2) * jnp.square(g_)
        m_hat = m_ / (1.0 - ADAM_B1 ** ADAM_STEP)
        v_hat = v_ / (1.0 - ADAM_B2 ** ADAM_STEP)
        d_ref[...] = -ADAM_LR * (m_hat / (jnp.sqrt(v_hat) + ADAM_EPS) + ADAM_WD * w_ref[...])
        mo_ref[...] = m_
        vo_ref[...] = v_

    blk = pl.BlockSpec((tr, cols), lambda i: (i, 0))
    outs = pl.pallas_call(
        body, name=name, out_shape=tuple(jax.ShapeDtypeStruct((rows, cols), F32) for _ in range(3)), grid=(rows // tr,),
        in_specs=[blk] * 4, out_specs=(blk,) * 3, compiler_params=_params(("parallel",)),
    )(*(a.reshape(rows, cols) for a in (w, g, m, v)))
    return tuple(o.reshape(shape) for o in outs)


_WEIGHTS = ('g_mix', 'w_in', 'b_fox_forget', 'w_gla_gate', 'b_gla_gate', 'g_gla_out', 'g_mla_q', 'w_mla_uq', 'g_mla_kv',
            'w_mla_ukv', 'b_branch_gate', 'w_up_fox', 'w_up_gla', 'w_up_mla', 'w_out', 'g_xa', 'g_mem', 'w_xq', 'w_xkv',
            'w_xo', 'g_mlp', 'w_mlp1', 'w_mlp2', 'g_final')
_SHARDED = (('w_in', 1), ('w_gla_gate', 2), ('w_mla_uq', 2), ('w_mla_ukv', 2), ('w_up_fox', 2), ('w_up_gla', 2),
            ('w_up_mla', 2), ('w_out', 1), ('w_xq', 1), ('w_xkv', 1), ('w_xo', 2), ('w_mlp1', 2), ('w_mlp2', 1))
_REPLICATED = tuple(n for n in _WEIGHTS if n not in dict(_SHARDED))
_ROW_PAD = 1024
_SMALL_ROW_PAD = 8


def _pack(flats, lead, row_pad=_ROW_PAD):
    cat = jnp.concatenate([a.reshape(a.shape[:lead] + (-1,)) for a in flats], axis=-1)
    n = cat.shape[-1]
    total = -(-n // (128 * row_pad)) * (128 * row_pad)
    cat = jnp.pad(cat, [(0, 0)] * lead + [(0, total - n)])
    return cat.reshape(cat.shape[:lead] + (total // 128, 128))


def _unpack(buf, shapes, lead):
    flat = buf.reshape(buf.shape[:lead] + (-1,))
    out, off = [], 0
    for shp in shapes:
        n = int(np.prod(shp))
        out.append(flat[..., off:off + n].reshape(buf.shape[:lead] + tuple(shp)))
        off += n
    return out


def _to_whole(g, axis):
    if axis == 1:
        return g.transpose(1, 0, 2, 3).reshape(g.shape[1], N_DEV * g.shape[2], g.shape[3])
    return g.transpose(1, 2, 0, 3).reshape(g.shape[1], g.shape[2], N_DEV * g.shape[3])


def _to_shards(w, axis):
    L, R, C = w.shape
    if axis == 1:
        return w.reshape(L, N_DEV, R // N_DEV, C).transpose(1, 0, 2, 3)
    return w.reshape(L, R, N_DEV, C // N_DEV).transpose(2, 0, 1, 3)


def kernel(x, mem, g_mix, w_in, b_fox_forget, w_gla_gate, b_gla_gate, g_gla_out, g_mla_q, w_mla_uq, g_mla_kv, w_mla_ukv, b_branch_gate, w_up_fox, w_up_gla, w_up_mla, w_out, g_xa, g_mem, w_xq, w_xkv, w_xo, g_mlp, w_mlp1, w_mlp2, g_final, loss_target, m_g_mix, m_w_in, m_b_fox_forget, m_w_gla_gate, m_b_gla_gate, m_g_gla_out, m_g_mla_q, m_w_mla_uq, m_g_mla_kv, m_w_mla_ukv, m_b_branch_gate, m_w_up_fox, m_w_up_gla, m_w_up_mla, m_w_out, m_g_xa, m_g_mem, m_w_xq, m_w_xkv, m_w_xo, m_g_mlp, m_w_mlp1, m_w_mlp2, m_g_final, v_g_mix, v_w_in, v_b_fox_forget, v_w_gla_gate, v_b_gla_gate, v_g_gla_out, v_g_mla_q, v_w_mla_uq, v_g_mla_kv, v_w_mla_ukv, v_b_branch_gate, v_w_up_fox, v_w_up_gla, v_w_up_mla, v_w_out, v_g_xa, v_g_mem, v_w_xq, v_w_xkv, v_w_xo, v_g_mlp, v_w_mlp1, v_w_mlp2, v_g_final):
    wts = dict(zip(_WEIGHTS, (g_mix, w_in, b_fox_forget, w_gla_gate, b_gla_gate, g_gla_out, g_mla_q, w_mla_uq, g_mla_kv,
                              w_mla_ukv, b_branch_gate, w_up_fox, w_up_gla, w_up_mla, w_out, g_xa, g_mem, w_xq, w_xkv, w_xo,
                              g_mlp, w_mlp1, w_mlp2, g_final)))
    mom1 = dict(zip(_WEIGHTS, (m_g_mix, m_w_in, m_b_fox_forget, m_w_gla_gate, m_b_gla_gate, m_g_gla_out, m_g_mla_q,
                               m_w_mla_uq, m_g_mla_kv, m_w_mla_ukv, m_b_branch_gate, m_w_up_fox, m_w_up_gla, m_w_up_mla,
                               m_w_out, m_g_xa, m_g_mem, m_w_xq, m_w_xkv, m_w_xo, m_g_mlp, m_w_mlp1, m_w_mlp2, m_g_final)))
    mom2 = dict(zip(_WEIGHTS, (v_g_mix, v_w_in, v_b_fox_forget, v_w_gla_gate, v_b_gla_gate, v_g_gla_out, v_g_mla_q,
                               v_w_mla_uq, v_g_mla_kv, v_w_mla_ukv, v_b_branch_gate, v_w_up_fox, v_w_up_gla, v_w_up_mla,
                               v_w_out, v_g_xa, v_g_mem, v_w_xq, v_w_xkv, v_w_xo, v_g_mlp, v_w_mlp1, v_w_mlp2, v_g_final)))
    depth = g_mix.shape[0]

    shard = {n: wts[n] for n, _ in _SHARDED}
    shard['w_in'] = jnp.where(_IN_VALID[None, None, :], w_in[:, :, _IN_IDX], 0.0)
    shard_shapes = [shard[n].shape for n, _ in _SHARDED]
    gathered = _all_gather(_pack([shard[n].astype(BF16) for n, _ in _SHARDED], 0), name="gather_weights")
    whole = {n: _to_whole(g, ax) for (n, ax), g in zip(_SHARDED, _unpack(gathered, shard_shapes, 1))}
    whole.update({n: wts[n] for n in _REPLICATED})

    loss, dx, grads, dg_final = _local_step(x[0], mem[0], loss_target[0], whole)
    loss = lax.psum(loss[0, 0], _MESH_AXES)

    slots = _pack([_to_shards(jnp.stack([grads[l][n] for l in range(depth)]), ax).astype(BF16) for n, ax in _SHARDED], 1)
    slots = slots.reshape((N_CHIP, 2) + slots.shape[1:])
    core = lax.axis_index("c").astype(jnp.int32).reshape(1)
    paired = _pair_sum(slots, _sibling_swap(slots, name="swap_grads"), core, name="pair_grads")
    summed = _sum_slots(_chip_all_to_all(paired, name="scatter_grads"), name="sum_grads")
    grad = dict(zip([n for n, _ in _SHARDED], _unpack(summed, shard_shapes, 0)))
    grad['w_in'] = grad['w_in'][:, :, _IN_INV]
    small = [dg_final if n == 'g_final' else jnp.stack([grads[l][n] for l in range(depth)]) for n in _REPLICATED]
    small_shapes = [wts[n].shape for n in _REPLICATED]
    small_sum = _sum_slots(_all_gather(_pack(small, 0, _SMALL_ROW_PAD), name="gather_small_grads"), name="sum_small_grads")
    grad.update(dict(zip(_REPLICATED, _unpack(small_sum, small_shapes, 0))))

    delta, new_m, new_v = {}, {}, {}
    for n, _ in _SHARDED:
        delta[n], new_m[n], new_v[n] = _adamw(wts[n], grad[n], mom1[n], mom2[n], name=f"adamw_{n}")
    packed = [_pack([d[n] for n in _REPLICATED], 0, _SMALL_ROW_PAD) for d in (wts, mom1, mom2)]
    outs = _adamw(packed[0], small_sum, packed[1], packed[2], name="adamw_small")
    for d, o in zip((delta, new_m, new_v), outs):
        d.update(dict(zip(_REPLICATED, _unpack(o, small_shapes, 0))))

    return (loss, dx[None], *[grad[n] for n in _WEIGHTS], *[delta[n] for n in _WEIGHTS],
            *[new_m[n] for n in _WEIGHTS], *[new_v[n] for n in _WEIGHTS])
```

```python
import jax
import jax.numpy as jnp
import numpy as np
from jax import lax
from jax.experimental import pallas as pl
from jax.experimental.pallas import tpu as pltpu

F32 = jnp.float32
BF16 = jnp.bfloat16

EPS = 1e-6
CHUNK = 64
FOX_HEADS, FOX_HD = 4, 64
GLA_HEADS, GLA_DK, GLA_DV, GLA_RANK, GLA_TAU = 4, 64, 128, 16, 16.0
MLA_HEADS, MLA_Q_RANK, MLA_KV_RANK, MLA_NOPE, MLA_ROPE, MLA_VD = 4, 256, 128, 64, 32, 64
ROPE_BASE = 10000.0
XA_HEADS, XA_HD = 4, 128
ADAM_LR, ADAM_B1, ADAM_B2, ADAM_EPS, ADAM_WD, ADAM_STEP = 0.001, 0.9, 0.999, 1e-08, 0.01, 10

N_DEV = 8
V7X_VMEM_LIMIT = 56 * 1024 * 1024
NEG = -1e30

NN = ((1,), (0,))
NT = ((1,), (1,))
TN = ((0,), (0,))


def _dot(a, b, dims):
    return lax.dot_general(a.astype(BF16), b.astype(BF16), (dims, ((), ())), preferred_element_type=F32)


@jax.custom_vjp
def bdot(a, b):
    return _dot(a, b, NN)


bdot.defvjp(lambda a, b: (_dot(a, b, NN), (a, b)),
            lambda res, g: (_dot(g, res[1], NT), _dot(res[0], g, TN)))


@jax.custom_vjp
def bdot_nt(a, b):
    return _dot(a, b, NT)


bdot_nt.defvjp(lambda a, b: (_dot(a, b, NT), (a, b)),
               lambda res, g: (_dot(g, res[1], NN), _dot(g, res[0], TN)))


@jax.custom_vjp
def bdot_tn(a, b):
    return _dot(a, b, TN)


bdot_tn.defvjp(lambda a, b: (_dot(a, b, TN), (a, b)),
               lambda res, g: (_dot(res[1], g, NT), _dot(res[0], g, NN)))


def _split2(x):
    hi = x.astype(BF16)
    lo = (x - hi.astype(F32)).astype(BF16)
    return hi, lo


def _tri(n, lower):
    r = lax.broadcasted_iota(jnp.int32, (n, n), 0)
    c = lax.broadcasted_iota(jnp.int32, (n, n), 1)
    return jnp.where((r >= c) if lower else (r <= c), 1.0, 0.0).astype(BF16)


def _tri_dot2(x, lower):
    hi, lo = _split2(x)
    t = _tri(x.shape[0], lower)
    return _dot(t, hi, NN) + _dot(t, lo, NN)


@jax.custom_vjp
def chunk_cumsum(x):
    return _tri_dot2(x, True)


chunk_cumsum.defvjp(lambda x: (_tri_dot2(x, True), None), lambda _, g: (_tri_dot2(g, False),))


def _log_sigmoid(x):
    return jnp.minimum(x, 0.0) - jnp.log(1.0 + jnp.exp(-jnp.abs(x)))


def _sigmoid(x):
    return 1.0 / (1.0 + jnp.exp(-x))


def _rms(x, g):
    return x * lax.rsqrt(jnp.mean(x * x, axis=-1, keepdims=True) + EPS) * g


def _pick(dim, prefs):
    for p in prefs:
        if dim % p == 0:
            return p
    return dim


def _params(sem):
    return pltpu.CompilerParams(dimension_semantics=sem, vmem_limit_bytes=V7X_VMEM_LIMIT)


def _mm(a, b, *, mode, out_dtype, name, act=None, residual=None, drelu_of=None, tm=None, tn=None, tk=None):
    if mode == 'nn':
        (M, K), N = a.shape, b.shape[1]
    elif mode == 'nt':
        (M, K), N = a.shape, b.shape[0]
    else:
        (K, M), N = a.shape, b.shape[1]
    tm = tm or _pick(M, (1024, 512, 256, 128))
    tn = tn or _pick(N, (1024, 1920, 1152, 768, 640, 512, 384, 256, 128))
    tk = tk or _pick(K, (1024, 1920, 1152, 640, 512, 256, 128))
    nk = K // tk
    dims = {'nn': NN, 'nt': NT, 'tn': TN}[mode]
    a_spec = pl.BlockSpec((tk, tm), lambda i, j, k: (k, i)) if mode == 'tn' else pl.BlockSpec((tm, tk), lambda i, j, k: (i, k))
    b_spec = pl.BlockSpec((tn, tk), lambda i, j, k: (j, k)) if mode == 'nt' else pl.BlockSpec((tk, tn), lambda i, j, k: (k, j))
    o_spec = pl.BlockSpec((tm, tn), lambda i, j, k: (i, j))
    extra = [e for e in (residual, drelu_of) if e is not None]

    def body(a_ref, b_ref, *rest):
        o_ref = rest[len(extra)]
        at = a_ref[...]
        if act == 'relu2':
            at = jnp.square(jnp.maximum(at.astype(F32), 0.0))
        part = _dot(at, b_ref[...], dims)

        def finish(acc):
            idx = 0
            if residual is not None:
                acc = acc + rest[idx][...]
                idx += 1
            if drelu_of is not None:
                acc = acc * (2.0 * jnp.maximum(rest[idx][...].astype(F32), 0.0))
            o_ref[...] = acc.astype(out_dtype)

        if nk == 1:
            finish(part)
        else:
            acc_ref = rest[len(extra) + 1]
            k = pl.program_id(2)

            @pl.when(k == 0)
            def _():
                acc_ref[...] = part

            @pl.when(k > 0)
            def _():
                acc_ref[...] += part

            @pl.when(k == nk - 1)
            def _():
                finish(acc_ref[...])

    return pl.pallas_call(
        body, name=name,
        out_shape=jax.ShapeDtypeStruct((M, N), out_dtype),
        grid=(M // tm, N // tn, nk),
        in_specs=[a_spec, b_spec] + [o_spec] * len(extra),
        out_specs=o_spec,
        scratch_shapes=[] if nk == 1 else [pltpu.VMEM((tm, tn), F32)],
        compiler_params=_params(("parallel", "parallel", "arbitrary")),
    )(a, b, *extra)


def _rms_fwd(x, g, *, name, out_dtype=BF16):
    S, D = x.shape
    tr = _pick(S, (512, 256, 128))

    def body(x_ref, g_ref, o_ref):
        o_ref[...] = _rms(x_ref[...], g_ref[...]).astype(out_dtype)

    return pl.pallas_call(
        body, name=name, out_shape=jax.ShapeDtypeStruct((S, D), out_dtype), grid=(S // tr,),
        in_specs=[pl.BlockSpec((tr, D), lambda i: (i, 0)), pl.BlockSpec((1, D), lambda i: (0, 0))],
        out_specs=pl.BlockSpec((tr, D), lambda i: (i, 0)),
        compiler_params=_params(("parallel",)),
    )(x, g.reshape(1, D))


def _rms_bwd(x, g, dy, dres, *, name):
    S, D = x.shape
    tr = _pick(S, (512, 256, 128))

    def body(x_ref, g_ref, dy_ref, *rest):
        dx_ref, dg_ref = rest[-2], rest[-1]
        x_ = x_ref[...]
        rstd = lax.rsqrt(jnp.mean(x_ * x_, axis=-1, keepdims=True) + EPS)
        xh = x_ * rstd
        dy_ = dy_ref[...].astype(F32)
        gdy = dy_ * g_ref[...]
        dx = (gdy - xh * jnp.mean(gdy * xh, axis=-1, keepdims=True)) * rstd
        if dres is not None:
            dx = dx + rest[0][...]
        dx_ref[...] = dx
        part = jnp.sum(dy_ * xh, axis=0, keepdims=True)

        @pl.when(pl.program_id(0) == 0)
        def _():
            dg_ref[...] = part

        @pl.when(pl.program_id(0) > 0)
        def _():
            dg_ref[...] += part

    row = pl.BlockSpec((tr, D), lambda i: (i, 0))
    vec = pl.BlockSpec((1, D), lambda i: (0, 0))
    ins = [x, g.reshape(1, D), dy] + ([dres] if dres is not None else [])
    return pl.pallas_call(
        body, name=name,
        out_shape=(jax.ShapeDtypeStruct((S, D), F32), jax.ShapeDtypeStruct((1, D), F32)),
        grid=(S // tr,),
        in_specs=[row, vec, row] + ([row] if dres is not None else []),
        out_specs=(row, vec),
        compiler_params=_params(("arbitrary",)),
    )(*ins)


def _loss_head(x, g, target, *, name):
    S, D = x.shape
    tr = _pick(S, (512, 256, 128))

    def body(x_ref, g_ref, t_ref, l_ref, dx_ref, dg_ref):
        x_ = x_ref[...]
        g_ = g_ref[...]
        rstd = lax.rsqrt(jnp.mean(x_ * x_, axis=-1, keepdims=True) + EPS)
        xh = x_ * rstd
        err = xh * g_ - t_ref[...]
        lpart = (0.5 / D) * jnp.sum(jnp.sum(err * err, axis=-1, keepdims=True), axis=0, keepdims=True)
        dy = err * (1.0 / D)
        gdy = dy * g_
        dx_ref[...] = (gdy - xh * jnp.mean(gdy * xh, axis=-1, keepdims=True)) * rstd
        gpart = jnp.sum(dy * xh, axis=0, keepdims=True)

        @pl.when(pl.program_id(0) == 0)
        def _():
            dg_ref[...] = gpart
            l_ref[...] = lpart

        @pl.when(pl.program_id(0) > 0)
        def _():
            dg_ref[...] += gpart
            l_ref[...] += lpart

    row = pl.BlockSpec((tr, D), lambda i: (i, 0))
    vec = pl.BlockSpec((1, D), lambda i: (0, 0))
    return pl.pallas_call(
        body, name=name,
        out_shape=(jax.ShapeDtypeStruct((1, 1), F32), jax.ShapeDtypeStruct((S, D), F32), jax.ShapeDtypeStruct((1, D), F32)),
        grid=(S // tr,),
        in_specs=[row, vec, row],
        out_specs=(pl.BlockSpec((1, 1), lambda i: (0, 0)), row, vec),
        compiler_params=_params(("arbitrary",)),
    )(x, g.reshape(1, D), target)


def _mask_of(mask, tq, tk):
    qpos = lax.broadcasted_iota(jnp.int32, (tq, tk), 0)
    kpos = lax.broadcasted_iota(jnp.int32, (tq, tk), 1)
    if mask == 'causal':
        return kpos <= qpos
    return kpos <= (qpos | (CHUNK - 1))


def _flash_fwd(q, k, v, cq, ck, *, scale, mask, name):
    H, Sq, dk = q.shape
    Sk, dv = k.shape[1], v.shape[2]
    tq = _pick(Sq, (512, 256, 128))
    tk = tq if mask else _pick(Sk, (512, 256, 128))
    nq, nk = Sq // tq, Sk // tk
    bias = cq is not None

    def body(*refs):
        q_ref, k_ref, v_ref = refs[:3]
        cq_ref, ck_ref = (refs[3], refs[4]) if bias else (None, None)
        o_ref, lse_ref, m_s, l_s, acc_s = refs[-5:]
        qi, ki = pl.program_id(0), pl.program_id(1)

        @pl.when(ki == 0)
        def _():
            m_s[...] = jnp.full(m_s.shape, NEG, F32)
            l_s[...] = jnp.zeros(l_s.shape, F32)
            acc_s[...] = jnp.zeros(acc_s.shape, F32)

        def compute(masked):
            keep = _mask_of(mask, tq, tk) if masked else None
            for h in range(H):
                s = _dot(q_ref[h], k_ref[h], NT) * scale
                if bias:
                    s = s + (cq_ref[:, h:h + 1] - ck_ref[h:h + 1, :])
                if masked:
                    s = jnp.where(keep, s, NEG)
                m_prev = m_s[h]
                m_new = jnp.maximum(m_prev, jnp.max(s, axis=1, keepdims=True))
                alpha = jnp.exp(m_prev - m_new)
                p = jnp.exp(s - m_new)
                l_s[h] = alpha * l_s[h] + jnp.sum(p, axis=1, keepdims=True)
                acc_s[h] = alpha * acc_s[h] + _dot(p, v_ref[h], NN)
                m_s[h] = m_new

        if mask is None:
            compute(False)
        else:
            pl.when(ki < qi)(lambda: compute(False))
            pl.when(ki == qi)(lambda: compute(True))

        @pl.when(ki == ((nk - 1) if mask is None else qi))
        def _():
            lse_ref[...] = jnp.zeros(lse_ref.shape, F32)
            for h in range(H):
                o_ref[h] = (acc_s[h] / l_s[h]).astype(BF16)
                lse_ref[:, h:h + 1] = m_s[h] + jnp.log(l_s[h])

    kv_idx = (lambda i, j: (0, jnp.minimum(i, j), 0)) if mask else (lambda i, j: (0, j, 0))
    ck_idx = (lambda i, j: (0, jnp.minimum(i, j))) if mask else (lambda i, j: (0, j))
    in_specs = [pl.BlockSpec((H, tq, dk), lambda i, j: (0, i, 0)),
                pl.BlockSpec((H, tk, dk), kv_idx), pl.BlockSpec((H, tk, dv), kv_idx)]
    ins = [q, k, v]
    if bias:
        in_specs += [pl.BlockSpec((tq, 8), lambda i, j: (i, 0)), pl.BlockSpec((8, tk), ck_idx)]
        ins += [cq, ck]
    return pl.pallas_call(
        body, name=name,
        out_shape=(jax.ShapeDtypeStruct((H, Sq, dv), BF16), jax.ShapeDtypeStruct((Sq, 8), F32)),
        grid=(nq, nk), in_specs=in_specs,
        out_specs=(pl.BlockSpec((H, tq, dv), lambda i, j: (0, i, 0)), pl.BlockSpec((tq, 8), lambda i, j: (i, 0))),
        scratch_shapes=[pltpu.VMEM((H, tq, 1), F32), pltpu.VMEM((H, tq, 1), F32), pltpu.VMEM((H, tq, dv), F32)],
        compiler_params=_params(("parallel", "arbitrary")),
    )(*ins)


def _flash_bwd(q, k, v, o, do, lse, cq, ck, *, scale, mask, name):
    H, Sq, dk = q.shape
    Sk, dv = k.shape[1], v.shape[2]
    tq = _pick(Sq, (512, 256, 128))
    tk = tq if mask else _pick(Sk, (512, 256, 128))
    nq, nk = Sq // tq, Sk // tk
    bias = cq is not None

    def body(*refs):
        q_ref, k_ref, v_ref, o_ref, do_ref, lse_ref = refs[:6]
        n_in = 8 if bias else 6
        cq_ref, ck_ref = (refs[6], refs[7]) if bias else (None, None)
        outs = refs[n_in:]
        dq_ref, dk_ref, dv_ref = outs[:3]
        dck_ref, dcq_ref = (outs[3], outs[4]) if bias else (None, None)
        dk_s, dv_s = refs[-2], refs[-1]
        ki, qi = pl.program_id(0), pl.program_id(1)
        first_q = ki if mask else 0

        @pl.when((ki == 0) & (qi == 0))
        def _():
            dq_ref[...] = jnp.zeros(dq_ref.shape, F32)
            if bias:
                dcq_ref[...] = jnp.zeros(dcq_ref.shape, F32)

        @pl.when(qi == first_q)
        def _():
            dk_s[...] = jnp.zeros(dk_s.shape, F32)
            dv_s[...] = jnp.zeros(dv_s.shape, F32)
            if bias:
                dck_ref[...] = jnp.zeros(dck_ref.shape, F32)

        def compute(masked):
            keep = _mask_of(mask, tq, tk) if masked else None
            rows = pl.ds(pl.multiple_of(qi * tq, tq), tq)
            for h in range(H):
                qh, kh, vh, doh = q_ref[h], k_ref[h], v_ref[h], do_ref[h]
                s = _dot(qh, kh, NT) * scale
                if bias:
                    s = s + (cq_ref[:, h:h + 1] - ck_ref[h:h + 1, :])
                if masked:
                    s = jnp.where(keep, s, NEG)
                p = jnp.exp(s - lse_ref[:, h:h + 1])
                dp = _dot(doh, vh, NT)
                delta = jnp.sum(doh.astype(F32) * o_ref[h].astype(F32), axis=1, keepdims=True)
                ds = p * (dp - delta)
                dv_s[h] += _dot(p, doh, TN)
                dk_s[h] += _dot(ds, qh, TN)
                dq_ref[h, rows, :] += _dot(ds, kh, NN) * scale
                if bias:
                    dck_ref[h:h + 1, :] -= jnp.sum(ds, axis=0, keepdims=True)
                    dcq_ref[rows, h:h + 1] += jnp.sum(ds, axis=1, keepdims=True)

        if mask is None:
            compute(False)
        else:
            pl.when(qi > ki)(lambda: compute(False))
            pl.when(qi == ki)(lambda: compute(True))

        @pl.when(qi == nq - 1)
        def _():
            dk_ref[...] = dk_s[...] * scale
            dv_ref[...] = dv_s[...]

    q_idx = (lambda j, i: (0, jnp.maximum(i, j), 0)) if mask else (lambda j, i: (0, i, 0))
    c_idx = (lambda j, i: (jnp.maximum(i, j), 0)) if mask else (lambda j, i: (i, 0))
    kv_idx = lambda j, i: (0, j, 0)
    in_specs = [pl.BlockSpec((H, tq, dk), q_idx), pl.BlockSpec((H, tk, dk), kv_idx), pl.BlockSpec((H, tk, dv), kv_idx),
                pl.BlockSpec((H, tq, dv), q_idx), pl.BlockSpec((H, tq, dv), q_idx), pl.BlockSpec((tq, 8), c_idx)]
    ins = [q, k, v, o, do, lse]
    out_shape = [jax.ShapeDtypeStruct((H, Sq, dk), F32), jax.ShapeDtypeStruct((H, Sk, dk), F32),
                 jax.ShapeDtypeStruct((H, Sk, dv), F32)]
    out_specs = [pl.BlockSpec((H, Sq, dk), lambda j, i: (0, 0, 0)), pl.BlockSpec((H, tk, dk), kv_idx),
                 pl.BlockSpec((H, tk, dv), kv_idx)]
    if bias:
        in_specs += [pl.BlockSpec((tq, 8), c_idx), pl.BlockSpec((8, tk), lambda j, i: (0, j))]
        ins += [cq, ck]
        out_shape += [jax.ShapeDtypeStruct((8, Sk), F32), jax.ShapeDtypeStruct((Sq, 8), F32)]
        out_specs += [pl.BlockSpec((8, tk), lambda j, i: (0, j)), pl.BlockSpec((Sq, 8), lambda j, i: (0, 0))]
    return pl.pallas_call(
        body, name=name, out_shape=tuple(out_shape), grid=(nk, nq), in_specs=in_specs, out_specs=tuple(out_specs),
        scratch_shapes=[pltpu.VMEM((H, tk, dk), F32), pltpu.VMEM((H, tk, dv), F32)],
        compiler_params=_params(("arbitrary", "arbitrary")),
    )(*ins)


def _split3_dot(x, t):
    hi = x.astype(BF16)
    r1 = x - hi.astype(F32)
    mid = r1.astype(BF16)
    lo = (r1 - mid.astype(F32)).astype(BF16)
    return _dot(hi, t, NN) + _dot(mid, t, NN) + _dot(lo, t, NN)


def _fox_cum_fwd(ff_t, b, *, name):
    _, S = ff_t.shape
    tb = _pick(S, (512, 256, 128))

    def body(f_ref, b_ref, o_ref, carry):
        @pl.when(pl.program_id(0) == 0)
        def _():
            carry[...] = jnp.zeros(carry.shape, F32)

        lf = _log_sigmoid(f_ref[...] + b_ref[...])
        o_ref[...] = _split3_dot(lf, _tri(tb, False)) + carry[...]
        carry[...] += jnp.sum(lf, axis=1, keepdims=True)

    return pl.pallas_call(
        body, name=name, out_shape=jax.ShapeDtypeStruct((8, S), F32), grid=(S // tb,),
        in_specs=[pl.BlockSpec((8, tb), lambda i: (0, i)), pl.BlockSpec((8, 1), lambda i: (0, 0))],
        out_specs=pl.BlockSpec((8, tb), lambda i: (0, i)),
        scratch_shapes=[pltpu.VMEM((8, 1), F32)],
        compiler_params=_params(("arbitrary",)),
    )(ff_t, b)


def _fox_cum_bwd(ff_t, b, dcum_t, *, name):
    _, S = ff_t.shape
    tb = _pick(S, (512, 256, 128))
    nb = S // tb

    def body(f_ref, b_ref, dc_ref, df_ref, db_ref, carry):
        @pl.when(pl.program_id(0) == 0)
        def _():
            carry[...] = jnp.zeros(carry.shape, F32)
            db_ref[...] = jnp.zeros(db_ref.shape, F32)

        dc = dc_ref[...]
        dlf = _split3_dot(dc, _tri(tb, True)) + carry[...]
        carry[...] += jnp.sum(dc, axis=1, keepdims=True)
        df = dlf * _sigmoid(-(f_ref[...] + b_ref[...]))
        df_ref[...] = df
        db_ref[...] += jnp.sum(df, axis=1, keepdims=True)

    rev = lambda i: (0, nb - 1 - i)
    return pl.pallas_call(
        body, name=name,
        out_shape=(jax.ShapeDtypeStruct((8, S), F32), jax.ShapeDtypeStruct((8, 1), F32)), grid=(nb,),
        in_specs=[pl.BlockSpec((8, tb), rev), pl.BlockSpec((8, 1), lambda i: (0, 0)), pl.BlockSpec((8, tb), rev)],
        out_specs=(pl.BlockSpec((8, tb), rev), pl.BlockSpec((8, 1), lambda i: (0, 0))),
        scratch_shapes=[pltpu.VMEM((8, 1), F32)],
        compiler_params=_params(("arbitrary",)),
    )(ff_t, b, dcum_t)


GLA_W = GLA_HEADS * GLA_DK
GLA_BLOCK_CHUNKS = 4


def _gla_chunk(q, k, zsm, wg, bg, go, vs, rs, states):
    la = _log_sigmoid(bdot(zsm, wg) + bg) * (1.0 / GLA_TAU)
    cum = chunk_cumsum(la)
    end = jnp.sum(la, axis=0, keepdims=True)
    kd = k * jnp.exp(end - cum)
    a = jnp.exp(end)
    qs = q * (GLA_DK ** -0.5)
    lane = lax.broadcasted_iota(jnp.int32, (1, GLA_W), 1)
    outs, new_states = [], []
    for h in range(GLA_HEADS):
        head = jnp.where((lane >= h * GLA_DK) & (lane < (h + 1) * GLA_DK), 1.0, 0.0)
        st = states[h] * a + bdot_tn(vs[h], kd * head)
        o = bdot_nt(qs, st)
        o = _rms(o, go)
        outs.append(o * (rs[h] * _sigmoid(rs[h])))
        new_states.append(st)
    return outs, new_states


def _gla_fwd(z, zsm, wg, bg, go, cols, *, name):
    S = z.shape[0]
    rb = GLA_BLOCK_CHUNKS * CHUNK
    nb = S // rb
    cq, ckk, cv, cr = cols
    H = GLA_HEADS

    def body(q_ref, k_ref, zsm_ref, wg_ref, bg_ref, go_ref, *rest):
        v_refs, r_refs = rest[:H], rest[H:2 * H]
        o_ref, st_ref, state = rest[2 * H], rest[2 * H + 1], rest[2 * H + 2]

        @pl.when(pl.program_id(0) == 0)
        def _():
            state[...] = jnp.zeros(state.shape, F32)

        wg_, bg_, go_ = wg_ref[...], bg_ref[...], go_ref[...]
        for c in range(GLA_BLOCK_CHUNKS):
            rows = pl.ds(c * CHUNK, CHUNK)
            states = [state[h] for h in range(H)]
            for h in range(H):
                st_ref[c, h] = states[h]
            outs, new_states = _gla_chunk(
                q_ref[rows, :].astype(F32), k_ref[rows, :].astype(F32), zsm_ref[rows, :], wg_, bg_, go_,
                [v_refs[h][rows, :].astype(F32) for h in range(H)], [r_refs[h][rows, :].astype(F32) for h in range(H)], states)
            for h in range(H):
                o_ref[rows, h * GLA_DV:(h + 1) * GLA_DV] = outs[h].astype(BF16)
                state[h] = new_states[h]

    def col(width, off):
        return pl.BlockSpec((rb, width), lambda i, o=off // width: (i, o))

    full = lambda shp: pl.BlockSpec(shp, lambda i: (0,) * len(shp))
    in_specs = [col(GLA_W, cq), col(GLA_W, ckk), pl.BlockSpec((rb, 128), lambda i: (i, 0)),
                full((128, GLA_W)), full((1, GLA_W)), full((1, GLA_DV))]
    in_specs += [col(GLA_DV, cv + h * GLA_DV) for h in range(H)] + [col(GLA_DV, cr + h * GLA_DV) for h in range(H)]
    return pl.pallas_call(
        body, name=name,
        out_shape=(jax.ShapeDtypeStruct((S, H * GLA_DV), BF16), jax.ShapeDtypeStruct((S // CHUNK, H, GLA_DV, GLA_W), F32)),
        grid=(nb,), in_specs=in_specs,
        out_specs=(pl.BlockSpec((rb, H * GLA_DV), lambda i: (i, 0)),
                   pl.BlockSpec((GLA_BLOCK_CHUNKS, H, GLA_DV, GLA_W), lambda i: (i, 0, 0, 0))),
        scratch_shapes=[pltpu.VMEM((H, GLA_DV, GLA_W), F32)],
        compiler_params=_params(("arbitrary",)),
    )(z, z, zsm, wg, bg, go, *([z] * (2 * H)))


def _gla_bwd(z, zsm, wg, bg, go, states, do, cols, *, name):
    S = z.shape[0]
    rb = GLA_BLOCK_CHUNKS * CHUNK
    nb = S // rb
    cq, ckk, cv, cr = cols
    H = GLA_HEADS

    def body(q_ref, k_ref, zsm_ref, wg_ref, bg_ref, go_ref, st_ref, do_ref, *rest):
        v_refs, r_refs = rest[:H], rest[H:2 * H]
        dq_ref, dk_ref, dv_ref, dr_ref, dzsm_ref, dwg_ref, dbg_ref, dgo_ref, dstate = rest[2 * H:]

        @pl.when(pl.program_id(0) == 0)
        def _():
            dstate[...] = jnp.zeros(dstate.shape, F32)
            dwg_ref[...] = jnp.zeros(dwg_ref.shape, F32)
            dbg_ref[...] = jnp.zeros(dbg_ref.shape, F32)
            dgo_ref[...] = jnp.zeros(dgo_ref.shape, F32)

        wg_, bg_, go_ = wg_ref[...], bg_ref[...], go_ref[...]
        for c in reversed(range(GLA_BLOCK_CHUNKS)):
            rows = pl.ds(c * CHUNK, CHUNK)
            prim = (q_ref[rows, :].astype(F32), k_ref[rows, :].astype(F32), zsm_ref[rows, :], wg_, bg_, go_,
                    [v_refs[h][rows, :].astype(F32) for h in range(H)], [r_refs[h][rows, :].astype(F32) for h in range(H)],
                    [st_ref[c, h] for h in range(H)])
            _, vjp = jax.vjp(_gla_chunk, *prim)
            douts = [do_ref[rows, h * GLA_DV:(h + 1) * GLA_DV].astype(F32) for h in range(H)]
            dq, dk, dzs, dwg, dbg, dgo, dvs, drs, dsts = vjp((douts, [dstate[h] for h in range(H)]))
            dq_ref[rows, :] = dq.astype(BF16)
            dk_ref[rows, :] = dk.astype(BF16)
            dzsm_ref[rows, :] = dzs
            dwg_ref[...] += dwg
            dbg_ref[...] += dbg
            dgo_ref[...] += dgo
            for h in range(H):
                dv_ref[rows, h * GLA_DV:(h + 1) * GLA_DV] = dvs[h].astype(BF16)
                dr_ref[rows, h * GLA_DV:(h + 1) * GLA_DV] = drs[h].astype(BF16)
                dstate[h] = dsts[h]

    rev = lambda i: nb - 1 - i

    def col(width, off):
        return pl.BlockSpec((rb, width), lambda i, o=off // width: (rev(i), o))

    full = lambda shp: pl.BlockSpec(shp, lambda i: (0,) * len(shp))
    rowb = lambda w: pl.BlockSpec((rb, w), lambda i: (rev(i), 0))
    in_specs = [col(GLA_W, cq), col(GLA_W, ckk), rowb(128), full((128, GLA_W)), full((1, GLA_W)), full((1, GLA_DV)),
                pl.BlockSpec((GLA_BLOCK_CHUNKS, H, GLA_DV, GLA_W), lambda i: (rev(i), 0, 0, 0)), rowb(H * GLA_DV)]
    in_specs += [col(GLA_DV, cv + h * GLA_DV) for h in range(H)] + [col(GLA_DV, cr + h * GLA_DV) for h in range(H)]
    return pl.pallas_call(
        body, name=name,
        out_shape=(jax.ShapeDtypeStruct((S, GLA_W), BF16), jax.ShapeDtypeStruct((S, GLA_W), BF16),
                   jax.ShapeDtypeStruct((S, H * GLA_DV), BF16), jax.ShapeDtypeStruct((S, H * GLA_DV), BF16),
                   jax.ShapeDtypeStruct((S, 128), F32), jax.ShapeDtypeStruct((128, GLA_W), F32),
                   jax.ShapeDtypeStruct((1, GLA_W), F32), jax.ShapeDtypeStruct((1, GLA_DV), F32)),
        grid=(nb,), in_specs=in_specs,
        out_specs=(rowb(GLA_W), rowb(GLA_W), rowb(H * GLA_DV), rowb(H * GLA_DV), rowb(128),
                   full((128, GLA_W)), full((1, GLA_W)), full((1, GLA_DV))),
        scratch_shapes=[pltpu.VMEM((H, GLA_DV, GLA_W), F32)],
        compiler_params=_params(("arbitrary",)),
    )(z, z, zsm, wg, bg, go, states, do, *([z] * (2 * H)))


def _row_spec(entry, tr):
    if isinstance(entry, tuple):
        arr, width, off = entry
        return arr, pl.BlockSpec((tr, width), lambda i, o=off // width: (i, o))
    return entry, pl.BlockSpec((tr, entry.shape[1]), lambda i: (i, 0))


def _stage_fwd(fn, rows, consts, outs, *, name, tr=None):
    first = rows[0][0] if isinstance(rows[0], tuple) else rows[0]
    S = first.shape[0]
    tr = tr or _pick(S, (512, 256, 128))
    arrs, specs = zip(*[_row_spec(e, tr) for e in rows])
    nr, nc = len(rows), len(consts)

    def body(*refs):
        vals = [r[...].astype(F32) for r in refs[:nr + nc]]
        res = fn(*vals)
        for o_ref, val in zip(refs[nr + nc:], res):
            o_ref[...] = val.astype(o_ref.dtype)

    cspecs = [pl.BlockSpec(c.shape, lambda i, n=c.ndim: (0,) * n) for c in consts]
    return pl.pallas_call(
        body, name=name,
        out_shape=tuple(jax.ShapeDtypeStruct((S, w), dt) for w, dt in outs), grid=(S // tr,),
        in_specs=list(specs) + cspecs,
        out_specs=tuple(pl.BlockSpec((tr, w), lambda i: (i, 0)) for w, _ in outs),
        compiler_params=_params(("parallel",)),
    )(*arrs, *consts)


def _stage_bwd(fn, rows, consts, cts, n_diff, drow_dtypes, *, name, tr=None):
    first = rows[0][0] if isinstance(rows[0], tuple) else rows[0]
    S = first.shape[0]
    tr = tr or _pick(S, (512, 256, 128))
    arrs, specs = zip(*[_row_spec(e, tr) for e in rows])
    widths = [e[1] if isinstance(e, tuple) else e.shape[1] for e in rows]
    nr, nc, nt = len(rows), len(consts), len(cts)

    def body(*refs):
        vals = [r[...].astype(F32) for r in refs[:nr + nc]]
        ct = [r[...].astype(F32) for r in refs[nr + nc:nr + nc + nt]]
        drow_refs = refs[nr + nc + nt:nr + nc + nt + n_diff]
        dconst_refs = refs[nr + nc + nt + n_diff:]
        rest_rows = vals[n_diff:nr]

        def f(diff_rows, cs):
            return tuple(fn(*diff_rows, *rest_rows, *cs))

        _, vjp = jax.vjp(f, vals[:n_diff], vals[nr:])
        drows, dcs = vjp(tuple(ct))
        for r, val in zip(drow_refs, drows):
            r[...] = val.astype(r.dtype)
        first_step = pl.program_id(0) == 0
        for r, val in zip(dconst_refs, dcs):
            @pl.when(first_step)
            def _(r=r, val=val):
                r[...] = val

            @pl.when(jnp.logical_not(first_step))
            def _(r=r, val=val):
                r[...] += val

    cspecs = [pl.BlockSpec(c.shape, lambda i, n=c.ndim: (0,) * n) for c in consts]
    ctspecs = [pl.BlockSpec((tr, c.shape[1]), lambda i: (i, 0)) for c in cts]
    out_shape = [jax.ShapeDtypeStruct((S, widths[j]), drow_dtypes[j]) for j in range(n_diff)]
    out_shape += [jax.ShapeDtypeStruct(c.shape, F32) for c in consts]
    out_specs = [pl.BlockSpec((tr, widths[j]), lambda i: (i, 0)) for j in range(n_diff)] + cspecs
    res = pl.pallas_call(
        body, name=name, out_shape=tuple(out_shape), grid=(S // tr,),
        in_specs=list(specs) + cspecs + ctspecs, out_specs=tuple(out_specs),
        compiler_params=_params(("arbitrary",)),
    )(*arrs, *consts, *cts)
    return list(res[:n_diff]), list(res[n_diff:])


def _mla_prep_fn(cq, ckv, kr1, kr2, cos16, sin16, cos64, sin64, gq, gkv, wq_n, wq_1, wq_2, wk, wv):
    hq = _rms(cq, gq)
    hkv = _rms(ckv, gkv)
    q1, q2 = bdot(hq, wq_1), bdot(hq, wq_2)
    return (bdot(hq, wq_n), q1 * cos64 - q2 * sin64, q2 * cos64 + q1 * sin64,
            bdot(hkv, wk), bdot(hkv, wv), kr1 * cos16 - kr2 * sin16, kr2 * cos16 + kr1 * sin16)


def _merge_fn(g0, g1, g2, of, og, om, b0, b1, b2, wf, wg, wm):
    return (_sigmoid(g0 + b0) * bdot(of, wf) + _sigmoid(g1 + b1) * bdot(og, wg) + _sigmoid(g2 + b2) * bdot(om, wm),)


_IN_SIZES = (256, 256, 256, 4, 256, 256, 512, 16, 512, 256, 128, 32, 3072)
_IN_OFF = np.concatenate([[0], np.cumsum(_IN_SIZES)])
(_O_FQ, _O_FK, _O_FV, _O_FF, _O_GQ, _O_GK, _O_GV, _O_GLOW, _O_GR, _O_MQ, _O_MKV, _O_MKR, _O_ZG) = [int(o) for o in _IN_OFF[:-1]]
N_IN = int(_IN_OFF[-1])
_BIG_GROUPS = ((_O_ZG, 3072), (_O_GV, 512), (_O_GR, 512), (_O_FQ, 256), (_O_FK, 256), (_O_FV, 256),
               (_O_GQ, 256), (_O_GK, 256), (_O_MQ, 256), (_O_MKV, 128))
Z_GATE, Z_GV, Z_GR, Z_FQ, Z_FK, Z_FV, Z_GQ, Z_GK, Z_MQ, Z_MKV = [int(o) for o in
                                                                    np.concatenate([[0], np.cumsum([w for _, w in _BIG_GROUPS])])[:-1]]
N_BIG = sum(w for _, w in _BIG_GROUPS)
SM_FF, SM_GLOW, SM_KR, N_SM = 0, 8, 32, 128
N_PAD = N_BIG + N_SM


def _in_perm():
    idx = np.concatenate([np.arange(o, o + w) for o, w in _BIG_GROUPS] + [np.zeros(N_SM, np.int64)])
    valid = np.concatenate([np.ones(N_BIG, bool), np.zeros(N_SM, bool)])
    for src, dst, w in ((_O_FF, SM_FF, 4), (_O_GLOW, SM_GLOW, 16), (_O_MKR, SM_KR, 32)):
        idx[N_BIG + dst:N_BIG + dst + w] = np.arange(src, src + w)
        valid[N_BIG + dst:N_BIG + dst + w] = True
    inv = np.zeros(N_IN, np.int64)
    inv[idx[valid]] = np.nonzero(valid)[0]
    return idx, valid, inv


_IN_IDX, _IN_VALID, _IN_INV = _in_perm()

_HALF = MLA_ROPE // 2
_QK_HD = MLA_NOPE + MLA_ROPE
_UQ_PERM = np.concatenate(
    [np.concatenate([np.arange(h * _QK_HD, h * _QK_HD + MLA_NOPE) for h in range(MLA_HEADS)]),
     np.concatenate([np.arange(h * _QK_HD + MLA_NOPE, h * _QK_HD + MLA_NOPE + _HALF) for h in range(MLA_HEADS)]),
     np.concatenate([np.arange(h * _QK_HD + MLA_NOPE + _HALF, (h + 1) * _QK_HD) for h in range(MLA_HEADS)])])
_UKV_PERM = np.concatenate(
    [np.concatenate([np.arange(h * 128, h * 128 + MLA_NOPE) for h in range(MLA_HEADS)]),
     np.concatenate([np.arange(h * 128 + MLA_NOPE, (h + 1) * 128) for h in range(MLA_HEADS)])])
_UQ_INV = np.argsort(_UQ_PERM)
_UKV_INV = np.argsort(_UKV_PERM)


def _heads(a, n):
    s, w = a.shape
    return a.reshape(s, n, w // n).transpose(1, 0, 2)


def _unheads(a):
    n, s, d = a.shape
    return a.transpose(1, 0, 2).reshape(s, n * d)


def _rope_tables(S):
    inv = ROPE_BASE ** (-jnp.arange(_HALF, dtype=F32) / _HALF)
    ang = jnp.arange(S, dtype=F32)[:, None] * inv[None, :]
    cos, sin = jnp.cos(ang), jnp.sin(ang)
    return cos, sin, jnp.tile(cos, (1, MLA_HEADS)), jnp.tile(sin, (1, MLA_HEADS))


def _prep_layer(w, l):
    p = {}
    p['w_big'] = w['w_in'][l][:, :N_BIG]
    p['w_sm'] = w['w_in'][l][:, N_BIG:]
    p['wg'] = jnp.zeros((N_SM, GLA_W), BF16).at[SM_GLOW:SM_GLOW + GLA_RANK].set(w['w_gla_gate'][l])
    uq = w['w_mla_uq'][l][:, _UQ_PERM]
    p['wq_n'], p['wq_1'], p['wq_2'] = uq[:, :256], uq[:, 256:320], uq[:, 320:]
    ukv = w['w_mla_ukv'][l][:, _UKV_PERM]
    p['wk'], p['wv'] = ukv[:, :256], ukv[:, 256:]
    for n in ('w_up_fox', 'w_up_gla', 'w_up_mla', 'w_out', 'w_xq', 'w_xkv', 'w_xo', 'w_mlp1', 'w_mlp2',
              'g_mix', 'g_xa', 'g_mem', 'g_mlp'):
        p[n] = w[n][l]
    p['b_f'] = jnp.zeros((8, 1), F32).at[:FOX_HEADS, 0].set(w['b_fox_forget'][l])
    p['bg'] = w['b_gla_gate'][l].reshape(1, GLA_W)
    p['go'] = w['g_gla_out'][l].reshape(1, GLA_DV)
    p['gq'] = w['g_mla_q'][l].reshape(1, MLA_Q_RANK)
    p['gkv'] = w['g_mla_kv'][l].reshape(1, MLA_KV_RANK)
    p['b_gate'] = [w['b_branch_gate'][l][i * 1024:(i + 1) * 1024].reshape(1, 1024) for i in range(3)]
    return p


_GLA_COLS = (Z_GQ, Z_GK, Z_GV, Z_GR)
_MLA_OUTS = [(256, BF16), (64, BF16), (64, BF16), (256, BF16), (256, BF16), (_HALF, BF16), (_HALF, BF16)]


def _mla_rows(z, zsm, rope):
    kr = zsm[:, SM_KR:SM_KR + MLA_ROPE]
    return [(z, 256, Z_MQ), (z, 128, Z_MKV), kr[:, :_HALF], kr[:, _HALF:], rope[0], rope[1], rope[2], rope[3]]


def _mla_consts(p):
    return [p['gq'], p['gkv'], p['wq_n'], p['wq_1'], p['wq_2'], p['wk'], p['wv']]


def _merge_rows(z, o_fox, o_gla, o_mla):
    return [(z, 1024, Z_GATE), (z, 1024, Z_GATE + 1024), (z, 1024, Z_GATE + 2048), o_fox, o_gla, o_mla]


def _merge_consts(p):
    return p['b_gate'] + [p['w_up_fox'], p['w_up_gla'], p['w_up_mla']]


def _layer_fwd(x0, mem, p, rope, l):
    S = x0.shape[0]
    sv = {'x0': x0}
    h1 = _rms_fwd(x0, p['g_mix'], name=f"rms_mix_{l}")
    z = _mm(h1, p['w_big'], mode='nn', out_dtype=BF16, name=f"in_big_{l}")
    zsm = _mm(h1, p['w_sm'], mode='nn', out_dtype=F32, name=f"in_small_{l}")
    sv.update(h1=h1, z=z, zsm=zsm)
    ff_t = jnp.zeros((8, S), F32).at[:FOX_HEADS].set(zsm[:, SM_FF:SM_FF + FOX_HEADS].T)
    cum_t = _fox_cum_fwd(ff_t, p['b_f'], name=f"fox_cum_{l}")
    cum = cum_t.T
    fq, fk, fv = (_heads(z[:, o:o + 256], FOX_HEADS) for o in (Z_FQ, Z_FK, Z_FV))
    o_fox_h, lse_f = _flash_fwd(fq, fk, fv, cum, cum_t, scale=FOX_HD ** -0.5, mask='causal', name=f"fox_fwd_{l}")
    o_fox = _unheads(o_fox_h)
    sv.update(ff_t=ff_t, cum=cum, cum_t=cum_t, fq=fq, fk=fk, fv=fv, o_fox_h=o_fox_h, lse_f=lse_f, o_fox=o_fox)
    o_gla, states = _gla_fwd(z, zsm, p['wg'], p['bg'], p['go'], _GLA_COLS, name=f"gla_fwd_{l}")
    sv.update(o_gla=o_gla, states=states)
    qn, q1, q2, kn, vv, k1, k2 = _stage_fwd(_mla_prep_fn, _mla_rows(z, zsm, rope), _mla_consts(p), _MLA_OUTS,
                                            name=f"mla_prep_{l}")
    mq = jnp.concatenate([qn.reshape(S, MLA_HEADS, MLA_NOPE), q1.reshape(S, MLA_HEADS, _HALF),
                          q2.reshape(S, MLA_HEADS, _HALF)], axis=-1).transpose(1, 0, 2)
    mk = jnp.concatenate([kn.reshape(S, MLA_HEADS, MLA_NOPE),
                          jnp.broadcast_to(k1[:, None, :], (S, MLA_HEADS, _HALF)),
                          jnp.broadcast_to(k2[:, None, :], (S, MLA_HEADS, _HALF))], axis=-1).transpose(1, 0, 2)
    mv = _heads(vv, MLA_HEADS)
    o_mla_h, lse_m = _flash_fwd(mq, mk, mv, None, None, scale=_QK_HD ** -0.5, mask='chunk', name=f"mla_fwd_{l}")
    o_mla = _unheads(o_mla_h)
    sv.update(mq=mq, mk=mk, mv=mv, o_mla_h=o_mla_h, lse_m=lse_m, o_mla=o_mla)
    (y,) = _stage_fwd(_merge_fn, _merge_rows(z, o_fox, o_gla, o_mla), _merge_consts(p), [(1024, BF16)], name=f"merge_{l}")
    x1 = _mm(y, p['w_out'], mode='nn', out_dtype=F32, residual=x0, name=f"out_proj_{l}")
    sv.update(y=y, x1=x1)
    h2 = _rms_fwd(x1, p['g_xa'], name=f"rms_xa_{l}")
    hm = _rms_fwd(mem, p['g_mem'], name=f"rms_mem_{l}")
    qx = _heads(_mm(h2, p['w_xq'], mode='nn', out_dtype=BF16, name=f"xq_{l}"), XA_HEADS)
    kvx = _mm(hm, p['w_xkv'], mode='nn', out_dtype=BF16, name=f"xkv_{l}")
    kx, vx = _heads(kvx[:, :512], XA_HEADS), _heads(kvx[:, 512:], XA_HEADS)
    ox_h, lse_x = _flash_fwd(qx, kx, vx, None, None, scale=XA_HD ** -0.5, mask=None, name=f"xa_fwd_{l}")
    ox = _unheads(ox_h)
    x2 = _mm(ox, p['w_xo'], mode='nn', out_dtype=F32, residual=x1, name=f"xo_{l}")
    sv.update(h2=h2, hm=hm, qx=qx, kx=kx, vx=vx, ox_h=ox_h, lse_x=lse_x, ox=ox, x2=x2)
    h3 = _rms_fwd(x2, p['g_mlp'], name=f"rms_mlp_{l}")
    a = _mm(h3, p['w_mlp1'], mode='nn', out_dtype=BF16, name=f"mlp1_{l}")
    x3 = _mm(a, p['w_mlp2'], mode='nn', out_dtype=F32, act='relu2', residual=x2, name=f"mlp2_{l}")
    sv.update(h3=h3, a=a)
    return x3, sv


def _layer_bwd(dx3, mem, p, rope, sv, l):
    S = dx3.shape[0]
    g = {}
    dx3b = dx3.astype(BF16)
    da = _mm(dx3b, p['w_mlp2'], mode='nt', out_dtype=BF16, drelu_of=sv['a'], name=f"d_mlp2_in_{l}")
    g['w_mlp2'] = _mm(sv['a'], dx3b, mode='tn', out_dtype=F32, act='relu2', name=f"d_w_mlp2_{l}")
    dh3 = _mm(da, p['w_mlp1'], mode='nt', out_dtype=F32, name=f"d_mlp1_in_{l}")
    g['w_mlp1'] = _mm(sv['h3'], da, mode='tn', out_dtype=F32, name=f"d_w_mlp1_{l}")
    dx2, g['g_mlp'] = _rms_bwd(sv['x2'], p['g_mlp'], dh3, dx3, name=f"d_rms_mlp_{l}")
    dx2b = dx2.astype(BF16)
    dox = _mm(dx2b, p['w_xo'], mode='nt', out_dtype=BF16, name=f"d_xo_in_{l}")
    g['w_xo'] = _mm(sv['ox'], dx2b, mode='tn', out_dtype=F32, name=f"d_w_xo_{l}")
    dqx, dkx, dvx = _flash_bwd(sv['qx'], sv['kx'], sv['vx'], sv['ox_h'], _heads(dox, XA_HEADS), sv['lse_x'], None, None,
                               scale=XA_HD ** -0.5, mask=None, name=f"xa_bwd_{l}")
    dqx = _unheads(dqx).astype(BF16)
    dkvx = jnp.concatenate([_unheads(dkx), _unheads(dvx)], axis=1).astype(BF16)
    dh2 = _mm(dqx, p['w_xq'], mode='nt', out_dtype=F32, name=f"d_xq_in_{l}")
    g['w_xq'] = _mm(sv['h2'], dqx, mode='tn', out_dtype=F32, name=f"d_w_xq_{l}")
    dhm = _mm(dkvx, p['w_xkv'], mode='nt', out_dtype=F32, name=f"d_xkv_in_{l}")
    g['w_xkv'] = _mm(sv['hm'], dkvx, mode='tn', out_dtype=F32, name=f"d_w_xkv_{l}")
    _, g['g_mem'] = _rms_bwd(mem, p['g_mem'], dhm, None, name=f"d_rms_mem_{l}")
    dx1, g['g_xa'] = _rms_bwd(sv['x1'], p['g_xa'], dh2, dx2, name=f"d_rms_xa_{l}")
    dx1b = dx1.astype(BF16)
    dy = _mm(dx1b, p['w_out'], mode='nt', out_dtype=F32, name=f"d_out_in_{l}")
    g['w_out'] = _mm(sv['y'], dx1b, mode='tn', out_dtype=F32, name=f"d_w_out_{l}")
    z, zsm = sv['z'], sv['zsm']
    (dg0, dg1, dg2, do_fox, do_gla, do_mla), (db0, db1, db2, g['w_up_fox'], g['w_up_gla'], g['w_up_mla']) = _stage_bwd(
        _merge_fn, _merge_rows(z, sv['o_fox'], sv['o_gla'], sv['o_mla']), _merge_consts(p), [dy], 6, [BF16] * 6,
        name=f"merge_bwd_{l}")
    g['b_branch_gate'] = jnp.concatenate([db0, db1, db2], axis=1).reshape(-1)
    dfq, dfk, dfv, dck, dcq = _flash_bwd(sv['fq'], sv['fk'], sv['fv'], sv['o_fox_h'], _heads(do_fox, FOX_HEADS), sv['lse_f'],
                                         sv['cum'], sv['cum_t'], scale=FOX_HD ** -0.5, mask='causal', name=f"fox_bwd_{l}")
    dff_t, db_f = _fox_cum_bwd(sv['ff_t'], p['b_f'], dck + dcq.T, name=f"fox_cum_bwd_{l}")
    g['b_fox_forget'] = db_f[:FOX_HEADS, 0]
    dgq, dgk, dgv, dgr, dzsm, dwg, dbg, dgo = _gla_bwd(z, zsm, p['wg'], p['bg'], p['go'], sv['states'], do_gla, _GLA_COLS,
                                                       name=f"gla_bwd_{l}")
    g['w_gla_gate'] = dwg[SM_GLOW:SM_GLOW + GLA_RANK]
    g['b_gla_gate'] = dbg.reshape(-1)
    g['g_gla_out'] = dgo.reshape(-1)
    dmq, dmk, dmv = _flash_bwd(sv['mq'], sv['mk'], sv['mv'], sv['o_mla_h'], _heads(do_mla, MLA_HEADS), sv['lse_m'], None, None,
                               scale=_QK_HD ** -0.5, mask='chunk', name=f"mla_bwd_{l}")
    dmq_r = dmq.transpose(1, 0, 2)
    cts = [dmq_r[:, :, :MLA_NOPE].reshape(S, 256), dmq_r[:, :, MLA_NOPE:MLA_NOPE + _HALF].reshape(S, 64),
           dmq_r[:, :, MLA_NOPE + _HALF:].reshape(S, 64), _unheads(dmk[:, :, :MLA_NOPE]), _unheads(dmv),
           jnp.sum(dmk[:, :, MLA_NOPE:MLA_NOPE + _HALF], axis=0), jnp.sum(dmk[:, :, MLA_NOPE + _HALF:], axis=0)]
    (dcq, dckv, dkr1, dkr2), (dgq_n, dgkv_n, dwq_n, dwq_1, dwq_2, dwk, dwv) = _stage_bwd(
        _mla_prep_fn, _mla_rows(z, zsm, rope), _mla_consts(p), cts, 4, [BF16, BF16, F32, F32], name=f"mla_prep_bwd_{l}")
    g['g_mla_q'] = dgq_n.reshape(-1)
    g['g_mla_kv'] = dgkv_n.reshape(-1)
    g['w_mla_uq'] = jnp.concatenate([dwq_n, dwq_1, dwq_2], axis=1)[:, _UQ_INV]
    g['w_mla_ukv'] = jnp.concatenate([dwk, dwv], axis=1)[:, _UKV_INV]
    dz = jnp.concatenate([dg0, dg1, dg2, dgv, dgr, _unheads(dfq).astype(BF16), _unheads(dfk).astype(BF16),
                          _unheads(dfv).astype(BF16), dgq, dgk, dcq, dckv], axis=1)
    dzsm = dzsm + jnp.concatenate([dff_t[:FOX_HEADS].T, jnp.zeros((S, SM_KR - FOX_HEADS), F32), dkr1, dkr2,
                                   jnp.zeros((S, N_SM - SM_KR - MLA_ROPE), F32)], axis=1)
    dzsm = dzsm.astype(BF16)
    dh1 = _mm(dz, p['w_big'], mode='nt', out_dtype=F32, name=f"d_in_big_{l}")
    dh1 = _mm(dzsm, p['w_sm'], mode='nt', out_dtype=F32, residual=dh1, name=f"d_in_small_{l}")
    g['w_in'] = jnp.concatenate([_mm(sv['h1'], dz, mode='tn', out_dtype=F32, name=f"d_w_big_{l}"),
                                 _mm(sv['h1'], dzsm, mode='tn', out_dtype=F32, name=f"d_w_small_{l}")], axis=1)
    dx0, g['g_mix'] = _rms_bwd(sv['x0'], p['g_mix'], dh1, dx1, name=f"d_rms_mix_{l}")
    for n in ('g_mlp', 'g_mem', 'g_xa', 'g_mix'):
        g[n] = g[n].reshape(-1)
    return dx0, g


def _local_step(x, mem, target, w):
    S = x.shape[0]
    depth = w['g_mix'].shape[0]
    rope = _rope_tables(S)
    ps = [_prep_layer(w, l) for l in range(depth)]
    saved = []
    for l in range(depth):
        x, sv = _layer_fwd(x, mem, ps[l], rope, l)
        saved.append(sv)
    loss, dx, dgf = _loss_head(x, w['g_final'], target, name="loss_head")
    grads = [None] * depth
    for l in reversed(range(depth)):
        dx, grads[l] = _layer_bwd(dx, mem, ps[l], rope, saved[l], l)
    return loss, dx, grads, dgf.reshape(-1)


_MESH_AXES = ("x", "y", "c")
_HBM = pl.BlockSpec(memory_space=pl.ANY)


N_CHIP = 4


def _place():
    x, y, c = (lax.axis_index(n) for n in _MESH_AXES)
    return (x, y, c), (x, y, 1 - c), [(1 - x, y), (x, 1 - y), (1 - x, 1 - y)]


def _remote(src, dst, sems, k, to):
    return pltpu.make_async_remote_copy(src_ref=src, dst_ref=dst, send_sem=sems[0].at[k], recv_sem=sems[1].at[k],
                                        device_id=to, device_id_type=pl.DeviceIdType.MESH)


def _all_gather(x, *, name):
    def body(x_ref, o_ref, send_sems, recv_sems, local_sem):
        me, sib, chips = _place()
        c = me[2]
        sems = (send_sems, recv_sems)
        slot = lambda px, py, pc: o_ref.at[4 * px + 2 * py + pc]
        mine = pltpu.make_async_copy(x_ref, slot(*me), local_sem)
        mine.start()
        first = [_remote(x_ref, slot(*me), sems, 0, sib)]
        first += [_remote(x_ref, slot(*me), sems, 1 + j, (*chip, c)) for j, chip in enumerate(chips)]
        for cp in first:
            cp.start()
        passed = [_remote(slot(*chip, c), slot(*chip, c), sems, 4 + j, sib) for j, chip in enumerate(chips)]
        for j, chip in enumerate(chips):
            _remote(x_ref, slot(*chip, c), sems, 1 + j, me).wait_recv()
            passed[j].start()
        _remote(x_ref, slot(*sib), sems, 0, me).wait_recv()
        for j, chip in enumerate(chips):
            _remote(x_ref, slot(*chip, 1 - c), sems, 4 + j, me).wait_recv()
        for cp in first + passed:
            cp.wait_send()
        mine.wait()

    return pl.pallas_call(
        body, name=name, out_shape=jax.ShapeDtypeStruct((N_DEV,) + x.shape, x.dtype),
        in_specs=[_HBM], out_specs=_HBM,
        scratch_shapes=[pltpu.SemaphoreType.DMA((N_DEV - 1,)), pltpu.SemaphoreType.DMA((N_DEV - 1,)), pltpu.SemaphoreType.DMA],
        compiler_params=pltpu.CompilerParams(has_side_effects=True),
    )(x)


def _sibling_swap(x, *, name):
    def body(x_ref, o_ref, send_sems, recv_sems):
        me, sib, _ = _place()
        c = me[2]
        sems = (send_sems, recv_sems)
        sends = [_remote(x_ref.at[j, 1 - c], o_ref.at[j], sems, j, sib) for j in range(N_CHIP)]
        for cp in sends:
            cp.start()
        for cp in sends:
            cp.wait_send()
            cp.wait_recv()

    return pl.pallas_call(
        body, name=name, out_shape=jax.ShapeDtypeStruct((N_CHIP,) + x.shape[2:], x.dtype),
        in_specs=[_HBM], out_specs=_HBM,
        scratch_shapes=[pltpu.SemaphoreType.DMA((N_CHIP,)), pltpu.SemaphoreType.DMA((N_CHIP,))],
        compiler_params=pltpu.CompilerParams(has_side_effects=True),
    )(x)


def _pair_sum(x, got, c, *, name):
    _, _, R, _ = x.shape
    tr = _pick(R, (1024, 512, 256, 128, 64, 32, 16, 8))

    def body(c_ref, x_ref, g_ref, o_ref):
        o_ref[...] = (x_ref[...].astype(F32) + g_ref[...].astype(F32)).astype(o_ref.dtype)

    return pl.pallas_call(
        body, name=name, out_shape=jax.ShapeDtypeStruct((N_CHIP, R, 128), x.dtype),
        grid_spec=pltpu.PrefetchScalarGridSpec(
            num_scalar_prefetch=1, grid=(N_CHIP, R // tr),
            in_specs=[pl.BlockSpec((None, None, tr, 128), lambda j, i, c_ref: (j, c_ref[0], i, 0)),
                      pl.BlockSpec((None, tr, 128), lambda j, i, c_ref: (j, i, 0))],
            out_specs=pl.BlockSpec((None, tr, 128), lambda j, i, c_ref: (j, i, 0))),
        compiler_params=_params(("parallel", "parallel")),
    )(c, x, got)


def _chip_all_to_all(x, *, name):
    def body(x_ref, o_ref, send_sems, recv_sems, local_sem):
        me, _, chips = _place()
        c = me[2]
        sems = (send_sems, recv_sems)
        mine = 2 * me[0] + me[1]
        local = pltpu.make_async_copy(x_ref.at[mine], o_ref.at[mine], local_sem)
        local.start()
        sends = [_remote(x_ref.at[2 * px + py], o_ref.at[mine], sems, j, (px, py, c)) for j, (px, py) in enumerate(chips)]
        for cp in sends:
            cp.start()
        for j, (px, py) in enumerate(chips):
            sends[j].wait_send()
            _remote(x_ref.at[mine], o_ref.at[2 * px + py], sems, j, me).wait_recv()
        local.wait()

    return pl.pallas_call(
        body, name=name, out_shape=jax.ShapeDtypeStruct(x.shape, x.dtype),
        in_specs=[_HBM], out_specs=_HBM,
        scratch_shapes=[pltpu.SemaphoreType.DMA((N_CHIP - 1,)), pltpu.SemaphoreType.DMA((N_CHIP - 1,)), pltpu.SemaphoreType.DMA],
        compiler_params=pltpu.CompilerParams(has_side_effects=True),
    )(x)


def _sum_slots(x, *, name):
    n, R, _ = x.shape
    tr = _pick(R, (1024, 512, 256, 128, 64, 32, 16, 8))

    def body(x_ref, o_ref):
        acc = x_ref[0].astype(F32)
        for j in range(1, n):
            acc = acc + x_ref[j].astype(F32)
        o_ref[...] = acc

    return pl.pallas_call(
        body, name=name, out_shape=jax.ShapeDtypeStruct((R, 128), F32), grid=(R // tr,),
        in_specs=[pl.BlockSpec((n, tr, 128), lambda i: (0, i, 0))], out_specs=pl.BlockSpec((tr, 128), lambda i: (i, 0)),
        compiler_params=_params(("parallel",)),
    )(x)


def _adamw(w, g, m, v, *, name):
    shape = w.shape
    cols = shape[-1]
    rows = int(np.prod(shape[:-1]))
    tr = next((t for t in (1024, 512, 256, 128, 64, 32, 16, 8) if rows % t == 0 and t * cols * 4 <= (1 << 20)), rows)

    def body(w_ref, g_ref, m_ref, v_ref, d_ref, mo_ref, vo_ref):
        g_ = g_ref[...]
        m_ = ADAM_B1 * m_ref[...] + (1.0 - ADAM_B1) * g_
        v_ = ADAM_B2 * v_ref[...] + (1.0 - ADAM_B2) * jnp.square(g_)
        m_hat = m_ / (1.0 - ADAM_B1 ** ADAM_STEP)
        v_hat = v_ / (1.0 - ADAM_B2 ** ADAM_STEP)
        d_ref[...] = -ADAM_LR * (m_hat / (jnp.sqrt(v_hat) + ADAM_EPS) + ADAM_WD * w_ref[...])
        mo_ref[...] = m_
        vo_ref[...] = v_

    blk = pl.BlockSpec((tr, cols), lambda i: (i, 0))
    outs = pl.pallas_call(
        body, name=name, out_shape=tuple(jax.ShapeDtypeStruct((rows, cols), F32) for _ in range(3)), grid=(rows // tr,),
        in_specs=[blk] * 4, out_specs=(blk,) * 3, compiler_params=_params(("parallel",)),
    )(*(a.reshape(rows, cols) for a in (w, g, m, v)))
    return tuple(o.reshape(shape) for o in outs)


_WEIGHTS = ('g_mix', 'w_in', 'b_fox_forget', 'w_gla_gate', 'b_gla_gate', 'g_gla_out', 'g_mla_q', 'w_mla_uq', 'g_mla_kv',
            'w_mla_ukv', 'b_branch_gate', 'w_up_fox', 'w_up_gla', 'w_up_mla', 'w_out', 'g_xa', 'g_mem', 'w_xq', 'w_xkv',
            'w_xo', 'g_mlp', 'w_mlp1', 'w_mlp2', 'g_final')
_SHARDED = (('w_in', 1), ('w_gla_gate', 2), ('w_mla_uq', 2), ('w_mla_ukv', 2), ('w_up_fox', 2), ('w_up_gla', 2),
            ('w_up_mla', 2), ('w_out', 1), ('w_xq', 1), ('w_xkv', 1), ('w_xo', 2), ('w_mlp1', 2), ('w_mlp2', 1))
_REPLICATED = tuple(n for n in _WEIGHTS if n not in dict(_SHARDED))
_ROW_PAD = 1024
_SMALL_ROW_PAD = 8


def _pack(flats, lead, row_pad=_ROW_PAD):
    if all(int(np.prod(a.shape[lead:])) % 128 == 0 for a in flats):
        cat = jnp.concatenate([a.reshape(a.shape[:lead] + (-1, 128)) for a in flats], axis=lead)
        rows = cat.shape[lead]
        return jnp.pad(cat, [(0, 0)] * lead + [(0, -(-rows // row_pad) * row_pad - rows), (0, 0)])
    cat = jnp.concatenate([a.reshape(a.shape[:lead] + (-1,)) for a in flats], axis=-1)
    n = cat.shape[-1]
    total = -(-n // (128 * row_pad)) * (128 * row_pad)
    cat = jnp.pad(cat, [(0, 0)] * lead + [(0, total - n)])
    return cat.reshape(cat.shape[:lead] + (total // 128, 128))


def _unpack(buf, shapes, lead):
    sizes = [int(np.prod(shp)) for shp in shapes]
    out, off = [], 0
    if all(n % 128 == 0 for n in sizes):
        for shp, n in zip(shapes, sizes):
            rows = buf[(slice(None),) * lead + (slice(off // 128, (off + n) // 128),)]
            out.append(rows.reshape(buf.shape[:lead] + tuple(shp)))
            off += n
        return out
    flat = buf.reshape(buf.shape[:lead] + (-1,))
    for shp, n in zip(shapes, sizes):
        out.append(flat[..., off:off + n].reshape(buf.shape[:lead] + tuple(shp)))
        off += n
    return out


def _to_whole(g, axis):
    if axis == 1:
        return g.transpose(1, 0, 2, 3).reshape(g.shape[1], N_DEV * g.shape[2], g.shape[3])
    return g.transpose(1, 2, 0, 3).reshape(g.shape[1], g.shape[2], N_DEV * g.shape[3])


def _to_shards(w, axis):
    L, R, C = w.shape
    if axis == 1:
        return w.reshape(L, N_DEV, R // N_DEV, C).transpose(1, 0, 2, 3)
    return w.reshape(L, R, N_DEV, C // N_DEV).transpose(2, 0, 1, 3)


def kernel(x, mem, g_mix, w_in, b_fox_forget, w_gla_gate, b_gla_gate, g_gla_out, g_mla_q, w_mla_uq, g_mla_kv, w_mla_ukv, b_branch_gate, w_up_fox, w_up_gla, w_up_mla, w_out, g_xa, g_mem, w_xq, w_xkv, w_xo, g_mlp, w_mlp1, w_mlp2, g_final, loss_target, m_g_mix, m_w_in, m_b_fox_forget, m_w_gla_gate, m_b_gla_gate, m_g_gla_out, m_g_mla_q, m_w_mla_uq, m_g_mla_kv, m_w_mla_ukv, m_b_branch_gate, m_w_up_fox, m_w_up_gla, m_w_up_mla, m_w_out, m_g_xa, m_g_mem, m_w_xq, m_w_xkv, m_w_xo, m_g_mlp, m_w_mlp1, m_w_mlp2, m_g_final, v_g_mix, v_w_in, v_b_fox_forget, v_w_gla_gate, v_b_gla_gate, v_g_gla_out, v_g_mla_q, v_w_mla_uq, v_g_mla_kv, v_w_mla_ukv, v_b_branch_gate, v_w_up_fox, v_w_up_gla, v_w_up_mla, v_w_out, v_g_xa, v_g_mem, v_w_xq, v_w_xkv, v_w_xo, v_g_mlp, v_w_mlp1, v_w_mlp2, v_g_final):
    wts = dict(zip(_WEIGHTS, (g_mix, w_in, b_fox_forget, w_gla_gate, b_gla_gate, g_gla_out, g_mla_q, w_mla_uq, g_mla_kv,
                              w_mla_ukv, b_branch_gate, w_up_fox, w_up_gla, w_up_mla, w_out, g_xa, g_mem, w_xq, w_xkv, w_xo,
                              g_mlp, w_mlp1, w_mlp2, g_final)))
    mom1 = dict(zip(_WEIGHTS, (m_g_mix, m_w_in, m_b_fox_forget, m_w_gla_gate, m_b_gla_gate, m_g_gla_out, m_g_mla_q,
                               m_w_mla_uq, m_g_mla_kv, m_w_mla_ukv, m_b_branch_gate, m_w_up_fox, m_w_up_gla, m_w_up_mla,
                               m_w_out, m_g_xa, m_g_mem, m_w_xq, m_w_xkv, m_w_xo, m_g_mlp, m_w_mlp1, m_w_mlp2, m_g_final)))
    mom2 = dict(zip(_WEIGHTS, (v_g_mix, v_w_in, v_b_fox_forget, v_w_gla_gate, v_b_gla_gate, v_g_gla_out, v_g_mla_q,
                               v_w_mla_uq, v_g_mla_kv, v_w_mla_ukv, v_b_branch_gate, v_w_up_fox, v_w_up_gla, v_w_up_mla,
                               v_w_out, v_g_xa, v_g_mem, v_w_xq, v_w_xkv, v_w_xo, v_g_mlp, v_w_mlp1, v_w_mlp2, v_g_final)))
    depth = g_mix.shape[0]

    shard = {n: wts[n] for n, _ in _SHARDED}
    shard['w_in'] = jnp.where(_IN_VALID[None, None, :], w_in[:, :, _IN_IDX], 0.0)
    shard_shapes = [shard[n].shape for n, _ in _SHARDED]
    gathered = _all_gather(_pack([shard[n].astype(BF16) for n, _ in _SHARDED], 0), name="gather_weights")
    whole = {n: _to_whole(g, ax) for (n, ax), g in zip(_SHARDED, _unpack(gathered, shard_shapes, 1))}
    whole.update({n: wts[n] for n in _REPLICATED})

    loss, dx, grads, dg_final = _local_step(x[0], mem[0], loss_target[0], whole)
    loss = lax.psum(loss[0, 0], _MESH_AXES)

    slots = _pack([_to_shards(jnp.stack([grads[l][n] for l in range(depth)]), ax).astype(BF16) for n, ax in _SHARDED], 1)
    slots = slots.reshape((N_CHIP, 2) + slots.shape[1:])
    core = lax.axis_index("c").astype(jnp.int32).reshape(1)
    paired = _pair_sum(slots, _sibling_swap(slots, name="swap_grads"), core, name="pair_grads")
    summed = _sum_slots(_chip_all_to_all(paired, name="scatter_grads"), name="sum_grads")
    grad = dict(zip([n for n, _ in _SHARDED], _unpack(summed, shard_shapes, 0)))
    grad['w_in'] = grad['w_in'][:, :, _IN_INV]
    small = [dg_final if n == 'g_final' else jnp.stack([grads[l][n] for l in range(depth)]) for n in _REPLICATED]
    small_shapes = [wts[n].shape for n in _REPLICATED]
    small_sum = _sum_slots(_all_gather(_pack(small, 0, _SMALL_ROW_PAD), name="gather_small_grads"), name="sum_small_grads")
    grad.update(dict(zip(_REPLICATED, _unpack(small_sum, small_shapes, 0))))

    delta, new_m, new_v = {}, {}, {}
    for n, _ in _SHARDED:
        delta[n], new_m[n], new_v[n] = _adamw(wts[n], grad[n], mom1[n], mom2[n], name=f"adamw_{n}")
    packed = [_pack([d[n] for n in _REPLICATED], 0, _SMALL_ROW_PAD) for d in (wts, mom1, mom2)]
    outs = _adamw(packed[0], small_sum, packed[1], packed[2], name="adamw_small")
    for d, o in zip((delta, new_m, new_v), outs):
        d.update(dict(zip(_REPLICATED, _unpack(o, small_shapes, 0))))

    return (loss, dx[None], *[grad[n] for n in _WEIGHTS], *[delta[n] for n in _WEIGHTS],
            *[new_m[n] for n in _WEIGHTS], *[new_v[n] for n in _WEIGHTS])
```

```python
import jax
import jax.numpy as jnp
import numpy as np
from jax import lax
from jax.experimental import pallas as pl
from jax.experimental.pallas import tpu as pltpu

F32 = jnp.float32
BF16 = jnp.bfloat16

EPS = 1e-6
CHUNK = 64
FOX_HEADS, FOX_HD = 4, 64
GLA_HEADS, GLA_DK, GLA_DV, GLA_RANK, GLA_TAU = 4, 64, 128, 16, 16.0
MLA_HEADS, MLA_Q_RANK, MLA_KV_RANK, MLA_NOPE, MLA_ROPE, MLA_VD = 4, 256, 128, 64, 32, 64
ROPE_BASE = 10000.0
XA_HEADS, XA_HD = 4, 128
ADAM_LR, ADAM_B1, ADAM_B2, ADAM_EPS, ADAM_WD, ADAM_STEP = 0.001, 0.9, 0.999, 1e-08, 0.01, 10

N_DEV = 8
V7X_VMEM_LIMIT = 56 * 1024 * 1024
NEG = -1e30

NN = ((1,), (0,))
NT = ((1,), (1,))
TN = ((0,), (0,))


def _dot(a, b, dims):
    return lax.dot_general(a.astype(BF16), b.astype(BF16), (dims, ((), ())), preferred_element_type=F32)


@jax.custom_vjp
def bdot(a, b):
    return _dot(a, b, NN)


bdot.defvjp(lambda a, b: (_dot(a, b, NN), (a, b)),
            lambda res, g: (_dot(g, res[1], NT), _dot(res[0], g, TN)))


@jax.custom_vjp
def bdot_nt(a, b):
    return _dot(a, b, NT)


bdot_nt.defvjp(lambda a, b: (_dot(a, b, NT), (a, b)),
               lambda res, g: (_dot(g, res[1], NN), _dot(g, res[0], TN)))


@jax.custom_vjp
def bdot_tn(a, b):
    return _dot(a, b, TN)


bdot_tn.defvjp(lambda a, b: (_dot(a, b, TN), (a, b)),
               lambda res, g: (_dot(res[1], g, NT), _dot(res[0], g, NN)))


def _split2(x):
    hi = x.astype(BF16)
    lo = (x - hi.astype(F32)).astype(BF16)
    return hi, lo


def _tri(n, lower):
    r = lax.broadcasted_iota(jnp.int32, (n, n), 0)
    c = lax.broadcasted_iota(jnp.int32, (n, n), 1)
    return jnp.where((r >= c) if lower else (r <= c), 1.0, 0.0).astype(BF16)


def _tri_dot2(x, lower):
    hi, lo = _split2(x)
    t = _tri(x.shape[0], lower)
    return _dot(t, hi, NN) + _dot(t, lo, NN)


@jax.custom_vjp
def chunk_cumsum(x):
    return _tri_dot2(x, True)


chunk_cumsum.defvjp(lambda x: (_tri_dot2(x, True), None), lambda _, g: (_tri_dot2(g, False),))


def _log_sigmoid(x):
    return jnp.minimum(x, 0.0) - jnp.log(1.0 + jnp.exp(-jnp.abs(x)))


def _sigmoid(x):
    return 1.0 / (1.0 + jnp.exp(-x))


def _rms(x, g):
    return x * lax.rsqrt(jnp.mean(x * x, axis=-1, keepdims=True) + EPS) * g


def _pick(dim, prefs):
    for p in prefs:
        if dim % p == 0:
            return p
    return dim


def _params(sem):
    return pltpu.CompilerParams(dimension_semantics=sem, vmem_limit_bytes=V7X_VMEM_LIMIT)


def _mm(a, b, *, mode, out_dtype, name, act=None, residual=None, drelu_of=None, b_cols=None, tm=None, tn=None, tk=None):
    b_off, b_width = b_cols or (0, b.shape[1])
    if mode == 'nn':
        (M, K), N = a.shape, b_width
    elif mode == 'nt':
        (M, K), N = a.shape, b.shape[0]
    else:
        (K, M), N = a.shape, b_width
    tm = tm or _pick(M, (1024, 512, 256, 128))
    tn = tn or _pick(N, (1024, 1920, 1152, 768, 640, 512, 384, 256, 128))
    tk = tk or _pick(K, (1024, 1920, 1152, 640, 512, 256, 128))
    nk = K // tk
    dims = {'nn': NN, 'nt': NT, 'tn': TN}[mode]
    a_spec = pl.BlockSpec((tk, tm), lambda i, j, k: (k, i)) if mode == 'tn' else pl.BlockSpec((tm, tk), lambda i, j, k: (i, k))
    if mode == 'nt':
        b_spec = pl.BlockSpec((tn, tk), lambda i, j, k, o=b_off // tk: (j, k + o))
    else:
        b_spec = pl.BlockSpec((tk, tn), lambda i, j, k, o=b_off // tn: (k, j + o))
    o_spec = pl.BlockSpec((tm, tn), lambda i, j, k: (i, j))
    extra = [e for e in (residual, drelu_of) if e is not None]

    def body(a_ref, b_ref, *rest):
        o_ref = rest[len(extra)]
        at = a_ref[...]
        if act == 'relu2':
            at = jnp.square(jnp.maximum(at.astype(F32), 0.0))
        part = _dot(at, b_ref[...], dims)

        def finish(acc):
            idx = 0
            if residual is not None:
                acc = acc + rest[idx][...]
                idx += 1
            if drelu_of is not None:
                acc = acc * (2.0 * jnp.maximum(rest[idx][...].astype(F32), 0.0))
            o_ref[...] = acc.astype(out_dtype)

        if nk == 1:
            finish(part)
        else:
            acc_ref = rest[len(extra) + 1]
            k = pl.program_id(2)

            @pl.when(k == 0)
            def _():
                acc_ref[...] = part

            @pl.when(k > 0)
            def _():
                acc_ref[...] += part

            @pl.when(k == nk - 1)
            def _():
                finish(acc_ref[...])

    return pl.pallas_call(
        body, name=name,
        out_shape=jax.ShapeDtypeStruct((M, N), out_dtype),
        grid=(M // tm, N // tn, nk),
        in_specs=[a_spec, b_spec] + [o_spec] * len(extra),
        out_specs=o_spec,
        scratch_shapes=[] if nk == 1 else [pltpu.VMEM((tm, tn), F32)],
        compiler_params=_params(("parallel", "parallel", "arbitrary")),
    )(a, b, *extra)


def _rms_fwd(x, g, *, name, out_dtype=BF16):
    S, D = x.shape
    tr = _pick(S, (512, 256, 128))

    def body(x_ref, g_ref, o_ref):
        o_ref[...] = _rms(x_ref[...], g_ref[...]).astype(out_dtype)

    return pl.pallas_call(
        body, name=name, out_shape=jax.ShapeDtypeStruct((S, D), out_dtype), grid=(S // tr,),
        in_specs=[pl.BlockSpec((tr, D), lambda i: (i, 0)), pl.BlockSpec((1, D), lambda i: (0, 0))],
        out_specs=pl.BlockSpec((tr, D), lambda i: (i, 0)),
        compiler_params=_params(("parallel",)),
    )(x, g.reshape(1, D))


def _rms_bwd(x, g, dy, dres, *, name):
    S, D = x.shape
    tr = _pick(S, (512, 256, 128))

    def body(x_ref, g_ref, dy_ref, *rest):
        dx_ref, dxb_ref, dg_ref = rest[-3], rest[-2], rest[-1]
        x_ = x_ref[...]
        rstd = lax.rsqrt(jnp.mean(x_ * x_, axis=-1, keepdims=True) + EPS)
        xh = x_ * rstd
        dy_ = dy_ref[...].astype(F32)
        gdy = dy_ * g_ref[...]
        dx = (gdy - xh * jnp.mean(gdy * xh, axis=-1, keepdims=True)) * rstd
        if dres is not None:
            dx = dx + rest[0][...]
        dx_ref[...] = dx
        dxb_ref[...] = dx.astype(BF16)
        part = jnp.sum(dy_ * xh, axis=0, keepdims=True)

        @pl.when(pl.program_id(0) == 0)
        def _():
            dg_ref[...] = part

        @pl.when(pl.program_id(0) > 0)
        def _():
            dg_ref[...] += part

    row = pl.BlockSpec((tr, D), lambda i: (i, 0))
    vec = pl.BlockSpec((1, D), lambda i: (0, 0))
    ins = [x, g.reshape(1, D), dy] + ([dres] if dres is not None else [])
    return pl.pallas_call(
        body, name=name,
        out_shape=(jax.ShapeDtypeStruct((S, D), F32), jax.ShapeDtypeStruct((S, D), BF16), jax.ShapeDtypeStruct((1, D), F32)),
        grid=(S // tr,),
        in_specs=[row, vec, row] + ([row] if dres is not None else []),
        out_specs=(row, row, vec),
        compiler_params=_params(("arbitrary",)),
    )(*ins)


def _loss_head(x, g, target, *, name):
    S, D = x.shape
    tr = _pick(S, (512, 256, 128))

    def body(x_ref, g_ref, t_ref, l_ref, dx_ref, dxb_ref, dg_ref):
        x_ = x_ref[...]
        g_ = g_ref[...]
        rstd = lax.rsqrt(jnp.mean(x_ * x_, axis=-1, keepdims=True) + EPS)
        xh = x_ * rstd
        err = xh * g_ - t_ref[...]
        lpart = (0.5 / D) * jnp.sum(jnp.sum(err * err, axis=-1, keepdims=True), axis=0, keepdims=True)
        dy = err * (1.0 / D)
        gdy = dy * g_
        dx = (gdy - xh * jnp.mean(gdy * xh, axis=-1, keepdims=True)) * rstd
        dx_ref[...] = dx
        dxb_ref[...] = dx.astype(BF16)
        gpart = jnp.sum(dy * xh, axis=0, keepdims=True)

        @pl.when(pl.program_id(0) == 0)
        def _():
            dg_ref[...] = gpart
            l_ref[...] = lpart

        @pl.when(pl.program_id(0) > 0)
        def _():
            dg_ref[...] += gpart
            l_ref[...] += lpart

    row = pl.BlockSpec((tr, D), lambda i: (i, 0))
    vec = pl.BlockSpec((1, D), lambda i: (0, 0))
    return pl.pallas_call(
        body, name=name,
        out_shape=(jax.ShapeDtypeStruct((1, 1), F32), jax.ShapeDtypeStruct((S, D), F32), jax.ShapeDtypeStruct((S, D), BF16),
                   jax.ShapeDtypeStruct((1, D), F32)),
        grid=(S // tr,),
        in_specs=[row, vec, row],
        out_specs=(pl.BlockSpec((1, 1), lambda i: (0, 0)), row, row, vec),
        compiler_params=_params(("arbitrary",)),
    )(x, g.reshape(1, D), target)


def _mask_of(mask, tq, tk):
    qpos = lax.broadcasted_iota(jnp.int32, (tq, tk), 0)
    kpos = lax.broadcasted_iota(jnp.int32, (tq, tk), 1)
    if mask == 'causal':
        return kpos <= qpos
    return kpos <= (qpos | (CHUNK - 1))


def _col_block(entry, rows, idx):
    arr, off, width = entry
    return pl.BlockSpec((rows, width), lambda i, j, o=off // width: (idx(i, j), o))


def _attn_fwd(qk, v, H, cq, ck, *, scale, mask, name):
    Sq, Sk = qk[0][0][0].shape[0], v[0].shape[0]
    dv = v[2] // H
    tq = _pick(Sq, (512, 256, 128))
    tk = tq if mask else _pick(Sk, (512, 256, 128))
    nq, nk = Sq // tq, Sk // tk
    bias = cq is not None
    npart = len(qk)

    def body(*refs):
        q_refs, k_refs = refs[0:2 * npart:2], refs[1:2 * npart:2]
        v_ref = refs[2 * npart]
        cq_ref, ck_ref = (refs[2 * npart + 1], refs[2 * npart + 2]) if bias else (None, None)
        o_ref, lse_ref, m_s, l_s, acc_s = refs[-5:]
        qi, ki = pl.program_id(0), pl.program_id(1)

        @pl.when(ki == 0)
        def _():
            m_s[...] = jnp.full(m_s.shape, NEG, F32)
            l_s[...] = jnp.zeros(l_s.shape, F32)
            acc_s[...] = jnp.zeros(acc_s.shape, F32)

        def compute(masked):
            keep = _mask_of(mask, tq, tk) if masked else None
            for h in range(H):
                s = None
                for (_, _, w, shared), q_ref, k_ref in zip(qk, q_refs, k_refs):
                    part = _dot(q_ref[:, h * w:(h + 1) * w], k_ref[...] if shared else k_ref[:, h * w:(h + 1) * w], NT)
                    s = part if s is None else s + part
                s = s * scale
                if bias:
                    s = s + (cq_ref[:, h:h + 1] - ck_ref[h:h + 1, :])
                if masked:
                    s = jnp.where(keep, s, NEG)
                m_prev = m_s[h]
                m_new = jnp.maximum(m_prev, jnp.max(s, axis=1, keepdims=True))
                alpha = jnp.exp(m_prev - m_new)
                p = jnp.exp(s - m_new)
                l_s[h] = alpha * l_s[h] + jnp.sum(p, axis=1, keepdims=True)
                acc_s[h] = alpha * acc_s[h] + _dot(p, v_ref[:, h * dv:(h + 1) * dv], NN)
                m_s[h] = m_new

        if mask is None:
            compute(False)
        else:
            pl.when(ki < qi)(lambda: compute(False))
            pl.when(ki == qi)(lambda: compute(True))

        @pl.when(ki == ((nk - 1) if mask is None else qi))
        def _():
            lse_ref[...] = jnp.zeros(lse_ref.shape, F32)
            for h in range(H):
                o_ref[:, h * dv:(h + 1) * dv] = (acc_s[h] / l_s[h]).astype(BF16)
                lse_ref[:, h:h + 1] = m_s[h] + jnp.log(l_s[h])

    q_idx = lambda i, j: i
    k_idx = (lambda i, j: jnp.minimum(i, j)) if mask else (lambda i, j: j)
    ins, in_specs = [], []
    for q_e, k_e, _, _ in qk:
        ins += [q_e[0], k_e[0]]
        in_specs += [_col_block(q_e, tq, q_idx), _col_block(k_e, tk, k_idx)]
    ins.append(v[0])
    in_specs.append(_col_block(v, tk, k_idx))
    if bias:
        in_specs += [pl.BlockSpec((tq, 8), lambda i, j: (i, 0)), pl.BlockSpec((8, tk), lambda i, j: (0, k_idx(i, j)))]
        ins += [cq, ck]
    return pl.pallas_call(
        body, name=name,
        out_shape=(jax.ShapeDtypeStruct((Sq, H * dv), BF16), jax.ShapeDtypeStruct((Sq, 8), F32)),
        grid=(nq, nk), in_specs=in_specs,
        out_specs=(pl.BlockSpec((tq, H * dv), lambda i, j: (i, 0)), pl.BlockSpec((tq, 8), lambda i, j: (i, 0))),
        scratch_shapes=[pltpu.VMEM((H, tq, 1), F32), pltpu.VMEM((H, tq, 1), F32), pltpu.VMEM((H, tq, dv), F32)],
        compiler_params=_params(("parallel", "arbitrary")),
    )(*ins)


def _attn_bwd(qk, v, H, o, do, lse, cq, ck, *, scale, mask, name):
    Sq, Sk = qk[0][0][0].shape[0], v[0].shape[0]
    dv = v[2] // H
    tq = _pick(Sq, (512, 256, 128))
    tk = tq if mask else _pick(Sk, (512, 256, 128))
    nq, nk = Sq // tq, Sk // tk
    bias = cq is not None
    npart = len(qk)
    n_in = 2 * npart + 4 + (2 if bias else 0)

    def body(*refs):
        q_refs, k_refs = refs[0:2 * npart:2], refs[1:2 * npart:2]
        v_ref, o_ref, do_ref, lse_ref = refs[2 * npart:2 * npart + 4]
        cq_ref, ck_ref = (refs[2 * npart + 4], refs[2 * npart + 5]) if bias else (None, None)
        outs = refs[n_in:]
        dq_refs, dk_refs, dv_ref = outs[:npart], outs[npart:2 * npart], outs[2 * npart]
        dck_ref, dcq_ref = (outs[2 * npart + 1], outs[2 * npart + 2]) if bias else (None, None)
        dk_accs, dv_acc = refs[-(npart + 1):-1], refs[-1]
        ki, qi = pl.program_id(0), pl.program_id(1)
        first_q = ki if mask else 0

        @pl.when((ki == 0) & (qi == 0))
        def _():
            for r in dq_refs:
                r[...] = jnp.zeros(r.shape, F32)
            if bias:
                dcq_ref[...] = jnp.zeros(dcq_ref.shape, F32)

        @pl.when(qi == first_q)
        def _():
            for r in dk_accs:
                r[...] = jnp.zeros(r.shape, F32)
            dv_acc[...] = jnp.zeros(dv_acc.shape, F32)
            if bias:
                dck_ref[...] = jnp.zeros(dck_ref.shape, F32)

        def compute(masked):
            keep = _mask_of(mask, tq, tk) if masked else None
            rows = pl.ds(pl.multiple_of(qi * tq, tq), tq)
            for h in range(H):
                hv = slice(h * dv, (h + 1) * dv)
                cols = [slice(h * w, (h + 1) * w) for _, _, w, _ in qk]
                kcols = [slice(None) if shared else c for (_, _, _, shared), c in zip(qk, cols)]
                s = None
                for q_ref, k_ref, c, kc in zip(q_refs, k_refs, cols, kcols):
                    part = _dot(q_ref[:, c], k_ref[:, kc], NT)
                    s = part if s is None else s + part
                s = s * scale
                if bias:
                    s = s + (cq_ref[:, h:h + 1] - ck_ref[h:h + 1, :])
                if masked:
                    s = jnp.where(keep, s, NEG)
                p = jnp.exp(s - lse_ref[:, h:h + 1])
                doh = do_ref[:, hv]
                dp = _dot(doh, v_ref[:, hv], NT)
                delta = jnp.sum(doh.astype(F32) * o_ref[:, hv].astype(F32), axis=1, keepdims=True)
                ds = p * (dp - delta)
                dv_acc[:, hv] += _dot(p, doh, TN)
                dss = (ds * scale).astype(BF16)
                for q_ref, k_ref, dq_ref, dk_acc, c, kc in zip(q_refs, k_refs, dq_refs, dk_accs, cols, kcols):
                    dk_acc[:, kc] += _dot(dss, q_ref[:, c], TN)
                    dq_ref[rows, c] += _dot(dss, k_ref[:, kc], NN)
                if bias:
                    dck_ref[h:h + 1, :] -= jnp.sum(ds, axis=0, keepdims=True)
                    dcq_ref[rows, h:h + 1] += jnp.sum(ds, axis=1, keepdims=True)

        if mask is None:
            compute(False)
        else:
            pl.when(qi > ki)(lambda: compute(False))
            pl.when(qi == ki)(lambda: compute(True))

        @pl.when(qi == nq - 1)
        def _():
            for r, acc in zip(dk_refs, dk_accs):
                r[...] = acc[...]
            dv_ref[...] = dv_acc[...]

    q_idx = (lambda j, i: jnp.maximum(i, j)) if mask else (lambda j, i: i)
    k_idx = lambda j, i: j
    ins, in_specs, dq_shapes, dq_specs, dk_shapes, dk_specs, scratch = [], [], [], [], [], [], []
    for q_e, k_e, w, shared in qk:
        ins += [q_e[0], k_e[0]]
        in_specs += [_col_block(q_e, tq, q_idx), _col_block(k_e, tk, k_idx)]
        dq_shapes.append(jax.ShapeDtypeStruct((Sq, H * w), F32))
        dq_specs.append(pl.BlockSpec((Sq, H * w), lambda j, i: (0, 0)))
        kw = w if shared else H * w
        dk_shapes.append(jax.ShapeDtypeStruct((Sk, kw), F32))
        dk_specs.append(pl.BlockSpec((tk, kw), lambda j, i: (j, 0)))
        scratch.append(pltpu.VMEM((tk, kw), F32))
    row_q = lambda width: pl.BlockSpec((tq, width), lambda j, i: (q_idx(j, i), 0))
    ins += [v[0], o, do, lse]
    in_specs += [_col_block(v, tk, k_idx), row_q(H * dv), row_q(H * dv), row_q(8)]
    out_shape = dq_shapes + dk_shapes + [jax.ShapeDtypeStruct((Sk, H * dv), F32)]
    out_specs = dq_specs + dk_specs + [pl.BlockSpec((tk, H * dv), lambda j, i: (j, 0))]
    if bias:
        in_specs += [row_q(8), pl.BlockSpec((8, tk), lambda j, i: (0, j))]
        ins += [cq, ck]
        out_shape += [jax.ShapeDtypeStruct((8, Sk), F32), jax.ShapeDtypeStruct((Sq, 8), F32)]
        out_specs += [pl.BlockSpec((8, tk), lambda j, i: (0, j)), pl.BlockSpec((Sq, 8), lambda j, i: (0, 0))]
    scratch.append(pltpu.VMEM((tk, H * dv), F32))
    res = pl.pallas_call(
        body, name=name, out_shape=tuple(out_shape), grid=(nk, nq), in_specs=in_specs, out_specs=tuple(out_specs),
        scratch_shapes=scratch, compiler_params=_params(("arbitrary", "arbitrary")),
    )(*ins)
    return (list(res[:npart]), list(res[npart:2 * npart]), res[2 * npart]) + tuple(res[2 * npart + 1:])


def _flash_fwd(q, k, v, cq, ck, *, scale, mask, name):
    H, Sq, dk = q.shape
    Sk, dv = k.shape[1], v.shape[2]
    tq = _pick(Sq, (512, 256, 128))
    tk = tq if mask else _pick(Sk, (512, 256, 128))
    nq, nk = Sq // tq, Sk // tk
    bias = cq is not None

    def body(*refs):
        q_ref, k_ref, v_ref = refs[:3]
        cq_ref, ck_ref = (refs[3], refs[4]) if bias else (None, None)
        o_ref, lse_ref, m_s, l_s, acc_s = refs[-5:]
        qi, ki = pl.program_id(0), pl.program_id(1)

        @pl.when(ki == 0)
        def _():
            m_s[...] = jnp.full(m_s.shape, NEG, F32)
            l_s[...] = jnp.zeros(l_s.shape, F32)
            acc_s[...] = jnp.zeros(acc_s.shape, F32)

        def compute(masked):
            keep = _mask_of(mask, tq, tk) if masked else None
            for h in range(H):
                s = _dot(q_ref[h], k_ref[h], NT) * scale
                if bias:
                    s = s + (cq_ref[:, h:h + 1] - ck_ref[h:h + 1, :])
                if masked:
                    s = jnp.where(keep, s, NEG)
                m_prev = m_s[h]
                m_new = jnp.maximum(m_prev, jnp.max(s, axis=1, keepdims=True))
                alpha = jnp.exp(m_prev - m_new)
                p = jnp.exp(s - m_new)
                l_s[h] = alpha * l_s[h] + jnp.sum(p, axis=1, keepdims=True)
                acc_s[h] = alpha * acc_s[h] + _dot(p, v_ref[h], NN)
                m_s[h] = m_new

        if mask is None:
            compute(False)
        else:
            pl.when(ki < qi)(lambda: compute(False))
            pl.when(ki == qi)(lambda: compute(True))

        @pl.when(ki == ((nk - 1) if mask is None else qi))
        def _():
            lse_ref[...] = jnp.zeros(lse_ref.shape, F32)
            for h in range(H):
                o_ref[h] = (acc_s[h] / l_s[h]).astype(BF16)
                lse_ref[:, h:h + 1] = m_s[h] + jnp.log(l_s[h])

    kv_idx = (lambda i, j: (0, jnp.minimum(i, j), 0)) if mask else (lambda i, j: (0, j, 0))
    ck_idx = (lambda i, j: (0, jnp.minimum(i, j))) if mask else (lambda i, j: (0, j))
    in_specs = [pl.BlockSpec((H, tq, dk), lambda i, j: (0, i, 0)),
                pl.BlockSpec((H, tk, dk), kv_idx), pl.BlockSpec((H, tk, dv), kv_idx)]
    ins = [q, k, v]
    if bias:
        in_specs += [pl.BlockSpec((tq, 8), lambda i, j: (i, 0)), pl.BlockSpec((8, tk), ck_idx)]
        ins += [cq, ck]
    return pl.pallas_call(
        body, name=name,
        out_shape=(jax.ShapeDtypeStruct((H, Sq, dv), BF16), jax.ShapeDtypeStruct((Sq, 8), F32)),
        grid=(nq, nk), in_specs=in_specs,
        out_specs=(pl.BlockSpec((H, tq, dv), lambda i, j: (0, i, 0)), pl.BlockSpec((tq, 8), lambda i, j: (i, 0))),
        scratch_shapes=[pltpu.VMEM((H, tq, 1), F32), pltpu.VMEM((H, tq, 1), F32), pltpu.VMEM((H, tq, dv), F32)],
        compiler_params=_params(("parallel", "arbitrary")),
    )(*ins)


def _flash_bwd(q, k, v, o, do, lse, cq, ck, *, scale, mask, name):
    H, Sq, dk = q.shape
    Sk, dv = k.shape[1], v.shape[2]
    tq = _pick(Sq, (512, 256, 128))
    tk = tq if mask else _pick(Sk, (512, 256, 128))
    nq, nk = Sq // tq, Sk // tk
    bias = cq is not None

    def body(*refs):
        q_ref, k_ref, v_ref, o_ref, do_ref, lse_ref = refs[:6]
        n_in = 8 if bias else 6
        cq_ref, ck_ref = (refs[6], refs[7]) if bias else (None, None)
        outs = refs[n_in:]
        dq_ref, dk_ref, dv_ref = outs[:3]
        dck_ref, dcq_ref = (outs[3], outs[4]) if bias else (None, None)
        dk_s, dv_s = refs[-2], refs[-1]
        ki, qi = pl.program_id(0), pl.program_id(1)
        first_q = ki if mask else 0

        @pl.when((ki == 0) & (qi == 0))
        def _():
            dq_ref[...] = jnp.zeros(dq_ref.shape, F32)
            if bias:
                dcq_ref[...] = jnp.zeros(dcq_ref.shape, F32)

        @pl.when(qi == first_q)
        def _():
            dk_s[...] = jnp.zeros(dk_s.shape, F32)
            dv_s[...] = jnp.zeros(dv_s.shape, F32)
            if bias:
                dck_ref[...] = jnp.zeros(dck_ref.shape, F32)

        def compute(masked):
            keep = _mask_of(mask, tq, tk) if masked else None
            rows = pl.ds(pl.multiple_of(qi * tq, tq), tq)
            for h in range(H):
                qh, kh, vh, doh = q_ref[h], k_ref[h], v_ref[h], do_ref[h]
                s = _dot(qh, kh, NT) * scale
                if bias:
                    s = s + (cq_ref[:, h:h + 1] - ck_ref[h:h + 1, :])
                if masked:
                    s = jnp.where(keep, s, NEG)
                p = jnp.exp(s - lse_ref[:, h:h + 1])
                dp = _dot(doh, vh, NT)
                delta = jnp.sum(doh.astype(F32) * o_ref[h].astype(F32), axis=1, keepdims=True)
                ds = p * (dp - delta)
                dv_s[h] += _dot(p, doh, TN)
                dk_s[h] += _dot(ds, qh, TN)
                dq_ref[h, rows, :] += _dot(ds, kh, NN) * scale
                if bias:
                    dck_ref[h:h + 1, :] -= jnp.sum(ds, axis=0, keepdims=True)
                    dcq_ref[rows, h:h + 1] += jnp.sum(ds, axis=1, keepdims=True)

        if mask is None:
            compute(False)
        else:
            pl.when(qi > ki)(lambda: compute(False))
            pl.when(qi == ki)(lambda: compute(True))

        @pl.when(qi == nq - 1)
        def _():
            dk_ref[...] = dk_s[...] * scale
            dv_ref[...] = dv_s[...]

    q_idx = (lambda j, i: (0, jnp.maximum(i, j), 0)) if mask else (lambda j, i: (0, i, 0))
    c_idx = (lambda j, i: (jnp.maximum(i, j), 0)) if mask else (lambda j, i: (i, 0))
    kv_idx = lambda j, i: (0, j, 0)
    in_specs = [pl.BlockSpec((H, tq, dk), q_idx), pl.BlockSpec((H, tk, dk), kv_idx), pl.BlockSpec((H, tk, dv), kv_idx),
                pl.BlockSpec((H, tq, dv), q_idx), pl.BlockSpec((H, tq, dv), q_idx), pl.BlockSpec((tq, 8), c_idx)]
    ins = [q, k, v, o, do, lse]
    out_shape = [jax.ShapeDtypeStruct((H, Sq, dk), F32), jax.ShapeDtypeStruct((H, Sk, dk), F32),
                 jax.ShapeDtypeStruct((H, Sk, dv), F32)]
    out_specs = [pl.BlockSpec((H, Sq, dk), lambda j, i: (0, 0, 0)), pl.BlockSpec((H, tk, dk), kv_idx),
                 pl.BlockSpec((H, tk, dv), kv_idx)]
    if bias:
        in_specs += [pl.BlockSpec((tq, 8), c_idx), pl.BlockSpec((8, tk), lambda j, i: (0, j))]
        ins += [cq, ck]
        out_shape += [jax.ShapeDtypeStruct((8, Sk), F32), jax.ShapeDtypeStruct((Sq, 8), F32)]
        out_specs += [pl.BlockSpec((8, tk), lambda j, i: (0, j)), pl.BlockSpec((Sq, 8), lambda j, i: (0, 0))]
    return pl.pallas_call(
        body, name=name, out_shape=tuple(out_shape), grid=(nk, nq), in_specs=in_specs, out_specs=tuple(out_specs),
        scratch_shapes=[pltpu.VMEM((H, tk, dk), F32), pltpu.VMEM((H, tk, dv), F32)],
        compiler_params=_params(("arbitrary", "arbitrary")),
    )(*ins)


def _split3_dot(x, t):
    hi = x.astype(BF16)
    r1 = x - hi.astype(F32)
    mid = r1.astype(BF16)
    lo = (r1 - mid.astype(F32)).astype(BF16)
    return _dot(hi, t, NN) + _dot(mid, t, NN) + _dot(lo, t, NN)


def _fox_cum_fwd(ff_t, b, *, name):
    _, S = ff_t.shape
    tb = _pick(S, (512, 256, 128))

    def body(f_ref, b_ref, o_ref, carry):
        @pl.when(pl.program_id(0) == 0)
        def _():
            carry[...] = jnp.zeros(carry.shape, F32)

        lf = _log_sigmoid(f_ref[...] + b_ref[...])
        o_ref[...] = _split3_dot(lf, _tri(tb, False)) + carry[...]
        carry[...] += jnp.sum(lf, axis=1, keepdims=True)

    return pl.pallas_call(
        body, name=name, out_shape=jax.ShapeDtypeStruct((8, S), F32), grid=(S // tb,),
        in_specs=[pl.BlockSpec((8, tb), lambda i: (0, i)), pl.BlockSpec((8, 1), lambda i: (0, 0))],
        out_specs=pl.BlockSpec((8, tb), lambda i: (0, i)),
        scratch_shapes=[pltpu.VMEM((8, 1), F32)],
        compiler_params=_params(("arbitrary",)),
    )(ff_t, b)


def _fox_cum_bwd(ff_t, b, dcum_t, *, name):
    _, S = ff_t.shape
    tb = _pick(S, (512, 256, 128))
    nb = S // tb

    def body(f_ref, b_ref, dc_ref, df_ref, db_ref, carry):
        @pl.when(pl.program_id(0) == 0)
        def _():
            carry[...] = jnp.zeros(carry.shape, F32)
            db_ref[...] = jnp.zeros(db_ref.shape, F32)

        dc = dc_ref[...]
        dlf = _split3_dot(dc, _tri(tb, True)) + carry[...]
        carry[...] += jnp.sum(dc, axis=1, keepdims=True)
        df = dlf * _sigmoid(-(f_ref[...] + b_ref[...]))
        df_ref[...] = df
        db_ref[...] += jnp.sum(df, axis=1, keepdims=True)

    rev = lambda i: (0, nb - 1 - i)
    return pl.pallas_call(
        body, name=name,
        out_shape=(jax.ShapeDtypeStruct((8, S), F32), jax.ShapeDtypeStruct((8, 1), F32)), grid=(nb,),
        in_specs=[pl.BlockSpec((8, tb), rev), pl.BlockSpec((8, 1), lambda i: (0, 0)), pl.BlockSpec((8, tb), rev)],
        out_specs=(pl.BlockSpec((8, tb), rev), pl.BlockSpec((8, 1), lambda i: (0, 0))),
        scratch_shapes=[pltpu.VMEM((8, 1), F32)],
        compiler_params=_params(("arbitrary",)),
    )(ff_t, b, dcum_t)


GLA_W = GLA_HEADS * GLA_DK
GLA_BLOCK_CHUNKS = 4


def _gla_chunk(q, k, zsm, wg, bg, go, vs, rs, states):
    la = _log_sigmoid(bdot(zsm, wg) + bg) * (1.0 / GLA_TAU)
    cum = chunk_cumsum(la)
    end = jnp.sum(la, axis=0, keepdims=True)
    kd = k * jnp.exp(end - cum)
    a = jnp.exp(end)
    qs = q * (GLA_DK ** -0.5)
    lane = lax.broadcasted_iota(jnp.int32, (1, GLA_W), 1)
    outs, new_states = [], []
    for h in range(GLA_HEADS):
        head = jnp.where((lane >= h * GLA_DK) & (lane < (h + 1) * GLA_DK), 1.0, 0.0)
        st = states[h] * a + bdot_tn(vs[h], kd * head)
        o = bdot_nt(qs, st)
        o = _rms(o, go)
        outs.append(o * (rs[h] * _sigmoid(rs[h])))
        new_states.append(st)
    return outs, new_states


def _gla_fwd(z, zsm, wg, bg, go, cols, *, name):
    S = z.shape[0]
    rb = GLA_BLOCK_CHUNKS * CHUNK
    nb = S // rb
    cq, ckk, cv, cr = cols
    H = GLA_HEADS

    def body(q_ref, k_ref, zsm_ref, wg_ref, bg_ref, go_ref, *rest):
        v_refs, r_refs = rest[:H], rest[H:2 * H]
        o_ref, st_ref, state = rest[2 * H], rest[2 * H + 1], rest[2 * H + 2]

        @pl.when(pl.program_id(0) == 0)
        def _():
            state[...] = jnp.zeros(state.shape, F32)

        wg_, bg_, go_ = wg_ref[...], bg_ref[...], go_ref[...]
        for c in range(GLA_BLOCK_CHUNKS):
            rows = pl.ds(c * CHUNK, CHUNK)
            states = [state[h] for h in range(H)]
            for h in range(H):
                st_ref[c, h] = states[h]
            outs, new_states = _gla_chunk(
                q_ref[rows, :].astype(F32), k_ref[rows, :].astype(F32), zsm_ref[rows, :], wg_, bg_, go_,
                [v_refs[h][rows, :].astype(F32) for h in range(H)], [r_refs[h][rows, :].astype(F32) for h in range(H)], states)
            for h in range(H):
                o_ref[rows, h * GLA_DV:(h + 1) * GLA_DV] = outs[h].astype(BF16)
                state[h] = new_states[h]

    def col(width, off):
        return pl.BlockSpec((rb, width), lambda i, o=off // width: (i, o))

    full = lambda shp: pl.BlockSpec(shp, lambda i: (0,) * len(shp))
    in_specs = [col(GLA_W, cq), col(GLA_W, ckk), pl.BlockSpec((rb, 128), lambda i: (i, 0)),
                full((128, GLA_W)), full((1, GLA_W)), full((1, GLA_DV))]
    in_specs += [col(GLA_DV, cv + h * GLA_DV) for h in range(H)] + [col(GLA_DV, cr + h * GLA_DV) for h in range(H)]
    return pl.pallas_call(
        body, name=name,
        out_shape=(jax.ShapeDtypeStruct((S, H * GLA_DV), BF16), jax.ShapeDtypeStruct((S // CHUNK, H, GLA_DV, GLA_W), F32)),
        grid=(nb,), in_specs=in_specs,
        out_specs=(pl.BlockSpec((rb, H * GLA_DV), lambda i: (i, 0)),
                   pl.BlockSpec((GLA_BLOCK_CHUNKS, H, GLA_DV, GLA_W), lambda i: (i, 0, 0, 0))),
        scratch_shapes=[pltpu.VMEM((H, GLA_DV, GLA_W), F32)],
        compiler_params=_params(("arbitrary",)),
    )(z, z, zsm, wg, bg, go, *([z] * (2 * H)))


def _gla_bwd(z, zsm, wg, bg, go, states, do, cols, *, name):
    S = z.shape[0]
    rb = GLA_BLOCK_CHUNKS * CHUNK
    nb = S // rb
    cq, ckk, cv, cr = cols
    H = GLA_HEADS

    def body(q_ref, k_ref, zsm_ref, wg_ref, bg_ref, go_ref, st_ref, do_ref, *rest):
        v_refs, r_refs = rest[:H], rest[H:2 * H]
        dq_ref, dk_ref, dv_ref, dr_ref, dzsm_ref, dwg_ref, dbg_ref, dgo_ref, dstate = rest[2 * H:]

        @pl.when(pl.program_id(0) == 0)
        def _():
            dstate[...] = jnp.zeros(dstate.shape, F32)
            dwg_ref[...] = jnp.zeros(dwg_ref.shape, F32)
            dbg_ref[...] = jnp.zeros(dbg_ref.shape, F32)
            dgo_ref[...] = jnp.zeros(dgo_ref.shape, F32)

        wg_, bg_, go_ = wg_ref[...], bg_ref[...], go_ref[...]
        for c in reversed(range(GLA_BLOCK_CHUNKS)):
            rows = pl.ds(c * CHUNK, CHUNK)
            prim = (q_ref[rows, :].astype(F32), k_ref[rows, :].astype(F32), zsm_ref[rows, :], wg_, bg_, go_,
                    [v_refs[h][rows, :].astype(F32) for h in range(H)], [r_refs[h][rows, :].astype(F32) for h in range(H)],
                    [st_ref[c, h] for h in range(H)])
            _, vjp = jax.vjp(_gla_chunk, *prim)
            douts = [do_ref[rows, h * GLA_DV:(h + 1) * GLA_DV].astype(F32) for h in range(H)]
            dq, dk, dzs, dwg, dbg, dgo, dvs, drs, dsts = vjp((douts, [dstate[h] for h in range(H)]))
            dq_ref[rows, :] = dq.astype(BF16)
            dk_ref[rows, :] = dk.astype(BF16)
            dzsm_ref[rows, :] = dzs
            dwg_ref[...] += dwg
            dbg_ref[...] += dbg
            dgo_ref[...] += dgo
            for h in range(H):
                dv_ref[rows, h * GLA_DV:(h + 1) * GLA_DV] = dvs[h].astype(BF16)
                dr_ref[rows, h * GLA_DV:(h + 1) * GLA_DV] = drs[h].astype(BF16)
                dstate[h] = dsts[h]

    rev = lambda i: nb - 1 - i

    def col(width, off):
        return pl.BlockSpec((rb, width), lambda i, o=off // width: (rev(i), o))

    full = lambda shp: pl.BlockSpec(shp, lambda i: (0,) * len(shp))
    rowb = lambda w: pl.BlockSpec((rb, w), lambda i: (rev(i), 0))
    in_specs = [col(GLA_W, cq), col(GLA_W, ckk), rowb(128), full((128, GLA_W)), full((1, GLA_W)), full((1, GLA_DV)),
                pl.BlockSpec((GLA_BLOCK_CHUNKS, H, GLA_DV, GLA_W), lambda i: (rev(i), 0, 0, 0)), rowb(H * GLA_DV)]
    in_specs += [col(GLA_DV, cv + h * GLA_DV) for h in range(H)] + [col(GLA_DV, cr + h * GLA_DV) for h in range(H)]
    return pl.pallas_call(
        body, name=name,
        out_shape=(jax.ShapeDtypeStruct((S, GLA_W), BF16), jax.ShapeDtypeStruct((S, GLA_W), BF16),
                   jax.ShapeDtypeStruct((S, H * GLA_DV), BF16), jax.ShapeDtypeStruct((S, H * GLA_DV), BF16),
                   jax.ShapeDtypeStruct((S, 128), F32), jax.ShapeDtypeStruct((128, GLA_W), F32),
                   jax.ShapeDtypeStruct((1, GLA_W), F32), jax.ShapeDtypeStruct((1, GLA_DV), F32)),
        grid=(nb,), in_specs=in_specs,
        out_specs=(rowb(GLA_W), rowb(GLA_W), rowb(H * GLA_DV), rowb(H * GLA_DV), rowb(128),
                   full((128, GLA_W)), full((1, GLA_W)), full((1, GLA_DV))),
        scratch_shapes=[pltpu.VMEM((H, GLA_DV, GLA_W), F32)],
        compiler_params=_params(("arbitrary",)),
    )(z, z, zsm, wg, bg, go, states, do, *([z] * (2 * H)))


def _row_spec(entry, tr):
    if isinstance(entry, tuple):
        arr, width, off = entry
        return arr, pl.BlockSpec((tr, width), lambda i, o=off // width: (i, o))
    return entry, pl.BlockSpec((tr, entry.shape[1]), lambda i: (i, 0))


def _stage_fwd(fn, rows, consts, outs, *, name, tr=None):
    first = rows[0][0] if isinstance(rows[0], tuple) else rows[0]
    S = first.shape[0]
    tr = tr or _pick(S, (512, 256, 128))
    arrs, specs = zip(*[_row_spec(e, tr) for e in rows])
    nr, nc = len(rows), len(consts)

    def body(*refs):
        vals = [r[...].astype(F32) for r in refs[:nr + nc]]
        res = fn(*vals)
        for o_ref, val in zip(refs[nr + nc:], res):
            o_ref[...] = val.astype(o_ref.dtype)

    cspecs = [pl.BlockSpec(c.shape, lambda i, n=c.ndim: (0,) * n) for c in consts]
    return pl.pallas_call(
        body, name=name,
        out_shape=tuple(jax.ShapeDtypeStruct((S, w), dt) for w, dt in outs), grid=(S // tr,),
        in_specs=list(specs) + cspecs,
        out_specs=tuple(pl.BlockSpec((tr, w), lambda i: (i, 0)) for w, _ in outs),
        compiler_params=_params(("parallel",)),
    )(*arrs, *consts)


def _stage_bwd(fn, rows, consts, cts, n_diff, drow_dtypes, *, name, tr=None):
    first = rows[0][0] if isinstance(rows[0], tuple) else rows[0]
    S = first.shape[0]
    tr = tr or _pick(S, (512, 256, 128))
    arrs, specs = zip(*[_row_spec(e, tr) for e in rows])
    widths = [e[1] if isinstance(e, tuple) else e.shape[1] for e in rows]
    nr, nc, nt = len(rows), len(consts), len(cts)

    def body(*refs):
        vals = [r[...].astype(F32) for r in refs[:nr + nc]]
        ct = [r[...].astype(F32) for r in refs[nr + nc:nr + nc + nt]]
        drow_refs = refs[nr + nc + nt:nr + nc + nt + n_diff]
        dconst_refs = refs[nr + nc + nt + n_diff:]
        rest_rows = vals[n_diff:nr]

        def f(diff_rows, cs):
            return tuple(fn(*diff_rows, *rest_rows, *cs))

        _, vjp = jax.vjp(f, vals[:n_diff], vals[nr:])
        drows, dcs = vjp(tuple(ct))
        for r, val in zip(drow_refs, drows):
            r[...] = val.astype(r.dtype)
        first_step = pl.program_id(0) == 0
        for r, val in zip(dconst_refs, dcs):
            @pl.when(first_step)
            def _(r=r, val=val):
                r[...] = val

            @pl.when(jnp.logical_not(first_step))
            def _(r=r, val=val):
                r[...] += val

    cspecs = [pl.BlockSpec(c.shape, lambda i, n=c.ndim: (0,) * n) for c in consts]
    ctspecs = [pl.BlockSpec((tr, c.shape[1]), lambda i: (i, 0)) for c in cts]
    out_shape = [jax.ShapeDtypeStruct((S, widths[j]), drow_dtypes[j]) for j in range(n_diff)]
    out_shape += [jax.ShapeDtypeStruct(c.shape, F32) for c in consts]
    out_specs = [pl.BlockSpec((tr, widths[j]), lambda i: (i, 0)) for j in range(n_diff)] + cspecs
    res = pl.pallas_call(
        body, name=name, out_shape=tuple(out_shape), grid=(S // tr,),
        in_specs=list(specs) + cspecs + ctspecs, out_specs=tuple(out_specs),
        compiler_params=_params(("arbitrary",)),
    )(*arrs, *consts, *cts)
    return list(res[:n_diff]), list(res[n_diff:])


def _mla_prep_fn(cq, ckv, kr, kr_sw, cos_q, sin_q, cos_k, sin_k, gq, gkv, wq_n, wq_r, wq_sw, wk, wv):
    hq = _rms(cq, gq)
    hkv = _rms(ckv, gkv)
    return (bdot(hq, wq_n), bdot(hq, wq_r) * cos_q + bdot(hq, wq_sw) * sin_q,
            bdot(hkv, wk), bdot(hkv, wv), kr * cos_k + kr_sw * sin_k)


def _merge_fn(g0, g1, g2, of, og, om, b0, b1, b2, wf, wg, wm):
    return (_sigmoid(g0 + b0) * bdot(of, wf) + _sigmoid(g1 + b1) * bdot(og, wg) + _sigmoid(g2 + b2) * bdot(om, wm),)


_IN_SIZES = (256, 256, 256, 4, 256, 256, 512, 16, 512, 256, 128, 32, 3072)
_IN_OFF = np.concatenate([[0], np.cumsum(_IN_SIZES)])
(_O_FQ, _O_FK, _O_FV, _O_FF, _O_GQ, _O_GK, _O_GV, _O_GLOW, _O_GR, _O_MQ, _O_MKV, _O_MKR, _O_ZG) = [int(o) for o in _IN_OFF[:-1]]
N_IN = int(_IN_OFF[-1])
_BIG_GROUPS = ((_O_ZG, 3072), (_O_GV, 512), (_O_GR, 512), (_O_FQ, 256), (_O_FK, 256), (_O_FV, 256),
               (_O_GQ, 256), (_O_GK, 256), (_O_MQ, 256), (_O_MKV, 128))
Z_GATE, Z_GV, Z_GR, Z_FQ, Z_FK, Z_FV, Z_GQ, Z_GK, Z_MQ, Z_MKV = [int(o) for o in
                                                                    np.concatenate([[0], np.cumsum([w for _, w in _BIG_GROUPS])])[:-1]]
N_BIG = sum(w for _, w in _BIG_GROUPS)
SM_FF, SM_GLOW, SM_KR, SM_KR_SW, N_SM = 0, 8, 32, 64, 128
N_PAD = N_BIG + N_SM
_HALF = MLA_ROPE // 2
_QK_HD = MLA_NOPE + MLA_ROPE


def _in_perm():
    idx = np.concatenate([np.arange(o, o + w) for o, w in _BIG_GROUPS] + [np.zeros(N_SM, np.int64)])
    sign = np.concatenate([np.ones(N_BIG), np.zeros(N_SM)])
    for src, dst, w in ((_O_FF, SM_FF, 4), (_O_GLOW, SM_GLOW, 16), (_O_MKR, SM_KR, 32)):
        idx[N_BIG + dst:N_BIG + dst + w] = np.arange(src, src + w)
        sign[N_BIG + dst:N_BIG + dst + w] = 1.0
    inv = np.zeros(N_IN, np.int64)
    inv[idx[sign > 0]] = np.nonzero(sign > 0)[0]
    sw = N_BIG + SM_KR_SW
    idx[sw:sw + _HALF] = np.arange(_O_MKR + _HALF, _O_MKR + MLA_ROPE)
    sign[sw:sw + _HALF] = -1.0
    idx[sw + _HALF:sw + MLA_ROPE] = np.arange(_O_MKR, _O_MKR + _HALF)
    sign[sw + _HALF:sw + MLA_ROPE] = 1.0
    inv2, sign2 = np.zeros(N_IN, np.int64), np.zeros(N_IN)
    inv2[idx[sw:sw + MLA_ROPE]] = np.arange(sw, sw + MLA_ROPE)
    sign2[idx[sw:sw + MLA_ROPE]] = sign[sw:sw + MLA_ROPE]
    return idx, sign.astype(np.float32), inv, inv2, sign2.astype(np.float32)


_IN_IDX, _IN_SIGN, _IN_INV, _IN_INV2, _IN_SIGN2 = _in_perm()


def _uq_perm():
    base = [h * _QK_HD for h in range(MLA_HEADS)]
    nope = np.concatenate([np.arange(b, b + MLA_NOPE) for b in base])
    rot = np.concatenate([np.arange(b + MLA_NOPE, b + _QK_HD) for b in base])
    sw = np.concatenate([np.concatenate([np.arange(b + MLA_NOPE + _HALF, b + _QK_HD), np.arange(b + MLA_NOPE, b + MLA_NOPE + _HALF)])
                         for b in base])
    sw_sign = np.tile(np.concatenate([-np.ones(_HALF), np.ones(_HALF)]), MLA_HEADS).astype(np.float32)
    return nope, rot, sw, sw_sign


_UQ_NOPE, _UQ_ROT, _UQ_SW, _UQ_SW_SIGN = _uq_perm()
_UKV_PERM = np.concatenate(
    [np.concatenate([np.arange(h * 128, h * 128 + MLA_NOPE) for h in range(MLA_HEADS)]),
     np.concatenate([np.arange(h * 128 + MLA_NOPE, (h + 1) * 128) for h in range(MLA_HEADS)])])
_UKV_INV = np.argsort(_UKV_PERM)


def _rope_tables(S):
    inv = ROPE_BASE ** (-jnp.arange(_HALF, dtype=F32) / _HALF)
    ang = jnp.arange(S, dtype=F32)[:, None] * inv[None, :]
    cos, sin = jnp.tile(jnp.cos(ang), (1, 2)), jnp.tile(jnp.sin(ang), (1, 2))
    return jnp.tile(cos, (1, MLA_HEADS)), jnp.tile(sin, (1, MLA_HEADS)), cos, sin


def _prep_layer(w, l):
    p = {}
    p['w_in'] = w['w_in'][l]
    p['wg'] = jnp.zeros((N_SM, GLA_W), BF16).at[SM_GLOW:SM_GLOW + GLA_RANK].set(w['w_gla_gate'][l])
    uq = w['w_mla_uq'][l]
    p['wq_n'], p['wq_r'], p['wq_sw'] = uq[:, _UQ_NOPE], uq[:, _UQ_ROT], uq[:, _UQ_SW] * _UQ_SW_SIGN.astype(BF16)
    ukv = w['w_mla_ukv'][l][:, _UKV_PERM]
    p['wk'], p['wv'] = ukv[:, :256], ukv[:, 256:]
    for n in ('w_up_fox', 'w_up_gla', 'w_up_mla', 'w_out', 'w_xq', 'w_xkv', 'w_xo', 'w_mlp1', 'w_mlp2',
              'g_mix', 'g_xa', 'g_mem', 'g_mlp'):
        p[n] = w[n][l]
    p['b_f'] = jnp.zeros((8, 1), F32).at[:FOX_HEADS, 0].set(w['b_fox_forget'][l])
    p['bg'] = w['b_gla_gate'][l].reshape(1, GLA_W)
    p['go'] = w['g_gla_out'][l].reshape(1, GLA_DV)
    p['gq'] = w['g_mla_q'][l].reshape(1, MLA_Q_RANK)
    p['gkv'] = w['g_mla_kv'][l].reshape(1, MLA_KV_RANK)
    p['b_gate'] = [w['b_branch_gate'][l][i * 1024:(i + 1) * 1024].reshape(1, 1024) for i in range(3)]
    return p


_GLA_COLS = (Z_GQ, Z_GK, Z_GV, Z_GR)
_MLA_OUTS = [(256, BF16), (128, BF16), (256, BF16), (256, BF16), (MLA_ROPE, BF16)]


def _mla_rows(z, zsm, rope):
    return [(z, 256, Z_MQ), (z, 128, Z_MKV), zsm[:, SM_KR:SM_KR + MLA_ROPE], zsm[:, SM_KR_SW:SM_KR_SW + MLA_ROPE], *rope]


def _mla_consts(p):
    return [p['gq'], p['gkv'], p['wq_n'], p['wq_r'], p['wq_sw'], p['wk'], p['wv']]


def _fox_qkv(z):
    return [((z, Z_FQ, 256), (z, Z_FK, 256), FOX_HD, False)], (z, Z_FV, 256)


def _mla_qkv(qn, qr, kn, vv, kr):
    return [((qn, 0, 256), (kn, 0, 256), MLA_NOPE, False), ((qr, 0, 128), (kr, 0, MLA_ROPE), MLA_ROPE, True)], (vv, 0, 256)


def _xa_qkv(qx, kvx):
    return [((qx, 0, 512), (kvx, 0, 512), XA_HD, False)], (kvx, 512, 512)


def _merge_rows(z, o_fox, o_gla, o_mla):
    return [(z, 1024, Z_GATE), (z, 1024, Z_GATE + 1024), (z, 1024, Z_GATE + 2048), o_fox, o_gla, o_mla]


def _merge_consts(p):
    return p['b_gate'] + [p['w_up_fox'], p['w_up_gla'], p['w_up_mla']]


def _layer_fwd(x0, mem, p, rope, l):
    S = x0.shape[0]
    sv = {'x0': x0}
    h1 = _rms_fwd(x0, p['g_mix'], name=f"rms_mix_{l}")
    z = _mm(h1, p['w_in'], mode='nn', out_dtype=BF16, b_cols=(0, N_BIG), name=f"in_big_{l}")
    zsm = _mm(h1, p['w_in'], mode='nn', out_dtype=F32, b_cols=(N_BIG, N_SM), name=f"in_small_{l}")
    sv.update(h1=h1, z=z, zsm=zsm)
    ff_t = jnp.zeros((8, S), F32).at[:FOX_HEADS].set(zsm[:, SM_FF:SM_FF + FOX_HEADS].T)
    cum_t = _fox_cum_fwd(ff_t, p['b_f'], name=f"fox_cum_{l}")
    cum = cum_t.T
    o_fox, lse_f = _attn_fwd(*_fox_qkv(z), FOX_HEADS, cum, cum_t, scale=FOX_HD ** -0.5, mask='causal', name=f"fox_fwd_{l}")
    sv.update(ff_t=ff_t, cum=cum, cum_t=cum_t, lse_f=lse_f, o_fox=o_fox)
    o_gla, states = _gla_fwd(z, zsm, p['wg'], p['bg'], p['go'], _GLA_COLS, name=f"gla_fwd_{l}")
    sv.update(o_gla=o_gla, states=states)
    mla = _stage_fwd(_mla_prep_fn, _mla_rows(z, zsm, rope), _mla_consts(p), _MLA_OUTS, name=f"mla_prep_{l}")
    o_mla, lse_m = _attn_fwd(*_mla_qkv(*mla), MLA_HEADS, None, None, scale=_QK_HD ** -0.5, mask='chunk', name=f"mla_fwd_{l}")
    sv.update(mla=mla, lse_m=lse_m, o_mla=o_mla)
    (y,) = _stage_fwd(_merge_fn, _merge_rows(z, o_fox, o_gla, o_mla), _merge_consts(p), [(1024, BF16)], name=f"merge_{l}")
    x1 = _mm(y, p['w_out'], mode='nn', out_dtype=F32, residual=x0, name=f"out_proj_{l}")
    sv.update(y=y, x1=x1)
    h2 = _rms_fwd(x1, p['g_xa'], name=f"rms_xa_{l}")
    hm = _rms_fwd(mem, p['g_mem'], name=f"rms_mem_{l}")
    qx = _mm(h2, p['w_xq'], mode='nn', out_dtype=BF16, name=f"xq_{l}")
    kvx = _mm(hm, p['w_xkv'], mode='nn', out_dtype=BF16, name=f"xkv_{l}")
    ox, lse_x = _attn_fwd(*_xa_qkv(qx, kvx), XA_HEADS, None, None, scale=XA_HD ** -0.5, mask=None, name=f"xa_fwd_{l}")
    x2 = _mm(ox, p['w_xo'], mode='nn', out_dtype=F32, residual=x1, name=f"xo_{l}")
    sv.update(h2=h2, hm=hm, qx=qx, kvx=kvx, lse_x=lse_x, ox=ox, x2=x2)
    h3 = _rms_fwd(x2, p['g_mlp'], name=f"rms_mlp_{l}")
    a = _mm(h3, p['w_mlp1'], mode='nn', out_dtype=BF16, name=f"mlp1_{l}")
    x3 = _mm(a, p['w_mlp2'], mode='nn', out_dtype=F32, act='relu2', residual=x2, name=f"mlp2_{l}")
    sv.update(h3=h3, a=a)
    return x3, sv


def _layer_bwd(dx3, dx3b, mem, p, rope, sv, l):
    S = dx3.shape[0]
    g = {}
    da = _mm(dx3b, p['w_mlp2'], mode='nt', out_dtype=BF16, drelu_of=sv['a'], name=f"d_mlp2_in_{l}")
    g['w_mlp2'] = _mm(sv['a'], dx3b, mode='tn', out_dtype=F32, act='relu2', name=f"d_w_mlp2_{l}")
    dh3 = _mm(da, p['w_mlp1'], mode='nt', out_dtype=F32, name=f"d_mlp1_in_{l}")
    g['w_mlp1'] = _mm(sv['h3'], da, mode='tn', out_dtype=F32, name=f"d_w_mlp1_{l}")
    dx2, dx2b, g['g_mlp'] = _rms_bwd(sv['x2'], p['g_mlp'], dh3, dx3, name=f"d_rms_mlp_{l}")
    dox = _mm(dx2b, p['w_xo'], mode='nt', out_dtype=BF16, name=f"d_xo_in_{l}")
    g['w_xo'] = _mm(sv['ox'], dx2b, mode='tn', out_dtype=F32, name=f"d_w_xo_{l}")
    (dqx,), (dkx,), dvx = _attn_bwd(*_xa_qkv(sv['qx'], sv['kvx']), XA_HEADS, sv['ox'], dox, sv['lse_x'], None, None,
                                    scale=XA_HD ** -0.5, mask=None, name=f"xa_bwd_{l}")
    dqx = dqx.astype(BF16)
    dkvx = jnp.concatenate([dkx, dvx], axis=1).astype(BF16)
    dh2 = _mm(dqx, p['w_xq'], mode='nt', out_dtype=F32, name=f"d_xq_in_{l}")
    g['w_xq'] = _mm(sv['h2'], dqx, mode='tn', out_dtype=F32, name=f"d_w_xq_{l}")
    dhm = _mm(dkvx, p['w_xkv'], mode='nt', out_dtype=F32, name=f"d_xkv_in_{l}")
    g['w_xkv'] = _mm(sv['hm'], dkvx, mode='tn', out_dtype=F32, name=f"d_w_xkv_{l}")
    _, _, g['g_mem'] = _rms_bwd(mem, p['g_mem'], dhm, None, name=f"d_rms_mem_{l}")
    dx1, dx1b, g['g_xa'] = _rms_bwd(sv['x1'], p['g_xa'], dh2, dx2, name=f"d_rms_xa_{l}")
    dy = _mm(dx1b, p['w_out'], mode='nt', out_dtype=F32, name=f"d_out_in_{l}")
    g['w_out'] = _mm(sv['y'], dx1b, mode='tn', out_dtype=F32, name=f"d_w_out_{l}")
    z, zsm = sv['z'], sv['zsm']
    (dg0, dg1, dg2, do_fox, do_gla, do_mla), (db0, db1, db2, g['w_up_fox'], g['w_up_gla'], g['w_up_mla']) = _stage_bwd(
        _merge_fn, _merge_rows(z, sv['o_fox'], sv['o_gla'], sv['o_mla']), _merge_consts(p), [dy], 6, [BF16] * 6,
        name=f"merge_bwd_{l}")
    g['b_branch_gate'] = jnp.concatenate([db0, db1, db2], axis=1).reshape(-1)
    (dfq,), (dfk,), dfv, dck, dcq = _attn_bwd(*_fox_qkv(z), FOX_HEADS, sv['o_fox'], do_fox, sv['lse_f'], sv['cum'], sv['cum_t'],
                                              scale=FOX_HD ** -0.5, mask='causal', name=f"fox_bwd_{l}")
    dff_t, db_f = _fox_cum_bwd(sv['ff_t'], p['b_f'], dck + dcq.T, name=f"fox_cum_bwd_{l}")
    g['b_fox_forget'] = db_f[:FOX_HEADS, 0]
    dgq, dgk, dgv, dgr, dzsm, dwg, dbg, dgo = _gla_bwd(z, zsm, p['wg'], p['bg'], p['go'], sv['states'], do_gla, _GLA_COLS,
                                                       name=f"gla_bwd_{l}")
    g['w_gla_gate'] = dwg[SM_GLOW:SM_GLOW + GLA_RANK]
    g['b_gla_gate'] = dbg.reshape(-1)
    g['g_gla_out'] = dgo.reshape(-1)
    (dmqn, dmqr), (dmkn, dmkr), dmv = _attn_bwd(*_mla_qkv(*sv['mla']), MLA_HEADS, sv['o_mla'], do_mla, sv['lse_m'], None, None,
                                                scale=_QK_HD ** -0.5, mask='chunk', name=f"mla_bwd_{l}")
    (dcq, dckv, dkr, dkr_sw), (dgq_n, dgkv_n, dwq_n, dwq_r, dwq_sw, dwk, dwv) = _stage_bwd(
        _mla_prep_fn, _mla_rows(z, zsm, rope), _mla_consts(p), [dmqn, dmqr, dmkn, dmv, dmkr], 4, [BF16, BF16, F32, F32],
        name=f"mla_prep_bwd_{l}")
    g['g_mla_q'] = dgq_n.reshape(-1)
    g['g_mla_kv'] = dgkv_n.reshape(-1)
    g['w_mla_uq'] = (jnp.zeros((MLA_Q_RANK, MLA_HEADS * _QK_HD), F32).at[:, _UQ_NOPE].set(dwq_n).at[:, _UQ_ROT].set(dwq_r)
                     .at[:, _UQ_SW].add(dwq_sw * _UQ_SW_SIGN))
    g['w_mla_ukv'] = jnp.concatenate([dwk, dwv], axis=1)[:, _UKV_INV]
    dz = jnp.concatenate([dg0, dg1, dg2, dgv, dgr, dfq.astype(BF16), dfk.astype(BF16), dfv.astype(BF16), dgq, dgk, dcq, dckv,
                          (dzsm + jnp.concatenate([dff_t[:FOX_HEADS].T, jnp.zeros((S, SM_KR - FOX_HEADS), F32), dkr, dkr_sw,
                                                   jnp.zeros((S, N_SM - SM_KR_SW - MLA_ROPE), F32)], axis=1)).astype(BF16)],
                         axis=1)
    dh1 = _mm(dz, p['w_in'], mode='nt', out_dtype=F32, tk=N_PAD // 2, name=f"d_in_{l}")
    g['w_in'] = _mm(sv['h1'], dz, mode='tn', out_dtype=F32, tm=512, tn=N_PAD // 2, tk=512, name=f"d_w_in_{l}")
    dx0, dx0b, g['g_mix'] = _rms_bwd(sv['x0'], p['g_mix'], dh1, dx1, name=f"d_rms_mix_{l}")
    for n in ('g_mlp', 'g_mem', 'g_xa', 'g_mix'):
        g[n] = g[n].reshape(-1)
    return dx0, dx0b, g


def _local_step(x, mem, target, w):
    S = x.shape[0]
    depth = w['g_mix'].shape[0]
    rope = _rope_tables(S)
    ps = [_prep_layer(w, l) for l in range(depth)]
    saved = []
    for l in range(depth):
        x, sv = _layer_fwd(x, mem, ps[l], rope, l)
        saved.append(sv)
    loss, dx, dxb, dgf = _loss_head(x, w['g_final'], target, name="loss_head")
    grads = [None] * depth
    for l in reversed(range(depth)):
        dx, dxb, grads[l] = _layer_bwd(dx, dxb, mem, ps[l], rope, saved[l], l)
    return loss, dx, grads, dgf.reshape(-1)


_MESH_AXES = ("x", "y", "c")
_HBM = pl.BlockSpec(memory_space=pl.ANY)


N_CHIP = 4


def _place():
    x, y, c = (lax.axis_index(n) for n in _MESH_AXES)
    return (x, y, c), (x, y, 1 - c), [(1 - x, y), (x, 1 - y), (1 - x, 1 - y)]


def _remote(src, dst, sems, k, to):
    return pltpu.make_async_remote_copy(src_ref=src, dst_ref=dst, send_sem=sems[0].at[k], recv_sem=sems[1].at[k],
                                        device_id=to, device_id_type=pl.DeviceIdType.MESH)


def _all_gather(x, *, name):
    def body(x_ref, o_ref, send_sems, recv_sems, local_sem):
        me, sib, chips = _place()
        c = me[2]
        sems = (send_sems, recv_sems)
        slot = lambda px, py, pc: o_ref.at[4 * px + 2 * py + pc]
        mine = pltpu.make_async_copy(x_ref, slot(*me), local_sem)
        mine.start()
        first = [_remote(x_ref, slot(*me), sems, 0, sib)]
        first += [_remote(x_ref, slot(*me), sems, 1 + j, (*chip, c)) for j, chip in enumerate(chips)]
        for cp in first:
            cp.start()
        passed = [_remote(slot(*chip, c), slot(*chip, c), sems, 4 + j, sib) for j, chip in enumerate(chips)]
        for j, chip in enumerate(chips):
            _remote(x_ref, slot(*chip, c), sems, 1 + j, me).wait_recv()
            passed[j].start()
        _remote(x_ref, slot(*sib), sems, 0, me).wait_recv()
        for j, chip in enumerate(chips):
            _remote(x_ref, slot(*chip, 1 - c), sems, 4 + j, me).wait_recv()
        for cp in first + passed:
            cp.wait_send()
        mine.wait()

    return pl.pallas_call(
        body, name=name, out_shape=jax.ShapeDtypeStruct((N_DEV,) + x.shape, x.dtype),
        in_specs=[_HBM], out_specs=_HBM,
        scratch_shapes=[pltpu.SemaphoreType.DMA((N_DEV - 1,)), pltpu.SemaphoreType.DMA((N_DEV - 1,)), pltpu.SemaphoreType.DMA],
        compiler_params=pltpu.CompilerParams(has_side_effects=True),
    )(x)


def _gather_weights(shards, axes, *, name):
    n = len(shards)
    srcs, out_shapes, kinds = [], [], []
    for s, ax in zip(shards, axes):
        L, a, b = s.shape
        if ax == 1:
            srcs.append(s.reshape(L, 1, a, b)), out_shapes.append((L, N_DEV, a, b)), kinds.append('row')
        elif b % 128 == 0:
            srcs.append(s), out_shapes.append((L, a, N_DEV * b)), kinds.append('col')
        else:
            srcs.append(s.reshape(1, L, a, b)), out_shapes.append((N_DEV, L, a, b)), kinds.append('slot')

    def body(*refs):
        x_refs, o_refs = refs[:n], refs[n:2 * n]
        send_sems, recv_sems, local_sem = refs[2 * n:]
        me, sib, chips = _place()
        c = me[2]
        sems = (send_sems, recv_sems)

        def win(t, px, py, pc):
            idx = 4 * px + 2 * py + pc
            if kinds[t] == 'row':
                return o_refs[t].at[:, pl.ds(idx, 1)]
            if kinds[t] == 'col':
                width = shards[t].shape[2]
                return o_refs[t].at[:, :, pl.ds(pl.multiple_of(idx * width, 128), width)]
            return o_refs[t].at[pl.ds(idx, 1)]

        def group(k, block, to, own):
            return [_remote(x_refs[t] if own else win(t, *block), win(t, *block), sems, k * n + t, to) for t in range(n)]

        mine = [pltpu.make_async_copy(x_refs[t], win(t, *me), local_sem.at[t]) for t in range(n)]
        first = group(0, me, sib, True)
        for j, chip in enumerate(chips):
            first += group(1 + j, me, (*chip, c), True)
        for cp in mine + first:
            cp.start()
        passed = []
        for j, chip in enumerate(chips):
            for cp in group(1 + j, (*chip, c), me, False):
                cp.wait_recv()
            fwd = group(4 + j, (*chip, c), sib, False)
            for cp in fwd:
                cp.start()
            passed += fwd
        for cp in group(0, sib, me, False):
            cp.wait_recv()
        for j, chip in enumerate(chips):
            for cp in group(4 + j, (*chip, 1 - c), me, False):
                cp.wait_recv()
        for cp in first + passed:
            cp.wait_send()
        for cp in mine:
            cp.wait()

    outs = pl.pallas_call(
        body, name=name, out_shape=tuple(jax.ShapeDtypeStruct(shp, s.dtype) for shp, s in zip(out_shapes, shards)),
        in_specs=[_HBM] * n, out_specs=(_HBM,) * n,
        scratch_shapes=[pltpu.SemaphoreType.DMA(((N_DEV - 1) * n,)), pltpu.SemaphoreType.DMA(((N_DEV - 1) * n,)),
                        pltpu.SemaphoreType.DMA((n,))],
        compiler_params=pltpu.CompilerParams(has_side_effects=True),
    )(*srcs)
    whole = []
    for o, s, kind in zip(outs, shards, kinds):
        L, a, b = s.shape
        whole.append(o.reshape(L, N_DEV * a, b) if kind == 'row' else o if kind == 'col' else _to_whole(o, 2))
    return whole


def _sibling_swap(x, *, name):
    def body(x_ref, o_ref, send_sems, recv_sems):
        me, sib, _ = _place()
        c = me[2]
        sems = (send_sems, recv_sems)
        sends = [_remote(x_ref.at[j, 1 - c], o_ref.at[j], sems, j, sib) for j in range(N_CHIP)]
        for cp in sends:
            cp.start()
        for cp in sends:
            cp.wait_send()
            cp.wait_recv()

    return pl.pallas_call(
        body, name=name, out_shape=jax.ShapeDtypeStruct((N_CHIP,) + x.shape[2:], x.dtype),
        in_specs=[_HBM], out_specs=_HBM,
        scratch_shapes=[pltpu.SemaphoreType.DMA((N_CHIP,)), pltpu.SemaphoreType.DMA((N_CHIP,))],
        compiler_params=pltpu.CompilerParams(has_side_effects=True),
    )(x)


def _pair_sum(x, got, c, *, name):
    _, _, R, _ = x.shape
    tr = _pick(R, (1024, 512, 256, 128, 64, 32, 16, 8))

    def body(c_ref, x_ref, g_ref, o_ref):
        o_ref[...] = (x_ref[...].astype(F32) + g_ref[...].astype(F32)).astype(o_ref.dtype)

    return pl.pallas_call(
        body, name=name, out_shape=jax.ShapeDtypeStruct((N_CHIP, R, 128), x.dtype),
        grid_spec=pltpu.PrefetchScalarGridSpec(
            num_scalar_prefetch=1, grid=(N_CHIP, R // tr),
            in_specs=[pl.BlockSpec((None, None, tr, 128), lambda j, i, c_ref: (j, c_ref[0], i, 0)),
                      pl.BlockSpec((None, tr, 128), lambda j, i, c_ref: (j, i, 0))],
            out_specs=pl.BlockSpec((None, tr, 128), lambda j, i, c_ref: (j, i, 0))),
        compiler_params=_params(("parallel", "parallel")),
    )(c, x, got)


def _chip_all_to_all(x, *, name):
    def body(x_ref, o_ref, send_sems, recv_sems, local_sem):
        me, _, chips = _place()
        c = me[2]
        sems = (send_sems, recv_sems)
        mine = 2 * me[0] + me[1]
        local = pltpu.make_async_copy(x_ref.at[mine], o_ref.at[mine], local_sem)
        local.start()
        sends = [_remote(x_ref.at[2 * px + py], o_ref.at[mine], sems, j, (px, py, c)) for j, (px, py) in enumerate(chips)]
        for cp in sends:
            cp.start()
        for j, (px, py) in enumerate(chips):
            sends[j].wait_send()
            _remote(x_ref.at[mine], o_ref.at[2 * px + py], sems, j, me).wait_recv()
        local.wait()

    return pl.pallas_call(
        body, name=name, out_shape=jax.ShapeDtypeStruct(x.shape, x.dtype),
        in_specs=[_HBM], out_specs=_HBM,
        scratch_shapes=[pltpu.SemaphoreType.DMA((N_CHIP - 1,)), pltpu.SemaphoreType.DMA((N_CHIP - 1,)), pltpu.SemaphoreType.DMA],
        compiler_params=pltpu.CompilerParams(has_side_effects=True),
    )(x)


def _sum_slots(x, *, name):
    n, R, _ = x.shape
    tr = _pick(R, (1024, 512, 256, 128, 64, 32, 16, 8))

    def body(x_ref, o_ref):
        acc = x_ref[0].astype(F32)
        for j in range(1, n):
            acc = acc + x_ref[j].astype(F32)
        o_ref[...] = acc

    return pl.pallas_call(
        body, name=name, out_shape=jax.ShapeDtypeStruct((R, 128), F32), grid=(R // tr,),
        in_specs=[pl.BlockSpec((n, tr, 128), lambda i: (0, i, 0))], out_specs=pl.BlockSpec((tr, 128), lambda i: (i, 0)),
        compiler_params=_params(("parallel",)),
    )(x)


def _adamw(w, g, m, v, *, name):
    shape = w.shape
    cols = shape[-1]
    rows = int(np.prod(shape[:-1]))
    tr = next((t for t in (1024, 512, 256, 128, 64, 32, 16, 8) if rows % t == 0 and t * cols * 4 <= (1 << 20)), rows)

    def body(w_ref, g_ref, m_ref, v_ref, d_ref, mo_ref, vo_ref):
        g_ = g_ref[...]
        m_ = ADAM_B1 * m_ref[...] + (1.0 - ADAM_B1) * g_
        v_ = ADAM_B2 * v_ref[...] + (1.0 - ADAM_B2) * jnp.square(g_)
        m_hat = m_ / (1.0 - ADAM_B1 ** ADAM_STEP)
        v_hat = v_ / (1.0 - ADAM_B2 ** ADAM_STEP)
        d_ref[...] = -ADAM_LR * (m_hat / (jnp.sqrt(v_hat) + ADAM_EPS) + ADAM_WD * w_ref[...])
        mo_ref[...] = m_
        vo_ref[...] = v_

    blk = pl.BlockSpec((tr, cols), lambda i: (i, 0))
    outs = pl.pallas_call(
        body, name=name, out_shape=tuple(jax.ShapeDtypeStruct((rows, cols), F32) for _ in range(3)), grid=(rows // tr,),
        in_specs=[blk] * 4, out_specs=(blk,) * 3, compiler_params=_params(("parallel",)),
    )(*(a.reshape(rows, cols) for a in (w, g, m, v)))
    return tuple(o.reshape(shape) for o in outs)


_WEIGHTS = ('g_mix', 'w_in', 'b_fox_forget', 'w_gla_gate', 'b_gla_gate', 'g_gla_out', 'g_mla_q', 'w_mla_uq', 'g_mla_kv',
            'w_mla_ukv', 'b_branch_gate', 'w_up_fox', 'w_up_gla', 'w_up_mla', 'w_out', 'g_xa', 'g_mem', 'w_xq', 'w_xkv',
            'w_xo', 'g_mlp', 'w_mlp1', 'w_mlp2', 'g_final')
_SHARDED = (('w_in', 1), ('w_gla_gate', 2), ('w_mla_uq', 2), ('w_mla_ukv', 2), ('w_up_fox', 2), ('w_up_gla', 2),
            ('w_up_mla', 2), ('w_out', 1), ('w_xq', 1), ('w_xkv', 1), ('w_xo', 2), ('w_mlp1', 2), ('w_mlp2', 1))
_REPLICATED = tuple(n for n in _WEIGHTS if n not in dict(_SHARDED))
_ROW_PAD = 1024
_SMALL_ROW_PAD = 8


def _pack(flats, lead, row_pad=_ROW_PAD):
    if all(int(np.prod(a.shape[lead:])) % 128 == 0 for a in flats):
        cat = jnp.concatenate([a.reshape(a.shape[:lead] + (-1, 128)) for a in flats], axis=lead)
        rows = cat.shape[lead]
        return jnp.pad(cat, [(0, 0)] * lead + [(0, -(-rows // row_pad) * row_pad - rows), (0, 0)])
    cat = jnp.concatenate([a.reshape(a.shape[:lead] + (-1,)) for a in flats], axis=-1)
    n = cat.shape[-1]
    total = -(-n // (128 * row_pad)) * (128 * row_pad)
    cat = jnp.pad(cat, [(0, 0)] * lead + [(0, total - n)])
    return cat.reshape(cat.shape[:lead] + (total // 128, 128))


def _unpack(buf, shapes, lead):
    sizes = [int(np.prod(shp)) for shp in shapes]
    out, off = [], 0
    if all(n % 128 == 0 for n in sizes):
        for shp, n in zip(shapes, sizes):
            rows = buf[(slice(None),) * lead + (slice(off // 128, (off + n) // 128),)]
            out.append(rows.reshape(buf.shape[:lead] + tuple(shp)))
            off += n
        return out
    flat = buf.reshape(buf.shape[:lead] + (-1,))
    for shp, n in zip(shapes, sizes):
        out.append(flat[..., off:off + n].reshape(buf.shape[:lead] + tuple(shp)))
        off += n
    return out


def _to_whole(g, axis):
    if axis == 1:
        return g.transpose(1, 0, 2, 3).reshape(g.shape[1], N_DEV * g.shape[2], g.shape[3])
    return g.transpose(1, 2, 0, 3).reshape(g.shape[1], g.shape[2], N_DEV * g.shape[3])


def _to_shards(w, axis):
    L, R, C = w.shape
    if axis == 1:
        return w.reshape(L, N_DEV, R // N_DEV, C).transpose(1, 0, 2, 3)
    return w.reshape(L, R, N_DEV, C // N_DEV).transpose(2, 0, 1, 3)


def kernel(x, mem, g_mix, w_in, b_fox_forget, w_gla_gate, b_gla_gate, g_gla_out, g_mla_q, w_mla_uq, g_mla_kv, w_mla_ukv, b_branch_gate, w_up_fox, w_up_gla, w_up_mla, w_out, g_xa, g_mem, w_xq, w_xkv, w_xo, g_mlp, w_mlp1, w_mlp2, g_final, loss_target, m_g_mix, m_w_in, m_b_fox_forget, m_w_gla_gate, m_b_gla_gate, m_g_gla_out, m_g_mla_q, m_w_mla_uq, m_g_mla_kv, m_w_mla_ukv, m_b_branch_gate, m_w_up_fox, m_w_up_gla, m_w_up_mla, m_w_out, m_g_xa, m_g_mem, m_w_xq, m_w_xkv, m_w_xo, m_g_mlp, m_w_mlp1, m_w_mlp2, m_g_final, v_g_mix, v_w_in, v_b_fox_forget, v_w_gla_gate, v_b_gla_gate, v_g_gla_out, v_g_mla_q, v_w_mla_uq, v_g_mla_kv, v_w_mla_ukv, v_b_branch_gate, v_w_up_fox, v_w_up_gla, v_w_up_mla, v_w_out, v_g_xa, v_g_mem, v_w_xq, v_w_xkv, v_w_xo, v_g_mlp, v_w_mlp1, v_w_mlp2, v_g_final):
    wts = dict(zip(_WEIGHTS, (g_mix, w_in, b_fox_forget, w_gla_gate, b_gla_gate, g_gla_out, g_mla_q, w_mla_uq, g_mla_kv,
                              w_mla_ukv, b_branch_gate, w_up_fox, w_up_gla, w_up_mla, w_out, g_xa, g_mem, w_xq, w_xkv, w_xo,
                              g_mlp, w_mlp1, w_mlp2, g_final)))
    mom1 = dict(zip(_WEIGHTS, (m_g_mix, m_w_in, m_b_fox_forget, m_w_gla_gate, m_b_gla_gate, m_g_gla_out, m_g_mla_q,
                               m_w_mla_uq, m_g_mla_kv, m_w_mla_ukv, m_b_branch_gate, m_w_up_fox, m_w_up_gla, m_w_up_mla,
                               m_w_out, m_g_xa, m_g_mem, m_w_xq, m_w_xkv, m_w_xo, m_g_mlp, m_w_mlp1, m_w_mlp2, m_g_final)))
    mom2 = dict(zip(_WEIGHTS, (v_g_mix, v_w_in, v_b_fox_forget, v_w_gla_gate, v_b_gla_gate, v_g_gla_out, v_g_mla_q,
                               v_w_mla_uq, v_g_mla_kv, v_w_mla_ukv, v_b_branch_gate, v_w_up_fox, v_w_up_gla, v_w_up_mla,
                               v_w_out, v_g_xa, v_g_mem, v_w_xq, v_w_xkv, v_w_xo, v_g_mlp, v_w_mlp1, v_w_mlp2, v_g_final)))
    depth = g_mix.shape[0]

    shard = {n: wts[n] for n, _ in _SHARDED}
    shard['w_in'] = w_in[:, :, _IN_IDX] * _IN_SIGN
    shard_shapes = [shard[n].shape for n, _ in _SHARDED]
    gathered = _gather_weights([shard[n].astype(BF16) for n, _ in _SHARDED], [ax for _, ax in _SHARDED], name="gather_weights")
    whole = dict(zip([n for n, _ in _SHARDED], gathered))
    whole.update({n: wts[n] for n in _REPLICATED})

    loss, dx, grads, dg_final = _local_step(x[0], mem[0], loss_target[0], whole)
    loss = lax.psum(loss[0, 0], _MESH_AXES)

    slots = _pack([_to_shards(jnp.stack([grads[l][n] for l in range(depth)]), ax).astype(BF16) for n, ax in _SHARDED], 1)
    slots = slots.reshape((N_CHIP, 2) + slots.shape[1:])
    core = lax.axis_index("c").astype(jnp.int32).reshape(1)
    paired = _pair_sum(slots, _sibling_swap(slots, name="swap_grads"), core, name="pair_grads")
    summed = _sum_slots(_chip_all_to_all(paired, name="scatter_grads"), name="sum_grads")
    grad = dict(zip([n for n, _ in _SHARDED], _unpack(summed, shard_shapes, 0)))
    grad['w_in'] = grad['w_in'][:, :, _IN_INV] + grad['w_in'][:, :, _IN_INV2] * _IN_SIGN2
    small = [dg_final if n == 'g_final' else jnp.stack([grads[l][n] for l in range(depth)]) for n in _REPLICATED]
    small_shapes = [wts[n].shape for n in _REPLICATED]
    small_sum = _sum_slots(_all_gather(_pack(small, 0, _SMALL_ROW_PAD), name="gather_small_grads"), name="sum_small_grads")
    grad.update(dict(zip(_REPLICATED, _unpack(small_sum, small_shapes, 0))))

    delta, new_m, new_v = {}, {}, {}
    for n, _ in _SHARDED:
        delta[n], new_m[n], new_v[n] = _adamw(wts[n], grad[n], mom1[n], mom2[n], name=f"adamw_{n}")
    packed = [_pack([d[n] for n in _REPLICATED], 0, _SMALL_ROW_PAD) for d in (wts, mom1, mom2)]
    outs = _adamw(packed[0], small_sum, packed[1], packed[2], name="adamw_small")
    for d, o in zip((delta, new_m, new_v), outs):
        d.update(dict(zip(_REPLICATED, _unpack(o, small_shapes, 0))))

    return (loss, dx[None], *[grad[n] for n in _WEIGHTS], *[delta[n] for n in _WEIGHTS],
            *[new_m[n] for n in _WEIGHTS], *[new_v[n] for n in _WEIGHTS])
```

```python
import jax
import jax.numpy as jnp
import numpy as np
from jax import lax
from jax.experimental import pallas as pl
from jax.experimental.pallas import tpu as pltpu

F32 = jnp.float32
BF16 = jnp.bfloat16

EPS = 1e-6
CHUNK = 64
FOX_HEADS, FOX_HD = 4, 64
GLA_HEADS, GLA_DK, GLA_DV, GLA_RANK, GLA_TAU = 4, 64, 128, 16, 16.0
MLA_HEADS, MLA_Q_RANK, MLA_KV_RANK, MLA_NOPE, MLA_ROPE, MLA_VD = 4, 256, 128, 64, 32, 64
ROPE_BASE = 10000.0
XA_HEADS, XA_HD = 4, 128
ADAM_LR, ADAM_B1, ADAM_B2, ADAM_EPS, ADAM_WD, ADAM_STEP = 0.001, 0.9, 0.999, 1e-08, 0.01, 10

N_DEV = 8
V7X_VMEM_LIMIT = 56 * 1024 * 1024
NEG = -1e30

NN = ((1,), (0,))
NT = ((1,), (1,))
TN = ((0,), (0,))


def _dot(a, b, dims):
    return lax.dot_general(a.astype(BF16), b.astype(BF16), (dims, ((), ())), preferred_element_type=F32)


@jax.custom_vjp
def bdot(a, b):
    return _dot(a, b, NN)


bdot.defvjp(lambda a, b: (_dot(a, b, NN), (a, b)),
            lambda res, g: (_dot(g, res[1], NT), _dot(res[0], g, TN)))


@jax.custom_vjp
def bdot_nt(a, b):
    return _dot(a, b, NT)


bdot_nt.defvjp(lambda a, b: (_dot(a, b, NT), (a, b)),
               lambda res, g: (_dot(g, res[1], NN), _dot(g, res[0], TN)))


@jax.custom_vjp
def bdot_tn(a, b):
    return _dot(a, b, TN)


bdot_tn.defvjp(lambda a, b: (_dot(a, b, TN), (a, b)),
               lambda res, g: (_dot(res[1], g, NT), _dot(res[0], g, NN)))


def _split2(x):
    hi = x.astype(BF16)
    lo = (x - hi.astype(F32)).astype(BF16)
    return hi, lo


def _tri(n, lower):
    r = lax.broadcasted_iota(jnp.int32, (n, n), 0)
    c = lax.broadcasted_iota(jnp.int32, (n, n), 1)
    return jnp.where((r >= c) if lower else (r <= c), 1.0, 0.0).astype(BF16)


def _tri_dot2(x, lower):
    hi, lo = _split2(x)
    t = _tri(x.shape[0], lower)
    return _dot(t, hi, NN) + _dot(t, lo, NN)


@jax.custom_vjp
def chunk_cumsum(x):
    return _tri_dot2(x, True)


chunk_cumsum.defvjp(lambda x: (_tri_dot2(x, True), None), lambda _, g: (_tri_dot2(g, False),))


def _log_sigmoid(x):
    return jnp.minimum(x, 0.0) - jnp.log(1.0 + jnp.exp(-jnp.abs(x)))


def _sigmoid(x):
    return 1.0 / (1.0 + jnp.exp(-x))


def _rms(x, g):
    return x * lax.rsqrt(jnp.mean(x * x, axis=-1, keepdims=True) + EPS) * g


def _pick(dim, prefs):
    for p in prefs:
        if dim % p == 0:
            return p
    return dim


def _params(sem):
    return pltpu.CompilerParams(dimension_semantics=sem, vmem_limit_bytes=V7X_VMEM_LIMIT)


def _mm(a, b, *, mode, out_dtype, name, act=None, residual=None, drelu_of=None, b_cols=None, tm=None, tn=None, tk=None):
    b_off, b_width = b_cols or (0, b.shape[1])
    if mode == 'nn':
        (M, K), N = a.shape, b_width
    elif mode == 'nt':
        (M, K), N = a.shape, b.shape[0]
    else:
        (K, M), N = a.shape, b_width
    tm = tm or _pick(M, (1024, 512, 256, 128))
    tn = tn or _pick(N, (1024, 1920, 1152, 768, 640, 512, 384, 256, 128))
    tk = tk or _pick(K, (1024, 1920, 1152, 640, 512, 256, 128))
    nk = K // tk
    dims = {'nn': NN, 'nt': NT, 'tn': TN}[mode]
    a_spec = pl.BlockSpec((tk, tm), lambda i, j, k: (k, i)) if mode == 'tn' else pl.BlockSpec((tm, tk), lambda i, j, k: (i, k))
    if mode == 'nt':
        b_spec = pl.BlockSpec((tn, tk), lambda i, j, k, o=b_off // tk: (j, k + o))
    else:
        b_spec = pl.BlockSpec((tk, tn), lambda i, j, k, o=b_off // tn: (k, j + o))
    o_spec = pl.BlockSpec((tm, tn), lambda i, j, k: (i, j))
    extra = [e for e in (residual, drelu_of) if e is not None]

    def body(a_ref, b_ref, *rest):
        o_ref = rest[len(extra)]
        at = a_ref[...]
        if act == 'relu2':
            at = jnp.square(jnp.maximum(at.astype(F32), 0.0))
        part = _dot(at, b_ref[...], dims)

        def finish(acc):
            idx = 0
            if residual is not None:
                acc = acc + rest[idx][...]
                idx += 1
            if drelu_of is not None:
                acc = acc * (2.0 * jnp.maximum(rest[idx][...].astype(F32), 0.0))
            o_ref[...] = acc.astype(out_dtype)

        if nk == 1:
            finish(part)
        else:
            acc_ref = rest[len(extra) + 1]
            k = pl.program_id(2)

            @pl.when(k == 0)
            def _():
                acc_ref[...] = part

            @pl.when(k > 0)
            def _():
                acc_ref[...] += part

            @pl.when(k == nk - 1)
            def _():
                finish(acc_ref[...])

    return pl.pallas_call(
        body, name=name,
        out_shape=jax.ShapeDtypeStruct((M, N), out_dtype),
        grid=(M // tm, N // tn, nk),
        in_specs=[a_spec, b_spec] + [o_spec] * len(extra),
        out_specs=o_spec,
        scratch_shapes=[] if nk == 1 else [pltpu.VMEM((tm, tn), F32)],
        compiler_params=_params(("parallel", "parallel", "arbitrary")),
    )(a, b, *extra)


def _rms_fwd(x, g, *, name, out_dtype=BF16):
    S, D = x.shape
    tr = _pick(S, (512, 256, 128))

    def body(x_ref, g_ref, o_ref):
        o_ref[...] = _rms(x_ref[...], g_ref[...]).astype(out_dtype)

    return pl.pallas_call(
        body, name=name, out_shape=jax.ShapeDtypeStruct((S, D), out_dtype), grid=(S // tr,),
        in_specs=[pl.BlockSpec((tr, D), lambda i: (i, 0)), pl.BlockSpec((1, D), lambda i: (0, 0))],
        out_specs=pl.BlockSpec((tr, D), lambda i: (i, 0)),
        compiler_params=_params(("parallel",)),
    )(x, g.reshape(1, D))


def _rms_bwd(x, g, dy, dres, *, name):
    S, D = x.shape
    tr = _pick(S, (512, 256, 128))

    def body(x_ref, g_ref, dy_ref, *rest):
        dx_ref, dxb_ref, dg_ref = rest[-3], rest[-2], rest[-1]
        x_ = x_ref[...]
        rstd = lax.rsqrt(jnp.mean(x_ * x_, axis=-1, keepdims=True) + EPS)
        xh = x_ * rstd
        dy_ = dy_ref[...].astype(F32)
        gdy = dy_ * g_ref[...]
        dx = (gdy - xh * jnp.mean(gdy * xh, axis=-1, keepdims=True)) * rstd
        if dres is not None:
            dx = dx + rest[0][...]
        dx_ref[...] = dx
        dxb_ref[...] = dx.astype(BF16)
        part = jnp.sum(dy_ * xh, axis=0, keepdims=True)

        @pl.when(pl.program_id(0) == 0)
        def _():
            dg_ref[...] = part

        @pl.when(pl.program_id(0) > 0)
        def _():
            dg_ref[...] += part

    row = pl.BlockSpec((tr, D), lambda i: (i, 0))
    vec = pl.BlockSpec((1, D), lambda i: (0, 0))
    ins = [x, g.reshape(1, D), dy] + ([dres] if dres is not None else [])
    return pl.pallas_call(
        body, name=name,
        out_shape=(jax.ShapeDtypeStruct((S, D), F32), jax.ShapeDtypeStruct((S, D), BF16), jax.ShapeDtypeStruct((1, D), F32)),
        grid=(S // tr,),
        in_specs=[row, vec, row] + ([row] if dres is not None else []),
        out_specs=(row, row, vec),
        compiler_params=_params(("arbitrary",)),
    )(*ins)


def _loss_head(x, g, target, *, name):
    S, D = x.shape
    tr = _pick(S, (512, 256, 128))

    def body(x_ref, g_ref, t_ref, l_ref, dx_ref, dxb_ref, dg_ref):
        x_ = x_ref[...]
        g_ = g_ref[...]
        rstd = lax.rsqrt(jnp.mean(x_ * x_, axis=-1, keepdims=True) + EPS)
        xh = x_ * rstd
        err = xh * g_ - t_ref[...]
        lpart = (0.5 / D) * jnp.sum(jnp.sum(err * err, axis=-1, keepdims=True), axis=0, keepdims=True)
        dy = err * (1.0 / D)
        gdy = dy * g_
        dx = (gdy - xh * jnp.mean(gdy * xh, axis=-1, keepdims=True)) * rstd
        dx_ref[...] = dx
        dxb_ref[...] = dx.astype(BF16)
        gpart = jnp.sum(dy * xh, axis=0, keepdims=True)

        @pl.when(pl.program_id(0) == 0)
        def _():
            dg_ref[...] = gpart
            l_ref[...] = lpart

        @pl.when(pl.program_id(0) > 0)
        def _():
            dg_ref[...] += gpart
            l_ref[...] += lpart

    row = pl.BlockSpec((tr, D), lambda i: (i, 0))
    vec = pl.BlockSpec((1, D), lambda i: (0, 0))
    return pl.pallas_call(
        body, name=name,
        out_shape=(jax.ShapeDtypeStruct((1, 1), F32), jax.ShapeDtypeStruct((S, D), F32), jax.ShapeDtypeStruct((S, D), BF16),
                   jax.ShapeDtypeStruct((1, D), F32)),
        grid=(S // tr,),
        in_specs=[row, vec, row],
        out_specs=(pl.BlockSpec((1, 1), lambda i: (0, 0)), row, row, vec),
        compiler_params=_params(("arbitrary",)),
    )(x, g.reshape(1, D), target)


def _mask_of(mask, tq, tk):
    qpos = lax.broadcasted_iota(jnp.int32, (tq, tk), 0)
    kpos = lax.broadcasted_iota(jnp.int32, (tq, tk), 1)
    if mask == 'causal':
        return kpos <= qpos
    return kpos <= (qpos | (CHUNK - 1))


def _col_block(entry, rows, idx):
    arr, off, width = entry
    return pl.BlockSpec((rows, width), lambda i, j, o=off // width: (idx(i, j), o))


def _attn_fwd(qk, v, H, cq, ck, *, scale, mask, name, rider=None):
    Sq, Sk = qk[0][0][0].shape[0], v[0].shape[0]
    dv = v[2] // H
    tq = _pick(Sq, (512, 256, 128))
    tk = tq if mask else _pick(Sk, (512, 256, 128))
    nq, nk = Sq // tq, Sk // tk
    bias = cq is not None
    npart = len(qk)

    def body(*refs):
        refs = split(refs)
        q_refs, k_refs = refs[0:2 * npart:2], refs[1:2 * npart:2]
        v_ref = refs[2 * npart]
        cq_ref, ck_ref = (refs[2 * npart + 1], refs[2 * npart + 2]) if bias else (None, None)
        o_ref, lse_ref, m_s, l_s, acc_s = refs[-5:]
        qi, ki = pl.program_id(0), pl.program_id(1)

        @pl.when(ki == 0)
        def _():
            m_s[...] = jnp.full(m_s.shape, NEG, F32)
            l_s[...] = jnp.zeros(l_s.shape, F32)
            acc_s[...] = jnp.zeros(acc_s.shape, F32)

        def compute(masked):
            keep = _mask_of(mask, tq, tk) if masked else None
            for h in range(H):
                s = None
                for (_, _, w, shared), q_ref, k_ref in zip(qk, q_refs, k_refs):
                    part = _dot(q_ref[:, h * w:(h + 1) * w], k_ref[...] if shared else k_ref[:, h * w:(h + 1) * w], NT)
                    s = part if s is None else s + part
                s = s * scale
                if bias:
                    s = s + (cq_ref[:, h:h + 1] - ck_ref[h:h + 1, :])
                if masked:
                    s = jnp.where(keep, s, NEG)
                m_prev = m_s[h]
                m_new = jnp.maximum(m_prev, jnp.max(s, axis=1, keepdims=True))
                alpha = jnp.exp(m_prev - m_new)
                p = jnp.exp(s - m_new)
                l_s[h] = alpha * l_s[h] + jnp.sum(p, axis=1, keepdims=True)
                acc_s[h] = alpha * acc_s[h] + _dot(p, v_ref[:, h * dv:(h + 1) * dv], NN)
                m_s[h] = m_new

        if mask is None:
            compute(False)
        else:
            pl.when(ki < qi)(lambda: compute(False))
            pl.when(ki == qi)(lambda: compute(True))

        @pl.when(ki == ((nk - 1) if mask is None else qi))
        def _():
            lse_ref[...] = jnp.zeros(lse_ref.shape, F32)
            for h in range(H):
                o_ref[:, h * dv:(h + 1) * dv] = (acc_s[h] / l_s[h]).astype(BF16)
                lse_ref[:, h:h + 1] = m_s[h] + jnp.log(l_s[h])

    q_idx = lambda i, j: i
    k_idx = (lambda i, j: jnp.minimum(i, j)) if mask else (lambda i, j: j)
    ins, in_specs = [], []
    for q_e, k_e, _, _ in qk:
        ins += [q_e[0], k_e[0]]
        in_specs += [_col_block(q_e, tq, q_idx), _col_block(k_e, tk, k_idx)]
    ins.append(v[0])
    in_specs.append(_col_block(v, tk, k_idx))
    if bias:
        in_specs += [pl.BlockSpec((tq, 8), lambda i, j: (i, 0)), pl.BlockSpec((8, tk), lambda i, j: (0, k_idx(i, j)))]
        ins += [cq, ck]
    r_ins, r_in_specs, r_outs, r_out_specs, r_scratch, split = _carry(
        rider, len(ins), 2, lambda: (pl.program_id(0) == 0) & (pl.program_id(1) == 0),
        lambda: (pl.program_id(0) == nq - 1) & (pl.program_id(1) == nk - 1))
    res = pl.pallas_call(
        body, name=name,
        out_shape=(jax.ShapeDtypeStruct((Sq, H * dv), BF16), jax.ShapeDtypeStruct((Sq, 8), F32), *r_outs),
        grid=(nq, nk), in_specs=in_specs + r_in_specs,
        out_specs=(pl.BlockSpec((tq, H * dv), lambda i, j: (i, 0)), pl.BlockSpec((tq, 8), lambda i, j: (i, 0)), *r_out_specs),
        scratch_shapes=[pltpu.VMEM((H, tq, 1), F32), pltpu.VMEM((H, tq, 1), F32), pltpu.VMEM((H, tq, dv), F32)] + r_scratch,
        compiler_params=_params(("arbitrary", "arbitrary")) if rider else _params(("parallel", "arbitrary")),
    )(*ins, *r_ins)
    return (res[0], res[1], rider.post(res[2:])) if rider else res


def _attn_bwd(qk, v, H, o, do, lse, cq, ck, *, scale, mask, name, rider=None):
    Sq, Sk = qk[0][0][0].shape[0], v[0].shape[0]
    dv = v[2] // H
    tq = _pick(Sq, (512, 256, 128))
    tk = tq if mask else _pick(Sk, (512, 256, 128))
    nq, nk = Sq // tq, Sk // tk
    bias = cq is not None
    npart = len(qk)
    n_in = 2 * npart + 4 + (2 if bias else 0)

    def body(*refs):
        refs = split(refs)
        q_refs, k_refs = refs[0:2 * npart:2], refs[1:2 * npart:2]
        v_ref, o_ref, do_ref, lse_ref = refs[2 * npart:2 * npart + 4]
        cq_ref, ck_ref = (refs[2 * npart + 4], refs[2 * npart + 5]) if bias else (None, None)
        outs = refs[n_in:]
        dq_refs, dk_refs, dv_ref = outs[:npart], outs[npart:2 * npart], outs[2 * npart]
        dck_ref, dcq_ref = (outs[2 * npart + 1], outs[2 * npart + 2]) if bias else (None, None)
        dk_accs, dv_acc = refs[-(npart + 1):-1], refs[-1]
        ki, qi = pl.program_id(0), pl.program_id(1)
        first_q = ki if mask else 0

        @pl.when((ki == 0) & (qi == 0))
        def _():
            for r in dq_refs:
                r[...] = jnp.zeros(r.shape, F32)
            if bias:
                dcq_ref[...] = jnp.zeros(dcq_ref.shape, F32)

        @pl.when(qi == first_q)
        def _():
            for r in dk_accs:
                r[...] = jnp.zeros(r.shape, F32)
            dv_acc[...] = jnp.zeros(dv_acc.shape, F32)
            if bias:
                dck_ref[...] = jnp.zeros(dck_ref.shape, F32)

        def compute(masked):
            keep = _mask_of(mask, tq, tk) if masked else None
            rows = pl.ds(pl.multiple_of(qi * tq, tq), tq)
            for h in range(H):
                hv = slice(h * dv, (h + 1) * dv)
                cols = [slice(h * w, (h + 1) * w) for _, _, w, _ in qk]
                kcols = [slice(None) if shared else c for (_, _, _, shared), c in zip(qk, cols)]
                s = None
                for q_ref, k_ref, c, kc in zip(q_refs, k_refs, cols, kcols):
                    part = _dot(q_ref[:, c], k_ref[:, kc], NT)
                    s = part if s is None else s + part
                s = s * scale
                if bias:
                    s = s + (cq_ref[:, h:h + 1] - ck_ref[h:h + 1, :])
                if masked:
                    s = jnp.where(keep, s, NEG)
                p = jnp.exp(s - lse_ref[:, h:h + 1])
                doh = do_ref[:, hv]
                dp = _dot(doh, v_ref[:, hv], NT)
                delta = jnp.sum(doh.astype(F32) * o_ref[:, hv].astype(F32), axis=1, keepdims=True)
                ds = p * (dp - delta)
                dv_acc[:, hv] += _dot(p, doh, TN)
                dss = (ds * scale).astype(BF16)
                for q_ref, k_ref, dq_ref, dk_acc, c, kc in zip(q_refs, k_refs, dq_refs, dk_accs, cols, kcols):
                    dk_acc[:, kc] += _dot(dss, q_ref[:, c], TN)
                    dq_ref[rows, c] += _dot(dss, k_ref[:, kc], NN)
                if bias:
                    dck_ref[h:h + 1, :] -= jnp.sum(ds, axis=0, keepdims=True)
                    dcq_ref[rows, h:h + 1] += jnp.sum(ds, axis=1, keepdims=True)

        if mask is None:
            compute(False)
        else:
            pl.when(qi > ki)(lambda: compute(False))
            pl.when(qi == ki)(lambda: compute(True))

        @pl.when(qi == nq - 1)
        def _():
            for r, acc in zip(dk_refs, dk_accs):
                r[...] = acc[...]
            dv_ref[...] = dv_acc[...]

    q_idx = (lambda j, i: jnp.maximum(i, j)) if mask else (lambda j, i: i)
    k_idx = lambda j, i: j
    ins, in_specs, dq_shapes, dq_specs, dk_shapes, dk_specs, scratch = [], [], [], [], [], [], []
    for q_e, k_e, w, shared in qk:
        ins += [q_e[0], k_e[0]]
        in_specs += [_col_block(q_e, tq, q_idx), _col_block(k_e, tk, k_idx)]
        dq_shapes.append(jax.ShapeDtypeStruct((Sq, H * w), F32))
        dq_specs.append(pl.BlockSpec((Sq, H * w), lambda j, i: (0, 0)))
        kw = w if shared else H * w
        dk_shapes.append(jax.ShapeDtypeStruct((Sk, kw), F32))
        dk_specs.append(pl.BlockSpec((tk, kw), lambda j, i: (j, 0)))
        scratch.append(pltpu.VMEM((tk, kw), F32))
    row_q = lambda width: pl.BlockSpec((tq, width), lambda j, i: (q_idx(j, i), 0))
    ins += [v[0], o, do, lse]
    in_specs += [_col_block(v, tk, k_idx), row_q(H * dv), row_q(H * dv), row_q(8)]
    out_shape = dq_shapes + dk_shapes + [jax.ShapeDtypeStruct((Sk, H * dv), F32)]
    out_specs = dq_specs + dk_specs + [pl.BlockSpec((tk, H * dv), lambda j, i: (j, 0))]
    if bias:
        in_specs += [row_q(8), pl.BlockSpec((8, tk), lambda j, i: (0, j))]
        ins += [cq, ck]
        out_shape += [jax.ShapeDtypeStruct((8, Sk), F32), jax.ShapeDtypeStruct((Sq, 8), F32)]
        out_specs += [pl.BlockSpec((8, tk), lambda j, i: (0, j)), pl.BlockSpec((Sq, 8), lambda j, i: (0, 0))]
    scratch.append(pltpu.VMEM((tk, H * dv), F32))
    n_out = len(out_shape)
    r_ins, r_in_specs, r_outs, r_out_specs, r_scratch, split = _carry(
        rider, len(ins), n_out, lambda: (pl.program_id(0) == 0) & (pl.program_id(1) == 0),
        lambda: (pl.program_id(0) == nk - 1) & (pl.program_id(1) == nq - 1))
    res = pl.pallas_call(
        body, name=name, out_shape=tuple(out_shape + r_outs), grid=(nk, nq), in_specs=in_specs + r_in_specs,
        out_specs=tuple(out_specs + r_out_specs), scratch_shapes=scratch + r_scratch,
        compiler_params=_params(("arbitrary", "arbitrary")),
    )(*ins, *r_ins)
    own = (list(res[:npart]), list(res[npart:2 * npart]), res[2 * npart]) + tuple(res[2 * npart + 1:n_out])
    return own + (rider.post(res[n_out:]),) if rider else own


def _flash_fwd(q, k, v, cq, ck, *, scale, mask, name):
    H, Sq, dk = q.shape
    Sk, dv = k.shape[1], v.shape[2]
    tq = _pick(Sq, (512, 256, 128))
    tk = tq if mask else _pick(Sk, (512, 256, 128))
    nq, nk = Sq // tq, Sk // tk
    bias = cq is not None

    def body(*refs):
        q_ref, k_ref, v_ref = refs[:3]
        cq_ref, ck_ref = (refs[3], refs[4]) if bias else (None, None)
        o_ref, lse_ref, m_s, l_s, acc_s = refs[-5:]
        qi, ki = pl.program_id(0), pl.program_id(1)

        @pl.when(ki == 0)
        def _():
            m_s[...] = jnp.full(m_s.shape, NEG, F32)
            l_s[...] = jnp.zeros(l_s.shape, F32)
            acc_s[...] = jnp.zeros(acc_s.shape, F32)

        def compute(masked):
            keep = _mask_of(mask, tq, tk) if masked else None
            for h in range(H):
                s = _dot(q_ref[h], k_ref[h], NT) * scale
                if bias:
                    s = s + (cq_ref[:, h:h + 1] - ck_ref[h:h + 1, :])
                if masked:
                    s = jnp.where(keep, s, NEG)
                m_prev = m_s[h]
                m_new = jnp.maximum(m_prev, jnp.max(s, axis=1, keepdims=True))
                alpha = jnp.exp(m_prev - m_new)
                p = jnp.exp(s - m_new)
                l_s[h] = alpha * l_s[h] + jnp.sum(p, axis=1, keepdims=True)
                acc_s[h] = alpha * acc_s[h] + _dot(p, v_ref[h], NN)
                m_s[h] = m_new

        if mask is None:
            compute(False)
        else:
            pl.when(ki < qi)(lambda: compute(False))
            pl.when(ki == qi)(lambda: compute(True))

        @pl.when(ki == ((nk - 1) if mask is None else qi))
        def _():
            lse_ref[...] = jnp.zeros(lse_ref.shape, F32)
            for h in range(H):
                o_ref[h] = (acc_s[h] / l_s[h]).astype(BF16)
                lse_ref[:, h:h + 1] = m_s[h] + jnp.log(l_s[h])

    kv_idx = (lambda i, j: (0, jnp.minimum(i, j), 0)) if mask else (lambda i, j: (0, j, 0))
    ck_idx = (lambda i, j: (0, jnp.minimum(i, j))) if mask else (lambda i, j: (0, j))
    in_specs = [pl.BlockSpec((H, tq, dk), lambda i, j: (0, i, 0)),
                pl.BlockSpec((H, tk, dk), kv_idx), pl.BlockSpec((H, tk, dv), kv_idx)]
    ins = [q, k, v]
    if bias:
        in_specs += [pl.BlockSpec((tq, 8), lambda i, j: (i, 0)), pl.BlockSpec((8, tk), ck_idx)]
        ins += [cq, ck]
    return pl.pallas_call(
        body, name=name,
        out_shape=(jax.ShapeDtypeStruct((H, Sq, dv), BF16), jax.ShapeDtypeStruct((Sq, 8), F32)),
        grid=(nq, nk), in_specs=in_specs,
        out_specs=(pl.BlockSpec((H, tq, dv), lambda i, j: (0, i, 0)), pl.BlockSpec((tq, 8), lambda i, j: (i, 0))),
        scratch_shapes=[pltpu.VMEM((H, tq, 1), F32), pltpu.VMEM((H, tq, 1), F32), pltpu.VMEM((H, tq, dv), F32)],
        compiler_params=_params(("parallel", "arbitrary")),
    )(*ins)


def _flash_bwd(q, k, v, o, do, lse, cq, ck, *, scale, mask, name):
    H, Sq, dk = q.shape
    Sk, dv = k.shape[1], v.shape[2]
    tq = _pick(Sq, (512, 256, 128))
    tk = tq if mask else _pick(Sk, (512, 256, 128))
    nq, nk = Sq // tq, Sk // tk
    bias = cq is not None

    def body(*refs):
        q_ref, k_ref, v_ref, o_ref, do_ref, lse_ref = refs[:6]
        n_in = 8 if bias else 6
        cq_ref, ck_ref = (refs[6], refs[7]) if bias else (None, None)
        outs = refs[n_in:]
        dq_ref, dk_ref, dv_ref = outs[:3]
        dck_ref, dcq_ref = (outs[3], outs[4]) if bias else (None, None)
        dk_s, dv_s = refs[-2], refs[-1]
        ki, qi = pl.program_id(0), pl.program_id(1)
        first_q = ki if mask else 0

        @pl.when((ki == 0) & (qi == 0))
        def _():
            dq_ref[...] = jnp.zeros(dq_ref.shape, F32)
            if bias:
                dcq_ref[...] = jnp.zeros(dcq_ref.shape, F32)

        @pl.when(qi == first_q)
        def _():
            dk_s[...] = jnp.zeros(dk_s.shape, F32)
            dv_s[...] = jnp.zeros(dv_s.shape, F32)
            if bias:
                dck_ref[...] = jnp.zeros(dck_ref.shape, F32)

        def compute(masked):
            keep = _mask_of(mask, tq, tk) if masked else None
            rows = pl.ds(pl.multiple_of(qi * tq, tq), tq)
            for h in range(H):
                qh, kh, vh, doh = q_ref[h], k_ref[h], v_ref[h], do_ref[h]
                s = _dot(qh, kh, NT) * scale
                if bias:
                    s = s + (cq_ref[:, h:h + 1] - ck_ref[h:h + 1, :])
                if masked:
                    s = jnp.where(keep, s, NEG)
                p = jnp.exp(s - lse_ref[:, h:h + 1])
                dp = _dot(doh, vh, NT)
                delta = jnp.sum(doh.astype(F32) * o_ref[h].astype(F32), axis=1, keepdims=True)
                ds = p * (dp - delta)
                dv_s[h] += _dot(p, doh, TN)
                dk_s[h] += _dot(ds, qh, TN)
                dq_ref[h, rows, :] += _dot(ds, kh, NN) * scale
                if bias:
                    dck_ref[h:h + 1, :] -= jnp.sum(ds, axis=0, keepdims=True)
                    dcq_ref[rows, h:h + 1] += jnp.sum(ds, axis=1, keepdims=True)

        if mask is None:
            compute(False)
        else:
            pl.when(qi > ki)(lambda: compute(False))
            pl.when(qi == ki)(lambda: compute(True))

        @pl.when(qi == nq - 1)
        def _():
            dk_ref[...] = dk_s[...] * scale
            dv_ref[...] = dv_s[...]

    q_idx = (lambda j, i: (0, jnp.maximum(i, j), 0)) if mask else (lambda j, i: (0, i, 0))
    c_idx = (lambda j, i: (jnp.maximum(i, j), 0)) if mask else (lambda j, i: (i, 0))
    kv_idx = lambda j, i: (0, j, 0)
    in_specs = [pl.BlockSpec((H, tq, dk), q_idx), pl.BlockSpec((H, tk, dk), kv_idx), pl.BlockSpec((H, tk, dv), kv_idx),
                pl.BlockSpec((H, tq, dv), q_idx), pl.BlockSpec((H, tq, dv), q_idx), pl.BlockSpec((tq, 8), c_idx)]
    ins = [q, k, v, o, do, lse]
    out_shape = [jax.ShapeDtypeStruct((H, Sq, dk), F32), jax.ShapeDtypeStruct((H, Sk, dk), F32),
                 jax.ShapeDtypeStruct((H, Sk, dv), F32)]
    out_specs = [pl.BlockSpec((H, Sq, dk), lambda j, i: (0, 0, 0)), pl.BlockSpec((H, tk, dk), kv_idx),
                 pl.BlockSpec((H, tk, dv), kv_idx)]
    if bias:
        in_specs += [pl.BlockSpec((tq, 8), c_idx), pl.BlockSpec((8, tk), lambda j, i: (0, j))]
        ins += [cq, ck]
        out_shape += [jax.ShapeDtypeStruct((8, Sk), F32), jax.ShapeDtypeStruct((Sq, 8), F32)]
        out_specs += [pl.BlockSpec((8, tk), lambda j, i: (0, j)), pl.BlockSpec((Sq, 8), lambda j, i: (0, 0))]
    return pl.pallas_call(
        body, name=name, out_shape=tuple(out_shape), grid=(nk, nq), in_specs=in_specs, out_specs=tuple(out_specs),
        scratch_shapes=[pltpu.VMEM((H, tk, dk), F32), pltpu.VMEM((H, tk, dv), F32)],
        compiler_params=_params(("arbitrary", "arbitrary")),
    )(*ins)


def _split3_dot(x, t):
    hi = x.astype(BF16)
    r1 = x - hi.astype(F32)
    mid = r1.astype(BF16)
    lo = (r1 - mid.astype(F32)).astype(BF16)
    return _dot(hi, t, NN) + _dot(mid, t, NN) + _dot(lo, t, NN)


def _fox_cum_fwd(ff_t, b, *, name):
    _, S = ff_t.shape
    tb = _pick(S, (512, 256, 128))

    def body(f_ref, b_ref, o_ref, carry):
        @pl.when(pl.program_id(0) == 0)
        def _():
            carry[...] = jnp.zeros(carry.shape, F32)

        lf = _log_sigmoid(f_ref[...] + b_ref[...])
        o_ref[...] = _split3_dot(lf, _tri(tb, False)) + carry[...]
        carry[...] += jnp.sum(lf, axis=1, keepdims=True)

    return pl.pallas_call(
        body, name=name, out_shape=jax.ShapeDtypeStruct((8, S), F32), grid=(S // tb,),
        in_specs=[pl.BlockSpec((8, tb), lambda i: (0, i)), pl.BlockSpec((8, 1), lambda i: (0, 0))],
        out_specs=pl.BlockSpec((8, tb), lambda i: (0, i)),
        scratch_shapes=[pltpu.VMEM((8, 1), F32)],
        compiler_params=_params(("arbitrary",)),
    )(ff_t, b)


def _fox_cum_bwd(ff_t, b, dcum_t, *, name):
    _, S = ff_t.shape
    tb = _pick(S, (512, 256, 128))
    nb = S // tb

    def body(f_ref, b_ref, dc_ref, df_ref, db_ref, carry):
        @pl.when(pl.program_id(0) == 0)
        def _():
            carry[...] = jnp.zeros(carry.shape, F32)
            db_ref[...] = jnp.zeros(db_ref.shape, F32)

        dc = dc_ref[...]
        dlf = _split3_dot(dc, _tri(tb, True)) + carry[...]
        carry[...] += jnp.sum(dc, axis=1, keepdims=True)
        df = dlf * _sigmoid(-(f_ref[...] + b_ref[...]))
        df_ref[...] = df
        db_ref[...] += jnp.sum(df, axis=1, keepdims=True)

    rev = lambda i: (0, nb - 1 - i)
    return pl.pallas_call(
        body, name=name,
        out_shape=(jax.ShapeDtypeStruct((8, S), F32), jax.ShapeDtypeStruct((8, 1), F32)), grid=(nb,),
        in_specs=[pl.BlockSpec((8, tb), rev), pl.BlockSpec((8, 1), lambda i: (0, 0)), pl.BlockSpec((8, tb), rev)],
        out_specs=(pl.BlockSpec((8, tb), rev), pl.BlockSpec((8, 1), lambda i: (0, 0))),
        scratch_shapes=[pltpu.VMEM((8, 1), F32)],
        compiler_params=_params(("arbitrary",)),
    )(ff_t, b, dcum_t)


GLA_W = GLA_HEADS * GLA_DK
GLA_BLOCK_CHUNKS = 4


def _gla_chunk(q, k, zsm, wg, bg, go, vs, rs, states):
    la = _log_sigmoid(bdot(zsm, wg) + bg) * (1.0 / GLA_TAU)
    cum = chunk_cumsum(la)
    end = jnp.sum(la, axis=0, keepdims=True)
    kd = k * jnp.exp(end - cum)
    a = jnp.exp(end)
    qs = q * (GLA_DK ** -0.5)
    lane = lax.broadcasted_iota(jnp.int32, (1, GLA_W), 1)
    outs, new_states = [], []
    for h in range(GLA_HEADS):
        head = jnp.where((lane >= h * GLA_DK) & (lane < (h + 1) * GLA_DK), 1.0, 0.0)
        st = states[h] * a + bdot_tn(vs[h], kd * head)
        o = bdot_nt(qs, st)
        o = _rms(o, go)
        outs.append(o * (rs[h] * _sigmoid(rs[h])))
        new_states.append(st)
    return outs, new_states


def _gla_fwd(z, zsm, wg, bg, go, cols, *, name):
    S = z.shape[0]
    rb = GLA_BLOCK_CHUNKS * CHUNK
    nb = S // rb
    cq, ckk, cv, cr = cols
    H = GLA_HEADS

    def body(q_ref, k_ref, zsm_ref, wg_ref, bg_ref, go_ref, *rest):
        v_refs, r_refs = rest[:H], rest[H:2 * H]
        o_ref, st_ref, state = rest[2 * H], rest[2 * H + 1], rest[2 * H + 2]

        @pl.when(pl.program_id(0) == 0)
        def _():
            state[...] = jnp.zeros(state.shape, F32)

        wg_, bg_, go_ = wg_ref[...], bg_ref[...], go_ref[...]
        for c in range(GLA_BLOCK_CHUNKS):
            rows = pl.ds(c * CHUNK, CHUNK)
            states = [state[h] for h in range(H)]
            for h in range(H):
                st_ref[c, h] = states[h]
            outs, new_states = _gla_chunk(
                q_ref[rows, :].astype(F32), k_ref[rows, :].astype(F32), zsm_ref[rows, :], wg_, bg_, go_,
                [v_refs[h][rows, :].astype(F32) for h in range(H)], [r_refs[h][rows, :].astype(F32) for h in range(H)], states)
            for h in range(H):
                o_ref[rows, h * GLA_DV:(h + 1) * GLA_DV] = outs[h].astype(BF16)
                state[h] = new_states[h]

    def col(width, off):
        return pl.BlockSpec((rb, width), lambda i, o=off // width: (i, o))

    full = lambda shp: pl.BlockSpec(shp, lambda i: (0,) * len(shp))
    in_specs = [col(GLA_W, cq), col(GLA_W, ckk), pl.BlockSpec((rb, 128), lambda i: (i, 0)),
                full((128, GLA_W)), full((1, GLA_W)), full((1, GLA_DV))]
    in_specs += [col(GLA_DV, cv + h * GLA_DV) for h in range(H)] + [col(GLA_DV, cr + h * GLA_DV) for h in range(H)]
    return pl.pallas_call(
        body, name=name,
        out_shape=(jax.ShapeDtypeStruct((S, H * GLA_DV), BF16), jax.ShapeDtypeStruct((S // CHUNK, H, GLA_DV, GLA_W), F32)),
        grid=(nb,), in_specs=in_specs,
        out_specs=(pl.BlockSpec((rb, H * GLA_DV), lambda i: (i, 0)),
                   pl.BlockSpec((GLA_BLOCK_CHUNKS, H, GLA_DV, GLA_W), lambda i: (i, 0, 0, 0))),
        scratch_shapes=[pltpu.VMEM((H, GLA_DV, GLA_W), F32)],
        compiler_params=_params(("arbitrary",)),
    )(z, z, zsm, wg, bg, go, *([z] * (2 * H)))


def _gla_bwd(z, zsm, wg, bg, go, states, do, cols, *, name):
    S = z.shape[0]
    rb = GLA_BLOCK_CHUNKS * CHUNK
    nb = S // rb
    cq, ckk, cv, cr = cols
    H = GLA_HEADS

    def body(q_ref, k_ref, zsm_ref, wg_ref, bg_ref, go_ref, st_ref, do_ref, *rest):
        v_refs, r_refs = rest[:H], rest[H:2 * H]
        dq_ref, dk_ref, dv_ref, dr_ref, dzsm_ref, dwg_ref, dbg_ref, dgo_ref, dstate = rest[2 * H:]

        @pl.when(pl.program_id(0) == 0)
        def _():
            dstate[...] = jnp.zeros(dstate.shape, F32)
            dwg_ref[...] = jnp.zeros(dwg_ref.shape, F32)
            dbg_ref[...] = jnp.zeros(dbg_ref.shape, F32)
            dgo_ref[...] = jnp.zeros(dgo_ref.shape, F32)

        wg_, bg_, go_ = wg_ref[...], bg_ref[...], go_ref[...]
        for c in reversed(range(GLA_BLOCK_CHUNKS)):
            rows = pl.ds(c * CHUNK, CHUNK)
            prim = (q_ref[rows, :].astype(F32), k_ref[rows, :].astype(F32), zsm_ref[rows, :], wg_, bg_, go_,
                    [v_refs[h][rows, :].astype(F32) for h in range(H)], [r_refs[h][rows, :].astype(F32) for h in range(H)],
                    [st_ref[c, h] for h in range(H)])
            _, vjp = jax.vjp(_gla_chunk, *prim)
            douts = [do_ref[rows, h * GLA_DV:(h + 1) * GLA_DV].astype(F32) for h in range(H)]
            dq, dk, dzs, dwg, dbg, dgo, dvs, drs, dsts = vjp((douts, [dstate[h] for h in range(H)]))
            dq_ref[rows, :] = dq.astype(BF16)
            dk_ref[rows, :] = dk.astype(BF16)
            dzsm_ref[rows, :] = dzs
            dwg_ref[...] += dwg
            dbg_ref[...] += dbg
            dgo_ref[...] += dgo
            for h in range(H):
                dv_ref[rows, h * GLA_DV:(h + 1) * GLA_DV] = dvs[h].astype(BF16)
                dr_ref[rows, h * GLA_DV:(h + 1) * GLA_DV] = drs[h].astype(BF16)
                dstate[h] = dsts[h]

    rev = lambda i: nb - 1 - i

    def col(width, off):
        return pl.BlockSpec((rb, width), lambda i, o=off // width: (rev(i), o))

    full = lambda shp: pl.BlockSpec(shp, lambda i: (0,) * len(shp))
    rowb = lambda w: pl.BlockSpec((rb, w), lambda i: (rev(i), 0))
    in_specs = [col(GLA_W, cq), col(GLA_W, ckk), rowb(128), full((128, GLA_W)), full((1, GLA_W)), full((1, GLA_DV)),
                pl.BlockSpec((GLA_BLOCK_CHUNKS, H, GLA_DV, GLA_W), lambda i: (rev(i), 0, 0, 0)), rowb(H * GLA_DV)]
    in_specs += [col(GLA_DV, cv + h * GLA_DV) for h in range(H)] + [col(GLA_DV, cr + h * GLA_DV) for h in range(H)]
    return pl.pallas_call(
        body, name=name,
        out_shape=(jax.ShapeDtypeStruct((S, GLA_W), BF16), jax.ShapeDtypeStruct((S, GLA_W), BF16),
                   jax.ShapeDtypeStruct((S, H * GLA_DV), BF16), jax.ShapeDtypeStruct((S, H * GLA_DV), BF16),
                   jax.ShapeDtypeStruct((S, 128), F32), jax.ShapeDtypeStruct((128, GLA_W), F32),
                   jax.ShapeDtypeStruct((1, GLA_W), F32), jax.ShapeDtypeStruct((1, GLA_DV), F32)),
        grid=(nb,), in_specs=in_specs,
        out_specs=(rowb(GLA_W), rowb(GLA_W), rowb(H * GLA_DV), rowb(H * GLA_DV), rowb(128),
                   full((128, GLA_W)), full((1, GLA_W)), full((1, GLA_DV))),
        scratch_shapes=[pltpu.VMEM((H, GLA_DV, GLA_W), F32)],
        compiler_params=_params(("arbitrary",)),
    )(z, z, zsm, wg, bg, go, states, do, *([z] * (2 * H)))


def _row_spec(entry, tr):
    if isinstance(entry, tuple):
        arr, width, off = entry
        return arr, pl.BlockSpec((tr, width), lambda i, o=off // width: (i, o))
    return entry, pl.BlockSpec((tr, entry.shape[1]), lambda i: (i, 0))


def _stage_fwd(fn, rows, consts, outs, *, name, tr=None):
    first = rows[0][0] if isinstance(rows[0], tuple) else rows[0]
    S = first.shape[0]
    tr = tr or _pick(S, (512, 256, 128))
    arrs, specs = zip(*[_row_spec(e, tr) for e in rows])
    nr, nc = len(rows), len(consts)

    def body(*refs):
        vals = [r[...].astype(F32) for r in refs[:nr + nc]]
        res = fn(*vals)
        for o_ref, val in zip(refs[nr + nc:], res):
            o_ref[...] = val.astype(o_ref.dtype)

    cspecs = [pl.BlockSpec(c.shape, lambda i, n=c.ndim: (0,) * n) for c in consts]
    return pl.pallas_call(
        body, name=name,
        out_shape=tuple(jax.ShapeDtypeStruct((S, w), dt) for w, dt in outs), grid=(S // tr,),
        in_specs=list(specs) + cspecs,
        out_specs=tuple(pl.BlockSpec((tr, w), lambda i: (i, 0)) for w, _ in outs),
        compiler_params=_params(("parallel",)),
    )(*arrs, *consts)


def _stage_bwd(fn, rows, consts, cts, n_diff, drow_dtypes, *, name, tr=None):
    first = rows[0][0] if isinstance(rows[0], tuple) else rows[0]
    S = first.shape[0]
    tr = tr or _pick(S, (512, 256, 128))
    arrs, specs = zip(*[_row_spec(e, tr) for e in rows])
    widths = [e[1] if isinstance(e, tuple) else e.shape[1] for e in rows]
    nr, nc, nt = len(rows), len(consts), len(cts)

    def body(*refs):
        vals = [r[...].astype(F32) for r in refs[:nr + nc]]
        ct = [r[...].astype(F32) for r in refs[nr + nc:nr + nc + nt]]
        drow_refs = refs[nr + nc + nt:nr + nc + nt + n_diff]
        dconst_refs = refs[nr + nc + nt + n_diff:]
        rest_rows = vals[n_diff:nr]

        def f(diff_rows, cs):
            return tuple(fn(*diff_rows, *rest_rows, *cs))

        _, vjp = jax.vjp(f, vals[:n_diff], vals[nr:])
        drows, dcs = vjp(tuple(ct))
        for r, val in zip(drow_refs, drows):
            r[...] = val.astype(r.dtype)
        first_step = pl.program_id(0) == 0
        for r, val in zip(dconst_refs, dcs):
            @pl.when(first_step)
            def _(r=r, val=val):
                r[...] = val

            @pl.when(jnp.logical_not(first_step))
            def _(r=r, val=val):
                r[...] += val

    cspecs = [pl.BlockSpec(c.shape, lambda i, n=c.ndim: (0,) * n) for c in consts]
    ctspecs = [pl.BlockSpec((tr, c.shape[1]), lambda i: (i, 0)) for c in cts]
    out_shape = [jax.ShapeDtypeStruct((S, widths[j]), drow_dtypes[j]) for j in range(n_diff)]
    out_shape += [jax.ShapeDtypeStruct(c.shape, F32) for c in consts]
    out_specs = [pl.BlockSpec((tr, widths[j]), lambda i: (i, 0)) for j in range(n_diff)] + cspecs
    res = pl.pallas_call(
        body, name=name, out_shape=tuple(out_shape), grid=(S // tr,),
        in_specs=list(specs) + cspecs + ctspecs, out_specs=tuple(out_specs),
        compiler_params=_params(("arbitrary",)),
    )(*arrs, *consts, *cts)
    return list(res[:n_diff]), list(res[n_diff:])


def _mla_prep_fn(cq, ckv, kr, kr_sw, cos_q, sin_q, cos_k, sin_k, gq, gkv, wq_n, wq_r, wq_sw, wk, wv):
    hq = _rms(cq, gq)
    hkv = _rms(ckv, gkv)
    return (bdot(hq, wq_n), bdot(hq, wq_r) * cos_q + bdot(hq, wq_sw) * sin_q,
            bdot(hkv, wk), bdot(hkv, wv), kr * cos_k + kr_sw * sin_k)


def _merge_fn(g0, g1, g2, of, og, om, b0, b1, b2, wf, wg, wm):
    return (_sigmoid(g0 + b0) * bdot(of, wf) + _sigmoid(g1 + b1) * bdot(og, wg) + _sigmoid(g2 + b2) * bdot(om, wm),)


_IN_SIZES = (256, 256, 256, 4, 256, 256, 512, 16, 512, 256, 128, 32, 3072)
_IN_OFF = np.concatenate([[0], np.cumsum(_IN_SIZES)])
(_O_FQ, _O_FK, _O_FV, _O_FF, _O_GQ, _O_GK, _O_GV, _O_GLOW, _O_GR, _O_MQ, _O_MKV, _O_MKR, _O_ZG) = [int(o) for o in _IN_OFF[:-1]]
N_IN = int(_IN_OFF[-1])
_BIG_GROUPS = ((_O_ZG, 3072), (_O_GV, 512), (_O_GR, 512), (_O_FQ, 256), (_O_FK, 256), (_O_FV, 256),
               (_O_GQ, 256), (_O_GK, 256), (_O_MQ, 256), (_O_MKV, 128))
Z_GATE, Z_GV, Z_GR, Z_FQ, Z_FK, Z_FV, Z_GQ, Z_GK, Z_MQ, Z_MKV = [int(o) for o in
                                                                    np.concatenate([[0], np.cumsum([w for _, w in _BIG_GROUPS])])[:-1]]
N_BIG = sum(w for _, w in _BIG_GROUPS)
SM_FF, SM_GLOW, SM_KR, SM_KR_SW, N_SM = 0, 8, 32, 64, 128
N_PAD = N_BIG + N_SM
_HALF = MLA_ROPE // 2
_QK_HD = MLA_NOPE + MLA_ROPE


def _in_perm():
    idx = np.concatenate([np.arange(o, o + w) for o, w in _BIG_GROUPS] + [np.zeros(N_SM, np.int64)])
    sign = np.concatenate([np.ones(N_BIG), np.zeros(N_SM)])
    for src, dst, w in ((_O_FF, SM_FF, 4), (_O_GLOW, SM_GLOW, 16), (_O_MKR, SM_KR, 32)):
        idx[N_BIG + dst:N_BIG + dst + w] = np.arange(src, src + w)
        sign[N_BIG + dst:N_BIG + dst + w] = 1.0
    inv = np.zeros(N_IN, np.int64)
    inv[idx[sign > 0]] = np.nonzero(sign > 0)[0]
    sw = N_BIG + SM_KR_SW
    idx[sw:sw + _HALF] = np.arange(_O_MKR + _HALF, _O_MKR + MLA_ROPE)
    sign[sw:sw + _HALF] = -1.0
    idx[sw + _HALF:sw + MLA_ROPE] = np.arange(_O_MKR, _O_MKR + _HALF)
    sign[sw + _HALF:sw + MLA_ROPE] = 1.0
    inv2, sign2 = np.zeros(N_IN, np.int64), np.zeros(N_IN)
    inv2[idx[sw:sw + MLA_ROPE]] = np.arange(sw, sw + MLA_ROPE)
    sign2[idx[sw:sw + MLA_ROPE]] = sign[sw:sw + MLA_ROPE]
    return idx, sign.astype(np.float32), inv, inv2, sign2.astype(np.float32)


_IN_IDX, _IN_SIGN, _IN_INV, _IN_INV2, _IN_SIGN2 = _in_perm()


def _uq_perm():
    base = [h * _QK_HD for h in range(MLA_HEADS)]
    nope = np.concatenate([np.arange(b, b + MLA_NOPE) for b in base])
    rot = np.concatenate([np.arange(b + MLA_NOPE, b + _QK_HD) for b in base])
    sw = np.concatenate([np.concatenate([np.arange(b + MLA_NOPE + _HALF, b + _QK_HD), np.arange(b + MLA_NOPE, b + MLA_NOPE + _HALF)])
                         for b in base])
    sw_sign = np.tile(np.concatenate([-np.ones(_HALF), np.ones(_HALF)]), MLA_HEADS).astype(np.float32)
    return nope, rot, sw, sw_sign


_UQ_NOPE, _UQ_ROT, _UQ_SW, _UQ_SW_SIGN = _uq_perm()
_UKV_PERM = np.concatenate(
    [np.concatenate([np.arange(h * 128, h * 128 + MLA_NOPE) for h in range(MLA_HEADS)]),
     np.concatenate([np.arange(h * 128 + MLA_NOPE, (h + 1) * 128) for h in range(MLA_HEADS)])])
_UKV_INV = np.argsort(_UKV_PERM)


def _rope_tables(S):
    inv = ROPE_BASE ** (-jnp.arange(_HALF, dtype=F32) / _HALF)
    ang = jnp.arange(S, dtype=F32)[:, None] * inv[None, :]
    cos, sin = jnp.tile(jnp.cos(ang), (1, 2)), jnp.tile(jnp.sin(ang), (1, 2))
    return jnp.tile(cos, (1, MLA_HEADS)), jnp.tile(sin, (1, MLA_HEADS)), cos, sin


class _LayerParams:
    def __init__(self, rep, l):
        self.w, self.rep, self.l, self.made = {}, rep, l, {}

    def __getitem__(self, k):
        if k not in self.made:
            self.made[k] = self._make(k)
        return self.made[k]

    def _make(self, k):
        w, rep, l = self.w, self.rep, self.l
        if k == 'wg':
            return jnp.zeros((N_SM, GLA_W), BF16).at[SM_GLOW:SM_GLOW + GLA_RANK].set(w['w_gla_gate'])
        if k in ('wq_n', 'wq_r'):
            return w['w_mla_uq'][:, _UQ_NOPE if k == 'wq_n' else _UQ_ROT]
        if k == 'wq_sw':
            return w['w_mla_uq'][:, _UQ_SW] * _UQ_SW_SIGN.astype(BF16)
        if k in ('wk', 'wv'):
            return w['w_mla_ukv'][:, _UKV_PERM[:256] if k == 'wk' else _UKV_PERM[256:]]
        if k == 'b_f':
            return jnp.zeros((8, 1), F32).at[:FOX_HEADS, 0].set(rep['b_fox_forget'][l])
        if k == 'b_gate':
            return [rep['b_branch_gate'][l][i * 1024:(i + 1) * 1024].reshape(1, 1024) for i in range(3)]
        vec = {'bg': 'b_gla_gate', 'go': 'g_gla_out', 'gq': 'g_mla_q', 'gkv': 'g_mla_kv'}
        if k in vec:
            return rep[vec[k]][l].reshape(1, -1)
        return rep[k][l] if k in rep else w[k]


_GLA_COLS = (Z_GQ, Z_GK, Z_GV, Z_GR)
_MLA_OUTS = [(256, BF16), (128, BF16), (256, BF16), (256, BF16), (MLA_ROPE, BF16)]


def _mla_rows(z, zsm, rope):
    return [(z, 256, Z_MQ), (z, 128, Z_MKV), zsm[:, SM_KR:SM_KR + MLA_ROPE], zsm[:, SM_KR_SW:SM_KR_SW + MLA_ROPE], *rope]


def _mla_consts(p):
    return [p['gq'], p['gkv'], p['wq_n'], p['wq_r'], p['wq_sw'], p['wk'], p['wv']]


def _fox_qkv(z):
    return [((z, Z_FQ, 256), (z, Z_FK, 256), FOX_HD, False)], (z, Z_FV, 256)


def _mla_qkv(qn, qr, kn, vv, kr):
    return [((qn, 0, 256), (kn, 0, 256), MLA_NOPE, False), ((qr, 0, 128), (kr, 0, MLA_ROPE), MLA_ROPE, True)], (vv, 0, 256)


def _xa_qkv(qx, kvx):
    return [((qx, 0, 512), (kvx, 0, 512), XA_HD, False)], (kvx, 512, 512)


def _merge_rows(z, o_fox, o_gla, o_mla):
    return [(z, 1024, Z_GATE), (z, 1024, Z_GATE + 1024), (z, 1024, Z_GATE + 2048), o_fox, o_gla, o_mla]


def _merge_consts(p):
    return p['b_gate'] + [p['w_up_fox'], p['w_up_gla'], p['w_up_mla']]


def _carried(hooks, key, call):
    rider, sink = hooks.pop(key, (None, None))
    res = call(rider=rider)
    if rider is None:
        return res
    sink(res[-1])
    return res[:-1]


def _layer_fwd(x0, mem, p, rope, l, hooks):
    S = x0.shape[0]
    sv = {'x0': x0}
    h1 = _rms_fwd(x0, p['g_mix'], name=f"rms_mix_{l}")
    z = _mm(h1, p['w_in'], mode='nn', out_dtype=BF16, b_cols=(0, N_BIG), name=f"in_big_{l}")
    zsm = _mm(h1, p['w_in'], mode='nn', out_dtype=F32, b_cols=(N_BIG, N_SM), name=f"in_small_{l}")
    sv.update(h1=h1, z=z, zsm=zsm)
    ff_t = jnp.zeros((8, S), F32).at[:FOX_HEADS].set(zsm[:, SM_FF:SM_FF + FOX_HEADS].T)
    cum_t = _fox_cum_fwd(ff_t, p['b_f'], name=f"fox_cum_{l}")
    cum = cum_t.T
    o_fox, lse_f = _carried(hooks, (l, 'fox_fwd'), lambda rider: _attn_fwd(
        *_fox_qkv(z), FOX_HEADS, cum, cum_t, scale=FOX_HD ** -0.5, mask='causal', name=f"fox_fwd_{l}", rider=rider))
    sv.update(ff_t=ff_t, cum=cum, cum_t=cum_t, lse_f=lse_f, o_fox=o_fox)
    o_gla, states = _gla_fwd(z, zsm, p['wg'], p['bg'], p['go'], _GLA_COLS, name=f"gla_fwd_{l}")
    sv.update(o_gla=o_gla, states=states)
    mla = _stage_fwd(_mla_prep_fn, _mla_rows(z, zsm, rope), _mla_consts(p), _MLA_OUTS, name=f"mla_prep_{l}")
    o_mla, lse_m = _carried(hooks, (l, 'mla_fwd'), lambda rider: _attn_fwd(
        *_mla_qkv(*mla), MLA_HEADS, None, None, scale=_QK_HD ** -0.5, mask='chunk', name=f"mla_fwd_{l}", rider=rider))
    sv.update(mla=mla, lse_m=lse_m, o_mla=o_mla)
    (y,) = _stage_fwd(_merge_fn, _merge_rows(z, o_fox, o_gla, o_mla), _merge_consts(p), [(1024, BF16)], name=f"merge_{l}")
    x1 = _mm(y, p['w_out'], mode='nn', out_dtype=F32, residual=x0, name=f"out_proj_{l}")
    sv.update(y=y, x1=x1)
    h2 = _rms_fwd(x1, p['g_xa'], name=f"rms_xa_{l}")
    hm = _rms_fwd(mem, p['g_mem'], name=f"rms_mem_{l}")
    qx = _mm(h2, p['w_xq'], mode='nn', out_dtype=BF16, name=f"xq_{l}")
    kvx = _mm(hm, p['w_xkv'], mode='nn', out_dtype=BF16, name=f"xkv_{l}")
    ox, lse_x = _attn_fwd(*_xa_qkv(qx, kvx), XA_HEADS, None, None, scale=XA_HD ** -0.5, mask=None, name=f"xa_fwd_{l}")
    x2 = _mm(ox, p['w_xo'], mode='nn', out_dtype=F32, residual=x1, name=f"xo_{l}")
    sv.update(h2=h2, hm=hm, qx=qx, kvx=kvx, lse_x=lse_x, ox=ox, x2=x2)
    h3 = _rms_fwd(x2, p['g_mlp'], name=f"rms_mlp_{l}")
    a = _mm(h3, p['w_mlp1'], mode='nn', out_dtype=BF16, name=f"mlp1_{l}")
    x3 = _mm(a, p['w_mlp2'], mode='nn', out_dtype=F32, act='relu2', residual=x2, name=f"mlp2_{l}")
    sv.update(h3=h3, a=a)
    return x3, sv


def _layer_bwd(dx3, dx3b, mem, p, rope, sv, l, hooks):
    S = dx3.shape[0]
    g = {}
    da = _mm(dx3b, p['w_mlp2'], mode='nt', out_dtype=BF16, drelu_of=sv['a'], name=f"d_mlp2_in_{l}")
    g['w_mlp2'] = _mm(sv['a'], dx3b, mode='tn', out_dtype=F32, act='relu2', name=f"d_w_mlp2_{l}")
    dh3 = _mm(da, p['w_mlp1'], mode='nt', out_dtype=F32, name=f"d_mlp1_in_{l}")
    g['w_mlp1'] = _mm(sv['h3'], da, mode='tn', out_dtype=F32, name=f"d_w_mlp1_{l}")
    dx2, dx2b, g['g_mlp'] = _rms_bwd(sv['x2'], p['g_mlp'], dh3, dx3, name=f"d_rms_mlp_{l}")
    dox = _mm(dx2b, p['w_xo'], mode='nt', out_dtype=BF16, name=f"d_xo_in_{l}")
    g['w_xo'] = _mm(sv['ox'], dx2b, mode='tn', out_dtype=F32, name=f"d_w_xo_{l}")
    (dqx,), (dkx,), dvx = _attn_bwd(*_xa_qkv(sv['qx'], sv['kvx']), XA_HEADS, sv['ox'], dox, sv['lse_x'], None, None,
                                    scale=XA_HD ** -0.5, mask=None, name=f"xa_bwd_{l}")
    dqx = dqx.astype(BF16)
    dkvx = jnp.concatenate([dkx, dvx], axis=1).astype(BF16)
    dh2 = _mm(dqx, p['w_xq'], mode='nt', out_dtype=F32, name=f"d_xq_in_{l}")
    g['w_xq'] = _mm(sv['h2'], dqx, mode='tn', out_dtype=F32, name=f"d_w_xq_{l}")
    dhm = _mm(dkvx, p['w_xkv'], mode='nt', out_dtype=F32, name=f"d_xkv_in_{l}")
    g['w_xkv'] = _mm(sv['hm'], dkvx, mode='tn', out_dtype=F32, name=f"d_w_xkv_{l}")
    _, _, g['g_mem'] = _rms_bwd(mem, p['g_mem'], dhm, None, name=f"d_rms_mem_{l}")
    dx1, dx1b, g['g_xa'] = _rms_bwd(sv['x1'], p['g_xa'], dh2, dx2, name=f"d_rms_xa_{l}")
    dy = _mm(dx1b, p['w_out'], mode='nt', out_dtype=F32, name=f"d_out_in_{l}")
    g['w_out'] = _mm(sv['y'], dx1b, mode='tn', out_dtype=F32, name=f"d_w_out_{l}")
    z, zsm = sv['z'], sv['zsm']
    (dg0, dg1, dg2, do_fox, do_gla, do_mla), (db0, db1, db2, g['w_up_fox'], g['w_up_gla'], g['w_up_mla']) = _stage_bwd(
        _merge_fn, _merge_rows(z, sv['o_fox'], sv['o_gla'], sv['o_mla']), _merge_consts(p), [dy], 6, [BF16] * 6,
        name=f"merge_bwd_{l}")
    g['b_branch_gate'] = jnp.concatenate([db0, db1, db2], axis=1).reshape(-1)
    (dfq,), (dfk,), dfv, dck, dcq = _carried(hooks, (l, 'fox_bwd'), lambda rider: _attn_bwd(
        *_fox_qkv(z), FOX_HEADS, sv['o_fox'], do_fox, sv['lse_f'], sv['cum'], sv['cum_t'],
        scale=FOX_HD ** -0.5, mask='causal', name=f"fox_bwd_{l}", rider=rider))
    dff_t, db_f = _fox_cum_bwd(sv['ff_t'], p['b_f'], dck + dcq.T, name=f"fox_cum_bwd_{l}")
    g['b_fox_forget'] = db_f[:FOX_HEADS, 0]
    dgq, dgk, dgv, dgr, dzsm, dwg, dbg, dgo = _gla_bwd(z, zsm, p['wg'], p['bg'], p['go'], sv['states'], do_gla, _GLA_COLS,
                                                       name=f"gla_bwd_{l}")
    g['w_gla_gate'] = dwg[SM_GLOW:SM_GLOW + GLA_RANK]
    g['b_gla_gate'] = dbg.reshape(-1)
    g['g_gla_out'] = dgo.reshape(-1)
    (dmqn, dmqr), (dmkn, dmkr), dmv = _attn_bwd(*_mla_qkv(*sv['mla']), MLA_HEADS, sv['o_mla'], do_mla, sv['lse_m'], None, None,
                                                scale=_QK_HD ** -0.5, mask='chunk', name=f"mla_bwd_{l}")
    (dcq, dckv, dkr, dkr_sw), (dgq_n, dgkv_n, dwq_n, dwq_r, dwq_sw, dwk, dwv) = _stage_bwd(
        _mla_prep_fn, _mla_rows(z, zsm, rope), _mla_consts(p), [dmqn, dmqr, dmkn, dmv, dmkr], 4, [BF16, BF16, F32, F32],
        name=f"mla_prep_bwd_{l}")
    g['g_mla_q'] = dgq_n.reshape(-1)
    g['g_mla_kv'] = dgkv_n.reshape(-1)
    g['w_mla_uq'] = (jnp.zeros((MLA_Q_RANK, MLA_HEADS * _QK_HD), F32).at[:, _UQ_NOPE].set(dwq_n).at[:, _UQ_ROT].set(dwq_r)
                     .at[:, _UQ_SW].add(dwq_sw * _UQ_SW_SIGN))
    g['w_mla_ukv'] = jnp.concatenate([dwk, dwv], axis=1)[:, _UKV_INV]
    dz = jnp.concatenate([dg0, dg1, dg2, dgv, dgr, dfq.astype(BF16), dfk.astype(BF16), dfv.astype(BF16), dgq, dgk, dcq, dckv,
                          (dzsm + jnp.concatenate([dff_t[:FOX_HEADS].T, jnp.zeros((S, SM_KR - FOX_HEADS), F32), dkr, dkr_sw,
                                                   jnp.zeros((S, N_SM - SM_KR_SW - MLA_ROPE), F32)], axis=1)).astype(BF16)],
                         axis=1)
    dh1 = _mm(dz, p['w_in'], mode='nt', out_dtype=F32, tk=N_PAD // 2, name=f"d_in_{l}")
    g['w_in'] = _mm(sv['h1'], dz, mode='tn', out_dtype=F32, tm=512, tn=N_PAD // 2, tk=512, name=f"d_w_in_{l}")
    dx0, dx0b, g['g_mix'] = _rms_bwd(sv['x0'], p['g_mix'], dh1, dx1, name=f"d_rms_mix_{l}")
    for n in ('g_mlp', 'g_mem', 'g_xa', 'g_mix'):
        g[n] = g[n].reshape(-1)
    return dx0, dx0b, g


def _local_step(x, mem, target, ps, g_final, hooks, layer_done):
    rope = _rope_tables(x.shape[0])
    saved = []
    for l, p in enumerate(ps):
        x, sv = _layer_fwd(x, mem, p, rope, l, hooks)
        saved.append(sv)
    loss, dx, dxb, dgf = _loss_head(x, g_final, target, name="loss_head")
    for l in reversed(range(len(ps))):
        dx, dxb, grads = _layer_bwd(dx, dxb, mem, ps[l], rope, saved[l], l, hooks)
        layer_done(l, grads)
    assert not hooks, f"exchanges without a carrier: {list(hooks)}"
    return loss, dx, dgf.reshape(-1)


_MESH_AXES = ("x", "y", "c")
_HBM = pl.BlockSpec(memory_space=pl.ANY)


N_CHIP = 4


def _place():
    x, y, c = (lax.axis_index(n) for n in _MESH_AXES)
    return (x, y, c), (x, y, 1 - c), [(1 - x, y), (x, 1 - y), (1 - x, 1 - y)]


def _remote(src, dst, sems, k, to):
    return pltpu.make_async_remote_copy(src_ref=src, dst_ref=dst, send_sem=sems[0].at[k], recv_sem=sems[1].at[k],
                                        device_id=to, device_id_type=pl.DeviceIdType.MESH)


def _all_gather(x, *, name):
    def body(x_ref, o_ref, send_sems, recv_sems, local_sem):
        me, sib, chips = _place()
        c = me[2]
        sems = (send_sems, recv_sems)
        slot = lambda px, py, pc: o_ref.at[4 * px + 2 * py + pc]
        mine = pltpu.make_async_copy(x_ref, slot(*me), local_sem)
        mine.start()
        first = [_remote(x_ref, slot(*me), sems, 0, sib)]
        first += [_remote(x_ref, slot(*me), sems, 1 + j, (*chip, c)) for j, chip in enumerate(chips)]
        for cp in first:
            cp.start()
        passed = [_remote(slot(*chip, c), slot(*chip, c), sems, 4 + j, sib) for j, chip in enumerate(chips)]
        for j, chip in enumerate(chips):
            _remote(x_ref, slot(*chip, c), sems, 1 + j, me).wait_recv()
            passed[j].start()
        _remote(x_ref, slot(*sib), sems, 0, me).wait_recv()
        for j, chip in enumerate(chips):
            _remote(x_ref, slot(*chip, 1 - c), sems, 4 + j, me).wait_recv()
        for cp in first + passed:
            cp.wait_send()
        mine.wait()

    return pl.pallas_call(
        body, name=name, out_shape=jax.ShapeDtypeStruct((N_DEV,) + x.shape, x.dtype),
        in_specs=[_HBM], out_specs=_HBM,
        scratch_shapes=[pltpu.SemaphoreType.DMA((N_DEV - 1,)), pltpu.SemaphoreType.DMA((N_DEV - 1,)), pltpu.SemaphoreType.DMA],
        compiler_params=pltpu.CompilerParams(has_side_effects=True),
    )(x)


class _Rider:
    def __init__(self, inputs, out_shapes, scratch, start, finish, post):
        self.inputs, self.out_shapes, self.scratch = list(inputs), list(out_shapes), list(scratch)
        self.start, self.finish, self.post = start, finish, post


def _run_rider(rider, *, name):
    def body(*refs):
        rider.start(refs)
        rider.finish(refs)

    outs = pl.pallas_call(
        body, name=name, out_shape=tuple(rider.out_shapes), in_specs=[_HBM] * len(rider.inputs),
        out_specs=(_HBM,) * len(rider.out_shapes), scratch_shapes=rider.scratch,
        compiler_params=pltpu.CompilerParams(has_side_effects=True),
    )(*rider.inputs)
    return rider.post(outs)


def _carry(rider, n_in, n_out, first, last):
    if rider is None:
        return [], [], [], [], [], lambda refs: refs
    ni, no = len(rider.inputs), len(rider.out_shapes)

    def split(refs):
        own_in, r_in = refs[:n_in], refs[n_in:n_in + ni]
        own_out, r_out = refs[n_in + ni:n_in + ni + n_out], refs[n_in + ni + n_out:n_in + ni + n_out + no]
        rest = refs[n_in + ni + n_out + no:]
        own_scr, r_scr = rest[:len(rest) - len(rider.scratch)], rest[len(rest) - len(rider.scratch):]
        rrefs = tuple(r_in) + tuple(r_out) + tuple(r_scr)
        pl.when(first())(lambda: rider.start(rrefs))
        pl.when(last())(lambda: rider.finish(rrefs))
        return tuple(own_in) + tuple(own_out) + tuple(own_scr)

    return list(rider.inputs), [_HBM] * ni, list(rider.out_shapes), [_HBM] * no, list(rider.scratch), split


def _gather_rider(shards, axes):
    n = len(shards)
    srcs, out_shapes, kinds = [], [], []
    for s, ax in zip(shards, axes):
        L, a, b = s.shape
        if ax == 1:
            srcs.append(s.reshape(L, 1, a, b)), out_shapes.append((L, N_DEV, a, b)), kinds.append('row')
        elif b % 128 == 0:
            srcs.append(s), out_shapes.append((L, a, N_DEV * b)), kinds.append('col')
        else:
            srcs.append(s.reshape(1, L, a, b)), out_shapes.append((N_DEV, L, a, b)), kinds.append('slot')

    def parts(refs):
        x_refs, o_refs = refs[:n], refs[n:2 * n]
        send_sems, recv_sems, local_sem = refs[2 * n:]
        me, sib, chips = _place()
        sems = (send_sems, recv_sems)

        def win(t, px, py, pc):
            idx = 4 * px + 2 * py + pc
            if kinds[t] == 'row':
                return o_refs[t].at[:, pl.ds(idx, 1)]
            if kinds[t] == 'col':
                width = shards[t].shape[2]
                return o_refs[t].at[:, :, pl.ds(pl.multiple_of(idx * width, 128), width)]
            return o_refs[t].at[pl.ds(idx, 1)]

        def group(k, block, to, own):
            return [_remote(x_refs[t] if own else win(t, *block), win(t, *block), sems, k * n + t, to) for t in range(n)]

        mine = [pltpu.make_async_copy(x_refs[t], win(t, *me), local_sem.at[t]) for t in range(n)]
        first = group(0, me, sib, True)
        for j, chip in enumerate(chips):
            first += group(1 + j, me, (*chip, me[2]), True)
        return me, sib, chips, group, mine, first

    def start(refs):
        *_, mine, first = parts(refs)
        for cp in mine + first:
            cp.start()

    def finish(refs):
        me, sib, chips, group, mine, first = parts(refs)
        c = me[2]
        passed = []
        for j, chip in enumerate(chips):
            for cp in group(1 + j, (*chip, c), me, False):
                cp.wait_recv()
            fwd = group(4 + j, (*chip, c), sib, False)
            for cp in fwd:
                cp.start()
            passed += fwd
        for cp in group(0, sib, me, False):
            cp.wait_recv()
        for j, chip in enumerate(chips):
            for cp in group(4 + j, (*chip, 1 - c), me, False):
                cp.wait_recv()
        for cp in first + passed:
            cp.wait_send()
        for cp in mine:
            cp.wait()

    def post(outs):
        whole = []
        for o, s, kind in zip(outs, shards, kinds):
            L, a, b = s.shape
            whole.append(o.reshape(L, N_DEV * a, b) if kind == 'row' else o if kind == 'col' else _to_whole(o, 2))
        return whole

    return _Rider(srcs, [jax.ShapeDtypeStruct(shp, s.dtype) for shp, s in zip(out_shapes, shards)],
                  [pltpu.SemaphoreType.DMA(((N_DEV - 1) * n,)), pltpu.SemaphoreType.DMA(((N_DEV - 1) * n,)),
                   pltpu.SemaphoreType.DMA((n,))], start, finish, post)


def _sibling_swap(x, *, name):
    def body(x_ref, o_ref, send_sems, recv_sems):
        me, sib, _ = _place()
        c = me[2]
        sems = (send_sems, recv_sems)
        sends = [_remote(x_ref.at[j, 1 - c], o_ref.at[j], sems, j, sib) for j in range(N_CHIP)]
        for cp in sends:
            cp.start()
        for cp in sends:
            cp.wait_send()
            cp.wait_recv()

    return pl.pallas_call(
        body, name=name, out_shape=jax.ShapeDtypeStruct((N_CHIP,) + x.shape[2:], x.dtype),
        in_specs=[_HBM], out_specs=_HBM,
        scratch_shapes=[pltpu.SemaphoreType.DMA((N_CHIP,)), pltpu.SemaphoreType.DMA((N_CHIP,))],
        compiler_params=pltpu.CompilerParams(has_side_effects=True),
    )(x)


def _pair_sum(x, got, c, *, name):
    _, _, R, _ = x.shape
    tr = _pick(R, (1024, 512, 256, 128, 64, 32, 16, 8))

    def body(c_ref, x_ref, g_ref, o_ref):
        o_ref[...] = (x_ref[...].astype(F32) + g_ref[...].astype(F32)).astype(o_ref.dtype)

    return pl.pallas_call(
        body, name=name, out_shape=jax.ShapeDtypeStruct((N_CHIP, R, 128), x.dtype),
        grid_spec=pltpu.PrefetchScalarGridSpec(
            num_scalar_prefetch=1, grid=(N_CHIP, R // tr),
            in_specs=[pl.BlockSpec((None, None, tr, 128), lambda j, i, c_ref: (j, c_ref[0], i, 0)),
                      pl.BlockSpec((None, tr, 128), lambda j, i, c_ref: (j, i, 0))],
            out_specs=pl.BlockSpec((None, tr, 128), lambda j, i, c_ref: (j, i, 0))),
        compiler_params=_params(("parallel", "parallel")),
    )(c, x, got)


def _chip_all_to_all_rider(x):
    def parts(refs):
        x_ref, o_ref, send_sems, recv_sems, local_sem = refs
        me, _, chips = _place()
        sems = (send_sems, recv_sems)
        mine = 2 * me[0] + me[1]
        local = pltpu.make_async_copy(x_ref.at[mine], o_ref.at[mine], local_sem)
        sends = [_remote(x_ref.at[2 * px + py], o_ref.at[mine], sems, j, (px, py, me[2])) for j, (px, py) in enumerate(chips)]
        arrival = lambda j: _remote(x_ref.at[mine], o_ref.at[2 * chips[j][0] + chips[j][1]], sems, j, me)
        return local, sends, arrival

    def start(refs):
        local, sends, _ = parts(refs)
        for cp in [local] + sends:
            cp.start()

    def finish(refs):
        local, sends, arrival = parts(refs)
        for j, cp in enumerate(sends):
            cp.wait_send()
            arrival(j).wait_recv()
        local.wait()

    return _Rider([x], [jax.ShapeDtypeStruct(x.shape, x.dtype)],
                  [pltpu.SemaphoreType.DMA((N_CHIP - 1,)), pltpu.SemaphoreType.DMA((N_CHIP - 1,)), pltpu.SemaphoreType.DMA],
                  start, finish, lambda outs: outs[0])


def _sum_slots(x, *, name):
    n, R, _ = x.shape
    tr = _pick(R, (1024, 512, 256, 128, 64, 32, 16, 8))

    def body(x_ref, o_ref):
        acc = x_ref[0].astype(F32)
        for j in range(1, n):
            acc = acc + x_ref[j].astype(F32)
        o_ref[...] = acc

    return pl.pallas_call(
        body, name=name, out_shape=jax.ShapeDtypeStruct((R, 128), F32), grid=(R // tr,),
        in_specs=[pl.BlockSpec((n, tr, 128), lambda i: (0, i, 0))], out_specs=pl.BlockSpec((tr, 128), lambda i: (i, 0)),
        compiler_params=_params(("parallel",)),
    )(x)


def _adamw(w, g, m, v, *, name):
    shape = w.shape
    cols = shape[-1]
    rows = int(np.prod(shape[:-1]))
    tr = next((t for t in (1024, 512, 256, 128, 64, 32, 16, 8) if rows % t == 0 and t * cols * 4 <= (1 << 20)), rows)

    def body(w_ref, g_ref, m_ref, v_ref, d_ref, mo_ref, vo_ref):
        g_ = g_ref[...]
        m_ = ADAM_B1 * m_ref[...] + (1.0 - ADAM_B1) * g_
        v_ = ADAM_B2 * v_ref[...] + (1.0 - ADAM_B2) * jnp.square(g_)
        m_hat = m_ / (1.0 - ADAM_B1 ** ADAM_STEP)
        v_hat = v_ / (1.0 - ADAM_B2 ** ADAM_STEP)
        d_ref[...] = -ADAM_LR * (m_hat / (jnp.sqrt(v_hat) + ADAM_EPS) + ADAM_WD * w_ref[...])
        mo_ref[...] = m_
        vo_ref[...] = v_

    blk = pl.BlockSpec((tr, cols), lambda i: (i, 0))
    outs = pl.pallas_call(
        body, name=name, out_shape=tuple(jax.ShapeDtypeStruct((rows, cols), F32) for _ in range(3)), grid=(rows // tr,),
        in_specs=[blk] * 4, out_specs=(blk,) * 3, compiler_params=_params(("parallel",)),
    )(*(a.reshape(rows, cols) for a in (w, g, m, v)))
    return tuple(o.reshape(shape) for o in outs)


_WEIGHTS = ('g_mix', 'w_in', 'b_fox_forget', 'w_gla_gate', 'b_gla_gate', 'g_gla_out', 'g_mla_q', 'w_mla_uq', 'g_mla_kv',
            'w_mla_ukv', 'b_branch_gate', 'w_up_fox', 'w_up_gla', 'w_up_mla', 'w_out', 'g_xa', 'g_mem', 'w_xq', 'w_xkv',
            'w_xo', 'g_mlp', 'w_mlp1', 'w_mlp2', 'g_final')
_SHARDED = (('w_in', 1), ('w_gla_gate', 2), ('w_mla_uq', 2), ('w_mla_ukv', 2), ('w_up_fox', 2), ('w_up_gla', 2),
            ('w_up_mla', 2), ('w_out', 1), ('w_xq', 1), ('w_xkv', 1), ('w_xo', 2), ('w_mlp1', 2), ('w_mlp2', 1))
_REPLICATED = tuple(n for n in _WEIGHTS if n not in dict(_SHARDED))
_ROW_PAD = 1024
_SMALL_ROW_PAD = 8
_PIECE_ROWS = 16


def _pack(flats, lead, row_pad=_ROW_PAD):
    if all(int(np.prod(a.shape[lead:])) % 128 == 0 for a in flats):
        def block(a):
            a = a.reshape(a.shape[:lead] + (-1, 128))
            return jnp.pad(a, [(0, 0)] * lead + [(0, -a.shape[lead] % _PIECE_ROWS), (0, 0)])
        cat = jnp.concatenate([block(a) for a in flats], axis=lead)
        rows = cat.shape[lead]
        return jnp.pad(cat, [(0, 0)] * lead + [(0, -(-rows // row_pad) * row_pad - rows), (0, 0)])
    cat = jnp.concatenate([a.reshape(a.shape[:lead] + (-1,)) for a in flats], axis=-1)
    n = cat.shape[-1]
    total = -(-n // (128 * row_pad)) * (128 * row_pad)
    cat = jnp.pad(cat, [(0, 0)] * lead + [(0, total - n)])
    return cat.reshape(cat.shape[:lead] + (total // 128, 128))


def _unpack(buf, shapes, lead):
    sizes = [int(np.prod(shp)) for shp in shapes]
    out, off = [], 0
    if all(n % 128 == 0 for n in sizes):
        for shp, n in zip(shapes, sizes):
            rows = buf[(slice(None),) * lead + (slice(off, off + n // 128),)]
            out.append(rows.reshape(buf.shape[:lead] + tuple(shp)))
            off += -(-(n // 128) // _PIECE_ROWS) * _PIECE_ROWS
        return out
    flat = buf.reshape(buf.shape[:lead] + (-1,))
    for shp, n in zip(shapes, sizes):
        out.append(flat[..., off:off + n].reshape(buf.shape[:lead] + tuple(shp)))
        off += n
    return out


def _to_whole(g, axis):
    if axis == 1:
        return g.transpose(1, 0, 2, 3).reshape(g.shape[1], N_DEV * g.shape[2], g.shape[3])
    return g.transpose(1, 2, 0, 3).reshape(g.shape[1], g.shape[2], N_DEV * g.shape[3])


def _to_shards(w, axis):
    L, R, C = w.shape
    if axis == 1:
        return w.reshape(L, N_DEV, R // N_DEV, C).transpose(1, 0, 2, 3)
    return w.reshape(L, R, N_DEV, C // N_DEV).transpose(2, 0, 1, 3)


def kernel(x, mem, g_mix, w_in, b_fox_forget, w_gla_gate, b_gla_gate, g_gla_out, g_mla_q, w_mla_uq, g_mla_kv, w_mla_ukv, b_branch_gate, w_up_fox, w_up_gla, w_up_mla, w_out, g_xa, g_mem, w_xq, w_xkv, w_xo, g_mlp, w_mlp1, w_mlp2, g_final, loss_target, m_g_mix, m_w_in, m_b_fox_forget, m_w_gla_gate, m_b_gla_gate, m_g_gla_out, m_g_mla_q, m_w_mla_uq, m_g_mla_kv, m_w_mla_ukv, m_b_branch_gate, m_w_up_fox, m_w_up_gla, m_w_up_mla, m_w_out, m_g_xa, m_g_mem, m_w_xq, m_w_xkv, m_w_xo, m_g_mlp, m_w_mlp1, m_w_mlp2, m_g_final, v_g_mix, v_w_in, v_b_fox_forget, v_w_gla_gate, v_b_gla_gate, v_g_gla_out, v_g_mla_q, v_w_mla_uq, v_g_mla_kv, v_w_mla_ukv, v_b_branch_gate, v_w_up_fox, v_w_up_gla, v_w_up_mla, v_w_out, v_g_xa, v_g_mem, v_w_xq, v_w_xkv, v_w_xo, v_g_mlp, v_w_mlp1, v_w_mlp2, v_g_final):
    wts = dict(zip(_WEIGHTS, (g_mix, w_in, b_fox_forget, w_gla_gate, b_gla_gate, g_gla_out, g_mla_q, w_mla_uq, g_mla_kv,
                              w_mla_ukv, b_branch_gate, w_up_fox, w_up_gla, w_up_mla, w_out, g_xa, g_mem, w_xq, w_xkv, w_xo,
                              g_mlp, w_mlp1, w_mlp2, g_final)))
    mom1 = dict(zip(_WEIGHTS, (m_g_mix, m_w_in, m_b_fox_forget, m_w_gla_gate, m_b_gla_gate, m_g_gla_out, m_g_mla_q,
                               m_w_mla_uq, m_g_mla_kv, m_w_mla_ukv, m_b_branch_gate, m_w_up_fox, m_w_up_gla, m_w_up_mla,
                               m_w_out, m_g_xa, m_g_mem, m_w_xq, m_w_xkv, m_w_xo, m_g_mlp, m_w_mlp1, m_w_mlp2, m_g_final)))
    mom2 = dict(zip(_WEIGHTS, (v_g_mix, v_w_in, v_b_fox_forget, v_w_gla_gate, v_b_gla_gate, v_g_gla_out, v_g_mla_q,
                               v_w_mla_uq, v_g_mla_kv, v_w_mla_ukv, v_b_branch_gate, v_w_up_fox, v_w_up_gla, v_w_up_mla,
                               v_w_out, v_g_xa, v_g_mem, v_w_xq, v_w_xkv, v_w_xo, v_g_mlp, v_w_mlp1, v_w_mlp2, v_g_final)))
    depth = g_mix.shape[0]

    names = [n for n, _ in _SHARDED]
    axes = dict(_SHARDED)
    shard = {n: wts[n] for n in names}
    shard['w_in'] = w_in[:, :, _IN_IDX] * _IN_SIGN
    rep = {n: wts[n] for n in _REPLICATED}
    ps = [_LayerParams(rep, l) for l in range(depth)]

    def gather(group, l):
        rider = _gather_rider([shard[n][l:l + 1].astype(BF16) for n in group], [axes[n] for n in group])
        return rider, lambda whole: ps[l].w.update({n: w[0] for n, w in zip(group, whole)})

    first, sink = gather(['w_in'], 0)
    sink(_run_rider(first, name="gather_w_in_0"))
    hooks = {(0, 'fox_fwd'): gather([n for n in names if n != 'w_in'], 0)}
    for l in range(1, depth):
        hooks[(l - 1, 'mla_fwd')] = gather(names, l)

    core = lax.axis_index("c").astype(jnp.int32).reshape(1)
    shard_shapes = [(1,) + shard[n].shape[1:] for n in names]
    small_grads, landed = {}, {}

    def layer_done(l, g):
        small_grads[l] = g
        slots = _pack([_to_shards(g[n][None], axes[n]).astype(BF16) for n in names], 1)
        slots = slots.reshape((N_CHIP, 2) + slots.shape[1:])
        paired = _pair_sum(slots, _sibling_swap(slots, name=f"swap_grads_{l}"), core, name=f"pair_grads_{l}")
        rider = _chip_all_to_all_rider(paired)
        if l > 0:
            hooks[(l - 1, 'fox_bwd')] = (rider, lambda got: landed.update({l: got}))
        else:
            landed[l] = _run_rider(rider, name=f"scatter_grads_{l}")

    loss, dx, dg_final = _local_step(x[0], mem[0], loss_target[0], ps, g_final, hooks, layer_done)
    loss = lax.psum(loss[0, 0], _MESH_AXES)

    per_layer = [_unpack(_sum_slots(landed[l], name=f"sum_grads_{l}"), shard_shapes, 0) for l in range(depth)]
    grad = {n: jnp.concatenate([per_layer[l][i] for l in range(depth)], axis=0) for i, n in enumerate(names)}
    grad['w_in'] = grad['w_in'][:, :, _IN_INV] + grad['w_in'][:, :, _IN_INV2] * _IN_SIGN2
    grads = small_grads
    small = [dg_final if n == 'g_final' else jnp.stack([grads[l][n] for l in range(depth)]) for n in _REPLICATED]
    small_shapes = [wts[n].shape for n in _REPLICATED]
    small_sum = _sum_slots(_all_gather(_pack(small, 0, _SMALL_ROW_PAD), name="gather_small_grads"), name="sum_small_grads")
    grad.update(dict(zip(_REPLICATED, _unpack(small_sum, small_shapes, 0))))

    delta, new_m, new_v = {}, {}, {}
    for n, _ in _SHARDED:
        delta[n], new_m[n], new_v[n] = _adamw(wts[n], grad[n], mom1[n], mom2[n], name=f"adamw_{n}")
    packed = [_pack([d[n] for n in _REPLICATED], 0, _SMALL_ROW_PAD) for d in (wts, mom1, mom2)]
    outs = _adamw(packed[0], small_sum, packed[1], packed[2], name="adamw_small")
    for d, o in zip((delta, new_m, new_v), outs):
        d.update(dict(zip(_REPLICATED, _unpack(o, small_shapes, 0))))

    return (loss, dx[None], *[grad[n] for n in _WEIGHTS], *[delta[n] for n in _WEIGHTS],
            *[new_m[n] for n in _WEIGHTS], *[new_v[n] for n in _WEIGHTS])
```

```python
import jax
import jax.numpy as jnp
import numpy as np
from jax import lax
from jax.experimental import pallas as pl
from jax.experimental.pallas import tpu as pltpu

F32 = jnp.float32
BF16 = jnp.bfloat16

EPS = 1e-6
CHUNK = 64
FOX_HEADS, FOX_HD = 4, 64
GLA_HEADS, GLA_DK, GLA_DV, GLA_RANK, GLA_TAU = 4, 64, 128, 16, 16.0
MLA_HEADS, MLA_Q_RANK, MLA_KV_RANK, MLA_NOPE, MLA_ROPE, MLA_VD = 4, 256, 128, 64, 32, 64
ROPE_BASE = 10000.0
XA_HEADS, XA_HD = 4, 128
ADAM_LR, ADAM_B1, ADAM_B2, ADAM_EPS, ADAM_WD, ADAM_STEP = 0.001, 0.9, 0.999, 1e-08, 0.01, 10

N_DEV = 8
V7X_VMEM_LIMIT = 56 * 1024 * 1024
NEG = -1e30

NN = ((1,), (0,))
NT = ((1,), (1,))
TN = ((0,), (0,))


def _dot(a, b, dims):
    return lax.dot_general(a.astype(BF16), b.astype(BF16), (dims, ((), ())), preferred_element_type=F32)


@jax.custom_vjp
def bdot(a, b):
    return _dot(a, b, NN)


bdot.defvjp(lambda a, b: (_dot(a, b, NN), (a, b)),
            lambda res, g: (_dot(g, res[1], NT), _dot(res[0], g, TN)))


@jax.custom_vjp
def bdot_nt(a, b):
    return _dot(a, b, NT)


bdot_nt.defvjp(lambda a, b: (_dot(a, b, NT), (a, b)),
               lambda res, g: (_dot(g, res[1], NN), _dot(g, res[0], TN)))


@jax.custom_vjp
def bdot_tn(a, b):
    return _dot(a, b, TN)


bdot_tn.defvjp(lambda a, b: (_dot(a, b, TN), (a, b)),
               lambda res, g: (_dot(res[1], g, NT), _dot(res[0], g, NN)))


def _split2(x):
    hi = x.astype(BF16)
    lo = (x - hi.astype(F32)).astype(BF16)
    return hi, lo


def _tri(n, lower):
    r = lax.broadcasted_iota(jnp.int32, (n, n), 0)
    c = lax.broadcasted_iota(jnp.int32, (n, n), 1)
    return jnp.where((r >= c) if lower else (r <= c), 1.0, 0.0).astype(BF16)


def _tri_dot2(x, lower):
    hi, lo = _split2(x)
    t = _tri(x.shape[0], lower)
    return _dot(t, hi, NN) + _dot(t, lo, NN)


@jax.custom_vjp
def chunk_cumsum(x):
    return _tri_dot2(x, True)


chunk_cumsum.defvjp(lambda x: (_tri_dot2(x, True), None), lambda _, g: (_tri_dot2(g, False),))


def _log_sigmoid(x):
    return jnp.minimum(x, 0.0) - jnp.log(1.0 + jnp.exp(-jnp.abs(x)))


def _sigmoid(x):
    return 1.0 / (1.0 + jnp.exp(-x))


def _rms(x, g):
    return x * lax.rsqrt(jnp.mean(x * x, axis=-1, keepdims=True) + EPS) * g


def _pick(dim, prefs):
    for p in prefs:
        if dim % p == 0:
            return p
    return dim


def _params(sem):
    return pltpu.CompilerParams(dimension_semantics=sem, vmem_limit_bytes=V7X_VMEM_LIMIT)


def _mm(a, b, *, mode, out_dtype, name, act=None, residual=None, drelu_of=None, b_cols=None, tm=None, tn=None, tk=None):
    b_off, b_width = b_cols or (0, b.shape[1])
    if mode == 'nn':
        (M, K), N = a.shape, b_width
    elif mode == 'nt':
        (M, K), N = a.shape, b.shape[0]
    else:
        (K, M), N = a.shape, b_width
    tm = tm or _pick(M, (1024, 512, 256, 128))
    tn = tn or _pick(N, (1024, 1920, 1152, 768, 640, 512, 384, 256, 128))
    tk = tk or _pick(K, (1024, 1920, 1152, 640, 512, 256, 128))
    nk = K // tk
    dims = {'nn': NN, 'nt': NT, 'tn': TN}[mode]
    a_spec = pl.BlockSpec((tk, tm), lambda i, j, k: (k, i)) if mode == 'tn' else pl.BlockSpec((tm, tk), lambda i, j, k: (i, k))
    if mode == 'nt':
        b_spec = pl.BlockSpec((tn, tk), lambda i, j, k, o=b_off // tk: (j, k + o))
    else:
        b_spec = pl.BlockSpec((tk, tn), lambda i, j, k, o=b_off // tn: (k, j + o))
    o_spec = pl.BlockSpec((tm, tn), lambda i, j, k: (i, j))
    extra = [e for e in (residual, drelu_of) if e is not None]

    def body(a_ref, b_ref, *rest):
        o_ref = rest[len(extra)]
        at = a_ref[...]
        if act == 'relu2':
            at = jnp.square(jnp.maximum(at.astype(F32), 0.0))
        part = _dot(at, b_ref[...], dims)

        def finish(acc):
            idx = 0
            if residual is not None:
                acc = acc + rest[idx][...]
                idx += 1
            if drelu_of is not None:
                acc = acc * (2.0 * jnp.maximum(rest[idx][...].astype(F32), 0.0))
            o_ref[...] = acc.astype(out_dtype)

        if nk == 1:
            finish(part)
        else:
            acc_ref = rest[len(extra) + 1]
            k = pl.program_id(2)

            @pl.when(k == 0)
            def _():
                acc_ref[...] = part

            @pl.when(k > 0)
            def _():
                acc_ref[...] += part

            @pl.when(k == nk - 1)
            def _():
                finish(acc_ref[...])

    return pl.pallas_call(
        body, name=name,
        out_shape=jax.ShapeDtypeStruct((M, N), out_dtype),
        grid=(M // tm, N // tn, nk),
        in_specs=[a_spec, b_spec] + [o_spec] * len(extra),
        out_specs=o_spec,
        scratch_shapes=[] if nk == 1 else [pltpu.VMEM((tm, tn), F32)],
        compiler_params=_params(("parallel", "parallel", "arbitrary")),
    )(a, b, *extra)


def _rms_fwd(x, g, *, name, out_dtype=BF16):
    S, D = x.shape
    tr = _pick(S, (512, 256, 128))

    def body(x_ref, g_ref, o_ref):
        o_ref[...] = _rms(x_ref[...], g_ref[...]).astype(out_dtype)

    return pl.pallas_call(
        body, name=name, out_shape=jax.ShapeDtypeStruct((S, D), out_dtype), grid=(S // tr,),
        in_specs=[pl.BlockSpec((tr, D), lambda i: (i, 0)), pl.BlockSpec((1, D), lambda i: (0, 0))],
        out_specs=pl.BlockSpec((tr, D), lambda i: (i, 0)),
        compiler_params=_params(("parallel",)),
    )(x, g.reshape(1, D))


def _rms_bwd(x, g, dy, dres, *, name):
    S, D = x.shape
    tr = _pick(S, (512, 256, 128))

    def body(x_ref, g_ref, dy_ref, *rest):
        dx_ref, dxb_ref, dg_ref = rest[-3], rest[-2], rest[-1]
        x_ = x_ref[...]
        rstd = lax.rsqrt(jnp.mean(x_ * x_, axis=-1, keepdims=True) + EPS)
        xh = x_ * rstd
        dy_ = dy_ref[...].astype(F32)
        gdy = dy_ * g_ref[...]
        dx = (gdy - xh * jnp.mean(gdy * xh, axis=-1, keepdims=True)) * rstd
        if dres is not None:
            dx = dx + rest[0][...]
        dx_ref[...] = dx
        dxb_ref[...] = dx.astype(BF16)
        part = jnp.sum(dy_ * xh, axis=0, keepdims=True)

        @pl.when(pl.program_id(0) == 0)
        def _():
            dg_ref[...] = part

        @pl.when(pl.program_id(0) > 0)
        def _():
            dg_ref[...] += part

    row = pl.BlockSpec((tr, D), lambda i: (i, 0))
    vec = pl.BlockSpec((1, D), lambda i: (0, 0))
    ins = [x, g.reshape(1, D), dy] + ([dres] if dres is not None else [])
    return pl.pallas_call(
        body, name=name,
        out_shape=(jax.ShapeDtypeStruct((S, D), F32), jax.ShapeDtypeStruct((S, D), BF16), jax.ShapeDtypeStruct((1, D), F32)),
        grid=(S // tr,),
        in_specs=[row, vec, row] + ([row] if dres is not None else []),
        out_specs=(row, row, vec),
        compiler_params=_params(("arbitrary",)),
    )(*ins)


def _loss_head(x, g, target, *, name):
    S, D = x.shape
    tr = _pick(S, (512, 256, 128))

    def body(x_ref, g_ref, t_ref, l_ref, dx_ref, dxb_ref, dg_ref):
        x_ = x_ref[...]
        g_ = g_ref[...]
        rstd = lax.rsqrt(jnp.mean(x_ * x_, axis=-1, keepdims=True) + EPS)
        xh = x_ * rstd
        err = xh * g_ - t_ref[...]
        lpart = (0.5 / D) * jnp.sum(jnp.sum(err * err, axis=-1, keepdims=True), axis=0, keepdims=True)
        dy = err * (1.0 / D)
        gdy = dy * g_
        dx = (gdy - xh * jnp.mean(gdy * xh, axis=-1, keepdims=True)) * rstd
        dx_ref[...] = dx
        dxb_ref[...] = dx.astype(BF16)
        gpart = jnp.sum(dy * xh, axis=0, keepdims=True)

        @pl.when(pl.program_id(0) == 0)
        def _():
            dg_ref[...] = gpart
            l_ref[...] = lpart

        @pl.when(pl.program_id(0) > 0)
        def _():
            dg_ref[...] += gpart
            l_ref[...] += lpart

    row = pl.BlockSpec((tr, D), lambda i: (i, 0))
    vec = pl.BlockSpec((1, D), lambda i: (0, 0))
    return pl.pallas_call(
        body, name=name,
        out_shape=(jax.ShapeDtypeStruct((1, 1), F32), jax.ShapeDtypeStruct((S, D), F32), jax.ShapeDtypeStruct((S, D), BF16),
                   jax.ShapeDtypeStruct((1, D), F32)),
        grid=(S // tr,),
        in_specs=[row, vec, row],
        out_specs=(pl.BlockSpec((1, 1), lambda i: (0, 0)), row, row, vec),
        compiler_params=_params(("arbitrary",)),
    )(x, g.reshape(1, D), target)


def _mask_of(mask, tq, tk):
    qpos = lax.broadcasted_iota(jnp.int32, (tq, tk), 0)
    kpos = lax.broadcasted_iota(jnp.int32, (tq, tk), 1)
    if mask == 'causal':
        return kpos <= qpos
    return kpos <= (qpos | (CHUNK - 1))


LANES = 128


def _lane_group(j, w, width):
    lane = lax.broadcasted_iota(jnp.int32, (1, width), 1)
    return (lane >= j * w) & (lane < (j + 1) * w)


def _only(x, j, w):
    if w == x.shape[1]:
        return x
    return jnp.where(_lane_group(j, w, x.shape[1]), x, jnp.zeros_like(x))


def _per_head(cols, w):
    out = cols[-1]
    for j in range(len(cols) - 2, -1, -1):
        out = jnp.where(_lane_group(j, w, LANES), cols[j], out)
    return out


def _side_by_side(xs):
    return xs[0] if len(xs) == 1 else jnp.concatenate(xs, axis=1)


def _on_top(xs):
    return xs[0] if len(xs) == 1 else jnp.concatenate(xs, axis=0)


def _stacked(x, hp, w):
    return _on_top([_only(x, j, w) for j in range(hp)])


def _col_block(entry, rows, idx):
    arr, off, width = entry
    return pl.BlockSpec((rows, width), lambda i, j, o=off // width: (idx(i, j), o))


def _attn_fwd(qk, v, H, cq, ck, *, scale, mask, name, rider=None):
    Sq, Sk = qk[0][0][0].shape[0], v[0].shape[0]
    dv = v[2] // H
    w0 = qk[0][2]
    hp = LANES // w0
    G = H // hp
    assert dv == w0 and not qk[0][3] and all(sh and H * w == LANES for _, _, w, sh in qk[1:])
    tq = _pick(Sq, (512, 256, 128))
    tk = tq if mask else _pick(Sk, (512, 256, 128))
    nq, nk = Sq // tq, Sk // tk
    bias = cq is not None
    npart = len(qk)

    def body(*refs):
        refs = split(refs)
        q_refs, k_refs = refs[0:2 * npart:2], refs[1:2 * npart:2]
        v_ref = refs[2 * npart]
        cq_ref, ck_ref = (refs[2 * npart + 1], refs[2 * npart + 2]) if bias else (None, None)
        o_ref, lse_ref, m_s, l_s, acc_s = refs[-5:]
        qi, ki = pl.program_id(0), pl.program_id(1)

        @pl.when(ki == 0)
        def _():
            m_s[...] = jnp.full(m_s.shape, NEG, F32)
            l_s[...] = jnp.zeros(l_s.shape, F32)
            acc_s[...] = jnp.zeros(acc_s.shape, F32)

        def compute(masked):
            keep = _mask_of(mask, tq, tk) if masked else None
            for g in range(G):
                lanes = slice(g * LANES, (g + 1) * LANES)
                q128, k128, v128 = q_refs[0][:, lanes], k_refs[0][:, lanes], v_ref[:, lanes]
                ps, alphas = [], []
                for j in range(hp):
                    h = g * hp + j
                    s = _dot(_only(q128, j, w0), k128, NT)
                    for (_, _, w, _), q_ref, k_ref in list(zip(qk, q_refs, k_refs))[1:]:
                        s = s + _dot(_only(q_ref[...], h, w), k_ref[...], NT)
                    s = s * scale
                    if bias:
                        s = s + (cq_ref[:, h:h + 1] - ck_ref[h:h + 1, :])
                    if masked:
                        s = jnp.where(keep, s, NEG)
                    m_prev = m_s[h]
                    m_new = jnp.maximum(m_prev, jnp.max(s, axis=1, keepdims=True))
                    alpha = jnp.exp(m_prev - m_new)
                    p = jnp.exp(s - m_new)
                    l_s[h] = alpha * l_s[h] + jnp.sum(p, axis=1, keepdims=True)
                    m_s[h] = m_new
                    ps.append(p.astype(BF16))
                    alphas.append(alpha)
                acc_s[g] = _per_head(alphas, w0) * acc_s[g] + _dot(_side_by_side(ps), _stacked(v128, hp, w0), NN)

        if mask is None:
            compute(False)
        else:
            pl.when(ki < qi)(lambda: compute(False))
            pl.when(ki == qi)(lambda: compute(True))

        @pl.when(ki == ((nk - 1) if mask is None else qi))
        def _():
            lse_ref[...] = jnp.zeros(lse_ref.shape, F32)
            for g in range(G):
                o_ref[:, g * LANES:(g + 1) * LANES] = (
                    acc_s[g] / _per_head([l_s[g * hp + j] for j in range(hp)], w0)).astype(BF16)
            for h in range(H):
                lse_ref[:, h:h + 1] = m_s[h] + jnp.log(l_s[h])

    q_idx = lambda i, j: i
    k_idx = (lambda i, j: jnp.minimum(i, j)) if mask else (lambda i, j: j)
    ins, in_specs = [], []
    for q_e, k_e, _, _ in qk:
        ins += [q_e[0], k_e[0]]
        in_specs += [_col_block(q_e, tq, q_idx), _col_block(k_e, tk, k_idx)]
    ins.append(v[0])
    in_specs.append(_col_block(v, tk, k_idx))
    if bias:
        in_specs += [pl.BlockSpec((tq, 8), lambda i, j: (i, 0)), pl.BlockSpec((8, tk), lambda i, j: (0, k_idx(i, j)))]
        ins += [cq, ck]
    r_ins, r_in_specs, r_outs, r_out_specs, r_scratch, split = _carry(
        rider, len(ins), 2, lambda: (pl.program_id(0) == 0) & (pl.program_id(1) == 0),
        lambda: (pl.program_id(0) == nq - 1) & (pl.program_id(1) == nk - 1))
    res = pl.pallas_call(
        body, name=name,
        out_shape=(jax.ShapeDtypeStruct((Sq, H * dv), BF16), jax.ShapeDtypeStruct((Sq, 8), F32), *r_outs),
        grid=(nq, nk), in_specs=in_specs + r_in_specs,
        out_specs=(pl.BlockSpec((tq, H * dv), lambda i, j: (i, 0)), pl.BlockSpec((tq, 8), lambda i, j: (i, 0)), *r_out_specs),
        scratch_shapes=[pltpu.VMEM((H, tq, 1), F32), pltpu.VMEM((H, tq, 1), F32), pltpu.VMEM((G, tq, LANES), F32)] + r_scratch,
        compiler_params=_params(("arbitrary", "arbitrary")) if rider else _params(("parallel", "arbitrary")),
    )(*ins, *r_ins)
    return (res[0], res[1], rider.post(res[2:])) if rider else res


def _attn_bwd(qk, v, H, o, do, lse, cq, ck, *, scale, mask, name, rider=None):
    Sq, Sk = qk[0][0][0].shape[0], v[0].shape[0]
    dv = v[2] // H
    w0 = qk[0][2]
    hp = LANES // w0
    G = H // hp
    tq = _pick(Sq, (512, 256, 128))
    tk = tq if mask else _pick(Sk, (512, 256, 128))
    nq, nk = Sq // tq, Sk // tk
    bias = cq is not None
    npart = len(qk)
    n_in = 2 * npart + 4 + (2 if bias else 0)

    def body(*refs):
        refs = split(refs)
        q_refs, k_refs = refs[0:2 * npart:2], refs[1:2 * npart:2]
        v_ref, o_ref, do_ref, lse_ref = refs[2 * npart:2 * npart + 4]
        cq_ref, ck_ref = (refs[2 * npart + 4], refs[2 * npart + 5]) if bias else (None, None)
        outs = refs[n_in:]
        dq_refs, dk_refs, dv_ref = outs[:npart], outs[npart:2 * npart], outs[2 * npart]
        dck_ref, dcq_ref = (outs[2 * npart + 1], outs[2 * npart + 2]) if bias else (None, None)
        dk_accs, dv_acc = refs[-(npart + 1):-1], refs[-1]
        ki, qi = pl.program_id(0), pl.program_id(1)
        first_q = ki if mask else 0

        @pl.when((ki == 0) & (qi == 0))
        def _():
            for r in dq_refs:
                r[...] = jnp.zeros(r.shape, F32)
            if bias:
                dcq_ref[...] = jnp.zeros(dcq_ref.shape, F32)

        @pl.when(qi == first_q)
        def _():
            for r in dk_accs:
                r[...] = jnp.zeros(r.shape, F32)
            dv_acc[...] = jnp.zeros(dv_acc.shape, F32)
            if bias:
                dck_ref[...] = jnp.zeros(dck_ref.shape, F32)

        def compute(masked):
            keep = _mask_of(mask, tq, tk) if masked else None
            rows = pl.ds(pl.multiple_of(qi * tq, tq), tq)
            extras = list(zip(qk, q_refs, k_refs, dq_refs, dk_accs))[1:]
            for g in range(G):
                lanes = slice(g * LANES, (g + 1) * LANES)
                q128, k128, v128 = q_refs[0][:, lanes], k_refs[0][:, lanes], v_ref[:, lanes]
                do128, o128 = do_ref[:, lanes], o_ref[:, lanes]
                ps, dss = [], []
                for j in range(hp):
                    h = g * hp + j
                    s = _dot(_only(q128, j, w0), k128, NT)
                    for (_, _, w, _), q_ref, k_ref, _, _ in extras:
                        s = s + _dot(_only(q_ref[...], h, w), k_ref[...], NT)
                    s = s * scale
                    if bias:
                        s = s + (cq_ref[:, h:h + 1] - ck_ref[h:h + 1, :])
                    if masked:
                        s = jnp.where(keep, s, NEG)
                    p = jnp.exp(s - lse_ref[:, h:h + 1])
                    doh = _only(do128, j, w0)
                    dp = _dot(doh, v128, NT)
                    delta = jnp.sum(doh.astype(F32) * o128.astype(F32), axis=1, keepdims=True)
                    ds = p * (dp - delta)
                    if bias:
                        dck_ref[h:h + 1, :] -= jnp.sum(ds, axis=0, keepdims=True)
                        dcq_ref[rows, h:h + 1] += jnp.sum(ds, axis=1, keepdims=True)
                    ds = (ds * scale).astype(BF16)
                    for (_, _, w, _), q_ref, k_ref, dq_ref, dk_acc in extras:
                        dk_acc[...] += _dot(ds, _only(q_ref[...], h, w), TN)
                        dq_ref[rows, :] += _only(_dot(ds, k_ref[...], NN), h, w)
                    ps.append(p.astype(BF16))
                    dss.append(ds)
                dv_acc[:, lanes] += _dot(_on_top(ps), _stacked(do128, hp, w0), TN)
                dk_accs[0][:, lanes] += _dot(_on_top(dss), _stacked(q128, hp, w0), TN)
                dq_refs[0][rows, lanes] += _dot(_side_by_side(dss), _stacked(k128, hp, w0), NN)

        if mask is None:
            compute(False)
        else:
            pl.when(qi > ki)(lambda: compute(False))
            pl.when(qi == ki)(lambda: compute(True))

        @pl.when(qi == nq - 1)
        def _():
            for r, acc in zip(dk_refs, dk_accs):
                r[...] = acc[...]
            dv_ref[...] = dv_acc[...]

    q_idx = (lambda j, i: jnp.maximum(i, j)) if mask else (lambda j, i: i)
    k_idx = lambda j, i: j
    ins, in_specs, dq_shapes, dq_specs, dk_shapes, dk_specs, scratch = [], [], [], [], [], [], []
    for q_e, k_e, w, shared in qk:
        ins += [q_e[0], k_e[0]]
        in_specs += [_col_block(q_e, tq, q_idx), _col_block(k_e, tk, k_idx)]
        dq_shapes.append(jax.ShapeDtypeStruct((Sq, H * w), F32))
        dq_specs.append(pl.BlockSpec((Sq, H * w), lambda j, i: (0, 0)))
        kw = k_e[2]
        dk_shapes.append(jax.ShapeDtypeStruct((Sk, kw), F32))
        dk_specs.append(pl.BlockSpec((tk, kw), lambda j, i: (j, 0)))
        scratch.append(pltpu.VMEM((tk, kw), F32))
    row_q = lambda width: pl.BlockSpec((tq, width), lambda j, i: (q_idx(j, i), 0))
    ins += [v[0], o, do, lse]
    in_specs += [_col_block(v, tk, k_idx), row_q(H * dv), row_q(H * dv), row_q(8)]
    out_shape = dq_shapes + dk_shapes + [jax.ShapeDtypeStruct((Sk, H * dv), F32)]
    out_specs = dq_specs + dk_specs + [pl.BlockSpec((tk, H * dv), lambda j, i: (j, 0))]
    if bias:
        in_specs += [row_q(8), pl.BlockSpec((8, tk), lambda j, i: (0, j))]
        ins += [cq, ck]
        out_shape += [jax.ShapeDtypeStruct((8, Sk), F32), jax.ShapeDtypeStruct((Sq, 8), F32)]
        out_specs += [pl.BlockSpec((8, tk), lambda j, i: (0, j)), pl.BlockSpec((Sq, 8), lambda j, i: (0, 0))]
    scratch.append(pltpu.VMEM((tk, H * dv), F32))
    n_out = len(out_shape)
    r_ins, r_in_specs, r_outs, r_out_specs, r_scratch, split = _carry(
        rider, len(ins), n_out, lambda: (pl.program_id(0) == 0) & (pl.program_id(1) == 0),
        lambda: (pl.program_id(0) == nk - 1) & (pl.program_id(1) == nq - 1))
    res = pl.pallas_call(
        body, name=name, out_shape=tuple(out_shape + r_outs), grid=(nk, nq), in_specs=in_specs + r_in_specs,
        out_specs=tuple(out_specs + r_out_specs), scratch_shapes=scratch + r_scratch,
        compiler_params=_params(("arbitrary", "arbitrary")),
    )(*ins, *r_ins)
    own = (list(res[:npart]), list(res[npart:2 * npart]), res[2 * npart]) + tuple(res[2 * npart + 1:n_out])
    return own + (rider.post(res[n_out:]),) if rider else own


def _flash_fwd(q, k, v, cq, ck, *, scale, mask, name):
    H, Sq, dk = q.shape
    Sk, dv = k.shape[1], v.shape[2]
    tq = _pick(Sq, (512, 256, 128))
    tk = tq if mask else _pick(Sk, (512, 256, 128))
    nq, nk = Sq // tq, Sk // tk
    bias = cq is not None

    def body(*refs):
        q_ref, k_ref, v_ref = refs[:3]
        cq_ref, ck_ref = (refs[3], refs[4]) if bias else (None, None)
        o_ref, lse_ref, m_s, l_s, acc_s = refs[-5:]
        qi, ki = pl.program_id(0), pl.program_id(1)

        @pl.when(ki == 0)
        def _():
            m_s[...] = jnp.full(m_s.shape, NEG, F32)
            l_s[...] = jnp.zeros(l_s.shape, F32)
            acc_s[...] = jnp.zeros(acc_s.shape, F32)

        def compute(masked):
            keep = _mask_of(mask, tq, tk) if masked else None
            for h in range(H):
                s = _dot(q_ref[h], k_ref[h], NT) * scale
                if bias:
                    s = s + (cq_ref[:, h:h + 1] - ck_ref[h:h + 1, :])
                if masked:
                    s = jnp.where(keep, s, NEG)
                m_prev = m_s[h]
                m_new = jnp.maximum(m_prev, jnp.max(s, axis=1, keepdims=True))
                alpha = jnp.exp(m_prev - m_new)
                p = jnp.exp(s - m_new)
                l_s[h] = alpha * l_s[h] + jnp.sum(p, axis=1, keepdims=True)
                acc_s[h] = alpha * acc_s[h] + _dot(p, v_ref[h], NN)
                m_s[h] = m_new

        if mask is None:
            compute(False)
        else:
            pl.when(ki < qi)(lambda: compute(False))
            pl.when(ki == qi)(lambda: compute(True))

        @pl.when(ki == ((nk - 1) if mask is None else qi))
        def _():
            lse_ref[...] = jnp.zeros(lse_ref.shape, F32)
            for h in range(H):
                o_ref[h] = (acc_s[h] / l_s[h]).astype(BF16)
                lse_ref[:, h:h + 1] = m_s[h] + jnp.log(l_s[h])

    kv_idx = (lambda i, j: (0, jnp.minimum(i, j), 0)) if mask else (lambda i, j: (0, j, 0))
    ck_idx = (lambda i, j: (0, jnp.minimum(i, j))) if mask else (lambda i, j: (0, j))
    in_specs = [pl.BlockSpec((H, tq, dk), lambda i, j: (0, i, 0)),
                pl.BlockSpec((H, tk, dk), kv_idx), pl.BlockSpec((H, tk, dv), kv_idx)]
    ins = [q, k, v]
    if bias:
        in_specs += [pl.BlockSpec((tq, 8), lambda i, j: (i, 0)), pl.BlockSpec((8, tk), ck_idx)]
        ins += [cq, ck]
    return pl.pallas_call(
        body, name=name,
        out_shape=(jax.ShapeDtypeStruct((H, Sq, dv), BF16), jax.ShapeDtypeStruct((Sq, 8), F32)),
        grid=(nq, nk), in_specs=in_specs,
        out_specs=(pl.BlockSpec((H, tq, dv), lambda i, j: (0, i, 0)), pl.BlockSpec((tq, 8), lambda i, j: (i, 0))),
        scratch_shapes=[pltpu.VMEM((H, tq, 1), F32), pltpu.VMEM((H, tq, 1), F32), pltpu.VMEM((H, tq, dv), F32)],
        compiler_params=_params(("parallel", "arbitrary")),
    )(*ins)


def _flash_bwd(q, k, v, o, do, lse, cq, ck, *, scale, mask, name):
    H, Sq, dk = q.shape
    Sk, dv = k.shape[1], v.shape[2]
    tq = _pick(Sq, (512, 256, 128))
    tk = tq if mask else _pick(Sk, (512, 256, 128))
    nq, nk = Sq // tq, Sk // tk
    bias = cq is not None

    def body(*refs):
        q_ref, k_ref, v_ref, o_ref, do_ref, lse_ref = refs[:6]
        n_in = 8 if bias else 6
        cq_ref, ck_ref = (refs[6], refs[7]) if bias else (None, None)
        outs = refs[n_in:]
        dq_ref, dk_ref, dv_ref = outs[:3]
        dck_ref, dcq_ref = (outs[3], outs[4]) if bias else (None, None)
        dk_s, dv_s = refs[-2], refs[-1]
        ki, qi = pl.program_id(0), pl.program_id(1)
        first_q = ki if mask else 0

        @pl.when((ki == 0) & (qi == 0))
        def _():
            dq_ref[...] = jnp.zeros(dq_ref.shape, F32)
            if bias:
                dcq_ref[...] = jnp.zeros(dcq_ref.shape, F32)

        @pl.when(qi == first_q)
        def _():
            dk_s[...] = jnp.zeros(dk_s.shape, F32)
            dv_s[...] = jnp.zeros(dv_s.shape, F32)
            if bias:
                dck_ref[...] = jnp.zeros(dck_ref.shape, F32)

        def compute(masked):
            keep = _mask_of(mask, tq, tk) if masked else None
            rows = pl.ds(pl.multiple_of(qi * tq, tq), tq)
            for h in range(H):
                qh, kh, vh, doh = q_ref[h], k_ref[h], v_ref[h], do_ref[h]
                s = _dot(qh, kh, NT) * scale
                if bias:
                    s = s + (cq_ref[:, h:h + 1] - ck_ref[h:h + 1, :])
                if masked:
                    s = jnp.where(keep, s, NEG)
                p = jnp.exp(s - lse_ref[:, h:h + 1])
                dp = _dot(doh, vh, NT)
                delta = jnp.sum(doh.astype(F32) * o_ref[h].astype(F32), axis=1, keepdims=True)
                ds = p * (dp - delta)
                dv_s[h] += _dot(p, doh, TN)
                dk_s[h] += _dot(ds, qh, TN)
                dq_ref[h, rows, :] += _dot(ds, kh, NN) * scale
                if bias:
                    dck_ref[h:h + 1, :] -= jnp.sum(ds, axis=0, keepdims=True)
                    dcq_ref[rows, h:h + 1] += jnp.sum(ds, axis=1, keepdims=True)

        if mask is None:
            compute(False)
        else:
            pl.when(qi > ki)(lambda: compute(False))
            pl.when(qi == ki)(lambda: compute(True))

        @pl.when(qi == nq - 1)
        def _():
            dk_ref[...] = dk_s[...] * scale
            dv_ref[...] = dv_s[...]

    q_idx = (lambda j, i: (0, jnp.maximum(i, j), 0)) if mask else (lambda j, i: (0, i, 0))
    c_idx = (lambda j, i: (jnp.maximum(i, j), 0)) if mask else (lambda j, i: (i, 0))
    kv_idx = lambda j, i: (0, j, 0)
    in_specs = [pl.BlockSpec((H, tq, dk), q_idx), pl.BlockSpec((H, tk, dk), kv_idx), pl.BlockSpec((H, tk, dv), kv_idx),
                pl.BlockSpec((H, tq, dv), q_idx), pl.BlockSpec((H, tq, dv), q_idx), pl.BlockSpec((tq, 8), c_idx)]
    ins = [q, k, v, o, do, lse]
    out_shape = [jax.ShapeDtypeStruct((H, Sq, dk), F32), jax.ShapeDtypeStruct((H, Sk, dk), F32),
                 jax.ShapeDtypeStruct((H, Sk, dv), F32)]
    out_specs = [pl.BlockSpec((H, Sq, dk), lambda j, i: (0, 0, 0)), pl.BlockSpec((H, tk, dk), kv_idx),
                 pl.BlockSpec((H, tk, dv), kv_idx)]
    if bias:
        in_specs += [pl.BlockSpec((tq, 8), c_idx), pl.BlockSpec((8, tk), lambda j, i: (0, j))]
        ins += [cq, ck]
        out_shape += [jax.ShapeDtypeStruct((8, Sk), F32), jax.ShapeDtypeStruct((Sq, 8), F32)]
        out_specs += [pl.BlockSpec((8, tk), lambda j, i: (0, j)), pl.BlockSpec((Sq, 8), lambda j, i: (0, 0))]
    return pl.pallas_call(
        body, name=name, out_shape=tuple(out_shape), grid=(nk, nq), in_specs=in_specs, out_specs=tuple(out_specs),
        scratch_shapes=[pltpu.VMEM((H, tk, dk), F32), pltpu.VMEM((H, tk, dv), F32)],
        compiler_params=_params(("arbitrary", "arbitrary")),
    )(*ins)


def _split3_dot(x, t):
    hi = x.astype(BF16)
    r1 = x - hi.astype(F32)
    mid = r1.astype(BF16)
    lo = (r1 - mid.astype(F32)).astype(BF16)
    return _dot(hi, t, NN) + _dot(mid, t, NN) + _dot(lo, t, NN)


def _fox_cum_fwd(ff_t, b, *, name):
    _, S = ff_t.shape
    tb = _pick(S, (512, 256, 128))

    def body(f_ref, b_ref, o_ref, carry):
        @pl.when(pl.program_id(0) == 0)
        def _():
            carry[...] = jnp.zeros(carry.shape, F32)

        lf = _log_sigmoid(f_ref[...] + b_ref[...])
        o_ref[...] = _split3_dot(lf, _tri(tb, False)) + carry[...]
        carry[...] += jnp.sum(lf, axis=1, keepdims=True)

    return pl.pallas_call(
        body, name=name, out_shape=jax.ShapeDtypeStruct((8, S), F32), grid=(S // tb,),
        in_specs=[pl.BlockSpec((8, tb), lambda i: (0, i)), pl.BlockSpec((8, 1), lambda i: (0, 0))],
        out_specs=pl.BlockSpec((8, tb), lambda i: (0, i)),
        scratch_shapes=[pltpu.VMEM((8, 1), F32)],
        compiler_params=_params(("arbitrary",)),
    )(ff_t, b)


def _fox_cum_bwd(ff_t, b, dcum_t, *, name):
    _, S = ff_t.shape
    tb = _pick(S, (512, 256, 128))
    nb = S // tb

    def body(f_ref, b_ref, dc_ref, df_ref, db_ref, carry):
        @pl.when(pl.program_id(0) == 0)
        def _():
            carry[...] = jnp.zeros(carry.shape, F32)
            db_ref[...] = jnp.zeros(db_ref.shape, F32)

        dc = dc_ref[...]
        dlf = _split3_dot(dc, _tri(tb, True)) + carry[...]
        carry[...] += jnp.sum(dc, axis=1, keepdims=True)
        df = dlf * _sigmoid(-(f_ref[...] + b_ref[...]))
        df_ref[...] = df
        db_ref[...] += jnp.sum(df, axis=1, keepdims=True)

    rev = lambda i: (0, nb - 1 - i)
    return pl.pallas_call(
        body, name=name,
        out_shape=(jax.ShapeDtypeStruct((8, S), F32), jax.ShapeDtypeStruct((8, 1), F32)), grid=(nb,),
        in_specs=[pl.BlockSpec((8, tb), rev), pl.BlockSpec((8, 1), lambda i: (0, 0)), pl.BlockSpec((8, tb), rev)],
        out_specs=(pl.BlockSpec((8, tb), rev), pl.BlockSpec((8, 1), lambda i: (0, 0))),
        scratch_shapes=[pltpu.VMEM((8, 1), F32)],
        compiler_params=_params(("arbitrary",)),
    )(ff_t, b, dcum_t)


GLA_W = GLA_HEADS * GLA_DK
GLA_BLOCK_CHUNKS = 4


def _gla_chunk(q, k, zsm, wg, bg, go, vs, rs, states):
    la = _log_sigmoid(bdot(zsm, wg) + bg) * (1.0 / GLA_TAU)
    cum = chunk_cumsum(la)
    end = jnp.sum(la, axis=0, keepdims=True)
    kd = k * jnp.exp(end - cum)
    a = jnp.exp(end)
    qs = q * (GLA_DK ** -0.5)
    lane = lax.broadcasted_iota(jnp.int32, (1, GLA_W), 1)
    outs, new_states = [], []
    for h in range(GLA_HEADS):
        head = jnp.where((lane >= h * GLA_DK) & (lane < (h + 1) * GLA_DK), 1.0, 0.0)
        st = states[h] * a + bdot_tn(vs[h], kd * head)
        o = bdot_nt(qs, st)
        o = _rms(o, go)
        outs.append(o * (rs[h] * _sigmoid(rs[h])))
        new_states.append(st)
    return outs, new_states


def _gla_fwd(z, zsm, wg, bg, go, cols, *, name):
    S = z.shape[0]
    rb = GLA_BLOCK_CHUNKS * CHUNK
    nb = S // rb
    cq, ckk, cv, cr = cols
    H = GLA_HEADS

    def body(q_ref, k_ref, zsm_ref, wg_ref, bg_ref, go_ref, *rest):
        v_refs, r_refs = rest[:H], rest[H:2 * H]
        o_ref, st_ref, state = rest[2 * H], rest[2 * H + 1], rest[2 * H + 2]

        @pl.when(pl.program_id(0) == 0)
        def _():
            state[...] = jnp.zeros(state.shape, F32)

        wg_, bg_, go_ = wg_ref[...], bg_ref[...], go_ref[...]
        for c in range(GLA_BLOCK_CHUNKS):
            rows = pl.ds(c * CHUNK, CHUNK)
            states = [state[h] for h in range(H)]
            for h in range(H):
                st_ref[c, h] = states[h]
            outs, new_states = _gla_chunk(
                q_ref[rows, :].astype(F32), k_ref[rows, :].astype(F32), zsm_ref[rows, :], wg_, bg_, go_,
                [v_refs[h][rows, :].astype(F32) for h in range(H)], [r_refs[h][rows, :].astype(F32) for h in range(H)], states)
            for h in range(H):
                o_ref[rows, h * GLA_DV:(h + 1) * GLA_DV] = outs[h].astype(BF16)
                state[h] = new_states[h]

    def col(width, off):
        return pl.BlockSpec((rb, width), lambda i, o=off // width: (i, o))

    full = lambda shp: pl.BlockSpec(shp, lambda i: (0,) * len(shp))
    in_specs = [col(GLA_W, cq), col(GLA_W, ckk), pl.BlockSpec((rb, 128), lambda i: (i, 0)),
                full((128, GLA_W)), full((1, GLA_W)), full((1, GLA_DV))]
    in_specs += [col(GLA_DV, cv + h * GLA_DV) for h in range(H)] + [col(GLA_DV, cr + h * GLA_DV) for h in range(H)]
    return pl.pallas_call(
        body, name=name,
        out_shape=(jax.ShapeDtypeStruct((S, H * GLA_DV), BF16), jax.ShapeDtypeStruct((S // CHUNK, H, GLA_DV, GLA_W), F32)),
        grid=(nb,), in_specs=in_specs,
        out_specs=(pl.BlockSpec((rb, H * GLA_DV), lambda i: (i, 0)),
                   pl.BlockSpec((GLA_BLOCK_CHUNKS, H, GLA_DV, GLA_W), lambda i: (i, 0, 0, 0))),
        scratch_shapes=[pltpu.VMEM((H, GLA_DV, GLA_W), F32)],
        compiler_params=_params(("arbitrary",)),
    )(z, z, zsm, wg, bg, go, *([z] * (2 * H)))


def _gla_bwd(z, zsm, wg, bg, go, states, do, cols, *, name):
    S = z.shape[0]
    rb = GLA_BLOCK_CHUNKS * CHUNK
    nb = S // rb
    cq, ckk, cv, cr = cols
    H = GLA_HEADS

    def body(q_ref, k_ref, zsm_ref, wg_ref, bg_ref, go_ref, st_ref, do_ref, *rest):
        v_refs, r_refs = rest[:H], rest[H:2 * H]
        dq_ref, dk_ref, dv_ref, dr_ref, dzsm_ref, dwg_ref, dbg_ref, dgo_ref, dstate = rest[2 * H:]

        @pl.when(pl.program_id(0) == 0)
        def _():
            dstate[...] = jnp.zeros(dstate.shape, F32)
            dwg_ref[...] = jnp.zeros(dwg_ref.shape, F32)
            dbg_ref[...] = jnp.zeros(dbg_ref.shape, F32)
            dgo_ref[...] = jnp.zeros(dgo_ref.shape, F32)

        wg_, bg_, go_ = wg_ref[...], bg_ref[...], go_ref[...]
        for c in reversed(range(GLA_BLOCK_CHUNKS)):
            rows = pl.ds(c * CHUNK, CHUNK)
            prim = (q_ref[rows, :].astype(F32), k_ref[rows, :].astype(F32), zsm_ref[rows, :], wg_, bg_, go_,
                    [v_refs[h][rows, :].astype(F32) for h in range(H)], [r_refs[h][rows, :].astype(F32) for h in range(H)],
                    [st_ref[c, h] for h in range(H)])
            _, vjp = jax.vjp(_gla_chunk, *prim)
            douts = [do_ref[rows, h * GLA_DV:(h + 1) * GLA_DV].astype(F32) for h in range(H)]
            dq, dk, dzs, dwg, dbg, dgo, dvs, drs, dsts = vjp((douts, [dstate[h] for h in range(H)]))
            dq_ref[rows, :] = dq.astype(BF16)
            dk_ref[rows, :] = dk.astype(BF16)
            dzsm_ref[rows, :] = dzs
            dwg_ref[...] += dwg
            dbg_ref[...] += dbg
            dgo_ref[...] += dgo
            for h in range(H):
                dv_ref[rows, h * GLA_DV:(h + 1) * GLA_DV] = dvs[h].astype(BF16)
                dr_ref[rows, h * GLA_DV:(h + 1) * GLA_DV] = drs[h].astype(BF16)
                dstate[h] = dsts[h]

    rev = lambda i: nb - 1 - i

    def col(width, off):
        return pl.BlockSpec((rb, width), lambda i, o=off // width: (rev(i), o))

    full = lambda shp: pl.BlockSpec(shp, lambda i: (0,) * len(shp))
    rowb = lambda w: pl.BlockSpec((rb, w), lambda i: (rev(i), 0))
    in_specs = [col(GLA_W, cq), col(GLA_W, ckk), rowb(128), full((128, GLA_W)), full((1, GLA_W)), full((1, GLA_DV)),
                pl.BlockSpec((GLA_BLOCK_CHUNKS, H, GLA_DV, GLA_W), lambda i: (rev(i), 0, 0, 0)), rowb(H * GLA_DV)]
    in_specs += [col(GLA_DV, cv + h * GLA_DV) for h in range(H)] + [col(GLA_DV, cr + h * GLA_DV) for h in range(H)]
    return pl.pallas_call(
        body, name=name,
        out_shape=(jax.ShapeDtypeStruct((S, GLA_W), BF16), jax.ShapeDtypeStruct((S, GLA_W), BF16),
                   jax.ShapeDtypeStruct((S, H * GLA_DV), BF16), jax.ShapeDtypeStruct((S, H * GLA_DV), BF16),
                   jax.ShapeDtypeStruct((S, 128), F32), jax.ShapeDtypeStruct((128, GLA_W), F32),
                   jax.ShapeDtypeStruct((1, GLA_W), F32), jax.ShapeDtypeStruct((1, GLA_DV), F32)),
        grid=(nb,), in_specs=in_specs,
        out_specs=(rowb(GLA_W), rowb(GLA_W), rowb(H * GLA_DV), rowb(H * GLA_DV), rowb(128),
                   full((128, GLA_W)), full((1, GLA_W)), full((1, GLA_DV))),
        scratch_shapes=[pltpu.VMEM((H, GLA_DV, GLA_W), F32)],
        compiler_params=_params(("arbitrary",)),
    )(z, z, zsm, wg, bg, go, states, do, *([z] * (2 * H)))


def _row_spec(entry, tr):
    if isinstance(entry, tuple):
        arr, width, off = entry
        return arr, pl.BlockSpec((tr, width), lambda i, o=off // width: (i, o))
    return entry, pl.BlockSpec((tr, entry.shape[1]), lambda i: (i, 0))


def _stage_fwd(fn, rows, consts, outs, *, name, tr=None):
    first = rows[0][0] if isinstance(rows[0], tuple) else rows[0]
    S = first.shape[0]
    tr = tr or _pick(S, (512, 256, 128))
    arrs, specs = zip(*[_row_spec(e, tr) for e in rows])
    nr, nc = len(rows), len(consts)

    def body(*refs):
        vals = [r[...].astype(F32) for r in refs[:nr + nc]]
        res = fn(*vals)
        for o_ref, val in zip(refs[nr + nc:], res):
            o_ref[...] = val.astype(o_ref.dtype)

    cspecs = [pl.BlockSpec(c.shape, lambda i, n=c.ndim: (0,) * n) for c in consts]
    return pl.pallas_call(
        body, name=name,
        out_shape=tuple(jax.ShapeDtypeStruct((S, w), dt) for w, dt in outs), grid=(S // tr,),
        in_specs=list(specs) + cspecs,
        out_specs=tuple(pl.BlockSpec((tr, w), lambda i: (i, 0)) for w, _ in outs),
        compiler_params=_params(("parallel",)),
    )(*arrs, *consts)


def _stage_bwd(fn, rows, consts, cts, n_diff, drow_dtypes, *, name, tr=None):
    first = rows[0][0] if isinstance(rows[0], tuple) else rows[0]
    S = first.shape[0]
    tr = tr or _pick(S, (512, 256, 128))
    arrs, specs = zip(*[_row_spec(e, tr) for e in rows])
    widths = [e[1] if isinstance(e, tuple) else e.shape[1] for e in rows]
    nr, nc, nt = len(rows), len(consts), len(cts)

    def body(*refs):
        vals = [r[...].astype(F32) for r in refs[:nr + nc]]
        ct = [r[...].astype(F32) for r in refs[nr + nc:nr + nc + nt]]
        drow_refs = refs[nr + nc + nt:nr + nc + nt + n_diff]
        dconst_refs = refs[nr + nc + nt + n_diff:]
        rest_rows = vals[n_diff:nr]

        def f(diff_rows, cs):
            return tuple(fn(*diff_rows, *rest_rows, *cs))

        _, vjp = jax.vjp(f, vals[:n_diff], vals[nr:])
        drows, dcs = vjp(tuple(ct))
        for r, val in zip(drow_refs, drows):
            r[...] = val.astype(r.dtype)
        first_step = pl.program_id(0) == 0
        for r, val in zip(dconst_refs, dcs):
            @pl.when(first_step)
            def _(r=r, val=val):
                r[...] = val

            @pl.when(jnp.logical_not(first_step))
            def _(r=r, val=val):
                r[...] += val

    cspecs = [pl.BlockSpec(c.shape, lambda i, n=c.ndim: (0,) * n) for c in consts]
    ctspecs = [pl.BlockSpec((tr, c.shape[1]), lambda i: (i, 0)) for c in cts]
    out_shape = [jax.ShapeDtypeStruct((S, widths[j]), drow_dtypes[j]) for j in range(n_diff)]
    out_shape += [jax.ShapeDtypeStruct(c.shape, F32) for c in consts]
    out_specs = [pl.BlockSpec((tr, widths[j]), lambda i: (i, 0)) for j in range(n_diff)] + cspecs
    res = pl.pallas_call(
        body, name=name, out_shape=tuple(out_shape), grid=(S // tr,),
        in_specs=list(specs) + cspecs + ctspecs, out_specs=tuple(out_specs),
        compiler_params=_params(("arbitrary",)),
    )(*arrs, *consts, *cts)
    return list(res[:n_diff]), list(res[n_diff:])


def _mla_prep_fn(cq, ckv, kr, kr_sw, cos, sin, gq, gkv, wq_n, wq_r, wq_sw, wk, wv):
    hq = _rms(cq, gq)
    hkv = _rms(ckv, gkv)
    return (bdot(hq, wq_n), bdot(hq, wq_r) * cos + bdot(hq, wq_sw) * sin,
            bdot(hkv, wk), bdot(hkv, wv), kr * cos + kr_sw * sin)


def _merge_fn(g0, g1, g2, of, og, om, b0, b1, b2, wf, wg, wm):
    return (_sigmoid(g0 + b0) * bdot(of, wf) + _sigmoid(g1 + b1) * bdot(og, wg) + _sigmoid(g2 + b2) * bdot(om, wm),)


_IN_SIZES = (256, 256, 256, 4, 256, 256, 512, 16, 512, 256, 128, 32, 3072)
_IN_OFF = np.concatenate([[0], np.cumsum(_IN_SIZES)])
(_O_FQ, _O_FK, _O_FV, _O_FF, _O_GQ, _O_GK, _O_GV, _O_GLOW, _O_GR, _O_MQ, _O_MKV, _O_MKR, _O_ZG) = [int(o) for o in _IN_OFF[:-1]]
N_IN = int(_IN_OFF[-1])
_BIG_GROUPS = ((_O_ZG, 3072), (_O_GV, 512), (_O_GR, 512), (_O_FQ, 256), (_O_FK, 256), (_O_FV, 256),
               (_O_GQ, 256), (_O_GK, 256), (_O_MQ, 256), (_O_MKV, 128))
Z_GATE, Z_GV, Z_GR, Z_FQ, Z_FK, Z_FV, Z_GQ, Z_GK, Z_MQ, Z_MKV = [int(o) for o in
                                                                    np.concatenate([[0], np.cumsum([w for _, w in _BIG_GROUPS])])[:-1]]
N_BIG = sum(w for _, w in _BIG_GROUPS)
SM_FF, SM_GLOW, SM_KR, SM_KR_SW, N_SM = 0, 8, 32, 64, 128
N_PAD = N_BIG + N_SM
_HALF = MLA_ROPE // 2
_QK_HD = MLA_NOPE + MLA_ROPE


def _in_perm():
    idx = np.concatenate([np.arange(o, o + w) for o, w in _BIG_GROUPS] + [np.zeros(N_SM, np.int64)])
    sign = np.concatenate([np.ones(N_BIG), np.zeros(N_SM)])
    for src, dst, w in ((_O_FF, SM_FF, 4), (_O_GLOW, SM_GLOW, 16), (_O_MKR, SM_KR, 32)):
        idx[N_BIG + dst:N_BIG + dst + w] = np.arange(src, src + w)
        sign[N_BIG + dst:N_BIG + dst + w] = 1.0
    inv = np.zeros(N_IN, np.int64)
    inv[idx[sign > 0]] = np.nonzero(sign > 0)[0]
    sw = N_BIG + SM_KR_SW
    idx[sw:sw + _HALF] = np.arange(_O_MKR + _HALF, _O_MKR + MLA_ROPE)
    sign[sw:sw + _HALF] = -1.0
    idx[sw + _HALF:sw + MLA_ROPE] = np.arange(_O_MKR, _O_MKR + _HALF)
    sign[sw + _HALF:sw + MLA_ROPE] = 1.0
    inv2, sign2 = np.zeros(N_IN, np.int64), np.zeros(N_IN)
    inv2[idx[sw:sw + MLA_ROPE]] = np.arange(sw, sw + MLA_ROPE)
    sign2[idx[sw:sw + MLA_ROPE]] = sign[sw:sw + MLA_ROPE]
    return idx, sign.astype(np.float32), inv, inv2, sign2.astype(np.float32)


_IN_IDX, _IN_SIGN, _IN_INV, _IN_INV2, _IN_SIGN2 = _in_perm()


def _uq_perm():
    base = [h * _QK_HD for h in range(MLA_HEADS)]
    nope = np.concatenate([np.arange(b, b + MLA_NOPE) for b in base])
    rot = np.concatenate([np.arange(b + MLA_NOPE, b + _QK_HD) for b in base])
    sw = np.concatenate([np.concatenate([np.arange(b + MLA_NOPE + _HALF, b + _QK_HD), np.arange(b + MLA_NOPE, b + MLA_NOPE + _HALF)])
                         for b in base])
    sw_sign = np.tile(np.concatenate([-np.ones(_HALF), np.ones(_HALF)]), MLA_HEADS).astype(np.float32)
    return nope, rot, sw, sw_sign


_UQ_NOPE, _UQ_ROT, _UQ_SW, _UQ_SW_SIGN = _uq_perm()
_UKV_PERM = np.concatenate(
    [np.concatenate([np.arange(h * 128, h * 128 + MLA_NOPE) for h in range(MLA_HEADS)]),
     np.concatenate([np.arange(h * 128 + MLA_NOPE, (h + 1) * 128) for h in range(MLA_HEADS)])])
_UKV_INV = np.argsort(_UKV_PERM)


def _rope_tables(S):
    inv = ROPE_BASE ** (-jnp.arange(_HALF, dtype=F32) / _HALF)
    ang = jnp.arange(S, dtype=F32)[:, None] * inv[None, :]
    return jnp.tile(jnp.cos(ang), (1, 2 * MLA_HEADS)), jnp.tile(jnp.sin(ang), (1, 2 * MLA_HEADS))


class _LayerParams:
    def __init__(self, rep, l):
        self.w, self.rep, self.l, self.made = {}, rep, l, {}

    def __getitem__(self, k):
        if k not in self.made:
            self.made[k] = self._make(k)
        return self.made[k]

    def _make(self, k):
        w, rep, l = self.w, self.rep, self.l
        if k == 'wg':
            return jnp.zeros((N_SM, GLA_W), BF16).at[SM_GLOW:SM_GLOW + GLA_RANK].set(w['w_gla_gate'])
        if k in ('wq_n', 'wq_r'):
            return w['w_mla_uq'][:, _UQ_NOPE if k == 'wq_n' else _UQ_ROT]
        if k == 'wq_sw':
            return w['w_mla_uq'][:, _UQ_SW] * _UQ_SW_SIGN.astype(BF16)
        if k in ('wk', 'wv'):
            return w['w_mla_ukv'][:, _UKV_PERM[:256] if k == 'wk' else _UKV_PERM[256:]]
        if k == 'b_f':
            return jnp.zeros((8, 1), F32).at[:FOX_HEADS, 0].set(rep['b_fox_forget'][l])
        if k == 'b_gate':
            return [rep['b_branch_gate'][l][i * 1024:(i + 1) * 1024].reshape(1, 1024) for i in range(3)]
        vec = {'bg': 'b_gla_gate', 'go': 'g_gla_out', 'gq': 'g_mla_q', 'gkv': 'g_mla_kv'}
        if k in vec:
            return rep[vec[k]][l].reshape(1, -1)
        return rep[k][l] if k in rep else w[k]


_GLA_COLS = (Z_GQ, Z_GK, Z_GV, Z_GR)
_MLA_OUTS = [(256, BF16), (128, BF16), (256, BF16), (256, BF16), (128, BF16)]


def _mla_rows(z, zsm, rope):
    per_head = lambda off: jnp.tile(zsm[:, off:off + MLA_ROPE], (1, MLA_HEADS))
    return [(z, 256, Z_MQ), (z, 128, Z_MKV), per_head(SM_KR), per_head(SM_KR_SW), *rope]


def _mla_consts(p):
    return [p['gq'], p['gkv'], p['wq_n'], p['wq_r'], p['wq_sw'], p['wk'], p['wv']]


def _fox_qkv(z):
    return [((z, Z_FQ, 256), (z, Z_FK, 256), FOX_HD, False)], (z, Z_FV, 256)


def _mla_qkv(qn, qr, kn, vv, kr):
    return [((qn, 0, 256), (kn, 0, 256), MLA_NOPE, False), ((qr, 0, 128), (kr, 0, 128), MLA_ROPE, True)], (vv, 0, 256)


def _xa_qkv(qx, kvx):
    return [((qx, 0, 512), (kvx, 0, 512), XA_HD, False)], (kvx, 512, 512)


def _merge_rows(z, o_fox, o_gla, o_mla):
    return [(z, 1024, Z_GATE), (z, 1024, Z_GATE + 1024), (z, 1024, Z_GATE + 2048), o_fox, o_gla, o_mla]


def _merge_consts(p):
    return p['b_gate'] + [p['w_up_fox'], p['w_up_gla'], p['w_up_mla']]


def _carried(hooks, key, call):
    rider, sink = hooks.pop(key, (None, None))
    res = call(rider=rider)
    if rider is None:
        return res
    sink(res[-1])
    return res[:-1]


def _layer_fwd(x0, mem, p, rope, l, hooks):
    S = x0.shape[0]
    sv = {'x0': x0}
    h1 = _rms_fwd(x0, p['g_mix'], name=f"rms_mix_{l}")
    z = _mm(h1, p['w_in'], mode='nn', out_dtype=BF16, b_cols=(0, N_BIG), name=f"in_big_{l}")
    zsm = _mm(h1, p['w_in'], mode='nn', out_dtype=F32, b_cols=(N_BIG, N_SM), name=f"in_small_{l}")
    sv.update(h1=h1, z=z, zsm=zsm)
    ff_t = jnp.zeros((8, S), F32).at[:FOX_HEADS].set(zsm[:, SM_FF:SM_FF + FOX_HEADS].T)
    cum_t = _fox_cum_fwd(ff_t, p['b_f'], name=f"fox_cum_{l}")
    cum = cum_t.T
    o_fox, lse_f = _carried(hooks, (l, 'fox_fwd'), lambda rider: _attn_fwd(
        *_fox_qkv(z), FOX_HEADS, cum, cum_t, scale=FOX_HD ** -0.5, mask='causal', name=f"fox_fwd_{l}", rider=rider))
    sv.update(ff_t=ff_t, cum=cum, cum_t=cum_t, lse_f=lse_f, o_fox=o_fox)
    o_gla, states = _gla_fwd(z, zsm, p['wg'], p['bg'], p['go'], _GLA_COLS, name=f"gla_fwd_{l}")
    sv.update(o_gla=o_gla, states=states)
    mla = _stage_fwd(_mla_prep_fn, _mla_rows(z, zsm, rope), _mla_consts(p), _MLA_OUTS, name=f"mla_prep_{l}")
    o_mla, lse_m = _carried(hooks, (l, 'mla_fwd'), lambda rider: _attn_fwd(
        *_mla_qkv(*mla), MLA_HEADS, None, None, scale=_QK_HD ** -0.5, mask='chunk', name=f"mla_fwd_{l}", rider=rider))
    sv.update(mla=mla, lse_m=lse_m, o_mla=o_mla)
    (y,) = _stage_fwd(_merge_fn, _merge_rows(z, o_fox, o_gla, o_mla), _merge_consts(p), [(1024, BF16)], name=f"merge_{l}")
    x1 = _mm(y, p['w_out'], mode='nn', out_dtype=F32, residual=x0, name=f"out_proj_{l}")
    sv.update(y=y, x1=x1)
    h2 = _rms_fwd(x1, p['g_xa'], name=f"rms_xa_{l}")
    hm = _rms_fwd(mem, p['g_mem'], name=f"rms_mem_{l}")
    qx = _mm(h2, p['w_xq'], mode='nn', out_dtype=BF16, name=f"xq_{l}")
    kvx = _mm(hm, p['w_xkv'], mode='nn', out_dtype=BF16, name=f"xkv_{l}")
    ox, lse_x = _attn_fwd(*_xa_qkv(qx, kvx), XA_HEADS, None, None, scale=XA_HD ** -0.5, mask=None, name=f"xa_fwd_{l}")
    x2 = _mm(ox, p['w_xo'], mode='nn', out_dtype=F32, residual=x1, name=f"xo_{l}")
    sv.update(h2=h2, hm=hm, qx=qx, kvx=kvx, lse_x=lse_x, ox=ox, x2=x2)
    h3 = _rms_fwd(x2, p['g_mlp'], name=f"rms_mlp_{l}")
    a = _mm(h3, p['w_mlp1'], mode='nn', out_dtype=BF16, name=f"mlp1_{l}")
    x3 = _mm(a, p['w_mlp2'], mode='nn', out_dtype=F32, act='relu2', residual=x2, name=f"mlp2_{l}")
    sv.update(h3=h3, a=a)
    return x3, sv


def _layer_bwd(dx3, dx3b, mem, p, rope, sv, l, hooks):
    S = dx3.shape[0]
    g = {}
    da = _mm(dx3b, p['w_mlp2'], mode='nt', out_dtype=BF16, drelu_of=sv['a'], name=f"d_mlp2_in_{l}")
    g['w_mlp2'] = _mm(sv['a'], dx3b, mode='tn', out_dtype=BF16, act='relu2', name=f"d_w_mlp2_{l}")
    dh3 = _mm(da, p['w_mlp1'], mode='nt', out_dtype=F32, name=f"d_mlp1_in_{l}")
    g['w_mlp1'] = _mm(sv['h3'], da, mode='tn', out_dtype=BF16, name=f"d_w_mlp1_{l}")
    dx2, dx2b, g['g_mlp'] = _rms_bwd(sv['x2'], p['g_mlp'], dh3, dx3, name=f"d_rms_mlp_{l}")
    dox = _mm(dx2b, p['w_xo'], mode='nt', out_dtype=BF16, name=f"d_xo_in_{l}")
    g['w_xo'] = _mm(sv['ox'], dx2b, mode='tn', out_dtype=BF16, name=f"d_w_xo_{l}")
    (dqx,), (dkx,), dvx = _attn_bwd(*_xa_qkv(sv['qx'], sv['kvx']), XA_HEADS, sv['ox'], dox, sv['lse_x'], None, None,
                                    scale=XA_HD ** -0.5, mask=None, name=f"xa_bwd_{l}")
    dqx = dqx.astype(BF16)
    dkvx = jnp.concatenate([dkx, dvx], axis=1).astype(BF16)
    dh2 = _mm(dqx, p['w_xq'], mode='nt', out_dtype=F32, name=f"d_xq_in_{l}")
    g['w_xq'] = _mm(sv['h2'], dqx, mode='tn', out_dtype=BF16, name=f"d_w_xq_{l}")
    dhm = _mm(dkvx, p['w_xkv'], mode='nt', out_dtype=F32, name=f"d_xkv_in_{l}")
    g['w_xkv'] = _mm(sv['hm'], dkvx, mode='tn', out_dtype=BF16, name=f"d_w_xkv_{l}")
    _, _, g['g_mem'] = _rms_bwd(mem, p['g_mem'], dhm, None, name=f"d_rms_mem_{l}")
    dx1, dx1b, g['g_xa'] = _rms_bwd(sv['x1'], p['g_xa'], dh2, dx2, name=f"d_rms_xa_{l}")
    dy = _mm(dx1b, p['w_out'], mode='nt', out_dtype=F32, name=f"d_out_in_{l}")
    g['w_out'] = _mm(sv['y'], dx1b, mode='tn', out_dtype=BF16, name=f"d_w_out_{l}")
    z, zsm = sv['z'], sv['zsm']
    (dg0, dg1, dg2, do_fox, do_gla, do_mla), (db0, db1, db2, g['w_up_fox'], g['w_up_gla'], g['w_up_mla']) = _stage_bwd(
        _merge_fn, _merge_rows(z, sv['o_fox'], sv['o_gla'], sv['o_mla']), _merge_consts(p), [dy], 6, [BF16] * 6,
        name=f"merge_bwd_{l}")
    g['b_branch_gate'] = jnp.concatenate([db0, db1, db2], axis=1).reshape(-1)
    (dfq,), (dfk,), dfv, dck, dcq = _carried(hooks, (l, 'fox_bwd'), lambda rider: _attn_bwd(
        *_fox_qkv(z), FOX_HEADS, sv['o_fox'], do_fox, sv['lse_f'], sv['cum'], sv['cum_t'],
        scale=FOX_HD ** -0.5, mask='causal', name=f"fox_bwd_{l}", rider=rider))
    dff_t, db_f = _fox_cum_bwd(sv['ff_t'], p['b_f'], dck + dcq.T, name=f"fox_cum_bwd_{l}")
    g['b_fox_forget'] = db_f[:FOX_HEADS, 0]
    dgq, dgk, dgv, dgr, dzsm, dwg, dbg, dgo = _gla_bwd(z, zsm, p['wg'], p['bg'], p['go'], sv['states'], do_gla, _GLA_COLS,
                                                       name=f"gla_bwd_{l}")
    g['w_gla_gate'] = dwg[SM_GLOW:SM_GLOW + GLA_RANK]
    g['b_gla_gate'] = dbg.reshape(-1)
    g['g_gla_out'] = dgo.reshape(-1)
    (dmqn, dmqr), (dmkn, dmkr), dmv = _attn_bwd(*_mla_qkv(*sv['mla']), MLA_HEADS, sv['o_mla'], do_mla, sv['lse_m'], None, None,
                                                scale=_QK_HD ** -0.5, mask='chunk', name=f"mla_bwd_{l}")
    (dcq, dckv, dkr, dkr_sw), (dgq_n, dgkv_n, dwq_n, dwq_r, dwq_sw, dwk, dwv) = _stage_bwd(
        _mla_prep_fn, _mla_rows(z, zsm, rope), _mla_consts(p), [dmqn, dmqr, dmkn, dmv, dmkr], 4, [BF16, BF16, F32, F32],
        name=f"mla_prep_bwd_{l}")
    dkr, dkr_sw = (jnp.sum(d.reshape(S, MLA_HEADS, MLA_ROPE), axis=1) for d in (dkr, dkr_sw))
    g['g_mla_q'] = dgq_n.reshape(-1)
    g['g_mla_kv'] = dgkv_n.reshape(-1)
    g['w_mla_uq'] = (jnp.zeros((MLA_Q_RANK, MLA_HEADS * _QK_HD), F32).at[:, _UQ_NOPE].set(dwq_n).at[:, _UQ_ROT].set(dwq_r)
                     .at[:, _UQ_SW].add(dwq_sw * _UQ_SW_SIGN))
    g['w_mla_ukv'] = jnp.concatenate([dwk, dwv], axis=1)[:, _UKV_INV]
    dz = jnp.concatenate([dg0, dg1, dg2, dgv, dgr, dfq.astype(BF16), dfk.astype(BF16), dfv.astype(BF16), dgq, dgk, dcq, dckv,
                          (dzsm + jnp.concatenate([dff_t[:FOX_HEADS].T, jnp.zeros((S, SM_KR - FOX_HEADS), F32), dkr, dkr_sw,
                                                   jnp.zeros((S, N_SM - SM_KR_SW - MLA_ROPE), F32)], axis=1)).astype(BF16)],
                         axis=1)
    dh1 = _mm(dz, p['w_in'], mode='nt', out_dtype=F32, tk=N_PAD // 2, name=f"d_in_{l}")
    g['w_in'] = _mm(sv['h1'], dz, mode='tn', out_dtype=BF16, tm=512, tn=N_PAD // 2, tk=512, name=f"d_w_in_{l}")
    dx0, dx0b, g['g_mix'] = _rms_bwd(sv['x0'], p['g_mix'], dh1, dx1, name=f"d_rms_mix_{l}")
    for n in ('g_mlp', 'g_mem', 'g_xa', 'g_mix'):
        g[n] = g[n].reshape(-1)
    return dx0, dx0b, g


def _local_step(x, mem, target, ps, g_final, hooks, layer_done):
    rope = _rope_tables(x.shape[0])
    saved = []
    for l, p in enumerate(ps):
        x, sv = _layer_fwd(x, mem, p, rope, l, hooks)
        saved.append(sv)
    loss, dx, dxb, dgf = _loss_head(x, g_final, target, name="loss_head")
    for l in reversed(range(len(ps))):
        dx, dxb, grads = _layer_bwd(dx, dxb, mem, ps[l], rope, saved[l], l, hooks)
        layer_done(l, grads)
    assert not hooks, f"exchanges without a carrier: {list(hooks)}"
    return loss, dx, dgf.reshape(-1)


_MESH_AXES = ("x", "y", "c")
_HBM = pl.BlockSpec(memory_space=pl.ANY)


N_CHIP = 4


def _place():
    x, y, c = (lax.axis_index(n) for n in _MESH_AXES)
    return (x, y, c), (x, y, 1 - c), [(1 - x, y), (x, 1 - y), (1 - x, 1 - y)]


def _remote(src, dst, sems, k, to):
    return pltpu.make_async_remote_copy(src_ref=src, dst_ref=dst, send_sem=sems[0].at[k], recv_sem=sems[1].at[k],
                                        device_id=to, device_id_type=pl.DeviceIdType.MESH)


def _all_gather(x, *, name):
    def body(x_ref, o_ref, send_sems, recv_sems, local_sem):
        me, sib, chips = _place()
        c = me[2]
        sems = (send_sems, recv_sems)
        slot = lambda px, py, pc: o_ref.at[4 * px + 2 * py + pc]
        mine = pltpu.make_async_copy(x_ref, slot(*me), local_sem)
        mine.start()
        first = [_remote(x_ref, slot(*me), sems, 0, sib)]
        first += [_remote(x_ref, slot(*me), sems, 1 + j, (*chip, c)) for j, chip in enumerate(chips)]
        for cp in first:
            cp.start()
        passed = [_remote(slot(*chip, c), slot(*chip, c), sems, 4 + j, sib) for j, chip in enumerate(chips)]
        for j, chip in enumerate(chips):
            _remote(x_ref, slot(*chip, c), sems, 1 + j, me).wait_recv()
            passed[j].start()
        _remote(x_ref, slot(*sib), sems, 0, me).wait_recv()
        for j, chip in enumerate(chips):
            _remote(x_ref, slot(*chip, 1 - c), sems, 4 + j, me).wait_recv()
        for cp in first + passed:
            cp.wait_send()
        mine.wait()

    return pl.pallas_call(
        body, name=name, out_shape=jax.ShapeDtypeStruct((N_DEV,) + x.shape, x.dtype),
        in_specs=[_HBM], out_specs=_HBM,
        scratch_shapes=[pltpu.SemaphoreType.DMA((N_DEV - 1,)), pltpu.SemaphoreType.DMA((N_DEV - 1,)), pltpu.SemaphoreType.DMA],
        compiler_params=pltpu.CompilerParams(has_side_effects=True),
    )(x)


class _Rider:
    def __init__(self, inputs, out_shapes, scratch, start, finish, post):
        self.inputs, self.out_shapes, self.scratch = list(inputs), list(out_shapes), list(scratch)
        self.start, self.finish, self.post = start, finish, post


def _run_rider(rider, *, name):
    def body(*refs):
        rider.start(refs)
        rider.finish(refs)

    outs = pl.pallas_call(
        body, name=name, out_shape=tuple(rider.out_shapes), in_specs=[_HBM] * len(rider.inputs),
        out_specs=(_HBM,) * len(rider.out_shapes), scratch_shapes=rider.scratch,
        compiler_params=pltpu.CompilerParams(has_side_effects=True),
    )(*rider.inputs)
    return rider.post(outs)


def _carry(rider, n_in, n_out, first, last):
    if rider is None:
        return [], [], [], [], [], lambda refs: refs
    ni, no = len(rider.inputs), len(rider.out_shapes)

    def split(refs):
        own_in, r_in = refs[:n_in], refs[n_in:n_in + ni]
        own_out, r_out = refs[n_in + ni:n_in + ni + n_out], refs[n_in + ni + n_out:n_in + ni + n_out + no]
        rest = refs[n_in + ni + n_out + no:]
        own_scr, r_scr = rest[:len(rest) - len(rider.scratch)], rest[len(rest) - len(rider.scratch):]
        rrefs = tuple(r_in) + tuple(r_out) + tuple(r_scr)
        pl.when(first())(lambda: rider.start(rrefs))
        pl.when(last())(lambda: rider.finish(rrefs))
        return tuple(own_in) + tuple(own_out) + tuple(own_scr)

    return list(rider.inputs), [_HBM] * ni, list(rider.out_shapes), [_HBM] * no, list(rider.scratch), split


def _gather_rider(shards, axes):
    n = len(shards)
    srcs, out_shapes, kinds = [], [], []
    for s, ax in zip(shards, axes):
        L, a, b = s.shape
        if ax == 1:
            srcs.append(s.reshape(L, 1, a, b)), out_shapes.append((L, N_DEV, a, b)), kinds.append('row')
        elif b % 128 == 0:
            srcs.append(s), out_shapes.append((L, a, N_DEV * b)), kinds.append('col')
        else:
            srcs.append(s.reshape(1, L, a, b)), out_shapes.append((N_DEV, L, a, b)), kinds.append('slot')

    def parts(refs):
        x_refs, o_refs = refs[:n], refs[n:2 * n]
        send_sems, recv_sems, local_sem = refs[2 * n:]
        me, sib, chips = _place()
        sems = (send_sems, recv_sems)

        def win(t, px, py, pc):
            idx = 4 * px + 2 * py + pc
            if kinds[t] == 'row':
                return o_refs[t].at[:, pl.ds(idx, 1)]
            if kinds[t] == 'col':
                width = shards[t].shape[2]
                return o_refs[t].at[:, :, pl.ds(pl.multiple_of(idx * width, 128), width)]
            return o_refs[t].at[pl.ds(idx, 1)]

        def group(k, block, to, own):
            return [_remote(x_refs[t] if own else win(t, *block), win(t, *block), sems, k * n + t, to) for t in range(n)]

        mine = [pltpu.make_async_copy(x_refs[t], win(t, *me), local_sem.at[t]) for t in range(n)]
        first = group(0, me, sib, True)
        for j, chip in enumerate(chips):
            first += group(1 + j, me, (*chip, me[2]), True)
        return me, sib, chips, group, mine, first

    def start(refs):
        *_, mine, first = parts(refs)
        for cp in mine + first:
            cp.start()

    def finish(refs):
        me, sib, chips, group, mine, first = parts(refs)
        c = me[2]
        passed = []
        for j, chip in enumerate(chips):
            for cp in group(1 + j, (*chip, c), me, False):
                cp.wait_recv()
            fwd = group(4 + j, (*chip, c), sib, False)
            for cp in fwd:
                cp.start()
            passed += fwd
        for cp in group(0, sib, me, False):
            cp.wait_recv()
        for j, chip in enumerate(chips):
            for cp in group(4 + j, (*chip, 1 - c), me, False):
                cp.wait_recv()
        for cp in first + passed:
            cp.wait_send()
        for cp in mine:
            cp.wait()

    def post(outs):
        whole = []
        for o, s, kind in zip(outs, shards, kinds):
            L, a, b = s.shape
            whole.append(o.reshape(L, N_DEV * a, b) if kind == 'row' else o if kind == 'col' else _to_whole(o, 2))
        return whole

    return _Rider(srcs, [jax.ShapeDtypeStruct(shp, s.dtype) for shp, s in zip(out_shapes, shards)],
                  [pltpu.SemaphoreType.DMA(((N_DEV - 1) * n,)), pltpu.SemaphoreType.DMA(((N_DEV - 1) * n,)),
                   pltpu.SemaphoreType.DMA((n,))], start, finish, post)


def _sibling_swap(x, *, name):
    def body(x_ref, o_ref, send_sems, recv_sems):
        me, sib, _ = _place()
        c = me[2]
        sems = (send_sems, recv_sems)
        sends = [_remote(x_ref.at[j, 1 - c], o_ref.at[j], sems, j, sib) for j in range(N_CHIP)]
        for cp in sends:
            cp.start()
        for cp in sends:
            cp.wait_send()
            cp.wait_recv()

    return pl.pallas_call(
        body, name=name, out_shape=jax.ShapeDtypeStruct((N_CHIP,) + x.shape[2:], x.dtype),
        in_specs=[_HBM], out_specs=_HBM,
        scratch_shapes=[pltpu.SemaphoreType.DMA((N_CHIP,)), pltpu.SemaphoreType.DMA((N_CHIP,))],
        compiler_params=pltpu.CompilerParams(has_side_effects=True),
    )(x)


def _pair_sum(x, got, c, *, name):
    _, _, R, _ = x.shape
    tr = _pick(R, (1024, 512, 256, 128, 64, 32, 16, 8))

    def body(c_ref, x_ref, g_ref, o_ref):
        o_ref[...] = (x_ref[...].astype(F32) + g_ref[...].astype(F32)).astype(o_ref.dtype)

    return pl.pallas_call(
        body, name=name, out_shape=jax.ShapeDtypeStruct((N_CHIP, R, 128), x.dtype),
        grid_spec=pltpu.PrefetchScalarGridSpec(
            num_scalar_prefetch=1, grid=(N_CHIP, R // tr),
            in_specs=[pl.BlockSpec((None, None, tr, 128), lambda j, i, c_ref: (j, c_ref[0], i, 0)),
                      pl.BlockSpec((None, tr, 128), lambda j, i, c_ref: (j, i, 0))],
            out_specs=pl.BlockSpec((None, tr, 128), lambda j, i, c_ref: (j, i, 0))),
        compiler_params=_params(("parallel", "parallel")),
    )(c, x, got)


def _chip_all_to_all_rider(x):
    def parts(refs):
        x_ref, o_ref, send_sems, recv_sems, local_sem = refs
        me, _, chips = _place()
        sems = (send_sems, recv_sems)
        mine = 2 * me[0] + me[1]
        local = pltpu.make_async_copy(x_ref.at[mine], o_ref.at[mine], local_sem)
        sends = [_remote(x_ref.at[2 * px + py], o_ref.at[mine], sems, j, (px, py, me[2])) for j, (px, py) in enumerate(chips)]
        arrival = lambda j: _remote(x_ref.at[mine], o_ref.at[2 * chips[j][0] + chips[j][1]], sems, j, me)
        return local, sends, arrival

    def start(refs):
        local, sends, _ = parts(refs)
        for cp in [local] + sends:
            cp.start()

    def finish(refs):
        local, sends, arrival = parts(refs)
        for j, cp in enumerate(sends):
            cp.wait_send()
            arrival(j).wait_recv()
        local.wait()

    return _Rider([x], [jax.ShapeDtypeStruct(x.shape, x.dtype)],
                  [pltpu.SemaphoreType.DMA((N_CHIP - 1,)), pltpu.SemaphoreType.DMA((N_CHIP - 1,)), pltpu.SemaphoreType.DMA],
                  start, finish, lambda outs: outs[0])


def _sum_slots(x, *, name):
    n, R, _ = x.shape
    tr = _pick(R, (1024, 512, 256, 128, 64, 32, 16, 8))

    def body(x_ref, o_ref):
        acc = x_ref[0].astype(F32)
        for j in range(1, n):
            acc = acc + x_ref[j].astype(F32)
        o_ref[...] = acc

    return pl.pallas_call(
        body, name=name, out_shape=jax.ShapeDtypeStruct((R, 128), F32), grid=(R // tr,),
        in_specs=[pl.BlockSpec((n, tr, 128), lambda i: (0, i, 0))], out_specs=pl.BlockSpec((tr, 128), lambda i: (i, 0)),
        compiler_params=_params(("parallel",)),
    )(x)


def _adamw(w, g, m, v, *, name):
    shape = w.shape
    cols = shape[-1]
    rows = int(np.prod(shape[:-1]))
    tr = next((t for t in (1024, 512, 256, 128, 64, 32, 16, 8) if rows % t == 0 and t * cols * 4 <= (1 << 20)), rows)

    def body(w_ref, g_ref, m_ref, v_ref, d_ref, mo_ref, vo_ref):
        g_ = g_ref[...]
        m_ = ADAM_B1 * m_ref[...] + (1.0 - ADAM_B1) * g_
        v_ = ADAM_B2 * v_ref[...] + (1.0 - ADAM_B2) * jnp.square(g_)
        m_hat = m_ / (1.0 - ADAM_B1 ** ADAM_STEP)
        v_hat = v_ / (1.0 - ADAM_B2 ** ADAM_STEP)
        d_ref[...] = -ADAM_LR * (m_hat / (jnp.sqrt(v_hat) + ADAM_EPS) + ADAM_WD * w_ref[...])
        mo_ref[...] = m_
        vo_ref[...] = v_

    blk = pl.BlockSpec((tr, cols), lambda i: (i, 0))
    outs = pl.pallas_call(
        body, name=name, out_shape=tuple(jax.ShapeDtypeStruct((rows, cols), F32) for _ in range(3)), grid=(rows // tr,),
        in_specs=[blk] * 4, out_specs=(blk,) * 3, compiler_params=_params(("parallel",)),
    )(*(a.reshape(rows, cols) for a in (w, g, m, v)))
    return tuple(o.reshape(shape) for o in outs)


_WEIGHTS = ('g_mix', 'w_in', 'b_fox_forget', 'w_gla_gate', 'b_gla_gate', 'g_gla_out', 'g_mla_q', 'w_mla_uq', 'g_mla_kv',
            'w_mla_ukv', 'b_branch_gate', 'w_up_fox', 'w_up_gla', 'w_up_mla', 'w_out', 'g_xa', 'g_mem', 'w_xq', 'w_xkv',
            'w_xo', 'g_mlp', 'w_mlp1', 'w_mlp2', 'g_final')
_SHARDED = (('w_in', 1), ('w_gla_gate', 2), ('w_mla_uq', 2), ('w_mla_ukv', 2), ('w_up_fox', 2), ('w_up_gla', 2),
            ('w_up_mla', 2), ('w_out', 1), ('w_xq', 1), ('w_xkv', 1), ('w_xo', 2), ('w_mlp1', 2), ('w_mlp2', 1))
_REPLICATED = tuple(n for n in _WEIGHTS if n not in dict(_SHARDED))
_ROW_PAD = 1024
_SMALL_ROW_PAD = 8
_PIECE_ROWS = 16


def _pack(flats, lead, row_pad=_ROW_PAD):
    if all(int(np.prod(a.shape[lead:])) % 128 == 0 for a in flats):
        def block(a):
            a = a.reshape(a.shape[:lead] + (-1, 128))
            return jnp.pad(a, [(0, 0)] * lead + [(0, -a.shape[lead] % _PIECE_ROWS), (0, 0)])
        cat = jnp.concatenate([block(a) for a in flats], axis=lead)
        rows = cat.shape[lead]
        return jnp.pad(cat, [(0, 0)] * lead + [(0, -(-rows // row_pad) * row_pad - rows), (0, 0)])
    cat = jnp.concatenate([a.reshape(a.shape[:lead] + (-1,)) for a in flats], axis=-1)
    n = cat.shape[-1]
    total = -(-n // (128 * row_pad)) * (128 * row_pad)
    cat = jnp.pad(cat, [(0, 0)] * lead + [(0, total - n)])
    return cat.reshape(cat.shape[:lead] + (total // 128, 128))


def _unpack(buf, shapes, lead):
    sizes = [int(np.prod(shp)) for shp in shapes]
    out, off = [], 0
    if all(n % 128 == 0 for n in sizes):
        for shp, n in zip(shapes, sizes):
            rows = buf[(slice(None),) * lead + (slice(off, off + n // 128),)]
            out.append(rows.reshape(buf.shape[:lead] + tuple(shp)))
            off += -(-(n // 128) // _PIECE_ROWS) * _PIECE_ROWS
        return out
    flat = buf.reshape(buf.shape[:lead] + (-1,))
    for shp, n in zip(shapes, sizes):
        out.append(flat[..., off:off + n].reshape(buf.shape[:lead] + tuple(shp)))
        off += n
    return out


def _to_whole(g, axis):
    if axis == 1:
        return g.transpose(1, 0, 2, 3).reshape(g.shape[1], N_DEV * g.shape[2], g.shape[3])
    return g.transpose(1, 2, 0, 3).reshape(g.shape[1], g.shape[2], N_DEV * g.shape[3])


def _to_shards(w, axis):
    L, R, C = w.shape
    if axis == 1:
        return w.reshape(L, N_DEV, R // N_DEV, C).transpose(1, 0, 2, 3)
    return w.reshape(L, R, N_DEV, C // N_DEV).transpose(2, 0, 1, 3)


def kernel(x, mem, g_mix, w_in, b_fox_forget, w_gla_gate, b_gla_gate, g_gla_out, g_mla_q, w_mla_uq, g_mla_kv, w_mla_ukv, b_branch_gate, w_up_fox, w_up_gla, w_up_mla, w_out, g_xa, g_mem, w_xq, w_xkv, w_xo, g_mlp, w_mlp1, w_mlp2, g_final, loss_target, m_g_mix, m_w_in, m_b_fox_forget, m_w_gla_gate, m_b_gla_gate, m_g_gla_out, m_g_mla_q, m_w_mla_uq, m_g_mla_kv, m_w_mla_ukv, m_b_branch_gate, m_w_up_fox, m_w_up_gla, m_w_up_mla, m_w_out, m_g_xa, m_g_mem, m_w_xq, m_w_xkv, m_w_xo, m_g_mlp, m_w_mlp1, m_w_mlp2, m_g_final, v_g_mix, v_w_in, v_b_fox_forget, v_w_gla_gate, v_b_gla_gate, v_g_gla_out, v_g_mla_q, v_w_mla_uq, v_g_mla_kv, v_w_mla_ukv, v_b_branch_gate, v_w_up_fox, v_w_up_gla, v_w_up_mla, v_w_out, v_g_xa, v_g_mem, v_w_xq, v_w_xkv, v_w_xo, v_g_mlp, v_w_mlp1, v_w_mlp2, v_g_final):
    wts = dict(zip(_WEIGHTS, (g_mix, w_in, b_fox_forget, w_gla_gate, b_gla_gate, g_gla_out, g_mla_q, w_mla_uq, g_mla_kv,
                              w_mla_ukv, b_branch_gate, w_up_fox, w_up_gla, w_up_mla, w_out, g_xa, g_mem, w_xq, w_xkv, w_xo,
                              g_mlp, w_mlp1, w_mlp2, g_final)))
    mom1 = dict(zip(_WEIGHTS, (m_g_mix, m_w_in, m_b_fox_forget, m_w_gla_gate, m_b_gla_gate, m_g_gla_out, m_g_mla_q,
                               m_w_mla_uq, m_g_mla_kv, m_w_mla_ukv, m_b_branch_gate, m_w_up_fox, m_w_up_gla, m_w_up_mla,
                               m_w_out, m_g_xa, m_g_mem, m_w_xq, m_w_xkv, m_w_xo, m_g_mlp, m_w_mlp1, m_w_mlp2, m_g_final)))
    mom2 = dict(zip(_WEIGHTS, (v_g_mix, v_w_in, v_b_fox_forget, v_w_gla_gate, v_b_gla_gate, v_g_gla_out, v_g_mla_q,
                               v_w_mla_uq, v_g_mla_kv, v_w_mla_ukv, v_b_branch_gate, v_w_up_fox, v_w_up_gla, v_w_up_mla,
                               v_w_out, v_g_xa, v_g_mem, v_w_xq, v_w_xkv, v_w_xo, v_g_mlp, v_w_mlp1, v_w_mlp2, v_g_final)))
    depth = g_mix.shape[0]

    names = [n for n, _ in _SHARDED]
    axes = dict(_SHARDED)
    shard = {n: wts[n] for n in names}
    shard['w_in'] = w_in[:, :, _IN_IDX] * _IN_SIGN
    rep = {n: wts[n] for n in _REPLICATED}
    ps = [_LayerParams(rep, l) for l in range(depth)]

    def gather(group, l):
        rider = _gather_rider([shard[n][l:l + 1].astype(BF16) for n in group], [axes[n] for n in group])
        return rider, lambda whole: ps[l].w.update({n: w[0] for n, w in zip(group, whole)})

    first, sink = gather(['w_in'], 0)
    sink(_run_rider(first, name="gather_w_in_0"))
    hooks = {(0, 'fox_fwd'): gather([n for n in names if n != 'w_in'], 0)}
    for l in range(1, depth):
        hooks[(l - 1, 'mla_fwd')] = gather(names, l)

    core = lax.axis_index("c").astype(jnp.int32).reshape(1)
    shard_shapes = [(1,) + shard[n].shape[1:] for n in names]
    small_grads, landed = {}, {}

    def layer_done(l, g):
        small_grads[l] = g
        slots = _pack([_to_shards(g[n][None], axes[n]).astype(BF16) for n in names], 1)
        slots = slots.reshape((N_CHIP, 2) + slots.shape[1:])
        paired = _pair_sum(slots, _sibling_swap(slots, name=f"swap_grads_{l}"), core, name=f"pair_grads_{l}")
        rider = _chip_all_to_all_rider(paired)
        if l > 0:
            hooks[(l - 1, 'fox_bwd')] = (rider, lambda got: landed.update({l: got}))
        else:
            landed[l] = _run_rider(rider, name=f"scatter_grads_{l}")

    loss, dx, dg_final = _local_step(x[0], mem[0], loss_target[0], ps, g_final, hooks, layer_done)
    loss = lax.psum(loss[0, 0], _MESH_AXES)

    per_layer = [_unpack(_sum_slots(landed[l], name=f"sum_grads_{l}"), shard_shapes, 0) for l in range(depth)]
    grad = {n: jnp.concatenate([per_layer[l][i] for l in range(depth)], axis=0) for i, n in enumerate(names)}
    grad['w_in'] = grad['w_in'][:, :, _IN_INV] + grad['w_in'][:, :, _IN_INV2] * _IN_SIGN2
    grads = small_grads
    small = [dg_final if n == 'g_final' else jnp.stack([grads[l][n] for l in range(depth)]) for n in _REPLICATED]
    small_shapes = [wts[n].shape for n in _REPLICATED]
    small_sum = _sum_slots(_all_gather(_pack(small, 0, _SMALL_ROW_PAD), name="gather_small_grads"), name="sum_small_grads")
    grad.update(dict(zip(_REPLICATED, _unpack(small_sum, small_shapes, 0))))

    delta, new_m, new_v = {}, {}, {}
    for n, _ in _SHARDED:
        delta[n], new_m[n], new_v[n] = _adamw(wts[n], grad[n], mom1[n], mom2[n], name=f"adamw_{n}")
    packed = [_pack([d[n] for n in _REPLICATED], 0, _SMALL_ROW_PAD) for d in (wts, mom1, mom2)]
    outs = _adamw(packed[0], small_sum, packed[1], packed[2], name="adamw_small")
    for d, o in zip((delta, new_m, new_v), outs):
        d.update(dict(zip(_REPLICATED, _unpack(o, small_shapes, 0))))

    return (loss, dx[None], *[grad[n] for n in _WEIGHTS], *[delta[n] for n in _WEIGHTS],
            *[new_m[n] for n in _WEIGHTS], *[new_v[n] for n in _WEIGHTS])
```

```python
import jax
import jax.numpy as jnp
import numpy as np
from jax import lax
from jax.experimental import pallas as pl
from jax.experimental.pallas import tpu as pltpu

F32 = jnp.float32
BF16 = jnp.bfloat16

EPS = 1e-6
CHUNK = 64
FOX_HEADS, FOX_HD = 4, 64
GLA_HEADS, GLA_DK, GLA_DV, GLA_RANK, GLA_TAU = 4, 64, 128, 16, 16.0
MLA_HEADS, MLA_Q_RANK, MLA_KV_RANK, MLA_NOPE, MLA_ROPE, MLA_VD = 4, 256, 128, 64, 32, 64
ROPE_BASE = 10000.0
XA_HEADS, XA_HD = 4, 128
ADAM_LR, ADAM_B1, ADAM_B2, ADAM_EPS, ADAM_WD, ADAM_STEP = 0.001, 0.9, 0.999, 1e-08, 0.01, 10

N_DEV = 8
V7X_VMEM_LIMIT = 56 * 1024 * 1024
NEG = -1e30

NN = ((1,), (0,))
NT = ((1,), (1,))
TN = ((0,), (0,))


def _dot(a, b, dims):
    return lax.dot_general(a.astype(BF16), b.astype(BF16), (dims, ((), ())), preferred_element_type=F32)


@jax.custom_vjp
def bdot(a, b):
    return _dot(a, b, NN)


bdot.defvjp(lambda a, b: (_dot(a, b, NN), (a, b)),
            lambda res, g: (_dot(g, res[1], NT), _dot(res[0], g, TN)))


@jax.custom_vjp
def bdot_nt(a, b):
    return _dot(a, b, NT)


bdot_nt.defvjp(lambda a, b: (_dot(a, b, NT), (a, b)),
               lambda res, g: (_dot(g, res[1], NN), _dot(g, res[0], TN)))


@jax.custom_vjp
def bdot_tn(a, b):
    return _dot(a, b, TN)


bdot_tn.defvjp(lambda a, b: (_dot(a, b, TN), (a, b)),
               lambda res, g: (_dot(res[1], g, NT), _dot(res[0], g, NN)))


def _split2(x):
    hi = x.astype(BF16)
    lo = (x - hi.astype(F32)).astype(BF16)
    return hi, lo


def _tri(n, lower):
    r = lax.broadcasted_iota(jnp.int32, (n, n), 0)
    c = lax.broadcasted_iota(jnp.int32, (n, n), 1)
    return jnp.where((r >= c) if lower else (r <= c), 1.0, 0.0).astype(BF16)


def _tri_dot2(x, lower):
    hi, lo = _split2(x)
    t = _tri(x.shape[0], lower)
    return _dot(t, hi, NN) + _dot(t, lo, NN)


@jax.custom_vjp
def chunk_cumsum(x):
    return _tri_dot2(x, True)


chunk_cumsum.defvjp(lambda x: (_tri_dot2(x, True), None), lambda _, g: (_tri_dot2(g, False),))


def _log_sigmoid(x):
    return jnp.minimum(x, 0.0) - jnp.log(1.0 + jnp.exp(-jnp.abs(x)))


def _sigmoid(x):
    return 1.0 / (1.0 + jnp.exp(-x))


def _rms(x, g):
    return x * lax.rsqrt(jnp.mean(x * x, axis=-1, keepdims=True) + EPS) * g


def _pick(dim, prefs):
    for p in prefs:
        if dim % p == 0:
            return p
    return dim


def _params(sem):
    return pltpu.CompilerParams(dimension_semantics=sem, vmem_limit_bytes=V7X_VMEM_LIMIT)


def _mm(a, b, *, mode, out_dtype, name, act=None, residual=None, drelu_of=None, b_cols=None, tm=None, tn=None, tk=None):
    b_off, b_width = b_cols or (0, b.shape[1])
    if mode == 'nn':
        (M, K), N = a.shape, b_width
    elif mode == 'nt':
        (M, K), N = a.shape, b.shape[0]
    else:
        (K, M), N = a.shape, b_width
    tm = tm or _pick(M, (1024, 512, 256, 128))
    tn = tn or _pick(N, (1024, 1920, 1152, 768, 640, 512, 384, 256, 128))
    tk = tk or _pick(K, (1024, 1920, 1152, 640, 512, 256, 128))
    nk = K // tk
    dims = {'nn': NN, 'nt': NT, 'tn': TN}[mode]
    a_spec = pl.BlockSpec((tk, tm), lambda i, j, k: (k, i)) if mode == 'tn' else pl.BlockSpec((tm, tk), lambda i, j, k: (i, k))
    if mode == 'nt':
        b_spec = pl.BlockSpec((tn, tk), lambda i, j, k, o=b_off // tk: (j, k + o))
    else:
        b_spec = pl.BlockSpec((tk, tn), lambda i, j, k, o=b_off // tn: (k, j + o))
    o_spec = pl.BlockSpec((tm, tn), lambda i, j, k: (i, j))
    extra = [e for e in (residual, drelu_of) if e is not None]

    def body(a_ref, b_ref, *rest):
        o_ref = rest[len(extra)]
        at = a_ref[...]
        if act == 'relu2':
            at = jnp.square(jnp.maximum(at.astype(F32), 0.0))
        part = _dot(at, b_ref[...], dims)

        def finish(acc):
            idx = 0
            if residual is not None:
                acc = acc + rest[idx][...]
                idx += 1
            if drelu_of is not None:
                acc = acc * (2.0 * jnp.maximum(rest[idx][...].astype(F32), 0.0))
            o_ref[...] = acc.astype(out_dtype)

        if nk == 1:
            finish(part)
        else:
            acc_ref = rest[len(extra) + 1]
            k = pl.program_id(2)

            @pl.when(k == 0)
            def _():
                acc_ref[...] = part

            @pl.when(k > 0)
            def _():
                acc_ref[...] += part

            @pl.when(k == nk - 1)
            def _():
                finish(acc_ref[...])

    return pl.pallas_call(
        body, name=name,
        out_shape=jax.ShapeDtypeStruct((M, N), out_dtype),
        grid=(M // tm, N // tn, nk),
        in_specs=[a_spec, b_spec] + [o_spec] * len(extra),
        out_specs=o_spec,
        scratch_shapes=[] if nk == 1 else [pltpu.VMEM((tm, tn), F32)],
        compiler_params=_params(("parallel", "parallel", "arbitrary")),
    )(a, b, *extra)


def _rms_fwd(x, g, *, name, out_dtype=BF16):
    S, D = x.shape
    tr = _pick(S, (512, 256, 128))

    def body(x_ref, g_ref, o_ref):
        o_ref[...] = _rms(x_ref[...], g_ref[...]).astype(out_dtype)

    return pl.pallas_call(
        body, name=name, out_shape=jax.ShapeDtypeStruct((S, D), out_dtype), grid=(S // tr,),
        in_specs=[pl.BlockSpec((tr, D), lambda i: (i, 0)), pl.BlockSpec((1, D), lambda i: (0, 0))],
        out_specs=pl.BlockSpec((tr, D), lambda i: (i, 0)),
        compiler_params=_params(("parallel",)),
    )(x, g.reshape(1, D))


def _rms_bwd(x, g, dy, dres, *, name):
    S, D = x.shape
    tr = _pick(S, (512, 256, 128))

    def body(x_ref, g_ref, dy_ref, *rest):
        dx_ref, dxb_ref, dg_ref = rest[-3], rest[-2], rest[-1]
        x_ = x_ref[...]
        rstd = lax.rsqrt(jnp.mean(x_ * x_, axis=-1, keepdims=True) + EPS)
        xh = x_ * rstd
        dy_ = dy_ref[...].astype(F32)
        gdy = dy_ * g_ref[...]
        dx = (gdy - xh * jnp.mean(gdy * xh, axis=-1, keepdims=True)) * rstd
        if dres is not None:
            dx = dx + rest[0][...]
        dx_ref[...] = dx
        dxb_ref[...] = dx.astype(BF16)
        part = jnp.sum(dy_ * xh, axis=0, keepdims=True)

        @pl.when(pl.program_id(0) == 0)
        def _():
            dg_ref[...] = part

        @pl.when(pl.program_id(0) > 0)
        def _():
            dg_ref[...] += part

    row = pl.BlockSpec((tr, D), lambda i: (i, 0))
    vec = pl.BlockSpec((1, D), lambda i: (0, 0))
    ins = [x, g.reshape(1, D), dy] + ([dres] if dres is not None else [])
    return pl.pallas_call(
        body, name=name,
        out_shape=(jax.ShapeDtypeStruct((S, D), F32), jax.ShapeDtypeStruct((S, D), BF16), jax.ShapeDtypeStruct((1, D), F32)),
        grid=(S // tr,),
        in_specs=[row, vec, row] + ([row] if dres is not None else []),
        out_specs=(row, row, vec),
        compiler_params=_params(("arbitrary",)),
    )(*ins)


def _loss_head(x, g, target, *, name):
    S, D = x.shape
    tr = _pick(S, (512, 256, 128))

    def body(x_ref, g_ref, t_ref, l_ref, dx_ref, dxb_ref, dg_ref):
        x_ = x_ref[...]
        g_ = g_ref[...]
        rstd = lax.rsqrt(jnp.mean(x_ * x_, axis=-1, keepdims=True) + EPS)
        xh = x_ * rstd
        err = xh * g_ - t_ref[...]
        lpart = (0.5 / D) * jnp.sum(jnp.sum(err * err, axis=-1, keepdims=True), axis=0, keepdims=True)
        dy = err * (1.0 / D)
        gdy = dy * g_
        dx = (gdy - xh * jnp.mean(gdy * xh, axis=-1, keepdims=True)) * rstd
        dx_ref[...] = dx
        dxb_ref[...] = dx.astype(BF16)
        gpart = jnp.sum(dy * xh, axis=0, keepdims=True)

        @pl.when(pl.program_id(0) == 0)
        def _():
            dg_ref[...] = gpart
            l_ref[...] = lpart

        @pl.when(pl.program_id(0) > 0)
        def _():
            dg_ref[...] += gpart
            l_ref[...] += lpart

    row = pl.BlockSpec((tr, D), lambda i: (i, 0))
    vec = pl.BlockSpec((1, D), lambda i: (0, 0))
    return pl.pallas_call(
        body, name=name,
        out_shape=(jax.ShapeDtypeStruct((1, 1), F32), jax.ShapeDtypeStruct((S, D), F32), jax.ShapeDtypeStruct((S, D), BF16),
                   jax.ShapeDtypeStruct((1, D), F32)),
        grid=(S // tr,),
        in_specs=[row, vec, row],
        out_specs=(pl.BlockSpec((1, 1), lambda i: (0, 0)), row, row, vec),
        compiler_params=_params(("arbitrary",)),
    )(x, g.reshape(1, D), target)


def _mask_of(mask, tq, tk):
    qpos = lax.broadcasted_iota(jnp.int32, (tq, tk), 0)
    kpos = lax.broadcasted_iota(jnp.int32, (tq, tk), 1)
    if mask == 'causal':
        return kpos <= qpos
    return kpos <= (qpos | (CHUNK - 1))


LANES = 128


def _lane_group(j, w, width):
    lane = lax.broadcasted_iota(jnp.int32, (1, width), 1)
    return (lane >= j * w) & (lane < (j + 1) * w)


def _only(x, j, w):
    if w == x.shape[1]:
        return x
    return jnp.where(_lane_group(j, w, x.shape[1]), x, jnp.zeros_like(x))


def _per_head(cols, w):
    out = cols[-1]
    for j in range(len(cols) - 2, -1, -1):
        out = jnp.where(_lane_group(j, w, LANES), cols[j], out)
    return out


def _side_by_side(xs):
    return xs[0] if len(xs) == 1 else jnp.concatenate(xs, axis=1)


def _on_top(xs):
    return xs[0] if len(xs) == 1 else jnp.concatenate(xs, axis=0)


def _stacked(x, hp, w):
    return _on_top([_only(x, j, w) for j in range(hp)])


def _col_block(entry, rows, idx):
    arr, off, width = entry
    return pl.BlockSpec((rows, width), lambda i, j, o=off // width: (idx(i, j), o))


def _attn_fwd(qk, v, H, cq, ck, *, scale, mask, name, rider=None):
    Sq, Sk = qk[0][0][0].shape[0], v[0].shape[0]
    dv = v[2] // H
    w0 = qk[0][2]
    hp = LANES // w0
    G = H // hp
    assert dv == w0 and not qk[0][3] and all(sh and H * w == LANES for _, _, w, sh in qk[1:])
    tq = _pick(Sq, (512, 256, 128))
    tk = tq if mask else _pick(Sk, (512, 256, 128))
    nq, nk = Sq // tq, Sk // tk
    bias = cq is not None
    npart = len(qk)

    def body(*refs):
        refs = split(refs)
        q_refs, k_refs = refs[0:2 * npart:2], refs[1:2 * npart:2]
        v_ref = refs[2 * npart]
        cq_ref, ck_ref = (refs[2 * npart + 1], refs[2 * npart + 2]) if bias else (None, None)
        o_ref, lse_ref, m_s, l_s, acc_s = refs[-5:]
        qi, ki = pl.program_id(0), pl.program_id(1)

        @pl.when(ki == 0)
        def _():
            m_s[...] = jnp.full(m_s.shape, NEG, F32)
            l_s[...] = jnp.zeros(l_s.shape, F32)
            acc_s[...] = jnp.zeros(acc_s.shape, F32)

        def compute(masked):
            keep = _mask_of(mask, tq, tk) if masked else None
            for g in range(G):
                lanes = slice(g * LANES, (g + 1) * LANES)
                q128, k128, v128 = q_refs[0][:, lanes], k_refs[0][:, lanes], v_ref[:, lanes]
                ps, alphas = [], []
                extras = list(zip(qk, q_refs, k_refs))[1:]
                k_all = _side_by_side([k128] + [k_ref[...] for _, _, k_ref in extras])
                for j in range(hp):
                    h = g * hp + j
                    q_all = _side_by_side([_only(q128, j, w0)] + [_only(q_ref[...], h, w) for (_, _, w, _), q_ref, _ in extras])
                    s = _dot(q_all, k_all, NT) * scale
                    if bias:
                        s = s + (cq_ref[:, h:h + 1] - ck_ref[h:h + 1, :])
                    if masked:
                        s = jnp.where(keep, s, NEG)
                    m_prev = m_s[h]
                    m_new = jnp.maximum(m_prev, jnp.max(s, axis=1, keepdims=True))
                    alpha = jnp.exp(m_prev - m_new)
                    p = jnp.exp(s - m_new)
                    l_s[h] = alpha * l_s[h] + jnp.sum(p, axis=1, keepdims=True)
                    m_s[h] = m_new
                    ps.append(p.astype(BF16))
                    alphas.append(alpha)
                acc_s[g] = _per_head(alphas, w0) * acc_s[g] + _dot(_side_by_side(ps), _stacked(v128, hp, w0), NN)

        if mask is None:
            compute(False)
        else:
            pl.when(ki < qi)(lambda: compute(False))
            pl.when(ki == qi)(lambda: compute(True))

        @pl.when(ki == ((nk - 1) if mask is None else qi))
        def _():
            lse_ref[...] = jnp.zeros(lse_ref.shape, F32)
            for g in range(G):
                o_ref[:, g * LANES:(g + 1) * LANES] = (
                    acc_s[g] / _per_head([l_s[g * hp + j] for j in range(hp)], w0)).astype(BF16)
            for h in range(H):
                lse_ref[:, h:h + 1] = m_s[h] + jnp.log(l_s[h])

    q_idx = lambda i, j: i
    k_idx = (lambda i, j: jnp.minimum(i, j)) if mask else (lambda i, j: j)
    ins, in_specs = [], []
    for q_e, k_e, _, _ in qk:
        ins += [q_e[0], k_e[0]]
        in_specs += [_col_block(q_e, tq, q_idx), _col_block(k_e, tk, k_idx)]
    ins.append(v[0])
    in_specs.append(_col_block(v, tk, k_idx))
    if bias:
        in_specs += [pl.BlockSpec((tq, 8), lambda i, j: (i, 0)), pl.BlockSpec((8, tk), lambda i, j: (0, k_idx(i, j)))]
        ins += [cq, ck]
    r_ins, r_in_specs, r_outs, r_out_specs, r_scratch, split = _carry(
        rider, len(ins), 2, lambda: (pl.program_id(0) == 0) & (pl.program_id(1) == 0),
        lambda: (pl.program_id(0) == nq - 1) & (pl.program_id(1) == nk - 1))
    res = pl.pallas_call(
        body, name=name,
        out_shape=(jax.ShapeDtypeStruct((Sq, H * dv), BF16), jax.ShapeDtypeStruct((Sq, 8), F32), *r_outs),
        grid=(nq, nk), in_specs=in_specs + r_in_specs,
        out_specs=(pl.BlockSpec((tq, H * dv), lambda i, j: (i, 0)), pl.BlockSpec((tq, 8), lambda i, j: (i, 0)), *r_out_specs),
        scratch_shapes=[pltpu.VMEM((H, tq, 1), F32), pltpu.VMEM((H, tq, 1), F32), pltpu.VMEM((G, tq, LANES), F32)] + r_scratch,
        compiler_params=_params(("arbitrary", "arbitrary")) if rider else _params(("parallel", "arbitrary")),
    )(*ins, *r_ins)
    return (res[0], res[1], rider.post(res[2:])) if rider else res


def _attn_bwd(qk, v, H, o, do, lse, cq, ck, *, scale, mask, name, rider=None):
    Sq, Sk = qk[0][0][0].shape[0], v[0].shape[0]
    dv = v[2] // H
    w0 = qk[0][2]
    hp = LANES // w0
    G = H // hp
    tq = _pick(Sq, (512, 256, 128))
    tk = tq if mask else _pick(Sk, (512, 256, 128))
    nq, nk = Sq // tq, Sk // tk
    bias = cq is not None
    npart = len(qk)
    n_in = 2 * npart + 4 + (2 if bias else 0)

    def body(*refs):
        refs = split(refs)
        q_refs, k_refs = refs[0:2 * npart:2], refs[1:2 * npart:2]
        v_ref, o_ref, do_ref, lse_ref = refs[2 * npart:2 * npart + 4]
        cq_ref, ck_ref = (refs[2 * npart + 4], refs[2 * npart + 5]) if bias else (None, None)
        outs = refs[n_in:]
        dq_refs, dk_refs, dv_ref = outs[:npart], outs[npart:2 * npart], outs[2 * npart]
        dck_ref, dcq_ref = (outs[2 * npart + 1], outs[2 * npart + 2]) if bias else (None, None)
        dk_accs, dv_acc = refs[-(npart + 1):-1], refs[-1]
        ki, qi = pl.program_id(0), pl.program_id(1)
        first_q = ki if mask else 0

        @pl.when((ki == 0) & (qi == 0))
        def _():
            for r in dq_refs:
                r[...] = jnp.zeros(r.shape, F32)
            if bias:
                dcq_ref[...] = jnp.zeros(dcq_ref.shape, F32)

        @pl.when(qi == first_q)
        def _():
            for r in dk_accs:
                r[...] = jnp.zeros(r.shape, F32)
            dv_acc[...] = jnp.zeros(dv_acc.shape, F32)
            if bias:
                dck_ref[...] = jnp.zeros(dck_ref.shape, F32)

        def compute(masked):
            keep = _mask_of(mask, tq, tk) if masked else None
            rows = pl.ds(pl.multiple_of(qi * tq, tq), tq)
            extras = list(zip(qk, q_refs, k_refs, dq_refs, dk_accs))[1:]
            for g in range(G):
                lanes = slice(g * LANES, (g + 1) * LANES)
                q128, k128, v128 = q_refs[0][:, lanes], k_refs[0][:, lanes], v_ref[:, lanes]
                do128, o128 = do_ref[:, lanes], o_ref[:, lanes]
                ps, dss = [], []
                k_all = _side_by_side([k128] + [e[2][...] for e in extras])
                for j in range(hp):
                    h = g * hp + j
                    q_all = _side_by_side([_only(q128, j, w0)] + [_only(e[1][...], h, e[0][2]) for e in extras])
                    s = _dot(q_all, k_all, NT) * scale
                    if bias:
                        s = s + (cq_ref[:, h:h + 1] - ck_ref[h:h + 1, :])
                    if masked:
                        s = jnp.where(keep, s, NEG)
                    p = jnp.exp(s - lse_ref[:, h:h + 1])
                    doh = _only(do128, j, w0)
                    dp = _dot(doh, v128, NT)
                    delta = jnp.sum(doh.astype(F32) * o128.astype(F32), axis=1, keepdims=True)
                    ds = p * (dp - delta)
                    if bias:
                        dck_ref[h:h + 1, :] -= jnp.sum(ds, axis=0, keepdims=True)
                        dcq_ref[rows, h:h + 1] += jnp.sum(ds, axis=1, keepdims=True)
                    ps.append(p.astype(BF16))
                    dss.append((ds * scale).astype(BF16))
                for (_, _, w, _), q_ref, k_ref, dq_ref, dk_acc in extras:
                    heads = range(g * hp, (g + 1) * hp)
                    dk_acc[...] += _dot(_on_top(dss), _on_top([_only(q_ref[...], h, w) for h in heads]), TN)
                    dq_ref[rows, :] += _dot(_side_by_side(dss), _on_top([_only(k_ref[...], h, w) for h in heads]), NN)
                dv_acc[:, lanes] += _dot(_on_top(ps), _stacked(do128, hp, w0), TN)
                dk_accs[0][:, lanes] += _dot(_on_top(dss), _stacked(q128, hp, w0), TN)
                dq_refs[0][rows, lanes] += _dot(_side_by_side(dss), _stacked(k128, hp, w0), NN)

        if mask is None:
            compute(False)
        else:
            pl.when(qi > ki)(lambda: compute(False))
            pl.when(qi == ki)(lambda: compute(True))

        @pl.when(qi == nq - 1)
        def _():
            for r, acc in zip(dk_refs, dk_accs):
                r[...] = acc[...]
            dv_ref[...] = dv_acc[...]

    q_idx = (lambda j, i: jnp.maximum(i, j)) if mask else (lambda j, i: i)
    k_idx = lambda j, i: j
    ins, in_specs, dq_shapes, dq_specs, dk_shapes, dk_specs, scratch = [], [], [], [], [], [], []
    for q_e, k_e, w, shared in qk:
        ins += [q_e[0], k_e[0]]
        in_specs += [_col_block(q_e, tq, q_idx), _col_block(k_e, tk, k_idx)]
        dq_shapes.append(jax.ShapeDtypeStruct((Sq, H * w), F32))
        dq_specs.append(pl.BlockSpec((Sq, H * w), lambda j, i: (0, 0)))
        kw = k_e[2]
        dk_shapes.append(jax.ShapeDtypeStruct((Sk, kw), F32))
        dk_specs.append(pl.BlockSpec((tk, kw), lambda j, i: (j, 0)))
        scratch.append(pltpu.VMEM((tk, kw), F32))
    row_q = lambda width: pl.BlockSpec((tq, width), lambda j, i: (q_idx(j, i), 0))
    ins += [v[0], o, do, lse]
    in_specs += [_col_block(v, tk, k_idx), row_q(H * dv), row_q(H * dv), row_q(8)]
    out_shape = dq_shapes + dk_shapes + [jax.ShapeDtypeStruct((Sk, H * dv), F32)]
    out_specs = dq_specs + dk_specs + [pl.BlockSpec((tk, H * dv), lambda j, i: (j, 0))]
    if bias:
        in_specs += [row_q(8), pl.BlockSpec((8, tk), lambda j, i: (0, j))]
        ins += [cq, ck]
        out_shape += [jax.ShapeDtypeStruct((8, Sk), F32), jax.ShapeDtypeStruct((Sq, 8), F32)]
        out_specs += [pl.BlockSpec((8, tk), lambda j, i: (0, j)), pl.BlockSpec((Sq, 8), lambda j, i: (0, 0))]
    scratch.append(pltpu.VMEM((tk, H * dv), F32))
    n_out = len(out_shape)
    r_ins, r_in_specs, r_outs, r_out_specs, r_scratch, split = _carry(
        rider, len(ins), n_out, lambda: (pl.program_id(0) == 0) & (pl.program_id(1) == 0),
        lambda: (pl.program_id(0) == nk - 1) & (pl.program_id(1) == nq - 1))
    res = pl.pallas_call(
        body, name=name, out_shape=tuple(out_shape + r_outs), grid=(nk, nq), in_specs=in_specs + r_in_specs,
        out_specs=tuple(out_specs + r_out_specs), scratch_shapes=scratch + r_scratch,
        compiler_params=_params(("arbitrary", "arbitrary")),
    )(*ins, *r_ins)
    own = (list(res[:npart]), list(res[npart:2 * npart]), res[2 * npart]) + tuple(res[2 * npart + 1:n_out])
    return own + (rider.post(res[n_out:]),) if rider else own


def _flash_fwd(q, k, v, cq, ck, *, scale, mask, name):
    H, Sq, dk = q.shape
    Sk, dv = k.shape[1], v.shape[2]
    tq = _pick(Sq, (512, 256, 128))
    tk = tq if mask else _pick(Sk, (512, 256, 128))
    nq, nk = Sq // tq, Sk // tk
    bias = cq is not None

    def body(*refs):
        q_ref, k_ref, v_ref = refs[:3]
        cq_ref, ck_ref = (refs[3], refs[4]) if bias else (None, None)
        o_ref, lse_ref, m_s, l_s, acc_s = refs[-5:]
        qi, ki = pl.program_id(0), pl.program_id(1)

        @pl.when(ki == 0)
        def _():
            m_s[...] = jnp.full(m_s.shape, NEG, F32)
            l_s[...] = jnp.zeros(l_s.shape, F32)
            acc_s[...] = jnp.zeros(acc_s.shape, F32)

        def compute(masked):
            keep = _mask_of(mask, tq, tk) if masked else None
            for h in range(H):
                s = _dot(q_ref[h], k_ref[h], NT) * scale
                if bias:
                    s = s + (cq_ref[:, h:h + 1] - ck_ref[h:h + 1, :])
                if masked:
                    s = jnp.where(keep, s, NEG)
                m_prev = m_s[h]
                m_new = jnp.maximum(m_prev, jnp.max(s, axis=1, keepdims=True))
                alpha = jnp.exp(m_prev - m_new)
                p = jnp.exp(s - m_new)
                l_s[h] = alpha * l_s[h] + jnp.sum(p, axis=1, keepdims=True)
                acc_s[h] = alpha * acc_s[h] + _dot(p, v_ref[h], NN)
                m_s[h] = m_new

        if mask is None:
            compute(False)
        else:
            pl.when(ki < qi)(lambda: compute(False))
            pl.when(ki == qi)(lambda: compute(True))

        @pl.when(ki == ((nk - 1) if mask is None else qi))
        def _():
            lse_ref[...] = jnp.zeros(lse_ref.shape, F32)
            for h in range(H):
                o_ref[h] = (acc_s[h] / l_s[h]).astype(BF16)
                lse_ref[:, h:h + 1] = m_s[h] + jnp.log(l_s[h])

    kv_idx = (lambda i, j: (0, jnp.minimum(i, j), 0)) if mask else (lambda i, j: (0, j, 0))
    ck_idx = (lambda i, j: (0, jnp.minimum(i, j))) if mask else (lambda i, j: (0, j))
    in_specs = [pl.BlockSpec((H, tq, dk), lambda i, j: (0, i, 0)),
                pl.BlockSpec((H, tk, dk), kv_idx), pl.BlockSpec((H, tk, dv), kv_idx)]
    ins = [q, k, v]
    if bias:
        in_specs += [pl.BlockSpec((tq, 8), lambda i, j: (i, 0)), pl.BlockSpec((8, tk), ck_idx)]
        ins += [cq, ck]
    return pl.pallas_call(
        body, name=name,
        out_shape=(jax.ShapeDtypeStruct((H, Sq, dv), BF16), jax.ShapeDtypeStruct((Sq, 8), F32)),
        grid=(nq, nk), in_specs=in_specs,
        out_specs=(pl.BlockSpec((H, tq, dv), lambda i, j: (0, i, 0)), pl.BlockSpec((tq, 8), lambda i, j: (i, 0))),
        scratch_shapes=[pltpu.VMEM((H, tq, 1), F32), pltpu.VMEM((H, tq, 1), F32), pltpu.VMEM((H, tq, dv), F32)],
        compiler_params=_params(("parallel", "arbitrary")),
    )(*ins)


def _flash_bwd(q, k, v, o, do, lse, cq, ck, *, scale, mask, name):
    H, Sq, dk = q.shape
    Sk, dv = k.shape[1], v.shape[2]
    tq = _pick(Sq, (512, 256, 128))
    tk = tq if mask else _pick(Sk, (512, 256, 128))
    nq, nk = Sq // tq, Sk // tk
    bias = cq is not None

    def body(*refs):
        q_ref, k_ref, v_ref, o_ref, do_ref, lse_ref = refs[:6]
        n_in = 8 if bias else 6
        cq_ref, ck_ref = (refs[6], refs[7]) if bias else (None, None)
        outs = refs[n_in:]
        dq_ref, dk_ref, dv_ref = outs[:3]
        dck_ref, dcq_ref = (outs[3], outs[4]) if bias else (None, None)
        dk_s, dv_s = refs[-2], refs[-1]
        ki, qi = pl.program_id(0), pl.program_id(1)
        first_q = ki if mask else 0

        @pl.when((ki == 0) & (qi == 0))
        def _():
            dq_ref[...] = jnp.zeros(dq_ref.shape, F32)
            if bias:
                dcq_ref[...] = jnp.zeros(dcq_ref.shape, F32)

        @pl.when(qi == first_q)
        def _():
            dk_s[...] = jnp.zeros(dk_s.shape, F32)
            dv_s[...] = jnp.zeros(dv_s.shape, F32)
            if bias:
                dck_ref[...] = jnp.zeros(dck_ref.shape, F32)

        def compute(masked):
            keep = _mask_of(mask, tq, tk) if masked else None
            rows = pl.ds(pl.multiple_of(qi * tq, tq), tq)
            for h in range(H):
                qh, kh, vh, doh = q_ref[h], k_ref[h], v_ref[h], do_ref[h]
                s = _dot(qh, kh, NT) * scale
                if bias:
                    s = s + (cq_ref[:, h:h + 1] - ck_ref[h:h + 1, :])
                if masked:
                    s = jnp.where(keep, s, NEG)
                p = jnp.exp(s - lse_ref[:, h:h + 1])
                dp = _dot(doh, vh, NT)
                delta = jnp.sum(doh.astype(F32) * o_ref[h].astype(F32), axis=1, keepdims=True)
                ds = p * (dp - delta)
                dv_s[h] += _dot(p, doh, TN)
                dk_s[h] += _dot(ds, qh, TN)
                dq_ref[h, rows, :] += _dot(ds, kh, NN) * scale
                if bias:
                    dck_ref[h:h + 1, :] -= jnp.sum(ds, axis=0, keepdims=True)
                    dcq_ref[rows, h:h + 1] += jnp.sum(ds, axis=1, keepdims=True)

        if mask is None:
            compute(False)
        else:
            pl.when(qi > ki)(lambda: compute(False))
            pl.when(qi == ki)(lambda: compute(True))

        @pl.when(qi == nq - 1)
        def _():
            dk_ref[...] = dk_s[...] * scale
            dv_ref[...] = dv_s[...]

    q_idx = (lambda j, i: (0, jnp.maximum(i, j), 0)) if mask else (lambda j, i: (0, i, 0))
    c_idx = (lambda j, i: (jnp.maximum(i, j), 0)) if mask else (lambda j, i: (i, 0))
    kv_idx = lambda j, i: (0, j, 0)
    in_specs = [pl.BlockSpec((H, tq, dk), q_idx), pl.BlockSpec((H, tk, dk), kv_idx), pl.BlockSpec((H, tk, dv), kv_idx),
                pl.BlockSpec((H, tq, dv), q_idx), pl.BlockSpec((H, tq, dv), q_idx), pl.BlockSpec((tq, 8), c_idx)]
    ins = [q, k, v, o, do, lse]
    out_shape = [jax.ShapeDtypeStruct((H, Sq, dk), F32), jax.ShapeDtypeStruct((H, Sk, dk), F32),
                 jax.ShapeDtypeStruct((H, Sk, dv), F32)]
    out_specs = [pl.BlockSpec((H, Sq, dk), lambda j, i: (0, 0, 0)), pl.BlockSpec((H, tk, dk), kv_idx),
                 pl.BlockSpec((H, tk, dv), kv_idx)]
    if bias:
        in_specs += [pl.BlockSpec((tq, 8), c_idx), pl.BlockSpec((8, tk), lambda j, i: (0, j))]
        ins += [cq, ck]
        out_shape += [jax.ShapeDtypeStruct((8, Sk), F32), jax.ShapeDtypeStruct((Sq, 8), F32)]
        out_specs += [pl.BlockSpec((8, tk), lambda j, i: (0, j)), pl.BlockSpec((Sq, 8), lambda j, i: (0, 0))]
    return pl.pallas_call(
        body, name=name, out_shape=tuple(out_shape), grid=(nk, nq), in_specs=in_specs, out_specs=tuple(out_specs),
        scratch_shapes=[pltpu.VMEM((H, tk, dk), F32), pltpu.VMEM((H, tk, dv), F32)],
        compiler_params=_params(("arbitrary", "arbitrary")),
    )(*ins)


def _split3_dot(x, t):
    hi = x.astype(BF16)
    r1 = x - hi.astype(F32)
    mid = r1.astype(BF16)
    lo = (r1 - mid.astype(F32)).astype(BF16)
    return _dot(hi, t, NN) + _dot(mid, t, NN) + _dot(lo, t, NN)


def _fox_cum_fwd(ff_t, b, *, name):
    _, S = ff_t.shape
    tb = _pick(S, (512, 256, 128))

    def body(f_ref, b_ref, o_ref, carry):
        @pl.when(pl.program_id(0) == 0)
        def _():
            carry[...] = jnp.zeros(carry.shape, F32)

        lf = _log_sigmoid(f_ref[...] + b_ref[...])
        o_ref[...] = _split3_dot(lf, _tri(tb, False)) + carry[...]
        carry[...] += jnp.sum(lf, axis=1, keepdims=True)

    return pl.pallas_call(
        body, name=name, out_shape=jax.ShapeDtypeStruct((8, S), F32), grid=(S // tb,),
        in_specs=[pl.BlockSpec((8, tb), lambda i: (0, i)), pl.BlockSpec((8, 1), lambda i: (0, 0))],
        out_specs=pl.BlockSpec((8, tb), lambda i: (0, i)),
        scratch_shapes=[pltpu.VMEM((8, 1), F32)],
        compiler_params=_params(("arbitrary",)),
    )(ff_t, b)


def _fox_cum_bwd(ff_t, b, dcum_t, *, name):
    _, S = ff_t.shape
    tb = _pick(S, (512, 256, 128))
    nb = S // tb

    def body(f_ref, b_ref, dc_ref, df_ref, db_ref, carry):
        @pl.when(pl.program_id(0) == 0)
        def _():
            carry[...] = jnp.zeros(carry.shape, F32)
            db_ref[...] = jnp.zeros(db_ref.shape, F32)

        dc = dc_ref[...]
        dlf = _split3_dot(dc, _tri(tb, True)) + carry[...]
        carry[...] += jnp.sum(dc, axis=1, keepdims=True)
        df = dlf * _sigmoid(-(f_ref[...] + b_ref[...]))
        df_ref[...] = df
        db_ref[...] += jnp.sum(df, axis=1, keepdims=True)

    rev = lambda i: (0, nb - 1 - i)
    return pl.pallas_call(
        body, name=name,
        out_shape=(jax.ShapeDtypeStruct((8, S), F32), jax.ShapeDtypeStruct((8, 1), F32)), grid=(nb,),
        in_specs=[pl.BlockSpec((8, tb), rev), pl.BlockSpec((8, 1), lambda i: (0, 0)), pl.BlockSpec((8, tb), rev)],
        out_specs=(pl.BlockSpec((8, tb), rev), pl.BlockSpec((8, 1), lambda i: (0, 0))),
        scratch_shapes=[pltpu.VMEM((8, 1), F32)],
        compiler_params=_params(("arbitrary",)),
    )(ff_t, b, dcum_t)


GLA_W = GLA_HEADS * GLA_DK
GLA_BLOCK_CHUNKS = 4


def _gla_chunk(q, k, zsm, wg, bg, go, vs, rs, states):
    la = _log_sigmoid(bdot(zsm, wg) + bg) * (1.0 / GLA_TAU)
    cum = chunk_cumsum(la)
    end = jnp.sum(la, axis=0, keepdims=True)
    kd = k * jnp.exp(end - cum)
    a = jnp.exp(end)
    qs = q * (GLA_DK ** -0.5)
    lane = lax.broadcasted_iota(jnp.int32, (1, GLA_W), 1)
    outs, new_states = [], []
    for h in range(GLA_HEADS):
        head = jnp.where((lane >= h * GLA_DK) & (lane < (h + 1) * GLA_DK), 1.0, 0.0)
        st = states[h] * a + bdot_tn(vs[h], kd * head)
        o = bdot_nt(qs, st)
        o = _rms(o, go)
        outs.append(o * (rs[h] * _sigmoid(rs[h])))
        new_states.append(st)
    return outs, new_states


def _gla_fwd(z, zsm, wg, bg, go, cols, *, name):
    S = z.shape[0]
    rb = GLA_BLOCK_CHUNKS * CHUNK
    nb = S // rb
    cq, ckk, cv, cr = cols
    H = GLA_HEADS

    def body(q_ref, k_ref, zsm_ref, wg_ref, bg_ref, go_ref, *rest):
        v_refs, r_refs = rest[:H], rest[H:2 * H]
        o_ref, st_ref, state = rest[2 * H], rest[2 * H + 1], rest[2 * H + 2]

        @pl.when(pl.program_id(0) == 0)
        def _():
            state[...] = jnp.zeros(state.shape, F32)

        wg_, bg_, go_ = wg_ref[...], bg_ref[...], go_ref[...]
        for c in range(GLA_BLOCK_CHUNKS):
            rows = pl.ds(c * CHUNK, CHUNK)
            states = [state[h] for h in range(H)]
            for h in range(H):
                st_ref[c, h] = states[h]
            outs, new_states = _gla_chunk(
                q_ref[rows, :].astype(F32), k_ref[rows, :].astype(F32), zsm_ref[rows, :], wg_, bg_, go_,
                [v_refs[h][rows, :].astype(F32) for h in range(H)], [r_refs[h][rows, :].astype(F32) for h in range(H)], states)
            for h in range(H):
                o_ref[rows, h * GLA_DV:(h + 1) * GLA_DV] = outs[h].astype(BF16)
                state[h] = new_states[h]

    def col(width, off):
        return pl.BlockSpec((rb, width), lambda i, o=off // width: (i, o))

    full = lambda shp: pl.BlockSpec(shp, lambda i: (0,) * len(shp))
    in_specs = [col(GLA_W, cq), col(GLA_W, ckk), pl.BlockSpec((rb, 128), lambda i: (i, 0)),
                full((128, GLA_W)), full((1, GLA_W)), full((1, GLA_DV))]
    in_specs += [col(GLA_DV, cv + h * GLA_DV) for h in range(H)] + [col(GLA_DV, cr + h * GLA_DV) for h in range(H)]
    return pl.pallas_call(
        body, name=name,
        out_shape=(jax.ShapeDtypeStruct((S, H * GLA_DV), BF16), jax.ShapeDtypeStruct((S // CHUNK, H, GLA_DV, GLA_W), F32)),
        grid=(nb,), in_specs=in_specs,
        out_specs=(pl.BlockSpec((rb, H * GLA_DV), lambda i: (i, 0)),
                   pl.BlockSpec((GLA_BLOCK_CHUNKS, H, GLA_DV, GLA_W), lambda i: (i, 0, 0, 0))),
        scratch_shapes=[pltpu.VMEM((H, GLA_DV, GLA_W), F32)],
        compiler_params=_params(("arbitrary",)),
    )(z, z, zsm, wg, bg, go, *([z] * (2 * H)))


def _gla_bwd(z, zsm, wg, bg, go, states, do, cols, *, name):
    S = z.shape[0]
    rb = GLA_BLOCK_CHUNKS * CHUNK
    nb = S // rb
    cq, ckk, cv, cr = cols
    H = GLA_HEADS

    def body(q_ref, k_ref, zsm_ref, wg_ref, bg_ref, go_ref, st_ref, do_ref, *rest):
        v_refs, r_refs = rest[:H], rest[H:2 * H]
        dq_ref, dk_ref, dv_ref, dr_ref, dzsm_ref, dwg_ref, dbg_ref, dgo_ref, dstate = rest[2 * H:]

        @pl.when(pl.program_id(0) == 0)
        def _():
            dstate[...] = jnp.zeros(dstate.shape, F32)
            dwg_ref[...] = jnp.zeros(dwg_ref.shape, F32)
            dbg_ref[...] = jnp.zeros(dbg_ref.shape, F32)
            dgo_ref[...] = jnp.zeros(dgo_ref.shape, F32)

        wg_, bg_, go_ = wg_ref[...], bg_ref[...], go_ref[...]
        for c in reversed(range(GLA_BLOCK_CHUNKS)):
            rows = pl.ds(c * CHUNK, CHUNK)
            prim = (q_ref[rows, :].astype(F32), k_ref[rows, :].astype(F32), zsm_ref[rows, :], wg_, bg_, go_,
                    [v_refs[h][rows, :].astype(F32) for h in range(H)], [r_refs[h][rows, :].astype(F32) for h in range(H)],
                    [st_ref[c, h] for h in range(H)])
            _, vjp = jax.vjp(_gla_chunk, *prim)
            douts = [do_ref[rows, h * GLA_DV:(h + 1) * GLA_DV].astype(F32) for h in range(H)]
            dq, dk, dzs, dwg, dbg, dgo, dvs, drs, dsts = vjp((douts, [dstate[h] for h in range(H)]))
            dq_ref[rows, :] = dq.astype(BF16)
            dk_ref[rows, :] = dk.astype(BF16)
            dzsm_ref[rows, :] = dzs
            dwg_ref[...] += dwg
            dbg_ref[...] += dbg
            dgo_ref[...] += dgo
            for h in range(H):
                dv_ref[rows, h * GLA_DV:(h + 1) * GLA_DV] = dvs[h].astype(BF16)
                dr_ref[rows, h * GLA_DV:(h + 1) * GLA_DV] = drs[h].astype(BF16)
                dstate[h] = dsts[h]

    rev = lambda i: nb - 1 - i

    def col(width, off):
        return pl.BlockSpec((rb, width), lambda i, o=off // width: (rev(i), o))

    full = lambda shp: pl.BlockSpec(shp, lambda i: (0,) * len(shp))
    rowb = lambda w: pl.BlockSpec((rb, w), lambda i: (rev(i), 0))
    in_specs = [col(GLA_W, cq), col(GLA_W, ckk), rowb(128), full((128, GLA_W)), full((1, GLA_W)), full((1, GLA_DV)),
                pl.BlockSpec((GLA_BLOCK_CHUNKS, H, GLA_DV, GLA_W), lambda i: (rev(i), 0, 0, 0)), rowb(H * GLA_DV)]
    in_specs += [col(GLA_DV, cv + h * GLA_DV) for h in range(H)] + [col(GLA_DV, cr + h * GLA_DV) for h in range(H)]
    return pl.pallas_call(
        body, name=name,
        out_shape=(jax.ShapeDtypeStruct((S, GLA_W), BF16), jax.ShapeDtypeStruct((S, GLA_W), BF16),
                   jax.ShapeDtypeStruct((S, H * GLA_DV), BF16), jax.ShapeDtypeStruct((S, H * GLA_DV), BF16),
                   jax.ShapeDtypeStruct((S, 128), F32), jax.ShapeDtypeStruct((128, GLA_W), F32),
                   jax.ShapeDtypeStruct((1, GLA_W), F32), jax.ShapeDtypeStruct((1, GLA_DV), F32)),
        grid=(nb,), in_specs=in_specs,
        out_specs=(rowb(GLA_W), rowb(GLA_W), rowb(H * GLA_DV), rowb(H * GLA_DV), rowb(128),
                   full((128, GLA_W)), full((1, GLA_W)), full((1, GLA_DV))),
        scratch_shapes=[pltpu.VMEM((H, GLA_DV, GLA_W), F32)],
        compiler_params=_params(("arbitrary",)),
    )(z, z, zsm, wg, bg, go, states, do, *([z] * (2 * H)))


def _row_spec(entry, tr):
    if isinstance(entry, tuple):
        arr, width, off = entry
        return arr, pl.BlockSpec((tr, width), lambda i, o=off // width: (i, o))
    return entry, pl.BlockSpec((tr, entry.shape[1]), lambda i: (i, 0))


def _stage_fwd(fn, rows, consts, outs, *, name, tr=None):
    first = rows[0][0] if isinstance(rows[0], tuple) else rows[0]
    S = first.shape[0]
    tr = tr or _pick(S, (512, 256, 128))
    arrs, specs = zip(*[_row_spec(e, tr) for e in rows])
    nr, nc = len(rows), len(consts)

    def body(*refs):
        vals = [r[...].astype(F32) for r in refs[:nr + nc]]
        res = fn(*vals)
        for o_ref, val in zip(refs[nr + nc:], res):
            o_ref[...] = val.astype(o_ref.dtype)

    cspecs = [pl.BlockSpec(c.shape, lambda i, n=c.ndim: (0,) * n) for c in consts]
    return pl.pallas_call(
        body, name=name,
        out_shape=tuple(jax.ShapeDtypeStruct((S, w), dt) for w, dt in outs), grid=(S // tr,),
        in_specs=list(specs) + cspecs,
        out_specs=tuple(pl.BlockSpec((tr, w), lambda i: (i, 0)) for w, _ in outs),
        compiler_params=_params(("parallel",)),
    )(*arrs, *consts)


def _stage_bwd(fn, rows, consts, cts, n_diff, drow_dtypes, *, name, tr=None):
    first = rows[0][0] if isinstance(rows[0], tuple) else rows[0]
    S = first.shape[0]
    tr = tr or _pick(S, (512, 256, 128))
    arrs, specs = zip(*[_row_spec(e, tr) for e in rows])
    widths = [e[1] if isinstance(e, tuple) else e.shape[1] for e in rows]
    nr, nc, nt = len(rows), len(consts), len(cts)

    def body(*refs):
        vals = [r[...].astype(F32) for r in refs[:nr + nc]]
        ct = [r[...].astype(F32) for r in refs[nr + nc:nr + nc + nt]]
        drow_refs = refs[nr + nc + nt:nr + nc + nt + n_diff]
        dconst_refs = refs[nr + nc + nt + n_diff:]
        rest_rows = vals[n_diff:nr]

        def f(diff_rows, cs):
            return tuple(fn(*diff_rows, *rest_rows, *cs))

        _, vjp = jax.vjp(f, vals[:n_diff], vals[nr:])
        drows, dcs = vjp(tuple(ct))
        for r, val in zip(drow_refs, drows):
            r[...] = val.astype(r.dtype)
        first_step = pl.program_id(0) == 0
        for r, val in zip(dconst_refs, dcs):
            @pl.when(first_step)
            def _(r=r, val=val):
                r[...] = val

            @pl.when(jnp.logical_not(first_step))
            def _(r=r, val=val):
                r[...] += val

    cspecs = [pl.BlockSpec(c.shape, lambda i, n=c.ndim: (0,) * n) for c in consts]
    ctspecs = [pl.BlockSpec((tr, c.shape[1]), lambda i: (i, 0)) for c in cts]
    out_shape = [jax.ShapeDtypeStruct((S, widths[j]), drow_dtypes[j]) for j in range(n_diff)]
    out_shape += [jax.ShapeDtypeStruct(c.shape, F32) for c in consts]
    out_specs = [pl.BlockSpec((tr, widths[j]), lambda i: (i, 0)) for j in range(n_diff)] + cspecs
    res = pl.pallas_call(
        body, name=name, out_shape=tuple(out_shape), grid=(S // tr,),
        in_specs=list(specs) + cspecs + ctspecs, out_specs=tuple(out_specs),
        compiler_params=_params(("arbitrary",)),
    )(*arrs, *consts, *cts)
    return list(res[:n_diff]), list(res[n_diff:])


def _mla_prep_fn(cq, ckv, kr, kr_sw, cos, sin, gq, gkv, wq_n, wq_r, wq_sw, wk, wv):
    hq = _rms(cq, gq)
    hkv = _rms(ckv, gkv)
    return (bdot(hq, wq_n), bdot(hq, wq_r) * cos + bdot(hq, wq_sw) * sin,
            bdot(hkv, wk), bdot(hkv, wv), kr * cos + kr_sw * sin)


def _merge_fn(g0, g1, g2, of, og, om, b0, b1, b2, wf, wg, wm):
    return (_sigmoid(g0 + b0) * bdot(of, wf) + _sigmoid(g1 + b1) * bdot(og, wg) + _sigmoid(g2 + b2) * bdot(om, wm),)


_IN_SIZES = (256, 256, 256, 4, 256, 256, 512, 16, 512, 256, 128, 32, 3072)
_IN_OFF = np.concatenate([[0], np.cumsum(_IN_SIZES)])
(_O_FQ, _O_FK, _O_FV, _O_FF, _O_GQ, _O_GK, _O_GV, _O_GLOW, _O_GR, _O_MQ, _O_MKV, _O_MKR, _O_ZG) = [int(o) for o in _IN_OFF[:-1]]
N_IN = int(_IN_OFF[-1])
_BIG_GROUPS = ((_O_ZG, 3072), (_O_GV, 512), (_O_GR, 512), (_O_FQ, 256), (_O_FK, 256), (_O_FV, 256),
               (_O_GQ, 256), (_O_GK, 256), (_O_MQ, 256), (_O_MKV, 128))
Z_GATE, Z_GV, Z_GR, Z_FQ, Z_FK, Z_FV, Z_GQ, Z_GK, Z_MQ, Z_MKV = [int(o) for o in
                                                                    np.concatenate([[0], np.cumsum([w for _, w in _BIG_GROUPS])])[:-1]]
N_BIG = sum(w for _, w in _BIG_GROUPS)
SM_FF, SM_GLOW, SM_KR, SM_KR_SW, N_SM = 0, 8, 32, 64, 128
N_PAD = N_BIG + N_SM
_HALF = MLA_ROPE // 2
_QK_HD = MLA_NOPE + MLA_ROPE


def _in_perm():
    idx = np.concatenate([np.arange(o, o + w) for o, w in _BIG_GROUPS] + [np.zeros(N_SM, np.int64)])
    sign = np.concatenate([np.ones(N_BIG), np.zeros(N_SM)])
    for src, dst, w in ((_O_FF, SM_FF, 4), (_O_GLOW, SM_GLOW, 16), (_O_MKR, SM_KR, 32)):
        idx[N_BIG + dst:N_BIG + dst + w] = np.arange(src, src + w)
        sign[N_BIG + dst:N_BIG + dst + w] = 1.0
    inv = np.zeros(N_IN, np.int64)
    inv[idx[sign > 0]] = np.nonzero(sign > 0)[0]
    sw = N_BIG + SM_KR_SW
    idx[sw:sw + _HALF] = np.arange(_O_MKR + _HALF, _O_MKR + MLA_ROPE)
    sign[sw:sw + _HALF] = -1.0
    idx[sw + _HALF:sw + MLA_ROPE] = np.arange(_O_MKR, _O_MKR + _HALF)
    sign[sw + _HALF:sw + MLA_ROPE] = 1.0
    inv2, sign2 = np.zeros(N_IN, np.int64), np.zeros(N_IN)
    inv2[idx[sw:sw + MLA_ROPE]] = np.arange(sw, sw + MLA_ROPE)
    sign2[idx[sw:sw + MLA_ROPE]] = sign[sw:sw + MLA_ROPE]
    return idx, sign.astype(np.float32), inv, inv2, sign2.astype(np.float32)


_IN_IDX, _IN_SIGN, _IN_INV, _IN_INV2, _IN_SIGN2 = _in_perm()


def _uq_perm():
    base = [h * _QK_HD for h in range(MLA_HEADS)]
    nope = np.concatenate([np.arange(b, b + MLA_NOPE) for b in base])
    rot = np.concatenate([np.arange(b + MLA_NOPE, b + _QK_HD) for b in base])
    sw = np.concatenate([np.concatenate([np.arange(b + MLA_NOPE + _HALF, b + _QK_HD), np.arange(b + MLA_NOPE, b + MLA_NOPE + _HALF)])
                         for b in base])
    sw_sign = np.tile(np.concatenate([-np.ones(_HALF), np.ones(_HALF)]), MLA_HEADS).astype(np.float32)
    return nope, rot, sw, sw_sign


_UQ_NOPE, _UQ_ROT, _UQ_SW, _UQ_SW_SIGN = _uq_perm()
_UKV_PERM = np.concatenate(
    [np.concatenate([np.arange(h * 128, h * 128 + MLA_NOPE) for h in range(MLA_HEADS)]),
     np.concatenate([np.arange(h * 128 + MLA_NOPE, (h + 1) * 128) for h in range(MLA_HEADS)])])
_UKV_INV = np.argsort(_UKV_PERM)


def _rope_tables(S):
    inv = ROPE_BASE ** (-jnp.arange(_HALF, dtype=F32) / _HALF)
    ang = jnp.arange(S, dtype=F32)[:, None] * inv[None, :]
    return jnp.tile(jnp.cos(ang), (1, 2 * MLA_HEADS)), jnp.tile(jnp.sin(ang), (1, 2 * MLA_HEADS))


class _LayerParams:
    def __init__(self, rep, l):
        self.w, self.rep, self.l, self.made = {}, rep, l, {}

    def __getitem__(self, k):
        if k not in self.made:
            self.made[k] = self._make(k)
        return self.made[k]

    def _make(self, k):
        w, rep, l = self.w, self.rep, self.l
        if k == 'wg':
            return jnp.zeros((N_SM, GLA_W), BF16).at[SM_GLOW:SM_GLOW + GLA_RANK].set(w['w_gla_gate'])
        if k in ('wq_n', 'wq_r'):
            return w['w_mla_uq'][:, _UQ_NOPE if k == 'wq_n' else _UQ_ROT]
        if k == 'wq_sw':
            return w['w_mla_uq'][:, _UQ_SW] * _UQ_SW_SIGN.astype(BF16)
        if k in ('wk', 'wv'):
            return w['w_mla_ukv'][:, _UKV_PERM[:256] if k == 'wk' else _UKV_PERM[256:]]
        if k == 'b_f':
            return jnp.zeros((8, 1), F32).at[:FOX_HEADS, 0].set(rep['b_fox_forget'][l])
        if k == 'b_gate':
            return [rep['b_branch_gate'][l][i * 1024:(i + 1) * 1024].reshape(1, 1024) for i in range(3)]
        vec = {'bg': 'b_gla_gate', 'go': 'g_gla_out', 'gq': 'g_mla_q', 'gkv': 'g_mla_kv'}
        if k in vec:
            return rep[vec[k]][l].reshape(1, -1)
        return rep[k][l] if k in rep else w[k]


_GLA_COLS = (Z_GQ, Z_GK, Z_GV, Z_GR)
_MLA_OUTS = [(256, BF16), (128, BF16), (256, BF16), (256, BF16), (128, BF16)]


def _mla_rows(z, zsm, rope):
    per_head = lambda off: jnp.tile(zsm[:, off:off + MLA_ROPE], (1, MLA_HEADS))
    return [(z, 256, Z_MQ), (z, 128, Z_MKV), per_head(SM_KR), per_head(SM_KR_SW), *rope]


def _mla_consts(p):
    return [p['gq'], p['gkv'], p['wq_n'], p['wq_r'], p['wq_sw'], p['wk'], p['wv']]


def _fox_qkv(z):
    return [((z, Z_FQ, 256), (z, Z_FK, 256), FOX_HD, False)], (z, Z_FV, 256)


def _mla_qkv(qn, qr, kn, vv, kr):
    return [((qn, 0, 256), (kn, 0, 256), MLA_NOPE, False), ((qr, 0, 128), (kr, 0, 128), MLA_ROPE, True)], (vv, 0, 256)


def _xa_qkv(qx, kvx):
    return [((qx, 0, 512), (kvx, 0, 512), XA_HD, False)], (kvx, 512, 512)


def _merge_rows(z, o_fox, o_gla, o_mla):
    return [(z, 1024, Z_GATE), (z, 1024, Z_GATE + 1024), (z, 1024, Z_GATE + 2048), o_fox, o_gla, o_mla]


def _merge_consts(p):
    return p['b_gate'] + [p['w_up_fox'], p['w_up_gla'], p['w_up_mla']]


def _carried(hooks, key, call):
    rider, sink = hooks.pop(key, (None, None))
    res = call(rider=rider)
    if rider is None:
        return res
    sink(res[-1])
    return res[:-1]


def _layer_fwd(x0, mem, p, rope, l, hooks):
    S = x0.shape[0]
    sv = {'x0': x0}
    h1 = _rms_fwd(x0, p['g_mix'], name=f"rms_mix_{l}")
    z = _mm(h1, p['w_in'], mode='nn', out_dtype=BF16, b_cols=(0, N_BIG), name=f"in_big_{l}")
    zsm = _mm(h1, p['w_in'], mode='nn', out_dtype=F32, b_cols=(N_BIG, N_SM), name=f"in_small_{l}")
    sv.update(h1=h1, z=z, zsm=zsm)
    ff_t = jnp.zeros((8, S), F32).at[:FOX_HEADS].set(zsm[:, SM_FF:SM_FF + FOX_HEADS].T)
    cum_t = _fox_cum_fwd(ff_t, p['b_f'], name=f"fox_cum_{l}")
    cum = cum_t.T
    o_fox, lse_f = _carried(hooks, (l, 'fox_fwd'), lambda rider: _attn_fwd(
        *_fox_qkv(z), FOX_HEADS, cum, cum_t, scale=FOX_HD ** -0.5, mask='causal', name=f"fox_fwd_{l}", rider=rider))
    sv.update(ff_t=ff_t, cum=cum, cum_t=cum_t, lse_f=lse_f, o_fox=o_fox)
    o_gla, states = _gla_fwd(z, zsm, p['wg'], p['bg'], p['go'], _GLA_COLS, name=f"gla_fwd_{l}")
    sv.update(o_gla=o_gla, states=states)
    mla = _stage_fwd(_mla_prep_fn, _mla_rows(z, zsm, rope), _mla_consts(p), _MLA_OUTS, name=f"mla_prep_{l}")
    o_mla, lse_m = _carried(hooks, (l, 'mla_fwd'), lambda rider: _attn_fwd(
        *_mla_qkv(*mla), MLA_HEADS, None, None, scale=_QK_HD ** -0.5, mask='chunk', name=f"mla_fwd_{l}", rider=rider))
    sv.update(mla=mla, lse_m=lse_m, o_mla=o_mla)
    (y,) = _stage_fwd(_merge_fn, _merge_rows(z, o_fox, o_gla, o_mla), _merge_consts(p), [(1024, BF16)], name=f"merge_{l}")
    x1 = _mm(y, p['w_out'], mode='nn', out_dtype=F32, residual=x0, name=f"out_proj_{l}")
    sv.update(y=y, x1=x1)
    h2 = _rms_fwd(x1, p['g_xa'], name=f"rms_xa_{l}")
    hm = _rms_fwd(mem, p['g_mem'], name=f"rms_mem_{l}")
    qx = _mm(h2, p['w_xq'], mode='nn', out_dtype=BF16, name=f"xq_{l}")
    kvx = _mm(hm, p['w_xkv'], mode='nn', out_dtype=BF16, name=f"xkv_{l}")
    ox, lse_x = _attn_fwd(*_xa_qkv(qx, kvx), XA_HEADS, None, None, scale=XA_HD ** -0.5, mask=None, name=f"xa_fwd_{l}")
    x2 = _mm(ox, p['w_xo'], mode='nn', out_dtype=F32, residual=x1, name=f"xo_{l}")
    sv.update(h2=h2, hm=hm, qx=qx, kvx=kvx, lse_x=lse_x, ox=ox, x2=x2)
    h3 = _rms_fwd(x2, p['g_mlp'], name=f"rms_mlp_{l}")
    a = _mm(h3, p['w_mlp1'], mode='nn', out_dtype=BF16, name=f"mlp1_{l}")
    x3 = _mm(a, p['w_mlp2'], mode='nn', out_dtype=F32, act='relu2', residual=x2, name=f"mlp2_{l}")
    sv.update(h3=h3, a=a)
    return x3, sv


def _layer_bwd(dx3, dx3b, mem, p, rope, sv, l, hooks, half_done):
    S = dx3.shape[0]
    g = {}
    da = _mm(dx3b, p['w_mlp2'], mode='nt', out_dtype=BF16, drelu_of=sv['a'], name=f"d_mlp2_in_{l}")
    g['w_mlp2'] = _mm(sv['a'], dx3b, mode='tn', out_dtype=BF16, act='relu2', name=f"d_w_mlp2_{l}")
    dh3 = _mm(da, p['w_mlp1'], mode='nt', out_dtype=F32, name=f"d_mlp1_in_{l}")
    g['w_mlp1'] = _mm(sv['h3'], da, mode='tn', out_dtype=BF16, name=f"d_w_mlp1_{l}")
    dx2, dx2b, g['g_mlp'] = _rms_bwd(sv['x2'], p['g_mlp'], dh3, dx3, name=f"d_rms_mlp_{l}")
    dox = _mm(dx2b, p['w_xo'], mode='nt', out_dtype=BF16, name=f"d_xo_in_{l}")
    g['w_xo'] = _mm(sv['ox'], dx2b, mode='tn', out_dtype=BF16, name=f"d_w_xo_{l}")
    (dqx,), (dkx,), dvx = _attn_bwd(*_xa_qkv(sv['qx'], sv['kvx']), XA_HEADS, sv['ox'], dox, sv['lse_x'], None, None,
                                    scale=XA_HD ** -0.5, mask=None, name=f"xa_bwd_{l}")
    dqx = dqx.astype(BF16)
    dkvx = jnp.concatenate([dkx, dvx], axis=1).astype(BF16)
    dh2 = _mm(dqx, p['w_xq'], mode='nt', out_dtype=F32, name=f"d_xq_in_{l}")
    g['w_xq'] = _mm(sv['h2'], dqx, mode='tn', out_dtype=BF16, name=f"d_w_xq_{l}")
    dhm = _mm(dkvx, p['w_xkv'], mode='nt', out_dtype=F32, name=f"d_xkv_in_{l}")
    g['w_xkv'] = _mm(sv['hm'], dkvx, mode='tn', out_dtype=BF16, name=f"d_w_xkv_{l}")
    _, _, g['g_mem'] = _rms_bwd(mem, p['g_mem'], dhm, None, name=f"d_rms_mem_{l}")
    dx1, dx1b, g['g_xa'] = _rms_bwd(sv['x1'], p['g_xa'], dh2, dx2, name=f"d_rms_xa_{l}")
    dy = _mm(dx1b, p['w_out'], mode='nt', out_dtype=F32, name=f"d_out_in_{l}")
    g['w_out'] = _mm(sv['y'], dx1b, mode='tn', out_dtype=BF16, name=f"d_w_out_{l}")
    z, zsm = sv['z'], sv['zsm']
    (dg0, dg1, dg2, do_fox, do_gla, do_mla), (db0, db1, db2, g['w_up_fox'], g['w_up_gla'], g['w_up_mla']) = _stage_bwd(
        _merge_fn, _merge_rows(z, sv['o_fox'], sv['o_gla'], sv['o_mla']), _merge_consts(p), [dy], 6, [BF16] * 6,
        name=f"merge_bwd_{l}")
    g['b_branch_gate'] = jnp.concatenate([db0, db1, db2], axis=1).reshape(-1)
    half_done(l, g)
    (dfq,), (dfk,), dfv, dck, dcq = _carried(hooks, (l, 'fox_bwd'), lambda rider: _attn_bwd(
        *_fox_qkv(z), FOX_HEADS, sv['o_fox'], do_fox, sv['lse_f'], sv['cum'], sv['cum_t'],
        scale=FOX_HD ** -0.5, mask='causal', name=f"fox_bwd_{l}", rider=rider))
    dff_t, db_f = _fox_cum_bwd(sv['ff_t'], p['b_f'], dck + dcq.T, name=f"fox_cum_bwd_{l}")
    g['b_fox_forget'] = db_f[:FOX_HEADS, 0]
    dgq, dgk, dgv, dgr, dzsm, dwg, dbg, dgo = _gla_bwd(z, zsm, p['wg'], p['bg'], p['go'], sv['states'], do_gla, _GLA_COLS,
                                                       name=f"gla_bwd_{l}")
    g['w_gla_gate'] = dwg[SM_GLOW:SM_GLOW + GLA_RANK]
    g['b_gla_gate'] = dbg.reshape(-1)
    g['g_gla_out'] = dgo.reshape(-1)
    (dmqn, dmqr), (dmkn, dmkr), dmv = _carried(hooks, (l, 'mla_bwd'), lambda rider: _attn_bwd(
        *_mla_qkv(*sv['mla']), MLA_HEADS, sv['o_mla'], do_mla, sv['lse_m'], None, None,
        scale=_QK_HD ** -0.5, mask='chunk', name=f"mla_bwd_{l}", rider=rider))
    (dcq, dckv, dkr, dkr_sw), (dgq_n, dgkv_n, dwq_n, dwq_r, dwq_sw, dwk, dwv) = _stage_bwd(
        _mla_prep_fn, _mla_rows(z, zsm, rope), _mla_consts(p), [dmqn, dmqr, dmkn, dmv, dmkr], 4, [BF16, BF16, F32, F32],
        name=f"mla_prep_bwd_{l}")
    dkr, dkr_sw = (jnp.sum(d.reshape(S, MLA_HEADS, MLA_ROPE), axis=1) for d in (dkr, dkr_sw))
    g['g_mla_q'] = dgq_n.reshape(-1)
    g['g_mla_kv'] = dgkv_n.reshape(-1)
    g['w_mla_uq'] = (jnp.zeros((MLA_Q_RANK, MLA_HEADS * _QK_HD), F32).at[:, _UQ_NOPE].set(dwq_n).at[:, _UQ_ROT].set(dwq_r)
                     .at[:, _UQ_SW].add(dwq_sw * _UQ_SW_SIGN))
    g['w_mla_ukv'] = jnp.concatenate([dwk, dwv], axis=1)[:, _UKV_INV]
    dz = jnp.concatenate([dg0, dg1, dg2, dgv, dgr, dfq.astype(BF16), dfk.astype(BF16), dfv.astype(BF16), dgq, dgk, dcq, dckv,
                          (dzsm + jnp.concatenate([dff_t[:FOX_HEADS].T, jnp.zeros((S, SM_KR - FOX_HEADS), F32), dkr, dkr_sw,
                                                   jnp.zeros((S, N_SM - SM_KR_SW - MLA_ROPE), F32)], axis=1)).astype(BF16)],
                         axis=1)
    dh1 = _mm(dz, p['w_in'], mode='nt', out_dtype=F32, tk=N_PAD // 2, name=f"d_in_{l}")
    g['w_in'] = _mm(sv['h1'], dz, mode='tn', out_dtype=BF16, tm=512, tn=N_PAD // 2, tk=512, name=f"d_w_in_{l}")
    dx0, dx0b, g['g_mix'] = _rms_bwd(sv['x0'], p['g_mix'], dh1, dx1, name=f"d_rms_mix_{l}")
    for n in ('g_mlp', 'g_mem', 'g_xa', 'g_mix'):
        g[n] = g[n].reshape(-1)
    return dx0, dx0b, g


def _local_step(x, mem, target, ps, g_final, hooks, half_done, layer_done):
    rope = _rope_tables(x.shape[0])
    saved = []
    for l, p in enumerate(ps):
        x, sv = _layer_fwd(x, mem, p, rope, l, hooks)
        saved.append(sv)
    loss, dx, dxb, dgf = _loss_head(x, g_final, target, name="loss_head")
    for l in reversed(range(len(ps))):
        dx, dxb, grads = _layer_bwd(dx, dxb, mem, ps[l], rope, saved[l], l, hooks, half_done)
        layer_done(l, grads)
    assert not hooks, f"exchanges without a carrier: {list(hooks)}"
    return loss, dx, dgf.reshape(-1)


_MESH_AXES = ("x", "y", "c")
_HBM = pl.BlockSpec(memory_space=pl.ANY)


N_CHIP = 4


def _place():
    x, y, c = (lax.axis_index(n) for n in _MESH_AXES)
    return (x, y, c), (x, y, 1 - c), [(1 - x, y), (x, 1 - y), (1 - x, 1 - y)]


def _remote(src, dst, sems, k, to):
    return pltpu.make_async_remote_copy(src_ref=src, dst_ref=dst, send_sem=sems[0].at[k], recv_sem=sems[1].at[k],
                                        device_id=to, device_id_type=pl.DeviceIdType.MESH)


def _all_gather(x, *, name):
    def body(x_ref, o_ref, send_sems, recv_sems, local_sem):
        me, sib, chips = _place()
        c = me[2]
        sems = (send_sems, recv_sems)
        slot = lambda px, py, pc: o_ref.at[4 * px + 2 * py + pc]
        mine = pltpu.make_async_copy(x_ref, slot(*me), local_sem)
        mine.start()
        first = [_remote(x_ref, slot(*me), sems, 0, sib)]
        first += [_remote(x_ref, slot(*me), sems, 1 + j, (*chip, c)) for j, chip in enumerate(chips)]
        for cp in first:
            cp.start()
        passed = [_remote(slot(*chip, c), slot(*chip, c), sems, 4 + j, sib) for j, chip in enumerate(chips)]
        for j, chip in enumerate(chips):
            _remote(x_ref, slot(*chip, c), sems, 1 + j, me).wait_recv()
            passed[j].start()
        _remote(x_ref, slot(*sib), sems, 0, me).wait_recv()
        for j, chip in enumerate(chips):
            _remote(x_ref, slot(*chip, 1 - c), sems, 4 + j, me).wait_recv()
        for cp in first + passed:
            cp.wait_send()
        mine.wait()

    return pl.pallas_call(
        body, name=name, out_shape=jax.ShapeDtypeStruct((N_DEV,) + x.shape, x.dtype),
        in_specs=[_HBM], out_specs=_HBM,
        scratch_shapes=[pltpu.SemaphoreType.DMA((N_DEV - 1,)), pltpu.SemaphoreType.DMA((N_DEV - 1,)), pltpu.SemaphoreType.DMA],
        compiler_params=pltpu.CompilerParams(has_side_effects=True),
    )(x)


class _Rider:
    def __init__(self, inputs, out_shapes, scratch, start, finish, post):
        self.inputs, self.out_shapes, self.scratch = list(inputs), list(out_shapes), list(scratch)
        self.start, self.finish, self.post = start, finish, post


def _run_rider(rider, *, name):
    def body(*refs):
        rider.start(refs)
        rider.finish(refs)

    outs = pl.pallas_call(
        body, name=name, out_shape=tuple(rider.out_shapes), in_specs=[_HBM] * len(rider.inputs),
        out_specs=(_HBM,) * len(rider.out_shapes), scratch_shapes=rider.scratch,
        compiler_params=pltpu.CompilerParams(has_side_effects=True),
    )(*rider.inputs)
    return rider.post(outs)


def _carry(rider, n_in, n_out, first, last):
    if rider is None:
        return [], [], [], [], [], lambda refs: refs
    ni, no = len(rider.inputs), len(rider.out_shapes)

    def split(refs):
        own_in, r_in = refs[:n_in], refs[n_in:n_in + ni]
        own_out, r_out = refs[n_in + ni:n_in + ni + n_out], refs[n_in + ni + n_out:n_in + ni + n_out + no]
        rest = refs[n_in + ni + n_out + no:]
        own_scr, r_scr = rest[:len(rest) - len(rider.scratch)], rest[len(rest) - len(rider.scratch):]
        rrefs = tuple(r_in) + tuple(r_out) + tuple(r_scr)
        pl.when(first())(lambda: rider.start(rrefs))
        pl.when(last())(lambda: rider.finish(rrefs))
        return tuple(own_in) + tuple(own_out) + tuple(own_scr)

    return list(rider.inputs), [_HBM] * ni, list(rider.out_shapes), [_HBM] * no, list(rider.scratch), split


def _gather_rider(shards, axes):
    n = len(shards)
    srcs, out_shapes, kinds = [], [], []
    for s, ax in zip(shards, axes):
        L, a, b = s.shape
        if ax == 1:
            srcs.append(s.reshape(L, 1, a, b)), out_shapes.append((L, N_DEV, a, b)), kinds.append('row')
        elif b % 128 == 0:
            srcs.append(s), out_shapes.append((L, a, N_DEV * b)), kinds.append('col')
        else:
            srcs.append(s.reshape(1, L, a, b)), out_shapes.append((N_DEV, L, a, b)), kinds.append('slot')

    def parts(refs):
        x_refs, o_refs = refs[:n], refs[n:2 * n]
        send_sems, recv_sems, local_sem = refs[2 * n:]
        me, sib, chips = _place()
        sems = (send_sems, recv_sems)

        def win(t, px, py, pc):
            idx = 4 * px + 2 * py + pc
            if kinds[t] == 'row':
                return o_refs[t].at[:, pl.ds(idx, 1)]
            if kinds[t] == 'col':
                width = shards[t].shape[2]
                return o_refs[t].at[:, :, pl.ds(pl.multiple_of(idx * width, 128), width)]
            return o_refs[t].at[pl.ds(idx, 1)]

        def group(k, block, to, own):
            return [_remote(x_refs[t] if own else win(t, *block), win(t, *block), sems, k * n + t, to) for t in range(n)]

        mine = [pltpu.make_async_copy(x_refs[t], win(t, *me), local_sem.at[t]) for t in range(n)]
        first = group(0, me, sib, True)
        for j, chip in enumerate(chips):
            first += group(1 + j, me, (*chip, me[2]), True)
        return me, sib, chips, group, mine, first

    def start(refs):
        *_, mine, first = parts(refs)
        for cp in mine + first:
            cp.start()

    def finish(refs):
        me, sib, chips, group, mine, first = parts(refs)
        c = me[2]
        passed = []
        for j, chip in enumerate(chips):
            for cp in group(1 + j, (*chip, c), me, False):
                cp.wait_recv()
            fwd = group(4 + j, (*chip, c), sib, False)
            for cp in fwd:
                cp.start()
            passed += fwd
        for cp in group(0, sib, me, False):
            cp.wait_recv()
        for j, chip in enumerate(chips):
            for cp in group(4 + j, (*chip, 1 - c), me, False):
                cp.wait_recv()
        for cp in first + passed:
            cp.wait_send()
        for cp in mine:
            cp.wait()

    def post(outs):
        whole = []
        for o, s, kind in zip(outs, shards, kinds):
            L, a, b = s.shape
            whole.append(o.reshape(L, N_DEV * a, b) if kind == 'row' else o if kind == 'col' else _to_whole(o, 2))
        return whole

    return _Rider(srcs, [jax.ShapeDtypeStruct(shp, s.dtype) for shp, s in zip(out_shapes, shards)],
                  [pltpu.SemaphoreType.DMA(((N_DEV - 1) * n,)), pltpu.SemaphoreType.DMA(((N_DEV - 1) * n,)),
                   pltpu.SemaphoreType.DMA((n,))], start, finish, post)


def _sibling_swap(x, *, name):
    def body(x_ref, o_ref, send_sems, recv_sems):
        me, sib, _ = _place()
        c = me[2]
        sems = (send_sems, recv_sems)
        sends = [_remote(x_ref.at[j, 1 - c], o_ref.at[j], sems, j, sib) for j in range(N_CHIP)]
        for cp in sends:
            cp.start()
        for cp in sends:
            cp.wait_send()
            cp.wait_recv()

    return pl.pallas_call(
        body, name=name, out_shape=jax.ShapeDtypeStruct((N_CHIP,) + x.shape[2:], x.dtype),
        in_specs=[_HBM], out_specs=_HBM,
        scratch_shapes=[pltpu.SemaphoreType.DMA((N_CHIP,)), pltpu.SemaphoreType.DMA((N_CHIP,))],
        compiler_params=pltpu.CompilerParams(has_side_effects=True),
    )(x)


def _pair_sum(x, got, c, *, name):
    _, _, R, _ = x.shape
    tr = _pick(R, (1024, 512, 256, 128, 64, 32, 16, 8))

    def body(c_ref, x_ref, g_ref, o_ref):
        o_ref[...] = (x_ref[...].astype(F32) + g_ref[...].astype(F32)).astype(o_ref.dtype)

    return pl.pallas_call(
        body, name=name, out_shape=jax.ShapeDtypeStruct((N_CHIP, R, 128), x.dtype),
        grid_spec=pltpu.PrefetchScalarGridSpec(
            num_scalar_prefetch=1, grid=(N_CHIP, R // tr),
            in_specs=[pl.BlockSpec((None, None, tr, 128), lambda j, i, c_ref: (j, c_ref[0], i, 0)),
                      pl.BlockSpec((None, tr, 128), lambda j, i, c_ref: (j, i, 0))],
            out_specs=pl.BlockSpec((None, tr, 128), lambda j, i, c_ref: (j, i, 0))),
        compiler_params=_params(("parallel", "parallel")),
    )(c, x, got)


def _chip_all_to_all_rider(x):
    def parts(refs):
        x_ref, o_ref, send_sems, recv_sems, local_sem = refs
        me, _, chips = _place()
        sems = (send_sems, recv_sems)
        mine = 2 * me[0] + me[1]
        local = pltpu.make_async_copy(x_ref.at[mine], o_ref.at[mine], local_sem)
        sends = [_remote(x_ref.at[2 * px + py], o_ref.at[mine], sems, j, (px, py, me[2])) for j, (px, py) in enumerate(chips)]
        arrival = lambda j: _remote(x_ref.at[mine], o_ref.at[2 * chips[j][0] + chips[j][1]], sems, j, me)
        return local, sends, arrival

    def start(refs):
        local, sends, _ = parts(refs)
        for cp in [local] + sends:
            cp.start()

    def finish(refs):
        local, sends, arrival = parts(refs)
        for j, cp in enumerate(sends):
            cp.wait_send()
            arrival(j).wait_recv()
        local.wait()

    return _Rider([x], [jax.ShapeDtypeStruct(x.shape, x.dtype)],
                  [pltpu.SemaphoreType.DMA((N_CHIP - 1,)), pltpu.SemaphoreType.DMA((N_CHIP - 1,)), pltpu.SemaphoreType.DMA],
                  start, finish, lambda outs: outs[0])


def _sum_slots(x, *, name):
    n, R, _ = x.shape
    tr = _pick(R, (1024, 512, 256, 128, 64, 32, 16, 8))

    def body(x_ref, o_ref):
        acc = x_ref[0].astype(F32)
        for j in range(1, n):
            acc = acc + x_ref[j].astype(F32)
        o_ref[...] = acc

    return pl.pallas_call(
        body, name=name, out_shape=jax.ShapeDtypeStruct((R, 128), F32), grid=(R // tr,),
        in_specs=[pl.BlockSpec((n, tr, 128), lambda i: (0, i, 0))], out_specs=pl.BlockSpec((tr, 128), lambda i: (i, 0)),
        compiler_params=_params(("parallel",)),
    )(x)


def _adamw(w, g, m, v, *, name):
    shape = w.shape
    cols = shape[-1]
    rows = int(np.prod(shape[:-1]))
    tr = next((t for t in (1024, 512, 256, 128, 64, 32, 16, 8) if rows % t == 0 and t * cols * 4 <= (1 << 20)), rows)

    def body(w_ref, g_ref, m_ref, v_ref, d_ref, mo_ref, vo_ref):
        g_ = g_ref[...]
        m_ = ADAM_B1 * m_ref[...] + (1.0 - ADAM_B1) * g_
        v_ = ADAM_B2 * v_ref[...] + (1.0 - ADAM_B2) * jnp.square(g_)
        m_hat = m_ / (1.0 - ADAM_B1 ** ADAM_STEP)
        v_hat = v_ / (1.0 - ADAM_B2 ** ADAM_STEP)
        d_ref[...] = -ADAM_LR * (m_hat / (jnp.sqrt(v_hat) + ADAM_EPS) + ADAM_WD * w_ref[...])
        mo_ref[...] = m_
        vo_ref[...] = v_

    blk = pl.BlockSpec((tr, cols), lambda i: (i, 0))
    outs = pl.pallas_call(
        body, name=name, out_shape=tuple(jax.ShapeDtypeStruct((rows, cols), F32) for _ in range(3)), grid=(rows // tr,),
        in_specs=[blk] * 4, out_specs=(blk,) * 3, compiler_params=_params(("parallel",)),
    )(*(a.reshape(rows, cols) for a in (w, g, m, v)))
    return tuple(o.reshape(shape) for o in outs)


_WEIGHTS = ('g_mix', 'w_in', 'b_fox_forget', 'w_gla_gate', 'b_gla_gate', 'g_gla_out', 'g_mla_q', 'w_mla_uq', 'g_mla_kv',
            'w_mla_ukv', 'b_branch_gate', 'w_up_fox', 'w_up_gla', 'w_up_mla', 'w_out', 'g_xa', 'g_mem', 'w_xq', 'w_xkv',
            'w_xo', 'g_mlp', 'w_mlp1', 'w_mlp2', 'g_final')
_SHARDED = (('w_in', 1), ('w_gla_gate', 2), ('w_mla_uq', 2), ('w_mla_ukv', 2), ('w_up_fox', 2), ('w_up_gla', 2),
            ('w_up_mla', 2), ('w_out', 1), ('w_xq', 1), ('w_xkv', 1), ('w_xo', 2), ('w_mlp1', 2), ('w_mlp2', 1))
_REPLICATED = tuple(n for n in _WEIGHTS if n not in dict(_SHARDED))
_ROW_PAD = 1024
_SMALL_ROW_PAD = 8
_PIECE_ROWS = 16


def _pack(flats, lead, row_pad=_ROW_PAD):
    if all(int(np.prod(a.shape[lead:])) % 128 == 0 for a in flats):
        def block(a):
            a = a.reshape(a.shape[:lead] + (-1, 128))
            return jnp.pad(a, [(0, 0)] * lead + [(0, -a.shape[lead] % _PIECE_ROWS), (0, 0)])
        cat = jnp.concatenate([block(a) for a in flats], axis=lead)
        rows = cat.shape[lead]
        return jnp.pad(cat, [(0, 0)] * lead + [(0, -(-rows // row_pad) * row_pad - rows), (0, 0)])
    cat = jnp.concatenate([a.reshape(a.shape[:lead] + (-1,)) for a in flats], axis=-1)
    n = cat.shape[-1]
    total = -(-n // (128 * row_pad)) * (128 * row_pad)
    cat = jnp.pad(cat, [(0, 0)] * lead + [(0, total - n)])
    return cat.reshape(cat.shape[:lead] + (total // 128, 128))


def _unpack(buf, shapes, lead):
    sizes = [int(np.prod(shp)) for shp in shapes]
    out, off = [], 0
    if all(n % 128 == 0 for n in sizes):
        for shp, n in zip(shapes, sizes):
            rows = buf[(slice(None),) * lead + (slice(off, off + n // 128),)]
            out.append(rows.reshape(buf.shape[:lead] + tuple(shp)))
            off += -(-(n // 128) // _PIECE_ROWS) * _PIECE_ROWS
        return out
    flat = buf.reshape(buf.shape[:lead] + (-1,))
    for shp, n in zip(shapes, sizes):
        out.append(flat[..., off:off + n].reshape(buf.shape[:lead] + tuple(shp)))
        off += n
    return out


def _to_whole(g, axis):
    if axis == 1:
        return g.transpose(1, 0, 2, 3).reshape(g.shape[1], N_DEV * g.shape[2], g.shape[3])
    return g.transpose(1, 2, 0, 3).reshape(g.shape[1], g.shape[2], N_DEV * g.shape[3])


def _to_shards(w, axis):
    L, R, C = w.shape
    if axis == 1:
        return w.reshape(L, N_DEV, R // N_DEV, C).transpose(1, 0, 2, 3)
    return w.reshape(L, R, N_DEV, C // N_DEV).transpose(2, 0, 1, 3)


def kernel(x, mem, g_mix, w_in, b_fox_forget, w_gla_gate, b_gla_gate, g_gla_out, g_mla_q, w_mla_uq, g_mla_kv, w_mla_ukv, b_branch_gate, w_up_fox, w_up_gla, w_up_mla, w_out, g_xa, g_mem, w_xq, w_xkv, w_xo, g_mlp, w_mlp1, w_mlp2, g_final, loss_target, m_g_mix, m_w_in, m_b_fox_forget, m_w_gla_gate, m_b_gla_gate, m_g_gla_out, m_g_mla_q, m_w_mla_uq, m_g_mla_kv, m_w_mla_ukv, m_b_branch_gate, m_w_up_fox, m_w_up_gla, m_w_up_mla, m_w_out, m_g_xa, m_g_mem, m_w_xq, m_w_xkv, m_w_xo, m_g_mlp, m_w_mlp1, m_w_mlp2, m_g_final, v_g_mix, v_w_in, v_b_fox_forget, v_w_gla_gate, v_b_gla_gate, v_g_gla_out, v_g_mla_q, v_w_mla_uq, v_g_mla_kv, v_w_mla_ukv, v_b_branch_gate, v_w_up_fox, v_w_up_gla, v_w_up_mla, v_w_out, v_g_xa, v_g_mem, v_w_xq, v_w_xkv, v_w_xo, v_g_mlp, v_w_mlp1, v_w_mlp2, v_g_final):
    wts = dict(zip(_WEIGHTS, (g_mix, w_in, b_fox_forget, w_gla_gate, b_gla_gate, g_gla_out, g_mla_q, w_mla_uq, g_mla_kv,
                              w_mla_ukv, b_branch_gate, w_up_fox, w_up_gla, w_up_mla, w_out, g_xa, g_mem, w_xq, w_xkv, w_xo,
                              g_mlp, w_mlp1, w_mlp2, g_final)))
    mom1 = dict(zip(_WEIGHTS, (m_g_mix, m_w_in, m_b_fox_forget, m_w_gla_gate, m_b_gla_gate, m_g_gla_out, m_g_mla_q,
                               m_w_mla_uq, m_g_mla_kv, m_w_mla_ukv, m_b_branch_gate, m_w_up_fox, m_w_up_gla, m_w_up_mla,
                               m_w_out, m_g_xa, m_g_mem, m_w_xq, m_w_xkv, m_w_xo, m_g_mlp, m_w_mlp1, m_w_mlp2, m_g_final)))
    mom2 = dict(zip(_WEIGHTS, (v_g_mix, v_w_in, v_b_fox_forget, v_w_gla_gate, v_b_gla_gate, v_g_gla_out, v_g_mla_q,
                               v_w_mla_uq, v_g_mla_kv, v_w_mla_ukv, v_b_branch_gate, v_w_up_fox, v_w_up_gla, v_w_up_mla,
                               v_w_out, v_g_xa, v_g_mem, v_w_xq, v_w_xkv, v_w_xo, v_g_mlp, v_w_mlp1, v_w_mlp2, v_g_final)))
    depth = g_mix.shape[0]

    names = [n for n, _ in _SHARDED]
    axes = dict(_SHARDED)
    shard = {n: wts[n] for n in names}
    shard['w_in'] = w_in[:, :, _IN_IDX] * _IN_SIGN
    rep = {n: wts[n] for n in _REPLICATED}
    ps = [_LayerParams(rep, l) for l in range(depth)]

    def gather(group, l):
        rider = _gather_rider([shard[n][l:l + 1].astype(BF16) for n in group], [axes[n] for n in group])
        return rider, lambda whole: ps[l].w.update({n: w[0] for n, w in zip(group, whole)})

    first, sink = gather(['w_in'], 0)
    sink(_run_rider(first, name="gather_w_in_0"))
    hooks = {(0, 'fox_fwd'): gather([n for n in names if n != 'w_in'], 0)}
    for l in range(1, depth):
        hooks[(l - 1, 'mla_fwd')] = gather(names, l)

    core = lax.axis_index("c").astype(jnp.int32).reshape(1)
    late = ['w_in', 'w_gla_gate', 'w_mla_uq', 'w_mla_ukv']
    groups = {'early': [n for n in names if n not in late], 'late': late}
    small_grads, landed = {}, {}

    def exchange(l, g, which):
        slots = _pack([_to_shards(g[n][None], axes[n]).astype(BF16) for n in groups[which]], 1)
        slots = slots.reshape((N_CHIP, 2) + slots.shape[1:])
        paired = _pair_sum(slots, _sibling_swap(slots, name=f"swap_grads_{which}_{l}"), core, name=f"pair_grads_{which}_{l}")
        return _chip_all_to_all_rider(paired), lambda got: landed.update({(l, which): got})

    def half_done(l, g):
        hooks[(l, 'mla_bwd')] = exchange(l, g, 'early')

    def layer_done(l, g):
        small_grads[l] = g
        rider, sink = exchange(l, g, 'late')
        if l > 0:
            hooks[(l - 1, 'fox_bwd')] = (rider, sink)
        else:
            sink(_run_rider(rider, name=f"scatter_grads_late_{l}"))

    loss, dx, dg_final = _local_step(x[0], mem[0], loss_target[0], ps, g_final, hooks, half_done, layer_done)
    loss = lax.psum(loss[0, 0], _MESH_AXES)

    grad = {}
    for which, group in groups.items():
        shapes = [(1,) + shard[n].shape[1:] for n in group]
        per_layer = [_unpack(_sum_slots(landed[(l, which)], name=f"sum_grads_{which}_{l}"), shapes, 0) for l in range(depth)]
        grad.update({n: jnp.concatenate([per_layer[l][i] for l in range(depth)], axis=0) for i, n in enumerate(group)})
    grad['w_in'] = grad['w_in'][:, :, _IN_INV] + grad['w_in'][:, :, _IN_INV2] * _IN_SIGN2
    grads = small_grads
    small = [dg_final if n == 'g_final' else jnp.stack([grads[l][n] for l in range(depth)]) for n in _REPLICATED]
    small_shapes = [wts[n].shape for n in _REPLICATED]
    small_sum = _sum_slots(_all_gather(_pack(small, 0, _SMALL_ROW_PAD), name="gather_small_grads"), name="sum_small_grads")
    grad.update(dict(zip(_REPLICATED, _unpack(small_sum, small_shapes, 0))))

    delta, new_m, new_v = {}, {}, {}
    for n, _ in _SHARDED:
        delta[n], new_m[n], new_v[n] = _adamw(wts[n], grad[n], mom1[n], mom2[n], name=f"adamw_{n}")
    packed = [_pack([d[n] for n in _REPLICATED], 0, _SMALL_ROW_PAD) for d in (wts, mom1, mom2)]
    outs = _adamw(packed[0], small_sum, packed[1], packed[2], name="adamw_small")
    for d, o in zip((delta, new_m, new_v), outs):
        d.update(dict(zip(_REPLICATED, _unpack(o, small_shapes, 0))))

    return (loss, dx[None], *[grad[n] for n in _WEIGHTS], *[delta[n] for n in _WEIGHTS],
            *[new_m[n] for n in _WEIGHTS], *[new_v[n] for n in _WEIGHTS])
```

```python
import jax
import jax.numpy as jnp
import numpy as np
from jax import lax
from jax.experimental import pallas as pl
from jax.experimental.pallas import tpu as pltpu

F32 = jnp.float32
BF16 = jnp.bfloat16

EPS = 1e-6
CHUNK = 64
FOX_HEADS, FOX_HD = 4, 64
GLA_HEADS, GLA_DK, GLA_DV, GLA_RANK, GLA_TAU = 4, 64, 128, 16, 16.0
MLA_HEADS, MLA_Q_RANK, MLA_KV_RANK, MLA_NOPE, MLA_ROPE, MLA_VD = 4, 256, 128, 64, 32, 64
ROPE_BASE = 10000.0
XA_HEADS, XA_HD = 4, 128
ADAM_LR, ADAM_B1, ADAM_B2, ADAM_EPS, ADAM_WD, ADAM_STEP = 0.001, 0.9, 0.999, 1e-08, 0.01, 10

N_DEV = 8
V7X_VMEM_LIMIT = 56 * 1024 * 1024
NEG = -1e30

NN = ((1,), (0,))
NT = ((1,), (1,))
TN = ((0,), (0,))


def _dot(a, b, dims):
    return lax.dot_general(a.astype(BF16), b.astype(BF16), (dims, ((), ())), preferred_element_type=F32)


@jax.custom_vjp
def bdot(a, b):
    return _dot(a, b, NN)


bdot.defvjp(lambda a, b: (_dot(a, b, NN), (a, b)),
            lambda res, g: (_dot(g, res[1], NT), _dot(res[0], g, TN)))


@jax.custom_vjp
def bdot_nt(a, b):
    return _dot(a, b, NT)


bdot_nt.defvjp(lambda a, b: (_dot(a, b, NT), (a, b)),
               lambda res, g: (_dot(g, res[1], NN), _dot(g, res[0], TN)))


@jax.custom_vjp
def bdot_tn(a, b):
    return _dot(a, b, TN)


bdot_tn.defvjp(lambda a, b: (_dot(a, b, TN), (a, b)),
               lambda res, g: (_dot(res[1], g, NT), _dot(res[0], g, NN)))


def _split2(x):
    hi = x.astype(BF16)
    lo = (x - hi.astype(F32)).astype(BF16)
    return hi, lo


def _tri(n, lower):
    r = lax.broadcasted_iota(jnp.int32, (n, n), 0)
    c = lax.broadcasted_iota(jnp.int32, (n, n), 1)
    return jnp.where((r >= c) if lower else (r <= c), 1.0, 0.0).astype(BF16)


def _tri_dot2(x, lower):
    hi, lo = _split2(x)
    t = _tri(x.shape[0], lower)
    return _dot(t, hi, NN) + _dot(t, lo, NN)


@jax.custom_vjp
def chunk_cumsum(x):
    return _tri_dot2(x, True)


chunk_cumsum.defvjp(lambda x: (_tri_dot2(x, True), None), lambda _, g: (_tri_dot2(g, False),))


def _log_sigmoid(x):
    return jnp.minimum(x, 0.0) - jnp.log(1.0 + jnp.exp(-jnp.abs(x)))


def _sigmoid(x):
    return 1.0 / (1.0 + jnp.exp(-x))


def _rms(x, g):
    return x * lax.rsqrt(jnp.mean(x * x, axis=-1, keepdims=True) + EPS) * g


def _pick(dim, prefs):
    for p in prefs:
        if dim % p == 0:
            return p
    return dim


def _params(sem):
    return pltpu.CompilerParams(dimension_semantics=sem, vmem_limit_bytes=V7X_VMEM_LIMIT)


def _mm(a, b, *, mode, out_dtype, name, act=None, residual=None, drelu_of=None, b_cols=None, tm=None, tn=None, tk=None):
    b_off, b_width = b_cols or (0, b.shape[1])
    if mode == 'nn':
        (M, K), N = a.shape, b_width
    elif mode == 'nt':
        (M, K), N = a.shape, b.shape[0]
    else:
        (K, M), N = a.shape, b_width
    tm = tm or _pick(M, (1024, 512, 256, 128))
    tn = tn or _pick(N, (1024, 1920, 1152, 768, 640, 512, 384, 256, 128))
    tk = tk or _pick(K, (1024, 1920, 1152, 640, 512, 256, 128))
    nk = K // tk
    dims = {'nn': NN, 'nt': NT, 'tn': TN}[mode]
    a_spec = pl.BlockSpec((tk, tm), lambda i, j, k: (k, i)) if mode == 'tn' else pl.BlockSpec((tm, tk), lambda i, j, k: (i, k))
    if mode == 'nt':
        b_spec = pl.BlockSpec((tn, tk), lambda i, j, k, o=b_off // tk: (j, k + o))
    else:
        b_spec = pl.BlockSpec((tk, tn), lambda i, j, k, o=b_off // tn: (k, j + o))
    o_spec = pl.BlockSpec((tm, tn), lambda i, j, k: (i, j))
    extra = [e for e in (residual, drelu_of) if e is not None]

    def body(a_ref, b_ref, *rest):
        o_ref = rest[len(extra)]
        at = a_ref[...]
        if act == 'relu2':
            at = jnp.square(jnp.maximum(at.astype(F32), 0.0))
        part = _dot(at, b_ref[...], dims)

        def finish(acc):
            idx = 0
            if residual is not None:
                acc = acc + rest[idx][...]
                idx += 1
            if drelu_of is not None:
                acc = acc * (2.0 * jnp.maximum(rest[idx][...].astype(F32), 0.0))
            o_ref[...] = acc.astype(out_dtype)

        if nk == 1:
            finish(part)
        else:
            acc_ref = rest[len(extra) + 1]
            k = pl.program_id(2)

            @pl.when(k == 0)
            def _():
                acc_ref[...] = part

            @pl.when(k > 0)
            def _():
                acc_ref[...] += part

            @pl.when(k == nk - 1)
            def _():
                finish(acc_ref[...])

    return pl.pallas_call(
        body, name=name,
        out_shape=jax.ShapeDtypeStruct((M, N), out_dtype),
        grid=(M // tm, N // tn, nk),
        in_specs=[a_spec, b_spec] + [o_spec] * len(extra),
        out_specs=o_spec,
        scratch_shapes=[] if nk == 1 else [pltpu.VMEM((tm, tn), F32)],
        compiler_params=_params(("parallel", "parallel", "arbitrary")),
    )(a, b, *extra)


def _rms_fwd(x, g, *, name, out_dtype=BF16):
    S, D = x.shape
    tr = _pick(S, (512, 256, 128))

    def body(x_ref, g_ref, o_ref):
        o_ref[...] = _rms(x_ref[...], g_ref[...]).astype(out_dtype)

    return pl.pallas_call(
        body, name=name, out_shape=jax.ShapeDtypeStruct((S, D), out_dtype), grid=(S // tr,),
        in_specs=[pl.BlockSpec((tr, D), lambda i: (i, 0)), pl.BlockSpec((1, D), lambda i: (0, 0))],
        out_specs=pl.BlockSpec((tr, D), lambda i: (i, 0)),
        compiler_params=_params(("parallel",)),
    )(x, g.reshape(1, D))


def _rms_bwd(x, g, dy, dres, *, name):
    S, D = x.shape
    tr = _pick(S, (512, 256, 128))

    def body(x_ref, g_ref, dy_ref, *rest):
        dx_ref, dxb_ref, dg_ref = rest[-3], rest[-2], rest[-1]
        x_ = x_ref[...]
        rstd = lax.rsqrt(jnp.mean(x_ * x_, axis=-1, keepdims=True) + EPS)
        xh = x_ * rstd
        dy_ = dy_ref[...].astype(F32)
        gdy = dy_ * g_ref[...]
        dx = (gdy - xh * jnp.mean(gdy * xh, axis=-1, keepdims=True)) * rstd
        if dres is not None:
            dx = dx + rest[0][...]
        dx_ref[...] = dx
        dxb_ref[...] = dx.astype(BF16)
        part = jnp.sum(dy_ * xh, axis=0, keepdims=True)

        @pl.when(pl.program_id(0) == 0)
        def _():
            dg_ref[...] = part

        @pl.when(pl.program_id(0) > 0)
        def _():
            dg_ref[...] += part

    row = pl.BlockSpec((tr, D), lambda i: (i, 0))
    vec = pl.BlockSpec((1, D), lambda i: (0, 0))
    ins = [x, g.reshape(1, D), dy] + ([dres] if dres is not None else [])
    return pl.pallas_call(
        body, name=name,
        out_shape=(jax.ShapeDtypeStruct((S, D), F32), jax.ShapeDtypeStruct((S, D), BF16), jax.ShapeDtypeStruct((1, D), F32)),
        grid=(S // tr,),
        in_specs=[row, vec, row] + ([row] if dres is not None else []),
        out_specs=(row, row, vec),
        compiler_params=_params(("arbitrary",)),
    )(*ins)


def _loss_head(x, g, target, *, name):
    S, D = x.shape
    tr = _pick(S, (512, 256, 128))

    def body(x_ref, g_ref, t_ref, l_ref, dx_ref, dxb_ref, dg_ref):
        x_ = x_ref[...]
        g_ = g_ref[...]
        rstd = lax.rsqrt(jnp.mean(x_ * x_, axis=-1, keepdims=True) + EPS)
        xh = x_ * rstd
        err = xh * g_ - t_ref[...]
        lpart = (0.5 / D) * jnp.sum(jnp.sum(err * err, axis=-1, keepdims=True), axis=0, keepdims=True)
        dy = err * (1.0 / D)
        gdy = dy * g_
        dx = (gdy - xh * jnp.mean(gdy * xh, axis=-1, keepdims=True)) * rstd
        dx_ref[...] = dx
        dxb_ref[...] = dx.astype(BF16)
        gpart = jnp.sum(dy * xh, axis=0, keepdims=True)

        @pl.when(pl.program_id(0) == 0)
        def _():
            dg_ref[...] = gpart
            l_ref[...] = lpart

        @pl.when(pl.program_id(0) > 0)
        def _():
            dg_ref[...] += gpart
            l_ref[...] += lpart

    row = pl.BlockSpec((tr, D), lambda i: (i, 0))
    vec = pl.BlockSpec((1, D), lambda i: (0, 0))
    return pl.pallas_call(
        body, name=name,
        out_shape=(jax.ShapeDtypeStruct((1, 1), F32), jax.ShapeDtypeStruct((S, D), F32), jax.ShapeDtypeStruct((S, D), BF16),
                   jax.ShapeDtypeStruct((1, D), F32)),
        grid=(S // tr,),
        in_specs=[row, vec, row],
        out_specs=(pl.BlockSpec((1, 1), lambda i: (0, 0)), row, row, vec),
        compiler_params=_params(("arbitrary",)),
    )(x, g.reshape(1, D), target)


def _mask_of(mask, tq, tk):
    qpos = lax.broadcasted_iota(jnp.int32, (tq, tk), 0)
    kpos = lax.broadcasted_iota(jnp.int32, (tq, tk), 1)
    if mask == 'causal':
        return kpos <= qpos
    return kpos <= (qpos | (CHUNK - 1))


LANES = 128


def _lane_group(j, w, width):
    lane = lax.broadcasted_iota(jnp.int32, (1, width), 1)
    return (lane >= j * w) & (lane < (j + 1) * w)


def _only(x, j, w):
    if w == x.shape[1]:
        return x
    return jnp.where(_lane_group(j, w, x.shape[1]), x, jnp.zeros_like(x))


def _per_head(cols, w):
    out = cols[-1]
    for j in range(len(cols) - 2, -1, -1):
        out = jnp.where(_lane_group(j, w, LANES), cols[j], out)
    return out


def _side_by_side(xs):
    return xs[0] if len(xs) == 1 else jnp.concatenate(xs, axis=1)


def _on_top(xs):
    return xs[0] if len(xs) == 1 else jnp.concatenate(xs, axis=0)


def _stacked(x, hp, w):
    return _on_top([_only(x, j, w) for j in range(hp)])


def _col_block(entry, rows, idx):
    arr, off, width = entry
    return pl.BlockSpec((rows, width), lambda i, j, o=off // width: (idx(i, j), o))


def _attn_fwd(qk, v, H, cq, ck, *, scale, mask, name, rider=None):
    Sq, Sk = qk[0][0][0].shape[0], v[0].shape[0]
    dv = v[2] // H
    w0 = qk[0][2]
    hp = LANES // w0
    G = H // hp
    assert dv == w0 and not qk[0][3] and all(sh and H * w == LANES for _, _, w, sh in qk[1:])
    tq = _pick(Sq, (512, 256, 128))
    tk = tq if mask else _pick(Sk, (512, 256, 128))
    nq, nk = Sq // tq, Sk // tk
    bias = cq is not None
    npart = len(qk)

    def body(*refs):
        refs = split(refs)
        q_refs, k_refs = refs[0:2 * npart:2], refs[1:2 * npart:2]
        v_ref = refs[2 * npart]
        cq_ref, ck_ref = (refs[2 * npart + 1], refs[2 * npart + 2]) if bias else (None, None)
        o_ref, lse_ref, m_s, l_s, acc_s = refs[-5:]
        qi, ki = pl.program_id(0), pl.program_id(1)

        @pl.when(ki == 0)
        def _():
            m_s[...] = jnp.full(m_s.shape, NEG, F32)
            l_s[...] = jnp.zeros(l_s.shape, F32)
            acc_s[...] = jnp.zeros(acc_s.shape, F32)

        def compute(masked):
            keep = _mask_of(mask, tq, tk) if masked else None
            for g in range(G):
                lanes = slice(g * LANES, (g + 1) * LANES)
                q128, k128, v128 = q_refs[0][:, lanes], k_refs[0][:, lanes], v_ref[:, lanes]
                ps, alphas = [], []
                extras = list(zip(qk, q_refs, k_refs))[1:]
                k_all = _side_by_side([k128] + [k_ref[...] for _, _, k_ref in extras])
                for j in range(hp):
                    h = g * hp + j
                    q_all = _side_by_side([_only(q128, j, w0)] + [_only(q_ref[...], h, w) for (_, _, w, _), q_ref, _ in extras])
                    s = _dot(q_all, k_all, NT) * scale
                    if bias:
                        s = s + (cq_ref[:, h:h + 1] - ck_ref[h:h + 1, :])
                    if masked:
                        s = jnp.where(keep, s, NEG)
                    m_prev = m_s[h]
                    m_new = jnp.maximum(m_prev, jnp.max(s, axis=1, keepdims=True))
                    alpha = jnp.exp(m_prev - m_new)
                    p = jnp.exp(s - m_new)
                    l_s[h] = alpha * l_s[h] + jnp.sum(p, axis=1, keepdims=True)
                    m_s[h] = m_new
                    ps.append(p.astype(BF16))
                    alphas.append(alpha)
                acc_s[g] = _per_head(alphas, w0) * acc_s[g] + _dot(_side_by_side(ps), _stacked(v128, hp, w0), NN)

        if mask is None:
            compute(False)
        else:
            pl.when(ki < qi)(lambda: compute(False))
            pl.when(ki == qi)(lambda: compute(True))

        @pl.when(ki == ((nk - 1) if mask is None else qi))
        def _():
            lse_ref[...] = jnp.zeros(lse_ref.shape, F32)
            for g in range(G):
                o_ref[:, g * LANES:(g + 1) * LANES] = (
                    acc_s[g] / _per_head([l_s[g * hp + j] for j in range(hp)], w0)).astype(BF16)
            for h in range(H):
                lse_ref[:, h:h + 1] = m_s[h] + jnp.log(l_s[h])

    q_idx = lambda i, j: i
    k_idx = (lambda i, j: jnp.minimum(i, j)) if mask else (lambda i, j: j)
    ins, in_specs = [], []
    for q_e, k_e, _, _ in qk:
        ins += [q_e[0], k_e[0]]
        in_specs += [_col_block(q_e, tq, q_idx), _col_block(k_e, tk, k_idx)]
    ins.append(v[0])
    in_specs.append(_col_block(v, tk, k_idx))
    if bias:
        in_specs += [pl.BlockSpec((tq, 8), lambda i, j: (i, 0)), pl.BlockSpec((8, tk), lambda i, j: (0, k_idx(i, j)))]
        ins += [cq, ck]
    r_ins, r_in_specs, r_outs, r_out_specs, r_scratch, split = _carry(
        rider, len(ins), 2, lambda: (pl.program_id(0) == 0) & (pl.program_id(1) == 0),
        lambda: (pl.program_id(0) == nq - 1) & (pl.program_id(1) == nk - 1))
    res = pl.pallas_call(
        body, name=name,
        out_shape=(jax.ShapeDtypeStruct((Sq, H * dv), BF16), jax.ShapeDtypeStruct((Sq, 8), F32), *r_outs),
        grid=(nq, nk), in_specs=in_specs + r_in_specs,
        out_specs=(pl.BlockSpec((tq, H * dv), lambda i, j: (i, 0)), pl.BlockSpec((tq, 8), lambda i, j: (i, 0)), *r_out_specs),
        scratch_shapes=[pltpu.VMEM((H, tq, 1), F32), pltpu.VMEM((H, tq, 1), F32), pltpu.VMEM((G, tq, LANES), F32)] + r_scratch,
        compiler_params=_params(("arbitrary", "arbitrary")) if rider else _params(("parallel", "arbitrary")),
    )(*ins, *r_ins)
    return (res[0], res[1], rider.post(res[2:])) if rider else res


def _attn_bwd(qk, v, H, o, do, lse, cq, ck, *, scale, mask, name, rider=None):
    Sq, Sk = qk[0][0][0].shape[0], v[0].shape[0]
    dv = v[2] // H
    w0 = qk[0][2]
    hp = LANES // w0
    G = H // hp
    tq = _pick(Sq, (512, 256, 128))
    tk = tq if mask else _pick(Sk, (512, 256, 128))
    nq, nk = Sq // tq, Sk // tk
    bias = cq is not None
    npart = len(qk)
    n_in = 2 * npart + 4 + (2 if bias else 0)

    def body(*refs):
        refs = split(refs)
        q_refs, k_refs = refs[0:2 * npart:2], refs[1:2 * npart:2]
        v_ref, o_ref, do_ref, lse_ref = refs[2 * npart:2 * npart + 4]
        cq_ref, ck_ref = (refs[2 * npart + 4], refs[2 * npart + 5]) if bias else (None, None)
        outs = refs[n_in:]
        dq_refs, dk_refs, dv_ref = outs[:npart], outs[npart:2 * npart], outs[2 * npart]
        dck_ref, dcq_ref = (outs[2 * npart + 1], outs[2 * npart + 2]) if bias else (None, None)
        dk_accs, dv_acc = refs[-(npart + 1):-1], refs[-1]
        ki, qi = pl.program_id(0), pl.program_id(1)
        first_q = ki if mask else 0

        @pl.when((ki == 0) & (qi == 0))
        def _():
            for r in dq_refs:
                r[...] = jnp.zeros(r.shape, F32)
            if bias:
                dcq_ref[...] = jnp.zeros(dcq_ref.shape, F32)

        @pl.when(qi == first_q)
        def _():
            for r in dk_accs:
                r[...] = jnp.zeros(r.shape, F32)
            dv_acc[...] = jnp.zeros(dv_acc.shape, F32)
            if bias:
                dck_ref[...] = jnp.zeros(dck_ref.shape, F32)

        def compute(masked):
            keep = _mask_of(mask, tq, tk) if masked else None
            rows = pl.ds(pl.multiple_of(qi * tq, tq), tq)
            extras = list(zip(qk, q_refs, k_refs, dq_refs, dk_accs))[1:]
            for g in range(G):
                lanes = slice(g * LANES, (g + 1) * LANES)
                q128, k128, v128 = q_refs[0][:, lanes], k_refs[0][:, lanes], v_ref[:, lanes]
                do128, o128 = do_ref[:, lanes], o_ref[:, lanes]
                ps, dss = [], []
                k_all = _side_by_side([k128] + [e[2][...] for e in extras])
                for j in range(hp):
                    h = g * hp + j
                    q_all = _side_by_side([_only(q128, j, w0)] + [_only(e[1][...], h, e[0][2]) for e in extras])
                    s = _dot(q_all, k_all, NT) * scale
                    if bias:
                        s = s + (cq_ref[:, h:h + 1] - ck_ref[h:h + 1, :])
                    if masked:
                        s = jnp.where(keep, s, NEG)
                    p = jnp.exp(s - lse_ref[:, h:h + 1])
                    doh = _only(do128, j, w0)
                    dp = _dot(doh, v128, NT)
                    delta = jnp.sum(doh.astype(F32) * o128.astype(F32), axis=1, keepdims=True)
                    ds = p * (dp - delta)
                    if bias:
                        dck_ref[h:h + 1, :] -= jnp.sum(ds, axis=0, keepdims=True)
                        dcq_ref[rows, h:h + 1] += jnp.sum(ds, axis=1, keepdims=True)
                    ps.append(p.astype(BF16))
                    dss.append((ds * scale).astype(BF16))
                for (_, _, w, _), q_ref, k_ref, dq_ref, dk_acc in extras:
                    heads = range(g * hp, (g + 1) * hp)
                    dk_acc[...] += _dot(_on_top(dss), _on_top([_only(q_ref[...], h, w) for h in heads]), TN)
                    dq_ref[rows, :] += _dot(_side_by_side(dss), _on_top([_only(k_ref[...], h, w) for h in heads]), NN)
                dv_acc[:, lanes] += _dot(_on_top(ps), _stacked(do128, hp, w0), TN)
                dk_accs[0][:, lanes] += _dot(_on_top(dss), _stacked(q128, hp, w0), TN)
                dq_refs[0][rows, lanes] += _dot(_side_by_side(dss), _stacked(k128, hp, w0), NN)

        if mask is None:
            compute(False)
        else:
            pl.when(qi > ki)(lambda: compute(False))
            pl.when(qi == ki)(lambda: compute(True))

        @pl.when(qi == nq - 1)
        def _():
            for r, acc in zip(dk_refs, dk_accs):
                r[...] = acc[...]
            dv_ref[...] = dv_acc[...]

    q_idx = (lambda j, i: jnp.maximum(i, j)) if mask else (lambda j, i: i)
    k_idx = lambda j, i: j
    ins, in_specs, dq_shapes, dq_specs, dk_shapes, dk_specs, scratch = [], [], [], [], [], [], []
    for q_e, k_e, w, shared in qk:
        ins += [q_e[0], k_e[0]]
        in_specs += [_col_block(q_e, tq, q_idx), _col_block(k_e, tk, k_idx)]
        dq_shapes.append(jax.ShapeDtypeStruct((Sq, H * w), F32))
        dq_specs.append(pl.BlockSpec((Sq, H * w), lambda j, i: (0, 0)))
        kw = k_e[2]
        dk_shapes.append(jax.ShapeDtypeStruct((Sk, kw), F32))
        dk_specs.append(pl.BlockSpec((tk, kw), lambda j, i: (j, 0)))
        scratch.append(pltpu.VMEM((tk, kw), F32))
    row_q = lambda width: pl.BlockSpec((tq, width), lambda j, i: (q_idx(j, i), 0))
    ins += [v[0], o, do, lse]
    in_specs += [_col_block(v, tk, k_idx), row_q(H * dv), row_q(H * dv), row_q(8)]
    out_shape = dq_shapes + dk_shapes + [jax.ShapeDtypeStruct((Sk, H * dv), F32)]
    out_specs = dq_specs + dk_specs + [pl.BlockSpec((tk, H * dv), lambda j, i: (j, 0))]
    if bias:
        in_specs += [row_q(8), pl.BlockSpec((8, tk), lambda j, i: (0, j))]
        ins += [cq, ck]
        out_shape += [jax.ShapeDtypeStruct((8, Sk), F32), jax.ShapeDtypeStruct((Sq, 8), F32)]
        out_specs += [pl.BlockSpec((8, tk), lambda j, i: (0, j)), pl.BlockSpec((Sq, 8), lambda j, i: (0, 0))]
    scratch.append(pltpu.VMEM((tk, H * dv), F32))
    n_out = len(out_shape)
    r_ins, r_in_specs, r_outs, r_out_specs, r_scratch, split = _carry(
        rider, len(ins), n_out, lambda: (pl.program_id(0) == 0) & (pl.program_id(1) == 0),
        lambda: (pl.program_id(0) == nk - 1) & (pl.program_id(1) == nq - 1))
    res = pl.pallas_call(
        body, name=name, out_shape=tuple(out_shape + r_outs), grid=(nk, nq), in_specs=in_specs + r_in_specs,
        out_specs=tuple(out_specs + r_out_specs), scratch_shapes=scratch + r_scratch,
        compiler_params=_params(("arbitrary", "arbitrary")),
    )(*ins, *r_ins)
    own = (list(res[:npart]), list(res[npart:2 * npart]), res[2 * npart]) + tuple(res[2 * npart + 1:n_out])
    return own + (rider.post(res[n_out:]),) if rider else own


def _split3_dot(x, t):
    hi = x.astype(BF16)
    r1 = x - hi.astype(F32)
    mid = r1.astype(BF16)
    lo = (r1 - mid.astype(F32)).astype(BF16)
    return _dot(hi, t, NN) + _dot(mid, t, NN) + _dot(lo, t, NN)


def _fox_cum_fwd(ff_t, b, *, name):
    _, S = ff_t.shape
    tb = _pick(S, (512, 256, 128))

    def body(f_ref, b_ref, o_ref, carry):
        @pl.when(pl.program_id(0) == 0)
        def _():
            carry[...] = jnp.zeros(carry.shape, F32)

        lf = _log_sigmoid(f_ref[...] + b_ref[...])
        o_ref[...] = _split3_dot(lf, _tri(tb, False)) + carry[...]
        carry[...] += jnp.sum(lf, axis=1, keepdims=True)

    return pl.pallas_call(
        body, name=name, out_shape=jax.ShapeDtypeStruct((8, S), F32), grid=(S // tb,),
        in_specs=[pl.BlockSpec((8, tb), lambda i: (0, i)), pl.BlockSpec((8, 1), lambda i: (0, 0))],
        out_specs=pl.BlockSpec((8, tb), lambda i: (0, i)),
        scratch_shapes=[pltpu.VMEM((8, 1), F32)],
        compiler_params=_params(("arbitrary",)),
    )(ff_t, b)


def _fox_cum_bwd(ff_t, b, dcum_t, *, name):
    _, S = ff_t.shape
    tb = _pick(S, (512, 256, 128))
    nb = S // tb

    def body(f_ref, b_ref, dc_ref, df_ref, db_ref, carry):
        @pl.when(pl.program_id(0) == 0)
        def _():
            carry[...] = jnp.zeros(carry.shape, F32)
            db_ref[...] = jnp.zeros(db_ref.shape, F32)

        dc = dc_ref[...]
        dlf = _split3_dot(dc, _tri(tb, True)) + carry[...]
        carry[...] += jnp.sum(dc, axis=1, keepdims=True)
        df = dlf * _sigmoid(-(f_ref[...] + b_ref[...]))
        df_ref[...] = df
        db_ref[...] += jnp.sum(df, axis=1, keepdims=True)

    rev = lambda i: (0, nb - 1 - i)
    return pl.pallas_call(
        body, name=name,
        out_shape=(jax.ShapeDtypeStruct((8, S), F32), jax.ShapeDtypeStruct((8, 1), F32)), grid=(nb,),
        in_specs=[pl.BlockSpec((8, tb), rev), pl.BlockSpec((8, 1), lambda i: (0, 0)), pl.BlockSpec((8, tb), rev)],
        out_specs=(pl.BlockSpec((8, tb), rev), pl.BlockSpec((8, 1), lambda i: (0, 0))),
        scratch_shapes=[pltpu.VMEM((8, 1), F32)],
        compiler_params=_params(("arbitrary",)),
    )(ff_t, b, dcum_t)


GLA_W = GLA_HEADS * GLA_DK
GLA_BLOCK_CHUNKS = 4


def _gla_chunk(q, k, zsm, wg, bg, go, vs, rs, states):
    la = _log_sigmoid(bdot(zsm, wg) + bg) * (1.0 / GLA_TAU)
    cum = chunk_cumsum(la)
    end = jnp.sum(la, axis=0, keepdims=True)
    kd = k * jnp.exp(end - cum)
    a = jnp.exp(end)
    qs = q * (GLA_DK ** -0.5)
    lane = lax.broadcasted_iota(jnp.int32, (1, GLA_W), 1)
    outs, new_states = [], []
    for h in range(GLA_HEADS):
        head = jnp.where((lane >= h * GLA_DK) & (lane < (h + 1) * GLA_DK), 1.0, 0.0)
        st = states[h] * a + bdot_tn(vs[h], kd * head)
        o = bdot_nt(qs, st)
        o = _rms(o, go)
        outs.append(o * (rs[h] * _sigmoid(rs[h])))
        new_states.append(st)
    return outs, new_states


def _gla_fwd(z, zsm, wg, bg, go, cols, *, name):
    S = z.shape[0]
    rb = GLA_BLOCK_CHUNKS * CHUNK
    nb = S // rb
    cq, ckk, cv, cr = cols
    H = GLA_HEADS

    def body(q_ref, k_ref, zsm_ref, wg_ref, bg_ref, go_ref, *rest):
        v_refs, r_refs = rest[:H], rest[H:2 * H]
        o_ref, st_ref, state = rest[2 * H], rest[2 * H + 1], rest[2 * H + 2]

        @pl.when(pl.program_id(0) == 0)
        def _():
            state[...] = jnp.zeros(state.shape, F32)

        wg_, bg_, go_ = wg_ref[...], bg_ref[...], go_ref[...]
        for c in range(GLA_BLOCK_CHUNKS):
            rows = pl.ds(c * CHUNK, CHUNK)
            states = [state[h] for h in range(H)]
            for h in range(H):
                st_ref[c, h] = states[h]
            outs, new_states = _gla_chunk(
                q_ref[rows, :].astype(F32), k_ref[rows, :].astype(F32), zsm_ref[rows, :], wg_, bg_, go_,
                [v_refs[h][rows, :].astype(F32) for h in range(H)], [r_refs[h][rows, :].astype(F32) for h in range(H)], states)
            for h in range(H):
                o_ref[rows, h * GLA_DV:(h + 1) * GLA_DV] = outs[h].astype(BF16)
                state[h] = new_states[h]

    def col(width, off):
        return pl.BlockSpec((rb, width), lambda i, o=off // width: (i, o))

    full = lambda shp: pl.BlockSpec(shp, lambda i: (0,) * len(shp))
    in_specs = [col(GLA_W, cq), col(GLA_W, ckk), pl.BlockSpec((rb, 128), lambda i: (i, 0)),
                full((128, GLA_W)), full((1, GLA_W)), full((1, GLA_DV))]
    in_specs += [col(GLA_DV, cv + h * GLA_DV) for h in range(H)] + [col(GLA_DV, cr + h * GLA_DV) for h in range(H)]
    return pl.pallas_call(
        body, name=name,
        out_shape=(jax.ShapeDtypeStruct((S, H * GLA_DV), BF16), jax.ShapeDtypeStruct((S // CHUNK, H, GLA_DV, GLA_W), F32)),
        grid=(nb,), in_specs=in_specs,
        out_specs=(pl.BlockSpec((rb, H * GLA_DV), lambda i: (i, 0)),
                   pl.BlockSpec((GLA_BLOCK_CHUNKS, H, GLA_DV, GLA_W), lambda i: (i, 0, 0, 0))),
        scratch_shapes=[pltpu.VMEM((H, GLA_DV, GLA_W), F32)],
        compiler_params=_params(("arbitrary",)),
    )(z, z, zsm, wg, bg, go, *([z] * (2 * H)))


def _gla_bwd(z, zsm, wg, bg, go, states, do, cols, *, name):
    S = z.shape[0]
    rb = GLA_BLOCK_CHUNKS * CHUNK
    nb = S // rb
    cq, ckk, cv, cr = cols
    H = GLA_HEADS

    def body(q_ref, k_ref, zsm_ref, wg_ref, bg_ref, go_ref, st_ref, do_ref, *rest):
        v_refs, r_refs = rest[:H], rest[H:2 * H]
        dq_ref, dk_ref, dv_ref, dr_ref, dzsm_ref, dwg_ref, dbg_ref, dgo_ref, dstate = rest[2 * H:]

        @pl.when(pl.program_id(0) == 0)
        def _():
            dstate[...] = jnp.zeros(dstate.shape, F32)
            dwg_ref[...] = jnp.zeros(dwg_ref.shape, F32)
            dbg_ref[...] = jnp.zeros(dbg_ref.shape, F32)
            dgo_ref[...] = jnp.zeros(dgo_ref.shape, F32)

        wg_, bg_, go_ = wg_ref[...], bg_ref[...], go_ref[...]
        for c in reversed(range(GLA_BLOCK_CHUNKS)):
            rows = pl.ds(c * CHUNK, CHUNK)
            prim = (q_ref[rows, :].astype(F32), k_ref[rows, :].astype(F32), zsm_ref[rows, :], wg_, bg_, go_,
                    [v_refs[h][rows, :].astype(F32) for h in range(H)], [r_refs[h][rows, :].astype(F32) for h in range(H)],
                    [st_ref[c, h] for h in range(H)])
            _, vjp = jax.vjp(_gla_chunk, *prim)
            douts = [do_ref[rows, h * GLA_DV:(h + 1) * GLA_DV].astype(F32) for h in range(H)]
            dq, dk, dzs, dwg, dbg, dgo, dvs, drs, dsts = vjp((douts, [dstate[h] for h in range(H)]))
            dq_ref[rows, :] = dq.astype(BF16)
            dk_ref[rows, :] = dk.astype(BF16)
            dzsm_ref[rows, :] = dzs
            dwg_ref[...] += dwg
            dbg_ref[...] += dbg
            dgo_ref[...] += dgo
            for h in range(H):
                dv_ref[rows, h * GLA_DV:(h + 1) * GLA_DV] = dvs[h].astype(BF16)
                dr_ref[rows, h * GLA_DV:(h + 1) * GLA_DV] = drs[h].astype(BF16)
                dstate[h] = dsts[h]

    rev = lambda i: nb - 1 - i

    def col(width, off):
        return pl.BlockSpec((rb, width), lambda i, o=off // width: (rev(i), o))

    full = lambda shp: pl.BlockSpec(shp, lambda i: (0,) * len(shp))
    rowb = lambda w: pl.BlockSpec((rb, w), lambda i: (rev(i), 0))
    in_specs = [col(GLA_W, cq), col(GLA_W, ckk), rowb(128), full((128, GLA_W)), full((1, GLA_W)), full((1, GLA_DV)),
                pl.BlockSpec((GLA_BLOCK_CHUNKS, H, GLA_DV, GLA_W), lambda i: (rev(i), 0, 0, 0)), rowb(H * GLA_DV)]
    in_specs += [col(GLA_DV, cv + h * GLA_DV) for h in range(H)] + [col(GLA_DV, cr + h * GLA_DV) for h in range(H)]
    return pl.pallas_call(
        body, name=name,
        out_shape=(jax.ShapeDtypeStruct((S, GLA_W), BF16), jax.ShapeDtypeStruct((S, GLA_W), BF16),
                   jax.ShapeDtypeStruct((S, H * GLA_DV), BF16), jax.ShapeDtypeStruct((S, H * GLA_DV), BF16),
                   jax.ShapeDtypeStruct((S, 128), F32), jax.ShapeDtypeStruct((128, GLA_W), F32),
                   jax.ShapeDtypeStruct((1, GLA_W), F32), jax.ShapeDtypeStruct((1, GLA_DV), F32)),
        grid=(nb,), in_specs=in_specs,
        out_specs=(rowb(GLA_W), rowb(GLA_W), rowb(H * GLA_DV), rowb(H * GLA_DV), rowb(128),
                   full((128, GLA_W)), full((1, GLA_W)), full((1, GLA_DV))),
        scratch_shapes=[pltpu.VMEM((H, GLA_DV, GLA_W), F32)],
        compiler_params=_params(("arbitrary",)),
    )(z, z, zsm, wg, bg, go, states, do, *([z] * (2 * H)))


def _row_spec(entry, tr):
    if isinstance(entry, tuple):
        arr, width, off = entry
        return arr, pl.BlockSpec((tr, width), lambda i, o=off // width: (i, o))
    return entry, pl.BlockSpec((tr, entry.shape[1]), lambda i: (i, 0))


def _stage_fwd(fn, rows, consts, outs, *, name, tr=None):
    first = rows[0][0] if isinstance(rows[0], tuple) else rows[0]
    S = first.shape[0]
    tr = tr or _pick(S, (512, 256, 128))
    arrs, specs = zip(*[_row_spec(e, tr) for e in rows])
    nr, nc = len(rows), len(consts)

    def body(*refs):
        vals = [r[...].astype(F32) for r in refs[:nr + nc]]
        res = fn(*vals)
        for o_ref, val in zip(refs[nr + nc:], res):
            o_ref[...] = val.astype(o_ref.dtype)

    cspecs = [pl.BlockSpec(c.shape, lambda i, n=c.ndim: (0,) * n) for c in consts]
    return pl.pallas_call(
        body, name=name,
        out_shape=tuple(jax.ShapeDtypeStruct((S, w), dt) for w, dt in outs), grid=(S // tr,),
        in_specs=list(specs) + cspecs,
        out_specs=tuple(pl.BlockSpec((tr, w), lambda i: (i, 0)) for w, _ in outs),
        compiler_params=_params(("parallel",)),
    )(*arrs, *consts)


def _stage_bwd(fn, rows, consts, cts, n_diff, drow_dtypes, *, name, tr=None):
    first = rows[0][0] if isinstance(rows[0], tuple) else rows[0]
    S = first.shape[0]
    tr = tr or _pick(S, (512, 256, 128))
    arrs, specs = zip(*[_row_spec(e, tr) for e in rows])
    widths = [e[1] if isinstance(e, tuple) else e.shape[1] for e in rows]
    nr, nc, nt = len(rows), len(consts), len(cts)

    def body(*refs):
        vals = [r[...].astype(F32) for r in refs[:nr + nc]]
        ct = [r[...].astype(F32) for r in refs[nr + nc:nr + nc + nt]]
        drow_refs = refs[nr + nc + nt:nr + nc + nt + n_diff]
        dconst_refs = refs[nr + nc + nt + n_diff:]
        rest_rows = vals[n_diff:nr]

        def f(diff_rows, cs):
            return tuple(fn(*diff_rows, *rest_rows, *cs))

        _, vjp = jax.vjp(f, vals[:n_diff], vals[nr:])
        drows, dcs = vjp(tuple(ct))
        for r, val in zip(drow_refs, drows):
            r[...] = val.astype(r.dtype)
        first_step = pl.program_id(0) == 0
        for r, val in zip(dconst_refs, dcs):
            @pl.when(first_step)
            def _(r=r, val=val):
                r[...] = val

            @pl.when(jnp.logical_not(first_step))
            def _(r=r, val=val):
                r[...] += val

    cspecs = [pl.BlockSpec(c.shape, lambda i, n=c.ndim: (0,) * n) for c in consts]
    ctspecs = [pl.BlockSpec((tr, c.shape[1]), lambda i: (i, 0)) for c in cts]
    out_shape = [jax.ShapeDtypeStruct((S, widths[j]), drow_dtypes[j]) for j in range(n_diff)]
    out_shape += [jax.ShapeDtypeStruct(c.shape, F32) for c in consts]
    out_specs = [pl.BlockSpec((tr, widths[j]), lambda i: (i, 0)) for j in range(n_diff)] + cspecs
    res = pl.pallas_call(
        body, name=name, out_shape=tuple(out_shape), grid=(S // tr,),
        in_specs=list(specs) + cspecs + ctspecs, out_specs=tuple(out_specs),
        compiler_params=_params(("arbitrary",)),
    )(*arrs, *consts, *cts)
    return list(res[:n_diff]), list(res[n_diff:])


def _mla_prep_fn(cq, ckv, kr, kr_sw, cos, sin, gq, gkv, wq_n, wq_r, wq_sw, wk, wv):
    hq = _rms(cq, gq)
    hkv = _rms(ckv, gkv)
    return (bdot(hq, wq_n), bdot(hq, wq_r) * cos + bdot(hq, wq_sw) * sin,
            bdot(hkv, wk), bdot(hkv, wv), kr * cos + kr_sw * sin)


def _merge_fn(g0, g1, g2, of, og, om, b0, b1, b2, wf, wg, wm):
    return (_sigmoid(g0 + b0) * bdot(of, wf) + _sigmoid(g1 + b1) * bdot(og, wg) + _sigmoid(g2 + b2) * bdot(om, wm),)


_IN_SIZES = (256, 256, 256, 4, 256, 256, 512, 16, 512, 256, 128, 32, 3072)
_IN_OFF = np.concatenate([[0], np.cumsum(_IN_SIZES)])
(_O_FQ, _O_FK, _O_FV, _O_FF, _O_GQ, _O_GK, _O_GV, _O_GLOW, _O_GR, _O_MQ, _O_MKV, _O_MKR, _O_ZG) = [int(o) for o in _IN_OFF[:-1]]
N_IN = int(_IN_OFF[-1])
_BIG_GROUPS = ((_O_ZG, 3072), (_O_GV, 512), (_O_GR, 512), (_O_FQ, 256), (_O_FK, 256), (_O_FV, 256),
               (_O_GQ, 256), (_O_GK, 256), (_O_MQ, 256), (_O_MKV, 128))
Z_GATE, Z_GV, Z_GR, Z_FQ, Z_FK, Z_FV, Z_GQ, Z_GK, Z_MQ, Z_MKV = [int(o) for o in
                                                                    np.concatenate([[0], np.cumsum([w for _, w in _BIG_GROUPS])])[:-1]]
N_BIG = sum(w for _, w in _BIG_GROUPS)
_HALF = MLA_ROPE // 2
_QK_HD = MLA_NOPE + MLA_ROPE
SM_FF, SM_GLOW, SM_KR, SM_KR_SW, N_SM = 0, 8, 128, 256, 384
N_PAD = N_BIG + N_SM
_IN_SEGS = ([(o, w, 1.0) for o, w in _BIG_GROUPS]
            + [(_O_FF, 4, 1.0), (None, SM_GLOW - 4, 0.0), (_O_GLOW, GLA_RANK, 1.0), (None, 128 - SM_GLOW - GLA_RANK, 0.0)]
            + [(_O_MKR, MLA_ROPE, 1.0)] * MLA_HEADS
            + [(_O_MKR + _HALF, _HALF, -1.0), (_O_MKR, _HALF, 1.0)] * MLA_HEADS)


def _cols(x, start, width):
    return lax.slice_in_dim(x, start, start + width, axis=x.ndim - 1)


def _pad_w_in(w):
    return jnp.concatenate([jnp.zeros(w.shape[:-1] + (n,), w.dtype) if src is None else
                            (_cols(w, src, n) if sign > 0 else -_cols(w, src, n)) for src, n, sign in _IN_SEGS], axis=-1)


def _unpad_w_in(g):
    groups = []
    for o, n in zip(_IN_OFF[:-1], _IN_SIZES):
        total, pos = None, 0
        for src, m, sign in _IN_SEGS:
            if src is not None and o <= src and src + m <= o + n:
                term = _cols(g, pos, m) if sign > 0 else -_cols(g, pos, m)
                if m != n:
                    term = jnp.pad(term, [(0, 0)] * (g.ndim - 1) + [(int(src - o), int(o + n - src - m))])
                total = term if total is None else total + term
            pos += m
        groups.append(total)
    return jnp.concatenate(groups, axis=-1)


def _take(x, idx):
    idx = np.asarray(idx)
    cuts = [0] + [i for i in range(1, len(idx)) if idx[i] != idx[i - 1] + 1] + [len(idx)]
    return jnp.concatenate([_cols(x, int(idx[a]), b - a) for a, b in zip(cuts[:-1], cuts[1:])], axis=1)


_UQ_NOPE = np.concatenate([np.arange(h * _QK_HD, h * _QK_HD + MLA_NOPE) for h in range(MLA_HEADS)])
_UQ_ROT = np.concatenate([np.arange(h * _QK_HD + MLA_NOPE, (h + 1) * _QK_HD) for h in range(MLA_HEADS)])
_UKV_PERM = np.concatenate(
    [np.concatenate([np.arange(h * 128, h * 128 + MLA_NOPE) for h in range(MLA_HEADS)]),
     np.concatenate([np.arange(h * 128 + MLA_NOPE, (h + 1) * 128) for h in range(MLA_HEADS)])])
_UKV_INV = np.argsort(_UKV_PERM)


def _rotary_partner(r):
    return jnp.concatenate([piece for h in range(MLA_HEADS) for piece in
                            (-_cols(r, h * MLA_ROPE + _HALF, _HALF), _cols(r, h * MLA_ROPE, _HALF))], axis=1)


def _uq_grad(dn, dr, dsw):
    dr = dr + jnp.concatenate([piece for h in range(MLA_HEADS) for piece in
                               (_cols(dsw, h * MLA_ROPE + _HALF, _HALF), -_cols(dsw, h * MLA_ROPE, _HALF))], axis=1)
    return jnp.concatenate([piece for h in range(MLA_HEADS) for piece in
                            (_cols(dn, h * MLA_NOPE, MLA_NOPE), _cols(dr, h * MLA_ROPE, MLA_ROPE))], axis=1)


def _rope_tables(S):
    inv = ROPE_BASE ** (-jnp.arange(_HALF, dtype=F32) / _HALF)
    ang = jnp.arange(S, dtype=F32)[:, None] * inv[None, :]
    return jnp.tile(jnp.cos(ang), (1, 2 * MLA_HEADS)), jnp.tile(jnp.sin(ang), (1, 2 * MLA_HEADS))


class _LayerParams:
    def __init__(self, rep, l):
        self.w, self.rep, self.l, self.made = {}, rep, l, {}

    def __getitem__(self, k):
        if k not in self.made:
            self.made[k] = self._make(k)
        return self.made[k]

    def _make(self, k):
        w, rep, l = self.w, self.rep, self.l
        if k == 'wg':
            return jnp.pad(w['w_gla_gate'], [(SM_GLOW, LANES - SM_GLOW - GLA_RANK), (0, 0)])
        if k in ('wq_n', 'wq_r'):
            return _take(w['w_mla_uq'], _UQ_NOPE if k == 'wq_n' else _UQ_ROT)
        if k == 'wq_sw':
            return _rotary_partner(self['wq_r'])
        if k in ('wk', 'wv'):
            return _take(w['w_mla_ukv'], _UKV_PERM[:256] if k == 'wk' else _UKV_PERM[256:])
        if k == 'b_f':
            return jnp.zeros((8, 1), F32).at[:FOX_HEADS, 0].set(rep['b_fox_forget'][l])
        if k == 'b_gate':
            return [rep['b_branch_gate'][l][i * 1024:(i + 1) * 1024].reshape(1, 1024) for i in range(3)]
        vec = {'bg': 'b_gla_gate', 'go': 'g_gla_out', 'gq': 'g_mla_q', 'gkv': 'g_mla_kv'}
        if k in vec:
            return rep[vec[k]][l].reshape(1, -1)
        return rep[k][l] if k in rep else w[k]


_GLA_COLS = (Z_GQ, Z_GK, Z_GV, Z_GR)
_MLA_OUTS = [(256, BF16), (128, BF16), (256, BF16), (256, BF16), (128, BF16)]


def _mla_rows(z, zsm, rope):
    return [(z, 256, Z_MQ), (z, 128, Z_MKV), (zsm, 128, SM_KR), (zsm, 128, SM_KR_SW), *rope]


def _mla_consts(p):
    return [p['gq'], p['gkv'], p['wq_n'], p['wq_r'], p['wq_sw'], p['wk'], p['wv']]


def _fox_qkv(z):
    return [((z, Z_FQ, 256), (z, Z_FK, 256), FOX_HD, False)], (z, Z_FV, 256)


def _mla_qkv(qn, qr, kn, vv, kr):
    return [((qn, 0, 256), (kn, 0, 256), MLA_NOPE, False), ((qr, 0, 128), (kr, 0, 128), MLA_ROPE, True)], (vv, 0, 256)


def _xa_qkv(qx, kvx):
    return [((qx, 0, 512), (kvx, 0, 512), XA_HD, False)], (kvx, 512, 512)


def _merge_rows(z, o_fox, o_gla, o_mla):
    return [(z, 1024, Z_GATE), (z, 1024, Z_GATE + 1024), (z, 1024, Z_GATE + 2048), o_fox, o_gla, o_mla]


def _merge_consts(p):
    return p['b_gate'] + [p['w_up_fox'], p['w_up_gla'], p['w_up_mla']]


def _carried(hooks, key, call):
    rider, sink = hooks.pop(key, (None, None))
    res = call(rider=rider)
    if rider is None:
        return res
    sink(res[-1])
    return res[:-1]


def _layer_fwd(x0, mem, p, rope, l, hooks):
    S = x0.shape[0]
    sv = {'x0': x0}
    h1 = _rms_fwd(x0, p['g_mix'], name=f"rms_mix_{l}")
    z = _mm(h1, p['w_in'], mode='nn', out_dtype=BF16, b_cols=(0, N_BIG), name=f"in_big_{l}")
    zsm = _mm(h1, p['w_in'], mode='nn', out_dtype=F32, b_cols=(N_BIG, N_SM), name=f"in_small_{l}")
    sv.update(h1=h1, z=z, zsm=zsm)
    ff_t = jnp.zeros((8, S), F32).at[:FOX_HEADS].set(zsm[:, SM_FF:SM_FF + FOX_HEADS].T)
    cum_t = _fox_cum_fwd(ff_t, p['b_f'], name=f"fox_cum_{l}")
    cum = cum_t.T
    o_fox, lse_f = _carried(hooks, (l, 'fox_fwd'), lambda rider: _attn_fwd(
        *_fox_qkv(z), FOX_HEADS, cum, cum_t, scale=FOX_HD ** -0.5, mask='causal', name=f"fox_fwd_{l}", rider=rider))
    sv.update(ff_t=ff_t, cum=cum, cum_t=cum_t, lse_f=lse_f, o_fox=o_fox)
    o_gla, states = _gla_fwd(z, zsm, p['wg'], p['bg'], p['go'], _GLA_COLS, name=f"gla_fwd_{l}")
    sv.update(o_gla=o_gla, states=states)
    mla = _stage_fwd(_mla_prep_fn, _mla_rows(z, zsm, rope), _mla_consts(p), _MLA_OUTS, name=f"mla_prep_{l}")
    o_mla, lse_m = _carried(hooks, (l, 'mla_fwd'), lambda rider: _attn_fwd(
        *_mla_qkv(*mla), MLA_HEADS, None, None, scale=_QK_HD ** -0.5, mask='chunk', name=f"mla_fwd_{l}", rider=rider))
    sv.update(mla=mla, lse_m=lse_m, o_mla=o_mla)
    (y,) = _stage_fwd(_merge_fn, _merge_rows(z, o_fox, o_gla, o_mla), _merge_consts(p), [(1024, BF16)], name=f"merge_{l}")
    x1 = _mm(y, p['w_out'], mode='nn', out_dtype=F32, residual=x0, name=f"out_proj_{l}")
    sv.update(y=y, x1=x1)
    h2 = _rms_fwd(x1, p['g_xa'], name=f"rms_xa_{l}")
    hm = _rms_fwd(mem, p['g_mem'], name=f"rms_mem_{l}")
    qx = _mm(h2, p['w_xq'], mode='nn', out_dtype=BF16, name=f"xq_{l}")
    kvx = _mm(hm, p['w_xkv'], mode='nn', out_dtype=BF16, name=f"xkv_{l}")
    ox, lse_x = _attn_fwd(*_xa_qkv(qx, kvx), XA_HEADS, None, None, scale=XA_HD ** -0.5, mask=None, name=f"xa_fwd_{l}")
    x2 = _mm(ox, p['w_xo'], mode='nn', out_dtype=F32, residual=x1, name=f"xo_{l}")
    sv.update(h2=h2, hm=hm, qx=qx, kvx=kvx, lse_x=lse_x, ox=ox, x2=x2)
    h3 = _rms_fwd(x2, p['g_mlp'], name=f"rms_mlp_{l}")
    a = _mm(h3, p['w_mlp1'], mode='nn', out_dtype=BF16, name=f"mlp1_{l}")
    x3 = _mm(a, p['w_mlp2'], mode='nn', out_dtype=F32, act='relu2', residual=x2, name=f"mlp2_{l}")
    sv.update(h3=h3, a=a)
    return x3, sv


def _layer_bwd(dx3, dx3b, mem, p, rope, sv, l, hooks, half_done):
    S = dx3.shape[0]
    g = {}
    da = _mm(dx3b, p['w_mlp2'], mode='nt', out_dtype=BF16, drelu_of=sv['a'], name=f"d_mlp2_in_{l}")
    g['w_mlp2'] = _mm(sv['a'], dx3b, mode='tn', out_dtype=BF16, act='relu2', name=f"d_w_mlp2_{l}")
    dh3 = _mm(da, p['w_mlp1'], mode='nt', out_dtype=F32, name=f"d_mlp1_in_{l}")
    g['w_mlp1'] = _mm(sv['h3'], da, mode='tn', out_dtype=BF16, name=f"d_w_mlp1_{l}")
    dx2, dx2b, g['g_mlp'] = _rms_bwd(sv['x2'], p['g_mlp'], dh3, dx3, name=f"d_rms_mlp_{l}")
    dox = _mm(dx2b, p['w_xo'], mode='nt', out_dtype=BF16, name=f"d_xo_in_{l}")
    g['w_xo'] = _mm(sv['ox'], dx2b, mode='tn', out_dtype=BF16, name=f"d_w_xo_{l}")
    (dqx,), (dkx,), dvx = _attn_bwd(*_xa_qkv(sv['qx'], sv['kvx']), XA_HEADS, sv['ox'], dox, sv['lse_x'], None, None,
                                    scale=XA_HD ** -0.5, mask=None, name=f"xa_bwd_{l}")
    dqx = dqx.astype(BF16)
    dkvx = jnp.concatenate([dkx, dvx], axis=1).astype(BF16)
    dh2 = _mm(dqx, p['w_xq'], mode='nt', out_dtype=F32, name=f"d_xq_in_{l}")
    g['w_xq'] = _mm(sv['h2'], dqx, mode='tn', out_dtype=BF16, name=f"d_w_xq_{l}")
    dhm = _mm(dkvx, p['w_xkv'], mode='nt', out_dtype=F32, name=f"d_xkv_in_{l}")
    g['w_xkv'] = _mm(sv['hm'], dkvx, mode='tn', out_dtype=BF16, name=f"d_w_xkv_{l}")
    _, _, g['g_mem'] = _rms_bwd(mem, p['g_mem'], dhm, None, name=f"d_rms_mem_{l}")
    dx1, dx1b, g['g_xa'] = _rms_bwd(sv['x1'], p['g_xa'], dh2, dx2, name=f"d_rms_xa_{l}")
    dy = _mm(dx1b, p['w_out'], mode='nt', out_dtype=F32, name=f"d_out_in_{l}")
    g['w_out'] = _mm(sv['y'], dx1b, mode='tn', out_dtype=BF16, name=f"d_w_out_{l}")
    z, zsm = sv['z'], sv['zsm']
    (dg0, dg1, dg2, do_fox, do_gla, do_mla), (db0, db1, db2, g['w_up_fox'], g['w_up_gla'], g['w_up_mla']) = _stage_bwd(
        _merge_fn, _merge_rows(z, sv['o_fox'], sv['o_gla'], sv['o_mla']), _merge_consts(p), [dy], 6, [BF16] * 6,
        name=f"merge_bwd_{l}")
    g['b_branch_gate'] = jnp.concatenate([db0, db1, db2], axis=1).reshape(-1)
    half_done(l, g)
    (dfq,), (dfk,), dfv, dck, dcq = _carried(hooks, (l, 'fox_bwd'), lambda rider: _attn_bwd(
        *_fox_qkv(z), FOX_HEADS, sv['o_fox'], do_fox, sv['lse_f'], sv['cum'], sv['cum_t'],
        scale=FOX_HD ** -0.5, mask='causal', name=f"fox_bwd_{l}", rider=rider))
    dff_t, db_f = _fox_cum_bwd(sv['ff_t'], p['b_f'], dck + dcq.T, name=f"fox_cum_bwd_{l}")
    g['b_fox_forget'] = db_f[:FOX_HEADS, 0]
    dgq, dgk, dgv, dgr, dzsm, dwg, dbg, dgo = _gla_bwd(z, zsm, p['wg'], p['bg'], p['go'], sv['states'], do_gla, _GLA_COLS,
                                                       name=f"gla_bwd_{l}")
    g['w_gla_gate'] = dwg[SM_GLOW:SM_GLOW + GLA_RANK]
    g['b_gla_gate'] = dbg.reshape(-1)
    g['g_gla_out'] = dgo.reshape(-1)
    (dmqn, dmqr), (dmkn, dmkr), dmv = _carried(hooks, (l, 'mla_bwd'), lambda rider: _attn_bwd(
        *_mla_qkv(*sv['mla']), MLA_HEADS, sv['o_mla'], do_mla, sv['lse_m'], None, None,
        scale=_QK_HD ** -0.5, mask='chunk', name=f"mla_bwd_{l}", rider=rider))
    (dcq, dckv, dkr, dkr_sw), (dgq_n, dgkv_n, dwq_n, dwq_r, dwq_sw, dwk, dwv) = _stage_bwd(
        _mla_prep_fn, _mla_rows(z, zsm, rope), _mla_consts(p), [dmqn, dmqr, dmkn, dmv, dmkr], 4, [BF16] * 4,
        name=f"mla_prep_bwd_{l}")
    g['g_mla_q'] = dgq_n.reshape(-1)
    g['g_mla_kv'] = dgkv_n.reshape(-1)
    g['w_mla_uq'] = _uq_grad(dwq_n, dwq_r, dwq_sw)
    g['w_mla_ukv'] = _take(jnp.concatenate([dwk, dwv], axis=1), _UKV_INV)
    dsm = dzsm + jnp.pad(dff_t[:FOX_HEADS].T, [(0, 0), (0, 128 - FOX_HEADS)])
    dz = jnp.concatenate([dg0, dg1, dg2, dgv, dgr, dfq.astype(BF16), dfk.astype(BF16), dfv.astype(BF16), dgq, dgk, dcq, dckv,
                          dsm.astype(BF16), dkr.astype(BF16), dkr_sw.astype(BF16)], axis=1)
    dh1 = _mm(dz, p['w_in'], mode='nt', out_dtype=F32, tk=N_PAD // 2, name=f"d_in_{l}")
    g['w_in'] = _mm(sv['h1'], dz, mode='tn', out_dtype=BF16, tn=N_PAD // 3, name=f"d_w_in_{l}")
    dx0, dx0b, g['g_mix'] = _rms_bwd(sv['x0'], p['g_mix'], dh1, dx1, name=f"d_rms_mix_{l}")
    for n in ('g_mlp', 'g_mem', 'g_xa', 'g_mix'):
        g[n] = g[n].reshape(-1)
    return dx0, dx0b, g


def _local_step(x, mem, target, ps, g_final, hooks, half_done, layer_done):
    rope = _rope_tables(x.shape[0])
    saved = []
    for l, p in enumerate(ps):
        x, sv = _layer_fwd(x, mem, p, rope, l, hooks)
        saved.append(sv)
    loss, dx, dxb, dgf = _loss_head(x, g_final, target, name="loss_head")
    for l in reversed(range(len(ps))):
        dx, dxb, grads = _layer_bwd(dx, dxb, mem, ps[l], rope, saved[l], l, hooks, half_done)
        layer_done(l, grads)
    assert not hooks, f"exchanges without a carrier: {list(hooks)}"
    return loss, dx, dgf.reshape(-1)


_MESH_AXES = ("x", "y", "c")
_HBM = pl.BlockSpec(memory_space=pl.ANY)


N_CHIP = 4


def _place():
    x, y, c = (lax.axis_index(n) for n in _MESH_AXES)
    return (x, y, c), (x, y, 1 - c), [(1 - x, y), (x, 1 - y), (1 - x, 1 - y)]


def _remote(src, dst, sems, k, to):
    return pltpu.make_async_remote_copy(src_ref=src, dst_ref=dst, send_sem=sems[0].at[k], recv_sem=sems[1].at[k],
                                        device_id=to, device_id_type=pl.DeviceIdType.MESH)


def _all_gather(x, *, name):
    def body(x_ref, o_ref, send_sems, recv_sems, local_sem):
        me, sib, chips = _place()
        c = me[2]
        sems = (send_sems, recv_sems)
        slot = lambda px, py, pc: o_ref.at[4 * px + 2 * py + pc]
        mine = pltpu.make_async_copy(x_ref, slot(*me), local_sem)
        mine.start()
        first = [_remote(x_ref, slot(*me), sems, 0, sib)]
        first += [_remote(x_ref, slot(*me), sems, 1 + j, (*chip, c)) for j, chip in enumerate(chips)]
        for cp in first:
            cp.start()
        passed = [_remote(slot(*chip, c), slot(*chip, c), sems, 4 + j, sib) for j, chip in enumerate(chips)]
        for j, chip in enumerate(chips):
            _remote(x_ref, slot(*chip, c), sems, 1 + j, me).wait_recv()
            passed[j].start()
        _remote(x_ref, slot(*sib), sems, 0, me).wait_recv()
        for j, chip in enumerate(chips):
            _remote(x_ref, slot(*chip, 1 - c), sems, 4 + j, me).wait_recv()
        for cp in first + passed:
            cp.wait_send()
        mine.wait()

    return pl.pallas_call(
        body, name=name, out_shape=jax.ShapeDtypeStruct((N_DEV,) + x.shape, x.dtype),
        in_specs=[_HBM], out_specs=_HBM,
        scratch_shapes=[pltpu.SemaphoreType.DMA((N_DEV - 1,)), pltpu.SemaphoreType.DMA((N_DEV - 1,)), pltpu.SemaphoreType.DMA],
        compiler_params=pltpu.CompilerParams(has_side_effects=True),
    )(x)


class _Rider:
    def __init__(self, inputs, out_shapes, scratch, start, finish, post):
        self.inputs, self.out_shapes, self.scratch = list(inputs), list(out_shapes), list(scratch)
        self.start, self.finish, self.post = start, finish, post


def _run_rider(rider, *, name):
    def body(*refs):
        rider.start(refs)
        rider.finish(refs)

    outs = pl.pallas_call(
        body, name=name, out_shape=tuple(rider.out_shapes), in_specs=[_HBM] * len(rider.inputs),
        out_specs=(_HBM,) * len(rider.out_shapes), scratch_shapes=rider.scratch,
        compiler_params=pltpu.CompilerParams(has_side_effects=True),
    )(*rider.inputs)
    return rider.post(outs)


def _carry(rider, n_in, n_out, first, last):
    if rider is None:
        return [], [], [], [], [], lambda refs: refs
    ni, no = len(rider.inputs), len(rider.out_shapes)

    def split(refs):
        own_in, r_in = refs[:n_in], refs[n_in:n_in + ni]
        own_out, r_out = refs[n_in + ni:n_in + ni + n_out], refs[n_in + ni + n_out:n_in + ni + n_out + no]
        rest = refs[n_in + ni + n_out + no:]
        own_scr, r_scr = rest[:len(rest) - len(rider.scratch)], rest[len(rest) - len(rider.scratch):]
        rrefs = tuple(r_in) + tuple(r_out) + tuple(r_scr)
        pl.when(first())(lambda: rider.start(rrefs))
        pl.when(last())(lambda: rider.finish(rrefs))
        return tuple(own_in) + tuple(own_out) + tuple(own_scr)

    return list(rider.inputs), [_HBM] * ni, list(rider.out_shapes), [_HBM] * no, list(rider.scratch), split


def _gather_rider(shards, axes):
    n = len(shards)
    srcs, out_shapes, kinds = [], [], []
    for s, ax in zip(shards, axes):
        L, a, b = s.shape
        if ax == 1:
            srcs.append(s.reshape(L, 1, a, b)), out_shapes.append((L, N_DEV, a, b)), kinds.append('row')
        elif b % 128 == 0:
            srcs.append(s), out_shapes.append((L, a, N_DEV * b)), kinds.append('col')
        else:
            srcs.append(s.reshape(1, L, a, b)), out_shapes.append((N_DEV, L, a, b)), kinds.append('slot')

    def parts(refs):
        x_refs, o_refs = refs[:n], refs[n:2 * n]
        send_sems, recv_sems, local_sem = refs[2 * n:]
        me, sib, chips = _place()
        sems = (send_sems, recv_sems)

        def win(t, px, py, pc):
            idx = 4 * px + 2 * py + pc
            if kinds[t] == 'row':
                return o_refs[t].at[:, pl.ds(idx, 1)]
            if kinds[t] == 'col':
                width = shards[t].shape[2]
                return o_refs[t].at[:, :, pl.ds(pl.multiple_of(idx * width, 128), width)]
            return o_refs[t].at[pl.ds(idx, 1)]

        def group(k, block, to, own):
            return [_remote(x_refs[t] if own else win(t, *block), win(t, *block), sems, k * n + t, to) for t in range(n)]

        mine = [pltpu.make_async_copy(x_refs[t], win(t, *me), local_sem.at[t]) for t in range(n)]
        first = group(0, me, sib, True)
        for j, chip in enumerate(chips):
            first += group(1 + j, me, (*chip, me[2]), True)
        return me, sib, chips, group, mine, first

    def start(refs):
        *_, mine, first = parts(refs)
        for cp in mine + first:
            cp.start()

    def finish(refs):
        me, sib, chips, group, mine, first = parts(refs)
        c = me[2]
        passed = []
        for j, chip in enumerate(chips):
            for cp in group(1 + j, (*chip, c), me, False):
                cp.wait_recv()
            fwd = group(4 + j, (*chip, c), sib, False)
            for cp in fwd:
                cp.start()
            passed += fwd
        for cp in group(0, sib, me, False):
            cp.wait_recv()
        for j, chip in enumerate(chips):
            for cp in group(4 + j, (*chip, 1 - c), me, False):
                cp.wait_recv()
        for cp in first + passed:
            cp.wait_send()
        for cp in mine:
            cp.wait()

    def post(outs):
        whole = []
        for o, s, kind in zip(outs, shards, kinds):
            L, a, b = s.shape
            whole.append(o.reshape(L, N_DEV * a, b) if kind == 'row' else o if kind == 'col' else _to_whole(o, 2))
        return whole

    return _Rider(srcs, [jax.ShapeDtypeStruct(shp, s.dtype) for shp, s in zip(out_shapes, shards)],
                  [pltpu.SemaphoreType.DMA(((N_DEV - 1) * n,)), pltpu.SemaphoreType.DMA(((N_DEV - 1) * n,)),
                   pltpu.SemaphoreType.DMA((n,))], start, finish, post)


def _sibling_swap(x, *, name):
    def body(x_ref, o_ref, send_sems, recv_sems):
        me, sib, _ = _place()
        c = me[2]
        sems = (send_sems, recv_sems)
        sends = [_remote(x_ref.at[j, 1 - c], o_ref.at[j], sems, j, sib) for j in range(N_CHIP)]
        for cp in sends:
            cp.start()
        for cp in sends:
            cp.wait_send()
            cp.wait_recv()

    return pl.pallas_call(
        body, name=name, out_shape=jax.ShapeDtypeStruct((N_CHIP,) + x.shape[2:], x.dtype),
        in_specs=[_HBM], out_specs=_HBM,
        scratch_shapes=[pltpu.SemaphoreType.DMA((N_CHIP,)), pltpu.SemaphoreType.DMA((N_CHIP,))],
        compiler_params=pltpu.CompilerParams(has_side_effects=True),
    )(x)


def _pair_sum(x, got, c, *, name):
    _, _, R, _ = x.shape
    tr = _pick(R, (1024, 512, 256, 128, 64, 32, 16, 8))

    def body(c_ref, x_ref, g_ref, o_ref):
        o_ref[...] = (x_ref[...].astype(F32) + g_ref[...].astype(F32)).astype(o_ref.dtype)

    return pl.pallas_call(
        body, name=name, out_shape=jax.ShapeDtypeStruct((N_CHIP, R, 128), x.dtype),
        grid_spec=pltpu.PrefetchScalarGridSpec(
            num_scalar_prefetch=1, grid=(N_CHIP, R // tr),
            in_specs=[pl.BlockSpec((None, None, tr, 128), lambda j, i, c_ref: (j, c_ref[0], i, 0)),
                      pl.BlockSpec((None, tr, 128), lambda j, i, c_ref: (j, i, 0))],
            out_specs=pl.BlockSpec((None, tr, 128), lambda j, i, c_ref: (j, i, 0))),
        compiler_params=_params(("parallel", "parallel")),
    )(c, x, got)


def _chip_all_to_all_rider(x):
    def parts(refs):
        x_ref, o_ref, send_sems, recv_sems, local_sem = refs
        me, _, chips = _place()
        sems = (send_sems, recv_sems)
        mine = 2 * me[0] + me[1]
        local = pltpu.make_async_copy(x_ref.at[mine], o_ref.at[mine], local_sem)
        sends = [_remote(x_ref.at[2 * px + py], o_ref.at[mine], sems, j, (px, py, me[2])) for j, (px, py) in enumerate(chips)]
        arrival = lambda j: _remote(x_ref.at[mine], o_ref.at[2 * chips[j][0] + chips[j][1]], sems, j, me)
        return local, sends, arrival

    def start(refs):
        local, sends, _ = parts(refs)
        for cp in [local] + sends:
            cp.start()

    def finish(refs):
        local, sends, arrival = parts(refs)
        for j, cp in enumerate(sends):
            cp.wait_send()
            arrival(j).wait_recv()
        local.wait()

    return _Rider([x], [jax.ShapeDtypeStruct(x.shape, x.dtype)],
                  [pltpu.SemaphoreType.DMA((N_CHIP - 1,)), pltpu.SemaphoreType.DMA((N_CHIP - 1,)), pltpu.SemaphoreType.DMA],
                  start, finish, lambda outs: outs[0])


def _sum_slots(x, *, name):
    n, R, _ = x.shape
    tr = _pick(R, (1024, 512, 256, 128, 64, 32, 16, 8))

    def body(x_ref, o_ref):
        acc = x_ref[0].astype(F32)
        for j in range(1, n):
            acc = acc + x_ref[j].astype(F32)
        o_ref[...] = acc

    return pl.pallas_call(
        body, name=name, out_shape=jax.ShapeDtypeStruct((R, 128), F32), grid=(R // tr,),
        in_specs=[pl.BlockSpec((n, tr, 128), lambda i: (0, i, 0))], out_specs=pl.BlockSpec((tr, 128), lambda i: (i, 0)),
        compiler_params=_params(("parallel",)),
    )(x)


def _adamw(w, g, m, v, *, name):
    shape = w.shape
    cols = shape[-1]
    rows = int(np.prod(shape[:-1]))
    tr = next((t for t in (1024, 512, 256, 128, 64, 32, 16, 8) if rows % t == 0 and t * cols * 4 <= (1 << 20)), rows)

    def body(w_ref, g_ref, m_ref, v_ref, d_ref, mo_ref, vo_ref):
        g_ = g_ref[...]
        m_ = ADAM_B1 * m_ref[...] + (1.0 - ADAM_B1) * g_
        v_ = ADAM_B2 * v_ref[...] + (1.0 - ADAM_B2) * jnp.square(g_)
        m_hat = m_ / (1.0 - ADAM_B1 ** ADAM_STEP)
        v_hat = v_ / (1.0 - ADAM_B2 ** ADAM_STEP)
        d_ref[...] = -ADAM_LR * (m_hat / (jnp.sqrt(v_hat) + ADAM_EPS) + ADAM_WD * w_ref[...])
        mo_ref[...] = m_
        vo_ref[...] = v_

    blk = pl.BlockSpec((tr, cols), lambda i: (i, 0))
    outs = pl.pallas_call(
        body, name=name, out_shape=tuple(jax.ShapeDtypeStruct((rows, cols), F32) for _ in range(3)), grid=(rows // tr,),
        in_specs=[blk] * 4, out_specs=(blk,) * 3, compiler_params=_params(("parallel",)),
    )(*(a.reshape(rows, cols) for a in (w, g, m, v)))
    return tuple(o.reshape(shape) for o in outs)


_WEIGHTS = ('g_mix', 'w_in', 'b_fox_forget', 'w_gla_gate', 'b_gla_gate', 'g_gla_out', 'g_mla_q', 'w_mla_uq', 'g_mla_kv',
            'w_mla_ukv', 'b_branch_gate', 'w_up_fox', 'w_up_gla', 'w_up_mla', 'w_out', 'g_xa', 'g_mem', 'w_xq', 'w_xkv',
            'w_xo', 'g_mlp', 'w_mlp1', 'w_mlp2', 'g_final')
_SHARDED = (('w_in', 1), ('w_gla_gate', 2), ('w_mla_uq', 2), ('w_mla_ukv', 2), ('w_up_fox', 2), ('w_up_gla', 2),
            ('w_up_mla', 2), ('w_out', 1), ('w_xq', 1), ('w_xkv', 1), ('w_xo', 2), ('w_mlp1', 2), ('w_mlp2', 1))
_REPLICATED = tuple(n for n in _WEIGHTS if n not in dict(_SHARDED))
_ROW_PAD = 1024
_SMALL_ROW_PAD = 8
_PIECE_ROWS = 16


def _pack(flats, lead, row_pad=_ROW_PAD):
    if all(int(np.prod(a.shape[lead:])) % 128 == 0 for a in flats):
        def block(a):
            a = a.reshape(a.shape[:lead] + (-1, 128))
            return jnp.pad(a, [(0, 0)] * lead + [(0, -a.shape[lead] % _PIECE_ROWS), (0, 0)])
        cat = jnp.concatenate([block(a) for a in flats], axis=lead)
        rows = cat.shape[lead]
        return jnp.pad(cat, [(0, 0)] * lead + [(0, -(-rows // row_pad) * row_pad - rows), (0, 0)])
    cat = jnp.concatenate([a.reshape(a.shape[:lead] + (-1,)) for a in flats], axis=-1)
    n = cat.shape[-1]
    total = -(-n // (128 * row_pad)) * (128 * row_pad)
    cat = jnp.pad(cat, [(0, 0)] * lead + [(0, total - n)])
    return cat.reshape(cat.shape[:lead] + (total // 128, 128))


def _unpack(buf, shapes, lead):
    sizes = [int(np.prod(shp)) for shp in shapes]
    out, off = [], 0
    if all(n % 128 == 0 for n in sizes):
        for shp, n in zip(shapes, sizes):
            rows = buf[(slice(None),) * lead + (slice(off, off + n // 128),)]
            out.append(rows.reshape(buf.shape[:lead] + tuple(shp)))
            off += -(-(n // 128) // _PIECE_ROWS) * _PIECE_ROWS
        return out
    flat = buf.reshape(buf.shape[:lead] + (-1,))
    for shp, n in zip(shapes, sizes):
        out.append(flat[..., off:off + n].reshape(buf.shape[:lead] + tuple(shp)))
        off += n
    return out


def _to_whole(g, axis):
    if axis == 1:
        return g.transpose(1, 0, 2, 3).reshape(g.shape[1], N_DEV * g.shape[2], g.shape[3])
    return g.transpose(1, 2, 0, 3).reshape(g.shape[1], g.shape[2], N_DEV * g.shape[3])


def _to_shards(w, axis):
    L, R, C = w.shape
    if axis == 1:
        return w.reshape(L, N_DEV, R // N_DEV, C).transpose(1, 0, 2, 3)
    return w.reshape(L, R, N_DEV, C // N_DEV).transpose(2, 0, 1, 3)


def kernel(x, mem, g_mix, w_in, b_fox_forget, w_gla_gate, b_gla_gate, g_gla_out, g_mla_q, w_mla_uq, g_mla_kv, w_mla_ukv, b_branch_gate, w_up_fox, w_up_gla, w_up_mla, w_out, g_xa, g_mem, w_xq, w_xkv, w_xo, g_mlp, w_mlp1, w_mlp2, g_final, loss_target, m_g_mix, m_w_in, m_b_fox_forget, m_w_gla_gate, m_b_gla_gate, m_g_gla_out, m_g_mla_q, m_w_mla_uq, m_g_mla_kv, m_w_mla_ukv, m_b_branch_gate, m_w_up_fox, m_w_up_gla, m_w_up_mla, m_w_out, m_g_xa, m_g_mem, m_w_xq, m_w_xkv, m_w_xo, m_g_mlp, m_w_mlp1, m_w_mlp2, m_g_final, v_g_mix, v_w_in, v_b_fox_forget, v_w_gla_gate, v_b_gla_gate, v_g_gla_out, v_g_mla_q, v_w_mla_uq, v_g_mla_kv, v_w_mla_ukv, v_b_branch_gate, v_w_up_fox, v_w_up_gla, v_w_up_mla, v_w_out, v_g_xa, v_g_mem, v_w_xq, v_w_xkv, v_w_xo, v_g_mlp, v_w_mlp1, v_w_mlp2, v_g_final):
    wts = dict(zip(_WEIGHTS, (g_mix, w_in, b_fox_forget, w_gla_gate, b_gla_gate, g_gla_out, g_mla_q, w_mla_uq, g_mla_kv,
                              w_mla_ukv, b_branch_gate, w_up_fox, w_up_gla, w_up_mla, w_out, g_xa, g_mem, w_xq, w_xkv, w_xo,
                              g_mlp, w_mlp1, w_mlp2, g_final)))
    mom1 = dict(zip(_WEIGHTS, (m_g_mix, m_w_in, m_b_fox_forget, m_w_gla_gate, m_b_gla_gate, m_g_gla_out, m_g_mla_q,
                               m_w_mla_uq, m_g_mla_kv, m_w_mla_ukv, m_b_branch_gate, m_w_up_fox, m_w_up_gla, m_w_up_mla,
                               m_w_out, m_g_xa, m_g_mem, m_w_xq, m_w_xkv, m_w_xo, m_g_mlp, m_w_mlp1, m_w_mlp2, m_g_final)))
    mom2 = dict(zip(_WEIGHTS, (v_g_mix, v_w_in, v_b_fox_forget, v_w_gla_gate, v_b_gla_gate, v_g_gla_out, v_g_mla_q,
                               v_w_mla_uq, v_g_mla_kv, v_w_mla_ukv, v_b_branch_gate, v_w_up_fox, v_w_up_gla, v_w_up_mla,
                               v_w_out, v_g_xa, v_g_mem, v_w_xq, v_w_xkv, v_w_xo, v_g_mlp, v_w_mlp1, v_w_mlp2, v_g_final)))
    depth = g_mix.shape[0]

    names = [n for n, _ in _SHARDED]
    axes = dict(_SHARDED)
    shard = {n: wts[n] for n in names}
    shard['w_in'] = _pad_w_in(w_in)
    rep = {n: wts[n] for n in _REPLICATED}
    ps = [_LayerParams(rep, l) for l in range(depth)]

    def gather(group, l):
        rider = _gather_rider([shard[n][l:l + 1].astype(BF16) for n in group], [axes[n] for n in group])
        return rider, lambda whole: ps[l].w.update({n: w[0] for n, w in zip(group, whole)})

    first, sink = gather(['w_in'], 0)
    sink(_run_rider(first, name="gather_w_in_0"))
    hooks = {(0, 'fox_fwd'): gather([n for n in names if n != 'w_in'], 0)}
    for l in range(1, depth):
        hooks[(l - 1, 'mla_fwd')] = gather(names, l)

    core = lax.axis_index("c").astype(jnp.int32).reshape(1)
    late = ['w_in', 'w_gla_gate', 'w_mla_uq', 'w_mla_ukv']
    groups = {'early': [n for n in names if n not in late], 'late': late}
    small_grads, landed = {}, {}

    def exchange(l, g, which):
        slots = _pack([_to_shards(g[n][None], axes[n]).astype(BF16) for n in groups[which]], 1)
        slots = slots.reshape((N_CHIP, 2) + slots.shape[1:])
        paired = _pair_sum(slots, _sibling_swap(slots, name=f"swap_grads_{which}_{l}"), core, name=f"pair_grads_{which}_{l}")
        return _chip_all_to_all_rider(paired), lambda got: landed.update({(l, which): got})

    def half_done(l, g):
        hooks[(l, 'mla_bwd')] = exchange(l, g, 'early')

    def layer_done(l, g):
        small_grads[l] = g
        rider, sink = exchange(l, g, 'late')
        if l > 0:
            hooks[(l - 1, 'fox_bwd')] = (rider, sink)
        else:
            sink(_run_rider(rider, name=f"scatter_grads_late_{l}"))

    loss, dx, dg_final = _local_step(x[0], mem[0], loss_target[0], ps, g_final, hooks, half_done, layer_done)
    loss = lax.psum(loss[0, 0], _MESH_AXES)

    grad = {}
    for which, group in groups.items():
        shapes = [(1,) + shard[n].shape[1:] for n in group]
        per_layer = [_unpack(_sum_slots(landed[(l, which)], name=f"sum_grads_{which}_{l}"), shapes, 0) for l in range(depth)]
        grad.update({n: jnp.concatenate([per_layer[l][i] for l in range(depth)], axis=0) for i, n in enumerate(group)})
    grad['w_in'] = _unpad_w_in(grad['w_in'])
    grads = small_grads
    small = [dg_final if n == 'g_final' else jnp.stack([grads[l][n] for l in range(depth)]) for n in _REPLICATED]
    small_shapes = [wts[n].shape for n in _REPLICATED]
    small_sum = _sum_slots(_all_gather(_pack(small, 0, _SMALL_ROW_PAD), name="gather_small_grads"), name="sum_small_grads")
    grad.update(dict(zip(_REPLICATED, _unpack(small_sum, small_shapes, 0))))

    delta, new_m, new_v = {}, {}, {}
    for n, _ in _SHARDED:
        delta[n], new_m[n], new_v[n] = _adamw(wts[n], grad[n], mom1[n], mom2[n], name=f"adamw_{n}")
    packed = [_pack([d[n] for n in _REPLICATED], 0, _SMALL_ROW_PAD) for d in (wts, mom1, mom2)]
    outs = _adamw(packed[0], small_sum, packed[1], packed[2], name="adamw_small")
    for d, o in zip((delta, new_m, new_v), outs):
        d.update(dict(zip(_REPLICATED, _unpack(o, small_shapes, 0))))

    return (loss, dx[None], *[grad[n] for n in _WEIGHTS], *[delta[n] for n in _WEIGHTS],
            *[new_m[n] for n in _WEIGHTS], *[new_v[n] for n in _WEIGHTS])
```

```python
import jax
import jax.numpy as jnp
import numpy as np
from jax import lax
from jax.experimental import pallas as pl
from jax.experimental.pallas import tpu as pltpu

F32 = jnp.float32
BF16 = jnp.bfloat16

EPS = 1e-6
CHUNK = 64
FOX_HEADS, FOX_HD = 4, 64
GLA_HEADS, GLA_DK, GLA_DV, GLA_RANK, GLA_TAU = 4, 64, 128, 16, 16.0
MLA_HEADS, MLA_Q_RANK, MLA_KV_RANK, MLA_NOPE, MLA_ROPE, MLA_VD = 4, 256, 128, 64, 32, 64
ROPE_BASE = 10000.0
XA_HEADS, XA_HD = 4, 128
ADAM_LR, ADAM_B1, ADAM_B2, ADAM_EPS, ADAM_WD, ADAM_STEP = 0.001, 0.9, 0.999, 1e-08, 0.01, 10

N_DEV = 8
V7X_VMEM_LIMIT = 56 * 1024 * 1024
NEG = -1e30

NN = ((1,), (0,))
NT = ((1,), (1,))
TN = ((0,), (0,))


def _dot(a, b, dims):
    return lax.dot_general(a.astype(BF16), b.astype(BF16), (dims, ((), ())), preferred_element_type=F32)


@jax.custom_vjp
def bdot(a, b):
    return _dot(a, b, NN)


bdot.defvjp(lambda a, b: (_dot(a, b, NN), (a, b)),
            lambda res, g: (_dot(g, res[1], NT), _dot(res[0], g, TN)))


@jax.custom_vjp
def bdot_nt(a, b):
    return _dot(a, b, NT)


bdot_nt.defvjp(lambda a, b: (_dot(a, b, NT), (a, b)),
               lambda res, g: (_dot(g, res[1], NN), _dot(g, res[0], TN)))


@jax.custom_vjp
def bdot_tn(a, b):
    return _dot(a, b, TN)


bdot_tn.defvjp(lambda a, b: (_dot(a, b, TN), (a, b)),
               lambda res, g: (_dot(res[1], g, NT), _dot(res[0], g, NN)))


def _split2(x):
    hi = x.astype(BF16)
    lo = (x - hi.astype(F32)).astype(BF16)
    return hi, lo


def _tri(n, lower):
    r = lax.broadcasted_iota(jnp.int32, (n, n), 0)
    c = lax.broadcasted_iota(jnp.int32, (n, n), 1)
    return jnp.where((r >= c) if lower else (r <= c), 1.0, 0.0).astype(BF16)


def _tri_dot2(x, lower):
    hi, lo = _split2(x)
    t = _tri(x.shape[0], lower)
    return _dot(t, hi, NN) + _dot(t, lo, NN)


@jax.custom_vjp
def chunk_cumsum(x):
    return _tri_dot2(x, True)


chunk_cumsum.defvjp(lambda x: (_tri_dot2(x, True), None), lambda _, g: (_tri_dot2(g, False),))


def _log_sigmoid(x):
    return jnp.minimum(x, 0.0) - jnp.log(1.0 + jnp.exp(-jnp.abs(x)))


def _sigmoid(x):
    return 1.0 / (1.0 + jnp.exp(-x))


def _rms(x, g):
    return x * lax.rsqrt(jnp.mean(x * x, axis=-1, keepdims=True) + EPS) * g


def _pick(dim, prefs):
    for p in prefs:
        if dim % p == 0:
            return p
    return dim


def _params(sem):
    return pltpu.CompilerParams(dimension_semantics=sem, vmem_limit_bytes=V7X_VMEM_LIMIT)


def _rms_vjp(x, g, dy, dres):
    rstd = lax.rsqrt(jnp.mean(x * x, axis=-1, keepdims=True) + EPS)
    xh = x * rstd
    gdy = dy * g
    dx = (gdy - xh * jnp.mean(gdy * xh, axis=-1, keepdims=True)) * rstd
    return (dx if dres is None else dx + dres), jnp.sum(dy * xh, axis=0, keepdims=True)


def _mm(a, b, *, mode, out_dtype, name, act=None, residual=None, drelu_of=None, norm_bwd=None, b_cols=None,
        tm=None, tn=None, tk=None):
    b_off, b_width = b_cols or (0, b.shape[1])
    if mode == 'nn':
        (M, K), N = a.shape, b_width
    elif mode == 'nt':
        (M, K), N = a.shape, b.shape[0]
    else:
        (K, M), N = a.shape, b_width
    tm = tm or _pick(M, (1024, 512, 256, 128))
    tn = tn or _pick(N, (1024, 1920, 1152, 768, 640, 512, 384, 256, 128))
    tk = tk or _pick(K, (1024, 1920, 1152, 640, 512, 256, 128))
    nk = K // tk
    dims = {'nn': NN, 'nt': NT, 'tn': TN}[mode]
    a_spec = pl.BlockSpec((tk, tm), lambda i, j, k: (k, i)) if mode == 'tn' else pl.BlockSpec((tm, tk), lambda i, j, k: (i, k))
    if mode == 'nt':
        b_spec = pl.BlockSpec((tn, tk), lambda i, j, k, o=b_off // tk: (j, k + o))
    else:
        b_spec = pl.BlockSpec((tk, tn), lambda i, j, k, o=b_off // tn: (k, j + o))
    o_spec = pl.BlockSpec((tm, tn), lambda i, j, k: (i, j))
    extra = [e for e in (residual, drelu_of) if e is not None]
    extra_specs = [o_spec] * len(extra)
    out_shape, out_specs, n_out = jax.ShapeDtypeStruct((M, N), out_dtype), o_spec, 1
    if norm_bwd is not None:
        x_in, g_in, dres_in = norm_bwd
        assert tn == N and residual is None and drelu_of is None
        vec = pl.BlockSpec((1, N), lambda i, j, k: (0, 0))
        extra, extra_specs = [x_in, g_in.reshape(1, N), dres_in], [o_spec, vec, o_spec]
        out_shape = (jax.ShapeDtypeStruct((M, N), F32), jax.ShapeDtypeStruct((M, N), BF16), jax.ShapeDtypeStruct((1, N), F32))
        out_specs, n_out = (o_spec, o_spec, vec), 3

    def body(a_ref, b_ref, *rest):
        o_ref = rest[len(extra)]
        first_rows = pl.program_id(0) == 0
        at = a_ref[...]
        if act == 'relu2':
            at = jnp.square(jnp.maximum(at.astype(F32), 0.0))
        part = _dot(at, b_ref[...], dims)

        def finish(acc):
            if norm_bwd is not None:
                dx, dg = _rms_vjp(rest[0][...], rest[1][...], acc, rest[2][...])
                o_ref[...] = dx
                rest[len(extra) + 1][...] = dx.astype(BF16)
                dg_ref = rest[len(extra) + 2]

                @pl.when(first_rows)
                def _():
                    dg_ref[...] = dg

                @pl.when(jnp.logical_not(first_rows))
                def _():
                    dg_ref[...] += dg
                return
            idx = 0
            if residual is not None:
                acc = acc + rest[idx][...]
                idx += 1
            if drelu_of is not None:
                acc = acc * (2.0 * jnp.maximum(rest[idx][...].astype(F32), 0.0))
            o_ref[...] = acc.astype(out_dtype)

        if nk == 1:
            finish(part)
        else:
            acc_ref = rest[len(extra) + n_out]
            k = pl.program_id(2)

            @pl.when(k == 0)
            def _():
                acc_ref[...] = part

            @pl.when(k > 0)
            def _():
                acc_ref[...] += part

            @pl.when(k == nk - 1)
            def _():
                finish(acc_ref[...])

    return pl.pallas_call(
        body, name=name,
        out_shape=out_shape,
        grid=(M // tm, N // tn, nk),
        in_specs=[a_spec, b_spec] + extra_specs,
        out_specs=out_specs,
        scratch_shapes=[] if nk == 1 else [pltpu.VMEM((tm, tn), F32)],
        compiler_params=_params(("arbitrary" if norm_bwd is not None else "parallel", "parallel", "arbitrary")),
    )(a, b, *extra)


def _rms_fwd(x, g, *, name, out_dtype=BF16):
    S, D = x.shape
    tr = _pick(S, (512, 256, 128))

    def body(x_ref, g_ref, o_ref):
        o_ref[...] = _rms(x_ref[...], g_ref[...]).astype(out_dtype)

    return pl.pallas_call(
        body, name=name, out_shape=jax.ShapeDtypeStruct((S, D), out_dtype), grid=(S // tr,),
        in_specs=[pl.BlockSpec((tr, D), lambda i: (i, 0)), pl.BlockSpec((1, D), lambda i: (0, 0))],
        out_specs=pl.BlockSpec((tr, D), lambda i: (i, 0)),
        compiler_params=_params(("parallel",)),
    )(x, g.reshape(1, D))


def _rms_bwd(x, g, dy, dres, *, name):
    S, D = x.shape
    tr = _pick(S, (512, 256, 128))

    def body(x_ref, g_ref, dy_ref, *rest):
        dx_ref, dxb_ref, dg_ref = rest[-3], rest[-2], rest[-1]
        dx, part = _rms_vjp(x_ref[...], g_ref[...], dy_ref[...].astype(F32), None if dres is None else rest[0][...])
        dx_ref[...] = dx
        dxb_ref[...] = dx.astype(BF16)

        @pl.when(pl.program_id(0) == 0)
        def _():
            dg_ref[...] = part

        @pl.when(pl.program_id(0) > 0)
        def _():
            dg_ref[...] += part

    row = pl.BlockSpec((tr, D), lambda i: (i, 0))
    vec = pl.BlockSpec((1, D), lambda i: (0, 0))
    ins = [x, g.reshape(1, D), dy] + ([dres] if dres is not None else [])
    return pl.pallas_call(
        body, name=name,
        out_shape=(jax.ShapeDtypeStruct((S, D), F32), jax.ShapeDtypeStruct((S, D), BF16), jax.ShapeDtypeStruct((1, D), F32)),
        grid=(S // tr,),
        in_specs=[row, vec, row] + ([row] if dres is not None else []),
        out_specs=(row, row, vec),
        compiler_params=_params(("arbitrary",)),
    )(*ins)


def _loss_head(x, g, target, *, name):
    S, D = x.shape
    tr = _pick(S, (512, 256, 128))

    def body(x_ref, g_ref, t_ref, l_ref, dx_ref, dxb_ref, dg_ref):
        x_ = x_ref[...]
        g_ = g_ref[...]
        rstd = lax.rsqrt(jnp.mean(x_ * x_, axis=-1, keepdims=True) + EPS)
        xh = x_ * rstd
        err = xh * g_ - t_ref[...]
        lpart = (0.5 / D) * jnp.sum(jnp.sum(err * err, axis=-1, keepdims=True), axis=0, keepdims=True)
        dy = err * (1.0 / D)
        gdy = dy * g_
        dx = (gdy - xh * jnp.mean(gdy * xh, axis=-1, keepdims=True)) * rstd
        dx_ref[...] = dx
        dxb_ref[...] = dx.astype(BF16)
        gpart = jnp.sum(dy * xh, axis=0, keepdims=True)

        @pl.when(pl.program_id(0) == 0)
        def _():
            dg_ref[...] = gpart
            l_ref[...] = lpart

        @pl.when(pl.program_id(0) > 0)
        def _():
            dg_ref[...] += gpart
            l_ref[...] += lpart

    row = pl.BlockSpec((tr, D), lambda i: (i, 0))
    vec = pl.BlockSpec((1, D), lambda i: (0, 0))
    return pl.pallas_call(
        body, name=name,
        out_shape=(jax.ShapeDtypeStruct((1, 1), F32), jax.ShapeDtypeStruct((S, D), F32), jax.ShapeDtypeStruct((S, D), BF16),
                   jax.ShapeDtypeStruct((1, D), F32)),
        grid=(S // tr,),
        in_specs=[row, vec, row],
        out_specs=(pl.BlockSpec((1, 1), lambda i: (0, 0)), row, row, vec),
        compiler_params=_params(("arbitrary",)),
    )(x, g.reshape(1, D), target)


def _mask_of(mask, tq, tk):
    qpos = lax.broadcasted_iota(jnp.int32, (tq, tk), 0)
    kpos = lax.broadcasted_iota(jnp.int32, (tq, tk), 1)
    if mask == 'causal':
        return kpos <= qpos
    return kpos <= (qpos | (CHUNK - 1))


LANES = 128
LOG2E, LN2 = 1.4426950408889634, 0.6931471805599453


def _lane_group(j, w, width):
    lane = lax.broadcasted_iota(jnp.int32, (1, width), 1)
    return (lane >= j * w) & (lane < (j + 1) * w)


def _only(x, j, w):
    if w == x.shape[1]:
        return x
    return jnp.where(_lane_group(j, w, x.shape[1]), x, jnp.zeros_like(x))


def _per_head(cols, w):
    out = cols[-1]
    for j in range(len(cols) - 2, -1, -1):
        out = jnp.where(_lane_group(j, w, LANES), cols[j], out)
    return out


def _side_by_side(xs):
    return xs[0] if len(xs) == 1 else jnp.concatenate(xs, axis=1)


def _on_top(xs):
    return xs[0] if len(xs) == 1 else jnp.concatenate(xs, axis=0)


def _stacked(x, hp, w):
    return _on_top([_only(x, j, w) for j in range(hp)])


def _col_block(entry, rows, idx):
    arr, off, width = entry
    return pl.BlockSpec((rows, width), lambda i, j, o=off // width: (idx(i, j), o))


def _attn_fwd(qk, v, H, cq, ck, *, scale, mask, name, rider=None):
    Sq, Sk = qk[0][0][0].shape[0], v[0].shape[0]
    dv = v[2] // H
    w0 = qk[0][2]
    hp = LANES // w0
    G = H // hp
    assert dv == w0 and not qk[0][3] and all(sh and H * w == LANES for _, _, w, sh in qk[1:])
    tq = _pick(Sq, (512, 256, 128))
    tk = tq if mask else _pick(Sk, (512, 256, 128))
    nq, nk = Sq // tq, Sk // tk
    bias = cq is not None
    npart = len(qk)

    def body(*refs):
        refs = split(refs)
        q_refs, k_refs = refs[0:2 * npart:2], refs[1:2 * npart:2]
        v_ref = refs[2 * npart]
        cq_ref, ck_ref = (refs[2 * npart + 1], refs[2 * npart + 2]) if bias else (None, None)
        o_ref, lse_ref, m_s, l_s, acc_s = refs[-5:]
        qi, ki = pl.program_id(0), pl.program_id(1)

        @pl.when(ki == 0)
        def _():
            m_s[...] = jnp.full(m_s.shape, NEG, F32)
            l_s[...] = jnp.zeros(l_s.shape, F32)
            acc_s[...] = jnp.zeros(acc_s.shape, F32)

        def compute(masked):
            keep = _mask_of(mask, tq, tk) if masked else None
            for g in range(G):
                lanes = slice(g * LANES, (g + 1) * LANES)
                q128, k128, v128 = q_refs[0][:, lanes], k_refs[0][:, lanes], v_ref[:, lanes]
                ps, alphas = [], []
                extras = list(zip(qk, q_refs, k_refs))[1:]
                k_all = _side_by_side([k128] + [k_ref[...] for _, _, k_ref in extras])
                for j in range(hp):
                    h = g * hp + j
                    q_all = _side_by_side([_only(q128, j, w0)] + [_only(q_ref[...], h, w) for (_, _, w, _), q_ref, _ in extras])
                    s = _dot(q_all, k_all, NT) * (scale * LOG2E)
                    if bias:
                        s = s - ck_ref[h:h + 1, :] * LOG2E
                    if masked:
                        s = jnp.where(keep, s, NEG)
                    row = cq_ref[:, h:h + 1] * LOG2E if bias else 0.0
                    m_prev = m_s[h]
                    m_new = jnp.maximum(m_prev, jnp.max(s, axis=1, keepdims=True) + row)
                    alpha = jnp.exp2(m_prev - m_new)
                    p = jnp.exp2(s - (m_new - row))
                    l_s[h] = alpha * l_s[h] + jnp.sum(p, axis=1, keepdims=True)
                    m_s[h] = m_new
                    ps.append(p.astype(BF16))
                    alphas.append(alpha)
                acc_s[g] = _per_head(alphas, w0) * acc_s[g] + _dot(_side_by_side(ps), _stacked(v128, hp, w0), NN)

        if mask is None:
            compute(False)
        else:
            pl.when(ki < qi)(lambda: compute(False))
            pl.when(ki == qi)(lambda: compute(True))

        @pl.when(ki == ((nk - 1) if mask is None else qi))
        def _():
            lse_ref[...] = jnp.zeros(lse_ref.shape, F32)
            for g in range(G):
                o_ref[:, g * LANES:(g + 1) * LANES] = (
                    acc_s[g] / _per_head([l_s[g * hp + j] for j in range(hp)], w0)).astype(BF16)
            for h in range(H):
                lse_ref[:, h:h + 1] = m_s[h] * LN2 + jnp.log(l_s[h])

    q_idx = lambda i, j: i
    k_idx = (lambda i, j: jnp.minimum(i, j)) if mask else (lambda i, j: j)
    ins, in_specs = [], []
    for q_e, k_e, _, _ in qk:
        ins += [q_e[0], k_e[0]]
        in_specs += [_col_block(q_e, tq, q_idx), _col_block(k_e, tk, k_idx)]
    ins.append(v[0])
    in_specs.append(_col_block(v, tk, k_idx))
    if bias:
        in_specs += [pl.BlockSpec((tq, 8), lambda i, j: (i, 0)), pl.BlockSpec((8, tk), lambda i, j: (0, k_idx(i, j)))]
        ins += [cq, ck]
    r_ins, r_in_specs, r_outs, r_out_specs, r_scratch, split = _carry(
        rider, len(ins), 2, lambda: (pl.program_id(0) == 0) & (pl.program_id(1) == 0),
        lambda: (pl.program_id(0) == nq - 1) & (pl.program_id(1) == nk - 1))
    res = pl.pallas_call(
        body, name=name,
        out_shape=(jax.ShapeDtypeStruct((Sq, H * dv), BF16), jax.ShapeDtypeStruct((Sq, 8), F32), *r_outs),
        grid=(nq, nk), in_specs=in_specs + r_in_specs,
        out_specs=(pl.BlockSpec((tq, H * dv), lambda i, j: (i, 0)), pl.BlockSpec((tq, 8), lambda i, j: (i, 0)), *r_out_specs),
        scratch_shapes=[pltpu.VMEM((H, tq, 1), F32), pltpu.VMEM((H, tq, 1), F32), pltpu.VMEM((G, tq, LANES), F32)] + r_scratch,
        compiler_params=_params(("arbitrary", "arbitrary")) if rider else _params(("parallel", "arbitrary")),
    )(*ins, *r_ins)
    return (res[0], res[1], rider.post(res[2:])) if rider else res


def _attn_bwd(qk, v, H, o, do, lse, cq, ck, *, scale, mask, name, rider=None):
    Sq, Sk = qk[0][0][0].shape[0], v[0].shape[0]
    dv = v[2] // H
    w0 = qk[0][2]
    hp = LANES // w0
    G = H // hp
    tq = _pick(Sq, (512, 256, 128))
    tk = tq if mask else _pick(Sk, (512, 256, 128))
    nq, nk = Sq // tq, Sk // tk
    bias = cq is not None
    npart = len(qk)
    n_in = 2 * npart + 4 + (2 if bias else 0)

    def body(*refs):
        refs = split(refs)
        q_refs, k_refs = refs[0:2 * npart:2], refs[1:2 * npart:2]
        v_ref, o_ref, do_ref, lse_ref = refs[2 * npart:2 * npart + 4]
        cq_ref, ck_ref = (refs[2 * npart + 4], refs[2 * npart + 5]) if bias else (None, None)
        outs = refs[n_in:]
        dq_refs, dk_refs, dv_ref = outs[:npart], outs[npart:2 * npart], outs[2 * npart]
        dck_ref, dcq_ref = (outs[2 * npart + 1], outs[2 * npart + 2]) if bias else (None, None)
        dk_accs, dv_acc = refs[-(npart + 1):-1], refs[-1]
        ki, qi = pl.program_id(0), pl.program_id(1)
        first_q = ki if mask else 0

        @pl.when((ki == 0) & (qi == 0))
        def _():
            for r in dq_refs:
                r[...] = jnp.zeros(r.shape, F32)
            if bias:
                dcq_ref[...] = jnp.zeros(dcq_ref.shape, F32)

        @pl.when(qi == first_q)
        def _():
            for r in dk_accs:
                r[...] = jnp.zeros(r.shape, F32)
            dv_acc[...] = jnp.zeros(dv_acc.shape, F32)
            if bias:
                dck_ref[...] = jnp.zeros(dck_ref.shape, F32)

        def compute(masked):
            keep = _mask_of(mask, tq, tk) if masked else None
            rows = pl.ds(pl.multiple_of(qi * tq, tq), tq)
            extras = list(zip(qk, q_refs, k_refs, dq_refs, dk_accs))[1:]
            for g in range(G):
                lanes = slice(g * LANES, (g + 1) * LANES)
                q128, k128, v128 = q_refs[0][:, lanes], k_refs[0][:, lanes], v_ref[:, lanes]
                do128, o128 = do_ref[:, lanes], o_ref[:, lanes]
                ps, dss = [], []
                k_all = _side_by_side([k128] + [e[2][...] for e in extras])
                for j in range(hp):
                    h = g * hp + j
                    q_all = _side_by_side([_only(q128, j, w0)] + [_only(e[1][...], h, e[0][2]) for e in extras])
                    s = _dot(q_all, k_all, NT) * (scale * LOG2E)
                    if bias:
                        s = s - ck_ref[h:h + 1, :] * LOG2E
                    if masked:
                        s = jnp.where(keep, s, NEG)
                    row = lse_ref[:, h:h + 1] - cq_ref[:, h:h + 1] if bias else lse_ref[:, h:h + 1]
                    p = jnp.exp2(s - row * LOG2E)
                    doh = _only(do128, j, w0)
                    dp = _dot(doh, v128, NT)
                    delta = jnp.sum(doh.astype(F32) * o128.astype(F32), axis=1, keepdims=True)
                    ds = p * (dp - delta)
                    if bias:
                        dck_ref[h:h + 1, :] -= jnp.sum(ds, axis=0, keepdims=True)
                        dcq_ref[rows, h:h + 1] += jnp.sum(ds, axis=1, keepdims=True)
                    ps.append(p.astype(BF16))
                    dss.append((ds * scale).astype(BF16))
                for (_, _, w, _), q_ref, k_ref, dq_ref, dk_acc in extras:
                    heads = range(g * hp, (g + 1) * hp)
                    dk_acc[...] += _dot(_on_top(dss), _on_top([_only(q_ref[...], h, w) for h in heads]), TN)
                    dq_ref[rows, :] += _dot(_side_by_side(dss), _on_top([_only(k_ref[...], h, w) for h in heads]), NN)
                dv_acc[:, lanes] += _dot(_on_top(ps), _stacked(do128, hp, w0), TN)
                dk_accs[0][:, lanes] += _dot(_on_top(dss), _stacked(q128, hp, w0), TN)
                dq_refs[0][rows, lanes] += _dot(_side_by_side(dss), _stacked(k128, hp, w0), NN)

        if mask is None:
            compute(False)
        else:
            pl.when(qi > ki)(lambda: compute(False))
            pl.when(qi == ki)(lambda: compute(True))

        @pl.when(qi == nq - 1)
        def _():
            for r, acc in zip(dk_refs, dk_accs):
                r[...] = acc[...]
            dv_ref[...] = dv_acc[...]

    q_idx = (lambda j, i: jnp.maximum(i, j)) if mask else (lambda j, i: i)
    k_idx = lambda j, i: j
    ins, in_specs, dq_shapes, dq_specs, dk_shapes, dk_specs, scratch = [], [], [], [], [], [], []
    for q_e, k_e, w, shared in qk:
        ins += [q_e[0], k_e[0]]
        in_specs += [_col_block(q_e, tq, q_idx), _col_block(k_e, tk, k_idx)]
        dq_shapes.append(jax.ShapeDtypeStruct((Sq, H * w), F32))
        dq_specs.append(pl.BlockSpec((Sq, H * w), lambda j, i: (0, 0)))
        kw = k_e[2]
        dk_shapes.append(jax.ShapeDtypeStruct((Sk, kw), F32))
        dk_specs.append(pl.BlockSpec((tk, kw), lambda j, i: (j, 0)))
        scratch.append(pltpu.VMEM((tk, kw), F32))
    row_q = lambda width: pl.BlockSpec((tq, width), lambda j, i: (q_idx(j, i), 0))
    ins += [v[0], o, do, lse]
    in_specs += [_col_block(v, tk, k_idx), row_q(H * dv), row_q(H * dv), row_q(8)]
    out_shape = dq_shapes + dk_shapes + [jax.ShapeDtypeStruct((Sk, H * dv), F32)]
    out_specs = dq_specs + dk_specs + [pl.BlockSpec((tk, H * dv), lambda j, i: (j, 0))]
    if bias:
        in_specs += [row_q(8), pl.BlockSpec((8, tk), lambda j, i: (0, j))]
        ins += [cq, ck]
        out_shape += [jax.ShapeDtypeStruct((8, Sk), F32), jax.ShapeDtypeStruct((Sq, 8), F32)]
        out_specs += [pl.BlockSpec((8, tk), lambda j, i: (0, j)), pl.BlockSpec((Sq, 8), lambda j, i: (0, 0))]
    scratch.append(pltpu.VMEM((tk, H * dv), F32))
    n_out = len(out_shape)
    r_ins, r_in_specs, r_outs, r_out_specs, r_scratch, split = _carry(
        rider, len(ins), n_out, lambda: (pl.program_id(0) == 0) & (pl.program_id(1) == 0),
        lambda: (pl.program_id(0) == nk - 1) & (pl.program_id(1) == nq - 1))
    res = pl.pallas_call(
        body, name=name, out_shape=tuple(out_shape + r_outs), grid=(nk, nq), in_specs=in_specs + r_in_specs,
        out_specs=tuple(out_specs + r_out_specs), scratch_shapes=scratch + r_scratch,
        compiler_params=_params(("arbitrary", "arbitrary")),
    )(*ins, *r_ins)
    own = (list(res[:npart]), list(res[npart:2 * npart]), res[2 * npart]) + tuple(res[2 * npart + 1:n_out])
    return own + (rider.post(res[n_out:]),) if rider else own


def _split3_dot(x, t):
    hi = x.astype(BF16)
    r1 = x - hi.astype(F32)
    mid = r1.astype(BF16)
    lo = (r1 - mid.astype(F32)).astype(BF16)
    return _dot(hi, t, NN) + _dot(mid, t, NN) + _dot(lo, t, NN)


def _fox_cum_fwd(ff_t, b, *, name):
    _, S = ff_t.shape
    tb = _pick(S, (512, 256, 128))

    def body(f_ref, b_ref, o_ref, carry):
        @pl.when(pl.program_id(0) == 0)
        def _():
            carry[...] = jnp.zeros(carry.shape, F32)

        lf = _log_sigmoid(f_ref[...] + b_ref[...])
        o_ref[...] = _split3_dot(lf, _tri(tb, False)) + carry[...]
        carry[...] += jnp.sum(lf, axis=1, keepdims=True)

    return pl.pallas_call(
        body, name=name, out_shape=jax.ShapeDtypeStruct((8, S), F32), grid=(S // tb,),
        in_specs=[pl.BlockSpec((8, tb), lambda i: (0, i)), pl.BlockSpec((8, 1), lambda i: (0, 0))],
        out_specs=pl.BlockSpec((8, tb), lambda i: (0, i)),
        scratch_shapes=[pltpu.VMEM((8, 1), F32)],
        compiler_params=_params(("arbitrary",)),
    )(ff_t, b)


def _fox_cum_bwd(ff_t, b, dcum_t, *, name):
    _, S = ff_t.shape
    tb = _pick(S, (512, 256, 128))
    nb = S // tb

    def body(f_ref, b_ref, dc_ref, df_ref, db_ref, carry):
        @pl.when(pl.program_id(0) == 0)
        def _():
            carry[...] = jnp.zeros(carry.shape, F32)
            db_ref[...] = jnp.zeros(db_ref.shape, F32)

        dc = dc_ref[...]
        dlf = _split3_dot(dc, _tri(tb, True)) + carry[...]
        carry[...] += jnp.sum(dc, axis=1, keepdims=True)
        df = dlf * _sigmoid(-(f_ref[...] + b_ref[...]))
        df_ref[...] = df
        db_ref[...] += jnp.sum(df, axis=1, keepdims=True)

    rev = lambda i: (0, nb - 1 - i)
    return pl.pallas_call(
        body, name=name,
        out_shape=(jax.ShapeDtypeStruct((8, S), F32), jax.ShapeDtypeStruct((8, 1), F32)), grid=(nb,),
        in_specs=[pl.BlockSpec((8, tb), rev), pl.BlockSpec((8, 1), lambda i: (0, 0)), pl.BlockSpec((8, tb), rev)],
        out_specs=(pl.BlockSpec((8, tb), rev), pl.BlockSpec((8, 1), lambda i: (0, 0))),
        scratch_shapes=[pltpu.VMEM((8, 1), F32)],
        compiler_params=_params(("arbitrary",)),
    )(ff_t, b, dcum_t)


GLA_W = GLA_HEADS * GLA_DK
GLA_BLOCK_CHUNKS = 4


def _gla_chunk(q, k, zsm, wg, bg, go, vs, rs, states):
    la = _log_sigmoid(bdot(zsm, wg) + bg) * (1.0 / GLA_TAU)
    cum = chunk_cumsum(la)
    end = jnp.sum(la, axis=0, keepdims=True)
    kd = k * jnp.exp(end - cum)
    a = jnp.exp(end)
    qs = q * (GLA_DK ** -0.5)
    lane = lax.broadcasted_iota(jnp.int32, (1, GLA_W), 1)
    outs, new_states = [], []
    for h in range(GLA_HEADS):
        head = jnp.where((lane >= h * GLA_DK) & (lane < (h + 1) * GLA_DK), 1.0, 0.0)
        st = states[h] * a + bdot_tn(vs[h], kd * head)
        o = bdot_nt(qs, st)
        o = _rms(o, go)
        outs.append(o * (rs[h] * _sigmoid(rs[h])))
        new_states.append(st)
    return outs, new_states


def _gla_fwd(z, zsm, wg, bg, go, cols, *, name):
    S = z.shape[0]
    rb = GLA_BLOCK_CHUNKS * CHUNK
    nb = S // rb
    cq, ckk, cv, cr = cols
    H = GLA_HEADS

    def body(q_ref, k_ref, zsm_ref, wg_ref, bg_ref, go_ref, *rest):
        v_refs, r_refs = rest[:H], rest[H:2 * H]
        o_ref, st_ref, state = rest[2 * H], rest[2 * H + 1], rest[2 * H + 2]

        @pl.when(pl.program_id(0) == 0)
        def _():
            state[...] = jnp.zeros(state.shape, F32)

        wg_, bg_, go_ = wg_ref[...], bg_ref[...], go_ref[...]
        for c in range(GLA_BLOCK_CHUNKS):
            rows = pl.ds(c * CHUNK, CHUNK)
            states = [state[h] for h in range(H)]
            for h in range(H):
                st_ref[c, h] = states[h]
            outs, new_states = _gla_chunk(
                q_ref[rows, :].astype(F32), k_ref[rows, :].astype(F32), zsm_ref[rows, :], wg_, bg_, go_,
                [v_refs[h][rows, :].astype(F32) for h in range(H)], [r_refs[h][rows, :].astype(F32) for h in range(H)], states)
            for h in range(H):
                o_ref[rows, h * GLA_DV:(h + 1) * GLA_DV] = outs[h].astype(BF16)
                state[h] = new_states[h]

    def col(width, off):
        return pl.BlockSpec((rb, width), lambda i, o=off // width: (i, o))

    full = lambda shp: pl.BlockSpec(shp, lambda i: (0,) * len(shp))
    in_specs = [col(GLA_W, cq), col(GLA_W, ckk), pl.BlockSpec((rb, 128), lambda i: (i, 0)),
                full((128, GLA_W)), full((1, GLA_W)), full((1, GLA_DV))]
    in_specs += [col(GLA_DV, cv + h * GLA_DV) for h in range(H)] + [col(GLA_DV, cr + h * GLA_DV) for h in range(H)]
    return pl.pallas_call(
        body, name=name,
        out_shape=(jax.ShapeDtypeStruct((S, H * GLA_DV), BF16), jax.ShapeDtypeStruct((S // CHUNK, H, GLA_DV, GLA_W), F32)),
        grid=(nb,), in_specs=in_specs,
        out_specs=(pl.BlockSpec((rb, H * GLA_DV), lambda i: (i, 0)),
                   pl.BlockSpec((GLA_BLOCK_CHUNKS, H, GLA_DV, GLA_W), lambda i: (i, 0, 0, 0))),
        scratch_shapes=[pltpu.VMEM((H, GLA_DV, GLA_W), F32)],
        compiler_params=_params(("arbitrary",)),
    )(z, z, zsm, wg, bg, go, *([z] * (2 * H)))


def _gla_bwd(z, zsm, wg, bg, go, states, do, cols, *, name):
    S = z.shape[0]
    rb = GLA_BLOCK_CHUNKS * CHUNK
    nb = S // rb
    cq, ckk, cv, cr = cols
    H = GLA_HEADS

    def body(q_ref, k_ref, zsm_ref, wg_ref, bg_ref, go_ref, st_ref, do_ref, *rest):
        v_refs, r_refs = rest[:H], rest[H:2 * H]
        dq_ref, dk_ref, dv_ref, dr_ref, dzsm_ref, dwg_ref, dbg_ref, dgo_ref, dstate = rest[2 * H:]

        @pl.when(pl.program_id(0) == 0)
        def _():
            dstate[...] = jnp.zeros(dstate.shape, F32)
            dwg_ref[...] = jnp.zeros(dwg_ref.shape, F32)
            dbg_ref[...] = jnp.zeros(dbg_ref.shape, F32)
            dgo_ref[...] = jnp.zeros(dgo_ref.shape, F32)

        wg_, bg_, go_ = wg_ref[...], bg_ref[...], go_ref[...]
        for c in reversed(range(GLA_BLOCK_CHUNKS)):
            rows = pl.ds(c * CHUNK, CHUNK)
            prim = (q_ref[rows, :].astype(F32), k_ref[rows, :].astype(F32), zsm_ref[rows, :], wg_, bg_, go_,
                    [v_refs[h][rows, :].astype(F32) for h in range(H)], [r_refs[h][rows, :].astype(F32) for h in range(H)],
                    [st_ref[c, h] for h in range(H)])
            _, vjp = jax.vjp(_gla_chunk, *prim)
            douts = [do_ref[rows, h * GLA_DV:(h + 1) * GLA_DV].astype(F32) for h in range(H)]
            dq, dk, dzs, dwg, dbg, dgo, dvs, drs, dsts = vjp((douts, [dstate[h] for h in range(H)]))
            dq_ref[rows, :] = dq.astype(BF16)
            dk_ref[rows, :] = dk.astype(BF16)
            dzsm_ref[rows, :] = dzs
            dwg_ref[...] += dwg
            dbg_ref[...] += dbg
            dgo_ref[...] += dgo
            for h in range(H):
                dv_ref[rows, h * GLA_DV:(h + 1) * GLA_DV] = dvs[h].astype(BF16)
                dr_ref[rows, h * GLA_DV:(h + 1) * GLA_DV] = drs[h].astype(BF16)
                dstate[h] = dsts[h]

    rev = lambda i: nb - 1 - i

    def col(width, off):
        return pl.BlockSpec((rb, width), lambda i, o=off // width: (rev(i), o))

    full = lambda shp: pl.BlockSpec(shp, lambda i: (0,) * len(shp))
    rowb = lambda w: pl.BlockSpec((rb, w), lambda i: (rev(i), 0))
    in_specs = [col(GLA_W, cq), col(GLA_W, ckk), rowb(128), full((128, GLA_W)), full((1, GLA_W)), full((1, GLA_DV)),
                pl.BlockSpec((GLA_BLOCK_CHUNKS, H, GLA_DV, GLA_W), lambda i: (rev(i), 0, 0, 0)), rowb(H * GLA_DV)]
    in_specs += [col(GLA_DV, cv + h * GLA_DV) for h in range(H)] + [col(GLA_DV, cr + h * GLA_DV) for h in range(H)]
    return pl.pallas_call(
        body, name=name,
        out_shape=(jax.ShapeDtypeStruct((S, GLA_W), BF16), jax.ShapeDtypeStruct((S, GLA_W), BF16),
                   jax.ShapeDtypeStruct((S, H * GLA_DV), BF16), jax.ShapeDtypeStruct((S, H * GLA_DV), BF16),
                   jax.ShapeDtypeStruct((S, 128), F32), jax.ShapeDtypeStruct((128, GLA_W), F32),
                   jax.ShapeDtypeStruct((1, GLA_W), F32), jax.ShapeDtypeStruct((1, GLA_DV), F32)),
        grid=(nb,), in_specs=in_specs,
        out_specs=(rowb(GLA_W), rowb(GLA_W), rowb(H * GLA_DV), rowb(H * GLA_DV), rowb(128),
                   full((128, GLA_W)), full((1, GLA_W)), full((1, GLA_DV))),
        scratch_shapes=[pltpu.VMEM((H, GLA_DV, GLA_W), F32)],
        compiler_params=_params(("arbitrary",)),
    )(z, z, zsm, wg, bg, go, states, do, *([z] * (2 * H)))


def _row_spec(entry, tr):
    if isinstance(entry, tuple):
        arr, width, off = entry
        return arr, pl.BlockSpec((tr, width), lambda i, o=off // width: (i, o))
    return entry, pl.BlockSpec((tr, entry.shape[1]), lambda i: (i, 0))


def _stage_fwd(fn, rows, consts, outs, *, name, tr=None):
    first = rows[0][0] if isinstance(rows[0], tuple) else rows[0]
    S = first.shape[0]
    tr = tr or _pick(S, (512, 256, 128))
    arrs, specs = zip(*[_row_spec(e, tr) for e in rows])
    nr, nc = len(rows), len(consts)

    def body(*refs):
        vals = [r[...].astype(F32) for r in refs[:nr + nc]]
        res = fn(*vals)
        for o_ref, val in zip(refs[nr + nc:], res):
            o_ref[...] = val.astype(o_ref.dtype)

    cspecs = [pl.BlockSpec(c.shape, lambda i, n=c.ndim: (0,) * n) for c in consts]
    return pl.pallas_call(
        body, name=name,
        out_shape=tuple(jax.ShapeDtypeStruct((S, w), dt) for w, dt in outs), grid=(S // tr,),
        in_specs=list(specs) + cspecs,
        out_specs=tuple(pl.BlockSpec((tr, w), lambda i: (i, 0)) for w, _ in outs),
        compiler_params=_params(("parallel",)),
    )(*arrs, *consts)


def _stage_bwd(fn, rows, consts, cts, n_diff, drow_dtypes, *, name, tr=None):
    first = rows[0][0] if isinstance(rows[0], tuple) else rows[0]
    S = first.shape[0]
    tr = tr or _pick(S, (512, 256, 128))
    arrs, specs = zip(*[_row_spec(e, tr) for e in rows])
    widths = [e[1] if isinstance(e, tuple) else e.shape[1] for e in rows]
    nr, nc, nt = len(rows), len(consts), len(cts)

    def body(*refs):
        vals = [r[...].astype(F32) for r in refs[:nr + nc]]
        ct = [r[...].astype(F32) for r in refs[nr + nc:nr + nc + nt]]
        drow_refs = refs[nr + nc + nt:nr + nc + nt + n_diff]
        dconst_refs = refs[nr + nc + nt + n_diff:]
        rest_rows = vals[n_diff:nr]

        def f(diff_rows, cs):
            return tuple(fn(*diff_rows, *rest_rows, *cs))

        _, vjp = jax.vjp(f, vals[:n_diff], vals[nr:])
        drows, dcs = vjp(tuple(ct))
        for r, val in zip(drow_refs, drows):
            r[...] = val.astype(r.dtype)
        first_step = pl.program_id(0) == 0
        for r, val in zip(dconst_refs, dcs):
            @pl.when(first_step)
            def _(r=r, val=val):
                r[...] = val

            @pl.when(jnp.logical_not(first_step))
            def _(r=r, val=val):
                r[...] += val

    cspecs = [pl.BlockSpec(c.shape, lambda i, n=c.ndim: (0,) * n) for c in consts]
    ctspecs = [pl.BlockSpec((tr, c.shape[1]), lambda i: (i, 0)) for c in cts]
    out_shape = [jax.ShapeDtypeStruct((S, widths[j]), drow_dtypes[j]) for j in range(n_diff)]
    out_shape += [jax.ShapeDtypeStruct(c.shape, F32) for c in consts]
    out_specs = [pl.BlockSpec((tr, widths[j]), lambda i: (i, 0)) for j in range(n_diff)] + cspecs
    res = pl.pallas_call(
        body, name=name, out_shape=tuple(out_shape), grid=(S // tr,),
        in_specs=list(specs) + cspecs + ctspecs, out_specs=tuple(out_specs),
        compiler_params=_params(("arbitrary",)),
    )(*arrs, *consts, *cts)
    return list(res[:n_diff]), list(res[n_diff:])


def _mla_prep_fn(cq, ckv, kr, kr_sw, cos, sin, gq, gkv, wq_n, wq_r, wq_sw, wk, wv):
    hq = _rms(cq, gq)
    hkv = _rms(ckv, gkv)
    return (bdot(hq, wq_n), bdot(hq, wq_r) * cos + bdot(hq, wq_sw) * sin,
            bdot(hkv, wk), bdot(hkv, wv), kr * cos + kr_sw * sin)


def _merge_fn(g0, g1, g2, of, og, om, b0, b1, b2, wf, wg, wm):
    return (_sigmoid(g0 + b0) * bdot(of, wf) + _sigmoid(g1 + b1) * bdot(og, wg) + _sigmoid(g2 + b2) * bdot(om, wm),)


_IN_SIZES = (256, 256, 256, 4, 256, 256, 512, 16, 512, 256, 128, 32, 3072)
_IN_OFF = np.concatenate([[0], np.cumsum(_IN_SIZES)])
(_O_FQ, _O_FK, _O_FV, _O_FF, _O_GQ, _O_GK, _O_GV, _O_GLOW, _O_GR, _O_MQ, _O_MKV, _O_MKR, _O_ZG) = [int(o) for o in _IN_OFF[:-1]]
N_IN = int(_IN_OFF[-1])
_BIG_GROUPS = ((_O_ZG, 3072), (_O_GV, 512), (_O_GR, 512), (_O_FQ, 256), (_O_FK, 256), (_O_FV, 256),
               (_O_GQ, 256), (_O_GK, 256), (_O_MQ, 256), (_O_MKV, 128))
Z_GATE, Z_GV, Z_GR, Z_FQ, Z_FK, Z_FV, Z_GQ, Z_GK, Z_MQ, Z_MKV = [int(o) for o in
                                                                    np.concatenate([[0], np.cumsum([w for _, w in _BIG_GROUPS])])[:-1]]
N_BIG = sum(w for _, w in _BIG_GROUPS)
_HALF = MLA_ROPE // 2
_QK_HD = MLA_NOPE + MLA_ROPE
SM_FF, SM_GLOW, SM_KR, SM_KR_SW, N_SM = 0, 8, 128, 256, 384
N_PAD = N_BIG + N_SM
_IN_SEGS = ([(o, w, 1.0) for o, w in _BIG_GROUPS]
            + [(_O_FF, 4, 1.0), (None, SM_GLOW - 4, 0.0), (_O_GLOW, GLA_RANK, 1.0), (None, 128 - SM_GLOW - GLA_RANK, 0.0)]
            + [(_O_MKR, MLA_ROPE, 1.0)] * MLA_HEADS
            + [(_O_MKR + _HALF, _HALF, -1.0), (_O_MKR, _HALF, 1.0)] * MLA_HEADS)


def _cols(x, start, width):
    return lax.slice_in_dim(x, start, start + width, axis=x.ndim - 1)


def _pad_w_in(w):
    return jnp.concatenate([jnp.zeros(w.shape[:-1] + (n,), w.dtype) if src is None else
                            (_cols(w, src, n) if sign > 0 else -_cols(w, src, n)) for src, n, sign in _IN_SEGS], axis=-1)


def _unpad_w_in(g):
    groups = []
    for o, n in zip(_IN_OFF[:-1], _IN_SIZES):
        total, pos = None, 0
        for src, m, sign in _IN_SEGS:
            if src is not None and o <= src and src + m <= o + n:
                term = _cols(g, pos, m) if sign > 0 else -_cols(g, pos, m)
                if m != n:
                    term = jnp.pad(term, [(0, 0)] * (g.ndim - 1) + [(int(src - o), int(o + n - src - m))])
                total = term if total is None else total + term
            pos += m
        groups.append(total)
    return jnp.concatenate(groups, axis=-1)


def _take(x, idx):
    idx = np.asarray(idx)
    cuts = [0] + [i for i in range(1, len(idx)) if idx[i] != idx[i - 1] + 1] + [len(idx)]
    return jnp.concatenate([_cols(x, int(idx[a]), b - a) for a, b in zip(cuts[:-1], cuts[1:])], axis=1)


_UQ_NOPE = np.concatenate([np.arange(h * _QK_HD, h * _QK_HD + MLA_NOPE) for h in range(MLA_HEADS)])
_UQ_ROT = np.concatenate([np.arange(h * _QK_HD + MLA_NOPE, (h + 1) * _QK_HD) for h in range(MLA_HEADS)])
_UKV_PERM = np.concatenate(
    [np.concatenate([np.arange(h * 128, h * 128 + MLA_NOPE) for h in range(MLA_HEADS)]),
     np.concatenate([np.arange(h * 128 + MLA_NOPE, (h + 1) * 128) for h in range(MLA_HEADS)])])
_UKV_INV = np.argsort(_UKV_PERM)


def _rotary_partner(r):
    return jnp.concatenate([piece for h in range(MLA_HEADS) for piece in
                            (-_cols(r, h * MLA_ROPE + _HALF, _HALF), _cols(r, h * MLA_ROPE, _HALF))], axis=1)


def _uq_grad(dn, dr, dsw):
    dr = dr + jnp.concatenate([piece for h in range(MLA_HEADS) for piece in
                               (_cols(dsw, h * MLA_ROPE + _HALF, _HALF), -_cols(dsw, h * MLA_ROPE, _HALF))], axis=1)
    return jnp.concatenate([piece for h in range(MLA_HEADS) for piece in
                            (_cols(dn, h * MLA_NOPE, MLA_NOPE), _cols(dr, h * MLA_ROPE, MLA_ROPE))], axis=1)


def _rope_tables(S):
    inv = ROPE_BASE ** (-jnp.arange(_HALF, dtype=F32) / _HALF)
    ang = jnp.arange(S, dtype=F32)[:, None] * inv[None, :]
    return jnp.tile(jnp.cos(ang), (1, 2 * MLA_HEADS)), jnp.tile(jnp.sin(ang), (1, 2 * MLA_HEADS))


class _LayerParams:
    def __init__(self, rep, l):
        self.w, self.rep, self.l, self.made = {}, rep, l, {}

    def __getitem__(self, k):
        if k not in self.made:
            self.made[k] = self._make(k)
        return self.made[k]

    def _make(self, k):
        w, rep, l = self.w, self.rep, self.l
        if k == 'wg':
            return jnp.pad(w['w_gla_gate'], [(SM_GLOW, LANES - SM_GLOW - GLA_RANK), (0, 0)])
        if k in ('wq_n', 'wq_r'):
            return _take(w['w_mla_uq'], _UQ_NOPE if k == 'wq_n' else _UQ_ROT)
        if k == 'wq_sw':
            return _rotary_partner(self['wq_r'])
        if k in ('wk', 'wv'):
            return _take(w['w_mla_ukv'], _UKV_PERM[:256] if k == 'wk' else _UKV_PERM[256:])
        if k == 'b_f':
            return jnp.zeros((8, 1), F32).at[:FOX_HEADS, 0].set(rep['b_fox_forget'][l])
        if k == 'b_gate':
            return [rep['b_branch_gate'][l][i * 1024:(i + 1) * 1024].reshape(1, 1024) for i in range(3)]
        vec = {'bg': 'b_gla_gate', 'go': 'g_gla_out', 'gq': 'g_mla_q', 'gkv': 'g_mla_kv'}
        if k in vec:
            return rep[vec[k]][l].reshape(1, -1)
        return rep[k][l] if k in rep else w[k]


_GLA_COLS = (Z_GQ, Z_GK, Z_GV, Z_GR)
_MLA_OUTS = [(256, BF16), (128, BF16), (256, BF16), (256, BF16), (128, BF16)]


def _mla_rows(z, zsm, rope):
    return [(z, 256, Z_MQ), (z, 128, Z_MKV), (zsm, 128, SM_KR), (zsm, 128, SM_KR_SW), *rope]


def _mla_consts(p):
    return [p['gq'], p['gkv'], p['wq_n'], p['wq_r'], p['wq_sw'], p['wk'], p['wv']]


def _fox_qkv(z):
    return [((z, Z_FQ, 256), (z, Z_FK, 256), FOX_HD, False)], (z, Z_FV, 256)


def _mla_qkv(qn, qr, kn, vv, kr):
    return [((qn, 0, 256), (kn, 0, 256), MLA_NOPE, False), ((qr, 0, 128), (kr, 0, 128), MLA_ROPE, True)], (vv, 0, 256)


def _xa_qkv(qx, kvx):
    return [((qx, 0, 512), (kvx, 0, 512), XA_HD, False)], (kvx, 512, 512)


def _merge_rows(z, o_fox, o_gla, o_mla):
    return [(z, 1024, Z_GATE), (z, 1024, Z_GATE + 1024), (z, 1024, Z_GATE + 2048), o_fox, o_gla, o_mla]


def _merge_consts(p):
    return p['b_gate'] + [p['w_up_fox'], p['w_up_gla'], p['w_up_mla']]


def _carried(hooks, key, call):
    rider, sink = hooks.pop(key, (None, None))
    res = call(rider=rider)
    if rider is None:
        return res
    sink(res[-1])
    return res[:-1]


def _layer_fwd(x0, mem, p, rope, l, hooks):
    S = x0.shape[0]
    sv = {'x0': x0}
    h1 = _rms_fwd(x0, p['g_mix'], name=f"rms_mix_{l}")
    z = _mm(h1, p['w_in'], mode='nn', out_dtype=BF16, b_cols=(0, N_BIG), name=f"in_big_{l}")
    zsm = _mm(h1, p['w_in'], mode='nn', out_dtype=F32, b_cols=(N_BIG, N_SM), name=f"in_small_{l}")
    sv.update(h1=h1, z=z, zsm=zsm)
    ff_t = jnp.zeros((8, S), F32).at[:FOX_HEADS].set(zsm[:, SM_FF:SM_FF + FOX_HEADS].T)
    cum_t = _fox_cum_fwd(ff_t, p['b_f'], name=f"fox_cum_{l}")
    cum = cum_t.T
    o_fox, lse_f = _carried(hooks, (l, 'fox_fwd'), lambda rider: _attn_fwd(
        *_fox_qkv(z), FOX_HEADS, cum, cum_t, scale=FOX_HD ** -0.5, mask='causal', name=f"fox_fwd_{l}", rider=rider))
    sv.update(ff_t=ff_t, cum=cum, cum_t=cum_t, lse_f=lse_f, o_fox=o_fox)
    o_gla, states = _gla_fwd(z, zsm, p['wg'], p['bg'], p['go'], _GLA_COLS, name=f"gla_fwd_{l}")
    sv.update(o_gla=o_gla, states=states)
    mla = _stage_fwd(_mla_prep_fn, _mla_rows(z, zsm, rope), _mla_consts(p), _MLA_OUTS, name=f"mla_prep_{l}")
    o_mla, lse_m = _carried(hooks, (l, 'mla_fwd'), lambda rider: _attn_fwd(
        *_mla_qkv(*mla), MLA_HEADS, None, None, scale=_QK_HD ** -0.5, mask='chunk', name=f"mla_fwd_{l}", rider=rider))
    sv.update(mla=mla, lse_m=lse_m, o_mla=o_mla)
    (y,) = _stage_fwd(_merge_fn, _merge_rows(z, o_fox, o_gla, o_mla), _merge_consts(p), [(1024, BF16)], name=f"merge_{l}")
    x1 = _mm(y, p['w_out'], mode='nn', out_dtype=F32, residual=x0, name=f"out_proj_{l}")
    sv.update(y=y, x1=x1)
    h2 = _rms_fwd(x1, p['g_xa'], name=f"rms_xa_{l}")
    hm = _rms_fwd(mem, p['g_mem'], name=f"rms_mem_{l}")
    qx = _mm(h2, p['w_xq'], mode='nn', out_dtype=BF16, name=f"xq_{l}")
    kvx = _mm(hm, p['w_xkv'], mode='nn', out_dtype=BF16, name=f"xkv_{l}")
    ox, lse_x = _attn_fwd(*_xa_qkv(qx, kvx), XA_HEADS, None, None, scale=XA_HD ** -0.5, mask=None, name=f"xa_fwd_{l}")
    x2 = _mm(ox, p['w_xo'], mode='nn', out_dtype=F32, residual=x1, name=f"xo_{l}")
    sv.update(h2=h2, hm=hm, qx=qx, kvx=kvx, lse_x=lse_x, ox=ox, x2=x2)
    h3 = _rms_fwd(x2, p['g_mlp'], name=f"rms_mlp_{l}")
    a = _mm(h3, p['w_mlp1'], mode='nn', out_dtype=BF16, name=f"mlp1_{l}")
    x3 = _mm(a, p['w_mlp2'], mode='nn', out_dtype=F32, act='relu2', residual=x2, name=f"mlp2_{l}")
    sv.update(h3=h3, a=a)
    return x3, sv


def _layer_bwd(dx3, dx3b, mem, p, rope, sv, l, hooks, half_done):
    S = dx3.shape[0]
    g = {}
    da = _mm(dx3b, p['w_mlp2'], mode='nt', out_dtype=BF16, drelu_of=sv['a'], name=f"d_mlp2_in_{l}")
    g['w_mlp2'] = _mm(sv['a'], dx3b, mode='tn', out_dtype=BF16, act='relu2', name=f"d_w_mlp2_{l}")
    dx2, dx2b, g['g_mlp'] = _mm(da, p['w_mlp1'], mode='nt', out_dtype=F32, norm_bwd=(sv['x2'], p['g_mlp'], dx3), tm=512,
                                name=f"d_mlp1_in_{l}")
    g['w_mlp1'] = _mm(sv['h3'], da, mode='tn', out_dtype=BF16, name=f"d_w_mlp1_{l}")
    dox = _mm(dx2b, p['w_xo'], mode='nt', out_dtype=BF16, name=f"d_xo_in_{l}")
    g['w_xo'] = _mm(sv['ox'], dx2b, mode='tn', out_dtype=BF16, name=f"d_w_xo_{l}")
    (dqx,), (dkx,), dvx = _attn_bwd(*_xa_qkv(sv['qx'], sv['kvx']), XA_HEADS, sv['ox'], dox, sv['lse_x'], None, None,
                                    scale=XA_HD ** -0.5, mask=None, name=f"xa_bwd_{l}")
    dqx = dqx.astype(BF16)
    dkvx = jnp.concatenate([dkx, dvx], axis=1).astype(BF16)
    dx1, dx1b, g['g_xa'] = _mm(dqx, p['w_xq'], mode='nt', out_dtype=F32, norm_bwd=(sv['x1'], p['g_xa'], dx2), tm=512,
                               name=f"d_xq_in_{l}")
    g['w_xq'] = _mm(sv['h2'], dqx, mode='tn', out_dtype=BF16, name=f"d_w_xq_{l}")
    dhm = _mm(dkvx, p['w_xkv'], mode='nt', out_dtype=F32, name=f"d_xkv_in_{l}")
    g['w_xkv'] = _mm(sv['hm'], dkvx, mode='tn', out_dtype=BF16, name=f"d_w_xkv_{l}")
    _, _, g['g_mem'] = _rms_bwd(mem, p['g_mem'], dhm, None, name=f"d_rms_mem_{l}")
    dy = _mm(dx1b, p['w_out'], mode='nt', out_dtype=F32, name=f"d_out_in_{l}")
    g['w_out'] = _mm(sv['y'], dx1b, mode='tn', out_dtype=BF16, name=f"d_w_out_{l}")
    z, zsm = sv['z'], sv['zsm']
    (dg0, dg1, dg2, do_fox, do_gla, do_mla), (db0, db1, db2, g['w_up_fox'], g['w_up_gla'], g['w_up_mla']) = _stage_bwd(
        _merge_fn, _merge_rows(z, sv['o_fox'], sv['o_gla'], sv['o_mla']), _merge_consts(p), [dy], 6, [BF16] * 6,
        name=f"merge_bwd_{l}")
    g['b_branch_gate'] = jnp.concatenate([db0, db1, db2], axis=1).reshape(-1)
    half_done(l, g)
    (dfq,), (dfk,), dfv, dck, dcq = _carried(hooks, (l, 'fox_bwd'), lambda rider: _attn_bwd(
        *_fox_qkv(z), FOX_HEADS, sv['o_fox'], do_fox, sv['lse_f'], sv['cum'], sv['cum_t'],
        scale=FOX_HD ** -0.5, mask='causal', name=f"fox_bwd_{l}", rider=rider))
    dff_t, db_f = _fox_cum_bwd(sv['ff_t'], p['b_f'], dck + dcq.T, name=f"fox_cum_bwd_{l}")
    g['b_fox_forget'] = db_f[:FOX_HEADS, 0]
    dgq, dgk, dgv, dgr, dzsm, dwg, dbg, dgo = _gla_bwd(z, zsm, p['wg'], p['bg'], p['go'], sv['states'], do_gla, _GLA_COLS,
                                                       name=f"gla_bwd_{l}")
    g['w_gla_gate'] = dwg[SM_GLOW:SM_GLOW + GLA_RANK]
    g['b_gla_gate'] = dbg.reshape(-1)
    g['g_gla_out'] = dgo.reshape(-1)
    (dmqn, dmqr), (dmkn, dmkr), dmv = _carried(hooks, (l, 'mla_bwd'), lambda rider: _attn_bwd(
        *_mla_qkv(*sv['mla']), MLA_HEADS, sv['o_mla'], do_mla, sv['lse_m'], None, None,
        scale=_QK_HD ** -0.5, mask='chunk', name=f"mla_bwd_{l}", rider=rider))
    (dcq, dckv, dkr, dkr_sw), (dgq_n, dgkv_n, dwq_n, dwq_r, dwq_sw, dwk, dwv) = _stage_bwd(
        _mla_prep_fn, _mla_rows(z, zsm, rope), _mla_consts(p), [dmqn, dmqr, dmkn, dmv, dmkr], 4, [BF16] * 4,
        name=f"mla_prep_bwd_{l}")
    g['g_mla_q'] = dgq_n.reshape(-1)
    g['g_mla_kv'] = dgkv_n.reshape(-1)
    g['w_mla_uq'] = _uq_grad(dwq_n, dwq_r, dwq_sw)
    g['w_mla_ukv'] = _take(jnp.concatenate([dwk, dwv], axis=1), _UKV_INV)
    dsm = dzsm + jnp.pad(dff_t[:FOX_HEADS].T, [(0, 0), (0, 128 - FOX_HEADS)])
    dz = jnp.concatenate([dg0, dg1, dg2, dgv, dgr, dfq.astype(BF16), dfk.astype(BF16), dfv.astype(BF16), dgq, dgk, dcq, dckv,
                          dsm.astype(BF16), dkr.astype(BF16), dkr_sw.astype(BF16)], axis=1)
    dx0, dx0b, g['g_mix'] = _mm(dz, p['w_in'], mode='nt', out_dtype=F32, norm_bwd=(sv['x0'], p['g_mix'], dx1), tm=512,
                                tk=N_PAD // 2, name=f"d_in_{l}")
    g['w_in'] = _mm(sv['h1'], dz, mode='tn', out_dtype=BF16, tn=N_PAD // 3, name=f"d_w_in_{l}")
    for n in ('g_mlp', 'g_mem', 'g_xa', 'g_mix'):
        g[n] = g[n].reshape(-1)
    return dx0, dx0b, g


def _local_step(x, mem, target, ps, g_final, hooks, half_done, layer_done):
    rope = _rope_tables(x.shape[0])
    saved = []
    for l, p in enumerate(ps):
        x, sv = _layer_fwd(x, mem, p, rope, l, hooks)
        saved.append(sv)
    loss, dx, dxb, dgf = _loss_head(x, g_final, target, name="loss_head")
    for l in reversed(range(len(ps))):
        dx, dxb, grads = _layer_bwd(dx, dxb, mem, ps[l], rope, saved[l], l, hooks, half_done)
        layer_done(l, grads)
    assert not hooks, f"exchanges without a carrier: {list(hooks)}"
    return loss, dx, dgf.reshape(-1)


_MESH_AXES = ("x", "y", "c")
_HBM = pl.BlockSpec(memory_space=pl.ANY)


N_CHIP = 4


def _place():
    x, y, c = (lax.axis_index(n) for n in _MESH_AXES)
    return (x, y, c), (x, y, 1 - c), [(1 - x, y), (x, 1 - y), (1 - x, 1 - y)]


def _remote(src, dst, sems, k, to):
    return pltpu.make_async_remote_copy(src_ref=src, dst_ref=dst, send_sem=sems[0].at[k], recv_sem=sems[1].at[k],
                                        device_id=to, device_id_type=pl.DeviceIdType.MESH)


def _all_gather(x, *, name):
    def body(x_ref, o_ref, send_sems, recv_sems, local_sem):
        me, sib, chips = _place()
        c = me[2]
        sems = (send_sems, recv_sems)
        slot = lambda px, py, pc: o_ref.at[4 * px + 2 * py + pc]
        mine = pltpu.make_async_copy(x_ref, slot(*me), local_sem)
        mine.start()
        first = [_remote(x_ref, slot(*me), sems, 0, sib)]
        first += [_remote(x_ref, slot(*me), sems, 1 + j, (*chip, c)) for j, chip in enumerate(chips)]
        for cp in first:
            cp.start()
        passed = [_remote(slot(*chip, c), slot(*chip, c), sems, 4 + j, sib) for j, chip in enumerate(chips)]
        for j, chip in enumerate(chips):
            _remote(x_ref, slot(*chip, c), sems, 1 + j, me).wait_recv()
            passed[j].start()
        _remote(x_ref, slot(*sib), sems, 0, me).wait_recv()
        for j, chip in enumerate(chips):
            _remote(x_ref, slot(*chip, 1 - c), sems, 4 + j, me).wait_recv()
        for cp in first + passed:
            cp.wait_send()
        mine.wait()

    return pl.pallas_call(
        body, name=name, out_shape=jax.ShapeDtypeStruct((N_DEV,) + x.shape, x.dtype),
        in_specs=[_HBM], out_specs=_HBM,
        scratch_shapes=[pltpu.SemaphoreType.DMA((N_DEV - 1,)), pltpu.SemaphoreType.DMA((N_DEV - 1,)), pltpu.SemaphoreType.DMA],
        compiler_params=pltpu.CompilerParams(has_side_effects=True),
    )(x)


class _Rider:
    def __init__(self, inputs, out_shapes, scratch, start, finish, post):
        self.inputs, self.out_shapes, self.scratch = list(inputs), list(out_shapes), list(scratch)
        self.start, self.finish, self.post = start, finish, post


def _run_rider(rider, *, name):
    def body(*refs):
        rider.start(refs)
        rider.finish(refs)

    outs = pl.pallas_call(
        body, name=name, out_shape=tuple(rider.out_shapes), in_specs=[_HBM] * len(rider.inputs),
        out_specs=(_HBM,) * len(rider.out_shapes), scratch_shapes=rider.scratch,
        compiler_params=pltpu.CompilerParams(has_side_effects=True),
    )(*rider.inputs)
    return rider.post(outs)


def _carry(rider, n_in, n_out, first, last):
    if rider is None:
        return [], [], [], [], [], lambda refs: refs
    ni, no = len(rider.inputs), len(rider.out_shapes)

    def split(refs):
        own_in, r_in = refs[:n_in], refs[n_in:n_in + ni]
        own_out, r_out = refs[n_in + ni:n_in + ni + n_out], refs[n_in + ni + n_out:n_in + ni + n_out + no]
        rest = refs[n_in + ni + n_out + no:]
        own_scr, r_scr = rest[:len(rest) - len(rider.scratch)], rest[len(rest) - len(rider.scratch):]
        rrefs = tuple(r_in) + tuple(r_out) + tuple(r_scr)
        pl.when(first())(lambda: rider.start(rrefs))
        pl.when(last())(lambda: rider.finish(rrefs))
        return tuple(own_in) + tuple(own_out) + tuple(own_scr)

    return list(rider.inputs), [_HBM] * ni, list(rider.out_shapes), [_HBM] * no, list(rider.scratch), split


def _gather_rider(shards, axes):
    n = len(shards)
    srcs, out_shapes, kinds = [], [], []
    for s, ax in zip(shards, axes):
        L, a, b = s.shape
        if ax == 1:
            srcs.append(s.reshape(L, 1, a, b)), out_shapes.append((L, N_DEV, a, b)), kinds.append('row')
        elif b % 128 == 0:
            srcs.append(s), out_shapes.append((L, a, N_DEV * b)), kinds.append('col')
        else:
            srcs.append(s.reshape(1, L, a, b)), out_shapes.append((N_DEV, L, a, b)), kinds.append('slot')

    def parts(refs):
        x_refs, o_refs = refs[:n], refs[n:2 * n]
        send_sems, recv_sems, local_sem = refs[2 * n:]
        me, sib, chips = _place()
        sems = (send_sems, recv_sems)

        def win(t, px, py, pc):
            idx = 4 * px + 2 * py + pc
            if kinds[t] == 'row':
                return o_refs[t].at[:, pl.ds(idx, 1)]
            if kinds[t] == 'col':
                width = shards[t].shape[2]
                return o_refs[t].at[:, :, pl.ds(pl.multiple_of(idx * width, 128), width)]
            return o_refs[t].at[pl.ds(idx, 1)]

        def group(k, block, to, own):
            return [_remote(x_refs[t] if own else win(t, *block), win(t, *block), sems, k * n + t, to) for t in range(n)]

        mine = [pltpu.make_async_copy(x_refs[t], win(t, *me), local_sem.at[t]) for t in range(n)]
        first = group(0, me, sib, True)
        for j, chip in enumerate(chips):
            first += group(1 + j, me, (*chip, me[2]), True)
        return me, sib, chips, group, mine, first

    def start(refs):
        *_, mine, first = parts(refs)
        for cp in mine + first:
            cp.start()

    def finish(refs):
        me, sib, chips, group, mine, first = parts(refs)
        c = me[2]
        passed = []
        for j, chip in enumerate(chips):
            for cp in group(1 + j, (*chip, c), me, False):
                cp.wait_recv()
            fwd = group(4 + j, (*chip, c), sib, False)
            for cp in fwd:
                cp.start()
            passed += fwd
        for cp in group(0, sib, me, False):
            cp.wait_recv()
        for j, chip in enumerate(chips):
            for cp in group(4 + j, (*chip, 1 - c), me, False):
                cp.wait_recv()
        for cp in first + passed:
            cp.wait_send()
        for cp in mine:
            cp.wait()

    def post(outs):
        whole = []
        for o, s, kind in zip(outs, shards, kinds):
            L, a, b = s.shape
            whole.append(o.reshape(L, N_DEV * a, b) if kind == 'row' else o if kind == 'col' else _to_whole(o, 2))
        return whole

    return _Rider(srcs, [jax.ShapeDtypeStruct(shp, s.dtype) for shp, s in zip(out_shapes, shards)],
                  [pltpu.SemaphoreType.DMA(((N_DEV - 1) * n,)), pltpu.SemaphoreType.DMA(((N_DEV - 1) * n,)),
                   pltpu.SemaphoreType.DMA((n,))], start, finish, post)


def _sibling_swap(x, *, name):
    def body(x_ref, o_ref, send_sems, recv_sems):
        me, sib, _ = _place()
        c = me[2]
        sems = (send_sems, recv_sems)
        sends = [_remote(x_ref.at[j, 1 - c], o_ref.at[j], sems, j, sib) for j in range(N_CHIP)]
        for cp in sends:
            cp.start()
        for cp in sends:
            cp.wait_send()
            cp.wait_recv()

    return pl.pallas_call(
        body, name=name, out_shape=jax.ShapeDtypeStruct((N_CHIP,) + x.shape[2:], x.dtype),
        in_specs=[_HBM], out_specs=_HBM,
        scratch_shapes=[pltpu.SemaphoreType.DMA((N_CHIP,)), pltpu.SemaphoreType.DMA((N_CHIP,))],
        compiler_params=pltpu.CompilerParams(has_side_effects=True),
    )(x)


def _pair_sum(x, got, c, *, name):
    _, _, R, _ = x.shape
    tr = _pick(R, (1024, 512, 256, 128, 64, 32, 16, 8))

    def body(c_ref, x_ref, g_ref, o_ref):
        o_ref[...] = (x_ref[...].astype(F32) + g_ref[...].astype(F32)).astype(o_ref.dtype)

    return pl.pallas_call(
        body, name=name, out_shape=jax.ShapeDtypeStruct((N_CHIP, R, 128), x.dtype),
        grid_spec=pltpu.PrefetchScalarGridSpec(
            num_scalar_prefetch=1, grid=(N_CHIP, R // tr),
            in_specs=[pl.BlockSpec((None, None, tr, 128), lambda j, i, c_ref: (j, c_ref[0], i, 0)),
                      pl.BlockSpec((None, tr, 128), lambda j, i, c_ref: (j, i, 0))],
            out_specs=pl.BlockSpec((None, tr, 128), lambda j, i, c_ref: (j, i, 0))),
        compiler_params=_params(("parallel", "parallel")),
    )(c, x, got)


def _chip_all_to_all_rider(x):
    def parts(refs):
        x_ref, o_ref, send_sems, recv_sems, local_sem = refs
        me, _, chips = _place()
        sems = (send_sems, recv_sems)
        mine = 2 * me[0] + me[1]
        local = pltpu.make_async_copy(x_ref.at[mine], o_ref.at[mine], local_sem)
        sends = [_remote(x_ref.at[2 * px + py], o_ref.at[mine], sems, j, (px, py, me[2])) for j, (px, py) in enumerate(chips)]
        arrival = lambda j: _remote(x_ref.at[mine], o_ref.at[2 * chips[j][0] + chips[j][1]], sems, j, me)
        return local, sends, arrival

    def start(refs):
        local, sends, _ = parts(refs)
        for cp in [local] + sends:
            cp.start()

    def finish(refs):
        local, sends, arrival = parts(refs)
        for j, cp in enumerate(sends):
            cp.wait_send()
            arrival(j).wait_recv()
        local.wait()

    return _Rider([x], [jax.ShapeDtypeStruct(x.shape, x.dtype)],
                  [pltpu.SemaphoreType.DMA((N_CHIP - 1,)), pltpu.SemaphoreType.DMA((N_CHIP - 1,)), pltpu.SemaphoreType.DMA],
                  start, finish, lambda outs: outs[0])


def _sum_slots(x, *, name):
    n, R, _ = x.shape
    tr = _pick(R, (1024, 512, 256, 128, 64, 32, 16, 8))

    def body(x_ref, o_ref):
        acc = x_ref[0].astype(F32)
        for j in range(1, n):
            acc = acc + x_ref[j].astype(F32)
        o_ref[...] = acc

    return pl.pallas_call(
        body, name=name, out_shape=jax.ShapeDtypeStruct((R, 128), F32), grid=(R // tr,),
        in_specs=[pl.BlockSpec((n, tr, 128), lambda i: (0, i, 0))], out_specs=pl.BlockSpec((tr, 128), lambda i: (i, 0)),
        compiler_params=_params(("parallel",)),
    )(x)


def _adamw(w, g, m, v, *, name):
    shape = w.shape
    cols = shape[-1]
    rows = int(np.prod(shape[:-1]))
    tr = next((t for t in (1024, 512, 256, 128, 64, 32, 16, 8) if rows % t == 0 and t * cols * 4 <= (1 << 20)), rows)

    def body(w_ref, g_ref, m_ref, v_ref, d_ref, mo_ref, vo_ref):
        g_ = g_ref[...]
        m_ = ADAM_B1 * m_ref[...] + (1.0 - ADAM_B1) * g_
        v_ = ADAM_B2 * v_ref[...] + (1.0 - ADAM_B2) * jnp.square(g_)
        m_hat = m_ / (1.0 - ADAM_B1 ** ADAM_STEP)
        v_hat = v_ / (1.0 - ADAM_B2 ** ADAM_STEP)
        d_ref[...] = -ADAM_LR * (m_hat / (jnp.sqrt(v_hat) + ADAM_EPS) + ADAM_WD * w_ref[...])
        mo_ref[...] = m_
        vo_ref[...] = v_

    blk = pl.BlockSpec((tr, cols), lambda i: (i, 0))
    outs = pl.pallas_call(
        body, name=name, out_shape=tuple(jax.ShapeDtypeStruct((rows, cols), F32) for _ in range(3)), grid=(rows // tr,),
        in_specs=[blk] * 4, out_specs=(blk,) * 3, compiler_params=_params(("parallel",)),
    )(*(a.reshape(rows, cols) for a in (w, g, m, v)))
    return tuple(o.reshape(shape) for o in outs)


_WEIGHTS = ('g_mix', 'w_in', 'b_fox_forget', 'w_gla_gate', 'b_gla_gate', 'g_gla_out', 'g_mla_q', 'w_mla_uq', 'g_mla_kv',
            'w_mla_ukv', 'b_branch_gate', 'w_up_fox', 'w_up_gla', 'w_up_mla', 'w_out', 'g_xa', 'g_mem', 'w_xq', 'w_xkv',
            'w_xo', 'g_mlp', 'w_mlp1', 'w_mlp2', 'g_final')
_SHARDED = (('w_in', 1), ('w_gla_gate', 2), ('w_mla_uq', 2), ('w_mla_ukv', 2), ('w_up_fox', 2), ('w_up_gla', 2),
            ('w_up_mla', 2), ('w_out', 1), ('w_xq', 1), ('w_xkv', 1), ('w_xo', 2), ('w_mlp1', 2), ('w_mlp2', 1))
_REPLICATED = tuple(n for n in _WEIGHTS if n not in dict(_SHARDED))
_ROW_PAD = 1024
_SMALL_ROW_PAD = 8
_PIECE_ROWS = 16


def _pack(flats, lead, row_pad=_ROW_PAD):
    if all(int(np.prod(a.shape[lead:])) % 128 == 0 for a in flats):
        def block(a):
            a = a.reshape(a.shape[:lead] + (-1, 128))
            return jnp.pad(a, [(0, 0)] * lead + [(0, -a.shape[lead] % _PIECE_ROWS), (0, 0)])
        cat = jnp.concatenate([block(a) for a in flats], axis=lead)
        rows = cat.shape[lead]
        return jnp.pad(cat, [(0, 0)] * lead + [(0, -(-rows // row_pad) * row_pad - rows), (0, 0)])
    cat = jnp.concatenate([a.reshape(a.shape[:lead] + (-1,)) for a in flats], axis=-1)
    n = cat.shape[-1]
    total = -(-n // (128 * row_pad)) * (128 * row_pad)
    cat = jnp.pad(cat, [(0, 0)] * lead + [(0, total - n)])
    return cat.reshape(cat.shape[:lead] + (total // 128, 128))


def _unpack(buf, shapes, lead):
    sizes = [int(np.prod(shp)) for shp in shapes]
    out, off = [], 0
    if all(n % 128 == 0 for n in sizes):
        for shp, n in zip(shapes, sizes):
            rows = buf[(slice(None),) * lead + (slice(off, off + n // 128),)]
            out.append(rows.reshape(buf.shape[:lead] + tuple(shp)))
            off += -(-(n // 128) // _PIECE_ROWS) * _PIECE_ROWS
        return out
    flat = buf.reshape(buf.shape[:lead] + (-1,))
    for shp, n in zip(shapes, sizes):
        out.append(flat[..., off:off + n].reshape(buf.shape[:lead] + tuple(shp)))
        off += n
    return out


def _to_whole(g, axis):
    if axis == 1:
        return g.transpose(1, 0, 2, 3).reshape(g.shape[1], N_DEV * g.shape[2], g.shape[3])
    return g.transpose(1, 2, 0, 3).reshape(g.shape[1], g.shape[2], N_DEV * g.shape[3])


def _to_shards(w, axis):
    L, R, C = w.shape
    if axis == 1:
        return w.reshape(L, N_DEV, R // N_DEV, C).transpose(1, 0, 2, 3)
    return w.reshape(L, R, N_DEV, C // N_DEV).transpose(2, 0, 1, 3)


def kernel(x, mem, g_mix, w_in, b_fox_forget, w_gla_gate, b_gla_gate, g_gla_out, g_mla_q, w_mla_uq, g_mla_kv, w_mla_ukv, b_branch_gate, w_up_fox, w_up_gla, w_up_mla, w_out, g_xa, g_mem, w_xq, w_xkv, w_xo, g_mlp, w_mlp1, w_mlp2, g_final, loss_target, m_g_mix, m_w_in, m_b_fox_forget, m_w_gla_gate, m_b_gla_gate, m_g_gla_out, m_g_mla_q, m_w_mla_uq, m_g_mla_kv, m_w_mla_ukv, m_b_branch_gate, m_w_up_fox, m_w_up_gla, m_w_up_mla, m_w_out, m_g_xa, m_g_mem, m_w_xq, m_w_xkv, m_w_xo, m_g_mlp, m_w_mlp1, m_w_mlp2, m_g_final, v_g_mix, v_w_in, v_b_fox_forget, v_w_gla_gate, v_b_gla_gate, v_g_gla_out, v_g_mla_q, v_w_mla_uq, v_g_mla_kv, v_w_mla_ukv, v_b_branch_gate, v_w_up_fox, v_w_up_gla, v_w_up_mla, v_w_out, v_g_xa, v_g_mem, v_w_xq, v_w_xkv, v_w_xo, v_g_mlp, v_w_mlp1, v_w_mlp2, v_g_final):
    wts = dict(zip(_WEIGHTS, (g_mix, w_in, b_fox_forget, w_gla_gate, b_gla_gate, g_gla_out, g_mla_q, w_mla_uq, g_mla_kv,
                              w_mla_ukv, b_branch_gate, w_up_fox, w_up_gla, w_up_mla, w_out, g_xa, g_mem, w_xq, w_xkv, w_xo,
                              g_mlp, w_mlp1, w_mlp2, g_final)))
    mom1 = dict(zip(_WEIGHTS, (m_g_mix, m_w_in, m_b_fox_forget, m_w_gla_gate, m_b_gla_gate, m_g_gla_out, m_g_mla_q,
                               m_w_mla_uq, m_g_mla_kv, m_w_mla_ukv, m_b_branch_gate, m_w_up_fox, m_w_up_gla, m_w_up_mla,
                               m_w_out, m_g_xa, m_g_mem, m_w_xq, m_w_xkv, m_w_xo, m_g_mlp, m_w_mlp1, m_w_mlp2, m_g_final)))
    mom2 = dict(zip(_WEIGHTS, (v_g_mix, v_w_in, v_b_fox_forget, v_w_gla_gate, v_b_gla_gate, v_g_gla_out, v_g_mla_q,
                               v_w_mla_uq, v_g_mla_kv, v_w_mla_ukv, v_b_branch_gate, v_w_up_fox, v_w_up_gla, v_w_up_mla,
                               v_w_out, v_g_xa, v_g_mem, v_w_xq, v_w_xkv, v_w_xo, v_g_mlp, v_w_mlp1, v_w_mlp2, v_g_final)))
    depth = g_mix.shape[0]

    names = [n for n, _ in _SHARDED]
    axes = dict(_SHARDED)
    shard = {n: wts[n] for n in names}
    shard['w_in'] = _pad_w_in(w_in)
    rep = {n: wts[n] for n in _REPLICATED}
    ps = [_LayerParams(rep, l) for l in range(depth)]

    def gather(group, l):
        rider = _gather_rider([shard[n][l:l + 1].astype(BF16) for n in group], [axes[n] for n in group])
        return rider, lambda whole: ps[l].w.update({n: w[0] for n, w in zip(group, whole)})

    first, sink = gather(['w_in'], 0)
    sink(_run_rider(first, name="gather_w_in_0"))
    hooks = {(0, 'fox_fwd'): gather([n for n in names if n != 'w_in'], 0)}
    for l in range(1, depth):
        hooks[(l - 1, 'mla_fwd')] = gather(names, l)

    core = lax.axis_index("c").astype(jnp.int32).reshape(1)
    late = ['w_in', 'w_gla_gate', 'w_mla_uq', 'w_mla_ukv']
    groups = {'early': [n for n in names if n not in late], 'late': late}
    small_grads, landed = {}, {}

    def exchange(l, g, which):
        slots = _pack([_to_shards(g[n][None], axes[n]).astype(BF16) for n in groups[which]], 1)
        slots = slots.reshape((N_CHIP, 2) + slots.shape[1:])
        paired = _pair_sum(slots, _sibling_swap(slots, name=f"swap_grads_{which}_{l}"), core, name=f"pair_grads_{which}_{l}")
        return _chip_all_to_all_rider(paired), lambda got: landed.update({(l, which): got})

    def half_done(l, g):
        hooks[(l, 'mla_bwd')] = exchange(l, g, 'early')

    def layer_done(l, g):
        small_grads[l] = g
        rider, sink = exchange(l, g, 'late')
        if l > 0:
            hooks[(l - 1, 'fox_bwd')] = (rider, sink)
        else:
            sink(_run_rider(rider, name=f"scatter_grads_late_{l}"))

    loss, dx, dg_final = _local_step(x[0], mem[0], loss_target[0], ps, g_final, hooks, half_done, layer_done)
    loss = lax.psum(loss[0, 0], _MESH_AXES)

    grad = {}
    for which, group in groups.items():
        shapes = [(1,) + shard[n].shape[1:] for n in group]
        per_layer = [_unpack(_sum_slots(landed[(l, which)], name=f"sum_grads_{which}_{l}"), shapes, 0) for l in range(depth)]
        grad.update({n: jnp.concatenate([per_layer[l][i] for l in range(depth)], axis=0) for i, n in enumerate(group)})
    grad['w_in'] = _unpad_w_in(grad['w_in'])
    grads = small_grads
    small = [dg_final if n == 'g_final' else jnp.stack([grads[l][n] for l in range(depth)]) for n in _REPLICATED]
    small_shapes = [wts[n].shape for n in _REPLICATED]
    small_sum = _sum_slots(_all_gather(_pack(small, 0, _SMALL_ROW_PAD), name="gather_small_grads"), name="sum_small_grads")
    grad.update(dict(zip(_REPLICATED, _unpack(small_sum, small_shapes, 0))))

    delta, new_m, new_v = {}, {}, {}
    for n, _ in _SHARDED:
        delta[n], new_m[n], new_v[n] = _adamw(wts[n], grad[n], mom1[n], mom2[n], name=f"adamw_{n}")
    packed = [_pack([d[n] for n in _REPLICATED], 0, _SMALL_ROW_PAD) for d in (wts, mom1, mom2)]
    outs = _adamw(packed[0], small_sum, packed[1], packed[2], name="adamw_small")
    for d, o in zip((delta, new_m, new_v), outs):
        d.update(dict(zip(_REPLICATED, _unpack(o, small_shapes, 0))))

    return (loss, dx[None], *[grad[n] for n in _WEIGHTS], *[delta[n] for n in _WEIGHTS],
            *[new_m[n] for n in _WEIGHTS], *[new_v[n] for n in _WEIGHTS])
```

```python
import jax
import jax.numpy as jnp
import numpy as np
from jax import lax
from jax.experimental import pallas as pl
from jax.experimental.pallas import tpu as pltpu

F32 = jnp.float32
BF16 = jnp.bfloat16

EPS = 1e-6
CHUNK = 64
FOX_HEADS, FOX_HD = 4, 64
GLA_HEADS, GLA_DK, GLA_DV, GLA_RANK, GLA_TAU = 4, 64, 128, 16, 16.0
MLA_HEADS, MLA_Q_RANK, MLA_KV_RANK, MLA_NOPE, MLA_ROPE, MLA_VD = 4, 256, 128, 64, 32, 64
ROPE_BASE = 10000.0
XA_HEADS, XA_HD = 4, 128
ADAM_LR, ADAM_B1, ADAM_B2, ADAM_EPS, ADAM_WD, ADAM_STEP = 0.001, 0.9, 0.999, 1e-08, 0.01, 10

N_DEV = 8
V7X_VMEM_LIMIT = 56 * 1024 * 1024
NEG = -1e30

NN = ((1,), (0,))
NT = ((1,), (1,))
TN = ((0,), (0,))


def _dot(a, b, dims):
    return lax.dot_general(a.astype(BF16), b.astype(BF16), (dims, ((), ())), preferred_element_type=F32)


@jax.custom_vjp
def bdot(a, b):
    return _dot(a, b, NN)


bdot.defvjp(lambda a, b: (_dot(a, b, NN), (a, b)),
            lambda res, g: (_dot(g, res[1], NT), _dot(res[0], g, TN)))


@jax.custom_vjp
def bdot_nt(a, b):
    return _dot(a, b, NT)


bdot_nt.defvjp(lambda a, b: (_dot(a, b, NT), (a, b)),
               lambda res, g: (_dot(g, res[1], NN), _dot(g, res[0], TN)))


@jax.custom_vjp
def bdot_tn(a, b):
    return _dot(a, b, TN)


bdot_tn.defvjp(lambda a, b: (_dot(a, b, TN), (a, b)),
               lambda res, g: (_dot(res[1], g, NT), _dot(res[0], g, NN)))


def _split2(x):
    hi = x.astype(BF16)
    lo = (x - hi.astype(F32)).astype(BF16)
    return hi, lo


def _tri(n, lower):
    r = lax.broadcasted_iota(jnp.int32, (n, n), 0)
    c = lax.broadcasted_iota(jnp.int32, (n, n), 1)
    return jnp.where((r >= c) if lower else (r <= c), 1.0, 0.0).astype(BF16)


def _log_sigmoid(x):
    return jnp.minimum(x, 0.0) - jnp.log(1.0 + jnp.exp(-jnp.abs(x)))


def _sigmoid(x):
    return 1.0 / (1.0 + jnp.exp(-x))


def _rms(x, g):
    return x * lax.rsqrt(jnp.mean(x * x, axis=-1, keepdims=True) + EPS) * g


def _pick(dim, prefs):
    for p in prefs:
        if dim % p == 0:
            return p
    return dim


def _params(sem):
    return pltpu.CompilerParams(dimension_semantics=sem, vmem_limit_bytes=V7X_VMEM_LIMIT)


def _rms_vjp(x, g, dy, dres):
    rstd = lax.rsqrt(jnp.mean(x * x, axis=-1, keepdims=True) + EPS)
    xh = x * rstd
    gdy = dy * g
    dx = (gdy - xh * jnp.mean(gdy * xh, axis=-1, keepdims=True)) * rstd
    return (dx if dres is None else dx + dres), jnp.sum(dy * xh, axis=0, keepdims=True)


def _mm(a, b, *, mode, out_dtype, name, act=None, residual=None, drelu_of=None, norm_bwd=None, b_cols=None,
        tm=None, tn=None, tk=None):
    b_off, b_width = b_cols or (0, b.shape[1])
    if mode == 'nn':
        (M, K), N = a.shape, b_width
    elif mode == 'nt':
        (M, K), N = a.shape, b.shape[0]
    else:
        (K, M), N = a.shape, b_width
    tm = tm or _pick(M, (1024, 512, 256, 128))
    tn = tn or _pick(N, (1024, 1920, 1152, 768, 640, 512, 384, 256, 128))
    tk = tk or _pick(K, (1024, 1920, 1152, 640, 512, 256, 128))
    nk = K // tk
    dims = {'nn': NN, 'nt': NT, 'tn': TN}[mode]
    a_spec = pl.BlockSpec((tk, tm), lambda i, j, k: (k, i)) if mode == 'tn' else pl.BlockSpec((tm, tk), lambda i, j, k: (i, k))
    if mode == 'nt':
        b_spec = pl.BlockSpec((tn, tk), lambda i, j, k, o=b_off // tk: (j, k + o))
    else:
        b_spec = pl.BlockSpec((tk, tn), lambda i, j, k, o=b_off // tn: (k, j + o))
    o_spec = pl.BlockSpec((tm, tn), lambda i, j, k: (i, j))
    extra = [e for e in (residual, drelu_of) if e is not None]
    extra_specs = [o_spec] * len(extra)
    out_shape, out_specs, n_out = jax.ShapeDtypeStruct((M, N), out_dtype), o_spec, 1
    if norm_bwd is not None:
        x_in, g_in, dres_in = norm_bwd
        assert tn == N and residual is None and drelu_of is None
        vec = pl.BlockSpec((1, N), lambda i, j, k: (0, 0))
        extra, extra_specs = [x_in, g_in.reshape(1, N), dres_in], [o_spec, vec, o_spec]
        out_shape = (jax.ShapeDtypeStruct((M, N), F32), jax.ShapeDtypeStruct((M, N), BF16), jax.ShapeDtypeStruct((1, N), F32))
        out_specs, n_out = (o_spec, o_spec, vec), 3

    def body(a_ref, b_ref, *rest):
        o_ref = rest[len(extra)]
        first_rows = pl.program_id(0) == 0
        at = a_ref[...]
        if act == 'relu2':
            at = jnp.square(jnp.maximum(at.astype(F32), 0.0))
        part = _dot(at, b_ref[...], dims)

        def finish(acc):
            if norm_bwd is not None:
                dx, dg = _rms_vjp(rest[0][...], rest[1][...], acc, rest[2][...])
                o_ref[...] = dx
                rest[len(extra) + 1][...] = dx.astype(BF16)
                dg_ref = rest[len(extra) + 2]

                @pl.when(first_rows)
                def _():
                    dg_ref[...] = dg

                @pl.when(jnp.logical_not(first_rows))
                def _():
                    dg_ref[...] += dg
                return
            idx = 0
            if residual is not None:
                acc = acc + rest[idx][...]
                idx += 1
            if drelu_of is not None:
                acc = acc * (2.0 * jnp.maximum(rest[idx][...].astype(F32), 0.0))
            o_ref[...] = acc.astype(out_dtype)

        if nk == 1:
            finish(part)
        else:
            acc_ref = rest[len(extra) + n_out]
            k = pl.program_id(2)

            @pl.when(k == 0)
            def _():
                acc_ref[...] = part

            @pl.when(k > 0)
            def _():
                acc_ref[...] += part

            @pl.when(k == nk - 1)
            def _():
                finish(acc_ref[...])

    return pl.pallas_call(
        body, name=name,
        out_shape=out_shape,
        grid=(M // tm, N // tn, nk),
        in_specs=[a_spec, b_spec] + extra_specs,
        out_specs=out_specs,
        scratch_shapes=[] if nk == 1 else [pltpu.VMEM((tm, tn), F32)],
        compiler_params=_params(("arbitrary" if norm_bwd is not None else "parallel", "parallel", "arbitrary")),
    )(a, b, *extra)


def _rms_fwd(x, g, *, name, out_dtype=BF16):
    S, D = x.shape
    tr = _pick(S, (512, 256, 128))

    def body(x_ref, g_ref, o_ref):
        o_ref[...] = _rms(x_ref[...], g_ref[...]).astype(out_dtype)

    return pl.pallas_call(
        body, name=name, out_shape=jax.ShapeDtypeStruct((S, D), out_dtype), grid=(S // tr,),
        in_specs=[pl.BlockSpec((tr, D), lambda i: (i, 0)), pl.BlockSpec((1, D), lambda i: (0, 0))],
        out_specs=pl.BlockSpec((tr, D), lambda i: (i, 0)),
        compiler_params=_params(("parallel",)),
    )(x, g.reshape(1, D))


def _rms_bwd(x, g, dy, dres, *, name):
    S, D = x.shape
    tr = _pick(S, (512, 256, 128))

    def body(x_ref, g_ref, dy_ref, *rest):
        dx_ref, dxb_ref, dg_ref = rest[-3], rest[-2], rest[-1]
        dx, part = _rms_vjp(x_ref[...], g_ref[...], dy_ref[...].astype(F32), None if dres is None else rest[0][...])
        dx_ref[...] = dx
        dxb_ref[...] = dx.astype(BF16)

        @pl.when(pl.program_id(0) == 0)
        def _():
            dg_ref[...] = part

        @pl.when(pl.program_id(0) > 0)
        def _():
            dg_ref[...] += part

    row = pl.BlockSpec((tr, D), lambda i: (i, 0))
    vec = pl.BlockSpec((1, D), lambda i: (0, 0))
    ins = [x, g.reshape(1, D), dy] + ([dres] if dres is not None else [])
    return pl.pallas_call(
        body, name=name,
        out_shape=(jax.ShapeDtypeStruct((S, D), F32), jax.ShapeDtypeStruct((S, D), BF16), jax.ShapeDtypeStruct((1, D), F32)),
        grid=(S // tr,),
        in_specs=[row, vec, row] + ([row] if dres is not None else []),
        out_specs=(row, row, vec),
        compiler_params=_params(("arbitrary",)),
    )(*ins)


def _loss_head(x, g, target, *, name):
    S, D = x.shape
    tr = _pick(S, (512, 256, 128))

    def body(x_ref, g_ref, t_ref, l_ref, dx_ref, dxb_ref, dg_ref):
        x_ = x_ref[...]
        g_ = g_ref[...]
        rstd = lax.rsqrt(jnp.mean(x_ * x_, axis=-1, keepdims=True) + EPS)
        xh = x_ * rstd
        err = xh * g_ - t_ref[...]
        lpart = (0.5 / D) * jnp.sum(jnp.sum(err * err, axis=-1, keepdims=True), axis=0, keepdims=True)
        dy = err * (1.0 / D)
        gdy = dy * g_
        dx = (gdy - xh * jnp.mean(gdy * xh, axis=-1, keepdims=True)) * rstd
        dx_ref[...] = dx
        dxb_ref[...] = dx.astype(BF16)
        gpart = jnp.sum(dy * xh, axis=0, keepdims=True)

        @pl.when(pl.program_id(0) == 0)
        def _():
            dg_ref[...] = gpart
            l_ref[...] = lpart

        @pl.when(pl.program_id(0) > 0)
        def _():
            dg_ref[...] += gpart
            l_ref[...] += lpart

    row = pl.BlockSpec((tr, D), lambda i: (i, 0))
    vec = pl.BlockSpec((1, D), lambda i: (0, 0))
    return pl.pallas_call(
        body, name=name,
        out_shape=(jax.ShapeDtypeStruct((1, 1), F32), jax.ShapeDtypeStruct((S, D), F32), jax.ShapeDtypeStruct((S, D), BF16),
                   jax.ShapeDtypeStruct((1, D), F32)),
        grid=(S // tr,),
        in_specs=[row, vec, row],
        out_specs=(pl.BlockSpec((1, 1), lambda i: (0, 0)), row, row, vec),
        compiler_params=_params(("arbitrary",)),
    )(x, g.reshape(1, D), target)


def _mask_of(mask, tq, tk):
    qpos = lax.broadcasted_iota(jnp.int32, (tq, tk), 0)
    kpos = lax.broadcasted_iota(jnp.int32, (tq, tk), 1)
    if mask == 'causal':
        return kpos <= qpos
    return kpos <= (qpos | (CHUNK - 1))


LANES = 128
LOG2E = 1.4426950408889634


def _lane_group(j, w, width):
    lane = lax.broadcasted_iota(jnp.int32, (1, width), 1)
    return (lane >= j * w) & (lane < (j + 1) * w)


def _only(x, j, w):
    if w == x.shape[1]:
        return x
    return jnp.where(_lane_group(j, w, x.shape[1]), x, jnp.zeros_like(x))


def _per_head(cols, w):
    out = cols[-1]
    for j in range(len(cols) - 2, -1, -1):
        out = jnp.where(_lane_group(j, w, LANES), cols[j], out)
    return out


def _side_by_side(xs):
    return xs[0] if len(xs) == 1 else jnp.concatenate(xs, axis=1)


def _on_top(xs):
    return xs[0] if len(xs) == 1 else jnp.concatenate(xs, axis=0)


def _stacked(x, hp, w):
    return _on_top([_only(x, j, w) for j in range(hp)])


def _col_block(entry, rows, idx):
    arr, off, width = entry
    return pl.BlockSpec((rows, width), lambda i, j, o=off // width: (idx(i, j), o))


def _attn_fwd(qk, v, H, cq, ck, *, scale, mask, name, rider=None):
    Sq, Sk = qk[0][0][0].shape[0], v[0].shape[0]
    dv = v[2] // H
    w0 = qk[0][2]
    hp = LANES // w0
    G = H // hp
    assert dv == w0 and not qk[0][3] and all(sh and H * w == LANES for _, _, w, sh in qk[1:])
    tq = _pick(Sq, (512, 256, 128))
    tk = tq if mask else _pick(Sk, (512, 256, 128))
    nq, nk = Sq // tq, Sk // tk
    bias = cq is not None
    npart = len(qk)

    def body(*refs):
        refs = split(refs)
        q_refs, k_refs = refs[0:2 * npart:2], refs[1:2 * npart:2]
        v_ref = refs[2 * npart]
        cq_ref, ck_ref = (refs[2 * npart + 1], refs[2 * npart + 2]) if bias else (None, None)
        o_ref, lse_ref, m_s, l_s, acc_s = refs[-5:]
        qi, ki = pl.program_id(0), pl.program_id(1)

        @pl.when(ki == 0)
        def _():
            m_s[...] = jnp.full(m_s.shape, NEG, F32)
            l_s[...] = jnp.zeros(l_s.shape, F32)
            acc_s[...] = jnp.zeros(acc_s.shape, F32)

        def compute(masked):
            keep = _mask_of(mask, tq, tk) if masked else None
            for g in range(G):
                lanes = slice(g * LANES, (g + 1) * LANES)
                q128, k128, v128 = q_refs[0][:, lanes], k_refs[0][:, lanes], v_ref[:, lanes]
                ps, alphas = [], []
                extras = list(zip(qk, q_refs, k_refs))[1:]
                k_all = _side_by_side([k128] + [k_ref[...] for _, _, k_ref in extras])
                for j in range(hp):
                    h = g * hp + j
                    q_all = _side_by_side([_only(q128, j, w0)] + [_only(q_ref[...], h, w) for (_, _, w, _), q_ref, _ in extras])
                    s = _dot(q_all, k_all, NT) * scale
                    if bias:
                        s = s + (cq_ref[:, h:h + 1] - ck_ref[h:h + 1, :])
                    if masked:
                        s = jnp.where(keep, s, NEG)
                    m_prev = m_s[h]
                    m_new = jnp.maximum(m_prev, jnp.max(s, axis=1, keepdims=True))
                    alpha = jnp.exp(m_prev - m_new)
                    p = jnp.exp(s - m_new)
                    l_s[h] = alpha * l_s[h] + jnp.sum(p, axis=1, keepdims=True)
                    m_s[h] = m_new
                    ps.append(p.astype(BF16))
                    alphas.append(alpha)
                acc_s[g] = _per_head(alphas, w0) * acc_s[g] + _dot(_side_by_side(ps), _stacked(v128, hp, w0), NN)

        if mask is None:
            compute(False)
        else:
            pl.when(ki < qi)(lambda: compute(False))
            pl.when(ki == qi)(lambda: compute(True))

        @pl.when(ki == ((nk - 1) if mask is None else qi))
        def _():
            lse_ref[...] = jnp.zeros(lse_ref.shape, F32)
            for g in range(G):
                o_ref[:, g * LANES:(g + 1) * LANES] = (
                    acc_s[g] / _per_head([l_s[g * hp + j] for j in range(hp)], w0)).astype(BF16)
            for h in range(H):
                lse_ref[:, h:h + 1] = m_s[h] + jnp.log(l_s[h])

    q_idx = lambda i, j: i
    k_idx = (lambda i, j: jnp.minimum(i, j)) if mask else (lambda i, j: j)
    ins, in_specs = [], []
    for q_e, k_e, _, _ in qk:
        ins += [q_e[0], k_e[0]]
        in_specs += [_col_block(q_e, tq, q_idx), _col_block(k_e, tk, k_idx)]
    ins.append(v[0])
    in_specs.append(_col_block(v, tk, k_idx))
    if bias:
        in_specs += [pl.BlockSpec((tq, 8), lambda i, j: (i, 0)), pl.BlockSpec((8, tk), lambda i, j: (0, k_idx(i, j)))]
        ins += [cq, ck]
    r_ins, r_in_specs, r_outs, r_out_specs, r_scratch, split = _carry(
        rider, len(ins), 2, lambda: (pl.program_id(0) == 0) & (pl.program_id(1) == 0),
        lambda: (pl.program_id(0) == nq - 1) & (pl.program_id(1) == nk - 1))
    res = pl.pallas_call(
        body, name=name,
        out_shape=(jax.ShapeDtypeStruct((Sq, H * dv), BF16), jax.ShapeDtypeStruct((Sq, 8), F32), *r_outs),
        grid=(nq, nk), in_specs=in_specs + r_in_specs,
        out_specs=(pl.BlockSpec((tq, H * dv), lambda i, j: (i, 0)), pl.BlockSpec((tq, 8), lambda i, j: (i, 0)), *r_out_specs),
        scratch_shapes=[pltpu.VMEM((H, tq, 1), F32), pltpu.VMEM((H, tq, 1), F32), pltpu.VMEM((G, tq, LANES), F32)] + r_scratch,
        compiler_params=_params(("arbitrary", "arbitrary")) if rider else _params(("parallel", "arbitrary")),
    )(*ins, *r_ins)
    return (res[0], res[1], rider.post(res[2:])) if rider else res


def _attn_bwd(qk, v, H, o, do, lse, cq, ck, *, scale, mask, name, rider=None):
    Sq, Sk = qk[0][0][0].shape[0], v[0].shape[0]
    dv = v[2] // H
    w0 = qk[0][2]
    hp = LANES // w0
    G = H // hp
    tq = _pick(Sq, (512, 256, 128))
    tk = tq if mask else _pick(Sk, (512, 256, 128))
    nq, nk = Sq // tq, Sk // tk
    bias = cq is not None
    npart = len(qk)
    n_in = 2 * npart + 4 + (2 if bias else 0)

    def body(*refs):
        refs = split(refs)
        q_refs, k_refs = refs[0:2 * npart:2], refs[1:2 * npart:2]
        v_ref, o_ref, do_ref, lse_ref = refs[2 * npart:2 * npart + 4]
        cq_ref, ck_ref = (refs[2 * npart + 4], refs[2 * npart + 5]) if bias else (None, None)
        outs = refs[n_in:]
        dq_refs, dk_refs, dv_ref = outs[:npart], outs[npart:2 * npart], outs[2 * npart]
        dck_ref, dcq_ref = (outs[2 * npart + 1], outs[2 * npart + 2]) if bias else (None, None)
        dk_accs, dv_acc = refs[-(npart + 1):-1], refs[-1]
        ki, qi = pl.program_id(0), pl.program_id(1)
        first_q = ki if mask else 0

        @pl.when((ki == 0) & (qi == 0))
        def _():
            for r in dq_refs:
                r[...] = jnp.zeros(r.shape, F32)
            if bias:
                dcq_ref[...] = jnp.zeros(dcq_ref.shape, F32)

        @pl.when(qi == first_q)
        def _():
            for r in dk_accs:
                r[...] = jnp.zeros(r.shape, F32)
            dv_acc[...] = jnp.zeros(dv_acc.shape, F32)
            if bias:
                dck_ref[...] = jnp.zeros(dck_ref.shape, F32)

        def compute(masked):
            keep = _mask_of(mask, tq, tk) if masked else None
            rows = pl.ds(pl.multiple_of(qi * tq, tq), tq)
            extras = list(zip(qk, q_refs, k_refs, dq_refs, dk_accs))[1:]
            for g in range(G):
                lanes = slice(g * LANES, (g + 1) * LANES)
                q128, k128, v128 = q_refs[0][:, lanes], k_refs[0][:, lanes], v_ref[:, lanes]
                do128, o128 = do_ref[:, lanes], o_ref[:, lanes]
                ps, dss = [], []
                k_all = _side_by_side([k128] + [e[2][...] for e in extras])
                for j in range(hp):
                    h = g * hp + j
                    q_all = _side_by_side([_only(q128, j, w0)] + [_only(e[1][...], h, e[0][2]) for e in extras])
                    s = _dot(q_all, k_all, NT) * (scale * LOG2E)
                    if bias:
                        s = s - ck_ref[h:h + 1, :] * LOG2E
                    if masked:
                        s = jnp.where(keep, s, NEG)
                    row = lse_ref[:, h:h + 1] - cq_ref[:, h:h + 1] if bias else lse_ref[:, h:h + 1]
                    p = jnp.exp2(s - row * LOG2E)
                    doh = _only(do128, j, w0)
                    dp = _dot(doh, v128, NT)
                    delta = jnp.sum(doh.astype(F32) * o128.astype(F32), axis=1, keepdims=True)
                    ds = p * (dp - delta)
                    if bias:
                        dck_ref[h:h + 1, :] -= jnp.sum(ds, axis=0, keepdims=True)
                        dcq_ref[rows, h:h + 1] += jnp.sum(ds, axis=1, keepdims=True)
                    ps.append(p.astype(BF16))
                    dss.append((ds * scale).astype(BF16))
                for (_, _, w, _), q_ref, k_ref, dq_ref, dk_acc in extras:
                    heads = range(g * hp, (g + 1) * hp)
                    dk_acc[...] += _dot(_on_top(dss), _on_top([_only(q_ref[...], h, w) for h in heads]), TN)
                    dq_ref[rows, :] += _dot(_side_by_side(dss), _on_top([_only(k_ref[...], h, w) for h in heads]), NN)
                dv_acc[:, lanes] += _dot(_on_top(ps), _stacked(do128, hp, w0), TN)
                dk_accs[0][:, lanes] += _dot(_on_top(dss), _stacked(q128, hp, w0), TN)
                dq_refs[0][rows, lanes] += _dot(_side_by_side(dss), _stacked(k128, hp, w0), NN)

        if mask is None:
            compute(False)
        else:
            pl.when(qi > ki)(lambda: compute(False))
            pl.when(qi == ki)(lambda: compute(True))

        @pl.when(qi == nq - 1)
        def _():
            for r, acc in zip(dk_refs, dk_accs):
                r[...] = acc[...]
            dv_ref[...] = dv_acc[...]

    q_idx = (lambda j, i: jnp.maximum(i, j)) if mask else (lambda j, i: i)
    k_idx = lambda j, i: j
    ins, in_specs, dq_shapes, dq_specs, dk_shapes, dk_specs, scratch = [], [], [], [], [], [], []
    for q_e, k_e, w, shared in qk:
        ins += [q_e[0], k_e[0]]
        in_specs += [_col_block(q_e, tq, q_idx), _col_block(k_e, tk, k_idx)]
        dq_shapes.append(jax.ShapeDtypeStruct((Sq, H * w), F32))
        dq_specs.append(pl.BlockSpec((Sq, H * w), lambda j, i: (0, 0)))
        kw = k_e[2]
        dk_shapes.append(jax.ShapeDtypeStruct((Sk, kw), F32))
        dk_specs.append(pl.BlockSpec((tk, kw), lambda j, i: (j, 0)))
        scratch.append(pltpu.VMEM((tk, kw), F32))
    row_q = lambda width: pl.BlockSpec((tq, width), lambda j, i: (q_idx(j, i), 0))
    ins += [v[0], o, do, lse]
    in_specs += [_col_block(v, tk, k_idx), row_q(H * dv), row_q(H * dv), row_q(8)]
    out_shape = dq_shapes + dk_shapes + [jax.ShapeDtypeStruct((Sk, H * dv), F32)]
    out_specs = dq_specs + dk_specs + [pl.BlockSpec((tk, H * dv), lambda j, i: (j, 0))]
    if bias:
        in_specs += [row_q(8), pl.BlockSpec((8, tk), lambda j, i: (0, j))]
        ins += [cq, ck]
        out_shape += [jax.ShapeDtypeStruct((8, Sk), F32), jax.ShapeDtypeStruct((Sq, 8), F32)]
        out_specs += [pl.BlockSpec((8, tk), lambda j, i: (0, j)), pl.BlockSpec((Sq, 8), lambda j, i: (0, 0))]
    scratch.append(pltpu.VMEM((tk, H * dv), F32))
    n_out = len(out_shape)
    r_ins, r_in_specs, r_outs, r_out_specs, r_scratch, split = _carry(
        rider, len(ins), n_out, lambda: (pl.program_id(0) == 0) & (pl.program_id(1) == 0),
        lambda: (pl.program_id(0) == nk - 1) & (pl.program_id(1) == nq - 1))
    res = pl.pallas_call(
        body, name=name, out_shape=tuple(out_shape + r_outs), grid=(nk, nq), in_specs=in_specs + r_in_specs,
        out_specs=tuple(out_specs + r_out_specs), scratch_shapes=scratch + r_scratch,
        compiler_params=_params(("arbitrary", "arbitrary")),
    )(*ins, *r_ins)
    own = (list(res[:npart]), list(res[npart:2 * npart]), res[2 * npart]) + tuple(res[2 * npart + 1:n_out])
    return own + (rider.post(res[n_out:]),) if rider else own


def _split3_dot(x, t):
    hi = x.astype(BF16)
    r1 = x - hi.astype(F32)
    mid = r1.astype(BF16)
    lo = (r1 - mid.astype(F32)).astype(BF16)
    return _dot(hi, t, NN) + _dot(mid, t, NN) + _dot(lo, t, NN)


def _fox_cum_fwd(ff_t, b, *, name):
    _, S = ff_t.shape
    tb = _pick(S, (512, 256, 128))

    def body(f_ref, b_ref, o_ref, carry):
        @pl.when(pl.program_id(0) == 0)
        def _():
            carry[...] = jnp.zeros(carry.shape, F32)

        lf = _log_sigmoid(f_ref[...] + b_ref[...])
        o_ref[...] = _split3_dot(lf, _tri(tb, False)) + carry[...]
        carry[...] += jnp.sum(lf, axis=1, keepdims=True)

    return pl.pallas_call(
        body, name=name, out_shape=jax.ShapeDtypeStruct((8, S), F32), grid=(S // tb,),
        in_specs=[pl.BlockSpec((8, tb), lambda i: (0, i)), pl.BlockSpec((8, 1), lambda i: (0, 0))],
        out_specs=pl.BlockSpec((8, tb), lambda i: (0, i)),
        scratch_shapes=[pltpu.VMEM((8, 1), F32)],
        compiler_params=_params(("arbitrary",)),
    )(ff_t, b)


def _fox_cum_bwd(ff_t, b, dcum_t, *, name):
    _, S = ff_t.shape
    tb = _pick(S, (512, 256, 128))
    nb = S // tb

    def body(f_ref, b_ref, dc_ref, df_ref, db_ref, carry):
        @pl.when(pl.program_id(0) == 0)
        def _():
            carry[...] = jnp.zeros(carry.shape, F32)
            db_ref[...] = jnp.zeros(db_ref.shape, F32)

        dc = dc_ref[...]
        dlf = _split3_dot(dc, _tri(tb, True)) + carry[...]
        carry[...] += jnp.sum(dc, axis=1, keepdims=True)
        df = dlf * _sigmoid(-(f_ref[...] + b_ref[...]))
        df_ref[...] = df
        db_ref[...] += jnp.sum(df, axis=1, keepdims=True)

    rev = lambda i: (0, nb - 1 - i)
    return pl.pallas_call(
        body, name=name,
        out_shape=(jax.ShapeDtypeStruct((8, S), F32), jax.ShapeDtypeStruct((8, 1), F32)), grid=(nb,),
        in_specs=[pl.BlockSpec((8, tb), rev), pl.BlockSpec((8, 1), lambda i: (0, 0)), pl.BlockSpec((8, tb), rev)],
        out_specs=(pl.BlockSpec((8, tb), rev), pl.BlockSpec((8, 1), lambda i: (0, 0))),
        scratch_shapes=[pltpu.VMEM((8, 1), F32)],
        compiler_params=_params(("arbitrary",)),
    )(ff_t, b, dcum_t)


GLA_W = GLA_HEADS * GLA_DK
GLA_BLOCK_CHUNKS = 4


def _same_chunk(n, lower):
    r = lax.broadcasted_iota(jnp.int32, (n, n), 0)
    c = lax.broadcasted_iota(jnp.int32, (n, n), 1)
    same = (r | (CHUNK - 1)) == (c | (CHUNK - 1))
    return jnp.where(same & (r >= c) if lower else same, 1.0, 0.0).astype(BF16)


def _chunk_mix(x, t, transpose):
    hi, lo = _split2(x)
    dims = TN if transpose else NN
    return _dot(t, hi, dims) + _dot(t, lo, dims)


@jax.custom_vjp
def chunk_cumsum(x):
    return _chunk_mix(x, _same_chunk(x.shape[0], True), False)


chunk_cumsum.defvjp(lambda x: (chunk_cumsum(x), None), lambda _, g: (_chunk_mix(g, _same_chunk(g.shape[0], True), True),))


@jax.custom_vjp
def chunk_total(x):
    return _chunk_mix(x, _same_chunk(x.shape[0], False), False)


chunk_total.defvjp(lambda x: (chunk_total(x), None), lambda _, g: (_chunk_mix(g, _same_chunk(g.shape[0], False), False),))


def _gla_block(q, k, zsm, wg, bg, go, vs, rs, states):
    n_chunks = q.shape[0] // CHUNK
    la = _log_sigmoid(bdot(zsm, wg) + bg) * (1.0 / GLA_TAU)
    end = chunk_total(la)
    kd = k * jnp.exp(end - chunk_cumsum(la))
    a = jnp.exp(end)
    qs = q * (GLA_DK ** -0.5)
    lane = lax.broadcasted_iota(jnp.int32, (1, GLA_W), 1)
    outs, new_states = [], []
    for h in range(GLA_HEADS):
        kdh = kd * jnp.where((lane >= h * GLA_DK) & (lane < (h + 1) * GLA_DK), 1.0, 0.0)
        st, o = states[h], []
        for c in range(n_chunks):
            rows = slice(c * CHUNK, (c + 1) * CHUNK)
            st = st * a[c * CHUNK:c * CHUNK + 1] + bdot_tn(vs[h][rows], kdh[rows])
            o.append(bdot_nt(qs[rows], st))
        o = _rms(jnp.concatenate(o, axis=0), go)
        outs.append(o * (rs[h] * _sigmoid(rs[h])))
        new_states.append(st)
    return outs, new_states


def _gla_fwd(z, zsm, wg, bg, go, cols, *, name):
    S = z.shape[0]
    rb = GLA_BLOCK_CHUNKS * CHUNK
    nb = S // rb
    cq, ckk, cv, cr = cols
    H = GLA_HEADS

    def body(q_ref, k_ref, zsm_ref, wg_ref, bg_ref, go_ref, *rest):
        v_refs, r_refs = rest[:H], rest[H:2 * H]
        o_ref, st_ref, state = rest[2 * H], rest[2 * H + 1], rest[2 * H + 2]

        @pl.when(pl.program_id(0) == 0)
        def _():
            state[...] = jnp.zeros(state.shape, F32)

        states = [state[h] for h in range(H)]
        for h in range(H):
            st_ref[0, h] = states[h]
        outs, new_states = _gla_block(
            q_ref[...].astype(F32), k_ref[...].astype(F32), zsm_ref[...], wg_ref[...], bg_ref[...], go_ref[...],
            [v_refs[h][...].astype(F32) for h in range(H)], [r_refs[h][...].astype(F32) for h in range(H)], states)
        for h in range(H):
            o_ref[:, h * GLA_DV:(h + 1) * GLA_DV] = outs[h].astype(BF16)
            state[h] = new_states[h]

    def col(width, off):
        return pl.BlockSpec((rb, width), lambda i, o=off // width: (i, o))

    full = lambda shp: pl.BlockSpec(shp, lambda i: (0,) * len(shp))
    in_specs = [col(GLA_W, cq), col(GLA_W, ckk), pl.BlockSpec((rb, 128), lambda i: (i, 0)),
                full((128, GLA_W)), full((1, GLA_W)), full((1, GLA_DV))]
    in_specs += [col(GLA_DV, cv + h * GLA_DV) for h in range(H)] + [col(GLA_DV, cr + h * GLA_DV) for h in range(H)]
    return pl.pallas_call(
        body, name=name,
        out_shape=(jax.ShapeDtypeStruct((S, H * GLA_DV), BF16), jax.ShapeDtypeStruct((nb, H, GLA_DV, GLA_W), F32)),
        grid=(nb,), in_specs=in_specs,
        out_specs=(pl.BlockSpec((rb, H * GLA_DV), lambda i: (i, 0)),
                   pl.BlockSpec((1, H, GLA_DV, GLA_W), lambda i: (i, 0, 0, 0))),
        scratch_shapes=[pltpu.VMEM((H, GLA_DV, GLA_W), F32)],
        compiler_params=_params(("arbitrary",)),
    )(z, z, zsm, wg, bg, go, *([z] * (2 * H)))


def _gla_bwd(z, zsm, wg, bg, go, states, do, cols, *, name):
    S = z.shape[0]
    rb = GLA_BLOCK_CHUNKS * CHUNK
    nb = S // rb
    cq, ckk, cv, cr = cols
    H = GLA_HEADS

    def body(q_ref, k_ref, zsm_ref, wg_ref, bg_ref, go_ref, st_ref, do_ref, *rest):
        v_refs, r_refs = rest[:H], rest[H:2 * H]
        dq_ref, dk_ref, dv_ref, dr_ref, dzsm_ref, dwg_ref, dbg_ref, dgo_ref, dstate = rest[2 * H:]

        @pl.when(pl.program_id(0) == 0)
        def _():
            dstate[...] = jnp.zeros(dstate.shape, F32)
            dwg_ref[...] = jnp.zeros(dwg_ref.shape, F32)
            dbg_ref[...] = jnp.zeros(dbg_ref.shape, F32)
            dgo_ref[...] = jnp.zeros(dgo_ref.shape, F32)

        prim = (q_ref[...].astype(F32), k_ref[...].astype(F32), zsm_ref[...], wg_ref[...], bg_ref[...], go_ref[...],
                [v_refs[h][...].astype(F32) for h in range(H)], [r_refs[h][...].astype(F32) for h in range(H)],
                [st_ref[0, h] for h in range(H)])
        _, vjp = jax.vjp(_gla_block, *prim)
        douts = [do_ref[:, h * GLA_DV:(h + 1) * GLA_DV].astype(F32) for h in range(H)]
        dq, dk, dzs, dwg, dbg, dgo, dvs, drs, dsts = vjp((douts, [dstate[h] for h in range(H)]))
        dq_ref[...] = dq.astype(BF16)
        dk_ref[...] = dk.astype(BF16)
        dzsm_ref[...] = dzs
        dwg_ref[...] += dwg
        dbg_ref[...] += dbg
        dgo_ref[...] += dgo
        for h in range(H):
            dv_ref[:, h * GLA_DV:(h + 1) * GLA_DV] = dvs[h].astype(BF16)
            dr_ref[:, h * GLA_DV:(h + 1) * GLA_DV] = drs[h].astype(BF16)
            dstate[h] = dsts[h]

    rev = lambda i: nb - 1 - i

    def col(width, off):
        return pl.BlockSpec((rb, width), lambda i, o=off // width: (rev(i), o))

    full = lambda shp: pl.BlockSpec(shp, lambda i: (0,) * len(shp))
    rowb = lambda w: pl.BlockSpec((rb, w), lambda i: (rev(i), 0))
    in_specs = [col(GLA_W, cq), col(GLA_W, ckk), rowb(128), full((128, GLA_W)), full((1, GLA_W)), full((1, GLA_DV)),
                pl.BlockSpec((1, H, GLA_DV, GLA_W), lambda i: (rev(i), 0, 0, 0)), rowb(H * GLA_DV)]
    in_specs += [col(GLA_DV, cv + h * GLA_DV) for h in range(H)] + [col(GLA_DV, cr + h * GLA_DV) for h in range(H)]
    return pl.pallas_call(
        body, name=name,
        out_shape=(jax.ShapeDtypeStruct((S, GLA_W), BF16), jax.ShapeDtypeStruct((S, GLA_W), BF16),
                   jax.ShapeDtypeStruct((S, H * GLA_DV), BF16), jax.ShapeDtypeStruct((S, H * GLA_DV), BF16),
                   jax.ShapeDtypeStruct((S, 128), F32), jax.ShapeDtypeStruct((128, GLA_W), F32),
                   jax.ShapeDtypeStruct((1, GLA_W), F32), jax.ShapeDtypeStruct((1, GLA_DV), F32)),
        grid=(nb,), in_specs=in_specs,
        out_specs=(rowb(GLA_W), rowb(GLA_W), rowb(H * GLA_DV), rowb(H * GLA_DV), rowb(128),
                   full((128, GLA_W)), full((1, GLA_W)), full((1, GLA_DV))),
        scratch_shapes=[pltpu.VMEM((H, GLA_DV, GLA_W), F32)],
        compiler_params=_params(("arbitrary",)),
    )(z, z, zsm, wg, bg, go, states, do, *([z] * (2 * H)))


def _row_spec(entry, tr):
    if isinstance(entry, tuple):
        arr, width, off = entry
        return arr, pl.BlockSpec((tr, width), lambda i, o=off // width: (i, o))
    return entry, pl.BlockSpec((tr, entry.shape[1]), lambda i: (i, 0))


def _stage_fwd(fn, rows, consts, outs, *, name, tr=None):
    first = rows[0][0] if isinstance(rows[0], tuple) else rows[0]
    S = first.shape[0]
    tr = tr or _pick(S, (512, 256, 128))
    arrs, specs = zip(*[_row_spec(e, tr) for e in rows])
    nr, nc = len(rows), len(consts)

    def body(*refs):
        vals = [r[...].astype(F32) for r in refs[:nr + nc]]
        res = fn(*vals)
        for o_ref, val in zip(refs[nr + nc:], res):
            o_ref[...] = val.astype(o_ref.dtype)

    cspecs = [pl.BlockSpec(c.shape, lambda i, n=c.ndim: (0,) * n) for c in consts]
    return pl.pallas_call(
        body, name=name,
        out_shape=tuple(jax.ShapeDtypeStruct((S, w), dt) for w, dt in outs), grid=(S // tr,),
        in_specs=list(specs) + cspecs,
        out_specs=tuple(pl.BlockSpec((tr, w), lambda i: (i, 0)) for w, _ in outs),
        compiler_params=_params(("parallel",)),
    )(*arrs, *consts)


def _stage_bwd(fn, rows, consts, cts, n_diff, drow_dtypes, *, name, tr=None):
    first = rows[0][0] if isinstance(rows[0], tuple) else rows[0]
    S = first.shape[0]
    tr = tr or _pick(S, (512, 256, 128))
    arrs, specs = zip(*[_row_spec(e, tr) for e in rows])
    widths = [e[1] if isinstance(e, tuple) else e.shape[1] for e in rows]
    nr, nc, nt = len(rows), len(consts), len(cts)

    def body(*refs):
        vals = [r[...].astype(F32) for r in refs[:nr + nc]]
        ct = [r[...].astype(F32) for r in refs[nr + nc:nr + nc + nt]]
        drow_refs = refs[nr + nc + nt:nr + nc + nt + n_diff]
        dconst_refs = refs[nr + nc + nt + n_diff:]
        rest_rows = vals[n_diff:nr]

        def f(diff_rows, cs):
            return tuple(fn(*diff_rows, *rest_rows, *cs))

        _, vjp = jax.vjp(f, vals[:n_diff], vals[nr:])
        drows, dcs = vjp(tuple(ct))
        for r, val in zip(drow_refs, drows):
            r[...] = val.astype(r.dtype)
        first_step = pl.program_id(0) == 0
        for r, val in zip(dconst_refs, dcs):
            @pl.when(first_step)
            def _(r=r, val=val):
                r[...] = val

            @pl.when(jnp.logical_not(first_step))
            def _(r=r, val=val):
                r[...] += val

    cspecs = [pl.BlockSpec(c.shape, lambda i, n=c.ndim: (0,) * n) for c in consts]
    ctspecs = [pl.BlockSpec((tr, c.shape[1]), lambda i: (i, 0)) for c in cts]
    out_shape = [jax.ShapeDtypeStruct((S, widths[j]), drow_dtypes[j]) for j in range(n_diff)]
    out_shape += [jax.ShapeDtypeStruct(c.shape, F32) for c in consts]
    out_specs = [pl.BlockSpec((tr, widths[j]), lambda i: (i, 0)) for j in range(n_diff)] + cspecs
    res = pl.pallas_call(
        body, name=name, out_shape=tuple(out_shape), grid=(S // tr,),
        in_specs=list(specs) + cspecs + ctspecs, out_specs=tuple(out_specs),
        compiler_params=_params(("arbitrary",)),
    )(*arrs, *consts, *cts)
    return list(res[:n_diff]), list(res[n_diff:])


def _mla_prep_fn(cq, ckv, kr, kr_sw, cos, sin, gq, gkv, wq_n, wq_r, wq_sw, wk, wv):
    hq = _rms(cq, gq)
    hkv = _rms(ckv, gkv)
    return (bdot(hq, wq_n), bdot(hq, wq_r) * cos + bdot(hq, wq_sw) * sin,
            bdot(hkv, wk), bdot(hkv, wv), kr * cos + kr_sw * sin)


def _merge_fn(g0, g1, g2, of, og, om, b0, b1, b2, wf, wg, wm):
    return (_sigmoid(g0 + b0) * bdot(of, wf) + _sigmoid(g1 + b1) * bdot(og, wg) + _sigmoid(g2 + b2) * bdot(om, wm),)


_IN_SIZES = (256, 256, 256, 4, 256, 256, 512, 16, 512, 256, 128, 32, 3072)
_IN_OFF = np.concatenate([[0], np.cumsum(_IN_SIZES)])
(_O_FQ, _O_FK, _O_FV, _O_FF, _O_GQ, _O_GK, _O_GV, _O_GLOW, _O_GR, _O_MQ, _O_MKV, _O_MKR, _O_ZG) = [int(o) for o in _IN_OFF[:-1]]
N_IN = int(_IN_OFF[-1])
_BIG_GROUPS = ((_O_ZG, 3072), (_O_GV, 512), (_O_GR, 512), (_O_FQ, 256), (_O_FK, 256), (_O_FV, 256),
               (_O_GQ, 256), (_O_GK, 256), (_O_MQ, 256), (_O_MKV, 128))
Z_GATE, Z_GV, Z_GR, Z_FQ, Z_FK, Z_FV, Z_GQ, Z_GK, Z_MQ, Z_MKV = [int(o) for o in
                                                                    np.concatenate([[0], np.cumsum([w for _, w in _BIG_GROUPS])])[:-1]]
N_BIG = sum(w for _, w in _BIG_GROUPS)
_HALF = MLA_ROPE // 2
_QK_HD = MLA_NOPE + MLA_ROPE
SM_FF, SM_GLOW, SM_KR, SM_KR_SW, N_SM = 0, 8, 128, 256, 384
N_PAD = N_BIG + N_SM
_IN_SEGS = ([(o, w, 1.0) for o, w in _BIG_GROUPS]
            + [(_O_FF, 4, 1.0), (None, SM_GLOW - 4, 0.0), (_O_GLOW, GLA_RANK, 1.0), (None, 128 - SM_GLOW - GLA_RANK, 0.0)]
            + [(_O_MKR, MLA_ROPE, 1.0)] * MLA_HEADS
            + [(_O_MKR + _HALF, _HALF, -1.0), (_O_MKR, _HALF, 1.0)] * MLA_HEADS)


def _cols(x, start, width):
    return lax.slice_in_dim(x, start, start + width, axis=x.ndim - 1)


def _pad_w_in(w):
    return jnp.concatenate([jnp.zeros(w.shape[:-1] + (n,), w.dtype) if src is None else
                            (_cols(w, src, n) if sign > 0 else -_cols(w, src, n)) for src, n, sign in _IN_SEGS], axis=-1)


def _unpad_w_in(g):
    groups = []
    for o, n in zip(_IN_OFF[:-1], _IN_SIZES):
        total, pos = None, 0
        for src, m, sign in _IN_SEGS:
            if src is not None and o <= src and src + m <= o + n:
                term = _cols(g, pos, m) if sign > 0 else -_cols(g, pos, m)
                if m != n:
                    term = jnp.pad(term, [(0, 0)] * (g.ndim - 1) + [(int(src - o), int(o + n - src - m))])
                total = term if total is None else total + term
            pos += m
        groups.append(total)
    return jnp.concatenate(groups, axis=-1)


def _take(x, idx):
    idx = np.asarray(idx)
    cuts = [0] + [i for i in range(1, len(idx)) if idx[i] != idx[i - 1] + 1] + [len(idx)]
    return jnp.concatenate([_cols(x, int(idx[a]), b - a) for a, b in zip(cuts[:-1], cuts[1:])], axis=1)


_UQ_NOPE = np.concatenate([np.arange(h * _QK_HD, h * _QK_HD + MLA_NOPE) for h in range(MLA_HEADS)])
_UQ_ROT = np.concatenate([np.arange(h * _QK_HD + MLA_NOPE, (h + 1) * _QK_HD) for h in range(MLA_HEADS)])
_UKV_PERM = np.concatenate(
    [np.concatenate([np.arange(h * 128, h * 128 + MLA_NOPE) for h in range(MLA_HEADS)]),
     np.concatenate([np.arange(h * 128 + MLA_NOPE, (h + 1) * 128) for h in range(MLA_HEADS)])])
_UKV_INV = np.argsort(_UKV_PERM)


def _rotary_partner(r):
    return jnp.concatenate([piece for h in range(MLA_HEADS) for piece in
                            (-_cols(r, h * MLA_ROPE + _HALF, _HALF), _cols(r, h * MLA_ROPE, _HALF))], axis=1)


def _uq_grad(dn, dr, dsw):
    dr = dr + jnp.concatenate([piece for h in range(MLA_HEADS) for piece in
                               (_cols(dsw, h * MLA_ROPE + _HALF, _HALF), -_cols(dsw, h * MLA_ROPE, _HALF))], axis=1)
    return jnp.concatenate([piece for h in range(MLA_HEADS) for piece in
                            (_cols(dn, h * MLA_NOPE, MLA_NOPE), _cols(dr, h * MLA_ROPE, MLA_ROPE))], axis=1)


def _rope_tables(S):
    inv = ROPE_BASE ** (-jnp.arange(_HALF, dtype=F32) / _HALF)
    ang = jnp.arange(S, dtype=F32)[:, None] * inv[None, :]
    return jnp.tile(jnp.cos(ang), (1, 2 * MLA_HEADS)), jnp.tile(jnp.sin(ang), (1, 2 * MLA_HEADS))


class _LayerParams:
    def __init__(self, rep, l):
        self.w, self.rep, self.l, self.made = {}, rep, l, {}

    def __getitem__(self, k):
        if k not in self.made:
            self.made[k] = self._make(k)
        return self.made[k]

    def _make(self, k):
        w, rep, l = self.w, self.rep, self.l
        if k == 'wg':
            return jnp.pad(w['w_gla_gate'], [(SM_GLOW, LANES - SM_GLOW - GLA_RANK), (0, 0)])
        if k in ('wq_n', 'wq_r'):
            return _take(w['w_mla_uq'], _UQ_NOPE if k == 'wq_n' else _UQ_ROT)
        if k == 'wq_sw':
            return _rotary_partner(self['wq_r'])
        if k in ('wk', 'wv'):
            return _take(w['w_mla_ukv'], _UKV_PERM[:256] if k == 'wk' else _UKV_PERM[256:])
        if k == 'b_f':
            return jnp.zeros((8, 1), F32).at[:FOX_HEADS, 0].set(rep['b_fox_forget'][l])
        if k == 'b_gate':
            return [rep['b_branch_gate'][l][i * 1024:(i + 1) * 1024].reshape(1, 1024) for i in range(3)]
        vec = {'bg': 'b_gla_gate', 'go': 'g_gla_out', 'gq': 'g_mla_q', 'gkv': 'g_mla_kv'}
        if k in vec:
            return rep[vec[k]][l].reshape(1, -1)
        return rep[k][l] if k in rep else w[k]


_GLA_COLS = (Z_GQ, Z_GK, Z_GV, Z_GR)
_MLA_OUTS = [(256, BF16), (128, BF16), (256, BF16), (256, BF16), (128, BF16)]


def _mla_rows(z, zsm, rope):
    return [(z, 256, Z_MQ), (z, 128, Z_MKV), (zsm, 128, SM_KR), (zsm, 128, SM_KR_SW), *rope]


def _mla_consts(p):
    return [p['gq'], p['gkv'], p['wq_n'], p['wq_r'], p['wq_sw'], p['wk'], p['wv']]


def _fox_qkv(z):
    return [((z, Z_FQ, 256), (z, Z_FK, 256), FOX_HD, False)], (z, Z_FV, 256)


def _mla_qkv(qn, qr, kn, vv, kr):
    return [((qn, 0, 256), (kn, 0, 256), MLA_NOPE, False), ((qr, 0, 128), (kr, 0, 128), MLA_ROPE, True)], (vv, 0, 256)


def _xa_qkv(qx, kvx):
    return [((qx, 0, 512), (kvx, 0, 512), XA_HD, False)], (kvx, 512, 512)


def _merge_rows(z, o_fox, o_gla, o_mla):
    return [(z, 1024, Z_GATE), (z, 1024, Z_GATE + 1024), (z, 1024, Z_GATE + 2048), o_fox, o_gla, o_mla]


def _merge_consts(p):
    return p['b_gate'] + [p['w_up_fox'], p['w_up_gla'], p['w_up_mla']]


def _carried(hooks, key, call):
    rider, sink = hooks.pop(key, (None, None))
    res = call(rider=rider)
    if rider is None:
        return res
    sink(res[-1])
    return res[:-1]


def _layer_fwd(x0, mem, p, rope, l, hooks):
    S = x0.shape[0]
    sv = {'x0': x0}
    h1 = _rms_fwd(x0, p['g_mix'], name=f"rms_mix_{l}")
    z = _mm(h1, p['w_in'], mode='nn', out_dtype=BF16, b_cols=(0, N_BIG), name=f"in_big_{l}")
    zsm = _mm(h1, p['w_in'], mode='nn', out_dtype=F32, b_cols=(N_BIG, N_SM), name=f"in_small_{l}")
    sv.update(h1=h1, z=z, zsm=zsm)
    ff_t = jnp.zeros((8, S), F32).at[:FOX_HEADS].set(zsm[:, SM_FF:SM_FF + FOX_HEADS].T)
    cum_t = _fox_cum_fwd(ff_t, p['b_f'], name=f"fox_cum_{l}")
    cum = cum_t.T
    o_fox, lse_f = _carried(hooks, (l, 'fox_fwd'), lambda rider: _attn_fwd(
        *_fox_qkv(z), FOX_HEADS, cum, cum_t, scale=FOX_HD ** -0.5, mask='causal', name=f"fox_fwd_{l}", rider=rider))
    sv.update(ff_t=ff_t, cum=cum, cum_t=cum_t, lse_f=lse_f, o_fox=o_fox)
    o_gla, states = _gla_fwd(z, zsm, p['wg'], p['bg'], p['go'], _GLA_COLS, name=f"gla_fwd_{l}")
    sv.update(o_gla=o_gla, states=states)
    mla = _stage_fwd(_mla_prep_fn, _mla_rows(z, zsm, rope), _mla_consts(p), _MLA_OUTS, name=f"mla_prep_{l}")
    o_mla, lse_m = _carried(hooks, (l, 'mla_fwd'), lambda rider: _attn_fwd(
        *_mla_qkv(*mla), MLA_HEADS, None, None, scale=_QK_HD ** -0.5, mask='chunk', name=f"mla_fwd_{l}", rider=rider))
    sv.update(mla=mla, lse_m=lse_m, o_mla=o_mla)
    (y,) = _stage_fwd(_merge_fn, _merge_rows(z, o_fox, o_gla, o_mla), _merge_consts(p), [(1024, BF16)], name=f"merge_{l}")
    x1 = _mm(y, p['w_out'], mode='nn', out_dtype=F32, residual=x0, name=f"out_proj_{l}")
    sv.update(y=y, x1=x1)
    h2 = _rms_fwd(x1, p['g_xa'], name=f"rms_xa_{l}")
    hm = _rms_fwd(mem, p['g_mem'], name=f"rms_mem_{l}")
    qx = _mm(h2, p['w_xq'], mode='nn', out_dtype=BF16, name=f"xq_{l}")
    kvx = _mm(hm, p['w_xkv'], mode='nn', out_dtype=BF16, name=f"xkv_{l}")
    ox, lse_x = _attn_fwd(*_xa_qkv(qx, kvx), XA_HEADS, None, None, scale=XA_HD ** -0.5, mask=None, name=f"xa_fwd_{l}")
    x2 = _mm(ox, p['w_xo'], mode='nn', out_dtype=F32, residual=x1, name=f"xo_{l}")
    sv.update(h2=h2, hm=hm, qx=qx, kvx=kvx, lse_x=lse_x, ox=ox, x2=x2)
    h3 = _rms_fwd(x2, p['g_mlp'], name=f"rms_mlp_{l}")
    a = _mm(h3, p['w_mlp1'], mode='nn', out_dtype=BF16, name=f"mlp1_{l}")
    x3 = _mm(a, p['w_mlp2'], mode='nn', out_dtype=F32, act='relu2', residual=x2, name=f"mlp2_{l}")
    sv.update(h3=h3, a=a)
    return x3, sv


def _layer_bwd(dx3, dx3b, mem, p, rope, sv, l, hooks, half_done):
    S = dx3.shape[0]
    g = {}
    da = _mm(dx3b, p['w_mlp2'], mode='nt', out_dtype=BF16, drelu_of=sv['a'], name=f"d_mlp2_in_{l}")
    g['w_mlp2'] = _mm(sv['a'], dx3b, mode='tn', out_dtype=BF16, act='relu2', name=f"d_w_mlp2_{l}")
    dx2, dx2b, g['g_mlp'] = _mm(da, p['w_mlp1'], mode='nt', out_dtype=F32, norm_bwd=(sv['x2'], p['g_mlp'], dx3), tm=512,
                                name=f"d_mlp1_in_{l}")
    g['w_mlp1'] = _mm(sv['h3'], da, mode='tn', out_dtype=BF16, name=f"d_w_mlp1_{l}")
    dox = _mm(dx2b, p['w_xo'], mode='nt', out_dtype=BF16, name=f"d_xo_in_{l}")
    g['w_xo'] = _mm(sv['ox'], dx2b, mode='tn', out_dtype=BF16, name=f"d_w_xo_{l}")
    (dqx,), (dkx,), dvx = _attn_bwd(*_xa_qkv(sv['qx'], sv['kvx']), XA_HEADS, sv['ox'], dox, sv['lse_x'], None, None,
                                    scale=XA_HD ** -0.5, mask=None, name=f"xa_bwd_{l}")
    dqx = dqx.astype(BF16)
    dkvx = jnp.concatenate([dkx, dvx], axis=1).astype(BF16)
    dx1, dx1b, g['g_xa'] = _mm(dqx, p['w_xq'], mode='nt', out_dtype=F32, norm_bwd=(sv['x1'], p['g_xa'], dx2), tm=512,
                               name=f"d_xq_in_{l}")
    g['w_xq'] = _mm(sv['h2'], dqx, mode='tn', out_dtype=BF16, name=f"d_w_xq_{l}")
    dhm = _mm(dkvx, p['w_xkv'], mode='nt', out_dtype=F32, name=f"d_xkv_in_{l}")
    g['w_xkv'] = _mm(sv['hm'], dkvx, mode='tn', out_dtype=BF16, name=f"d_w_xkv_{l}")
    _, _, g['g_mem'] = _rms_bwd(mem, p['g_mem'], dhm, None, name=f"d_rms_mem_{l}")
    dy = _mm(dx1b, p['w_out'], mode='nt', out_dtype=F32, name=f"d_out_in_{l}")
    g['w_out'] = _mm(sv['y'], dx1b, mode='tn', out_dtype=BF16, name=f"d_w_out_{l}")
    z, zsm = sv['z'], sv['zsm']
    (dg0, dg1, dg2, do_fox, do_gla, do_mla), (db0, db1, db2, g['w_up_fox'], g['w_up_gla'], g['w_up_mla']) = _stage_bwd(
        _merge_fn, _merge_rows(z, sv['o_fox'], sv['o_gla'], sv['o_mla']), _merge_consts(p), [dy], 6, [BF16] * 6,
        name=f"merge_bwd_{l}")
    g['b_branch_gate'] = jnp.concatenate([db0, db1, db2], axis=1).reshape(-1)
    half_done(l, g)
    (dfq,), (dfk,), dfv, dck, dcq = _carried(hooks, (l, 'fox_bwd'), lambda rider: _attn_bwd(
        *_fox_qkv(z), FOX_HEADS, sv['o_fox'], do_fox, sv['lse_f'], sv['cum'], sv['cum_t'],
        scale=FOX_HD ** -0.5, mask='causal', name=f"fox_bwd_{l}", rider=rider))
    dff_t, db_f = _fox_cum_bwd(sv['ff_t'], p['b_f'], dck + dcq.T, name=f"fox_cum_bwd_{l}")
    g['b_fox_forget'] = db_f[:FOX_HEADS, 0]
    dgq, dgk, dgv, dgr, dzsm, dwg, dbg, dgo = _gla_bwd(z, zsm, p['wg'], p['bg'], p['go'], sv['states'], do_gla, _GLA_COLS,
                                                       name=f"gla_bwd_{l}")
    g['w_gla_gate'] = dwg[SM_GLOW:SM_GLOW + GLA_RANK]
    g['b_gla_gate'] = dbg.reshape(-1)
    g['g_gla_out'] = dgo.reshape(-1)
    (dmqn, dmqr), (dmkn, dmkr), dmv = _carried(hooks, (l, 'mla_bwd'), lambda rider: _attn_bwd(
        *_mla_qkv(*sv['mla']), MLA_HEADS, sv['o_mla'], do_mla, sv['lse_m'], None, None,
        scale=_QK_HD ** -0.5, mask='chunk', name=f"mla_bwd_{l}", rider=rider))
    (dcq, dckv, dkr, dkr_sw), (dgq_n, dgkv_n, dwq_n, dwq_r, dwq_sw, dwk, dwv) = _stage_bwd(
        _mla_prep_fn, _mla_rows(z, zsm, rope), _mla_consts(p), [dmqn, dmqr, dmkn, dmv, dmkr], 4, [BF16] * 4,
        name=f"mla_prep_bwd_{l}")
    g['g_mla_q'] = dgq_n.reshape(-1)
    g['g_mla_kv'] = dgkv_n.reshape(-1)
    g['w_mla_uq'] = _uq_grad(dwq_n, dwq_r, dwq_sw)
    g['w_mla_ukv'] = _take(jnp.concatenate([dwk, dwv], axis=1), _UKV_INV)
    dsm = dzsm + jnp.pad(dff_t[:FOX_HEADS].T, [(0, 0), (0, 128 - FOX_HEADS)])
    dz = jnp.concatenate([dg0, dg1, dg2, dgv, dgr, dfq.astype(BF16), dfk.astype(BF16), dfv.astype(BF16), dgq, dgk, dcq, dckv,
                          dsm.astype(BF16), dkr.astype(BF16), dkr_sw.astype(BF16)], axis=1)
    dx0, dx0b, g['g_mix'] = _mm(dz, p['w_in'], mode='nt', out_dtype=F32, norm_bwd=(sv['x0'], p['g_mix'], dx1), tm=512,
                                tk=N_PAD // 2, name=f"d_in_{l}")
    g['w_in'] = _mm(sv['h1'], dz, mode='tn', out_dtype=BF16, tn=N_PAD // 3, name=f"d_w_in_{l}")
    for n in ('g_mlp', 'g_mem', 'g_xa', 'g_mix'):
        g[n] = g[n].reshape(-1)
    return dx0, dx0b, g


def _local_step(x, mem, target, ps, g_final, hooks, half_done, layer_done):
    rope = _rope_tables(x.shape[0])
    saved = []
    for l, p in enumerate(ps):
        x, sv = _layer_fwd(x, mem, p, rope, l, hooks)
        saved.append(sv)
    loss, dx, dxb, dgf = _loss_head(x, g_final, target, name="loss_head")
    for l in reversed(range(len(ps))):
        dx, dxb, grads = _layer_bwd(dx, dxb, mem, ps[l], rope, saved[l], l, hooks, half_done)
        layer_done(l, grads)
    assert not hooks, f"exchanges without a carrier: {list(hooks)}"
    return loss, dx, dgf.reshape(-1)


_MESH_AXES = ("x", "y", "c")
_HBM = pl.BlockSpec(memory_space=pl.ANY)


N_CHIP = 4


def _place():
    x, y, c = (lax.axis_index(n) for n in _MESH_AXES)
    return (x, y, c), (x, y, 1 - c), [(1 - x, y), (x, 1 - y), (1 - x, 1 - y)]


def _remote(src, dst, sems, k, to):
    return pltpu.make_async_remote_copy(src_ref=src, dst_ref=dst, send_sem=sems[0].at[k], recv_sem=sems[1].at[k],
                                        device_id=to, device_id_type=pl.DeviceIdType.MESH)


def _all_gather(x, *, name):
    def body(x_ref, o_ref, send_sems, recv_sems, local_sem):
        me, sib, chips = _place()
        c = me[2]
        sems = (send_sems, recv_sems)
        slot = lambda px, py, pc: o_ref.at[4 * px + 2 * py + pc]
        mine = pltpu.make_async_copy(x_ref, slot(*me), local_sem)
        mine.start()
        first = [_remote(x_ref, slot(*me), sems, 0, sib)]
        first += [_remote(x_ref, slot(*me), sems, 1 + j, (*chip, c)) for j, chip in enumerate(chips)]
        for cp in first:
            cp.start()
        passed = [_remote(slot(*chip, c), slot(*chip, c), sems, 4 + j, sib) for j, chip in enumerate(chips)]
        for j, chip in enumerate(chips):
            _remote(x_ref, slot(*chip, c), sems, 1 + j, me).wait_recv()
            passed[j].start()
        _remote(x_ref, slot(*sib), sems, 0, me).wait_recv()
        for j, chip in enumerate(chips):
            _remote(x_ref, slot(*chip, 1 - c), sems, 4 + j, me).wait_recv()
        for cp in first + passed:
            cp.wait_send()
        mine.wait()

    return pl.pallas_call(
        body, name=name, out_shape=jax.ShapeDtypeStruct((N_DEV,) + x.shape, x.dtype),
        in_specs=[_HBM], out_specs=_HBM,
        scratch_shapes=[pltpu.SemaphoreType.DMA((N_DEV - 1,)), pltpu.SemaphoreType.DMA((N_DEV - 1,)), pltpu.SemaphoreType.DMA],
        compiler_params=pltpu.CompilerParams(has_side_effects=True),
    )(x)


class _Rider:
    def __init__(self, inputs, out_shapes, scratch, start, finish, post):
        self.inputs, self.out_shapes, self.scratch = list(inputs), list(out_shapes), list(scratch)
        self.start, self.finish, self.post = start, finish, post


def _run_rider(rider, *, name):
    def body(*refs):
        rider.start(refs)
        rider.finish(refs)

    outs = pl.pallas_call(
        body, name=name, out_shape=tuple(rider.out_shapes), in_specs=[_HBM] * len(rider.inputs),
        out_specs=(_HBM,) * len(rider.out_shapes), scratch_shapes=rider.scratch,
        compiler_params=pltpu.CompilerParams(has_side_effects=True),
    )(*rider.inputs)
    return rider.post(outs)


def _carry(rider, n_in, n_out, first, last):
    if rider is None:
        return [], [], [], [], [], lambda refs: refs
    ni, no = len(rider.inputs), len(rider.out_shapes)

    def split(refs):
        own_in, r_in = refs[:n_in], refs[n_in:n_in + ni]
        own_out, r_out = refs[n_in + ni:n_in + ni + n_out], refs[n_in + ni + n_out:n_in + ni + n_out + no]
        rest = refs[n_in + ni + n_out + no:]
        own_scr, r_scr = rest[:len(rest) - len(rider.scratch)], rest[len(rest) - len(rider.scratch):]
        rrefs = tuple(r_in) + tuple(r_out) + tuple(r_scr)
        pl.when(first())(lambda: rider.start(rrefs))
        pl.when(last())(lambda: rider.finish(rrefs))
        return tuple(own_in) + tuple(own_out) + tuple(own_scr)

    return list(rider.inputs), [_HBM] * ni, list(rider.out_shapes), [_HBM] * no, list(rider.scratch), split


def _gather_rider(shards, axes):
    n = len(shards)
    srcs, out_shapes, kinds = [], [], []
    for s, ax in zip(shards, axes):
        L, a, b = s.shape
        if ax == 1:
            srcs.append(s.reshape(L, 1, a, b)), out_shapes.append((L, N_DEV, a, b)), kinds.append('row')
        elif b % 128 == 0:
            srcs.append(s), out_shapes.append((L, a, N_DEV * b)), kinds.append('col')
        else:
            srcs.append(s.reshape(1, L, a, b)), out_shapes.append((N_DEV, L, a, b)), kinds.append('slot')

    def parts(refs):
        x_refs, o_refs = refs[:n], refs[n:2 * n]
        send_sems, recv_sems, local_sem = refs[2 * n:]
        me, sib, chips = _place()
        sems = (send_sems, recv_sems)

        def win(t, px, py, pc):
            idx = 4 * px + 2 * py + pc
            if kinds[t] == 'row':
                return o_refs[t].at[:, pl.ds(idx, 1)]
            if kinds[t] == 'col':
                width = shards[t].shape[2]
                return o_refs[t].at[:, :, pl.ds(pl.multiple_of(idx * width, 128), width)]
            return o_refs[t].at[pl.ds(idx, 1)]

        def group(k, block, to, own):
            return [_remote(x_refs[t] if own else win(t, *block), win(t, *block), sems, k * n + t, to) for t in range(n)]

        mine = [pltpu.make_async_copy(x_refs[t], win(t, *me), local_sem.at[t]) for t in range(n)]
        first = group(0, me, sib, True)
        for j, chip in enumerate(chips):
            first += group(1 + j, me, (*chip, me[2]), True)
        return me, sib, chips, group, mine, first

    def start(refs):
        *_, mine, first = parts(refs)
        for cp in mine + first:
            cp.start()

    def finish(refs):
        me, sib, chips, group, mine, first = parts(refs)
        c = me[2]
        passed = []
        for j, chip in enumerate(chips):
            for cp in group(1 + j, (*chip, c), me, False):
                cp.wait_recv()
            fwd = group(4 + j, (*chip, c), sib, False)
            for cp in fwd:
                cp.start()
            passed += fwd
        for cp in group(0, sib, me, False):
            cp.wait_recv()
        for j, chip in enumerate(chips):
            for cp in group(4 + j, (*chip, 1 - c), me, False):
                cp.wait_recv()
        for cp in first + passed:
            cp.wait_send()
        for cp in mine:
            cp.wait()

    def post(outs):
        whole = []
        for o, s, kind in zip(outs, shards, kinds):
            L, a, b = s.shape
            whole.append(o.reshape(L, N_DEV * a, b) if kind == 'row' else o if kind == 'col' else _to_whole(o, 2))
        return whole

    return _Rider(srcs, [jax.ShapeDtypeStruct(shp, s.dtype) for shp, s in zip(out_shapes, shards)],
                  [pltpu.SemaphoreType.DMA(((N_DEV - 1) * n,)), pltpu.SemaphoreType.DMA(((N_DEV - 1) * n,)),
                   pltpu.SemaphoreType.DMA((n,))], start, finish, post)


def _sibling_swap(x, *, name):
    def body(x_ref, o_ref, send_sems, recv_sems):
        me, sib, _ = _place()
        c = me[2]
        sems = (send_sems, recv_sems)
        sends = [_remote(x_ref.at[j, 1 - c], o_ref.at[j], sems, j, sib) for j in range(N_CHIP)]
        for cp in sends:
            cp.start()
        for cp in sends:
            cp.wait_send()
            cp.wait_recv()

    return pl.pallas_call(
        body, name=name, out_shape=jax.ShapeDtypeStruct((N_CHIP,) + x.shape[2:], x.dtype),
        in_specs=[_HBM], out_specs=_HBM,
        scratch_shapes=[pltpu.SemaphoreType.DMA((N_CHIP,)), pltpu.SemaphoreType.DMA((N_CHIP,))],
        compiler_params=pltpu.CompilerParams(has_side_effects=True),
    )(x)


def _pair_sum(x, got, c, *, name):
    _, _, R, _ = x.shape
    tr = _pick(R, (1024, 512, 256, 128, 64, 32, 16, 8))

    def body(c_ref, x_ref, g_ref, o_ref):
        o_ref[...] = (x_ref[...].astype(F32) + g_ref[...].astype(F32)).astype(o_ref.dtype)

    return pl.pallas_call(
        body, name=name, out_shape=jax.ShapeDtypeStruct((N_CHIP, R, 128), x.dtype),
        grid_spec=pltpu.PrefetchScalarGridSpec(
            num_scalar_prefetch=1, grid=(N_CHIP, R // tr),
            in_specs=[pl.BlockSpec((None, None, tr, 128), lambda j, i, c_ref: (j, c_ref[0], i, 0)),
                      pl.BlockSpec((None, tr, 128), lambda j, i, c_ref: (j, i, 0))],
            out_specs=pl.BlockSpec((None, tr, 128), lambda j, i, c_ref: (j, i, 0))),
        compiler_params=_params(("parallel", "parallel")),
    )(c, x, got)


def _chip_all_to_all_rider(x):
    def parts(refs):
        x_ref, o_ref, send_sems, recv_sems, local_sem = refs
        me, _, chips = _place()
        sems = (send_sems, recv_sems)
        mine = 2 * me[0] + me[1]
        local = pltpu.make_async_copy(x_ref.at[mine], o_ref.at[mine], local_sem)
        sends = [_remote(x_ref.at[2 * px + py], o_ref.at[mine], sems, j, (px, py, me[2])) for j, (px, py) in enumerate(chips)]
        arrival = lambda j: _remote(x_ref.at[mine], o_ref.at[2 * chips[j][0] + chips[j][1]], sems, j, me)
        return local, sends, arrival

    def start(refs):
        local, sends, _ = parts(refs)
        for cp in [local] + sends:
            cp.start()

    def finish(refs):
        local, sends, arrival = parts(refs)
        for j, cp in enumerate(sends):
            cp.wait_send()
            arrival(j).wait_recv()
        local.wait()

    return _Rider([x], [jax.ShapeDtypeStruct(x.shape, x.dtype)],
                  [pltpu.SemaphoreType.DMA((N_CHIP - 1,)), pltpu.SemaphoreType.DMA((N_CHIP - 1,)), pltpu.SemaphoreType.DMA],
                  start, finish, lambda outs: outs[0])


def _sum_slots(x, *, name):
    n, R, _ = x.shape
    tr = _pick(R, (1024, 512, 256, 128, 64, 32, 16, 8))

    def body(x_ref, o_ref):
        acc = x_ref[0].astype(F32)
        for j in range(1, n):
            acc = acc + x_ref[j].astype(F32)
        o_ref[...] = acc

    return pl.pallas_call(
        body, name=name, out_shape=jax.ShapeDtypeStruct((R, 128), F32), grid=(R // tr,),
        in_specs=[pl.BlockSpec((n, tr, 128), lambda i: (0, i, 0))], out_specs=pl.BlockSpec((tr, 128), lambda i: (i, 0)),
        compiler_params=_params(("parallel",)),
    )(x)


def _adamw(w, g, m, v, *, name):
    shape = w.shape
    cols = shape[-1]
    rows = int(np.prod(shape[:-1]))
    tr = next((t for t in (1024, 512, 256, 128, 64, 32, 16, 8) if rows % t == 0 and t * cols * 4 <= (1 << 20)), rows)

    def body(w_ref, g_ref, m_ref, v_ref, d_ref, mo_ref, vo_ref):
        g_ = g_ref[...]
        m_ = ADAM_B1 * m_ref[...] + (1.0 - ADAM_B1) * g_
        v_ = ADAM_B2 * v_ref[...] + (1.0 - ADAM_B2) * jnp.square(g_)
        m_hat = m_ / (1.0 - ADAM_B1 ** ADAM_STEP)
        v_hat = v_ / (1.0 - ADAM_B2 ** ADAM_STEP)
        d_ref[...] = -ADAM_LR * (m_hat / (jnp.sqrt(v_hat) + ADAM_EPS) + ADAM_WD * w_ref[...])
        mo_ref[...] = m_
        vo_ref[...] = v_

    blk = pl.BlockSpec((tr, cols), lambda i: (i, 0))
    outs = pl.pallas_call(
        body, name=name, out_shape=tuple(jax.ShapeDtypeStruct((rows, cols), F32) for _ in range(3)), grid=(rows // tr,),
        in_specs=[blk] * 4, out_specs=(blk,) * 3, compiler_params=_params(("parallel",)),
    )(*(a.reshape(rows, cols) for a in (w, g, m, v)))
    return tuple(o.reshape(shape) for o in outs)


_WEIGHTS = ('g_mix', 'w_in', 'b_fox_forget', 'w_gla_gate', 'b_gla_gate', 'g_gla_out', 'g_mla_q', 'w_mla_uq', 'g_mla_kv',
            'w_mla_ukv', 'b_branch_gate', 'w_up_fox', 'w_up_gla', 'w_up_mla', 'w_out', 'g_xa', 'g_mem', 'w_xq', 'w_xkv',
            'w_xo', 'g_mlp', 'w_mlp1', 'w_mlp2', 'g_final')
_SHARDED = (('w_in', 1), ('w_gla_gate', 2), ('w_mla_uq', 2), ('w_mla_ukv', 2), ('w_up_fox', 2), ('w_up_gla', 2),
            ('w_up_mla', 2), ('w_out', 1), ('w_xq', 1), ('w_xkv', 1), ('w_xo', 2), ('w_mlp1', 2), ('w_mlp2', 1))
_REPLICATED = tuple(n for n in _WEIGHTS if n not in dict(_SHARDED))
_ROW_PAD = 1024
_SMALL_ROW_PAD = 8
_PIECE_ROWS = 16


def _pack(flats, lead, row_pad=_ROW_PAD):
    if all(int(np.prod(a.shape[lead:])) % 128 == 0 for a in flats):
        def block(a):
            a = a.reshape(a.shape[:lead] + (-1, 128))
            return jnp.pad(a, [(0, 0)] * lead + [(0, -a.shape[lead] % _PIECE_ROWS), (0, 0)])
        cat = jnp.concatenate([block(a) for a in flats], axis=lead)
        rows = cat.shape[lead]
        return jnp.pad(cat, [(0, 0)] * lead + [(0, -(-rows // row_pad) * row_pad - rows), (0, 0)])
    cat = jnp.concatenate([a.reshape(a.shape[:lead] + (-1,)) for a in flats], axis=-1)
    n = cat.shape[-1]
    total = -(-n // (128 * row_pad)) * (128 * row_pad)
    cat = jnp.pad(cat, [(0, 0)] * lead + [(0, total - n)])
    return cat.reshape(cat.shape[:lead] + (total // 128, 128))


def _unpack(buf, shapes, lead):
    sizes = [int(np.prod(shp)) for shp in shapes]
    out, off = [], 0
    if all(n % 128 == 0 for n in sizes):
        for shp, n in zip(shapes, sizes):
            rows = buf[(slice(None),) * lead + (slice(off, off + n // 128),)]
            out.append(rows.reshape(buf.shape[:lead] + tuple(shp)))
            off += -(-(n // 128) // _PIECE_ROWS) * _PIECE_ROWS
        return out
    flat = buf.reshape(buf.shape[:lead] + (-1,))
    for shp, n in zip(shapes, sizes):
        out.append(flat[..., off:off + n].reshape(buf.shape[:lead] + tuple(shp)))
        off += n
    return out


def _to_whole(g, axis):
    if axis == 1:
        return g.transpose(1, 0, 2, 3).reshape(g.shape[1], N_DEV * g.shape[2], g.shape[3])
    return g.transpose(1, 2, 0, 3).reshape(g.shape[1], g.shape[2], N_DEV * g.shape[3])


def _to_shards(w, axis):
    L, R, C = w.shape
    if axis == 1:
        return w.reshape(L, N_DEV, R // N_DEV, C).transpose(1, 0, 2, 3)
    return w.reshape(L, R, N_DEV, C // N_DEV).transpose(2, 0, 1, 3)


def kernel(x, mem, g_mix, w_in, b_fox_forget, w_gla_gate, b_gla_gate, g_gla_out, g_mla_q, w_mla_uq, g_mla_kv, w_mla_ukv, b_branch_gate, w_up_fox, w_up_gla, w_up_mla, w_out, g_xa, g_mem, w_xq, w_xkv, w_xo, g_mlp, w_mlp1, w_mlp2, g_final, loss_target, m_g_mix, m_w_in, m_b_fox_forget, m_w_gla_gate, m_b_gla_gate, m_g_gla_out, m_g_mla_q, m_w_mla_uq, m_g_mla_kv, m_w_mla_ukv, m_b_branch_gate, m_w_up_fox, m_w_up_gla, m_w_up_mla, m_w_out, m_g_xa, m_g_mem, m_w_xq, m_w_xkv, m_w_xo, m_g_mlp, m_w_mlp1, m_w_mlp2, m_g_final, v_g_mix, v_w_in, v_b_fox_forget, v_w_gla_gate, v_b_gla_gate, v_g_gla_out, v_g_mla_q, v_w_mla_uq, v_g_mla_kv, v_w_mla_ukv, v_b_branch_gate, v_w_up_fox, v_w_up_gla, v_w_up_mla, v_w_out, v_g_xa, v_g_mem, v_w_xq, v_w_xkv, v_w_xo, v_g_mlp, v_w_mlp1, v_w_mlp2, v_g_final):
    wts = dict(zip(_WEIGHTS, (g_mix, w_in, b_fox_forget, w_gla_gate, b_gla_gate, g_gla_out, g_mla_q, w_mla_uq, g_mla_kv,
                              w_mla_ukv, b_branch_gate, w_up_fox, w_up_gla, w_up_mla, w_out, g_xa, g_mem, w_xq, w_xkv, w_xo,
                              g_mlp, w_mlp1, w_mlp2, g_final)))
    mom1 = dict(zip(_WEIGHTS, (m_g_mix, m_w_in, m_b_fox_forget, m_w_gla_gate, m_b_gla_gate, m_g_gla_out, m_g_mla_q,
                               m_w_mla_uq, m_g_mla_kv, m_w_mla_ukv, m_b_branch_gate, m_w_up_fox, m_w_up_gla, m_w_up_mla,
                               m_w_out, m_g_xa, m_g_mem, m_w_xq, m_w_xkv, m_w_xo, m_g_mlp, m_w_mlp1, m_w_mlp2, m_g_final)))
    mom2 = dict(zip(_WEIGHTS, (v_g_mix, v_w_in, v_b_fox_forget, v_w_gla_gate, v_b_gla_gate, v_g_gla_out, v_g_mla_q,
                               v_w_mla_uq, v_g_mla_kv, v_w_mla_ukv, v_b_branch_gate, v_w_up_fox, v_w_up_gla, v_w_up_mla,
                               v_w_out, v_g_xa, v_g_mem, v_w_xq, v_w_xkv, v_w_xo, v_g_mlp, v_w_mlp1, v_w_mlp2, v_g_final)))
    depth = g_mix.shape[0]

    names = [n for n, _ in _SHARDED]
    axes = dict(_SHARDED)
    shard = {n: wts[n] for n in names}
    shard['w_in'] = _pad_w_in(w_in)
    rep = {n: wts[n] for n in _REPLICATED}
    ps = [_LayerParams(rep, l) for l in range(depth)]

    def gather(group, l):
        rider = _gather_rider([shard[n][l:l + 1].astype(BF16) for n in group], [axes[n] for n in group])
        return rider, lambda whole: ps[l].w.update({n: w[0] for n, w in zip(group, whole)})

    first, sink = gather(['w_in'], 0)
    sink(_run_rider(first, name="gather_w_in_0"))
    hooks = {(0, 'fox_fwd'): gather([n for n in names if n != 'w_in'], 0)}
    for l in range(1, depth):
        hooks[(l - 1, 'mla_fwd')] = gather(names, l)

    core = lax.axis_index("c").astype(jnp.int32).reshape(1)
    late = ['w_in', 'w_gla_gate', 'w_mla_uq', 'w_mla_ukv']
    groups = {'early': [n for n in names if n not in late], 'late': late}
    small_grads, landed = {}, {}

    def exchange(l, g, which):
        slots = _pack([_to_shards(g[n][None], axes[n]).astype(BF16) for n in groups[which]], 1)
        slots = slots.reshape((N_CHIP, 2) + slots.shape[1:])
        paired = _pair_sum(slots, _sibling_swap(slots, name=f"swap_grads_{which}_{l}"), core, name=f"pair_grads_{which}_{l}")
        return _chip_all_to_all_rider(paired), lambda got: landed.update({(l, which): got})

    def half_done(l, g):
        hooks[(l, 'mla_bwd')] = exchange(l, g, 'early')

    def layer_done(l, g):
        small_grads[l] = g
        rider, sink = exchange(l, g, 'late')
        if l > 0:
            hooks[(l - 1, 'fox_bwd')] = (rider, sink)
        else:
            sink(_run_rider(rider, name=f"scatter_grads_late_{l}"))

    loss, dx, dg_final = _local_step(x[0], mem[0], loss_target[0], ps, g_final, hooks, half_done, layer_done)
    loss = lax.psum(loss[0, 0], _MESH_AXES)

    grad = {}
    for which, group in groups.items():
        shapes = [(1,) + shard[n].shape[1:] for n in group]
        per_layer = [_unpack(_sum_slots(landed[(l, which)], name=f"sum_grads_{which}_{l}"), shapes, 0) for l in range(depth)]
        grad.update({n: jnp.concatenate([per_layer[l][i] for l in range(depth)], axis=0) for i, n in enumerate(group)})
    grad['w_in'] = _unpad_w_in(grad['w_in'])
    grads = small_grads
    small = [dg_final if n == 'g_final' else jnp.stack([grads[l][n] for l in range(depth)]) for n in _REPLICATED]
    small_shapes = [wts[n].shape for n in _REPLICATED]
    small_sum = _sum_slots(_all_gather(_pack(small, 0, _SMALL_ROW_PAD), name="gather_small_grads"), name="sum_small_grads")
    grad.update(dict(zip(_REPLICATED, _unpack(small_sum, small_shapes, 0))))

    delta, new_m, new_v = {}, {}, {}
    for n, _ in _SHARDED:
        delta[n], new_m[n], new_v[n] = _adamw(wts[n], grad[n], mom1[n], mom2[n], name=f"adamw_{n}")
    packed = [_pack([d[n] for n in _REPLICATED], 0, _SMALL_ROW_PAD) for d in (wts, mom1, mom2)]
    outs = _adamw(packed[0], small_sum, packed[1], packed[2], name="adamw_small")
    for d, o in zip((delta, new_m, new_v), outs):
        d.update(dict(zip(_REPLICATED, _unpack(o, small_shapes, 0))))

    return (loss, dx[None], *[grad[n] for n in _WEIGHTS], *[delta[n] for n in _WEIGHTS],
            *[new_m[n] for n in _WEIGHTS], *[new_v[n] for n in _WEIGHTS])
```

```python
import jax
import jax.numpy as jnp
import numpy as np
from jax import lax
from jax.experimental import pallas as pl
from jax.experimental.pallas import tpu as pltpu

F32 = jnp.float32
BF16 = jnp.bfloat16

EPS = 1e-6
CHUNK = 64
FOX_HEADS, FOX_HD = 4, 64
GLA_HEADS, GLA_DK, GLA_DV, GLA_RANK, GLA_TAU = 4, 64, 128, 16, 16.0
MLA_HEADS, MLA_Q_RANK, MLA_KV_RANK, MLA_NOPE, MLA_ROPE, MLA_VD = 4, 256, 128, 64, 32, 64
ROPE_BASE = 10000.0
XA_HEADS, XA_HD = 4, 128
ADAM_LR, ADAM_B1, ADAM_B2, ADAM_EPS, ADAM_WD, ADAM_STEP = 0.001, 0.9, 0.999, 1e-08, 0.01, 10

N_DEV = 8
V7X_VMEM_LIMIT = 56 * 1024 * 1024
NEG = -1e30

NN = ((1,), (0,))
NT = ((1,), (1,))
TN = ((0,), (0,))


def _dot(a, b, dims):
    return lax.dot_general(a.astype(BF16), b.astype(BF16), (dims, ((), ())), preferred_element_type=F32)


@jax.custom_vjp
def bdot(a, b):
    return _dot(a, b, NN)


bdot.defvjp(lambda a, b: (_dot(a, b, NN), (a, b)),
            lambda res, g: (_dot(g, res[1], NT), _dot(res[0], g, TN)))


@jax.custom_vjp
def bdot_nt(a, b):
    return _dot(a, b, NT)


bdot_nt.defvjp(lambda a, b: (_dot(a, b, NT), (a, b)),
               lambda res, g: (_dot(g, res[1], NN), _dot(g, res[0], TN)))


@jax.custom_vjp
def bdot_tn(a, b):
    return _dot(a, b, TN)


bdot_tn.defvjp(lambda a, b: (_dot(a, b, TN), (a, b)),
               lambda res, g: (_dot(res[1], g, NT), _dot(res[0], g, NN)))


def _split2(x):
    hi = x.astype(BF16)
    lo = (x - hi.astype(F32)).astype(BF16)
    return hi, lo


def _tri(n, lower):
    r = lax.broadcasted_iota(jnp.int32, (n, n), 0)
    c = lax.broadcasted_iota(jnp.int32, (n, n), 1)
    return jnp.where((r >= c) if lower else (r <= c), 1.0, 0.0).astype(BF16)


def _log_sigmoid(x):
    return jnp.minimum(x, 0.0) - jnp.log(1.0 + jnp.exp(-jnp.abs(x)))


def _sigmoid(x):
    return 1.0 / (1.0 + jnp.exp(-x))


def _rms(x, g):
    return x * lax.rsqrt(jnp.mean(x * x, axis=-1, keepdims=True) + EPS) * g


def _pick(dim, prefs):
    for p in prefs:
        if dim % p == 0:
            return p
    return dim


def _params(sem):
    return pltpu.CompilerParams(dimension_semantics=sem, vmem_limit_bytes=V7X_VMEM_LIMIT)


def _rms_vjp(x, g, dy, dres):
    rstd = lax.rsqrt(jnp.mean(x * x, axis=-1, keepdims=True) + EPS)
    xh = x * rstd
    gdy = dy * g
    dx = (gdy - xh * jnp.mean(gdy * xh, axis=-1, keepdims=True)) * rstd
    return (dx if dres is None else dx + dres), jnp.sum(dy * xh, axis=0, keepdims=True)


def _mm(a, b, *, mode, out_dtype, name, act=None, residual=None, drelu_of=None, norm_bwd=None, b_cols=None,
        tm=None, tn=None, tk=None):
    b_off, b_width = b_cols or (0, b.shape[1])
    if mode == 'nn':
        (M, K), N = a.shape, b_width
    elif mode == 'nt':
        (M, K), N = a.shape, b.shape[0]
    else:
        (K, M), N = a.shape, b_width
    tm = tm or _pick(M, (1024, 512, 256, 128))
    tn = tn or _pick(N, (1024, 1920, 1152, 768, 640, 512, 384, 256, 128))
    tk = tk or _pick(K, (1024, 1920, 1152, 640, 512, 256, 128))
    nk = K // tk
    dims = {'nn': NN, 'nt': NT, 'tn': TN}[mode]
    a_spec = pl.BlockSpec((tk, tm), lambda i, j, k: (k, i)) if mode == 'tn' else pl.BlockSpec((tm, tk), lambda i, j, k: (i, k))
    if mode == 'nt':
        b_spec = pl.BlockSpec((tn, tk), lambda i, j, k, o=b_off // tk: (j, k + o))
    else:
        b_spec = pl.BlockSpec((tk, tn), lambda i, j, k, o=b_off // tn: (k, j + o))
    o_spec = pl.BlockSpec((tm, tn), lambda i, j, k: (i, j))
    extra = [e for e in (residual, drelu_of) if e is not None]
    extra_specs = [o_spec] * len(extra)
    out_shape, out_specs, n_out = jax.ShapeDtypeStruct((M, N), out_dtype), o_spec, 1
    if norm_bwd is not None:
        x_in, g_in, dres_in = norm_bwd
        assert tn == N and residual is None and drelu_of is None
        vec = pl.BlockSpec((1, N), lambda i, j, k: (0, 0))
        extra, extra_specs = [x_in, g_in.reshape(1, N), dres_in], [o_spec, vec, o_spec]
        out_shape = (jax.ShapeDtypeStruct((M, N), F32), jax.ShapeDtypeStruct((M, N), BF16), jax.ShapeDtypeStruct((1, N), F32))
        out_specs, n_out = (o_spec, o_spec, vec), 3

    def body(a_ref, b_ref, *rest):
        o_ref = rest[len(extra)]
        first_rows = pl.program_id(0) == 0
        at = a_ref[...]
        if act == 'relu2':
            at = jnp.square(jnp.maximum(at.astype(F32), 0.0))
        part = _dot(at, b_ref[...], dims)

        def finish(acc):
            if norm_bwd is not None:
                dx, dg = _rms_vjp(rest[0][...], rest[1][...], acc, rest[2][...])
                o_ref[...] = dx
                rest[len(extra) + 1][...] = dx.astype(BF16)
                dg_ref = rest[len(extra) + 2]

                @pl.when(first_rows)
                def _():
                    dg_ref[...] = dg

                @pl.when(jnp.logical_not(first_rows))
                def _():
                    dg_ref[...] += dg
                return
            idx = 0
            if residual is not None:
                acc = acc + rest[idx][...]
                idx += 1
            if drelu_of is not None:
                acc = acc * (2.0 * jnp.maximum(rest[idx][...].astype(F32), 0.0))
            o_ref[...] = acc.astype(out_dtype)

        if nk == 1:
            finish(part)
        else:
            acc_ref = rest[len(extra) + n_out]
            k = pl.program_id(2)

            @pl.when(k == 0)
            def _():
                acc_ref[...] = part

            @pl.when(k > 0)
            def _():
                acc_ref[...] += part

            @pl.when(k == nk - 1)
            def _():
                finish(acc_ref[...])

    return pl.pallas_call(
        body, name=name,
        out_shape=out_shape,
        grid=(M // tm, N // tn, nk),
        in_specs=[a_spec, b_spec] + extra_specs,
        out_specs=out_specs,
        scratch_shapes=[] if nk == 1 else [pltpu.VMEM((tm, tn), F32)],
        compiler_params=_params(("arbitrary" if norm_bwd is not None else "parallel", "parallel", "arbitrary")),
    )(a, b, *extra)


def _rms_fwd(x, g, *, name, out_dtype=BF16):
    S, D = x.shape
    tr = _pick(S, (512, 256, 128))

    def body(x_ref, g_ref, o_ref):
        o_ref[...] = _rms(x_ref[...], g_ref[...]).astype(out_dtype)

    return pl.pallas_call(
        body, name=name, out_shape=jax.ShapeDtypeStruct((S, D), out_dtype), grid=(S // tr,),
        in_specs=[pl.BlockSpec((tr, D), lambda i: (i, 0)), pl.BlockSpec((1, D), lambda i: (0, 0))],
        out_specs=pl.BlockSpec((tr, D), lambda i: (i, 0)),
        compiler_params=_params(("parallel",)),
    )(x, g.reshape(1, D))


def _rms_bwd(x, g, dy, dres, *, name):
    S, D = x.shape
    tr = _pick(S, (512, 256, 128))

    def body(x_ref, g_ref, dy_ref, *rest):
        dx_ref, dxb_ref, dg_ref = rest[-3], rest[-2], rest[-1]
        dx, part = _rms_vjp(x_ref[...], g_ref[...], dy_ref[...].astype(F32), None if dres is None else rest[0][...])
        dx_ref[...] = dx
        dxb_ref[...] = dx.astype(BF16)

        @pl.when(pl.program_id(0) == 0)
        def _():
            dg_ref[...] = part

        @pl.when(pl.program_id(0) > 0)
        def _():
            dg_ref[...] += part

    row = pl.BlockSpec((tr, D), lambda i: (i, 0))
    vec = pl.BlockSpec((1, D), lambda i: (0, 0))
    ins = [x, g.reshape(1, D), dy] + ([dres] if dres is not None else [])
    return pl.pallas_call(
        body, name=name,
        out_shape=(jax.ShapeDtypeStruct((S, D), F32), jax.ShapeDtypeStruct((S, D), BF16), jax.ShapeDtypeStruct((1, D), F32)),
        grid=(S // tr,),
        in_specs=[row, vec, row] + ([row] if dres is not None else []),
        out_specs=(row, row, vec),
        compiler_params=_params(("arbitrary",)),
    )(*ins)


def _loss_head(x, g, target, *, name):
    S, D = x.shape
    tr = _pick(S, (512, 256, 128))

    def body(x_ref, g_ref, t_ref, l_ref, dx_ref, dxb_ref, dg_ref):
        x_ = x_ref[...]
        g_ = g_ref[...]
        rstd = lax.rsqrt(jnp.mean(x_ * x_, axis=-1, keepdims=True) + EPS)
        xh = x_ * rstd
        err = xh * g_ - t_ref[...]
        lpart = (0.5 / D) * jnp.sum(jnp.sum(err * err, axis=-1, keepdims=True), axis=0, keepdims=True)
        dy = err * (1.0 / D)
        gdy = dy * g_
        dx = (gdy - xh * jnp.mean(gdy * xh, axis=-1, keepdims=True)) * rstd
        dx_ref[...] = dx
        dxb_ref[...] = dx.astype(BF16)
        gpart = jnp.sum(dy * xh, axis=0, keepdims=True)

        @pl.when(pl.program_id(0) == 0)
        def _():
            dg_ref[...] = gpart
            l_ref[...] = lpart

        @pl.when(pl.program_id(0) > 0)
        def _():
            dg_ref[...] += gpart
            l_ref[...] += lpart

    row = pl.BlockSpec((tr, D), lambda i: (i, 0))
    vec = pl.BlockSpec((1, D), lambda i: (0, 0))
    return pl.pallas_call(
        body, name=name,
        out_shape=(jax.ShapeDtypeStruct((1, 1), F32), jax.ShapeDtypeStruct((S, D), F32), jax.ShapeDtypeStruct((S, D), BF16),
                   jax.ShapeDtypeStruct((1, D), F32)),
        grid=(S // tr,),
        in_specs=[row, vec, row],
        out_specs=(pl.BlockSpec((1, 1), lambda i: (0, 0)), row, row, vec),
        compiler_params=_params(("arbitrary",)),
    )(x, g.reshape(1, D), target)


def _mask_of(mask, tq, tk, keys_first=False):
    shape, q_axis = ((tk, tq), 1) if keys_first else ((tq, tk), 0)
    qpos = lax.broadcasted_iota(jnp.int32, shape, q_axis)
    kpos = lax.broadcasted_iota(jnp.int32, shape, 1 - q_axis)
    if mask == 'causal':
        return kpos <= qpos
    return kpos <= (qpos | (CHUNK - 1))


LANES = 128
LOG2E = 1.4426950408889634


def _lane_group(j, w, width):
    lane = lax.broadcasted_iota(jnp.int32, (1, width), 1)
    return (lane >= j * w) & (lane < (j + 1) * w)


def _only(x, j, w):
    if w == x.shape[1]:
        return x
    return jnp.where(_lane_group(j, w, x.shape[1]), x, jnp.zeros_like(x))


def _per_head(cols, w):
    out = cols[-1]
    for j in range(len(cols) - 2, -1, -1):
        out = jnp.where(_lane_group(j, w, LANES), cols[j], out)
    return out


def _side_by_side(xs):
    return xs[0] if len(xs) == 1 else jnp.concatenate(xs, axis=1)


def _on_top(xs):
    return xs[0] if len(xs) == 1 else jnp.concatenate(xs, axis=0)


def _stacked(x, hp, w):
    return _on_top([_only(x, j, w) for j in range(hp)])


def _col_block(entry, rows, idx):
    arr, off, width = entry
    return pl.BlockSpec((rows, width), lambda i, j, o=off // width: (idx(i, j), o))


def _attn_fwd(qk, v, H, cq, ck, *, scale, mask, name, rider=None):
    Sq, Sk = qk[0][0][0].shape[0], v[0].shape[0]
    dv = v[2] // H
    w0 = qk[0][2]
    hp = LANES // w0
    G = H // hp
    assert dv == w0 and not qk[0][3] and all(sh and H * w == LANES for _, _, w, sh in qk[1:])
    tq = _pick(Sq, (512, 256, 128))
    tk = tq if mask else _pick(Sk, (512, 256, 128))
    nq, nk = Sq // tq, Sk // tk
    bias = cq is not None
    npart = len(qk)

    def body(*refs):
        refs = split(refs)
        q_refs, k_refs = refs[0:2 * npart:2], refs[1:2 * npart:2]
        v_ref = refs[2 * npart]
        cq_ref, ck_ref = (refs[2 * npart + 1], refs[2 * npart + 2]) if bias else (None, None)
        o_ref, lse_ref, m_s, l_s, acc_s = refs[-5:]
        qi, ki = pl.program_id(0), pl.program_id(1)

        @pl.when(ki == 0)
        def _():
            m_s[...] = jnp.full(m_s.shape, NEG, F32)
            l_s[...] = jnp.zeros(l_s.shape, F32)
            acc_s[...] = jnp.zeros(acc_s.shape, F32)

        def rows_of(vals):
            return _on_top([jnp.broadcast_to(r, (w0, tq)) for r in vals])

        def compute(masked):
            keep = _mask_of(mask, tq, tk, keys_first=True) if masked else None
            for g in range(G):
                lanes = slice(g * LANES, (g + 1) * LANES)
                q128, k128, v128 = q_refs[0][:, lanes], k_refs[0][:, lanes], v_ref[:, lanes]
                ps, alphas = [], []
                extras = list(zip(qk, q_refs, k_refs))[1:]
                k_all = _side_by_side([k128] + [k_ref[...] for _, _, k_ref in extras])
                for j in range(hp):
                    h = g * hp + j
                    q_all = _side_by_side([_only(q128, j, w0)] + [_only(q_ref[...], h, w) for (_, _, w, _), q_ref, _ in extras])
                    s = _dot(k_all, q_all, NT) * scale
                    if bias:
                        s = s + (cq_ref[h:h + 1, :] - ck_ref[:, h:h + 1])
                    if masked:
                        s = jnp.where(keep, s, NEG)
                    m_prev = m_s[h:h + 1, :]
                    m_new = jnp.maximum(m_prev, jnp.max(s, axis=0, keepdims=True))
                    alpha = jnp.exp(m_prev - m_new)
                    p = jnp.exp(s - m_new)
                    l_s[h:h + 1, :] = alpha * l_s[h:h + 1, :] + jnp.sum(p, axis=0, keepdims=True)
                    m_s[h:h + 1, :] = m_new
                    ps.append(p.astype(BF16))
                    alphas.append(alpha)
                acc_s[g] = rows_of(alphas) * acc_s[g] + _dot(_stacked(v128, hp, w0), _on_top(ps), TN)

        if mask is None:
            compute(False)
        else:
            pl.when(ki < qi)(lambda: compute(False))
            pl.when(ki == qi)(lambda: compute(True))

        @pl.when(ki == ((nk - 1) if mask is None else qi))
        def _():
            for g in range(G):
                norm = acc_s[g] / rows_of([l_s[g * hp + j:g * hp + j + 1, :] for j in range(hp)])
                o_ref[:, g * LANES:(g + 1) * LANES] = norm.T.astype(BF16)
            lse_ref[...] = jnp.zeros(lse_ref.shape, F32)
            lse_ref[0:H, :] = m_s[0:H, :] + jnp.log(l_s[0:H, :])

    q_idx = lambda i, j: i
    k_idx = (lambda i, j: jnp.minimum(i, j)) if mask else (lambda i, j: j)
    ins, in_specs = [], []
    for q_e, k_e, _, _ in qk:
        ins += [q_e[0], k_e[0]]
        in_specs += [_col_block(q_e, tq, q_idx), _col_block(k_e, tk, k_idx)]
    ins.append(v[0])
    in_specs.append(_col_block(v, tk, k_idx))
    if bias:
        in_specs += [pl.BlockSpec((8, tq), lambda i, j: (0, i)), pl.BlockSpec((tk, 8), lambda i, j: (k_idx(i, j), 0))]
        ins += [cq, ck]
    r_ins, r_in_specs, r_outs, r_out_specs, r_scratch, split = _carry(
        rider, len(ins), 2, lambda: (pl.program_id(0) == 0) & (pl.program_id(1) == 0),
        lambda: (pl.program_id(0) == nq - 1) & (pl.program_id(1) == nk - 1))
    res = pl.pallas_call(
        body, name=name,
        out_shape=(jax.ShapeDtypeStruct((Sq, H * dv), BF16), jax.ShapeDtypeStruct((8, Sq), F32), *r_outs),
        grid=(nq, nk), in_specs=in_specs + r_in_specs,
        out_specs=(pl.BlockSpec((tq, H * dv), lambda i, j: (i, 0)), pl.BlockSpec((8, tq), lambda i, j: (0, i)), *r_out_specs),
        scratch_shapes=[pltpu.VMEM((8, tq), F32), pltpu.VMEM((8, tq), F32), pltpu.VMEM((G, LANES, tq), F32)] + r_scratch,
        compiler_params=_params(("arbitrary", "arbitrary")) if rider else _params(("parallel", "arbitrary")),
    )(*ins, *r_ins)
    return (res[0], res[1], rider.post(res[2:])) if rider else res


def _attn_bwd(qk, v, H, o, do, lse, cq, ck, *, scale, mask, name, rider=None):
    Sq, Sk = qk[0][0][0].shape[0], v[0].shape[0]
    dv = v[2] // H
    w0 = qk[0][2]
    hp = LANES // w0
    G = H // hp
    tq = _pick(Sq, (512, 256, 128))
    tk = tq if mask else _pick(Sk, (512, 256, 128))
    nq, nk = Sq // tq, Sk // tk
    bias = cq is not None
    npart = len(qk)
    n_in = 2 * npart + 4 + (2 if bias else 0)

    def body(*refs):
        refs = split(refs)
        q_refs, k_refs = refs[0:2 * npart:2], refs[1:2 * npart:2]
        v_ref, o_ref, do_ref, lse_ref = refs[2 * npart:2 * npart + 4]
        cq_ref, ck_ref = (refs[2 * npart + 4], refs[2 * npart + 5]) if bias else (None, None)
        outs = refs[n_in:]
        dq_refs, dk_refs, dv_ref = outs[:npart], outs[npart:2 * npart], outs[2 * npart]
        dck_ref, dcq_ref = (outs[2 * npart + 1], outs[2 * npart + 2]) if bias else (None, None)
        dk_accs, dv_acc = refs[-(npart + 1):-1], refs[-1]
        ki, qi = pl.program_id(0), pl.program_id(1)
        first_q = ki if mask else 0

        @pl.when((ki == 0) & (qi == 0))
        def _():
            for r in dq_refs:
                r[...] = jnp.zeros(r.shape, F32)
            if bias:
                dcq_ref[...] = jnp.zeros(dcq_ref.shape, F32)

        @pl.when(qi == first_q)
        def _():
            for r in dk_accs:
                r[...] = jnp.zeros(r.shape, F32)
            dv_acc[...] = jnp.zeros(dv_acc.shape, F32)
            if bias:
                dck_ref[...] = jnp.zeros(dck_ref.shape, F32)

        def compute(masked):
            keep = _mask_of(mask, tq, tk, keys_first=True) if masked else None
            rows = pl.ds(pl.multiple_of(qi * tq, tq), tq)
            extras = list(zip(qk, q_refs, k_refs, dq_refs, dk_accs))[1:]
            for g in range(G):
                lanes = slice(g * LANES, (g + 1) * LANES)
                q128, k128, v128 = q_refs[0][:, lanes], k_refs[0][:, lanes], v_ref[:, lanes]
                do128, o128 = do_ref[:, lanes], o_ref[:, lanes]
                prod = do128.astype(F32) * o128.astype(F32)
                ps, dss = [], []
                k_all = _side_by_side([k128] + [e[2][...] for e in extras])
                for j in range(hp):
                    h = g * hp + j
                    q_all = _side_by_side([_only(q128, j, w0)] + [_only(e[1][...], h, e[0][2]) for e in extras])
                    s = _dot(k_all, q_all, NT) * (scale * LOG2E)
                    if bias:
                        s = s - ck_ref[:, h:h + 1] * LOG2E
                    if masked:
                        s = jnp.where(keep, s, NEG)
                    row = lse_ref[h:h + 1, :] - cq_ref[h:h + 1, :] if bias else lse_ref[h:h + 1, :]
                    p = jnp.exp2(s - row * LOG2E)
                    dp = _dot(v128, _only(do128, j, w0), NT)
                    delta = jnp.sum(_only(prod, j, w0), axis=1, keepdims=True).T
                    ds = p * (dp - delta)
                    if bias:
                        dck_ref[:, h:h + 1] -= jnp.sum(ds, axis=1, keepdims=True)
                        dcq_ref[h:h + 1, rows] += jnp.sum(ds, axis=0, keepdims=True)
                    ps.append(p.astype(BF16))
                    dss.append((ds * scale).astype(BF16))
                for (_, _, w, _), q_ref, k_ref, dq_ref, dk_acc in extras:
                    heads = range(g * hp, (g + 1) * hp)
                    dk_acc[...] += _dot(_side_by_side(dss), _on_top([_only(q_ref[...], h, w) for h in heads]), NN)
                    dq_ref[rows, :] += _dot(_on_top(dss), _on_top([_only(k_ref[...], h, w) for h in heads]), TN)
                dv_acc[:, lanes] += _dot(_side_by_side(ps), _stacked(do128, hp, w0), NN)
                dk_accs[0][:, lanes] += _dot(_side_by_side(dss), _stacked(q128, hp, w0), NN)
                dq_refs[0][rows, lanes] += _dot(_on_top(dss), _stacked(k128, hp, w0), TN)

        if mask is None:
            compute(False)
        else:
            pl.when(qi > ki)(lambda: compute(False))
            pl.when(qi == ki)(lambda: compute(True))

        @pl.when(qi == nq - 1)
        def _():
            for r, acc in zip(dk_refs, dk_accs):
                r[...] = acc[...]
            dv_ref[...] = dv_acc[...]

    q_idx = (lambda j, i: jnp.maximum(i, j)) if mask else (lambda j, i: i)
    k_idx = lambda j, i: j
    ins, in_specs, dq_shapes, dq_specs, dk_shapes, dk_specs, scratch = [], [], [], [], [], [], []
    for q_e, k_e, w, shared in qk:
        ins += [q_e[0], k_e[0]]
        in_specs += [_col_block(q_e, tq, q_idx), _col_block(k_e, tk, k_idx)]
        dq_shapes.append(jax.ShapeDtypeStruct((Sq, H * w), F32))
        dq_specs.append(pl.BlockSpec((Sq, H * w), lambda j, i: (0, 0)))
        kw = k_e[2]
        dk_shapes.append(jax.ShapeDtypeStruct((Sk, kw), F32))
        dk_specs.append(pl.BlockSpec((tk, kw), lambda j, i: (j, 0)))
        scratch.append(pltpu.VMEM((tk, kw), F32))
    row_q = lambda width: pl.BlockSpec((tq, width), lambda j, i: (q_idx(j, i), 0))
    per_q = pl.BlockSpec((8, tq), lambda j, i: (0, q_idx(j, i)))
    ins += [v[0], o, do, lse]
    in_specs += [_col_block(v, tk, k_idx), row_q(H * dv), row_q(H * dv), per_q]
    out_shape = dq_shapes + dk_shapes + [jax.ShapeDtypeStruct((Sk, H * dv), F32)]
    out_specs = dq_specs + dk_specs + [pl.BlockSpec((tk, H * dv), lambda j, i: (j, 0))]
    if bias:
        in_specs += [per_q, pl.BlockSpec((tk, 8), lambda j, i: (j, 0))]
        ins += [cq, ck]
        out_shape += [jax.ShapeDtypeStruct((Sk, 8), F32), jax.ShapeDtypeStruct((8, Sq), F32)]
        out_specs += [pl.BlockSpec((tk, 8), lambda j, i: (j, 0)), pl.BlockSpec((8, Sq), lambda j, i: (0, 0))]
    scratch.append(pltpu.VMEM((tk, H * dv), F32))
    n_out = len(out_shape)
    r_ins, r_in_specs, r_outs, r_out_specs, r_scratch, split = _carry(
        rider, len(ins), n_out, lambda: (pl.program_id(0) == 0) & (pl.program_id(1) == 0),
        lambda: (pl.program_id(0) == nk - 1) & (pl.program_id(1) == nq - 1))
    res = pl.pallas_call(
        body, name=name, out_shape=tuple(out_shape + r_outs), grid=(nk, nq), in_specs=in_specs + r_in_specs,
        out_specs=tuple(out_specs + r_out_specs), scratch_shapes=scratch + r_scratch,
        compiler_params=_params(("arbitrary", "arbitrary")),
    )(*ins, *r_ins)
    own = (list(res[:npart]), list(res[npart:2 * npart]), res[2 * npart]) + tuple(res[2 * npart + 1:n_out])
    return own + (rider.post(res[n_out:]),) if rider else own


def _split3_dot(x, t):
    hi = x.astype(BF16)
    r1 = x - hi.astype(F32)
    mid = r1.astype(BF16)
    lo = (r1 - mid.astype(F32)).astype(BF16)
    return _dot(hi, t, NN) + _dot(mid, t, NN) + _dot(lo, t, NN)


def _fox_cum_fwd(ff_t, b, *, name):
    _, S = ff_t.shape
    tb = _pick(S, (512, 256, 128))

    def body(f_ref, b_ref, o_ref, carry):
        @pl.when(pl.program_id(0) == 0)
        def _():
            carry[...] = jnp.zeros(carry.shape, F32)

        lf = _log_sigmoid(f_ref[...] + b_ref[...])
        o_ref[...] = _split3_dot(lf, _tri(tb, False)) + carry[...]
        carry[...] += jnp.sum(lf, axis=1, keepdims=True)

    return pl.pallas_call(
        body, name=name, out_shape=jax.ShapeDtypeStruct((8, S), F32), grid=(S // tb,),
        in_specs=[pl.BlockSpec((8, tb), lambda i: (0, i)), pl.BlockSpec((8, 1), lambda i: (0, 0))],
        out_specs=pl.BlockSpec((8, tb), lambda i: (0, i)),
        scratch_shapes=[pltpu.VMEM((8, 1), F32)],
        compiler_params=_params(("arbitrary",)),
    )(ff_t, b)


def _fox_cum_bwd(ff_t, b, dcum_t, *, name):
    _, S = ff_t.shape
    tb = _pick(S, (512, 256, 128))
    nb = S // tb

    def body(f_ref, b_ref, dc_ref, df_ref, db_ref, carry):
        @pl.when(pl.program_id(0) == 0)
        def _():
            carry[...] = jnp.zeros(carry.shape, F32)
            db_ref[...] = jnp.zeros(db_ref.shape, F32)

        dc = dc_ref[...]
        dlf = _split3_dot(dc, _tri(tb, True)) + carry[...]
        carry[...] += jnp.sum(dc, axis=1, keepdims=True)
        df = dlf * _sigmoid(-(f_ref[...] + b_ref[...]))
        df_ref[...] = df
        db_ref[...] += jnp.sum(df, axis=1, keepdims=True)

    rev = lambda i: (0, nb - 1 - i)
    return pl.pallas_call(
        body, name=name,
        out_shape=(jax.ShapeDtypeStruct((8, S), F32), jax.ShapeDtypeStruct((8, 1), F32)), grid=(nb,),
        in_specs=[pl.BlockSpec((8, tb), rev), pl.BlockSpec((8, 1), lambda i: (0, 0)), pl.BlockSpec((8, tb), rev)],
        out_specs=(pl.BlockSpec((8, tb), rev), pl.BlockSpec((8, 1), lambda i: (0, 0))),
        scratch_shapes=[pltpu.VMEM((8, 1), F32)],
        compiler_params=_params(("arbitrary",)),
    )(ff_t, b, dcum_t)


GLA_W = GLA_HEADS * GLA_DK
GLA_BLOCK_CHUNKS = 4


def _same_chunk(n, lower):
    r = lax.broadcasted_iota(jnp.int32, (n, n), 0)
    c = lax.broadcasted_iota(jnp.int32, (n, n), 1)
    same = (r | (CHUNK - 1)) == (c | (CHUNK - 1))
    return jnp.where(same & (r >= c) if lower else same, 1.0, 0.0).astype(BF16)


def _chunk_mix(x, t, transpose):
    hi, lo = _split2(x)
    dims = TN if transpose else NN
    return _dot(t, hi, dims) + _dot(t, lo, dims)


@jax.custom_vjp
def chunk_cumsum(x):
    return _chunk_mix(x, _same_chunk(x.shape[0], True), False)


chunk_cumsum.defvjp(lambda x: (chunk_cumsum(x), None), lambda _, g: (_chunk_mix(g, _same_chunk(g.shape[0], True), True),))


@jax.custom_vjp
def chunk_total(x):
    return _chunk_mix(x, _same_chunk(x.shape[0], False), False)


chunk_total.defvjp(lambda x: (chunk_total(x), None), lambda _, g: (_chunk_mix(g, _same_chunk(g.shape[0], False), False),))


def _gla_block(q, k, zsm, wg, bg, go, vs, rs, states):
    n_chunks = q.shape[0] // CHUNK
    la = _log_sigmoid(bdot(zsm, wg) + bg) * (1.0 / GLA_TAU)
    end = chunk_total(la)
    kd = k * jnp.exp(end - chunk_cumsum(la))
    a = jnp.exp(end)
    qs = q * (GLA_DK ** -0.5)
    lane = lax.broadcasted_iota(jnp.int32, (1, GLA_W), 1)
    outs, new_states = [], []
    for h in range(GLA_HEADS):
        kdh = kd * jnp.where((lane >= h * GLA_DK) & (lane < (h + 1) * GLA_DK), 1.0, 0.0)
        st, o = states[h], []
        for c in range(n_chunks):
            rows = slice(c * CHUNK, (c + 1) * CHUNK)
            st = st * a[c * CHUNK:c * CHUNK + 1] + bdot_tn(vs[h][rows], kdh[rows])
            o.append(bdot_nt(qs[rows], st))
        o = _rms(jnp.concatenate(o, axis=0), go)
        outs.append(o * (rs[h] * _sigmoid(rs[h])))
        new_states.append(st)
    return outs, new_states


def _gla_fwd(z, zsm, wg, bg, go, cols, *, name):
    S = z.shape[0]
    rb = GLA_BLOCK_CHUNKS * CHUNK
    nb = S // rb
    cq, ckk, cv, cr = cols
    H = GLA_HEADS

    def body(q_ref, k_ref, zsm_ref, wg_ref, bg_ref, go_ref, *rest):
        v_refs, r_refs = rest[:H], rest[H:2 * H]
        o_ref, st_ref, state = rest[2 * H], rest[2 * H + 1], rest[2 * H + 2]

        @pl.when(pl.program_id(0) == 0)
        def _():
            state[...] = jnp.zeros(state.shape, F32)

        states = [state[h] for h in range(H)]
        for h in range(H):
            st_ref[0, h] = states[h]
        outs, new_states = _gla_block(
            q_ref[...].astype(F32), k_ref[...].astype(F32), zsm_ref[...], wg_ref[...], bg_ref[...], go_ref[...],
            [v_refs[h][...].astype(F32) for h in range(H)], [r_refs[h][...].astype(F32) for h in range(H)], states)
        for h in range(H):
            o_ref[:, h * GLA_DV:(h + 1) * GLA_DV] = outs[h].astype(BF16)
            state[h] = new_states[h]

    def col(width, off):
        return pl.BlockSpec((rb, width), lambda i, o=off // width: (i, o))

    full = lambda shp: pl.BlockSpec(shp, lambda i: (0,) * len(shp))
    in_specs = [col(GLA_W, cq), col(GLA_W, ckk), pl.BlockSpec((rb, 128), lambda i: (i, 0)),
                full((128, GLA_W)), full((1, GLA_W)), full((1, GLA_DV))]
    in_specs += [col(GLA_DV, cv + h * GLA_DV) for h in range(H)] + [col(GLA_DV, cr + h * GLA_DV) for h in range(H)]
    return pl.pallas_call(
        body, name=name,
        out_shape=(jax.ShapeDtypeStruct((S, H * GLA_DV), BF16), jax.ShapeDtypeStruct((nb, H, GLA_DV, GLA_W), F32)),
        grid=(nb,), in_specs=in_specs,
        out_specs=(pl.BlockSpec((rb, H * GLA_DV), lambda i: (i, 0)),
                   pl.BlockSpec((1, H, GLA_DV, GLA_W), lambda i: (i, 0, 0, 0))),
        scratch_shapes=[pltpu.VMEM((H, GLA_DV, GLA_W), F32)],
        compiler_params=_params(("arbitrary",)),
    )(z, z, zsm, wg, bg, go, *([z] * (2 * H)))


def _gla_bwd(z, zsm, wg, bg, go, states, do, cols, *, name):
    S = z.shape[0]
    rb = GLA_BLOCK_CHUNKS * CHUNK
    nb = S // rb
    cq, ckk, cv, cr = cols
    H = GLA_HEADS

    def body(q_ref, k_ref, zsm_ref, wg_ref, bg_ref, go_ref, st_ref, do_ref, *rest):
        v_refs, r_refs = rest[:H], rest[H:2 * H]
        dq_ref, dk_ref, dv_ref, dr_ref, dzsm_ref, dwg_ref, dbg_ref, dgo_ref, dstate = rest[2 * H:]

        @pl.when(pl.program_id(0) == 0)
        def _():
            dstate[...] = jnp.zeros(dstate.shape, F32)
            dwg_ref[...] = jnp.zeros(dwg_ref.shape, F32)
            dbg_ref[...] = jnp.zeros(dbg_ref.shape, F32)
            dgo_ref[...] = jnp.zeros(dgo_ref.shape, F32)

        prim = (q_ref[...].astype(F32), k_ref[...].astype(F32), zsm_ref[...], wg_ref[...], bg_ref[...], go_ref[...],
                [v_refs[h][...].astype(F32) for h in range(H)], [r_refs[h][...].astype(F32) for h in range(H)],
                [st_ref[0, h] for h in range(H)])
        _, vjp = jax.vjp(_gla_block, *prim)
        douts = [do_ref[:, h * GLA_DV:(h + 1) * GLA_DV].astype(F32) for h in range(H)]
        dq, dk, dzs, dwg, dbg, dgo, dvs, drs, dsts = vjp((douts, [dstate[h] for h in range(H)]))
        dq_ref[...] = dq.astype(BF16)
        dk_ref[...] = dk.astype(BF16)
        dzsm_ref[...] = dzs
        dwg_ref[...] += dwg
        dbg_ref[...] += dbg
        dgo_ref[...] += dgo
        for h in range(H):
            dv_ref[:, h * GLA_DV:(h + 1) * GLA_DV] = dvs[h].astype(BF16)
            dr_ref[:, h * GLA_DV:(h + 1) * GLA_DV] = drs[h].astype(BF16)
            dstate[h] = dsts[h]

    rev = lambda i: nb - 1 - i

    def col(width, off):
        return pl.BlockSpec((rb, width), lambda i, o=off // width: (rev(i), o))

    full = lambda shp: pl.BlockSpec(shp, lambda i: (0,) * len(shp))
    rowb = lambda w: pl.BlockSpec((rb, w), lambda i: (rev(i), 0))
    in_specs = [col(GLA_W, cq), col(GLA_W, ckk), rowb(128), full((128, GLA_W)), full((1, GLA_W)), full((1, GLA_DV)),
                pl.BlockSpec((1, H, GLA_DV, GLA_W), lambda i: (rev(i), 0, 0, 0)), rowb(H * GLA_DV)]
    in_specs += [col(GLA_DV, cv + h * GLA_DV) for h in range(H)] + [col(GLA_DV, cr + h * GLA_DV) for h in range(H)]
    return pl.pallas_call(
        body, name=name,
        out_shape=(jax.ShapeDtypeStruct((S, GLA_W), BF16), jax.ShapeDtypeStruct((S, GLA_W), BF16),
                   jax.ShapeDtypeStruct((S, H * GLA_DV), BF16), jax.ShapeDtypeStruct((S, H * GLA_DV), BF16),
                   jax.ShapeDtypeStruct((S, 128), F32), jax.ShapeDtypeStruct((128, GLA_W), F32),
                   jax.ShapeDtypeStruct((1, GLA_W), F32), jax.ShapeDtypeStruct((1, GLA_DV), F32)),
        grid=(nb,), in_specs=in_specs,
        out_specs=(rowb(GLA_W), rowb(GLA_W), rowb(H * GLA_DV), rowb(H * GLA_DV), rowb(128),
                   full((128, GLA_W)), full((1, GLA_W)), full((1, GLA_DV))),
        scratch_shapes=[pltpu.VMEM((H, GLA_DV, GLA_W), F32)],
        compiler_params=_params(("arbitrary",)),
    )(z, z, zsm, wg, bg, go, states, do, *([z] * (2 * H)))


def _row_spec(entry, tr):
    if isinstance(entry, tuple):
        arr, width, off = entry
        return arr, pl.BlockSpec((tr, width), lambda i, o=off // width: (i, o))
    return entry, pl.BlockSpec((tr, entry.shape[1]), lambda i: (i, 0))


def _stage_fwd(fn, rows, consts, outs, *, name, tr=None):
    first = rows[0][0] if isinstance(rows[0], tuple) else rows[0]
    S = first.shape[0]
    tr = tr or _pick(S, (512, 256, 128))
    arrs, specs = zip(*[_row_spec(e, tr) for e in rows])
    nr, nc = len(rows), len(consts)

    def body(*refs):
        vals = [r[...].astype(F32) for r in refs[:nr + nc]]
        res = fn(*vals)
        for o_ref, val in zip(refs[nr + nc:], res):
            o_ref[...] = val.astype(o_ref.dtype)

    cspecs = [pl.BlockSpec(c.shape, lambda i, n=c.ndim: (0,) * n) for c in consts]
    return pl.pallas_call(
        body, name=name,
        out_shape=tuple(jax.ShapeDtypeStruct((S, w), dt) for w, dt in outs), grid=(S // tr,),
        in_specs=list(specs) + cspecs,
        out_specs=tuple(pl.BlockSpec((tr, w), lambda i: (i, 0)) for w, _ in outs),
        compiler_params=_params(("parallel",)),
    )(*arrs, *consts)


def _stage_bwd(fn, rows, consts, cts, n_diff, drow_dtypes, *, name, tr=None):
    first = rows[0][0] if isinstance(rows[0], tuple) else rows[0]
    S = first.shape[0]
    tr = tr or _pick(S, (512, 256, 128))
    arrs, specs = zip(*[_row_spec(e, tr) for e in rows])
    widths = [e[1] if isinstance(e, tuple) else e.shape[1] for e in rows]
    nr, nc, nt = len(rows), len(consts), len(cts)

    def body(*refs):
        vals = [r[...].astype(F32) for r in refs[:nr + nc]]
        ct = [r[...].astype(F32) for r in refs[nr + nc:nr + nc + nt]]
        drow_refs = refs[nr + nc + nt:nr + nc + nt + n_diff]
        dconst_refs = refs[nr + nc + nt + n_diff:]
        rest_rows = vals[n_diff:nr]

        def f(diff_rows, cs):
            return tuple(fn(*diff_rows, *rest_rows, *cs))

        _, vjp = jax.vjp(f, vals[:n_diff], vals[nr:])
        drows, dcs = vjp(tuple(ct))
        for r, val in zip(drow_refs, drows):
            r[...] = val.astype(r.dtype)
        first_step = pl.program_id(0) == 0
        for r, val in zip(dconst_refs, dcs):
            @pl.when(first_step)
            def _(r=r, val=val):
                r[...] = val

            @pl.when(jnp.logical_not(first_step))
            def _(r=r, val=val):
                r[...] += val

    cspecs = [pl.BlockSpec(c.shape, lambda i, n=c.ndim: (0,) * n) for c in consts]
    ctspecs = [pl.BlockSpec((tr, c.shape[1]), lambda i: (i, 0)) for c in cts]
    out_shape = [jax.ShapeDtypeStruct((S, widths[j]), drow_dtypes[j]) for j in range(n_diff)]
    out_shape += [jax.ShapeDtypeStruct(c.shape, F32) for c in consts]
    out_specs = [pl.BlockSpec((tr, widths[j]), lambda i: (i, 0)) for j in range(n_diff)] + cspecs
    res = pl.pallas_call(
        body, name=name, out_shape=tuple(out_shape), grid=(S // tr,),
        in_specs=list(specs) + cspecs + ctspecs, out_specs=tuple(out_specs),
        compiler_params=_params(("arbitrary",)),
    )(*arrs, *consts, *cts)
    return list(res[:n_diff]), list(res[n_diff:])


def _mla_prep_fn(cq, ckv, kr, kr_sw, cos, sin, gq, gkv, wq_n, wq_r, wq_sw, wk, wv):
    hq = _rms(cq, gq)
    hkv = _rms(ckv, gkv)
    return (bdot(hq, wq_n), bdot(hq, wq_r) * cos + bdot(hq, wq_sw) * sin,
            bdot(hkv, wk), bdot(hkv, wv), kr * cos + kr_sw * sin)


def _merge_fn(g0, g1, g2, of, og, om, b0, b1, b2, wf, wg, wm):
    return (_sigmoid(g0 + b0) * bdot(of, wf) + _sigmoid(g1 + b1) * bdot(og, wg) + _sigmoid(g2 + b2) * bdot(om, wm),)


_IN_SIZES = (256, 256, 256, 4, 256, 256, 512, 16, 512, 256, 128, 32, 3072)
_IN_OFF = np.concatenate([[0], np.cumsum(_IN_SIZES)])
(_O_FQ, _O_FK, _O_FV, _O_FF, _O_GQ, _O_GK, _O_GV, _O_GLOW, _O_GR, _O_MQ, _O_MKV, _O_MKR, _O_ZG) = [int(o) for o in _IN_OFF[:-1]]
N_IN = int(_IN_OFF[-1])
_BIG_GROUPS = ((_O_ZG, 3072), (_O_GV, 512), (_O_GR, 512), (_O_FQ, 256), (_O_FK, 256), (_O_FV, 256),
               (_O_GQ, 256), (_O_GK, 256), (_O_MQ, 256), (_O_MKV, 128))
Z_GATE, Z_GV, Z_GR, Z_FQ, Z_FK, Z_FV, Z_GQ, Z_GK, Z_MQ, Z_MKV = [int(o) for o in
                                                                    np.concatenate([[0], np.cumsum([w for _, w in _BIG_GROUPS])])[:-1]]
N_BIG = sum(w for _, w in _BIG_GROUPS)
_HALF = MLA_ROPE // 2
_QK_HD = MLA_NOPE + MLA_ROPE
SM_FF, SM_GLOW, SM_KR, SM_KR_SW, N_SM = 0, 8, 128, 256, 384
N_PAD = N_BIG + N_SM
_IN_SEGS = ([(o, w, 1.0) for o, w in _BIG_GROUPS]
            + [(_O_FF, 4, 1.0), (None, SM_GLOW - 4, 0.0), (_O_GLOW, GLA_RANK, 1.0), (None, 128 - SM_GLOW - GLA_RANK, 0.0)]
            + [(_O_MKR, MLA_ROPE, 1.0)] * MLA_HEADS
            + [(_O_MKR + _HALF, _HALF, -1.0), (_O_MKR, _HALF, 1.0)] * MLA_HEADS)


def _cols(x, start, width):
    return lax.slice_in_dim(x, start, start + width, axis=x.ndim - 1)


def _pad_w_in(w):
    return jnp.concatenate([jnp.zeros(w.shape[:-1] + (n,), w.dtype) if src is None else
                            (_cols(w, src, n) if sign > 0 else -_cols(w, src, n)) for src, n, sign in _IN_SEGS], axis=-1)


def _unpad_w_in(g):
    groups = []
    for o, n in zip(_IN_OFF[:-1], _IN_SIZES):
        total, pos = None, 0
        for src, m, sign in _IN_SEGS:
            if src is not None and o <= src and src + m <= o + n:
                term = _cols(g, pos, m) if sign > 0 else -_cols(g, pos, m)
                if m != n:
                    term = jnp.pad(term, [(0, 0)] * (g.ndim - 1) + [(int(src - o), int(o + n - src - m))])
                total = term if total is None else total + term
            pos += m
        groups.append(total)
    return jnp.concatenate(groups, axis=-1)


def _take(x, idx):
    idx = np.asarray(idx)
    cuts = [0] + [i for i in range(1, len(idx)) if idx[i] != idx[i - 1] + 1] + [len(idx)]
    return jnp.concatenate([_cols(x, int(idx[a]), b - a) for a, b in zip(cuts[:-1], cuts[1:])], axis=1)


_UQ_NOPE = np.concatenate([np.arange(h * _QK_HD, h * _QK_HD + MLA_NOPE) for h in range(MLA_HEADS)])
_UQ_ROT = np.concatenate([np.arange(h * _QK_HD + MLA_NOPE, (h + 1) * _QK_HD) for h in range(MLA_HEADS)])
_UKV_PERM = np.concatenate(
    [np.concatenate([np.arange(h * 128, h * 128 + MLA_NOPE) for h in range(MLA_HEADS)]),
     np.concatenate([np.arange(h * 128 + MLA_NOPE, (h + 1) * 128) for h in range(MLA_HEADS)])])
_UKV_INV = np.argsort(_UKV_PERM)


def _rotary_partner(r):
    return jnp.concatenate([piece for h in range(MLA_HEADS) for piece in
                            (-_cols(r, h * MLA_ROPE + _HALF, _HALF), _cols(r, h * MLA_ROPE, _HALF))], axis=1)


def _uq_grad(dn, dr, dsw):
    dr = dr + jnp.concatenate([piece for h in range(MLA_HEADS) for piece in
                               (_cols(dsw, h * MLA_ROPE + _HALF, _HALF), -_cols(dsw, h * MLA_ROPE, _HALF))], axis=1)
    return jnp.concatenate([piece for h in range(MLA_HEADS) for piece in
                            (_cols(dn, h * MLA_NOPE, MLA_NOPE), _cols(dr, h * MLA_ROPE, MLA_ROPE))], axis=1)


def _rope_tables(S):
    inv = ROPE_BASE ** (-jnp.arange(_HALF, dtype=F32) / _HALF)
    ang = jnp.arange(S, dtype=F32)[:, None] * inv[None, :]
    return jnp.tile(jnp.cos(ang), (1, 2 * MLA_HEADS)), jnp.tile(jnp.sin(ang), (1, 2 * MLA_HEADS))


class _LayerParams:
    def __init__(self, rep, l):
        self.w, self.rep, self.l, self.made = {}, rep, l, {}

    def __getitem__(self, k):
        if k not in self.made:
            self.made[k] = self._make(k)
        return self.made[k]

    def _make(self, k):
        w, rep, l = self.w, self.rep, self.l
        if k == 'wg':
            return jnp.pad(w['w_gla_gate'], [(SM_GLOW, LANES - SM_GLOW - GLA_RANK), (0, 0)])
        if k in ('wq_n', 'wq_r'):
            return _take(w['w_mla_uq'], _UQ_NOPE if k == 'wq_n' else _UQ_ROT)
        if k == 'wq_sw':
            return _rotary_partner(self['wq_r'])
        if k in ('wk', 'wv'):
            return _take(w['w_mla_ukv'], _UKV_PERM[:256] if k == 'wk' else _UKV_PERM[256:])
        if k == 'b_f':
            return jnp.zeros((8, 1), F32).at[:FOX_HEADS, 0].set(rep['b_fox_forget'][l])
        if k == 'b_gate':
            return [rep['b_branch_gate'][l][i * 1024:(i + 1) * 1024].reshape(1, 1024) for i in range(3)]
        vec = {'bg': 'b_gla_gate', 'go': 'g_gla_out', 'gq': 'g_mla_q', 'gkv': 'g_mla_kv'}
        if k in vec:
            return rep[vec[k]][l].reshape(1, -1)
        return rep[k][l] if k in rep else w[k]


_GLA_COLS = (Z_GQ, Z_GK, Z_GV, Z_GR)
_MLA_OUTS = [(256, BF16), (128, BF16), (256, BF16), (256, BF16), (128, BF16)]


def _mla_rows(z, zsm, rope):
    return [(z, 256, Z_MQ), (z, 128, Z_MKV), (zsm, 128, SM_KR), (zsm, 128, SM_KR_SW), *rope]


def _mla_consts(p):
    return [p['gq'], p['gkv'], p['wq_n'], p['wq_r'], p['wq_sw'], p['wk'], p['wv']]


def _fox_qkv(z):
    return [((z, Z_FQ, 256), (z, Z_FK, 256), FOX_HD, False)], (z, Z_FV, 256)


def _mla_qkv(qn, qr, kn, vv, kr):
    return [((qn, 0, 256), (kn, 0, 256), MLA_NOPE, False), ((qr, 0, 128), (kr, 0, 128), MLA_ROPE, True)], (vv, 0, 256)


def _xa_qkv(qx, kvx):
    return [((qx, 0, 512), (kvx, 0, 512), XA_HD, False)], (kvx, 512, 512)


def _merge_rows(z, o_fox, o_gla, o_mla):
    return [(z, 1024, Z_GATE), (z, 1024, Z_GATE + 1024), (z, 1024, Z_GATE + 2048), o_fox, o_gla, o_mla]


def _merge_consts(p):
    return p['b_gate'] + [p['w_up_fox'], p['w_up_gla'], p['w_up_mla']]


def _carried(hooks, key, call):
    rider, sink = hooks.pop(key, (None, None))
    res = call(rider=rider)
    if rider is None:
        return res
    sink(res[-1])
    return res[:-1]


def _layer_fwd(x0, mem, p, rope, l, hooks):
    S = x0.shape[0]
    sv = {'x0': x0}
    h1 = _rms_fwd(x0, p['g_mix'], name=f"rms_mix_{l}")
    z = _mm(h1, p['w_in'], mode='nn', out_dtype=BF16, b_cols=(0, N_BIG), name=f"in_big_{l}")
    zsm = _mm(h1, p['w_in'], mode='nn', out_dtype=F32, b_cols=(N_BIG, N_SM), name=f"in_small_{l}")
    sv.update(h1=h1, z=z, zsm=zsm)
    ff_t = jnp.zeros((8, S), F32).at[:FOX_HEADS].set(zsm[:, SM_FF:SM_FF + FOX_HEADS].T)
    cum_t = _fox_cum_fwd(ff_t, p['b_f'], name=f"fox_cum_{l}")
    cum = cum_t.T
    o_fox, lse_f = _carried(hooks, (l, 'fox_fwd'), lambda rider: _attn_fwd(
        *_fox_qkv(z), FOX_HEADS, cum_t, cum, scale=FOX_HD ** -0.5, mask='causal', name=f"fox_fwd_{l}", rider=rider))
    sv.update(ff_t=ff_t, cum=cum, cum_t=cum_t, lse_f=lse_f, o_fox=o_fox)
    o_gla, states = _gla_fwd(z, zsm, p['wg'], p['bg'], p['go'], _GLA_COLS, name=f"gla_fwd_{l}")
    sv.update(o_gla=o_gla, states=states)
    mla = _stage_fwd(_mla_prep_fn, _mla_rows(z, zsm, rope), _mla_consts(p), _MLA_OUTS, name=f"mla_prep_{l}")
    o_mla, lse_m = _carried(hooks, (l, 'mla_fwd'), lambda rider: _attn_fwd(
        *_mla_qkv(*mla), MLA_HEADS, None, None, scale=_QK_HD ** -0.5, mask='chunk', name=f"mla_fwd_{l}", rider=rider))
    sv.update(mla=mla, lse_m=lse_m, o_mla=o_mla)
    (y,) = _stage_fwd(_merge_fn, _merge_rows(z, o_fox, o_gla, o_mla), _merge_consts(p), [(1024, BF16)], name=f"merge_{l}")
    x1 = _mm(y, p['w_out'], mode='nn', out_dtype=F32, residual=x0, name=f"out_proj_{l}")
    sv.update(y=y, x1=x1)
    h2 = _rms_fwd(x1, p['g_xa'], name=f"rms_xa_{l}")
    hm = _rms_fwd(mem, p['g_mem'], name=f"rms_mem_{l}")
    qx = _mm(h2, p['w_xq'], mode='nn', out_dtype=BF16, name=f"xq_{l}")
    kvx = _mm(hm, p['w_xkv'], mode='nn', out_dtype=BF16, name=f"xkv_{l}")
    ox, lse_x = _attn_fwd(*_xa_qkv(qx, kvx), XA_HEADS, None, None, scale=XA_HD ** -0.5, mask=None, name=f"xa_fwd_{l}")
    x2 = _mm(ox, p['w_xo'], mode='nn', out_dtype=F32, residual=x1, name=f"xo_{l}")
    sv.update(h2=h2, hm=hm, qx=qx, kvx=kvx, lse_x=lse_x, ox=ox, x2=x2)
    h3 = _rms_fwd(x2, p['g_mlp'], name=f"rms_mlp_{l}")
    a = _mm(h3, p['w_mlp1'], mode='nn', out_dtype=BF16, name=f"mlp1_{l}")
    x3 = _mm(a, p['w_mlp2'], mode='nn', out_dtype=F32, act='relu2', residual=x2, name=f"mlp2_{l}")
    sv.update(h3=h3, a=a)
    return x3, sv


def _layer_bwd(dx3, dx3b, mem, p, rope, sv, l, hooks, half_done):
    S = dx3.shape[0]
    g = {}
    da = _mm(dx3b, p['w_mlp2'], mode='nt', out_dtype=BF16, drelu_of=sv['a'], name=f"d_mlp2_in_{l}")
    g['w_mlp2'] = _mm(sv['a'], dx3b, mode='tn', out_dtype=BF16, act='relu2', name=f"d_w_mlp2_{l}")
    dx2, dx2b, g['g_mlp'] = _mm(da, p['w_mlp1'], mode='nt', out_dtype=F32, norm_bwd=(sv['x2'], p['g_mlp'], dx3), tm=512,
                                name=f"d_mlp1_in_{l}")
    g['w_mlp1'] = _mm(sv['h3'], da, mode='tn', out_dtype=BF16, name=f"d_w_mlp1_{l}")
    dox = _mm(dx2b, p['w_xo'], mode='nt', out_dtype=BF16, name=f"d_xo_in_{l}")
    g['w_xo'] = _mm(sv['ox'], dx2b, mode='tn', out_dtype=BF16, name=f"d_w_xo_{l}")
    (dqx,), (dkx,), dvx = _attn_bwd(*_xa_qkv(sv['qx'], sv['kvx']), XA_HEADS, sv['ox'], dox, sv['lse_x'], None, None,
                                    scale=XA_HD ** -0.5, mask=None, name=f"xa_bwd_{l}")
    dqx = dqx.astype(BF16)
    dkvx = jnp.concatenate([dkx, dvx], axis=1).astype(BF16)
    dx1, dx1b, g['g_xa'] = _mm(dqx, p['w_xq'], mode='nt', out_dtype=F32, norm_bwd=(sv['x1'], p['g_xa'], dx2), tm=512,
                               name=f"d_xq_in_{l}")
    g['w_xq'] = _mm(sv['h2'], dqx, mode='tn', out_dtype=BF16, name=f"d_w_xq_{l}")
    dhm = _mm(dkvx, p['w_xkv'], mode='nt', out_dtype=F32, name=f"d_xkv_in_{l}")
    g['w_xkv'] = _mm(sv['hm'], dkvx, mode='tn', out_dtype=BF16, name=f"d_w_xkv_{l}")
    _, _, g['g_mem'] = _rms_bwd(mem, p['g_mem'], dhm, None, name=f"d_rms_mem_{l}")
    dy = _mm(dx1b, p['w_out'], mode='nt', out_dtype=F32, name=f"d_out_in_{l}")
    g['w_out'] = _mm(sv['y'], dx1b, mode='tn', out_dtype=BF16, name=f"d_w_out_{l}")
    z, zsm = sv['z'], sv['zsm']
    (dg0, dg1, dg2, do_fox, do_gla, do_mla), (db0, db1, db2, g['w_up_fox'], g['w_up_gla'], g['w_up_mla']) = _stage_bwd(
        _merge_fn, _merge_rows(z, sv['o_fox'], sv['o_gla'], sv['o_mla']), _merge_consts(p), [dy], 6, [BF16] * 6,
        name=f"merge_bwd_{l}")
    g['b_branch_gate'] = jnp.concatenate([db0, db1, db2], axis=1).reshape(-1)
    half_done(l, g)
    (dfq,), (dfk,), dfv, dck, dcq = _carried(hooks, (l, 'fox_bwd'), lambda rider: _attn_bwd(
        *_fox_qkv(z), FOX_HEADS, sv['o_fox'], do_fox, sv['lse_f'], sv['cum_t'], sv['cum'],
        scale=FOX_HD ** -0.5, mask='causal', name=f"fox_bwd_{l}", rider=rider))
    dff_t, db_f = _fox_cum_bwd(sv['ff_t'], p['b_f'], dcq + dck.T, name=f"fox_cum_bwd_{l}")
    g['b_fox_forget'] = db_f[:FOX_HEADS, 0]
    dgq, dgk, dgv, dgr, dzsm, dwg, dbg, dgo = _gla_bwd(z, zsm, p['wg'], p['bg'], p['go'], sv['states'], do_gla, _GLA_COLS,
                                                       name=f"gla_bwd_{l}")
    g['w_gla_gate'] = dwg[SM_GLOW:SM_GLOW + GLA_RANK]
    g['b_gla_gate'] = dbg.reshape(-1)
    g['g_gla_out'] = dgo.reshape(-1)
    (dmqn, dmqr), (dmkn, dmkr), dmv = _carried(hooks, (l, 'mla_bwd'), lambda rider: _attn_bwd(
        *_mla_qkv(*sv['mla']), MLA_HEADS, sv['o_mla'], do_mla, sv['lse_m'], None, None,
        scale=_QK_HD ** -0.5, mask='chunk', name=f"mla_bwd_{l}", rider=rider))
    (dcq, dckv, dkr, dkr_sw), (dgq_n, dgkv_n, dwq_n, dwq_r, dwq_sw, dwk, dwv) = _stage_bwd(
        _mla_prep_fn, _mla_rows(z, zsm, rope), _mla_consts(p), [dmqn, dmqr, dmkn, dmv, dmkr], 4, [BF16] * 4,
        name=f"mla_prep_bwd_{l}")
    g['g_mla_q'] = dgq_n.reshape(-1)
    g['g_mla_kv'] = dgkv_n.reshape(-1)
    g['w_mla_uq'] = _uq_grad(dwq_n, dwq_r, dwq_sw)
    g['w_mla_ukv'] = _take(jnp.concatenate([dwk, dwv], axis=1), _UKV_INV)
    dsm = dzsm + jnp.pad(dff_t[:FOX_HEADS].T, [(0, 0), (0, 128 - FOX_HEADS)])
    dz = jnp.concatenate([dg0, dg1, dg2, dgv, dgr, dfq.astype(BF16), dfk.astype(BF16), dfv.astype(BF16), dgq, dgk, dcq, dckv,
                          dsm.astype(BF16), dkr.astype(BF16), dkr_sw.astype(BF16)], axis=1)
    dx0, dx0b, g['g_mix'] = _mm(dz, p['w_in'], mode='nt', out_dtype=F32, norm_bwd=(sv['x0'], p['g_mix'], dx1), tm=512,
                                tk=N_PAD // 2, name=f"d_in_{l}")
    g['w_in'] = _mm(sv['h1'], dz, mode='tn', out_dtype=BF16, tn=N_PAD // 3, name=f"d_w_in_{l}")
    for n in ('g_mlp', 'g_mem', 'g_xa', 'g_mix'):
        g[n] = g[n].reshape(-1)
    return dx0, dx0b, g


def _local_step(x, mem, target, ps, g_final, hooks, half_done, layer_done):
    rope = _rope_tables(x.shape[0])
    saved = []
    for l, p in enumerate(ps):
        x, sv = _layer_fwd(x, mem, p, rope, l, hooks)
        saved.append(sv)
    loss, dx, dxb, dgf = _loss_head(x, g_final, target, name="loss_head")
    for l in reversed(range(len(ps))):
        dx, dxb, grads = _layer_bwd(dx, dxb, mem, ps[l], rope, saved[l], l, hooks, half_done)
        layer_done(l, grads)
    assert not hooks, f"exchanges without a carrier: {list(hooks)}"
    return loss, dx, dgf.reshape(-1)


_MESH_AXES = ("x", "y", "c")
_HBM = pl.BlockSpec(memory_space=pl.ANY)


N_CHIP = 4


def _place():
    x, y, c = (lax.axis_index(n) for n in _MESH_AXES)
    return (x, y, c), (x, y, 1 - c), [(1 - x, y), (x, 1 - y), (1 - x, 1 - y)]


def _remote(src, dst, sems, k, to):
    return pltpu.make_async_remote_copy(src_ref=src, dst_ref=dst, send_sem=sems[0].at[k], recv_sem=sems[1].at[k],
                                        device_id=to, device_id_type=pl.DeviceIdType.MESH)


def _all_gather(x, *, name):
    def body(x_ref, o_ref, send_sems, recv_sems, local_sem):
        me, sib, chips = _place()
        c = me[2]
        sems = (send_sems, recv_sems)
        slot = lambda px, py, pc: o_ref.at[4 * px + 2 * py + pc]
        mine = pltpu.make_async_copy(x_ref, slot(*me), local_sem)
        mine.start()
        first = [_remote(x_ref, slot(*me), sems, 0, sib)]
        first += [_remote(x_ref, slot(*me), sems, 1 + j, (*chip, c)) for j, chip in enumerate(chips)]
        for cp in first:
            cp.start()
        passed = [_remote(slot(*chip, c), slot(*chip, c), sems, 4 + j, sib) for j, chip in enumerate(chips)]
        for j, chip in enumerate(chips):
            _remote(x_ref, slot(*chip, c), sems, 1 + j, me).wait_recv()
            passed[j].start()
        _remote(x_ref, slot(*sib), sems, 0, me).wait_recv()
        for j, chip in enumerate(chips):
            _remote(x_ref, slot(*chip, 1 - c), sems, 4 + j, me).wait_recv()
        for cp in first + passed:
            cp.wait_send()
        mine.wait()

    return pl.pallas_call(
        body, name=name, out_shape=jax.ShapeDtypeStruct((N_DEV,) + x.shape, x.dtype),
        in_specs=[_HBM], out_specs=_HBM,
        scratch_shapes=[pltpu.SemaphoreType.DMA((N_DEV - 1,)), pltpu.SemaphoreType.DMA((N_DEV - 1,)), pltpu.SemaphoreType.DMA],
        compiler_params=pltpu.CompilerParams(has_side_effects=True),
    )(x)


class _Rider:
    def __init__(self, inputs, out_shapes, scratch, start, finish, post):
        self.inputs, self.out_shapes, self.scratch = list(inputs), list(out_shapes), list(scratch)
        self.start, self.finish, self.post = start, finish, post


def _run_rider(rider, *, name):
    def body(*refs):
        rider.start(refs)
        rider.finish(refs)

    outs = pl.pallas_call(
        body, name=name, out_shape=tuple(rider.out_shapes), in_specs=[_HBM] * len(rider.inputs),
        out_specs=(_HBM,) * len(rider.out_shapes), scratch_shapes=rider.scratch,
        compiler_params=pltpu.CompilerParams(has_side_effects=True),
    )(*rider.inputs)
    return rider.post(outs)


def _carry(rider, n_in, n_out, first, last):
    if rider is None:
        return [], [], [], [], [], lambda refs: refs
    ni, no = len(rider.inputs), len(rider.out_shapes)

    def split(refs):
        own_in, r_in = refs[:n_in], refs[n_in:n_in + ni]
        own_out, r_out = refs[n_in + ni:n_in + ni + n_out], refs[n_in + ni + n_out:n_in + ni + n_out + no]
        rest = refs[n_in + ni + n_out + no:]
        own_scr, r_scr = rest[:len(rest) - len(rider.scratch)], rest[len(rest) - len(rider.scratch):]
        rrefs = tuple(r_in) + tuple(r_out) + tuple(r_scr)
        pl.when(first())(lambda: rider.start(rrefs))
        pl.when(last())(lambda: rider.finish(rrefs))
        return tuple(own_in) + tuple(own_out) + tuple(own_scr)

    return list(rider.inputs), [_HBM] * ni, list(rider.out_shapes), [_HBM] * no, list(rider.scratch), split


def _gather_rider(shards, axes):
    n = len(shards)
    srcs, out_shapes, kinds = [], [], []
    for s, ax in zip(shards, axes):
        L, a, b = s.shape
        if ax == 1:
            srcs.append(s.reshape(L, 1, a, b)), out_shapes.append((L, N_DEV, a, b)), kinds.append('row')
        elif b % 128 == 0:
            srcs.append(s), out_shapes.append((L, a, N_DEV * b)), kinds.append('col')
        else:
            srcs.append(s.reshape(1, L, a, b)), out_shapes.append((N_DEV, L, a, b)), kinds.append('slot')

    def parts(refs):
        x_refs, o_refs = refs[:n], refs[n:2 * n]
        send_sems, recv_sems, local_sem = refs[2 * n:]
        me, sib, chips = _place()
        sems = (send_sems, recv_sems)

        def win(t, px, py, pc):
            idx = 4 * px + 2 * py + pc
            if kinds[t] == 'row':
                return o_refs[t].at[:, pl.ds(idx, 1)]
            if kinds[t] == 'col':
                width = shards[t].shape[2]
                return o_refs[t].at[:, :, pl.ds(pl.multiple_of(idx * width, 128), width)]
            return o_refs[t].at[pl.ds(idx, 1)]

        def group(k, block, to, own):
            return [_remote(x_refs[t] if own else win(t, *block), win(t, *block), sems, k * n + t, to) for t in range(n)]

        mine = [pltpu.make_async_copy(x_refs[t], win(t, *me), local_sem.at[t]) for t in range(n)]
        first = group(0, me, sib, True)
        for j, chip in enumerate(chips):
            first += group(1 + j, me, (*chip, me[2]), True)
        return me, sib, chips, group, mine, first

    def start(refs):
        *_, mine, first = parts(refs)
        for cp in mine + first:
            cp.start()

    def finish(refs):
        me, sib, chips, group, mine, first = parts(refs)
        c = me[2]
        passed = []
        for j, chip in enumerate(chips):
            for cp in group(1 + j, (*chip, c), me, False):
                cp.wait_recv()
            fwd = group(4 + j, (*chip, c), sib, False)
            for cp in fwd:
                cp.start()
            passed += fwd
        for cp in group(0, sib, me, False):
            cp.wait_recv()
        for j, chip in enumerate(chips):
            for cp in group(4 + j, (*chip, 1 - c), me, False):
                cp.wait_recv()
        for cp in first + passed:
            cp.wait_send()
        for cp in mine:
            cp.wait()

    def post(outs):
        whole = []
        for o, s, kind in zip(outs, shards, kinds):
            L, a, b = s.shape
            whole.append(o.reshape(L, N_DEV * a, b) if kind == 'row' else o if kind == 'col' else _to_whole(o, 2))
        return whole

    return _Rider(srcs, [jax.ShapeDtypeStruct(shp, s.dtype) for shp, s in zip(out_shapes, shards)],
                  [pltpu.SemaphoreType.DMA(((N_DEV - 1) * n,)), pltpu.SemaphoreType.DMA(((N_DEV - 1) * n,)),
                   pltpu.SemaphoreType.DMA((n,))], start, finish, post)


def _sibling_swap(x, *, name):
    def body(x_ref, o_ref, send_sems, recv_sems):
        me, sib, _ = _place()
        c = me[2]
        sems = (send_sems, recv_sems)
        sends = [_remote(x_ref.at[j, 1 - c], o_ref.at[j], sems, j, sib) for j in range(N_CHIP)]
        for cp in sends:
            cp.start()
        for cp in sends:
            cp.wait_send()
            cp.wait_recv()

    return pl.pallas_call(
        body, name=name, out_shape=jax.ShapeDtypeStruct((N_CHIP,) + x.shape[2:], x.dtype),
        in_specs=[_HBM], out_specs=_HBM,
        scratch_shapes=[pltpu.SemaphoreType.DMA((N_CHIP,)), pltpu.SemaphoreType.DMA((N_CHIP,))],
        compiler_params=pltpu.CompilerParams(has_side_effects=True),
    )(x)


def _pair_sum(x, got, c, *, name):
    _, _, R, _ = x.shape
    tr = _pick(R, (1024, 512, 256, 128, 64, 32, 16, 8))

    def body(c_ref, x_ref, g_ref, o_ref):
        o_ref[...] = (x_ref[...].astype(F32) + g_ref[...].astype(F32)).astype(o_ref.dtype)

    return pl.pallas_call(
        body, name=name, out_shape=jax.ShapeDtypeStruct((N_CHIP, R, 128), x.dtype),
        grid_spec=pltpu.PrefetchScalarGridSpec(
            num_scalar_prefetch=1, grid=(N_CHIP, R // tr),
            in_specs=[pl.BlockSpec((None, None, tr, 128), lambda j, i, c_ref: (j, c_ref[0], i, 0)),
                      pl.BlockSpec((None, tr, 128), lambda j, i, c_ref: (j, i, 0))],
            out_specs=pl.BlockSpec((None, tr, 128), lambda j, i, c_ref: (j, i, 0))),
        compiler_params=_params(("parallel", "parallel")),
    )(c, x, got)


def _chip_all_to_all_rider(x):
    def parts(refs):
        x_ref, o_ref, send_sems, recv_sems, local_sem = refs
        me, _, chips = _place()
        sems = (send_sems, recv_sems)
        mine = 2 * me[0] + me[1]
        local = pltpu.make_async_copy(x_ref.at[mine], o_ref.at[mine], local_sem)
        sends = [_remote(x_ref.at[2 * px + py], o_ref.at[mine], sems, j, (px, py, me[2])) for j, (px, py) in enumerate(chips)]
        arrival = lambda j: _remote(x_ref.at[mine], o_ref.at[2 * chips[j][0] + chips[j][1]], sems, j, me)
        return local, sends, arrival

    def start(refs):
        local, sends, _ = parts(refs)
        for cp in [local] + sends:
            cp.start()

    def finish(refs):
        local, sends, arrival = parts(refs)
        for j, cp in enumerate(sends):
            cp.wait_send()
            arrival(j).wait_recv()
        local.wait()

    return _Rider([x], [jax.ShapeDtypeStruct(x.shape, x.dtype)],
                  [pltpu.SemaphoreType.DMA((N_CHIP - 1,)), pltpu.SemaphoreType.DMA((N_CHIP - 1,)), pltpu.SemaphoreType.DMA],
                  start, finish, lambda outs: outs[0])


def _sum_slots(x, *, name):
    n, R, _ = x.shape
    tr = _pick(R, (1024, 512, 256, 128, 64, 32, 16, 8))

    def body(x_ref, o_ref):
        acc = x_ref[0].astype(F32)
        for j in range(1, n):
            acc = acc + x_ref[j].astype(F32)
        o_ref[...] = acc

    return pl.pallas_call(
        body, name=name, out_shape=jax.ShapeDtypeStruct((R, 128), F32), grid=(R // tr,),
        in_specs=[pl.BlockSpec((n, tr, 128), lambda i: (0, i, 0))], out_specs=pl.BlockSpec((tr, 128), lambda i: (i, 0)),
        compiler_params=_params(("parallel",)),
    )(x)


def _adamw(w, g, m, v, *, name):
    shape = w.shape
    cols = shape[-1]
    rows = int(np.prod(shape[:-1]))
    tr = next((t for t in (1024, 512, 256, 128, 64, 32, 16, 8) if rows % t == 0 and t * cols * 4 <= (1 << 20)), rows)

    def body(w_ref, g_ref, m_ref, v_ref, d_ref, mo_ref, vo_ref):
        g_ = g_ref[...]
        m_ = ADAM_B1 * m_ref[...] + (1.0 - ADAM_B1) * g_
        v_ = ADAM_B2 * v_ref[...] + (1.0 - ADAM_B2) * jnp.square(g_)
        m_hat = m_ / (1.0 - ADAM_B1 ** ADAM_STEP)
        v_hat = v_ / (1.0 - ADAM_B2 ** ADAM_STEP)
        d_ref[...] = -ADAM_LR * (m_hat / (jnp.sqrt(v_hat) + ADAM_EPS) + ADAM_WD * w_ref[...])
        mo_ref[...] = m_
        vo_ref[...] = v_

    blk = pl.BlockSpec((tr, cols), lambda i: (i, 0))
    outs = pl.pallas_call(
        body, name=name, out_shape=tuple(jax.ShapeDtypeStruct((rows, cols), F32) for _ in range(3)), grid=(rows // tr,),
        in_specs=[blk] * 4, out_specs=(blk,) * 3, compiler_params=_params(("parallel",)),
    )(*(a.reshape(rows, cols) for a in (w, g, m, v)))
    return tuple(o.reshape(shape) for o in outs)


_WEIGHTS = ('g_mix', 'w_in', 'b_fox_forget', 'w_gla_gate', 'b_gla_gate', 'g_gla_out', 'g_mla_q', 'w_mla_uq', 'g_mla_kv',
            'w_mla_ukv', 'b_branch_gate', 'w_up_fox', 'w_up_gla', 'w_up_mla', 'w_out', 'g_xa', 'g_mem', 'w_xq', 'w_xkv',
            'w_xo', 'g_mlp', 'w_mlp1', 'w_mlp2', 'g_final')
_SHARDED = (('w_in', 1), ('w_gla_gate', 2), ('w_mla_uq', 2), ('w_mla_ukv', 2), ('w_up_fox', 2), ('w_up_gla', 2),
            ('w_up_mla', 2), ('w_out', 1), ('w_xq', 1), ('w_xkv', 1), ('w_xo', 2), ('w_mlp1', 2), ('w_mlp2', 1))
_REPLICATED = tuple(n for n in _WEIGHTS if n not in dict(_SHARDED))
_ROW_PAD = 1024
_SMALL_ROW_PAD = 8
_PIECE_ROWS = 16


def _pack(flats, lead, row_pad=_ROW_PAD):
    if all(int(np.prod(a.shape[lead:])) % 128 == 0 for a in flats):
        def block(a):
            a = a.reshape(a.shape[:lead] + (-1, 128))
            return jnp.pad(a, [(0, 0)] * lead + [(0, -a.shape[lead] % _PIECE_ROWS), (0, 0)])
        cat = jnp.concatenate([block(a) for a in flats], axis=lead)
        rows = cat.shape[lead]
        return jnp.pad(cat, [(0, 0)] * lead + [(0, -(-rows // row_pad) * row_pad - rows), (0, 0)])
    cat = jnp.concatenate([a.reshape(a.shape[:lead] + (-1,)) for a in flats], axis=-1)
    n = cat.shape[-1]
    total = -(-n // (128 * row_pad)) * (128 * row_pad)
    cat = jnp.pad(cat, [(0, 0)] * lead + [(0, total - n)])
    return cat.reshape(cat.shape[:lead] + (total // 128, 128))


def _unpack(buf, shapes, lead):
    sizes = [int(np.prod(shp)) for shp in shapes]
    out, off = [], 0
    if all(n % 128 == 0 for n in sizes):
        for shp, n in zip(shapes, sizes):
            rows = buf[(slice(None),) * lead + (slice(off, off + n // 128),)]
            out.append(rows.reshape(buf.shape[:lead] + tuple(shp)))
            off += -(-(n // 128) // _PIECE_ROWS) * _PIECE_ROWS
        return out
    flat = buf.reshape(buf.shape[:lead] + (-1,))
    for shp, n in zip(shapes, sizes):
        out.append(flat[..., off:off + n].reshape(buf.shape[:lead] + tuple(shp)))
        off += n
    return out


def _to_whole(g, axis):
    if axis == 1:
        return g.transpose(1, 0, 2, 3).reshape(g.shape[1], N_DEV * g.shape[2], g.shape[3])
    return g.transpose(1, 2, 0, 3).reshape(g.shape[1], g.shape[2], N_DEV * g.shape[3])


def _to_shards(w, axis):
    L, R, C = w.shape
    if axis == 1:
        return w.reshape(L, N_DEV, R // N_DEV, C).transpose(1, 0, 2, 3)
    return w.reshape(L, R, N_DEV, C // N_DEV).transpose(2, 0, 1, 3)


def kernel(x, mem, g_mix, w_in, b_fox_forget, w_gla_gate, b_gla_gate, g_gla_out, g_mla_q, w_mla_uq, g_mla_kv, w_mla_ukv, b_branch_gate, w_up_fox, w_up_gla, w_up_mla, w_out, g_xa, g_mem, w_xq, w_xkv, w_xo, g_mlp, w_mlp1, w_mlp2, g_final, loss_target, m_g_mix, m_w_in, m_b_fox_forget, m_w_gla_gate, m_b_gla_gate, m_g_gla_out, m_g_mla_q, m_w_mla_uq, m_g_mla_kv, m_w_mla_ukv, m_b_branch_gate, m_w_up_fox, m_w_up_gla, m_w_up_mla, m_w_out, m_g_xa, m_g_mem, m_w_xq, m_w_xkv, m_w_xo, m_g_mlp, m_w_mlp1, m_w_mlp2, m_g_final, v_g_mix, v_w_in, v_b_fox_forget, v_w_gla_gate, v_b_gla_gate, v_g_gla_out, v_g_mla_q, v_w_mla_uq, v_g_mla_kv, v_w_mla_ukv, v_b_branch_gate, v_w_up_fox, v_w_up_gla, v_w_up_mla, v_w_out, v_g_xa, v_g_mem, v_w_xq, v_w_xkv, v_w_xo, v_g_mlp, v_w_mlp1, v_w_mlp2, v_g_final):
    wts = dict(zip(_WEIGHTS, (g_mix, w_in, b_fox_forget, w_gla_gate, b_gla_gate, g_gla_out, g_mla_q, w_mla_uq, g_mla_kv,
                              w_mla_ukv, b_branch_gate, w_up_fox, w_up_gla, w_up_mla, w_out, g_xa, g_mem, w_xq, w_xkv, w_xo,
                              g_mlp, w_mlp1, w_mlp2, g_final)))
    mom1 = dict(zip(_WEIGHTS, (m_g_mix, m_w_in, m_b_fox_forget, m_w_gla_gate, m_b_gla_gate, m_g_gla_out, m_g_mla_q,
                               m_w_mla_uq, m_g_mla_kv, m_w_mla_ukv, m_b_branch_gate, m_w_up_fox, m_w_up_gla, m_w_up_mla,
                               m_w_out, m_g_xa, m_g_mem, m_w_xq, m_w_xkv, m_w_xo, m_g_mlp, m_w_mlp1, m_w_mlp2, m_g_final)))
    mom2 = dict(zip(_WEIGHTS, (v_g_mix, v_w_in, v_b_fox_forget, v_w_gla_gate, v_b_gla_gate, v_g_gla_out, v_g_mla_q,
                               v_w_mla_uq, v_g_mla_kv, v_w_mla_ukv, v_b_branch_gate, v_w_up_fox, v_w_up_gla, v_w_up_mla,
                               v_w_out, v_g_xa, v_g_mem, v_w_xq, v_w_xkv, v_w_xo, v_g_mlp, v_w_mlp1, v_w_mlp2, v_g_final)))
    depth = g_mix.shape[0]

    names = [n for n, _ in _SHARDED]
    axes = dict(_SHARDED)
    shard = {n: wts[n] for n in names}
    shard['w_in'] = _pad_w_in(w_in)
    rep = {n: wts[n] for n in _REPLICATED}
    ps = [_LayerParams(rep, l) for l in range(depth)]

    def gather(group, l):
        rider = _gather_rider([shard[n][l:l + 1].astype(BF16) for n in group], [axes[n] for n in group])
        return rider, lambda whole: ps[l].w.update({n: w[0] for n, w in zip(group, whole)})

    first, sink = gather(['w_in'], 0)
    sink(_run_rider(first, name="gather_w_in_0"))
    hooks = {(0, 'fox_fwd'): gather([n for n in names if n != 'w_in'], 0)}
    for l in range(1, depth):
        hooks[(l - 1, 'mla_fwd')] = gather(names, l)

    core = lax.axis_index("c").astype(jnp.int32).reshape(1)
    late = ['w_in', 'w_gla_gate', 'w_mla_uq', 'w_mla_ukv']
    groups = {'early': [n for n in names if n not in late], 'late': late}
    small_grads, landed = {}, {}

    def exchange(l, g, which):
        slots = _pack([_to_shards(g[n][None], axes[n]).astype(BF16) for n in groups[which]], 1)
        slots = slots.reshape((N_CHIP, 2) + slots.shape[1:])
        paired = _pair_sum(slots, _sibling_swap(slots, name=f"swap_grads_{which}_{l}"), core, name=f"pair_grads_{which}_{l}")
        return _chip_all_to_all_rider(paired), lambda got: landed.update({(l, which): got})

    def half_done(l, g):
        hooks[(l, 'mla_bwd')] = exchange(l, g, 'early')

    def layer_done(l, g):
        small_grads[l] = g
        rider, sink = exchange(l, g, 'late')
        if l > 0:
            hooks[(l - 1, 'fox_bwd')] = (rider, sink)
        else:
            sink(_run_rider(rider, name=f"scatter_grads_late_{l}"))

    loss, dx, dg_final = _local_step(x[0], mem[0], loss_target[0], ps, g_final, hooks, half_done, layer_done)
    loss = lax.psum(loss[0, 0], _MESH_AXES)

    grad = {}
    for which, group in groups.items():
        shapes = [(1,) + shard[n].shape[1:] for n in group]
        per_layer = [_unpack(_sum_slots(landed[(l, which)], name=f"sum_grads_{which}_{l}"), shapes, 0) for l in range(depth)]
        grad.update({n: jnp.concatenate([per_layer[l][i] for l in range(depth)], axis=0) for i, n in enumerate(group)})
    grad['w_in'] = _unpad_w_in(grad['w_in'])
    grads = small_grads
    small = [dg_final if n == 'g_final' else jnp.stack([grads[l][n] for l in range(depth)]) for n in _REPLICATED]
    small_shapes = [wts[n].shape for n in _REPLICATED]
    small_sum = _sum_slots(_all_gather(_pack(small, 0, _SMALL_ROW_PAD), name="gather_small_grads"), name="sum_small_grads")
    grad.update(dict(zip(_REPLICATED, _unpack(small_sum, small_shapes, 0))))

    delta, new_m, new_v = {}, {}, {}
    for n, _ in _SHARDED:
        delta[n], new_m[n], new_v[n] = _adamw(wts[n], grad[n], mom1[n], mom2[n], name=f"adamw_{n}")
    packed = [_pack([d[n] for n in _REPLICATED], 0, _SMALL_ROW_PAD) for d in (wts, mom1, mom2)]
    outs = _adamw(packed[0], small_sum, packed[1], packed[2], name="adamw_small")
    for d, o in zip((delta, new_m, new_v), outs):
        d.update(dict(zip(_REPLICATED, _unpack(o, small_shapes, 0))))

    return (loss, dx[None], *[grad[n] for n in _WEIGHTS], *[delta[n] for n in _WEIGHTS],
            *[new_m[n] for n in _WEIGHTS], *[new_v[n] for n in _WEIGHTS])
```

```python
import jax
import jax.numpy as jnp
import numpy as np
from jax import lax
from jax.experimental import pallas as pl
from jax.experimental.pallas import tpu as pltpu

F32 = jnp.float32
BF16 = jnp.bfloat16

EPS = 1e-6
CHUNK = 64
FOX_HEADS, FOX_HD = 4, 64
GLA_HEADS, GLA_DK, GLA_DV, GLA_RANK, GLA_TAU = 4, 64, 128, 16, 16.0
MLA_HEADS, MLA_Q_RANK, MLA_KV_RANK, MLA_NOPE, MLA_ROPE, MLA_VD = 4, 256, 128, 64, 32, 64
ROPE_BASE = 10000.0
XA_HEADS, XA_HD = 4, 128
ADAM_LR, ADAM_B1, ADAM_B2, ADAM_EPS, ADAM_WD, ADAM_STEP = 0.001, 0.9, 0.999, 1e-08, 0.01, 10

N_DEV = 8
V7X_VMEM_LIMIT = 56 * 1024 * 1024
NEG = -1e30

NN = ((1,), (0,))
NT = ((1,), (1,))
TN = ((0,), (0,))


def _dot(a, b, dims):
    return lax.dot_general(a.astype(BF16), b.astype(BF16), (dims, ((), ())), preferred_element_type=F32)


@jax.custom_vjp
def bdot(a, b):
    return _dot(a, b, NN)


bdot.defvjp(lambda a, b: (_dot(a, b, NN), (a, b)),
            lambda res, g: (_dot(g, res[1], NT), _dot(res[0], g, TN)))


@jax.custom_vjp
def bdot_nt(a, b):
    return _dot(a, b, NT)


bdot_nt.defvjp(lambda a, b: (_dot(a, b, NT), (a, b)),
               lambda res, g: (_dot(g, res[1], NN), _dot(g, res[0], TN)))


@jax.custom_vjp
def bdot_tn(a, b):
    return _dot(a, b, TN)


bdot_tn.defvjp(lambda a, b: (_dot(a, b, TN), (a, b)),
               lambda res, g: (_dot(res[1], g, NT), _dot(res[0], g, NN)))


def _split2(x):
    hi = x.astype(BF16)
    lo = (x - hi.astype(F32)).astype(BF16)
    return hi, lo


def _tri(n, lower):
    r = lax.broadcasted_iota(jnp.int32, (n, n), 0)
    c = lax.broadcasted_iota(jnp.int32, (n, n), 1)
    return jnp.where((r >= c) if lower else (r <= c), 1.0, 0.0).astype(BF16)


def _log_sigmoid(x):
    return jnp.minimum(x, 0.0) - jnp.log(1.0 + jnp.exp(-jnp.abs(x)))


def _sigmoid(x):
    return 1.0 / (1.0 + jnp.exp(-x))


def _rms(x, g):
    return x * lax.rsqrt(jnp.mean(x * x, axis=-1, keepdims=True) + EPS) * g


def _pick(dim, prefs):
    for p in prefs:
        if dim % p == 0:
            return p
    return dim


def _params(sem):
    return pltpu.CompilerParams(dimension_semantics=sem, vmem_limit_bytes=V7X_VMEM_LIMIT)


def _rms_vjp(x, g, dy, dres):
    rstd = lax.rsqrt(jnp.mean(x * x, axis=-1, keepdims=True) + EPS)
    xh = x * rstd
    gdy = dy * g
    dx = (gdy - xh * jnp.mean(gdy * xh, axis=-1, keepdims=True)) * rstd
    return (dx if dres is None else dx + dres), jnp.sum(dy * xh, axis=0, keepdims=True)


def _mm(a, b, *, mode, out_dtype, name, act=None, residual=None, drelu_of=None, norm_bwd=None, b_cols=None,
        col_shards=None, tm=None, tn=None, tk=None):
    b_off, b_width = b_cols or (0, b.shape[1])
    if mode == 'nn':
        (M, K), N = a.shape, b_width
    elif mode == 'nt':
        (M, K), N = a.shape, b.shape[0]
    else:
        (K, M), N = a.shape, b_width
    tm = tm or _pick(M, (1024, 512, 256, 128))
    tn = tn or _pick(N, (1024, 1920, 1152, 768, 640, 512, 384, 256, 128))
    tk = tk or _pick(K, (1024, 1920, 1152, 640, 512, 256, 128))
    nk = K // tk
    dims = {'nn': NN, 'nt': NT, 'tn': TN}[mode]
    a_spec = pl.BlockSpec((tk, tm), lambda i, j, k: (k, i)) if mode == 'tn' else pl.BlockSpec((tm, tk), lambda i, j, k: (i, k))
    if mode == 'nt':
        b_spec = pl.BlockSpec((tn, tk), lambda i, j, k, o=b_off // tk: (j, k + o))
    else:
        b_spec = pl.BlockSpec((tk, tn), lambda i, j, k, o=b_off // tn: (k, j + o))
    o_spec = pl.BlockSpec((tm, tn), lambda i, j, k: (i, j))
    extra = [e for e in (residual, drelu_of) if e is not None]
    extra_specs = [o_spec] * len(extra)
    out_shape, out_specs, n_out = jax.ShapeDtypeStruct((M, N), out_dtype), o_spec, 1
    if col_shards:
        n_sh = N // col_shards
        assert tn % n_sh == 0 and not extra and norm_bwd is None
        out_shape = jax.ShapeDtypeStruct((col_shards, M, n_sh), out_dtype)
        out_specs = pl.BlockSpec((tn // n_sh, tm, n_sh), lambda i, j, k: (j, i, 0))
    if norm_bwd is not None:
        x_in, g_in, dres_in = norm_bwd
        assert tn == N and residual is None and drelu_of is None
        vec = pl.BlockSpec((1, N), lambda i, j, k: (0, 0))
        extra, extra_specs = [x_in, g_in.reshape(1, N), dres_in], [o_spec, vec, o_spec]
        out_shape = (jax.ShapeDtypeStruct((M, N), F32), jax.ShapeDtypeStruct((M, N), BF16), jax.ShapeDtypeStruct((1, N), F32))
        out_specs, n_out = (o_spec, o_spec, vec), 3

    def body(a_ref, b_ref, *rest):
        o_ref = rest[len(extra)]
        first_rows = pl.program_id(0) == 0
        at = a_ref[...]
        if act == 'relu2':
            at = jnp.square(jnp.maximum(at.astype(F32), 0.0))
        part = _dot(at, b_ref[...], dims)

        def finish(acc):
            if norm_bwd is not None:
                dx, dg = _rms_vjp(rest[0][...], rest[1][...], acc, rest[2][...])
                o_ref[...] = dx
                rest[len(extra) + 1][...] = dx.astype(BF16)
                dg_ref = rest[len(extra) + 2]

                @pl.when(first_rows)
                def _():
                    dg_ref[...] = dg

                @pl.when(jnp.logical_not(first_rows))
                def _():
                    dg_ref[...] += dg
                return
            idx = 0
            if residual is not None:
                acc = acc + rest[idx][...]
                idx += 1
            if drelu_of is not None:
                acc = acc * (2.0 * jnp.maximum(rest[idx][...].astype(F32), 0.0))
            if col_shards:
                for t in range(tn // n_sh):
                    o_ref[t] = acc[:, t * n_sh:(t + 1) * n_sh].astype(out_dtype)
            else:
                o_ref[...] = acc.astype(out_dtype)

        if nk == 1:
            finish(part)
        else:
            acc_ref = rest[len(extra) + n_out]
            k = pl.program_id(2)

            @pl.when(k == 0)
            def _():
                acc_ref[...] = part

            @pl.when(k > 0)
            def _():
                acc_ref[...] += part

            @pl.when(k == nk - 1)
            def _():
                finish(acc_ref[...])

    return pl.pallas_call(
        body, name=name,
        out_shape=out_shape,
        grid=(M // tm, N // tn, nk),
        in_specs=[a_spec, b_spec] + extra_specs,
        out_specs=out_specs,
        scratch_shapes=[] if nk == 1 else [pltpu.VMEM((tm, tn), F32)],
        compiler_params=_params(("arbitrary" if norm_bwd is not None else "parallel", "parallel", "arbitrary")),
    )(a, b, *extra)


def _rms_fwd(x, g, *, name, out_dtype=BF16):
    S, D = x.shape
    tr = _pick(S, (512, 256, 128))

    def body(x_ref, g_ref, o_ref):
        o_ref[...] = _rms(x_ref[...], g_ref[...]).astype(out_dtype)

    return pl.pallas_call(
        body, name=name, out_shape=jax.ShapeDtypeStruct((S, D), out_dtype), grid=(S // tr,),
        in_specs=[pl.BlockSpec((tr, D), lambda i: (i, 0)), pl.BlockSpec((1, D), lambda i: (0, 0))],
        out_specs=pl.BlockSpec((tr, D), lambda i: (i, 0)),
        compiler_params=_params(("parallel",)),
    )(x, g.reshape(1, D))


def _rms_bwd(x, g, dy, dres, *, name):
    S, D = x.shape
    tr = _pick(S, (512, 256, 128))

    def body(x_ref, g_ref, dy_ref, *rest):
        dx_ref, dxb_ref, dg_ref = rest[-3], rest[-2], rest[-1]
        dx, part = _rms_vjp(x_ref[...], g_ref[...], dy_ref[...].astype(F32), None if dres is None else rest[0][...])
        dx_ref[...] = dx
        dxb_ref[...] = dx.astype(BF16)

        @pl.when(pl.program_id(0) == 0)
        def _():
            dg_ref[...] = part

        @pl.when(pl.program_id(0) > 0)
        def _():
            dg_ref[...] += part

    row = pl.BlockSpec((tr, D), lambda i: (i, 0))
    vec = pl.BlockSpec((1, D), lambda i: (0, 0))
    ins = [x, g.reshape(1, D), dy] + ([dres] if dres is not None else [])
    return pl.pallas_call(
        body, name=name,
        out_shape=(jax.ShapeDtypeStruct((S, D), F32), jax.ShapeDtypeStruct((S, D), BF16), jax.ShapeDtypeStruct((1, D), F32)),
        grid=(S // tr,),
        in_specs=[row, vec, row] + ([row] if dres is not None else []),
        out_specs=(row, row, vec),
        compiler_params=_params(("arbitrary",)),
    )(*ins)


def _loss_head(x, g, target, *, name):
    S, D = x.shape
    tr = _pick(S, (512, 256, 128))

    def body(x_ref, g_ref, t_ref, l_ref, dx_ref, dxb_ref, dg_ref):
        x_ = x_ref[...]
        g_ = g_ref[...]
        rstd = lax.rsqrt(jnp.mean(x_ * x_, axis=-1, keepdims=True) + EPS)
        xh = x_ * rstd
        err = xh * g_ - t_ref[...]
        lpart = (0.5 / D) * jnp.sum(jnp.sum(err * err, axis=-1, keepdims=True), axis=0, keepdims=True)
        dy = err * (1.0 / D)
        gdy = dy * g_
        dx = (gdy - xh * jnp.mean(gdy * xh, axis=-1, keepdims=True)) * rstd
        dx_ref[...] = dx
        dxb_ref[...] = dx.astype(BF16)
        gpart = jnp.sum(dy * xh, axis=0, keepdims=True)

        @pl.when(pl.program_id(0) == 0)
        def _():
            dg_ref[...] = gpart
            l_ref[...] = lpart

        @pl.when(pl.program_id(0) > 0)
        def _():
            dg_ref[...] += gpart
            l_ref[...] += lpart

    row = pl.BlockSpec((tr, D), lambda i: (i, 0))
    vec = pl.BlockSpec((1, D), lambda i: (0, 0))
    return pl.pallas_call(
        body, name=name,
        out_shape=(jax.ShapeDtypeStruct((1, 1), F32), jax.ShapeDtypeStruct((S, D), F32), jax.ShapeDtypeStruct((S, D), BF16),
                   jax.ShapeDtypeStruct((1, D), F32)),
        grid=(S // tr,),
        in_specs=[row, vec, row],
        out_specs=(pl.BlockSpec((1, 1), lambda i: (0, 0)), row, row, vec),
        compiler_params=_params(("arbitrary",)),
    )(x, g.reshape(1, D), target)


def _mask_of(mask, tq, tk, keys_first=False):
    shape, q_axis = ((tk, tq), 1) if keys_first else ((tq, tk), 0)
    qpos = lax.broadcasted_iota(jnp.int32, shape, q_axis)
    kpos = lax.broadcasted_iota(jnp.int32, shape, 1 - q_axis)
    if mask == 'causal':
        return kpos <= qpos
    return kpos <= (qpos | (CHUNK - 1))


LANES = 128
LOG2E = 1.4426950408889634


def _lane_group(j, w, width):
    lane = lax.broadcasted_iota(jnp.int32, (1, width), 1)
    return (lane >= j * w) & (lane < (j + 1) * w)


def _only(x, j, w):
    if w == x.shape[1]:
        return x
    return jnp.where(_lane_group(j, w, x.shape[1]), x, jnp.zeros_like(x))


def _per_head(cols, w):
    out = cols[-1]
    for j in range(len(cols) - 2, -1, -1):
        out = jnp.where(_lane_group(j, w, LANES), cols[j], out)
    return out


def _side_by_side(xs):
    return xs[0] if len(xs) == 1 else jnp.concatenate(xs, axis=1)


def _on_top(xs):
    return xs[0] if len(xs) == 1 else jnp.concatenate(xs, axis=0)


def _stacked(x, hp, w):
    return _on_top([_only(x, j, w) for j in range(hp)])


def _col_block(entry, rows, idx):
    arr, off, width = entry
    return pl.BlockSpec((rows, width), lambda i, j, o=off // width: (idx(i, j), o))


def _attn_fwd(qk, v, H, cq, ck, *, scale, mask, name, rider=None):
    Sq, Sk = qk[0][0][0].shape[0], v[0].shape[0]
    dv = v[2] // H
    w0 = qk[0][2]
    hp = LANES // w0
    G = H // hp
    assert dv == w0 and not qk[0][3] and all(sh and H * w == LANES for _, _, w, sh in qk[1:])
    tq = _pick(Sq, (512, 256, 128))
    tk = tq if mask else _pick(Sk, (512, 256, 128))
    nq, nk = Sq // tq, Sk // tk
    bias = cq is not None
    npart = len(qk)

    def body(*refs):
        refs = split(refs)
        q_refs, k_refs = refs[0:2 * npart:2], refs[1:2 * npart:2]
        v_ref = refs[2 * npart]
        cq_ref, ck_ref = (refs[2 * npart + 1], refs[2 * npart + 2]) if bias else (None, None)
        o_ref, lse_ref, m_s, l_s, acc_s = refs[-5:]
        qi, ki = pl.program_id(0), pl.program_id(1)

        @pl.when(ki == 0)
        def _():
            m_s[...] = jnp.full(m_s.shape, NEG, F32)
            l_s[...] = jnp.zeros(l_s.shape, F32)
            acc_s[...] = jnp.zeros(acc_s.shape, F32)

        def rows_of(vals):
            return _on_top([jnp.broadcast_to(r, (w0, tq)) for r in vals])

        def compute(masked):
            keep = _mask_of(mask, tq, tk, keys_first=True) if masked else None
            for g in range(G):
                lanes = slice(g * LANES, (g + 1) * LANES)
                q128, k128, v128 = q_refs[0][:, lanes], k_refs[0][:, lanes], v_ref[:, lanes]
                ps, alphas = [], []
                extras = list(zip(qk, q_refs, k_refs))[1:]
                k_all = _side_by_side([k128] + [k_ref[...] for _, _, k_ref in extras])
                for j in range(hp):
                    h = g * hp + j
                    q_all = _side_by_side([_only(q128, j, w0)] + [_only(q_ref[...], h, w) for (_, _, w, _), q_ref, _ in extras])
                    s = _dot(k_all, q_all, NT) * scale
                    if bias:
                        s = s + (cq_ref[h:h + 1, :] - ck_ref[:, h:h + 1])
                    if masked:
                        s = jnp.where(keep, s, NEG)
                    m_prev = m_s[h:h + 1, :]
                    m_new = jnp.maximum(m_prev, jnp.max(s, axis=0, keepdims=True))
                    alpha = jnp.exp(m_prev - m_new)
                    p = jnp.exp(s - m_new)
                    l_s[h:h + 1, :] = alpha * l_s[h:h + 1, :] + jnp.sum(p, axis=0, keepdims=True)
                    m_s[h:h + 1, :] = m_new
                    ps.append(p.astype(BF16))
                    alphas.append(alpha)
                acc_s[g] = rows_of(alphas) * acc_s[g] + _dot(_stacked(v128, hp, w0), _on_top(ps), TN)

        if mask is None:
            compute(False)
        else:
            pl.when(ki < qi)(lambda: compute(False))
            pl.when(ki == qi)(lambda: compute(True))

        @pl.when(ki == ((nk - 1) if mask is None else qi))
        def _():
            for g in range(G):
                norm = acc_s[g] / rows_of([l_s[g * hp + j:g * hp + j + 1, :] for j in range(hp)])
                o_ref[:, g * LANES:(g + 1) * LANES] = norm.T.astype(BF16)
            lse_ref[...] = jnp.zeros(lse_ref.shape, F32)
            lse_ref[0:H, :] = m_s[0:H, :] + jnp.log(l_s[0:H, :])

    q_idx = lambda i, j: i
    k_idx = (lambda i, j: jnp.minimum(i, j)) if mask else (lambda i, j: j)
    ins, in_specs = [], []
    for q_e, k_e, _, _ in qk:
        ins += [q_e[0], k_e[0]]
        in_specs += [_col_block(q_e, tq, q_idx), _col_block(k_e, tk, k_idx)]
    ins.append(v[0])
    in_specs.append(_col_block(v, tk, k_idx))
    if bias:
        in_specs += [pl.BlockSpec((8, tq), lambda i, j: (0, i)), pl.BlockSpec((tk, 8), lambda i, j: (k_idx(i, j), 0))]
        ins += [cq, ck]
    r_ins, r_in_specs, r_outs, r_out_specs, r_scratch, split = _carry(
        rider, len(ins), 2, lambda: (pl.program_id(0) == 0) & (pl.program_id(1) == 0),
        lambda: (pl.program_id(0) == nq - 1) & (pl.program_id(1) == nk - 1))
    res = pl.pallas_call(
        body, name=name,
        out_shape=(jax.ShapeDtypeStruct((Sq, H * dv), BF16), jax.ShapeDtypeStruct((8, Sq), F32), *r_outs),
        grid=(nq, nk), in_specs=in_specs + r_in_specs,
        out_specs=(pl.BlockSpec((tq, H * dv), lambda i, j: (i, 0)), pl.BlockSpec((8, tq), lambda i, j: (0, i)), *r_out_specs),
        scratch_shapes=[pltpu.VMEM((8, tq), F32), pltpu.VMEM((8, tq), F32), pltpu.VMEM((G, LANES, tq), F32)] + r_scratch,
        compiler_params=_params(("arbitrary", "arbitrary")) if rider else _params(("parallel", "arbitrary")),
    )(*ins, *r_ins)
    return (res[0], res[1], rider.post(res[2:])) if rider else res


def _attn_bwd(qk, v, H, o, do, lse, cq, ck, *, scale, mask, name, rider=None):
    Sq, Sk = qk[0][0][0].shape[0], v[0].shape[0]
    dv = v[2] // H
    w0 = qk[0][2]
    hp = LANES // w0
    G = H // hp
    tq = _pick(Sq, (512, 256, 128))
    tk = tq if mask else _pick(Sk, (512, 256, 128))
    nq, nk = Sq // tq, Sk // tk
    bias = cq is not None
    npart = len(qk)
    n_in = 2 * npart + 4 + (2 if bias else 0)

    def body(*refs):
        refs = split(refs)
        q_refs, k_refs = refs[0:2 * npart:2], refs[1:2 * npart:2]
        v_ref, o_ref, do_ref, lse_ref = refs[2 * npart:2 * npart + 4]
        cq_ref, ck_ref = (refs[2 * npart + 4], refs[2 * npart + 5]) if bias else (None, None)
        outs = refs[n_in:]
        dq_refs, dk_refs, dv_ref = outs[:npart], outs[npart:2 * npart], outs[2 * npart]
        dck_ref, dcq_ref = (outs[2 * npart + 1], outs[2 * npart + 2]) if bias else (None, None)
        dk_accs, dv_acc = refs[-(npart + 1):-1], refs[-1]
        ki, qi = pl.program_id(0), pl.program_id(1)
        first_q = ki if mask else 0

        @pl.when((ki == 0) & (qi == 0))
        def _():
            for r in dq_refs:
                r[...] = jnp.zeros(r.shape, F32)
            if bias:
                dcq_ref[...] = jnp.zeros(dcq_ref.shape, F32)

        @pl.when(qi == first_q)
        def _():
            for r in dk_accs:
                r[...] = jnp.zeros(r.shape, F32)
            dv_acc[...] = jnp.zeros(dv_acc.shape, F32)
            if bias:
                dck_ref[...] = jnp.zeros(dck_ref.shape, F32)

        def compute(masked):
            keep = _mask_of(mask, tq, tk, keys_first=True) if masked else None
            rows = pl.ds(pl.multiple_of(qi * tq, tq), tq)
            extras = list(zip(qk, q_refs, k_refs, dq_refs, dk_accs))[1:]
            for g in range(G):
                lanes = slice(g * LANES, (g + 1) * LANES)
                q128, k128, v128 = q_refs[0][:, lanes], k_refs[0][:, lanes], v_ref[:, lanes]
                do128, o128 = do_ref[:, lanes], o_ref[:, lanes]
                prod = do128.astype(F32) * o128.astype(F32)
                ps, dss = [], []
                k_all = _side_by_side([k128] + [e[2][...] for e in extras])
                for j in range(hp):
                    h = g * hp + j
                    q_all = _side_by_side([_only(q128, j, w0)] + [_only(e[1][...], h, e[0][2]) for e in extras])
                    s = _dot(k_all, q_all, NT) * (scale * LOG2E)
                    if bias:
                        s = s - ck_ref[:, h:h + 1] * LOG2E
                    if masked:
                        s = jnp.where(keep, s, NEG)
                    row = lse_ref[h:h + 1, :] - cq_ref[h:h + 1, :] if bias else lse_ref[h:h + 1, :]
                    p = jnp.exp2(s - row * LOG2E)
                    dp = _dot(v128, _only(do128, j, w0), NT)
                    delta = jnp.sum(_only(prod, j, w0), axis=1, keepdims=True).T
                    ds = p * (dp - delta)
                    if bias:
                        dck_ref[:, h:h + 1] -= jnp.sum(ds, axis=1, keepdims=True)
                        dcq_ref[h:h + 1, rows] += jnp.sum(ds, axis=0, keepdims=True)
                    ps.append(p.astype(BF16))
                    dss.append((ds * scale).astype(BF16))
                for (_, _, w, _), q_ref, k_ref, dq_ref, dk_acc in extras:
                    heads = range(g * hp, (g + 1) * hp)
                    dk_acc[...] += _dot(_side_by_side(dss), _on_top([_only(q_ref[...], h, w) for h in heads]), NN)
                    dq_ref[rows, :] += _dot(_on_top(dss), _on_top([_only(k_ref[...], h, w) for h in heads]), TN)
                dv_acc[:, lanes] += _dot(_side_by_side(ps), _stacked(do128, hp, w0), NN)
                dk_accs[0][:, lanes] += _dot(_side_by_side(dss), _stacked(q128, hp, w0), NN)
                dq_refs[0][rows, lanes] += _dot(_on_top(dss), _stacked(k128, hp, w0), TN)

        if mask is None:
            compute(False)
        else:
            pl.when(qi > ki)(lambda: compute(False))
            pl.when(qi == ki)(lambda: compute(True))

        @pl.when(qi == nq - 1)
        def _():
            for r, acc in zip(dk_refs, dk_accs):
                r[...] = acc[...]
            dv_ref[...] = dv_acc[...]

    q_idx = (lambda j, i: jnp.maximum(i, j)) if mask else (lambda j, i: i)
    k_idx = lambda j, i: j
    ins, in_specs, dq_shapes, dq_specs, dk_shapes, dk_specs, scratch = [], [], [], [], [], [], []
    for q_e, k_e, w, shared in qk:
        ins += [q_e[0], k_e[0]]
        in_specs += [_col_block(q_e, tq, q_idx), _col_block(k_e, tk, k_idx)]
        dq_shapes.append(jax.ShapeDtypeStruct((Sq, H * w), F32))
        dq_specs.append(pl.BlockSpec((Sq, H * w), lambda j, i: (0, 0)))
        kw = k_e[2]
        dk_shapes.append(jax.ShapeDtypeStruct((Sk, kw), F32))
        dk_specs.append(pl.BlockSpec((tk, kw), lambda j, i: (j, 0)))
        scratch.append(pltpu.VMEM((tk, kw), F32))
    row_q = lambda width: pl.BlockSpec((tq, width), lambda j, i: (q_idx(j, i), 0))
    per_q = pl.BlockSpec((8, tq), lambda j, i: (0, q_idx(j, i)))
    ins += [v[0], o, do, lse]
    in_specs += [_col_block(v, tk, k_idx), row_q(H * dv), row_q(H * dv), per_q]
    out_shape = dq_shapes + dk_shapes + [jax.ShapeDtypeStruct((Sk, H * dv), F32)]
    out_specs = dq_specs + dk_specs + [pl.BlockSpec((tk, H * dv), lambda j, i: (j, 0))]
    if bias:
        in_specs += [per_q, pl.BlockSpec((tk, 8), lambda j, i: (j, 0))]
        ins += [cq, ck]
        out_shape += [jax.ShapeDtypeStruct((Sk, 8), F32), jax.ShapeDtypeStruct((8, Sq), F32)]
        out_specs += [pl.BlockSpec((tk, 8), lambda j, i: (j, 0)), pl.BlockSpec((8, Sq), lambda j, i: (0, 0))]
    scratch.append(pltpu.VMEM((tk, H * dv), F32))
    n_out = len(out_shape)
    r_ins, r_in_specs, r_outs, r_out_specs, r_scratch, split = _carry(
        rider, len(ins), n_out, lambda: (pl.program_id(0) == 0) & (pl.program_id(1) == 0),
        lambda: (pl.program_id(0) == nk - 1) & (pl.program_id(1) == nq - 1))
    res = pl.pallas_call(
        body, name=name, out_shape=tuple(out_shape + r_outs), grid=(nk, nq), in_specs=in_specs + r_in_specs,
        out_specs=tuple(out_specs + r_out_specs), scratch_shapes=scratch + r_scratch,
        compiler_params=_params(("arbitrary", "arbitrary")),
    )(*ins, *r_ins)
    own = (list(res[:npart]), list(res[npart:2 * npart]), res[2 * npart]) + tuple(res[2 * npart + 1:n_out])
    return own + (rider.post(res[n_out:]),) if rider else own


def _split3_dot(x, t):
    hi = x.astype(BF16)
    r1 = x - hi.astype(F32)
    mid = r1.astype(BF16)
    lo = (r1 - mid.astype(F32)).astype(BF16)
    return _dot(hi, t, NN) + _dot(mid, t, NN) + _dot(lo, t, NN)


def _fox_cum_fwd(ff_t, b, *, name):
    _, S = ff_t.shape
    tb = _pick(S, (512, 256, 128))

    def body(f_ref, b_ref, o_ref, carry):
        @pl.when(pl.program_id(0) == 0)
        def _():
            carry[...] = jnp.zeros(carry.shape, F32)

        lf = _log_sigmoid(f_ref[...] + b_ref[...])
        o_ref[...] = _split3_dot(lf, _tri(tb, False)) + carry[...]
        carry[...] += jnp.sum(lf, axis=1, keepdims=True)

    return pl.pallas_call(
        body, name=name, out_shape=jax.ShapeDtypeStruct((8, S), F32), grid=(S // tb,),
        in_specs=[pl.BlockSpec((8, tb), lambda i: (0, i)), pl.BlockSpec((8, 1), lambda i: (0, 0))],
        out_specs=pl.BlockSpec((8, tb), lambda i: (0, i)),
        scratch_shapes=[pltpu.VMEM((8, 1), F32)],
        compiler_params=_params(("arbitrary",)),
    )(ff_t, b)


def _fox_cum_bwd(ff_t, b, dcum_t, *, name):
    _, S = ff_t.shape
    tb = _pick(S, (512, 256, 128))
    nb = S // tb

    def body(f_ref, b_ref, dc_ref, df_ref, db_ref, carry):
        @pl.when(pl.program_id(0) == 0)
        def _():
            carry[...] = jnp.zeros(carry.shape, F32)
            db_ref[...] = jnp.zeros(db_ref.shape, F32)

        dc = dc_ref[...]
        dlf = _split3_dot(dc, _tri(tb, True)) + carry[...]
        carry[...] += jnp.sum(dc, axis=1, keepdims=True)
        df = dlf * _sigmoid(-(f_ref[...] + b_ref[...]))
        df_ref[...] = df
        db_ref[...] += jnp.sum(df, axis=1, keepdims=True)

    rev = lambda i: (0, nb - 1 - i)
    return pl.pallas_call(
        body, name=name,
        out_shape=(jax.ShapeDtypeStruct((8, S), F32), jax.ShapeDtypeStruct((8, 1), F32)), grid=(nb,),
        in_specs=[pl.BlockSpec((8, tb), rev), pl.BlockSpec((8, 1), lambda i: (0, 0)), pl.BlockSpec((8, tb), rev)],
        out_specs=(pl.BlockSpec((8, tb), rev), pl.BlockSpec((8, 1), lambda i: (0, 0))),
        scratch_shapes=[pltpu.VMEM((8, 1), F32)],
        compiler_params=_params(("arbitrary",)),
    )(ff_t, b, dcum_t)


GLA_W = GLA_HEADS * GLA_DK
GLA_BLOCK_CHUNKS = 4


def _same_chunk(n, lower):
    r = lax.broadcasted_iota(jnp.int32, (n, n), 0)
    c = lax.broadcasted_iota(jnp.int32, (n, n), 1)
    same = (r | (CHUNK - 1)) == (c | (CHUNK - 1))
    return jnp.where(same & (r >= c) if lower else same, 1.0, 0.0).astype(BF16)


def _chunk_mix(x, t, transpose):
    hi, lo = _split2(x)
    dims = TN if transpose else NN
    return _dot(t, hi, dims) + _dot(t, lo, dims)


@jax.custom_vjp
def chunk_cumsum(x):
    return _chunk_mix(x, _same_chunk(x.shape[0], True), False)


chunk_cumsum.defvjp(lambda x: (chunk_cumsum(x), None), lambda _, g: (_chunk_mix(g, _same_chunk(g.shape[0], True), True),))


@jax.custom_vjp
def chunk_total(x):
    return _chunk_mix(x, _same_chunk(x.shape[0], False), False)


chunk_total.defvjp(lambda x: (chunk_total(x), None), lambda _, g: (_chunk_mix(g, _same_chunk(g.shape[0], False), False),))


def _gla_block(q, k, zsm, wg, bg, go, vs, rs, states):
    n_chunks = q.shape[0] // CHUNK
    la = _log_sigmoid(bdot(zsm, wg) + bg) * (1.0 / GLA_TAU)
    end = chunk_total(la)
    kd = k * jnp.exp(end - chunk_cumsum(la))
    a = jnp.exp(end)
    qs = q * (GLA_DK ** -0.5)
    lane = lax.broadcasted_iota(jnp.int32, (1, GLA_W), 1)
    outs, new_states = [], []
    for h in range(GLA_HEADS):
        kdh = kd * jnp.where((lane >= h * GLA_DK) & (lane < (h + 1) * GLA_DK), 1.0, 0.0)
        st, o = states[h], []
        for c in range(n_chunks):
            rows = slice(c * CHUNK, (c + 1) * CHUNK)
            st = st * a[c * CHUNK:c * CHUNK + 1] + bdot_tn(vs[h][rows], kdh[rows])
            o.append(bdot_nt(qs[rows], st))
        o = _rms(jnp.concatenate(o, axis=0), go)
        outs.append(o * (rs[h] * _sigmoid(rs[h])))
        new_states.append(st)
    return outs, new_states


def _gla_fwd(z, zsm, wg, bg, go, cols, *, name):
    S = z.shape[0]
    rb = GLA_BLOCK_CHUNKS * CHUNK
    nb = S // rb
    cq, ckk, cv, cr = cols
    H = GLA_HEADS

    def body(q_ref, k_ref, zsm_ref, wg_ref, bg_ref, go_ref, *rest):
        v_refs, r_refs = rest[:H], rest[H:2 * H]
        o_ref, st_ref, state = rest[2 * H], rest[2 * H + 1], rest[2 * H + 2]

        @pl.when(pl.program_id(0) == 0)
        def _():
            state[...] = jnp.zeros(state.shape, F32)

        states = [state[h] for h in range(H)]
        for h in range(H):
            st_ref[0, h] = states[h]
        outs, new_states = _gla_block(
            q_ref[...].astype(F32), k_ref[...].astype(F32), zsm_ref[...], wg_ref[...], bg_ref[...], go_ref[...],
            [v_refs[h][...].astype(F32) for h in range(H)], [r_refs[h][...].astype(F32) for h in range(H)], states)
        for h in range(H):
            o_ref[:, h * GLA_DV:(h + 1) * GLA_DV] = outs[h].astype(BF16)
            state[h] = new_states[h]

    def col(width, off):
        return pl.BlockSpec((rb, width), lambda i, o=off // width: (i, o))

    full = lambda shp: pl.BlockSpec(shp, lambda i: (0,) * len(shp))
    in_specs = [col(GLA_W, cq), col(GLA_W, ckk), pl.BlockSpec((rb, 128), lambda i: (i, 0)),
                full((128, GLA_W)), full((1, GLA_W)), full((1, GLA_DV))]
    in_specs += [col(GLA_DV, cv + h * GLA_DV) for h in range(H)] + [col(GLA_DV, cr + h * GLA_DV) for h in range(H)]
    return pl.pallas_call(
        body, name=name,
        out_shape=(jax.ShapeDtypeStruct((S, H * GLA_DV), BF16), jax.ShapeDtypeStruct((nb, H, GLA_DV, GLA_W), F32)),
        grid=(nb,), in_specs=in_specs,
        out_specs=(pl.BlockSpec((rb, H * GLA_DV), lambda i: (i, 0)),
                   pl.BlockSpec((1, H, GLA_DV, GLA_W), lambda i: (i, 0, 0, 0))),
        scratch_shapes=[pltpu.VMEM((H, GLA_DV, GLA_W), F32)],
        compiler_params=_params(("arbitrary",)),
    )(z, z, zsm, wg, bg, go, *([z] * (2 * H)))


def _gla_bwd(z, zsm, wg, bg, go, states, do, cols, *, name):
    S = z.shape[0]
    rb = GLA_BLOCK_CHUNKS * CHUNK
    nb = S // rb
    cq, ckk, cv, cr = cols
    H = GLA_HEADS

    def body(q_ref, k_ref, zsm_ref, wg_ref, bg_ref, go_ref, st_ref, do_ref, *rest):
        v_refs, r_refs = rest[:H], rest[H:2 * H]
        dq_ref, dk_ref, dv_ref, dr_ref, dzsm_ref, dwg_ref, dbg_ref, dgo_ref, dstate = rest[2 * H:]

        @pl.when(pl.program_id(0) == 0)
        def _():
            dstate[...] = jnp.zeros(dstate.shape, F32)
            dwg_ref[...] = jnp.zeros(dwg_ref.shape, F32)
            dbg_ref[...] = jnp.zeros(dbg_ref.shape, F32)
            dgo_ref[...] = jnp.zeros(dgo_ref.shape, F32)

        prim = (q_ref[...].astype(F32), k_ref[...].astype(F32), zsm_ref[...], wg_ref[...], bg_ref[...], go_ref[...],
                [v_refs[h][...].astype(F32) for h in range(H)], [r_refs[h][...].astype(F32) for h in range(H)],
                [st_ref[0, h] for h in range(H)])
        _, vjp = jax.vjp(_gla_block, *prim)
        douts = [do_ref[:, h * GLA_DV:(h + 1) * GLA_DV].astype(F32) for h in range(H)]
        dq, dk, dzs, dwg, dbg, dgo, dvs, drs, dsts = vjp((douts, [dstate[h] for h in range(H)]))
        dq_ref[...] = dq.astype(BF16)
        dk_ref[...] = dk.astype(BF16)
        dzsm_ref[...] = dzs
        dwg_ref[...] += dwg
        dbg_ref[...] += dbg
        dgo_ref[...] += dgo
        for h in range(H):
            dv_ref[:, h * GLA_DV:(h + 1) * GLA_DV] = dvs[h].astype(BF16)
            dr_ref[:, h * GLA_DV:(h + 1) * GLA_DV] = drs[h].astype(BF16)
            dstate[h] = dsts[h]

    rev = lambda i: nb - 1 - i

    def col(width, off):
        return pl.BlockSpec((rb, width), lambda i, o=off // width: (rev(i), o))

    full = lambda shp: pl.BlockSpec(shp, lambda i: (0,) * len(shp))
    rowb = lambda w: pl.BlockSpec((rb, w), lambda i: (rev(i), 0))
    in_specs = [col(GLA_W, cq), col(GLA_W, ckk), rowb(128), full((128, GLA_W)), full((1, GLA_W)), full((1, GLA_DV)),
                pl.BlockSpec((1, H, GLA_DV, GLA_W), lambda i: (rev(i), 0, 0, 0)), rowb(H * GLA_DV)]
    in_specs += [col(GLA_DV, cv + h * GLA_DV) for h in range(H)] + [col(GLA_DV, cr + h * GLA_DV) for h in range(H)]
    return pl.pallas_call(
        body, name=name,
        out_shape=(jax.ShapeDtypeStruct((S, GLA_W), BF16), jax.ShapeDtypeStruct((S, GLA_W), BF16),
                   jax.ShapeDtypeStruct((S, H * GLA_DV), BF16), jax.ShapeDtypeStruct((S, H * GLA_DV), BF16),
                   jax.ShapeDtypeStruct((S, 128), F32), jax.ShapeDtypeStruct((128, GLA_W), F32),
                   jax.ShapeDtypeStruct((1, GLA_W), F32), jax.ShapeDtypeStruct((1, GLA_DV), F32)),
        grid=(nb,), in_specs=in_specs,
        out_specs=(rowb(GLA_W), rowb(GLA_W), rowb(H * GLA_DV), rowb(H * GLA_DV), rowb(128),
                   full((128, GLA_W)), full((1, GLA_W)), full((1, GLA_DV))),
        scratch_shapes=[pltpu.VMEM((H, GLA_DV, GLA_W), F32)],
        compiler_params=_params(("arbitrary",)),
    )(z, z, zsm, wg, bg, go, states, do, *([z] * (2 * H)))


def _row_spec(entry, tr):
    if isinstance(entry, tuple):
        arr, width, off = entry
        return arr, pl.BlockSpec((tr, width), lambda i, o=off // width: (i, o))
    return entry, pl.BlockSpec((tr, entry.shape[1]), lambda i: (i, 0))


def _stage_fwd(fn, rows, consts, outs, *, name, tr=None):
    first = rows[0][0] if isinstance(rows[0], tuple) else rows[0]
    S = first.shape[0]
    tr = tr or _pick(S, (512, 256, 128))
    arrs, specs = zip(*[_row_spec(e, tr) for e in rows])
    nr, nc = len(rows), len(consts)

    def body(*refs):
        vals = [r[...].astype(F32) for r in refs[:nr + nc]]
        res = fn(*vals)
        for o_ref, val in zip(refs[nr + nc:], res):
            o_ref[...] = val.astype(o_ref.dtype)

    cspecs = [pl.BlockSpec(c.shape, lambda i, n=c.ndim: (0,) * n) for c in consts]
    return pl.pallas_call(
        body, name=name,
        out_shape=tuple(jax.ShapeDtypeStruct((S, w), dt) for w, dt in outs), grid=(S // tr,),
        in_specs=list(specs) + cspecs,
        out_specs=tuple(pl.BlockSpec((tr, w), lambda i: (i, 0)) for w, _ in outs),
        compiler_params=_params(("parallel",)),
    )(*arrs, *consts)


def _stage_bwd(fn, rows, consts, cts, n_diff, drow_dtypes, *, name, tr=None, lead=None):
    first = rows[0][0] if isinstance(rows[0], tuple) else rows[0]
    S = first.shape[0]
    tr = tr or _pick(S, (512, 256, 128))
    arrs, specs = zip(*[_row_spec(e, tr) for e in rows])
    widths = [e[1] if isinstance(e, tuple) else e.shape[1] for e in rows]
    nr, nc, nt = len(rows), len(consts), len(cts)
    n_lead, lead_width = lead or (1, widths[0])
    n_rows_out = n_diff - n_lead + 1

    def body(*refs):
        vals = [r[...].astype(F32) for r in refs[:nr + nc]]
        ct = [r[...].astype(F32) for r in refs[nr + nc:nr + nc + nt]]
        drow_refs = refs[nr + nc + nt:nr + nc + nt + n_rows_out]
        dconst_refs = refs[nr + nc + nt + n_rows_out:]
        rest_rows = vals[n_diff:nr]

        def f(diff_rows, cs):
            return tuple(fn(*diff_rows, *rest_rows, *cs))

        _, vjp = jax.vjp(f, vals[:n_diff], vals[nr:])
        drows, dcs = vjp(tuple(ct))
        off = 0
        for val, w in zip(drows[:n_lead], widths):
            drow_refs[0][:, off:off + w] = val.astype(drow_refs[0].dtype)
            off += w
        for r, val in zip(drow_refs[1:], drows[n_lead:]):
            r[...] = val.astype(r.dtype)
        first_step = pl.program_id(0) == 0
        for r, val in zip(dconst_refs, dcs):
            @pl.when(first_step)
            def _(r=r, val=val):
                r[...] = val

            @pl.when(jnp.logical_not(first_step))
            def _(r=r, val=val):
                r[...] += val

    cspecs = [pl.BlockSpec(c.shape, lambda i, n=c.ndim: (0,) * n) for c in consts]
    ctspecs = [pl.BlockSpec((tr, c.shape[1]), lambda i: (i, 0)) for c in cts]
    out_shape = [jax.ShapeDtypeStruct((S, lead_width), drow_dtypes[0])]
    out_shape += [jax.ShapeDtypeStruct((S, widths[j]), drow_dtypes[j]) for j in range(n_lead, n_diff)]
    out_shape += [jax.ShapeDtypeStruct(c.shape, F32) for c in consts]
    out_specs = [pl.BlockSpec((tr, sum(widths[:n_lead])), lambda i: (i, 0))]
    out_specs += [pl.BlockSpec((tr, widths[j]), lambda i: (i, 0)) for j in range(n_lead, n_diff)] + cspecs
    res = pl.pallas_call(
        body, name=name, out_shape=tuple(out_shape), grid=(S // tr,),
        in_specs=list(specs) + cspecs + ctspecs, out_specs=tuple(out_specs),
        compiler_params=_params(("arbitrary",)),
    )(*arrs, *consts, *cts)
    return list(res[:n_rows_out]), list(res[n_rows_out:])


def _mla_prep_fn(cq, ckv, kr, kr_sw, cos, sin, gq, gkv, wq_n, wq_r, wq_sw, wk, wv):
    hq = _rms(cq, gq)
    hkv = _rms(ckv, gkv)
    return (bdot(hq, wq_n), bdot(hq, wq_r) * cos + bdot(hq, wq_sw) * sin,
            bdot(hkv, wk), bdot(hkv, wv), kr * cos + kr_sw * sin)


def _merge_fn(g0, g1, g2, of, og, om, b0, b1, b2, wf, wg, wm):
    return (_sigmoid(g0 + b0) * bdot(of, wf) + _sigmoid(g1 + b1) * bdot(og, wg) + _sigmoid(g2 + b2) * bdot(om, wm),)


_IN_SIZES = (256, 256, 256, 4, 256, 256, 512, 16, 512, 256, 128, 32, 3072)
_IN_OFF = np.concatenate([[0], np.cumsum(_IN_SIZES)])
(_O_FQ, _O_FK, _O_FV, _O_FF, _O_GQ, _O_GK, _O_GV, _O_GLOW, _O_GR, _O_MQ, _O_MKV, _O_MKR, _O_ZG) = [int(o) for o in _IN_OFF[:-1]]
N_IN = int(_IN_OFF[-1])
_BIG_GROUPS = ((_O_ZG, 3072), (_O_GV, 512), (_O_GR, 512), (_O_FQ, 256), (_O_FK, 256), (_O_FV, 256),
               (_O_GQ, 256), (_O_GK, 256), (_O_MQ, 256), (_O_MKV, 128))
Z_GATE, Z_GV, Z_GR, Z_FQ, Z_FK, Z_FV, Z_GQ, Z_GK, Z_MQ, Z_MKV = [int(o) for o in
                                                                    np.concatenate([[0], np.cumsum([w for _, w in _BIG_GROUPS])])[:-1]]
N_BIG = sum(w for _, w in _BIG_GROUPS)
_HALF = MLA_ROPE // 2
_QK_HD = MLA_NOPE + MLA_ROPE
SM_FF, SM_GLOW, SM_KR, SM_KR_SW, N_SM = 0, 8, 128, 256, 384
N_PAD = N_BIG + N_SM
_IN_SEGS = ([(o, w, 1.0) for o, w in _BIG_GROUPS]
            + [(_O_FF, 4, 1.0), (None, SM_GLOW - 4, 0.0), (_O_GLOW, GLA_RANK, 1.0), (None, 128 - SM_GLOW - GLA_RANK, 0.0)]
            + [(_O_MKR, MLA_ROPE, 1.0)] * MLA_HEADS
            + [(_O_MKR + _HALF, _HALF, -1.0), (_O_MKR, _HALF, 1.0)] * MLA_HEADS)


def _cols(x, start, width):
    return lax.slice_in_dim(x, start, start + width, axis=x.ndim - 1)


def _pad_w_in(w):
    return jnp.concatenate([jnp.zeros(w.shape[:-1] + (n,), w.dtype) if src is None else
                            (_cols(w, src, n) if sign > 0 else -_cols(w, src, n)) for src, n, sign in _IN_SEGS], axis=-1)


def _unpad_w_in(g):
    groups = []
    for o, n in zip(_IN_OFF[:-1], _IN_SIZES):
        total, pos = None, 0
        for src, m, sign in _IN_SEGS:
            if src is not None and o <= src and src + m <= o + n:
                term = _cols(g, pos, m) if sign > 0 else -_cols(g, pos, m)
                if m != n:
                    term = jnp.pad(term, [(0, 0)] * (g.ndim - 1) + [(int(src - o), int(o + n - src - m))])
                total = term if total is None else total + term
            pos += m
        groups.append(total)
    return jnp.concatenate(groups, axis=-1)


def _take(x, idx):
    idx = np.asarray(idx)
    cuts = [0] + [i for i in range(1, len(idx)) if idx[i] != idx[i - 1] + 1] + [len(idx)]
    return jnp.concatenate([_cols(x, int(idx[a]), b - a) for a, b in zip(cuts[:-1], cuts[1:])], axis=1)


_UQ_NOPE = np.concatenate([np.arange(h * _QK_HD, h * _QK_HD + MLA_NOPE) for h in range(MLA_HEADS)])
_UQ_ROT = np.concatenate([np.arange(h * _QK_HD + MLA_NOPE, (h + 1) * _QK_HD) for h in range(MLA_HEADS)])
_UKV_PERM = np.concatenate(
    [np.concatenate([np.arange(h * 128, h * 128 + MLA_NOPE) for h in range(MLA_HEADS)]),
     np.concatenate([np.arange(h * 128 + MLA_NOPE, (h + 1) * 128) for h in range(MLA_HEADS)])])
_UKV_INV = np.argsort(_UKV_PERM)


def _rotary_partner(r):
    return jnp.concatenate([piece for h in range(MLA_HEADS) for piece in
                            (-_cols(r, h * MLA_ROPE + _HALF, _HALF), _cols(r, h * MLA_ROPE, _HALF))], axis=1)


def _uq_grad(dn, dr, dsw):
    dr = dr + jnp.concatenate([piece for h in range(MLA_HEADS) for piece in
                               (_cols(dsw, h * MLA_ROPE + _HALF, _HALF), -_cols(dsw, h * MLA_ROPE, _HALF))], axis=1)
    return jnp.concatenate([piece for h in range(MLA_HEADS) for piece in
                            (_cols(dn, h * MLA_NOPE, MLA_NOPE), _cols(dr, h * MLA_ROPE, MLA_ROPE))], axis=1)


def _rope_tables(S):
    inv = ROPE_BASE ** (-jnp.arange(_HALF, dtype=F32) / _HALF)
    ang = jnp.arange(S, dtype=F32)[:, None] * inv[None, :]
    return jnp.tile(jnp.cos(ang), (1, 2 * MLA_HEADS)), jnp.tile(jnp.sin(ang), (1, 2 * MLA_HEADS))


class _LayerParams:
    def __init__(self, rep, l):
        self.w, self.rep, self.l, self.made = {}, rep, l, {}

    def __getitem__(self, k):
        if k not in self.made:
            self.made[k] = self._make(k)
        return self.made[k]

    def _make(self, k):
        w, rep, l = self.w, self.rep, self.l
        if k == 'wg':
            return jnp.pad(w['w_gla_gate'], [(SM_GLOW, LANES - SM_GLOW - GLA_RANK), (0, 0)])
        if k in ('wq_n', 'wq_r'):
            return _take(w['w_mla_uq'], _UQ_NOPE if k == 'wq_n' else _UQ_ROT)
        if k == 'wq_sw':
            return _rotary_partner(self['wq_r'])
        if k in ('wk', 'wv'):
            return _take(w['w_mla_ukv'], _UKV_PERM[:256] if k == 'wk' else _UKV_PERM[256:])
        if k == 'b_f':
            return jnp.zeros((8, 1), F32).at[:FOX_HEADS, 0].set(rep['b_fox_forget'][l])
        if k == 'b_gate':
            return [rep['b_branch_gate'][l][i * 1024:(i + 1) * 1024].reshape(1, 1024) for i in range(3)]
        vec = {'bg': 'b_gla_gate', 'go': 'g_gla_out', 'gq': 'g_mla_q', 'gkv': 'g_mla_kv'}
        if k in vec:
            return rep[vec[k]][l].reshape(1, -1)
        return rep[k][l] if k in rep else w[k]


_GLA_COLS = (Z_GQ, Z_GK, Z_GV, Z_GR)
_MLA_OUTS = [(256, BF16), (128, BF16), (256, BF16), (256, BF16), (128, BF16)]


def _mla_rows(z, zsm, rope):
    return [(z, 256, Z_MQ), (z, 128, Z_MKV), (zsm, 128, SM_KR), (zsm, 128, SM_KR_SW), *rope]


def _mla_consts(p):
    return [p['gq'], p['gkv'], p['wq_n'], p['wq_r'], p['wq_sw'], p['wk'], p['wv']]


def _fox_qkv(z):
    return [((z, Z_FQ, 256), (z, Z_FK, 256), FOX_HD, False)], (z, Z_FV, 256)


def _mla_qkv(qn, qr, kn, vv, kr):
    return [((qn, 0, 256), (kn, 0, 256), MLA_NOPE, False), ((qr, 0, 128), (kr, 0, 128), MLA_ROPE, True)], (vv, 0, 256)


def _xa_qkv(qx, kvx):
    return [((qx, 0, 512), (kvx, 0, 512), XA_HD, False)], (kvx, 512, 512)


def _merge_rows(z, o_fox, o_gla, o_mla):
    return [(z, 1024, Z_GATE), (z, 1024, Z_GATE + 1024), (z, 1024, Z_GATE + 2048), o_fox, o_gla, o_mla]


def _merge_consts(p):
    return p['b_gate'] + [p['w_up_fox'], p['w_up_gla'], p['w_up_mla']]


def _carried(hooks, key, call):
    rider, sink = hooks.pop(key, (None, None))
    res = call(rider=rider)
    if rider is None:
        return res
    sink(res[-1])
    return res[:-1]


def _layer_fwd(x0, mem, p, rope, l, hooks):
    S = x0.shape[0]
    sv = {'x0': x0}
    h1 = _rms_fwd(x0, p['g_mix'], name=f"rms_mix_{l}")
    z = _mm(h1, p['w_in'], mode='nn', out_dtype=BF16, b_cols=(0, N_BIG), name=f"in_big_{l}")
    zsm = _mm(h1, p['w_in'], mode='nn', out_dtype=F32, b_cols=(N_BIG, N_SM), name=f"in_small_{l}")
    sv.update(h1=h1, z=z, zsm=zsm)
    ff_t = jnp.zeros((8, S), F32).at[:FOX_HEADS].set(zsm[:, SM_FF:SM_FF + FOX_HEADS].T)
    cum_t = _fox_cum_fwd(ff_t, p['b_f'], name=f"fox_cum_{l}")
    cum = cum_t.T
    o_fox, lse_f = _carried(hooks, (l, 'fox_fwd'), lambda rider: _attn_fwd(
        *_fox_qkv(z), FOX_HEADS, cum_t, cum, scale=FOX_HD ** -0.5, mask='causal', name=f"fox_fwd_{l}", rider=rider))
    sv.update(ff_t=ff_t, cum=cum, cum_t=cum_t, lse_f=lse_f, o_fox=o_fox)
    o_gla, states = _gla_fwd(z, zsm, p['wg'], p['bg'], p['go'], _GLA_COLS, name=f"gla_fwd_{l}")
    sv.update(o_gla=o_gla, states=states)
    mla = _stage_fwd(_mla_prep_fn, _mla_rows(z, zsm, rope), _mla_consts(p), _MLA_OUTS, name=f"mla_prep_{l}")
    o_mla, lse_m = _carried(hooks, (l, 'mla_fwd'), lambda rider: _attn_fwd(
        *_mla_qkv(*mla), MLA_HEADS, None, None, scale=_QK_HD ** -0.5, mask='chunk', name=f"mla_fwd_{l}", rider=rider))
    sv.update(mla=mla, lse_m=lse_m, o_mla=o_mla)
    (y,) = _stage_fwd(_merge_fn, _merge_rows(z, o_fox, o_gla, o_mla), _merge_consts(p), [(1024, BF16)], name=f"merge_{l}")
    x1 = _mm(y, p['w_out'], mode='nn', out_dtype=F32, residual=x0, name=f"out_proj_{l}")
    sv.update(y=y, x1=x1)
    h2 = _rms_fwd(x1, p['g_xa'], name=f"rms_xa_{l}")
    hm = _rms_fwd(mem, p['g_mem'], name=f"rms_mem_{l}")
    qx = _mm(h2, p['w_xq'], mode='nn', out_dtype=BF16, name=f"xq_{l}")
    kvx = _mm(hm, p['w_xkv'], mode='nn', out_dtype=BF16, name=f"xkv_{l}")
    ox, lse_x = _attn_fwd(*_xa_qkv(qx, kvx), XA_HEADS, None, None, scale=XA_HD ** -0.5, mask=None, name=f"xa_fwd_{l}")
    x2 = _mm(ox, p['w_xo'], mode='nn', out_dtype=F32, residual=x1, name=f"xo_{l}")
    sv.update(h2=h2, hm=hm, qx=qx, kvx=kvx, lse_x=lse_x, ox=ox, x2=x2)
    h3 = _rms_fwd(x2, p['g_mlp'], name=f"rms_mlp_{l}")
    a = _mm(h3, p['w_mlp1'], mode='nn', out_dtype=BF16, name=f"mlp1_{l}")
    x3 = _mm(a, p['w_mlp2'], mode='nn', out_dtype=F32, act='relu2', residual=x2, name=f"mlp2_{l}")
    sv.update(h3=h3, a=a)
    return x3, sv


def _layer_bwd(dx3, dx3b, mem, p, rope, sv, l, hooks, half_done):
    S = dx3.shape[0]
    g = {}
    da = _mm(dx3b, p['w_mlp2'], mode='nt', out_dtype=BF16, drelu_of=sv['a'], name=f"d_mlp2_in_{l}")
    g['w_mlp2'] = _mm(sv['a'], dx3b, mode='tn', out_dtype=BF16, act='relu2', name=f"d_w_mlp2_{l}")
    dx2, dx2b, g['g_mlp'] = _mm(da, p['w_mlp1'], mode='nt', out_dtype=F32, norm_bwd=(sv['x2'], p['g_mlp'], dx3), tm=512,
                                name=f"d_mlp1_in_{l}")
    g['w_mlp1'] = _mm(sv['h3'], da, mode='tn', out_dtype=BF16, col_shards=N_DEV, name=f"d_w_mlp1_{l}")
    dox = _mm(dx2b, p['w_xo'], mode='nt', out_dtype=BF16, name=f"d_xo_in_{l}")
    g['w_xo'] = _mm(sv['ox'], dx2b, mode='tn', out_dtype=BF16, name=f"d_w_xo_{l}")
    (dqx,), (dkx,), dvx = _attn_bwd(*_xa_qkv(sv['qx'], sv['kvx']), XA_HEADS, sv['ox'], dox, sv['lse_x'], None, None,
                                    scale=XA_HD ** -0.5, mask=None, name=f"xa_bwd_{l}")
    dqx = dqx.astype(BF16)
    dkvx = jnp.concatenate([dkx, dvx], axis=1).astype(BF16)
    dx1, dx1b, g['g_xa'] = _mm(dqx, p['w_xq'], mode='nt', out_dtype=F32, norm_bwd=(sv['x1'], p['g_xa'], dx2), tm=512,
                               name=f"d_xq_in_{l}")
    g['w_xq'] = _mm(sv['h2'], dqx, mode='tn', out_dtype=BF16, name=f"d_w_xq_{l}")
    dhm = _mm(dkvx, p['w_xkv'], mode='nt', out_dtype=F32, name=f"d_xkv_in_{l}")
    g['w_xkv'] = _mm(sv['hm'], dkvx, mode='tn', out_dtype=BF16, name=f"d_w_xkv_{l}")
    _, _, g['g_mem'] = _rms_bwd(mem, p['g_mem'], dhm, None, name=f"d_rms_mem_{l}")
    dy = _mm(dx1b, p['w_out'], mode='nt', out_dtype=F32, name=f"d_out_in_{l}")
    g['w_out'] = _mm(sv['y'], dx1b, mode='tn', out_dtype=BF16, name=f"d_w_out_{l}")
    z, zsm = sv['z'], sv['zsm']
    (dz, do_fox, do_gla, do_mla), (db0, db1, db2, g['w_up_fox'], g['w_up_gla'], g['w_up_mla']) = _stage_bwd(
        _merge_fn, _merge_rows(z, sv['o_fox'], sv['o_gla'], sv['o_mla']), _merge_consts(p), [dy], 6, [BF16] * 6,
        lead=(3, N_PAD), name=f"merge_bwd_{l}")
    g['b_branch_gate'] = jnp.concatenate([db0, db1, db2], axis=1).reshape(-1)
    half_done(l, g)
    (dfq,), (dfk,), dfv, dck, dcq = _carried(hooks, (l, 'fox_bwd'), lambda rider: _attn_bwd(
        *_fox_qkv(z), FOX_HEADS, sv['o_fox'], do_fox, sv['lse_f'], sv['cum_t'], sv['cum'],
        scale=FOX_HD ** -0.5, mask='causal', name=f"fox_bwd_{l}", rider=rider))
    dff_t, db_f = _fox_cum_bwd(sv['ff_t'], p['b_f'], dcq + dck.T, name=f"fox_cum_bwd_{l}")
    g['b_fox_forget'] = db_f[:FOX_HEADS, 0]
    dgq, dgk, dgv, dgr, dzsm, dwg, dbg, dgo = _gla_bwd(z, zsm, p['wg'], p['bg'], p['go'], sv['states'], do_gla, _GLA_COLS,
                                                       name=f"gla_bwd_{l}")
    g['w_gla_gate'] = dwg[SM_GLOW:SM_GLOW + GLA_RANK]
    g['b_gla_gate'] = dbg.reshape(-1)
    g['g_gla_out'] = dgo.reshape(-1)
    (dmqn, dmqr), (dmkn, dmkr), dmv = _carried(hooks, (l, 'mla_bwd'), lambda rider: _attn_bwd(
        *_mla_qkv(*sv['mla']), MLA_HEADS, sv['o_mla'], do_mla, sv['lse_m'], None, None,
        scale=_QK_HD ** -0.5, mask='chunk', name=f"mla_bwd_{l}", rider=rider))
    (dcq, dckv, dkr, dkr_sw), (dgq_n, dgkv_n, dwq_n, dwq_r, dwq_sw, dwk, dwv) = _stage_bwd(
        _mla_prep_fn, _mla_rows(z, zsm, rope), _mla_consts(p), [dmqn, dmqr, dmkn, dmv, dmkr], 4, [BF16] * 4,
        name=f"mla_prep_bwd_{l}")
    g['g_mla_q'] = dgq_n.reshape(-1)
    g['g_mla_kv'] = dgkv_n.reshape(-1)
    g['w_mla_uq'] = _uq_grad(dwq_n, dwq_r, dwq_sw)
    g['w_mla_ukv'] = _take(jnp.concatenate([dwk, dwv], axis=1), _UKV_INV)
    dsm = dzsm + jnp.pad(dff_t[:FOX_HEADS].T, [(0, 0), (0, 128 - FOX_HEADS)])
    dz = lax.dynamic_update_slice(dz, jnp.concatenate(
        [dgv, dgr, dfq.astype(BF16), dfk.astype(BF16), dfv.astype(BF16), dgq, dgk, dcq, dckv, dsm.astype(BF16), dkr, dkr_sw],
        axis=1), (0, Z_GV))
    dx0, dx0b, g['g_mix'] = _mm(dz, p['w_in'], mode='nt', out_dtype=F32, norm_bwd=(sv['x0'], p['g_mix'], dx1), tm=512,
                                tk=N_PAD // 2, name=f"d_in_{l}")
    g['w_in'] = _mm(sv['h1'], dz, mode='tn', out_dtype=BF16, tn=N_PAD // 3, name=f"d_w_in_{l}")
    for n in ('g_mlp', 'g_mem', 'g_xa', 'g_mix'):
        g[n] = g[n].reshape(-1)
    return dx0, dx0b, g


def _local_step(x, mem, target, ps, g_final, hooks, half_done, layer_done):
    rope = _rope_tables(x.shape[0])
    saved = []
    for l, p in enumerate(ps):
        x, sv = _layer_fwd(x, mem, p, rope, l, hooks)
        saved.append(sv)
    loss, dx, dxb, dgf = _loss_head(x, g_final, target, name="loss_head")
    for l in reversed(range(len(ps))):
        dx, dxb, grads = _layer_bwd(dx, dxb, mem, ps[l], rope, saved[l], l, hooks, half_done)
        layer_done(l, grads)
    assert not hooks, f"exchanges without a carrier: {list(hooks)}"
    return loss, dx, dgf.reshape(-1)


_MESH_AXES = ("x", "y", "c")
_HBM = pl.BlockSpec(memory_space=pl.ANY)


N_CHIP = 4


def _place():
    x, y, c = (lax.axis_index(n) for n in _MESH_AXES)
    return (x, y, c), (x, y, 1 - c), [(1 - x, y), (x, 1 - y), (1 - x, 1 - y)]


def _remote(src, dst, sems, k, to):
    return pltpu.make_async_remote_copy(src_ref=src, dst_ref=dst, send_sem=sems[0].at[k], recv_sem=sems[1].at[k],
                                        device_id=to, device_id_type=pl.DeviceIdType.MESH)


def _all_gather(x, *, name):
    def body(x_ref, o_ref, send_sems, recv_sems, local_sem):
        me, sib, chips = _place()
        c = me[2]
        sems = (send_sems, recv_sems)
        slot = lambda px, py, pc: o_ref.at[4 * px + 2 * py + pc]
        mine = pltpu.make_async_copy(x_ref, slot(*me), local_sem)
        mine.start()
        first = [_remote(x_ref, slot(*me), sems, 0, sib)]
        first += [_remote(x_ref, slot(*me), sems, 1 + j, (*chip, c)) for j, chip in enumerate(chips)]
        for cp in first:
            cp.start()
        passed = [_remote(slot(*chip, c), slot(*chip, c), sems, 4 + j, sib) for j, chip in enumerate(chips)]
        for j, chip in enumerate(chips):
            _remote(x_ref, slot(*chip, c), sems, 1 + j, me).wait_recv()
            passed[j].start()
        _remote(x_ref, slot(*sib), sems, 0, me).wait_recv()
        for j, chip in enumerate(chips):
            _remote(x_ref, slot(*chip, 1 - c), sems, 4 + j, me).wait_recv()
        for cp in first + passed:
            cp.wait_send()
        mine.wait()

    return pl.pallas_call(
        body, name=name, out_shape=jax.ShapeDtypeStruct((N_DEV,) + x.shape, x.dtype),
        in_specs=[_HBM], out_specs=_HBM,
        scratch_shapes=[pltpu.SemaphoreType.DMA((N_DEV - 1,)), pltpu.SemaphoreType.DMA((N_DEV - 1,)), pltpu.SemaphoreType.DMA],
        compiler_params=pltpu.CompilerParams(has_side_effects=True),
    )(x)


class _Rider:
    def __init__(self, inputs, out_shapes, scratch, start, finish, post):
        self.inputs, self.out_shapes, self.scratch = list(inputs), list(out_shapes), list(scratch)
        self.start, self.finish, self.post = start, finish, post


def _run_rider(rider, *, name):
    def body(*refs):
        rider.start(refs)
        rider.finish(refs)

    outs = pl.pallas_call(
        body, name=name, out_shape=tuple(rider.out_shapes), in_specs=[_HBM] * len(rider.inputs),
        out_specs=(_HBM,) * len(rider.out_shapes), scratch_shapes=rider.scratch,
        compiler_params=pltpu.CompilerParams(has_side_effects=True),
    )(*rider.inputs)
    return rider.post(outs)


def _carry(rider, n_in, n_out, first, last):
    if rider is None:
        return [], [], [], [], [], lambda refs: refs
    ni, no = len(rider.inputs), len(rider.out_shapes)

    def split(refs):
        own_in, r_in = refs[:n_in], refs[n_in:n_in + ni]
        own_out, r_out = refs[n_in + ni:n_in + ni + n_out], refs[n_in + ni + n_out:n_in + ni + n_out + no]
        rest = refs[n_in + ni + n_out + no:]
        own_scr, r_scr = rest[:len(rest) - len(rider.scratch)], rest[len(rest) - len(rider.scratch):]
        rrefs = tuple(r_in) + tuple(r_out) + tuple(r_scr)
        pl.when(first())(lambda: rider.start(rrefs))
        pl.when(last())(lambda: rider.finish(rrefs))
        return tuple(own_in) + tuple(own_out) + tuple(own_scr)

    return list(rider.inputs), [_HBM] * ni, list(rider.out_shapes), [_HBM] * no, list(rider.scratch), split


def _gather_rider(shards, axes):
    n = len(shards)
    srcs, out_shapes, kinds = [], [], []
    for s, ax in zip(shards, axes):
        L, a, b = s.shape
        if ax == 1:
            srcs.append(s.reshape(L, 1, a, b)), out_shapes.append((L, N_DEV, a, b)), kinds.append('row')
        elif b % 128 == 0:
            srcs.append(s), out_shapes.append((L, a, N_DEV * b)), kinds.append('col')
        else:
            srcs.append(s.reshape(1, L, a, b)), out_shapes.append((N_DEV, L, a, b)), kinds.append('slot')

    def parts(refs):
        x_refs, o_refs = refs[:n], refs[n:2 * n]
        send_sems, recv_sems, local_sem = refs[2 * n:]
        me, sib, chips = _place()
        sems = (send_sems, recv_sems)

        def win(t, px, py, pc):
            idx = 4 * px + 2 * py + pc
            if kinds[t] == 'row':
                return o_refs[t].at[:, pl.ds(idx, 1)]
            if kinds[t] == 'col':
                width = shards[t].shape[2]
                return o_refs[t].at[:, :, pl.ds(pl.multiple_of(idx * width, 128), width)]
            return o_refs[t].at[pl.ds(idx, 1)]

        def group(k, block, to, own):
            return [_remote(x_refs[t] if own else win(t, *block), win(t, *block), sems, k * n + t, to) for t in range(n)]

        mine = [pltpu.make_async_copy(x_refs[t], win(t, *me), local_sem.at[t]) for t in range(n)]
        first = group(0, me, sib, True)
        for j, chip in enumerate(chips):
            first += group(1 + j, me, (*chip, me[2]), True)
        return me, sib, chips, group, mine, first

    def start(refs):
        *_, mine, first = parts(refs)
        for cp in mine + first:
            cp.start()

    def finish(refs):
        me, sib, chips, group, mine, first = parts(refs)
        c = me[2]
        passed = []
        for j, chip in enumerate(chips):
            for cp in group(1 + j, (*chip, c), me, False):
                cp.wait_recv()
            fwd = group(4 + j, (*chip, c), sib, False)
            for cp in fwd:
                cp.start()
            passed += fwd
        for cp in group(0, sib, me, False):
            cp.wait_recv()
        for j, chip in enumerate(chips):
            for cp in group(4 + j, (*chip, 1 - c), me, False):
                cp.wait_recv()
        for cp in first + passed:
            cp.wait_send()
        for cp in mine:
            cp.wait()

    def post(outs):
        whole = []
        for o, s, kind in zip(outs, shards, kinds):
            L, a, b = s.shape
            whole.append(o.reshape(L, N_DEV * a, b) if kind == 'row' else o if kind == 'col' else _to_whole(o, 2))
        return whole

    return _Rider(srcs, [jax.ShapeDtypeStruct(shp, s.dtype) for shp, s in zip(out_shapes, shards)],
                  [pltpu.SemaphoreType.DMA(((N_DEV - 1) * n,)), pltpu.SemaphoreType.DMA(((N_DEV - 1) * n,)),
                   pltpu.SemaphoreType.DMA((n,))], start, finish, post)


def _sibling_swap(x, *, name):
    def body(x_ref, o_ref, send_sems, recv_sems):
        me, sib, _ = _place()
        c = me[2]
        sems = (send_sems, recv_sems)
        sends = [_remote(x_ref.at[j, 1 - c], o_ref.at[j], sems, j, sib) for j in range(N_CHIP)]
        for cp in sends:
            cp.start()
        for cp in sends:
            cp.wait_send()
            cp.wait_recv()

    return pl.pallas_call(
        body, name=name, out_shape=jax.ShapeDtypeStruct((N_CHIP,) + x.shape[2:], x.dtype),
        in_specs=[_HBM], out_specs=_HBM,
        scratch_shapes=[pltpu.SemaphoreType.DMA((N_CHIP,)), pltpu.SemaphoreType.DMA((N_CHIP,))],
        compiler_params=pltpu.CompilerParams(has_side_effects=True),
    )(x)


def _pair_sum(x, got, c, *, name):
    _, _, R, _ = x.shape
    tr = _pick(R, (1024, 512, 256, 128, 64, 32, 16, 8))

    def body(c_ref, x_ref, g_ref, o_ref):
        o_ref[...] = (x_ref[...].astype(F32) + g_ref[...].astype(F32)).astype(o_ref.dtype)

    return pl.pallas_call(
        body, name=name, out_shape=jax.ShapeDtypeStruct((N_CHIP, R, 128), x.dtype),
        grid_spec=pltpu.PrefetchScalarGridSpec(
            num_scalar_prefetch=1, grid=(N_CHIP, R // tr),
            in_specs=[pl.BlockSpec((None, None, tr, 128), lambda j, i, c_ref: (j, c_ref[0], i, 0)),
                      pl.BlockSpec((None, tr, 128), lambda j, i, c_ref: (j, i, 0))],
            out_specs=pl.BlockSpec((None, tr, 128), lambda j, i, c_ref: (j, i, 0))),
        compiler_params=_params(("parallel", "parallel")),
    )(c, x, got)


def _chip_all_to_all_rider(x):
    def parts(refs):
        x_ref, o_ref, send_sems, recv_sems, local_sem = refs
        me, _, chips = _place()
        sems = (send_sems, recv_sems)
        mine = 2 * me[0] + me[1]
        local = pltpu.make_async_copy(x_ref.at[mine], o_ref.at[mine], local_sem)
        sends = [_remote(x_ref.at[2 * px + py], o_ref.at[mine], sems, j, (px, py, me[2])) for j, (px, py) in enumerate(chips)]
        arrival = lambda j: _remote(x_ref.at[mine], o_ref.at[2 * chips[j][0] + chips[j][1]], sems, j, me)
        return local, sends, arrival

    def start(refs):
        local, sends, _ = parts(refs)
        for cp in [local] + sends:
            cp.start()

    def finish(refs):
        local, sends, arrival = parts(refs)
        for j, cp in enumerate(sends):
            cp.wait_send()
            arrival(j).wait_recv()
        local.wait()

    return _Rider([x], [jax.ShapeDtypeStruct(x.shape, x.dtype)],
                  [pltpu.SemaphoreType.DMA((N_CHIP - 1,)), pltpu.SemaphoreType.DMA((N_CHIP - 1,)), pltpu.SemaphoreType.DMA],
                  start, finish, lambda outs: outs[0])


def _sum_slots(x, *, name):
    n, R, _ = x.shape
    tr = _pick(R, (1024, 512, 256, 128, 64, 32, 16, 8))

    def body(x_ref, o_ref):
        acc = x_ref[0].astype(F32)
        for j in range(1, n):
            acc = acc + x_ref[j].astype(F32)
        o_ref[...] = acc

    return pl.pallas_call(
        body, name=name, out_shape=jax.ShapeDtypeStruct((R, 128), F32), grid=(R // tr,),
        in_specs=[pl.BlockSpec((n, tr, 128), lambda i: (0, i, 0))], out_specs=pl.BlockSpec((tr, 128), lambda i: (i, 0)),
        compiler_params=_params(("parallel",)),
    )(x)


def _adamw(w, g, m, v, *, name):
    shape = w.shape
    cols = shape[-1]
    rows = int(np.prod(shape[:-1]))
    tr = next((t for t in (1024, 512, 256, 128, 64, 32, 16, 8) if rows % t == 0 and t * cols * 4 <= (1 << 20)), rows)

    def body(w_ref, g_ref, m_ref, v_ref, d_ref, mo_ref, vo_ref):
        g_ = g_ref[...]
        m_ = ADAM_B1 * m_ref[...] + (1.0 - ADAM_B1) * g_
        v_ = ADAM_B2 * v_ref[...] + (1.0 - ADAM_B2) * jnp.square(g_)
        m_hat = m_ / (1.0 - ADAM_B1 ** ADAM_STEP)
        v_hat = v_ / (1.0 - ADAM_B2 ** ADAM_STEP)
        d_ref[...] = -ADAM_LR * (m_hat / (jnp.sqrt(v_hat) + ADAM_EPS) + ADAM_WD * w_ref[...])
        mo_ref[...] = m_
        vo_ref[...] = v_

    blk = pl.BlockSpec((tr, cols), lambda i: (i, 0))
    outs = pl.pallas_call(
        body, name=name, out_shape=tuple(jax.ShapeDtypeStruct((rows, cols), F32) for _ in range(3)), grid=(rows // tr,),
        in_specs=[blk] * 4, out_specs=(blk,) * 3, compiler_params=_params(("parallel",)),
    )(*(a.reshape(rows, cols) for a in (w, g, m, v)))
    return tuple(o.reshape(shape) for o in outs)


_WEIGHTS = ('g_mix', 'w_in', 'b_fox_forget', 'w_gla_gate', 'b_gla_gate', 'g_gla_out', 'g_mla_q', 'w_mla_uq', 'g_mla_kv',
            'w_mla_ukv', 'b_branch_gate', 'w_up_fox', 'w_up_gla', 'w_up_mla', 'w_out', 'g_xa', 'g_mem', 'w_xq', 'w_xkv',
            'w_xo', 'g_mlp', 'w_mlp1', 'w_mlp2', 'g_final')
_SHARDED = (('w_in', 1), ('w_gla_gate', 2), ('w_mla_uq', 2), ('w_mla_ukv', 2), ('w_up_fox', 2), ('w_up_gla', 2),
            ('w_up_mla', 2), ('w_out', 1), ('w_xq', 1), ('w_xkv', 1), ('w_xo', 2), ('w_mlp1', 2), ('w_mlp2', 1))
_REPLICATED = tuple(n for n in _WEIGHTS if n not in dict(_SHARDED))
_ROW_PAD = 1024
_SMALL_ROW_PAD = 8
_PIECE_ROWS = 16


def _pack(flats, lead, row_pad=_ROW_PAD):
    if all(int(np.prod(a.shape[lead:])) % 128 == 0 for a in flats):
        def block(a):
            a = a.reshape(a.shape[:lead] + (-1, 128))
            return jnp.pad(a, [(0, 0)] * lead + [(0, -a.shape[lead] % _PIECE_ROWS), (0, 0)])
        cat = jnp.concatenate([block(a) for a in flats], axis=lead)
        rows = cat.shape[lead]
        return jnp.pad(cat, [(0, 0)] * lead + [(0, -(-rows // row_pad) * row_pad - rows), (0, 0)])
    cat = jnp.concatenate([a.reshape(a.shape[:lead] + (-1,)) for a in flats], axis=-1)
    n = cat.shape[-1]
    total = -(-n // (128 * row_pad)) * (128 * row_pad)
    cat = jnp.pad(cat, [(0, 0)] * lead + [(0, total - n)])
    return cat.reshape(cat.shape[:lead] + (total // 128, 128))


def _unpack(buf, shapes, lead):
    sizes = [int(np.prod(shp)) for shp in shapes]
    out, off = [], 0
    if all(n % 128 == 0 for n in sizes):
        for shp, n in zip(shapes, sizes):
            rows = buf[(slice(None),) * lead + (slice(off, off + n // 128),)]
            out.append(rows.reshape(buf.shape[:lead] + tuple(shp)))
            off += -(-(n // 128) // _PIECE_ROWS) * _PIECE_ROWS
        return out
    flat = buf.reshape(buf.shape[:lead] + (-1,))
    for shp, n in zip(shapes, sizes):
        out.append(flat[..., off:off + n].reshape(buf.shape[:lead] + tuple(shp)))
        off += n
    return out


def _to_whole(g, axis):
    if axis == 1:
        return g.transpose(1, 0, 2, 3).reshape(g.shape[1], N_DEV * g.shape[2], g.shape[3])
    return g.transpose(1, 2, 0, 3).reshape(g.shape[1], g.shape[2], N_DEV * g.shape[3])


def _to_shards(w, axis):
    L, R, C = w.shape
    if axis == 1:
        return w.reshape(L, N_DEV, R // N_DEV, C).transpose(1, 0, 2, 3)
    return w.reshape(L, R, N_DEV, C // N_DEV).transpose(2, 0, 1, 3)


def kernel(x, mem, g_mix, w_in, b_fox_forget, w_gla_gate, b_gla_gate, g_gla_out, g_mla_q, w_mla_uq, g_mla_kv, w_mla_ukv, b_branch_gate, w_up_fox, w_up_gla, w_up_mla, w_out, g_xa, g_mem, w_xq, w_xkv, w_xo, g_mlp, w_mlp1, w_mlp2, g_final, loss_target, m_g_mix, m_w_in, m_b_fox_forget, m_w_gla_gate, m_b_gla_gate, m_g_gla_out, m_g_mla_q, m_w_mla_uq, m_g_mla_kv, m_w_mla_ukv, m_b_branch_gate, m_w_up_fox, m_w_up_gla, m_w_up_mla, m_w_out, m_g_xa, m_g_mem, m_w_xq, m_w_xkv, m_w_xo, m_g_mlp, m_w_mlp1, m_w_mlp2, m_g_final, v_g_mix, v_w_in, v_b_fox_forget, v_w_gla_gate, v_b_gla_gate, v_g_gla_out, v_g_mla_q, v_w_mla_uq, v_g_mla_kv, v_w_mla_ukv, v_b_branch_gate, v_w_up_fox, v_w_up_gla, v_w_up_mla, v_w_out, v_g_xa, v_g_mem, v_w_xq, v_w_xkv, v_w_xo, v_g_mlp, v_w_mlp1, v_w_mlp2, v_g_final):
    wts = dict(zip(_WEIGHTS, (g_mix, w_in, b_fox_forget, w_gla_gate, b_gla_gate, g_gla_out, g_mla_q, w_mla_uq, g_mla_kv,
                              w_mla_ukv, b_branch_gate, w_up_fox, w_up_gla, w_up_mla, w_out, g_xa, g_mem, w_xq, w_xkv, w_xo,
                              g_mlp, w_mlp1, w_mlp2, g_final)))
    mom1 = dict(zip(_WEIGHTS, (m_g_mix, m_w_in, m_b_fox_forget, m_w_gla_gate, m_b_gla_gate, m_g_gla_out, m_g_mla_q,
                               m_w_mla_uq, m_g_mla_kv, m_w_mla_ukv, m_b_branch_gate, m_w_up_fox, m_w_up_gla, m_w_up_mla,
                               m_w_out, m_g_xa, m_g_mem, m_w_xq, m_w_xkv, m_w_xo, m_g_mlp, m_w_mlp1, m_w_mlp2, m_g_final)))
    mom2 = dict(zip(_WEIGHTS, (v_g_mix, v_w_in, v_b_fox_forget, v_w_gla_gate, v_b_gla_gate, v_g_gla_out, v_g_mla_q,
                               v_w_mla_uq, v_g_mla_kv, v_w_mla_ukv, v_b_branch_gate, v_w_up_fox, v_w_up_gla, v_w_up_mla,
                               v_w_out, v_g_xa, v_g_mem, v_w_xq, v_w_xkv, v_w_xo, v_g_mlp, v_w_mlp1, v_w_mlp2, v_g_final)))
    depth = g_mix.shape[0]

    names = [n for n, _ in _SHARDED]
    axes = dict(_SHARDED)
    shard = {n: wts[n] for n in names}
    shard['w_in'] = _pad_w_in(w_in)
    rep = {n: wts[n] for n in _REPLICATED}
    ps = [_LayerParams(rep, l) for l in range(depth)]

    def gather(group, l):
        rider = _gather_rider([shard[n][l:l + 1].astype(BF16) for n in group], [axes[n] for n in group])
        return rider, lambda whole: ps[l].w.update({n: w[0] for n, w in zip(group, whole)})

    first, sink = gather(['w_in'], 0)
    sink(_run_rider(first, name="gather_w_in_0"))
    hooks = {(0, 'fox_fwd'): gather([n for n in names if n != 'w_in'], 0)}
    for l in range(1, depth):
        hooks[(l - 1, 'mla_fwd')] = gather(names, l)

    core = lax.axis_index("c").astype(jnp.int32).reshape(1)
    late = ['w_in', 'w_gla_gate', 'w_mla_uq', 'w_mla_ukv']
    groups = {'early': [n for n in names if n not in late], 'late': late}
    small_grads, landed = {}, {}

    def exchange(l, g, which):
        slots = _pack([(g[n][:, None] if g[n].ndim == 3 else _to_shards(g[n][None], axes[n])).astype(BF16)
                       for n in groups[which]], 1)
        slots = slots.reshape((N_CHIP, 2) + slots.shape[1:])
        paired = _pair_sum(slots, _sibling_swap(slots, name=f"swap_grads_{which}_{l}"), core, name=f"pair_grads_{which}_{l}")
        return _chip_all_to_all_rider(paired), lambda got: landed.update({(l, which): got})

    def half_done(l, g):
        hooks[(l, 'mla_bwd')] = exchange(l, g, 'early')

    def layer_done(l, g):
        small_grads[l] = g
        rider, sink = exchange(l, g, 'late')
        if l > 0:
            hooks[(l - 1, 'fox_bwd')] = (rider, sink)
        else:
            sink(_run_rider(rider, name=f"scatter_grads_late_{l}"))

    loss, dx, dg_final = _local_step(x[0], mem[0], loss_target[0], ps, g_final, hooks, half_done, layer_done)
    loss = lax.psum(loss[0, 0], _MESH_AXES)

    grad = {}
    for which, group in groups.items():
        shapes = [(1,) + shard[n].shape[1:] for n in group]
        per_layer = [_unpack(_sum_slots(landed[(l, which)], name=f"sum_grads_{which}_{l}"), shapes, 0) for l in range(depth)]
        grad.update({n: jnp.concatenate([per_layer[l][i] for l in range(depth)], axis=0) for i, n in enumerate(group)})
    grad['w_in'] = _unpad_w_in(grad['w_in'])
    grads = small_grads
    small = [dg_final if n == 'g_final' else jnp.stack([grads[l][n] for l in range(depth)]) for n in _REPLICATED]
    small_shapes = [wts[n].shape for n in _REPLICATED]
    small_sum = _sum_slots(_all_gather(_pack(small, 0, _SMALL_ROW_PAD), name="gather_small_grads"), name="sum_small_grads")
    grad.update(dict(zip(_REPLICATED, _unpack(small_sum, small_shapes, 0))))

    delta, new_m, new_v = {}, {}, {}
    for n, _ in _SHARDED:
        delta[n], new_m[n], new_v[n] = _adamw(wts[n], grad[n], mom1[n], mom2[n], name=f"adamw_{n}")
    packed = [_pack([d[n] for n in _REPLICATED], 0, _SMALL_ROW_PAD) for d in (wts, mom1, mom2)]
    outs = _adamw(packed[0], small_sum, packed[1], packed[2], name="adamw_small")
    for d, o in zip((delta, new_m, new_v), outs):
        d.update(dict(zip(_REPLICATED, _unpack(o, small_shapes, 0))))

    return (loss, dx[None], *[grad[n] for n in _WEIGHTS], *[delta[n] for n in _WEIGHTS],
            *[new_m[n] for n in _WEIGHTS], *[new_v[n] for n in _WEIGHTS])
```

```python
import functools

import jax
import jax.numpy as jnp
import numpy as np
from jax import lax
from jax.experimental import pallas as pl
from jax.experimental.pallas import tpu as pltpu

F32 = jnp.float32
BF16 = jnp.bfloat16

EPS = 1e-6
CHUNK = 64
FOX_HEADS, FOX_HD = 4, 64
GLA_HEADS, GLA_DK, GLA_DV, GLA_RANK, GLA_TAU = 4, 64, 128, 16, 16.0
MLA_HEADS, MLA_Q_RANK, MLA_KV_RANK, MLA_NOPE, MLA_ROPE, MLA_VD = 4, 256, 128, 64, 32, 64
ROPE_BASE = 10000.0
XA_HEADS, XA_HD = 4, 128
ADAM_LR, ADAM_B1, ADAM_B2, ADAM_EPS, ADAM_WD, ADAM_STEP = 0.001, 0.9, 0.999, 1e-08, 0.01, 10

N_DEV = 8
V7X_VMEM_LIMIT = 56 * 1024 * 1024
NEG = -1e30

NN = ((1,), (0,))
NT = ((1,), (1,))
TN = ((0,), (0,))


def _dot(a, b, dims):
    return lax.dot_general(a.astype(BF16), b.astype(BF16), (dims, ((), ())), preferred_element_type=F32)


@jax.custom_vjp
def bdot(a, b):
    return _dot(a, b, NN)


bdot.defvjp(lambda a, b: (_dot(a, b, NN), (a, b)),
            lambda res, g: (_dot(g, res[1], NT), _dot(res[0], g, TN)))


@jax.custom_vjp
def bdot_nt(a, b):
    return _dot(a, b, NT)


bdot_nt.defvjp(lambda a, b: (_dot(a, b, NT), (a, b)),
               lambda res, g: (_dot(g, res[1], NN), _dot(g, res[0], TN)))


@jax.custom_vjp
def bdot_tn(a, b):
    return _dot(a, b, TN)


bdot_tn.defvjp(lambda a, b: (_dot(a, b, TN), (a, b)),
               lambda res, g: (_dot(res[1], g, NT), _dot(res[0], g, NN)))


def _split2(x):
    hi = x.astype(BF16)
    lo = (x - hi.astype(F32)).astype(BF16)
    return hi, lo


def _tri(n, lower):
    r = lax.broadcasted_iota(jnp.int32, (n, n), 0)
    c = lax.broadcasted_iota(jnp.int32, (n, n), 1)
    return jnp.where((r >= c) if lower else (r <= c), 1.0, 0.0).astype(BF16)


def _log_sigmoid(x):
    return jnp.minimum(x, 0.0) - jnp.log(1.0 + jnp.exp(-jnp.abs(x)))


def _sigmoid(x):
    return 1.0 / (1.0 + jnp.exp(-x))


def _rms(x, g):
    return x * lax.rsqrt(jnp.mean(x * x, axis=-1, keepdims=True) + EPS) * g


def _pick(dim, prefs):
    for p in prefs:
        if dim % p == 0:
            return p
    return dim


def _params(sem):
    return pltpu.CompilerParams(dimension_semantics=sem, vmem_limit_bytes=V7X_VMEM_LIMIT)


def _rms_vjp(x, g, dy, dres):
    rstd = lax.rsqrt(jnp.mean(x * x, axis=-1, keepdims=True) + EPS)
    xh = x * rstd
    gdy = dy * g
    dx = (gdy - xh * jnp.mean(gdy * xh, axis=-1, keepdims=True)) * rstd
    return (dx if dres is None else dx + dres), jnp.sum(dy * xh, axis=0, keepdims=True)


def _mm(a, b, *, mode, out_dtype, name, act=None, residual=None, drelu_of=None, norm_bwd=None, b_cols=None,
        col_shards=None, rider=None, tm=None, tn=None, tk=None):
    b_off, b_width = b_cols or (0, b.shape[1])
    if mode == 'nn':
        (M, K), N = a.shape, b_width
    elif mode == 'nt':
        (M, K), N = a.shape, b.shape[0]
    else:
        (K, M), N = a.shape, b_width
    tm = tm or _pick(M, (1024, 512, 256, 128))
    tn = tn or _pick(N, (1024, 1920, 1152, 768, 640, 512, 384, 256, 128))
    tk = tk or _pick(K, (1024, 1920, 1152, 640, 512, 256, 128))
    nk = K // tk
    dims = {'nn': NN, 'nt': NT, 'tn': TN}[mode]
    a_spec = pl.BlockSpec((tk, tm), lambda i, j, k: (k, i)) if mode == 'tn' else pl.BlockSpec((tm, tk), lambda i, j, k: (i, k))
    if mode == 'nt':
        b_spec = pl.BlockSpec((tn, tk), lambda i, j, k, o=b_off // tk: (j, k + o))
    else:
        b_spec = pl.BlockSpec((tk, tn), lambda i, j, k, o=b_off // tn: (k, j + o))
    o_spec = pl.BlockSpec((tm, tn), lambda i, j, k: (i, j))
    extra = [e for e in (residual, drelu_of) if e is not None]
    extra_specs = [o_spec] * len(extra)
    out_shape, out_specs, n_out = jax.ShapeDtypeStruct((M, N), out_dtype), o_spec, 1
    if col_shards:
        n_sh = N // col_shards
        assert tn % n_sh == 0 and not extra and norm_bwd is None
        out_shape = jax.ShapeDtypeStruct((col_shards, M, n_sh), out_dtype)
        out_specs = pl.BlockSpec((tn // n_sh, tm, n_sh), lambda i, j, k: (j, i, 0))
    if norm_bwd is not None:
        x_in, g_in, dres_in = norm_bwd
        assert tn == N and residual is None and drelu_of is None
        vec = pl.BlockSpec((1, N), lambda i, j, k: (0, 0))
        extra, extra_specs = [x_in, g_in.reshape(1, N), dres_in], [o_spec, vec, o_spec]
        out_shape = (jax.ShapeDtypeStruct((M, N), F32), jax.ShapeDtypeStruct((M, N), BF16), jax.ShapeDtypeStruct((1, N), F32))
        out_specs, n_out = (o_spec, o_spec, vec), 3

    grid = (M // tm, N // tn, nk)
    r_ins, r_in_specs, r_outs, r_out_specs, r_scratch, split = _carry(
        rider, 2 + len(extra), n_out, lambda: functools.reduce(jnp.logical_and, [pl.program_id(d) == 0 for d in range(3)]),
        lambda: functools.reduce(jnp.logical_and, [pl.program_id(d) == grid[d] - 1 for d in range(3)]))
    assert rider is None or n_out == 1

    def body(*refs):
        a_ref, b_ref, *rest = split(refs)
        o_ref = rest[len(extra)]
        first_rows = pl.program_id(0) == 0
        at = a_ref[...]
        if act == 'relu2':
            at = jnp.square(jnp.maximum(at.astype(F32), 0.0))
        part = _dot(at, b_ref[...], dims)

        def finish(acc):
            if norm_bwd is not None:
                dx, dg = _rms_vjp(rest[0][...], rest[1][...], acc, rest[2][...])
                o_ref[...] = dx
                rest[len(extra) + 1][...] = dx.astype(BF16)
                dg_ref = rest[len(extra) + 2]

                @pl.when(first_rows)
                def _():
                    dg_ref[...] = dg

                @pl.when(jnp.logical_not(first_rows))
                def _():
                    dg_ref[...] += dg
                return
            idx = 0
            if residual is not None:
                acc = acc + rest[idx][...]
                idx += 1
            if drelu_of is not None:
                acc = acc * (2.0 * jnp.maximum(rest[idx][...].astype(F32), 0.0))
            if col_shards:
                for t in range(tn // n_sh):
                    o_ref[t] = acc[:, t * n_sh:(t + 1) * n_sh].astype(out_dtype)
            else:
                o_ref[...] = acc.astype(out_dtype)

        if nk == 1:
            finish(part)
        else:
            acc_ref = rest[len(extra) + n_out]
            k = pl.program_id(2)

            @pl.when(k == 0)
            def _():
                acc_ref[...] = part

            @pl.when(k > 0)
            def _():
                acc_ref[...] += part

            @pl.when(k == nk - 1)
            def _():
                finish(acc_ref[...])

    scratch = [] if nk == 1 else [pltpu.VMEM((tm, tn), F32)]
    if rider is not None:
        res = pl.pallas_call(
            body, name=name, out_shape=(out_shape, *r_outs), grid=grid, in_specs=[a_spec, b_spec] + extra_specs + r_in_specs,
            out_specs=(out_specs, *r_out_specs), scratch_shapes=scratch + r_scratch,
            compiler_params=_params(("arbitrary", "arbitrary", "arbitrary")),
        )(a, b, *extra, *r_ins)
        return res[0], rider.post(res[1:])
    return pl.pallas_call(
        body, name=name, out_shape=out_shape, grid=grid, in_specs=[a_spec, b_spec] + extra_specs, out_specs=out_specs,
        scratch_shapes=scratch,
        compiler_params=_params(("arbitrary" if norm_bwd is not None else "parallel", "parallel", "arbitrary")),
    )(a, b, *extra)


def _rms_fwd(x, g, *, name, out_dtype=BF16):
    S, D = x.shape
    tr = _pick(S, (512, 256, 128))

    def body(x_ref, g_ref, o_ref):
        o_ref[...] = _rms(x_ref[...], g_ref[...]).astype(out_dtype)

    return pl.pallas_call(
        body, name=name, out_shape=jax.ShapeDtypeStruct((S, D), out_dtype), grid=(S // tr,),
        in_specs=[pl.BlockSpec((tr, D), lambda i: (i, 0)), pl.BlockSpec((1, D), lambda i: (0, 0))],
        out_specs=pl.BlockSpec((tr, D), lambda i: (i, 0)),
        compiler_params=_params(("parallel",)),
    )(x, g.reshape(1, D))


def _rms_bwd(x, g, dy, dres, *, name):
    S, D = x.shape
    tr = _pick(S, (512, 256, 128))

    def body(x_ref, g_ref, dy_ref, *rest):
        dx_ref, dxb_ref, dg_ref = rest[-3], rest[-2], rest[-1]
        dx, part = _rms_vjp(x_ref[...], g_ref[...], dy_ref[...].astype(F32), None if dres is None else rest[0][...])
        dx_ref[...] = dx
        dxb_ref[...] = dx.astype(BF16)

        @pl.when(pl.program_id(0) == 0)
        def _():
            dg_ref[...] = part

        @pl.when(pl.program_id(0) > 0)
        def _():
            dg_ref[...] += part

    row = pl.BlockSpec((tr, D), lambda i: (i, 0))
    vec = pl.BlockSpec((1, D), lambda i: (0, 0))
    ins = [x, g.reshape(1, D), dy] + ([dres] if dres is not None else [])
    return pl.pallas_call(
        body, name=name,
        out_shape=(jax.ShapeDtypeStruct((S, D), F32), jax.ShapeDtypeStruct((S, D), BF16), jax.ShapeDtypeStruct((1, D), F32)),
        grid=(S // tr,),
        in_specs=[row, vec, row] + ([row] if dres is not None else []),
        out_specs=(row, row, vec),
        compiler_params=_params(("arbitrary",)),
    )(*ins)


def _loss_head(x, g, target, *, name):
    S, D = x.shape
    tr = _pick(S, (512, 256, 128))

    def body(x_ref, g_ref, t_ref, l_ref, dx_ref, dxb_ref, dg_ref):
        x_ = x_ref[...]
        g_ = g_ref[...]
        rstd = lax.rsqrt(jnp.mean(x_ * x_, axis=-1, keepdims=True) + EPS)
        xh = x_ * rstd
        err = xh * g_ - t_ref[...]
        lpart = (0.5 / D) * jnp.sum(jnp.sum(err * err, axis=-1, keepdims=True), axis=0, keepdims=True)
        dy = err * (1.0 / D)
        gdy = dy * g_
        dx = (gdy - xh * jnp.mean(gdy * xh, axis=-1, keepdims=True)) * rstd
        dx_ref[...] = dx
        dxb_ref[...] = dx.astype(BF16)
        gpart = jnp.sum(dy * xh, axis=0, keepdims=True)

        @pl.when(pl.program_id(0) == 0)
        def _():
            dg_ref[...] = gpart
            l_ref[...] = lpart

        @pl.when(pl.program_id(0) > 0)
        def _():
            dg_ref[...] += gpart
            l_ref[...] += lpart

    row = pl.BlockSpec((tr, D), lambda i: (i, 0))
    vec = pl.BlockSpec((1, D), lambda i: (0, 0))
    return pl.pallas_call(
        body, name=name,
        out_shape=(jax.ShapeDtypeStruct((1, 1), F32), jax.ShapeDtypeStruct((S, D), F32), jax.ShapeDtypeStruct((S, D), BF16),
                   jax.ShapeDtypeStruct((1, D), F32)),
        grid=(S // tr,),
        in_specs=[row, vec, row],
        out_specs=(pl.BlockSpec((1, 1), lambda i: (0, 0)), row, row, vec),
        compiler_params=_params(("arbitrary",)),
    )(x, g.reshape(1, D), target)


def _mask_of(mask, tq, tk, keys_first=False):
    shape, q_axis = ((tk, tq), 1) if keys_first else ((tq, tk), 0)
    qpos = lax.broadcasted_iota(jnp.int32, shape, q_axis)
    kpos = lax.broadcasted_iota(jnp.int32, shape, 1 - q_axis)
    if mask == 'causal':
        return kpos <= qpos
    return kpos <= (qpos | (CHUNK - 1))


LANES = 128
LOG2E = 1.4426950408889634


def _lane_group(j, w, width):
    lane = lax.broadcasted_iota(jnp.int32, (1, width), 1)
    return (lane >= j * w) & (lane < (j + 1) * w)


def _only(x, j, w):
    if w == x.shape[1]:
        return x
    return jnp.where(_lane_group(j, w, x.shape[1]), x, jnp.zeros_like(x))


def _per_head(cols, w):
    out = cols[-1]
    for j in range(len(cols) - 2, -1, -1):
        out = jnp.where(_lane_group(j, w, LANES), cols[j], out)
    return out


def _side_by_side(xs):
    return xs[0] if len(xs) == 1 else jnp.concatenate(xs, axis=1)


def _on_top(xs):
    return xs[0] if len(xs) == 1 else jnp.concatenate(xs, axis=0)


def _stacked(x, hp, w):
    return _on_top([_only(x, j, w) for j in range(hp)])


def _col_block(entry, rows, idx):
    arr, off, width = entry
    return pl.BlockSpec((rows, width), lambda i, j, o=off // width: (idx(i, j), o))


def _attn_fwd(qk, v, H, cq, ck, *, scale, mask, name, rider=None):
    Sq, Sk = qk[0][0][0].shape[0], v[0].shape[0]
    dv = v[2] // H
    w0 = qk[0][2]
    hp = LANES // w0
    G = H // hp
    assert dv == w0 and not qk[0][3] and all(sh and H * w == LANES for _, _, w, sh in qk[1:])
    tq = _pick(Sq, (512, 256, 128))
    tk = tq if mask else _pick(Sk, (512, 256, 128))
    nq, nk = Sq // tq, Sk // tk
    bias = cq is not None
    npart = len(qk)

    def body(*refs):
        refs = split(refs)
        q_refs, k_refs = refs[0:2 * npart:2], refs[1:2 * npart:2]
        v_ref = refs[2 * npart]
        cq_ref, ck_ref = (refs[2 * npart + 1], refs[2 * npart + 2]) if bias else (None, None)
        o_ref, lse_ref, m_s, l_s, acc_s = refs[-5:]
        qi, ki = pl.program_id(0), pl.program_id(1)

        @pl.when(ki == 0)
        def _():
            m_s[...] = jnp.full(m_s.shape, NEG, F32)
            l_s[...] = jnp.zeros(l_s.shape, F32)
            acc_s[...] = jnp.zeros(acc_s.shape, F32)

        def rows_of(vals):
            return _on_top([jnp.broadcast_to(r, (w0, tq)) for r in vals])

        def compute(masked):
            keep = _mask_of(mask, tq, tk, keys_first=True) if masked else None
            for g in range(G):
                lanes = slice(g * LANES, (g + 1) * LANES)
                q128, k128, v128 = q_refs[0][:, lanes], k_refs[0][:, lanes], v_ref[:, lanes]
                ps, alphas = [], []
                extras = list(zip(qk, q_refs, k_refs))[1:]
                k_all = _side_by_side([k128] + [k_ref[...] for _, _, k_ref in extras])
                for j in range(hp):
                    h = g * hp + j
                    q_all = _side_by_side([_only(q128, j, w0)] + [_only(q_ref[...], h, w) for (_, _, w, _), q_ref, _ in extras])
                    s = _dot(k_all, q_all, NT) * scale
                    if bias:
                        s = s + (cq_ref[h:h + 1, :] - ck_ref[:, h:h + 1])
                    if masked:
                        s = jnp.where(keep, s, NEG)
                    m_prev = m_s[h:h + 1, :]
                    m_new = jnp.maximum(m_prev, jnp.max(s, axis=0, keepdims=True))
                    alpha = jnp.exp(m_prev - m_new)
                    p = jnp.exp(s - m_new)
                    l_s[h:h + 1, :] = alpha * l_s[h:h + 1, :] + jnp.sum(p, axis=0, keepdims=True)
                    m_s[h:h + 1, :] = m_new
                    ps.append(p.astype(BF16))
                    alphas.append(alpha)
                acc_s[g] = rows_of(alphas) * acc_s[g] + _dot(_stacked(v128, hp, w0), _on_top(ps), TN)

        if mask is None:
            compute(False)
        else:
            pl.when(ki < qi)(lambda: compute(False))
            pl.when(ki == qi)(lambda: compute(True))

        @pl.when(ki == ((nk - 1) if mask is None else qi))
        def _():
            for g in range(G):
                norm = acc_s[g] / rows_of([l_s[g * hp + j:g * hp + j + 1, :] for j in range(hp)])
                o_ref[:, g * LANES:(g + 1) * LANES] = norm.T.astype(BF16)
            lse_ref[...] = jnp.zeros(lse_ref.shape, F32)
            lse_ref[0:H, :] = m_s[0:H, :] + jnp.log(l_s[0:H, :])

    q_idx = lambda i, j: i
    k_idx = (lambda i, j: jnp.minimum(i, j)) if mask else (lambda i, j: j)
    ins, in_specs = [], []
    for q_e, k_e, _, _ in qk:
        ins += [q_e[0], k_e[0]]
        in_specs += [_col_block(q_e, tq, q_idx), _col_block(k_e, tk, k_idx)]
    ins.append(v[0])
    in_specs.append(_col_block(v, tk, k_idx))
    if bias:
        in_specs += [pl.BlockSpec((8, tq), lambda i, j: (0, i)), pl.BlockSpec((tk, 8), lambda i, j: (k_idx(i, j), 0))]
        ins += [cq, ck]
    r_ins, r_in_specs, r_outs, r_out_specs, r_scratch, split = _carry(
        rider, len(ins), 2, lambda: (pl.program_id(0) == 0) & (pl.program_id(1) == 0),
        lambda: (pl.program_id(0) == nq - 1) & (pl.program_id(1) == nk - 1))
    res = pl.pallas_call(
        body, name=name,
        out_shape=(jax.ShapeDtypeStruct((Sq, H * dv), BF16), jax.ShapeDtypeStruct((8, Sq), F32), *r_outs),
        grid=(nq, nk), in_specs=in_specs + r_in_specs,
        out_specs=(pl.BlockSpec((tq, H * dv), lambda i, j: (i, 0)), pl.BlockSpec((8, tq), lambda i, j: (0, i)), *r_out_specs),
        scratch_shapes=[pltpu.VMEM((8, tq), F32), pltpu.VMEM((8, tq), F32), pltpu.VMEM((G, LANES, tq), F32)] + r_scratch,
        compiler_params=_params(("arbitrary", "arbitrary")) if rider else _params(("parallel", "arbitrary")),
    )(*ins, *r_ins)
    return (res[0], res[1], rider.post(res[2:])) if rider else res


def _attn_bwd(qk, v, H, o, do, lse, cq, ck, *, scale, mask, name, rider=None):
    Sq, Sk = qk[0][0][0].shape[0], v[0].shape[0]
    dv = v[2] // H
    w0 = qk[0][2]
    hp = LANES // w0
    G = H // hp
    tq = _pick(Sq, (512, 256, 128))
    tk = tq if mask else _pick(Sk, (512, 256, 128))
    nq, nk = Sq // tq, Sk // tk
    bias = cq is not None
    npart = len(qk)
    n_in = 2 * npart + 4 + (2 if bias else 0)

    def body(*refs):
        refs = split(refs)
        q_refs, k_refs = refs[0:2 * npart:2], refs[1:2 * npart:2]
        v_ref, o_ref, do_ref, lse_ref = refs[2 * npart:2 * npart + 4]
        cq_ref, ck_ref = (refs[2 * npart + 4], refs[2 * npart + 5]) if bias else (None, None)
        outs = refs[n_in:]
        dq_refs, dk_refs, dv_ref = outs[:npart], outs[npart:2 * npart], outs[2 * npart]
        dck_ref, dcq_ref = (outs[2 * npart + 1], outs[2 * npart + 2]) if bias else (None, None)
        dk_accs, dv_acc = refs[-(npart + 1):-1], refs[-1]
        ki, qi = pl.program_id(0), pl.program_id(1)
        first_q = ki if mask else 0

        @pl.when((ki == 0) & (qi == 0))
        def _():
            for r in dq_refs:
                r[...] = jnp.zeros(r.shape, F32)
            if bias:
                dcq_ref[...] = jnp.zeros(dcq_ref.shape, F32)

        @pl.when(qi == first_q)
        def _():
            for r in dk_accs:
                r[...] = jnp.zeros(r.shape, F32)
            dv_acc[...] = jnp.zeros(dv_acc.shape, F32)
            if bias:
                dck_ref[...] = jnp.zeros(dck_ref.shape, F32)

        def compute(masked):
            keep = _mask_of(mask, tq, tk, keys_first=True) if masked else None
            rows = pl.ds(pl.multiple_of(qi * tq, tq), tq)
            extras = list(zip(qk, q_refs, k_refs, dq_refs, dk_accs))[1:]
            for g in range(G):
                lanes = slice(g * LANES, (g + 1) * LANES)
                q128, k128, v128 = q_refs[0][:, lanes], k_refs[0][:, lanes], v_ref[:, lanes]
                do128, o128 = do_ref[:, lanes], o_ref[:, lanes]
                prod = do128.astype(F32) * o128.astype(F32)
                ps, dss = [], []
                k_all = _side_by_side([k128] + [e[2][...] for e in extras])
                for j in range(hp):
                    h = g * hp + j
                    q_all = _side_by_side([_only(q128, j, w0)] + [_only(e[1][...], h, e[0][2]) for e in extras])
                    s = _dot(k_all, q_all, NT) * (scale * LOG2E)
                    if bias:
                        s = s - ck_ref[:, h:h + 1] * LOG2E
                    if masked:
                        s = jnp.where(keep, s, NEG)
                    row = lse_ref[h:h + 1, :] - cq_ref[h:h + 1, :] if bias else lse_ref[h:h + 1, :]
                    p = jnp.exp2(s - row * LOG2E)
                    dp = _dot(v128, _only(do128, j, w0), NT)
                    delta = jnp.sum(_only(prod, j, w0), axis=1, keepdims=True).T
                    ds = p * (dp - delta)
                    if bias:
                        dck_ref[:, h:h + 1] -= jnp.sum(ds, axis=1, keepdims=True)
                        dcq_ref[h:h + 1, rows] += jnp.sum(ds, axis=0, keepdims=True)
                    ps.append(p.astype(BF16))
                    dss.append((ds * scale).astype(BF16))
                for (_, _, w, _), q_ref, k_ref, dq_ref, dk_acc in extras:
                    heads = range(g * hp, (g + 1) * hp)
                    dk_acc[...] += _dot(_side_by_side(dss), _on_top([_only(q_ref[...], h, w) for h in heads]), NN)
                    dq_ref[rows, :] += _dot(_on_top(dss), _on_top([_only(k_ref[...], h, w) for h in heads]), TN)
                dv_acc[:, lanes] += _dot(_side_by_side(ps), _stacked(do128, hp, w0), NN)
                dk_accs[0][:, lanes] += _dot(_side_by_side(dss), _stacked(q128, hp, w0), NN)
                dq_refs[0][rows, lanes] += _dot(_on_top(dss), _stacked(k128, hp, w0), TN)

        if mask is None:
            compute(False)
        else:
            pl.when(qi > ki)(lambda: compute(False))
            pl.when(qi == ki)(lambda: compute(True))

        @pl.when(qi == nq - 1)
        def _():
            for r, acc in zip(dk_refs, dk_accs):
                r[...] = acc[...]
            dv_ref[...] = dv_acc[...]

    q_idx = (lambda j, i: jnp.maximum(i, j)) if mask else (lambda j, i: i)
    k_idx = lambda j, i: j
    ins, in_specs, dq_shapes, dq_specs, dk_shapes, dk_specs, scratch = [], [], [], [], [], [], []
    for q_e, k_e, w, shared in qk:
        ins += [q_e[0], k_e[0]]
        in_specs += [_col_block(q_e, tq, q_idx), _col_block(k_e, tk, k_idx)]
        dq_shapes.append(jax.ShapeDtypeStruct((Sq, H * w), F32))
        dq_specs.append(pl.BlockSpec((Sq, H * w), lambda j, i: (0, 0)))
        kw = k_e[2]
        dk_shapes.append(jax.ShapeDtypeStruct((Sk, kw), F32))
        dk_specs.append(pl.BlockSpec((tk, kw), lambda j, i: (j, 0)))
        scratch.append(pltpu.VMEM((tk, kw), F32))
    row_q = lambda width: pl.BlockSpec((tq, width), lambda j, i: (q_idx(j, i), 0))
    per_q = pl.BlockSpec((8, tq), lambda j, i: (0, q_idx(j, i)))
    ins += [v[0], o, do, lse]
    in_specs += [_col_block(v, tk, k_idx), row_q(H * dv), row_q(H * dv), per_q]
    out_shape = dq_shapes + dk_shapes + [jax.ShapeDtypeStruct((Sk, H * dv), F32)]
    out_specs = dq_specs + dk_specs + [pl.BlockSpec((tk, H * dv), lambda j, i: (j, 0))]
    if bias:
        in_specs += [per_q, pl.BlockSpec((tk, 8), lambda j, i: (j, 0))]
        ins += [cq, ck]
        out_shape += [jax.ShapeDtypeStruct((Sk, 8), F32), jax.ShapeDtypeStruct((8, Sq), F32)]
        out_specs += [pl.BlockSpec((tk, 8), lambda j, i: (j, 0)), pl.BlockSpec((8, Sq), lambda j, i: (0, 0))]
    scratch.append(pltpu.VMEM((tk, H * dv), F32))
    n_out = len(out_shape)
    r_ins, r_in_specs, r_outs, r_out_specs, r_scratch, split = _carry(
        rider, len(ins), n_out, lambda: (pl.program_id(0) == 0) & (pl.program_id(1) == 0),
        lambda: (pl.program_id(0) == nk - 1) & (pl.program_id(1) == nq - 1))
    res = pl.pallas_call(
        body, name=name, out_shape=tuple(out_shape + r_outs), grid=(nk, nq), in_specs=in_specs + r_in_specs,
        out_specs=tuple(out_specs + r_out_specs), scratch_shapes=scratch + r_scratch,
        compiler_params=_params(("arbitrary", "arbitrary")),
    )(*ins, *r_ins)
    own = (list(res[:npart]), list(res[npart:2 * npart]), res[2 * npart]) + tuple(res[2 * npart + 1:n_out])
    return own + (rider.post(res[n_out:]),) if rider else own


def _split3_dot(x, t):
    hi = x.astype(BF16)
    r1 = x - hi.astype(F32)
    mid = r1.astype(BF16)
    lo = (r1 - mid.astype(F32)).astype(BF16)
    return _dot(hi, t, NN) + _dot(mid, t, NN) + _dot(lo, t, NN)


def _fox_cum_fwd(ff_t, b, *, name):
    _, S = ff_t.shape
    tb = _pick(S, (512, 256, 128))

    def body(f_ref, b_ref, o_ref, carry):
        @pl.when(pl.program_id(0) == 0)
        def _():
            carry[...] = jnp.zeros(carry.shape, F32)

        lf = _log_sigmoid(f_ref[...] + b_ref[...])
        o_ref[...] = _split3_dot(lf, _tri(tb, False)) + carry[...]
        carry[...] += jnp.sum(lf, axis=1, keepdims=True)

    return pl.pallas_call(
        body, name=name, out_shape=jax.ShapeDtypeStruct((8, S), F32), grid=(S // tb,),
        in_specs=[pl.BlockSpec((8, tb), lambda i: (0, i)), pl.BlockSpec((8, 1), lambda i: (0, 0))],
        out_specs=pl.BlockSpec((8, tb), lambda i: (0, i)),
        scratch_shapes=[pltpu.VMEM((8, 1), F32)],
        compiler_params=_params(("arbitrary",)),
    )(ff_t, b)


def _fox_cum_bwd(ff_t, b, dcum_t, *, name):
    _, S = ff_t.shape
    tb = _pick(S, (512, 256, 128))
    nb = S // tb

    def body(f_ref, b_ref, dc_ref, df_ref, db_ref, carry):
        @pl.when(pl.program_id(0) == 0)
        def _():
            carry[...] = jnp.zeros(carry.shape, F32)
            db_ref[...] = jnp.zeros(db_ref.shape, F32)

        dc = dc_ref[...]
        dlf = _split3_dot(dc, _tri(tb, True)) + carry[...]
        carry[...] += jnp.sum(dc, axis=1, keepdims=True)
        df = dlf * _sigmoid(-(f_ref[...] + b_ref[...]))
        df_ref[...] = df
        db_ref[...] += jnp.sum(df, axis=1, keepdims=True)

    rev = lambda i: (0, nb - 1 - i)
    return pl.pallas_call(
        body, name=name,
        out_shape=(jax.ShapeDtypeStruct((8, S), F32), jax.ShapeDtypeStruct((8, 1), F32)), grid=(nb,),
        in_specs=[pl.BlockSpec((8, tb), rev), pl.BlockSpec((8, 1), lambda i: (0, 0)), pl.BlockSpec((8, tb), rev)],
        out_specs=(pl.BlockSpec((8, tb), rev), pl.BlockSpec((8, 1), lambda i: (0, 0))),
        scratch_shapes=[pltpu.VMEM((8, 1), F32)],
        compiler_params=_params(("arbitrary",)),
    )(ff_t, b, dcum_t)


GLA_W = GLA_HEADS * GLA_DK
GLA_BLOCK_CHUNKS = 4


def _same_chunk(n, lower):
    r = lax.broadcasted_iota(jnp.int32, (n, n), 0)
    c = lax.broadcasted_iota(jnp.int32, (n, n), 1)
    same = (r | (CHUNK - 1)) == (c | (CHUNK - 1))
    return jnp.where(same & (r >= c) if lower else same, 1.0, 0.0).astype(BF16)


def _chunk_mix(x, t, transpose):
    hi, lo = _split2(x)
    dims = TN if transpose else NN
    return _dot(t, hi, dims) + _dot(t, lo, dims)


@jax.custom_vjp
def chunk_cumsum(x):
    return _chunk_mix(x, _same_chunk(x.shape[0], True), False)


chunk_cumsum.defvjp(lambda x: (chunk_cumsum(x), None), lambda _, g: (_chunk_mix(g, _same_chunk(g.shape[0], True), True),))


@jax.custom_vjp
def chunk_total(x):
    return _chunk_mix(x, _same_chunk(x.shape[0], False), False)


chunk_total.defvjp(lambda x: (chunk_total(x), None), lambda _, g: (_chunk_mix(g, _same_chunk(g.shape[0], False), False),))


def _gla_block(q, k, zsm, wg, bg, go, vs, rs, states):
    n_chunks = q.shape[0] // CHUNK
    la = _log_sigmoid(bdot(zsm, wg) + bg) * (1.0 / GLA_TAU)
    end = chunk_total(la)
    kd = k * jnp.exp(end - chunk_cumsum(la))
    a = jnp.exp(end)
    qs = q * (GLA_DK ** -0.5)
    lane = lax.broadcasted_iota(jnp.int32, (1, GLA_W), 1)
    outs, new_states = [], []
    for h in range(GLA_HEADS):
        kdh = kd * jnp.where((lane >= h * GLA_DK) & (lane < (h + 1) * GLA_DK), 1.0, 0.0)
        st, o = states[h], []
        for c in range(n_chunks):
            rows = slice(c * CHUNK, (c + 1) * CHUNK)
            st = st * a[c * CHUNK:c * CHUNK + 1] + bdot_tn(vs[h][rows], kdh[rows])
            o.append(bdot_nt(qs[rows], st))
        o = _rms(jnp.concatenate(o, axis=0), go)
        outs.append(o * (rs[h] * _sigmoid(rs[h])))
        new_states.append(st)
    return outs, new_states


def _gla_fwd(z, zsm, wg, bg, go, cols, *, name):
    S = z.shape[0]
    rb = GLA_BLOCK_CHUNKS * CHUNK
    nb = S // rb
    cq, ckk, cv, cr = cols
    H = GLA_HEADS

    def body(q_ref, k_ref, zsm_ref, wg_ref, bg_ref, go_ref, *rest):
        v_refs, r_refs = rest[:H], rest[H:2 * H]
        o_ref, st_ref, state = rest[2 * H], rest[2 * H + 1], rest[2 * H + 2]

        @pl.when(pl.program_id(0) == 0)
        def _():
            state[...] = jnp.zeros(state.shape, F32)

        states = [state[h] for h in range(H)]
        for h in range(H):
            st_ref[0, h] = states[h]
        outs, new_states = _gla_block(
            q_ref[...].astype(F32), k_ref[...].astype(F32), zsm_ref[...], wg_ref[...], bg_ref[...], go_ref[...],
            [v_refs[h][...].astype(F32) for h in range(H)], [r_refs[h][...].astype(F32) for h in range(H)], states)
        for h in range(H):
            o_ref[:, h * GLA_DV:(h + 1) * GLA_DV] = outs[h].astype(BF16)
            state[h] = new_states[h]

    def col(width, off):
        return pl.BlockSpec((rb, width), lambda i, o=off // width: (i, o))

    full = lambda shp: pl.BlockSpec(shp, lambda i: (0,) * len(shp))
    in_specs = [col(GLA_W, cq), col(GLA_W, ckk), pl.BlockSpec((rb, 128), lambda i: (i, 0)),
                full((128, GLA_W)), full((1, GLA_W)), full((1, GLA_DV))]
    in_specs += [col(GLA_DV, cv + h * GLA_DV) for h in range(H)] + [col(GLA_DV, cr + h * GLA_DV) for h in range(H)]
    return pl.pallas_call(
        body, name=name,
        out_shape=(jax.ShapeDtypeStruct((S, H * GLA_DV), BF16), jax.ShapeDtypeStruct((nb, H, GLA_DV, GLA_W), F32)),
        grid=(nb,), in_specs=in_specs,
        out_specs=(pl.BlockSpec((rb, H * GLA_DV), lambda i: (i, 0)),
                   pl.BlockSpec((1, H, GLA_DV, GLA_W), lambda i: (i, 0, 0, 0))),
        scratch_shapes=[pltpu.VMEM((H, GLA_DV, GLA_W), F32)],
        compiler_params=_params(("arbitrary",)),
    )(z, z, zsm, wg, bg, go, *([z] * (2 * H)))


def _gla_bwd(z, zsm, wg, bg, go, states, do, cols, *, name):
    S = z.shape[0]
    rb = GLA_BLOCK_CHUNKS * CHUNK
    nb = S // rb
    cq, ckk, cv, cr = cols
    H = GLA_HEADS

    def body(q_ref, k_ref, zsm_ref, wg_ref, bg_ref, go_ref, st_ref, do_ref, *rest):
        v_refs, r_refs = rest[:H], rest[H:2 * H]
        dq_ref, dk_ref, dv_ref, dr_ref, dzsm_ref, dwg_ref, dbg_ref, dgo_ref, dstate = rest[2 * H:]

        @pl.when(pl.program_id(0) == 0)
        def _():
            dstate[...] = jnp.zeros(dstate.shape, F32)
            dwg_ref[...] = jnp.zeros(dwg_ref.shape, F32)
            dbg_ref[...] = jnp.zeros(dbg_ref.shape, F32)
            dgo_ref[...] = jnp.zeros(dgo_ref.shape, F32)

        prim = (q_ref[...].astype(F32), k_ref[...].astype(F32), zsm_ref[...], wg_ref[...], bg_ref[...], go_ref[...],
                [v_refs[h][...].astype(F32) for h in range(H)], [r_refs[h][...].astype(F32) for h in range(H)],
                [st_ref[0, h] for h in range(H)])
        _, vjp = jax.vjp(_gla_block, *prim)
        douts = [do_ref[:, h * GLA_DV:(h + 1) * GLA_DV].astype(F32) for h in range(H)]
        dq, dk, dzs, dwg, dbg, dgo, dvs, drs, dsts = vjp((douts, [dstate[h] for h in range(H)]))
        dq_ref[...] = dq.astype(BF16)
        dk_ref[...] = dk.astype(BF16)
        dzsm_ref[...] = dzs
        dwg_ref[...] += dwg
        dbg_ref[...] += dbg
        dgo_ref[...] += dgo
        for h in range(H):
            dv_ref[:, h * GLA_DV:(h + 1) * GLA_DV] = dvs[h].astype(BF16)
            dr_ref[:, h * GLA_DV:(h + 1) * GLA_DV] = drs[h].astype(BF16)
            dstate[h] = dsts[h]

    rev = lambda i: nb - 1 - i

    def col(width, off):
        return pl.BlockSpec((rb, width), lambda i, o=off // width: (rev(i), o))

    full = lambda shp: pl.BlockSpec(shp, lambda i: (0,) * len(shp))
    rowb = lambda w: pl.BlockSpec((rb, w), lambda i: (rev(i), 0))
    in_specs = [col(GLA_W, cq), col(GLA_W, ckk), rowb(128), full((128, GLA_W)), full((1, GLA_W)), full((1, GLA_DV)),
                pl.BlockSpec((1, H, GLA_DV, GLA_W), lambda i: (rev(i), 0, 0, 0)), rowb(H * GLA_DV)]
    in_specs += [col(GLA_DV, cv + h * GLA_DV) for h in range(H)] + [col(GLA_DV, cr + h * GLA_DV) for h in range(H)]
    return pl.pallas_call(
        body, name=name,
        out_shape=(jax.ShapeDtypeStruct((S, GLA_W), BF16), jax.ShapeDtypeStruct((S, GLA_W), BF16),
                   jax.ShapeDtypeStruct((S, H * GLA_DV), BF16), jax.ShapeDtypeStruct((S, H * GLA_DV), BF16),
                   jax.ShapeDtypeStruct((S, 128), F32), jax.ShapeDtypeStruct((128, GLA_W), F32),
                   jax.ShapeDtypeStruct((1, GLA_W), F32), jax.ShapeDtypeStruct((1, GLA_DV), F32)),
        grid=(nb,), in_specs=in_specs,
        out_specs=(rowb(GLA_W), rowb(GLA_W), rowb(H * GLA_DV), rowb(H * GLA_DV), rowb(128),
                   full((128, GLA_W)), full((1, GLA_W)), full((1, GLA_DV))),
        scratch_shapes=[pltpu.VMEM((H, GLA_DV, GLA_W), F32)],
        compiler_params=_params(("arbitrary",)),
    )(z, z, zsm, wg, bg, go, states, do, *([z] * (2 * H)))


def _row_spec(entry, tr):
    if isinstance(entry, tuple):
        arr, width, off = entry
        return arr, pl.BlockSpec((tr, width), lambda i, o=off // width: (i, o))
    return entry, pl.BlockSpec((tr, entry.shape[1]), lambda i: (i, 0))


def _stage_fwd(fn, rows, consts, outs, *, name, tr=None):
    first = rows[0][0] if isinstance(rows[0], tuple) else rows[0]
    S = first.shape[0]
    tr = tr or _pick(S, (512, 256, 128))
    arrs, specs = zip(*[_row_spec(e, tr) for e in rows])
    nr, nc = len(rows), len(consts)

    def body(*refs):
        vals = [r[...].astype(F32) for r in refs[:nr + nc]]
        res = fn(*vals)
        for o_ref, val in zip(refs[nr + nc:], res):
            o_ref[...] = val.astype(o_ref.dtype)

    cspecs = [pl.BlockSpec(c.shape, lambda i, n=c.ndim: (0,) * n) for c in consts]
    return pl.pallas_call(
        body, name=name,
        out_shape=tuple(jax.ShapeDtypeStruct((S, w), dt) for w, dt in outs), grid=(S // tr,),
        in_specs=list(specs) + cspecs,
        out_specs=tuple(pl.BlockSpec((tr, w), lambda i: (i, 0)) for w, _ in outs),
        compiler_params=_params(("parallel",)),
    )(*arrs, *consts)


def _stage_bwd(fn, rows, consts, cts, n_diff, drow_dtypes, *, name, tr=None, lead=None):
    first = rows[0][0] if isinstance(rows[0], tuple) else rows[0]
    S = first.shape[0]
    tr = tr or _pick(S, (512, 256, 128))
    arrs, specs = zip(*[_row_spec(e, tr) for e in rows])
    widths = [e[1] if isinstance(e, tuple) else e.shape[1] for e in rows]
    nr, nc, nt = len(rows), len(consts), len(cts)
    n_lead, lead_width = lead or (1, widths[0])
    n_rows_out = n_diff - n_lead + 1

    def body(*refs):
        vals = [r[...].astype(F32) for r in refs[:nr + nc]]
        ct = [r[...].astype(F32) for r in refs[nr + nc:nr + nc + nt]]
        drow_refs = refs[nr + nc + nt:nr + nc + nt + n_rows_out]
        dconst_refs = refs[nr + nc + nt + n_rows_out:]
        rest_rows = vals[n_diff:nr]

        def f(diff_rows, cs):
            return tuple(fn(*diff_rows, *rest_rows, *cs))

        _, vjp = jax.vjp(f, vals[:n_diff], vals[nr:])
        drows, dcs = vjp(tuple(ct))
        off = 0
        for val, w in zip(drows[:n_lead], widths):
            drow_refs[0][:, off:off + w] = val.astype(drow_refs[0].dtype)
            off += w
        for r, val in zip(drow_refs[1:], drows[n_lead:]):
            r[...] = val.astype(r.dtype)
        first_step = pl.program_id(0) == 0
        for r, val in zip(dconst_refs, dcs):
            @pl.when(first_step)
            def _(r=r, val=val):
                r[...] = val

            @pl.when(jnp.logical_not(first_step))
            def _(r=r, val=val):
                r[...] += val

    cspecs = [pl.BlockSpec(c.shape, lambda i, n=c.ndim: (0,) * n) for c in consts]
    ctspecs = [pl.BlockSpec((tr, c.shape[1]), lambda i: (i, 0)) for c in cts]
    out_shape = [jax.ShapeDtypeStruct((S, lead_width), drow_dtypes[0])]
    out_shape += [jax.ShapeDtypeStruct((S, widths[j]), drow_dtypes[j]) for j in range(n_lead, n_diff)]
    out_shape += [jax.ShapeDtypeStruct(c.shape, F32) for c in consts]
    out_specs = [pl.BlockSpec((tr, sum(widths[:n_lead])), lambda i: (i, 0))]
    out_specs += [pl.BlockSpec((tr, widths[j]), lambda i: (i, 0)) for j in range(n_lead, n_diff)] + cspecs
    res = pl.pallas_call(
        body, name=name, out_shape=tuple(out_shape), grid=(S // tr,),
        in_specs=list(specs) + cspecs + ctspecs, out_specs=tuple(out_specs),
        compiler_params=_params(("arbitrary",)),
    )(*arrs, *consts, *cts)
    return list(res[:n_rows_out]), list(res[n_rows_out:])


def _mla_prep_fn(cq, ckv, kr, kr_sw, cos, sin, gq, gkv, wq_n, wq_r, wq_sw, wk, wv):
    hq = _rms(cq, gq)
    hkv = _rms(ckv, gkv)
    return (bdot(hq, wq_n), bdot(hq, wq_r) * cos + bdot(hq, wq_sw) * sin,
            bdot(hkv, wk), bdot(hkv, wv), kr * cos + kr_sw * sin)


def _merge_fn(g0, g1, g2, of, og, om, b0, b1, b2, wf, wg, wm):
    return (_sigmoid(g0 + b0) * bdot(of, wf) + _sigmoid(g1 + b1) * bdot(og, wg) + _sigmoid(g2 + b2) * bdot(om, wm),)


_IN_SIZES = (256, 256, 256, 4, 256, 256, 512, 16, 512, 256, 128, 32, 3072)
_IN_OFF = np.concatenate([[0], np.cumsum(_IN_SIZES)])
(_O_FQ, _O_FK, _O_FV, _O_FF, _O_GQ, _O_GK, _O_GV, _O_GLOW, _O_GR, _O_MQ, _O_MKV, _O_MKR, _O_ZG) = [int(o) for o in _IN_OFF[:-1]]
N_IN = int(_IN_OFF[-1])
_BIG_GROUPS = ((_O_ZG, 3072), (_O_GV, 512), (_O_GR, 512), (_O_FQ, 256), (_O_FK, 256), (_O_FV, 256),
               (_O_GQ, 256), (_O_GK, 256), (_O_MQ, 256), (_O_MKV, 128))
Z_GATE, Z_GV, Z_GR, Z_FQ, Z_FK, Z_FV, Z_GQ, Z_GK, Z_MQ, Z_MKV = [int(o) for o in
                                                                    np.concatenate([[0], np.cumsum([w for _, w in _BIG_GROUPS])])[:-1]]
N_BIG = sum(w for _, w in _BIG_GROUPS)
_HALF = MLA_ROPE // 2
_QK_HD = MLA_NOPE + MLA_ROPE
SM_FF, SM_GLOW, SM_KR, SM_KR_SW, N_SM = 0, 8, 128, 256, 384
N_PAD = N_BIG + N_SM
_IN_SEGS = ([(o, w, 1.0) for o, w in _BIG_GROUPS]
            + [(_O_FF, 4, 1.0), (None, SM_GLOW - 4, 0.0), (_O_GLOW, GLA_RANK, 1.0), (None, 128 - SM_GLOW - GLA_RANK, 0.0)]
            + [(_O_MKR, MLA_ROPE, 1.0)] * MLA_HEADS
            + [(_O_MKR + _HALF, _HALF, -1.0), (_O_MKR, _HALF, 1.0)] * MLA_HEADS)


def _cols(x, start, width):
    return lax.slice_in_dim(x, start, start + width, axis=x.ndim - 1)


def _pad_w_in(w):
    return jnp.concatenate([jnp.zeros(w.shape[:-1] + (n,), w.dtype) if src is None else
                            (_cols(w, src, n) if sign > 0 else -_cols(w, src, n)) for src, n, sign in _IN_SEGS], axis=-1)


def _unpad_w_in(g):
    groups = []
    for o, n in zip(_IN_OFF[:-1], _IN_SIZES):
        total, pos = None, 0
        for src, m, sign in _IN_SEGS:
            if src is not None and o <= src and src + m <= o + n:
                term = _cols(g, pos, m) if sign > 0 else -_cols(g, pos, m)
                if m != n:
                    term = jnp.pad(term, [(0, 0)] * (g.ndim - 1) + [(int(src - o), int(o + n - src - m))])
                total = term if total is None else total + term
            pos += m
        groups.append(total)
    return jnp.concatenate(groups, axis=-1)


def _take(x, idx):
    idx = np.asarray(idx)
    cuts = [0] + [i for i in range(1, len(idx)) if idx[i] != idx[i - 1] + 1] + [len(idx)]
    return jnp.concatenate([_cols(x, int(idx[a]), b - a) for a, b in zip(cuts[:-1], cuts[1:])], axis=1)


_UQ_NOPE = np.concatenate([np.arange(h * _QK_HD, h * _QK_HD + MLA_NOPE) for h in range(MLA_HEADS)])
_UQ_ROT = np.concatenate([np.arange(h * _QK_HD + MLA_NOPE, (h + 1) * _QK_HD) for h in range(MLA_HEADS)])
_UKV_PERM = np.concatenate(
    [np.concatenate([np.arange(h * 128, h * 128 + MLA_NOPE) for h in range(MLA_HEADS)]),
     np.concatenate([np.arange(h * 128 + MLA_NOPE, (h + 1) * 128) for h in range(MLA_HEADS)])])
_UKV_INV = np.argsort(_UKV_PERM)


def _rotary_partner(r):
    return jnp.concatenate([piece for h in range(MLA_HEADS) for piece in
                            (-_cols(r, h * MLA_ROPE + _HALF, _HALF), _cols(r, h * MLA_ROPE, _HALF))], axis=1)


def _uq_grad(dn, dr, dsw):
    dr = dr + jnp.concatenate([piece for h in range(MLA_HEADS) for piece in
                               (_cols(dsw, h * MLA_ROPE + _HALF, _HALF), -_cols(dsw, h * MLA_ROPE, _HALF))], axis=1)
    return jnp.concatenate([piece for h in range(MLA_HEADS) for piece in
                            (_cols(dn, h * MLA_NOPE, MLA_NOPE), _cols(dr, h * MLA_ROPE, MLA_ROPE))], axis=1)


def _rope_tables(S):
    inv = ROPE_BASE ** (-jnp.arange(_HALF, dtype=F32) / _HALF)
    ang = jnp.arange(S, dtype=F32)[:, None] * inv[None, :]
    return jnp.tile(jnp.cos(ang), (1, 2 * MLA_HEADS)), jnp.tile(jnp.sin(ang), (1, 2 * MLA_HEADS))


class _LayerParams:
    def __init__(self, rep, l):
        self.w, self.rep, self.l, self.made = {}, rep, l, {}

    def __getitem__(self, k):
        if k not in self.made:
            self.made[k] = self._make(k)
        return self.made[k]

    def _make(self, k):
        w, rep, l = self.w, self.rep, self.l
        if k == 'wg':
            return jnp.pad(w['w_gla_gate'], [(SM_GLOW, LANES - SM_GLOW - GLA_RANK), (0, 0)])
        if k in ('wq_n', 'wq_r'):
            return _take(w['w_mla_uq'], _UQ_NOPE if k == 'wq_n' else _UQ_ROT)
        if k == 'wq_sw':
            return _rotary_partner(self['wq_r'])
        if k in ('wk', 'wv'):
            return _take(w['w_mla_ukv'], _UKV_PERM[:256] if k == 'wk' else _UKV_PERM[256:])
        if k == 'b_f':
            return jnp.zeros((8, 1), F32).at[:FOX_HEADS, 0].set(rep['b_fox_forget'][l])
        if k == 'b_gate':
            return [rep['b_branch_gate'][l][i * 1024:(i + 1) * 1024].reshape(1, 1024) for i in range(3)]
        vec = {'bg': 'b_gla_gate', 'go': 'g_gla_out', 'gq': 'g_mla_q', 'gkv': 'g_mla_kv'}
        if k in vec:
            return rep[vec[k]][l].reshape(1, -1)
        return rep[k][l] if k in rep else w[k]


_GLA_COLS = (Z_GQ, Z_GK, Z_GV, Z_GR)
_MLA_OUTS = [(256, BF16), (128, BF16), (256, BF16), (256, BF16), (128, BF16)]


def _mla_rows(z, zsm, rope):
    return [(z, 256, Z_MQ), (z, 128, Z_MKV), (zsm, 128, SM_KR), (zsm, 128, SM_KR_SW), *rope]


def _mla_consts(p):
    return [p['gq'], p['gkv'], p['wq_n'], p['wq_r'], p['wq_sw'], p['wk'], p['wv']]


def _fox_qkv(z):
    return [((z, Z_FQ, 256), (z, Z_FK, 256), FOX_HD, False)], (z, Z_FV, 256)


def _mla_qkv(qn, qr, kn, vv, kr):
    return [((qn, 0, 256), (kn, 0, 256), MLA_NOPE, False), ((qr, 0, 128), (kr, 0, 128), MLA_ROPE, True)], (vv, 0, 256)


def _xa_qkv(qx, kvx):
    return [((qx, 0, 512), (kvx, 0, 512), XA_HD, False)], (kvx, 512, 512)


def _merge_rows(z, o_fox, o_gla, o_mla):
    return [(z, 1024, Z_GATE), (z, 1024, Z_GATE + 1024), (z, 1024, Z_GATE + 2048), o_fox, o_gla, o_mla]


def _merge_consts(p):
    return p['b_gate'] + [p['w_up_fox'], p['w_up_gla'], p['w_up_mla']]


def _carried(hooks, key, call, single=False):
    rider, sink = hooks.pop(key, (None, None))
    res = call(rider=rider)
    if rider is None:
        return res
    sink(res[-1])
    return res[0] if single else res[:-1]


def _layer_fwd(x0, mem, p, rope, l, hooks):
    S = x0.shape[0]
    sv = {'x0': x0}

    def mm(key, a, b, **kw):
        return _carried(hooks, (l, key), lambda rider: _mm(a, b, mode='nn', rider=rider, name=f"{key}_{l}", **kw), single=True)

    h1 = _rms_fwd(x0, p['g_mix'], name=f"rms_mix_{l}")
    z = mm('in_big', h1, p['w_in'], out_dtype=BF16, b_cols=(0, N_BIG))
    zsm = _mm(h1, p['w_in'], mode='nn', out_dtype=F32, b_cols=(N_BIG, N_SM), name=f"in_small_{l}")
    sv.update(h1=h1, z=z, zsm=zsm)
    ff_t = jnp.zeros((8, S), F32).at[:FOX_HEADS].set(zsm[:, SM_FF:SM_FF + FOX_HEADS].T)
    cum_t = _fox_cum_fwd(ff_t, p['b_f'], name=f"fox_cum_{l}")
    cum = cum_t.T
    o_fox, lse_f = _carried(hooks, (l, 'fox_fwd'), lambda rider: _attn_fwd(
        *_fox_qkv(z), FOX_HEADS, cum_t, cum, scale=FOX_HD ** -0.5, mask='causal', name=f"fox_fwd_{l}", rider=rider))
    sv.update(ff_t=ff_t, cum=cum, cum_t=cum_t, lse_f=lse_f, o_fox=o_fox)
    o_gla, states = _gla_fwd(z, zsm, p['wg'], p['bg'], p['go'], _GLA_COLS, name=f"gla_fwd_{l}")
    sv.update(o_gla=o_gla, states=states)
    mla = _stage_fwd(_mla_prep_fn, _mla_rows(z, zsm, rope), _mla_consts(p), _MLA_OUTS, name=f"mla_prep_{l}")
    o_mla, lse_m = _carried(hooks, (l, 'mla_fwd'), lambda rider: _attn_fwd(
        *_mla_qkv(*mla), MLA_HEADS, None, None, scale=_QK_HD ** -0.5, mask='chunk', name=f"mla_fwd_{l}", rider=rider))
    sv.update(mla=mla, lse_m=lse_m, o_mla=o_mla)
    (y,) = _stage_fwd(_merge_fn, _merge_rows(z, o_fox, o_gla, o_mla), _merge_consts(p), [(1024, BF16)], name=f"merge_{l}")
    x1 = mm('out_proj', y, p['w_out'], out_dtype=F32, residual=x0)
    sv.update(y=y, x1=x1)
    h2 = _rms_fwd(x1, p['g_xa'], name=f"rms_xa_{l}")
    hm = _rms_fwd(mem, p['g_mem'], name=f"rms_mem_{l}")
    qx = _mm(h2, p['w_xq'], mode='nn', out_dtype=BF16, name=f"xq_{l}")
    kvx = _mm(hm, p['w_xkv'], mode='nn', out_dtype=BF16, name=f"xkv_{l}")
    ox, lse_x = _carried(hooks, (l, 'xa_fwd'), lambda rider: _attn_fwd(
        *_xa_qkv(qx, kvx), XA_HEADS, None, None, scale=XA_HD ** -0.5, mask=None, name=f"xa_fwd_{l}", rider=rider))
    x2 = mm('xo', ox, p['w_xo'], out_dtype=F32, residual=x1)
    sv.update(h2=h2, hm=hm, qx=qx, kvx=kvx, lse_x=lse_x, ox=ox, x2=x2)
    h3 = _rms_fwd(x2, p['g_mlp'], name=f"rms_mlp_{l}")
    a = mm('mlp1', h3, p['w_mlp1'], out_dtype=BF16)
    x3 = mm('mlp2', a, p['w_mlp2'], out_dtype=F32, act='relu2', residual=x2)
    sv.update(h3=h3, a=a)
    return x3, sv


def _layer_bwd(dx3, dx3b, mem, p, rope, sv, l, hooks, half_done):
    S = dx3.shape[0]
    g = {}
    da = _mm(dx3b, p['w_mlp2'], mode='nt', out_dtype=BF16, drelu_of=sv['a'], name=f"d_mlp2_in_{l}")
    g['w_mlp2'] = _mm(sv['a'], dx3b, mode='tn', out_dtype=BF16, act='relu2', name=f"d_w_mlp2_{l}")
    dx2, dx2b, g['g_mlp'] = _mm(da, p['w_mlp1'], mode='nt', out_dtype=F32, norm_bwd=(sv['x2'], p['g_mlp'], dx3), tm=512,
                                name=f"d_mlp1_in_{l}")
    g['w_mlp1'] = _mm(sv['h3'], da, mode='tn', out_dtype=BF16, col_shards=N_DEV, name=f"d_w_mlp1_{l}")
    dox = _mm(dx2b, p['w_xo'], mode='nt', out_dtype=BF16, name=f"d_xo_in_{l}")
    g['w_xo'] = _mm(sv['ox'], dx2b, mode='tn', out_dtype=BF16, name=f"d_w_xo_{l}")
    (dqx,), (dkx,), dvx = _attn_bwd(*_xa_qkv(sv['qx'], sv['kvx']), XA_HEADS, sv['ox'], dox, sv['lse_x'], None, None,
                                    scale=XA_HD ** -0.5, mask=None, name=f"xa_bwd_{l}")
    dqx = dqx.astype(BF16)
    dkvx = jnp.concatenate([dkx, dvx], axis=1).astype(BF16)
    dx1, dx1b, g['g_xa'] = _mm(dqx, p['w_xq'], mode='nt', out_dtype=F32, norm_bwd=(sv['x1'], p['g_xa'], dx2), tm=512,
                               name=f"d_xq_in_{l}")
    g['w_xq'] = _mm(sv['h2'], dqx, mode='tn', out_dtype=BF16, name=f"d_w_xq_{l}")
    dhm = _mm(dkvx, p['w_xkv'], mode='nt', out_dtype=F32, name=f"d_xkv_in_{l}")
    g['w_xkv'] = _mm(sv['hm'], dkvx, mode='tn', out_dtype=BF16, name=f"d_w_xkv_{l}")
    _, _, g['g_mem'] = _rms_bwd(mem, p['g_mem'], dhm, None, name=f"d_rms_mem_{l}")
    dy = _mm(dx1b, p['w_out'], mode='nt', out_dtype=F32, name=f"d_out_in_{l}")
    g['w_out'] = _mm(sv['y'], dx1b, mode='tn', out_dtype=BF16, name=f"d_w_out_{l}")
    z, zsm = sv['z'], sv['zsm']
    (dz, do_fox, do_gla, do_mla), (db0, db1, db2, g['w_up_fox'], g['w_up_gla'], g['w_up_mla']) = _stage_bwd(
        _merge_fn, _merge_rows(z, sv['o_fox'], sv['o_gla'], sv['o_mla']), _merge_consts(p), [dy], 6, [BF16] * 6,
        lead=(3, N_PAD), name=f"merge_bwd_{l}")
    g['b_branch_gate'] = jnp.concatenate([db0, db1, db2], axis=1).reshape(-1)
    half_done(l, g)
    (dfq,), (dfk,), dfv, dck, dcq = _carried(hooks, (l, 'fox_bwd'), lambda rider: _attn_bwd(
        *_fox_qkv(z), FOX_HEADS, sv['o_fox'], do_fox, sv['lse_f'], sv['cum_t'], sv['cum'],
        scale=FOX_HD ** -0.5, mask='causal', name=f"fox_bwd_{l}", rider=rider))
    dff_t, db_f = _fox_cum_bwd(sv['ff_t'], p['b_f'], dcq + dck.T, name=f"fox_cum_bwd_{l}")
    g['b_fox_forget'] = db_f[:FOX_HEADS, 0]
    dgq, dgk, dgv, dgr, dzsm, dwg, dbg, dgo = _gla_bwd(z, zsm, p['wg'], p['bg'], p['go'], sv['states'], do_gla, _GLA_COLS,
                                                       name=f"gla_bwd_{l}")
    g['w_gla_gate'] = dwg[SM_GLOW:SM_GLOW + GLA_RANK]
    g['b_gla_gate'] = dbg.reshape(-1)
    g['g_gla_out'] = dgo.reshape(-1)
    (dmqn, dmqr), (dmkn, dmkr), dmv = _carried(hooks, (l, 'mla_bwd'), lambda rider: _attn_bwd(
        *_mla_qkv(*sv['mla']), MLA_HEADS, sv['o_mla'], do_mla, sv['lse_m'], None, None,
        scale=_QK_HD ** -0.5, mask='chunk', name=f"mla_bwd_{l}", rider=rider))
    (dcq, dckv, dkr, dkr_sw), (dgq_n, dgkv_n, dwq_n, dwq_r, dwq_sw, dwk, dwv) = _stage_bwd(
        _mla_prep_fn, _mla_rows(z, zsm, rope), _mla_consts(p), [dmqn, dmqr, dmkn, dmv, dmkr], 4, [BF16] * 4,
        name=f"mla_prep_bwd_{l}")
    g['g_mla_q'] = dgq_n.reshape(-1)
    g['g_mla_kv'] = dgkv_n.reshape(-1)
    g['w_mla_uq'] = _uq_grad(dwq_n, dwq_r, dwq_sw)
    g['w_mla_ukv'] = _take(jnp.concatenate([dwk, dwv], axis=1), _UKV_INV)
    dsm = dzsm + jnp.pad(dff_t[:FOX_HEADS].T, [(0, 0), (0, 128 - FOX_HEADS)])
    dz = lax.dynamic_update_slice(dz, jnp.concatenate(
        [dgv, dgr, dfq.astype(BF16), dfk.astype(BF16), dfv.astype(BF16), dgq, dgk, dcq, dckv, dsm.astype(BF16), dkr, dkr_sw],
        axis=1), (0, Z_GV))
    dx0, dx0b, g['g_mix'] = _mm(dz, p['w_in'], mode='nt', out_dtype=F32, norm_bwd=(sv['x0'], p['g_mix'], dx1), tm=512,
                                tk=N_PAD // 2, name=f"d_in_{l}")
    g['w_in'] = _mm(sv['h1'], dz, mode='tn', out_dtype=BF16, tn=N_PAD // 3, name=f"d_w_in_{l}")
    for n in ('g_mlp', 'g_mem', 'g_xa', 'g_mix'):
        g[n] = g[n].reshape(-1)
    return dx0, dx0b, g


def _local_step(x, mem, target, ps, g_final, hooks, half_done, layer_done):
    rope = _rope_tables(x.shape[0])
    saved = []
    for l, p in enumerate(ps):
        x, sv = _layer_fwd(x, mem, p, rope, l, hooks)
        saved.append(sv)
    loss, dx, dxb, dgf = _loss_head(x, g_final, target, name="loss_head")
    for l in reversed(range(len(ps))):
        dx, dxb, grads = _layer_bwd(dx, dxb, mem, ps[l], rope, saved[l], l, hooks, half_done)
        layer_done(l, grads)
    assert not hooks, f"exchanges without a carrier: {list(hooks)}"
    return loss, dx, dgf.reshape(-1)


_MESH_AXES = ("x", "y", "c")
_HBM = pl.BlockSpec(memory_space=pl.ANY)


N_CHIP = 4


def _place():
    x, y, c = (lax.axis_index(n) for n in _MESH_AXES)
    return (x, y, c), (x, y, 1 - c), [(1 - x, y), (x, 1 - y), (1 - x, 1 - y)]


def _remote(src, dst, sems, k, to):
    return pltpu.make_async_remote_copy(src_ref=src, dst_ref=dst, send_sem=sems[0].at[k], recv_sem=sems[1].at[k],
                                        device_id=to, device_id_type=pl.DeviceIdType.MESH)


def _all_gather(x, *, name):
    def body(x_ref, o_ref, send_sems, recv_sems, local_sem):
        me, sib, chips = _place()
        c = me[2]
        sems = (send_sems, recv_sems)
        slot = lambda px, py, pc: o_ref.at[4 * px + 2 * py + pc]
        mine = pltpu.make_async_copy(x_ref, slot(*me), local_sem)
        mine.start()
        first = [_remote(x_ref, slot(*me), sems, 0, sib)]
        first += [_remote(x_ref, slot(*me), sems, 1 + j, (*chip, c)) for j, chip in enumerate(chips)]
        for cp in first:
            cp.start()
        passed = [_remote(slot(*chip, c), slot(*chip, c), sems, 4 + j, sib) for j, chip in enumerate(chips)]
        for j, chip in enumerate(chips):
            _remote(x_ref, slot(*chip, c), sems, 1 + j, me).wait_recv()
            passed[j].start()
        _remote(x_ref, slot(*sib), sems, 0, me).wait_recv()
        for j, chip in enumerate(chips):
            _remote(x_ref, slot(*chip, 1 - c), sems, 4 + j, me).wait_recv()
        for cp in first + passed:
            cp.wait_send()
        mine.wait()

    return pl.pallas_call(
        body, name=name, out_shape=jax.ShapeDtypeStruct((N_DEV,) + x.shape, x.dtype),
        in_specs=[_HBM], out_specs=_HBM,
        scratch_shapes=[pltpu.SemaphoreType.DMA((N_DEV - 1,)), pltpu.SemaphoreType.DMA((N_DEV - 1,)), pltpu.SemaphoreType.DMA],
        compiler_params=pltpu.CompilerParams(has_side_effects=True),
    )(x)


class _Rider:
    def __init__(self, inputs, out_shapes, scratch, start, finish, post):
        self.inputs, self.out_shapes, self.scratch = list(inputs), list(out_shapes), list(scratch)
        self.start, self.finish, self.post = start, finish, post


def _run_rider(rider, *, name):
    def body(*refs):
        rider.start(refs)
        rider.finish(refs)

    outs = pl.pallas_call(
        body, name=name, out_shape=tuple(rider.out_shapes), in_specs=[_HBM] * len(rider.inputs),
        out_specs=(_HBM,) * len(rider.out_shapes), scratch_shapes=rider.scratch,
        compiler_params=pltpu.CompilerParams(has_side_effects=True),
    )(*rider.inputs)
    return rider.post(outs)


def _carry(rider, n_in, n_out, first, last):
    if rider is None:
        return [], [], [], [], [], lambda refs: refs
    ni, no = len(rider.inputs), len(rider.out_shapes)

    def split(refs):
        own_in, r_in = refs[:n_in], refs[n_in:n_in + ni]
        own_out, r_out = refs[n_in + ni:n_in + ni + n_out], refs[n_in + ni + n_out:n_in + ni + n_out + no]
        rest = refs[n_in + ni + n_out + no:]
        own_scr, r_scr = rest[:len(rest) - len(rider.scratch)], rest[len(rest) - len(rider.scratch):]
        rrefs = tuple(r_in) + tuple(r_out) + tuple(r_scr)
        pl.when(first())(lambda: rider.start(rrefs))
        pl.when(last())(lambda: rider.finish(rrefs))
        return tuple(own_in) + tuple(own_out) + tuple(own_scr)

    return list(rider.inputs), [_HBM] * ni, list(rider.out_shapes), [_HBM] * no, list(rider.scratch), split


def _gather_rider(shards, axes):
    n = len(shards)
    srcs, out_shapes, kinds = [], [], []
    for s, ax in zip(shards, axes):
        L, a, b = s.shape
        if ax == 1:
            srcs.append(s.reshape(L, 1, a, b)), out_shapes.append((L, N_DEV, a, b)), kinds.append('row')
        elif b % 128 == 0:
            srcs.append(s), out_shapes.append((L, a, N_DEV * b)), kinds.append('col')
        else:
            srcs.append(s.reshape(1, L, a, b)), out_shapes.append((N_DEV, L, a, b)), kinds.append('slot')

    def parts(refs):
        x_refs, o_refs = refs[:n], refs[n:2 * n]
        send_sems, recv_sems, local_sem = refs[2 * n:]
        me, sib, chips = _place()
        sems = (send_sems, recv_sems)

        def win(t, px, py, pc):
            idx = 4 * px + 2 * py + pc
            if kinds[t] == 'row':
                return o_refs[t].at[:, pl.ds(idx, 1)]
            if kinds[t] == 'col':
                width = shards[t].shape[2]
                return o_refs[t].at[:, :, pl.ds(pl.multiple_of(idx * width, 128), width)]
            return o_refs[t].at[pl.ds(idx, 1)]

        def group(k, block, to, own):
            return [_remote(x_refs[t] if own else win(t, *block), win(t, *block), sems, k * n + t, to) for t in range(n)]

        mine = [pltpu.make_async_copy(x_refs[t], win(t, *me), local_sem.at[t]) for t in range(n)]
        first = group(0, me, sib, True)
        for j, chip in enumerate(chips):
            first += group(1 + j, me, (*chip, me[2]), True)
        return me, sib, chips, group, mine, first

    def start(refs):
        *_, mine, first = parts(refs)
        for cp in mine + first:
            cp.start()

    def finish(refs):
        me, sib, chips, group, mine, first = parts(refs)
        c = me[2]
        passed = []
        for j, chip in enumerate(chips):
            for cp in group(1 + j, (*chip, c), me, False):
                cp.wait_recv()
            fwd = group(4 + j, (*chip, c), sib, False)
            for cp in fwd:
                cp.start()
            passed += fwd
        for cp in group(0, sib, me, False):
            cp.wait_recv()
        for j, chip in enumerate(chips):
            for cp in group(4 + j, (*chip, 1 - c), me, False):
                cp.wait_recv()
        for cp in first + passed:
            cp.wait_send()
        for cp in mine:
            cp.wait()

    def post(outs):
        whole = []
        for o, s, kind in zip(outs, shards, kinds):
            L, a, b = s.shape
            whole.append(o.reshape(L, N_DEV * a, b) if kind == 'row' else o if kind == 'col' else _to_whole(o, 2))
        return whole

    return _Rider(srcs, [jax.ShapeDtypeStruct(shp, s.dtype) for shp, s in zip(out_shapes, shards)],
                  [pltpu.SemaphoreType.DMA(((N_DEV - 1) * n,)), pltpu.SemaphoreType.DMA(((N_DEV - 1) * n,)),
                   pltpu.SemaphoreType.DMA((n,))], start, finish, post)


def _sibling_swap(x, *, name):
    def body(x_ref, o_ref, send_sems, recv_sems):
        me, sib, _ = _place()
        c = me[2]
        sems = (send_sems, recv_sems)
        sends = [_remote(x_ref.at[j, 1 - c], o_ref.at[j], sems, j, sib) for j in range(N_CHIP)]
        for cp in sends:
            cp.start()
        for cp in sends:
            cp.wait_send()
            cp.wait_recv()

    return pl.pallas_call(
        body, name=name, out_shape=jax.ShapeDtypeStruct((N_CHIP,) + x.shape[2:], x.dtype),
        in_specs=[_HBM], out_specs=_HBM,
        scratch_shapes=[pltpu.SemaphoreType.DMA((N_CHIP,)), pltpu.SemaphoreType.DMA((N_CHIP,))],
        compiler_params=pltpu.CompilerParams(has_side_effects=True),
    )(x)


def _pair_sum(x, got, c, *, name):
    _, _, R, _ = x.shape
    tr = _pick(R, (1024, 512, 256, 128, 64, 32, 16, 8))

    def body(c_ref, x_ref, g_ref, o_ref):
        o_ref[...] = (x_ref[...].astype(F32) + g_ref[...].astype(F32)).astype(o_ref.dtype)

    return pl.pallas_call(
        body, name=name, out_shape=jax.ShapeDtypeStruct((N_CHIP, R, 128), x.dtype),
        grid_spec=pltpu.PrefetchScalarGridSpec(
            num_scalar_prefetch=1, grid=(N_CHIP, R // tr),
            in_specs=[pl.BlockSpec((None, None, tr, 128), lambda j, i, c_ref: (j, c_ref[0], i, 0)),
                      pl.BlockSpec((None, tr, 128), lambda j, i, c_ref: (j, i, 0))],
            out_specs=pl.BlockSpec((None, tr, 128), lambda j, i, c_ref: (j, i, 0))),
        compiler_params=_params(("parallel", "parallel")),
    )(c, x, got)


def _chip_all_to_all_rider(x):
    def parts(refs):
        x_ref, o_ref, send_sems, recv_sems, local_sem = refs
        me, _, chips = _place()
        sems = (send_sems, recv_sems)
        mine = 2 * me[0] + me[1]
        local = pltpu.make_async_copy(x_ref.at[mine], o_ref.at[mine], local_sem)
        sends = [_remote(x_ref.at[2 * px + py], o_ref.at[mine], sems, j, (px, py, me[2])) for j, (px, py) in enumerate(chips)]
        arrival = lambda j: _remote(x_ref.at[mine], o_ref.at[2 * chips[j][0] + chips[j][1]], sems, j, me)
        return local, sends, arrival

    def start(refs):
        local, sends, _ = parts(refs)
        for cp in [local] + sends:
            cp.start()

    def finish(refs):
        local, sends, arrival = parts(refs)
        for j, cp in enumerate(sends):
            cp.wait_send()
            arrival(j).wait_recv()
        local.wait()

    return _Rider([x], [jax.ShapeDtypeStruct(x.shape, x.dtype)],
                  [pltpu.SemaphoreType.DMA((N_CHIP - 1,)), pltpu.SemaphoreType.DMA((N_CHIP - 1,)), pltpu.SemaphoreType.DMA],
                  start, finish, lambda outs: outs[0])


def _sum_slots(x, *, name):
    n, R, _ = x.shape
    tr = _pick(R, (1024, 512, 256, 128, 64, 32, 16, 8))

    def body(x_ref, o_ref):
        acc = x_ref[0].astype(F32)
        for j in range(1, n):
            acc = acc + x_ref[j].astype(F32)
        o_ref[...] = acc

    return pl.pallas_call(
        body, name=name, out_shape=jax.ShapeDtypeStruct((R, 128), F32), grid=(R // tr,),
        in_specs=[pl.BlockSpec((n, tr, 128), lambda i: (0, i, 0))], out_specs=pl.BlockSpec((tr, 128), lambda i: (i, 0)),
        compiler_params=_params(("parallel",)),
    )(x)


def _adamw(w, g, m, v, *, name):
    shape = w.shape
    cols = shape[-1]
    rows = int(np.prod(shape[:-1]))
    tr = next((t for t in (1024, 512, 256, 128, 64, 32, 16, 8) if rows % t == 0 and t * cols * 4 <= (1 << 20)), rows)

    def body(w_ref, g_ref, m_ref, v_ref, d_ref, mo_ref, vo_ref):
        g_ = g_ref[...]
        m_ = ADAM_B1 * m_ref[...] + (1.0 - ADAM_B1) * g_
        v_ = ADAM_B2 * v_ref[...] + (1.0 - ADAM_B2) * jnp.square(g_)
        m_hat = m_ / (1.0 - ADAM_B1 ** ADAM_STEP)
        v_hat = v_ / (1.0 - ADAM_B2 ** ADAM_STEP)
        d_ref[...] = -ADAM_LR * (m_hat / (jnp.sqrt(v_hat) + ADAM_EPS) + ADAM_WD * w_ref[...])
        mo_ref[...] = m_
        vo_ref[...] = v_

    blk = pl.BlockSpec((tr, cols), lambda i: (i, 0))
    outs = pl.pallas_call(
        body, name=name, out_shape=tuple(jax.ShapeDtypeStruct((rows, cols), F32) for _ in range(3)), grid=(rows // tr,),
        in_specs=[blk] * 4, out_specs=(blk,) * 3, compiler_params=_params(("parallel",)),
    )(*(a.reshape(rows, cols) for a in (w, g, m, v)))
    return tuple(o.reshape(shape) for o in outs)


_WEIGHTS = ('g_mix', 'w_in', 'b_fox_forget', 'w_gla_gate', 'b_gla_gate', 'g_gla_out', 'g_mla_q', 'w_mla_uq', 'g_mla_kv',
            'w_mla_ukv', 'b_branch_gate', 'w_up_fox', 'w_up_gla', 'w_up_mla', 'w_out', 'g_xa', 'g_mem', 'w_xq', 'w_xkv',
            'w_xo', 'g_mlp', 'w_mlp1', 'w_mlp2', 'g_final')
_SHARDED = (('w_in', 1), ('w_gla_gate', 2), ('w_mla_uq', 2), ('w_mla_ukv', 2), ('w_up_fox', 2), ('w_up_gla', 2),
            ('w_up_mla', 2), ('w_out', 1), ('w_xq', 1), ('w_xkv', 1), ('w_xo', 2), ('w_mlp1', 2), ('w_mlp2', 1))
_REPLICATED = tuple(n for n in _WEIGHTS if n not in dict(_SHARDED))
_ROW_PAD = 1024
_SMALL_ROW_PAD = 8
_PIECE_ROWS = 16


def _pack(flats, lead, row_pad=_ROW_PAD):
    if all(int(np.prod(a.shape[lead:])) % 128 == 0 for a in flats):
        def block(a):
            a = a.reshape(a.shape[:lead] + (-1, 128))
            return jnp.pad(a, [(0, 0)] * lead + [(0, -a.shape[lead] % _PIECE_ROWS), (0, 0)])
        cat = jnp.concatenate([block(a) for a in flats], axis=lead)
        rows = cat.shape[lead]
        return jnp.pad(cat, [(0, 0)] * lead + [(0, -(-rows // row_pad) * row_pad - rows), (0, 0)])
    cat = jnp.concatenate([a.reshape(a.shape[:lead] + (-1,)) for a in flats], axis=-1)
    n = cat.shape[-1]
    total = -(-n // (128 * row_pad)) * (128 * row_pad)
    cat = jnp.pad(cat, [(0, 0)] * lead + [(0, total - n)])
    return cat.reshape(cat.shape[:lead] + (total // 128, 128))


def _unpack(buf, shapes, lead):
    sizes = [int(np.prod(shp)) for shp in shapes]
    out, off = [], 0
    if all(n % 128 == 0 for n in sizes):
        for shp, n in zip(shapes, sizes):
            rows = buf[(slice(None),) * lead + (slice(off, off + n // 128),)]
            out.append(rows.reshape(buf.shape[:lead] + tuple(shp)))
            off += -(-(n // 128) // _PIECE_ROWS) * _PIECE_ROWS
        return out
    flat = buf.reshape(buf.shape[:lead] + (-1,))
    for shp, n in zip(shapes, sizes):
        out.append(flat[..., off:off + n].reshape(buf.shape[:lead] + tuple(shp)))
        off += n
    return out


def _to_whole(g, axis):
    if axis == 1:
        return g.transpose(1, 0, 2, 3).reshape(g.shape[1], N_DEV * g.shape[2], g.shape[3])
    return g.transpose(1, 2, 0, 3).reshape(g.shape[1], g.shape[2], N_DEV * g.shape[3])


def _to_shards(w, axis):
    L, R, C = w.shape
    if axis == 1:
        return w.reshape(L, N_DEV, R // N_DEV, C).transpose(1, 0, 2, 3)
    return w.reshape(L, R, N_DEV, C // N_DEV).transpose(2, 0, 1, 3)


def kernel(x, mem, g_mix, w_in, b_fox_forget, w_gla_gate, b_gla_gate, g_gla_out, g_mla_q, w_mla_uq, g_mla_kv, w_mla_ukv, b_branch_gate, w_up_fox, w_up_gla, w_up_mla, w_out, g_xa, g_mem, w_xq, w_xkv, w_xo, g_mlp, w_mlp1, w_mlp2, g_final, loss_target, m_g_mix, m_w_in, m_b_fox_forget, m_w_gla_gate, m_b_gla_gate, m_g_gla_out, m_g_mla_q, m_w_mla_uq, m_g_mla_kv, m_w_mla_ukv, m_b_branch_gate, m_w_up_fox, m_w_up_gla, m_w_up_mla, m_w_out, m_g_xa, m_g_mem, m_w_xq, m_w_xkv, m_w_xo, m_g_mlp, m_w_mlp1, m_w_mlp2, m_g_final, v_g_mix, v_w_in, v_b_fox_forget, v_w_gla_gate, v_b_gla_gate, v_g_gla_out, v_g_mla_q, v_w_mla_uq, v_g_mla_kv, v_w_mla_ukv, v_b_branch_gate, v_w_up_fox, v_w_up_gla, v_w_up_mla, v_w_out, v_g_xa, v_g_mem, v_w_xq, v_w_xkv, v_w_xo, v_g_mlp, v_w_mlp1, v_w_mlp2, v_g_final):
    wts = dict(zip(_WEIGHTS, (g_mix, w_in, b_fox_forget, w_gla_gate, b_gla_gate, g_gla_out, g_mla_q, w_mla_uq, g_mla_kv,
                              w_mla_ukv, b_branch_gate, w_up_fox, w_up_gla, w_up_mla, w_out, g_xa, g_mem, w_xq, w_xkv, w_xo,
                              g_mlp, w_mlp1, w_mlp2, g_final)))
    mom1 = dict(zip(_WEIGHTS, (m_g_mix, m_w_in, m_b_fox_forget, m_w_gla_gate, m_b_gla_gate, m_g_gla_out, m_g_mla_q,
                               m_w_mla_uq, m_g_mla_kv, m_w_mla_ukv, m_b_branch_gate, m_w_up_fox, m_w_up_gla, m_w_up_mla,
                               m_w_out, m_g_xa, m_g_mem, m_w_xq, m_w_xkv, m_w_xo, m_g_mlp, m_w_mlp1, m_w_mlp2, m_g_final)))
    mom2 = dict(zip(_WEIGHTS, (v_g_mix, v_w_in, v_b_fox_forget, v_w_gla_gate, v_b_gla_gate, v_g_gla_out, v_g_mla_q,
                               v_w_mla_uq, v_g_mla_kv, v_w_mla_ukv, v_b_branch_gate, v_w_up_fox, v_w_up_gla, v_w_up_mla,
                               v_w_out, v_g_xa, v_g_mem, v_w_xq, v_w_xkv, v_w_xo, v_g_mlp, v_w_mlp1, v_w_mlp2, v_g_final)))
    depth = g_mix.shape[0]

    names = [n for n, _ in _SHARDED]
    axes = dict(_SHARDED)
    shard = {n: wts[n] for n in names}
    shard['w_in'] = _pad_w_in(w_in)
    rep = {n: wts[n] for n in _REPLICATED}
    ps = [_LayerParams(rep, l) for l in range(depth)]

    def gather(group, l):
        rider = _gather_rider([shard[n][l:l + 1].astype(BF16) for n in group], [axes[n] for n in group])
        return rider, lambda whole: ps[l].w.update({n: w[0] for n, w in zip(group, whole)})

    first, sink = gather(['w_in'], 0)
    sink(_run_rider(first, name="gather_w_in_0"))
    narrow = ['w_gla_gate', 'w_mla_uq', 'w_mla_ukv', 'w_up_fox', 'w_up_gla', 'w_up_mla']
    hooks = {(0, 'in_big'): gather(narrow + ['w_out', 'w_xq', 'w_xkv', 'w_xo'], 0),
             (0, 'fox_fwd'): gather(['w_mlp1', 'w_mlp2'], 0)}
    ahead = (('mla_fwd', ['w_in'] + narrow), ('out_proj', ['w_out']), ('xa_fwd', ['w_xq', 'w_xo']), ('xo', ['w_xkv']),
             ('mlp1', ['w_mlp1']), ('mlp2', ['w_mlp2']))
    assert sorted(n for _, group in ahead for n in group) == sorted(names)
    for l in range(1, depth):
        for key, group in ahead:
            hooks[(l - 1, key)] = gather(group, l)

    core = lax.axis_index("c").astype(jnp.int32).reshape(1)
    late = ['w_in', 'w_gla_gate', 'w_mla_uq', 'w_mla_ukv']
    groups = {'early': [n for n in names if n not in late], 'late': late}
    small_grads, landed = {}, {}

    def exchange(l, g, which):
        slots = _pack([(g[n][:, None] if g[n].ndim == 3 else _to_shards(g[n][None], axes[n])).astype(BF16)
                       for n in groups[which]], 1)
        slots = slots.reshape((N_CHIP, 2) + slots.shape[1:])
        paired = _pair_sum(slots, _sibling_swap(slots, name=f"swap_grads_{which}_{l}"), core, name=f"pair_grads_{which}_{l}")
        return _chip_all_to_all_rider(paired), lambda got: landed.update({(l, which): got})

    def half_done(l, g):
        hooks[(l, 'mla_bwd')] = exchange(l, g, 'early')

    def layer_done(l, g):
        small_grads[l] = g
        rider, sink = exchange(l, g, 'late')
        if l > 0:
            hooks[(l - 1, 'fox_bwd')] = (rider, sink)
        else:
            sink(_run_rider(rider, name=f"scatter_grads_late_{l}"))

    loss, dx, dg_final = _local_step(x[0], mem[0], loss_target[0], ps, g_final, hooks, half_done, layer_done)
    loss = lax.psum(loss[0, 0], _MESH_AXES)

    grad = {}
    for which, group in groups.items():
        shapes = [(1,) + shard[n].shape[1:] for n in group]
        per_layer = [_unpack(_sum_slots(landed[(l, which)], name=f"sum_grads_{which}_{l}"), shapes, 0) for l in range(depth)]
        grad.update({n: jnp.concatenate([per_layer[l][i] for l in range(depth)], axis=0) for i, n in enumerate(group)})
    grad['w_in'] = _unpad_w_in(grad['w_in'])
    grads = small_grads
    small = [dg_final if n == 'g_final' else jnp.stack([grads[l][n] for l in range(depth)]) for n in _REPLICATED]
    small_shapes = [wts[n].shape for n in _REPLICATED]
    small_sum = _sum_slots(_all_gather(_pack(small, 0, _SMALL_ROW_PAD), name="gather_small_grads"), name="sum_small_grads")
    grad.update(dict(zip(_REPLICATED, _unpack(small_sum, small_shapes, 0))))

    delta, new_m, new_v = {}, {}, {}
    for n, _ in _SHARDED:
        delta[n], new_m[n], new_v[n] = _adamw(wts[n], grad[n], mom1[n], mom2[n], name=f"adamw_{n}")
    packed = [_pack([d[n] for n in _REPLICATED], 0, _SMALL_ROW_PAD) for d in (wts, mom1, mom2)]
    outs = _adamw(packed[0], small_sum, packed[1], packed[2], name="adamw_small")
    for d, o in zip((delta, new_m, new_v), outs):
        d.update(dict(zip(_REPLICATED, _unpack(o, small_shapes, 0))))

    return (loss, dx[None], *[grad[n] for n in _WEIGHTS], *[delta[n] for n in _WEIGHTS],
            *[new_m[n] for n in _WEIGHTS], *[new_v[n] for n in _WEIGHTS])
```

```python
import functools

import jax
import jax.numpy as jnp
import numpy as np
from jax import lax
from jax.experimental import pallas as pl
from jax.experimental.pallas import tpu as pltpu

F32 = jnp.float32
BF16 = jnp.bfloat16

EPS = 1e-6
CHUNK = 64
FOX_HEADS, FOX_HD = 4, 64
GLA_HEADS, GLA_DK, GLA_DV, GLA_RANK, GLA_TAU = 4, 64, 128, 16, 16.0
MLA_HEADS, MLA_Q_RANK, MLA_KV_RANK, MLA_NOPE, MLA_ROPE, MLA_VD = 4, 256, 128, 64, 32, 64
ROPE_BASE = 10000.0
XA_HEADS, XA_HD = 4, 128
ADAM_LR, ADAM_B1, ADAM_B2, ADAM_EPS, ADAM_WD, ADAM_STEP = 0.001, 0.9, 0.999, 1e-08, 0.01, 10

N_DEV = 8
V7X_VMEM_LIMIT = 56 * 1024 * 1024
NEG = -1e30

NN = ((1,), (0,))
NT = ((1,), (1,))
TN = ((0,), (0,))


def _dot(a, b, dims):
    return lax.dot_general(a.astype(BF16), b.astype(BF16), (dims, ((), ())), preferred_element_type=F32)


@jax.custom_vjp
def bdot(a, b):
    return _dot(a, b, NN)


bdot.defvjp(lambda a, b: (_dot(a, b, NN), (a, b)),
            lambda res, g: (_dot(g, res[1], NT), _dot(res[0], g, TN)))


@jax.custom_vjp
def bdot_nt(a, b):
    return _dot(a, b, NT)


bdot_nt.defvjp(lambda a, b: (_dot(a, b, NT), (a, b)),
               lambda res, g: (_dot(g, res[1], NN), _dot(g, res[0], TN)))


@jax.custom_vjp
def bdot_tn(a, b):
    return _dot(a, b, TN)


bdot_tn.defvjp(lambda a, b: (_dot(a, b, TN), (a, b)),
               lambda res, g: (_dot(res[1], g, NT), _dot(res[0], g, NN)))


def _split2(x):
    hi = x.astype(BF16)
    lo = (x - hi.astype(F32)).astype(BF16)
    return hi, lo


def _tri(n, lower):
    r = lax.broadcasted_iota(jnp.int32, (n, n), 0)
    c = lax.broadcasted_iota(jnp.int32, (n, n), 1)
    return jnp.where((r >= c) if lower else (r <= c), 1.0, 0.0).astype(BF16)


def _log_sigmoid(x):
    return jnp.minimum(x, 0.0) - jnp.log(1.0 + jnp.exp(-jnp.abs(x)))


def _sigmoid(x):
    return 1.0 / (1.0 + jnp.exp(-x))


def _rms(x, g):
    return x * lax.rsqrt(jnp.mean(x * x, axis=-1, keepdims=True) + EPS) * g


def _pick(dim, prefs):
    for p in prefs:
        if dim % p == 0:
            return p
    return dim


def _params(sem):
    return pltpu.CompilerParams(dimension_semantics=sem, vmem_limit_bytes=V7X_VMEM_LIMIT)


def _rms_vjp(x, g, dy, dres):
    rstd = lax.rsqrt(jnp.mean(x * x, axis=-1, keepdims=True) + EPS)
    xh = x * rstd
    gdy = dy * g
    dx = (gdy - xh * jnp.mean(gdy * xh, axis=-1, keepdims=True)) * rstd
    return (dx if dres is None else dx + dres), jnp.sum(dy * xh, axis=0, keepdims=True)


def _mm(a, b, *, mode, out_dtype, name, act=None, residual=None, drelu_of=None, norm_bwd=None, b_cols=None,
        col_shards=None, rider=None, tm=None, tn=None, tk=None):
    b_off, b_width = b_cols or (0, b.shape[1])
    if mode == 'nn':
        (M, K), N = a.shape, b_width
    elif mode == 'nt':
        (M, K), N = a.shape, b.shape[0]
    else:
        (K, M), N = a.shape, b_width
    tm = tm or _pick(M, (1024, 512, 256, 128))
    tn = tn or _pick(N, (1024, 1920, 1152, 768, 640, 512, 384, 256, 128))
    tk = tk or _pick(K, (1024, 1920, 1152, 640, 512, 256, 128))
    nk = K // tk
    dims = {'nn': NN, 'nt': NT, 'tn': TN}[mode]
    a_spec = pl.BlockSpec((tk, tm), lambda i, j, k: (k, i)) if mode == 'tn' else pl.BlockSpec((tm, tk), lambda i, j, k: (i, k))
    if mode == 'nt':
        b_spec = pl.BlockSpec((tn, tk), lambda i, j, k, o=b_off // tk: (j, k + o))
    else:
        b_spec = pl.BlockSpec((tk, tn), lambda i, j, k, o=b_off // tn: (k, j + o))
    o_spec = pl.BlockSpec((tm, tn), lambda i, j, k: (i, j))
    extra = [e for e in (residual, drelu_of) if e is not None]
    extra_specs = [o_spec] * len(extra)
    out_shape, out_specs, n_out = jax.ShapeDtypeStruct((M, N), out_dtype), o_spec, 1
    if col_shards:
        n_sh = N // col_shards
        assert tn % n_sh == 0 and not extra and norm_bwd is None
        out_shape = jax.ShapeDtypeStruct((col_shards, M, n_sh), out_dtype)
        out_specs = pl.BlockSpec((tn // n_sh, tm, n_sh), lambda i, j, k: (j, i, 0))
    if norm_bwd is not None:
        x_in, g_in, dres_in = norm_bwd
        assert tn == N and residual is None and drelu_of is None
        vec = pl.BlockSpec((1, N), lambda i, j, k: (0, 0))
        extra, extra_specs = [x_in, g_in.reshape(1, N), dres_in], [o_spec, vec, o_spec]
        out_shape = (jax.ShapeDtypeStruct((M, N), F32), jax.ShapeDtypeStruct((M, N), BF16), jax.ShapeDtypeStruct((1, N), F32))
        out_specs, n_out = (o_spec, o_spec, vec), 3

    grid = (M // tm, N // tn, nk)
    r_ins, r_in_specs, r_outs, r_out_specs, r_scratch, split = _carry(
        rider, 2 + len(extra), n_out, lambda: functools.reduce(jnp.logical_and, [pl.program_id(d) == 0 for d in range(3)]),
        lambda: functools.reduce(jnp.logical_and, [pl.program_id(d) == grid[d] - 1 for d in range(3)]))
    assert rider is None or n_out == 1

    def body(*refs):
        a_ref, b_ref, *rest = split(refs)
        o_ref = rest[len(extra)]
        first_rows = pl.program_id(0) == 0
        at = a_ref[...]
        if act == 'relu2':
            at = jnp.square(jnp.maximum(at.astype(F32), 0.0))
        part = _dot(at, b_ref[...], dims)

        def finish(acc):
            if norm_bwd is not None:
                dx, dg = _rms_vjp(rest[0][...], rest[1][...], acc, rest[2][...])
                o_ref[...] = dx
                rest[len(extra) + 1][...] = dx.astype(BF16)
                dg_ref = rest[len(extra) + 2]

                @pl.when(first_rows)
                def _():
                    dg_ref[...] = dg

                @pl.when(jnp.logical_not(first_rows))
                def _():
                    dg_ref[...] += dg
                return
            idx = 0
            if residual is not None:
                acc = acc + rest[idx][...]
                idx += 1
            if drelu_of is not None:
                acc = acc * (2.0 * jnp.maximum(rest[idx][...].astype(F32), 0.0))
            if col_shards:
                for t in range(tn // n_sh):
                    o_ref[t] = acc[:, t * n_sh:(t + 1) * n_sh].astype(out_dtype)
            else:
                o_ref[...] = acc.astype(out_dtype)

        if nk == 1:
            finish(part)
        else:
            acc_ref = rest[len(extra) + n_out]
            k = pl.program_id(2)

            @pl.when(k == 0)
            def _():
                acc_ref[...] = part

            @pl.when(k > 0)
            def _():
                acc_ref[...] += part

            @pl.when(k == nk - 1)
            def _():
                finish(acc_ref[...])

    scratch = [] if nk == 1 else [pltpu.VMEM((tm, tn), F32)]
    if rider is not None:
        res = pl.pallas_call(
            body, name=name, out_shape=(out_shape, *r_outs), grid=grid, in_specs=[a_spec, b_spec] + extra_specs + r_in_specs,
            out_specs=(out_specs, *r_out_specs), scratch_shapes=scratch + r_scratch,
            compiler_params=_params(("arbitrary", "arbitrary", "arbitrary")),
        )(a, b, *extra, *r_ins)
        return res[0], rider.post(res[1:])
    return pl.pallas_call(
        body, name=name, out_shape=out_shape, grid=grid, in_specs=[a_spec, b_spec] + extra_specs, out_specs=out_specs,
        scratch_shapes=scratch,
        compiler_params=_params(("arbitrary" if norm_bwd is not None else "parallel", "parallel", "arbitrary")),
    )(a, b, *extra)


def _rms_fwd(x, g, *, name, out_dtype=BF16):
    S, D = x.shape
    tr = _pick(S, (512, 256, 128))

    def body(x_ref, g_ref, o_ref):
        o_ref[...] = _rms(x_ref[...], g_ref[...]).astype(out_dtype)

    return pl.pallas_call(
        body, name=name, out_shape=jax.ShapeDtypeStruct((S, D), out_dtype), grid=(S // tr,),
        in_specs=[pl.BlockSpec((tr, D), lambda i: (i, 0)), pl.BlockSpec((1, D), lambda i: (0, 0))],
        out_specs=pl.BlockSpec((tr, D), lambda i: (i, 0)),
        compiler_params=_params(("parallel",)),
    )(x, g.reshape(1, D))


def _rms_bwd(x, g, dy, dres, *, name):
    S, D = x.shape
    tr = _pick(S, (512, 256, 128))

    def body(x_ref, g_ref, dy_ref, *rest):
        dx_ref, dxb_ref, dg_ref = rest[-3], rest[-2], rest[-1]
        dx, part = _rms_vjp(x_ref[...], g_ref[...], dy_ref[...].astype(F32), None if dres is None else rest[0][...])
        dx_ref[...] = dx
        dxb_ref[...] = dx.astype(BF16)

        @pl.when(pl.program_id(0) == 0)
        def _():
            dg_ref[...] = part

        @pl.when(pl.program_id(0) > 0)
        def _():
            dg_ref[...] += part

    row = pl.BlockSpec((tr, D), lambda i: (i, 0))
    vec = pl.BlockSpec((1, D), lambda i: (0, 0))
    ins = [x, g.reshape(1, D), dy] + ([dres] if dres is not None else [])
    return pl.pallas_call(
        body, name=name,
        out_shape=(jax.ShapeDtypeStruct((S, D), F32), jax.ShapeDtypeStruct((S, D), BF16), jax.ShapeDtypeStruct((1, D), F32)),
        grid=(S // tr,),
        in_specs=[row, vec, row] + ([row] if dres is not None else []),
        out_specs=(row, row, vec),
        compiler_params=_params(("arbitrary",)),
    )(*ins)


def _loss_head(x, g, target, *, name):
    S, D = x.shape
    tr = _pick(S, (512, 256, 128))

    def body(x_ref, g_ref, t_ref, l_ref, dx_ref, dxb_ref, dg_ref):
        x_ = x_ref[...]
        g_ = g_ref[...]
        rstd = lax.rsqrt(jnp.mean(x_ * x_, axis=-1, keepdims=True) + EPS)
        xh = x_ * rstd
        err = xh * g_ - t_ref[...]
        lpart = (0.5 / D) * jnp.sum(jnp.sum(err * err, axis=-1, keepdims=True), axis=0, keepdims=True)
        dy = err * (1.0 / D)
        gdy = dy * g_
        dx = (gdy - xh * jnp.mean(gdy * xh, axis=-1, keepdims=True)) * rstd
        dx_ref[...] = dx
        dxb_ref[...] = dx.astype(BF16)
        gpart = jnp.sum(dy * xh, axis=0, keepdims=True)

        @pl.when(pl.program_id(0) == 0)
        def _():
            dg_ref[...] = gpart
            l_ref[...] = lpart

        @pl.when(pl.program_id(0) > 0)
        def _():
            dg_ref[...] += gpart
            l_ref[...] += lpart

    row = pl.BlockSpec((tr, D), lambda i: (i, 0))
    vec = pl.BlockSpec((1, D), lambda i: (0, 0))
    return pl.pallas_call(
        body, name=name,
        out_shape=(jax.ShapeDtypeStruct((1, 1), F32), jax.ShapeDtypeStruct((S, D), F32), jax.ShapeDtypeStruct((S, D), BF16),
                   jax.ShapeDtypeStruct((1, D), F32)),
        grid=(S // tr,),
        in_specs=[row, vec, row],
        out_specs=(pl.BlockSpec((1, 1), lambda i: (0, 0)), row, row, vec),
        compiler_params=_params(("arbitrary",)),
    )(x, g.reshape(1, D), target)


def _mask_of(mask, tq, tk, keys_first=False):
    shape, q_axis = ((tk, tq), 1) if keys_first else ((tq, tk), 0)
    qpos = lax.broadcasted_iota(jnp.int32, shape, q_axis)
    kpos = lax.broadcasted_iota(jnp.int32, shape, 1 - q_axis)
    if mask == 'causal':
        return kpos <= qpos
    return kpos <= (qpos | (CHUNK - 1))


LANES = 128
LOG2E = 1.4426950408889634


def _lane_group(j, w, width):
    lane = lax.broadcasted_iota(jnp.int32, (1, width), 1)
    return (lane >= j * w) & (lane < (j + 1) * w)


def _only(x, j, w):
    if w == x.shape[1]:
        return x
    return jnp.where(_lane_group(j, w, x.shape[1]), x, jnp.zeros_like(x))


def _side_by_side(xs):
    return xs[0] if len(xs) == 1 else jnp.concatenate(xs, axis=1)


def _on_top(xs):
    return xs[0] if len(xs) == 1 else jnp.concatenate(xs, axis=0)


def _stacked(x, hp, w):
    return _on_top([_only(x, j, w) for j in range(hp)])


def _col_block(entry, rows, idx):
    arr, off, width = entry
    return pl.BlockSpec((rows, width), lambda i, j, o=off // width: (idx(i, j), o))


def _attn_fwd(qk, v, H, cq, ck, *, scale, mask, name, rider=None):
    Sq, Sk = qk[0][0][0].shape[0], v[0].shape[0]
    dv = v[2] // H
    w0 = qk[0][2]
    hp = LANES // w0
    G = H // hp
    assert dv == w0 and not qk[0][3] and all(sh and H * w == LANES for _, _, w, sh in qk[1:])
    tq = _pick(Sq, (512, 256, 128))
    tk = tq if mask else _pick(Sk, (512, 256, 128))
    nq, nk = Sq // tq, Sk // tk
    bias = cq is not None
    npart = len(qk)

    def body(*refs):
        refs = split(refs)
        q_refs, k_refs = refs[0:2 * npart:2], refs[1:2 * npart:2]
        v_ref = refs[2 * npart]
        cq_ref, ck_ref = (refs[2 * npart + 1], refs[2 * npart + 2]) if bias else (None, None)
        o_ref, lse_ref, m_s, l_s, acc_s = refs[-5:]
        qi, ki = pl.program_id(0), pl.program_id(1)

        @pl.when(ki == 0)
        def _():
            m_s[...] = jnp.full(m_s.shape, NEG, F32)
            l_s[...] = jnp.zeros(l_s.shape, F32)
            acc_s[...] = jnp.zeros(acc_s.shape, F32)

        def rows_of(vals):
            return _on_top([jnp.broadcast_to(r, (w0, tq)) for r in vals])

        def compute(masked):
            keep = _mask_of(mask, tq, tk, keys_first=True) if masked else None
            for g in range(G):
                lanes = slice(g * LANES, (g + 1) * LANES)
                q128, k128, v128 = q_refs[0][:, lanes], k_refs[0][:, lanes], v_ref[:, lanes]
                ps, alphas = [], []
                extras = list(zip(qk, q_refs, k_refs))[1:]
                k_all = _side_by_side([k128] + [k_ref[...] for _, _, k_ref in extras])
                for j in range(hp):
                    h = g * hp + j
                    q_all = _side_by_side([_only(q128, j, w0)] + [_only(q_ref[...], h, w) for (_, _, w, _), q_ref, _ in extras])
                    s = _dot(k_all, q_all, NT) * scale
                    if bias:
                        s = s + (cq_ref[h:h + 1, :] - ck_ref[:, h:h + 1])
                    if masked:
                        s = jnp.where(keep, s, NEG)
                    m_prev = m_s[h:h + 1, :]
                    m_new = jnp.maximum(m_prev, jnp.max(s, axis=0, keepdims=True))
                    alpha = jnp.exp(m_prev - m_new)
                    p = jnp.exp(s - m_new)
                    l_s[h:h + 1, :] = alpha * l_s[h:h + 1, :] + jnp.sum(p, axis=0, keepdims=True)
                    m_s[h:h + 1, :] = m_new
                    ps.append(p.astype(BF16))
                    alphas.append(alpha)
                acc_s[g] = rows_of(alphas) * acc_s[g] + _dot(_stacked(v128, hp, w0), _on_top(ps), TN)

        if mask is None:
            compute(False)
        else:
            pl.when(ki < qi)(lambda: compute(False))
            pl.when(ki == qi)(lambda: compute(True))

        @pl.when(ki == ((nk - 1) if mask is None else qi))
        def _():
            for g in range(G):
                norm = acc_s[g] / rows_of([l_s[g * hp + j:g * hp + j + 1, :] for j in range(hp)])
                o_ref[:, g * LANES:(g + 1) * LANES] = norm.T.astype(BF16)
            lse_ref[...] = jnp.zeros(lse_ref.shape, F32)
            lse_ref[0:H, :] = m_s[0:H, :] + jnp.log(l_s[0:H, :])

    q_idx = lambda i, j: i
    k_idx = (lambda i, j: jnp.minimum(i, j)) if mask else (lambda i, j: j)
    ins, in_specs = [], []
    for q_e, k_e, _, _ in qk:
        ins += [q_e[0], k_e[0]]
        in_specs += [_col_block(q_e, tq, q_idx), _col_block(k_e, tk, k_idx)]
    ins.append(v[0])
    in_specs.append(_col_block(v, tk, k_idx))
    if bias:
        in_specs += [pl.BlockSpec((8, tq), lambda i, j: (0, i)), pl.BlockSpec((tk, 8), lambda i, j: (k_idx(i, j), 0))]
        ins += [cq, ck]
    r_ins, r_in_specs, r_outs, r_out_specs, r_scratch, split = _carry(
        rider, len(ins), 2, lambda: (pl.program_id(0) == 0) & (pl.program_id(1) == 0),
        lambda: (pl.program_id(0) == nq - 1) & (pl.program_id(1) == nk - 1))
    res = pl.pallas_call(
        body, name=name,
        out_shape=(jax.ShapeDtypeStruct((Sq, H * dv), BF16), jax.ShapeDtypeStruct((8, Sq), F32), *r_outs),
        grid=(nq, nk), in_specs=in_specs + r_in_specs,
        out_specs=(pl.BlockSpec((tq, H * dv), lambda i, j: (i, 0)), pl.BlockSpec((8, tq), lambda i, j: (0, i)), *r_out_specs),
        scratch_shapes=[pltpu.VMEM((8, tq), F32), pltpu.VMEM((8, tq), F32), pltpu.VMEM((G, LANES, tq), F32)] + r_scratch,
        compiler_params=_params(("arbitrary", "arbitrary")) if rider else _params(("parallel", "arbitrary")),
    )(*ins, *r_ins)
    return (res[0], res[1], rider.post(res[2:])) if rider else res


def _attn_bwd(qk, v, H, o, do, lse, cq, ck, *, scale, mask, name, rider=None):
    Sq, Sk = qk[0][0][0].shape[0], v[0].shape[0]
    dv = v[2] // H
    w0 = qk[0][2]
    hp = LANES // w0
    G = H // hp
    tq = _pick(Sq, (512, 256, 128))
    tk = tq if mask else _pick(Sk, (512, 256, 128))
    nq, nk = Sq // tq, Sk // tk
    bias = cq is not None
    npart = len(qk)
    n_in = 2 * npart + 4 + (2 if bias else 0)

    def body(*refs):
        refs = split(refs)
        q_refs, k_refs = refs[0:2 * npart:2], refs[1:2 * npart:2]
        v_ref, o_ref, do_ref, lse_ref = refs[2 * npart:2 * npart + 4]
        cq_ref, ck_ref = (refs[2 * npart + 4], refs[2 * npart + 5]) if bias else (None, None)
        outs = refs[n_in:]
        dq_refs, dk_refs, dv_ref = outs[:npart], outs[npart:2 * npart], outs[2 * npart]
        dck_ref, dcq_ref = (outs[2 * npart + 1], outs[2 * npart + 2]) if bias else (None, None)
        dk_accs, dv_acc = refs[-(npart + 1):-1], refs[-1]
        ki, qi = pl.program_id(0), pl.program_id(1)
        first_q = ki if mask else 0

        @pl.when((ki == 0) & (qi == 0))
        def _():
            for r in dq_refs:
                r[...] = jnp.zeros(r.shape, F32)
            if bias:
                dcq_ref[...] = jnp.zeros(dcq_ref.shape, F32)

        @pl.when(qi == first_q)
        def _():
            for r in dk_accs:
                r[...] = jnp.zeros(r.shape, F32)
            dv_acc[...] = jnp.zeros(dv_acc.shape, F32)
            if bias:
                dck_ref[...] = jnp.zeros(dck_ref.shape, F32)

        def compute(masked):
            keep = _mask_of(mask, tq, tk, keys_first=True) if masked else None
            rows = pl.ds(pl.multiple_of(qi * tq, tq), tq)
            extras = list(zip(qk, q_refs, k_refs, dq_refs, dk_accs))[1:]
            for g in range(G):
                lanes = slice(g * LANES, (g + 1) * LANES)
                q128, k128, v128 = q_refs[0][:, lanes], k_refs[0][:, lanes], v_ref[:, lanes]
                do128, o128 = do_ref[:, lanes], o_ref[:, lanes]
                prod = do128.astype(F32) * o128.astype(F32)
                ps, dss = [], []
                k_all = _side_by_side([k128] + [e[2][...] for e in extras])
                for j in range(hp):
                    h = g * hp + j
                    q_all = _side_by_side([_only(q128, j, w0)] + [_only(e[1][...], h, e[0][2]) for e in extras])
                    s = _dot(k_all, q_all, NT) * (scale * LOG2E)
                    if bias:
                        s = s - ck_ref[:, h:h + 1] * LOG2E
                    if masked:
                        s = jnp.where(keep, s, NEG)
                    row = lse_ref[h:h + 1, :] - cq_ref[h:h + 1, :] if bias else lse_ref[h:h + 1, :]
                    p = jnp.exp2(s - row * LOG2E)
                    dp = _dot(v128, _only(do128, j, w0), NT)
                    delta = jnp.sum(_only(prod, j, w0), axis=1, keepdims=True).T
                    ds = p * (dp - delta)
                    if bias:
                        dck_ref[:, h:h + 1] -= jnp.sum(ds, axis=1, keepdims=True)
                        dcq_ref[h:h + 1, rows] += jnp.sum(ds, axis=0, keepdims=True)
                    ps.append(p.astype(BF16))
                    dss.append((ds * scale).astype(BF16))
                for (_, _, w, _), q_ref, k_ref, dq_ref, dk_acc in extras:
                    heads = range(g * hp, (g + 1) * hp)
                    dk_acc[...] += _dot(_side_by_side(dss), _on_top([_only(q_ref[...], h, w) for h in heads]), NN)
                    dq_ref[rows, :] += _dot(_on_top(dss), _on_top([_only(k_ref[...], h, w) for h in heads]), TN)
                dv_acc[:, lanes] += _dot(_side_by_side(ps), _stacked(do128, hp, w0), NN)
                dk_accs[0][:, lanes] += _dot(_side_by_side(dss), _stacked(q128, hp, w0), NN)
                dq_refs[0][rows, lanes] += _dot(_on_top(dss), _stacked(k128, hp, w0), TN)

        if mask is None:
            compute(False)
        else:
            pl.when(qi > ki)(lambda: compute(False))
            pl.when(qi == ki)(lambda: compute(True))

        @pl.when(qi == nq - 1)
        def _():
            for r, acc in zip(dk_refs, dk_accs):
                r[...] = acc[...]
            dv_ref[...] = dv_acc[...]

    q_idx = (lambda j, i: jnp.maximum(i, j)) if mask else (lambda j, i: i)
    k_idx = lambda j, i: j
    ins, in_specs, dq_shapes, dq_specs, dk_shapes, dk_specs, scratch = [], [], [], [], [], [], []
    for q_e, k_e, w, shared in qk:
        ins += [q_e[0], k_e[0]]
        in_specs += [_col_block(q_e, tq, q_idx), _col_block(k_e, tk, k_idx)]
        dq_shapes.append(jax.ShapeDtypeStruct((Sq, H * w), F32))
        dq_specs.append(pl.BlockSpec((Sq, H * w), lambda j, i: (0, 0)))
        kw = k_e[2]
        dk_shapes.append(jax.ShapeDtypeStruct((Sk, kw), F32))
        dk_specs.append(pl.BlockSpec((tk, kw), lambda j, i: (j, 0)))
        scratch.append(pltpu.VMEM((tk, kw), F32))
    row_q = lambda width: pl.BlockSpec((tq, width), lambda j, i: (q_idx(j, i), 0))
    per_q = pl.BlockSpec((8, tq), lambda j, i: (0, q_idx(j, i)))
    ins += [v[0], o, do, lse]
    in_specs += [_col_block(v, tk, k_idx), row_q(H * dv), row_q(H * dv), per_q]
    out_shape = dq_shapes + dk_shapes + [jax.ShapeDtypeStruct((Sk, H * dv), F32)]
    out_specs = dq_specs + dk_specs + [pl.BlockSpec((tk, H * dv), lambda j, i: (j, 0))]
    if bias:
        in_specs += [per_q, pl.BlockSpec((tk, 8), lambda j, i: (j, 0))]
        ins += [cq, ck]
        out_shape += [jax.ShapeDtypeStruct((Sk, 8), F32), jax.ShapeDtypeStruct((8, Sq), F32)]
        out_specs += [pl.BlockSpec((tk, 8), lambda j, i: (j, 0)), pl.BlockSpec((8, Sq), lambda j, i: (0, 0))]
    scratch.append(pltpu.VMEM((tk, H * dv), F32))
    n_out = len(out_shape)
    r_ins, r_in_specs, r_outs, r_out_specs, r_scratch, split = _carry(
        rider, len(ins), n_out, lambda: (pl.program_id(0) == 0) & (pl.program_id(1) == 0),
        lambda: (pl.program_id(0) == nk - 1) & (pl.program_id(1) == nq - 1))
    res = pl.pallas_call(
        body, name=name, out_shape=tuple(out_shape + r_outs), grid=(nk, nq), in_specs=in_specs + r_in_specs,
        out_specs=tuple(out_specs + r_out_specs), scratch_shapes=scratch + r_scratch,
        compiler_params=_params(("arbitrary", "arbitrary")),
    )(*ins, *r_ins)
    own = (list(res[:npart]), list(res[npart:2 * npart]), res[2 * npart]) + tuple(res[2 * npart + 1:n_out])
    return own + (rider.post(res[n_out:]),) if rider else own


def _split3_dot(x, t):
    hi = x.astype(BF16)
    r1 = x - hi.astype(F32)
    mid = r1.astype(BF16)
    lo = (r1 - mid.astype(F32)).astype(BF16)
    return _dot(hi, t, NN) + _dot(mid, t, NN) + _dot(lo, t, NN)


def _fox_cum_fwd(ff_t, b, *, name):
    _, S = ff_t.shape
    tb = _pick(S, (512, 256, 128))

    def body(f_ref, b_ref, o_ref, carry):
        @pl.when(pl.program_id(0) == 0)
        def _():
            carry[...] = jnp.zeros(carry.shape, F32)

        lf = _log_sigmoid(f_ref[...] + b_ref[...])
        o_ref[...] = _split3_dot(lf, _tri(tb, False)) + carry[...]
        carry[...] += jnp.sum(lf, axis=1, keepdims=True)

    return pl.pallas_call(
        body, name=name, out_shape=jax.ShapeDtypeStruct((8, S), F32), grid=(S // tb,),
        in_specs=[pl.BlockSpec((8, tb), lambda i: (0, i)), pl.BlockSpec((8, 1), lambda i: (0, 0))],
        out_specs=pl.BlockSpec((8, tb), lambda i: (0, i)),
        scratch_shapes=[pltpu.VMEM((8, 1), F32)],
        compiler_params=_params(("arbitrary",)),
    )(ff_t, b)


def _fox_cum_bwd(ff_t, b, dcum_t, *, name):
    _, S = ff_t.shape
    tb = _pick(S, (512, 256, 128))
    nb = S // tb

    def body(f_ref, b_ref, dc_ref, df_ref, db_ref, carry):
        @pl.when(pl.program_id(0) == 0)
        def _():
            carry[...] = jnp.zeros(carry.shape, F32)
            db_ref[...] = jnp.zeros(db_ref.shape, F32)

        dc = dc_ref[...]
        dlf = _split3_dot(dc, _tri(tb, True)) + carry[...]
        carry[...] += jnp.sum(dc, axis=1, keepdims=True)
        df = dlf * _sigmoid(-(f_ref[...] + b_ref[...]))
        df_ref[...] = df
        db_ref[...] += jnp.sum(df, axis=1, keepdims=True)

    rev = lambda i: (0, nb - 1 - i)
    return pl.pallas_call(
        body, name=name,
        out_shape=(jax.ShapeDtypeStruct((8, S), F32), jax.ShapeDtypeStruct((8, 1), F32)), grid=(nb,),
        in_specs=[pl.BlockSpec((8, tb), rev), pl.BlockSpec((8, 1), lambda i: (0, 0)), pl.BlockSpec((8, tb), rev)],
        out_specs=(pl.BlockSpec((8, tb), rev), pl.BlockSpec((8, 1), lambda i: (0, 0))),
        scratch_shapes=[pltpu.VMEM((8, 1), F32)],
        compiler_params=_params(("arbitrary",)),
    )(ff_t, b, dcum_t)


GLA_W = GLA_HEADS * GLA_DK
GLA_BLOCK_CHUNKS = 4


def _same_chunk(n, lower):
    r = lax.broadcasted_iota(jnp.int32, (n, n), 0)
    c = lax.broadcasted_iota(jnp.int32, (n, n), 1)
    same = (r | (CHUNK - 1)) == (c | (CHUNK - 1))
    return jnp.where(same & (r >= c) if lower else same, 1.0, 0.0).astype(BF16)


def _chunk_mix(x, t, transpose):
    hi, lo = _split2(x)
    dims = TN if transpose else NN
    return _dot(t, hi, dims) + _dot(t, lo, dims)


@jax.custom_vjp
def chunk_cumsum(x):
    return _chunk_mix(x, _same_chunk(x.shape[0], True), False)


chunk_cumsum.defvjp(lambda x: (chunk_cumsum(x), None), lambda _, g: (_chunk_mix(g, _same_chunk(g.shape[0], True), True),))


@jax.custom_vjp
def chunk_total(x):
    return _chunk_mix(x, _same_chunk(x.shape[0], False), False)


chunk_total.defvjp(lambda x: (chunk_total(x), None), lambda _, g: (_chunk_mix(g, _same_chunk(g.shape[0], False), False),))


def _gla_block(q, k, zsm, wg, bg, go, vs, rs, states):
    n_chunks = q.shape[0] // CHUNK
    la = _log_sigmoid(bdot(zsm, wg) + bg) * (1.0 / GLA_TAU)
    end = chunk_total(la)
    kd = k * jnp.exp(end - chunk_cumsum(la))
    a = jnp.exp(end)
    qs = q * (GLA_DK ** -0.5)
    lane = lax.broadcasted_iota(jnp.int32, (1, GLA_W), 1)
    outs, new_states = [], []
    for h in range(GLA_HEADS):
        kdh = kd * jnp.where((lane >= h * GLA_DK) & (lane < (h + 1) * GLA_DK), 1.0, 0.0)
        st, o = states[h], []
        for c in range(n_chunks):
            rows = slice(c * CHUNK, (c + 1) * CHUNK)
            st = st * a[c * CHUNK:c * CHUNK + 1] + bdot_tn(vs[h][rows], kdh[rows])
            o.append(bdot_nt(qs[rows], st))
        o = _rms(jnp.concatenate(o, axis=0), go)
        outs.append(o * (rs[h] * _sigmoid(rs[h])))
        new_states.append(st)
    return outs, new_states


def _gla_fwd(z, zsm, wg, bg, go, cols, *, name):
    S = z.shape[0]
    rb = GLA_BLOCK_CHUNKS * CHUNK
    nb = S // rb
    cq, ckk, cv, cr = cols
    H = GLA_HEADS

    def body(q_ref, k_ref, zsm_ref, wg_ref, bg_ref, go_ref, *rest):
        v_refs, r_refs = rest[:H], rest[H:2 * H]
        o_ref, st_ref, state = rest[2 * H], rest[2 * H + 1], rest[2 * H + 2]

        @pl.when(pl.program_id(0) == 0)
        def _():
            state[...] = jnp.zeros(state.shape, F32)

        states = [state[h] for h in range(H)]
        for h in range(H):
            st_ref[0, h] = states[h]
        outs, new_states = _gla_block(
            q_ref[...].astype(F32), k_ref[...].astype(F32), zsm_ref[...], wg_ref[...], bg_ref[...], go_ref[...],
            [v_refs[h][...].astype(F32) for h in range(H)], [r_refs[h][...].astype(F32) for h in range(H)], states)
        for h in range(H):
            o_ref[:, h * GLA_DV:(h + 1) * GLA_DV] = outs[h].astype(BF16)
            state[h] = new_states[h]

    def col(width, off):
        return pl.BlockSpec((rb, width), lambda i, o=off // width: (i, o))

    full = lambda shp: pl.BlockSpec(shp, lambda i: (0,) * len(shp))
    in_specs = [col(GLA_W, cq), col(GLA_W, ckk), pl.BlockSpec((rb, 128), lambda i: (i, 0)),
                full((128, GLA_W)), full((1, GLA_W)), full((1, GLA_DV))]
    in_specs += [col(GLA_DV, cv + h * GLA_DV) for h in range(H)] + [col(GLA_DV, cr + h * GLA_DV) for h in range(H)]
    return pl.pallas_call(
        body, name=name,
        out_shape=(jax.ShapeDtypeStruct((S, H * GLA_DV), BF16), jax.ShapeDtypeStruct((nb, H, GLA_DV, GLA_W), F32)),
        grid=(nb,), in_specs=in_specs,
        out_specs=(pl.BlockSpec((rb, H * GLA_DV), lambda i: (i, 0)),
                   pl.BlockSpec((1, H, GLA_DV, GLA_W), lambda i: (i, 0, 0, 0))),
        scratch_shapes=[pltpu.VMEM((H, GLA_DV, GLA_W), F32)],
        compiler_params=_params(("arbitrary",)),
    )(z, z, zsm, wg, bg, go, *([z] * (2 * H)))


def _gla_bwd(z, zsm, wg, bg, go, states, do, cols, *, name):
    S = z.shape[0]
    rb = GLA_BLOCK_CHUNKS * CHUNK
    nb = S // rb
    cq, ckk, cv, cr = cols
    H = GLA_HEADS

    def body(q_ref, k_ref, zsm_ref, wg_ref, bg_ref, go_ref, st_ref, do_ref, *rest):
        v_refs, r_refs = rest[:H], rest[H:2 * H]
        dq_ref, dk_ref, dv_ref, dr_ref, dzsm_ref, dwg_ref, dbg_ref, dgo_ref, dstate = rest[2 * H:]

        @pl.when(pl.program_id(0) == 0)
        def _():
            dstate[...] = jnp.zeros(dstate.shape, F32)
            dwg_ref[...] = jnp.zeros(dwg_ref.shape, F32)
            dbg_ref[...] = jnp.zeros(dbg_ref.shape, F32)
            dgo_ref[...] = jnp.zeros(dgo_ref.shape, F32)

        prim = (q_ref[...].astype(F32), k_ref[...].astype(F32), zsm_ref[...], wg_ref[...], bg_ref[...], go_ref[...],
                [v_refs[h][...].astype(F32) for h in range(H)], [r_refs[h][...].astype(F32) for h in range(H)],
                [st_ref[0, h] for h in range(H)])
        _, vjp = jax.vjp(_gla_block, *prim)
        douts = [do_ref[:, h * GLA_DV:(h + 1) * GLA_DV].astype(F32) for h in range(H)]
        dq, dk, dzs, dwg, dbg, dgo, dvs, drs, dsts = vjp((douts, [dstate[h] for h in range(H)]))
        dq_ref[...] = dq.astype(BF16)
        dk_ref[...] = dk.astype(BF16)
        dzsm_ref[...] = dzs
        dwg_ref[...] += dwg
        dbg_ref[...] += dbg
        dgo_ref[...] += dgo
        for h in range(H):
            dv_ref[:, h * GLA_DV:(h + 1) * GLA_DV] = dvs[h].astype(BF16)
            dr_ref[:, h * GLA_DV:(h + 1) * GLA_DV] = drs[h].astype(BF16)
            dstate[h] = dsts[h]

    rev = lambda i: nb - 1 - i

    def col(width, off):
        return pl.BlockSpec((rb, width), lambda i, o=off // width: (rev(i), o))

    full = lambda shp: pl.BlockSpec(shp, lambda i: (0,) * len(shp))
    rowb = lambda w: pl.BlockSpec((rb, w), lambda i: (rev(i), 0))
    in_specs = [col(GLA_W, cq), col(GLA_W, ckk), rowb(128), full((128, GLA_W)), full((1, GLA_W)), full((1, GLA_DV)),
                pl.BlockSpec((1, H, GLA_DV, GLA_W), lambda i: (rev(i), 0, 0, 0)), rowb(H * GLA_DV)]
    in_specs += [col(GLA_DV, cv + h * GLA_DV) for h in range(H)] + [col(GLA_DV, cr + h * GLA_DV) for h in range(H)]
    return pl.pallas_call(
        body, name=name,
        out_shape=(jax.ShapeDtypeStruct((S, GLA_W), BF16), jax.ShapeDtypeStruct((S, GLA_W), BF16),
                   jax.ShapeDtypeStruct((S, H * GLA_DV), BF16), jax.ShapeDtypeStruct((S, H * GLA_DV), BF16),
                   jax.ShapeDtypeStruct((S, 128), F32), jax.ShapeDtypeStruct((128, GLA_W), F32),
                   jax.ShapeDtypeStruct((1, GLA_W), F32), jax.ShapeDtypeStruct((1, GLA_DV), F32)),
        grid=(nb,), in_specs=in_specs,
        out_specs=(rowb(GLA_W), rowb(GLA_W), rowb(H * GLA_DV), rowb(H * GLA_DV), rowb(128),
                   full((128, GLA_W)), full((1, GLA_W)), full((1, GLA_DV))),
        scratch_shapes=[pltpu.VMEM((H, GLA_DV, GLA_W), F32)],
        compiler_params=_params(("arbitrary",)),
    )(z, z, zsm, wg, bg, go, states, do, *([z] * (2 * H)))


def _row_spec(entry, tr):
    if isinstance(entry, tuple):
        arr, width, off = entry
        return arr, pl.BlockSpec((tr, width), lambda i, o=off // width: (i, o))
    return entry, pl.BlockSpec((tr, entry.shape[1]), lambda i: (i, 0))


def _stage_fwd(fn, rows, consts, outs, *, name, tr=None):
    first = rows[0][0] if isinstance(rows[0], tuple) else rows[0]
    S = first.shape[0]
    tr = tr or _pick(S, (512, 256, 128))
    arrs, specs = zip(*[_row_spec(e, tr) for e in rows])
    nr, nc = len(rows), len(consts)

    def body(*refs):
        vals = [r[...].astype(F32) for r in refs[:nr + nc]]
        res = fn(*vals)
        for o_ref, val in zip(refs[nr + nc:], res):
            o_ref[...] = val.astype(o_ref.dtype)

    cspecs = [pl.BlockSpec(c.shape, lambda i, n=c.ndim: (0,) * n) for c in consts]
    return pl.pallas_call(
        body, name=name,
        out_shape=tuple(jax.ShapeDtypeStruct((S, w), dt) for w, dt in outs), grid=(S // tr,),
        in_specs=list(specs) + cspecs,
        out_specs=tuple(pl.BlockSpec((tr, w), lambda i: (i, 0)) for w, _ in outs),
        compiler_params=_params(("parallel",)),
    )(*arrs, *consts)


def _stage_bwd(fn, rows, consts, cts, n_diff, drow_dtypes, *, name, tr=None, lead=None):
    first = rows[0][0] if isinstance(rows[0], tuple) else rows[0]
    S = first.shape[0]
    tr = tr or _pick(S, (512, 256, 128))
    arrs, specs = zip(*[_row_spec(e, tr) for e in rows])
    widths = [e[1] if isinstance(e, tuple) else e.shape[1] for e in rows]
    nr, nc, nt = len(rows), len(consts), len(cts)
    n_lead, lead_width = lead or (1, widths[0])
    n_rows_out = n_diff - n_lead + 1

    def body(*refs):
        vals = [r[...].astype(F32) for r in refs[:nr + nc]]
        ct = [r[...].astype(F32) for r in refs[nr + nc:nr + nc + nt]]
        drow_refs = refs[nr + nc + nt:nr + nc + nt + n_rows_out]
        dconst_refs = refs[nr + nc + nt + n_rows_out:]
        rest_rows = vals[n_diff:nr]

        def f(diff_rows, cs):
            return tuple(fn(*diff_rows, *rest_rows, *cs))

        _, vjp = jax.vjp(f, vals[:n_diff], vals[nr:])
        drows, dcs = vjp(tuple(ct))
        off = 0
        for val, w in zip(drows[:n_lead], widths):
            drow_refs[0][:, off:off + w] = val.astype(drow_refs[0].dtype)
            off += w
        for r, val in zip(drow_refs[1:], drows[n_lead:]):
            r[...] = val.astype(r.dtype)
        first_step = pl.program_id(0) == 0
        for r, val in zip(dconst_refs, dcs):
            @pl.when(first_step)
            def _(r=r, val=val):
                r[...] = val

            @pl.when(jnp.logical_not(first_step))
            def _(r=r, val=val):
                r[...] += val

    cspecs = [pl.BlockSpec(c.shape, lambda i, n=c.ndim: (0,) * n) for c in consts]
    ctspecs = [pl.BlockSpec((tr, c.shape[1]), lambda i: (i, 0)) for c in cts]
    out_shape = [jax.ShapeDtypeStruct((S, lead_width), drow_dtypes[0])]
    out_shape += [jax.ShapeDtypeStruct((S, widths[j]), drow_dtypes[j]) for j in range(n_lead, n_diff)]
    out_shape += [jax.ShapeDtypeStruct(c.shape, F32) for c in consts]
    out_specs = [pl.BlockSpec((tr, sum(widths[:n_lead])), lambda i: (i, 0))]
    out_specs += [pl.BlockSpec((tr, widths[j]), lambda i: (i, 0)) for j in range(n_lead, n_diff)] + cspecs
    res = pl.pallas_call(
        body, name=name, out_shape=tuple(out_shape), grid=(S // tr,),
        in_specs=list(specs) + cspecs + ctspecs, out_specs=tuple(out_specs),
        compiler_params=_params(("arbitrary",)),
    )(*arrs, *consts, *cts)
    return list(res[:n_rows_out]), list(res[n_rows_out:])


def _mla_prep_fn(cq, ckv, kr, kr_sw, cos, sin, gq, gkv, wq_n, wq_r, wq_sw, wk, wv):
    hq = _rms(cq, gq)
    hkv = _rms(ckv, gkv)
    return (bdot(hq, wq_n), bdot(hq, wq_r) * cos + bdot(hq, wq_sw) * sin,
            bdot(hkv, wk), bdot(hkv, wv), kr * cos + kr_sw * sin)


def _merge_fn(g0, g1, g2, of, og, om, b0, b1, b2, wf, wg, wm):
    return (_sigmoid(g0 + b0) * bdot(of, wf) + _sigmoid(g1 + b1) * bdot(og, wg) + _sigmoid(g2 + b2) * bdot(om, wm),)


_IN_SIZES = (256, 256, 256, 4, 256, 256, 512, 16, 512, 256, 128, 32, 3072)
_IN_OFF = np.concatenate([[0], np.cumsum(_IN_SIZES)])
(_O_FQ, _O_FK, _O_FV, _O_FF, _O_GQ, _O_GK, _O_GV, _O_GLOW, _O_GR, _O_MQ, _O_MKV, _O_MKR, _O_ZG) = [int(o) for o in _IN_OFF[:-1]]
N_IN = int(_IN_OFF[-1])
_BIG_GROUPS = ((_O_ZG, 3072), (_O_GV, 512), (_O_GR, 512), (_O_FQ, 256), (_O_FK, 256), (_O_FV, 256),
               (_O_GQ, 256), (_O_GK, 256), (_O_MQ, 256), (_O_MKV, 128))
Z_GATE, Z_GV, Z_GR, Z_FQ, Z_FK, Z_FV, Z_GQ, Z_GK, Z_MQ, Z_MKV = [int(o) for o in
                                                                    np.concatenate([[0], np.cumsum([w for _, w in _BIG_GROUPS])])[:-1]]
N_BIG = sum(w for _, w in _BIG_GROUPS)
_HALF = MLA_ROPE // 2
_QK_HD = MLA_NOPE + MLA_ROPE
SM_FF, SM_GLOW, SM_KR, SM_KR_SW, N_SM = 0, 8, 128, 256, 384
N_PAD = N_BIG + N_SM
_IN_SEGS = ([(o, w, 1.0) for o, w in _BIG_GROUPS]
            + [(_O_FF, 4, 1.0), (None, SM_GLOW - 4, 0.0), (_O_GLOW, GLA_RANK, 1.0), (None, 128 - SM_GLOW - GLA_RANK, 0.0)]
            + [(_O_MKR, MLA_ROPE, 1.0)] * MLA_HEADS
            + [(_O_MKR + _HALF, _HALF, -1.0), (_O_MKR, _HALF, 1.0)] * MLA_HEADS)


def _cols(x, start, width):
    return lax.slice_in_dim(x, start, start + width, axis=x.ndim - 1)


def _pad_w_in(w):
    return jnp.concatenate([jnp.zeros(w.shape[:-1] + (n,), w.dtype) if src is None else
                            (_cols(w, src, n) if sign > 0 else -_cols(w, src, n)) for src, n, sign in _IN_SEGS], axis=-1)


def _unpad_w_in(g):
    groups = []
    for o, n in zip(_IN_OFF[:-1], _IN_SIZES):
        total, pos = None, 0
        for src, m, sign in _IN_SEGS:
            if src is not None and o <= src and src + m <= o + n:
                term = _cols(g, pos, m) if sign > 0 else -_cols(g, pos, m)
                if m != n:
                    term = jnp.pad(term, [(0, 0)] * (g.ndim - 1) + [(int(src - o), int(o + n - src - m))])
                total = term if total is None else total + term
            pos += m
        groups.append(total)
    return jnp.concatenate(groups, axis=-1)


def _take(x, idx):
    idx = np.asarray(idx)
    cuts = [0] + [i for i in range(1, len(idx)) if idx[i] != idx[i - 1] + 1] + [len(idx)]
    return jnp.concatenate([_cols(x, int(idx[a]), b - a) for a, b in zip(cuts[:-1], cuts[1:])], axis=1)


_UQ_NOPE = np.concatenate([np.arange(h * _QK_HD, h * _QK_HD + MLA_NOPE) for h in range(MLA_HEADS)])
_UQ_ROT = np.concatenate([np.arange(h * _QK_HD + MLA_NOPE, (h + 1) * _QK_HD) for h in range(MLA_HEADS)])
_UKV_PERM = np.concatenate(
    [np.concatenate([np.arange(h * 128, h * 128 + MLA_NOPE) for h in range(MLA_HEADS)]),
     np.concatenate([np.arange(h * 128 + MLA_NOPE, (h + 1) * 128) for h in range(MLA_HEADS)])])
_UKV_INV = np.argsort(_UKV_PERM)


def _rotary_partner(r):
    return jnp.concatenate([piece for h in range(MLA_HEADS) for piece in
                            (-_cols(r, h * MLA_ROPE + _HALF, _HALF), _cols(r, h * MLA_ROPE, _HALF))], axis=1)


def _uq_grad(dn, dr, dsw):
    dr = dr + jnp.concatenate([piece for h in range(MLA_HEADS) for piece in
                               (_cols(dsw, h * MLA_ROPE + _HALF, _HALF), -_cols(dsw, h * MLA_ROPE, _HALF))], axis=1)
    return jnp.concatenate([piece for h in range(MLA_HEADS) for piece in
                            (_cols(dn, h * MLA_NOPE, MLA_NOPE), _cols(dr, h * MLA_ROPE, MLA_ROPE))], axis=1)


def _rope_tables(S):
    inv = ROPE_BASE ** (-jnp.arange(_HALF, dtype=F32) / _HALF)
    ang = jnp.arange(S, dtype=F32)[:, None] * inv[None, :]
    return jnp.tile(jnp.cos(ang), (1, 2 * MLA_HEADS)), jnp.tile(jnp.sin(ang), (1, 2 * MLA_HEADS))


class _LayerParams:
    def __init__(self, rep, l):
        self.w, self.rep, self.l, self.made = {}, rep, l, {}

    def __getitem__(self, k):
        if k not in self.made:
            self.made[k] = self._make(k)
        return self.made[k]

    def _make(self, k):
        w, rep, l = self.w, self.rep, self.l
        if k == 'wg':
            return jnp.pad(w['w_gla_gate'], [(SM_GLOW, LANES - SM_GLOW - GLA_RANK), (0, 0)])
        if k in ('wq_n', 'wq_r'):
            return _take(w['w_mla_uq'], _UQ_NOPE if k == 'wq_n' else _UQ_ROT)
        if k == 'wq_sw':
            return _rotary_partner(self['wq_r'])
        if k in ('wk', 'wv'):
            return _take(w['w_mla_ukv'], _UKV_PERM[:256] if k == 'wk' else _UKV_PERM[256:])
        if k == 'b_f':
            return jnp.zeros((8, 1), F32).at[:FOX_HEADS, 0].set(rep['b_fox_forget'][l])
        if k == 'b_gate':
            return [rep['b_branch_gate'][l][i * 1024:(i + 1) * 1024].reshape(1, 1024) for i in range(3)]
        vec = {'bg': 'b_gla_gate', 'go': 'g_gla_out', 'gq': 'g_mla_q', 'gkv': 'g_mla_kv'}
        if k in vec:
            return rep[vec[k]][l].reshape(1, -1)
        return rep[k][l] if k in rep else w[k]


_GLA_COLS = (Z_GQ, Z_GK, Z_GV, Z_GR)
_MLA_OUTS = [(256, BF16), (128, BF16), (256, BF16), (256, BF16), (128, BF16)]


def _mla_rows(z, zsm, rope):
    return [(z, 256, Z_MQ), (z, 128, Z_MKV), (zsm, 128, SM_KR), (zsm, 128, SM_KR_SW), *rope]


def _mla_consts(p):
    return [p['gq'], p['gkv'], p['wq_n'], p['wq_r'], p['wq_sw'], p['wk'], p['wv']]


def _fox_qkv(z):
    return [((z, Z_FQ, 256), (z, Z_FK, 256), FOX_HD, False)], (z, Z_FV, 256)


def _mla_qkv(qn, qr, kn, vv, kr):
    return [((qn, 0, 256), (kn, 0, 256), MLA_NOPE, False), ((qr, 0, 128), (kr, 0, 128), MLA_ROPE, True)], (vv, 0, 256)


def _xa_qkv(qx, kvx):
    return [((qx, 0, 512), (kvx, 0, 512), XA_HD, False)], (kvx, 512, 512)


def _merge_rows(z, o_fox, o_gla, o_mla):
    return [(z, 1024, Z_GATE), (z, 1024, Z_GATE + 1024), (z, 1024, Z_GATE + 2048), o_fox, o_gla, o_mla]


def _merge_consts(p):
    return p['b_gate'] + [p['w_up_fox'], p['w_up_gla'], p['w_up_mla']]


def _carried(hooks, key, call, single=False):
    rider, sink = hooks.pop(key, (None, None))
    res = call(rider=rider)
    if rider is None:
        return res
    sink(res[-1])
    return res[0] if single else res[:-1]


def _layer_fwd(x0, mem, p, rope, l, hooks):
    S = x0.shape[0]
    sv = {'x0': x0}

    def mm(key, a, b, **kw):
        return _carried(hooks, (l, key), lambda rider: _mm(a, b, mode='nn', rider=rider, name=f"{key}_{l}", **kw), single=True)

    h1 = _rms_fwd(x0, p['g_mix'], name=f"rms_mix_{l}")
    z = mm('in_big', h1, p['w_in'], out_dtype=BF16, b_cols=(0, N_BIG))
    zsm = _mm(h1, p['w_in'], mode='nn', out_dtype=F32, b_cols=(N_BIG, N_SM), name=f"in_small_{l}")
    sv.update(h1=h1, z=z, zsm=zsm)
    ff_t = jnp.zeros((8, S), F32).at[:FOX_HEADS].set(zsm[:, SM_FF:SM_FF + FOX_HEADS].T)
    cum_t = _fox_cum_fwd(ff_t, p['b_f'], name=f"fox_cum_{l}")
    cum = cum_t.T
    o_fox, lse_f = _carried(hooks, (l, 'fox_fwd'), lambda rider: _attn_fwd(
        *_fox_qkv(z), FOX_HEADS, cum_t, cum, scale=FOX_HD ** -0.5, mask='causal', name=f"fox_fwd_{l}", rider=rider))
    sv.update(ff_t=ff_t, cum=cum, cum_t=cum_t, lse_f=lse_f, o_fox=o_fox)
    o_gla, states = _gla_fwd(z, zsm, p['wg'], p['bg'], p['go'], _GLA_COLS, name=f"gla_fwd_{l}")
    sv.update(o_gla=o_gla, states=states)
    mla = _stage_fwd(_mla_prep_fn, _mla_rows(z, zsm, rope), _mla_consts(p), _MLA_OUTS, name=f"mla_prep_{l}")
    o_mla, lse_m = _carried(hooks, (l, 'mla_fwd'), lambda rider: _attn_fwd(
        *_mla_qkv(*mla), MLA_HEADS, None, None, scale=_QK_HD ** -0.5, mask='chunk', name=f"mla_fwd_{l}", rider=rider))
    sv.update(mla=mla, lse_m=lse_m, o_mla=o_mla)
    (y,) = _stage_fwd(_merge_fn, _merge_rows(z, o_fox, o_gla, o_mla), _merge_consts(p), [(1024, BF16)], name=f"merge_{l}")
    x1 = mm('out_proj', y, p['w_out'], out_dtype=F32, residual=x0)
    sv.update(y=y, x1=x1)
    h2 = _rms_fwd(x1, p['g_xa'], name=f"rms_xa_{l}")
    hm = _rms_fwd(mem, p['g_mem'], name=f"rms_mem_{l}")
    qx = _mm(h2, p['w_xq'], mode='nn', out_dtype=BF16, name=f"xq_{l}")
    kvx = _mm(hm, p['w_xkv'], mode='nn', out_dtype=BF16, name=f"xkv_{l}")
    ox, lse_x = _carried(hooks, (l, 'xa_fwd'), lambda rider: _attn_fwd(
        *_xa_qkv(qx, kvx), XA_HEADS, None, None, scale=XA_HD ** -0.5, mask=None, name=f"xa_fwd_{l}", rider=rider))
    x2 = mm('xo', ox, p['w_xo'], out_dtype=F32, residual=x1)
    sv.update(h2=h2, hm=hm, qx=qx, kvx=kvx, lse_x=lse_x, ox=ox, x2=x2)
    h3 = _rms_fwd(x2, p['g_mlp'], name=f"rms_mlp_{l}")
    a = mm('mlp1', h3, p['w_mlp1'], out_dtype=BF16)
    x3 = mm('mlp2', a, p['w_mlp2'], out_dtype=F32, act='relu2', residual=x2)
    sv.update(h3=h3, a=a)
    return x3, sv


def _layer_bwd(dx3, dx3b, mem, p, rope, sv, l, hooks, half_done):
    S = dx3.shape[0]
    g = {}
    da = _mm(dx3b, p['w_mlp2'], mode='nt', out_dtype=BF16, drelu_of=sv['a'], name=f"d_mlp2_in_{l}")
    g['w_mlp2'] = _mm(sv['a'], dx3b, mode='tn', out_dtype=BF16, act='relu2', name=f"d_w_mlp2_{l}")
    dx2, dx2b, g['g_mlp'] = _mm(da, p['w_mlp1'], mode='nt', out_dtype=F32, norm_bwd=(sv['x2'], p['g_mlp'], dx3), tm=512,
                                name=f"d_mlp1_in_{l}")
    g['w_mlp1'] = _mm(sv['h3'], da, mode='tn', out_dtype=BF16, col_shards=N_DEV, name=f"d_w_mlp1_{l}")
    dox = _mm(dx2b, p['w_xo'], mode='nt', out_dtype=BF16, name=f"d_xo_in_{l}")
    g['w_xo'] = _mm(sv['ox'], dx2b, mode='tn', out_dtype=BF16, name=f"d_w_xo_{l}")
    (dqx,), (dkx,), dvx = _attn_bwd(*_xa_qkv(sv['qx'], sv['kvx']), XA_HEADS, sv['ox'], dox, sv['lse_x'], None, None,
                                    scale=XA_HD ** -0.5, mask=None, name=f"xa_bwd_{l}")
    dqx = dqx.astype(BF16)
    dkvx = jnp.concatenate([dkx, dvx], axis=1).astype(BF16)
    dx1, dx1b, g['g_xa'] = _mm(dqx, p['w_xq'], mode='nt', out_dtype=F32, norm_bwd=(sv['x1'], p['g_xa'], dx2), tm=512,
                               name=f"d_xq_in_{l}")
    g['w_xq'] = _mm(sv['h2'], dqx, mode='tn', out_dtype=BF16, name=f"d_w_xq_{l}")
    dhm = _mm(dkvx, p['w_xkv'], mode='nt', out_dtype=F32, name=f"d_xkv_in_{l}")
    g['w_xkv'] = _mm(sv['hm'], dkvx, mode='tn', out_dtype=BF16, name=f"d_w_xkv_{l}")
    _, _, g['g_mem'] = _rms_bwd(mem, p['g_mem'], dhm, None, name=f"d_rms_mem_{l}")
    dy = _mm(dx1b, p['w_out'], mode='nt', out_dtype=F32, name=f"d_out_in_{l}")
    g['w_out'] = _mm(sv['y'], dx1b, mode='tn', out_dtype=BF16, name=f"d_w_out_{l}")
    z, zsm = sv['z'], sv['zsm']
    (dz, do_fox, do_gla, do_mla), (db0, db1, db2, g['w_up_fox'], g['w_up_gla'], g['w_up_mla']) = _stage_bwd(
        _merge_fn, _merge_rows(z, sv['o_fox'], sv['o_gla'], sv['o_mla']), _merge_consts(p), [dy], 6, [BF16] * 6,
        lead=(3, N_PAD), name=f"merge_bwd_{l}")
    g['b_branch_gate'] = jnp.concatenate([db0, db1, db2], axis=1).reshape(-1)
    half_done(l, g)
    (dfq,), (dfk,), dfv, dck, dcq = _carried(hooks, (l, 'fox_bwd'), lambda rider: _attn_bwd(
        *_fox_qkv(z), FOX_HEADS, sv['o_fox'], do_fox, sv['lse_f'], sv['cum_t'], sv['cum'],
        scale=FOX_HD ** -0.5, mask='causal', name=f"fox_bwd_{l}", rider=rider))
    dff_t, db_f = _fox_cum_bwd(sv['ff_t'], p['b_f'], dcq + dck.T, name=f"fox_cum_bwd_{l}")
    g['b_fox_forget'] = db_f[:FOX_HEADS, 0]
    dgq, dgk, dgv, dgr, dzsm, dwg, dbg, dgo = _gla_bwd(z, zsm, p['wg'], p['bg'], p['go'], sv['states'], do_gla, _GLA_COLS,
                                                       name=f"gla_bwd_{l}")
    g['w_gla_gate'] = dwg[SM_GLOW:SM_GLOW + GLA_RANK]
    g['b_gla_gate'] = dbg.reshape(-1)
    g['g_gla_out'] = dgo.reshape(-1)
    (dmqn, dmqr), (dmkn, dmkr), dmv = _carried(hooks, (l, 'mla_bwd'), lambda rider: _attn_bwd(
        *_mla_qkv(*sv['mla']), MLA_HEADS, sv['o_mla'], do_mla, sv['lse_m'], None, None,
        scale=_QK_HD ** -0.5, mask='chunk', name=f"mla_bwd_{l}", rider=rider))
    (dcq, dckv, dkr, dkr_sw), (dgq_n, dgkv_n, dwq_n, dwq_r, dwq_sw, dwk, dwv) = _stage_bwd(
        _mla_prep_fn, _mla_rows(z, zsm, rope), _mla_consts(p), [dmqn, dmqr, dmkn, dmv, dmkr], 4, [BF16] * 4,
        name=f"mla_prep_bwd_{l}")
    g['g_mla_q'] = dgq_n.reshape(-1)
    g['g_mla_kv'] = dgkv_n.reshape(-1)
    g['w_mla_uq'] = _uq_grad(dwq_n, dwq_r, dwq_sw)
    g['w_mla_ukv'] = _take(jnp.concatenate([dwk, dwv], axis=1), _UKV_INV)
    dsm = dzsm + jnp.pad(dff_t[:FOX_HEADS].T, [(0, 0), (0, 128 - FOX_HEADS)])
    dz = lax.dynamic_update_slice(dz, jnp.concatenate(
        [dgv, dgr, dfq.astype(BF16), dfk.astype(BF16), dfv.astype(BF16), dgq, dgk, dcq, dckv, dsm.astype(BF16), dkr, dkr_sw],
        axis=1), (0, Z_GV))
    dx0, dx0b, g['g_mix'] = _mm(dz, p['w_in'], mode='nt', out_dtype=F32, norm_bwd=(sv['x0'], p['g_mix'], dx1), tm=512,
                                tk=N_PAD // 2, name=f"d_in_{l}")
    g['w_in'] = _mm(sv['h1'], dz, mode='tn', out_dtype=BF16, tn=N_PAD // 3, name=f"d_w_in_{l}")
    for n in ('g_mlp', 'g_mem', 'g_xa', 'g_mix'):
        g[n] = g[n].reshape(-1)
    return dx0, dx0b, g


def _local_step(x, mem, target, ps, g_final, hooks, half_done, layer_done):
    rope = _rope_tables(x.shape[0])
    saved = []
    for l, p in enumerate(ps):
        x, sv = _layer_fwd(x, mem, p, rope, l, hooks)
        saved.append(sv)
    loss, dx, dxb, dgf = _loss_head(x, g_final, target, name="loss_head")
    for l in reversed(range(len(ps))):
        dx, dxb, grads = _layer_bwd(dx, dxb, mem, ps[l], rope, saved[l], l, hooks, half_done)
        layer_done(l, grads)
    assert not hooks, f"exchanges without a carrier: {list(hooks)}"
    return loss, dx, dgf.reshape(-1)


_MESH_AXES = ("x", "y", "c")
_HBM = pl.BlockSpec(memory_space=pl.ANY)


N_CHIP = 4
_SLOT_ROWS = (2048, 1024, 512, 256, 128, 64, 32, 16, 8)


def _place():
    x, y, c = (lax.axis_index(n) for n in _MESH_AXES)
    return (x, y, c), (x, y, 1 - c), [(1 - x, y), (x, 1 - y), (1 - x, 1 - y)]


def _remote(src, dst, sems, k, to):
    return pltpu.make_async_remote_copy(src_ref=src, dst_ref=dst, send_sem=sems[0].at[k], recv_sem=sems[1].at[k],
                                        device_id=to, device_id_type=pl.DeviceIdType.MESH)


def _all_gather(x, *, name):
    def body(x_ref, o_ref, send_sems, recv_sems, local_sem):
        me, sib, chips = _place()
        c = me[2]
        sems = (send_sems, recv_sems)
        slot = lambda px, py, pc: o_ref.at[4 * px + 2 * py + pc]
        mine = pltpu.make_async_copy(x_ref, slot(*me), local_sem)
        mine.start()
        first = [_remote(x_ref, slot(*me), sems, 0, sib)]
        first += [_remote(x_ref, slot(*me), sems, 1 + j, (*chip, c)) for j, chip in enumerate(chips)]
        for cp in first:
            cp.start()
        passed = [_remote(slot(*chip, c), slot(*chip, c), sems, 4 + j, sib) for j, chip in enumerate(chips)]
        for j, chip in enumerate(chips):
            _remote(x_ref, slot(*chip, c), sems, 1 + j, me).wait_recv()
            passed[j].start()
        _remote(x_ref, slot(*sib), sems, 0, me).wait_recv()
        for j, chip in enumerate(chips):
            _remote(x_ref, slot(*chip, 1 - c), sems, 4 + j, me).wait_recv()
        for cp in first + passed:
            cp.wait_send()
        mine.wait()

    return pl.pallas_call(
        body, name=name, out_shape=jax.ShapeDtypeStruct((N_DEV,) + x.shape, x.dtype),
        in_specs=[_HBM], out_specs=_HBM,
        scratch_shapes=[pltpu.SemaphoreType.DMA((N_DEV - 1,)), pltpu.SemaphoreType.DMA((N_DEV - 1,)), pltpu.SemaphoreType.DMA],
        compiler_params=pltpu.CompilerParams(has_side_effects=True),
    )(x)


class _Rider:
    def __init__(self, inputs, out_shapes, scratch, start, finish, post):
        self.inputs, self.out_shapes, self.scratch = list(inputs), list(out_shapes), list(scratch)
        self.start, self.finish, self.post = start, finish, post


def _run_rider(rider, *, name):
    def body(*refs):
        rider.start(refs)
        rider.finish(refs)

    outs = pl.pallas_call(
        body, name=name, out_shape=tuple(rider.out_shapes), in_specs=[_HBM] * len(rider.inputs),
        out_specs=(_HBM,) * len(rider.out_shapes), scratch_shapes=rider.scratch,
        compiler_params=pltpu.CompilerParams(has_side_effects=True),
    )(*rider.inputs)
    return rider.post(outs)


def _carry(rider, n_in, n_out, first, last):
    if rider is None:
        return [], [], [], [], [], lambda refs: refs
    ni, no = len(rider.inputs), len(rider.out_shapes)

    def split(refs):
        own_in, r_in = refs[:n_in], refs[n_in:n_in + ni]
        own_out, r_out = refs[n_in + ni:n_in + ni + n_out], refs[n_in + ni + n_out:n_in + ni + n_out + no]
        rest = refs[n_in + ni + n_out + no:]
        own_scr, r_scr = rest[:len(rest) - len(rider.scratch)], rest[len(rest) - len(rider.scratch):]
        rrefs = tuple(r_in) + tuple(r_out) + tuple(r_scr)
        pl.when(first())(lambda: rider.start(rrefs))
        pl.when(last())(lambda: rider.finish(rrefs))
        return tuple(own_in) + tuple(own_out) + tuple(own_scr)

    return list(rider.inputs), [_HBM] * ni, list(rider.out_shapes), [_HBM] * no, list(rider.scratch), split


def _gather_rider(shards, axes):
    n = len(shards)
    srcs, out_shapes, kinds = [], [], []
    for s, ax in zip(shards, axes):
        L, a, b = s.shape
        if ax == 1:
            srcs.append(s.reshape(L, 1, a, b)), out_shapes.append((L, N_DEV, a, b)), kinds.append('row')
        elif b % 128 == 0:
            srcs.append(s), out_shapes.append((L, a, N_DEV * b)), kinds.append('col')
        else:
            srcs.append(s.reshape(1, L, a, b)), out_shapes.append((N_DEV, L, a, b)), kinds.append('slot')

    def parts(refs):
        x_refs, o_refs = refs[:n], refs[n:2 * n]
        send_sems, recv_sems, local_sem = refs[2 * n:]
        me, sib, chips = _place()
        sems = (send_sems, recv_sems)

        def win(t, px, py, pc):
            idx = 4 * px + 2 * py + pc
            if kinds[t] == 'row':
                return o_refs[t].at[:, pl.ds(idx, 1)]
            if kinds[t] == 'col':
                width = shards[t].shape[2]
                return o_refs[t].at[:, :, pl.ds(pl.multiple_of(idx * width, 128), width)]
            return o_refs[t].at[pl.ds(idx, 1)]

        def group(k, block, to, own):
            return [_remote(x_refs[t] if own else win(t, *block), win(t, *block), sems, k * n + t, to) for t in range(n)]

        mine = [pltpu.make_async_copy(x_refs[t], win(t, *me), local_sem.at[t]) for t in range(n)]
        first = group(0, me, sib, True)
        for j, chip in enumerate(chips):
            first += group(1 + j, me, (*chip, me[2]), True)
        return me, sib, chips, group, mine, first

    def start(refs):
        *_, mine, first = parts(refs)
        for cp in mine + first:
            cp.start()

    def finish(refs):
        me, sib, chips, group, mine, first = parts(refs)
        c = me[2]
        passed = []
        for j, chip in enumerate(chips):
            for cp in group(1 + j, (*chip, c), me, False):
                cp.wait_recv()
            fwd = group(4 + j, (*chip, c), sib, False)
            for cp in fwd:
                cp.start()
            passed += fwd
        for cp in group(0, sib, me, False):
            cp.wait_recv()
        for j, chip in enumerate(chips):
            for cp in group(4 + j, (*chip, 1 - c), me, False):
                cp.wait_recv()
        for cp in first + passed:
            cp.wait_send()
        for cp in mine:
            cp.wait()

    def post(outs):
        whole = []
        for o, s, kind in zip(outs, shards, kinds):
            L, a, b = s.shape
            whole.append(o.reshape(L, N_DEV * a, b) if kind == 'row' else o if kind == 'col' else _to_whole(o, 2))
        return whole

    return _Rider(srcs, [jax.ShapeDtypeStruct(shp, s.dtype) for shp, s in zip(out_shapes, shards)],
                  [pltpu.SemaphoreType.DMA(((N_DEV - 1) * n,)), pltpu.SemaphoreType.DMA(((N_DEV - 1) * n,)),
                   pltpu.SemaphoreType.DMA((n,))], start, finish, post)


def _sibling_swap(x, *, name):
    def body(x_ref, o_ref, send_sems, recv_sems):
        me, sib, _ = _place()
        c = me[2]
        sems = (send_sems, recv_sems)
        sends = [_remote(x_ref.at[j, 1 - c], o_ref.at[j], sems, j, sib) for j in range(N_CHIP)]
        for cp in sends:
            cp.start()
        for cp in sends:
            cp.wait_send()
            cp.wait_recv()

    return pl.pallas_call(
        body, name=name, out_shape=jax.ShapeDtypeStruct((N_CHIP,) + x.shape[2:], x.dtype),
        in_specs=[_HBM], out_specs=_HBM,
        scratch_shapes=[pltpu.SemaphoreType.DMA((N_CHIP,)), pltpu.SemaphoreType.DMA((N_CHIP,))],
        compiler_params=pltpu.CompilerParams(has_side_effects=True),
    )(x)


def _pair_sum(x, got, c, *, name):
    _, _, R, _ = x.shape
    tr = _pick(R, _SLOT_ROWS)

    def body(c_ref, x_ref, g_ref, o_ref):
        o_ref[...] = (x_ref[...].astype(F32) + g_ref[...].astype(F32)).astype(o_ref.dtype)

    return pl.pallas_call(
        body, name=name, out_shape=jax.ShapeDtypeStruct((N_CHIP, R, 128), x.dtype),
        grid_spec=pltpu.PrefetchScalarGridSpec(
            num_scalar_prefetch=1, grid=(R // tr,),
            in_specs=[pl.BlockSpec((N_CHIP, None, tr, 128), lambda i, c_ref: (0, c_ref[0], i, 0)),
                      pl.BlockSpec((N_CHIP, tr, 128), lambda i, c_ref: (0, i, 0))],
            out_specs=pl.BlockSpec((N_CHIP, tr, 128), lambda i, c_ref: (0, i, 0))),
        compiler_params=_params(("parallel",)),
    )(c, x, got)


def _chip_all_to_all_rider(x):
    def parts(refs):
        x_ref, o_ref, send_sems, recv_sems, local_sem = refs
        me, _, chips = _place()
        sems = (send_sems, recv_sems)
        mine = 2 * me[0] + me[1]
        local = pltpu.make_async_copy(x_ref.at[mine], o_ref.at[mine], local_sem)
        sends = [_remote(x_ref.at[2 * px + py], o_ref.at[mine], sems, j, (px, py, me[2])) for j, (px, py) in enumerate(chips)]
        arrival = lambda j: _remote(x_ref.at[mine], o_ref.at[2 * chips[j][0] + chips[j][1]], sems, j, me)
        return local, sends, arrival

    def start(refs):
        local, sends, _ = parts(refs)
        for cp in [local] + sends:
            cp.start()

    def finish(refs):
        local, sends, arrival = parts(refs)
        for j, cp in enumerate(sends):
            cp.wait_send()
            arrival(j).wait_recv()
        local.wait()

    return _Rider([x], [jax.ShapeDtypeStruct(x.shape, x.dtype)],
                  [pltpu.SemaphoreType.DMA((N_CHIP - 1,)), pltpu.SemaphoreType.DMA((N_CHIP - 1,)), pltpu.SemaphoreType.DMA],
                  start, finish, lambda outs: outs[0])


def _sum_slots(x, *, name):
    n, R, _ = x.shape
    tr = _pick(R, _SLOT_ROWS)

    def body(x_ref, o_ref):
        acc = x_ref[0].astype(F32)
        for j in range(1, n):
            acc = acc + x_ref[j].astype(F32)
        o_ref[...] = acc

    return pl.pallas_call(
        body, name=name, out_shape=jax.ShapeDtypeStruct((R, 128), F32), grid=(R // tr,),
        in_specs=[pl.BlockSpec((n, tr, 128), lambda i: (0, i, 0))], out_specs=pl.BlockSpec((tr, 128), lambda i: (i, 0)),
        compiler_params=_params(("parallel",)),
    )(x)


def _adamw(w, g, m, v, *, name):
    shape = w.shape
    cols = shape[-1]
    rows = int(np.prod(shape[:-1]))
    tr = next((t for t in (1024, 512, 256, 128, 64, 32, 16, 8) if rows % t == 0 and t * cols * 4 <= (1 << 20)), rows)

    def body(w_ref, g_ref, m_ref, v_ref, d_ref, mo_ref, vo_ref):
        g_ = g_ref[...]
        m_ = ADAM_B1 * m_ref[...] + (1.0 - ADAM_B1) * g_
        v_ = ADAM_B2 * v_ref[...] + (1.0 - ADAM_B2) * jnp.square(g_)
        m_hat = m_ / (1.0 - ADAM_B1 ** ADAM_STEP)
        v_hat = v_ / (1.0 - ADAM_B2 ** ADAM_STEP)
        d_ref[...] = -ADAM_LR * (m_hat / (jnp.sqrt(v_hat) + ADAM_EPS) + ADAM_WD * w_ref[...])
        mo_ref[...] = m_
        vo_ref[...] = v_

    blk = pl.BlockSpec((tr, cols), lambda i: (i, 0))
    outs = pl.pallas_call(
        body, name=name, out_shape=tuple(jax.ShapeDtypeStruct((rows, cols), F32) for _ in range(3)), grid=(rows // tr,),
        in_specs=[blk] * 4, out_specs=(blk,) * 3, compiler_params=_params(("parallel",)),
    )(*(a.reshape(rows, cols) for a in (w, g, m, v)))
    return tuple(o.reshape(shape) for o in outs)


_WEIGHTS = ('g_mix', 'w_in', 'b_fox_forget', 'w_gla_gate', 'b_gla_gate', 'g_gla_out', 'g_mla_q', 'w_mla_uq', 'g_mla_kv',
            'w_mla_ukv', 'b_branch_gate', 'w_up_fox', 'w_up_gla', 'w_up_mla', 'w_out', 'g_xa', 'g_mem', 'w_xq', 'w_xkv',
            'w_xo', 'g_mlp', 'w_mlp1', 'w_mlp2', 'g_final')
_SHARDED = (('w_in', 1), ('w_gla_gate', 2), ('w_mla_uq', 2), ('w_mla_ukv', 2), ('w_up_fox', 2), ('w_up_gla', 2),
            ('w_up_mla', 2), ('w_out', 1), ('w_xq', 1), ('w_xkv', 1), ('w_xo', 2), ('w_mlp1', 2), ('w_mlp2', 1))
_REPLICATED = tuple(n for n in _WEIGHTS if n not in dict(_SHARDED))
_ROW_PAD = 1024
_SMALL_ROW_PAD = 8
_PIECE_ROWS = 16


def _pack(flats, lead, row_pad=_ROW_PAD):
    if all(int(np.prod(a.shape[lead:])) % 128 == 0 for a in flats):
        def block(a):
            a = a.reshape(a.shape[:lead] + (-1, 128))
            return jnp.pad(a, [(0, 0)] * lead + [(0, -a.shape[lead] % _PIECE_ROWS), (0, 0)])
        cat = jnp.concatenate([block(a) for a in flats], axis=lead)
        rows = cat.shape[lead]
        return jnp.pad(cat, [(0, 0)] * lead + [(0, -(-rows // row_pad) * row_pad - rows), (0, 0)])
    cat = jnp.concatenate([a.reshape(a.shape[:lead] + (-1,)) for a in flats], axis=-1)
    n = cat.shape[-1]
    total = -(-n // (128 * row_pad)) * (128 * row_pad)
    cat = jnp.pad(cat, [(0, 0)] * lead + [(0, total - n)])
    return cat.reshape(cat.shape[:lead] + (total // 128, 128))


def _unpack(buf, shapes, lead):
    sizes = [int(np.prod(shp)) for shp in shapes]
    out, off = [], 0
    if all(n % 128 == 0 for n in sizes):
        for shp, n in zip(shapes, sizes):
            rows = buf[(slice(None),) * lead + (slice(off, off + n // 128),)]
            out.append(rows.reshape(buf.shape[:lead] + tuple(shp)))
            off += -(-(n // 128) // _PIECE_ROWS) * _PIECE_ROWS
        return out
    flat = buf.reshape(buf.shape[:lead] + (-1,))
    for shp, n in zip(shapes, sizes):
        out.append(flat[..., off:off + n].reshape(buf.shape[:lead] + tuple(shp)))
        off += n
    return out


def _to_whole(g, axis):
    if axis == 1:
        return g.transpose(1, 0, 2, 3).reshape(g.shape[1], N_DEV * g.shape[2], g.shape[3])
    return g.transpose(1, 2, 0, 3).reshape(g.shape[1], g.shape[2], N_DEV * g.shape[3])


def _to_shards(w, axis):
    L, R, C = w.shape
    if axis == 1:
        return w.reshape(L, N_DEV, R // N_DEV, C).transpose(1, 0, 2, 3)
    return w.reshape(L, R, N_DEV, C // N_DEV).transpose(2, 0, 1, 3)


def kernel(x, mem, g_mix, w_in, b_fox_forget, w_gla_gate, b_gla_gate, g_gla_out, g_mla_q, w_mla_uq, g_mla_kv, w_mla_ukv, b_branch_gate, w_up_fox, w_up_gla, w_up_mla, w_out, g_xa, g_mem, w_xq, w_xkv, w_xo, g_mlp, w_mlp1, w_mlp2, g_final, loss_target, m_g_mix, m_w_in, m_b_fox_forget, m_w_gla_gate, m_b_gla_gate, m_g_gla_out, m_g_mla_q, m_w_mla_uq, m_g_mla_kv, m_w_mla_ukv, m_b_branch_gate, m_w_up_fox, m_w_up_gla, m_w_up_mla, m_w_out, m_g_xa, m_g_mem, m_w_xq, m_w_xkv, m_w_xo, m_g_mlp, m_w_mlp1, m_w_mlp2, m_g_final, v_g_mix, v_w_in, v_b_fox_forget, v_w_gla_gate, v_b_gla_gate, v_g_gla_out, v_g_mla_q, v_w_mla_uq, v_g_mla_kv, v_w_mla_ukv, v_b_branch_gate, v_w_up_fox, v_w_up_gla, v_w_up_mla, v_w_out, v_g_xa, v_g_mem, v_w_xq, v_w_xkv, v_w_xo, v_g_mlp, v_w_mlp1, v_w_mlp2, v_g_final):
    wts = dict(zip(_WEIGHTS, (g_mix, w_in, b_fox_forget, w_gla_gate, b_gla_gate, g_gla_out, g_mla_q, w_mla_uq, g_mla_kv,
                              w_mla_ukv, b_branch_gate, w_up_fox, w_up_gla, w_up_mla, w_out, g_xa, g_mem, w_xq, w_xkv, w_xo,
                              g_mlp, w_mlp1, w_mlp2, g_final)))
    mom1 = dict(zip(_WEIGHTS, (m_g_mix, m_w_in, m_b_fox_forget, m_w_gla_gate, m_b_gla_gate, m_g_gla_out, m_g_mla_q,
                               m_w_mla_uq, m_g_mla_kv, m_w_mla_ukv, m_b_branch_gate, m_w_up_fox, m_w_up_gla, m_w_up_mla,
                               m_w_out, m_g_xa, m_g_mem, m_w_xq, m_w_xkv, m_w_xo, m_g_mlp, m_w_mlp1, m_w_mlp2, m_g_final)))
    mom2 = dict(zip(_WEIGHTS, (v_g_mix, v_w_in, v_b_fox_forget, v_w_gla_gate, v_b_gla_gate, v_g_gla_out, v_g_mla_q,
                               v_w_mla_uq, v_g_mla_kv, v_w_mla_ukv, v_b_branch_gate, v_w_up_fox, v_w_up_gla, v_w_up_mla,
                               v_w_out, v_g_xa, v_g_mem, v_w_xq, v_w_xkv, v_w_xo, v_g_mlp, v_w_mlp1, v_w_mlp2, v_g_final)))
    depth = g_mix.shape[0]

    names = [n for n, _ in _SHARDED]
    axes = dict(_SHARDED)
    shard = {n: wts[n] for n in names}
    shard['w_in'] = _pad_w_in(w_in)
    rep = {n: wts[n] for n in _REPLICATED}
    ps = [_LayerParams(rep, l) for l in range(depth)]

    def gather(group, l):
        rider = _gather_rider([shard[n][l:l + 1].astype(BF16) for n in group], [axes[n] for n in group])
        return rider, lambda whole: ps[l].w.update({n: w[0] for n, w in zip(group, whole)})

    first, sink = gather(['w_in'], 0)
    sink(_run_rider(first, name="gather_w_in_0"))
    narrow = ['w_gla_gate', 'w_mla_uq', 'w_mla_ukv', 'w_up_fox', 'w_up_gla', 'w_up_mla']
    hooks = {(0, 'in_big'): gather(narrow + ['w_out', 'w_xq', 'w_xkv', 'w_xo'], 0),
             (0, 'fox_fwd'): gather(['w_mlp1', 'w_mlp2'], 0)}
    ahead = (('mla_fwd', ['w_in'] + narrow), ('out_proj', ['w_out']), ('xa_fwd', ['w_xq', 'w_xo']), ('xo', ['w_xkv']),
             ('mlp1', ['w_mlp1']), ('mlp2', ['w_mlp2']))
    assert sorted(n for _, group in ahead for n in group) == sorted(names)
    for l in range(1, depth):
        for key, group in ahead:
            hooks[(l - 1, key)] = gather(group, l)

    core = lax.axis_index("c").astype(jnp.int32).reshape(1)
    late = ['w_in', 'w_gla_gate', 'w_mla_uq', 'w_mla_ukv']
    groups = {'early': [n for n in names if n not in late], 'late': late}
    small_grads, landed = {}, {}

    def exchange(l, g, which):
        slots = _pack([(g[n][:, None] if g[n].ndim == 3 else _to_shards(g[n][None], axes[n])).astype(BF16)
                       for n in groups[which]], 1)
        slots = slots.reshape((N_CHIP, 2) + slots.shape[1:])
        paired = _pair_sum(slots, _sibling_swap(slots, name=f"swap_grads_{which}_{l}"), core, name=f"pair_grads_{which}_{l}")
        return _chip_all_to_all_rider(paired), lambda got: landed.update({(l, which): got})

    def half_done(l, g):
        hooks[(l, 'mla_bwd')] = exchange(l, g, 'early')

    def layer_done(l, g):
        small_grads[l] = g
        rider, sink = exchange(l, g, 'late')
        if l > 0:
            hooks[(l - 1, 'fox_bwd')] = (rider, sink)
        else:
            sink(_run_rider(rider, name=f"scatter_grads_late_{l}"))

    loss, dx, dg_final = _local_step(x[0], mem[0], loss_target[0], ps, g_final, hooks, half_done, layer_done)
    loss = lax.psum(loss[0, 0], _MESH_AXES)

    grad = {}
    for which, group in groups.items():
        shapes = [(1,) + shard[n].shape[1:] for n in group]
        per_layer = [_unpack(_sum_slots(landed[(l, which)], name=f"sum_grads_{which}_{l}"), shapes, 0) for l in range(depth)]
        grad.update({n: jnp.concatenate([per_layer[l][i] for l in range(depth)], axis=0) for i, n in enumerate(group)})
    grad['w_in'] = _unpad_w_in(grad['w_in'])
    grads = small_grads
    small = [dg_final if n == 'g_final' else jnp.stack([grads[l][n] for l in range(depth)]) for n in _REPLICATED]
    small_shapes = [wts[n].shape for n in _REPLICATED]
    small_sum = _sum_slots(_all_gather(_pack(small, 0, _SMALL_ROW_PAD), name="gather_small_grads"), name="sum_small_grads")
    grad.update(dict(zip(_REPLICATED, _unpack(small_sum, small_shapes, 0))))

    delta, new_m, new_v = {}, {}, {}
    for n, _ in _SHARDED:
        delta[n], new_m[n], new_v[n] = _adamw(wts[n], grad[n], mom1[n], mom2[n], name=f"adamw_{n}")
    packed = [_pack([d[n] for n in _REPLICATED], 0, _SMALL_ROW_PAD) for d in (wts, mom1, mom2)]
    outs = _adamw(packed[0], small_sum, packed[1], packed[2], name="adamw_small")
    for d, o in zip((delta, new_m, new_v), outs):
        d.update(dict(zip(_REPLICATED, _unpack(o, small_shapes, 0))))

    return (loss, dx[None], *[grad[n] for n in _WEIGHTS], *[delta[n] for n in _WEIGHTS],
            *[new_m[n] for n in _WEIGHTS], *[new_v[n] for n in _WEIGHTS])
```

```python
import functools

import jax
import jax.numpy as jnp
import numpy as np
from jax import lax
from jax.experimental import pallas as pl
from jax.experimental.pallas import tpu as pltpu

F32 = jnp.float32
BF16 = jnp.bfloat16

EPS = 1e-6
CHUNK = 64
FOX_HEADS, FOX_HD = 4, 64
GLA_HEADS, GLA_DK, GLA_DV, GLA_RANK, GLA_TAU = 4, 64, 128, 16, 16.0
MLA_HEADS, MLA_Q_RANK, MLA_KV_RANK, MLA_NOPE, MLA_ROPE, MLA_VD = 4, 256, 128, 64, 32, 64
ROPE_BASE = 10000.0
XA_HEADS, XA_HD = 4, 128
ADAM_LR, ADAM_B1, ADAM_B2, ADAM_EPS, ADAM_WD, ADAM_STEP = 0.001, 0.9, 0.999, 1e-08, 0.01, 10

N_DEV = 8
V7X_VMEM_LIMIT = 56 * 1024 * 1024
NEG = -1e30

NN = ((1,), (0,))
NT = ((1,), (1,))
TN = ((0,), (0,))


def _dot(a, b, dims):
    return lax.dot_general(a.astype(BF16), b.astype(BF16), (dims, ((), ())), preferred_element_type=F32)


@jax.custom_vjp
def bdot(a, b):
    return _dot(a, b, NN)


bdot.defvjp(lambda a, b: (_dot(a, b, NN), (a, b)),
            lambda res, g: (_dot(g, res[1], NT), _dot(res[0], g, TN)))


@jax.custom_vjp
def bdot_nt(a, b):
    return _dot(a, b, NT)


bdot_nt.defvjp(lambda a, b: (_dot(a, b, NT), (a, b)),
               lambda res, g: (_dot(g, res[1], NN), _dot(g, res[0], TN)))


@jax.custom_vjp
def bdot_tn(a, b):
    return _dot(a, b, TN)


bdot_tn.defvjp(lambda a, b: (_dot(a, b, TN), (a, b)),
               lambda res, g: (_dot(res[1], g, NT), _dot(res[0], g, NN)))


def _split2(x):
    hi = x.astype(BF16)
    lo = (x - hi.astype(F32)).astype(BF16)
    return hi, lo


def _tri(n, lower):
    r = lax.broadcasted_iota(jnp.int32, (n, n), 0)
    c = lax.broadcasted_iota(jnp.int32, (n, n), 1)
    return jnp.where((r >= c) if lower else (r <= c), 1.0, 0.0).astype(BF16)


def _log_sigmoid(x):
    return jnp.minimum(x, 0.0) - jnp.log(1.0 + jnp.exp(-jnp.abs(x)))


def _sigmoid(x):
    return 1.0 / (1.0 + jnp.exp(-x))


def _rms(x, g):
    return x * lax.rsqrt(jnp.mean(x * x, axis=-1, keepdims=True) + EPS) * g


def _pick(dim, prefs):
    for p in prefs:
        if dim % p == 0:
            return p
    return dim


def _params(sem):
    return pltpu.CompilerParams(dimension_semantics=sem, vmem_limit_bytes=V7X_VMEM_LIMIT)


def _rms_vjp(x, g, dy, dres):
    rstd = lax.rsqrt(jnp.mean(x * x, axis=-1, keepdims=True) + EPS)
    xh = x * rstd
    gdy = dy * g
    dx = (gdy - xh * jnp.mean(gdy * xh, axis=-1, keepdims=True)) * rstd
    return (dx if dres is None else dx + dres), jnp.sum(dy * xh, axis=0, keepdims=True)


def _mm(a, b, *, mode, out_dtype, name, act=None, residual=None, drelu_of=None, norm_bwd=None, b_cols=None,
        col_shards=None, rider=None, tm=None, tn=None, tk=None):
    b_off, b_width = b_cols or (0, b.shape[1])
    if mode == 'nn':
        (M, K), N = a.shape, b_width
    elif mode == 'nt':
        (M, K), N = a.shape, b.shape[0]
    else:
        (K, M), N = a.shape, b_width
    tm = tm or _pick(M, (1024, 512, 256, 128))
    tn = tn or _pick(N, (1024, 1920, 1152, 768, 640, 512, 384, 256, 128))
    tk = tk or _pick(K, (1024, 1920, 1152, 640, 512, 256, 128))
    nk = K // tk
    dims = {'nn': NN, 'nt': NT, 'tn': TN}[mode]
    a_spec = pl.BlockSpec((tk, tm), lambda i, j, k: (k, i)) if mode == 'tn' else pl.BlockSpec((tm, tk), lambda i, j, k: (i, k))
    if mode == 'nt':
        b_spec = pl.BlockSpec((tn, tk), lambda i, j, k, o=b_off // tk: (j, k + o))
    else:
        b_spec = pl.BlockSpec((tk, tn), lambda i, j, k, o=b_off // tn: (k, j + o))
    o_spec = pl.BlockSpec((tm, tn), lambda i, j, k: (i, j))
    extra = [e for e in (residual, drelu_of) if e is not None]
    extra_specs = [o_spec] * len(extra)
    out_shape, out_specs, n_out = jax.ShapeDtypeStruct((M, N), out_dtype), o_spec, 1
    if col_shards:
        n_sh = N // col_shards
        assert tn % n_sh == 0 and not extra and norm_bwd is None
        out_shape = jax.ShapeDtypeStruct((col_shards, M, n_sh), out_dtype)
        out_specs = pl.BlockSpec((tn // n_sh, tm, n_sh), lambda i, j, k: (j, i, 0))
    if norm_bwd is not None:
        x_in, g_in, dres_in = norm_bwd
        assert tn == N and residual is None and drelu_of is None
        vec = pl.BlockSpec((1, N), lambda i, j, k: (0, 0))
        extra, extra_specs = [x_in, g_in.reshape(1, N), dres_in], [o_spec, vec, o_spec]
        out_shape = (jax.ShapeDtypeStruct((M, N), F32), jax.ShapeDtypeStruct((M, N), BF16), jax.ShapeDtypeStruct((1, N), F32))
        out_specs, n_out = (o_spec, o_spec, vec), 3

    grid = (M // tm, N // tn, nk)
    r_ins, r_in_specs, r_outs, r_out_specs, r_scratch, split = _carry(
        rider, 2 + len(extra), n_out, lambda: functools.reduce(jnp.logical_and, [pl.program_id(d) == 0 for d in range(3)]),
        lambda: functools.reduce(jnp.logical_and, [pl.program_id(d) == grid[d] - 1 for d in range(3)]))
    assert rider is None or n_out == 1

    def body(*refs):
        a_ref, b_ref, *rest = split(refs)
        o_ref = rest[len(extra)]
        first_rows = pl.program_id(0) == 0
        at = a_ref[...]
        if act == 'relu2':
            at = jnp.square(jnp.maximum(at.astype(F32), 0.0))
        part = _dot(at, b_ref[...], dims)

        def finish(acc):
            if norm_bwd is not None:
                dx, dg = _rms_vjp(rest[0][...], rest[1][...], acc, rest[2][...])
                o_ref[...] = dx
                rest[len(extra) + 1][...] = dx.astype(BF16)
                dg_ref = rest[len(extra) + 2]

                @pl.when(first_rows)
                def _():
                    dg_ref[...] = dg

                @pl.when(jnp.logical_not(first_rows))
                def _():
                    dg_ref[...] += dg
                return
            idx = 0
            if residual is not None:
                acc = acc + rest[idx][...]
                idx += 1
            if drelu_of is not None:
                acc = acc * (2.0 * jnp.maximum(rest[idx][...].astype(F32), 0.0))
            if col_shards:
                for t in range(tn // n_sh):
                    o_ref[t] = acc[:, t * n_sh:(t + 1) * n_sh].astype(out_dtype)
            else:
                o_ref[...] = acc.astype(out_dtype)

        if nk == 1:
            finish(part)
        else:
            acc_ref = rest[len(extra) + n_out]
            k = pl.program_id(2)

            @pl.when(k == 0)
            def _():
                acc_ref[...] = part

            @pl.when(k > 0)
            def _():
                acc_ref[...] += part

            @pl.when(k == nk - 1)
            def _():
                finish(acc_ref[...])

    scratch = [] if nk == 1 else [pltpu.VMEM((tm, tn), F32)]
    if rider is not None:
        res = pl.pallas_call(
            body, name=name, out_shape=(out_shape, *r_outs), grid=grid, in_specs=[a_spec, b_spec] + extra_specs + r_in_specs,
            out_specs=(out_specs, *r_out_specs), scratch_shapes=scratch + r_scratch,
            compiler_params=_params(("arbitrary", "arbitrary", "arbitrary")),
        )(a, b, *extra, *r_ins)
        return res[0], rider.post(res[1:])
    return pl.pallas_call(
        body, name=name, out_shape=out_shape, grid=grid, in_specs=[a_spec, b_spec] + extra_specs, out_specs=out_specs,
        scratch_shapes=scratch,
        compiler_params=_params(("arbitrary" if norm_bwd is not None else "parallel", "parallel", "arbitrary")),
    )(a, b, *extra)


def _rms_fwd(x, g, *, name, out_dtype=BF16):
    S, D = x.shape
    tr = _pick(S, (512, 256, 128))

    def body(x_ref, g_ref, o_ref):
        o_ref[...] = _rms(x_ref[...], g_ref[...]).astype(out_dtype)

    return pl.pallas_call(
        body, name=name, out_shape=jax.ShapeDtypeStruct((S, D), out_dtype), grid=(S // tr,),
        in_specs=[pl.BlockSpec((tr, D), lambda i: (i, 0)), pl.BlockSpec((1, D), lambda i: (0, 0))],
        out_specs=pl.BlockSpec((tr, D), lambda i: (i, 0)),
        compiler_params=_params(("parallel",)),
    )(x, g.reshape(1, D))


def _rms_bwd(x, g, dy, dres, *, name):
    S, D = x.shape
    tr = _pick(S, (512, 256, 128))

    def body(x_ref, g_ref, dy_ref, *rest):
        dx_ref, dxb_ref, dg_ref = rest[-3], rest[-2], rest[-1]
        dx, part = _rms_vjp(x_ref[...], g_ref[...], dy_ref[...].astype(F32), None if dres is None else rest[0][...])
        dx_ref[...] = dx
        dxb_ref[...] = dx.astype(BF16)

        @pl.when(pl.program_id(0) == 0)
        def _():
            dg_ref[...] = part

        @pl.when(pl.program_id(0) > 0)
        def _():
            dg_ref[...] += part

    row = pl.BlockSpec((tr, D), lambda i: (i, 0))
    vec = pl.BlockSpec((1, D), lambda i: (0, 0))
    ins = [x, g.reshape(1, D), dy] + ([dres] if dres is not None else [])
    return pl.pallas_call(
        body, name=name,
        out_shape=(jax.ShapeDtypeStruct((S, D), F32), jax.ShapeDtypeStruct((S, D), BF16), jax.ShapeDtypeStruct((1, D), F32)),
        grid=(S // tr,),
        in_specs=[row, vec, row] + ([row] if dres is not None else []),
        out_specs=(row, row, vec),
        compiler_params=_params(("arbitrary",)),
    )(*ins)


def _loss_head(x, g, target, *, name):
    S, D = x.shape
    tr = _pick(S, (512, 256, 128))

    def body(x_ref, g_ref, t_ref, l_ref, dx_ref, dxb_ref, dg_ref):
        x_ = x_ref[...]
        g_ = g_ref[...]
        rstd = lax.rsqrt(jnp.mean(x_ * x_, axis=-1, keepdims=True) + EPS)
        xh = x_ * rstd
        err = xh * g_ - t_ref[...]
        lpart = (0.5 / D) * jnp.sum(jnp.sum(err * err, axis=-1, keepdims=True), axis=0, keepdims=True)
        dy = err * (1.0 / D)
        gdy = dy * g_
        dx = (gdy - xh * jnp.mean(gdy * xh, axis=-1, keepdims=True)) * rstd
        dx_ref[...] = dx
        dxb_ref[...] = dx.astype(BF16)
        gpart = jnp.sum(dy * xh, axis=0, keepdims=True)

        @pl.when(pl.program_id(0) == 0)
        def _():
            dg_ref[...] = gpart
            l_ref[...] = lpart

        @pl.when(pl.program_id(0) > 0)
        def _():
            dg_ref[...] += gpart
            l_ref[...] += lpart

    row = pl.BlockSpec((tr, D), lambda i: (i, 0))
    vec = pl.BlockSpec((1, D), lambda i: (0, 0))
    return pl.pallas_call(
        body, name=name,
        out_shape=(jax.ShapeDtypeStruct((1, 1), F32), jax.ShapeDtypeStruct((S, D), F32), jax.ShapeDtypeStruct((S, D), BF16),
                   jax.ShapeDtypeStruct((1, D), F32)),
        grid=(S // tr,),
        in_specs=[row, vec, row],
        out_specs=(pl.BlockSpec((1, 1), lambda i: (0, 0)), row, row, vec),
        compiler_params=_params(("arbitrary",)),
    )(x, g.reshape(1, D), target)


def _mask_of(mask, tq, tk, keys_first=False):
    shape, q_axis = ((tk, tq), 1) if keys_first else ((tq, tk), 0)
    qpos = lax.broadcasted_iota(jnp.int32, shape, q_axis)
    kpos = lax.broadcasted_iota(jnp.int32, shape, 1 - q_axis)
    if mask == 'causal':
        return kpos <= qpos
    return kpos <= (qpos | (CHUNK - 1))


LANES = 128
LOG2E = 1.4426950408889634


def _lane_group(j, w, width):
    lane = lax.broadcasted_iota(jnp.int32, (1, width), 1)
    return (lane >= j * w) & (lane < (j + 1) * w)


def _only(x, j, w):
    if w == x.shape[1]:
        return x
    return jnp.where(_lane_group(j, w, x.shape[1]), x, jnp.zeros_like(x))


def _side_by_side(xs):
    return xs[0] if len(xs) == 1 else jnp.concatenate(xs, axis=1)


def _on_top(xs):
    return xs[0] if len(xs) == 1 else jnp.concatenate(xs, axis=0)


def _stacked(x, hp, w):
    return _on_top([_only(x, j, w) for j in range(hp)])


def _col_block(entry, rows, idx):
    arr, off, width = entry
    return pl.BlockSpec((rows, width), lambda i, j, o=off // width: (idx(i, j), o))


def _attn_fwd(qk, v, H, cq, ck, *, scale, mask, name, rider=None):
    Sq, Sk = qk[0][0][0].shape[0], v[0].shape[0]
    dv = v[2] // H
    w0 = qk[0][2]
    hp = LANES // w0
    G = H // hp
    assert dv == w0 and not qk[0][3] and all(sh and H * w == LANES for _, _, w, sh in qk[1:])
    tq = _pick(Sq, (512, 256, 128))
    tk = tq if mask else _pick(Sk, (512, 256, 128))
    nq, nk = Sq // tq, Sk // tk
    bias = cq is not None
    npart = len(qk)

    def body(*refs):
        refs = split(refs)
        q_refs, k_refs = refs[0:2 * npart:2], refs[1:2 * npart:2]
        v_ref = refs[2 * npart]
        cq_ref, ck_ref = (refs[2 * npart + 1], refs[2 * npart + 2]) if bias else (None, None)
        o_ref, lse_ref, m_s, l_s, acc_s = refs[-5:]
        qi, ki = pl.program_id(0), pl.program_id(1)

        @pl.when(ki == 0)
        def _():
            m_s[...] = jnp.full(m_s.shape, NEG, F32)
            l_s[...] = jnp.zeros(l_s.shape, F32)
            acc_s[...] = jnp.zeros(acc_s.shape, F32)

        def rows_of(vals):
            return _on_top([jnp.broadcast_to(r, (w0, tq)) for r in vals])

        def compute(masked):
            keep = _mask_of(mask, tq, tk, keys_first=True) if masked else None
            for g in range(G):
                lanes = slice(g * LANES, (g + 1) * LANES)
                q128, k128, v128 = q_refs[0][:, lanes], k_refs[0][:, lanes], v_ref[:, lanes]
                ps, alphas = [], []
                extras = list(zip(qk, q_refs, k_refs))[1:]
                k_all = _side_by_side([k128] + [k_ref[...] for _, _, k_ref in extras])
                for j in range(hp):
                    h = g * hp + j
                    q_all = _side_by_side([_only(q128, j, w0)] + [_only(q_ref[...], h, w) for (_, _, w, _), q_ref, _ in extras])
                    s = _dot(k_all, q_all, NT) * scale
                    if bias:
                        s = s + (cq_ref[h:h + 1, :] - ck_ref[:, h:h + 1])
                    if masked:
                        s = jnp.where(keep, s, NEG)
                    m_prev = m_s[h:h + 1, :]
                    m_new = jnp.maximum(m_prev, jnp.max(s, axis=0, keepdims=True))
                    alpha = jnp.exp(m_prev - m_new)
                    p = jnp.exp(s - m_new)
                    l_s[h:h + 1, :] = alpha * l_s[h:h + 1, :] + jnp.sum(p, axis=0, keepdims=True)
                    m_s[h:h + 1, :] = m_new
                    ps.append(p.astype(BF16))
                    alphas.append(alpha)
                acc_s[g] = rows_of(alphas) * acc_s[g] + _dot(_stacked(v128, hp, w0), _on_top(ps), TN)

        if mask is None:
            compute(False)
        else:
            pl.when(ki < qi)(lambda: compute(False))
            pl.when(ki == qi)(lambda: compute(True))

        @pl.when(ki == ((nk - 1) if mask is None else qi))
        def _():
            for g in range(G):
                norm = acc_s[g] / rows_of([l_s[g * hp + j:g * hp + j + 1, :] for j in range(hp)])
                o_ref[:, g * LANES:(g + 1) * LANES] = norm.T.astype(BF16)
            lse_ref[...] = jnp.zeros(lse_ref.shape, F32)
            lse_ref[0:H, :] = m_s[0:H, :] + jnp.log(l_s[0:H, :])

    q_idx = lambda i, j: i
    k_idx = (lambda i, j: jnp.minimum(i, j)) if mask else (lambda i, j: j)
    ins, in_specs = [], []
    for q_e, k_e, _, _ in qk:
        ins += [q_e[0], k_e[0]]
        in_specs += [_col_block(q_e, tq, q_idx), _col_block(k_e, tk, k_idx)]
    ins.append(v[0])
    in_specs.append(_col_block(v, tk, k_idx))
    if bias:
        in_specs += [pl.BlockSpec((8, tq), lambda i, j: (0, i)), pl.BlockSpec((tk, 8), lambda i, j: (k_idx(i, j), 0))]
        ins += [cq, ck]
    r_ins, r_in_specs, r_outs, r_out_specs, r_scratch, split = _carry(
        rider, len(ins), 2, lambda: (pl.program_id(0) == 0) & (pl.program_id(1) == 0),
        lambda: (pl.program_id(0) == nq - 1) & (pl.program_id(1) == nk - 1))
    res = pl.pallas_call(
        body, name=name,
        out_shape=(jax.ShapeDtypeStruct((Sq, H * dv), BF16), jax.ShapeDtypeStruct((8, Sq), F32), *r_outs),
        grid=(nq, nk), in_specs=in_specs + r_in_specs,
        out_specs=(pl.BlockSpec((tq, H * dv), lambda i, j: (i, 0)), pl.BlockSpec((8, tq), lambda i, j: (0, i)), *r_out_specs),
        scratch_shapes=[pltpu.VMEM((8, tq), F32), pltpu.VMEM((8, tq), F32), pltpu.VMEM((G, LANES, tq), F32)] + r_scratch,
        compiler_params=_params(("arbitrary", "arbitrary")) if rider else _params(("parallel", "arbitrary")),
    )(*ins, *r_ins)
    return (res[0], res[1], rider.post(res[2:])) if rider else res


def _attn_bwd(qk, v, H, o, do, lse, cq, ck, *, scale, mask, name, rider=None):
    Sq, Sk = qk[0][0][0].shape[0], v[0].shape[0]
    dv = v[2] // H
    w0 = qk[0][2]
    hp = LANES // w0
    G = H // hp
    tq = _pick(Sq, (512, 256, 128))
    tk = tq if mask else _pick(Sk, (512, 256, 128))
    nq, nk = Sq // tq, Sk // tk
    bias = cq is not None
    npart = len(qk)
    n_in = 2 * npart + 4 + (2 if bias else 0)

    def body(*refs):
        refs = split(refs)
        q_refs, k_refs = refs[0:2 * npart:2], refs[1:2 * npart:2]
        v_ref, o_ref, do_ref, lse_ref = refs[2 * npart:2 * npart + 4]
        cq_ref, ck_ref = (refs[2 * npart + 4], refs[2 * npart + 5]) if bias else (None, None)
        outs = refs[n_in:]
        dq_refs, dk_refs, dv_ref = outs[:npart], outs[npart:2 * npart], outs[2 * npart]
        dck_ref, dcq_ref = (outs[2 * npart + 1], outs[2 * npart + 2]) if bias else (None, None)
        dk_accs, dv_acc = refs[-(npart + 1):-1], refs[-1]
        ki, qi = pl.program_id(0), pl.program_id(1)
        first_q = ki if mask else 0

        @pl.when((ki == 0) & (qi == 0))
        def _():
            for r in dq_refs:
                r[...] = jnp.zeros(r.shape, F32)
            if bias:
                dcq_ref[...] = jnp.zeros(dcq_ref.shape, F32)

        @pl.when(qi == first_q)
        def _():
            for r in dk_accs:
                r[...] = jnp.zeros(r.shape, F32)
            dv_acc[...] = jnp.zeros(dv_acc.shape, F32)
            if bias:
                dck_ref[...] = jnp.zeros(dck_ref.shape, F32)

        def compute(masked):
            keep = _mask_of(mask, tq, tk, keys_first=True) if masked else None
            rows = pl.ds(pl.multiple_of(qi * tq, tq), tq)
            extras = list(zip(qk, q_refs, k_refs, dq_refs, dk_accs))[1:]
            for g in range(G):
                lanes = slice(g * LANES, (g + 1) * LANES)
                q128, k128, v128 = q_refs[0][:, lanes], k_refs[0][:, lanes], v_ref[:, lanes]
                do128, o128 = do_ref[:, lanes], o_ref[:, lanes]
                prod = do128.astype(F32) * o128.astype(F32)
                ps, dss = [], []
                k_all = _side_by_side([k128] + [e[2][...] for e in extras])
                for j in range(hp):
                    h = g * hp + j
                    q_all = _side_by_side([_only(q128, j, w0)] + [_only(e[1][...], h, e[0][2]) for e in extras])
                    s = _dot(k_all, q_all, NT) * (scale * LOG2E)
                    if bias:
                        s = s - ck_ref[:, h:h + 1] * LOG2E
                    if masked:
                        s = jnp.where(keep, s, NEG)
                    row = lse_ref[h:h + 1, :] - cq_ref[h:h + 1, :] if bias else lse_ref[h:h + 1, :]
                    p = jnp.exp2(s - row * LOG2E)
                    dp = _dot(v128, _only(do128, j, w0), NT)
                    delta = jnp.sum(_only(prod, j, w0), axis=1, keepdims=True).T
                    ds = p * (dp - delta)
                    if bias:
                        dck_ref[:, h:h + 1] -= jnp.sum(ds, axis=1, keepdims=True)
                        dcq_ref[h:h + 1, rows] += jnp.sum(ds, axis=0, keepdims=True)
                    ps.append(p.astype(BF16))
                    dss.append((ds * scale).astype(BF16))
                for (_, _, w, _), q_ref, k_ref, dq_ref, dk_acc in extras:
                    heads = range(g * hp, (g + 1) * hp)
                    dk_acc[...] += _dot(_side_by_side(dss), _on_top([_only(q_ref[...], h, w) for h in heads]), NN)
                    dq_ref[rows, :] += _dot(_on_top(dss), _on_top([_only(k_ref[...], h, w) for h in heads]), TN)
                dv_acc[:, lanes] += _dot(_side_by_side(ps), _stacked(do128, hp, w0), NN)
                dk_accs[0][:, lanes] += _dot(_side_by_side(dss), _stacked(q128, hp, w0), NN)
                dq_refs[0][rows, lanes] += _dot(_on_top(dss), _stacked(k128, hp, w0), TN)

        if mask is None:
            compute(False)
        else:
            pl.when(qi > ki)(lambda: compute(False))
            pl.when(qi == ki)(lambda: compute(True))

        @pl.when(qi == nq - 1)
        def _():
            for r, acc in zip(dk_refs, dk_accs):
                r[...] = acc[...]
            dv_ref[...] = dv_acc[...]

    q_idx = (lambda j, i: jnp.maximum(i, j)) if mask else (lambda j, i: i)
    k_idx = lambda j, i: j
    ins, in_specs, dq_shapes, dq_specs, dk_shapes, dk_specs, scratch = [], [], [], [], [], [], []
    for q_e, k_e, w, shared in qk:
        ins += [q_e[0], k_e[0]]
        in_specs += [_col_block(q_e, tq, q_idx), _col_block(k_e, tk, k_idx)]
        dq_shapes.append(jax.ShapeDtypeStruct((Sq, H * w), F32))
        dq_specs.append(pl.BlockSpec((Sq, H * w), lambda j, i: (0, 0)))
        kw = k_e[2]
        dk_shapes.append(jax.ShapeDtypeStruct((Sk, kw), F32))
        dk_specs.append(pl.BlockSpec((tk, kw), lambda j, i: (j, 0)))
        scratch.append(pltpu.VMEM((tk, kw), F32))
    row_q = lambda width: pl.BlockSpec((tq, width), lambda j, i: (q_idx(j, i), 0))
    per_q = pl.BlockSpec((8, tq), lambda j, i: (0, q_idx(j, i)))
    ins += [v[0], o, do, lse]
    in_specs += [_col_block(v, tk, k_idx), row_q(H * dv), row_q(H * dv), per_q]
    out_shape = dq_shapes + dk_shapes + [jax.ShapeDtypeStruct((Sk, H * dv), F32)]
    out_specs = dq_specs + dk_specs + [pl.BlockSpec((tk, H * dv), lambda j, i: (j, 0))]
    if bias:
        in_specs += [per_q, pl.BlockSpec((tk, 8), lambda j, i: (j, 0))]
        ins += [cq, ck]
        out_shape += [jax.ShapeDtypeStruct((Sk, 8), F32), jax.ShapeDtypeStruct((8, Sq), F32)]
        out_specs += [pl.BlockSpec((tk, 8), lambda j, i: (j, 0)), pl.BlockSpec((8, Sq), lambda j, i: (0, 0))]
    scratch.append(pltpu.VMEM((tk, H * dv), F32))
    n_out = len(out_shape)
    r_ins, r_in_specs, r_outs, r_out_specs, r_scratch, split = _carry(
        rider, len(ins), n_out, lambda: (pl.program_id(0) == 0) & (pl.program_id(1) == 0),
        lambda: (pl.program_id(0) == nk - 1) & (pl.program_id(1) == nq - 1))
    res = pl.pallas_call(
        body, name=name, out_shape=tuple(out_shape + r_outs), grid=(nk, nq), in_specs=in_specs + r_in_specs,
        out_specs=tuple(out_specs + r_out_specs), scratch_shapes=scratch + r_scratch,
        compiler_params=_params(("arbitrary", "arbitrary")),
    )(*ins, *r_ins)
    own = (list(res[:npart]), list(res[npart:2 * npart]), res[2 * npart]) + tuple(res[2 * npart + 1:n_out])
    return own + (rider.post(res[n_out:]),) if rider else own


def _split3_dot(x, t):
    hi = x.astype(BF16)
    r1 = x - hi.astype(F32)
    mid = r1.astype(BF16)
    lo = (r1 - mid.astype(F32)).astype(BF16)
    return _dot(hi, t, NN) + _dot(mid, t, NN) + _dot(lo, t, NN)


def _fox_cum_fwd(ff_t, b, *, name):
    _, S = ff_t.shape
    tb = _pick(S, (512, 256, 128))

    def body(f_ref, b_ref, o_ref, carry):
        @pl.when(pl.program_id(0) == 0)
        def _():
            carry[...] = jnp.zeros(carry.shape, F32)

        lf = _log_sigmoid(f_ref[...] + b_ref[...])
        o_ref[...] = _split3_dot(lf, _tri(tb, False)) + carry[...]
        carry[...] += jnp.sum(lf, axis=1, keepdims=True)

    return pl.pallas_call(
        body, name=name, out_shape=jax.ShapeDtypeStruct((8, S), F32), grid=(S // tb,),
        in_specs=[pl.BlockSpec((8, tb), lambda i: (0, i)), pl.BlockSpec((8, 1), lambda i: (0, 0))],
        out_specs=pl.BlockSpec((8, tb), lambda i: (0, i)),
        scratch_shapes=[pltpu.VMEM((8, 1), F32)],
        compiler_params=_params(("arbitrary",)),
    )(ff_t, b)


def _fox_cum_bwd(ff_t, b, dcum_t, *, name):
    _, S = ff_t.shape
    tb = _pick(S, (512, 256, 128))
    nb = S // tb

    def body(f_ref, b_ref, dc_ref, df_ref, db_ref, carry):
        @pl.when(pl.program_id(0) == 0)
        def _():
            carry[...] = jnp.zeros(carry.shape, F32)
            db_ref[...] = jnp.zeros(db_ref.shape, F32)

        dc = dc_ref[...]
        dlf = _split3_dot(dc, _tri(tb, True)) + carry[...]
        carry[...] += jnp.sum(dc, axis=1, keepdims=True)
        df = dlf * _sigmoid(-(f_ref[...] + b_ref[...]))
        df_ref[...] = df
        db_ref[...] += jnp.sum(df, axis=1, keepdims=True)

    rev = lambda i: (0, nb - 1 - i)
    return pl.pallas_call(
        body, name=name,
        out_shape=(jax.ShapeDtypeStruct((8, S), F32), jax.ShapeDtypeStruct((8, 1), F32)), grid=(nb,),
        in_specs=[pl.BlockSpec((8, tb), rev), pl.BlockSpec((8, 1), lambda i: (0, 0)), pl.BlockSpec((8, tb), rev)],
        out_specs=(pl.BlockSpec((8, tb), rev), pl.BlockSpec((8, 1), lambda i: (0, 0))),
        scratch_shapes=[pltpu.VMEM((8, 1), F32)],
        compiler_params=_params(("arbitrary",)),
    )(ff_t, b, dcum_t)


GLA_W = GLA_HEADS * GLA_DK
GLA_BLOCK_CHUNKS = 4


def _same_chunk(n, lower):
    r = lax.broadcasted_iota(jnp.int32, (n, n), 0)
    c = lax.broadcasted_iota(jnp.int32, (n, n), 1)
    same = (r | (CHUNK - 1)) == (c | (CHUNK - 1))
    return jnp.where(same & (r >= c) if lower else same, 1.0, 0.0).astype(BF16)


def _chunk_mix(x, t, transpose):
    hi, lo = _split2(x)
    dims = TN if transpose else NN
    return _dot(t, hi, dims) + _dot(t, lo, dims)


@jax.custom_vjp
def chunk_cumsum(x):
    return _chunk_mix(x, _same_chunk(x.shape[0], True), False)


chunk_cumsum.defvjp(lambda x: (chunk_cumsum(x), None), lambda _, g: (_chunk_mix(g, _same_chunk(g.shape[0], True), True),))


@jax.custom_vjp
def chunk_total(x):
    return _chunk_mix(x, _same_chunk(x.shape[0], False), False)


chunk_total.defvjp(lambda x: (chunk_total(x), None), lambda _, g: (_chunk_mix(g, _same_chunk(g.shape[0], False), False),))


def _gla_block(q, k, zsm, wg, bg, go, vs, rs, states):
    n_chunks = q.shape[0] // CHUNK
    la = _log_sigmoid(bdot(zsm, wg) + bg) * (1.0 / GLA_TAU)
    end = chunk_total(la)
    kd = k * jnp.exp(end - chunk_cumsum(la))
    a = jnp.exp(end)
    qs = q * (GLA_DK ** -0.5)
    lane = lax.broadcasted_iota(jnp.int32, (1, GLA_W), 1)
    outs, new_states = [], []
    for h in range(GLA_HEADS):
        kdh = kd * jnp.where((lane >= h * GLA_DK) & (lane < (h + 1) * GLA_DK), 1.0, 0.0)
        st, o = states[h], []
        for c in range(n_chunks):
            rows = slice(c * CHUNK, (c + 1) * CHUNK)
            st = st * a[c * CHUNK:c * CHUNK + 1] + bdot_tn(vs[h][rows], kdh[rows])
            o.append(bdot_nt(qs[rows], st))
        o = _rms(jnp.concatenate(o, axis=0), go)
        outs.append(o * (rs[h] * _sigmoid(rs[h])))
        new_states.append(st)
    return outs, new_states


def _gla_fwd(z, zsm, wg, bg, go, cols, *, name):
    S = z.shape[0]
    rb = GLA_BLOCK_CHUNKS * CHUNK
    nb = S // rb
    cq, ckk, cv, cr = cols
    H = GLA_HEADS

    def body(q_ref, k_ref, zsm_ref, wg_ref, bg_ref, go_ref, *rest):
        v_refs, r_refs = rest[:H], rest[H:2 * H]
        o_ref, st_ref, state = rest[2 * H], rest[2 * H + 1], rest[2 * H + 2]

        @pl.when(pl.program_id(0) == 0)
        def _():
            state[...] = jnp.zeros(state.shape, F32)

        states = [state[h] for h in range(H)]
        for h in range(H):
            st_ref[0, h] = states[h]
        outs, new_states = _gla_block(
            q_ref[...].astype(F32), k_ref[...].astype(F32), zsm_ref[...], wg_ref[...], bg_ref[...], go_ref[...],
            [v_refs[h][...].astype(F32) for h in range(H)], [r_refs[h][...].astype(F32) for h in range(H)], states)
        for h in range(H):
            o_ref[:, h * GLA_DV:(h + 1) * GLA_DV] = outs[h].astype(BF16)
            state[h] = new_states[h]

    def col(width, off):
        return pl.BlockSpec((rb, width), lambda i, o=off // width: (i, o))

    full = lambda shp: pl.BlockSpec(shp, lambda i: (0,) * len(shp))
    in_specs = [col(GLA_W, cq), col(GLA_W, ckk), pl.BlockSpec((rb, 128), lambda i: (i, 0)),
                full((128, GLA_W)), full((1, GLA_W)), full((1, GLA_DV))]
    in_specs += [col(GLA_DV, cv + h * GLA_DV) for h in range(H)] + [col(GLA_DV, cr + h * GLA_DV) for h in range(H)]
    return pl.pallas_call(
        body, name=name,
        out_shape=(jax.ShapeDtypeStruct((S, H * GLA_DV), BF16), jax.ShapeDtypeStruct((nb, H, GLA_DV, GLA_W), F32)),
        grid=(nb,), in_specs=in_specs,
        out_specs=(pl.BlockSpec((rb, H * GLA_DV), lambda i: (i, 0)),
                   pl.BlockSpec((1, H, GLA_DV, GLA_W), lambda i: (i, 0, 0, 0))),
        scratch_shapes=[pltpu.VMEM((H, GLA_DV, GLA_W), F32)],
        compiler_params=_params(("arbitrary",)),
    )(z, z, zsm, wg, bg, go, *([z] * (2 * H)))


def _gla_bwd(z, zsm, wg, bg, go, states, do, cols, *, name):
    S = z.shape[0]
    rb = GLA_BLOCK_CHUNKS * CHUNK
    nb = S // rb
    cq, ckk, cv, cr = cols
    H = GLA_HEADS

    def body(q_ref, k_ref, zsm_ref, wg_ref, bg_ref, go_ref, st_ref, do_ref, *rest):
        v_refs, r_refs = rest[:H], rest[H:2 * H]
        dq_ref, dk_ref, dv_ref, dr_ref, dzsm_ref, dwg_ref, dbg_ref, dgo_ref, dstate = rest[2 * H:]

        @pl.when(pl.program_id(0) == 0)
        def _():
            dstate[...] = jnp.zeros(dstate.shape, F32)
            dwg_ref[...] = jnp.zeros(dwg_ref.shape, F32)
            dbg_ref[...] = jnp.zeros(dbg_ref.shape, F32)
            dgo_ref[...] = jnp.zeros(dgo_ref.shape, F32)

        prim = (q_ref[...].astype(F32), k_ref[...].astype(F32), zsm_ref[...], wg_ref[...], bg_ref[...], go_ref[...],
                [v_refs[h][...].astype(F32) for h in range(H)], [r_refs[h][...].astype(F32) for h in range(H)],
                [st_ref[0, h] for h in range(H)])
        _, vjp = jax.vjp(_gla_block, *prim)
        douts = [do_ref[:, h * GLA_DV:(h + 1) * GLA_DV].astype(F32) for h in range(H)]
        dq, dk, dzs, dwg, dbg, dgo, dvs, drs, dsts = vjp((douts, [dstate[h] for h in range(H)]))
        dq_ref[...] = dq.astype(BF16)
        dk_ref[...] = dk.astype(BF16)
        dzsm_ref[...] = dzs
        dwg_ref[...] += dwg
        dbg_ref[...] += dbg
        dgo_ref[...] += dgo
        for h in range(H):
            dv_ref[:, h * GLA_DV:(h + 1) * GLA_DV] = dvs[h].astype(BF16)
            dr_ref[:, h * GLA_DV:(h + 1) * GLA_DV] = drs[h].astype(BF16)
            dstate[h] = dsts[h]

    rev = lambda i: nb - 1 - i

    def col(width, off):
        return pl.BlockSpec((rb, width), lambda i, o=off // width: (rev(i), o))

    full = lambda shp: pl.BlockSpec(shp, lambda i: (0,) * len(shp))
    rowb = lambda w: pl.BlockSpec((rb, w), lambda i: (rev(i), 0))
    in_specs = [col(GLA_W, cq), col(GLA_W, ckk), rowb(128), full((128, GLA_W)), full((1, GLA_W)), full((1, GLA_DV)),
                pl.BlockSpec((1, H, GLA_DV, GLA_W), lambda i: (rev(i), 0, 0, 0)), rowb(H * GLA_DV)]
    in_specs += [col(GLA_DV, cv + h * GLA_DV) for h in range(H)] + [col(GLA_DV, cr + h * GLA_DV) for h in range(H)]
    return pl.pallas_call(
        body, name=name,
        out_shape=(jax.ShapeDtypeStruct((S, GLA_W), BF16), jax.ShapeDtypeStruct((S, GLA_W), BF16),
                   jax.ShapeDtypeStruct((S, H * GLA_DV), BF16), jax.ShapeDtypeStruct((S, H * GLA_DV), BF16),
                   jax.ShapeDtypeStruct((S, 128), F32), jax.ShapeDtypeStruct((128, GLA_W), F32),
                   jax.ShapeDtypeStruct((1, GLA_W), F32), jax.ShapeDtypeStruct((1, GLA_DV), F32)),
        grid=(nb,), in_specs=in_specs,
        out_specs=(rowb(GLA_W), rowb(GLA_W), rowb(H * GLA_DV), rowb(H * GLA_DV), rowb(128),
                   full((128, GLA_W)), full((1, GLA_W)), full((1, GLA_DV))),
        scratch_shapes=[pltpu.VMEM((H, GLA_DV, GLA_W), F32)],
        compiler_params=_params(("arbitrary",)),
    )(z, z, zsm, wg, bg, go, states, do, *([z] * (2 * H)))


def _row_spec(entry, tr):
    if isinstance(entry, tuple):
        arr, width, off = entry
        return arr, pl.BlockSpec((tr, width), lambda i, o=off // width: (i, o))
    return entry, pl.BlockSpec((tr, entry.shape[1]), lambda i: (i, 0))


def _stage_fwd(fn, rows, consts, outs, *, name, tr=None):
    first = rows[0][0] if isinstance(rows[0], tuple) else rows[0]
    S = first.shape[0]
    tr = tr or _pick(S, (512, 256, 128))
    arrs, specs = zip(*[_row_spec(e, tr) for e in rows])
    nr, nc = len(rows), len(consts)

    def body(*refs):
        vals = [r[...].astype(F32) for r in refs[:nr + nc]]
        res = fn(*vals)
        for o_ref, val in zip(refs[nr + nc:], res):
            o_ref[...] = val.astype(o_ref.dtype)

    cspecs = [pl.BlockSpec(c.shape, lambda i, n=c.ndim: (0,) * n) for c in consts]
    return pl.pallas_call(
        body, name=name,
        out_shape=tuple(jax.ShapeDtypeStruct((S, w), dt) for w, dt in outs), grid=(S // tr,),
        in_specs=list(specs) + cspecs,
        out_specs=tuple(pl.BlockSpec((tr, w), lambda i: (i, 0)) for w, _ in outs),
        compiler_params=_params(("parallel",)),
    )(*arrs, *consts)


def _stage_bwd(fn, rows, consts, cts, n_diff, drow_dtypes, *, name, tr=None, lead=None):
    first = rows[0][0] if isinstance(rows[0], tuple) else rows[0]
    S = first.shape[0]
    tr = tr or _pick(S, (512, 256, 128))
    arrs, specs = zip(*[_row_spec(e, tr) for e in rows])
    widths = [e[1] if isinstance(e, tuple) else e.shape[1] for e in rows]
    nr, nc, nt = len(rows), len(consts), len(cts)
    n_lead, lead_width = lead or (1, widths[0])
    n_rows_out = n_diff - n_lead + 1

    def body(*refs):
        vals = [r[...].astype(F32) for r in refs[:nr + nc]]
        ct = [r[...].astype(F32) for r in refs[nr + nc:nr + nc + nt]]
        drow_refs = refs[nr + nc + nt:nr + nc + nt + n_rows_out]
        dconst_refs = refs[nr + nc + nt + n_rows_out:]
        rest_rows = vals[n_diff:nr]

        def f(diff_rows, cs):
            return tuple(fn(*diff_rows, *rest_rows, *cs))

        _, vjp = jax.vjp(f, vals[:n_diff], vals[nr:])
        drows, dcs = vjp(tuple(ct))
        off = 0
        for val, w in zip(drows[:n_lead], widths):
            drow_refs[0][:, off:off + w] = val.astype(drow_refs[0].dtype)
            off += w
        for r, val in zip(drow_refs[1:], drows[n_lead:]):
            r[...] = val.astype(r.dtype)
        first_step = pl.program_id(0) == 0
        for r, val in zip(dconst_refs, dcs):
            @pl.when(first_step)
            def _(r=r, val=val):
                r[...] = val

            @pl.when(jnp.logical_not(first_step))
            def _(r=r, val=val):
                r[...] += val

    cspecs = [pl.BlockSpec(c.shape, lambda i, n=c.ndim: (0,) * n) for c in consts]
    ctspecs = [pl.BlockSpec((tr, c.shape[1]), lambda i: (i, 0)) for c in cts]
    out_shape = [jax.ShapeDtypeStruct((S, lead_width), drow_dtypes[0])]
    out_shape += [jax.ShapeDtypeStruct((S, widths[j]), drow_dtypes[j]) for j in range(n_lead, n_diff)]
    out_shape += [jax.ShapeDtypeStruct(c.shape, F32) for c in consts]
    out_specs = [pl.BlockSpec((tr, sum(widths[:n_lead])), lambda i: (i, 0))]
    out_specs += [pl.BlockSpec((tr, widths[j]), lambda i: (i, 0)) for j in range(n_lead, n_diff)] + cspecs
    res = pl.pallas_call(
        body, name=name, out_shape=tuple(out_shape), grid=(S // tr,),
        in_specs=list(specs) + cspecs + ctspecs, out_specs=tuple(out_specs),
        compiler_params=_params(("arbitrary",)),
    )(*arrs, *consts, *cts)
    return list(res[:n_rows_out]), list(res[n_rows_out:])


def _mla_prep_fn(cq, ckv, kr, kr_sw, cos, sin, gq, gkv, wq_n, wq_r, wq_sw, wk, wv):
    hq = _rms(cq, gq)
    hkv = _rms(ckv, gkv)
    return (bdot(hq, wq_n), bdot(hq, wq_r) * cos + bdot(hq, wq_sw) * sin,
            bdot(hkv, wk), bdot(hkv, wv), kr * cos + kr_sw * sin)


def _merge_fn(g0, g1, g2, of, og, om, b0, b1, b2, wf, wg, wm):
    return (_sigmoid(g0 + b0) * bdot(of, wf) + _sigmoid(g1 + b1) * bdot(og, wg) + _sigmoid(g2 + b2) * bdot(om, wm),)


_IN_SIZES = (256, 256, 256, 4, 256, 256, 512, 16, 512, 256, 128, 32, 3072)
_IN_OFF = np.concatenate([[0], np.cumsum(_IN_SIZES)])
(_O_FQ, _O_FK, _O_FV, _O_FF, _O_GQ, _O_GK, _O_GV, _O_GLOW, _O_GR, _O_MQ, _O_MKV, _O_MKR, _O_ZG) = [int(o) for o in _IN_OFF[:-1]]
N_IN = int(_IN_OFF[-1])
_BIG_GROUPS = ((_O_ZG, 3072), (_O_GV, 512), (_O_GR, 512), (_O_FQ, 256), (_O_FK, 256), (_O_FV, 256),
               (_O_GQ, 256), (_O_GK, 256), (_O_MQ, 256), (_O_MKV, 128))
Z_GATE, Z_GV, Z_GR, Z_FQ, Z_FK, Z_FV, Z_GQ, Z_GK, Z_MQ, Z_MKV = [int(o) for o in
                                                                    np.concatenate([[0], np.cumsum([w for _, w in _BIG_GROUPS])])[:-1]]
N_BIG = sum(w for _, w in _BIG_GROUPS)
_HALF = MLA_ROPE // 2
_QK_HD = MLA_NOPE + MLA_ROPE
SM_FF, SM_GLOW, SM_KR, SM_KR_SW, N_SM = 0, 8, 128, 256, 384
N_PAD = N_BIG + N_SM
_IN_SEGS = ([(o, w, 1.0) for o, w in _BIG_GROUPS]
            + [(_O_FF, 4, 1.0), (None, SM_GLOW - 4, 0.0), (_O_GLOW, GLA_RANK, 1.0), (None, 128 - SM_GLOW - GLA_RANK, 0.0)]
            + [(_O_MKR, MLA_ROPE, 1.0)] * MLA_HEADS
            + [(_O_MKR + _HALF, _HALF, -1.0), (_O_MKR, _HALF, 1.0)] * MLA_HEADS)


def _cols(x, start, width):
    return lax.slice_in_dim(x, start, start + width, axis=x.ndim - 1)


def _pad_w_in(w):
    return jnp.concatenate([jnp.zeros(w.shape[:-1] + (n,), w.dtype) if src is None else
                            (_cols(w, src, n) if sign > 0 else -_cols(w, src, n)) for src, n, sign in _IN_SEGS], axis=-1)


def _unpad_w_in(g):
    groups = []
    for o, n in zip(_IN_OFF[:-1], _IN_SIZES):
        total, pos = None, 0
        for src, m, sign in _IN_SEGS:
            if src is not None and o <= src and src + m <= o + n:
                term = _cols(g, pos, m) if sign > 0 else -_cols(g, pos, m)
                if m != n:
                    term = jnp.pad(term, [(0, 0)] * (g.ndim - 1) + [(int(src - o), int(o + n - src - m))])
                total = term if total is None else total + term
            pos += m
        groups.append(total)
    return jnp.concatenate(groups, axis=-1)


def _take(x, idx):
    idx = np.asarray(idx)
    cuts = [0] + [i for i in range(1, len(idx)) if idx[i] != idx[i - 1] + 1] + [len(idx)]
    return jnp.concatenate([_cols(x, int(idx[a]), b - a) for a, b in zip(cuts[:-1], cuts[1:])], axis=1)


_UQ_NOPE = np.concatenate([np.arange(h * _QK_HD, h * _QK_HD + MLA_NOPE) for h in range(MLA_HEADS)])
_UQ_ROT = np.concatenate([np.arange(h * _QK_HD + MLA_NOPE, (h + 1) * _QK_HD) for h in range(MLA_HEADS)])
_UKV_PERM = np.concatenate(
    [np.concatenate([np.arange(h * 128, h * 128 + MLA_NOPE) for h in range(MLA_HEADS)]),
     np.concatenate([np.arange(h * 128 + MLA_NOPE, (h + 1) * 128) for h in range(MLA_HEADS)])])
_UKV_INV = np.argsort(_UKV_PERM)


def _rotary_partner(r):
    return jnp.concatenate([piece for h in range(MLA_HEADS) for piece in
                            (-_cols(r, h * MLA_ROPE + _HALF, _HALF), _cols(r, h * MLA_ROPE, _HALF))], axis=1)


def _uq_grad(dn, dr, dsw):
    dr = dr + jnp.concatenate([piece for h in range(MLA_HEADS) for piece in
                               (_cols(dsw, h * MLA_ROPE + _HALF, _HALF), -_cols(dsw, h * MLA_ROPE, _HALF))], axis=1)
    return jnp.concatenate([piece for h in range(MLA_HEADS) for piece in
                            (_cols(dn, h * MLA_NOPE, MLA_NOPE), _cols(dr, h * MLA_ROPE, MLA_ROPE))], axis=1)


def _rope_tables(S):
    inv = ROPE_BASE ** (-jnp.arange(_HALF, dtype=F32) / _HALF)
    ang = jnp.arange(S, dtype=F32)[:, None] * inv[None, :]
    return jnp.tile(jnp.cos(ang), (1, 2 * MLA_HEADS)), jnp.tile(jnp.sin(ang), (1, 2 * MLA_HEADS))


class _LayerParams:
    def __init__(self, rep, l):
        self.w, self.rep, self.l, self.made = {}, rep, l, {}

    def __getitem__(self, k):
        if k not in self.made:
            self.made[k] = self._make(k)
        return self.made[k]

    def _make(self, k):
        w, rep, l = self.w, self.rep, self.l
        if k == 'wg':
            return jnp.pad(w['w_gla_gate'], [(SM_GLOW, LANES - SM_GLOW - GLA_RANK), (0, 0)])
        if k in ('wq_n', 'wq_r'):
            return _take(w['w_mla_uq'], _UQ_NOPE if k == 'wq_n' else _UQ_ROT)
        if k == 'wq_sw':
            return _rotary_partner(self['wq_r'])
        if k in ('wk', 'wv'):
            return _take(w['w_mla_ukv'], _UKV_PERM[:256] if k == 'wk' else _UKV_PERM[256:])
        if k == 'b_f':
            return jnp.zeros((8, 1), F32).at[:FOX_HEADS, 0].set(rep['b_fox_forget'][l])
        if k == 'b_gate':
            return [rep['b_branch_gate'][l][i * 1024:(i + 1) * 1024].reshape(1, 1024) for i in range(3)]
        vec = {'bg': 'b_gla_gate', 'go': 'g_gla_out', 'gq': 'g_mla_q', 'gkv': 'g_mla_kv'}
        if k in vec:
            return rep[vec[k]][l].reshape(1, -1)
        return rep[k][l] if k in rep else w[k]


_GLA_COLS = (Z_GQ, Z_GK, Z_GV, Z_GR)
_MLA_OUTS = [(256, BF16), (128, BF16), (256, BF16), (256, BF16), (128, BF16)]


def _mla_rows(z, zsm, rope):
    return [(z, 256, Z_MQ), (z, 128, Z_MKV), (zsm, 128, SM_KR), (zsm, 128, SM_KR_SW), *rope]


def _mla_consts(p):
    return [p['gq'], p['gkv'], p['wq_n'], p['wq_r'], p['wq_sw'], p['wk'], p['wv']]


def _fox_qkv(z):
    return [((z, Z_FQ, 256), (z, Z_FK, 256), FOX_HD, False)], (z, Z_FV, 256)


def _mla_qkv(qn, qr, kn, vv, kr):
    return [((qn, 0, 256), (kn, 0, 256), MLA_NOPE, False), ((qr, 0, 128), (kr, 0, 128), MLA_ROPE, True)], (vv, 0, 256)


def _xa_qkv(qx, kvx):
    return [((qx, 0, 512), (kvx, 0, 512), XA_HD, False)], (kvx, 512, 512)


def _merge_rows(z, o_fox, o_gla, o_mla):
    return [(z, 1024, Z_GATE), (z, 1024, Z_GATE + 1024), (z, 1024, Z_GATE + 2048), o_fox, o_gla, o_mla]


def _merge_consts(p):
    return p['b_gate'] + [p['w_up_fox'], p['w_up_gla'], p['w_up_mla']]


def _carried(hooks, key, call, single=False):
    rider, sink = hooks.pop(key, (None, None))
    res = call(rider=rider)
    if rider is None:
        return res
    sink(res[-1])
    return res[0] if single else res[:-1]


def _layer_fwd(x0, mem, p, rope, l, hooks):
    S = x0.shape[0]
    sv = {'x0': x0}

    def mm(key, a, b, **kw):
        return _carried(hooks, (l, key), lambda rider: _mm(a, b, mode='nn', rider=rider, name=f"{key}_{l}", **kw), single=True)

    h1 = _rms_fwd(x0, p['g_mix'], name=f"rms_mix_{l}")
    z = mm('in_big', h1, p['w_in'], out_dtype=BF16, b_cols=(0, N_BIG))
    zsm = _mm(h1, p['w_in'], mode='nn', out_dtype=F32, b_cols=(N_BIG, N_SM), name=f"in_small_{l}")
    sv.update(h1=h1, z=z, zsm=zsm)
    ff_t = jnp.zeros((8, S), F32).at[:FOX_HEADS].set(zsm[:, SM_FF:SM_FF + FOX_HEADS].T)
    cum_t = _fox_cum_fwd(ff_t, p['b_f'], name=f"fox_cum_{l}")
    cum = cum_t.T
    o_fox, lse_f = _carried(hooks, (l, 'fox_fwd'), lambda rider: _attn_fwd(
        *_fox_qkv(z), FOX_HEADS, cum_t, cum, scale=FOX_HD ** -0.5, mask='causal', name=f"fox_fwd_{l}", rider=rider))
    sv.update(ff_t=ff_t, cum=cum, cum_t=cum_t, lse_f=lse_f, o_fox=o_fox)
    o_gla, states = _gla_fwd(z, zsm, p['wg'], p['bg'], p['go'], _GLA_COLS, name=f"gla_fwd_{l}")
    sv.update(o_gla=o_gla, states=states)
    mla = _stage_fwd(_mla_prep_fn, _mla_rows(z, zsm, rope), _mla_consts(p), _MLA_OUTS, name=f"mla_prep_{l}")
    o_mla, lse_m = _carried(hooks, (l, 'mla_fwd'), lambda rider: _attn_fwd(
        *_mla_qkv(*mla), MLA_HEADS, None, None, scale=_QK_HD ** -0.5, mask='chunk', name=f"mla_fwd_{l}", rider=rider))
    sv.update(mla=mla, lse_m=lse_m, o_mla=o_mla)
    (y,) = _stage_fwd(_merge_fn, _merge_rows(z, o_fox, o_gla, o_mla), _merge_consts(p), [(1024, BF16)], name=f"merge_{l}")
    x1 = mm('out_proj', y, p['w_out'], out_dtype=F32, residual=x0)
    sv.update(y=y, x1=x1)
    h2 = _rms_fwd(x1, p['g_xa'], name=f"rms_xa_{l}")
    hm = _rms_fwd(mem, p['g_mem'], name=f"rms_mem_{l}")
    qx = _mm(h2, p['w_xq'], mode='nn', out_dtype=BF16, name=f"xq_{l}")
    kvx = _mm(hm, p['w_xkv'], mode='nn', out_dtype=BF16, name=f"xkv_{l}")
    ox, lse_x = _carried(hooks, (l, 'xa_fwd'), lambda rider: _attn_fwd(
        *_xa_qkv(qx, kvx), XA_HEADS, None, None, scale=XA_HD ** -0.5, mask=None, name=f"xa_fwd_{l}", rider=rider))
    x2 = mm('xo', ox, p['w_xo'], out_dtype=F32, residual=x1)
    sv.update(h2=h2, hm=hm, qx=qx, kvx=kvx, lse_x=lse_x, ox=ox, x2=x2)
    h3 = _rms_fwd(x2, p['g_mlp'], name=f"rms_mlp_{l}")
    a = mm('mlp1', h3, p['w_mlp1'], out_dtype=BF16)
    x3 = mm('mlp2', a, p['w_mlp2'], out_dtype=F32, act='relu2', residual=x2)
    sv.update(h3=h3, a=a)
    return x3, sv


def _layer_bwd(dx3, dx3b, mem, p, rope, sv, l, hooks, half_done):
    S = dx3.shape[0]
    g = {}

    def dw(key, a, b, **kw):
        return _mm(a, b, mode='tn', out_dtype=BF16, col_shards=b.shape[1] // LANES, name=f"d_{key}_{l}", **kw)

    da = _mm(dx3b, p['w_mlp2'], mode='nt', out_dtype=BF16, drelu_of=sv['a'], name=f"d_mlp2_in_{l}")
    g['w_mlp2'] = dw('w_mlp2', sv['a'], dx3b, act='relu2')
    dx2, dx2b, g['g_mlp'] = _mm(da, p['w_mlp1'], mode='nt', out_dtype=F32, norm_bwd=(sv['x2'], p['g_mlp'], dx3), tm=512,
                                name=f"d_mlp1_in_{l}")
    g['w_mlp1'] = dw('w_mlp1', sv['h3'], da)
    dox = _mm(dx2b, p['w_xo'], mode='nt', out_dtype=BF16, name=f"d_xo_in_{l}")
    g['w_xo'] = dw('w_xo', sv['ox'], dx2b)
    (dqx,), (dkx,), dvx = _attn_bwd(*_xa_qkv(sv['qx'], sv['kvx']), XA_HEADS, sv['ox'], dox, sv['lse_x'], None, None,
                                    scale=XA_HD ** -0.5, mask=None, name=f"xa_bwd_{l}")
    dqx = dqx.astype(BF16)
    dkvx = jnp.concatenate([dkx, dvx], axis=1).astype(BF16)
    dx1, dx1b, g['g_xa'] = _mm(dqx, p['w_xq'], mode='nt', out_dtype=F32, norm_bwd=(sv['x1'], p['g_xa'], dx2), tm=512,
                               name=f"d_xq_in_{l}")
    g['w_xq'] = dw('w_xq', sv['h2'], dqx)
    dhm = _mm(dkvx, p['w_xkv'], mode='nt', out_dtype=F32, name=f"d_xkv_in_{l}")
    g['w_xkv'] = dw('w_xkv', sv['hm'], dkvx)
    _, _, g['g_mem'] = _rms_bwd(mem, p['g_mem'], dhm, None, name=f"d_rms_mem_{l}")
    dy = _mm(dx1b, p['w_out'], mode='nt', out_dtype=F32, name=f"d_out_in_{l}")
    g['w_out'] = dw('w_out', sv['y'], dx1b)
    z, zsm = sv['z'], sv['zsm']
    (dz, do_fox, do_gla, do_mla), (db0, db1, db2, g['w_up_fox'], g['w_up_gla'], g['w_up_mla']) = _stage_bwd(
        _merge_fn, _merge_rows(z, sv['o_fox'], sv['o_gla'], sv['o_mla']), _merge_consts(p), [dy], 6, [BF16] * 6,
        lead=(3, N_PAD), name=f"merge_bwd_{l}")
    g['b_branch_gate'] = jnp.concatenate([db0, db1, db2], axis=1).reshape(-1)
    half_done(l, g)
    (dfq,), (dfk,), dfv, dck, dcq = _carried(hooks, (l, 'fox_bwd'), lambda rider: _attn_bwd(
        *_fox_qkv(z), FOX_HEADS, sv['o_fox'], do_fox, sv['lse_f'], sv['cum_t'], sv['cum'],
        scale=FOX_HD ** -0.5, mask='causal', name=f"fox_bwd_{l}", rider=rider))
    dff_t, db_f = _fox_cum_bwd(sv['ff_t'], p['b_f'], dcq + dck.T, name=f"fox_cum_bwd_{l}")
    g['b_fox_forget'] = db_f[:FOX_HEADS, 0]
    dgq, dgk, dgv, dgr, dzsm, dwg, dbg, dgo = _gla_bwd(z, zsm, p['wg'], p['bg'], p['go'], sv['states'], do_gla, _GLA_COLS,
                                                       name=f"gla_bwd_{l}")
    g['w_gla_gate'] = dwg[SM_GLOW:SM_GLOW + GLA_RANK]
    g['b_gla_gate'] = dbg.reshape(-1)
    g['g_gla_out'] = dgo.reshape(-1)
    (dmqn, dmqr), (dmkn, dmkr), dmv = _carried(hooks, (l, 'mla_bwd'), lambda rider: _attn_bwd(
        *_mla_qkv(*sv['mla']), MLA_HEADS, sv['o_mla'], do_mla, sv['lse_m'], None, None,
        scale=_QK_HD ** -0.5, mask='chunk', name=f"mla_bwd_{l}", rider=rider))
    (dcq, dckv, dkr, dkr_sw), (dgq_n, dgkv_n, dwq_n, dwq_r, dwq_sw, dwk, dwv) = _stage_bwd(
        _mla_prep_fn, _mla_rows(z, zsm, rope), _mla_consts(p), [dmqn, dmqr, dmkn, dmv, dmkr], 4, [BF16] * 4,
        name=f"mla_prep_bwd_{l}")
    g['g_mla_q'] = dgq_n.reshape(-1)
    g['g_mla_kv'] = dgkv_n.reshape(-1)
    g['w_mla_uq'] = _uq_grad(dwq_n, dwq_r, dwq_sw)
    g['w_mla_ukv'] = _take(jnp.concatenate([dwk, dwv], axis=1), _UKV_INV)
    dsm = dzsm + jnp.pad(dff_t[:FOX_HEADS].T, [(0, 0), (0, 128 - FOX_HEADS)])
    dz = lax.dynamic_update_slice(dz, jnp.concatenate(
        [dgv, dgr, dfq.astype(BF16), dfk.astype(BF16), dfv.astype(BF16), dgq, dgk, dcq, dckv, dsm.astype(BF16), dkr, dkr_sw],
        axis=1), (0, Z_GV))
    dx0, dx0b, g['g_mix'] = _mm(dz, p['w_in'], mode='nt', out_dtype=F32, norm_bwd=(sv['x0'], p['g_mix'], dx1), tm=512,
                                tk=N_PAD // 2, name=f"d_in_{l}")
    g['w_in'] = dw('w_in', sv['h1'], dz, tn=N_PAD // 3)
    for n in ('g_mlp', 'g_mem', 'g_xa', 'g_mix'):
        g[n] = g[n].reshape(-1)
    return dx0, dx0b, g


def _local_step(x, mem, target, ps, g_final, hooks, half_done, layer_done):
    rope = _rope_tables(x.shape[0])
    saved = []
    for l, p in enumerate(ps):
        x, sv = _layer_fwd(x, mem, p, rope, l, hooks)
        saved.append(sv)
    loss, dx, dxb, dgf = _loss_head(x, g_final, target, name="loss_head")
    for l in reversed(range(len(ps))):
        dx, dxb, grads = _layer_bwd(dx, dxb, mem, ps[l], rope, saved[l], l, hooks, half_done)
        layer_done(l, grads)
    assert not hooks, f"exchanges without a carrier: {list(hooks)}"
    return loss, dx, dgf.reshape(-1)


_MESH_AXES = ("x", "y", "c")
_HBM = pl.BlockSpec(memory_space=pl.ANY)


N_CHIP = 4
_SLOT_ROWS = (2048, 1024, 512, 256, 128, 64, 32, 16, 8)


def _place():
    x, y, c = (lax.axis_index(n) for n in _MESH_AXES)
    return (x, y, c), (x, y, 1 - c), [(1 - x, y), (x, 1 - y), (1 - x, 1 - y)]


def _remote(src, dst, sems, k, to):
    return pltpu.make_async_remote_copy(src_ref=src, dst_ref=dst, send_sem=sems[0].at[k], recv_sem=sems[1].at[k],
                                        device_id=to, device_id_type=pl.DeviceIdType.MESH)


def _all_gather(x, *, name):
    def body(x_ref, o_ref, send_sems, recv_sems, local_sem):
        me, sib, chips = _place()
        c = me[2]
        sems = (send_sems, recv_sems)
        slot = lambda px, py, pc: o_ref.at[4 * px + 2 * py + pc]
        mine = pltpu.make_async_copy(x_ref, slot(*me), local_sem)
        mine.start()
        first = [_remote(x_ref, slot(*me), sems, 0, sib)]
        first += [_remote(x_ref, slot(*me), sems, 1 + j, (*chip, c)) for j, chip in enumerate(chips)]
        for cp in first:
            cp.start()
        passed = [_remote(slot(*chip, c), slot(*chip, c), sems, 4 + j, sib) for j, chip in enumerate(chips)]
        for j, chip in enumerate(chips):
            _remote(x_ref, slot(*chip, c), sems, 1 + j, me).wait_recv()
            passed[j].start()
        _remote(x_ref, slot(*sib), sems, 0, me).wait_recv()
        for j, chip in enumerate(chips):
            _remote(x_ref, slot(*chip, 1 - c), sems, 4 + j, me).wait_recv()
        for cp in first + passed:
            cp.wait_send()
        mine.wait()

    return pl.pallas_call(
        body, name=name, out_shape=jax.ShapeDtypeStruct((N_DEV,) + x.shape, x.dtype),
        in_specs=[_HBM], out_specs=_HBM,
        scratch_shapes=[pltpu.SemaphoreType.DMA((N_DEV - 1,)), pltpu.SemaphoreType.DMA((N_DEV - 1,)), pltpu.SemaphoreType.DMA],
        compiler_params=pltpu.CompilerParams(has_side_effects=True),
    )(x)


class _Rider:
    def __init__(self, inputs, out_shapes, scratch, start, finish, post):
        self.inputs, self.out_shapes, self.scratch = list(inputs), list(out_shapes), list(scratch)
        self.start, self.finish, self.post = start, finish, post


def _run_rider(rider, *, name):
    def body(*refs):
        rider.start(refs)
        rider.finish(refs)

    outs = pl.pallas_call(
        body, name=name, out_shape=tuple(rider.out_shapes), in_specs=[_HBM] * len(rider.inputs),
        out_specs=(_HBM,) * len(rider.out_shapes), scratch_shapes=rider.scratch,
        compiler_params=pltpu.CompilerParams(has_side_effects=True),
    )(*rider.inputs)
    return rider.post(outs)


def _carry(rider, n_in, n_out, first, last):
    if rider is None:
        return [], [], [], [], [], lambda refs: refs
    ni, no = len(rider.inputs), len(rider.out_shapes)

    def split(refs):
        own_in, r_in = refs[:n_in], refs[n_in:n_in + ni]
        own_out, r_out = refs[n_in + ni:n_in + ni + n_out], refs[n_in + ni + n_out:n_in + ni + n_out + no]
        rest = refs[n_in + ni + n_out + no:]
        own_scr, r_scr = rest[:len(rest) - len(rider.scratch)], rest[len(rest) - len(rider.scratch):]
        rrefs = tuple(r_in) + tuple(r_out) + tuple(r_scr)
        pl.when(first())(lambda: rider.start(rrefs))
        pl.when(last())(lambda: rider.finish(rrefs))
        return tuple(own_in) + tuple(own_out) + tuple(own_scr)

    return list(rider.inputs), [_HBM] * ni, list(rider.out_shapes), [_HBM] * no, list(rider.scratch), split


def _gather_rider(shards, axes):
    n = len(shards)
    srcs, out_shapes, kinds = [], [], []
    for s, ax in zip(shards, axes):
        L, a, b = s.shape
        if ax == 1:
            srcs.append(s.reshape(L, 1, a, b)), out_shapes.append((L, N_DEV, a, b)), kinds.append('row')
        elif b % 128 == 0:
            srcs.append(s), out_shapes.append((L, a, N_DEV * b)), kinds.append('col')
        else:
            srcs.append(s.reshape(1, L, a, b)), out_shapes.append((N_DEV, L, a, b)), kinds.append('slot')

    def parts(refs):
        x_refs, o_refs = refs[:n], refs[n:2 * n]
        send_sems, recv_sems, local_sem = refs[2 * n:]
        me, sib, chips = _place()
        sems = (send_sems, recv_sems)

        def win(t, px, py, pc):
            idx = 4 * px + 2 * py + pc
            if kinds[t] == 'row':
                return o_refs[t].at[:, pl.ds(idx, 1)]
            if kinds[t] == 'col':
                width = shards[t].shape[2]
                return o_refs[t].at[:, :, pl.ds(pl.multiple_of(idx * width, 128), width)]
            return o_refs[t].at[pl.ds(idx, 1)]

        def group(k, block, to, own):
            return [_remote(x_refs[t] if own else win(t, *block), win(t, *block), sems, k * n + t, to) for t in range(n)]

        mine = [pltpu.make_async_copy(x_refs[t], win(t, *me), local_sem.at[t]) for t in range(n)]
        first = group(0, me, sib, True)
        for j, chip in enumerate(chips):
            first += group(1 + j, me, (*chip, me[2]), True)
        return me, sib, chips, group, mine, first

    def start(refs):
        *_, mine, first = parts(refs)
        for cp in mine + first:
            cp.start()

    def finish(refs):
        me, sib, chips, group, mine, first = parts(refs)
        c = me[2]
        passed = []
        for j, chip in enumerate(chips):
            for cp in group(1 + j, (*chip, c), me, False):
                cp.wait_recv()
            fwd = group(4 + j, (*chip, c), sib, False)
            for cp in fwd:
                cp.start()
            passed += fwd
        for cp in group(0, sib, me, False):
            cp.wait_recv()
        for j, chip in enumerate(chips):
            for cp in group(4 + j, (*chip, 1 - c), me, False):
                cp.wait_recv()
        for cp in first + passed:
            cp.wait_send()
        for cp in mine:
            cp.wait()

    def post(outs):
        whole = []
        for o, s, kind in zip(outs, shards, kinds):
            L, a, b = s.shape
            whole.append(o.reshape(L, N_DEV * a, b) if kind == 'row' else o if kind == 'col' else _to_whole(o, 2))
        return whole

    return _Rider(srcs, [jax.ShapeDtypeStruct(shp, s.dtype) for shp, s in zip(out_shapes, shards)],
                  [pltpu.SemaphoreType.DMA(((N_DEV - 1) * n,)), pltpu.SemaphoreType.DMA(((N_DEV - 1) * n,)),
                   pltpu.SemaphoreType.DMA((n,))], start, finish, post)


def _sibling_swap(x, *, name):
    def body(x_ref, o_ref, send_sems, recv_sems):
        me, sib, _ = _place()
        c = me[2]
        sems = (send_sems, recv_sems)
        sends = [_remote(x_ref.at[j, 1 - c], o_ref.at[j], sems, j, sib) for j in range(N_CHIP)]
        for cp in sends:
            cp.start()
        for cp in sends:
            cp.wait_send()
            cp.wait_recv()

    return pl.pallas_call(
        body, name=name, out_shape=jax.ShapeDtypeStruct((N_CHIP,) + x.shape[2:], x.dtype),
        in_specs=[_HBM], out_specs=_HBM,
        scratch_shapes=[pltpu.SemaphoreType.DMA((N_CHIP,)), pltpu.SemaphoreType.DMA((N_CHIP,))],
        compiler_params=pltpu.CompilerParams(has_side_effects=True),
    )(x)


def _pair_sum(x, got, c, *, name):
    _, _, R, _ = x.shape
    tr = _pick(R, _SLOT_ROWS)

    def body(c_ref, x_ref, g_ref, o_ref):
        o_ref[...] = (x_ref[...].astype(F32) + g_ref[...].astype(F32)).astype(o_ref.dtype)

    return pl.pallas_call(
        body, name=name, out_shape=jax.ShapeDtypeStruct((N_CHIP, R, 128), x.dtype),
        grid_spec=pltpu.PrefetchScalarGridSpec(
            num_scalar_prefetch=1, grid=(R // tr,),
            in_specs=[pl.BlockSpec((N_CHIP, None, tr, 128), lambda i, c_ref: (0, c_ref[0], i, 0)),
                      pl.BlockSpec((N_CHIP, tr, 128), lambda i, c_ref: (0, i, 0))],
            out_specs=pl.BlockSpec((N_CHIP, tr, 128), lambda i, c_ref: (0, i, 0))),
        compiler_params=_params(("parallel",)),
    )(c, x, got)


def _chip_all_to_all_rider(x):
    def parts(refs):
        x_ref, o_ref, send_sems, recv_sems, local_sem = refs
        me, _, chips = _place()
        sems = (send_sems, recv_sems)
        mine = 2 * me[0] + me[1]
        local = pltpu.make_async_copy(x_ref.at[mine], o_ref.at[mine], local_sem)
        sends = [_remote(x_ref.at[2 * px + py], o_ref.at[mine], sems, j, (px, py, me[2])) for j, (px, py) in enumerate(chips)]
        arrival = lambda j: _remote(x_ref.at[mine], o_ref.at[2 * chips[j][0] + chips[j][1]], sems, j, me)
        return local, sends, arrival

    def start(refs):
        local, sends, _ = parts(refs)
        for cp in [local] + sends:
            cp.start()

    def finish(refs):
        local, sends, arrival = parts(refs)
        for j, cp in enumerate(sends):
            cp.wait_send()
            arrival(j).wait_recv()
        local.wait()

    return _Rider([x], [jax.ShapeDtypeStruct(x.shape, x.dtype)],
                  [pltpu.SemaphoreType.DMA((N_CHIP - 1,)), pltpu.SemaphoreType.DMA((N_CHIP - 1,)), pltpu.SemaphoreType.DMA],
                  start, finish, lambda outs: outs[0])


def _sum_slots(x, *, name):
    n, R, _ = x.shape
    tr = _pick(R, _SLOT_ROWS)

    def body(x_ref, o_ref):
        acc = x_ref[0].astype(F32)
        for j in range(1, n):
            acc = acc + x_ref[j].astype(F32)
        o_ref[...] = acc

    return pl.pallas_call(
        body, name=name, out_shape=jax.ShapeDtypeStruct((R, 128), F32), grid=(R // tr,),
        in_specs=[pl.BlockSpec((n, tr, 128), lambda i: (0, i, 0))], out_specs=pl.BlockSpec((tr, 128), lambda i: (i, 0)),
        compiler_params=_params(("parallel",)),
    )(x)


def _adamw(w, g, m, v, *, name):
    shape = w.shape
    cols = shape[-1]
    rows = int(np.prod(shape[:-1]))
    tr = next((t for t in (1024, 512, 256, 128, 64, 32, 16, 8) if rows % t == 0 and t * cols * 4 <= (1 << 20)), rows)

    def body(w_ref, g_ref, m_ref, v_ref, d_ref, mo_ref, vo_ref):
        g_ = g_ref[...]
        m_ = ADAM_B1 * m_ref[...] + (1.0 - ADAM_B1) * g_
        v_ = ADAM_B2 * v_ref[...] + (1.0 - ADAM_B2) * jnp.square(g_)
        m_hat = m_ / (1.0 - ADAM_B1 ** ADAM_STEP)
        v_hat = v_ / (1.0 - ADAM_B2 ** ADAM_STEP)
        d_ref[...] = -ADAM_LR * (m_hat / (jnp.sqrt(v_hat) + ADAM_EPS) + ADAM_WD * w_ref[...])
        mo_ref[...] = m_
        vo_ref[...] = v_

    blk = pl.BlockSpec((tr, cols), lambda i: (i, 0))
    outs = pl.pallas_call(
        body, name=name, out_shape=tuple(jax.ShapeDtypeStruct((rows, cols), F32) for _ in range(3)), grid=(rows // tr,),
        in_specs=[blk] * 4, out_specs=(blk,) * 3, compiler_params=_params(("parallel",)),
    )(*(a.reshape(rows, cols) for a in (w, g, m, v)))
    return tuple(o.reshape(shape) for o in outs)


_WEIGHTS = ('g_mix', 'w_in', 'b_fox_forget', 'w_gla_gate', 'b_gla_gate', 'g_gla_out', 'g_mla_q', 'w_mla_uq', 'g_mla_kv',
            'w_mla_ukv', 'b_branch_gate', 'w_up_fox', 'w_up_gla', 'w_up_mla', 'w_out', 'g_xa', 'g_mem', 'w_xq', 'w_xkv',
            'w_xo', 'g_mlp', 'w_mlp1', 'w_mlp2', 'g_final')
_SHARDED = (('w_in', 1), ('w_gla_gate', 2), ('w_mla_uq', 2), ('w_mla_ukv', 2), ('w_up_fox', 2), ('w_up_gla', 2),
            ('w_up_mla', 2), ('w_out', 1), ('w_xq', 1), ('w_xkv', 1), ('w_xo', 2), ('w_mlp1', 2), ('w_mlp2', 1))
_REPLICATED = tuple(n for n in _WEIGHTS if n not in dict(_SHARDED))
_ROW_PAD = 1024
_SMALL_ROW_PAD = 8
_PIECE_ROWS = 16


def _pack(flats, lead, row_pad=_ROW_PAD):
    if all(int(np.prod(a.shape[lead:])) % 128 == 0 for a in flats):
        def block(a):
            a = a.reshape(a.shape[:lead] + (-1, 128))
            return jnp.pad(a, [(0, 0)] * lead + [(0, -a.shape[lead] % _PIECE_ROWS), (0, 0)])
        cat = jnp.concatenate([block(a) for a in flats], axis=lead)
        rows = cat.shape[lead]
        return jnp.pad(cat, [(0, 0)] * lead + [(0, -(-rows // row_pad) * row_pad - rows), (0, 0)])
    cat = jnp.concatenate([a.reshape(a.shape[:lead] + (-1,)) for a in flats], axis=-1)
    n = cat.shape[-1]
    total = -(-n // (128 * row_pad)) * (128 * row_pad)
    cat = jnp.pad(cat, [(0, 0)] * lead + [(0, total - n)])
    return cat.reshape(cat.shape[:lead] + (total // 128, 128))


def _unpack(buf, shapes, lead):
    sizes = [int(np.prod(shp)) for shp in shapes]
    out, off = [], 0
    if all(n % 128 == 0 for n in sizes):
        for shp, n in zip(shapes, sizes):
            rows = buf[(slice(None),) * lead + (slice(off, off + n // 128),)]
            out.append(rows.reshape(buf.shape[:lead] + tuple(shp)))
            off += -(-(n // 128) // _PIECE_ROWS) * _PIECE_ROWS
        return out
    flat = buf.reshape(buf.shape[:lead] + (-1,))
    for shp, n in zip(shapes, sizes):
        out.append(flat[..., off:off + n].reshape(buf.shape[:lead] + tuple(shp)))
        off += n
    return out


def _to_whole(g, axis):
    if axis == 1:
        return g.transpose(1, 0, 2, 3).reshape(g.shape[1], N_DEV * g.shape[2], g.shape[3])
    return g.transpose(1, 2, 0, 3).reshape(g.shape[1], g.shape[2], N_DEV * g.shape[3])


def _to_shards(w, axis):
    L, R, C = w.shape
    if axis == 1:
        return w.reshape(L, N_DEV, R // N_DEV, C).transpose(1, 0, 2, 3)
    return w.reshape(L, R, N_DEV, C // N_DEV).transpose(2, 0, 1, 3)


def kernel(x, mem, g_mix, w_in, b_fox_forget, w_gla_gate, b_gla_gate, g_gla_out, g_mla_q, w_mla_uq, g_mla_kv, w_mla_ukv, b_branch_gate, w_up_fox, w_up_gla, w_up_mla, w_out, g_xa, g_mem, w_xq, w_xkv, w_xo, g_mlp, w_mlp1, w_mlp2, g_final, loss_target, m_g_mix, m_w_in, m_b_fox_forget, m_w_gla_gate, m_b_gla_gate, m_g_gla_out, m_g_mla_q, m_w_mla_uq, m_g_mla_kv, m_w_mla_ukv, m_b_branch_gate, m_w_up_fox, m_w_up_gla, m_w_up_mla, m_w_out, m_g_xa, m_g_mem, m_w_xq, m_w_xkv, m_w_xo, m_g_mlp, m_w_mlp1, m_w_mlp2, m_g_final, v_g_mix, v_w_in, v_b_fox_forget, v_w_gla_gate, v_b_gla_gate, v_g_gla_out, v_g_mla_q, v_w_mla_uq, v_g_mla_kv, v_w_mla_ukv, v_b_branch_gate, v_w_up_fox, v_w_up_gla, v_w_up_mla, v_w_out, v_g_xa, v_g_mem, v_w_xq, v_w_xkv, v_w_xo, v_g_mlp, v_w_mlp1, v_w_mlp2, v_g_final):
    wts = dict(zip(_WEIGHTS, (g_mix, w_in, b_fox_forget, w_gla_gate, b_gla_gate, g_gla_out, g_mla_q, w_mla_uq, g_mla_kv,
                              w_mla_ukv, b_branch_gate, w_up_fox, w_up_gla, w_up_mla, w_out, g_xa, g_mem, w_xq, w_xkv, w_xo,
                              g_mlp, w_mlp1, w_mlp2, g_final)))
    mom1 = dict(zip(_WEIGHTS, (m_g_mix, m_w_in, m_b_fox_forget, m_w_gla_gate, m_b_gla_gate, m_g_gla_out, m_g_mla_q,
                               m_w_mla_uq, m_g_mla_kv, m_w_mla_ukv, m_b_branch_gate, m_w_up_fox, m_w_up_gla, m_w_up_mla,
                               m_w_out, m_g_xa, m_g_mem, m_w_xq, m_w_xkv, m_w_xo, m_g_mlp, m_w_mlp1, m_w_mlp2, m_g_final)))
    mom2 = dict(zip(_WEIGHTS, (v_g_mix, v_w_in, v_b_fox_forget, v_w_gla_gate, v_b_gla_gate, v_g_gla_out, v_g_mla_q,
                               v_w_mla_uq, v_g_mla_kv, v_w_mla_ukv, v_b_branch_gate, v_w_up_fox, v_w_up_gla, v_w_up_mla,
                               v_w_out, v_g_xa, v_g_mem, v_w_xq, v_w_xkv, v_w_xo, v_g_mlp, v_w_mlp1, v_w_mlp2, v_g_final)))
    depth = g_mix.shape[0]

    names = [n for n, _ in _SHARDED]
    axes = dict(_SHARDED)
    shard = {n: wts[n] for n in names}
    shard['w_in'] = _pad_w_in(w_in)
    rep = {n: wts[n] for n in _REPLICATED}
    ps = [_LayerParams(rep, l) for l in range(depth)]

    def gather(group, l):
        rider = _gather_rider([shard[n][l:l + 1].astype(BF16) for n in group], [axes[n] for n in group])
        return rider, lambda whole: ps[l].w.update({n: w[0] for n, w in zip(group, whole)})

    first, sink = gather(['w_in'], 0)
    sink(_run_rider(first, name="gather_w_in_0"))
    narrow = ['w_gla_gate', 'w_mla_uq', 'w_mla_ukv', 'w_up_fox', 'w_up_gla', 'w_up_mla']
    hooks = {(0, 'in_big'): gather(narrow + ['w_out', 'w_xq', 'w_xkv', 'w_xo'], 0),
             (0, 'fox_fwd'): gather(['w_mlp1', 'w_mlp2'], 0)}
    ahead = (('mla_fwd', ['w_in'] + narrow), ('out_proj', ['w_out']), ('xa_fwd', ['w_xq', 'w_xo']), ('xo', ['w_xkv']),
             ('mlp1', ['w_mlp1']), ('mlp2', ['w_mlp2']))
    assert sorted(n for _, group in ahead for n in group) == sorted(names)
    for l in range(1, depth):
        for key, group in ahead:
            hooks[(l - 1, key)] = gather(group, l)

    core = lax.axis_index("c").astype(jnp.int32).reshape(1)
    late = ['w_in', 'w_gla_gate', 'w_mla_uq', 'w_mla_ukv']
    groups = {'early': [n for n in names if n not in late], 'late': late}
    small_grads, landed = {}, {}

    def to_slots(gl, axis):
        if gl.ndim != 3:
            return _to_shards(gl[None], axis)
        blocks, rows, _ = gl.shape
        if axis == 1:
            return gl.reshape(blocks, N_DEV, rows // N_DEV, LANES).transpose(1, 0, 2, 3)
        return gl.reshape(N_DEV, blocks // N_DEV, rows, LANES)

    def slot_shape(n):
        _, a, b = shard[n].shape
        return (b // LANES, a, LANES) if n in blocked else (1, a, b)

    def from_slot(n, x):
        return x.transpose(1, 0, 2).reshape((1,) + shard[n].shape[1:]) if n in blocked else x

    blocked = {'w_in', 'w_out', 'w_xq', 'w_xkv', 'w_xo', 'w_mlp1', 'w_mlp2'}

    def exchange(l, g, which):
        assert all((g[n].ndim == 3) == (n in blocked) for n in groups[which])
        slots = _pack([to_slots(g[n], axes[n]).astype(BF16) for n in groups[which]], 1)
        slots = slots.reshape((N_CHIP, 2) + slots.shape[1:])
        paired = _pair_sum(slots, _sibling_swap(slots, name=f"swap_grads_{which}_{l}"), core, name=f"pair_grads_{which}_{l}")
        return _chip_all_to_all_rider(paired), lambda got: landed.update({(l, which): got})

    def half_done(l, g):
        hooks[(l, 'mla_bwd')] = exchange(l, g, 'early')

    def layer_done(l, g):
        small_grads[l] = g
        rider, sink = exchange(l, g, 'late')
        if l > 0:
            hooks[(l - 1, 'fox_bwd')] = (rider, sink)
        else:
            sink(_run_rider(rider, name=f"scatter_grads_late_{l}"))

    loss, dx, dg_final = _local_step(x[0], mem[0], loss_target[0], ps, g_final, hooks, half_done, layer_done)
    loss = lax.psum(loss[0, 0], _MESH_AXES)

    grad = {}
    for which, group in groups.items():
        shapes = [slot_shape(n) for n in group]
        per_layer = [_unpack(_sum_slots(landed[(l, which)], name=f"sum_grads_{which}_{l}"), shapes, 0) for l in range(depth)]
        grad.update({n: jnp.concatenate([from_slot(n, per_layer[l][i]) for l in range(depth)], axis=0)
                     for i, n in enumerate(group)})
    grad['w_in'] = _unpad_w_in(grad['w_in'])
    grads = small_grads
    small = [dg_final if n == 'g_final' else jnp.stack([grads[l][n] for l in range(depth)]) for n in _REPLICATED]
    small_shapes = [wts[n].shape for n in _REPLICATED]
    small_sum = _sum_slots(_all_gather(_pack(small, 0, _SMALL_ROW_PAD), name="gather_small_grads"), name="sum_small_grads")
    grad.update(dict(zip(_REPLICATED, _unpack(small_sum, small_shapes, 0))))

    delta, new_m, new_v = {}, {}, {}
    for n, _ in _SHARDED:
        delta[n], new_m[n], new_v[n] = _adamw(wts[n], grad[n], mom1[n], mom2[n], name=f"adamw_{n}")
    packed = [_pack([d[n] for n in _REPLICATED], 0, _SMALL_ROW_PAD) for d in (wts, mom1, mom2)]
    outs = _adamw(packed[0], small_sum, packed[1], packed[2], name="adamw_small")
    for d, o in zip((delta, new_m, new_v), outs):
        d.update(dict(zip(_REPLICATED, _unpack(o, small_shapes, 0))))

    return (loss, dx[None], *[grad[n] for n in _WEIGHTS], *[delta[n] for n in _WEIGHTS],
            *[new_m[n] for n in _WEIGHTS], *[new_v[n] for n in _WEIGHTS])
```

```python
import functools

import jax
import jax.numpy as jnp
import numpy as np
from jax import lax
from jax.experimental import pallas as pl
from jax.experimental.pallas import tpu as pltpu

F32 = jnp.float32
BF16 = jnp.bfloat16

EPS = 1e-6
CHUNK = 64
FOX_HEADS, FOX_HD = 4, 64
GLA_HEADS, GLA_DK, GLA_DV, GLA_RANK, GLA_TAU = 4, 64, 128, 16, 16.0
MLA_HEADS, MLA_Q_RANK, MLA_KV_RANK, MLA_NOPE, MLA_ROPE, MLA_VD = 4, 256, 128, 64, 32, 64
ROPE_BASE = 10000.0
XA_HEADS, XA_HD = 4, 128
ADAM_LR, ADAM_B1, ADAM_B2, ADAM_EPS, ADAM_WD, ADAM_STEP = 0.001, 0.9, 0.999, 1e-08, 0.01, 10

N_DEV = 8
V7X_VMEM_LIMIT = 56 * 1024 * 1024
NEG = -1e30

NN = ((1,), (0,))
NT = ((1,), (1,))
TN = ((0,), (0,))


def _dot(a, b, dims):
    return lax.dot_general(a.astype(BF16), b.astype(BF16), (dims, ((), ())), preferred_element_type=F32)


@jax.custom_vjp
def bdot(a, b):
    return _dot(a, b, NN)


bdot.defvjp(lambda a, b: (_dot(a, b, NN), (a, b)),
            lambda res, g: (_dot(g, res[1], NT), _dot(res[0], g, TN)))


@jax.custom_vjp
def bdot_nt(a, b):
    return _dot(a, b, NT)


bdot_nt.defvjp(lambda a, b: (_dot(a, b, NT), (a, b)),
               lambda res, g: (_dot(g, res[1], NN), _dot(g, res[0], TN)))


@jax.custom_vjp
def bdot_tn(a, b):
    return _dot(a, b, TN)


bdot_tn.defvjp(lambda a, b: (_dot(a, b, TN), (a, b)),
               lambda res, g: (_dot(res[1], g, NT), _dot(res[0], g, NN)))


def _split2(x):
    hi = x.astype(BF16)
    lo = (x - hi.astype(F32)).astype(BF16)
    return hi, lo


def _tri(n, lower):
    r = lax.broadcasted_iota(jnp.int32, (n, n), 0)
    c = lax.broadcasted_iota(jnp.int32, (n, n), 1)
    return jnp.where((r >= c) if lower else (r <= c), 1.0, 0.0).astype(BF16)


def _log_sigmoid(x):
    return jnp.minimum(x, 0.0) - jnp.log(1.0 + jnp.exp(-jnp.abs(x)))


def _sigmoid(x):
    return 1.0 / (1.0 + jnp.exp(-x))


def _rms(x, g):
    return x * lax.rsqrt(jnp.mean(x * x, axis=-1, keepdims=True) + EPS) * g


def _pick(dim, prefs):
    for p in prefs:
        if dim % p == 0:
            return p
    return dim


def _params(sem):
    return pltpu.CompilerParams(dimension_semantics=sem, vmem_limit_bytes=V7X_VMEM_LIMIT)


def _rms_vjp(x, g, dy, dres):
    rstd = lax.rsqrt(jnp.mean(x * x, axis=-1, keepdims=True) + EPS)
    xh = x * rstd
    gdy = dy * g
    dx = (gdy - xh * jnp.mean(gdy * xh, axis=-1, keepdims=True)) * rstd
    return (dx if dres is None else dx + dres), jnp.sum(dy * xh, axis=0, keepdims=True)


def _mm(a, b, *, mode, out_dtype, name, act=None, residual=None, drelu_of=None, norm_bwd=None, b_cols=None,
        col_shards=None, rider=None, tm=None, tn=None, tk=None):
    b_off, b_width = b_cols or (0, b.shape[1])
    if mode == 'nn':
        (M, K), N = a.shape, b_width
    elif mode == 'nt':
        (M, K), N = a.shape, b.shape[0]
    else:
        (K, M), N = a.shape, b_width
    tm = tm or _pick(M, (1024, 512, 256, 128))
    tn = tn or _pick(N, (1024, 1920, 1152, 768, 640, 512, 384, 256, 128))
    tk = tk or _pick(K, (1024, 1920, 1152, 640, 512, 256, 128))
    nk = K // tk
    dims = {'nn': NN, 'nt': NT, 'tn': TN}[mode]
    a_spec = pl.BlockSpec((tk, tm), lambda i, j, k: (k, i)) if mode == 'tn' else pl.BlockSpec((tm, tk), lambda i, j, k: (i, k))
    if mode == 'nt':
        b_spec = pl.BlockSpec((tn, tk), lambda i, j, k, o=b_off // tk: (j, k + o))
    else:
        b_spec = pl.BlockSpec((tk, tn), lambda i, j, k, o=b_off // tn: (k, j + o))
    o_spec = pl.BlockSpec((tm, tn), lambda i, j, k: (i, j))
    extra = [e for e in (residual, drelu_of) if e is not None]
    extra_specs = [o_spec] * len(extra)
    out_shape, out_specs, n_out = jax.ShapeDtypeStruct((M, N), out_dtype), o_spec, 1
    if col_shards:
        n_sh = N // col_shards
        assert tn % n_sh == 0 and not extra and norm_bwd is None
        out_shape = jax.ShapeDtypeStruct((col_shards, M, n_sh), out_dtype)
        out_specs = pl.BlockSpec((tn // n_sh, tm, n_sh), lambda i, j, k: (j, i, 0))
    if norm_bwd is not None:
        x_in, g_in, dres_in = norm_bwd
        assert tn == N and residual is None and drelu_of is None
        vec = pl.BlockSpec((1, N), lambda i, j, k: (0, 0))
        extra, extra_specs = [x_in, g_in.reshape(1, N), dres_in], [o_spec, vec, o_spec]
        out_shape = (jax.ShapeDtypeStruct((M, N), F32), jax.ShapeDtypeStruct((M, N), BF16), jax.ShapeDtypeStruct((1, N), F32))
        out_specs, n_out = (o_spec, o_spec, vec), 3

    grid = (M // tm, N // tn, nk)
    r_ins, r_in_specs, r_outs, r_out_specs, r_scratch, split = _carry(
        rider, 2 + len(extra), n_out, lambda: functools.reduce(jnp.logical_and, [pl.program_id(d) == 0 for d in range(3)]),
        lambda: functools.reduce(jnp.logical_and, [pl.program_id(d) == grid[d] - 1 for d in range(3)]))
    assert rider is None or n_out == 1

    def body(*refs):
        a_ref, b_ref, *rest = split(refs)
        o_ref = rest[len(extra)]
        first_rows = pl.program_id(0) == 0
        at = a_ref[...]
        if act == 'relu2':
            at = jnp.square(jnp.maximum(at.astype(F32), 0.0))
        part = _dot(at, b_ref[...], dims)

        def finish(acc):
            if norm_bwd is not None:
                dx, dg = _rms_vjp(rest[0][...], rest[1][...], acc, rest[2][...])
                o_ref[...] = dx
                rest[len(extra) + 1][...] = dx.astype(BF16)
                dg_ref = rest[len(extra) + 2]

                @pl.when(first_rows)
                def _():
                    dg_ref[...] = dg

                @pl.when(jnp.logical_not(first_rows))
                def _():
                    dg_ref[...] += dg
                return
            idx = 0
            if residual is not None:
                acc = acc + rest[idx][...]
                idx += 1
            if drelu_of is not None:
                acc = acc * (2.0 * jnp.maximum(rest[idx][...].astype(F32), 0.0))
            if col_shards:
                for t in range(tn // n_sh):
                    o_ref[t] = acc[:, t * n_sh:(t + 1) * n_sh].astype(out_dtype)
            else:
                o_ref[...] = acc.astype(out_dtype)

        if nk == 1:
            finish(part)
        else:
            acc_ref = rest[len(extra) + n_out]
            k = pl.program_id(2)

            @pl.when(k == 0)
            def _():
                acc_ref[...] = part

            @pl.when(k > 0)
            def _():
                acc_ref[...] += part

            @pl.when(k == nk - 1)
            def _():
                finish(acc_ref[...])

    scratch = [] if nk == 1 else [pltpu.VMEM((tm, tn), F32)]
    if rider is not None:
        res = pl.pallas_call(
            body, name=name, out_shape=(out_shape, *r_outs), grid=grid, in_specs=[a_spec, b_spec] + extra_specs + r_in_specs,
            out_specs=(out_specs, *r_out_specs), scratch_shapes=scratch + r_scratch,
            compiler_params=_params(("arbitrary", "arbitrary", "arbitrary")),
        )(a, b, *extra, *r_ins)
        return res[0], rider.post(res[1:])
    return pl.pallas_call(
        body, name=name, out_shape=out_shape, grid=grid, in_specs=[a_spec, b_spec] + extra_specs, out_specs=out_specs,
        scratch_shapes=scratch,
        compiler_params=_params(("arbitrary" if norm_bwd is not None else "parallel", "parallel", "arbitrary")),
    )(a, b, *extra)


def _rms_fwd(x, g, *, name, out_dtype=BF16):
    S, D = x.shape
    tr = _pick(S, (512, 256, 128))

    def body(x_ref, g_ref, o_ref):
        o_ref[...] = _rms(x_ref[...], g_ref[...]).astype(out_dtype)

    return pl.pallas_call(
        body, name=name, out_shape=jax.ShapeDtypeStruct((S, D), out_dtype), grid=(S // tr,),
        in_specs=[pl.BlockSpec((tr, D), lambda i: (i, 0)), pl.BlockSpec((1, D), lambda i: (0, 0))],
        out_specs=pl.BlockSpec((tr, D), lambda i: (i, 0)),
        compiler_params=_params(("parallel",)),
    )(x, g.reshape(1, D))


def _rms_bwd(x, g, dy, dres, *, name):
    S, D = x.shape
    tr = _pick(S, (512, 256, 128))

    def body(x_ref, g_ref, dy_ref, *rest):
        dx_ref, dxb_ref, dg_ref = rest[-3], rest[-2], rest[-1]
        dx, part = _rms_vjp(x_ref[...], g_ref[...], dy_ref[...].astype(F32), None if dres is None else rest[0][...])
        dx_ref[...] = dx
        dxb_ref[...] = dx.astype(BF16)

        @pl.when(pl.program_id(0) == 0)
        def _():
            dg_ref[...] = part

        @pl.when(pl.program_id(0) > 0)
        def _():
            dg_ref[...] += part

    row = pl.BlockSpec((tr, D), lambda i: (i, 0))
    vec = pl.BlockSpec((1, D), lambda i: (0, 0))
    ins = [x, g.reshape(1, D), dy] + ([dres] if dres is not None else [])
    return pl.pallas_call(
        body, name=name,
        out_shape=(jax.ShapeDtypeStruct((S, D), F32), jax.ShapeDtypeStruct((S, D), BF16), jax.ShapeDtypeStruct((1, D), F32)),
        grid=(S // tr,),
        in_specs=[row, vec, row] + ([row] if dres is not None else []),
        out_specs=(row, row, vec),
        compiler_params=_params(("arbitrary",)),
    )(*ins)


def _loss_head(x, g, target, *, name):
    S, D = x.shape
    tr = _pick(S, (512, 256, 128))

    def body(x_ref, g_ref, t_ref, l_ref, dx_ref, dxb_ref, dg_ref):
        x_ = x_ref[...]
        g_ = g_ref[...]
        rstd = lax.rsqrt(jnp.mean(x_ * x_, axis=-1, keepdims=True) + EPS)
        xh = x_ * rstd
        err = xh * g_ - t_ref[...]
        lpart = (0.5 / D) * jnp.sum(jnp.sum(err * err, axis=-1, keepdims=True), axis=0, keepdims=True)
        dy = err * (1.0 / D)
        gdy = dy * g_
        dx = (gdy - xh * jnp.mean(gdy * xh, axis=-1, keepdims=True)) * rstd
        dx_ref[...] = dx
        dxb_ref[...] = dx.astype(BF16)
        gpart = jnp.sum(dy * xh, axis=0, keepdims=True)

        @pl.when(pl.program_id(0) == 0)
        def _():
            dg_ref[...] = gpart
            l_ref[...] = lpart

        @pl.when(pl.program_id(0) > 0)
        def _():
            dg_ref[...] += gpart
            l_ref[...] += lpart

    row = pl.BlockSpec((tr, D), lambda i: (i, 0))
    vec = pl.BlockSpec((1, D), lambda i: (0, 0))
    return pl.pallas_call(
        body, name=name,
        out_shape=(jax.ShapeDtypeStruct((1, 1), F32), jax.ShapeDtypeStruct((S, D), F32), jax.ShapeDtypeStruct((S, D), BF16),
                   jax.ShapeDtypeStruct((1, D), F32)),
        grid=(S // tr,),
        in_specs=[row, vec, row],
        out_specs=(pl.BlockSpec((1, 1), lambda i: (0, 0)), row, row, vec),
        compiler_params=_params(("arbitrary",)),
    )(x, g.reshape(1, D), target)


def _mask_of(mask, tq, tk, keys_first=False):
    shape, q_axis = ((tk, tq), 1) if keys_first else ((tq, tk), 0)
    qpos = lax.broadcasted_iota(jnp.int32, shape, q_axis)
    kpos = lax.broadcasted_iota(jnp.int32, shape, 1 - q_axis)
    if mask == 'causal':
        return kpos <= qpos
    return kpos <= (qpos | (CHUNK - 1))


LANES = 128
LOG2E = 1.4426950408889634


def _lane_group(j, w, width):
    lane = lax.broadcasted_iota(jnp.int32, (1, width), 1)
    return (lane >= j * w) & (lane < (j + 1) * w)


def _only(x, j, w):
    if w == x.shape[1]:
        return x
    return jnp.where(_lane_group(j, w, x.shape[1]), x, jnp.zeros_like(x))


def _side_by_side(xs):
    return xs[0] if len(xs) == 1 else jnp.concatenate(xs, axis=1)


def _on_top(xs):
    return xs[0] if len(xs) == 1 else jnp.concatenate(xs, axis=0)


def _stacked(x, hp, w):
    return _on_top([_only(x, j, w) for j in range(hp)])


def _col_block(entry, rows, idx):
    arr, off, width = entry
    return pl.BlockSpec((rows, width), lambda i, j, o=off // width: (idx(i, j), o))


def _attn_fwd(qk, v, H, cq, ck, *, scale, mask, name, rider=None):
    Sq, Sk = qk[0][0][0].shape[0], v[0].shape[0]
    dv = v[2] // H
    w0 = qk[0][2]
    hp = LANES // w0
    G = H // hp
    assert dv == w0 and not qk[0][3] and all(sh and H * w == LANES for _, _, w, sh in qk[1:])
    tq = _pick(Sq, (512, 256, 128))
    tk = tq if mask else _pick(Sk, (512, 256, 128))
    nq, nk = Sq // tq, Sk // tk
    bias = cq is not None
    npart = len(qk)

    def body(*refs):
        refs = split(refs)
        q_refs, k_refs = refs[0:2 * npart:2], refs[1:2 * npart:2]
        v_ref = refs[2 * npart]
        cq_ref, ck_ref = (refs[2 * npart + 1], refs[2 * npart + 2]) if bias else (None, None)
        o_ref, lse_ref, m_s, l_s, acc_s = refs[-5:]
        qi, ki = pl.program_id(0), pl.program_id(1)

        @pl.when(ki == 0)
        def _():
            m_s[...] = jnp.full(m_s.shape, NEG, F32)
            l_s[...] = jnp.zeros(l_s.shape, F32)
            acc_s[...] = jnp.zeros(acc_s.shape, F32)

        def rows_of(vals):
            return _on_top([jnp.broadcast_to(r, (w0, tq)) for r in vals])

        def compute(masked):
            keep = _mask_of(mask, tq, tk, keys_first=True) if masked else None
            for g in range(G):
                lanes = slice(g * LANES, (g + 1) * LANES)
                q128, k128, v128 = q_refs[0][:, lanes], k_refs[0][:, lanes], v_ref[:, lanes]
                ps, alphas = [], []
                extras = list(zip(qk, q_refs, k_refs))[1:]
                k_all = _side_by_side([k128] + [k_ref[...] for _, _, k_ref in extras])
                for j in range(hp):
                    h = g * hp + j
                    q_all = _side_by_side([_only(q128, j, w0)] + [_only(q_ref[...], h, w) for (_, _, w, _), q_ref, _ in extras])
                    s = _dot(k_all, q_all, NT) * scale
                    if bias:
                        s = s + (cq_ref[h:h + 1, :] - ck_ref[:, h:h + 1])
                    if masked:
                        s = jnp.where(keep, s, NEG)
                    m_prev = m_s[h:h + 1, :]
                    m_new = jnp.maximum(m_prev, jnp.max(s, axis=0, keepdims=True))
                    alpha = jnp.exp(m_prev - m_new)
                    p = jnp.exp(s - m_new)
                    l_s[h:h + 1, :] = alpha * l_s[h:h + 1, :] + jnp.sum(p, axis=0, keepdims=True)
                    m_s[h:h + 1, :] = m_new
                    ps.append(p.astype(BF16))
                    alphas.append(alpha)
                acc_s[g] = rows_of(alphas) * acc_s[g] + _dot(_stacked(v128, hp, w0), _on_top(ps), TN)

        if mask is None:
            compute(False)
        else:
            pl.when(ki < qi)(lambda: compute(False))
            pl.when(ki == qi)(lambda: compute(True))

        @pl.when(ki == ((nk - 1) if mask is None else qi))
        def _():
            for g in range(G):
                norm = acc_s[g] / rows_of([l_s[g * hp + j:g * hp + j + 1, :] for j in range(hp)])
                o_ref[:, g * LANES:(g + 1) * LANES] = norm.T.astype(BF16)
            lse_ref[...] = jnp.zeros(lse_ref.shape, F32)
            lse_ref[0:H, :] = m_s[0:H, :] + jnp.log(l_s[0:H, :])

    q_idx = lambda i, j: i
    k_idx = (lambda i, j: jnp.minimum(i, j)) if mask else (lambda i, j: j)
    ins, in_specs = [], []
    for q_e, k_e, _, _ in qk:
        ins += [q_e[0], k_e[0]]
        in_specs += [_col_block(q_e, tq, q_idx), _col_block(k_e, tk, k_idx)]
    ins.append(v[0])
    in_specs.append(_col_block(v, tk, k_idx))
    if bias:
        in_specs += [pl.BlockSpec((8, tq), lambda i, j: (0, i)), pl.BlockSpec((tk, 8), lambda i, j: (k_idx(i, j), 0))]
        ins += [cq, ck]
    r_ins, r_in_specs, r_outs, r_out_specs, r_scratch, split = _carry(
        rider, len(ins), 2, lambda: (pl.program_id(0) == 0) & (pl.program_id(1) == 0),
        lambda: (pl.program_id(0) == nq - 1) & (pl.program_id(1) == nk - 1))
    res = pl.pallas_call(
        body, name=name,
        out_shape=(jax.ShapeDtypeStruct((Sq, H * dv), BF16), jax.ShapeDtypeStruct((8, Sq), F32), *r_outs),
        grid=(nq, nk), in_specs=in_specs + r_in_specs,
        out_specs=(pl.BlockSpec((tq, H * dv), lambda i, j: (i, 0)), pl.BlockSpec((8, tq), lambda i, j: (0, i)), *r_out_specs),
        scratch_shapes=[pltpu.VMEM((8, tq), F32), pltpu.VMEM((8, tq), F32), pltpu.VMEM((G, LANES, tq), F32)] + r_scratch,
        compiler_params=_params(("arbitrary", "arbitrary")) if rider else _params(("parallel", "arbitrary")),
    )(*ins, *r_ins)
    return (res[0], res[1], rider.post(res[2:])) if rider else res


def _attn_bwd(qk, v, H, o, do, lse, cq, ck, *, scale, mask, name, rider=None):
    Sq, Sk = qk[0][0][0].shape[0], v[0].shape[0]
    dv = v[2] // H
    w0 = qk[0][2]
    hp = LANES // w0
    G = H // hp
    tq = _pick(Sq, (512, 256, 128))
    tk = tq if mask else _pick(Sk, (512, 256, 128))
    nq, nk = Sq // tq, Sk // tk
    bias = cq is not None
    npart = len(qk)
    n_in = 2 * npart + 4 + (2 if bias else 0)

    def body(*refs):
        refs = split(refs)
        q_refs, k_refs = refs[0:2 * npart:2], refs[1:2 * npart:2]
        v_ref, o_ref, do_ref, lse_ref = refs[2 * npart:2 * npart + 4]
        cq_ref, ck_ref = (refs[2 * npart + 4], refs[2 * npart + 5]) if bias else (None, None)
        outs = refs[n_in:]
        dq_refs, dk_refs, dv_ref = outs[:npart], outs[npart:2 * npart], outs[2 * npart]
        dck_ref, dcq_ref = (outs[2 * npart + 1], outs[2 * npart + 2]) if bias else (None, None)
        dk_accs, dv_acc = refs[-(npart + 1):-1], refs[-1]
        ki, qi = pl.program_id(0), pl.program_id(1)
        first_q = ki if mask else 0

        @pl.when((ki == 0) & (qi == 0))
        def _():
            for r in dq_refs:
                r[...] = jnp.zeros(r.shape, F32)
            if bias:
                dcq_ref[...] = jnp.zeros(dcq_ref.shape, F32)

        @pl.when(qi == first_q)
        def _():
            for r in dk_accs:
                r[...] = jnp.zeros(r.shape, F32)
            dv_acc[...] = jnp.zeros(dv_acc.shape, F32)
            if bias:
                dck_ref[...] = jnp.zeros(dck_ref.shape, F32)

        def compute(masked):
            keep = _mask_of(mask, tq, tk, keys_first=True) if masked else None
            rows = pl.ds(pl.multiple_of(qi * tq, tq), tq)
            extras = list(zip(qk, q_refs, k_refs, dq_refs, dk_accs))[1:]
            for g in range(G):
                lanes = slice(g * LANES, (g + 1) * LANES)
                q128, k128, v128 = q_refs[0][:, lanes], k_refs[0][:, lanes], v_ref[:, lanes]
                do128, o128 = do_ref[:, lanes], o_ref[:, lanes]
                prod = do128.astype(F32) * o128.astype(F32)
                ps, dss = [], []
                k_all = _side_by_side([k128] + [e[2][...] for e in extras])
                for j in range(hp):
                    h = g * hp + j
                    q_all = _side_by_side([_only(q128, j, w0)] + [_only(e[1][...], h, e[0][2]) for e in extras])
                    s = _dot(k_all, q_all, NT) * (scale * LOG2E)
                    if bias:
                        s = s - ck_ref[:, h:h + 1] * LOG2E
                    if masked:
                        s = jnp.where(keep, s, NEG)
                    row = lse_ref[h:h + 1, :] - cq_ref[h:h + 1, :] if bias else lse_ref[h:h + 1, :]
                    p = jnp.exp2(s - row * LOG2E)
                    dp = _dot(v128, _only(do128, j, w0), NT)
                    delta = jnp.sum(_only(prod, j, w0), axis=1, keepdims=True).T
                    ds = p * (dp - delta)
                    if bias:
                        dck_ref[:, h:h + 1] -= jnp.sum(ds, axis=1, keepdims=True)
                        dcq_ref[h:h + 1, rows] += jnp.sum(ds, axis=0, keepdims=True)
                    ps.append(p.astype(BF16))
                    dss.append((ds * scale).astype(BF16))
                for (_, _, w, _), q_ref, k_ref, dq_ref, dk_acc in extras:
                    heads = range(g * hp, (g + 1) * hp)
                    dk_acc[...] += _dot(_side_by_side(dss), _on_top([_only(q_ref[...], h, w) for h in heads]), NN)
                    dq_ref[rows, :] += _dot(_on_top(dss), _on_top([_only(k_ref[...], h, w) for h in heads]), TN)
                dv_acc[:, lanes] += _dot(_side_by_side(ps), _stacked(do128, hp, w0), NN)
                dk_accs[0][:, lanes] += _dot(_side_by_side(dss), _stacked(q128, hp, w0), NN)
                dq_refs[0][rows, lanes] += _dot(_on_top(dss), _stacked(k128, hp, w0), TN)

        if mask is None:
            compute(False)
        else:
            pl.when(qi > ki)(lambda: compute(False))
            pl.when(qi == ki)(lambda: compute(True))

        @pl.when(qi == nq - 1)
        def _():
            for r, acc in zip(dk_refs, dk_accs):
                r[...] = acc[...]
            dv_ref[...] = dv_acc[...]

    q_idx = (lambda j, i: jnp.maximum(i, j)) if mask else (lambda j, i: i)
    k_idx = lambda j, i: j
    ins, in_specs, dq_shapes, dq_specs, dk_shapes, dk_specs, scratch = [], [], [], [], [], [], []
    for q_e, k_e, w, shared in qk:
        ins += [q_e[0], k_e[0]]
        in_specs += [_col_block(q_e, tq, q_idx), _col_block(k_e, tk, k_idx)]
        dq_shapes.append(jax.ShapeDtypeStruct((Sq, H * w), F32))
        dq_specs.append(pl.BlockSpec((Sq, H * w), lambda j, i: (0, 0)))
        kw = k_e[2]
        dk_shapes.append(jax.ShapeDtypeStruct((Sk, kw), F32))
        dk_specs.append(pl.BlockSpec((tk, kw), lambda j, i: (j, 0)))
        scratch.append(pltpu.VMEM((tk, kw), F32))
    row_q = lambda width: pl.BlockSpec((tq, width), lambda j, i: (q_idx(j, i), 0))
    per_q = pl.BlockSpec((8, tq), lambda j, i: (0, q_idx(j, i)))
    ins += [v[0], o, do, lse]
    in_specs += [_col_block(v, tk, k_idx), row_q(H * dv), row_q(H * dv), per_q]
    out_shape = dq_shapes + dk_shapes + [jax.ShapeDtypeStruct((Sk, H * dv), F32)]
    out_specs = dq_specs + dk_specs + [pl.BlockSpec((tk, H * dv), lambda j, i: (j, 0))]
    if bias:
        in_specs += [per_q, pl.BlockSpec((tk, 8), lambda j, i: (j, 0))]
        ins += [cq, ck]
        out_shape += [jax.ShapeDtypeStruct((Sk, 8), F32), jax.ShapeDtypeStruct((8, Sq), F32)]
        out_specs += [pl.BlockSpec((tk, 8), lambda j, i: (j, 0)), pl.BlockSpec((8, Sq), lambda j, i: (0, 0))]
    scratch.append(pltpu.VMEM((tk, H * dv), F32))
    n_out = len(out_shape)
    r_ins, r_in_specs, r_outs, r_out_specs, r_scratch, split = _carry(
        rider, len(ins), n_out, lambda: (pl.program_id(0) == 0) & (pl.program_id(1) == 0),
        lambda: (pl.program_id(0) == nk - 1) & (pl.program_id(1) == nq - 1))
    res = pl.pallas_call(
        body, name=name, out_shape=tuple(out_shape + r_outs), grid=(nk, nq), in_specs=in_specs + r_in_specs,
        out_specs=tuple(out_specs + r_out_specs), scratch_shapes=scratch + r_scratch,
        compiler_params=_params(("arbitrary", "arbitrary")),
    )(*ins, *r_ins)
    own = (list(res[:npart]), list(res[npart:2 * npart]), res[2 * npart]) + tuple(res[2 * npart + 1:n_out])
    return own + (rider.post(res[n_out:]),) if rider else own


def _split3_dot(x, t):
    hi = x.astype(BF16)
    r1 = x - hi.astype(F32)
    mid = r1.astype(BF16)
    lo = (r1 - mid.astype(F32)).astype(BF16)
    return _dot(hi, t, NN) + _dot(mid, t, NN) + _dot(lo, t, NN)


def _fox_cum_fwd(ff_t, b, *, name):
    _, S = ff_t.shape
    tb = _pick(S, (512, 256, 128))

    def body(f_ref, b_ref, o_ref, carry):
        @pl.when(pl.program_id(0) == 0)
        def _():
            carry[...] = jnp.zeros(carry.shape, F32)

        lf = _log_sigmoid(f_ref[...] + b_ref[...])
        o_ref[...] = _split3_dot(lf, _tri(tb, False)) + carry[...]
        carry[...] += jnp.sum(lf, axis=1, keepdims=True)

    return pl.pallas_call(
        body, name=name, out_shape=jax.ShapeDtypeStruct((8, S), F32), grid=(S // tb,),
        in_specs=[pl.BlockSpec((8, tb), lambda i: (0, i)), pl.BlockSpec((8, 1), lambda i: (0, 0))],
        out_specs=pl.BlockSpec((8, tb), lambda i: (0, i)),
        scratch_shapes=[pltpu.VMEM((8, 1), F32)],
        compiler_params=_params(("arbitrary",)),
    )(ff_t, b)


def _fox_cum_bwd(ff_t, b, dcum_t, *, name):
    _, S = ff_t.shape
    tb = _pick(S, (512, 256, 128))
    nb = S // tb

    def body(f_ref, b_ref, dc_ref, df_ref, db_ref, carry):
        @pl.when(pl.program_id(0) == 0)
        def _():
            carry[...] = jnp.zeros(carry.shape, F32)
            db_ref[...] = jnp.zeros(db_ref.shape, F32)

        dc = dc_ref[...]
        dlf = _split3_dot(dc, _tri(tb, True)) + carry[...]
        carry[...] += jnp.sum(dc, axis=1, keepdims=True)
        df = dlf * _sigmoid(-(f_ref[...] + b_ref[...]))
        df_ref[...] = df
        db_ref[...] += jnp.sum(df, axis=1, keepdims=True)

    rev = lambda i: (0, nb - 1 - i)
    return pl.pallas_call(
        body, name=name,
        out_shape=(jax.ShapeDtypeStruct((8, S), F32), jax.ShapeDtypeStruct((8, 1), F32)), grid=(nb,),
        in_specs=[pl.BlockSpec((8, tb), rev), pl.BlockSpec((8, 1), lambda i: (0, 0)), pl.BlockSpec((8, tb), rev)],
        out_specs=(pl.BlockSpec((8, tb), rev), pl.BlockSpec((8, 1), lambda i: (0, 0))),
        scratch_shapes=[pltpu.VMEM((8, 1), F32)],
        compiler_params=_params(("arbitrary",)),
    )(ff_t, b, dcum_t)


GLA_W = GLA_HEADS * GLA_DK
GLA_BLOCK_CHUNKS = 4


def _same_chunk(n, lower):
    r = lax.broadcasted_iota(jnp.int32, (n, n), 0)
    c = lax.broadcasted_iota(jnp.int32, (n, n), 1)
    same = (r | (CHUNK - 1)) == (c | (CHUNK - 1))
    return jnp.where(same & (r >= c) if lower else same, 1.0, 0.0).astype(BF16)


def _chunk_mix(x, t, transpose):
    hi, lo = _split2(x)
    dims = TN if transpose else NN
    return _dot(t, hi, dims) + _dot(t, lo, dims)


@jax.custom_vjp
def chunk_cumsum(x):
    return _chunk_mix(x, _same_chunk(x.shape[0], True), False)


chunk_cumsum.defvjp(lambda x: (chunk_cumsum(x), None), lambda _, g: (_chunk_mix(g, _same_chunk(g.shape[0], True), True),))


@jax.custom_vjp
def chunk_total(x):
    return _chunk_mix(x, _same_chunk(x.shape[0], False), False)


chunk_total.defvjp(lambda x: (chunk_total(x), None), lambda _, g: (_chunk_mix(g, _same_chunk(g.shape[0], False), False),))


def _gla_block(q, k, zsm, wg, bg, go, vs, rs, states):
    n_chunks = q.shape[0] // CHUNK
    la = _log_sigmoid(bdot(zsm, wg) + bg) * (1.0 / GLA_TAU)
    end = chunk_total(la)
    kd = k * jnp.exp(end - chunk_cumsum(la))
    a = jnp.exp(end)
    qs = q * (GLA_DK ** -0.5)
    lane = lax.broadcasted_iota(jnp.int32, (1, GLA_W), 1)
    outs, new_states = [], []
    for h in range(GLA_HEADS):
        kdh = kd * jnp.where((lane >= h * GLA_DK) & (lane < (h + 1) * GLA_DK), 1.0, 0.0)
        st, o = states[h], []
        for c in range(n_chunks):
            rows = slice(c * CHUNK, (c + 1) * CHUNK)
            st = st * a[c * CHUNK:c * CHUNK + 1] + bdot_tn(vs[h][rows], kdh[rows])
            o.append(bdot_nt(qs[rows], st))
        o = _rms(jnp.concatenate(o, axis=0), go)
        outs.append(o * (rs[h] * _sigmoid(rs[h])))
        new_states.append(st)
    return outs, new_states


def _gla_fwd(z, zsm, wg, bg, go, cols, *, name):
    S = z.shape[0]
    rb = GLA_BLOCK_CHUNKS * CHUNK
    nb = S // rb
    cq, ckk, cv, cr = cols
    H = GLA_HEADS

    def body(q_ref, k_ref, zsm_ref, wg_ref, bg_ref, go_ref, *rest):
        v_refs, r_refs = rest[:H], rest[H:2 * H]
        o_ref, st_ref, state = rest[2 * H], rest[2 * H + 1], rest[2 * H + 2]

        @pl.when(pl.program_id(0) == 0)
        def _():
            state[...] = jnp.zeros(state.shape, F32)

        states = [state[h] for h in range(H)]
        for h in range(H):
            st_ref[0, h] = states[h]
        outs, new_states = _gla_block(
            q_ref[...].astype(F32), k_ref[...].astype(F32), zsm_ref[...], wg_ref[...], bg_ref[...], go_ref[...],
            [v_refs[h][...].astype(F32) for h in range(H)], [r_refs[h][...].astype(F32) for h in range(H)], states)
        for h in range(H):
            o_ref[:, h * GLA_DV:(h + 1) * GLA_DV] = outs[h].astype(BF16)
            state[h] = new_states[h]

    def col(width, off):
        return pl.BlockSpec((rb, width), lambda i, o=off // width: (i, o))

    full = lambda shp: pl.BlockSpec(shp, lambda i: (0,) * len(shp))
    in_specs = [col(GLA_W, cq), col(GLA_W, ckk), pl.BlockSpec((rb, 128), lambda i: (i, 0)),
                full((128, GLA_W)), full((1, GLA_W)), full((1, GLA_DV))]
    in_specs += [col(GLA_DV, cv + h * GLA_DV) for h in range(H)] + [col(GLA_DV, cr + h * GLA_DV) for h in range(H)]
    return pl.pallas_call(
        body, name=name,
        out_shape=(jax.ShapeDtypeStruct((S, H * GLA_DV), BF16), jax.ShapeDtypeStruct((nb, H, GLA_DV, GLA_W), F32)),
        grid=(nb,), in_specs=in_specs,
        out_specs=(pl.BlockSpec((rb, H * GLA_DV), lambda i: (i, 0)),
                   pl.BlockSpec((1, H, GLA_DV, GLA_W), lambda i: (i, 0, 0, 0))),
        scratch_shapes=[pltpu.VMEM((H, GLA_DV, GLA_W), F32)],
        compiler_params=_params(("arbitrary",)),
    )(z, z, zsm, wg, bg, go, *([z] * (2 * H)))


def _gla_bwd(z, zsm, wg, bg, go, states, do, cols, *, name):
    S = z.shape[0]
    rb = GLA_BLOCK_CHUNKS * CHUNK
    nb = S // rb
    cq, ckk, cv, cr = cols
    H = GLA_HEADS

    def body(q_ref, k_ref, zsm_ref, wg_ref, bg_ref, go_ref, st_ref, do_ref, *rest):
        v_refs, r_refs = rest[:H], rest[H:2 * H]
        dq_ref, dk_ref, dv_ref, dr_ref, dzsm_ref, dwg_ref, dbg_ref, dgo_ref, dstate = rest[2 * H:]

        @pl.when(pl.program_id(0) == 0)
        def _():
            dstate[...] = jnp.zeros(dstate.shape, F32)
            dwg_ref[...] = jnp.zeros(dwg_ref.shape, F32)
            dbg_ref[...] = jnp.zeros(dbg_ref.shape, F32)
            dgo_ref[...] = jnp.zeros(dgo_ref.shape, F32)

        prim = (q_ref[...].astype(F32), k_ref[...].astype(F32), zsm_ref[...], wg_ref[...], bg_ref[...], go_ref[...],
                [v_refs[h][...].astype(F32) for h in range(H)], [r_refs[h][...].astype(F32) for h in range(H)],
                [st_ref[0, h] for h in range(H)])
        _, vjp = jax.vjp(_gla_block, *prim)
        douts = [do_ref[:, h * GLA_DV:(h + 1) * GLA_DV].astype(F32) for h in range(H)]
        dq, dk, dzs, dwg, dbg, dgo, dvs, drs, dsts = vjp((douts, [dstate[h] for h in range(H)]))
        dq_ref[...] = dq.astype(BF16)
        dk_ref[...] = dk.astype(BF16)
        dzsm_ref[...] = dzs
        dwg_ref[...] += dwg
        dbg_ref[...] += dbg
        dgo_ref[...] += dgo
        for h in range(H):
            dv_ref[:, h * GLA_DV:(h + 1) * GLA_DV] = dvs[h].astype(BF16)
            dr_ref[:, h * GLA_DV:(h + 1) * GLA_DV] = drs[h].astype(BF16)
            dstate[h] = dsts[h]

    rev = lambda i: nb - 1 - i

    def col(width, off):
        return pl.BlockSpec((rb, width), lambda i, o=off // width: (rev(i), o))

    full = lambda shp: pl.BlockSpec(shp, lambda i: (0,) * len(shp))
    rowb = lambda w: pl.BlockSpec((rb, w), lambda i: (rev(i), 0))
    in_specs = [col(GLA_W, cq), col(GLA_W, ckk), rowb(128), full((128, GLA_W)), full((1, GLA_W)), full((1, GLA_DV)),
                pl.BlockSpec((1, H, GLA_DV, GLA_W), lambda i: (rev(i), 0, 0, 0)), rowb(H * GLA_DV)]
    in_specs += [col(GLA_DV, cv + h * GLA_DV) for h in range(H)] + [col(GLA_DV, cr + h * GLA_DV) for h in range(H)]
    return pl.pallas_call(
        body, name=name,
        out_shape=(jax.ShapeDtypeStruct((S, GLA_W), BF16), jax.ShapeDtypeStruct((S, GLA_W), BF16),
                   jax.ShapeDtypeStruct((S, H * GLA_DV), BF16), jax.ShapeDtypeStruct((S, H * GLA_DV), BF16),
                   jax.ShapeDtypeStruct((S, 128), F32), jax.ShapeDtypeStruct((128, GLA_W), F32),
                   jax.ShapeDtypeStruct((1, GLA_W), F32), jax.ShapeDtypeStruct((1, GLA_DV), F32)),
        grid=(nb,), in_specs=in_specs,
        out_specs=(rowb(GLA_W), rowb(GLA_W), rowb(H * GLA_DV), rowb(H * GLA_DV), rowb(128),
                   full((128, GLA_W)), full((1, GLA_W)), full((1, GLA_DV))),
        scratch_shapes=[pltpu.VMEM((H, GLA_DV, GLA_W), F32)],
        compiler_params=_params(("arbitrary",)),
    )(z, z, zsm, wg, bg, go, states, do, *([z] * (2 * H)))


def _row_spec(entry, tr):
    if isinstance(entry, tuple):
        arr, width, off = entry
        return arr, pl.BlockSpec((tr, width), lambda i, o=off // width: (i, o))
    return entry, pl.BlockSpec((tr, entry.shape[1]), lambda i: (i, 0))


def _stage_fwd(fn, rows, consts, outs, *, name, tr=None):
    first = rows[0][0] if isinstance(rows[0], tuple) else rows[0]
    S = first.shape[0]
    tr = tr or _pick(S, (512, 256, 128))
    arrs, specs = zip(*[_row_spec(e, tr) for e in rows])
    nr, nc = len(rows), len(consts)

    def body(*refs):
        vals = [r[...].astype(F32) for r in refs[:nr + nc]]
        res = fn(*vals)
        for o_ref, val in zip(refs[nr + nc:], res):
            o_ref[...] = val.astype(o_ref.dtype)

    cspecs = [pl.BlockSpec(c.shape, lambda i, n=c.ndim: (0,) * n) for c in consts]
    return pl.pallas_call(
        body, name=name,
        out_shape=tuple(jax.ShapeDtypeStruct((S, w), dt) for w, dt in outs), grid=(S // tr,),
        in_specs=list(specs) + cspecs,
        out_specs=tuple(pl.BlockSpec((tr, w), lambda i: (i, 0)) for w, _ in outs),
        compiler_params=_params(("parallel",)),
    )(*arrs, *consts)


def _stage_bwd(fn, rows, consts, cts, n_diff, drow_dtypes, *, name, tr=None, lead=None):
    first = rows[0][0] if isinstance(rows[0], tuple) else rows[0]
    S = first.shape[0]
    tr = tr or _pick(S, (512, 256, 128))
    arrs, specs = zip(*[_row_spec(e, tr) for e in rows])
    widths = [e[1] if isinstance(e, tuple) else e.shape[1] for e in rows]
    nr, nc, nt = len(rows), len(consts), len(cts)
    n_lead, lead_width = lead or (1, widths[0])
    n_rows_out = n_diff - n_lead + 1

    def body(*refs):
        vals = [r[...].astype(F32) for r in refs[:nr + nc]]
        ct = [r[...].astype(F32) for r in refs[nr + nc:nr + nc + nt]]
        drow_refs = refs[nr + nc + nt:nr + nc + nt + n_rows_out]
        dconst_refs = refs[nr + nc + nt + n_rows_out:]
        rest_rows = vals[n_diff:nr]

        def f(diff_rows, cs):
            return tuple(fn(*diff_rows, *rest_rows, *cs))

        _, vjp = jax.vjp(f, vals[:n_diff], vals[nr:])
        drows, dcs = vjp(tuple(ct))
        off = 0
        for val, w in zip(drows[:n_lead], widths):
            drow_refs[0][:, off:off + w] = val.astype(drow_refs[0].dtype)
            off += w
        for r, val in zip(drow_refs[1:], drows[n_lead:]):
            r[...] = val.astype(r.dtype)
        first_step = pl.program_id(0) == 0
        for r, val in zip(dconst_refs, dcs):
            @pl.when(first_step)
            def _(r=r, val=val):
                r[...] = val

            @pl.when(jnp.logical_not(first_step))
            def _(r=r, val=val):
                r[...] += val

    cspecs = [pl.BlockSpec(c.shape, lambda i, n=c.ndim: (0,) * n) for c in consts]
    ctspecs = [pl.BlockSpec((tr, c.shape[1]), lambda i: (i, 0)) for c in cts]
    out_shape = [jax.ShapeDtypeStruct((S, lead_width), drow_dtypes[0])]
    out_shape += [jax.ShapeDtypeStruct((S, widths[j]), drow_dtypes[j]) for j in range(n_lead, n_diff)]
    out_shape += [jax.ShapeDtypeStruct(c.shape, F32) for c in consts]
    out_specs = [pl.BlockSpec((tr, sum(widths[:n_lead])), lambda i: (i, 0))]
    out_specs += [pl.BlockSpec((tr, widths[j]), lambda i: (i, 0)) for j in range(n_lead, n_diff)] + cspecs
    res = pl.pallas_call(
        body, name=name, out_shape=tuple(out_shape), grid=(S // tr,),
        in_specs=list(specs) + cspecs + ctspecs, out_specs=tuple(out_specs),
        compiler_params=_params(("arbitrary",)),
    )(*arrs, *consts, *cts)
    return list(res[:n_rows_out]), list(res[n_rows_out:])


def _mla_prep_fn(cq, ckv, kr, kr_sw, cos, sin, gq, gkv, wq_n, wq_r, wq_sw, wk, wv):
    hq = _rms(cq, gq)
    hkv = _rms(ckv, gkv)
    return (bdot(hq, wq_n), bdot(hq, wq_r) * cos + bdot(hq, wq_sw) * sin,
            bdot(hkv, wk), bdot(hkv, wv), kr * cos + kr_sw * sin)


def _merge_fn(g0, g1, g2, of, og, om, b0, b1, b2, wf, wg, wm):
    return (_sigmoid(g0 + b0) * bdot(of, wf) + _sigmoid(g1 + b1) * bdot(og, wg) + _sigmoid(g2 + b2) * bdot(om, wm),)


_IN_SIZES = (256, 256, 256, 4, 256, 256, 512, 16, 512, 256, 128, 32, 3072)
_IN_OFF = np.concatenate([[0], np.cumsum(_IN_SIZES)])
(_O_FQ, _O_FK, _O_FV, _O_FF, _O_GQ, _O_GK, _O_GV, _O_GLOW, _O_GR, _O_MQ, _O_MKV, _O_MKR, _O_ZG) = [int(o) for o in _IN_OFF[:-1]]
N_IN = int(_IN_OFF[-1])
_BIG_GROUPS = ((_O_ZG, 3072), (_O_GV, 512), (_O_GR, 512), (_O_FQ, 256), (_O_FK, 256), (_O_FV, 256),
               (_O_GQ, 256), (_O_GK, 256), (_O_MQ, 256), (_O_MKV, 128))
Z_GATE, Z_GV, Z_GR, Z_FQ, Z_FK, Z_FV, Z_GQ, Z_GK, Z_MQ, Z_MKV = [int(o) for o in
                                                                    np.concatenate([[0], np.cumsum([w for _, w in _BIG_GROUPS])])[:-1]]
N_BIG = sum(w for _, w in _BIG_GROUPS)
_HALF = MLA_ROPE // 2
_QK_HD = MLA_NOPE + MLA_ROPE
SM_FF, SM_GLOW, SM_KR, SM_KR_SW, N_SM = 0, 8, 128, 256, 384
N_PAD = N_BIG + N_SM
_IN_SEGS = ([(o, w, 1.0) for o, w in _BIG_GROUPS]
            + [(_O_FF, 4, 1.0), (None, SM_GLOW - 4, 0.0), (_O_GLOW, GLA_RANK, 1.0), (None, 128 - SM_GLOW - GLA_RANK, 0.0)]
            + [(_O_MKR, MLA_ROPE, 1.0)] * MLA_HEADS
            + [(_O_MKR + _HALF, _HALF, -1.0), (_O_MKR, _HALF, 1.0)] * MLA_HEADS)


def _cols(x, start, width):
    return lax.slice_in_dim(x, start, start + width, axis=x.ndim - 1)


def _pad_w_in(w):
    return jnp.concatenate([jnp.zeros(w.shape[:-1] + (n,), w.dtype) if src is None else
                            (_cols(w, src, n) if sign > 0 else -_cols(w, src, n)) for src, n, sign in _IN_SEGS], axis=-1)


def _unpad_w_in(g):
    groups = []
    for o, n in zip(_IN_OFF[:-1], _IN_SIZES):
        total, pos = None, 0
        for src, m, sign in _IN_SEGS:
            if src is not None and o <= src and src + m <= o + n:
                term = _cols(g, pos, m) if sign > 0 else -_cols(g, pos, m)
                if m != n:
                    term = jnp.pad(term, [(0, 0)] * (g.ndim - 1) + [(int(src - o), int(o + n - src - m))])
                total = term if total is None else total + term
            pos += m
        groups.append(total)
    return jnp.concatenate(groups, axis=-1)


def _take(x, idx):
    idx = np.asarray(idx)
    cuts = [0] + [i for i in range(1, len(idx)) if idx[i] != idx[i - 1] + 1] + [len(idx)]
    return jnp.concatenate([_cols(x, int(idx[a]), b - a) for a, b in zip(cuts[:-1], cuts[1:])], axis=1)


_UQ_NOPE = np.concatenate([np.arange(h * _QK_HD, h * _QK_HD + MLA_NOPE) for h in range(MLA_HEADS)])
_UQ_ROT = np.concatenate([np.arange(h * _QK_HD + MLA_NOPE, (h + 1) * _QK_HD) for h in range(MLA_HEADS)])
_UKV_PERM = np.concatenate(
    [np.concatenate([np.arange(h * 128, h * 128 + MLA_NOPE) for h in range(MLA_HEADS)]),
     np.concatenate([np.arange(h * 128 + MLA_NOPE, (h + 1) * 128) for h in range(MLA_HEADS)])])
_UKV_INV = np.argsort(_UKV_PERM)


def _rotary_partner(r):
    return jnp.concatenate([piece for h in range(MLA_HEADS) for piece in
                            (-_cols(r, h * MLA_ROPE + _HALF, _HALF), _cols(r, h * MLA_ROPE, _HALF))], axis=1)


def _uq_grad(dn, dr, dsw):
    dr = dr + jnp.concatenate([piece for h in range(MLA_HEADS) for piece in
                               (_cols(dsw, h * MLA_ROPE + _HALF, _HALF), -_cols(dsw, h * MLA_ROPE, _HALF))], axis=1)
    return jnp.concatenate([piece for h in range(MLA_HEADS) for piece in
                            (_cols(dn, h * MLA_NOPE, MLA_NOPE), _cols(dr, h * MLA_ROPE, MLA_ROPE))], axis=1)


def _rope_tables(S):
    inv = ROPE_BASE ** (-jnp.arange(_HALF, dtype=F32) / _HALF)
    ang = jnp.arange(S, dtype=F32)[:, None] * inv[None, :]
    return jnp.tile(jnp.cos(ang), (1, 2 * MLA_HEADS)), jnp.tile(jnp.sin(ang), (1, 2 * MLA_HEADS))


class _LayerParams:
    def __init__(self, rep, l):
        self.w, self.rep, self.l, self.made = {}, rep, l, {}

    def __getitem__(self, k):
        if k not in self.made:
            self.made[k] = self._make(k)
        return self.made[k]

    def _make(self, k):
        w, rep, l = self.w, self.rep, self.l
        if k == 'wg':
            return jnp.pad(w['w_gla_gate'], [(SM_GLOW, LANES - SM_GLOW - GLA_RANK), (0, 0)])
        if k in ('wq_n', 'wq_r'):
            return _take(w['w_mla_uq'], _UQ_NOPE if k == 'wq_n' else _UQ_ROT)
        if k == 'wq_sw':
            return _rotary_partner(self['wq_r'])
        if k in ('wk', 'wv'):
            return _take(w['w_mla_ukv'], _UKV_PERM[:256] if k == 'wk' else _UKV_PERM[256:])
        if k == 'b_f':
            return jnp.zeros((8, 1), F32).at[:FOX_HEADS, 0].set(rep['b_fox_forget'][l])
        if k == 'b_gate':
            return [rep['b_branch_gate'][l][i * 1024:(i + 1) * 1024].reshape(1, 1024) for i in range(3)]
        vec = {'bg': 'b_gla_gate', 'go': 'g_gla_out', 'gq': 'g_mla_q', 'gkv': 'g_mla_kv'}
        if k in vec:
            return rep[vec[k]][l].reshape(1, -1)
        return rep[k][l] if k in rep else w[k]


_GLA_COLS = (Z_GQ, Z_GK, Z_GV, Z_GR)
_MLA_OUTS = [(256, BF16), (128, BF16), (256, BF16), (256, BF16), (128, BF16)]


def _mla_rows(z, zsm, rope):
    return [(z, 256, Z_MQ), (z, 128, Z_MKV), (zsm, 128, SM_KR), (zsm, 128, SM_KR_SW), *rope]


def _mla_consts(p):
    return [p['gq'], p['gkv'], p['wq_n'], p['wq_r'], p['wq_sw'], p['wk'], p['wv']]


def _fox_qkv(z):
    return [((z, Z_FQ, 256), (z, Z_FK, 256), FOX_HD, False)], (z, Z_FV, 256)


def _mla_qkv(qn, qr, kn, vv, kr):
    return [((qn, 0, 256), (kn, 0, 256), MLA_NOPE, False), ((qr, 0, 128), (kr, 0, 128), MLA_ROPE, True)], (vv, 0, 256)


def _xa_qkv(qx, kvx):
    return [((qx, 0, 512), (kvx, 0, 512), XA_HD, False)], (kvx, 512, 512)


def _merge_rows(z, o_fox, o_gla, o_mla):
    return [(z, 1024, Z_GATE), (z, 1024, Z_GATE + 1024), (z, 1024, Z_GATE + 2048), o_fox, o_gla, o_mla]


def _merge_consts(p):
    return p['b_gate'] + [p['w_up_fox'], p['w_up_gla'], p['w_up_mla']]


def _carried(hooks, key, call, single=False):
    entries = hooks.pop(key, [])
    if not entries:
        return call(rider=None)
    res = call(rider=_join_riders([rider for rider, _ in entries]))
    for (_, sink), got in zip(entries, res[-1]):
        sink(got)
    return res[0] if single else res[:-1]


def _layer_fwd(x0, mem, p, rope, l, hooks):
    S = x0.shape[0]
    sv = {'x0': x0}

    def mm(key, a, b, **kw):
        return _carried(hooks, (l, key), lambda rider: _mm(a, b, mode='nn', rider=rider, name=f"{key}_{l}", **kw), single=True)

    h1 = _rms_fwd(x0, p['g_mix'], name=f"rms_mix_{l}")
    z = mm('in_big', h1, p['w_in'], out_dtype=BF16, b_cols=(0, N_BIG))
    zsm = _mm(h1, p['w_in'], mode='nn', out_dtype=F32, b_cols=(N_BIG, N_SM), name=f"in_small_{l}")
    sv.update(h1=h1, z=z, zsm=zsm)
    ff_t = jnp.zeros((8, S), F32).at[:FOX_HEADS].set(zsm[:, SM_FF:SM_FF + FOX_HEADS].T)
    cum_t = _fox_cum_fwd(ff_t, p['b_f'], name=f"fox_cum_{l}")
    cum = cum_t.T
    o_fox, lse_f = _carried(hooks, (l, 'fox_fwd'), lambda rider: _attn_fwd(
        *_fox_qkv(z), FOX_HEADS, cum_t, cum, scale=FOX_HD ** -0.5, mask='causal', name=f"fox_fwd_{l}", rider=rider))
    sv.update(ff_t=ff_t, cum=cum, cum_t=cum_t, lse_f=lse_f, o_fox=o_fox)
    o_gla, states = _gla_fwd(z, zsm, p['wg'], p['bg'], p['go'], _GLA_COLS, name=f"gla_fwd_{l}")
    sv.update(o_gla=o_gla, states=states)
    mla = _stage_fwd(_mla_prep_fn, _mla_rows(z, zsm, rope), _mla_consts(p), _MLA_OUTS, name=f"mla_prep_{l}")
    o_mla, lse_m = _carried(hooks, (l, 'mla_fwd'), lambda rider: _attn_fwd(
        *_mla_qkv(*mla), MLA_HEADS, None, None, scale=_QK_HD ** -0.5, mask='chunk', name=f"mla_fwd_{l}", rider=rider))
    sv.update(mla=mla, lse_m=lse_m, o_mla=o_mla)
    (y,) = _stage_fwd(_merge_fn, _merge_rows(z, o_fox, o_gla, o_mla), _merge_consts(p), [(1024, BF16)], name=f"merge_{l}")
    x1 = mm('out_proj', y, p['w_out'], out_dtype=F32, residual=x0)
    sv.update(y=y, x1=x1)
    h2 = _rms_fwd(x1, p['g_xa'], name=f"rms_xa_{l}")
    hm = _rms_fwd(mem, p['g_mem'], name=f"rms_mem_{l}")
    qx = _mm(h2, p['w_xq'], mode='nn', out_dtype=BF16, name=f"xq_{l}")
    kvx = _mm(hm, p['w_xkv'], mode='nn', out_dtype=BF16, name=f"xkv_{l}")
    ox, lse_x = _carried(hooks, (l, 'xa_fwd'), lambda rider: _attn_fwd(
        *_xa_qkv(qx, kvx), XA_HEADS, None, None, scale=XA_HD ** -0.5, mask=None, name=f"xa_fwd_{l}", rider=rider))
    x2 = mm('xo', ox, p['w_xo'], out_dtype=F32, residual=x1)
    sv.update(h2=h2, hm=hm, qx=qx, kvx=kvx, lse_x=lse_x, ox=ox, x2=x2)
    h3 = _rms_fwd(x2, p['g_mlp'], name=f"rms_mlp_{l}")
    a = mm('mlp1', h3, p['w_mlp1'], out_dtype=BF16)
    x3 = mm('mlp2', a, p['w_mlp2'], out_dtype=F32, act='relu2', residual=x2)
    sv.update(h3=h3, a=a)
    return x3, sv


def _layer_bwd(dx3, dx3b, mem, p, rope, sv, l, hooks, half_done):
    S = dx3.shape[0]
    g = {}

    def dw(key, a, b, **kw):
        return _mm(a, b, mode='tn', out_dtype=BF16, col_shards=b.shape[1] // LANES, name=f"d_{key}_{l}", **kw)

    da = _mm(dx3b, p['w_mlp2'], mode='nt', out_dtype=BF16, drelu_of=sv['a'], name=f"d_mlp2_in_{l}")
    g['w_mlp2'] = dw('w_mlp2', sv['a'], dx3b, act='relu2')
    dx2, dx2b, g['g_mlp'] = _mm(da, p['w_mlp1'], mode='nt', out_dtype=F32, norm_bwd=(sv['x2'], p['g_mlp'], dx3), tm=512,
                                name=f"d_mlp1_in_{l}")
    g['w_mlp1'] = dw('w_mlp1', sv['h3'], da)
    dox = _mm(dx2b, p['w_xo'], mode='nt', out_dtype=BF16, name=f"d_xo_in_{l}")
    g['w_xo'] = dw('w_xo', sv['ox'], dx2b)
    (dqx,), (dkx,), dvx = _carried(hooks, (l, 'xa_bwd'), lambda rider: _attn_bwd(
        *_xa_qkv(sv['qx'], sv['kvx']), XA_HEADS, sv['ox'], dox, sv['lse_x'], None, None,
        scale=XA_HD ** -0.5, mask=None, name=f"xa_bwd_{l}", rider=rider))
    dqx = dqx.astype(BF16)
    dkvx = jnp.concatenate([dkx, dvx], axis=1).astype(BF16)
    dx1, dx1b, g['g_xa'] = _mm(dqx, p['w_xq'], mode='nt', out_dtype=F32, norm_bwd=(sv['x1'], p['g_xa'], dx2), tm=512,
                               name=f"d_xq_in_{l}")
    g['w_xq'] = dw('w_xq', sv['h2'], dqx)
    dhm = _mm(dkvx, p['w_xkv'], mode='nt', out_dtype=F32, name=f"d_xkv_in_{l}")
    g['w_xkv'] = dw('w_xkv', sv['hm'], dkvx)
    _, _, g['g_mem'] = _rms_bwd(mem, p['g_mem'], dhm, None, name=f"d_rms_mem_{l}")
    dy = _mm(dx1b, p['w_out'], mode='nt', out_dtype=F32, name=f"d_out_in_{l}")
    g['w_out'] = dw('w_out', sv['y'], dx1b)
    z, zsm = sv['z'], sv['zsm']
    (dz, do_fox, do_gla, do_mla), (db0, db1, db2, g['w_up_fox'], g['w_up_gla'], g['w_up_mla']) = _stage_bwd(
        _merge_fn, _merge_rows(z, sv['o_fox'], sv['o_gla'], sv['o_mla']), _merge_consts(p), [dy], 6, [BF16] * 6,
        lead=(3, N_PAD), name=f"merge_bwd_{l}")
    g['b_branch_gate'] = jnp.concatenate([db0, db1, db2], axis=1).reshape(-1)
    half_done(l, g)
    (dfq,), (dfk,), dfv, dck, dcq = _carried(hooks, (l, 'fox_bwd'), lambda rider: _attn_bwd(
        *_fox_qkv(z), FOX_HEADS, sv['o_fox'], do_fox, sv['lse_f'], sv['cum_t'], sv['cum'],
        scale=FOX_HD ** -0.5, mask='causal', name=f"fox_bwd_{l}", rider=rider))
    dff_t, db_f = _fox_cum_bwd(sv['ff_t'], p['b_f'], dcq + dck.T, name=f"fox_cum_bwd_{l}")
    g['b_fox_forget'] = db_f[:FOX_HEADS, 0]
    dgq, dgk, dgv, dgr, dzsm, dwg, dbg, dgo = _gla_bwd(z, zsm, p['wg'], p['bg'], p['go'], sv['states'], do_gla, _GLA_COLS,
                                                       name=f"gla_bwd_{l}")
    g['w_gla_gate'] = dwg[SM_GLOW:SM_GLOW + GLA_RANK]
    g['b_gla_gate'] = dbg.reshape(-1)
    g['g_gla_out'] = dgo.reshape(-1)
    (dmqn, dmqr), (dmkn, dmkr), dmv = _carried(hooks, (l, 'mla_bwd'), lambda rider: _attn_bwd(
        *_mla_qkv(*sv['mla']), MLA_HEADS, sv['o_mla'], do_mla, sv['lse_m'], None, None,
        scale=_QK_HD ** -0.5, mask='chunk', name=f"mla_bwd_{l}", rider=rider))
    (dcq, dckv, dkr, dkr_sw), (dgq_n, dgkv_n, dwq_n, dwq_r, dwq_sw, dwk, dwv) = _stage_bwd(
        _mla_prep_fn, _mla_rows(z, zsm, rope), _mla_consts(p), [dmqn, dmqr, dmkn, dmv, dmkr], 4, [BF16] * 4,
        name=f"mla_prep_bwd_{l}")
    g['g_mla_q'] = dgq_n.reshape(-1)
    g['g_mla_kv'] = dgkv_n.reshape(-1)
    g['w_mla_uq'] = _uq_grad(dwq_n, dwq_r, dwq_sw)
    g['w_mla_ukv'] = _take(jnp.concatenate([dwk, dwv], axis=1), _UKV_INV)
    dsm = dzsm + jnp.pad(dff_t[:FOX_HEADS].T, [(0, 0), (0, 128 - FOX_HEADS)])
    dz = lax.dynamic_update_slice(dz, jnp.concatenate(
        [dgv, dgr, dfq.astype(BF16), dfk.astype(BF16), dfv.astype(BF16), dgq, dgk, dcq, dckv, dsm.astype(BF16), dkr, dkr_sw],
        axis=1), (0, Z_GV))
    dx0, dx0b, g['g_mix'] = _mm(dz, p['w_in'], mode='nt', out_dtype=F32, norm_bwd=(sv['x0'], p['g_mix'], dx1), tm=512,
                                tk=N_PAD // 2, name=f"d_in_{l}")
    g['w_in'] = dw('w_in', sv['h1'], dz, tn=N_PAD // 3)
    for n in ('g_mlp', 'g_mem', 'g_xa', 'g_mix'):
        g[n] = g[n].reshape(-1)
    return dx0, dx0b, g


def _local_step(x, mem, target, ps, g_final, hooks, half_done, layer_done):
    rope = _rope_tables(x.shape[0])
    saved = []
    for l, p in enumerate(ps):
        x, sv = _layer_fwd(x, mem, p, rope, l, hooks)
        saved.append(sv)
    loss, dx, dxb, dgf = _loss_head(x, g_final, target, name="loss_head")
    for l in reversed(range(len(ps))):
        dx, dxb, grads = _layer_bwd(dx, dxb, mem, ps[l], rope, saved[l], l, hooks, half_done)
        layer_done(l, grads)
    assert not hooks, f"exchanges without a carrier: {list(hooks)}"
    return loss, dx, dgf.reshape(-1)


_MESH_AXES = ("x", "y", "c")
_HBM = pl.BlockSpec(memory_space=pl.ANY)


N_CHIP = 4
_SLOT_ROWS = (2048, 1024, 512, 256, 128, 64, 32, 16, 8)


def _place():
    x, y, c = (lax.axis_index(n) for n in _MESH_AXES)
    return (x, y, c), (x, y, 1 - c), [(1 - x, y), (x, 1 - y), (1 - x, 1 - y)]


def _remote(src, dst, sems, k, to):
    return pltpu.make_async_remote_copy(src_ref=src, dst_ref=dst, send_sem=sems[0].at[k], recv_sem=sems[1].at[k],
                                        device_id=to, device_id_type=pl.DeviceIdType.MESH)


def _all_gather(x, *, name):
    def body(x_ref, o_ref, send_sems, recv_sems, local_sem):
        me, sib, chips = _place()
        c = me[2]
        sems = (send_sems, recv_sems)
        slot = lambda px, py, pc: o_ref.at[4 * px + 2 * py + pc]
        mine = pltpu.make_async_copy(x_ref, slot(*me), local_sem)
        mine.start()
        first = [_remote(x_ref, slot(*me), sems, 0, sib)]
        first += [_remote(x_ref, slot(*me), sems, 1 + j, (*chip, c)) for j, chip in enumerate(chips)]
        for cp in first:
            cp.start()
        passed = [_remote(slot(*chip, c), slot(*chip, c), sems, 4 + j, sib) for j, chip in enumerate(chips)]
        for j, chip in enumerate(chips):
            _remote(x_ref, slot(*chip, c), sems, 1 + j, me).wait_recv()
            passed[j].start()
        _remote(x_ref, slot(*sib), sems, 0, me).wait_recv()
        for j, chip in enumerate(chips):
            _remote(x_ref, slot(*chip, 1 - c), sems, 4 + j, me).wait_recv()
        for cp in first + passed:
            cp.wait_send()
        mine.wait()

    return pl.pallas_call(
        body, name=name, out_shape=jax.ShapeDtypeStruct((N_DEV,) + x.shape, x.dtype),
        in_specs=[_HBM], out_specs=_HBM,
        scratch_shapes=[pltpu.SemaphoreType.DMA((N_DEV - 1,)), pltpu.SemaphoreType.DMA((N_DEV - 1,)), pltpu.SemaphoreType.DMA],
        compiler_params=pltpu.CompilerParams(has_side_effects=True),
    )(x)


class _Rider:
    def __init__(self, inputs, out_shapes, scratch, start, finish, post):
        self.inputs, self.out_shapes, self.scratch = list(inputs), list(out_shapes), list(scratch)
        self.start, self.finish, self.post = start, finish, post


def _run_rider(rider, *, name):
    def body(*refs):
        rider.start(refs)
        rider.finish(refs)

    outs = pl.pallas_call(
        body, name=name, out_shape=tuple(rider.out_shapes), in_specs=[_HBM] * len(rider.inputs),
        out_specs=(_HBM,) * len(rider.out_shapes), scratch_shapes=rider.scratch,
        compiler_params=pltpu.CompilerParams(has_side_effects=True),
    )(*rider.inputs)
    return rider.post(outs)


def _carry(rider, n_in, n_out, first, last):
    if rider is None:
        return [], [], [], [], [], lambda refs: refs
    ni, no = len(rider.inputs), len(rider.out_shapes)

    def split(refs):
        own_in, r_in = refs[:n_in], refs[n_in:n_in + ni]
        own_out, r_out = refs[n_in + ni:n_in + ni + n_out], refs[n_in + ni + n_out:n_in + ni + n_out + no]
        rest = refs[n_in + ni + n_out + no:]
        own_scr, r_scr = rest[:len(rest) - len(rider.scratch)], rest[len(rest) - len(rider.scratch):]
        rrefs = tuple(r_in) + tuple(r_out) + tuple(r_scr)
        pl.when(first())(lambda: rider.start(rrefs))
        pl.when(last())(lambda: rider.finish(rrefs))
        return tuple(own_in) + tuple(own_out) + tuple(own_scr)

    return list(rider.inputs), [_HBM] * ni, list(rider.out_shapes), [_HBM] * no, list(rider.scratch), split


def _gather_rider(shards, axes):
    n = len(shards)
    srcs, out_shapes, kinds = [], [], []
    for s, ax in zip(shards, axes):
        L, a, b = s.shape
        if ax == 1:
            srcs.append(s.reshape(L, 1, a, b)), out_shapes.append((L, N_DEV, a, b)), kinds.append('row')
        elif b % 128 == 0:
            srcs.append(s), out_shapes.append((L, a, N_DEV * b)), kinds.append('col')
        else:
            srcs.append(s.reshape(1, L, a, b)), out_shapes.append((N_DEV, L, a, b)), kinds.append('slot')

    def parts(refs):
        x_refs, o_refs = refs[:n], refs[n:2 * n]
        send_sems, recv_sems, local_sem = refs[2 * n:]
        me, sib, chips = _place()
        sems = (send_sems, recv_sems)

        def win(t, px, py, pc):
            idx = 4 * px + 2 * py + pc
            if kinds[t] == 'row':
                return o_refs[t].at[:, pl.ds(idx, 1)]
            if kinds[t] == 'col':
                width = shards[t].shape[2]
                return o_refs[t].at[:, :, pl.ds(pl.multiple_of(idx * width, 128), width)]
            return o_refs[t].at[pl.ds(idx, 1)]

        def group(k, block, to, own):
            return [_remote(x_refs[t] if own else win(t, *block), win(t, *block), sems, k * n + t, to) for t in range(n)]

        mine = [pltpu.make_async_copy(x_refs[t], win(t, *me), local_sem.at[t]) for t in range(n)]
        first = group(0, me, sib, True)
        for j, chip in enumerate(chips):
            first += group(1 + j, me, (*chip, me[2]), True)
        return me, sib, chips, group, mine, first

    def start(refs):
        *_, mine, first = parts(refs)
        for cp in mine + first:
            cp.start()

    def finish(refs):
        me, sib, chips, group, mine, first = parts(refs)
        c = me[2]
        passed = []
        for j, chip in enumerate(chips):
            for cp in group(1 + j, (*chip, c), me, False):
                cp.wait_recv()
            fwd = group(4 + j, (*chip, c), sib, False)
            for cp in fwd:
                cp.start()
            passed += fwd
        for cp in group(0, sib, me, False):
            cp.wait_recv()
        for j, chip in enumerate(chips):
            for cp in group(4 + j, (*chip, 1 - c), me, False):
                cp.wait_recv()
        for cp in first + passed:
            cp.wait_send()
        for cp in mine:
            cp.wait()

    def post(outs):
        whole = []
        for o, s, kind in zip(outs, shards, kinds):
            L, a, b = s.shape
            whole.append(o.reshape(L, N_DEV * a, b) if kind == 'row' else o if kind == 'col' else _to_whole(o, 2))
        return whole

    return _Rider(srcs, [jax.ShapeDtypeStruct(shp, s.dtype) for shp, s in zip(out_shapes, shards)],
                  [pltpu.SemaphoreType.DMA(((N_DEV - 1) * n,)), pltpu.SemaphoreType.DMA(((N_DEV - 1) * n,)),
                   pltpu.SemaphoreType.DMA((n,))], start, finish, post)


def _sibling_swap_rider(x):
    def sends(refs):
        x_ref, o_ref, send_sems, recv_sems = refs
        me, sib, _ = _place()
        return [_remote(x_ref.at[j, 1 - me[2]], o_ref.at[j], (send_sems, recv_sems), j, sib) for j in range(N_CHIP)]

    def start(refs):
        for cp in sends(refs):
            cp.start()

    def finish(refs):
        for cp in sends(refs):
            cp.wait_send()
            cp.wait_recv()

    return _Rider([x], [jax.ShapeDtypeStruct((N_CHIP,) + x.shape[2:], x.dtype)],
                  [pltpu.SemaphoreType.DMA((N_CHIP,)), pltpu.SemaphoreType.DMA((N_CHIP,))], start, finish, lambda outs: outs[0])


def _join_riders(riders):
    counts = [(len(r.inputs), len(r.out_shapes), len(r.scratch)) for r in riders]
    n_in, n_out = sum(c[0] for c in counts), sum(c[1] for c in counts)

    def refs_of(refs, k):
        a = sum(c[0] for c in counts[:k])
        b = n_in + sum(c[1] for c in counts[:k])
        s = n_in + n_out + sum(c[2] for c in counts[:k])
        return tuple(refs[a:a + counts[k][0]]) + tuple(refs[b:b + counts[k][1]]) + tuple(refs[s:s + counts[k][2]])

    def each(method):
        def run(refs):
            for k, r in enumerate(riders):
                getattr(r, method)(refs_of(refs, k))
        return run

    def post(outs):
        got, at = [], 0
        for r, c in zip(riders, counts):
            got.append(r.post(outs[at:at + c[1]]))
            at += c[1]
        return got

    return _Rider([x for r in riders for x in r.inputs], [o for r in riders for o in r.out_shapes],
                  [s for r in riders for s in r.scratch], each('start'), each('finish'), post)


def _pair_sum(x, got, c, *, name):
    _, _, R, _ = x.shape
    tr = _pick(R, _SLOT_ROWS)

    def body(c_ref, x_ref, g_ref, o_ref):
        o_ref[...] = (x_ref[...].astype(F32) + g_ref[...].astype(F32)).astype(o_ref.dtype)

    return pl.pallas_call(
        body, name=name, out_shape=jax.ShapeDtypeStruct((N_CHIP, R, 128), x.dtype),
        grid_spec=pltpu.PrefetchScalarGridSpec(
            num_scalar_prefetch=1, grid=(R // tr,),
            in_specs=[pl.BlockSpec((N_CHIP, None, tr, 128), lambda i, c_ref: (0, c_ref[0], i, 0)),
                      pl.BlockSpec((N_CHIP, tr, 128), lambda i, c_ref: (0, i, 0))],
            out_specs=pl.BlockSpec((N_CHIP, tr, 128), lambda i, c_ref: (0, i, 0))),
        compiler_params=_params(("parallel",)),
    )(c, x, got)


def _chip_all_to_all_rider(x):
    def parts(refs):
        x_ref, o_ref, send_sems, recv_sems, local_sem = refs
        me, _, chips = _place()
        sems = (send_sems, recv_sems)
        mine = 2 * me[0] + me[1]
        local = pltpu.make_async_copy(x_ref.at[mine], o_ref.at[mine], local_sem)
        sends = [_remote(x_ref.at[2 * px + py], o_ref.at[mine], sems, j, (px, py, me[2])) for j, (px, py) in enumerate(chips)]
        arrival = lambda j: _remote(x_ref.at[mine], o_ref.at[2 * chips[j][0] + chips[j][1]], sems, j, me)
        return local, sends, arrival

    def start(refs):
        local, sends, _ = parts(refs)
        for cp in [local] + sends:
            cp.start()

    def finish(refs):
        local, sends, arrival = parts(refs)
        for j, cp in enumerate(sends):
            cp.wait_send()
            arrival(j).wait_recv()
        local.wait()

    return _Rider([x], [jax.ShapeDtypeStruct(x.shape, x.dtype)],
                  [pltpu.SemaphoreType.DMA((N_CHIP - 1,)), pltpu.SemaphoreType.DMA((N_CHIP - 1,)), pltpu.SemaphoreType.DMA],
                  start, finish, lambda outs: outs[0])


def _sum_slots(x, *, name):
    n, R, _ = x.shape
    tr = _pick(R, _SLOT_ROWS)

    def body(x_ref, o_ref):
        acc = x_ref[0].astype(F32)
        for j in range(1, n):
            acc = acc + x_ref[j].astype(F32)
        o_ref[...] = acc

    return pl.pallas_call(
        body, name=name, out_shape=jax.ShapeDtypeStruct((R, 128), F32), grid=(R // tr,),
        in_specs=[pl.BlockSpec((n, tr, 128), lambda i: (0, i, 0))], out_specs=pl.BlockSpec((tr, 128), lambda i: (i, 0)),
        compiler_params=_params(("parallel",)),
    )(x)


def _adamw(w, g, m, v, *, name):
    shape = w.shape
    cols = shape[-1]
    rows = int(np.prod(shape[:-1]))
    tr = next((t for t in (1024, 512, 256, 128, 64, 32, 16, 8) if rows % t == 0 and t * cols * 4 <= (1 << 20)), rows)

    def body(w_ref, g_ref, m_ref, v_ref, d_ref, mo_ref, vo_ref):
        g_ = g_ref[...]
        m_ = ADAM_B1 * m_ref[...] + (1.0 - ADAM_B1) * g_
        v_ = ADAM_B2 * v_ref[...] + (1.0 - ADAM_B2) * jnp.square(g_)
        m_hat = m_ / (1.0 - ADAM_B1 ** ADAM_STEP)
        v_hat = v_ / (1.0 - ADAM_B2 ** ADAM_STEP)
        d_ref[...] = -ADAM_LR * (m_hat / (jnp.sqrt(v_hat) + ADAM_EPS) + ADAM_WD * w_ref[...])
        mo_ref[...] = m_
        vo_ref[...] = v_

    blk = pl.BlockSpec((tr, cols), lambda i: (i, 0))
    outs = pl.pallas_call(
        body, name=name, out_shape=tuple(jax.ShapeDtypeStruct((rows, cols), F32) for _ in range(3)), grid=(rows // tr,),
        in_specs=[blk] * 4, out_specs=(blk,) * 3, compiler_params=_params(("parallel",)),
    )(*(a.reshape(rows, cols) for a in (w, g, m, v)))
    return tuple(o.reshape(shape) for o in outs)


_WEIGHTS = ('g_mix', 'w_in', 'b_fox_forget', 'w_gla_gate', 'b_gla_gate', 'g_gla_out', 'g_mla_q', 'w_mla_uq', 'g_mla_kv',
            'w_mla_ukv', 'b_branch_gate', 'w_up_fox', 'w_up_gla', 'w_up_mla', 'w_out', 'g_xa', 'g_mem', 'w_xq', 'w_xkv',
            'w_xo', 'g_mlp', 'w_mlp1', 'w_mlp2', 'g_final')
_SHARDED = (('w_in', 1), ('w_gla_gate', 2), ('w_mla_uq', 2), ('w_mla_ukv', 2), ('w_up_fox', 2), ('w_up_gla', 2),
            ('w_up_mla', 2), ('w_out', 1), ('w_xq', 1), ('w_xkv', 1), ('w_xo', 2), ('w_mlp1', 2), ('w_mlp2', 1))
_REPLICATED = tuple(n for n in _WEIGHTS if n not in dict(_SHARDED))
_ROW_PAD = 1024
_SMALL_ROW_PAD = 8
_PIECE_ROWS = 16


def _pack(flats, lead, row_pad=_ROW_PAD):
    if all(int(np.prod(a.shape[lead:])) % 128 == 0 for a in flats):
        def block(a):
            a = a.reshape(a.shape[:lead] + (-1, 128))
            return jnp.pad(a, [(0, 0)] * lead + [(0, -a.shape[lead] % _PIECE_ROWS), (0, 0)])
        cat = jnp.concatenate([block(a) for a in flats], axis=lead)
        rows = cat.shape[lead]
        return jnp.pad(cat, [(0, 0)] * lead + [(0, -(-rows // row_pad) * row_pad - rows), (0, 0)])
    cat = jnp.concatenate([a.reshape(a.shape[:lead] + (-1,)) for a in flats], axis=-1)
    n = cat.shape[-1]
    total = -(-n // (128 * row_pad)) * (128 * row_pad)
    cat = jnp.pad(cat, [(0, 0)] * lead + [(0, total - n)])
    return cat.reshape(cat.shape[:lead] + (total // 128, 128))


def _unpack(buf, shapes, lead):
    sizes = [int(np.prod(shp)) for shp in shapes]
    out, off = [], 0
    if all(n % 128 == 0 for n in sizes):
        for shp, n in zip(shapes, sizes):
            rows = buf[(slice(None),) * lead + (slice(off, off + n // 128),)]
            out.append(rows.reshape(buf.shape[:lead] + tuple(shp)))
            off += -(-(n // 128) // _PIECE_ROWS) * _PIECE_ROWS
        return out
    flat = buf.reshape(buf.shape[:lead] + (-1,))
    for shp, n in zip(shapes, sizes):
        out.append(flat[..., off:off + n].reshape(buf.shape[:lead] + tuple(shp)))
        off += n
    return out


def _to_whole(g, axis):
    if axis == 1:
        return g.transpose(1, 0, 2, 3).reshape(g.shape[1], N_DEV * g.shape[2], g.shape[3])
    return g.transpose(1, 2, 0, 3).reshape(g.shape[1], g.shape[2], N_DEV * g.shape[3])


def _to_shards(w, axis):
    L, R, C = w.shape
    if axis == 1:
        return w.reshape(L, N_DEV, R // N_DEV, C).transpose(1, 0, 2, 3)
    return w.reshape(L, R, N_DEV, C // N_DEV).transpose(2, 0, 1, 3)


def kernel(x, mem, g_mix, w_in, b_fox_forget, w_gla_gate, b_gla_gate, g_gla_out, g_mla_q, w_mla_uq, g_mla_kv, w_mla_ukv, b_branch_gate, w_up_fox, w_up_gla, w_up_mla, w_out, g_xa, g_mem, w_xq, w_xkv, w_xo, g_mlp, w_mlp1, w_mlp2, g_final, loss_target, m_g_mix, m_w_in, m_b_fox_forget, m_w_gla_gate, m_b_gla_gate, m_g_gla_out, m_g_mla_q, m_w_mla_uq, m_g_mla_kv, m_w_mla_ukv, m_b_branch_gate, m_w_up_fox, m_w_up_gla, m_w_up_mla, m_w_out, m_g_xa, m_g_mem, m_w_xq, m_w_xkv, m_w_xo, m_g_mlp, m_w_mlp1, m_w_mlp2, m_g_final, v_g_mix, v_w_in, v_b_fox_forget, v_w_gla_gate, v_b_gla_gate, v_g_gla_out, v_g_mla_q, v_w_mla_uq, v_g_mla_kv, v_w_mla_ukv, v_b_branch_gate, v_w_up_fox, v_w_up_gla, v_w_up_mla, v_w_out, v_g_xa, v_g_mem, v_w_xq, v_w_xkv, v_w_xo, v_g_mlp, v_w_mlp1, v_w_mlp2, v_g_final):
    wts = dict(zip(_WEIGHTS, (g_mix, w_in, b_fox_forget, w_gla_gate, b_gla_gate, g_gla_out, g_mla_q, w_mla_uq, g_mla_kv,
                              w_mla_ukv, b_branch_gate, w_up_fox, w_up_gla, w_up_mla, w_out, g_xa, g_mem, w_xq, w_xkv, w_xo,
                              g_mlp, w_mlp1, w_mlp2, g_final)))
    mom1 = dict(zip(_WEIGHTS, (m_g_mix, m_w_in, m_b_fox_forget, m_w_gla_gate, m_b_gla_gate, m_g_gla_out, m_g_mla_q,
                               m_w_mla_uq, m_g_mla_kv, m_w_mla_ukv, m_b_branch_gate, m_w_up_fox, m_w_up_gla, m_w_up_mla,
                               m_w_out, m_g_xa, m_g_mem, m_w_xq, m_w_xkv, m_w_xo, m_g_mlp, m_w_mlp1, m_w_mlp2, m_g_final)))
    mom2 = dict(zip(_WEIGHTS, (v_g_mix, v_w_in, v_b_fox_forget, v_w_gla_gate, v_b_gla_gate, v_g_gla_out, v_g_mla_q,
                               v_w_mla_uq, v_g_mla_kv, v_w_mla_ukv, v_b_branch_gate, v_w_up_fox, v_w_up_gla, v_w_up_mla,
                               v_w_out, v_g_xa, v_g_mem, v_w_xq, v_w_xkv, v_w_xo, v_g_mlp, v_w_mlp1, v_w_mlp2, v_g_final)))
    depth = g_mix.shape[0]

    names = [n for n, _ in _SHARDED]
    axes = dict(_SHARDED)
    shard = {n: wts[n] for n in names}
    shard['w_in'] = _pad_w_in(w_in)
    rep = {n: wts[n] for n in _REPLICATED}
    ps = [_LayerParams(rep, l) for l in range(depth)]

    def gather(group, l):
        rider = _gather_rider([shard[n][l:l + 1].astype(BF16) for n in group], [axes[n] for n in group])
        return rider, lambda whole: ps[l].w.update({n: w[0] for n, w in zip(group, whole)})

    first, sink = gather(['w_in'], 0)
    sink(_run_rider(first, name="gather_w_in_0"))
    narrow = ['w_gla_gate', 'w_mla_uq', 'w_mla_ukv', 'w_up_fox', 'w_up_gla', 'w_up_mla']
    hooks = {(0, 'in_big'): [gather(narrow + ['w_out', 'w_xq', 'w_xkv', 'w_xo'], 0)],
             (0, 'fox_fwd'): [gather(['w_mlp1', 'w_mlp2'], 0)]}
    ahead = (('mla_fwd', ['w_in'] + narrow), ('out_proj', ['w_out']), ('xa_fwd', ['w_xq', 'w_xo']), ('xo', ['w_xkv']),
             ('mlp1', ['w_mlp1']), ('mlp2', ['w_mlp2']))
    assert sorted(n for _, group in ahead for n in group) == sorted(names)
    for l in range(1, depth):
        for key, group in ahead:
            hooks.setdefault((l - 1, key), []).append(gather(group, l))

    core = lax.axis_index("c").astype(jnp.int32).reshape(1)
    late = ['w_in', 'w_gla_gate', 'w_mla_uq', 'w_mla_ukv']
    groups = {'early': [n for n in names if n not in late], 'late': late}
    small_grads, landed = {}, {}

    def to_slots(gl, axis):
        if gl.ndim != 3:
            return _to_shards(gl[None], axis)
        blocks, rows, _ = gl.shape
        if axis == 1:
            return gl.reshape(blocks, N_DEV, rows // N_DEV, LANES).transpose(1, 0, 2, 3)
        return gl.reshape(N_DEV, blocks // N_DEV, rows, LANES)

    def slot_shape(n):
        _, a, b = shard[n].shape
        return (b // LANES, a, LANES) if n in blocked else (1, a, b)

    def from_slot(n, x):
        return x.transpose(1, 0, 2).reshape((1,) + shard[n].shape[1:]) if n in blocked else x

    blocked = {'w_in', 'w_out', 'w_xq', 'w_xkv', 'w_xo', 'w_mlp1', 'w_mlp2'}

    def ride(key, rider, sink, name):
        if key is None:
            sink(_run_rider(rider, name=name))
        else:
            hooks.setdefault(key, []).append((rider, sink))

    def exchange(l, g, which, swap_in, scatter_in):
        assert all((g[n].ndim == 3) == (n in blocked) for n in groups[which])
        slots = _pack([to_slots(g[n], axes[n]).astype(BF16) for n in groups[which]], 1)
        slots = slots.reshape((N_CHIP, 2) + slots.shape[1:])

        def swapped(got):
            paired = _pair_sum(slots, got, core, name=f"pair_grads_{which}_{l}")
            ride(scatter_in, _chip_all_to_all_rider(paired), lambda landing: landed.update({(l, which): landing}),
                 f"scatter_grads_{which}_{l}")

        ride(swap_in, _sibling_swap_rider(slots), swapped, f"swap_grads_{which}_{l}")

    def half_done(l, g):
        exchange(l, g, 'early', (l, 'fox_bwd'), (l, 'mla_bwd'))

    def layer_done(l, g):
        small_grads[l] = g
        if l > 0:
            exchange(l, g, 'late', (l - 1, 'xa_bwd'), (l - 1, 'fox_bwd'))
        else:
            exchange(l, g, 'late', None, None)

    loss, dx, dg_final = _local_step(x[0], mem[0], loss_target[0], ps, g_final, hooks, half_done, layer_done)
    loss = lax.psum(loss[0, 0], _MESH_AXES)

    grad = {}
    for which, group in groups.items():
        shapes = [slot_shape(n) for n in group]
        per_layer = [_unpack(_sum_slots(landed[(l, which)], name=f"sum_grads_{which}_{l}"), shapes, 0) for l in range(depth)]
        grad.update({n: jnp.concatenate([from_slot(n, per_layer[l][i]) for l in range(depth)], axis=0)
                     for i, n in enumerate(group)})
    grad['w_in'] = _unpad_w_in(grad['w_in'])
    grads = small_grads
    small = [dg_final if n == 'g_final' else jnp.stack([grads[l][n] for l in range(depth)]) for n in _REPLICATED]
    small_shapes = [wts[n].shape for n in _REPLICATED]
    small_sum = _sum_slots(_all_gather(_pack(small, 0, _SMALL_ROW_PAD), name="gather_small_grads"), name="sum_small_grads")
    grad.update(dict(zip(_REPLICATED, _unpack(small_sum, small_shapes, 0))))

    delta, new_m, new_v = {}, {}, {}
    for n, _ in _SHARDED:
        delta[n], new_m[n], new_v[n] = _adamw(wts[n], grad[n], mom1[n], mom2[n], name=f"adamw_{n}")
    packed = [_pack([d[n] for n in _REPLICATED], 0, _SMALL_ROW_PAD) for d in (wts, mom1, mom2)]
    outs = _adamw(packed[0], small_sum, packed[1], packed[2], name="adamw_small")
    for d, o in zip((delta, new_m, new_v), outs):
        d.update(dict(zip(_REPLICATED, _unpack(o, small_shapes, 0))))

    return (loss, dx[None], *[grad[n] for n in _WEIGHTS], *[delta[n] for n in _WEIGHTS],
            *[new_m[n] for n in _WEIGHTS], *[new_v[n] for n in _WEIGHTS])
```

```python
import functools

import jax
import jax.numpy as jnp
import numpy as np
from jax import lax
from jax.experimental import pallas as pl
from jax.experimental.pallas import tpu as pltpu

F32 = jnp.float32
BF16 = jnp.bfloat16

EPS = 1e-6
CHUNK = 64
FOX_HEADS, FOX_HD = 4, 64
GLA_HEADS, GLA_DK, GLA_DV, GLA_RANK, GLA_TAU = 4, 64, 128, 16, 16.0
MLA_HEADS, MLA_Q_RANK, MLA_KV_RANK, MLA_NOPE, MLA_ROPE, MLA_VD = 4, 256, 128, 64, 32, 64
ROPE_BASE = 10000.0
XA_HEADS, XA_HD = 4, 128
ADAM_LR, ADAM_B1, ADAM_B2, ADAM_EPS, ADAM_WD, ADAM_STEP = 0.001, 0.9, 0.999, 1e-08, 0.01, 10

N_DEV = 8
V7X_VMEM_LIMIT = 56 * 1024 * 1024
NEG = -1e30

NN = ((1,), (0,))
NT = ((1,), (1,))
TN = ((0,), (0,))


def _dot(a, b, dims):
    return lax.dot_general(a.astype(BF16), b.astype(BF16), (dims, ((), ())), preferred_element_type=F32)


@jax.custom_vjp
def bdot(a, b):
    return _dot(a, b, NN)


bdot.defvjp(lambda a, b: (_dot(a, b, NN), (a, b)),
            lambda res, g: (_dot(g, res[1], NT), _dot(res[0], g, TN)))


@jax.custom_vjp
def bdot_nt(a, b):
    return _dot(a, b, NT)


bdot_nt.defvjp(lambda a, b: (_dot(a, b, NT), (a, b)),
               lambda res, g: (_dot(g, res[1], NN), _dot(g, res[0], TN)))


@jax.custom_vjp
def bdot_tn(a, b):
    return _dot(a, b, TN)


bdot_tn.defvjp(lambda a, b: (_dot(a, b, TN), (a, b)),
               lambda res, g: (_dot(res[1], g, NT), _dot(res[0], g, NN)))


def _split2(x):
    hi = x.astype(BF16)
    lo = (x - hi.astype(F32)).astype(BF16)
    return hi, lo


def _tri(n, lower):
    r = lax.broadcasted_iota(jnp.int32, (n, n), 0)
    c = lax.broadcasted_iota(jnp.int32, (n, n), 1)
    return jnp.where((r >= c) if lower else (r <= c), 1.0, 0.0).astype(BF16)


def _log_sigmoid(x):
    return jnp.minimum(x, 0.0) - jnp.log(1.0 + jnp.exp(-jnp.abs(x)))


def _sigmoid(x):
    return 1.0 / (1.0 + jnp.exp(-x))


def _rms(x, g):
    return x * lax.rsqrt(jnp.mean(x * x, axis=-1, keepdims=True) + EPS) * g


def _pick(dim, prefs):
    for p in prefs:
        if dim % p == 0:
            return p
    return dim


def _params(sem):
    return pltpu.CompilerParams(dimension_semantics=sem, vmem_limit_bytes=V7X_VMEM_LIMIT)


def _rms_vjp(x, g, dy, dres):
    rstd = lax.rsqrt(jnp.mean(x * x, axis=-1, keepdims=True) + EPS)
    xh = x * rstd
    gdy = dy * g
    dx = (gdy - xh * jnp.mean(gdy * xh, axis=-1, keepdims=True)) * rstd
    return (dx if dres is None else dx + dres), jnp.sum(dy * xh, axis=0, keepdims=True)


def _mm(a, b, *, mode, out_dtype, name, act=None, residual=None, drelu_of=None, norm_bwd=None, b_cols=None,
        col_shards=None, rider=None, tm=None, tn=None, tk=None):
    b_off, b_width = b_cols or (0, b.shape[1])
    if mode == 'nn':
        (M, K), N = a.shape, b_width
    elif mode == 'nt':
        (M, K), N = a.shape, b.shape[0]
    else:
        (K, M), N = a.shape, b_width
    tm = tm or _pick(M, (1024, 512, 256, 128))
    tn = tn or _pick(N, (1024, 1920, 1152, 768, 640, 512, 384, 256, 128))
    tk = tk or _pick(K, (1024, 1920, 1152, 640, 512, 256, 128))
    nk = K // tk
    dims = {'nn': NN, 'nt': NT, 'tn': TN}[mode]
    a_spec = pl.BlockSpec((tk, tm), lambda i, j, k: (k, i)) if mode == 'tn' else pl.BlockSpec((tm, tk), lambda i, j, k: (i, k))
    if mode == 'nt':
        b_spec = pl.BlockSpec((tn, tk), lambda i, j, k, o=b_off // tk: (j, k + o))
    else:
        b_spec = pl.BlockSpec((tk, tn), lambda i, j, k, o=b_off // tn: (k, j + o))
    o_spec = pl.BlockSpec((tm, tn), lambda i, j, k: (i, j))
    extra = [e for e in (residual, drelu_of) if e is not None]
    extra_specs = [o_spec] * len(extra)
    out_shape, out_specs, n_out = jax.ShapeDtypeStruct((M, N), out_dtype), o_spec, 1
    if col_shards:
        n_sh = N // col_shards
        assert tn % n_sh == 0 and not extra and norm_bwd is None
        out_shape = jax.ShapeDtypeStruct((col_shards, M, n_sh), out_dtype)
        out_specs = pl.BlockSpec((tn // n_sh, tm, n_sh), lambda i, j, k: (j, i, 0))
    if norm_bwd is not None:
        x_in, g_in, dres_in = norm_bwd
        assert tn == N and residual is None and drelu_of is None
        vec = pl.BlockSpec((1, N), lambda i, j, k: (0, 0))
        extra, extra_specs = [x_in, g_in.reshape(1, N), dres_in], [o_spec, vec, o_spec]
        out_shape = (jax.ShapeDtypeStruct((M, N), F32), jax.ShapeDtypeStruct((M, N), BF16), jax.ShapeDtypeStruct((1, N), F32))
        out_specs, n_out = (o_spec, o_spec, vec), 3

    grid = (M // tm, N // tn, nk)
    r_ins, r_in_specs, r_outs, r_out_specs, r_scratch, split = _carry(
        rider, 2 + len(extra), n_out, lambda: functools.reduce(jnp.logical_and, [pl.program_id(d) == 0 for d in range(3)]),
        lambda: functools.reduce(jnp.logical_and, [pl.program_id(d) == grid[d] - 1 for d in range(3)]))
    assert rider is None or n_out == 1

    def body(*refs):
        a_ref, b_ref, *rest = split(refs)
        o_ref = rest[len(extra)]
        first_rows = pl.program_id(0) == 0
        at = a_ref[...]
        if act == 'relu2':
            at = jnp.square(jnp.maximum(at.astype(F32), 0.0))
        part = _dot(at, b_ref[...], dims)

        def finish(acc):
            if norm_bwd is not None:
                dx, dg = _rms_vjp(rest[0][...], rest[1][...], acc, rest[2][...])
                o_ref[...] = dx
                rest[len(extra) + 1][...] = dx.astype(BF16)
                dg_ref = rest[len(extra) + 2]

                @pl.when(first_rows)
                def _():
                    dg_ref[...] = dg

                @pl.when(jnp.logical_not(first_rows))
                def _():
                    dg_ref[...] += dg
                return
            idx = 0
            if residual is not None:
                acc = acc + rest[idx][...]
                idx += 1
            if drelu_of is not None:
                acc = acc * (2.0 * jnp.maximum(rest[idx][...].astype(F32), 0.0))
            if col_shards:
                for t in range(tn // n_sh):
                    o_ref[t] = acc[:, t * n_sh:(t + 1) * n_sh].astype(out_dtype)
            else:
                o_ref[...] = acc.astype(out_dtype)

        if nk == 1:
            finish(part)
        else:
            acc_ref = rest[len(extra) + n_out]
            k = pl.program_id(2)

            @pl.when(k == 0)
            def _():
                acc_ref[...] = part

            @pl.when(k > 0)
            def _():
                acc_ref[...] += part

            @pl.when(k == nk - 1)
            def _():
                finish(acc_ref[...])

    scratch = [] if nk == 1 else [pltpu.VMEM((tm, tn), F32)]
    if rider is not None:
        res = pl.pallas_call(
            body, name=name, out_shape=(out_shape, *r_outs), grid=grid, in_specs=[a_spec, b_spec] + extra_specs + r_in_specs,
            out_specs=(out_specs, *r_out_specs), scratch_shapes=scratch + r_scratch,
            compiler_params=_params(("arbitrary", "arbitrary", "arbitrary")),
        )(a, b, *extra, *r_ins)
        return res[0], rider.post(res[1:])
    return pl.pallas_call(
        body, name=name, out_shape=out_shape, grid=grid, in_specs=[a_spec, b_spec] + extra_specs, out_specs=out_specs,
        scratch_shapes=scratch,
        compiler_params=_params(("arbitrary" if norm_bwd is not None else "parallel", "parallel", "arbitrary")),
    )(a, b, *extra)


def _rms_fwd(x, g, *, name, out_dtype=BF16):
    S, D = x.shape
    tr = _pick(S, (512, 256, 128))

    def body(x_ref, g_ref, o_ref):
        o_ref[...] = _rms(x_ref[...], g_ref[...]).astype(out_dtype)

    return pl.pallas_call(
        body, name=name, out_shape=jax.ShapeDtypeStruct((S, D), out_dtype), grid=(S // tr,),
        in_specs=[pl.BlockSpec((tr, D), lambda i: (i, 0)), pl.BlockSpec((1, D), lambda i: (0, 0))],
        out_specs=pl.BlockSpec((tr, D), lambda i: (i, 0)),
        compiler_params=_params(("parallel",)),
    )(x, g.reshape(1, D))


def _rms_bwd(x, g, dy, dres, *, name):
    S, D = x.shape
    tr = _pick(S, (512, 256, 128))

    def body(x_ref, g_ref, dy_ref, *rest):
        dx_ref, dxb_ref, dg_ref = rest[-3], rest[-2], rest[-1]
        dx, part = _rms_vjp(x_ref[...], g_ref[...], dy_ref[...].astype(F32), None if dres is None else rest[0][...])
        dx_ref[...] = dx
        dxb_ref[...] = dx.astype(BF16)

        @pl.when(pl.program_id(0) == 0)
        def _():
            dg_ref[...] = part

        @pl.when(pl.program_id(0) > 0)
        def _():
            dg_ref[...] += part

    row = pl.BlockSpec((tr, D), lambda i: (i, 0))
    vec = pl.BlockSpec((1, D), lambda i: (0, 0))
    ins = [x, g.reshape(1, D), dy] + ([dres] if dres is not None else [])
    return pl.pallas_call(
        body, name=name,
        out_shape=(jax.ShapeDtypeStruct((S, D), F32), jax.ShapeDtypeStruct((S, D), BF16), jax.ShapeDtypeStruct((1, D), F32)),
        grid=(S // tr,),
        in_specs=[row, vec, row] + ([row] if dres is not None else []),
        out_specs=(row, row, vec),
        compiler_params=_params(("arbitrary",)),
    )(*ins)


def _loss_head(x, g, target, *, name):
    S, D = x.shape
    tr = _pick(S, (512, 256, 128))

    def body(x_ref, g_ref, t_ref, l_ref, dx_ref, dxb_ref, dg_ref):
        x_ = x_ref[...]
        g_ = g_ref[...]
        rstd = lax.rsqrt(jnp.mean(x_ * x_, axis=-1, keepdims=True) + EPS)
        xh = x_ * rstd
        err = xh * g_ - t_ref[...]
        lpart = (0.5 / D) * jnp.sum(jnp.sum(err * err, axis=-1, keepdims=True), axis=0, keepdims=True)
        dy = err * (1.0 / D)
        gdy = dy * g_
        dx = (gdy - xh * jnp.mean(gdy * xh, axis=-1, keepdims=True)) * rstd
        dx_ref[...] = dx
        dxb_ref[...] = dx.astype(BF16)
        gpart = jnp.sum(dy * xh, axis=0, keepdims=True)

        @pl.when(pl.program_id(0) == 0)
        def _():
            dg_ref[...] = gpart
            l_ref[...] = lpart

        @pl.when(pl.program_id(0) > 0)
        def _():
            dg_ref[...] += gpart
            l_ref[...] += lpart

    row = pl.BlockSpec((tr, D), lambda i: (i, 0))
    vec = pl.BlockSpec((1, D), lambda i: (0, 0))
    return pl.pallas_call(
        body, name=name,
        out_shape=(jax.ShapeDtypeStruct((1, 1), F32), jax.ShapeDtypeStruct((S, D), F32), jax.ShapeDtypeStruct((S, D), BF16),
                   jax.ShapeDtypeStruct((1, D), F32)),
        grid=(S // tr,),
        in_specs=[row, vec, row],
        out_specs=(pl.BlockSpec((1, 1), lambda i: (0, 0)), row, row, vec),
        compiler_params=_params(("arbitrary",)),
    )(x, g.reshape(1, D), target)


def _mask_of(mask, tq, tk, keys_first=False):
    shape, q_axis = ((tk, tq), 1) if keys_first else ((tq, tk), 0)
    qpos = lax.broadcasted_iota(jnp.int32, shape, q_axis)
    kpos = lax.broadcasted_iota(jnp.int32, shape, 1 - q_axis)
    if mask == 'causal':
        return kpos <= qpos
    return kpos <= (qpos | (CHUNK - 1))


LANES = 128
LOG2E = 1.4426950408889634


def _lane_group(j, w, width):
    lane = lax.broadcasted_iota(jnp.int32, (1, width), 1)
    return (lane >= j * w) & (lane < (j + 1) * w)


def _only(x, j, w):
    if w == x.shape[1]:
        return x
    return jnp.where(_lane_group(j, w, x.shape[1]), x, jnp.zeros_like(x))


def _side_by_side(xs):
    return xs[0] if len(xs) == 1 else jnp.concatenate(xs, axis=1)


def _on_top(xs):
    return xs[0] if len(xs) == 1 else jnp.concatenate(xs, axis=0)


def _stacked(x, hp, w):
    return _on_top([_only(x, j, w) for j in range(hp)])


def _col_block(entry, rows, idx):
    arr, off, width = entry
    return pl.BlockSpec((rows, width), lambda i, j, o=off // width: (idx(i, j), o))


def _attn_fwd(qk, v, H, cq, ck, *, scale, mask, name, rider=None):
    Sq, Sk = qk[0][0][0].shape[0], v[0].shape[0]
    dv = v[2] // H
    w0 = qk[0][2]
    hp = LANES // w0
    G = H // hp
    assert dv == w0 and not qk[0][3] and all(sh and H * w == LANES for _, _, w, sh in qk[1:])
    tq = _pick(Sq, (512, 256, 128))
    tk = tq if mask else _pick(Sk, (512, 256, 128))
    nq, nk = Sq // tq, Sk // tk
    bias = cq is not None
    npart = len(qk)

    def body(*refs):
        refs = split(refs)
        q_refs, k_refs = refs[0:2 * npart:2], refs[1:2 * npart:2]
        v_ref = refs[2 * npart]
        cq_ref, ck_ref = (refs[2 * npart + 1], refs[2 * npart + 2]) if bias else (None, None)
        o_ref, lse_ref, m_s, l_s, acc_s = refs[-5:]
        qi, ki = pl.program_id(0), pl.program_id(1)

        @pl.when(ki == 0)
        def _():
            m_s[...] = jnp.full(m_s.shape, NEG, F32)
            l_s[...] = jnp.zeros(l_s.shape, F32)
            acc_s[...] = jnp.zeros(acc_s.shape, F32)

        def rows_of(vals):
            return _on_top([jnp.broadcast_to(r, (w0, tq)) for r in vals])

        def compute(masked):
            keep = _mask_of(mask, tq, tk, keys_first=True) if masked else None
            for g in range(G):
                lanes = slice(g * LANES, (g + 1) * LANES)
                q128, k128, v128 = q_refs[0][:, lanes], k_refs[0][:, lanes], v_ref[:, lanes]
                ps, alphas = [], []
                extras = list(zip(qk, q_refs, k_refs))[1:]
                k_all = _side_by_side([k128] + [k_ref[...] for _, _, k_ref in extras])
                for j in range(hp):
                    h = g * hp + j
                    q_all = _side_by_side([_only(q128, j, w0)] + [_only(q_ref[...], h, w) for (_, _, w, _), q_ref, _ in extras])
                    s = _dot(k_all, q_all, NT) * scale
                    if bias:
                        s = s + (cq_ref[h:h + 1, :] - ck_ref[:, h:h + 1])
                    if masked:
                        s = jnp.where(keep, s, NEG)
                    m_prev = m_s[h:h + 1, :]
                    m_new = jnp.maximum(m_prev, jnp.max(s, axis=0, keepdims=True))
                    alpha = jnp.exp(m_prev - m_new)
                    p = jnp.exp(s - m_new)
                    l_s[h:h + 1, :] = alpha * l_s[h:h + 1, :] + jnp.sum(p, axis=0, keepdims=True)
                    m_s[h:h + 1, :] = m_new
                    ps.append(p.astype(BF16))
                    alphas.append(alpha)
                acc_s[g] = rows_of(alphas) * acc_s[g] + _dot(_stacked(v128, hp, w0), _on_top(ps), TN)

        if mask is None:
            compute(False)
        else:
            pl.when(ki < qi)(lambda: compute(False))
            pl.when(ki == qi)(lambda: compute(True))

        @pl.when(ki == ((nk - 1) if mask is None else qi))
        def _():
            for g in range(G):
                norm = acc_s[g] / rows_of([l_s[g * hp + j:g * hp + j + 1, :] for j in range(hp)])
                o_ref[:, g * LANES:(g + 1) * LANES] = norm.T.astype(BF16)
            lse_ref[...] = jnp.zeros(lse_ref.shape, F32)
            lse_ref[0:H, :] = m_s[0:H, :] + jnp.log(l_s[0:H, :])

    q_idx = lambda i, j: i
    k_idx = (lambda i, j: jnp.minimum(i, j)) if mask else (lambda i, j: j)
    ins, in_specs = [], []
    for q_e, k_e, _, _ in qk:
        ins += [q_e[0], k_e[0]]
        in_specs += [_col_block(q_e, tq, q_idx), _col_block(k_e, tk, k_idx)]
    ins.append(v[0])
    in_specs.append(_col_block(v, tk, k_idx))
    if bias:
        in_specs += [pl.BlockSpec((8, tq), lambda i, j: (0, i)), pl.BlockSpec((tk, 8), lambda i, j: (k_idx(i, j), 0))]
        ins += [cq, ck]
    r_ins, r_in_specs, r_outs, r_out_specs, r_scratch, split = _carry(
        rider, len(ins), 2, lambda: (pl.program_id(0) == 0) & (pl.program_id(1) == 0),
        lambda: (pl.program_id(0) == nq - 1) & (pl.program_id(1) == nk - 1))
    res = pl.pallas_call(
        body, name=name,
        out_shape=(jax.ShapeDtypeStruct((Sq, H * dv), BF16), jax.ShapeDtypeStruct((8, Sq), F32), *r_outs),
        grid=(nq, nk), in_specs=in_specs + r_in_specs,
        out_specs=(pl.BlockSpec((tq, H * dv), lambda i, j: (i, 0)), pl.BlockSpec((8, tq), lambda i, j: (0, i)), *r_out_specs),
        scratch_shapes=[pltpu.VMEM((8, tq), F32), pltpu.VMEM((8, tq), F32), pltpu.VMEM((G, LANES, tq), F32)] + r_scratch,
        compiler_params=_params(("arbitrary", "arbitrary")) if rider else _params(("parallel", "arbitrary")),
    )(*ins, *r_ins)
    return (res[0], res[1], rider.post(res[2:])) if rider else res


def _attn_bwd(qk, v, H, o, do, lse, cq, ck, *, scale, mask, name, rider=None):
    Sq, Sk = qk[0][0][0].shape[0], v[0].shape[0]
    dv = v[2] // H
    w0 = qk[0][2]
    hp = LANES // w0
    G = H // hp
    tq = _pick(Sq, (512, 256, 128))
    tk = tq if mask else _pick(Sk, (512, 256, 128))
    nq, nk = Sq // tq, Sk // tk
    bias = cq is not None
    npart = len(qk)
    n_in = 2 * npart + 4 + (2 if bias else 0)

    def body(*refs):
        refs = split(refs)
        q_refs, k_refs = refs[0:2 * npart:2], refs[1:2 * npart:2]
        v_ref, o_ref, do_ref, lse_ref = refs[2 * npart:2 * npart + 4]
        cq_ref, ck_ref = (refs[2 * npart + 4], refs[2 * npart + 5]) if bias else (None, None)
        outs = refs[n_in:]
        dq_refs, dk_refs, dv_ref = outs[:npart], outs[npart:2 * npart], outs[2 * npart]
        dck_ref, dcq_ref = (outs[2 * npart + 1], outs[2 * npart + 2]) if bias else (None, None)
        dk_accs, dv_acc = refs[-(npart + 1):-1], refs[-1]
        ki, qi = pl.program_id(0), pl.program_id(1)
        first_q = ki if mask else 0

        @pl.when((ki == 0) & (qi == 0))
        def _():
            for r in dq_refs:
                r[...] = jnp.zeros(r.shape, F32)
            if bias:
                dcq_ref[...] = jnp.zeros(dcq_ref.shape, F32)

        @pl.when(qi == first_q)
        def _():
            for r in dk_accs:
                r[...] = jnp.zeros(r.shape, F32)
            dv_acc[...] = jnp.zeros(dv_acc.shape, F32)
            if bias:
                dck_ref[...] = jnp.zeros(dck_ref.shape, F32)

        def compute(masked):
            keep = _mask_of(mask, tq, tk, keys_first=True) if masked else None
            rows = pl.ds(pl.multiple_of(qi * tq, tq), tq)
            extras = list(zip(qk, q_refs, k_refs, dq_refs, dk_accs))[1:]
            for g in range(G):
                lanes = slice(g * LANES, (g + 1) * LANES)
                q128, k128, v128 = q_refs[0][:, lanes], k_refs[0][:, lanes], v_ref[:, lanes]
                do128, o128 = do_ref[:, lanes], o_ref[:, lanes]
                prod = do128.astype(F32) * o128.astype(F32)
                ps, dss = [], []
                k_all = _side_by_side([k128] + [e[2][...] for e in extras])
                for j in range(hp):
                    h = g * hp + j
                    q_all = _side_by_side([_only(q128, j, w0)] + [_only(e[1][...], h, e[0][2]) for e in extras])
                    s = _dot(k_all, q_all, NT) * (scale * LOG2E)
                    if bias:
                        s = s - ck_ref[:, h:h + 1] * LOG2E
                    if masked:
                        s = jnp.where(keep, s, NEG)
                    row = lse_ref[h:h + 1, :] - cq_ref[h:h + 1, :] if bias else lse_ref[h:h + 1, :]
                    p = jnp.exp2(s - row * LOG2E)
                    dp = _dot(v128, _only(do128, j, w0), NT)
                    delta = jnp.sum(_only(prod, j, w0), axis=1, keepdims=True).T
                    ds = p * (dp - delta)
                    if bias:
                        dck_ref[:, h:h + 1] -= jnp.sum(ds, axis=1, keepdims=True)
                        dcq_ref[h:h + 1, rows] += jnp.sum(ds, axis=0, keepdims=True)
                    ps.append(p.astype(BF16))
                    dss.append((ds * scale).astype(BF16))
                for (_, _, w, _), q_ref, k_ref, dq_ref, dk_acc in extras:
                    heads = range(g * hp, (g + 1) * hp)
                    dk_acc[...] += _dot(_side_by_side(dss), _on_top([_only(q_ref[...], h, w) for h in heads]), NN)
                    dq_ref[rows, :] += _dot(_on_top(dss), _on_top([_only(k_ref[...], h, w) for h in heads]), TN)
                dv_acc[:, lanes] += _dot(_side_by_side(ps), _stacked(do128, hp, w0), NN)
                dk_accs[0][:, lanes] += _dot(_side_by_side(dss), _stacked(q128, hp, w0), NN)
                dq_refs[0][rows, lanes] += _dot(_on_top(dss), _stacked(k128, hp, w0), TN)

        if mask is None:
            compute(False)
        else:
            pl.when(qi > ki)(lambda: compute(False))
            pl.when(qi == ki)(lambda: compute(True))

        @pl.when(qi == nq - 1)
        def _():
            for r, acc in zip(dk_refs, dk_accs):
                r[...] = acc[...]
            dv_ref[...] = dv_acc[...]

    q_idx = (lambda j, i: jnp.maximum(i, j)) if mask else (lambda j, i: i)
    k_idx = lambda j, i: j
    ins, in_specs, dq_shapes, dq_specs, dk_shapes, dk_specs, scratch = [], [], [], [], [], [], []
    for q_e, k_e, w, shared in qk:
        ins += [q_e[0], k_e[0]]
        in_specs += [_col_block(q_e, tq, q_idx), _col_block(k_e, tk, k_idx)]
        dq_shapes.append(jax.ShapeDtypeStruct((Sq, H * w), F32))
        dq_specs.append(pl.BlockSpec((Sq, H * w), lambda j, i: (0, 0)))
        kw = k_e[2]
        dk_shapes.append(jax.ShapeDtypeStruct((Sk, kw), F32))
        dk_specs.append(pl.BlockSpec((tk, kw), lambda j, i: (j, 0)))
        scratch.append(pltpu.VMEM((tk, kw), F32))
    row_q = lambda width: pl.BlockSpec((tq, width), lambda j, i: (q_idx(j, i), 0))
    per_q = pl.BlockSpec((8, tq), lambda j, i: (0, q_idx(j, i)))
    ins += [v[0], o, do, lse]
    in_specs += [_col_block(v, tk, k_idx), row_q(H * dv), row_q(H * dv), per_q]
    out_shape = dq_shapes + dk_shapes + [jax.ShapeDtypeStruct((Sk, H * dv), F32)]
    out_specs = dq_specs + dk_specs + [pl.BlockSpec((tk, H * dv), lambda j, i: (j, 0))]
    if bias:
        in_specs += [per_q, pl.BlockSpec((tk, 8), lambda j, i: (j, 0))]
        ins += [cq, ck]
        out_shape += [jax.ShapeDtypeStruct((Sk, 8), F32), jax.ShapeDtypeStruct((8, Sq), F32)]
        out_specs += [pl.BlockSpec((tk, 8), lambda j, i: (j, 0)), pl.BlockSpec((8, Sq), lambda j, i: (0, 0))]
    scratch.append(pltpu.VMEM((tk, H * dv), F32))
    n_out = len(out_shape)
    r_ins, r_in_specs, r_outs, r_out_specs, r_scratch, split = _carry(
        rider, len(ins), n_out, lambda: (pl.program_id(0) == 0) & (pl.program_id(1) == 0),
        lambda: (pl.program_id(0) == nk - 1) & (pl.program_id(1) == nq - 1))
    res = pl.pallas_call(
        body, name=name, out_shape=tuple(out_shape + r_outs), grid=(nk, nq), in_specs=in_specs + r_in_specs,
        out_specs=tuple(out_specs + r_out_specs), scratch_shapes=scratch + r_scratch,
        compiler_params=_params(("arbitrary", "arbitrary")),
    )(*ins, *r_ins)
    own = (list(res[:npart]), list(res[npart:2 * npart]), res[2 * npart]) + tuple(res[2 * npart + 1:n_out])
    return own + (rider.post(res[n_out:]),) if rider else own


def _split3_dot(x, t):
    hi = x.astype(BF16)
    r1 = x - hi.astype(F32)
    mid = r1.astype(BF16)
    lo = (r1 - mid.astype(F32)).astype(BF16)
    return _dot(hi, t, NN) + _dot(mid, t, NN) + _dot(lo, t, NN)


def _fox_cum_fwd(ff_t, b, *, name):
    _, S = ff_t.shape
    tb = _pick(S, (512, 256, 128))

    def body(f_ref, b_ref, o_ref, carry):
        @pl.when(pl.program_id(0) == 0)
        def _():
            carry[...] = jnp.zeros(carry.shape, F32)

        lf = _log_sigmoid(f_ref[...] + b_ref[...])
        o_ref[...] = _split3_dot(lf, _tri(tb, False)) + carry[...]
        carry[...] += jnp.sum(lf, axis=1, keepdims=True)

    return pl.pallas_call(
        body, name=name, out_shape=jax.ShapeDtypeStruct((8, S), F32), grid=(S // tb,),
        in_specs=[pl.BlockSpec((8, tb), lambda i: (0, i)), pl.BlockSpec((8, 1), lambda i: (0, 0))],
        out_specs=pl.BlockSpec((8, tb), lambda i: (0, i)),
        scratch_shapes=[pltpu.VMEM((8, 1), F32)],
        compiler_params=_params(("arbitrary",)),
    )(ff_t, b)


def _fox_cum_bwd(ff_t, b, dcum_t, *, name):
    _, S = ff_t.shape
    tb = _pick(S, (512, 256, 128))
    nb = S // tb

    def body(f_ref, b_ref, dc_ref, df_ref, db_ref, carry):
        @pl.when(pl.program_id(0) == 0)
        def _():
            carry[...] = jnp.zeros(carry.shape, F32)
            db_ref[...] = jnp.zeros(db_ref.shape, F32)

        dc = dc_ref[...]
        dlf = _split3_dot(dc, _tri(tb, True)) + carry[...]
        carry[...] += jnp.sum(dc, axis=1, keepdims=True)
        df = dlf * _sigmoid(-(f_ref[...] + b_ref[...]))
        df_ref[...] = df
        db_ref[...] += jnp.sum(df, axis=1, keepdims=True)

    rev = lambda i: (0, nb - 1 - i)
    return pl.pallas_call(
        body, name=name,
        out_shape=(jax.ShapeDtypeStruct((8, S), F32), jax.ShapeDtypeStruct((8, 1), F32)), grid=(nb,),
        in_specs=[pl.BlockSpec((8, tb), rev), pl.BlockSpec((8, 1), lambda i: (0, 0)), pl.BlockSpec((8, tb), rev)],
        out_specs=(pl.BlockSpec((8, tb), rev), pl.BlockSpec((8, 1), lambda i: (0, 0))),
        scratch_shapes=[pltpu.VMEM((8, 1), F32)],
        compiler_params=_params(("arbitrary",)),
    )(ff_t, b, dcum_t)


GLA_W = GLA_HEADS * GLA_DK
GLA_BLOCK_CHUNKS = 4


def _same_chunk(n, lower):
    r = lax.broadcasted_iota(jnp.int32, (n, n), 0)
    c = lax.broadcasted_iota(jnp.int32, (n, n), 1)
    same = (r | (CHUNK - 1)) == (c | (CHUNK - 1))
    return jnp.where(same & (r >= c) if lower else same, 1.0, 0.0).astype(BF16)


def _chunk_mix(x, t, transpose):
    hi, lo = _split2(x)
    dims = TN if transpose else NN
    return _dot(t, hi, dims) + _dot(t, lo, dims)


@jax.custom_vjp
def chunk_cumsum(x):
    return _chunk_mix(x, _same_chunk(x.shape[0], True), False)


chunk_cumsum.defvjp(lambda x: (chunk_cumsum(x), None), lambda _, g: (_chunk_mix(g, _same_chunk(g.shape[0], True), True),))


@jax.custom_vjp
def chunk_total(x):
    return _chunk_mix(x, _same_chunk(x.shape[0], False), False)


chunk_total.defvjp(lambda x: (chunk_total(x), None), lambda _, g: (_chunk_mix(g, _same_chunk(g.shape[0], False), False),))


def _gla_block(q, k, zsm, wg, bg, go, vs, rs, states):
    n_chunks = q.shape[0] // CHUNK
    la = _log_sigmoid(bdot(zsm, wg) + bg) * (1.0 / GLA_TAU)
    end = chunk_total(la)
    kd = k * jnp.exp(end - chunk_cumsum(la))
    a = jnp.exp(end)
    qs = q * (GLA_DK ** -0.5)
    lane = lax.broadcasted_iota(jnp.int32, (1, GLA_W), 1)
    outs, new_states = [], []
    for h in range(GLA_HEADS):
        kdh = kd * jnp.where((lane >= h * GLA_DK) & (lane < (h + 1) * GLA_DK), 1.0, 0.0)
        st, o = states[h], []
        for c in range(n_chunks):
            rows = slice(c * CHUNK, (c + 1) * CHUNK)
            st = st * a[c * CHUNK:c * CHUNK + 1] + bdot_tn(vs[h][rows], kdh[rows])
            o.append(bdot_nt(qs[rows], st))
        o = _rms(jnp.concatenate(o, axis=0), go)
        outs.append(o * (rs[h] * _sigmoid(rs[h])))
        new_states.append(st)
    return outs, new_states


def _gla_fwd(z, zsm, wg, bg, go, cols, *, name):
    S = z.shape[0]
    rb = GLA_BLOCK_CHUNKS * CHUNK
    nb = S // rb
    cq, ckk, cv, cr = cols
    H = GLA_HEADS

    def body(q_ref, k_ref, zsm_ref, wg_ref, bg_ref, go_ref, *rest):
        v_refs, r_refs = rest[:H], rest[H:2 * H]
        o_ref, st_ref, state = rest[2 * H], rest[2 * H + 1], rest[2 * H + 2]

        @pl.when(pl.program_id(0) == 0)
        def _():
            state[...] = jnp.zeros(state.shape, F32)

        states = [state[h] for h in range(H)]
        for h in range(H):
            st_ref[0, h] = states[h]
        outs, new_states = _gla_block(
            q_ref[...].astype(F32), k_ref[...].astype(F32), zsm_ref[...], wg_ref[...], bg_ref[...], go_ref[...],
            [v_refs[h][...].astype(F32) for h in range(H)], [r_refs[h][...].astype(F32) for h in range(H)], states)
        for h in range(H):
            o_ref[:, h * GLA_DV:(h + 1) * GLA_DV] = outs[h].astype(BF16)
            state[h] = new_states[h]

    def col(width, off):
        return pl.BlockSpec((rb, width), lambda i, o=off // width: (i, o))

    full = lambda shp: pl.BlockSpec(shp, lambda i: (0,) * len(shp))
    in_specs = [col(GLA_W, cq), col(GLA_W, ckk), pl.BlockSpec((rb, 128), lambda i: (i, 0)),
                full((128, GLA_W)), full((1, GLA_W)), full((1, GLA_DV))]
    in_specs += [col(GLA_DV, cv + h * GLA_DV) for h in range(H)] + [col(GLA_DV, cr + h * GLA_DV) for h in range(H)]
    return pl.pallas_call(
        body, name=name,
        out_shape=(jax.ShapeDtypeStruct((S, H * GLA_DV), BF16), jax.ShapeDtypeStruct((nb, H, GLA_DV, GLA_W), F32)),
        grid=(nb,), in_specs=in_specs,
        out_specs=(pl.BlockSpec((rb, H * GLA_DV), lambda i: (i, 0)),
                   pl.BlockSpec((1, H, GLA_DV, GLA_W), lambda i: (i, 0, 0, 0))),
        scratch_shapes=[pltpu.VMEM((H, GLA_DV, GLA_W), F32)],
        compiler_params=_params(("arbitrary",)),
    )(z, z, zsm, wg, bg, go, *([z] * (2 * H)))


def _gla_bwd(z, zsm, wg, bg, go, states, do, cols, *, name):
    S = z.shape[0]
    rb = GLA_BLOCK_CHUNKS * CHUNK
    nb = S // rb
    cq, ckk, cv, cr = cols
    H = GLA_HEADS

    def body(q_ref, k_ref, zsm_ref, wg_ref, bg_ref, go_ref, st_ref, do_ref, *rest):
        v_refs, r_refs = rest[:H], rest[H:2 * H]
        dq_ref, dk_ref, dv_ref, dr_ref, dzsm_ref, dwg_ref, dbg_ref, dgo_ref, dstate = rest[2 * H:]

        @pl.when(pl.program_id(0) == 0)
        def _():
            dstate[...] = jnp.zeros(dstate.shape, F32)
            dwg_ref[...] = jnp.zeros(dwg_ref.shape, F32)
            dbg_ref[...] = jnp.zeros(dbg_ref.shape, F32)
            dgo_ref[...] = jnp.zeros(dgo_ref.shape, F32)

        prim = (q_ref[...].astype(F32), k_ref[...].astype(F32), zsm_ref[...], wg_ref[...], bg_ref[...], go_ref[...],
                [v_refs[h][...].astype(F32) for h in range(H)], [r_refs[h][...].astype(F32) for h in range(H)],
                [st_ref[0, h] for h in range(H)])
        _, vjp = jax.vjp(_gla_block, *prim)
        douts = [do_ref[:, h * GLA_DV:(h + 1) * GLA_DV].astype(F32) for h in range(H)]
        dq, dk, dzs, dwg, dbg, dgo, dvs, drs, dsts = vjp((douts, [dstate[h] for h in range(H)]))
        dq_ref[...] = dq.astype(BF16)
        dk_ref[...] = dk.astype(BF16)
        dzsm_ref[...] = dzs
        dwg_ref[...] += dwg
        dbg_ref[...] += dbg
        dgo_ref[...] += dgo
        for h in range(H):
            dv_ref[:, h * GLA_DV:(h + 1) * GLA_DV] = dvs[h].astype(BF16)
            dr_ref[:, h * GLA_DV:(h + 1) * GLA_DV] = drs[h].astype(BF16)
            dstate[h] = dsts[h]

    rev = lambda i: nb - 1 - i

    def col(width, off):
        return pl.BlockSpec((rb, width), lambda i, o=off // width: (rev(i), o))

    full = lambda shp: pl.BlockSpec(shp, lambda i: (0,) * len(shp))
    rowb = lambda w: pl.BlockSpec((rb, w), lambda i: (rev(i), 0))
    in_specs = [col(GLA_W, cq), col(GLA_W, ckk), rowb(128), full((128, GLA_W)), full((1, GLA_W)), full((1, GLA_DV)),
                pl.BlockSpec((1, H, GLA_DV, GLA_W), lambda i: (rev(i), 0, 0, 0)), rowb(H * GLA_DV)]
    in_specs += [col(GLA_DV, cv + h * GLA_DV) for h in range(H)] + [col(GLA_DV, cr + h * GLA_DV) for h in range(H)]
    return pl.pallas_call(
        body, name=name,
        out_shape=(jax.ShapeDtypeStruct((S, GLA_W), BF16), jax.ShapeDtypeStruct((S, GLA_W), BF16),
                   jax.ShapeDtypeStruct((S, H * GLA_DV), BF16), jax.ShapeDtypeStruct((S, H * GLA_DV), BF16),
                   jax.ShapeDtypeStruct((S, 128), F32), jax.ShapeDtypeStruct((128, GLA_W), F32),
                   jax.ShapeDtypeStruct((1, GLA_W), F32), jax.ShapeDtypeStruct((1, GLA_DV), F32)),
        grid=(nb,), in_specs=in_specs,
        out_specs=(rowb(GLA_W), rowb(GLA_W), rowb(H * GLA_DV), rowb(H * GLA_DV), rowb(128),
                   full((128, GLA_W)), full((1, GLA_W)), full((1, GLA_DV))),
        scratch_shapes=[pltpu.VMEM((H, GLA_DV, GLA_W), F32)],
        compiler_params=_params(("arbitrary",)),
    )(z, z, zsm, wg, bg, go, states, do, *([z] * (2 * H)))


def _row_spec(entry, tr):
    if isinstance(entry, tuple):
        arr, width, off = entry
        return arr, pl.BlockSpec((tr, width), lambda i, o=off // width: (i, o))
    return entry, pl.BlockSpec((tr, entry.shape[1]), lambda i: (i, 0))


def _stage_fwd(fn, rows, consts, outs, *, name, tr=None):
    first = rows[0][0] if isinstance(rows[0], tuple) else rows[0]
    S = first.shape[0]
    tr = tr or _pick(S, (512, 256, 128))
    arrs, specs = zip(*[_row_spec(e, tr) for e in rows])
    nr, nc = len(rows), len(consts)

    def body(*refs):
        vals = [r[...].astype(F32) for r in refs[:nr + nc]]
        res = fn(*vals)
        for o_ref, val in zip(refs[nr + nc:], res):
            o_ref[...] = val.astype(o_ref.dtype)

    cspecs = [pl.BlockSpec(c.shape, lambda i, n=c.ndim: (0,) * n) for c in consts]
    return pl.pallas_call(
        body, name=name,
        out_shape=tuple(jax.ShapeDtypeStruct((S, w), dt) for w, dt in outs), grid=(S // tr,),
        in_specs=list(specs) + cspecs,
        out_specs=tuple(pl.BlockSpec((tr, w), lambda i: (i, 0)) for w, _ in outs),
        compiler_params=_params(("parallel",)),
    )(*arrs, *consts)


def _stage_bwd(fn, rows, consts, cts, n_diff, drow_dtypes, *, name, tr=None, lead=None):
    first = rows[0][0] if isinstance(rows[0], tuple) else rows[0]
    S = first.shape[0]
    tr = tr or _pick(S, (512, 256, 128))
    arrs, specs = zip(*[_row_spec(e, tr) for e in rows])
    widths = [e[1] if isinstance(e, tuple) else e.shape[1] for e in rows]
    nr, nc, nt = len(rows), len(consts), len(cts)
    n_lead, lead_width = lead or (1, widths[0])
    n_rows_out = n_diff - n_lead + 1

    def body(*refs):
        vals = [r[...].astype(F32) for r in refs[:nr + nc]]
        ct = [r[...].astype(F32) for r in refs[nr + nc:nr + nc + nt]]
        drow_refs = refs[nr + nc + nt:nr + nc + nt + n_rows_out]
        dconst_refs = refs[nr + nc + nt + n_rows_out:]
        rest_rows = vals[n_diff:nr]

        def f(diff_rows, cs):
            return tuple(fn(*diff_rows, *rest_rows, *cs))

        _, vjp = jax.vjp(f, vals[:n_diff], vals[nr:])
        drows, dcs = vjp(tuple(ct))
        off = 0
        for val, w in zip(drows[:n_lead], widths):
            drow_refs[0][:, off:off + w] = val.astype(drow_refs[0].dtype)
            off += w
        for r, val in zip(drow_refs[1:], drows[n_lead:]):
            r[...] = val.astype(r.dtype)
        first_step = pl.program_id(0) == 0
        for r, val in zip(dconst_refs, dcs):
            @pl.when(first_step)
            def _(r=r, val=val):
                r[...] = val

            @pl.when(jnp.logical_not(first_step))
            def _(r=r, val=val):
                r[...] += val

    cspecs = [pl.BlockSpec(c.shape, lambda i, n=c.ndim: (0,) * n) for c in consts]
    ctspecs = [pl.BlockSpec((tr, c.shape[1]), lambda i: (i, 0)) for c in cts]
    out_shape = [jax.ShapeDtypeStruct((S, lead_width), drow_dtypes[0])]
    out_shape += [jax.ShapeDtypeStruct((S, widths[j]), drow_dtypes[j]) for j in range(n_lead, n_diff)]
    out_shape += [jax.ShapeDtypeStruct(c.shape, F32) for c in consts]
    out_specs = [pl.BlockSpec((tr, sum(widths[:n_lead])), lambda i: (i, 0))]
    out_specs += [pl.BlockSpec((tr, widths[j]), lambda i: (i, 0)) for j in range(n_lead, n_diff)] + cspecs
    res = pl.pallas_call(
        body, name=name, out_shape=tuple(out_shape), grid=(S // tr,),
        in_specs=list(specs) + cspecs + ctspecs, out_specs=tuple(out_specs),
        compiler_params=_params(("arbitrary",)),
    )(*arrs, *consts, *cts)
    return list(res[:n_rows_out]), list(res[n_rows_out:])


def _mla_prep_fn(cq, ckv, kr, kr_sw, cos, sin, gq, gkv, wq_n, wq_r, wq_sw, wk, wv):
    hq = _rms(cq, gq)
    hkv = _rms(ckv, gkv)
    return (bdot(hq, wq_n), bdot(hq, wq_r) * cos + bdot(hq, wq_sw) * sin,
            bdot(hkv, wk), bdot(hkv, wv), kr * cos + kr_sw * sin)


def _merge_fn(g0, g1, g2, of, og, om, b0, b1, b2, wf, wg, wm):
    return (_sigmoid(g0 + b0) * bdot(of, wf) + _sigmoid(g1 + b1) * bdot(og, wg) + _sigmoid(g2 + b2) * bdot(om, wm),)


_IN_SIZES = (256, 256, 256, 4, 256, 256, 512, 16, 512, 256, 128, 32, 3072)
_IN_OFF = np.concatenate([[0], np.cumsum(_IN_SIZES)])
(_O_FQ, _O_FK, _O_FV, _O_FF, _O_GQ, _O_GK, _O_GV, _O_GLOW, _O_GR, _O_MQ, _O_MKV, _O_MKR, _O_ZG) = [int(o) for o in _IN_OFF[:-1]]
N_IN = int(_IN_OFF[-1])
_BIG_GROUPS = ((_O_ZG, 3072), (_O_GV, 512), (_O_GR, 512), (_O_FQ, 256), (_O_FK, 256), (_O_FV, 256),
               (_O_GQ, 256), (_O_GK, 256), (_O_MQ, 256), (_O_MKV, 128))
Z_GATE, Z_GV, Z_GR, Z_FQ, Z_FK, Z_FV, Z_GQ, Z_GK, Z_MQ, Z_MKV = [int(o) for o in
                                                                    np.concatenate([[0], np.cumsum([w for _, w in _BIG_GROUPS])])[:-1]]
N_BIG = sum(w for _, w in _BIG_GROUPS)
_HALF = MLA_ROPE // 2
_QK_HD = MLA_NOPE + MLA_ROPE
SM_FF, SM_GLOW, SM_KR, SM_KR_SW, N_SM = 0, 8, 128, 256, 384
N_PAD = N_BIG + N_SM
_IN_SEGS = ([(o, w, 1.0) for o, w in _BIG_GROUPS]
            + [(_O_FF, 4, 1.0), (None, SM_GLOW - 4, 0.0), (_O_GLOW, GLA_RANK, 1.0), (None, 128 - SM_GLOW - GLA_RANK, 0.0)]
            + [(_O_MKR, MLA_ROPE, 1.0)] * MLA_HEADS
            + [(_O_MKR + _HALF, _HALF, -1.0), (_O_MKR, _HALF, 1.0)] * MLA_HEADS)


def _cols(x, start, width):
    return lax.slice_in_dim(x, start, start + width, axis=x.ndim - 1)


def _pad_w_in(w):
    return jnp.concatenate([jnp.zeros(w.shape[:-1] + (n,), w.dtype) if src is None else
                            (_cols(w, src, n) if sign > 0 else -_cols(w, src, n)) for src, n, sign in _IN_SEGS], axis=-1)


def _unpad_w_in(g):
    groups = []
    for o, n in zip(_IN_OFF[:-1], _IN_SIZES):
        total, pos = None, 0
        for src, m, sign in _IN_SEGS:
            if src is not None and o <= src and src + m <= o + n:
                term = _cols(g, pos, m) if sign > 0 else -_cols(g, pos, m)
                if m != n:
                    term = jnp.pad(term, [(0, 0)] * (g.ndim - 1) + [(int(src - o), int(o + n - src - m))])
                total = term if total is None else total + term
            pos += m
        groups.append(total)
    return jnp.concatenate(groups, axis=-1)


def _take(x, idx):
    idx = np.asarray(idx)
    cuts = [0] + [i for i in range(1, len(idx)) if idx[i] != idx[i - 1] + 1] + [len(idx)]
    return jnp.concatenate([_cols(x, int(idx[a]), b - a) for a, b in zip(cuts[:-1], cuts[1:])], axis=1)


_UQ_NOPE = np.concatenate([np.arange(h * _QK_HD, h * _QK_HD + MLA_NOPE) for h in range(MLA_HEADS)])
_UQ_ROT = np.concatenate([np.arange(h * _QK_HD + MLA_NOPE, (h + 1) * _QK_HD) for h in range(MLA_HEADS)])
_UKV_PERM = np.concatenate(
    [np.concatenate([np.arange(h * 128, h * 128 + MLA_NOPE) for h in range(MLA_HEADS)]),
     np.concatenate([np.arange(h * 128 + MLA_NOPE, (h + 1) * 128) for h in range(MLA_HEADS)])])
_UKV_INV = np.argsort(_UKV_PERM)


def _rotary_partner(r):
    return jnp.concatenate([piece for h in range(MLA_HEADS) for piece in
                            (-_cols(r, h * MLA_ROPE + _HALF, _HALF), _cols(r, h * MLA_ROPE, _HALF))], axis=1)


def _uq_grad(dn, dr, dsw):
    dr = dr + jnp.concatenate([piece for h in range(MLA_HEADS) for piece in
                               (_cols(dsw, h * MLA_ROPE + _HALF, _HALF), -_cols(dsw, h * MLA_ROPE, _HALF))], axis=1)
    return jnp.concatenate([piece for h in range(MLA_HEADS) for piece in
                            (_cols(dn, h * MLA_NOPE, MLA_NOPE), _cols(dr, h * MLA_ROPE, MLA_ROPE))], axis=1)


def _rope_tables(S):
    inv = ROPE_BASE ** (-jnp.arange(_HALF, dtype=F32) / _HALF)
    ang = jnp.arange(S, dtype=F32)[:, None] * inv[None, :]
    return jnp.tile(jnp.cos(ang), (1, 2 * MLA_HEADS)), jnp.tile(jnp.sin(ang), (1, 2 * MLA_HEADS))


class _LayerParams:
    def __init__(self, rep, l):
        self.w, self.rep, self.l, self.made = {}, rep, l, {}

    def __getitem__(self, k):
        if k not in self.made:
            self.made[k] = self._make(k)
        return self.made[k]

    def _make(self, k):
        w, rep, l = self.w, self.rep, self.l
        if k == 'wg':
            return jnp.pad(w['w_gla_gate'], [(SM_GLOW, LANES - SM_GLOW - GLA_RANK), (0, 0)])
        if k in ('wq_n', 'wq_r'):
            return _take(w['w_mla_uq'], _UQ_NOPE if k == 'wq_n' else _UQ_ROT)
        if k == 'wq_sw':
            return _rotary_partner(self['wq_r'])
        if k in ('wk', 'wv'):
            return _take(w['w_mla_ukv'], _UKV_PERM[:256] if k == 'wk' else _UKV_PERM[256:])
        if k == 'b_f':
            return jnp.zeros((8, 1), F32).at[:FOX_HEADS, 0].set(rep['b_fox_forget'][l])
        if k == 'b_gate':
            return [rep['b_branch_gate'][l][i * 1024:(i + 1) * 1024].reshape(1, 1024) for i in range(3)]
        vec = {'bg': 'b_gla_gate', 'go': 'g_gla_out', 'gq': 'g_mla_q', 'gkv': 'g_mla_kv'}
        if k in vec:
            return rep[vec[k]][l].reshape(1, -1)
        return rep[k][l] if k in rep else w[k]


_GLA_COLS = (Z_GQ, Z_GK, Z_GV, Z_GR)
_MLA_OUTS = [(256, BF16), (128, BF16), (256, BF16), (256, BF16), (128, BF16)]


def _mla_rows(z, zsm, rope):
    return [(z, 256, Z_MQ), (z, 128, Z_MKV), (zsm, 128, SM_KR), (zsm, 128, SM_KR_SW), *rope]


def _mla_consts(p):
    return [p['gq'], p['gkv'], p['wq_n'], p['wq_r'], p['wq_sw'], p['wk'], p['wv']]


def _fox_qkv(z):
    return [((z, Z_FQ, 256), (z, Z_FK, 256), FOX_HD, False)], (z, Z_FV, 256)


def _mla_qkv(qn, qr, kn, vv, kr):
    return [((qn, 0, 256), (kn, 0, 256), MLA_NOPE, False), ((qr, 0, 128), (kr, 0, 128), MLA_ROPE, True)], (vv, 0, 256)


def _xa_qkv(qx, kvx):
    return [((qx, 0, 512), (kvx, 0, 512), XA_HD, False)], (kvx, 512, 512)


def _merge_rows(z, o_fox, o_gla, o_mla):
    return [(z, 1024, Z_GATE), (z, 1024, Z_GATE + 1024), (z, 1024, Z_GATE + 2048), o_fox, o_gla, o_mla]


def _merge_consts(p):
    return p['b_gate'] + [p['w_up_fox'], p['w_up_gla'], p['w_up_mla']]


def _carried(hooks, key, call, single=False):
    entries = hooks.pop(key, [])
    if not entries:
        return call(rider=None)
    res = call(rider=_join_riders([rider for rider, _ in entries]))
    for (_, sink), got in zip(entries, res[-1]):
        sink(got)
    return res[0] if single else res[:-1]


def _layer_fwd(x0, mem, p, rope, l, hooks):
    S = x0.shape[0]
    sv = {'x0': x0}

    def mm(key, a, b, **kw):
        return _carried(hooks, (l, key), lambda rider: _mm(a, b, mode='nn', rider=rider, name=f"{key}_{l}", **kw), single=True)

    h1 = _rms_fwd(x0, p['g_mix'], name=f"rms_mix_{l}")
    z = mm('in_big', h1, p['w_in'], out_dtype=BF16, b_cols=(0, N_BIG))
    zsm = _mm(h1, p['w_in'], mode='nn', out_dtype=F32, b_cols=(N_BIG, N_SM), name=f"in_small_{l}")
    sv.update(h1=h1, z=z, zsm=zsm)
    ff_t = jnp.zeros((8, S), F32).at[:FOX_HEADS].set(zsm[:, SM_FF:SM_FF + FOX_HEADS].T)
    cum_t = _fox_cum_fwd(ff_t, p['b_f'], name=f"fox_cum_{l}")
    cum = cum_t.T
    o_fox, lse_f = _carried(hooks, (l, 'fox_fwd'), lambda rider: _attn_fwd(
        *_fox_qkv(z), FOX_HEADS, cum_t, cum, scale=FOX_HD ** -0.5, mask='causal', name=f"fox_fwd_{l}", rider=rider))
    sv.update(ff_t=ff_t, cum=cum, cum_t=cum_t, lse_f=lse_f, o_fox=o_fox)
    o_gla, states = _gla_fwd(z, zsm, p['wg'], p['bg'], p['go'], _GLA_COLS, name=f"gla_fwd_{l}")
    sv.update(o_gla=o_gla, states=states)
    mla = _stage_fwd(_mla_prep_fn, _mla_rows(z, zsm, rope), _mla_consts(p), _MLA_OUTS, name=f"mla_prep_{l}")
    o_mla, lse_m = _carried(hooks, (l, 'mla_fwd'), lambda rider: _attn_fwd(
        *_mla_qkv(*mla), MLA_HEADS, None, None, scale=_QK_HD ** -0.5, mask='chunk', name=f"mla_fwd_{l}", rider=rider))
    sv.update(mla=mla, lse_m=lse_m, o_mla=o_mla)
    (y,) = _stage_fwd(_merge_fn, _merge_rows(z, o_fox, o_gla, o_mla), _merge_consts(p), [(1024, BF16)], name=f"merge_{l}")
    x1 = mm('out_proj', y, p['w_out'], out_dtype=F32, residual=x0)
    sv.update(y=y, x1=x1)
    h2 = _rms_fwd(x1, p['g_xa'], name=f"rms_xa_{l}")
    hm = _rms_fwd(mem, p['g_mem'], name=f"rms_mem_{l}")
    qx = _mm(h2, p['w_xq'], mode='nn', out_dtype=BF16, name=f"xq_{l}")
    kvx = _mm(hm, p['w_xkv'], mode='nn', out_dtype=BF16, name=f"xkv_{l}")
    ox, lse_x = _carried(hooks, (l, 'xa_fwd'), lambda rider: _attn_fwd(
        *_xa_qkv(qx, kvx), XA_HEADS, None, None, scale=XA_HD ** -0.5, mask=None, name=f"xa_fwd_{l}", rider=rider))
    x2 = mm('xo', ox, p['w_xo'], out_dtype=F32, residual=x1)
    sv.update(h2=h2, hm=hm, qx=qx, kvx=kvx, lse_x=lse_x, ox=ox, x2=x2)
    h3 = _rms_fwd(x2, p['g_mlp'], name=f"rms_mlp_{l}")
    a = mm('mlp1', h3, p['w_mlp1'], out_dtype=BF16)
    x3 = mm('mlp2', a, p['w_mlp2'], out_dtype=F32, act='relu2', residual=x2)
    sv.update(h3=h3, a=a)
    return x3, sv


def _layer_bwd(dx3, dx3b, mem, p, rope, sv, l, hooks, half_done):
    S = dx3.shape[0]
    g = {}

    def dw(key, a, b, **kw):
        return _mm(a, b, mode='tn', out_dtype=BF16, col_shards=b.shape[1] // LANES, name=f"d_{key}_{l}", **kw)

    da = _mm(dx3b, p['w_mlp2'], mode='nt', out_dtype=BF16, drelu_of=sv['a'], name=f"d_mlp2_in_{l}")
    g['w_mlp2'] = dw('w_mlp2', sv['a'], dx3b, act='relu2')
    dx2, dx2b, g['g_mlp'] = _mm(da, p['w_mlp1'], mode='nt', out_dtype=F32, norm_bwd=(sv['x2'], p['g_mlp'], dx3), tm=512,
                                name=f"d_mlp1_in_{l}")
    g['w_mlp1'] = dw('w_mlp1', sv['h3'], da)
    dox = _mm(dx2b, p['w_xo'], mode='nt', out_dtype=BF16, name=f"d_xo_in_{l}")
    g['w_xo'] = dw('w_xo', sv['ox'], dx2b)
    (dqx,), (dkx,), dvx = _carried(hooks, (l, 'xa_bwd'), lambda rider: _attn_bwd(
        *_xa_qkv(sv['qx'], sv['kvx']), XA_HEADS, sv['ox'], dox, sv['lse_x'], None, None,
        scale=XA_HD ** -0.5, mask=None, name=f"xa_bwd_{l}", rider=rider))
    dqx = dqx.astype(BF16)
    dkvx = jnp.concatenate([dkx, dvx], axis=1).astype(BF16)
    dx1, dx1b, g['g_xa'] = _mm(dqx, p['w_xq'], mode='nt', out_dtype=F32, norm_bwd=(sv['x1'], p['g_xa'], dx2), tm=512,
                               name=f"d_xq_in_{l}")
    g['w_xq'] = dw('w_xq', sv['h2'], dqx)
    dhm = _mm(dkvx, p['w_xkv'], mode='nt', out_dtype=F32, name=f"d_xkv_in_{l}")
    g['w_xkv'] = dw('w_xkv', sv['hm'], dkvx)
    _, _, g['g_mem'] = _rms_bwd(mem, p['g_mem'], dhm, None, name=f"d_rms_mem_{l}")
    dy = _mm(dx1b, p['w_out'], mode='nt', out_dtype=F32, name=f"d_out_in_{l}")
    g['w_out'] = dw('w_out', sv['y'], dx1b)
    z, zsm = sv['z'], sv['zsm']
    (dz, do_fox, do_gla, do_mla), (db0, db1, db2, g['w_up_fox'], g['w_up_gla'], g['w_up_mla']) = _stage_bwd(
        _merge_fn, _merge_rows(z, sv['o_fox'], sv['o_gla'], sv['o_mla']), _merge_consts(p), [dy], 6, [BF16] * 6,
        lead=(3, N_PAD), name=f"merge_bwd_{l}")
    g['b_branch_gate'] = jnp.concatenate([db0, db1, db2], axis=1).reshape(-1)
    half_done(l, g)
    (dfq,), (dfk,), dfv, dck, dcq = _carried(hooks, (l, 'fox_bwd'), lambda rider: _attn_bwd(
        *_fox_qkv(z), FOX_HEADS, sv['o_fox'], do_fox, sv['lse_f'], sv['cum_t'], sv['cum'],
        scale=FOX_HD ** -0.5, mask='causal', name=f"fox_bwd_{l}", rider=rider))
    dff_t, db_f = _fox_cum_bwd(sv['ff_t'], p['b_f'], dcq + dck.T, name=f"fox_cum_bwd_{l}")
    g['b_fox_forget'] = db_f[:FOX_HEADS, 0]
    dgq, dgk, dgv, dgr, dzsm, dwg, dbg, dgo = _gla_bwd(z, zsm, p['wg'], p['bg'], p['go'], sv['states'], do_gla, _GLA_COLS,
                                                       name=f"gla_bwd_{l}")
    g['w_gla_gate'] = dwg[SM_GLOW:SM_GLOW + GLA_RANK]
    g['b_gla_gate'] = dbg.reshape(-1)
    g['g_gla_out'] = dgo.reshape(-1)
    (dmqn, dmqr), (dmkn, dmkr), dmv = _carried(hooks, (l, 'mla_bwd'), lambda rider: _attn_bwd(
        *_mla_qkv(*sv['mla']), MLA_HEADS, sv['o_mla'], do_mla, sv['lse_m'], None, None,
        scale=_QK_HD ** -0.5, mask='chunk', name=f"mla_bwd_{l}", rider=rider))
    (dcq, dckv, dkr, dkr_sw), (dgq_n, dgkv_n, dwq_n, dwq_r, dwq_sw, dwk, dwv) = _stage_bwd(
        _mla_prep_fn, _mla_rows(z, zsm, rope), _mla_consts(p), [dmqn, dmqr, dmkn, dmv, dmkr], 4, [BF16] * 4,
        name=f"mla_prep_bwd_{l}")
    g['g_mla_q'] = dgq_n.reshape(-1)
    g['g_mla_kv'] = dgkv_n.reshape(-1)
    g['w_mla_uq'] = _uq_grad(dwq_n, dwq_r, dwq_sw)
    g['w_mla_ukv'] = _take(jnp.concatenate([dwk, dwv], axis=1), _UKV_INV)
    dsm = dzsm + jnp.pad(dff_t[:FOX_HEADS].T, [(0, 0), (0, 128 - FOX_HEADS)])
    dz = lax.dynamic_update_slice(dz, jnp.concatenate(
        [dgv, dgr, dfq.astype(BF16), dfk.astype(BF16), dfv.astype(BF16), dgq, dgk, dcq, dckv, dsm.astype(BF16), dkr, dkr_sw],
        axis=1), (0, Z_GV))
    dx0, dx0b, g['g_mix'] = _mm(dz, p['w_in'], mode='nt', out_dtype=F32, norm_bwd=(sv['x0'], p['g_mix'], dx1), tm=512,
                                tk=N_PAD // 2, name=f"d_in_{l}")
    g['w_in'] = dw('w_in', sv['h1'], dz, tn=N_PAD // 3)
    for n in ('g_mlp', 'g_mem', 'g_xa', 'g_mix'):
        g[n] = g[n].reshape(-1)
    return dx0, dx0b, g


def _local_step(x, mem, target, ps, g_final, hooks, half_done, layer_done):
    rope = _rope_tables(x.shape[0])
    saved = []
    for l, p in enumerate(ps):
        x, sv = _layer_fwd(x, mem, p, rope, l, hooks)
        saved.append(sv)
    loss, dx, dxb, dgf = _loss_head(x, g_final, target, name="loss_head")
    for l in reversed(range(len(ps))):
        dx, dxb, grads = _layer_bwd(dx, dxb, mem, ps[l], rope, saved[l], l, hooks, half_done)
        layer_done(l, grads)
    assert not hooks, f"exchanges without a carrier: {list(hooks)}"
    return loss, dx, dgf.reshape(-1)


_MESH_AXES = ("x", "y", "c")
_HBM = pl.BlockSpec(memory_space=pl.ANY)


N_CHIP = 4
_SLOT_ROWS = (2048, 1024, 512, 256, 128, 64, 32, 16, 8)


def _place():
    x, y, c = (lax.axis_index(n) for n in _MESH_AXES)
    return (x, y, c), (x, y, 1 - c), [(1 - x, y), (x, 1 - y), (1 - x, 1 - y)]


def _remote(src, dst, sems, k, to):
    return pltpu.make_async_remote_copy(src_ref=src, dst_ref=dst, send_sem=sems[0].at[k], recv_sem=sems[1].at[k],
                                        device_id=to, device_id_type=pl.DeviceIdType.MESH)


def _all_gather(x, *, name):
    def body(x_ref, o_ref, send_sems, recv_sems, local_sem):
        me, sib, chips = _place()
        c = me[2]
        sems = (send_sems, recv_sems)
        slot = lambda px, py, pc: o_ref.at[4 * px + 2 * py + pc]
        mine = pltpu.make_async_copy(x_ref, slot(*me), local_sem)
        mine.start()
        first = [_remote(x_ref, slot(*me), sems, 0, sib)]
        first += [_remote(x_ref, slot(*me), sems, 1 + j, (*chip, c)) for j, chip in enumerate(chips)]
        for cp in first:
            cp.start()
        passed = [_remote(slot(*chip, c), slot(*chip, c), sems, 4 + j, sib) for j, chip in enumerate(chips)]
        for j, chip in enumerate(chips):
            _remote(x_ref, slot(*chip, c), sems, 1 + j, me).wait_recv()
            passed[j].start()
        _remote(x_ref, slot(*sib), sems, 0, me).wait_recv()
        for j, chip in enumerate(chips):
            _remote(x_ref, slot(*chip, 1 - c), sems, 4 + j, me).wait_recv()
        for cp in first + passed:
            cp.wait_send()
        mine.wait()

    return pl.pallas_call(
        body, name=name, out_shape=jax.ShapeDtypeStruct((N_DEV,) + x.shape, x.dtype),
        in_specs=[_HBM], out_specs=_HBM,
        scratch_shapes=[pltpu.SemaphoreType.DMA((N_DEV - 1,)), pltpu.SemaphoreType.DMA((N_DEV - 1,)), pltpu.SemaphoreType.DMA],
        compiler_params=pltpu.CompilerParams(has_side_effects=True),
    )(x)


class _Rider:
    def __init__(self, inputs, out_shapes, scratch, start, finish, post):
        self.inputs, self.out_shapes, self.scratch = list(inputs), list(out_shapes), list(scratch)
        self.start, self.finish, self.post = start, finish, post


def _run_rider(rider, *, name):
    def body(*refs):
        rider.start(refs)
        rider.finish(refs)

    outs = pl.pallas_call(
        body, name=name, out_shape=tuple(rider.out_shapes), in_specs=[_HBM] * len(rider.inputs),
        out_specs=(_HBM,) * len(rider.out_shapes), scratch_shapes=rider.scratch,
        compiler_params=pltpu.CompilerParams(has_side_effects=True),
    )(*rider.inputs)
    return rider.post(outs)


def _carry(rider, n_in, n_out, first, last):
    if rider is None:
        return [], [], [], [], [], lambda refs: refs
    ni, no = len(rider.inputs), len(rider.out_shapes)

    def split(refs):
        own_in, r_in = refs[:n_in], refs[n_in:n_in + ni]
        own_out, r_out = refs[n_in + ni:n_in + ni + n_out], refs[n_in + ni + n_out:n_in + ni + n_out + no]
        rest = refs[n_in + ni + n_out + no:]
        own_scr, r_scr = rest[:len(rest) - len(rider.scratch)], rest[len(rest) - len(rider.scratch):]
        rrefs = tuple(r_in) + tuple(r_out) + tuple(r_scr)
        pl.when(first())(lambda: rider.start(rrefs))
        pl.when(last())(lambda: rider.finish(rrefs))
        return tuple(own_in) + tuple(own_out) + tuple(own_scr)

    return list(rider.inputs), [_HBM] * ni, list(rider.out_shapes), [_HBM] * no, list(rider.scratch), split


def _gather_rider(shards, axes):
    n = len(shards)
    srcs, out_shapes, kinds = [], [], []
    for s, ax in zip(shards, axes):
        L, a, b = s.shape
        if ax == 1:
            srcs.append(s.reshape(L, 1, a, b)), out_shapes.append((L, N_DEV, a, b)), kinds.append('row')
        elif b % 128 == 0:
            srcs.append(s), out_shapes.append((L, a, N_DEV * b)), kinds.append('col')
        else:
            srcs.append(s.reshape(1, L, a, b)), out_shapes.append((N_DEV, L, a, b)), kinds.append('slot')

    def parts(refs):
        x_refs, o_refs = refs[:n], refs[n:2 * n]
        send_sems, recv_sems, local_sem = refs[2 * n:]
        me, sib, chips = _place()
        sems = (send_sems, recv_sems)

        def win(t, px, py, pc):
            idx = 4 * px + 2 * py + pc
            if kinds[t] == 'row':
                return o_refs[t].at[:, pl.ds(idx, 1)]
            if kinds[t] == 'col':
                width = shards[t].shape[2]
                return o_refs[t].at[:, :, pl.ds(pl.multiple_of(idx * width, 128), width)]
            return o_refs[t].at[pl.ds(idx, 1)]

        def group(k, block, to, own):
            return [_remote(x_refs[t] if own else win(t, *block), win(t, *block), sems, k * n + t, to) for t in range(n)]

        mine = [pltpu.make_async_copy(x_refs[t], win(t, *me), local_sem.at[t]) for t in range(n)]
        first = group(0, me, sib, True)
        for j, chip in enumerate(chips):
            first += group(1 + j, me, (*chip, me[2]), True)
        return me, sib, chips, group, mine, first

    def start(refs):
        *_, mine, first = parts(refs)
        for cp in mine + first:
            cp.start()

    def finish(refs):
        me, sib, chips, group, mine, first = parts(refs)
        c = me[2]
        passed = []
        for j, chip in enumerate(chips):
            for cp in group(1 + j, (*chip, c), me, False):
                cp.wait_recv()
            fwd = group(4 + j, (*chip, c), sib, False)
            for cp in fwd:
                cp.start()
            passed += fwd
        for cp in group(0, sib, me, False):
            cp.wait_recv()
        for j, chip in enumerate(chips):
            for cp in group(4 + j, (*chip, 1 - c), me, False):
                cp.wait_recv()
        for cp in first + passed:
            cp.wait_send()
        for cp in mine:
            cp.wait()

    def post(outs):
        whole = []
        for o, s, kind in zip(outs, shards, kinds):
            L, a, b = s.shape
            whole.append(o.reshape(L, N_DEV * a, b) if kind == 'row' else o if kind == 'col' else _to_whole(o, 2))
        return whole

    return _Rider(srcs, [jax.ShapeDtypeStruct(shp, s.dtype) for shp, s in zip(out_shapes, shards)],
                  [pltpu.SemaphoreType.DMA(((N_DEV - 1) * n,)), pltpu.SemaphoreType.DMA(((N_DEV - 1) * n,)),
                   pltpu.SemaphoreType.DMA((n,))], start, finish, post)


def _sibling_swap_rider(x):
    def sends(refs):
        x_ref, o_ref, send_sems, recv_sems = refs
        me, sib, _ = _place()
        return [_remote(x_ref.at[j, 1 - me[2]], o_ref.at[j], (send_sems, recv_sems), j, sib) for j in range(N_CHIP)]

    def start(refs):
        for cp in sends(refs):
            cp.start()

    def finish(refs):
        for cp in sends(refs):
            cp.wait_send()
            cp.wait_recv()

    return _Rider([x], [jax.ShapeDtypeStruct((N_CHIP,) + x.shape[2:], x.dtype)],
                  [pltpu.SemaphoreType.DMA((N_CHIP,)), pltpu.SemaphoreType.DMA((N_CHIP,))], start, finish, lambda outs: outs[0])


def _join_riders(riders):
    counts = [(len(r.inputs), len(r.out_shapes), len(r.scratch)) for r in riders]
    n_in, n_out = sum(c[0] for c in counts), sum(c[1] for c in counts)

    def refs_of(refs, k):
        a = sum(c[0] for c in counts[:k])
        b = n_in + sum(c[1] for c in counts[:k])
        s = n_in + n_out + sum(c[2] for c in counts[:k])
        return tuple(refs[a:a + counts[k][0]]) + tuple(refs[b:b + counts[k][1]]) + tuple(refs[s:s + counts[k][2]])

    def each(method):
        def run(refs):
            for k, r in enumerate(riders):
                getattr(r, method)(refs_of(refs, k))
        return run

    def post(outs):
        got, at = [], 0
        for r, c in zip(riders, counts):
            got.append(r.post(outs[at:at + c[1]]))
            at += c[1]
        return got

    return _Rider([x for r in riders for x in r.inputs], [o for r in riders for o in r.out_shapes],
                  [s for r in riders for s in r.scratch], each('start'), each('finish'), post)


def _pair_sum(x, got, c, *, name):
    _, _, R, _ = x.shape
    tr = _pick(R, _SLOT_ROWS)

    def body(c_ref, x_ref, g_ref, o_ref):
        o_ref[...] = (x_ref[...].astype(F32) + g_ref[...].astype(F32)).astype(o_ref.dtype)

    return pl.pallas_call(
        body, name=name, out_shape=jax.ShapeDtypeStruct((N_CHIP, R, 128), x.dtype),
        grid_spec=pltpu.PrefetchScalarGridSpec(
            num_scalar_prefetch=1, grid=(R // tr,),
            in_specs=[pl.BlockSpec((N_CHIP, None, tr, 128), lambda i, c_ref: (0, c_ref[0], i, 0)),
                      pl.BlockSpec((N_CHIP, tr, 128), lambda i, c_ref: (0, i, 0))],
            out_specs=pl.BlockSpec((N_CHIP, tr, 128), lambda i, c_ref: (0, i, 0))),
        compiler_params=_params(("parallel",)),
    )(c, x, got)


def _chip_all_to_all_rider(x):
    def parts(refs):
        x_ref, o_ref, send_sems, recv_sems, local_sem = refs
        me, _, chips = _place()
        sems = (send_sems, recv_sems)
        mine = 2 * me[0] + me[1]
        local = pltpu.make_async_copy(x_ref.at[mine], o_ref.at[mine], local_sem)
        sends = [_remote(x_ref.at[2 * px + py], o_ref.at[mine], sems, j, (px, py, me[2])) for j, (px, py) in enumerate(chips)]
        arrival = lambda j: _remote(x_ref.at[mine], o_ref.at[2 * chips[j][0] + chips[j][1]], sems, j, me)
        return local, sends, arrival

    def start(refs):
        local, sends, _ = parts(refs)
        for cp in [local] + sends:
            cp.start()

    def finish(refs):
        local, sends, arrival = parts(refs)
        for j, cp in enumerate(sends):
            cp.wait_send()
            arrival(j).wait_recv()
        local.wait()

    return _Rider([x], [jax.ShapeDtypeStruct(x.shape, x.dtype)],
                  [pltpu.SemaphoreType.DMA((N_CHIP - 1,)), pltpu.SemaphoreType.DMA((N_CHIP - 1,)), pltpu.SemaphoreType.DMA],
                  start, finish, lambda outs: outs[0])


def _sum_slots(x, *, name):
    n, R, _ = x.shape
    tr = _pick(R, _SLOT_ROWS)

    def body(x_ref, o_ref):
        acc = x_ref[0].astype(F32)
        for j in range(1, n):
            acc = acc + x_ref[j].astype(F32)
        o_ref[...] = acc

    return pl.pallas_call(
        body, name=name, out_shape=jax.ShapeDtypeStruct((R, 128), F32), grid=(R // tr,),
        in_specs=[pl.BlockSpec((n, tr, 128), lambda i: (0, i, 0))], out_specs=pl.BlockSpec((tr, 128), lambda i: (i, 0)),
        compiler_params=_params(("parallel",)),
    )(x)


def _adamw(w, g, m, v, *, name):
    shape = w.shape
    cols = shape[-1]
    rows = int(np.prod(shape[:-1]))
    tr = next((t for t in (1024, 512, 256, 128, 64, 32, 16, 8) if rows % t == 0 and t * cols * 4 <= (1 << 20)), rows)

    def body(w_ref, g_ref, m_ref, v_ref, d_ref, mo_ref, vo_ref):
        g_ = g_ref[...]
        m_ = ADAM_B1 * m_ref[...] + (1.0 - ADAM_B1) * g_
        v_ = ADAM_B2 * v_ref[...] + (1.0 - ADAM_B2) * jnp.square(g_)
        m_hat = m_ / (1.0 - ADAM_B1 ** ADAM_STEP)
        v_hat = v_ / (1.0 - ADAM_B2 ** ADAM_STEP)
        d_ref[...] = -ADAM_LR * (m_hat / (jnp.sqrt(v_hat) + ADAM_EPS) + ADAM_WD * w_ref[...])
        mo_ref[...] = m_
        vo_ref[...] = v_

    blk = pl.BlockSpec((tr, cols), lambda i: (i, 0))
    outs = pl.pallas_call(
        body, name=name, out_shape=tuple(jax.ShapeDtypeStruct((rows, cols), F32) for _ in range(3)), grid=(rows // tr,),
        in_specs=[blk] * 4, out_specs=(blk,) * 3, compiler_params=_params(("parallel",)),
    )(*(a.reshape(rows, cols) for a in (w, g, m, v)))
    return tuple(o.reshape(shape) for o in outs)


_WEIGHTS = ('g_mix', 'w_in', 'b_fox_forget', 'w_gla_gate', 'b_gla_gate', 'g_gla_out', 'g_mla_q', 'w_mla_uq', 'g_mla_kv',
            'w_mla_ukv', 'b_branch_gate', 'w_up_fox', 'w_up_gla', 'w_up_mla', 'w_out', 'g_xa', 'g_mem', 'w_xq', 'w_xkv',
            'w_xo', 'g_mlp', 'w_mlp1', 'w_mlp2', 'g_final')
_SHARDED = (('w_in', 1), ('w_gla_gate', 2), ('w_mla_uq', 2), ('w_mla_ukv', 2), ('w_up_fox', 2), ('w_up_gla', 2),
            ('w_up_mla', 2), ('w_out', 1), ('w_xq', 1), ('w_xkv', 1), ('w_xo', 2), ('w_mlp1', 2), ('w_mlp2', 1))
_REPLICATED = tuple(n for n in _WEIGHTS if n not in dict(_SHARDED))
_ROW_PAD = 1024
_SMALL_ROW_PAD = 8
_PIECE_ROWS = 16


def _pack(flats, lead, row_pad=_ROW_PAD):
    if all(int(np.prod(a.shape[lead:])) % 128 == 0 for a in flats):
        def block(a):
            a = a.reshape(a.shape[:lead] + (-1, 128))
            return jnp.pad(a, [(0, 0)] * lead + [(0, -a.shape[lead] % _PIECE_ROWS), (0, 0)])
        cat = jnp.concatenate([block(a) for a in flats], axis=lead)
        rows = cat.shape[lead]
        return jnp.pad(cat, [(0, 0)] * lead + [(0, -(-rows // row_pad) * row_pad - rows), (0, 0)])
    cat = jnp.concatenate([a.reshape(a.shape[:lead] + (-1,)) for a in flats], axis=-1)
    n = cat.shape[-1]
    total = -(-n // (128 * row_pad)) * (128 * row_pad)
    cat = jnp.pad(cat, [(0, 0)] * lead + [(0, total - n)])
    return cat.reshape(cat.shape[:lead] + (total // 128, 128))


def _unpack(buf, shapes, lead):
    sizes = [int(np.prod(shp)) for shp in shapes]
    out, off = [], 0
    if all(n % 128 == 0 for n in sizes):
        for shp, n in zip(shapes, sizes):
            rows = buf[(slice(None),) * lead + (slice(off, off + n // 128),)]
            out.append(rows.reshape(buf.shape[:lead] + tuple(shp)))
            off += -(-(n // 128) // _PIECE_ROWS) * _PIECE_ROWS
        return out
    flat = buf.reshape(buf.shape[:lead] + (-1,))
    for shp, n in zip(shapes, sizes):
        out.append(flat[..., off:off + n].reshape(buf.shape[:lead] + tuple(shp)))
        off += n
    return out


def _to_whole(g, axis):
    if axis == 1:
        return g.transpose(1, 0, 2, 3).reshape(g.shape[1], N_DEV * g.shape[2], g.shape[3])
    return g.transpose(1, 2, 0, 3).reshape(g.shape[1], g.shape[2], N_DEV * g.shape[3])


def _to_shards(w, axis):
    L, R, C = w.shape
    if axis == 1:
        return w.reshape(L, N_DEV, R // N_DEV, C).transpose(1, 0, 2, 3)
    return w.reshape(L, R, N_DEV, C // N_DEV).transpose(2, 0, 1, 3)


def kernel(x, mem, g_mix, w_in, b_fox_forget, w_gla_gate, b_gla_gate, g_gla_out, g_mla_q, w_mla_uq, g_mla_kv, w_mla_ukv, b_branch_gate, w_up_fox, w_up_gla, w_up_mla, w_out, g_xa, g_mem, w_xq, w_xkv, w_xo, g_mlp, w_mlp1, w_mlp2, g_final, loss_target, m_g_mix, m_w_in, m_b_fox_forget, m_w_gla_gate, m_b_gla_gate, m_g_gla_out, m_g_mla_q, m_w_mla_uq, m_g_mla_kv, m_w_mla_ukv, m_b_branch_gate, m_w_up_fox, m_w_up_gla, m_w_up_mla, m_w_out, m_g_xa, m_g_mem, m_w_xq, m_w_xkv, m_w_xo, m_g_mlp, m_w_mlp1, m_w_mlp2, m_g_final, v_g_mix, v_w_in, v_b_fox_forget, v_w_gla_gate, v_b_gla_gate, v_g_gla_out, v_g_mla_q, v_w_mla_uq, v_g_mla_kv, v_w_mla_ukv, v_b_branch_gate, v_w_up_fox, v_w_up_gla, v_w_up_mla, v_w_out, v_g_xa, v_g_mem, v_w_xq, v_w_xkv, v_w_xo, v_g_mlp, v_w_mlp1, v_w_mlp2, v_g_final):
    wts = dict(zip(_WEIGHTS, (g_mix, w_in, b_fox_forget, w_gla_gate, b_gla_gate, g_gla_out, g_mla_q, w_mla_uq, g_mla_kv,
                              w_mla_ukv, b_branch_gate, w_up_fox, w_up_gla, w_up_mla, w_out, g_xa, g_mem, w_xq, w_xkv, w_xo,
                              g_mlp, w_mlp1, w_mlp2, g_final)))
    mom1 = dict(zip(_WEIGHTS, (m_g_mix, m_w_in, m_b_fox_forget, m_w_gla_gate, m_b_gla_gate, m_g_gla_out, m_g_mla_q,
                               m_w_mla_uq, m_g_mla_kv, m_w_mla_ukv, m_b_branch_gate, m_w_up_fox, m_w_up_gla, m_w_up_mla,
                               m_w_out, m_g_xa, m_g_mem, m_w_xq, m_w_xkv, m_w_xo, m_g_mlp, m_w_mlp1, m_w_mlp2, m_g_final)))
    mom2 = dict(zip(_WEIGHTS, (v_g_mix, v_w_in, v_b_fox_forget, v_w_gla_gate, v_b_gla_gate, v_g_gla_out, v_g_mla_q,
                               v_w_mla_uq, v_g_mla_kv, v_w_mla_ukv, v_b_branch_gate, v_w_up_fox, v_w_up_gla, v_w_up_mla,
                               v_w_out, v_g_xa, v_g_mem, v_w_xq, v_w_xkv, v_w_xo, v_g_mlp, v_w_mlp1, v_w_mlp2, v_g_final)))
    depth = g_mix.shape[0]

    names = [n for n, _ in _SHARDED]
    axes = dict(_SHARDED)
    shard = {n: wts[n] for n in names}
    shard['w_in'] = _pad_w_in(w_in)
    rep = {n: wts[n] for n in _REPLICATED}
    ps = [_LayerParams(rep, l) for l in range(depth)]

    def gather(group, l):
        rider = _gather_rider([shard[n][l:l + 1].astype(BF16) for n in group], [axes[n] for n in group])
        return rider, lambda whole: ps[l].w.update({n: w[0] for n, w in zip(group, whole)})

    first, sink = gather(['w_in'], 0)
    sink(_run_rider(first, name="gather_w_in_0"))
    narrow = ['w_gla_gate', 'w_mla_uq', 'w_mla_ukv', 'w_up_fox', 'w_up_gla', 'w_up_mla']
    hooks = {(0, 'in_big'): [gather(narrow + ['w_out', 'w_xq', 'w_xkv', 'w_xo'], 0)],
             (0, 'fox_fwd'): [gather(['w_mlp1', 'w_mlp2'], 0)]}
    ahead = (('mla_fwd', ['w_in'] + narrow + ['w_out']), ('xo', ['w_xkv']), ('mlp1', ['w_mlp1']),
             ('mlp2', ['w_mlp2', 'w_xq', 'w_xo']))
    assert sorted(n for _, group in ahead for n in group) == sorted(names)
    for l in range(1, depth):
        for key, group in ahead:
            hooks.setdefault((l - 1, key), []).append(gather(group, l))

    core = lax.axis_index("c").astype(jnp.int32).reshape(1)
    late = ['w_in', 'w_gla_gate', 'w_mla_uq', 'w_mla_ukv']
    groups = {'early': [n for n in names if n not in late], 'late': late}
    small_grads, landed = {}, {}

    def to_slots(gl, axis):
        if gl.ndim != 3:
            return _to_shards(gl[None], axis)
        blocks, rows, _ = gl.shape
        if axis == 1:
            return gl.reshape(blocks, N_DEV, rows // N_DEV, LANES).transpose(1, 0, 2, 3)
        return gl.reshape(N_DEV, blocks // N_DEV, rows, LANES)

    def slot_shape(n):
        _, a, b = shard[n].shape
        return (b // LANES, a, LANES) if n in blocked else (1, a, b)

    def from_slot(n, x):
        return x.transpose(1, 0, 2).reshape((1,) + shard[n].shape[1:]) if n in blocked else x

    blocked = {'w_in', 'w_out', 'w_xq', 'w_xkv', 'w_xo', 'w_mlp1', 'w_mlp2'}

    def ride(key, rider, sink, name):
        if key is None:
            sink(_run_rider(rider, name=name))
        else:
            hooks.setdefault(key, []).append((rider, sink))

    def exchange(l, g, which, swap_in, scatter_in):
        assert all((g[n].ndim == 3) == (n in blocked) for n in groups[which])
        slots = _pack([to_slots(g[n], axes[n]).astype(BF16) for n in groups[which]], 1)
        slots = slots.reshape((N_CHIP, 2) + slots.shape[1:])

        def swapped(got):
            paired = _pair_sum(slots, got, core, name=f"pair_grads_{which}_{l}")
            ride(scatter_in, _chip_all_to_all_rider(paired), lambda landing: landed.update({(l, which): landing}),
                 f"scatter_grads_{which}_{l}")

        ride(swap_in, _sibling_swap_rider(slots), swapped, f"swap_grads_{which}_{l}")

    def half_done(l, g):
        exchange(l, g, 'early', (l, 'fox_bwd'), (l, 'mla_bwd'))

    def layer_done(l, g):
        small_grads[l] = g
        if l > 0:
            exchange(l, g, 'late', (l - 1, 'xa_bwd'), (l - 1, 'fox_bwd'))
        else:
            exchange(l, g, 'late', None, None)

    loss, dx, dg_final = _local_step(x[0], mem[0], loss_target[0], ps, g_final, hooks, half_done, layer_done)
    loss = lax.psum(loss[0, 0], _MESH_AXES)

    grad = {}
    for which, group in groups.items():
        shapes = [slot_shape(n) for n in group]
        per_layer = [_unpack(_sum_slots(landed[(l, which)], name=f"sum_grads_{which}_{l}"), shapes, 0) for l in range(depth)]
        grad.update({n: jnp.concatenate([from_slot(n, per_layer[l][i]) for l in range(depth)], axis=0)
                     for i, n in enumerate(group)})
    grad['w_in'] = _unpad_w_in(grad['w_in'])
    grads = small_grads
    small = [dg_final if n == 'g_final' else jnp.stack([grads[l][n] for l in range(depth)]) for n in _REPLICATED]
    small_shapes = [wts[n].shape for n in _REPLICATED]
    small_sum = _sum_slots(_all_gather(_pack(small, 0, _SMALL_ROW_PAD), name="gather_small_grads"), name="sum_small_grads")
    grad.update(dict(zip(_REPLICATED, _unpack(small_sum, small_shapes, 0))))

    delta, new_m, new_v = {}, {}, {}
    for n, _ in _SHARDED:
        delta[n], new_m[n], new_v[n] = _adamw(wts[n], grad[n], mom1[n], mom2[n], name=f"adamw_{n}")
    packed = [_pack([d[n] for n in _REPLICATED], 0, _SMALL_ROW_PAD) for d in (wts, mom1, mom2)]
    outs = _adamw(packed[0], small_sum, packed[1], packed[2], name="adamw_small")
    for d, o in zip((delta, new_m, new_v), outs):
        d.update(dict(zip(_REPLICATED, _unpack(o, small_shapes, 0))))

    return (loss, dx[None], *[grad[n] for n in _WEIGHTS], *[delta[n] for n in _WEIGHTS],
            *[new_m[n] for n in _WEIGHTS], *[new_v[n] for n in _WEIGHTS])
```

```python
import functools

import jax
import jax.numpy as jnp
import numpy as np
from jax import lax
from jax.experimental import pallas as pl
from jax.experimental.pallas import tpu as pltpu

F32 = jnp.float32
BF16 = jnp.bfloat16

EPS = 1e-6
CHUNK = 64
FOX_HEADS, FOX_HD = 4, 64
GLA_HEADS, GLA_DK, GLA_DV, GLA_RANK, GLA_TAU = 4, 64, 128, 16, 16.0
MLA_HEADS, MLA_Q_RANK, MLA_KV_RANK, MLA_NOPE, MLA_ROPE, MLA_VD = 4, 256, 128, 64, 32, 64
ROPE_BASE = 10000.0
XA_HEADS, XA_HD = 4, 128
ADAM_LR, ADAM_B1, ADAM_B2, ADAM_EPS, ADAM_WD, ADAM_STEP = 0.001, 0.9, 0.999, 1e-08, 0.01, 10

N_DEV = 8
V7X_VMEM_LIMIT = 56 * 1024 * 1024
NEG = -1e30

NN = ((1,), (0,))
NT = ((1,), (1,))
TN = ((0,), (0,))


def _dot(a, b, dims):
    return lax.dot_general(a.astype(BF16), b.astype(BF16), (dims, ((), ())), preferred_element_type=F32)


@jax.custom_vjp
def bdot(a, b):
    return _dot(a, b, NN)


bdot.defvjp(lambda a, b: (_dot(a, b, NN), (a, b)),
            lambda res, g: (_dot(g, res[1], NT), _dot(res[0], g, TN)))


@jax.custom_vjp
def bdot_nt(a, b):
    return _dot(a, b, NT)


bdot_nt.defvjp(lambda a, b: (_dot(a, b, NT), (a, b)),
               lambda res, g: (_dot(g, res[1], NN), _dot(g, res[0], TN)))


@jax.custom_vjp
def bdot_tn(a, b):
    return _dot(a, b, TN)


bdot_tn.defvjp(lambda a, b: (_dot(a, b, TN), (a, b)),
               lambda res, g: (_dot(res[1], g, NT), _dot(res[0], g, NN)))


def _split2(x):
    hi = x.astype(BF16)
    lo = (x - hi.astype(F32)).astype(BF16)
    return hi, lo


def _tri(n, lower):
    r = lax.broadcasted_iota(jnp.int32, (n, n), 0)
    c = lax.broadcasted_iota(jnp.int32, (n, n), 1)
    return jnp.where((r >= c) if lower else (r <= c), 1.0, 0.0).astype(BF16)


def _log_sigmoid(x):
    return jnp.minimum(x, 0.0) - jnp.log(1.0 + jnp.exp(-jnp.abs(x)))


def _sigmoid(x):
    return 1.0 / (1.0 + jnp.exp(-x))


def _rms(x, g):
    return x * lax.rsqrt(jnp.mean(x * x, axis=-1, keepdims=True) + EPS) * g


def _pick(dim, prefs):
    for p in prefs:
        if dim % p == 0:
            return p
    return dim


def _params(sem):
    return pltpu.CompilerParams(dimension_semantics=sem, vmem_limit_bytes=V7X_VMEM_LIMIT)


def _rms_vjp(x, g, dy, dres):
    rstd = lax.rsqrt(jnp.mean(x * x, axis=-1, keepdims=True) + EPS)
    xh = x * rstd
    gdy = dy * g
    dx = (gdy - xh * jnp.mean(gdy * xh, axis=-1, keepdims=True)) * rstd
    return (dx if dres is None else dx + dres), jnp.sum(dy * xh, axis=0, keepdims=True)


def _mm(a, b, *, mode, out_dtype, name, act=None, residual=None, drelu_of=None, norm_bwd=None, b_cols=None,
        col_shards=None, rider=None, tm=None, tn=None, tk=None):
    b_off, b_width = b_cols or (0, b.shape[1])
    if mode == 'nn':
        (M, K), N = a.shape, b_width
    elif mode == 'nt':
        (M, K), N = a.shape, b.shape[0]
    else:
        (K, M), N = a.shape, b_width
    tm = tm or _pick(M, (1024, 512, 256, 128))
    tn = tn or _pick(N, (1024, 1920, 1152, 768, 640, 512, 384, 256, 128))
    tk = tk or _pick(K, (1024, 1920, 1152, 640, 512, 256, 128))
    nk = K // tk
    dims = {'nn': NN, 'nt': NT, 'tn': TN}[mode]
    a_spec = pl.BlockSpec((tk, tm), lambda i, j, k: (k, i)) if mode == 'tn' else pl.BlockSpec((tm, tk), lambda i, j, k: (i, k))
    if mode == 'nt':
        b_spec = pl.BlockSpec((tn, tk), lambda i, j, k, o=b_off // tk: (j, k + o))
    else:
        b_spec = pl.BlockSpec((tk, tn), lambda i, j, k, o=b_off // tn: (k, j + o))
    o_spec = pl.BlockSpec((tm, tn), lambda i, j, k: (i, j))
    extra = [e for e in (residual, drelu_of) if e is not None]
    extra_specs = [o_spec] * len(extra)
    out_shape, out_specs, n_out = jax.ShapeDtypeStruct((M, N), out_dtype), o_spec, 1
    if col_shards:
        n_sh = N // col_shards
        assert tn % n_sh == 0 and not extra and norm_bwd is None
        out_shape = jax.ShapeDtypeStruct((col_shards, M, n_sh), out_dtype)
        out_specs = pl.BlockSpec((tn // n_sh, tm, n_sh), lambda i, j, k: (j, i, 0))
    if norm_bwd is not None:
        x_in, g_in, dres_in = norm_bwd
        assert tn == N and residual is None and drelu_of is None
        vec = pl.BlockSpec((1, N), lambda i, j, k: (0, 0))
        extra, extra_specs = [x_in, g_in.reshape(1, N), dres_in], [o_spec, vec, o_spec]
        out_shape = (jax.ShapeDtypeStruct((M, N), F32), jax.ShapeDtypeStruct((M, N), BF16), jax.ShapeDtypeStruct((1, N), F32))
        out_specs, n_out = (o_spec, o_spec, vec), 3

    grid = (M // tm, N // tn, nk)
    r_ins, r_in_specs, r_outs, r_out_specs, r_scratch, split = _carry(
        rider, 2 + len(extra), n_out, lambda: functools.reduce(jnp.logical_and, [pl.program_id(d) == 0 for d in range(3)]),
        lambda: functools.reduce(jnp.logical_and, [pl.program_id(d) == grid[d] - 1 for d in range(3)]))

    def body(*refs):
        a_ref, b_ref, *rest = split(refs)
        o_ref = rest[len(extra)]
        first_rows = pl.program_id(0) == 0
        at = a_ref[...]
        if act == 'relu2':
            at = jnp.square(jnp.maximum(at.astype(F32), 0.0))
        part = _dot(at, b_ref[...], dims)

        def finish(acc):
            if norm_bwd is not None:
                dx, dg = _rms_vjp(rest[0][...], rest[1][...], acc, rest[2][...])
                o_ref[...] = dx
                rest[len(extra) + 1][...] = dx.astype(BF16)
                dg_ref = rest[len(extra) + 2]

                @pl.when(first_rows)
                def _():
                    dg_ref[...] = dg

                @pl.when(jnp.logical_not(first_rows))
                def _():
                    dg_ref[...] += dg
                return
            idx = 0
            if residual is not None:
                acc = acc + rest[idx][...]
                idx += 1
            if drelu_of is not None:
                acc = acc * (2.0 * jnp.maximum(rest[idx][...].astype(F32), 0.0))
            if col_shards:
                for t in range(tn // n_sh):
                    o_ref[t] = acc[:, t * n_sh:(t + 1) * n_sh].astype(out_dtype)
            else:
                o_ref[...] = acc.astype(out_dtype)

        if nk == 1:
            finish(part)
        else:
            acc_ref = rest[len(extra) + n_out]
            k = pl.program_id(2)

            @pl.when(k == 0)
            def _():
                acc_ref[...] = part

            @pl.when(k > 0)
            def _():
                acc_ref[...] += part

            @pl.when(k == nk - 1)
            def _():
                finish(acc_ref[...])

    scratch = [] if nk == 1 else [pltpu.VMEM((tm, tn), F32)]
    if rider is not None:
        own_shapes, own_specs = (out_shape, out_specs) if n_out > 1 else ((out_shape,), (out_specs,))
        res = pl.pallas_call(
            body, name=name, out_shape=(*own_shapes, *r_outs), grid=grid, in_specs=[a_spec, b_spec] + extra_specs + r_in_specs,
            out_specs=(*own_specs, *r_out_specs), scratch_shapes=scratch + r_scratch,
            compiler_params=_params(("arbitrary", "arbitrary", "arbitrary")),
        )(a, b, *extra, *r_ins)
        return (*res[:n_out], rider.post(res[n_out:]))
    return pl.pallas_call(
        body, name=name, out_shape=out_shape, grid=grid, in_specs=[a_spec, b_spec] + extra_specs, out_specs=out_specs,
        scratch_shapes=scratch,
        compiler_params=_params(("arbitrary" if norm_bwd is not None else "parallel", "parallel", "arbitrary")),
    )(a, b, *extra)


def _rms_fwd(x, g, *, name, out_dtype=BF16):
    S, D = x.shape
    tr = _pick(S, (512, 256, 128))

    def body(x_ref, g_ref, o_ref):
        o_ref[...] = _rms(x_ref[...], g_ref[...]).astype(out_dtype)

    return pl.pallas_call(
        body, name=name, out_shape=jax.ShapeDtypeStruct((S, D), out_dtype), grid=(S // tr,),
        in_specs=[pl.BlockSpec((tr, D), lambda i: (i, 0)), pl.BlockSpec((1, D), lambda i: (0, 0))],
        out_specs=pl.BlockSpec((tr, D), lambda i: (i, 0)),
        compiler_params=_params(("parallel",)),
    )(x, g.reshape(1, D))


def _rms_bwd(x, g, dy, dres, *, name):
    S, D = x.shape
    tr = _pick(S, (512, 256, 128))

    def body(x_ref, g_ref, dy_ref, *rest):
        dx_ref, dxb_ref, dg_ref = rest[-3], rest[-2], rest[-1]
        dx, part = _rms_vjp(x_ref[...], g_ref[...], dy_ref[...].astype(F32), None if dres is None else rest[0][...])
        dx_ref[...] = dx
        dxb_ref[...] = dx.astype(BF16)

        @pl.when(pl.program_id(0) == 0)
        def _():
            dg_ref[...] = part

        @pl.when(pl.program_id(0) > 0)
        def _():
            dg_ref[...] += part

    row = pl.BlockSpec((tr, D), lambda i: (i, 0))
    vec = pl.BlockSpec((1, D), lambda i: (0, 0))
    ins = [x, g.reshape(1, D), dy] + ([dres] if dres is not None else [])
    return pl.pallas_call(
        body, name=name,
        out_shape=(jax.ShapeDtypeStruct((S, D), F32), jax.ShapeDtypeStruct((S, D), BF16), jax.ShapeDtypeStruct((1, D), F32)),
        grid=(S // tr,),
        in_specs=[row, vec, row] + ([row] if dres is not None else []),
        out_specs=(row, row, vec),
        compiler_params=_params(("arbitrary",)),
    )(*ins)


def _loss_head(x, g, target, *, name):
    S, D = x.shape
    tr = _pick(S, (512, 256, 128))

    def body(x_ref, g_ref, t_ref, l_ref, dx_ref, dxb_ref, dg_ref):
        x_ = x_ref[...]
        g_ = g_ref[...]
        rstd = lax.rsqrt(jnp.mean(x_ * x_, axis=-1, keepdims=True) + EPS)
        xh = x_ * rstd
        err = xh * g_ - t_ref[...]
        lpart = (0.5 / D) * jnp.sum(jnp.sum(err * err, axis=-1, keepdims=True), axis=0, keepdims=True)
        dy = err * (1.0 / D)
        gdy = dy * g_
        dx = (gdy - xh * jnp.mean(gdy * xh, axis=-1, keepdims=True)) * rstd
        dx_ref[...] = dx
        dxb_ref[...] = dx.astype(BF16)
        gpart = jnp.sum(dy * xh, axis=0, keepdims=True)

        @pl.when(pl.program_id(0) == 0)
        def _():
            dg_ref[...] = gpart
            l_ref[...] = lpart

        @pl.when(pl.program_id(0) > 0)
        def _():
            dg_ref[...] += gpart
            l_ref[...] += lpart

    row = pl.BlockSpec((tr, D), lambda i: (i, 0))
    vec = pl.BlockSpec((1, D), lambda i: (0, 0))
    return pl.pallas_call(
        body, name=name,
        out_shape=(jax.ShapeDtypeStruct((1, 1), F32), jax.ShapeDtypeStruct((S, D), F32), jax.ShapeDtypeStruct((S, D), BF16),
                   jax.ShapeDtypeStruct((1, D), F32)),
        grid=(S // tr,),
        in_specs=[row, vec, row],
        out_specs=(pl.BlockSpec((1, 1), lambda i: (0, 0)), row, row, vec),
        compiler_params=_params(("arbitrary",)),
    )(x, g.reshape(1, D), target)


def _mask_of(mask, tq, tk, keys_first=False):
    shape, q_axis = ((tk, tq), 1) if keys_first else ((tq, tk), 0)
    qpos = lax.broadcasted_iota(jnp.int32, shape, q_axis)
    kpos = lax.broadcasted_iota(jnp.int32, shape, 1 - q_axis)
    if mask == 'causal':
        return kpos <= qpos
    return kpos <= (qpos | (CHUNK - 1))


LANES = 128
LOG2E = 1.4426950408889634


def _lane_group(j, w, width):
    lane = lax.broadcasted_iota(jnp.int32, (1, width), 1)
    return (lane >= j * w) & (lane < (j + 1) * w)


def _only(x, j, w):
    if w == x.shape[1]:
        return x
    return jnp.where(_lane_group(j, w, x.shape[1]), x, jnp.zeros_like(x))


def _side_by_side(xs):
    return xs[0] if len(xs) == 1 else jnp.concatenate(xs, axis=1)


def _on_top(xs):
    return xs[0] if len(xs) == 1 else jnp.concatenate(xs, axis=0)


def _stacked(x, hp, w):
    return _on_top([_only(x, j, w) for j in range(hp)])


def _col_block(entry, rows, idx):
    arr, off, width = entry
    return pl.BlockSpec((rows, width), lambda i, j, o=off // width: (idx(i, j), o))


def _attn_fwd(qk, v, H, cq, ck, *, scale, mask, name, rider=None):
    Sq, Sk = qk[0][0][0].shape[0], v[0].shape[0]
    dv = v[2] // H
    w0 = qk[0][2]
    hp = LANES // w0
    G = H // hp
    assert dv == w0 and not qk[0][3] and all(sh and H * w == LANES for _, _, w, sh in qk[1:])
    tq = _pick(Sq, (512, 256, 128))
    tk = tq if mask else _pick(Sk, (512, 256, 128))
    nq, nk = Sq // tq, Sk // tk
    bias = cq is not None
    npart = len(qk)

    def body(*refs):
        refs = split(refs)
        q_refs, k_refs = refs[0:2 * npart:2], refs[1:2 * npart:2]
        v_ref = refs[2 * npart]
        cq_ref, ck_ref = (refs[2 * npart + 1], refs[2 * npart + 2]) if bias else (None, None)
        o_ref, lse_ref, m_s, l_s, acc_s = refs[-5:]
        qi, ki = pl.program_id(0), pl.program_id(1)

        @pl.when(ki == 0)
        def _():
            m_s[...] = jnp.full(m_s.shape, NEG, F32)
            l_s[...] = jnp.zeros(l_s.shape, F32)
            acc_s[...] = jnp.zeros(acc_s.shape, F32)

        def rows_of(vals):
            return _on_top([jnp.broadcast_to(r, (w0, tq)) for r in vals])

        def compute(masked):
            keep = _mask_of(mask, tq, tk, keys_first=True) if masked else None
            for g in range(G):
                lanes = slice(g * LANES, (g + 1) * LANES)
                q128, k128, v128 = q_refs[0][:, lanes], k_refs[0][:, lanes], v_ref[:, lanes]
                ps, alphas = [], []
                extras = list(zip(qk, q_refs, k_refs))[1:]
                k_all = _side_by_side([k128] + [k_ref[...] for _, _, k_ref in extras])
                for j in range(hp):
                    h = g * hp + j
                    q_all = _side_by_side([_only(q128, j, w0)] + [_only(q_ref[...], h, w) for (_, _, w, _), q_ref, _ in extras])
                    s = _dot(k_all, q_all, NT) * scale
                    if bias:
                        s = s + (cq_ref[h:h + 1, :] - ck_ref[:, h:h + 1])
                    if masked:
                        s = jnp.where(keep, s, NEG)
                    m_prev = m_s[h:h + 1, :]
                    m_new = jnp.maximum(m_prev, jnp.max(s, axis=0, keepdims=True))
                    alpha = jnp.exp(m_prev - m_new)
                    p = jnp.exp(s - m_new)
                    l_s[h:h + 1, :] = alpha * l_s[h:h + 1, :] + jnp.sum(p, axis=0, keepdims=True)
                    m_s[h:h + 1, :] = m_new
                    ps.append(p.astype(BF16))
                    alphas.append(alpha)
                acc_s[g] = rows_of(alphas) * acc_s[g] + _dot(_stacked(v128, hp, w0), _on_top(ps), TN)

        if mask is None:
            compute(False)
        else:
            pl.when(ki < qi)(lambda: compute(False))
            pl.when(ki == qi)(lambda: compute(True))

        @pl.when(ki == ((nk - 1) if mask is None else qi))
        def _():
            for g in range(G):
                norm = acc_s[g] / rows_of([l_s[g * hp + j:g * hp + j + 1, :] for j in range(hp)])
                o_ref[:, g * LANES:(g + 1) * LANES] = norm.T.astype(BF16)
            lse_ref[...] = jnp.zeros(lse_ref.shape, F32)
            lse_ref[0:H, :] = m_s[0:H, :] + jnp.log(l_s[0:H, :])

    q_idx = lambda i, j: i
    k_idx = (lambda i, j: jnp.minimum(i, j)) if mask else (lambda i, j: j)
    ins, in_specs = [], []
    for q_e, k_e, _, _ in qk:
        ins += [q_e[0], k_e[0]]
        in_specs += [_col_block(q_e, tq, q_idx), _col_block(k_e, tk, k_idx)]
    ins.append(v[0])
    in_specs.append(_col_block(v, tk, k_idx))
    if bias:
        in_specs += [pl.BlockSpec((8, tq), lambda i, j: (0, i)), pl.BlockSpec((tk, 8), lambda i, j: (k_idx(i, j), 0))]
        ins += [cq, ck]
    r_ins, r_in_specs, r_outs, r_out_specs, r_scratch, split = _carry(
        rider, len(ins), 2, lambda: (pl.program_id(0) == 0) & (pl.program_id(1) == 0),
        lambda: (pl.program_id(0) == nq - 1) & (pl.program_id(1) == nk - 1))
    res = pl.pallas_call(
        body, name=name,
        out_shape=(jax.ShapeDtypeStruct((Sq, H * dv), BF16), jax.ShapeDtypeStruct((8, Sq), F32), *r_outs),
        grid=(nq, nk), in_specs=in_specs + r_in_specs,
        out_specs=(pl.BlockSpec((tq, H * dv), lambda i, j: (i, 0)), pl.BlockSpec((8, tq), lambda i, j: (0, i)), *r_out_specs),
        scratch_shapes=[pltpu.VMEM((8, tq), F32), pltpu.VMEM((8, tq), F32), pltpu.VMEM((G, LANES, tq), F32)] + r_scratch,
        compiler_params=_params(("arbitrary", "arbitrary")) if rider else _params(("parallel", "arbitrary")),
    )(*ins, *r_ins)
    return (res[0], res[1], rider.post(res[2:])) if rider else res


def _attn_bwd(qk, v, H, o, do, lse, cq, ck, *, scale, mask, name, rider=None):
    Sq, Sk = qk[0][0][0].shape[0], v[0].shape[0]
    dv = v[2] // H
    w0 = qk[0][2]
    hp = LANES // w0
    G = H // hp
    tq = _pick(Sq, (512, 256, 128))
    tk = tq if mask else _pick(Sk, (512, 256, 128))
    nq, nk = Sq // tq, Sk // tk
    bias = cq is not None
    npart = len(qk)
    n_in = 2 * npart + 4 + (2 if bias else 0)

    def body(*refs):
        refs = split(refs)
        q_refs, k_refs = refs[0:2 * npart:2], refs[1:2 * npart:2]
        v_ref, o_ref, do_ref, lse_ref = refs[2 * npart:2 * npart + 4]
        cq_ref, ck_ref = (refs[2 * npart + 4], refs[2 * npart + 5]) if bias else (None, None)
        outs = refs[n_in:]
        dq_refs, dk_refs, dv_ref = outs[:npart], outs[npart:2 * npart], outs[2 * npart]
        dck_ref, dcq_ref = (outs[2 * npart + 1], outs[2 * npart + 2]) if bias else (None, None)
        dk_accs, dv_acc = refs[-(npart + 1):-1], refs[-1]
        ki, qi = pl.program_id(0), pl.program_id(1)
        first_q = ki if mask else 0

        @pl.when((ki == 0) & (qi == 0))
        def _():
            for r in dq_refs:
                r[...] = jnp.zeros(r.shape, F32)
            if bias:
                dcq_ref[...] = jnp.zeros(dcq_ref.shape, F32)

        @pl.when(qi == first_q)
        def _():
            for r in dk_accs:
                r[...] = jnp.zeros(r.shape, F32)
            dv_acc[...] = jnp.zeros(dv_acc.shape, F32)
            if bias:
                dck_ref[...] = jnp.zeros(dck_ref.shape, F32)

        def compute(masked):
            keep = _mask_of(mask, tq, tk, keys_first=True) if masked else None
            rows = pl.ds(pl.multiple_of(qi * tq, tq), tq)
            extras = list(zip(qk, q_refs, k_refs, dq_refs, dk_accs))[1:]
            for g in range(G):
                lanes = slice(g * LANES, (g + 1) * LANES)
                q128, k128, v128 = q_refs[0][:, lanes], k_refs[0][:, lanes], v_ref[:, lanes]
                do128, o128 = do_ref[:, lanes], o_ref[:, lanes]
                prod = do128.astype(F32) * o128.astype(F32)
                ps, dss = [], []
                k_all = _side_by_side([k128] + [e[2][...] for e in extras])
                for j in range(hp):
                    h = g * hp + j
                    q_all = _side_by_side([_only(q128, j, w0)] + [_only(e[1][...], h, e[0][2]) for e in extras])
                    s = _dot(k_all, q_all, NT) * (scale * LOG2E)
                    if bias:
                        s = s - ck_ref[:, h:h + 1] * LOG2E
                    if masked:
                        s = jnp.where(keep, s, NEG)
                    row = lse_ref[h:h + 1, :] - cq_ref[h:h + 1, :] if bias else lse_ref[h:h + 1, :]
                    p = jnp.exp2(s - row * LOG2E)
                    dp = _dot(v128, _only(do128, j, w0), NT)
                    delta = jnp.sum(_only(prod, j, w0), axis=1, keepdims=True).T
                    ds = p * (dp - delta)
                    if bias:
                        dck_ref[:, h:h + 1] -= jnp.sum(ds, axis=1, keepdims=True)
                        dcq_ref[h:h + 1, rows] += jnp.sum(ds, axis=0, keepdims=True)
                    ps.append(p.astype(BF16))
                    dss.append((ds * scale).astype(BF16))
                for (_, _, w, _), q_ref, k_ref, dq_ref, dk_acc in extras:
                    heads = range(g * hp, (g + 1) * hp)
                    dk_acc[...] += _dot(_side_by_side(dss), _on_top([_only(q_ref[...], h, w) for h in heads]), NN)
                    dq_ref[rows, :] += _dot(_on_top(dss), _on_top([_only(k_ref[...], h, w) for h in heads]), TN)
                dv_acc[:, lanes] += _dot(_side_by_side(ps), _stacked(do128, hp, w0), NN)
                dk_accs[0][:, lanes] += _dot(_side_by_side(dss), _stacked(q128, hp, w0), NN)
                dq_refs[0][rows, lanes] += _dot(_on_top(dss), _stacked(k128, hp, w0), TN)

        if mask is None:
            compute(False)
        else:
            pl.when(qi > ki)(lambda: compute(False))
            pl.when(qi == ki)(lambda: compute(True))

        @pl.when(qi == nq - 1)
        def _():
            for r, acc in zip(dk_refs, dk_accs):
                r[...] = acc[...]
            dv_ref[...] = dv_acc[...]

    q_idx = (lambda j, i: jnp.maximum(i, j)) if mask else (lambda j, i: i)
    k_idx = lambda j, i: j
    ins, in_specs, dq_shapes, dq_specs, dk_shapes, dk_specs, scratch = [], [], [], [], [], [], []
    for q_e, k_e, w, shared in qk:
        ins += [q_e[0], k_e[0]]
        in_specs += [_col_block(q_e, tq, q_idx), _col_block(k_e, tk, k_idx)]
        dq_shapes.append(jax.ShapeDtypeStruct((Sq, H * w), F32))
        dq_specs.append(pl.BlockSpec((Sq, H * w), lambda j, i: (0, 0)))
        kw = k_e[2]
        dk_shapes.append(jax.ShapeDtypeStruct((Sk, kw), F32))
        dk_specs.append(pl.BlockSpec((tk, kw), lambda j, i: (j, 0)))
        scratch.append(pltpu.VMEM((tk, kw), F32))
    row_q = lambda width: pl.BlockSpec((tq, width), lambda j, i: (q_idx(j, i), 0))
    per_q = pl.BlockSpec((8, tq), lambda j, i: (0, q_idx(j, i)))
    ins += [v[0], o, do, lse]
    in_specs += [_col_block(v, tk, k_idx), row_q(H * dv), row_q(H * dv), per_q]
    out_shape = dq_shapes + dk_shapes + [jax.ShapeDtypeStruct((Sk, H * dv), F32)]
    out_specs = dq_specs + dk_specs + [pl.BlockSpec((tk, H * dv), lambda j, i: (j, 0))]
    if bias:
        in_specs += [per_q, pl.BlockSpec((tk, 8), lambda j, i: (j, 0))]
        ins += [cq, ck]
        out_shape += [jax.ShapeDtypeStruct((Sk, 8), F32), jax.ShapeDtypeStruct((8, Sq), F32)]
        out_specs += [pl.BlockSpec((tk, 8), lambda j, i: (j, 0)), pl.BlockSpec((8, Sq), lambda j, i: (0, 0))]
    scratch.append(pltpu.VMEM((tk, H * dv), F32))
    n_out = len(out_shape)
    r_ins, r_in_specs, r_outs, r_out_specs, r_scratch, split = _carry(
        rider, len(ins), n_out, lambda: (pl.program_id(0) == 0) & (pl.program_id(1) == 0),
        lambda: (pl.program_id(0) == nk - 1) & (pl.program_id(1) == nq - 1))
    res = pl.pallas_call(
        body, name=name, out_shape=tuple(out_shape + r_outs), grid=(nk, nq), in_specs=in_specs + r_in_specs,
        out_specs=tuple(out_specs + r_out_specs), scratch_shapes=scratch + r_scratch,
        compiler_params=_params(("arbitrary", "arbitrary")),
    )(*ins, *r_ins)
    own = (list(res[:npart]), list(res[npart:2 * npart]), res[2 * npart]) + tuple(res[2 * npart + 1:n_out])
    return own + (rider.post(res[n_out:]),) if rider else own


def _split3_dot(x, t):
    hi = x.astype(BF16)
    r1 = x - hi.astype(F32)
    mid = r1.astype(BF16)
    lo = (r1 - mid.astype(F32)).astype(BF16)
    return _dot(hi, t, NN) + _dot(mid, t, NN) + _dot(lo, t, NN)


def _fox_cum_fwd(ff_t, b, *, name):
    _, S = ff_t.shape
    tb = _pick(S, (512, 256, 128))

    def body(f_ref, b_ref, o_ref, carry):
        @pl.when(pl.program_id(0) == 0)
        def _():
            carry[...] = jnp.zeros(carry.shape, F32)

        lf = _log_sigmoid(f_ref[...] + b_ref[...])
        o_ref[...] = _split3_dot(lf, _tri(tb, False)) + carry[...]
        carry[...] += jnp.sum(lf, axis=1, keepdims=True)

    return pl.pallas_call(
        body, name=name, out_shape=jax.ShapeDtypeStruct((8, S), F32), grid=(S // tb,),
        in_specs=[pl.BlockSpec((8, tb), lambda i: (0, i)), pl.BlockSpec((8, 1), lambda i: (0, 0))],
        out_specs=pl.BlockSpec((8, tb), lambda i: (0, i)),
        scratch_shapes=[pltpu.VMEM((8, 1), F32)],
        compiler_params=_params(("arbitrary",)),
    )(ff_t, b)


def _fox_cum_bwd(ff_t, b, dcum_t, *, name):
    _, S = ff_t.shape
    tb = _pick(S, (512, 256, 128))
    nb = S // tb

    def body(f_ref, b_ref, dc_ref, df_ref, db_ref, carry):
        @pl.when(pl.program_id(0) == 0)
        def _():
            carry[...] = jnp.zeros(carry.shape, F32)
            db_ref[...] = jnp.zeros(db_ref.shape, F32)

        dc = dc_ref[...]
        dlf = _split3_dot(dc, _tri(tb, True)) + carry[...]
        carry[...] += jnp.sum(dc, axis=1, keepdims=True)
        df = dlf * _sigmoid(-(f_ref[...] + b_ref[...]))
        df_ref[...] = df
        db_ref[...] += jnp.sum(df, axis=1, keepdims=True)

    rev = lambda i: (0, nb - 1 - i)
    return pl.pallas_call(
        body, name=name,
        out_shape=(jax.ShapeDtypeStruct((8, S), F32), jax.ShapeDtypeStruct((8, 1), F32)), grid=(nb,),
        in_specs=[pl.BlockSpec((8, tb), rev), pl.BlockSpec((8, 1), lambda i: (0, 0)), pl.BlockSpec((8, tb), rev)],
        out_specs=(pl.BlockSpec((8, tb), rev), pl.BlockSpec((8, 1), lambda i: (0, 0))),
        scratch_shapes=[pltpu.VMEM((8, 1), F32)],
        compiler_params=_params(("arbitrary",)),
    )(ff_t, b, dcum_t)


GLA_W = GLA_HEADS * GLA_DK
GLA_BLOCK_CHUNKS = 4


def _same_chunk(n, lower):
    r = lax.broadcasted_iota(jnp.int32, (n, n), 0)
    c = lax.broadcasted_iota(jnp.int32, (n, n), 1)
    same = (r | (CHUNK - 1)) == (c | (CHUNK - 1))
    return jnp.where(same & (r >= c) if lower else same, 1.0, 0.0).astype(BF16)


def _chunk_mix(x, t, transpose):
    hi, lo = _split2(x)
    dims = TN if transpose else NN
    return _dot(t, hi, dims) + _dot(t, lo, dims)


@jax.custom_vjp
def chunk_cumsum(x):
    return _chunk_mix(x, _same_chunk(x.shape[0], True), False)


chunk_cumsum.defvjp(lambda x: (chunk_cumsum(x), None), lambda _, g: (_chunk_mix(g, _same_chunk(g.shape[0], True), True),))


@jax.custom_vjp
def chunk_total(x):
    return _chunk_mix(x, _same_chunk(x.shape[0], False), False)


chunk_total.defvjp(lambda x: (chunk_total(x), None), lambda _, g: (_chunk_mix(g, _same_chunk(g.shape[0], False), False),))


def _gla_block(q, k, zsm, wg, bg, go, vs, rs, states):
    n_chunks = q.shape[0] // CHUNK
    la = _log_sigmoid(bdot(zsm, wg) + bg) * (1.0 / GLA_TAU)
    end = chunk_total(la)
    kd = k * jnp.exp(end - chunk_cumsum(la))
    a = jnp.exp(end)
    qs = q * (GLA_DK ** -0.5)
    lane = lax.broadcasted_iota(jnp.int32, (1, GLA_W), 1)
    outs, new_states = [], []
    for h in range(GLA_HEADS):
        kdh = kd * jnp.where((lane >= h * GLA_DK) & (lane < (h + 1) * GLA_DK), 1.0, 0.0)
        st, o = states[h], []
        for c in range(n_chunks):
            rows = slice(c * CHUNK, (c + 1) * CHUNK)
            st = st * a[c * CHUNK:c * CHUNK + 1] + bdot_tn(vs[h][rows], kdh[rows])
            o.append(bdot_nt(qs[rows], st))
        o = _rms(jnp.concatenate(o, axis=0), go)
        outs.append(o * (rs[h] * _sigmoid(rs[h])))
        new_states.append(st)
    return outs, new_states


def _gla_fwd(z, zsm, wg, bg, go, cols, *, name):
    S = z.shape[0]
    rb = GLA_BLOCK_CHUNKS * CHUNK
    nb = S // rb
    cq, ckk, cv, cr = cols
    H = GLA_HEADS

    def body(q_ref, k_ref, zsm_ref, wg_ref, bg_ref, go_ref, *rest):
        v_refs, r_refs = rest[:H], rest[H:2 * H]
        o_ref, st_ref, state = rest[2 * H], rest[2 * H + 1], rest[2 * H + 2]

        @pl.when(pl.program_id(0) == 0)
        def _():
            state[...] = jnp.zeros(state.shape, F32)

        states = [state[h] for h in range(H)]
        for h in range(H):
            st_ref[0, h] = states[h]
        outs, new_states = _gla_block(
            q_ref[...].astype(F32), k_ref[...].astype(F32), zsm_ref[...], wg_ref[...], bg_ref[...], go_ref[...],
            [v_refs[h][...].astype(F32) for h in range(H)], [r_refs[h][...].astype(F32) for h in range(H)], states)
        for h in range(H):
            o_ref[:, h * GLA_DV:(h + 1) * GLA_DV] = outs[h].astype(BF16)
            state[h] = new_states[h]

    def col(width, off):
        return pl.BlockSpec((rb, width), lambda i, o=off // width: (i, o))

    full = lambda shp: pl.BlockSpec(shp, lambda i: (0,) * len(shp))
    in_specs = [col(GLA_W, cq), col(GLA_W, ckk), pl.BlockSpec((rb, 128), lambda i: (i, 0)),
                full((128, GLA_W)), full((1, GLA_W)), full((1, GLA_DV))]
    in_specs += [col(GLA_DV, cv + h * GLA_DV) for h in range(H)] + [col(GLA_DV, cr + h * GLA_DV) for h in range(H)]
    return pl.pallas_call(
        body, name=name,
        out_shape=(jax.ShapeDtypeStruct((S, H * GLA_DV), BF16), jax.ShapeDtypeStruct((nb, H, GLA_DV, GLA_W), F32)),
        grid=(nb,), in_specs=in_specs,
        out_specs=(pl.BlockSpec((rb, H * GLA_DV), lambda i: (i, 0)),
                   pl.BlockSpec((1, H, GLA_DV, GLA_W), lambda i: (i, 0, 0, 0))),
        scratch_shapes=[pltpu.VMEM((H, GLA_DV, GLA_W), F32)],
        compiler_params=_params(("arbitrary",)),
    )(z, z, zsm, wg, bg, go, *([z] * (2 * H)))


def _gla_bwd(z, zsm, wg, bg, go, states, do, cols, *, name):
    S = z.shape[0]
    rb = GLA_BLOCK_CHUNKS * CHUNK
    nb = S // rb
    cq, ckk, cv, cr = cols
    H = GLA_HEADS

    def body(q_ref, k_ref, zsm_ref, wg_ref, bg_ref, go_ref, st_ref, do_ref, *rest):
        v_refs, r_refs = rest[:H], rest[H:2 * H]
        dq_ref, dk_ref, dv_ref, dr_ref, dzsm_ref, dwg_ref, dbg_ref, dgo_ref, dstate = rest[2 * H:]

        @pl.when(pl.program_id(0) == 0)
        def _():
            dstate[...] = jnp.zeros(dstate.shape, F32)
            dwg_ref[...] = jnp.zeros(dwg_ref.shape, F32)
            dbg_ref[...] = jnp.zeros(dbg_ref.shape, F32)
            dgo_ref[...] = jnp.zeros(dgo_ref.shape, F32)

        prim = (q_ref[...].astype(F32), k_ref[...].astype(F32), zsm_ref[...], wg_ref[...], bg_ref[...], go_ref[...],
                [v_refs[h][...].astype(F32) for h in range(H)], [r_refs[h][...].astype(F32) for h in range(H)],
                [st_ref[0, h] for h in range(H)])
        _, vjp = jax.vjp(_gla_block, *prim)
        douts = [do_ref[:, h * GLA_DV:(h + 1) * GLA_DV].astype(F32) for h in range(H)]
        dq, dk, dzs, dwg, dbg, dgo, dvs, drs, dsts = vjp((douts, [dstate[h] for h in range(H)]))
        dq_ref[...] = dq.astype(BF16)
        dk_ref[...] = dk.astype(BF16)
        dzsm_ref[...] = dzs
        dwg_ref[...] += dwg
        dbg_ref[...] += dbg
        dgo_ref[...] += dgo
        for h in range(H):
            dv_ref[:, h * GLA_DV:(h + 1) * GLA_DV] = dvs[h].astype(BF16)
            dr_ref[:, h * GLA_DV:(h + 1) * GLA_DV] = drs[h].astype(BF16)
            dstate[h] = dsts[h]

    rev = lambda i: nb - 1 - i

    def col(width, off):
        return pl.BlockSpec((rb, width), lambda i, o=off // width: (rev(i), o))

    full = lambda shp: pl.BlockSpec(shp, lambda i: (0,) * len(shp))
    rowb = lambda w: pl.BlockSpec((rb, w), lambda i: (rev(i), 0))
    in_specs = [col(GLA_W, cq), col(GLA_W, ckk), rowb(128), full((128, GLA_W)), full((1, GLA_W)), full((1, GLA_DV)),
                pl.BlockSpec((1, H, GLA_DV, GLA_W), lambda i: (rev(i), 0, 0, 0)), rowb(H * GLA_DV)]
    in_specs += [col(GLA_DV, cv + h * GLA_DV) for h in range(H)] + [col(GLA_DV, cr + h * GLA_DV) for h in range(H)]
    return pl.pallas_call(
        body, name=name,
        out_shape=(jax.ShapeDtypeStruct((S, GLA_W), BF16), jax.ShapeDtypeStruct((S, GLA_W), BF16),
                   jax.ShapeDtypeStruct((S, H * GLA_DV), BF16), jax.ShapeDtypeStruct((S, H * GLA_DV), BF16),
                   jax.ShapeDtypeStruct((S, 128), F32), jax.ShapeDtypeStruct((128, GLA_W), F32),
                   jax.ShapeDtypeStruct((1, GLA_W), F32), jax.ShapeDtypeStruct((1, GLA_DV), F32)),
        grid=(nb,), in_specs=in_specs,
        out_specs=(rowb(GLA_W), rowb(GLA_W), rowb(H * GLA_DV), rowb(H * GLA_DV), rowb(128),
                   full((128, GLA_W)), full((1, GLA_W)), full((1, GLA_DV))),
        scratch_shapes=[pltpu.VMEM((H, GLA_DV, GLA_W), F32)],
        compiler_params=_params(("arbitrary",)),
    )(z, z, zsm, wg, bg, go, states, do, *([z] * (2 * H)))


def _row_spec(entry, tr):
    if isinstance(entry, tuple):
        arr, width, off = entry
        return arr, pl.BlockSpec((tr, width), lambda i, o=off // width: (i, o))
    return entry, pl.BlockSpec((tr, entry.shape[1]), lambda i: (i, 0))


def _stage_fwd(fn, rows, consts, outs, *, name, tr=None):
    first = rows[0][0] if isinstance(rows[0], tuple) else rows[0]
    S = first.shape[0]
    tr = tr or _pick(S, (512, 256, 128))
    arrs, specs = zip(*[_row_spec(e, tr) for e in rows])
    nr, nc = len(rows), len(consts)

    def body(*refs):
        vals = [r[...].astype(F32) for r in refs[:nr + nc]]
        res = fn(*vals)
        for o_ref, val in zip(refs[nr + nc:], res):
            o_ref[...] = val.astype(o_ref.dtype)

    cspecs = [pl.BlockSpec(c.shape, lambda i, n=c.ndim: (0,) * n) for c in consts]
    return pl.pallas_call(
        body, name=name,
        out_shape=tuple(jax.ShapeDtypeStruct((S, w), dt) for w, dt in outs), grid=(S // tr,),
        in_specs=list(specs) + cspecs,
        out_specs=tuple(pl.BlockSpec((tr, w), lambda i: (i, 0)) for w, _ in outs),
        compiler_params=_params(("parallel",)),
    )(*arrs, *consts)


def _stage_bwd(fn, rows, consts, cts, n_diff, drow_dtypes, *, name, tr=None, lead=None):
    first = rows[0][0] if isinstance(rows[0], tuple) else rows[0]
    S = first.shape[0]
    tr = tr or _pick(S, (512, 256, 128))
    arrs, specs = zip(*[_row_spec(e, tr) for e in rows])
    widths = [e[1] if isinstance(e, tuple) else e.shape[1] for e in rows]
    nr, nc, nt = len(rows), len(consts), len(cts)
    n_lead, lead_width = lead or (1, widths[0])
    n_rows_out = n_diff - n_lead + 1

    def body(*refs):
        vals = [r[...].astype(F32) for r in refs[:nr + nc]]
        ct = [r[...].astype(F32) for r in refs[nr + nc:nr + nc + nt]]
        drow_refs = refs[nr + nc + nt:nr + nc + nt + n_rows_out]
        dconst_refs = refs[nr + nc + nt + n_rows_out:]
        rest_rows = vals[n_diff:nr]

        def f(diff_rows, cs):
            return tuple(fn(*diff_rows, *rest_rows, *cs))

        _, vjp = jax.vjp(f, vals[:n_diff], vals[nr:])
        drows, dcs = vjp(tuple(ct))
        off = 0
        for val, w in zip(drows[:n_lead], widths):
            drow_refs[0][:, off:off + w] = val.astype(drow_refs[0].dtype)
            off += w
        for r, val in zip(drow_refs[1:], drows[n_lead:]):
            r[...] = val.astype(r.dtype)
        first_step = pl.program_id(0) == 0
        for r, val in zip(dconst_refs, dcs):
            @pl.when(first_step)
            def _(r=r, val=val):
                r[...] = val

            @pl.when(jnp.logical_not(first_step))
            def _(r=r, val=val):
                r[...] += val

    cspecs = [pl.BlockSpec(c.shape, lambda i, n=c.ndim: (0,) * n) for c in consts]
    ctspecs = [pl.BlockSpec((tr, c.shape[1]), lambda i: (i, 0)) for c in cts]
    out_shape = [jax.ShapeDtypeStruct((S, lead_width), drow_dtypes[0])]
    out_shape += [jax.ShapeDtypeStruct((S, widths[j]), drow_dtypes[j]) for j in range(n_lead, n_diff)]
    out_shape += [jax.ShapeDtypeStruct(c.shape, F32) for c in consts]
    out_specs = [pl.BlockSpec((tr, sum(widths[:n_lead])), lambda i: (i, 0))]
    out_specs += [pl.BlockSpec((tr, widths[j]), lambda i: (i, 0)) for j in range(n_lead, n_diff)] + cspecs
    res = pl.pallas_call(
        body, name=name, out_shape=tuple(out_shape), grid=(S // tr,),
        in_specs=list(specs) + cspecs + ctspecs, out_specs=tuple(out_specs),
        compiler_params=_params(("arbitrary",)),
    )(*arrs, *consts, *cts)
    return list(res[:n_rows_out]), list(res[n_rows_out:])


def _mla_prep_fn(cq, ckv, kr, kr_sw, cos, sin, gq, gkv, wq_n, wq_r, wq_sw, wk, wv):
    hq = _rms(cq, gq)
    hkv = _rms(ckv, gkv)
    return (bdot(hq, wq_n), bdot(hq, wq_r) * cos + bdot(hq, wq_sw) * sin,
            bdot(hkv, wk), bdot(hkv, wv), kr * cos + kr_sw * sin)


def _merge_fn(g0, g1, g2, of, og, om, b0, b1, b2, wf, wg, wm):
    return (_sigmoid(g0 + b0) * bdot(of, wf) + _sigmoid(g1 + b1) * bdot(og, wg) + _sigmoid(g2 + b2) * bdot(om, wm),)


_IN_SIZES = (256, 256, 256, 4, 256, 256, 512, 16, 512, 256, 128, 32, 3072)
_IN_OFF = np.concatenate([[0], np.cumsum(_IN_SIZES)])
(_O_FQ, _O_FK, _O_FV, _O_FF, _O_GQ, _O_GK, _O_GV, _O_GLOW, _O_GR, _O_MQ, _O_MKV, _O_MKR, _O_ZG) = [int(o) for o in _IN_OFF[:-1]]
N_IN = int(_IN_OFF[-1])
_BIG_GROUPS = ((_O_ZG, 3072), (_O_GV, 512), (_O_GR, 512), (_O_FQ, 256), (_O_FK, 256), (_O_FV, 256),
               (_O_GQ, 256), (_O_GK, 256), (_O_MQ, 256), (_O_MKV, 128))
Z_GATE, Z_GV, Z_GR, Z_FQ, Z_FK, Z_FV, Z_GQ, Z_GK, Z_MQ, Z_MKV = [int(o) for o in
                                                                    np.concatenate([[0], np.cumsum([w for _, w in _BIG_GROUPS])])[:-1]]
N_BIG = sum(w for _, w in _BIG_GROUPS)
_HALF = MLA_ROPE // 2
_QK_HD = MLA_NOPE + MLA_ROPE
SM_FF, SM_GLOW, SM_KR, SM_KR_SW, N_SM = 0, 8, 128, 256, 384
N_PAD = N_BIG + N_SM
_IN_SEGS = ([(o, w, 1.0) for o, w in _BIG_GROUPS]
            + [(_O_FF, 4, 1.0), (None, SM_GLOW - 4, 0.0), (_O_GLOW, GLA_RANK, 1.0), (None, 128 - SM_GLOW - GLA_RANK, 0.0)]
            + [(_O_MKR, MLA_ROPE, 1.0)] * MLA_HEADS
            + [(_O_MKR + _HALF, _HALF, -1.0), (_O_MKR, _HALF, 1.0)] * MLA_HEADS)


def _cols(x, start, width):
    return lax.slice_in_dim(x, start, start + width, axis=x.ndim - 1)


def _pad_w_in(w):
    return jnp.concatenate([jnp.zeros(w.shape[:-1] + (n,), w.dtype) if src is None else
                            (_cols(w, src, n) if sign > 0 else -_cols(w, src, n)) for src, n, sign in _IN_SEGS], axis=-1)


def _unpad_w_in(g):
    groups = []
    for o, n in zip(_IN_OFF[:-1], _IN_SIZES):
        total, pos = None, 0
        for src, m, sign in _IN_SEGS:
            if src is not None and o <= src and src + m <= o + n:
                term = _cols(g, pos, m) if sign > 0 else -_cols(g, pos, m)
                if m != n:
                    term = jnp.pad(term, [(0, 0)] * (g.ndim - 1) + [(int(src - o), int(o + n - src - m))])
                total = term if total is None else total + term
            pos += m
        groups.append(total)
    return jnp.concatenate(groups, axis=-1)


def _take(x, idx):
    idx = np.asarray(idx)
    cuts = [0] + [i for i in range(1, len(idx)) if idx[i] != idx[i - 1] + 1] + [len(idx)]
    return jnp.concatenate([_cols(x, int(idx[a]), b - a) for a, b in zip(cuts[:-1], cuts[1:])], axis=1)


_UQ_NOPE = np.concatenate([np.arange(h * _QK_HD, h * _QK_HD + MLA_NOPE) for h in range(MLA_HEADS)])
_UQ_ROT = np.concatenate([np.arange(h * _QK_HD + MLA_NOPE, (h + 1) * _QK_HD) for h in range(MLA_HEADS)])
_UKV_PERM = np.concatenate(
    [np.concatenate([np.arange(h * 128, h * 128 + MLA_NOPE) for h in range(MLA_HEADS)]),
     np.concatenate([np.arange(h * 128 + MLA_NOPE, (h + 1) * 128) for h in range(MLA_HEADS)])])
_UKV_INV = np.argsort(_UKV_PERM)


def _rotary_partner(r):
    return jnp.concatenate([piece for h in range(MLA_HEADS) for piece in
                            (-_cols(r, h * MLA_ROPE + _HALF, _HALF), _cols(r, h * MLA_ROPE, _HALF))], axis=1)


def _uq_grad(dn, dr, dsw):
    dr = dr + jnp.concatenate([piece for h in range(MLA_HEADS) for piece in
                               (_cols(dsw, h * MLA_ROPE + _HALF, _HALF), -_cols(dsw, h * MLA_ROPE, _HALF))], axis=1)
    return jnp.concatenate([piece for h in range(MLA_HEADS) for piece in
                            (_cols(dn, h * MLA_NOPE, MLA_NOPE), _cols(dr, h * MLA_ROPE, MLA_ROPE))], axis=1)


def _rope_tables(S):
    inv = ROPE_BASE ** (-jnp.arange(_HALF, dtype=F32) / _HALF)
    ang = jnp.arange(S, dtype=F32)[:, None] * inv[None, :]
    return jnp.tile(jnp.cos(ang), (1, 2 * MLA_HEADS)), jnp.tile(jnp.sin(ang), (1, 2 * MLA_HEADS))


class _LayerParams:
    def __init__(self, rep, l):
        self.w, self.rep, self.l, self.made = {}, rep, l, {}

    def __getitem__(self, k):
        if k not in self.made:
            self.made[k] = self._make(k)
        return self.made[k]

    def _make(self, k):
        w, rep, l = self.w, self.rep, self.l
        if k == 'wg':
            return jnp.pad(w['w_gla_gate'], [(SM_GLOW, LANES - SM_GLOW - GLA_RANK), (0, 0)])
        if k in ('wq_n', 'wq_r'):
            return _take(w['w_mla_uq'], _UQ_NOPE if k == 'wq_n' else _UQ_ROT)
        if k == 'wq_sw':
            return _rotary_partner(self['wq_r'])
        if k in ('wk', 'wv'):
            return _take(w['w_mla_ukv'], _UKV_PERM[:256] if k == 'wk' else _UKV_PERM[256:])
        if k == 'b_f':
            return jnp.zeros((8, 1), F32).at[:FOX_HEADS, 0].set(rep['b_fox_forget'][l])
        if k == 'b_gate':
            return [rep['b_branch_gate'][l][i * 1024:(i + 1) * 1024].reshape(1, 1024) for i in range(3)]
        vec = {'bg': 'b_gla_gate', 'go': 'g_gla_out', 'gq': 'g_mla_q', 'gkv': 'g_mla_kv'}
        if k in vec:
            return rep[vec[k]][l].reshape(1, -1)
        return rep[k][l] if k in rep else w[k]


_GLA_COLS = (Z_GQ, Z_GK, Z_GV, Z_GR)
_MLA_OUTS = [(256, BF16), (128, BF16), (256, BF16), (256, BF16), (128, BF16)]


def _mla_rows(z, zsm, rope):
    return [(z, 256, Z_MQ), (z, 128, Z_MKV), (zsm, 128, SM_KR), (zsm, 128, SM_KR_SW), *rope]


def _mla_consts(p):
    return [p['gq'], p['gkv'], p['wq_n'], p['wq_r'], p['wq_sw'], p['wk'], p['wv']]


def _fox_qkv(z):
    return [((z, Z_FQ, 256), (z, Z_FK, 256), FOX_HD, False)], (z, Z_FV, 256)


def _mla_qkv(qn, qr, kn, vv, kr):
    return [((qn, 0, 256), (kn, 0, 256), MLA_NOPE, False), ((qr, 0, 128), (kr, 0, 128), MLA_ROPE, True)], (vv, 0, 256)


def _xa_qkv(qx, kvx):
    return [((qx, 0, 512), (kvx, 0, 512), XA_HD, False)], (kvx, 512, 512)


def _merge_rows(z, o_fox, o_gla, o_mla):
    return [(z, 1024, Z_GATE), (z, 1024, Z_GATE + 1024), (z, 1024, Z_GATE + 2048), o_fox, o_gla, o_mla]


def _merge_consts(p):
    return p['b_gate'] + [p['w_up_fox'], p['w_up_gla'], p['w_up_mla']]


def _carried(hooks, key, call, single=False):
    entries = hooks.pop(key, [])
    if not entries:
        return call(rider=None)
    res = call(rider=_join_riders([rider for rider, _ in entries]))
    for (_, sink), got in zip(entries, res[-1]):
        sink(got)
    return res[0] if single else res[:-1]


def _layer_fwd(x0, mem, p, rope, l, hooks):
    S = x0.shape[0]
    sv = {'x0': x0}

    def mm(key, a, b, **kw):
        return _carried(hooks, (l, key), lambda rider: _mm(a, b, mode='nn', rider=rider, name=f"{key}_{l}", **kw), single=True)

    h1 = _rms_fwd(x0, p['g_mix'], name=f"rms_mix_{l}")
    z = mm('in_big', h1, p['w_in'], out_dtype=BF16, b_cols=(0, N_BIG))
    zsm = _mm(h1, p['w_in'], mode='nn', out_dtype=F32, b_cols=(N_BIG, N_SM), name=f"in_small_{l}")
    sv.update(h1=h1, z=z, zsm=zsm)
    ff_t = jnp.zeros((8, S), F32).at[:FOX_HEADS].set(zsm[:, SM_FF:SM_FF + FOX_HEADS].T)
    cum_t = _fox_cum_fwd(ff_t, p['b_f'], name=f"fox_cum_{l}")
    cum = cum_t.T
    o_fox, lse_f = _carried(hooks, (l, 'fox_fwd'), lambda rider: _attn_fwd(
        *_fox_qkv(z), FOX_HEADS, cum_t, cum, scale=FOX_HD ** -0.5, mask='causal', name=f"fox_fwd_{l}", rider=rider))
    sv.update(ff_t=ff_t, cum=cum, cum_t=cum_t, lse_f=lse_f, o_fox=o_fox)
    o_gla, states = _gla_fwd(z, zsm, p['wg'], p['bg'], p['go'], _GLA_COLS, name=f"gla_fwd_{l}")
    sv.update(o_gla=o_gla, states=states)
    mla = _stage_fwd(_mla_prep_fn, _mla_rows(z, zsm, rope), _mla_consts(p), _MLA_OUTS, name=f"mla_prep_{l}")
    o_mla, lse_m = _carried(hooks, (l, 'mla_fwd'), lambda rider: _attn_fwd(
        *_mla_qkv(*mla), MLA_HEADS, None, None, scale=_QK_HD ** -0.5, mask='chunk', name=f"mla_fwd_{l}", rider=rider))
    sv.update(mla=mla, lse_m=lse_m, o_mla=o_mla)
    (y,) = _stage_fwd(_merge_fn, _merge_rows(z, o_fox, o_gla, o_mla), _merge_consts(p), [(1024, BF16)], name=f"merge_{l}")
    x1 = mm('out_proj', y, p['w_out'], out_dtype=F32, residual=x0)
    sv.update(y=y, x1=x1)
    h2 = _rms_fwd(x1, p['g_xa'], name=f"rms_xa_{l}")
    hm = _rms_fwd(mem, p['g_mem'], name=f"rms_mem_{l}")
    qx = _mm(h2, p['w_xq'], mode='nn', out_dtype=BF16, name=f"xq_{l}")
    kvx = _mm(hm, p['w_xkv'], mode='nn', out_dtype=BF16, name=f"xkv_{l}")
    ox, lse_x = _carried(hooks, (l, 'xa_fwd'), lambda rider: _attn_fwd(
        *_xa_qkv(qx, kvx), XA_HEADS, None, None, scale=XA_HD ** -0.5, mask=None, name=f"xa_fwd_{l}", rider=rider))
    x2 = mm('xo', ox, p['w_xo'], out_dtype=F32, residual=x1)
    sv.update(h2=h2, hm=hm, qx=qx, kvx=kvx, lse_x=lse_x, ox=ox, x2=x2)
    h3 = _rms_fwd(x2, p['g_mlp'], name=f"rms_mlp_{l}")
    a = mm('mlp1', h3, p['w_mlp1'], out_dtype=BF16)
    x3 = mm('mlp2', a, p['w_mlp2'], out_dtype=F32, act='relu2', residual=x2)
    sv.update(h3=h3, a=a)
    return x3, sv


def _layer_bwd(dx3, dx3b, mem, p, rope, sv, l, hooks, half_done, matrices_done):
    S = dx3.shape[0]
    g = {}

    def dw(key, a, b, **kw):
        return _mm(a, b, mode='tn', out_dtype=BF16, col_shards=b.shape[1] // LANES, name=f"d_{key}_{l}", **kw)

    da = _mm(dx3b, p['w_mlp2'], mode='nt', out_dtype=BF16, drelu_of=sv['a'], name=f"d_mlp2_in_{l}")
    g['w_mlp2'] = dw('w_mlp2', sv['a'], dx3b, act='relu2')
    dx2, dx2b, g['g_mlp'] = _mm(da, p['w_mlp1'], mode='nt', out_dtype=F32, norm_bwd=(sv['x2'], p['g_mlp'], dx3), tm=512,
                                name=f"d_mlp1_in_{l}")
    g['w_mlp1'] = dw('w_mlp1', sv['h3'], da)
    dox = _mm(dx2b, p['w_xo'], mode='nt', out_dtype=BF16, name=f"d_xo_in_{l}")
    g['w_xo'] = dw('w_xo', sv['ox'], dx2b)
    (dqx,), (dkx,), dvx = _carried(hooks, (l, 'xa_bwd'), lambda rider: _attn_bwd(
        *_xa_qkv(sv['qx'], sv['kvx']), XA_HEADS, sv['ox'], dox, sv['lse_x'], None, None,
        scale=XA_HD ** -0.5, mask=None, name=f"xa_bwd_{l}", rider=rider))
    dqx = dqx.astype(BF16)
    dkvx = jnp.concatenate([dkx, dvx], axis=1).astype(BF16)
    dx1, dx1b, g['g_xa'] = _mm(dqx, p['w_xq'], mode='nt', out_dtype=F32, norm_bwd=(sv['x1'], p['g_xa'], dx2), tm=512,
                               name=f"d_xq_in_{l}")
    g['w_xq'] = dw('w_xq', sv['h2'], dqx)
    dhm = _mm(dkvx, p['w_xkv'], mode='nt', out_dtype=F32, name=f"d_xkv_in_{l}")
    g['w_xkv'] = dw('w_xkv', sv['hm'], dkvx)
    _, _, g['g_mem'] = _rms_bwd(mem, p['g_mem'], dhm, None, name=f"d_rms_mem_{l}")
    dy = _mm(dx1b, p['w_out'], mode='nt', out_dtype=F32, name=f"d_out_in_{l}")
    g['w_out'] = dw('w_out', sv['y'], dx1b)
    z, zsm = sv['z'], sv['zsm']
    (dz, do_fox, do_gla, do_mla), (db0, db1, db2, g['w_up_fox'], g['w_up_gla'], g['w_up_mla']) = _stage_bwd(
        _merge_fn, _merge_rows(z, sv['o_fox'], sv['o_gla'], sv['o_mla']), _merge_consts(p), [dy], 6, [BF16] * 6,
        lead=(3, N_PAD), name=f"merge_bwd_{l}")
    g['b_branch_gate'] = jnp.concatenate([db0, db1, db2], axis=1).reshape(-1)
    half_done(l, g)
    (dfq,), (dfk,), dfv, dck, dcq = _carried(hooks, (l, 'fox_bwd'), lambda rider: _attn_bwd(
        *_fox_qkv(z), FOX_HEADS, sv['o_fox'], do_fox, sv['lse_f'], sv['cum_t'], sv['cum'],
        scale=FOX_HD ** -0.5, mask='causal', name=f"fox_bwd_{l}", rider=rider))
    dff_t, db_f = _fox_cum_bwd(sv['ff_t'], p['b_f'], dcq + dck.T, name=f"fox_cum_bwd_{l}")
    g['b_fox_forget'] = db_f[:FOX_HEADS, 0]
    dgq, dgk, dgv, dgr, dzsm, dwg, dbg, dgo = _gla_bwd(z, zsm, p['wg'], p['bg'], p['go'], sv['states'], do_gla, _GLA_COLS,
                                                       name=f"gla_bwd_{l}")
    g['w_gla_gate'] = dwg[SM_GLOW:SM_GLOW + GLA_RANK]
    g['b_gla_gate'] = dbg.reshape(-1)
    g['g_gla_out'] = dgo.reshape(-1)
    (dmqn, dmqr), (dmkn, dmkr), dmv = _carried(hooks, (l, 'mla_bwd'), lambda rider: _attn_bwd(
        *_mla_qkv(*sv['mla']), MLA_HEADS, sv['o_mla'], do_mla, sv['lse_m'], None, None,
        scale=_QK_HD ** -0.5, mask='chunk', name=f"mla_bwd_{l}", rider=rider))
    (dcq, dckv, dkr, dkr_sw), (dgq_n, dgkv_n, dwq_n, dwq_r, dwq_sw, dwk, dwv) = _stage_bwd(
        _mla_prep_fn, _mla_rows(z, zsm, rope), _mla_consts(p), [dmqn, dmqr, dmkn, dmv, dmkr], 4, [BF16] * 4,
        name=f"mla_prep_bwd_{l}")
    g['g_mla_q'] = dgq_n.reshape(-1)
    g['g_mla_kv'] = dgkv_n.reshape(-1)
    g['w_mla_uq'] = _uq_grad(dwq_n, dwq_r, dwq_sw)
    g['w_mla_ukv'] = _take(jnp.concatenate([dwk, dwv], axis=1), _UKV_INV)
    dsm = dzsm + jnp.pad(dff_t[:FOX_HEADS].T, [(0, 0), (0, 128 - FOX_HEADS)])
    dz = lax.dynamic_update_slice(dz, jnp.concatenate(
        [dgv, dgr, dfq.astype(BF16), dfk.astype(BF16), dfv.astype(BF16), dgq, dgk, dcq, dckv, dsm.astype(BF16), dkr, dkr_sw],
        axis=1), (0, Z_GV))
    g['w_in'] = dw('w_in', sv['h1'], dz, tn=N_PAD // 3)
    matrices_done(l, g)
    dx0, dx0b, g['g_mix'] = _carried(hooks, (l, 'd_in'), lambda rider: _mm(
        dz, p['w_in'], mode='nt', out_dtype=F32, norm_bwd=(sv['x0'], p['g_mix'], dx1), tm=512, tk=N_PAD // 2,
        name=f"d_in_{l}", rider=rider))
    for n in ('g_mlp', 'g_mem', 'g_xa', 'g_mix'):
        g[n] = g[n].reshape(-1)
    return dx0, dx0b, g


def _local_step(x, mem, target, ps, g_final, hooks, half_done, matrices_done, layer_done):
    rope = _rope_tables(x.shape[0])
    saved = []
    for l, p in enumerate(ps):
        x, sv = _layer_fwd(x, mem, p, rope, l, hooks)
        saved.append(sv)
    loss, dx, dxb, dgf = _loss_head(x, g_final, target, name="loss_head")
    for l in reversed(range(len(ps))):
        dx, dxb, grads = _layer_bwd(dx, dxb, mem, ps[l], rope, saved[l], l, hooks, half_done, matrices_done)
        layer_done(l, grads)
    assert not hooks, f"exchanges without a carrier: {list(hooks)}"
    return loss, dx, dgf.reshape(-1)


_MESH_AXES = ("x", "y", "c")
_HBM = pl.BlockSpec(memory_space=pl.ANY)


N_CHIP = 4
_SLOT_ROWS = (2048, 1024, 512, 256, 128, 64, 32, 16, 8)


def _place():
    x, y, c = (lax.axis_index(n) for n in _MESH_AXES)
    return (x, y, c), (x, y, 1 - c), [(1 - x, y), (x, 1 - y), (1 - x, 1 - y)]


def _remote(src, dst, sems, k, to):
    return pltpu.make_async_remote_copy(src_ref=src, dst_ref=dst, send_sem=sems[0].at[k], recv_sem=sems[1].at[k],
                                        device_id=to, device_id_type=pl.DeviceIdType.MESH)


def _all_gather(x, *, name):
    def body(x_ref, o_ref, send_sems, recv_sems, local_sem):
        me, sib, chips = _place()
        c = me[2]
        sems = (send_sems, recv_sems)
        slot = lambda px, py, pc: o_ref.at[4 * px + 2 * py + pc]
        mine = pltpu.make_async_copy(x_ref, slot(*me), local_sem)
        mine.start()
        first = [_remote(x_ref, slot(*me), sems, 0, sib)]
        first += [_remote(x_ref, slot(*me), sems, 1 + j, (*chip, c)) for j, chip in enumerate(chips)]
        for cp in first:
            cp.start()
        passed = [_remote(slot(*chip, c), slot(*chip, c), sems, 4 + j, sib) for j, chip in enumerate(chips)]
        for j, chip in enumerate(chips):
            _remote(x_ref, slot(*chip, c), sems, 1 + j, me).wait_recv()
            passed[j].start()
        _remote(x_ref, slot(*sib), sems, 0, me).wait_recv()
        for j, chip in enumerate(chips):
            _remote(x_ref, slot(*chip, 1 - c), sems, 4 + j, me).wait_recv()
        for cp in first + passed:
            cp.wait_send()
        mine.wait()

    return pl.pallas_call(
        body, name=name, out_shape=jax.ShapeDtypeStruct((N_DEV,) + x.shape, x.dtype),
        in_specs=[_HBM], out_specs=_HBM,
        scratch_shapes=[pltpu.SemaphoreType.DMA((N_DEV - 1,)), pltpu.SemaphoreType.DMA((N_DEV - 1,)), pltpu.SemaphoreType.DMA],
        compiler_params=pltpu.CompilerParams(has_side_effects=True),
    )(x)


class _Rider:
    def __init__(self, inputs, out_shapes, scratch, start, finish, post):
        self.inputs, self.out_shapes, self.scratch = list(inputs), list(out_shapes), list(scratch)
        self.start, self.finish, self.post = start, finish, post


def _run_rider(rider, *, name):
    def body(*refs):
        rider.start(refs)
        rider.finish(refs)

    outs = pl.pallas_call(
        body, name=name, out_shape=tuple(rider.out_shapes), in_specs=[_HBM] * len(rider.inputs),
        out_specs=(_HBM,) * len(rider.out_shapes), scratch_shapes=rider.scratch,
        compiler_params=pltpu.CompilerParams(has_side_effects=True),
    )(*rider.inputs)
    return rider.post(outs)


def _carry(rider, n_in, n_out, first, last):
    if rider is None:
        return [], [], [], [], [], lambda refs: refs
    ni, no = len(rider.inputs), len(rider.out_shapes)

    def split(refs):
        own_in, r_in = refs[:n_in], refs[n_in:n_in + ni]
        own_out, r_out = refs[n_in + ni:n_in + ni + n_out], refs[n_in + ni + n_out:n_in + ni + n_out + no]
        rest = refs[n_in + ni + n_out + no:]
        own_scr, r_scr = rest[:len(rest) - len(rider.scratch)], rest[len(rest) - len(rider.scratch):]
        rrefs = tuple(r_in) + tuple(r_out) + tuple(r_scr)
        pl.when(first())(lambda: rider.start(rrefs))
        pl.when(last())(lambda: rider.finish(rrefs))
        return tuple(own_in) + tuple(own_out) + tuple(own_scr)

    return list(rider.inputs), [_HBM] * ni, list(rider.out_shapes), [_HBM] * no, list(rider.scratch), split


def _gather_rider(shards, axes):
    n = len(shards)
    srcs, out_shapes, kinds = [], [], []
    for s, ax in zip(shards, axes):
        L, a, b = s.shape
        if ax == 1:
            srcs.append(s.reshape(L, 1, a, b)), out_shapes.append((L, N_DEV, a, b)), kinds.append('row')
        elif b % 128 == 0:
            srcs.append(s), out_shapes.append((L, a, N_DEV * b)), kinds.append('col')
        else:
            srcs.append(s.reshape(1, L, a, b)), out_shapes.append((N_DEV, L, a, b)), kinds.append('slot')

    def parts(refs):
        x_refs, o_refs = refs[:n], refs[n:2 * n]
        send_sems, recv_sems, local_sem = refs[2 * n:]
        me, sib, chips = _place()
        sems = (send_sems, recv_sems)

        def win(t, px, py, pc):
            idx = 4 * px + 2 * py + pc
            if kinds[t] == 'row':
                return o_refs[t].at[:, pl.ds(idx, 1)]
            if kinds[t] == 'col':
                width = shards[t].shape[2]
                return o_refs[t].at[:, :, pl.ds(pl.multiple_of(idx * width, 128), width)]
            return o_refs[t].at[pl.ds(idx, 1)]

        def group(k, block, to, own):
            return [_remote(x_refs[t] if own else win(t, *block), win(t, *block), sems, k * n + t, to) for t in range(n)]

        mine = [pltpu.make_async_copy(x_refs[t], win(t, *me), local_sem.at[t]) for t in range(n)]
        first = group(0, me, sib, True)
        for j, chip in enumerate(chips):
            first += group(1 + j, me, (*chip, me[2]), True)
        return me, sib, chips, group, mine, first

    def start(refs):
        *_, mine, first = parts(refs)
        for cp in mine + first:
            cp.start()

    def finish(refs):
        me, sib, chips, group, mine, first = parts(refs)
        c = me[2]
        passed = []
        for j, chip in enumerate(chips):
            for cp in group(1 + j, (*chip, c), me, False):
                cp.wait_recv()
            fwd = group(4 + j, (*chip, c), sib, False)
            for cp in fwd:
                cp.start()
            passed += fwd
        for cp in group(0, sib, me, False):
            cp.wait_recv()
        for j, chip in enumerate(chips):
            for cp in group(4 + j, (*chip, 1 - c), me, False):
                cp.wait_recv()
        for cp in first + passed:
            cp.wait_send()
        for cp in mine:
            cp.wait()

    def post(outs):
        whole = []
        for o, s, kind in zip(outs, shards, kinds):
            L, a, b = s.shape
            whole.append(o.reshape(L, N_DEV * a, b) if kind == 'row' else o if kind == 'col' else _to_whole(o, 2))
        return whole

    return _Rider(srcs, [jax.ShapeDtypeStruct(shp, s.dtype) for shp, s in zip(out_shapes, shards)],
                  [pltpu.SemaphoreType.DMA(((N_DEV - 1) * n,)), pltpu.SemaphoreType.DMA(((N_DEV - 1) * n,)),
                   pltpu.SemaphoreType.DMA((n,))], start, finish, post)


def _sibling_swap_rider(x):
    def sends(refs):
        x_ref, o_ref, send_sems, recv_sems = refs
        me, sib, _ = _place()
        return [_remote(x_ref.at[j, 1 - me[2]], o_ref.at[j], (send_sems, recv_sems), j, sib) for j in range(N_CHIP)]

    def start(refs):
        for cp in sends(refs):
            cp.start()

    def finish(refs):
        for cp in sends(refs):
            cp.wait_send()
            cp.wait_recv()

    return _Rider([x], [jax.ShapeDtypeStruct((N_CHIP,) + x.shape[2:], x.dtype)],
                  [pltpu.SemaphoreType.DMA((N_CHIP,)), pltpu.SemaphoreType.DMA((N_CHIP,))], start, finish, lambda outs: outs[0])


def _join_riders(riders):
    counts = [(len(r.inputs), len(r.out_shapes), len(r.scratch)) for r in riders]
    n_in, n_out = sum(c[0] for c in counts), sum(c[1] for c in counts)

    def refs_of(refs, k):
        a = sum(c[0] for c in counts[:k])
        b = n_in + sum(c[1] for c in counts[:k])
        s = n_in + n_out + sum(c[2] for c in counts[:k])
        return tuple(refs[a:a + counts[k][0]]) + tuple(refs[b:b + counts[k][1]]) + tuple(refs[s:s + counts[k][2]])

    def each(method):
        def run(refs):
            for k, r in enumerate(riders):
                getattr(r, method)(refs_of(refs, k))
        return run

    def post(outs):
        got, at = [], 0
        for r, c in zip(riders, counts):
            got.append(r.post(outs[at:at + c[1]]))
            at += c[1]
        return got

    return _Rider([x for r in riders for x in r.inputs], [o for r in riders for o in r.out_shapes],
                  [s for r in riders for s in r.scratch], each('start'), each('finish'), post)


def _pair_sum(x, got, c, *, name):
    _, _, R, _ = x.shape
    tr = _pick(R, _SLOT_ROWS)

    def body(c_ref, x_ref, g_ref, o_ref):
        o_ref[...] = (x_ref[...].astype(F32) + g_ref[...].astype(F32)).astype(o_ref.dtype)

    return pl.pallas_call(
        body, name=name, out_shape=jax.ShapeDtypeStruct((N_CHIP, R, 128), x.dtype),
        grid_spec=pltpu.PrefetchScalarGridSpec(
            num_scalar_prefetch=1, grid=(R // tr,),
            in_specs=[pl.BlockSpec((N_CHIP, None, tr, 128), lambda i, c_ref: (0, c_ref[0], i, 0)),
                      pl.BlockSpec((N_CHIP, tr, 128), lambda i, c_ref: (0, i, 0))],
            out_specs=pl.BlockSpec((N_CHIP, tr, 128), lambda i, c_ref: (0, i, 0))),
        compiler_params=_params(("parallel",)),
    )(c, x, got)


def _chip_all_to_all_rider(x):
    def parts(refs):
        x_ref, o_ref, send_sems, recv_sems, local_sem = refs
        me, _, chips = _place()
        sems = (send_sems, recv_sems)
        mine = 2 * me[0] + me[1]
        local = pltpu.make_async_copy(x_ref.at[mine], o_ref.at[mine], local_sem)
        sends = [_remote(x_ref.at[2 * px + py], o_ref.at[mine], sems, j, (px, py, me[2])) for j, (px, py) in enumerate(chips)]
        arrival = lambda j: _remote(x_ref.at[mine], o_ref.at[2 * chips[j][0] + chips[j][1]], sems, j, me)
        return local, sends, arrival

    def start(refs):
        local, sends, _ = parts(refs)
        for cp in [local] + sends:
            cp.start()

    def finish(refs):
        local, sends, arrival = parts(refs)
        for j, cp in enumerate(sends):
            cp.wait_send()
            arrival(j).wait_recv()
        local.wait()

    return _Rider([x], [jax.ShapeDtypeStruct(x.shape, x.dtype)],
                  [pltpu.SemaphoreType.DMA((N_CHIP - 1,)), pltpu.SemaphoreType.DMA((N_CHIP - 1,)), pltpu.SemaphoreType.DMA],
                  start, finish, lambda outs: outs[0])


def _sum_slots(x, *, name):
    n, R, _ = x.shape
    tr = _pick(R, _SLOT_ROWS)

    def body(x_ref, o_ref):
        acc = x_ref[0].astype(F32)
        for j in range(1, n):
            acc = acc + x_ref[j].astype(F32)
        o_ref[...] = acc

    return pl.pallas_call(
        body, name=name, out_shape=jax.ShapeDtypeStruct((R, 128), F32), grid=(R // tr,),
        in_specs=[pl.BlockSpec((n, tr, 128), lambda i: (0, i, 0))], out_specs=pl.BlockSpec((tr, 128), lambda i: (i, 0)),
        compiler_params=_params(("parallel",)),
    )(x)


def _adamw(w, g, m, v, *, name):
    shape = w.shape
    cols = shape[-1]
    rows = int(np.prod(shape[:-1]))
    tr = next((t for t in (1024, 512, 256, 128, 64, 32, 16, 8) if rows % t == 0 and t * cols * 4 <= (1 << 20)), rows)

    def body(w_ref, g_ref, m_ref, v_ref, d_ref, mo_ref, vo_ref):
        g_ = g_ref[...]
        m_ = ADAM_B1 * m_ref[...] + (1.0 - ADAM_B1) * g_
        v_ = ADAM_B2 * v_ref[...] + (1.0 - ADAM_B2) * jnp.square(g_)
        m_hat = m_ / (1.0 - ADAM_B1 ** ADAM_STEP)
        v_hat = v_ / (1.0 - ADAM_B2 ** ADAM_STEP)
        d_ref[...] = -ADAM_LR * (m_hat / (jnp.sqrt(v_hat) + ADAM_EPS) + ADAM_WD * w_ref[...])
        mo_ref[...] = m_
        vo_ref[...] = v_

    blk = pl.BlockSpec((tr, cols), lambda i: (i, 0))
    outs = pl.pallas_call(
        body, name=name, out_shape=tuple(jax.ShapeDtypeStruct((rows, cols), F32) for _ in range(3)), grid=(rows // tr,),
        in_specs=[blk] * 4, out_specs=(blk,) * 3, compiler_params=_params(("parallel",)),
    )(*(a.reshape(rows, cols) for a in (w, g, m, v)))
    return tuple(o.reshape(shape) for o in outs)


_WEIGHTS = ('g_mix', 'w_in', 'b_fox_forget', 'w_gla_gate', 'b_gla_gate', 'g_gla_out', 'g_mla_q', 'w_mla_uq', 'g_mla_kv',
            'w_mla_ukv', 'b_branch_gate', 'w_up_fox', 'w_up_gla', 'w_up_mla', 'w_out', 'g_xa', 'g_mem', 'w_xq', 'w_xkv',
            'w_xo', 'g_mlp', 'w_mlp1', 'w_mlp2', 'g_final')
_SHARDED = (('w_in', 1), ('w_gla_gate', 2), ('w_mla_uq', 2), ('w_mla_ukv', 2), ('w_up_fox', 2), ('w_up_gla', 2),
            ('w_up_mla', 2), ('w_out', 1), ('w_xq', 1), ('w_xkv', 1), ('w_xo', 2), ('w_mlp1', 2), ('w_mlp2', 1))
_REPLICATED = tuple(n for n in _WEIGHTS if n not in dict(_SHARDED))
_ROW_PAD = 1024
_SMALL_ROW_PAD = 8
_PIECE_ROWS = 16


def _pack(flats, lead, row_pad=_ROW_PAD):
    if all(int(np.prod(a.shape[lead:])) % 128 == 0 for a in flats):
        def block(a):
            a = a.reshape(a.shape[:lead] + (-1, 128))
            return jnp.pad(a, [(0, 0)] * lead + [(0, -a.shape[lead] % _PIECE_ROWS), (0, 0)])
        cat = jnp.concatenate([block(a) for a in flats], axis=lead)
        rows = cat.shape[lead]
        return jnp.pad(cat, [(0, 0)] * lead + [(0, -(-rows // row_pad) * row_pad - rows), (0, 0)])
    cat = jnp.concatenate([a.reshape(a.shape[:lead] + (-1,)) for a in flats], axis=-1)
    n = cat.shape[-1]
    total = -(-n // (128 * row_pad)) * (128 * row_pad)
    cat = jnp.pad(cat, [(0, 0)] * lead + [(0, total - n)])
    return cat.reshape(cat.shape[:lead] + (total // 128, 128))


def _unpack(buf, shapes, lead):
    sizes = [int(np.prod(shp)) for shp in shapes]
    out, off = [], 0
    if all(n % 128 == 0 for n in sizes):
        for shp, n in zip(shapes, sizes):
            rows = buf[(slice(None),) * lead + (slice(off, off + n // 128),)]
            out.append(rows.reshape(buf.shape[:lead] + tuple(shp)))
            off += -(-(n // 128) // _PIECE_ROWS) * _PIECE_ROWS
        return out
    flat = buf.reshape(buf.shape[:lead] + (-1,))
    for shp, n in zip(shapes, sizes):
        out.append(flat[..., off:off + n].reshape(buf.shape[:lead] + tuple(shp)))
        off += n
    return out


def _to_whole(g, axis):
    if axis == 1:
        return g.transpose(1, 0, 2, 3).reshape(g.shape[1], N_DEV * g.shape[2], g.shape[3])
    return g.transpose(1, 2, 0, 3).reshape(g.shape[1], g.shape[2], N_DEV * g.shape[3])


def _to_shards(w, axis):
    L, R, C = w.shape
    if axis == 1:
        return w.reshape(L, N_DEV, R // N_DEV, C).transpose(1, 0, 2, 3)
    return w.reshape(L, R, N_DEV, C // N_DEV).transpose(2, 0, 1, 3)


def kernel(x, mem, g_mix, w_in, b_fox_forget, w_gla_gate, b_gla_gate, g_gla_out, g_mla_q, w_mla_uq, g_mla_kv, w_mla_ukv, b_branch_gate, w_up_fox, w_up_gla, w_up_mla, w_out, g_xa, g_mem, w_xq, w_xkv, w_xo, g_mlp, w_mlp1, w_mlp2, g_final, loss_target, m_g_mix, m_w_in, m_b_fox_forget, m_w_gla_gate, m_b_gla_gate, m_g_gla_out, m_g_mla_q, m_w_mla_uq, m_g_mla_kv, m_w_mla_ukv, m_b_branch_gate, m_w_up_fox, m_w_up_gla, m_w_up_mla, m_w_out, m_g_xa, m_g_mem, m_w_xq, m_w_xkv, m_w_xo, m_g_mlp, m_w_mlp1, m_w_mlp2, m_g_final, v_g_mix, v_w_in, v_b_fox_forget, v_w_gla_gate, v_b_gla_gate, v_g_gla_out, v_g_mla_q, v_w_mla_uq, v_g_mla_kv, v_w_mla_ukv, v_b_branch_gate, v_w_up_fox, v_w_up_gla, v_w_up_mla, v_w_out, v_g_xa, v_g_mem, v_w_xq, v_w_xkv, v_w_xo, v_g_mlp, v_w_mlp1, v_w_mlp2, v_g_final):
    wts = dict(zip(_WEIGHTS, (g_mix, w_in, b_fox_forget, w_gla_gate, b_gla_gate, g_gla_out, g_mla_q, w_mla_uq, g_mla_kv,
                              w_mla_ukv, b_branch_gate, w_up_fox, w_up_gla, w_up_mla, w_out, g_xa, g_mem, w_xq, w_xkv, w_xo,
                              g_mlp, w_mlp1, w_mlp2, g_final)))
    mom1 = dict(zip(_WEIGHTS, (m_g_mix, m_w_in, m_b_fox_forget, m_w_gla_gate, m_b_gla_gate, m_g_gla_out, m_g_mla_q,
                               m_w_mla_uq, m_g_mla_kv, m_w_mla_ukv, m_b_branch_gate, m_w_up_fox, m_w_up_gla, m_w_up_mla,
                               m_w_out, m_g_xa, m_g_mem, m_w_xq, m_w_xkv, m_w_xo, m_g_mlp, m_w_mlp1, m_w_mlp2, m_g_final)))
    mom2 = dict(zip(_WEIGHTS, (v_g_mix, v_w_in, v_b_fox_forget, v_w_gla_gate, v_b_gla_gate, v_g_gla_out, v_g_mla_q,
                               v_w_mla_uq, v_g_mla_kv, v_w_mla_ukv, v_b_branch_gate, v_w_up_fox, v_w_up_gla, v_w_up_mla,
                               v_w_out, v_g_xa, v_g_mem, v_w_xq, v_w_xkv, v_w_xo, v_g_mlp, v_w_mlp1, v_w_mlp2, v_g_final)))
    depth = g_mix.shape[0]

    names = [n for n, _ in _SHARDED]
    axes = dict(_SHARDED)
    shard = {n: wts[n] for n in names}
    shard['w_in'] = _pad_w_in(w_in)
    rep = {n: wts[n] for n in _REPLICATED}
    ps = [_LayerParams(rep, l) for l in range(depth)]

    def gather(group, l):
        rider = _gather_rider([shard[n][l:l + 1].astype(BF16) for n in group], [axes[n] for n in group])
        return rider, lambda whole: ps[l].w.update({n: w[0] for n, w in zip(group, whole)})

    first, sink = gather(['w_in'], 0)
    sink(_run_rider(first, name="gather_w_in_0"))
    narrow = ['w_gla_gate', 'w_mla_uq', 'w_mla_ukv', 'w_up_fox', 'w_up_gla', 'w_up_mla']
    hooks = {(0, 'in_big'): [gather(narrow + ['w_out', 'w_xq', 'w_xkv', 'w_xo'], 0)],
             (0, 'fox_fwd'): [gather(['w_mlp1', 'w_mlp2'], 0)]}
    ahead = (('mla_fwd', ['w_in'] + narrow + ['w_out']), ('xo', ['w_xkv']), ('mlp1', ['w_mlp1']),
             ('mlp2', ['w_mlp2', 'w_xq', 'w_xo']))
    assert sorted(n for _, group in ahead for n in group) == sorted(names)
    for l in range(1, depth):
        for key, group in ahead:
            hooks.setdefault((l - 1, key), []).append(gather(group, l))

    core = lax.axis_index("c").astype(jnp.int32).reshape(1)
    late = ['w_in', 'w_gla_gate', 'w_mla_uq', 'w_mla_ukv']
    groups = {'early': [n for n in names if n not in late], 'late': late}
    small_grads, landed = {}, {}

    def to_slots(gl, axis):
        if gl.ndim != 3:
            return _to_shards(gl[None], axis)
        blocks, rows, _ = gl.shape
        if axis == 1:
            return gl.reshape(blocks, N_DEV, rows // N_DEV, LANES).transpose(1, 0, 2, 3)
        return gl.reshape(N_DEV, blocks // N_DEV, rows, LANES)

    def slot_shape(n):
        _, a, b = shard[n].shape
        return (b // LANES, a, LANES) if n in blocked else (1, a, b)

    def from_slot(n, x):
        return x.transpose(1, 0, 2).reshape((1,) + shard[n].shape[1:]) if n in blocked else x

    blocked = {'w_in', 'w_out', 'w_xq', 'w_xkv', 'w_xo', 'w_mlp1', 'w_mlp2'}

    def ride(key, rider, sink, name):
        if key is None:
            sink(_run_rider(rider, name=name))
        else:
            hooks.setdefault(key, []).append((rider, sink))

    def exchange(l, g, which, swap_in, scatter_in):
        assert all((g[n].ndim == 3) == (n in blocked) for n in groups[which])
        slots = _pack([to_slots(g[n], axes[n]).astype(BF16) for n in groups[which]], 1)
        slots = slots.reshape((N_CHIP, 2) + slots.shape[1:])

        def swapped(got):
            paired = _pair_sum(slots, got, core, name=f"pair_grads_{which}_{l}")
            ride(scatter_in, _chip_all_to_all_rider(paired), lambda landing: landed.update({(l, which): landing}),
                 f"scatter_grads_{which}_{l}")

        ride(swap_in, _sibling_swap_rider(slots), swapped, f"swap_grads_{which}_{l}")

    def half_done(l, g):
        exchange(l, g, 'early', (l, 'fox_bwd'), (l, 'mla_bwd'))

    def matrices_done(l, g):
        if l > 0:
            exchange(l, g, 'late', (l - 1, 'xa_bwd'), (l - 1, 'fox_bwd'))
        else:
            exchange(l, g, 'late', None, (l, 'd_in'))

    def layer_done(l, g):
        small_grads[l] = g

    loss, dx, dg_final = _local_step(x[0], mem[0], loss_target[0], ps, g_final, hooks, half_done, matrices_done, layer_done)
    loss = lax.psum(loss[0, 0], _MESH_AXES)

    grad = {}
    for which, group in groups.items():
        shapes = [slot_shape(n) for n in group]
        per_layer = [_unpack(_sum_slots(landed[(l, which)], name=f"sum_grads_{which}_{l}"), shapes, 0) for l in range(depth)]
        grad.update({n: jnp.concatenate([from_slot(n, per_layer[l][i]) for l in range(depth)], axis=0)
                     for i, n in enumerate(group)})
    grad['w_in'] = _unpad_w_in(grad['w_in'])
    grads = small_grads
    small = [dg_final if n == 'g_final' else jnp.stack([grads[l][n] for l in range(depth)]) for n in _REPLICATED]
    small_shapes = [wts[n].shape for n in _REPLICATED]
    small_sum = _sum_slots(_all_gather(_pack(small, 0, _SMALL_ROW_PAD), name="gather_small_grads"), name="sum_small_grads")
    grad.update(dict(zip(_REPLICATED, _unpack(small_sum, small_shapes, 0))))

    delta, new_m, new_v = {}, {}, {}
    for n, _ in _SHARDED:
        delta[n], new_m[n], new_v[n] = _adamw(wts[n], grad[n], mom1[n], mom2[n], name=f"adamw_{n}")
    packed = [_pack([d[n] for n in _REPLICATED], 0, _SMALL_ROW_PAD) for d in (wts, mom1, mom2)]
    outs = _adamw(packed[0], small_sum, packed[1], packed[2], name="adamw_small")
    for d, o in zip((delta, new_m, new_v), outs):
        d.update(dict(zip(_REPLICATED, _unpack(o, small_shapes, 0))))

    return (loss, dx[None], *[grad[n] for n in _WEIGHTS], *[delta[n] for n in _WEIGHTS],
            *[new_m[n] for n in _WEIGHTS], *[new_v[n] for n in _WEIGHTS])
```

```python
import functools

import jax
import jax.numpy as jnp
import numpy as np
from jax import lax
from jax.experimental import pallas as pl
from jax.experimental.pallas import tpu as pltpu

F32 = jnp.float32
BF16 = jnp.bfloat16

EPS = 1e-6
CHUNK = 64
FOX_HEADS, FOX_HD = 4, 64
GLA_HEADS, GLA_DK, GLA_DV, GLA_RANK, GLA_TAU = 4, 64, 128, 16, 16.0
MLA_HEADS, MLA_Q_RANK, MLA_KV_RANK, MLA_NOPE, MLA_ROPE, MLA_VD = 4, 256, 128, 64, 32, 64
ROPE_BASE = 10000.0
XA_HEADS, XA_HD = 4, 128
ADAM_LR, ADAM_B1, ADAM_B2, ADAM_EPS, ADAM_WD, ADAM_STEP = 0.001, 0.9, 0.999, 1e-08, 0.01, 10

N_DEV = 8
V7X_VMEM_LIMIT = 56 * 1024 * 1024
NEG = -1e30

NN = ((1,), (0,))
NT = ((1,), (1,))
TN = ((0,), (0,))


def _dot(a, b, dims):
    return lax.dot_general(a.astype(BF16), b.astype(BF16), (dims, ((), ())), preferred_element_type=F32)


@jax.custom_vjp
def bdot(a, b):
    return _dot(a, b, NN)


bdot.defvjp(lambda a, b: (_dot(a, b, NN), (a, b)),
            lambda res, g: (_dot(g, res[1], NT), _dot(res[0], g, TN)))


@jax.custom_vjp
def bdot_nt(a, b):
    return _dot(a, b, NT)


bdot_nt.defvjp(lambda a, b: (_dot(a, b, NT), (a, b)),
               lambda res, g: (_dot(g, res[1], NN), _dot(g, res[0], TN)))


@jax.custom_vjp
def bdot_tn(a, b):
    return _dot(a, b, TN)


bdot_tn.defvjp(lambda a, b: (_dot(a, b, TN), (a, b)),
               lambda res, g: (_dot(res[1], g, NT), _dot(res[0], g, NN)))


def _split2(x):
    hi = x.astype(BF16)
    lo = (x - hi.astype(F32)).astype(BF16)
    return hi, lo


def _tri(n, lower):
    r = lax.broadcasted_iota(jnp.int32, (n, n), 0)
    c = lax.broadcasted_iota(jnp.int32, (n, n), 1)
    return jnp.where((r >= c) if lower else (r <= c), 1.0, 0.0).astype(BF16)


def _log_sigmoid(x):
    return jnp.minimum(x, 0.0) - jnp.log(1.0 + jnp.exp(-jnp.abs(x)))


def _sigmoid(x):
    return 1.0 / (1.0 + jnp.exp(-x))


def _rms(x, g):
    return x * lax.rsqrt(jnp.mean(x * x, axis=-1, keepdims=True) + EPS) * g


def _pick(dim, prefs):
    for p in prefs:
        if dim % p == 0:
            return p
    return dim


def _params(sem):
    return pltpu.CompilerParams(dimension_semantics=sem, vmem_limit_bytes=V7X_VMEM_LIMIT)


def _rms_vjp(x, g, dy, dres):
    rstd = lax.rsqrt(jnp.mean(x * x, axis=-1, keepdims=True) + EPS)
    xh = x * rstd
    gdy = dy * g
    dx = (gdy - xh * jnp.mean(gdy * xh, axis=-1, keepdims=True)) * rstd
    return (dx if dres is None else dx + dres), jnp.sum(dy * xh, axis=0, keepdims=True)


def _mm(a, b, *, mode, out_dtype, name, act=None, residual=None, drelu_of=None, norm_bwd=None, b_cols=None,
        col_shards=None, rider=None, tm=None, tn=None, tk=None):
    b_off, b_width = b_cols or (0, b.shape[1])
    if mode == 'nn':
        (M, K), N = a.shape, b_width
    elif mode == 'nt':
        (M, K), N = a.shape, b.shape[0]
    else:
        (K, M), N = a.shape, b_width
    tm = tm or _pick(M, (1024, 512, 256, 128))
    tn = tn or _pick(N, (1024, 1920, 1152, 768, 640, 512, 384, 256, 128))
    tk = tk or _pick(K, (1024, 1920, 1152, 640, 512, 256, 128))
    nk = K // tk
    dims = {'nn': NN, 'nt': NT, 'tn': TN}[mode]
    a_spec = pl.BlockSpec((tk, tm), lambda i, j, k: (k, i)) if mode == 'tn' else pl.BlockSpec((tm, tk), lambda i, j, k: (i, k))
    if mode == 'nt':
        b_spec = pl.BlockSpec((tn, tk), lambda i, j, k, o=b_off // tk: (j, k + o))
    else:
        b_spec = pl.BlockSpec((tk, tn), lambda i, j, k, o=b_off // tn: (k, j + o))
    o_spec = pl.BlockSpec((tm, tn), lambda i, j, k: (i, j))
    extra = [e for e in (residual, drelu_of) if e is not None]
    extra_specs = [o_spec] * len(extra)
    out_shape, out_specs, n_out = jax.ShapeDtypeStruct((M, N), out_dtype), o_spec, 1
    if col_shards:
        n_sh = N // col_shards
        assert tn % n_sh == 0 and not extra and norm_bwd is None
        out_shape = jax.ShapeDtypeStruct((col_shards, M, n_sh), out_dtype)
        out_specs = pl.BlockSpec((tn // n_sh, tm, n_sh), lambda i, j, k: (j, i, 0))
    if norm_bwd is not None:
        x_in, g_in, dres_in = norm_bwd
        assert tn == N and residual is None and drelu_of is None
        vec = pl.BlockSpec((1, N), lambda i, j, k: (0, 0))
        extra, extra_specs = [x_in, g_in.reshape(1, N), dres_in], [o_spec, vec, o_spec]
        out_shape = (jax.ShapeDtypeStruct((M, N), F32), jax.ShapeDtypeStruct((M, N), BF16), jax.ShapeDtypeStruct((1, N), F32))
        out_specs, n_out = (o_spec, o_spec, vec), 3

    grid = (M // tm, N // tn, nk)
    r_ins, r_in_specs, r_outs, r_out_specs, r_scratch, split = _carry(
        rider, 2 + len(extra), n_out, lambda: functools.reduce(jnp.logical_and, [pl.program_id(d) == 0 for d in range(3)]),
        lambda: functools.reduce(jnp.logical_and, [pl.program_id(d) == grid[d] - 1 for d in range(3)]))

    def body(*refs):
        a_ref, b_ref, *rest = split(refs)
        o_ref = rest[len(extra)]
        first_rows = pl.program_id(0) == 0
        at = a_ref[...]
        if act == 'relu2':
            at = jnp.square(jnp.maximum(at.astype(F32), 0.0))
        part = _dot(at, b_ref[...], dims)

        def finish(acc):
            if norm_bwd is not None:
                dx, dg = _rms_vjp(rest[0][...], rest[1][...], acc, rest[2][...])
                o_ref[...] = dx
                rest[len(extra) + 1][...] = dx.astype(BF16)
                dg_ref = rest[len(extra) + 2]

                @pl.when(first_rows)
                def _():
                    dg_ref[...] = dg

                @pl.when(jnp.logical_not(first_rows))
                def _():
                    dg_ref[...] += dg
                return
            idx = 0
            if residual is not None:
                acc = acc + rest[idx][...]
                idx += 1
            if drelu_of is not None:
                acc = acc * (2.0 * jnp.maximum(rest[idx][...].astype(F32), 0.0))
            if col_shards:
                for t in range(tn // n_sh):
                    o_ref[t] = acc[:, t * n_sh:(t + 1) * n_sh].astype(out_dtype)
            else:
                o_ref[...] = acc.astype(out_dtype)

        if nk == 1:
            finish(part)
        else:
            acc_ref = rest[len(extra) + n_out]
            k = pl.program_id(2)

            @pl.when(k == 0)
            def _():
                acc_ref[...] = part

            @pl.when(k > 0)
            def _():
                acc_ref[...] += part

            @pl.when(k == nk - 1)
            def _():
                finish(acc_ref[...])

    scratch = [] if nk == 1 else [pltpu.VMEM((tm, tn), F32)]
    if rider is not None:
        own_shapes, own_specs = (out_shape, out_specs) if n_out > 1 else ((out_shape,), (out_specs,))
        res = pl.pallas_call(
            body, name=name, out_shape=(*own_shapes, *r_outs), grid=grid, in_specs=[a_spec, b_spec] + extra_specs + r_in_specs,
            out_specs=(*own_specs, *r_out_specs), scratch_shapes=scratch + r_scratch,
            compiler_params=_params(("arbitrary", "arbitrary", "arbitrary")),
        )(a, b, *extra, *r_ins)
        return (*res[:n_out], rider.post(res[n_out:]))
    return pl.pallas_call(
        body, name=name, out_shape=out_shape, grid=grid, in_specs=[a_spec, b_spec] + extra_specs, out_specs=out_specs,
        scratch_shapes=scratch,
        compiler_params=_params(("arbitrary" if norm_bwd is not None else "parallel", "parallel", "arbitrary")),
    )(a, b, *extra)


def _rms_fwd(x, g, *, name, out_dtype=BF16):
    S, D = x.shape
    tr = _pick(S, (512, 256, 128))

    def body(x_ref, g_ref, o_ref):
        o_ref[...] = _rms(x_ref[...], g_ref[...]).astype(out_dtype)

    return pl.pallas_call(
        body, name=name, out_shape=jax.ShapeDtypeStruct((S, D), out_dtype), grid=(S // tr,),
        in_specs=[pl.BlockSpec((tr, D), lambda i: (i, 0)), pl.BlockSpec((1, D), lambda i: (0, 0))],
        out_specs=pl.BlockSpec((tr, D), lambda i: (i, 0)),
        compiler_params=_params(("parallel",)),
    )(x, g.reshape(1, D))


def _rms_bwd(x, g, dy, dres, *, name):
    S, D = x.shape
    tr = _pick(S, (512, 256, 128))

    def body(x_ref, g_ref, dy_ref, *rest):
        dx_ref, dxb_ref, dg_ref = rest[-3], rest[-2], rest[-1]
        dx, part = _rms_vjp(x_ref[...], g_ref[...], dy_ref[...].astype(F32), None if dres is None else rest[0][...])
        dx_ref[...] = dx
        dxb_ref[...] = dx.astype(BF16)

        @pl.when(pl.program_id(0) == 0)
        def _():
            dg_ref[...] = part

        @pl.when(pl.program_id(0) > 0)
        def _():
            dg_ref[...] += part

    row = pl.BlockSpec((tr, D), lambda i: (i, 0))
    vec = pl.BlockSpec((1, D), lambda i: (0, 0))
    ins = [x, g.reshape(1, D), dy] + ([dres] if dres is not None else [])
    return pl.pallas_call(
        body, name=name,
        out_shape=(jax.ShapeDtypeStruct((S, D), F32), jax.ShapeDtypeStruct((S, D), BF16), jax.ShapeDtypeStruct((1, D), F32)),
        grid=(S // tr,),
        in_specs=[row, vec, row] + ([row] if dres is not None else []),
        out_specs=(row, row, vec),
        compiler_params=_params(("arbitrary",)),
    )(*ins)


def _loss_head(x, g, target, *, name):
    S, D = x.shape
    tr = _pick(S, (512, 256, 128))

    def body(x_ref, g_ref, t_ref, l_ref, dx_ref, dxb_ref, dg_ref):
        x_ = x_ref[...]
        g_ = g_ref[...]
        rstd = lax.rsqrt(jnp.mean(x_ * x_, axis=-1, keepdims=True) + EPS)
        xh = x_ * rstd
        err = xh * g_ - t_ref[...]
        lpart = (0.5 / D) * jnp.sum(jnp.sum(err * err, axis=-1, keepdims=True), axis=0, keepdims=True)
        dy = err * (1.0 / D)
        gdy = dy * g_
        dx = (gdy - xh * jnp.mean(gdy * xh, axis=-1, keepdims=True)) * rstd
        dx_ref[...] = dx
        dxb_ref[...] = dx.astype(BF16)
        gpart = jnp.sum(dy * xh, axis=0, keepdims=True)

        @pl.when(pl.program_id(0) == 0)
        def _():
            dg_ref[...] = gpart
            l_ref[...] = lpart

        @pl.when(pl.program_id(0) > 0)
        def _():
            dg_ref[...] += gpart
            l_ref[...] += lpart

    row = pl.BlockSpec((tr, D), lambda i: (i, 0))
    vec = pl.BlockSpec((1, D), lambda i: (0, 0))
    return pl.pallas_call(
        body, name=name,
        out_shape=(jax.ShapeDtypeStruct((1, 1), F32), jax.ShapeDtypeStruct((S, D), F32), jax.ShapeDtypeStruct((S, D), BF16),
                   jax.ShapeDtypeStruct((1, D), F32)),
        grid=(S // tr,),
        in_specs=[row, vec, row],
        out_specs=(pl.BlockSpec((1, 1), lambda i: (0, 0)), row, row, vec),
        compiler_params=_params(("arbitrary",)),
    )(x, g.reshape(1, D), target)


def _mask_of(mask, tq, tk, keys_first=False):
    shape, q_axis = ((tk, tq), 1) if keys_first else ((tq, tk), 0)
    qpos = lax.broadcasted_iota(jnp.int32, shape, q_axis)
    kpos = lax.broadcasted_iota(jnp.int32, shape, 1 - q_axis)
    if mask == 'causal':
        return kpos <= qpos
    return kpos <= (qpos | (CHUNK - 1))


LANES = 128
LOG2E = 1.4426950408889634


def _lane_group(j, w, width):
    lane = lax.broadcasted_iota(jnp.int32, (1, width), 1)
    return (lane >= j * w) & (lane < (j + 1) * w)


def _only(x, j, w):
    if w == x.shape[1]:
        return x
    return jnp.where(_lane_group(j, w, x.shape[1]), x, jnp.zeros_like(x))


def _side_by_side(xs):
    return xs[0] if len(xs) == 1 else jnp.concatenate(xs, axis=1)


def _on_top(xs):
    return xs[0] if len(xs) == 1 else jnp.concatenate(xs, axis=0)


def _stacked(x, hp, w):
    return _on_top([_only(x, j, w) for j in range(hp)])


def _col_block(entry, rows, idx):
    arr, off, width = entry
    return pl.BlockSpec((rows, width), lambda i, j, o=off // width: (idx(i, j), o))


def _attn_fwd(qk, v, H, cq, ck, *, scale, mask, name, rider=None):
    Sq, Sk = qk[0][0][0].shape[0], v[0].shape[0]
    dv = v[2] // H
    w0 = qk[0][2]
    hp = LANES // w0
    G = H // hp
    assert dv == w0 and not qk[0][3] and all(sh and H * w == LANES for _, _, w, sh in qk[1:])
    tq = _pick(Sq, (512, 256, 128))
    tk = tq if mask else _pick(Sk, (512, 256, 128))
    nq, nk = Sq // tq, Sk // tk
    bias = cq is not None
    npart = len(qk)

    def body(*refs):
        refs = split(refs)
        q_refs, k_refs = refs[0:2 * npart:2], refs[1:2 * npart:2]
        v_ref = refs[2 * npart]
        cq_ref, ck_ref = (refs[2 * npart + 1], refs[2 * npart + 2]) if bias else (None, None)
        o_ref, lse_ref, m_s, l_s, acc_s = refs[-5:]
        qi, ki = pl.program_id(0), pl.program_id(1)

        @pl.when(ki == 0)
        def _():
            m_s[...] = jnp.full(m_s.shape, NEG, F32)
            l_s[...] = jnp.zeros(l_s.shape, F32)
            acc_s[...] = jnp.zeros(acc_s.shape, F32)

        def rows_of(vals):
            return _on_top([jnp.broadcast_to(r, (w0, tq)) for r in vals])

        def compute(masked):
            keep = _mask_of(mask, tq, tk, keys_first=True) if masked else None
            for g in range(G):
                lanes = slice(g * LANES, (g + 1) * LANES)
                q128, k128, v128 = q_refs[0][:, lanes], k_refs[0][:, lanes], v_ref[:, lanes]
                ps, alphas = [], []
                extras = list(zip(qk, q_refs, k_refs))[1:]
                k_all = _side_by_side([k128] + [k_ref[...] for _, _, k_ref in extras])
                for j in range(hp):
                    h = g * hp + j
                    q_all = _side_by_side([_only(q128, j, w0)] + [_only(q_ref[...], h, w) for (_, _, w, _), q_ref, _ in extras])
                    s = _dot(k_all, q_all, NT) * scale
                    if bias:
                        s = s + (cq_ref[h:h + 1, :] - ck_ref[:, h:h + 1])
                    if masked:
                        s = jnp.where(keep, s, NEG)
                    m_prev = m_s[h:h + 1, :]
                    m_new = jnp.maximum(m_prev, jnp.max(s, axis=0, keepdims=True))
                    alpha = jnp.exp(m_prev - m_new)
                    p = jnp.exp(s - m_new)
                    l_s[h:h + 1, :] = alpha * l_s[h:h + 1, :] + jnp.sum(p, axis=0, keepdims=True)
                    m_s[h:h + 1, :] = m_new
                    ps.append(p.astype(BF16))
                    alphas.append(alpha)
                acc_s[g] = rows_of(alphas) * acc_s[g] + _dot(_stacked(v128, hp, w0), _on_top(ps), TN)

        if mask is None:
            compute(False)
        else:
            pl.when(ki < qi)(lambda: compute(False))
            pl.when(ki == qi)(lambda: compute(True))

        @pl.when(ki == ((nk - 1) if mask is None else qi))
        def _():
            for g in range(G):
                norm = acc_s[g] / rows_of([l_s[g * hp + j:g * hp + j + 1, :] for j in range(hp)])
                o_ref[:, g * LANES:(g + 1) * LANES] = norm.T.astype(BF16)
            lse_ref[...] = jnp.zeros(lse_ref.shape, F32)
            lse_ref[0:H, :] = m_s[0:H, :] + jnp.log(l_s[0:H, :])

    q_idx = lambda i, j: i
    k_idx = (lambda i, j: jnp.minimum(i, j)) if mask else (lambda i, j: j)
    ins, in_specs = [], []
    for q_e, k_e, _, _ in qk:
        ins += [q_e[0], k_e[0]]
        in_specs += [_col_block(q_e, tq, q_idx), _col_block(k_e, tk, k_idx)]
    ins.append(v[0])
    in_specs.append(_col_block(v, tk, k_idx))
    if bias:
        in_specs += [pl.BlockSpec((8, tq), lambda i, j: (0, i)), pl.BlockSpec((tk, 8), lambda i, j: (k_idx(i, j), 0))]
        ins += [cq, ck]
    r_ins, r_in_specs, r_outs, r_out_specs, r_scratch, split = _carry(
        rider, len(ins), 2, lambda: (pl.program_id(0) == 0) & (pl.program_id(1) == 0),
        lambda: (pl.program_id(0) == nq - 1) & (pl.program_id(1) == nk - 1))
    res = pl.pallas_call(
        body, name=name,
        out_shape=(jax.ShapeDtypeStruct((Sq, H * dv), BF16), jax.ShapeDtypeStruct((8, Sq), F32), *r_outs),
        grid=(nq, nk), in_specs=in_specs + r_in_specs,
        out_specs=(pl.BlockSpec((tq, H * dv), lambda i, j: (i, 0)), pl.BlockSpec((8, tq), lambda i, j: (0, i)), *r_out_specs),
        scratch_shapes=[pltpu.VMEM((8, tq), F32), pltpu.VMEM((8, tq), F32), pltpu.VMEM((G, LANES, tq), F32)] + r_scratch,
        compiler_params=_params(("arbitrary", "arbitrary")) if rider else _params(("parallel", "arbitrary")),
    )(*ins, *r_ins)
    return (res[0], res[1], rider.post(res[2:])) if rider else res


def _attn_bwd(qk, v, H, o, do, lse, cq, ck, *, scale, mask, name, rider=None):
    Sq, Sk = qk[0][0][0].shape[0], v[0].shape[0]
    dv = v[2] // H
    w0 = qk[0][2]
    hp = LANES // w0
    G = H // hp
    tq = _pick(Sq, (512, 256, 128))
    tk = tq if mask else _pick(Sk, (512, 256, 128))
    nq, nk = Sq // tq, Sk // tk
    bias = cq is not None
    npart = len(qk)
    n_in = 2 * npart + 4 + (2 if bias else 0)

    def body(*refs):
        refs = split(refs)
        q_refs, k_refs = refs[0:2 * npart:2], refs[1:2 * npart:2]
        v_ref, o_ref, do_ref, lse_ref = refs[2 * npart:2 * npart + 4]
        cq_ref, ck_ref = (refs[2 * npart + 4], refs[2 * npart + 5]) if bias else (None, None)
        outs = refs[n_in:]
        dq_refs, dk_refs, dv_ref = outs[:npart], outs[npart:2 * npart], outs[2 * npart]
        dck_ref, dcq_ref = (outs[2 * npart + 1], outs[2 * npart + 2]) if bias else (None, None)
        dk_accs, dv_acc = refs[-(npart + 1):-1], refs[-1]
        ki, qi = pl.program_id(0), pl.program_id(1)
        first_q = ki if mask else 0

        @pl.when((ki == 0) & (qi == 0))
        def _():
            for r in dq_refs:
                r[...] = jnp.zeros(r.shape, F32)
            if bias:
                dcq_ref[...] = jnp.zeros(dcq_ref.shape, F32)

        @pl.when(qi == first_q)
        def _():
            for r in dk_accs:
                r[...] = jnp.zeros(r.shape, F32)
            dv_acc[...] = jnp.zeros(dv_acc.shape, F32)
            if bias:
                dck_ref[...] = jnp.zeros(dck_ref.shape, F32)

        def compute(masked):
            keep = _mask_of(mask, tq, tk, keys_first=True) if masked else None
            rows = pl.ds(pl.multiple_of(qi * tq, tq), tq)
            extras = list(zip(qk, q_refs, k_refs, dq_refs, dk_accs))[1:]
            for g in range(G):
                lanes = slice(g * LANES, (g + 1) * LANES)
                q128, k128, v128 = q_refs[0][:, lanes], k_refs[0][:, lanes], v_ref[:, lanes]
                do128, o128 = do_ref[:, lanes], o_ref[:, lanes]
                prod = do128.astype(F32) * o128.astype(F32)
                ps, dss = [], []
                k_all = _side_by_side([k128] + [e[2][...] for e in extras])
                for j in range(hp):
                    h = g * hp + j
                    q_all = _side_by_side([_only(q128, j, w0)] + [_only(e[1][...], h, e[0][2]) for e in extras])
                    s = _dot(k_all, q_all, NT) * (scale * LOG2E)
                    if bias:
                        s = s - ck_ref[:, h:h + 1] * LOG2E
                    if masked:
                        s = jnp.where(keep, s, NEG)
                    row = lse_ref[h:h + 1, :] - cq_ref[h:h + 1, :] if bias else lse_ref[h:h + 1, :]
                    p = jnp.exp2(s - row * LOG2E)
                    dp = _dot(v128, _only(do128, j, w0), NT)
                    delta = jnp.sum(_only(prod, j, w0), axis=1, keepdims=True).T
                    ds = p * (dp - delta)
                    if bias:
                        dck_ref[:, h:h + 1] -= jnp.sum(ds, axis=1, keepdims=True)
                        dcq_ref[h:h + 1, rows] += jnp.sum(ds, axis=0, keepdims=True)
                    ps.append(p.astype(BF16))
                    dss.append((ds * scale).astype(BF16))
                for (_, _, w, _), q_ref, k_ref, dq_ref, dk_acc in extras:
                    heads = range(g * hp, (g + 1) * hp)
                    dk_acc[...] += _dot(_side_by_side(dss), _on_top([_only(q_ref[...], h, w) for h in heads]), NN)
                    dq_ref[rows, :] += _dot(_on_top(dss), _on_top([_only(k_ref[...], h, w) for h in heads]), TN)
                dv_acc[:, lanes] += _dot(_side_by_side(ps), _stacked(do128, hp, w0), NN)
                dk_accs[0][:, lanes] += _dot(_side_by_side(dss), _stacked(q128, hp, w0), NN)
                dq_refs[0][rows, lanes] += _dot(_on_top(dss), _stacked(k128, hp, w0), TN)

        if mask is None:
            compute(False)
        else:
            pl.when(qi > ki)(lambda: compute(False))
            pl.when(qi == ki)(lambda: compute(True))

        @pl.when(qi == nq - 1)
        def _():
            for r, acc in zip(dk_refs, dk_accs):
                r[...] = acc[...]
            dv_ref[...] = dv_acc[...]

    q_idx = (lambda j, i: jnp.maximum(i, j)) if mask else (lambda j, i: i)
    k_idx = lambda j, i: j
    ins, in_specs, dq_shapes, dq_specs, dk_shapes, dk_specs, scratch = [], [], [], [], [], [], []
    for q_e, k_e, w, shared in qk:
        ins += [q_e[0], k_e[0]]
        in_specs += [_col_block(q_e, tq, q_idx), _col_block(k_e, tk, k_idx)]
        dq_shapes.append(jax.ShapeDtypeStruct((Sq, H * w), F32))
        dq_specs.append(pl.BlockSpec((Sq, H * w), lambda j, i: (0, 0)))
        kw = k_e[2]
        dk_shapes.append(jax.ShapeDtypeStruct((Sk, kw), F32))
        dk_specs.append(pl.BlockSpec((tk, kw), lambda j, i: (j, 0)))
        scratch.append(pltpu.VMEM((tk, kw), F32))
    row_q = lambda width: pl.BlockSpec((tq, width), lambda j, i: (q_idx(j, i), 0))
    per_q = pl.BlockSpec((8, tq), lambda j, i: (0, q_idx(j, i)))
    ins += [v[0], o, do, lse]
    in_specs += [_col_block(v, tk, k_idx), row_q(H * dv), row_q(H * dv), per_q]
    out_shape = dq_shapes + dk_shapes + [jax.ShapeDtypeStruct((Sk, H * dv), F32)]
    out_specs = dq_specs + dk_specs + [pl.BlockSpec((tk, H * dv), lambda j, i: (j, 0))]
    if bias:
        in_specs += [per_q, pl.BlockSpec((tk, 8), lambda j, i: (j, 0))]
        ins += [cq, ck]
        out_shape += [jax.ShapeDtypeStruct((Sk, 8), F32), jax.ShapeDtypeStruct((8, Sq), F32)]
        out_specs += [pl.BlockSpec((tk, 8), lambda j, i: (j, 0)), pl.BlockSpec((8, Sq), lambda j, i: (0, 0))]
    scratch.append(pltpu.VMEM((tk, H * dv), F32))
    n_out = len(out_shape)
    r_ins, r_in_specs, r_outs, r_out_specs, r_scratch, split = _carry(
        rider, len(ins), n_out, lambda: (pl.program_id(0) == 0) & (pl.program_id(1) == 0),
        lambda: (pl.program_id(0) == nk - 1) & (pl.program_id(1) == nq - 1))
    res = pl.pallas_call(
        body, name=name, out_shape=tuple(out_shape + r_outs), grid=(nk, nq), in_specs=in_specs + r_in_specs,
        out_specs=tuple(out_specs + r_out_specs), scratch_shapes=scratch + r_scratch,
        compiler_params=_params(("arbitrary", "arbitrary")),
    )(*ins, *r_ins)
    own = (list(res[:npart]), list(res[npart:2 * npart]), res[2 * npart]) + tuple(res[2 * npart + 1:n_out])
    return own + (rider.post(res[n_out:]),) if rider else own


def _split3_dot(x, t):
    hi = x.astype(BF16)
    r1 = x - hi.astype(F32)
    mid = r1.astype(BF16)
    lo = (r1 - mid.astype(F32)).astype(BF16)
    return _dot(hi, t, NN) + _dot(mid, t, NN) + _dot(lo, t, NN)


def _fox_cum_fwd(ff_t, b, *, name):
    _, S = ff_t.shape
    tb = _pick(S, (512, 256, 128))

    def body(f_ref, b_ref, o_ref, carry):
        @pl.when(pl.program_id(0) == 0)
        def _():
            carry[...] = jnp.zeros(carry.shape, F32)

        lf = _log_sigmoid(f_ref[...] + b_ref[...])
        o_ref[...] = _split3_dot(lf, _tri(tb, False)) + carry[...]
        carry[...] += jnp.sum(lf, axis=1, keepdims=True)

    return pl.pallas_call(
        body, name=name, out_shape=jax.ShapeDtypeStruct((8, S), F32), grid=(S // tb,),
        in_specs=[pl.BlockSpec((8, tb), lambda i: (0, i)), pl.BlockSpec((8, 1), lambda i: (0, 0))],
        out_specs=pl.BlockSpec((8, tb), lambda i: (0, i)),
        scratch_shapes=[pltpu.VMEM((8, 1), F32)],
        compiler_params=_params(("arbitrary",)),
    )(ff_t, b)


def _fox_cum_bwd(ff_t, b, dcum_t, *, name):
    _, S = ff_t.shape
    tb = _pick(S, (512, 256, 128))
    nb = S // tb

    def body(f_ref, b_ref, dc_ref, df_ref, db_ref, carry):
        @pl.when(pl.program_id(0) == 0)
        def _():
            carry[...] = jnp.zeros(carry.shape, F32)
            db_ref[...] = jnp.zeros(db_ref.shape, F32)

        dc = dc_ref[...]
        dlf = _split3_dot(dc, _tri(tb, True)) + carry[...]
        carry[...] += jnp.sum(dc, axis=1, keepdims=True)
        df = dlf * _sigmoid(-(f_ref[...] + b_ref[...]))
        df_ref[...] = df
        db_ref[...] += jnp.sum(df, axis=1, keepdims=True)

    rev = lambda i: (0, nb - 1 - i)
    return pl.pallas_call(
        body, name=name,
        out_shape=(jax.ShapeDtypeStruct((8, S), F32), jax.ShapeDtypeStruct((8, 1), F32)), grid=(nb,),
        in_specs=[pl.BlockSpec((8, tb), rev), pl.BlockSpec((8, 1), lambda i: (0, 0)), pl.BlockSpec((8, tb), rev)],
        out_specs=(pl.BlockSpec((8, tb), rev), pl.BlockSpec((8, 1), lambda i: (0, 0))),
        scratch_shapes=[pltpu.VMEM((8, 1), F32)],
        compiler_params=_params(("arbitrary",)),
    )(ff_t, b, dcum_t)


GLA_W = GLA_HEADS * GLA_DK
GLA_BLOCK_CHUNKS = 4


def _same_chunk(n, lower):
    r = lax.broadcasted_iota(jnp.int32, (n, n), 0)
    c = lax.broadcasted_iota(jnp.int32, (n, n), 1)
    same = (r | (CHUNK - 1)) == (c | (CHUNK - 1))
    return jnp.where(same & (r >= c) if lower else same, 1.0, 0.0).astype(BF16)


def _chunk_mix(x, t, transpose):
    hi, lo = _split2(x)
    dims = TN if transpose else NN
    return _dot(t, hi, dims) + _dot(t, lo, dims)


@jax.custom_vjp
def chunk_cumsum(x):
    return _chunk_mix(x, _same_chunk(x.shape[0], True), False)


chunk_cumsum.defvjp(lambda x: (chunk_cumsum(x), None), lambda _, g: (_chunk_mix(g, _same_chunk(g.shape[0], True), True),))


@jax.custom_vjp
def chunk_total(x):
    return _chunk_mix(x, _same_chunk(x.shape[0], False), False)


chunk_total.defvjp(lambda x: (chunk_total(x), None), lambda _, g: (_chunk_mix(g, _same_chunk(g.shape[0], False), False),))


def _gla_block(q, k, zsm, wg, bg, go, vs, rs, states):
    n_chunks = q.shape[0] // CHUNK
    la = _log_sigmoid(bdot(zsm, wg) + bg) * (1.0 / GLA_TAU)
    end = chunk_total(la)
    kd = k * jnp.exp(end - chunk_cumsum(la))
    a = jnp.exp(end)
    qs = q * (GLA_DK ** -0.5)
    lane = lax.broadcasted_iota(jnp.int32, (1, GLA_W), 1)
    outs, new_states = [], []
    for h in range(GLA_HEADS):
        kdh = kd * jnp.where((lane >= h * GLA_DK) & (lane < (h + 1) * GLA_DK), 1.0, 0.0)
        st, o = states[h], []
        for c in range(n_chunks):
            rows = slice(c * CHUNK, (c + 1) * CHUNK)
            st = st * a[c * CHUNK:c * CHUNK + 1] + bdot_tn(vs[h][rows], kdh[rows])
            o.append(bdot_nt(qs[rows], st))
        o = _rms(jnp.concatenate(o, axis=0), go)
        outs.append(o * (rs[h] * _sigmoid(rs[h])))
        new_states.append(st)
    return outs, new_states


def _gla_fwd(z, zsm, wg, bg, go, cols, *, name):
    S = z.shape[0]
    rb = GLA_BLOCK_CHUNKS * CHUNK
    nb = S // rb
    cq, ckk, cv, cr = cols
    H = GLA_HEADS

    def body(q_ref, k_ref, zsm_ref, wg_ref, bg_ref, go_ref, *rest):
        v_refs, r_refs = rest[:H], rest[H:2 * H]
        o_ref, st_ref, state = rest[2 * H], rest[2 * H + 1], rest[2 * H + 2]

        @pl.when(pl.program_id(0) == 0)
        def _():
            state[...] = jnp.zeros(state.shape, F32)

        states = [state[h] for h in range(H)]
        for h in range(H):
            st_ref[0, h] = states[h]
        outs, new_states = _gla_block(
            q_ref[...].astype(F32), k_ref[...].astype(F32), zsm_ref[...], wg_ref[...], bg_ref[...], go_ref[...],
            [v_refs[h][...].astype(F32) for h in range(H)], [r_refs[h][...].astype(F32) for h in range(H)], states)
        for h in range(H):
            o_ref[:, h * GLA_DV:(h + 1) * GLA_DV] = outs[h].astype(BF16)
            state[h] = new_states[h]

    def col(width, off):
        return pl.BlockSpec((rb, width), lambda i, o=off // width: (i, o))

    full = lambda shp: pl.BlockSpec(shp, lambda i: (0,) * len(shp))
    in_specs = [col(GLA_W, cq), col(GLA_W, ckk), pl.BlockSpec((rb, 128), lambda i: (i, 0)),
                full((128, GLA_W)), full((1, GLA_W)), full((1, GLA_DV))]
    in_specs += [col(GLA_DV, cv + h * GLA_DV) for h in range(H)] + [col(GLA_DV, cr + h * GLA_DV) for h in range(H)]
    return pl.pallas_call(
        body, name=name,
        out_shape=(jax.ShapeDtypeStruct((S, H * GLA_DV), BF16), jax.ShapeDtypeStruct((nb, H, GLA_DV, GLA_W), F32)),
        grid=(nb,), in_specs=in_specs,
        out_specs=(pl.BlockSpec((rb, H * GLA_DV), lambda i: (i, 0)),
                   pl.BlockSpec((1, H, GLA_DV, GLA_W), lambda i: (i, 0, 0, 0))),
        scratch_shapes=[pltpu.VMEM((H, GLA_DV, GLA_W), F32)],
        compiler_params=_params(("arbitrary",)),
    )(z, z, zsm, wg, bg, go, *([z] * (2 * H)))


def _gla_bwd(z, zsm, wg, bg, go, states, do, cols, *, name):
    S = z.shape[0]
    rb = GLA_BLOCK_CHUNKS * CHUNK
    nb = S // rb
    cq, ckk, cv, cr = cols
    H = GLA_HEADS

    def body(q_ref, k_ref, zsm_ref, wg_ref, bg_ref, go_ref, st_ref, do_ref, *rest):
        v_refs, r_refs = rest[:H], rest[H:2 * H]
        dq_ref, dk_ref, dv_ref, dr_ref, dzsm_ref, dwg_ref, dbg_ref, dgo_ref, dstate = rest[2 * H:]

        @pl.when(pl.program_id(0) == 0)
        def _():
            dstate[...] = jnp.zeros(dstate.shape, F32)
            dwg_ref[...] = jnp.zeros(dwg_ref.shape, F32)
            dbg_ref[...] = jnp.zeros(dbg_ref.shape, F32)
            dgo_ref[...] = jnp.zeros(dgo_ref.shape, F32)

        prim = (q_ref[...].astype(F32), k_ref[...].astype(F32), zsm_ref[...], wg_ref[...], bg_ref[...], go_ref[...],
                [v_refs[h][...].astype(F32) for h in range(H)], [r_refs[h][...].astype(F32) for h in range(H)],
                [st_ref[0, h] for h in range(H)])
        _, vjp = jax.vjp(_gla_block, *prim)
        douts = [do_ref[:, h * GLA_DV:(h + 1) * GLA_DV].astype(F32) for h in range(H)]
        dq, dk, dzs, dwg, dbg, dgo, dvs, drs, dsts = vjp((douts, [dstate[h] for h in range(H)]))
        dq_ref[...] = dq.astype(BF16)
        dk_ref[...] = dk.astype(BF16)
        dzsm_ref[...] = dzs
        dwg_ref[...] += dwg
        dbg_ref[...] += dbg
        dgo_ref[...] += dgo
        for h in range(H):
            dv_ref[:, h * GLA_DV:(h + 1) * GLA_DV] = dvs[h].astype(BF16)
            dr_ref[:, h * GLA_DV:(h + 1) * GLA_DV] = drs[h].astype(BF16)
            dstate[h] = dsts[h]

    rev = lambda i: nb - 1 - i

    def col(width, off):
        return pl.BlockSpec((rb, width), lambda i, o=off // width: (rev(i), o))

    full = lambda shp: pl.BlockSpec(shp, lambda i: (0,) * len(shp))
    rowb = lambda w: pl.BlockSpec((rb, w), lambda i: (rev(i), 0))
    in_specs = [col(GLA_W, cq), col(GLA_W, ckk), rowb(128), full((128, GLA_W)), full((1, GLA_W)), full((1, GLA_DV)),
                pl.BlockSpec((1, H, GLA_DV, GLA_W), lambda i: (rev(i), 0, 0, 0)), rowb(H * GLA_DV)]
    in_specs += [col(GLA_DV, cv + h * GLA_DV) for h in range(H)] + [col(GLA_DV, cr + h * GLA_DV) for h in range(H)]
    return pl.pallas_call(
        body, name=name,
        out_shape=(jax.ShapeDtypeStruct((S, GLA_W), BF16), jax.ShapeDtypeStruct((S, GLA_W), BF16),
                   jax.ShapeDtypeStruct((S, H * GLA_DV), BF16), jax.ShapeDtypeStruct((S, H * GLA_DV), BF16),
                   jax.ShapeDtypeStruct((S, 128), F32), jax.ShapeDtypeStruct((128, GLA_W), F32),
                   jax.ShapeDtypeStruct((1, GLA_W), F32), jax.ShapeDtypeStruct((1, GLA_DV), F32)),
        grid=(nb,), in_specs=in_specs,
        out_specs=(rowb(GLA_W), rowb(GLA_W), rowb(H * GLA_DV), rowb(H * GLA_DV), rowb(128),
                   full((128, GLA_W)), full((1, GLA_W)), full((1, GLA_DV))),
        scratch_shapes=[pltpu.VMEM((H, GLA_DV, GLA_W), F32)],
        compiler_params=_params(("arbitrary",)),
    )(z, z, zsm, wg, bg, go, states, do, *([z] * (2 * H)))


def _row_spec(entry, tr):
    if isinstance(entry, tuple):
        arr, width, off = entry
        return arr, pl.BlockSpec((tr, width), lambda i, o=off // width: (i, o))
    return entry, pl.BlockSpec((tr, entry.shape[1]), lambda i: (i, 0))


def _stage_fwd(fn, rows, consts, outs, *, name, tr=None):
    first = rows[0][0] if isinstance(rows[0], tuple) else rows[0]
    S = first.shape[0]
    tr = tr or _pick(S, (512, 256, 128))
    arrs, specs = zip(*[_row_spec(e, tr) for e in rows])
    nr, nc = len(rows), len(consts)

    def body(*refs):
        vals = [r[...].astype(F32) for r in refs[:nr + nc]]
        res = fn(*vals)
        for o_ref, val in zip(refs[nr + nc:], res):
            o_ref[...] = val.astype(o_ref.dtype)

    cspecs = [pl.BlockSpec(c.shape, lambda i, n=c.ndim: (0,) * n) for c in consts]
    return pl.pallas_call(
        body, name=name,
        out_shape=tuple(jax.ShapeDtypeStruct((S, w), dt) for w, dt in outs), grid=(S // tr,),
        in_specs=list(specs) + cspecs,
        out_specs=tuple(pl.BlockSpec((tr, w), lambda i: (i, 0)) for w, _ in outs),
        compiler_params=_params(("parallel",)),
    )(*arrs, *consts)


def _stage_bwd(fn, rows, consts, cts, n_diff, drow_dtypes, *, name, tr=None, lead=None):
    first = rows[0][0] if isinstance(rows[0], tuple) else rows[0]
    S = first.shape[0]
    tr = tr or _pick(S, (512, 256, 128))
    arrs, specs = zip(*[_row_spec(e, tr) for e in rows])
    widths = [e[1] if isinstance(e, tuple) else e.shape[1] for e in rows]
    nr, nc, nt = len(rows), len(consts), len(cts)
    n_lead, lead_width = lead or (1, widths[0])
    n_rows_out = n_diff - n_lead + 1

    def body(*refs):
        vals = [r[...].astype(F32) for r in refs[:nr + nc]]
        ct = [r[...].astype(F32) for r in refs[nr + nc:nr + nc + nt]]
        drow_refs = refs[nr + nc + nt:nr + nc + nt + n_rows_out]
        dconst_refs = refs[nr + nc + nt + n_rows_out:]
        rest_rows = vals[n_diff:nr]

        def f(diff_rows, cs):
            return tuple(fn(*diff_rows, *rest_rows, *cs))

        _, vjp = jax.vjp(f, vals[:n_diff], vals[nr:])
        drows, dcs = vjp(tuple(ct))
        off = 0
        for val, w in zip(drows[:n_lead], widths):
            drow_refs[0][:, off:off + w] = val.astype(drow_refs[0].dtype)
            off += w
        for r, val in zip(drow_refs[1:], drows[n_lead:]):
            r[...] = val.astype(r.dtype)
        first_step = pl.program_id(0) == 0
        for r, val in zip(dconst_refs, dcs):
            @pl.when(first_step)
            def _(r=r, val=val):
                r[...] = val

            @pl.when(jnp.logical_not(first_step))
            def _(r=r, val=val):
                r[...] += val

    cspecs = [pl.BlockSpec(c.shape, lambda i, n=c.ndim: (0,) * n) for c in consts]
    ctspecs = [pl.BlockSpec((tr, c.shape[1]), lambda i: (i, 0)) for c in cts]
    out_shape = [jax.ShapeDtypeStruct((S, lead_width), drow_dtypes[0])]
    out_shape += [jax.ShapeDtypeStruct((S, widths[j]), drow_dtypes[j]) for j in range(n_lead, n_diff)]
    out_shape += [jax.ShapeDtypeStruct(c.shape, F32) for c in consts]
    out_specs = [pl.BlockSpec((tr, sum(widths[:n_lead])), lambda i: (i, 0))]
    out_specs += [pl.BlockSpec((tr, widths[j]), lambda i: (i, 0)) for j in range(n_lead, n_diff)] + cspecs
    res = pl.pallas_call(
        body, name=name, out_shape=tuple(out_shape), grid=(S // tr,),
        in_specs=list(specs) + cspecs + ctspecs, out_specs=tuple(out_specs),
        compiler_params=_params(("arbitrary",)),
    )(*arrs, *consts, *cts)
    return list(res[:n_rows_out]), list(res[n_rows_out:])


def _mla_prep_fn(cq, ckv, kr, kr_sw, cos, sin, gq, gkv, wq_n, wq_r, wq_sw, wk, wv):
    hq = _rms(cq, gq)
    hkv = _rms(ckv, gkv)
    return (bdot(hq, wq_n), bdot(hq, wq_r) * cos + bdot(hq, wq_sw) * sin,
            bdot(hkv, wk), bdot(hkv, wv), kr * cos + kr_sw * sin)


def _merge_fn(g0, g1, g2, of, og, om, b0, b1, b2, wf, wg, wm):
    return (_sigmoid(g0 + b0) * bdot(of, wf) + _sigmoid(g1 + b1) * bdot(og, wg) + _sigmoid(g2 + b2) * bdot(om, wm),)


_IN_SIZES = (256, 256, 256, 4, 256, 256, 512, 16, 512, 256, 128, 32, 3072)
_IN_OFF = np.concatenate([[0], np.cumsum(_IN_SIZES)])
(_O_FQ, _O_FK, _O_FV, _O_FF, _O_GQ, _O_GK, _O_GV, _O_GLOW, _O_GR, _O_MQ, _O_MKV, _O_MKR, _O_ZG) = [int(o) for o in _IN_OFF[:-1]]
N_IN = int(_IN_OFF[-1])
_BIG_GROUPS = ((_O_ZG, 3072), (_O_GV, 512), (_O_GR, 512), (_O_FQ, 256), (_O_FK, 256), (_O_FV, 256),
               (_O_GQ, 256), (_O_GK, 256), (_O_MQ, 256), (_O_MKV, 128))
Z_GATE, Z_GV, Z_GR, Z_FQ, Z_FK, Z_FV, Z_GQ, Z_GK, Z_MQ, Z_MKV = [int(o) for o in
                                                                    np.concatenate([[0], np.cumsum([w for _, w in _BIG_GROUPS])])[:-1]]
N_BIG = sum(w for _, w in _BIG_GROUPS)
_HALF = MLA_ROPE // 2
_QK_HD = MLA_NOPE + MLA_ROPE
SM_FF, SM_GLOW, SM_KR, SM_KR_SW, N_SM = 0, 8, 128, 256, 384
N_PAD = N_BIG + N_SM
_IN_SEGS = ([(o, w, 1.0) for o, w in _BIG_GROUPS]
            + [(_O_FF, 4, 1.0), (None, SM_GLOW - 4, 0.0), (_O_GLOW, GLA_RANK, 1.0), (None, 128 - SM_GLOW - GLA_RANK, 0.0)]
            + [(_O_MKR, MLA_ROPE, 1.0)] * MLA_HEADS
            + [(_O_MKR + _HALF, _HALF, -1.0), (_O_MKR, _HALF, 1.0)] * MLA_HEADS)


def _cols(x, start, width):
    return lax.slice_in_dim(x, start, start + width, axis=x.ndim - 1)


def _pad_w_in(w):
    return jnp.concatenate([jnp.zeros(w.shape[:-1] + (n,), w.dtype) if src is None else
                            (_cols(w, src, n) if sign > 0 else -_cols(w, src, n)) for src, n, sign in _IN_SEGS], axis=-1)


def _unpad_w_in(g):
    groups = []
    for o, n in zip(_IN_OFF[:-1], _IN_SIZES):
        total, pos = None, 0
        for src, m, sign in _IN_SEGS:
            if src is not None and o <= src and src + m <= o + n:
                term = _cols(g, pos, m) if sign > 0 else -_cols(g, pos, m)
                if m != n:
                    term = jnp.pad(term, [(0, 0)] * (g.ndim - 1) + [(int(src - o), int(o + n - src - m))])
                total = term if total is None else total + term
            pos += m
        groups.append(total)
    return jnp.concatenate(groups, axis=-1)


def _take(x, idx):
    idx = np.asarray(idx)
    cuts = [0] + [i for i in range(1, len(idx)) if idx[i] != idx[i - 1] + 1] + [len(idx)]
    return jnp.concatenate([_cols(x, int(idx[a]), b - a) for a, b in zip(cuts[:-1], cuts[1:])], axis=1)


_UQ_NOPE = np.concatenate([np.arange(h * _QK_HD, h * _QK_HD + MLA_NOPE) for h in range(MLA_HEADS)])
_UQ_ROT = np.concatenate([np.arange(h * _QK_HD + MLA_NOPE, (h + 1) * _QK_HD) for h in range(MLA_HEADS)])
_UKV_PERM = np.concatenate(
    [np.concatenate([np.arange(h * 128, h * 128 + MLA_NOPE) for h in range(MLA_HEADS)]),
     np.concatenate([np.arange(h * 128 + MLA_NOPE, (h + 1) * 128) for h in range(MLA_HEADS)])])
_UKV_INV = np.argsort(_UKV_PERM)


def _rotary_partner(r):
    return jnp.concatenate([piece for h in range(MLA_HEADS) for piece in
                            (-_cols(r, h * MLA_ROPE + _HALF, _HALF), _cols(r, h * MLA_ROPE, _HALF))], axis=1)


def _uq_grad(dn, dr, dsw):
    dr = dr + jnp.concatenate([piece for h in range(MLA_HEADS) for piece in
                               (_cols(dsw, h * MLA_ROPE + _HALF, _HALF), -_cols(dsw, h * MLA_ROPE, _HALF))], axis=1)
    return jnp.concatenate([piece for h in range(MLA_HEADS) for piece in
                            (_cols(dn, h * MLA_NOPE, MLA_NOPE), _cols(dr, h * MLA_ROPE, MLA_ROPE))], axis=1)


def _rope_tables(S):
    inv = ROPE_BASE ** (-jnp.arange(_HALF, dtype=F32) / _HALF)
    ang = jnp.arange(S, dtype=F32)[:, None] * inv[None, :]
    return jnp.tile(jnp.cos(ang), (1, 2 * MLA_HEADS)), jnp.tile(jnp.sin(ang), (1, 2 * MLA_HEADS))


class _LayerParams:
    def __init__(self, rep, l):
        self.w, self.rep, self.l, self.made = {}, rep, l, {}

    def __getitem__(self, k):
        if k not in self.made:
            self.made[k] = self._make(k)
        return self.made[k]

    def _make(self, k):
        w, rep, l = self.w, self.rep, self.l
        if k == 'wg':
            return jnp.pad(w['w_gla_gate'], [(SM_GLOW, LANES - SM_GLOW - GLA_RANK), (0, 0)])
        if k in ('wq_n', 'wq_r'):
            return _take(w['w_mla_uq'], _UQ_NOPE if k == 'wq_n' else _UQ_ROT)
        if k == 'wq_sw':
            return _rotary_partner(self['wq_r'])
        if k in ('wk', 'wv'):
            return _take(w['w_mla_ukv'], _UKV_PERM[:256] if k == 'wk' else _UKV_PERM[256:])
        if k == 'b_f':
            return jnp.zeros((8, 1), F32).at[:FOX_HEADS, 0].set(rep['b_fox_forget'][l])
        if k == 'b_gate':
            return [rep['b_branch_gate'][l][i * 1024:(i + 1) * 1024].reshape(1, 1024) for i in range(3)]
        vec = {'bg': 'b_gla_gate', 'go': 'g_gla_out', 'gq': 'g_mla_q', 'gkv': 'g_mla_kv'}
        if k in vec:
            return rep[vec[k]][l].reshape(1, -1)
        return rep[k][l] if k in rep else w[k]


_GLA_COLS = (Z_GQ, Z_GK, Z_GV, Z_GR)
_MLA_OUTS = [(256, BF16), (128, BF16), (256, BF16), (256, BF16), (128, BF16)]


def _mla_rows(z, zsm, rope):
    return [(z, 256, Z_MQ), (z, 128, Z_MKV), (zsm, 128, SM_KR), (zsm, 128, SM_KR_SW), *rope]


def _mla_consts(p):
    return [p['gq'], p['gkv'], p['wq_n'], p['wq_r'], p['wq_sw'], p['wk'], p['wv']]


def _fox_qkv(z):
    return [((z, Z_FQ, 256), (z, Z_FK, 256), FOX_HD, False)], (z, Z_FV, 256)


def _mla_qkv(qn, qr, kn, vv, kr):
    return [((qn, 0, 256), (kn, 0, 256), MLA_NOPE, False), ((qr, 0, 128), (kr, 0, 128), MLA_ROPE, True)], (vv, 0, 256)


def _xa_qkv(qx, kvx):
    return [((qx, 0, 512), (kvx, 0, 512), XA_HD, False)], (kvx, 512, 512)


def _merge_rows(z, o_fox, o_gla, o_mla):
    return [(z, 1024, Z_GATE), (z, 1024, Z_GATE + 1024), (z, 1024, Z_GATE + 2048), o_fox, o_gla, o_mla]


def _merge_consts(p):
    return p['b_gate'] + [p['w_up_fox'], p['w_up_gla'], p['w_up_mla']]


def _carried(hooks, key, call, single=False):
    entries = hooks.pop(key, [])
    if not entries:
        return call(rider=None)
    res = call(rider=_join_riders([rider for rider, _ in entries]))
    for (_, sink), got in zip(entries, res[-1]):
        sink(got)
    return res[0] if single else res[:-1]


def _layer_fwd(x0, mem, p, rope, l, hooks):
    S = x0.shape[0]
    sv = {'x0': x0}

    def mm(key, a, b, **kw):
        return _carried(hooks, (l, key), lambda rider: _mm(a, b, mode='nn', rider=rider, name=f"{key}_{l}", **kw), single=True)

    h1 = _rms_fwd(x0, p['g_mix'], name=f"rms_mix_{l}")
    z = mm('in_big', h1, p['w_in'], out_dtype=BF16, b_cols=(0, N_BIG))
    zsm = _mm(h1, p['w_in'], mode='nn', out_dtype=F32, b_cols=(N_BIG, N_SM), name=f"in_small_{l}")
    sv.update(h1=h1, z=z, zsm=zsm)
    ff_t = jnp.zeros((8, S), F32).at[:FOX_HEADS].set(zsm[:, SM_FF:SM_FF + FOX_HEADS].T)
    cum_t = _fox_cum_fwd(ff_t, p['b_f'], name=f"fox_cum_{l}")
    cum = cum_t.T
    o_fox, lse_f = _carried(hooks, (l, 'fox_fwd'), lambda rider: _attn_fwd(
        *_fox_qkv(z), FOX_HEADS, cum_t, cum, scale=FOX_HD ** -0.5, mask='causal', name=f"fox_fwd_{l}", rider=rider))
    sv.update(ff_t=ff_t, cum=cum, cum_t=cum_t, lse_f=lse_f, o_fox=o_fox)
    o_gla, states = _gla_fwd(z, zsm, p['wg'], p['bg'], p['go'], _GLA_COLS, name=f"gla_fwd_{l}")
    sv.update(o_gla=o_gla, states=states)
    mla = _stage_fwd(_mla_prep_fn, _mla_rows(z, zsm, rope), _mla_consts(p), _MLA_OUTS, name=f"mla_prep_{l}")
    o_mla, lse_m = _carried(hooks, (l, 'mla_fwd'), lambda rider: _attn_fwd(
        *_mla_qkv(*mla), MLA_HEADS, None, None, scale=_QK_HD ** -0.5, mask='chunk', name=f"mla_fwd_{l}", rider=rider))
    sv.update(mla=mla, lse_m=lse_m, o_mla=o_mla)
    (y,) = _stage_fwd(_merge_fn, _merge_rows(z, o_fox, o_gla, o_mla), _merge_consts(p), [(1024, BF16)], name=f"merge_{l}")
    x1 = mm('out_proj', y, p['w_out'], out_dtype=F32, residual=x0)
    sv.update(y=y, x1=x1)
    h2 = _rms_fwd(x1, p['g_xa'], name=f"rms_xa_{l}")
    hm = _rms_fwd(mem, p['g_mem'], name=f"rms_mem_{l}")
    qx = _mm(h2, p['w_xq'], mode='nn', out_dtype=BF16, name=f"xq_{l}")
    kvx = _mm(hm, p['w_xkv'], mode='nn', out_dtype=BF16, name=f"xkv_{l}")
    ox, lse_x = _carried(hooks, (l, 'xa_fwd'), lambda rider: _attn_fwd(
        *_xa_qkv(qx, kvx), XA_HEADS, None, None, scale=XA_HD ** -0.5, mask=None, name=f"xa_fwd_{l}", rider=rider))
    x2 = mm('xo', ox, p['w_xo'], out_dtype=F32, residual=x1)
    sv.update(h2=h2, hm=hm, qx=qx, kvx=kvx, lse_x=lse_x, ox=ox, x2=x2)
    h3 = _rms_fwd(x2, p['g_mlp'], name=f"rms_mlp_{l}")
    a = mm('mlp1', h3, p['w_mlp1'], out_dtype=BF16)
    x3 = mm('mlp2', a, p['w_mlp2'], out_dtype=F32, act='relu2', residual=x2)
    sv.update(h3=h3, a=a)
    return x3, sv


def _layer_bwd(dx3, dx3b, mem, p, rope, sv, l, hooks, half_done, matrices_done):
    S = dx3.shape[0]
    g = {}

    def dw(key, a, b, **kw):
        return _mm(a, b, mode='tn', out_dtype=BF16, col_shards=b.shape[1] // LANES, name=f"d_{key}_{l}", **kw)

    da = _mm(dx3b, p['w_mlp2'], mode='nt', out_dtype=BF16, drelu_of=sv['a'], name=f"d_mlp2_in_{l}")
    g['w_mlp2'] = dw('w_mlp2', sv['a'], dx3b, act='relu2')
    dx2, dx2b, g['g_mlp'] = _mm(da, p['w_mlp1'], mode='nt', out_dtype=F32, norm_bwd=(sv['x2'], p['g_mlp'], dx3), tm=1024,
                                name=f"d_mlp1_in_{l}")
    g['w_mlp1'] = dw('w_mlp1', sv['h3'], da)
    dox = _mm(dx2b, p['w_xo'], mode='nt', out_dtype=BF16, name=f"d_xo_in_{l}")
    g['w_xo'] = dw('w_xo', sv['ox'], dx2b)
    (dqx,), (dkx,), dvx = _carried(hooks, (l, 'xa_bwd'), lambda rider: _attn_bwd(
        *_xa_qkv(sv['qx'], sv['kvx']), XA_HEADS, sv['ox'], dox, sv['lse_x'], None, None,
        scale=XA_HD ** -0.5, mask=None, name=f"xa_bwd_{l}", rider=rider))
    dqx = dqx.astype(BF16)
    dkvx = jnp.concatenate([dkx, dvx], axis=1).astype(BF16)
    dx1, dx1b, g['g_xa'] = _mm(dqx, p['w_xq'], mode='nt', out_dtype=F32, norm_bwd=(sv['x1'], p['g_xa'], dx2), tm=1024,
                               name=f"d_xq_in_{l}")
    g['w_xq'] = dw('w_xq', sv['h2'], dqx)
    dhm = _mm(dkvx, p['w_xkv'], mode='nt', out_dtype=F32, name=f"d_xkv_in_{l}")
    g['w_xkv'] = dw('w_xkv', sv['hm'], dkvx)
    _, _, g['g_mem'] = _rms_bwd(mem, p['g_mem'], dhm, None, name=f"d_rms_mem_{l}")
    dy = _mm(dx1b, p['w_out'], mode='nt', out_dtype=F32, name=f"d_out_in_{l}")
    g['w_out'] = dw('w_out', sv['y'], dx1b)
    z, zsm = sv['z'], sv['zsm']
    (dz, do_fox, do_gla, do_mla), (db0, db1, db2, g['w_up_fox'], g['w_up_gla'], g['w_up_mla']) = _stage_bwd(
        _merge_fn, _merge_rows(z, sv['o_fox'], sv['o_gla'], sv['o_mla']), _merge_consts(p), [dy], 6, [BF16] * 6,
        lead=(3, N_PAD), name=f"merge_bwd_{l}")
    g['b_branch_gate'] = jnp.concatenate([db0, db1, db2], axis=1).reshape(-1)
    half_done(l, g)
    (dfq,), (dfk,), dfv, dck, dcq = _carried(hooks, (l, 'fox_bwd'), lambda rider: _attn_bwd(
        *_fox_qkv(z), FOX_HEADS, sv['o_fox'], do_fox, sv['lse_f'], sv['cum_t'], sv['cum'],
        scale=FOX_HD ** -0.5, mask='causal', name=f"fox_bwd_{l}", rider=rider))
    dff_t, db_f = _fox_cum_bwd(sv['ff_t'], p['b_f'], dcq + dck.T, name=f"fox_cum_bwd_{l}")
    g['b_fox_forget'] = db_f[:FOX_HEADS, 0]
    dgq, dgk, dgv, dgr, dzsm, dwg, dbg, dgo = _gla_bwd(z, zsm, p['wg'], p['bg'], p['go'], sv['states'], do_gla, _GLA_COLS,
                                                       name=f"gla_bwd_{l}")
    g['w_gla_gate'] = dwg[SM_GLOW:SM_GLOW + GLA_RANK]
    g['b_gla_gate'] = dbg.reshape(-1)
    g['g_gla_out'] = dgo.reshape(-1)
    (dmqn, dmqr), (dmkn, dmkr), dmv = _carried(hooks, (l, 'mla_bwd'), lambda rider: _attn_bwd(
        *_mla_qkv(*sv['mla']), MLA_HEADS, sv['o_mla'], do_mla, sv['lse_m'], None, None,
        scale=_QK_HD ** -0.5, mask='chunk', name=f"mla_bwd_{l}", rider=rider))
    (dcq, dckv, dkr, dkr_sw), (dgq_n, dgkv_n, dwq_n, dwq_r, dwq_sw, dwk, dwv) = _stage_bwd(
        _mla_prep_fn, _mla_rows(z, zsm, rope), _mla_consts(p), [dmqn, dmqr, dmkn, dmv, dmkr], 4, [BF16] * 4,
        name=f"mla_prep_bwd_{l}")
    g['g_mla_q'] = dgq_n.reshape(-1)
    g['g_mla_kv'] = dgkv_n.reshape(-1)
    g['w_mla_uq'] = _uq_grad(dwq_n, dwq_r, dwq_sw)
    g['w_mla_ukv'] = _take(jnp.concatenate([dwk, dwv], axis=1), _UKV_INV)
    dsm = dzsm + jnp.pad(dff_t[:FOX_HEADS].T, [(0, 0), (0, 128 - FOX_HEADS)])
    dz = lax.dynamic_update_slice(dz, jnp.concatenate(
        [dgv, dgr, dfq.astype(BF16), dfk.astype(BF16), dfv.astype(BF16), dgq, dgk, dcq, dckv, dsm.astype(BF16), dkr, dkr_sw],
        axis=1), (0, Z_GV))
    g['w_in'] = dw('w_in', sv['h1'], dz, tn=N_PAD // 3)
    matrices_done(l, g)
    dx0, dx0b, g['g_mix'] = _carried(hooks, (l, 'd_in'), lambda rider: _mm(
        dz, p['w_in'], mode='nt', out_dtype=F32, norm_bwd=(sv['x0'], p['g_mix'], dx1), tm=512, tk=N_PAD // 2,
        name=f"d_in_{l}", rider=rider))
    for n in ('g_mlp', 'g_mem', 'g_xa', 'g_mix'):
        g[n] = g[n].reshape(-1)
    return dx0, dx0b, g


def _local_step(x, mem, target, ps, g_final, hooks, half_done, matrices_done, layer_done):
    rope = _rope_tables(x.shape[0])
    saved = []
    for l, p in enumerate(ps):
        x, sv = _layer_fwd(x, mem, p, rope, l, hooks)
        saved.append(sv)
    loss, dx, dxb, dgf = _loss_head(x, g_final, target, name="loss_head")
    for l in reversed(range(len(ps))):
        dx, dxb, grads = _layer_bwd(dx, dxb, mem, ps[l], rope, saved[l], l, hooks, half_done, matrices_done)
        layer_done(l, grads)
    assert not hooks, f"exchanges without a carrier: {list(hooks)}"
    return loss, dx, dgf.reshape(-1)


_MESH_AXES = ("x", "y", "c")
_HBM = pl.BlockSpec(memory_space=pl.ANY)


N_CHIP = 4
_SLOT_ROWS = (2048, 1024, 512, 256, 128, 64, 32, 16, 8)


def _place():
    x, y, c = (lax.axis_index(n) for n in _MESH_AXES)
    return (x, y, c), (x, y, 1 - c), [(1 - x, y), (x, 1 - y), (1 - x, 1 - y)]


def _remote(src, dst, sems, k, to):
    return pltpu.make_async_remote_copy(src_ref=src, dst_ref=dst, send_sem=sems[0].at[k], recv_sem=sems[1].at[k],
                                        device_id=to, device_id_type=pl.DeviceIdType.MESH)


def _all_gather(x, *, name):
    def body(x_ref, o_ref, send_sems, recv_sems, local_sem):
        me, sib, chips = _place()
        c = me[2]
        sems = (send_sems, recv_sems)
        slot = lambda px, py, pc: o_ref.at[4 * px + 2 * py + pc]
        mine = pltpu.make_async_copy(x_ref, slot(*me), local_sem)
        mine.start()
        first = [_remote(x_ref, slot(*me), sems, 0, sib)]
        first += [_remote(x_ref, slot(*me), sems, 1 + j, (*chip, c)) for j, chip in enumerate(chips)]
        for cp in first:
            cp.start()
        passed = [_remote(slot(*chip, c), slot(*chip, c), sems, 4 + j, sib) for j, chip in enumerate(chips)]
        for j, chip in enumerate(chips):
            _remote(x_ref, slot(*chip, c), sems, 1 + j, me).wait_recv()
            passed[j].start()
        _remote(x_ref, slot(*sib), sems, 0, me).wait_recv()
        for j, chip in enumerate(chips):
            _remote(x_ref, slot(*chip, 1 - c), sems, 4 + j, me).wait_recv()
        for cp in first + passed:
            cp.wait_send()
        mine.wait()

    return pl.pallas_call(
        body, name=name, out_shape=jax.ShapeDtypeStruct((N_DEV,) + x.shape, x.dtype),
        in_specs=[_HBM], out_specs=_HBM,
        scratch_shapes=[pltpu.SemaphoreType.DMA((N_DEV - 1,)), pltpu.SemaphoreType.DMA((N_DEV - 1,)), pltpu.SemaphoreType.DMA],
        compiler_params=pltpu.CompilerParams(has_side_effects=True),
    )(x)


class _Rider:
    def __init__(self, inputs, out_shapes, scratch, start, finish, post):
        self.inputs, self.out_shapes, self.scratch = list(inputs), list(out_shapes), list(scratch)
        self.start, self.finish, self.post = start, finish, post


def _run_rider(rider, *, name):
    def body(*refs):
        rider.start(refs)
        rider.finish(refs)

    outs = pl.pallas_call(
        body, name=name, out_shape=tuple(rider.out_shapes), in_specs=[_HBM] * len(rider.inputs),
        out_specs=(_HBM,) * len(rider.out_shapes), scratch_shapes=rider.scratch,
        compiler_params=pltpu.CompilerParams(has_side_effects=True),
    )(*rider.inputs)
    return rider.post(outs)


def _carry(rider, n_in, n_out, first, last):
    if rider is None:
        return [], [], [], [], [], lambda refs: refs
    ni, no = len(rider.inputs), len(rider.out_shapes)

    def split(refs):
        own_in, r_in = refs[:n_in], refs[n_in:n_in + ni]
        own_out, r_out = refs[n_in + ni:n_in + ni + n_out], refs[n_in + ni + n_out:n_in + ni + n_out + no]
        rest = refs[n_in + ni + n_out + no:]
        own_scr, r_scr = rest[:len(rest) - len(rider.scratch)], rest[len(rest) - len(rider.scratch):]
        rrefs = tuple(r_in) + tuple(r_out) + tuple(r_scr)
        pl.when(first())(lambda: rider.start(rrefs))
        pl.when(last())(lambda: rider.finish(rrefs))
        return tuple(own_in) + tuple(own_out) + tuple(own_scr)

    return list(rider.inputs), [_HBM] * ni, list(rider.out_shapes), [_HBM] * no, list(rider.scratch), split


def _gather_rider(shards, axes):
    n = len(shards)
    srcs, out_shapes, kinds = [], [], []
    for s, ax in zip(shards, axes):
        L, a, b = s.shape
        if ax == 1:
            srcs.append(s.reshape(L, 1, a, b)), out_shapes.append((L, N_DEV, a, b)), kinds.append('row')
        elif b % 128 == 0:
            srcs.append(s), out_shapes.append((L, a, N_DEV * b)), kinds.append('col')
        else:
            srcs.append(s.reshape(1, L, a, b)), out_shapes.append((N_DEV, L, a, b)), kinds.append('slot')

    def parts(refs):
        x_refs, o_refs = refs[:n], refs[n:2 * n]
        send_sems, recv_sems, local_sem = refs[2 * n:]
        me, sib, chips = _place()
        sems = (send_sems, recv_sems)

        def win(t, px, py, pc):
            idx = 4 * px + 2 * py + pc
            if kinds[t] == 'row':
                return o_refs[t].at[:, pl.ds(idx, 1)]
            if kinds[t] == 'col':
                width = shards[t].shape[2]
                return o_refs[t].at[:, :, pl.ds(pl.multiple_of(idx * width, 128), width)]
            return o_refs[t].at[pl.ds(idx, 1)]

        def group(k, block, to, own):
            return [_remote(x_refs[t] if own else win(t, *block), win(t, *block), sems, k * n + t, to) for t in range(n)]

        mine = [pltpu.make_async_copy(x_refs[t], win(t, *me), local_sem.at[t]) for t in range(n)]
        first = group(0, me, sib, True)
        for j, chip in enumerate(chips):
            first += group(1 + j, me, (*chip, me[2]), True)
        return me, sib, chips, group, mine, first

    def start(refs):
        *_, mine, first = parts(refs)
        for cp in mine + first:
            cp.start()

    def finish(refs):
        me, sib, chips, group, mine, first = parts(refs)
        c = me[2]
        passed = []
        for j, chip in enumerate(chips):
            for cp in group(1 + j, (*chip, c), me, False):
                cp.wait_recv()
            fwd = group(4 + j, (*chip, c), sib, False)
            for cp in fwd:
                cp.start()
            passed += fwd
        for cp in group(0, sib, me, False):
            cp.wait_recv()
        for j, chip in enumerate(chips):
            for cp in group(4 + j, (*chip, 1 - c), me, False):
                cp.wait_recv()
        for cp in first + passed:
            cp.wait_send()
        for cp in mine:
            cp.wait()

    def post(outs):
        whole = []
        for o, s, kind in zip(outs, shards, kinds):
            L, a, b = s.shape
            whole.append(o.reshape(L, N_DEV * a, b) if kind == 'row' else o if kind == 'col' else _to_whole(o, 2))
        return whole

    return _Rider(srcs, [jax.ShapeDtypeStruct(shp, s.dtype) for shp, s in zip(out_shapes, shards)],
                  [pltpu.SemaphoreType.DMA(((N_DEV - 1) * n,)), pltpu.SemaphoreType.DMA(((N_DEV - 1) * n,)),
                   pltpu.SemaphoreType.DMA((n,))], start, finish, post)


def _sibling_swap_rider(x):
    def sends(refs):
        x_ref, o_ref, send_sems, recv_sems = refs
        me, sib, _ = _place()
        return [_remote(x_ref.at[j, 1 - me[2]], o_ref.at[j], (send_sems, recv_sems), j, sib) for j in range(N_CHIP)]

    def start(refs):
        for cp in sends(refs):
            cp.start()

    def finish(refs):
        for cp in sends(refs):
            cp.wait_send()
            cp.wait_recv()

    return _Rider([x], [jax.ShapeDtypeStruct((N_CHIP,) + x.shape[2:], x.dtype)],
                  [pltpu.SemaphoreType.DMA((N_CHIP,)), pltpu.SemaphoreType.DMA((N_CHIP,))], start, finish, lambda outs: outs[0])


def _join_riders(riders):
    counts = [(len(r.inputs), len(r.out_shapes), len(r.scratch)) for r in riders]
    n_in, n_out = sum(c[0] for c in counts), sum(c[1] for c in counts)

    def refs_of(refs, k):
        a = sum(c[0] for c in counts[:k])
        b = n_in + sum(c[1] for c in counts[:k])
        s = n_in + n_out + sum(c[2] for c in counts[:k])
        return tuple(refs[a:a + counts[k][0]]) + tuple(refs[b:b + counts[k][1]]) + tuple(refs[s:s + counts[k][2]])

    def each(method):
        def run(refs):
            for k, r in enumerate(riders):
                getattr(r, method)(refs_of(refs, k))
        return run

    def post(outs):
        got, at = [], 0
        for r, c in zip(riders, counts):
            got.append(r.post(outs[at:at + c[1]]))
            at += c[1]
        return got

    return _Rider([x for r in riders for x in r.inputs], [o for r in riders for o in r.out_shapes],
                  [s for r in riders for s in r.scratch], each('start'), each('finish'), post)


def _pair_sum(x, got, c, *, name):
    _, _, R, _ = x.shape
    tr = _pick(R, _SLOT_ROWS)

    def body(c_ref, x_ref, g_ref, o_ref):
        o_ref[...] = (x_ref[...].astype(F32) + g_ref[...].astype(F32)).astype(o_ref.dtype)

    return pl.pallas_call(
        body, name=name, out_shape=jax.ShapeDtypeStruct((N_CHIP, R, 128), x.dtype),
        grid_spec=pltpu.PrefetchScalarGridSpec(
            num_scalar_prefetch=1, grid=(R // tr,),
            in_specs=[pl.BlockSpec((N_CHIP, None, tr, 128), lambda i, c_ref: (0, c_ref[0], i, 0)),
                      pl.BlockSpec((N_CHIP, tr, 128), lambda i, c_ref: (0, i, 0))],
            out_specs=pl.BlockSpec((N_CHIP, tr, 128), lambda i, c_ref: (0, i, 0))),
        compiler_params=_params(("parallel",)),
    )(c, x, got)


def _chip_all_to_all_rider(x):
    def parts(refs):
        x_ref, o_ref, send_sems, recv_sems, local_sem = refs
        me, _, chips = _place()
        sems = (send_sems, recv_sems)
        mine = 2 * me[0] + me[1]
        local = pltpu.make_async_copy(x_ref.at[mine], o_ref.at[mine], local_sem)
        sends = [_remote(x_ref.at[2 * px + py], o_ref.at[mine], sems, j, (px, py, me[2])) for j, (px, py) in enumerate(chips)]
        arrival = lambda j: _remote(x_ref.at[mine], o_ref.at[2 * chips[j][0] + chips[j][1]], sems, j, me)
        return local, sends, arrival

    def start(refs):
        local, sends, _ = parts(refs)
        for cp in [local] + sends:
            cp.start()

    def finish(refs):
        local, sends, arrival = parts(refs)
        for j, cp in enumerate(sends):
            cp.wait_send()
            arrival(j).wait_recv()
        local.wait()

    return _Rider([x], [jax.ShapeDtypeStruct(x.shape, x.dtype)],
                  [pltpu.SemaphoreType.DMA((N_CHIP - 1,)), pltpu.SemaphoreType.DMA((N_CHIP - 1,)), pltpu.SemaphoreType.DMA],
                  start, finish, lambda outs: outs[0])


def _sum_slots(x, *, name):
    n, R, _ = x.shape
    tr = _pick(R, _SLOT_ROWS)

    def body(x_ref, o_ref):
        acc = x_ref[0].astype(F32)
        for j in range(1, n):
            acc = acc + x_ref[j].astype(F32)
        o_ref[...] = acc

    return pl.pallas_call(
        body, name=name, out_shape=jax.ShapeDtypeStruct((R, 128), F32), grid=(R // tr,),
        in_specs=[pl.BlockSpec((n, tr, 128), lambda i: (0, i, 0))], out_specs=pl.BlockSpec((tr, 128), lambda i: (i, 0)),
        compiler_params=_params(("parallel",)),
    )(x)


def _adamw(w, g, m, v, *, name):
    shape = w.shape
    cols = shape[-1]
    rows = int(np.prod(shape[:-1]))
    tr = next((t for t in (1024, 512, 256, 128, 64, 32, 16, 8) if rows % t == 0 and t * cols * 4 <= (1 << 20)), rows)

    def body(w_ref, g_ref, m_ref, v_ref, d_ref, mo_ref, vo_ref):
        g_ = g_ref[...]
        m_ = ADAM_B1 * m_ref[...] + (1.0 - ADAM_B1) * g_
        v_ = ADAM_B2 * v_ref[...] + (1.0 - ADAM_B2) * jnp.square(g_)
        m_hat = m_ / (1.0 - ADAM_B1 ** ADAM_STEP)
        v_hat = v_ / (1.0 - ADAM_B2 ** ADAM_STEP)
        d_ref[...] = -ADAM_LR * (m_hat / (jnp.sqrt(v_hat) + ADAM_EPS) + ADAM_WD * w_ref[...])
        mo_ref[...] = m_
        vo_ref[...] = v_

    blk = pl.BlockSpec((tr, cols), lambda i: (i, 0))
    outs = pl.pallas_call(
        body, name=name, out_shape=tuple(jax.ShapeDtypeStruct((rows, cols), F32) for _ in range(3)), grid=(rows // tr,),
        in_specs=[blk] * 4, out_specs=(blk,) * 3, compiler_params=_params(("parallel",)),
    )(*(a.reshape(rows, cols) for a in (w, g, m, v)))
    return tuple(o.reshape(shape) for o in outs)


_WEIGHTS = ('g_mix', 'w_in', 'b_fox_forget', 'w_gla_gate', 'b_gla_gate', 'g_gla_out', 'g_mla_q', 'w_mla_uq', 'g_mla_kv',
            'w_mla_ukv', 'b_branch_gate', 'w_up_fox', 'w_up_gla', 'w_up_mla', 'w_out', 'g_xa', 'g_mem', 'w_xq', 'w_xkv',
            'w_xo', 'g_mlp', 'w_mlp1', 'w_mlp2', 'g_final')
_SHARDED = (('w_in', 1), ('w_gla_gate', 2), ('w_mla_uq', 2), ('w_mla_ukv', 2), ('w_up_fox', 2), ('w_up_gla', 2),
            ('w_up_mla', 2), ('w_out', 1), ('w_xq', 1), ('w_xkv', 1), ('w_xo', 2), ('w_mlp1', 2), ('w_mlp2', 1))
_REPLICATED = tuple(n for n in _WEIGHTS if n not in dict(_SHARDED))
_ROW_PAD = 1024
_SMALL_ROW_PAD = 8
_PIECE_ROWS = 16


def _pack(flats, lead, row_pad=_ROW_PAD):
    if all(int(np.prod(a.shape[lead:])) % 128 == 0 for a in flats):
        def block(a):
            a = a.reshape(a.shape[:lead] + (-1, 128))
            return jnp.pad(a, [(0, 0)] * lead + [(0, -a.shape[lead] % _PIECE_ROWS), (0, 0)])
        cat = jnp.concatenate([block(a) for a in flats], axis=lead)
        rows = cat.shape[lead]
        return jnp.pad(cat, [(0, 0)] * lead + [(0, -(-rows // row_pad) * row_pad - rows), (0, 0)])
    cat = jnp.concatenate([a.reshape(a.shape[:lead] + (-1,)) for a in flats], axis=-1)
    n = cat.shape[-1]
    total = -(-n // (128 * row_pad)) * (128 * row_pad)
    cat = jnp.pad(cat, [(0, 0)] * lead + [(0, total - n)])
    return cat.reshape(cat.shape[:lead] + (total // 128, 128))


def _unpack(buf, shapes, lead):
    sizes = [int(np.prod(shp)) for shp in shapes]
    out, off = [], 0
    if all(n % 128 == 0 for n in sizes):
        for shp, n in zip(shapes, sizes):
            rows = buf[(slice(None),) * lead + (slice(off, off + n // 128),)]
            out.append(rows.reshape(buf.shape[:lead] + tuple(shp)))
            off += -(-(n // 128) // _PIECE_ROWS) * _PIECE_ROWS
        return out
    flat = buf.reshape(buf.shape[:lead] + (-1,))
    for shp, n in zip(shapes, sizes):
        out.append(flat[..., off:off + n].reshape(buf.shape[:lead] + tuple(shp)))
        off += n
    return out


def _to_whole(g, axis):
    if axis == 1:
        return g.transpose(1, 0, 2, 3).reshape(g.shape[1], N_DEV * g.shape[2], g.shape[3])
    return g.transpose(1, 2, 0, 3).reshape(g.shape[1], g.shape[2], N_DEV * g.shape[3])


def _to_shards(w, axis):
    L, R, C = w.shape
    if axis == 1:
        return w.reshape(L, N_DEV, R // N_DEV, C).transpose(1, 0, 2, 3)
    return w.reshape(L, R, N_DEV, C // N_DEV).transpose(2, 0, 1, 3)


def kernel(x, mem, g_mix, w_in, b_fox_forget, w_gla_gate, b_gla_gate, g_gla_out, g_mla_q, w_mla_uq, g_mla_kv, w_mla_ukv, b_branch_gate, w_up_fox, w_up_gla, w_up_mla, w_out, g_xa, g_mem, w_xq, w_xkv, w_xo, g_mlp, w_mlp1, w_mlp2, g_final, loss_target, m_g_mix, m_w_in, m_b_fox_forget, m_w_gla_gate, m_b_gla_gate, m_g_gla_out, m_g_mla_q, m_w_mla_uq, m_g_mla_kv, m_w_mla_ukv, m_b_branch_gate, m_w_up_fox, m_w_up_gla, m_w_up_mla, m_w_out, m_g_xa, m_g_mem, m_w_xq, m_w_xkv, m_w_xo, m_g_mlp, m_w_mlp1, m_w_mlp2, m_g_final, v_g_mix, v_w_in, v_b_fox_forget, v_w_gla_gate, v_b_gla_gate, v_g_gla_out, v_g_mla_q, v_w_mla_uq, v_g_mla_kv, v_w_mla_ukv, v_b_branch_gate, v_w_up_fox, v_w_up_gla, v_w_up_mla, v_w_out, v_g_xa, v_g_mem, v_w_xq, v_w_xkv, v_w_xo, v_g_mlp, v_w_mlp1, v_w_mlp2, v_g_final):
    wts = dict(zip(_WEIGHTS, (g_mix, w_in, b_fox_forget, w_gla_gate, b_gla_gate, g_gla_out, g_mla_q, w_mla_uq, g_mla_kv,
                              w_mla_ukv, b_branch_gate, w_up_fox, w_up_gla, w_up_mla, w_out, g_xa, g_mem, w_xq, w_xkv, w_xo,
                              g_mlp, w_mlp1, w_mlp2, g_final)))
    mom1 = dict(zip(_WEIGHTS, (m_g_mix, m_w_in, m_b_fox_forget, m_w_gla_gate, m_b_gla_gate, m_g_gla_out, m_g_mla_q,
                               m_w_mla_uq, m_g_mla_kv, m_w_mla_ukv, m_b_branch_gate, m_w_up_fox, m_w_up_gla, m_w_up_mla,
                               m_w_out, m_g_xa, m_g_mem, m_w_xq, m_w_xkv, m_w_xo, m_g_mlp, m_w_mlp1, m_w_mlp2, m_g_final)))
    mom2 = dict(zip(_WEIGHTS, (v_g_mix, v_w_in, v_b_fox_forget, v_w_gla_gate, v_b_gla_gate, v_g_gla_out, v_g_mla_q,
                               v_w_mla_uq, v_g_mla_kv, v_w_mla_ukv, v_b_branch_gate, v_w_up_fox, v_w_up_gla, v_w_up_mla,
                               v_w_out, v_g_xa, v_g_mem, v_w_xq, v_w_xkv, v_w_xo, v_g_mlp, v_w_mlp1, v_w_mlp2, v_g_final)))
    depth = g_mix.shape[0]

    names = [n for n, _ in _SHARDED]
    axes = dict(_SHARDED)
    shard = {n: wts[n] for n in names}
    shard['w_in'] = _pad_w_in(w_in)
    rep = {n: wts[n] for n in _REPLICATED}
    ps = [_LayerParams(rep, l) for l in range(depth)]

    def gather(group, l):
        rider = _gather_rider([shard[n][l:l + 1].astype(BF16) for n in group], [axes[n] for n in group])
        return rider, lambda whole: ps[l].w.update({n: w[0] for n, w in zip(group, whole)})

    first, sink = gather(['w_in'], 0)
    sink(_run_rider(first, name="gather_w_in_0"))
    narrow = ['w_gla_gate', 'w_mla_uq', 'w_mla_ukv', 'w_up_fox', 'w_up_gla', 'w_up_mla']
    hooks = {(0, 'in_big'): [gather(narrow + ['w_out', 'w_xq', 'w_xkv', 'w_xo'], 0)],
             (0, 'fox_fwd'): [gather(['w_mlp1', 'w_mlp2'], 0)]}
    ahead = (('mla_fwd', ['w_in'] + narrow + ['w_out']), ('xo', ['w_xkv']), ('mlp1', ['w_mlp1']),
             ('mlp2', ['w_mlp2', 'w_xq', 'w_xo']))
    assert sorted(n for _, group in ahead for n in group) == sorted(names)
    for l in range(1, depth):
        for key, group in ahead:
            hooks.setdefault((l - 1, key), []).append(gather(group, l))

    core = lax.axis_index("c").astype(jnp.int32).reshape(1)
    late = ['w_in', 'w_gla_gate', 'w_mla_uq', 'w_mla_ukv']
    groups = {'early': [n for n in names if n not in late], 'late': late}
    small_grads, landed = {}, {}

    def to_slots(gl, axis):
        if gl.ndim != 3:
            return _to_shards(gl[None], axis)
        blocks, rows, _ = gl.shape
        if axis == 1:
            return gl.reshape(blocks, N_DEV, rows // N_DEV, LANES).transpose(1, 0, 2, 3)
        return gl.reshape(N_DEV, blocks // N_DEV, rows, LANES)

    def slot_shape(n):
        _, a, b = shard[n].shape
        return (b // LANES, a, LANES) if n in blocked else (1, a, b)

    def from_slot(n, x):
        return x.transpose(1, 0, 2).reshape((1,) + shard[n].shape[1:]) if n in blocked else x

    blocked = {'w_in', 'w_out', 'w_xq', 'w_xkv', 'w_xo', 'w_mlp1', 'w_mlp2'}

    def ride(key, rider, sink, name):
        if key is None:
            sink(_run_rider(rider, name=name))
        else:
            hooks.setdefault(key, []).append((rider, sink))

    def exchange(l, g, which, swap_in, scatter_in):
        assert all((g[n].ndim == 3) == (n in blocked) for n in groups[which])
        slots = _pack([to_slots(g[n], axes[n]).astype(BF16) for n in groups[which]], 1)
        slots = slots.reshape((N_CHIP, 2) + slots.shape[1:])

        def swapped(got):
            paired = _pair_sum(slots, got, core, name=f"pair_grads_{which}_{l}")
            ride(scatter_in, _chip_all_to_all_rider(paired), lambda landing: landed.update({(l, which): landing}),
                 f"scatter_grads_{which}_{l}")

        ride(swap_in, _sibling_swap_rider(slots), swapped, f"swap_grads_{which}_{l}")

    def half_done(l, g):
        exchange(l, g, 'early', (l, 'fox_bwd'), (l, 'mla_bwd'))

    def matrices_done(l, g):
        if l > 0:
            exchange(l, g, 'late', (l - 1, 'xa_bwd'), (l - 1, 'fox_bwd'))
        else:
            exchange(l, g, 'late', None, (l, 'd_in'))

    def layer_done(l, g):
        small_grads[l] = g

    loss, dx, dg_final = _local_step(x[0], mem[0], loss_target[0], ps, g_final, hooks, half_done, matrices_done, layer_done)
    loss = lax.psum(loss[0, 0], _MESH_AXES)

    grad = {}
    for which, group in groups.items():
        shapes = [slot_shape(n) for n in group]
        per_layer = [_unpack(_sum_slots(landed[(l, which)], name=f"sum_grads_{which}_{l}"), shapes, 0) for l in range(depth)]
        grad.update({n: jnp.concatenate([from_slot(n, per_layer[l][i]) for l in range(depth)], axis=0)
                     for i, n in enumerate(group)})
    grad['w_in'] = _unpad_w_in(grad['w_in'])
    grads = small_grads
    small = [dg_final if n == 'g_final' else jnp.stack([grads[l][n] for l in range(depth)]) for n in _REPLICATED]
    small_shapes = [wts[n].shape for n in _REPLICATED]
    small_sum = _sum_slots(_all_gather(_pack(small, 0, _SMALL_ROW_PAD), name="gather_small_grads"), name="sum_small_grads")
    grad.update(dict(zip(_REPLICATED, _unpack(small_sum, small_shapes, 0))))

    delta, new_m, new_v = {}, {}, {}
    for n, _ in _SHARDED:
        delta[n], new_m[n], new_v[n] = _adamw(wts[n], grad[n], mom1[n], mom2[n], name=f"adamw_{n}")
    packed = [_pack([d[n] for n in _REPLICATED], 0, _SMALL_ROW_PAD) for d in (wts, mom1, mom2)]
    outs = _adamw(packed[0], small_sum, packed[1], packed[2], name="adamw_small")
    for d, o in zip((delta, new_m, new_v), outs):
        d.update(dict(zip(_REPLICATED, _unpack(o, small_shapes, 0))))

    return (loss, dx[None], *[grad[n] for n in _WEIGHTS], *[delta[n] for n in _WEIGHTS],
            *[new_m[n] for n in _WEIGHTS], *[new_v[n] for n in _WEIGHTS])
```

```python
import functools

import jax
import jax.numpy as jnp
import numpy as np
from jax import lax
from jax.experimental import pallas as pl
from jax.experimental.pallas import tpu as pltpu

F32 = jnp.float32
BF16 = jnp.bfloat16

EPS = 1e-6
CHUNK = 64
FOX_HEADS, FOX_HD = 4, 64
GLA_HEADS, GLA_DK, GLA_DV, GLA_RANK, GLA_TAU = 4, 64, 128, 16, 16.0
MLA_HEADS, MLA_Q_RANK, MLA_KV_RANK, MLA_NOPE, MLA_ROPE, MLA_VD = 4, 256, 128, 64, 32, 64
ROPE_BASE = 10000.0
XA_HEADS, XA_HD = 4, 128
ADAM_LR, ADAM_B1, ADAM_B2, ADAM_EPS, ADAM_WD, ADAM_STEP = 0.001, 0.9, 0.999, 1e-08, 0.01, 10

N_DEV = 8
V7X_VMEM_LIMIT = 56 * 1024 * 1024
NEG = -1e30

NN = ((1,), (0,))
NT = ((1,), (1,))
TN = ((0,), (0,))


def _dot(a, b, dims):
    return lax.dot_general(a.astype(BF16), b.astype(BF16), (dims, ((), ())), preferred_element_type=F32)


@jax.custom_vjp
def bdot(a, b):
    return _dot(a, b, NN)


bdot.defvjp(lambda a, b: (_dot(a, b, NN), (a, b)),
            lambda res, g: (_dot(g, res[1], NT), _dot(res[0], g, TN)))


@jax.custom_vjp
def bdot_nt(a, b):
    return _dot(a, b, NT)


bdot_nt.defvjp(lambda a, b: (_dot(a, b, NT), (a, b)),
               lambda res, g: (_dot(g, res[1], NN), _dot(g, res[0], TN)))


@jax.custom_vjp
def bdot_tn(a, b):
    return _dot(a, b, TN)


bdot_tn.defvjp(lambda a, b: (_dot(a, b, TN), (a, b)),
               lambda res, g: (_dot(res[1], g, NT), _dot(res[0], g, NN)))


def _split2(x):
    hi = x.astype(BF16)
    lo = (x - hi.astype(F32)).astype(BF16)
    return hi, lo


def _tri(n, lower):
    r = lax.broadcasted_iota(jnp.int32, (n, n), 0)
    c = lax.broadcasted_iota(jnp.int32, (n, n), 1)
    return jnp.where((r >= c) if lower else (r <= c), 1.0, 0.0).astype(BF16)


def _log_sigmoid(x):
    return jnp.minimum(x, 0.0) - jnp.log(1.0 + jnp.exp(-jnp.abs(x)))


def _sigmoid(x):
    return 1.0 / (1.0 + jnp.exp(-x))


def _rms(x, g):
    return x * lax.rsqrt(jnp.mean(x * x, axis=-1, keepdims=True) + EPS) * g


def _pick(dim, prefs):
    for p in prefs:
        if dim % p == 0:
            return p
    return dim


def _params(sem):
    return pltpu.CompilerParams(dimension_semantics=sem, vmem_limit_bytes=V7X_VMEM_LIMIT)


def _rms_vjp(x, g, dy, dres):
    rstd = lax.rsqrt(jnp.mean(x * x, axis=-1, keepdims=True) + EPS)
    xh = x * rstd
    gdy = dy * g
    dx = (gdy - xh * jnp.mean(gdy * xh, axis=-1, keepdims=True)) * rstd
    return (dx if dres is None else dx + dres), jnp.sum(dy * xh, axis=0, keepdims=True)


def _mm(a, b, *, mode, out_dtype, name, act=None, residual=None, drelu_of=None, norm_bwd=None, b_cols=None,
        col_shards=None, rider=None, tm=None, tn=None, tk=None):
    b_off, b_width = b_cols or (0, b.shape[1])
    if mode == 'nn':
        (M, K), N = a.shape, b_width
    elif mode == 'nt':
        (M, K), N = a.shape, b.shape[0]
    else:
        (K, M), N = a.shape, b_width
    tm = tm or _pick(M, (1024, 512, 256, 128))
    tn = tn or _pick(N, (1024, 1920, 1152, 768, 640, 512, 384, 256, 128))
    tk = tk or _pick(K, (1024, 1920, 1152, 640, 512, 256, 128))
    nk = K // tk
    dims = {'nn': NN, 'nt': NT, 'tn': TN}[mode]
    a_spec = pl.BlockSpec((tk, tm), lambda i, j, k: (k, i)) if mode == 'tn' else pl.BlockSpec((tm, tk), lambda i, j, k: (i, k))
    if mode == 'nt':
        b_spec = pl.BlockSpec((tn, tk), lambda i, j, k, o=b_off // tk: (j, k + o))
    else:
        b_spec = pl.BlockSpec((tk, tn), lambda i, j, k, o=b_off // tn: (k, j + o))
    o_spec = pl.BlockSpec((tm, tn), lambda i, j, k: (i, j))
    extra = [e for e in (residual, drelu_of) if e is not None]
    extra_specs = [o_spec] * len(extra)
    out_shape, out_specs, n_out = jax.ShapeDtypeStruct((M, N), out_dtype), o_spec, 1
    if col_shards:
        n_sh = N // col_shards
        assert tn % n_sh == 0 and not extra and norm_bwd is None
        out_shape = jax.ShapeDtypeStruct((col_shards, M, n_sh), out_dtype)
        out_specs = pl.BlockSpec((tn // n_sh, tm, n_sh), lambda i, j, k: (j, i, 0))
    if norm_bwd is not None:
        x_in, g_in, dres_in = norm_bwd
        assert tn == N and residual is None and drelu_of is None
        vec = pl.BlockSpec((1, N), lambda i, j, k: (0, 0))
        extra, extra_specs = [x_in, g_in.reshape(1, N), dres_in], [o_spec, vec, o_spec]
        out_shape = (jax.ShapeDtypeStruct((M, N), F32), jax.ShapeDtypeStruct((M, N), BF16), jax.ShapeDtypeStruct((1, N), F32))
        out_specs, n_out = (o_spec, o_spec, vec), 3

    grid = (M // tm, N // tn, nk)
    r_ins, r_in_specs, r_outs, r_out_specs, r_scratch, split = _carry(
        rider, 2 + len(extra), n_out, lambda: functools.reduce(jnp.logical_and, [pl.program_id(d) == 0 for d in range(3)]),
        lambda: functools.reduce(jnp.logical_and, [pl.program_id(d) == grid[d] - 1 for d in range(3)]))

    def body(*refs):
        a_ref, b_ref, *rest = split(refs)
        o_ref = rest[len(extra)]
        first_rows = pl.program_id(0) == 0
        at = a_ref[...]
        if act == 'relu2':
            at = jnp.square(jnp.maximum(at.astype(F32), 0.0))
        part = _dot(at, b_ref[...], dims)

        def finish(acc):
            if norm_bwd is not None:
                dx, dg = _rms_vjp(rest[0][...], rest[1][...], acc, rest[2][...])
                o_ref[...] = dx
                rest[len(extra) + 1][...] = dx.astype(BF16)
                dg_ref = rest[len(extra) + 2]

                @pl.when(first_rows)
                def _():
                    dg_ref[...] = dg

                @pl.when(jnp.logical_not(first_rows))
                def _():
                    dg_ref[...] += dg
                return
            idx = 0
            if residual is not None:
                acc = acc + rest[idx][...]
                idx += 1
            if drelu_of is not None:
                acc = acc * (2.0 * jnp.maximum(rest[idx][...].astype(F32), 0.0))
            if col_shards:
                for t in range(tn // n_sh):
                    o_ref[t] = acc[:, t * n_sh:(t + 1) * n_sh].astype(out_dtype)
            else:
                o_ref[...] = acc.astype(out_dtype)

        if nk == 1:
            finish(part)
        else:
            acc_ref = rest[len(extra) + n_out]
            k = pl.program_id(2)

            @pl.when(k == 0)
            def _():
                acc_ref[...] = part

            @pl.when(k > 0)
            def _():
                acc_ref[...] += part

            @pl.when(k == nk - 1)
            def _():
                finish(acc_ref[...])

    scratch = [] if nk == 1 else [pltpu.VMEM((tm, tn), F32)]
    if rider is not None:
        own_shapes, own_specs = (out_shape, out_specs) if n_out > 1 else ((out_shape,), (out_specs,))
        res = pl.pallas_call(
            body, name=name, out_shape=(*own_shapes, *r_outs), grid=grid, in_specs=[a_spec, b_spec] + extra_specs + r_in_specs,
            out_specs=(*own_specs, *r_out_specs), scratch_shapes=scratch + r_scratch,
            compiler_params=_params(("arbitrary", "arbitrary", "arbitrary")),
        )(a, b, *extra, *r_ins)
        return (*res[:n_out], rider.post(res[n_out:]))
    return pl.pallas_call(
        body, name=name, out_shape=out_shape, grid=grid, in_specs=[a_spec, b_spec] + extra_specs, out_specs=out_specs,
        scratch_shapes=scratch,
        compiler_params=_params(("arbitrary" if norm_bwd is not None else "parallel", "parallel", "arbitrary")),
    )(a, b, *extra)


def _rms_fwd(x, g, *, name, out_dtype=BF16):
    S, D = x.shape
    tr = _pick(S, (512, 256, 128))

    def body(x_ref, g_ref, o_ref):
        o_ref[...] = _rms(x_ref[...], g_ref[...]).astype(out_dtype)

    return pl.pallas_call(
        body, name=name, out_shape=jax.ShapeDtypeStruct((S, D), out_dtype), grid=(S // tr,),
        in_specs=[pl.BlockSpec((tr, D), lambda i: (i, 0)), pl.BlockSpec((1, D), lambda i: (0, 0))],
        out_specs=pl.BlockSpec((tr, D), lambda i: (i, 0)),
        compiler_params=_params(("parallel",)),
    )(x, g.reshape(1, D))


def _rms_bwd(x, g, dy, dres, *, name):
    S, D = x.shape
    tr = _pick(S, (512, 256, 128))

    def body(x_ref, g_ref, dy_ref, *rest):
        dx_ref, dxb_ref, dg_ref = rest[-3], rest[-2], rest[-1]
        dx, part = _rms_vjp(x_ref[...], g_ref[...], dy_ref[...].astype(F32), None if dres is None else rest[0][...])
        dx_ref[...] = dx
        dxb_ref[...] = dx.astype(BF16)

        @pl.when(pl.program_id(0) == 0)
        def _():
            dg_ref[...] = part

        @pl.when(pl.program_id(0) > 0)
        def _():
            dg_ref[...] += part

    row = pl.BlockSpec((tr, D), lambda i: (i, 0))
    vec = pl.BlockSpec((1, D), lambda i: (0, 0))
    ins = [x, g.reshape(1, D), dy] + ([dres] if dres is not None else [])
    return pl.pallas_call(
        body, name=name,
        out_shape=(jax.ShapeDtypeStruct((S, D), F32), jax.ShapeDtypeStruct((S, D), BF16), jax.ShapeDtypeStruct((1, D), F32)),
        grid=(S // tr,),
        in_specs=[row, vec, row] + ([row] if dres is not None else []),
        out_specs=(row, row, vec),
        compiler_params=_params(("arbitrary",)),
    )(*ins)


def _loss_head(x, g, target, *, name):
    S, D = x.shape
    tr = _pick(S, (512, 256, 128))

    def body(x_ref, g_ref, t_ref, l_ref, dx_ref, dxb_ref, dg_ref):
        x_ = x_ref[...]
        g_ = g_ref[...]
        rstd = lax.rsqrt(jnp.mean(x_ * x_, axis=-1, keepdims=True) + EPS)
        xh = x_ * rstd
        err = xh * g_ - t_ref[...]
        lpart = (0.5 / D) * jnp.sum(jnp.sum(err * err, axis=-1, keepdims=True), axis=0, keepdims=True)
        dy = err * (1.0 / D)
        gdy = dy * g_
        dx = (gdy - xh * jnp.mean(gdy * xh, axis=-1, keepdims=True)) * rstd
        dx_ref[...] = dx
        dxb_ref[...] = dx.astype(BF16)
        gpart = jnp.sum(dy * xh, axis=0, keepdims=True)

        @pl.when(pl.program_id(0) == 0)
        def _():
            dg_ref[...] = gpart
            l_ref[...] = lpart

        @pl.when(pl.program_id(0) > 0)
        def _():
            dg_ref[...] += gpart
            l_ref[...] += lpart

    row = pl.BlockSpec((tr, D), lambda i: (i, 0))
    vec = pl.BlockSpec((1, D), lambda i: (0, 0))
    return pl.pallas_call(
        body, name=name,
        out_shape=(jax.ShapeDtypeStruct((1, 1), F32), jax.ShapeDtypeStruct((S, D), F32), jax.ShapeDtypeStruct((S, D), BF16),
                   jax.ShapeDtypeStruct((1, D), F32)),
        grid=(S // tr,),
        in_specs=[row, vec, row],
        out_specs=(pl.BlockSpec((1, 1), lambda i: (0, 0)), row, row, vec),
        compiler_params=_params(("arbitrary",)),
    )(x, g.reshape(1, D), target)


def _mask_of(mask, tq, tk, keys_first=False):
    shape, q_axis = ((tk, tq), 1) if keys_first else ((tq, tk), 0)
    qpos = lax.broadcasted_iota(jnp.int32, shape, q_axis)
    kpos = lax.broadcasted_iota(jnp.int32, shape, 1 - q_axis)
    if mask == 'causal':
        return kpos <= qpos
    return kpos <= (qpos | (CHUNK - 1))


LANES = 128
LOG2E = 1.4426950408889634


def _lane_group(j, w, width):
    lane = lax.broadcasted_iota(jnp.int32, (1, width), 1)
    return (lane >= j * w) & (lane < (j + 1) * w)


def _only(x, j, w):
    if w == x.shape[1]:
        return x
    return jnp.where(_lane_group(j, w, x.shape[1]), x, jnp.zeros_like(x))


def _side_by_side(xs):
    return xs[0] if len(xs) == 1 else jnp.concatenate(xs, axis=1)


def _on_top(xs):
    return xs[0] if len(xs) == 1 else jnp.concatenate(xs, axis=0)


def _stacked(x, hp, w):
    return _on_top([_only(x, j, w) for j in range(hp)])


def _col_block(entry, rows, idx):
    arr, off, width = entry
    return pl.BlockSpec((rows, width), lambda i, j, o=off // width: (idx(i, j), o))


def _attn_fwd(qk, v, H, cq, ck, *, scale, mask, name, rider=None):
    Sq, Sk = qk[0][0][0].shape[0], v[0].shape[0]
    dv = v[2] // H
    w0 = qk[0][2]
    hp = LANES // w0
    G = H // hp
    assert dv == w0 and not qk[0][3] and all(sh and H * w == LANES for _, _, w, sh in qk[1:])
    tq = _pick(Sq, (512, 256, 128))
    tk = tq if mask else _pick(Sk, (512, 256, 128))
    nq, nk = Sq // tq, Sk // tk
    bias = cq is not None
    npart = len(qk)

    def body(*refs):
        refs = split(refs)
        q_refs, k_refs = refs[0:2 * npart:2], refs[1:2 * npart:2]
        v_ref = refs[2 * npart]
        cq_ref, ck_ref = (refs[2 * npart + 1], refs[2 * npart + 2]) if bias else (None, None)
        o_ref, lse_ref, m_s, l_s, acc_s = refs[-5:]
        qi, ki = pl.program_id(0), pl.program_id(1)

        @pl.when(ki == 0)
        def _():
            m_s[...] = jnp.full(m_s.shape, NEG, F32)
            l_s[...] = jnp.zeros(l_s.shape, F32)
            acc_s[...] = jnp.zeros(acc_s.shape, F32)

        def rows_of(vals):
            return _on_top([jnp.broadcast_to(r, (w0, tq)) for r in vals])

        def compute(masked):
            keep = _mask_of(mask, tq, tk, keys_first=True) if masked else None
            for g in range(G):
                lanes = slice(g * LANES, (g + 1) * LANES)
                q128, k128, v128 = q_refs[0][:, lanes], k_refs[0][:, lanes], v_ref[:, lanes]
                ps, alphas = [], []
                extras = list(zip(qk, q_refs, k_refs))[1:]
                k_all = _side_by_side([k128] + [k_ref[...] for _, _, k_ref in extras])
                for j in range(hp):
                    h = g * hp + j
                    q_all = _side_by_side([_only(q128, j, w0)] + [_only(q_ref[...], h, w) for (_, _, w, _), q_ref, _ in extras])
                    s = _dot(k_all, q_all, NT) * scale
                    if bias:
                        s = s + (cq_ref[h:h + 1, :] - ck_ref[:, h:h + 1])
                    if masked:
                        s = jnp.where(keep, s, NEG)
                    m_prev = m_s[h:h + 1, :]
                    m_new = jnp.maximum(m_prev, jnp.max(s, axis=0, keepdims=True))
                    alpha = jnp.exp(m_prev - m_new)
                    p = jnp.exp(s - m_new)
                    l_s[h:h + 1, :] = alpha * l_s[h:h + 1, :] + jnp.sum(p, axis=0, keepdims=True)
                    m_s[h:h + 1, :] = m_new
                    ps.append(p.astype(BF16))
                    alphas.append(alpha)
                acc_s[g] = rows_of(alphas) * acc_s[g] + _dot(_stacked(v128, hp, w0), _on_top(ps), TN)

        if mask is None:
            compute(False)
        else:
            pl.when(ki < qi)(lambda: compute(False))
            pl.when(ki == qi)(lambda: compute(True))

        @pl.when(ki == ((nk - 1) if mask is None else qi))
        def _():
            for g in range(G):
                norm = acc_s[g] / rows_of([l_s[g * hp + j:g * hp + j + 1, :] for j in range(hp)])
                o_ref[:, g * LANES:(g + 1) * LANES] = norm.T.astype(BF16)
            lse_ref[...] = jnp.zeros(lse_ref.shape, F32)
            lse_ref[0:H, :] = m_s[0:H, :] + jnp.log(l_s[0:H, :])

    q_idx = lambda i, j: i
    k_idx = (lambda i, j: jnp.minimum(i, j)) if mask else (lambda i, j: j)
    ins, in_specs = [], []
    for q_e, k_e, _, _ in qk:
        ins += [q_e[0], k_e[0]]
        in_specs += [_col_block(q_e, tq, q_idx), _col_block(k_e, tk, k_idx)]
    ins.append(v[0])
    in_specs.append(_col_block(v, tk, k_idx))
    if bias:
        in_specs += [pl.BlockSpec((8, tq), lambda i, j: (0, i)), pl.BlockSpec((tk, 8), lambda i, j: (k_idx(i, j), 0))]
        ins += [cq, ck]
    r_ins, r_in_specs, r_outs, r_out_specs, r_scratch, split = _carry(
        rider, len(ins), 2, lambda: (pl.program_id(0) == 0) & (pl.program_id(1) == 0),
        lambda: (pl.program_id(0) == nq - 1) & (pl.program_id(1) == nk - 1))
    res = pl.pallas_call(
        body, name=name,
        out_shape=(jax.ShapeDtypeStruct((Sq, H * dv), BF16), jax.ShapeDtypeStruct((8, Sq), F32), *r_outs),
        grid=(nq, nk), in_specs=in_specs + r_in_specs,
        out_specs=(pl.BlockSpec((tq, H * dv), lambda i, j: (i, 0)), pl.BlockSpec((8, tq), lambda i, j: (0, i)), *r_out_specs),
        scratch_shapes=[pltpu.VMEM((8, tq), F32), pltpu.VMEM((8, tq), F32), pltpu.VMEM((G, LANES, tq), F32)] + r_scratch,
        compiler_params=_params(("arbitrary", "arbitrary")) if rider else _params(("parallel", "arbitrary")),
    )(*ins, *r_ins)
    return (res[0], res[1], rider.post(res[2:])) if rider else res


def _attn_bwd(qk, v, H, o, do, lse, cq, ck, *, scale, mask, name, rider=None):
    Sq, Sk = qk[0][0][0].shape[0], v[0].shape[0]
    dv = v[2] // H
    w0 = qk[0][2]
    hp = LANES // w0
    G = H // hp
    tq = _pick(Sq, (512, 256, 128))
    tk = tq if mask else _pick(Sk, (512, 256, 128))
    nq, nk = Sq // tq, Sk // tk
    bias = cq is not None
    npart = len(qk)
    n_in = 2 * npart + 4 + (2 if bias else 0)

    def body(*refs):
        refs = split(refs)
        q_refs, k_refs = refs[0:2 * npart:2], refs[1:2 * npart:2]
        v_ref, o_ref, do_ref, lse_ref = refs[2 * npart:2 * npart + 4]
        cq_ref, ck_ref = (refs[2 * npart + 4], refs[2 * npart + 5]) if bias else (None, None)
        outs = refs[n_in:]
        dq_refs, dk_refs, dv_ref = outs[:npart], outs[npart:2 * npart], outs[2 * npart]
        dck_ref, dcq_ref = (outs[2 * npart + 1], outs[2 * npart + 2]) if bias else (None, None)
        dk_accs, dv_acc = refs[-(npart + 1):-1], refs[-1]
        ki, qi = pl.program_id(0), pl.program_id(1)
        first_q = ki if mask else 0

        @pl.when((ki == 0) & (qi == 0))
        def _():
            for r in dq_refs:
                r[...] = jnp.zeros(r.shape, F32)
            if bias:
                dcq_ref[...] = jnp.zeros(dcq_ref.shape, F32)

        @pl.when(qi == first_q)
        def _():
            for r in dk_accs:
                r[...] = jnp.zeros(r.shape, F32)
            dv_acc[...] = jnp.zeros(dv_acc.shape, F32)
            if bias:
                dck_ref[...] = jnp.zeros(dck_ref.shape, F32)

        def compute(masked):
            keep = _mask_of(mask, tq, tk, keys_first=True) if masked else None
            rows = pl.ds(pl.multiple_of(qi * tq, tq), tq)
            extras = list(zip(qk, q_refs, k_refs, dq_refs, dk_accs))[1:]
            for g in range(G):
                lanes = slice(g * LANES, (g + 1) * LANES)
                q128, k128, v128 = q_refs[0][:, lanes], k_refs[0][:, lanes], v_ref[:, lanes]
                do128, o128 = do_ref[:, lanes], o_ref[:, lanes]
                prod = do128.astype(F32) * o128.astype(F32)
                ps, dss = [], []
                k_all = _side_by_side([k128] + [e[2][...] for e in extras])
                for j in range(hp):
                    h = g * hp + j
                    q_all = _side_by_side([_only(q128, j, w0)] + [_only(e[1][...], h, e[0][2]) for e in extras])
                    s = _dot(k_all, q_all, NT) * (scale * LOG2E)
                    if bias:
                        s = s - ck_ref[:, h:h + 1] * LOG2E
                    if masked:
                        s = jnp.where(keep, s, NEG)
                    row = lse_ref[h:h + 1, :] - cq_ref[h:h + 1, :] if bias else lse_ref[h:h + 1, :]
                    p = jnp.exp2(s - row * LOG2E)
                    dp = _dot(v128, _only(do128, j, w0), NT)
                    delta = jnp.sum(_only(prod, j, w0), axis=1, keepdims=True).T
                    ds = p * (dp - delta)
                    if bias:
                        dck_ref[:, h:h + 1] -= jnp.sum(ds, axis=1, keepdims=True)
                        dcq_ref[h:h + 1, rows] += jnp.sum(ds, axis=0, keepdims=True)
                    ps.append(p.astype(BF16))
                    dss.append((ds * scale).astype(BF16))
                for (_, _, w, _), q_ref, k_ref, dq_ref, dk_acc in extras:
                    heads = range(g * hp, (g + 1) * hp)
                    dk_acc[...] += _dot(_side_by_side(dss), _on_top([_only(q_ref[...], h, w) for h in heads]), NN)
                    dq_ref[rows, :] += _dot(_on_top(dss), _on_top([_only(k_ref[...], h, w) for h in heads]), TN)
                dv_acc[:, lanes] += _dot(_side_by_side(ps), _stacked(do128, hp, w0), NN)
                dk_accs[0][:, lanes] += _dot(_side_by_side(dss), _stacked(q128, hp, w0), NN)
                dq_refs[0][rows, lanes] += _dot(_on_top(dss), _stacked(k128, hp, w0), TN)

        if mask is None:
            compute(False)
        else:
            pl.when(qi > ki)(lambda: compute(False))
            pl.when(qi == ki)(lambda: compute(True))

        @pl.when(qi == nq - 1)
        def _():
            for r, acc in zip(dk_refs, dk_accs):
                r[...] = acc[...]
            dv_ref[...] = dv_acc[...]

    q_idx = (lambda j, i: jnp.maximum(i, j)) if mask else (lambda j, i: i)
    k_idx = lambda j, i: j
    ins, in_specs, dq_shapes, dq_specs, dk_shapes, dk_specs, scratch = [], [], [], [], [], [], []
    for q_e, k_e, w, shared in qk:
        ins += [q_e[0], k_e[0]]
        in_specs += [_col_block(q_e, tq, q_idx), _col_block(k_e, tk, k_idx)]
        dq_shapes.append(jax.ShapeDtypeStruct((Sq, H * w), F32))
        dq_specs.append(pl.BlockSpec((Sq, H * w), lambda j, i: (0, 0)))
        kw = k_e[2]
        dk_shapes.append(jax.ShapeDtypeStruct((Sk, kw), F32))
        dk_specs.append(pl.BlockSpec((tk, kw), lambda j, i: (j, 0)))
        scratch.append(pltpu.VMEM((tk, kw), F32))
    row_q = lambda width: pl.BlockSpec((tq, width), lambda j, i: (q_idx(j, i), 0))
    per_q = pl.BlockSpec((8, tq), lambda j, i: (0, q_idx(j, i)))
    ins += [v[0], o, do, lse]
    in_specs += [_col_block(v, tk, k_idx), row_q(H * dv), row_q(H * dv), per_q]
    out_shape = dq_shapes + dk_shapes + [jax.ShapeDtypeStruct((Sk, H * dv), F32)]
    out_specs = dq_specs + dk_specs + [pl.BlockSpec((tk, H * dv), lambda j, i: (j, 0))]
    if bias:
        in_specs += [per_q, pl.BlockSpec((tk, 8), lambda j, i: (j, 0))]
        ins += [cq, ck]
        out_shape += [jax.ShapeDtypeStruct((Sk, 8), F32), jax.ShapeDtypeStruct((8, Sq), F32)]
        out_specs += [pl.BlockSpec((tk, 8), lambda j, i: (j, 0)), pl.BlockSpec((8, Sq), lambda j, i: (0, 0))]
    scratch.append(pltpu.VMEM((tk, H * dv), F32))
    n_out = len(out_shape)
    r_ins, r_in_specs, r_outs, r_out_specs, r_scratch, split = _carry(
        rider, len(ins), n_out, lambda: (pl.program_id(0) == 0) & (pl.program_id(1) == 0),
        lambda: (pl.program_id(0) == nk - 1) & (pl.program_id(1) == nq - 1))
    res = pl.pallas_call(
        body, name=name, out_shape=tuple(out_shape + r_outs), grid=(nk, nq), in_specs=in_specs + r_in_specs,
        out_specs=tuple(out_specs + r_out_specs), scratch_shapes=scratch + r_scratch,
        compiler_params=_params(("arbitrary", "arbitrary")),
    )(*ins, *r_ins)
    own = (list(res[:npart]), list(res[npart:2 * npart]), res[2 * npart]) + tuple(res[2 * npart + 1:n_out])
    return own + (rider.post(res[n_out:]),) if rider else own


def _split3_dot(x, t):
    hi = x.astype(BF16)
    r1 = x - hi.astype(F32)
    mid = r1.astype(BF16)
    lo = (r1 - mid.astype(F32)).astype(BF16)
    return _dot(hi, t, NN) + _dot(mid, t, NN) + _dot(lo, t, NN)


def _fox_cum_fwd(ff_t, b, *, name):
    _, S = ff_t.shape
    tb = _pick(S, (512, 256, 128))

    def body(f_ref, b_ref, o_ref, carry):
        @pl.when(pl.program_id(0) == 0)
        def _():
            carry[...] = jnp.zeros(carry.shape, F32)

        lf = _log_sigmoid(f_ref[...] + b_ref[...])
        o_ref[...] = _split3_dot(lf, _tri(tb, False)) + carry[...]
        carry[...] += jnp.sum(lf, axis=1, keepdims=True)

    return pl.pallas_call(
        body, name=name, out_shape=jax.ShapeDtypeStruct((8, S), F32), grid=(S // tb,),
        in_specs=[pl.BlockSpec((8, tb), lambda i: (0, i)), pl.BlockSpec((8, 1), lambda i: (0, 0))],
        out_specs=pl.BlockSpec((8, tb), lambda i: (0, i)),
        scratch_shapes=[pltpu.VMEM((8, 1), F32)],
        compiler_params=_params(("arbitrary",)),
    )(ff_t, b)


def _fox_cum_bwd(ff_t, b, dcum_t, *, name):
    _, S = ff_t.shape
    tb = _pick(S, (512, 256, 128))
    nb = S // tb

    def body(f_ref, b_ref, dc_ref, df_ref, db_ref, carry):
        @pl.when(pl.program_id(0) == 0)
        def _():
            carry[...] = jnp.zeros(carry.shape, F32)
            db_ref[...] = jnp.zeros(db_ref.shape, F32)

        dc = dc_ref[...]
        dlf = _split3_dot(dc, _tri(tb, True)) + carry[...]
        carry[...] += jnp.sum(dc, axis=1, keepdims=True)
        df = dlf * _sigmoid(-(f_ref[...] + b_ref[...]))
        df_ref[...] = df
        db_ref[...] += jnp.sum(df, axis=1, keepdims=True)

    rev = lambda i: (0, nb - 1 - i)
    return pl.pallas_call(
        body, name=name,
        out_shape=(jax.ShapeDtypeStruct((8, S), F32), jax.ShapeDtypeStruct((8, 1), F32)), grid=(nb,),
        in_specs=[pl.BlockSpec((8, tb), rev), pl.BlockSpec((8, 1), lambda i: (0, 0)), pl.BlockSpec((8, tb), rev)],
        out_specs=(pl.BlockSpec((8, tb), rev), pl.BlockSpec((8, 1), lambda i: (0, 0))),
        scratch_shapes=[pltpu.VMEM((8, 1), F32)],
        compiler_params=_params(("arbitrary",)),
    )(ff_t, b, dcum_t)


GLA_W = GLA_HEADS * GLA_DK
GLA_BLOCK_CHUNKS = 4


def _same_chunk(n, lower):
    r = lax.broadcasted_iota(jnp.int32, (n, n), 0)
    c = lax.broadcasted_iota(jnp.int32, (n, n), 1)
    same = (r | (CHUNK - 1)) == (c | (CHUNK - 1))
    return jnp.where(same & (r >= c) if lower else same, 1.0, 0.0).astype(BF16)


def _chunk_mix(x, t, transpose):
    hi, lo = _split2(x)
    dims = TN if transpose else NN
    return _dot(t, hi, dims) + _dot(t, lo, dims)


@jax.custom_vjp
def chunk_cumsum(x):
    return _chunk_mix(x, _same_chunk(x.shape[0], True), False)


chunk_cumsum.defvjp(lambda x: (chunk_cumsum(x), None), lambda _, g: (_chunk_mix(g, _same_chunk(g.shape[0], True), True),))


@jax.custom_vjp
def chunk_total(x):
    return _chunk_mix(x, _same_chunk(x.shape[0], False), False)


chunk_total.defvjp(lambda x: (chunk_total(x), None), lambda _, g: (_chunk_mix(g, _same_chunk(g.shape[0], False), False),))


def _gla_block(q, k, zsm, wg, bg, go, vs, rs, states):
    n_chunks = q.shape[0] // CHUNK
    la = _log_sigmoid(bdot(zsm, wg) + bg) * (1.0 / GLA_TAU)
    end = chunk_total(la)
    kd = k * jnp.exp(end - chunk_cumsum(la))
    a = jnp.exp(end)
    qs = q * (GLA_DK ** -0.5)
    lane = lax.broadcasted_iota(jnp.int32, (1, GLA_W), 1)
    outs, new_states = [], []
    for h in range(GLA_HEADS):
        kdh = kd * jnp.where((lane >= h * GLA_DK) & (lane < (h + 1) * GLA_DK), 1.0, 0.0)
        st, o = states[h], []
        for c in range(n_chunks):
            rows = slice(c * CHUNK, (c + 1) * CHUNK)
            st = st * a[c * CHUNK:c * CHUNK + 1] + bdot_tn(vs[h][rows], kdh[rows])
            o.append(bdot_nt(qs[rows], st))
        o = _rms(jnp.concatenate(o, axis=0), go)
        outs.append(o * (rs[h] * _sigmoid(rs[h])))
        new_states.append(st)
    return outs, new_states


def _gla_fwd(z, zsm, wg, bg, go, cols, *, name):
    S = z.shape[0]
    rb = GLA_BLOCK_CHUNKS * CHUNK
    nb = S // rb
    cq, ckk, cv, cr = cols
    H = GLA_HEADS

    def body(q_ref, k_ref, zsm_ref, wg_ref, bg_ref, go_ref, *rest):
        v_refs, r_refs = rest[:H], rest[H:2 * H]
        o_ref, st_ref, state = rest[2 * H], rest[2 * H + 1], rest[2 * H + 2]

        @pl.when(pl.program_id(0) == 0)
        def _():
            state[...] = jnp.zeros(state.shape, F32)

        states = [state[h] for h in range(H)]
        for h in range(H):
            st_ref[0, h] = states[h]
        outs, new_states = _gla_block(
            q_ref[...].astype(F32), k_ref[...].astype(F32), zsm_ref[...], wg_ref[...], bg_ref[...], go_ref[...],
            [v_refs[h][...].astype(F32) for h in range(H)], [r_refs[h][...].astype(F32) for h in range(H)], states)
        for h in range(H):
            o_ref[:, h * GLA_DV:(h + 1) * GLA_DV] = outs[h].astype(BF16)
            state[h] = new_states[h]

    def col(width, off):
        return pl.BlockSpec((rb, width), lambda i, o=off // width: (i, o))

    full = lambda shp: pl.BlockSpec(shp, lambda i: (0,) * len(shp))
    in_specs = [col(GLA_W, cq), col(GLA_W, ckk), pl.BlockSpec((rb, 128), lambda i: (i, 0)),
                full((128, GLA_W)), full((1, GLA_W)), full((1, GLA_DV))]
    in_specs += [col(GLA_DV, cv + h * GLA_DV) for h in range(H)] + [col(GLA_DV, cr + h * GLA_DV) for h in range(H)]
    return pl.pallas_call(
        body, name=name,
        out_shape=(jax.ShapeDtypeStruct((S, H * GLA_DV), BF16), jax.ShapeDtypeStruct((nb, H, GLA_DV, GLA_W), F32)),
        grid=(nb,), in_specs=in_specs,
        out_specs=(pl.BlockSpec((rb, H * GLA_DV), lambda i: (i, 0)),
                   pl.BlockSpec((1, H, GLA_DV, GLA_W), lambda i: (i, 0, 0, 0))),
        scratch_shapes=[pltpu.VMEM((H, GLA_DV, GLA_W), F32)],
        compiler_params=_params(("arbitrary",)),
    )(z, z, zsm, wg, bg, go, *([z] * (2 * H)))


def _gla_bwd(z, zsm, wg, bg, go, states, do, cols, *, name):
    S = z.shape[0]
    rb = GLA_BLOCK_CHUNKS * CHUNK
    nb = S // rb
    cq, ckk, cv, cr = cols
    H = GLA_HEADS

    def body(q_ref, k_ref, zsm_ref, wg_ref, bg_ref, go_ref, st_ref, do_ref, *rest):
        v_refs, r_refs = rest[:H], rest[H:2 * H]
        dq_ref, dk_ref, dv_ref, dr_ref, dzsm_ref, dwg_ref, dbg_ref, dgo_ref, dstate = rest[2 * H:]

        @pl.when(pl.program_id(0) == 0)
        def _():
            dstate[...] = jnp.zeros(dstate.shape, F32)
            dwg_ref[...] = jnp.zeros(dwg_ref.shape, F32)
            dbg_ref[...] = jnp.zeros(dbg_ref.shape, F32)
            dgo_ref[...] = jnp.zeros(dgo_ref.shape, F32)

        prim = (q_ref[...].astype(F32), k_ref[...].astype(F32), zsm_ref[...], wg_ref[...], bg_ref[...], go_ref[...],
                [v_refs[h][...].astype(F32) for h in range(H)], [r_refs[h][...].astype(F32) for h in range(H)],
                [st_ref[0, h] for h in range(H)])
        _, vjp = jax.vjp(_gla_block, *prim)
        douts = [do_ref[:, h * GLA_DV:(h + 1) * GLA_DV].astype(F32) for h in range(H)]
        dq, dk, dzs, dwg, dbg, dgo, dvs, drs, dsts = vjp((douts, [dstate[h] for h in range(H)]))
        dq_ref[...] = dq.astype(BF16)
        dk_ref[...] = dk.astype(BF16)
        dzsm_ref[...] = dzs
        dwg_ref[...] += dwg
        dbg_ref[...] += dbg
        dgo_ref[...] += dgo
        for h in range(H):
            dv_ref[:, h * GLA_DV:(h + 1) * GLA_DV] = dvs[h].astype(BF16)
            dr_ref[:, h * GLA_DV:(h + 1) * GLA_DV] = drs[h].astype(BF16)
            dstate[h] = dsts[h]

    rev = lambda i: nb - 1 - i

    def col(width, off):
        return pl.BlockSpec((rb, width), lambda i, o=off // width: (rev(i), o))

    full = lambda shp: pl.BlockSpec(shp, lambda i: (0,) * len(shp))
    rowb = lambda w: pl.BlockSpec((rb, w), lambda i: (rev(i), 0))
    in_specs = [col(GLA_W, cq), col(GLA_W, ckk), rowb(128), full((128, GLA_W)), full((1, GLA_W)), full((1, GLA_DV)),
                pl.BlockSpec((1, H, GLA_DV, GLA_W), lambda i: (rev(i), 0, 0, 0)), rowb(H * GLA_DV)]
    in_specs += [col(GLA_DV, cv + h * GLA_DV) for h in range(H)] + [col(GLA_DV, cr + h * GLA_DV) for h in range(H)]
    return pl.pallas_call(
        body, name=name,
        out_shape=(jax.ShapeDtypeStruct((S, GLA_W), BF16), jax.ShapeDtypeStruct((S, GLA_W), BF16),
                   jax.ShapeDtypeStruct((S, H * GLA_DV), BF16), jax.ShapeDtypeStruct((S, H * GLA_DV), BF16),
                   jax.ShapeDtypeStruct((S, 128), F32), jax.ShapeDtypeStruct((128, GLA_W), F32),
                   jax.ShapeDtypeStruct((1, GLA_W), F32), jax.ShapeDtypeStruct((1, GLA_DV), F32)),
        grid=(nb,), in_specs=in_specs,
        out_specs=(rowb(GLA_W), rowb(GLA_W), rowb(H * GLA_DV), rowb(H * GLA_DV), rowb(128),
                   full((128, GLA_W)), full((1, GLA_W)), full((1, GLA_DV))),
        scratch_shapes=[pltpu.VMEM((H, GLA_DV, GLA_W), F32)],
        compiler_params=_params(("arbitrary",)),
    )(z, z, zsm, wg, bg, go, states, do, *([z] * (2 * H)))


def _row_spec(entry, tr):
    if isinstance(entry, tuple):
        arr, width, off = entry
        return arr, pl.BlockSpec((tr, width), lambda i, o=off // width: (i, o))
    return entry, pl.BlockSpec((tr, entry.shape[1]), lambda i: (i, 0))


def _stage_fwd(fn, rows, consts, outs, *, name, tr=None):
    first = rows[0][0] if isinstance(rows[0], tuple) else rows[0]
    S = first.shape[0]
    tr = tr or _pick(S, (512, 256, 128))
    arrs, specs = zip(*[_row_spec(e, tr) for e in rows])
    nr, nc = len(rows), len(consts)

    def body(*refs):
        vals = [r[...].astype(F32) for r in refs[:nr + nc]]
        res = fn(*vals)
        for o_ref, val in zip(refs[nr + nc:], res):
            o_ref[...] = val.astype(o_ref.dtype)

    cspecs = [pl.BlockSpec(c.shape, lambda i, n=c.ndim: (0,) * n) for c in consts]
    return pl.pallas_call(
        body, name=name,
        out_shape=tuple(jax.ShapeDtypeStruct((S, w), dt) for w, dt in outs), grid=(S // tr,),
        in_specs=list(specs) + cspecs,
        out_specs=tuple(pl.BlockSpec((tr, w), lambda i: (i, 0)) for w, _ in outs),
        compiler_params=_params(("parallel",)),
    )(*arrs, *consts)


def _stage_bwd(fn, rows, consts, cts, n_diff, drow_dtypes, *, name, tr=None, lead=None):
    first = rows[0][0] if isinstance(rows[0], tuple) else rows[0]
    S = first.shape[0]
    tr = tr or _pick(S, (512, 256, 128))
    arrs, specs = zip(*[_row_spec(e, tr) for e in rows])
    widths = [e[1] if isinstance(e, tuple) else e.shape[1] for e in rows]
    nr, nc, nt = len(rows), len(consts), len(cts)
    n_lead, lead_width = lead or (1, widths[0])
    n_rows_out = n_diff - n_lead + 1

    def body(*refs):
        vals = [r[...].astype(F32) for r in refs[:nr + nc]]
        ct = [r[...].astype(F32) for r in refs[nr + nc:nr + nc + nt]]
        drow_refs = refs[nr + nc + nt:nr + nc + nt + n_rows_out]
        dconst_refs = refs[nr + nc + nt + n_rows_out:]
        rest_rows = vals[n_diff:nr]

        def f(diff_rows, cs):
            return tuple(fn(*diff_rows, *rest_rows, *cs))

        _, vjp = jax.vjp(f, vals[:n_diff], vals[nr:])
        drows, dcs = vjp(tuple(ct))
        off = 0
        for val, w in zip(drows[:n_lead], widths):
            drow_refs[0][:, off:off + w] = val.astype(drow_refs[0].dtype)
            off += w
        for r, val in zip(drow_refs[1:], drows[n_lead:]):
            r[...] = val.astype(r.dtype)
        first_step = pl.program_id(0) == 0
        for r, val in zip(dconst_refs, dcs):
            @pl.when(first_step)
            def _(r=r, val=val):
                r[...] = val

            @pl.when(jnp.logical_not(first_step))
            def _(r=r, val=val):
                r[...] += val

    cspecs = [pl.BlockSpec(c.shape, lambda i, n=c.ndim: (0,) * n) for c in consts]
    ctspecs = [pl.BlockSpec((tr, c.shape[1]), lambda i: (i, 0)) for c in cts]
    out_shape = [jax.ShapeDtypeStruct((S, lead_width), drow_dtypes[0])]
    out_shape += [jax.ShapeDtypeStruct((S, widths[j]), drow_dtypes[j]) for j in range(n_lead, n_diff)]
    out_shape += [jax.ShapeDtypeStruct(c.shape, F32) for c in consts]
    out_specs = [pl.BlockSpec((tr, sum(widths[:n_lead])), lambda i: (i, 0))]
    out_specs += [pl.BlockSpec((tr, widths[j]), lambda i: (i, 0)) for j in range(n_lead, n_diff)] + cspecs
    res = pl.pallas_call(
        body, name=name, out_shape=tuple(out_shape), grid=(S // tr,),
        in_specs=list(specs) + cspecs + ctspecs, out_specs=tuple(out_specs),
        compiler_params=_params(("arbitrary",)),
    )(*arrs, *consts, *cts)
    return list(res[:n_rows_out]), list(res[n_rows_out:])


def _mla_prep_fn(cq, ckv, kr, kr_sw, cos, sin, gq, gkv, wq_n, wq_r, wq_sw, wk, wv):
    hq = _rms(cq, gq)
    hkv = _rms(ckv, gkv)
    return (bdot(hq, wq_n), bdot(hq, wq_r) * cos + bdot(hq, wq_sw) * sin,
            bdot(hkv, wk), bdot(hkv, wv), kr * cos + kr_sw * sin)


def _merge_fn(g0, g1, g2, of, og, om, b0, b1, b2, wf, wg, wm):
    return (_sigmoid(g0 + b0) * bdot(of, wf) + _sigmoid(g1 + b1) * bdot(og, wg) + _sigmoid(g2 + b2) * bdot(om, wm),)


_IN_SIZES = (256, 256, 256, 4, 256, 256, 512, 16, 512, 256, 128, 32, 3072)
_IN_OFF = np.concatenate([[0], np.cumsum(_IN_SIZES)])
(_O_FQ, _O_FK, _O_FV, _O_FF, _O_GQ, _O_GK, _O_GV, _O_GLOW, _O_GR, _O_MQ, _O_MKV, _O_MKR, _O_ZG) = [int(o) for o in _IN_OFF[:-1]]
N_IN = int(_IN_OFF[-1])
_BIG_GROUPS = ((_O_ZG, 3072), (_O_GV, 512), (_O_GR, 512), (_O_FQ, 256), (_O_FK, 256), (_O_FV, 256),
               (_O_GQ, 256), (_O_GK, 256), (_O_MQ, 256), (_O_MKV, 128))
Z_GATE, Z_GV, Z_GR, Z_FQ, Z_FK, Z_FV, Z_GQ, Z_GK, Z_MQ, Z_MKV = [int(o) for o in
                                                                    np.concatenate([[0], np.cumsum([w for _, w in _BIG_GROUPS])])[:-1]]
N_BIG = sum(w for _, w in _BIG_GROUPS)
_HALF = MLA_ROPE // 2
_QK_HD = MLA_NOPE + MLA_ROPE
SM_FF, SM_GLOW, SM_KR, SM_KR_SW, N_SM = 0, 8, 128, 256, 384
N_PAD = N_BIG + N_SM
_IN_SEGS = ([(o, w, 1.0) for o, w in _BIG_GROUPS]
            + [(_O_FF, 4, 1.0), (None, SM_GLOW - 4, 0.0), (_O_GLOW, GLA_RANK, 1.0), (None, 128 - SM_GLOW - GLA_RANK, 0.0)]
            + [(_O_MKR, MLA_ROPE, 1.0)] * MLA_HEADS
            + [(_O_MKR + _HALF, _HALF, -1.0), (_O_MKR, _HALF, 1.0)] * MLA_HEADS)


def _cols(x, start, width):
    return lax.slice_in_dim(x, start, start + width, axis=x.ndim - 1)


def _pad_w_in(w):
    return jnp.concatenate([jnp.zeros(w.shape[:-1] + (n,), w.dtype) if src is None else
                            (_cols(w, src, n) if sign > 0 else -_cols(w, src, n)) for src, n, sign in _IN_SEGS], axis=-1)


def _unpad_w_in(g):
    groups = []
    for o, n in zip(_IN_OFF[:-1], _IN_SIZES):
        total, pos = None, 0
        for src, m, sign in _IN_SEGS:
            if src is not None and o <= src and src + m <= o + n:
                term = _cols(g, pos, m) if sign > 0 else -_cols(g, pos, m)
                if m != n:
                    term = jnp.pad(term, [(0, 0)] * (g.ndim - 1) + [(int(src - o), int(o + n - src - m))])
                total = term if total is None else total + term
            pos += m
        groups.append(total)
    return jnp.concatenate(groups, axis=-1)


def _take(x, idx):
    idx = np.asarray(idx)
    cuts = [0] + [i for i in range(1, len(idx)) if idx[i] != idx[i - 1] + 1] + [len(idx)]
    return jnp.concatenate([_cols(x, int(idx[a]), b - a) for a, b in zip(cuts[:-1], cuts[1:])], axis=1)


_UQ_NOPE = np.concatenate([np.arange(h * _QK_HD, h * _QK_HD + MLA_NOPE) for h in range(MLA_HEADS)])
_UQ_ROT = np.concatenate([np.arange(h * _QK_HD + MLA_NOPE, (h + 1) * _QK_HD) for h in range(MLA_HEADS)])
_UKV_PERM = np.concatenate(
    [np.concatenate([np.arange(h * 128, h * 128 + MLA_NOPE) for h in range(MLA_HEADS)]),
     np.concatenate([np.arange(h * 128 + MLA_NOPE, (h + 1) * 128) for h in range(MLA_HEADS)])])
_UKV_INV = np.argsort(_UKV_PERM)


def _rotary_partner(r):
    return jnp.concatenate([piece for h in range(MLA_HEADS) for piece in
                            (-_cols(r, h * MLA_ROPE + _HALF, _HALF), _cols(r, h * MLA_ROPE, _HALF))], axis=1)


def _uq_grad(dn, dr, dsw):
    dr = dr + jnp.concatenate([piece for h in range(MLA_HEADS) for piece in
                               (_cols(dsw, h * MLA_ROPE + _HALF, _HALF), -_cols(dsw, h * MLA_ROPE, _HALF))], axis=1)
    return jnp.concatenate([piece for h in range(MLA_HEADS) for piece in
                            (_cols(dn, h * MLA_NOPE, MLA_NOPE), _cols(dr, h * MLA_ROPE, MLA_ROPE))], axis=1)


def _rope_tables(S):
    inv = ROPE_BASE ** (-jnp.arange(_HALF, dtype=F32) / _HALF)
    ang = jnp.arange(S, dtype=F32)[:, None] * inv[None, :]
    return jnp.tile(jnp.cos(ang), (1, 2 * MLA_HEADS)), jnp.tile(jnp.sin(ang), (1, 2 * MLA_HEADS))


class _LayerParams:
    def __init__(self, rep, l):
        self.w, self.rep, self.l, self.made = {}, rep, l, {}

    def __getitem__(self, k):
        if k not in self.made:
            self.made[k] = self._make(k)
        return self.made[k]

    def _make(self, k):
        w, rep, l = self.w, self.rep, self.l
        if k == 'wg':
            return jnp.pad(w['w_gla_gate'], [(SM_GLOW, LANES - SM_GLOW - GLA_RANK), (0, 0)])
        if k in ('wq_n', 'wq_r'):
            return _take(w['w_mla_uq'], _UQ_NOPE if k == 'wq_n' else _UQ_ROT)
        if k == 'wq_sw':
            return _rotary_partner(self['wq_r'])
        if k in ('wk', 'wv'):
            return _take(w['w_mla_ukv'], _UKV_PERM[:256] if k == 'wk' else _UKV_PERM[256:])
        if k == 'b_f':
            return jnp.zeros((8, 1), F32).at[:FOX_HEADS, 0].set(rep['b_fox_forget'][l])
        if k == 'b_gate':
            return [rep['b_branch_gate'][l][i * 1024:(i + 1) * 1024].reshape(1, 1024) for i in range(3)]
        vec = {'bg': 'b_gla_gate', 'go': 'g_gla_out', 'gq': 'g_mla_q', 'gkv': 'g_mla_kv'}
        if k in vec:
            return rep[vec[k]][l].reshape(1, -1)
        return rep[k][l] if k in rep else w[k]


_GLA_COLS = (Z_GQ, Z_GK, Z_GV, Z_GR)
_MLA_OUTS = [(256, BF16), (128, BF16), (256, BF16), (256, BF16), (128, BF16)]


def _mla_rows(z, zsm, rope):
    return [(z, 256, Z_MQ), (z, 128, Z_MKV), (zsm, 128, SM_KR), (zsm, 128, SM_KR_SW), *rope]


def _mla_consts(p):
    return [p['gq'], p['gkv'], p['wq_n'], p['wq_r'], p['wq_sw'], p['wk'], p['wv']]


def _fox_qkv(z):
    return [((z, Z_FQ, 256), (z, Z_FK, 256), FOX_HD, False)], (z, Z_FV, 256)


def _mla_qkv(qn, qr, kn, vv, kr):
    return [((qn, 0, 256), (kn, 0, 256), MLA_NOPE, False), ((qr, 0, 128), (kr, 0, 128), MLA_ROPE, True)], (vv, 0, 256)


def _xa_qkv(qx, kvx):
    return [((qx, 0, 512), (kvx, 0, 512), XA_HD, False)], (kvx, 512, 512)


def _merge_rows(z, o_fox, o_gla, o_mla):
    return [(z, 1024, Z_GATE), (z, 1024, Z_GATE + 1024), (z, 1024, Z_GATE + 2048), o_fox, o_gla, o_mla]


def _merge_consts(p):
    return p['b_gate'] + [p['w_up_fox'], p['w_up_gla'], p['w_up_mla']]


def _carried(hooks, key, call, single=False):
    entries = hooks.pop(key, [])
    if not entries:
        return call(rider=None)
    res = call(rider=_join_riders([rider for rider, _ in entries]))
    for (_, sink), got in zip(entries, res[-1]):
        sink(got)
    return res[0] if single else res[:-1]


def _layer_fwd(x0, mem, p, rope, l, hooks):
    S = x0.shape[0]
    sv = {'x0': x0}

    def mm(key, a, b, **kw):
        return _carried(hooks, (l, key), lambda rider: _mm(a, b, mode='nn', rider=rider, name=f"{key}_{l}", **kw), single=True)

    h1 = _rms_fwd(x0, p['g_mix'], name=f"rms_mix_{l}")
    z = mm('in_big', h1, p['w_in'], out_dtype=BF16, b_cols=(0, N_BIG))
    zsm = _mm(h1, p['w_in'], mode='nn', out_dtype=F32, b_cols=(N_BIG, N_SM), name=f"in_small_{l}")
    sv.update(h1=h1, z=z, zsm=zsm)
    ff_t = jnp.zeros((8, S), F32).at[:FOX_HEADS].set(zsm[:, SM_FF:SM_FF + FOX_HEADS].T)
    cum_t = _fox_cum_fwd(ff_t, p['b_f'], name=f"fox_cum_{l}")
    cum = cum_t.T
    o_fox, lse_f = _carried(hooks, (l, 'fox_fwd'), lambda rider: _attn_fwd(
        *_fox_qkv(z), FOX_HEADS, cum_t, cum, scale=FOX_HD ** -0.5, mask='causal', name=f"fox_fwd_{l}", rider=rider))
    sv.update(ff_t=ff_t, cum=cum, cum_t=cum_t, lse_f=lse_f, o_fox=o_fox)
    o_gla, states = _gla_fwd(z, zsm, p['wg'], p['bg'], p['go'], _GLA_COLS, name=f"gla_fwd_{l}")
    sv.update(o_gla=o_gla, states=states)
    mla = _stage_fwd(_mla_prep_fn, _mla_rows(z, zsm, rope), _mla_consts(p), _MLA_OUTS, name=f"mla_prep_{l}")
    o_mla, lse_m = _carried(hooks, (l, 'mla_fwd'), lambda rider: _attn_fwd(
        *_mla_qkv(*mla), MLA_HEADS, None, None, scale=_QK_HD ** -0.5, mask='chunk', name=f"mla_fwd_{l}", rider=rider))
    sv.update(mla=mla, lse_m=lse_m, o_mla=o_mla)
    (y,) = _stage_fwd(_merge_fn, _merge_rows(z, o_fox, o_gla, o_mla), _merge_consts(p), [(1024, BF16)], name=f"merge_{l}")
    x1 = mm('out_proj', y, p['w_out'], out_dtype=F32, residual=x0)
    sv.update(y=y, x1=x1)
    h2 = _rms_fwd(x1, p['g_xa'], name=f"rms_xa_{l}")
    hm = _rms_fwd(mem, p['g_mem'], name=f"rms_mem_{l}")
    qx = _mm(h2, p['w_xq'], mode='nn', out_dtype=BF16, name=f"xq_{l}")
    kvx = _mm(hm, p['w_xkv'], mode='nn', out_dtype=BF16, name=f"xkv_{l}")
    ox, lse_x = _carried(hooks, (l, 'xa_fwd'), lambda rider: _attn_fwd(
        *_xa_qkv(qx, kvx), XA_HEADS, None, None, scale=XA_HD ** -0.5, mask=None, name=f"xa_fwd_{l}", rider=rider))
    x2 = mm('xo', ox, p['w_xo'], out_dtype=F32, residual=x1)
    sv.update(h2=h2, hm=hm, qx=qx, kvx=kvx, lse_x=lse_x, ox=ox, x2=x2)
    h3 = _rms_fwd(x2, p['g_mlp'], name=f"rms_mlp_{l}")
    a = mm('mlp1', h3, p['w_mlp1'], out_dtype=BF16)
    x3 = mm('mlp2', a, p['w_mlp2'], out_dtype=F32, act='relu2', residual=x2)
    sv.update(h3=h3, a=a)
    return x3, sv


def _layer_bwd(dx3, dx3b, mem, p, rope, sv, l, hooks, half_done, matrices_done):
    S = dx3.shape[0]
    g = {}

    def dw(key, a, b, **kw):
        return _mm(a, b, mode='tn', out_dtype=BF16, col_shards=b.shape[1] // LANES, name=f"d_{key}_{l}", **kw)

    da = _mm(dx3b, p['w_mlp2'], mode='nt', out_dtype=BF16, drelu_of=sv['a'], name=f"d_mlp2_in_{l}")
    g['w_mlp2'] = dw('w_mlp2', sv['a'], dx3b, act='relu2')
    dx2, dx2b, g['g_mlp'] = _mm(da, p['w_mlp1'], mode='nt', out_dtype=F32, norm_bwd=(sv['x2'], p['g_mlp'], dx3), tm=1024,
                                name=f"d_mlp1_in_{l}")
    g['w_mlp1'] = dw('w_mlp1', sv['h3'], da)
    dox = _mm(dx2b, p['w_xo'], mode='nt', out_dtype=BF16, name=f"d_xo_in_{l}")
    g['w_xo'] = dw('w_xo', sv['ox'], dx2b)
    (dqx,), (dkx,), dvx = _carried(hooks, (l, 'xa_bwd'), lambda rider: _attn_bwd(
        *_xa_qkv(sv['qx'], sv['kvx']), XA_HEADS, sv['ox'], dox, sv['lse_x'], None, None,
        scale=XA_HD ** -0.5, mask=None, name=f"xa_bwd_{l}", rider=rider))
    dqx = dqx.astype(BF16)
    dkvx = jnp.concatenate([dkx, dvx], axis=1).astype(BF16)
    dx1, dx1b, g['g_xa'] = _mm(dqx, p['w_xq'], mode='nt', out_dtype=F32, norm_bwd=(sv['x1'], p['g_xa'], dx2), tm=1024,
                               name=f"d_xq_in_{l}")
    g['w_xq'] = dw('w_xq', sv['h2'], dqx)
    dhm = _mm(dkvx, p['w_xkv'], mode='nt', out_dtype=F32, name=f"d_xkv_in_{l}")
    g['w_xkv'] = dw('w_xkv', sv['hm'], dkvx)
    _, _, g['g_mem'] = _rms_bwd(mem, p['g_mem'], dhm, None, name=f"d_rms_mem_{l}")
    dy = _mm(dx1b, p['w_out'], mode='nt', out_dtype=F32, name=f"d_out_in_{l}")
    g['w_out'] = dw('w_out', sv['y'], dx1b)
    z, zsm = sv['z'], sv['zsm']
    (dz, do_fox, do_gla, do_mla), (db0, db1, db2, g['w_up_fox'], g['w_up_gla'], g['w_up_mla']) = _stage_bwd(
        _merge_fn, _merge_rows(z, sv['o_fox'], sv['o_gla'], sv['o_mla']), _merge_consts(p), [dy], 6, [BF16] * 6,
        lead=(3, N_PAD), tr=256, name=f"merge_bwd_{l}")
    g['b_branch_gate'] = jnp.concatenate([db0, db1, db2], axis=1).reshape(-1)
    half_done(l, g)
    (dfq,), (dfk,), dfv, dck, dcq = _carried(hooks, (l, 'fox_bwd'), lambda rider: _attn_bwd(
        *_fox_qkv(z), FOX_HEADS, sv['o_fox'], do_fox, sv['lse_f'], sv['cum_t'], sv['cum'],
        scale=FOX_HD ** -0.5, mask='causal', name=f"fox_bwd_{l}", rider=rider))
    dff_t, db_f = _fox_cum_bwd(sv['ff_t'], p['b_f'], dcq + dck.T, name=f"fox_cum_bwd_{l}")
    g['b_fox_forget'] = db_f[:FOX_HEADS, 0]
    dgq, dgk, dgv, dgr, dzsm, dwg, dbg, dgo = _gla_bwd(z, zsm, p['wg'], p['bg'], p['go'], sv['states'], do_gla, _GLA_COLS,
                                                       name=f"gla_bwd_{l}")
    g['w_gla_gate'] = dwg[SM_GLOW:SM_GLOW + GLA_RANK]
    g['b_gla_gate'] = dbg.reshape(-1)
    g['g_gla_out'] = dgo.reshape(-1)
    (dmqn, dmqr), (dmkn, dmkr), dmv = _carried(hooks, (l, 'mla_bwd'), lambda rider: _attn_bwd(
        *_mla_qkv(*sv['mla']), MLA_HEADS, sv['o_mla'], do_mla, sv['lse_m'], None, None,
        scale=_QK_HD ** -0.5, mask='chunk', name=f"mla_bwd_{l}", rider=rider))
    (dcq, dckv, dkr, dkr_sw), (dgq_n, dgkv_n, dwq_n, dwq_r, dwq_sw, dwk, dwv) = _stage_bwd(
        _mla_prep_fn, _mla_rows(z, zsm, rope), _mla_consts(p), [dmqn, dmqr, dmkn, dmv, dmkr], 4, [BF16] * 4,
        name=f"mla_prep_bwd_{l}")
    g['g_mla_q'] = dgq_n.reshape(-1)
    g['g_mla_kv'] = dgkv_n.reshape(-1)
    g['w_mla_uq'] = _uq_grad(dwq_n, dwq_r, dwq_sw)
    g['w_mla_ukv'] = _take(jnp.concatenate([dwk, dwv], axis=1), _UKV_INV)
    dsm = dzsm + jnp.pad(dff_t[:FOX_HEADS].T, [(0, 0), (0, 128 - FOX_HEADS)])
    dz = lax.dynamic_update_slice(dz, jnp.concatenate(
        [dgv, dgr, dfq.astype(BF16), dfk.astype(BF16), dfv.astype(BF16), dgq, dgk, dcq, dckv, dsm.astype(BF16), dkr, dkr_sw],
        axis=1), (0, Z_GV))
    g['w_in'] = dw('w_in', sv['h1'], dz, tn=N_PAD // 3)
    matrices_done(l, g)
    dx0, dx0b, g['g_mix'] = _carried(hooks, (l, 'd_in'), lambda rider: _mm(
        dz, p['w_in'], mode='nt', out_dtype=F32, norm_bwd=(sv['x0'], p['g_mix'], dx1), tm=512, tk=N_PAD // 2,
        name=f"d_in_{l}", rider=rider))
    for n in ('g_mlp', 'g_mem', 'g_xa', 'g_mix'):
        g[n] = g[n].reshape(-1)
    return dx0, dx0b, g


def _local_step(x, mem, target, ps, g_final, hooks, half_done, matrices_done, layer_done):
    rope = _rope_tables(x.shape[0])
    saved = []
    for l, p in enumerate(ps):
        x, sv = _layer_fwd(x, mem, p, rope, l, hooks)
        saved.append(sv)
    loss, dx, dxb, dgf = _loss_head(x, g_final, target, name="loss_head")
    for l in reversed(range(len(ps))):
        dx, dxb, grads = _layer_bwd(dx, dxb, mem, ps[l], rope, saved[l], l, hooks, half_done, matrices_done)
        layer_done(l, grads)
    assert not hooks, f"exchanges without a carrier: {list(hooks)}"
    return loss, dx, dgf.reshape(-1)


_MESH_AXES = ("x", "y", "c")
_HBM = pl.BlockSpec(memory_space=pl.ANY)


N_CHIP = 4
_SLOT_ROWS = (2048, 1024, 512, 256, 128, 64, 32, 16, 8)


def _place():
    x, y, c = (lax.axis_index(n) for n in _MESH_AXES)
    return (x, y, c), (x, y, 1 - c), [(1 - x, y), (x, 1 - y), (1 - x, 1 - y)]


def _remote(src, dst, sems, k, to):
    return pltpu.make_async_remote_copy(src_ref=src, dst_ref=dst, send_sem=sems[0].at[k], recv_sem=sems[1].at[k],
                                        device_id=to, device_id_type=pl.DeviceIdType.MESH)


def _all_gather(x, *, name):
    def body(x_ref, o_ref, send_sems, recv_sems, local_sem):
        me, sib, chips = _place()
        c = me[2]
        sems = (send_sems, recv_sems)
        slot = lambda px, py, pc: o_ref.at[4 * px + 2 * py + pc]
        mine = pltpu.make_async_copy(x_ref, slot(*me), local_sem)
        mine.start()
        first = [_remote(x_ref, slot(*me), sems, 0, sib)]
        first += [_remote(x_ref, slot(*me), sems, 1 + j, (*chip, c)) for j, chip in enumerate(chips)]
        for cp in first:
            cp.start()
        passed = [_remote(slot(*chip, c), slot(*chip, c), sems, 4 + j, sib) for j, chip in enumerate(chips)]
        for j, chip in enumerate(chips):
            _remote(x_ref, slot(*chip, c), sems, 1 + j, me).wait_recv()
            passed[j].start()
        _remote(x_ref, slot(*sib), sems, 0, me).wait_recv()
        for j, chip in enumerate(chips):
            _remote(x_ref, slot(*chip, 1 - c), sems, 4 + j, me).wait_recv()
        for cp in first + passed:
            cp.wait_send()
        mine.wait()

    return pl.pallas_call(
        body, name=name, out_shape=jax.ShapeDtypeStruct((N_DEV,) + x.shape, x.dtype),
        in_specs=[_HBM], out_specs=_HBM,
        scratch_shapes=[pltpu.SemaphoreType.DMA((N_DEV - 1,)), pltpu.SemaphoreType.DMA((N_DEV - 1,)), pltpu.SemaphoreType.DMA],
        compiler_params=pltpu.CompilerParams(has_side_effects=True),
    )(x)


class _Rider:
    def __init__(self, inputs, out_shapes, scratch, start, finish, post):
        self.inputs, self.out_shapes, self.scratch = list(inputs), list(out_shapes), list(scratch)
        self.start, self.finish, self.post = start, finish, post


def _run_rider(rider, *, name):
    def body(*refs):
        rider.start(refs)
        rider.finish(refs)

    outs = pl.pallas_call(
        body, name=name, out_shape=tuple(rider.out_shapes), in_specs=[_HBM] * len(rider.inputs),
        out_specs=(_HBM,) * len(rider.out_shapes), scratch_shapes=rider.scratch,
        compiler_params=pltpu.CompilerParams(has_side_effects=True),
    )(*rider.inputs)
    return rider.post(outs)


def _carry(rider, n_in, n_out, first, last):
    if rider is None:
        return [], [], [], [], [], lambda refs: refs
    ni, no = len(rider.inputs), len(rider.out_shapes)

    def split(refs):
        own_in, r_in = refs[:n_in], refs[n_in:n_in + ni]
        own_out, r_out = refs[n_in + ni:n_in + ni + n_out], refs[n_in + ni + n_out:n_in + ni + n_out + no]
        rest = refs[n_in + ni + n_out + no:]
        own_scr, r_scr = rest[:len(rest) - len(rider.scratch)], rest[len(rest) - len(rider.scratch):]
        rrefs = tuple(r_in) + tuple(r_out) + tuple(r_scr)
        pl.when(first())(lambda: rider.start(rrefs))
        pl.when(last())(lambda: rider.finish(rrefs))
        return tuple(own_in) + tuple(own_out) + tuple(own_scr)

    return list(rider.inputs), [_HBM] * ni, list(rider.out_shapes), [_HBM] * no, list(rider.scratch), split


def _gather_rider(shards, axes):
    n = len(shards)
    srcs, out_shapes, kinds = [], [], []
    for s, ax in zip(shards, axes):
        L, a, b = s.shape
        if ax == 1:
            srcs.append(s.reshape(L, 1, a, b)), out_shapes.append((L, N_DEV, a, b)), kinds.append('row')
        elif b % 128 == 0:
            srcs.append(s), out_shapes.append((L, a, N_DEV * b)), kinds.append('col')
        else:
            srcs.append(s.reshape(1, L, a, b)), out_shapes.append((N_DEV, L, a, b)), kinds.append('slot')

    def parts(refs):
        x_refs, o_refs = refs[:n], refs[n:2 * n]
        send_sems, recv_sems, local_sem = refs[2 * n:]
        me, sib, chips = _place()
        sems = (send_sems, recv_sems)

        def win(t, px, py, pc):
            idx = 4 * px + 2 * py + pc
            if kinds[t] == 'row':
                return o_refs[t].at[:, pl.ds(idx, 1)]
            if kinds[t] == 'col':
                width = shards[t].shape[2]
                return o_refs[t].at[:, :, pl.ds(pl.multiple_of(idx * width, 128), width)]
            return o_refs[t].at[pl.ds(idx, 1)]

        def group(k, block, to, own):
            return [_remote(x_refs[t] if own else win(t, *block), win(t, *block), sems, k * n + t, to) for t in range(n)]

        mine = [pltpu.make_async_copy(x_refs[t], win(t, *me), local_sem.at[t]) for t in range(n)]
        first = group(0, me, sib, True)
        for j, chip in enumerate(chips):
            first += group(1 + j, me, (*chip, me[2]), True)
        return me, sib, chips, group, mine, first

    def start(refs):
        *_, mine, first = parts(refs)
        for cp in mine + first:
            cp.start()

    def finish(refs):
        me, sib, chips, group, mine, first = parts(refs)
        c = me[2]
        passed = []
        for j, chip in enumerate(chips):
            for cp in group(1 + j, (*chip, c), me, False):
                cp.wait_recv()
            fwd = group(4 + j, (*chip, c), sib, False)
            for cp in fwd:
                cp.start()
            passed += fwd
        for cp in group(0, sib, me, False):
            cp.wait_recv()
        for j, chip in enumerate(chips):
            for cp in group(4 + j, (*chip, 1 - c), me, False):
                cp.wait_recv()
        for cp in first + passed:
            cp.wait_send()
        for cp in mine:
            cp.wait()

    def post(outs):
        whole = []
        for o, s, kind in zip(outs, shards, kinds):
            L, a, b = s.shape
            whole.append(o.reshape(L, N_DEV * a, b) if kind == 'row' else o if kind == 'col' else _to_whole(o, 2))
        return whole

    return _Rider(srcs, [jax.ShapeDtypeStruct(shp, s.dtype) for shp, s in zip(out_shapes, shards)],
                  [pltpu.SemaphoreType.DMA(((N_DEV - 1) * n,)), pltpu.SemaphoreType.DMA(((N_DEV - 1) * n,)),
                   pltpu.SemaphoreType.DMA((n,))], start, finish, post)


def _sibling_swap_rider(x):
    def sends(refs):
        x_ref, o_ref, send_sems, recv_sems = refs
        me, sib, _ = _place()
        return [_remote(x_ref.at[j, 1 - me[2]], o_ref.at[j], (send_sems, recv_sems), j, sib) for j in range(N_CHIP)]

    def start(refs):
        for cp in sends(refs):
            cp.start()

    def finish(refs):
        for cp in sends(refs):
            cp.wait_send()
            cp.wait_recv()

    return _Rider([x], [jax.ShapeDtypeStruct((N_CHIP,) + x.shape[2:], x.dtype)],
                  [pltpu.SemaphoreType.DMA((N_CHIP,)), pltpu.SemaphoreType.DMA((N_CHIP,))], start, finish, lambda outs: outs[0])


def _join_riders(riders):
    counts = [(len(r.inputs), len(r.out_shapes), len(r.scratch)) for r in riders]
    n_in, n_out = sum(c[0] for c in counts), sum(c[1] for c in counts)

    def refs_of(refs, k):
        a = sum(c[0] for c in counts[:k])
        b = n_in + sum(c[1] for c in counts[:k])
        s = n_in + n_out + sum(c[2] for c in counts[:k])
        return tuple(refs[a:a + counts[k][0]]) + tuple(refs[b:b + counts[k][1]]) + tuple(refs[s:s + counts[k][2]])

    def each(method):
        def run(refs):
            for k, r in enumerate(riders):
                getattr(r, method)(refs_of(refs, k))
        return run

    def post(outs):
        got, at = [], 0
        for r, c in zip(riders, counts):
            got.append(r.post(outs[at:at + c[1]]))
            at += c[1]
        return got

    return _Rider([x for r in riders for x in r.inputs], [o for r in riders for o in r.out_shapes],
                  [s for r in riders for s in r.scratch], each('start'), each('finish'), post)


def _pair_sum(x, got, c, *, name):
    _, _, R, _ = x.shape
    tr = _pick(R, _SLOT_ROWS)

    def body(c_ref, x_ref, g_ref, o_ref):
        o_ref[...] = (x_ref[...].astype(F32) + g_ref[...].astype(F32)).astype(o_ref.dtype)

    return pl.pallas_call(
        body, name=name, out_shape=jax.ShapeDtypeStruct((N_CHIP, R, 128), x.dtype),
        grid_spec=pltpu.PrefetchScalarGridSpec(
            num_scalar_prefetch=1, grid=(R // tr,),
            in_specs=[pl.BlockSpec((N_CHIP, None, tr, 128), lambda i, c_ref: (0, c_ref[0], i, 0)),
                      pl.BlockSpec((N_CHIP, tr, 128), lambda i, c_ref: (0, i, 0))],
            out_specs=pl.BlockSpec((N_CHIP, tr, 128), lambda i, c_ref: (0, i, 0))),
        compiler_params=_params(("parallel",)),
    )(c, x, got)


def _chip_all_to_all_rider(x):
    def parts(refs):
        x_ref, o_ref, send_sems, recv_sems, local_sem = refs
        me, _, chips = _place()
        sems = (send_sems, recv_sems)
        mine = 2 * me[0] + me[1]
        local = pltpu.make_async_copy(x_ref.at[mine], o_ref.at[mine], local_sem)
        sends = [_remote(x_ref.at[2 * px + py], o_ref.at[mine], sems, j, (px, py, me[2])) for j, (px, py) in enumerate(chips)]
        arrival = lambda j: _remote(x_ref.at[mine], o_ref.at[2 * chips[j][0] + chips[j][1]], sems, j, me)
        return local, sends, arrival

    def start(refs):
        local, sends, _ = parts(refs)
        for cp in [local] + sends:
            cp.start()

    def finish(refs):
        local, sends, arrival = parts(refs)
        for j, cp in enumerate(sends):
            cp.wait_send()
            arrival(j).wait_recv()
        local.wait()

    return _Rider([x], [jax.ShapeDtypeStruct(x.shape, x.dtype)],
                  [pltpu.SemaphoreType.DMA((N_CHIP - 1,)), pltpu.SemaphoreType.DMA((N_CHIP - 1,)), pltpu.SemaphoreType.DMA],
                  start, finish, lambda outs: outs[0])


def _sum_slots(x, *, name):
    n, R, _ = x.shape
    tr = _pick(R, _SLOT_ROWS)

    def body(x_ref, o_ref):
        acc = x_ref[0].astype(F32)
        for j in range(1, n):
            acc = acc + x_ref[j].astype(F32)
        o_ref[...] = acc

    return pl.pallas_call(
        body, name=name, out_shape=jax.ShapeDtypeStruct((R, 128), F32), grid=(R // tr,),
        in_specs=[pl.BlockSpec((n, tr, 128), lambda i: (0, i, 0))], out_specs=pl.BlockSpec((tr, 128), lambda i: (i, 0)),
        compiler_params=_params(("parallel",)),
    )(x)


def _adamw(w, g, m, v, *, name):
    shape = w.shape
    cols = shape[-1]
    rows = int(np.prod(shape[:-1]))
    tr = next((t for t in (1024, 512, 256, 128, 64, 32, 16, 8) if rows % t == 0 and t * cols * 4 <= (1 << 20)), rows)

    def body(w_ref, g_ref, m_ref, v_ref, d_ref, mo_ref, vo_ref):
        g_ = g_ref[...]
        m_ = ADAM_B1 * m_ref[...] + (1.0 - ADAM_B1) * g_
        v_ = ADAM_B2 * v_ref[...] + (1.0 - ADAM_B2) * jnp.square(g_)
        m_hat = m_ / (1.0 - ADAM_B1 ** ADAM_STEP)
        v_hat = v_ / (1.0 - ADAM_B2 ** ADAM_STEP)
        d_ref[...] = -ADAM_LR * (m_hat / (jnp.sqrt(v_hat) + ADAM_EPS) + ADAM_WD * w_ref[...])
        mo_ref[...] = m_
        vo_ref[...] = v_

    blk = pl.BlockSpec((tr, cols), lambda i: (i, 0))
    outs = pl.pallas_call(
        body, name=name, out_shape=tuple(jax.ShapeDtypeStruct((rows, cols), F32) for _ in range(3)), grid=(rows // tr,),
        in_specs=[blk] * 4, out_specs=(blk,) * 3, compiler_params=_params(("parallel",)),
    )(*(a.reshape(rows, cols) for a in (w, g, m, v)))
    return tuple(o.reshape(shape) for o in outs)


_WEIGHTS = ('g_mix', 'w_in', 'b_fox_forget', 'w_gla_gate', 'b_gla_gate', 'g_gla_out', 'g_mla_q', 'w_mla_uq', 'g_mla_kv',
            'w_mla_ukv', 'b_branch_gate', 'w_up_fox', 'w_up_gla', 'w_up_mla', 'w_out', 'g_xa', 'g_mem', 'w_xq', 'w_xkv',
            'w_xo', 'g_mlp', 'w_mlp1', 'w_mlp2', 'g_final')
_SHARDED = (('w_in', 1), ('w_gla_gate', 2), ('w_mla_uq', 2), ('w_mla_ukv', 2), ('w_up_fox', 2), ('w_up_gla', 2),
            ('w_up_mla', 2), ('w_out', 1), ('w_xq', 1), ('w_xkv', 1), ('w_xo', 2), ('w_mlp1', 2), ('w_mlp2', 1))
_REPLICATED = tuple(n for n in _WEIGHTS if n not in dict(_SHARDED))
_ROW_PAD = 1024
_SMALL_ROW_PAD = 8
_PIECE_ROWS = 16


def _pack(flats, lead, row_pad=_ROW_PAD):
    if all(int(np.prod(a.shape[lead:])) % 128 == 0 for a in flats):
        def block(a):
            a = a.reshape(a.shape[:lead] + (-1, 128))
            return jnp.pad(a, [(0, 0)] * lead + [(0, -a.shape[lead] % _PIECE_ROWS), (0, 0)])
        cat = jnp.concatenate([block(a) for a in flats], axis=lead)
        rows = cat.shape[lead]
        return jnp.pad(cat, [(0, 0)] * lead + [(0, -(-rows // row_pad) * row_pad - rows), (0, 0)])
    cat = jnp.concatenate([a.reshape(a.shape[:lead] + (-1,)) for a in flats], axis=-1)
    n = cat.shape[-1]
    total = -(-n // (128 * row_pad)) * (128 * row_pad)
    cat = jnp.pad(cat, [(0, 0)] * lead + [(0, total - n)])
    return cat.reshape(cat.shape[:lead] + (total // 128, 128))


def _unpack(buf, shapes, lead):
    sizes = [int(np.prod(shp)) for shp in shapes]
    out, off = [], 0
    if all(n % 128 == 0 for n in sizes):
        for shp, n in zip(shapes, sizes):
            rows = buf[(slice(None),) * lead + (slice(off, off + n // 128),)]
            out.append(rows.reshape(buf.shape[:lead] + tuple(shp)))
            off += -(-(n // 128) // _PIECE_ROWS) * _PIECE_ROWS
        return out
    flat = buf.reshape(buf.shape[:lead] + (-1,))
    for shp, n in zip(shapes, sizes):
        out.append(flat[..., off:off + n].reshape(buf.shape[:lead] + tuple(shp)))
        off += n
    return out


def _to_whole(g, axis):
    if axis == 1:
        return g.transpose(1, 0, 2, 3).reshape(g.shape[1], N_DEV * g.shape[2], g.shape[3])
    return g.transpose(1, 2, 0, 3).reshape(g.shape[1], g.shape[2], N_DEV * g.shape[3])


def _to_shards(w, axis):
    L, R, C = w.shape
    if axis == 1:
        return w.reshape(L, N_DEV, R // N_DEV, C).transpose(1, 0, 2, 3)
    return w.reshape(L, R, N_DEV, C // N_DEV).transpose(2, 0, 1, 3)


def kernel(x, mem, g_mix, w_in, b_fox_forget, w_gla_gate, b_gla_gate, g_gla_out, g_mla_q, w_mla_uq, g_mla_kv, w_mla_ukv, b_branch_gate, w_up_fox, w_up_gla, w_up_mla, w_out, g_xa, g_mem, w_xq, w_xkv, w_xo, g_mlp, w_mlp1, w_mlp2, g_final, loss_target, m_g_mix, m_w_in, m_b_fox_forget, m_w_gla_gate, m_b_gla_gate, m_g_gla_out, m_g_mla_q, m_w_mla_uq, m_g_mla_kv, m_w_mla_ukv, m_b_branch_gate, m_w_up_fox, m_w_up_gla, m_w_up_mla, m_w_out, m_g_xa, m_g_mem, m_w_xq, m_w_xkv, m_w_xo, m_g_mlp, m_w_mlp1, m_w_mlp2, m_g_final, v_g_mix, v_w_in, v_b_fox_forget, v_w_gla_gate, v_b_gla_gate, v_g_gla_out, v_g_mla_q, v_w_mla_uq, v_g_mla_kv, v_w_mla_ukv, v_b_branch_gate, v_w_up_fox, v_w_up_gla, v_w_up_mla, v_w_out, v_g_xa, v_g_mem, v_w_xq, v_w_xkv, v_w_xo, v_g_mlp, v_w_mlp1, v_w_mlp2, v_g_final):
    wts = dict(zip(_WEIGHTS, (g_mix, w_in, b_fox_forget, w_gla_gate, b_gla_gate, g_gla_out, g_mla_q, w_mla_uq, g_mla_kv,
                              w_mla_ukv, b_branch_gate, w_up_fox, w_up_gla, w_up_mla, w_out, g_xa, g_mem, w_xq, w_xkv, w_xo,
                              g_mlp, w_mlp1, w_mlp2, g_final)))
    mom1 = dict(zip(_WEIGHTS, (m_g_mix, m_w_in, m_b_fox_forget, m_w_gla_gate, m_b_gla_gate, m_g_gla_out, m_g_mla_q,
                               m_w_mla_uq, m_g_mla_kv, m_w_mla_ukv, m_b_branch_gate, m_w_up_fox, m_w_up_gla, m_w_up_mla,
                               m_w_out, m_g_xa, m_g_mem, m_w_xq, m_w_xkv, m_w_xo, m_g_mlp, m_w_mlp1, m_w_mlp2, m_g_final)))
    mom2 = dict(zip(_WEIGHTS, (v_g_mix, v_w_in, v_b_fox_forget, v_w_gla_gate, v_b_gla_gate, v_g_gla_out, v_g_mla_q,
                               v_w_mla_uq, v_g_mla_kv, v_w_mla_ukv, v_b_branch_gate, v_w_up_fox, v_w_up_gla, v_w_up_mla,
                               v_w_out, v_g_xa, v_g_mem, v_w_xq, v_w_xkv, v_w_xo, v_g_mlp, v_w_mlp1, v_w_mlp2, v_g_final)))
    depth = g_mix.shape[0]

    names = [n for n, _ in _SHARDED]
    axes = dict(_SHARDED)
    shard = {n: wts[n] for n in names}
    shard['w_in'] = _pad_w_in(w_in)
    rep = {n: wts[n] for n in _REPLICATED}
    ps = [_LayerParams(rep, l) for l in range(depth)]

    def gather(group, l):
        rider = _gather_rider([shard[n][l:l + 1].astype(BF16) for n in group], [axes[n] for n in group])
        return rider, lambda whole: ps[l].w.update({n: w[0] for n, w in zip(group, whole)})

    first, sink = gather(['w_in'], 0)
    sink(_run_rider(first, name="gather_w_in_0"))
    narrow = ['w_gla_gate', 'w_mla_uq', 'w_mla_ukv', 'w_up_fox', 'w_up_gla', 'w_up_mla']
    hooks = {(0, 'in_big'): [gather(narrow + ['w_out', 'w_xq', 'w_xkv', 'w_xo'], 0)],
             (0, 'fox_fwd'): [gather(['w_mlp1', 'w_mlp2'], 0)]}
    ahead = (('mla_fwd', ['w_in'] + narrow + ['w_out']), ('xo', ['w_xkv']), ('mlp1', ['w_mlp1']),
             ('mlp2', ['w_mlp2', 'w_xq', 'w_xo']))
    assert sorted(n for _, group in ahead for n in group) == sorted(names)
    for l in range(1, depth):
        for key, group in ahead:
            hooks.setdefault((l - 1, key), []).append(gather(group, l))

    core = lax.axis_index("c").astype(jnp.int32).reshape(1)
    late = ['w_in', 'w_gla_gate', 'w_mla_uq', 'w_mla_ukv']
    groups = {'early': [n for n in names if n not in late], 'late': late}
    small_grads, landed = {}, {}

    def to_slots(gl, axis):
        if gl.ndim != 3:
            return _to_shards(gl[None], axis)
        blocks, rows, _ = gl.shape
        if axis == 1:
            return gl.reshape(blocks, N_DEV, rows // N_DEV, LANES).transpose(1, 0, 2, 3)
        return gl.reshape(N_DEV, blocks // N_DEV, rows, LANES)

    def slot_shape(n):
        _, a, b = shard[n].shape
        return (b // LANES, a, LANES) if n in blocked else (1, a, b)

    def from_slot(n, x):
        return x.transpose(1, 0, 2).reshape((1,) + shard[n].shape[1:]) if n in blocked else x

    blocked = {'w_in', 'w_out', 'w_xq', 'w_xkv', 'w_xo', 'w_mlp1', 'w_mlp2'}

    def ride(key, rider, sink, name):
        if key is None:
            sink(_run_rider(rider, name=name))
        else:
            hooks.setdefault(key, []).append((rider, sink))

    def exchange(l, g, which, swap_in, scatter_in):
        assert all((g[n].ndim == 3) == (n in blocked) for n in groups[which])
        slots = _pack([to_slots(g[n], axes[n]).astype(BF16) for n in groups[which]], 1)
        slots = slots.reshape((N_CHIP, 2) + slots.shape[1:])

        def swapped(got):
            paired = _pair_sum(slots, got, core, name=f"pair_grads_{which}_{l}")
            ride(scatter_in, _chip_all_to_all_rider(paired), lambda landing: landed.update({(l, which): landing}),
                 f"scatter_grads_{which}_{l}")

        ride(swap_in, _sibling_swap_rider(slots), swapped, f"swap_grads_{which}_{l}")

    def half_done(l, g):
        exchange(l, g, 'early', (l, 'fox_bwd'), (l, 'mla_bwd'))

    def matrices_done(l, g):
        if l > 0:
            exchange(l, g, 'late', (l - 1, 'xa_bwd'), (l - 1, 'fox_bwd'))
        else:
            exchange(l, g, 'late', None, (l, 'd_in'))

    def layer_done(l, g):
        small_grads[l] = g

    loss, dx, dg_final = _local_step(x[0], mem[0], loss_target[0], ps, g_final, hooks, half_done, matrices_done, layer_done)
    loss = lax.psum(loss[0, 0], _MESH_AXES)

    grad = {}
    for which, group in groups.items():
        shapes = [slot_shape(n) for n in group]
        per_layer = [_unpack(_sum_slots(landed[(l, which)], name=f"sum_grads_{which}_{l}"), shapes, 0) for l in range(depth)]
        grad.update({n: jnp.concatenate([from_slot(n, per_layer[l][i]) for l in range(depth)], axis=0)
                     for i, n in enumerate(group)})
    grad['w_in'] = _unpad_w_in(grad['w_in'])
    grads = small_grads
    small = [dg_final if n == 'g_final' else jnp.stack([grads[l][n] for l in range(depth)]) for n in _REPLICATED]
    small_shapes = [wts[n].shape for n in _REPLICATED]
    small_sum = _sum_slots(_all_gather(_pack(small, 0, _SMALL_ROW_PAD), name="gather_small_grads"), name="sum_small_grads")
    grad.update(dict(zip(_REPLICATED, _unpack(small_sum, small_shapes, 0))))

    delta, new_m, new_v = {}, {}, {}
    for n, _ in _SHARDED:
        delta[n], new_m[n], new_v[n] = _adamw(wts[n], grad[n], mom1[n], mom2[n], name=f"adamw_{n}")
    packed = [_pack([d[n] for n in _REPLICATED], 0, _SMALL_ROW_PAD) for d in (wts, mom1, mom2)]
    outs = _adamw(packed[0], small_sum, packed[1], packed[2], name="adamw_small")
    for d, o in zip((delta, new_m, new_v), outs):
        d.update(dict(zip(_REPLICATED, _unpack(o, small_shapes, 0))))

    return (loss, dx[None], *[grad[n] for n in _WEIGHTS], *[delta[n] for n in _WEIGHTS],
            *[new_m[n] for n in _WEIGHTS], *[new_v[n] for n in _WEIGHTS])
```
